```python
import math
import jax, jax.numpy as jnp
from jax import lax
import numpy as np

D_MODEL = 1024
BATCH = 8
SEQ = 2048
DEPTH = 1

GDN_HEADS = 8
GDN_HEAD_DIM = 128
GDN_WIDTH = GDN_HEADS * GDN_HEAD_DIM
CONV_K = 5
GLA_HEADS = 4
GLA_KEY_DIM = D_MODEL // 2
GLA_VAL_DIM = D_MODEL
GLA_HEAD_K = GLA_KEY_DIM // GLA_HEADS
GLA_HEAD_V = GLA_VAL_DIM // GLA_HEADS
GLA_GATE_RANK = 16
GLA_GATE_NORMALIZER = 16.0
CHUNK = 64
NORM_EPS = 1e-6

IN_SIZES = [
    3 * GDN_WIDTH,
    GDN_WIDTH,
    GDN_HEADS,
    GDN_HEADS,
    GDN_HEADS,
    GDN_HEADS,
    GLA_KEY_DIM,
    GLA_KEY_DIM,
    GLA_VAL_DIM,
    GLA_VAL_DIM,
    GLA_GATE_RANK,
    GLA_GATE_RANK,
    D_MODEL,
    D_MODEL,
]
N_IN = int(sum(IN_SIZES))
IN_SPLITS = [int(s) for s in np.cumsum(IN_SIZES)[:-1]]

kernel_name = "bidir_gdn_gla_gated_hybrid"


def rmsnorm(x, w):
    xf = x.astype(jnp.float32)
    y = xf * lax.rsqrt(jnp.mean(xf * xf, axis=-1, keepdims=True) + NORM_EPS)
    return (y * w.astype(jnp.float32)).astype(x.dtype)


def l2norm(x):
    return x * lax.rsqrt(jnp.sum(x * x, axis=-1, keepdims=True) + NORM_EPS)


def to_heads(x, n_heads):
    b, t, _ = x.shape
    return x.reshape(b, t, n_heads, -1).transpose(0, 2, 1, 3)


def from_heads(x):
    return x.transpose(0, 2, 1, 3)


def to_chunks(x):
    b, h, t = x.shape[:3]
    return x.reshape(b, h, t // CHUNK, CHUNK, *x.shape[3:])


def centred_depthwise_conv(x, w):
    c = x.shape[-1]
    return lax.conv_general_dilated(
        x, w[:, None, :].astype(x.dtype), window_strides=(1,),
        padding=[(CONV_K // 2, CONV_K // 2)],
        dimension_numbers=("NWC", "WIO", "NWC"), feature_group_count=c)


def gated_delta_rule(q, k, v, g, beta):
    bsz, nh, t, dk = q.shape
    dv = v.shape[-1]
    q = to_chunks(q * (dk ** -0.5))
    k = to_chunks(k)
    v = to_chunks(v)
    beta = to_chunks(beta)
    g = jnp.cumsum(to_chunks(g), axis=-1)
    incl = jnp.tril(jnp.ones((CHUNK, CHUNK), dtype=bool))
    strict = jnp.tril(jnp.ones((CHUNK, CHUNK), dtype=bool), -1)
    diff = g[..., :, None] - g[..., None, :]
    decay = jnp.where(incl, jnp.exp(jnp.where(incl, diff, 0.0)), 0.0)
    kb = k * beta[..., None]
    lower = jnp.where(strict, jnp.einsum("bhnid,bhnjd->bhnij", kb, k) * decay, 0.0)
    rhs = jnp.concatenate([v * beta[..., None], kb * jnp.exp(g)[..., None]], axis=-1)
    sol = lax.linalg.triangular_solve(lower, rhs, left_side=True, lower=True,
                                      unit_diagonal=True)
    u, w = sol[..., :dv], sol[..., dv:]
    attn = jnp.einsum("bhnid,bhnjd->bhnij", q, k) * decay
    q_dec = q * jnp.exp(g)[..., None]
    g_last = g[..., -1]
    k_dec = k * jnp.exp(g_last[..., None] - g)[..., None]

    def step(S, inp):
        u_n, w_n, attn_n, qd_n, kd_n, gl_n = inp
        v_new = u_n - jnp.einsum("bhcd,bhde->bhce", w_n, S)
        o = (jnp.einsum("bhcd,bhde->bhce", qd_n, S)
             + jnp.einsum("bhij,bhje->bhie", attn_n, v_new))
        S = S * jnp.exp(gl_n)[..., None, None] + jnp.einsum("bhcd,bhce->bhde", kd_n, v_new)
        return S, o

    xs = tuple(jnp.moveaxis(a, 2, 0) for a in (u, w, attn, q_dec, k_dec, g_last))
    S0 = jnp.zeros((bsz, nh, dk, dv), q.dtype)
    _, o = lax.scan(step, S0, xs)
    return jnp.moveaxis(o, 0, 2).reshape(bsz, nh, t, dv)


def gla_chunked(q, k, v, gk):
    bsz, nh, t, dk = q.shape
    dv = v.shape[-1]
    q = to_chunks(q * (dk ** -0.5))
    k = to_chunks(k)
    v = to_chunks(v)
    G = jnp.cumsum(to_chunks(gk), axis=3)
    qg = q * jnp.exp(G)
    kg = k * jnp.exp(-G)
    incl = jnp.tril(jnp.ones((CHUNK, CHUNK), dtype=bool))
    attn = jnp.where(incl, jnp.einsum("bhnid,bhnjd->bhnij", qg, kg), 0.0)
    intra = jnp.einsum("bhnij,bhnje->bhnie", attn, v)
    G_last = G[..., -1, :]
    k_dec = k * jnp.exp(G_last[..., None, :] - G)

    def step(S, inp):
        qg_n, kd_n, v_n, gl_n = inp
        o = jnp.einsum("bhcd,bhde->bhce", qg_n, S)
        S = S * jnp.exp(gl_n)[..., :, None] + jnp.einsum("bhcd,bhce->bhde", kd_n, v_n)
        return S, o

    xs = tuple(jnp.moveaxis(a, 2, 0) for a in (qg, k_dec, v, G_last))
    S0 = jnp.zeros((bsz, nh, dk, dv), q.dtype)
    _, inter = lax.scan(step, S0, xs)
    o = jnp.moveaxis(inter, 0, 2) + intra
    return o.reshape(bsz, nh, t, dv)


def flip_t(a):
    return jnp.flip(a, axis=2)


def hybrid_layer(x, ln_pre_w, w_in, conv_w, a_log_fwd, a_log_bwd, dt_bias_fwd, dt_bias_bwd,
                 gdn_norm_w, w_proj_gdn, gk_w2_fwd, gk_b2_fwd, gk_w2_bwd, gk_b2_bwd,
                 gla_norm_w, w_proj_gla, w_out, ln_post_w):
    f32 = jnp.float32
    h = rmsnorm(x, ln_pre_w)
    proj = h @ w_in
    (qkv_a, z_a, a_f, a_b, b_f, b_b, q_b, k_b, v_b, g_b,
     r_f, r_b, gate_a, gate_b) = jnp.split(proj, IN_SPLITS, axis=-1)

    qkv_a = jax.nn.silu(centred_depthwise_conv(qkv_a, conv_w)).astype(f32)
    q_a, k_a, v_a = jnp.split(qkv_a, 3, axis=-1)
    q_a = l2norm(to_heads(q_a, GDN_HEADS))
    k_a = l2norm(to_heads(k_a, GDN_HEADS))
    v_a = to_heads(v_a, GDN_HEADS)
    lg_f = (-jnp.exp(a_log_fwd.astype(f32)) * jax.nn.softplus(a_f.astype(f32) + dt_bias_fwd.astype(f32))).transpose(0, 2, 1)
    lg_b = (-jnp.exp(a_log_bwd.astype(f32)) * jax.nn.softplus(a_b.astype(f32) + dt_bias_bwd.astype(f32))).transpose(0, 2, 1)
    beta_f = jax.nn.sigmoid(b_f.astype(f32)).transpose(0, 2, 1)
    beta_b = jax.nn.sigmoid(b_b.astype(f32)).transpose(0, 2, 1)
    o_a = (gated_delta_rule(q_a, k_a, v_a, lg_f, beta_f)
           + flip_t(gated_delta_rule(flip_t(q_a), flip_t(k_a), flip_t(v_a),
                                     flip_t(lg_b), flip_t(beta_b))))
    o_a = rmsnorm(from_heads(o_a), gdn_norm_w)
    o_a = o_a * jax.nn.silu(z_a.astype(f32)).reshape(o_a.shape)
    y_a = o_a.reshape(x.shape[0], x.shape[1], GDN_WIDTH).astype(x.dtype) @ w_proj_gdn

    q_bh = to_heads(q_b.astype(f32), GLA_HEADS)
    k_bh = to_heads(k_b.astype(f32), GLA_HEADS)
    v_bh = to_heads(v_b.astype(f32), GLA_HEADS)
    gk_f = jax.nn.log_sigmoid((r_f @ gk_w2_fwd + gk_b2_fwd).astype(f32)) / GLA_GATE_NORMALIZER
    gk_b = jax.nn.log_sigmoid((r_b @ gk_w2_bwd + gk_b2_bwd).astype(f32)) / GLA_GATE_NORMALIZER
    gk_f = to_heads(gk_f, GLA_HEADS)
    gk_b = to_heads(gk_b, GLA_HEADS)
    o_b = (gla_chunked(q_bh, k_bh, v_bh, gk_f)
           + flip_t(gla_chunked(flip_t(q_bh), flip_t(k_bh), flip_t(v_bh), flip_t(gk_b))))
    o_b = rmsnorm(from_heads(o_b), gla_norm_w)
    o_b = o_b * jax.nn.silu(g_b.astype(f32)).reshape(o_b.shape)
    y_b = o_b.reshape(x.shape[0], x.shape[1], GLA_VAL_DIM).astype(x.dtype) @ w_proj_gla

    merged = jax.nn.sigmoid(gate_a) * y_a + jax.nn.sigmoid(gate_b) * y_b
    out = merged @ w_out
    return x + rmsnorm(out, ln_post_w)


def _fwd_setup_inputs(seed: int = 0) -> dict:
    key = jax.random.key(seed)
    ks = jax.random.split(key, 20)
    L, D = DEPTH, D_MODEL

    def nrm(k, shape, scale):
        return jax.random.normal(k, shape, jnp.float32) * scale

    def dt_bias(k):
        u = jax.random.uniform(k, (L, GDN_HEADS), jnp.float32)
        dt = jnp.exp(u * (math.log(0.1) - math.log(0.001)) + math.log(0.001))
        return dt + jnp.log(-jnp.expm1(-dt))

    def a_log(k):
        return jnp.log(jax.random.uniform(k, (L, GDN_HEADS), jnp.float32, 1.0, 16.0))

    return {
        "x": nrm(ks[0], (BATCH, SEQ, D), 1.0),
        "ln_pre_w": 1.0 + nrm(ks[1], (L, D), 0.02),
        "w_in": nrm(ks[2], (L, D, N_IN), D ** -0.5),
        "conv_w": nrm(ks[3], (L, CONV_K, 3 * GDN_WIDTH), CONV_K ** -0.5),
        "a_log_fwd": a_log(ks[4]),
        "a_log_bwd": a_log(ks[5]),
        "dt_bias_fwd": dt_bias(ks[6]),
        "dt_bias_bwd": dt_bias(ks[7]),
        "gdn_norm_w": 1.0 + nrm(ks[8], (L, GDN_HEAD_DIM), 0.02),
        "w_proj_gdn": nrm(ks[9], (L, GDN_WIDTH, D), GDN_WIDTH ** -0.5),
        "gk_w2_fwd": nrm(ks[10], (L, GLA_GATE_RANK, GLA_KEY_DIM), GLA_GATE_RANK ** -0.5),
        "gk_b2_fwd": nrm(ks[11], (L, GLA_KEY_DIM), 0.01),
        "gk_w2_bwd": nrm(ks[12], (L, GLA_GATE_RANK, GLA_KEY_DIM), GLA_GATE_RANK ** -0.5),
        "gk_b2_bwd": nrm(ks[13], (L, GLA_KEY_DIM), 0.01),
        "gla_norm_w": 1.0 + nrm(ks[14], (L, GLA_HEAD_V), 0.02),
        "w_proj_gla": nrm(ks[15], (L, GLA_VAL_DIM, D), GLA_VAL_DIM ** -0.5),
        "w_out": nrm(ks[16], (L, D, D), D ** -0.5),
        "ln_post_w": 1.0 + nrm(ks[17], (L, D), 0.02),
    }


def _fwd_reference(x, ln_pre_w, w_in, conv_w, a_log_fwd, a_log_bwd, dt_bias_fwd, dt_bias_bwd,
              gdn_norm_w, w_proj_gdn, gk_w2_fwd, gk_b2_fwd, gk_w2_bwd, gk_b2_bwd,
              gla_norm_w, w_proj_gla, w_out, ln_post_w):
    h = x
    for l in range(DEPTH):
        h = hybrid_layer(h, ln_pre_w[l], w_in[l], conv_w[l], a_log_fwd[l], a_log_bwd[l],
                         dt_bias_fwd[l], dt_bias_bwd[l], gdn_norm_w[l], w_proj_gdn[l],
                         gk_w2_fwd[l], gk_b2_fwd[l], gk_w2_bwd[l], gk_b2_bwd[l],
                         gla_norm_w[l], w_proj_gla[l], w_out[l], ln_post_w[l])
    return h


import jax as _jax
import jax.numpy as _jnp

TWIN_FORMAT = 'train_step'
FWD_PARAMS = ['x', 'ln_pre_w', 'w_in', 'conv_w', 'a_log_fwd', 'a_log_bwd', 'dt_bias_fwd', 'dt_bias_bwd', 'gdn_norm_w', 'w_proj_gdn', 'gk_w2_fwd', 'gk_b2_fwd', 'gk_w2_bwd', 'gk_b2_bwd', 'gla_norm_w', 'w_proj_gla', 'w_out', 'ln_post_w']
TWIN_WEIGHTS = ['ln_pre_w', 'w_in', 'conv_w', 'a_log_fwd', 'a_log_bwd', 'dt_bias_fwd', 'dt_bias_bwd', 'gdn_norm_w', 'w_proj_gdn', 'gk_w2_fwd', 'gk_b2_fwd', 'gk_w2_bwd', 'gk_b2_bwd', 'gla_norm_w', 'w_proj_gla', 'w_out', 'ln_post_w']
TWIN_DIFF_INPUT = 'x'
TWIN_INPUTS = ['x', 'ln_pre_w', 'w_in', 'conv_w', 'a_log_fwd', 'a_log_bwd', 'dt_bias_fwd', 'dt_bias_bwd', 'gdn_norm_w', 'w_proj_gdn', 'gk_w2_fwd', 'gk_b2_fwd', 'gk_w2_bwd', 'gk_b2_bwd', 'gla_norm_w', 'w_proj_gla', 'w_out', 'ln_post_w', 'loss_target', 'm_ln_pre_w', 'm_w_in', 'm_conv_w', 'm_a_log_fwd', 'm_a_log_bwd', 'm_dt_bias_fwd', 'm_dt_bias_bwd', 'm_gdn_norm_w', 'm_w_proj_gdn', 'm_gk_w2_fwd', 'm_gk_b2_fwd', 'm_gk_w2_bwd', 'm_gk_b2_bwd', 'm_gla_norm_w', 'm_w_proj_gla', 'm_w_out', 'm_ln_post_w', 'v_ln_pre_w', 'v_w_in', 'v_conv_w', 'v_a_log_fwd', 'v_a_log_bwd', 'v_dt_bias_fwd', 'v_dt_bias_bwd', 'v_gdn_norm_w', 'v_w_proj_gdn', 'v_gk_w2_fwd', 'v_gk_b2_fwd', 'v_gk_w2_bwd', 'v_gk_b2_bwd', 'v_gla_norm_w', 'v_w_proj_gla', 'v_w_out', 'v_ln_post_w']
TWIN_OUTPUTS = ['loss', 'grad_x', 'grad_ln_pre_w', 'grad_w_in', 'grad_conv_w', 'grad_a_log_fwd', 'grad_a_log_bwd', 'grad_dt_bias_fwd', 'grad_dt_bias_bwd', 'grad_gdn_norm_w', 'grad_w_proj_gdn', 'grad_gk_w2_fwd', 'grad_gk_b2_fwd', 'grad_gk_w2_bwd', 'grad_gk_b2_bwd', 'grad_gla_norm_w', 'grad_w_proj_gla', 'grad_w_out', 'grad_ln_post_w', 'delta_ln_pre_w', 'delta_w_in', 'delta_conv_w', 'delta_a_log_fwd', 'delta_a_log_bwd', 'delta_dt_bias_fwd', 'delta_dt_bias_bwd', 'delta_gdn_norm_w', 'delta_w_proj_gdn', 'delta_gk_w2_fwd', 'delta_gk_b2_fwd', 'delta_gk_w2_bwd', 'delta_gk_b2_bwd', 'delta_gla_norm_w', 'delta_w_proj_gla', 'delta_w_out', 'delta_ln_post_w', 'new_m_ln_pre_w', 'new_m_w_in', 'new_m_conv_w', 'new_m_a_log_fwd', 'new_m_a_log_bwd', 'new_m_dt_bias_fwd', 'new_m_dt_bias_bwd', 'new_m_gdn_norm_w', 'new_m_w_proj_gdn', 'new_m_gk_w2_fwd', 'new_m_gk_b2_fwd', 'new_m_gk_w2_bwd', 'new_m_gk_b2_bwd', 'new_m_gla_norm_w', 'new_m_w_proj_gla', 'new_m_w_out', 'new_m_ln_post_w', 'new_v_ln_pre_w', 'new_v_w_in', 'new_v_conv_w', 'new_v_a_log_fwd', 'new_v_a_log_bwd', 'new_v_dt_bias_fwd', 'new_v_dt_bias_bwd', 'new_v_gdn_norm_w', 'new_v_w_proj_gdn', 'new_v_gk_w2_fwd', 'new_v_gk_b2_fwd', 'new_v_gk_w2_bwd', 'new_v_gk_b2_bwd', 'new_v_gla_norm_w', 'new_v_w_proj_gla', 'new_v_w_out', 'new_v_ln_post_w']
TWIN_LEAF_KINDS = {'loss': 'loss', 'grad_x': 'grad_x', 'grad_ln_pre_w': 'grad_w', 'grad_w_in': 'grad_w', 'grad_conv_w': 'grad_w', 'grad_a_log_fwd': 'grad_w', 'grad_a_log_bwd': 'grad_w', 'grad_dt_bias_fwd': 'grad_w', 'grad_dt_bias_bwd': 'grad_w', 'grad_gdn_norm_w': 'grad_w', 'grad_w_proj_gdn': 'grad_w', 'grad_gk_w2_fwd': 'grad_w', 'grad_gk_b2_fwd': 'grad_w', 'grad_gk_w2_bwd': 'grad_w', 'grad_gk_b2_bwd': 'grad_w', 'grad_gla_norm_w': 'grad_w', 'grad_w_proj_gla': 'grad_w', 'grad_w_out': 'grad_w', 'grad_ln_post_w': 'grad_w', 'delta_ln_pre_w': 'delta_w', 'delta_w_in': 'delta_w', 'delta_conv_w': 'delta_w', 'delta_a_log_fwd': 'delta_w', 'delta_a_log_bwd': 'delta_w', 'delta_dt_bias_fwd': 'delta_w', 'delta_dt_bias_bwd': 'delta_w', 'delta_gdn_norm_w': 'delta_w', 'delta_w_proj_gdn': 'delta_w', 'delta_gk_w2_fwd': 'delta_w', 'delta_gk_b2_fwd': 'delta_w', 'delta_gk_w2_bwd': 'delta_w', 'delta_gk_b2_bwd': 'delta_w', 'delta_gla_norm_w': 'delta_w', 'delta_w_proj_gla': 'delta_w', 'delta_w_out': 'delta_w', 'delta_ln_post_w': 'delta_w', 'new_m_ln_pre_w': 'new_m', 'new_m_w_in': 'new_m', 'new_m_conv_w': 'new_m', 'new_m_a_log_fwd': 'new_m', 'new_m_a_log_bwd': 'new_m', 'new_m_dt_bias_fwd': 'new_m', 'new_m_dt_bias_bwd': 'new_m', 'new_m_gdn_norm_w': 'new_m', 'new_m_w_proj_gdn': 'new_m', 'new_m_gk_w2_fwd': 'new_m', 'new_m_gk_b2_fwd': 'new_m', 'new_m_gk_w2_bwd': 'new_m', 'new_m_gk_b2_bwd': 'new_m', 'new_m_gla_norm_w': 'new_m', 'new_m_w_proj_gla': 'new_m', 'new_m_w_out': 'new_m', 'new_m_ln_post_w': 'new_m', 'new_v_ln_pre_w': 'new_v', 'new_v_w_in': 'new_v', 'new_v_conv_w': 'new_v', 'new_v_a_log_fwd': 'new_v', 'new_v_a_log_bwd': 'new_v', 'new_v_dt_bias_fwd': 'new_v', 'new_v_dt_bias_bwd': 'new_v', 'new_v_gdn_norm_w': 'new_v', 'new_v_w_proj_gdn': 'new_v', 'new_v_gk_w2_fwd': 'new_v', 'new_v_gk_b2_fwd': 'new_v', 'new_v_gk_w2_bwd': 'new_v', 'new_v_gk_b2_bwd': 'new_v', 'new_v_gla_norm_w': 'new_v', 'new_v_w_proj_gla': 'new_v', 'new_v_w_out': 'new_v', 'new_v_ln_post_w': 'new_v'}


def _forward(args):
    return _fwd_reference(*[args[k] for k in FWD_PARAMS])


def _output_shape():
    out = _jax.eval_shape(lambda: _forward(_fwd_setup_inputs(0)))
    return out.shape, out.dtype

N_MICROBATCH = 1
ADAM_LR = 0.001
ADAM_B1 = 0.9
ADAM_B2 = 0.999
ADAM_EPS = 1e-08
ADAM_WD = 0.01
ADAM_STEP = 10
PER_EXAMPLE_BATCH_AXIS = {'x': 0, 'loss_target': 0}
SHARED_INPUTS = []
_WEIGHT_DTYPES = {'ln_pre_w': _jnp.float32, 'w_in': _jnp.float32, 'conv_w': _jnp.float32, 'a_log_fwd': _jnp.float32, 'a_log_bwd': _jnp.float32, 'dt_bias_fwd': _jnp.float32, 'dt_bias_bwd': _jnp.float32, 'gdn_norm_w': _jnp.float32, 'w_proj_gdn': _jnp.float32, 'gk_w2_fwd': _jnp.float32, 'gk_b2_fwd': _jnp.float32, 'gk_w2_bwd': _jnp.float32, 'gk_b2_bwd': _jnp.float32, 'gla_norm_w': _jnp.float32, 'w_proj_gla': _jnp.float32, 'w_out': _jnp.float32, 'ln_post_w': _jnp.float32}
MOMENT_SCALE = {'ln_pre_w': 3.493498e-01, 'w_in': 1.195147e-01, 'conv_w': 1.026587e-01, 'a_log_fwd': 2.817774e-01, 'a_log_bwd': 3.644772e-01, 'dt_bias_fwd': 2.771013e-01, 'dt_bias_bwd': 3.595305e-01, 'gdn_norm_w': 5.306529e-01, 'w_proj_gdn': 1.922930e-01, 'gk_w2_fwd': 1.330263e-02, 'gk_b2_fwd': 5.304927e-02, 'gk_w2_bwd': 1.407285e-02, 'gk_b2_bwd': 5.822898e-02, 'gla_norm_w': 2.860301e-01, 'w_proj_gla': 1.275718e-01, 'w_out': 2.284182e-01, 'ln_post_w': 1.602214e+01}


def _to_microbatches(a, axis):
    t = _jnp.moveaxis(a, axis, 0)
    t = t.reshape((N_MICROBATCH, t.shape[0] // N_MICROBATCH) + t.shape[1:])
    return _jnp.moveaxis(t, 1, axis + 1)


def setup_inputs(seed: int = 0) -> dict:
    inp = _fwd_setup_inputs(seed)
    key = _jax.random.fold_in(_jax.random.key(seed), 7919)
    shape, _ = _output_shape()
    out = dict(inp)
    out["loss_target"] = _jax.random.normal(_jax.random.fold_in(key, 0), shape, _jnp.float32)
    for i, name in enumerate(TWIN_WEIGHTS):
        w = inp[name].astype(_jnp.float32)
        if MOMENT_SCALE is None:
            s = _jnp.sqrt(_jnp.mean(_jnp.square(w)) + 1e-30)
        else:
            s = MOMENT_SCALE[name]
        km, kv = _jax.random.split(_jax.random.fold_in(key, i + 1))
        out[name] = w
        out["m_" + name] = s * _jax.random.normal(km, w.shape, _jnp.float32)
        out["v_" + name] = (s * s) * _jax.random.uniform(kv, w.shape, _jnp.float32, 0.5, 1.5)
    if N_MICROBATCH > 1:
        for name, axis in PER_EXAMPLE_BATCH_AXIS.items():
            out[name] = _to_microbatches(out[name], axis)
    return {'x': out['x'], 'ln_pre_w': out['ln_pre_w'], 'w_in': out['w_in'], 'conv_w': out['conv_w'], 'a_log_fwd': out['a_log_fwd'], 'a_log_bwd': out['a_log_bwd'], 'dt_bias_fwd': out['dt_bias_fwd'], 'dt_bias_bwd': out['dt_bias_bwd'], 'gdn_norm_w': out['gdn_norm_w'], 'w_proj_gdn': out['w_proj_gdn'], 'gk_w2_fwd': out['gk_w2_fwd'], 'gk_b2_fwd': out['gk_b2_fwd'], 'gk_w2_bwd': out['gk_w2_bwd'], 'gk_b2_bwd': out['gk_b2_bwd'], 'gla_norm_w': out['gla_norm_w'], 'w_proj_gla': out['w_proj_gla'], 'w_out': out['w_out'], 'ln_post_w': out['ln_post_w'], 'loss_target': out['loss_target'], 'm_ln_pre_w': out['m_ln_pre_w'], 'm_w_in': out['m_w_in'], 'm_conv_w': out['m_conv_w'], 'm_a_log_fwd': out['m_a_log_fwd'], 'm_a_log_bwd': out['m_a_log_bwd'], 'm_dt_bias_fwd': out['m_dt_bias_fwd'], 'm_dt_bias_bwd': out['m_dt_bias_bwd'], 'm_gdn_norm_w': out['m_gdn_norm_w'], 'm_w_proj_gdn': out['m_w_proj_gdn'], 'm_gk_w2_fwd': out['m_gk_w2_fwd'], 'm_gk_b2_fwd': out['m_gk_b2_fwd'], 'm_gk_w2_bwd': out['m_gk_w2_bwd'], 'm_gk_b2_bwd': out['m_gk_b2_bwd'], 'm_gla_norm_w': out['m_gla_norm_w'], 'm_w_proj_gla': out['m_w_proj_gla'], 'm_w_out': out['m_w_out'], 'm_ln_post_w': out['m_ln_post_w'], 'v_ln_pre_w': out['v_ln_pre_w'], 'v_w_in': out['v_w_in'], 'v_conv_w': out['v_conv_w'], 'v_a_log_fwd': out['v_a_log_fwd'], 'v_a_log_bwd': out['v_a_log_bwd'], 'v_dt_bias_fwd': out['v_dt_bias_fwd'], 'v_dt_bias_bwd': out['v_dt_bias_bwd'], 'v_gdn_norm_w': out['v_gdn_norm_w'], 'v_w_proj_gdn': out['v_w_proj_gdn'], 'v_gk_w2_fwd': out['v_gk_w2_fwd'], 'v_gk_b2_fwd': out['v_gk_b2_fwd'], 'v_gk_w2_bwd': out['v_gk_w2_bwd'], 'v_gk_b2_bwd': out['v_gk_b2_bwd'], 'v_gla_norm_w': out['v_gla_norm_w'], 'v_w_proj_gla': out['v_w_proj_gla'], 'v_w_out': out['v_w_out'], 'v_ln_post_w': out['v_ln_post_w']}


def _loss(weights, diff, rest, loss_target):
    with _jax.named_scope("forward"):
        args = {**rest, TWIN_DIFF_INPUT: diff, **{k: w.astype(_WEIGHT_DTYPES[k]) for k, w in weights.items()}}
        y = _forward(args)
    with _jax.named_scope("loss_head"):
        err = _jnp.square(y.astype(_jnp.float32) - loss_target)
        return 0.5 * _jnp.sum(_jnp.mean(err, axis=-1)) if err.ndim else 0.5 * err


def _adamw(w, g, m, v):
    m = ADAM_B1 * m + (1.0 - ADAM_B1) * g
    v = ADAM_B2 * v + (1.0 - ADAM_B2) * _jnp.square(g)
    m_hat = m / (1.0 - ADAM_B1 ** ADAM_STEP)
    v_hat = v / (1.0 - ADAM_B2 ** ADAM_STEP)
    delta = -ADAM_LR * (m_hat / (_jnp.sqrt(v_hat) + ADAM_EPS) + ADAM_WD * w)
    return delta, m, v


def reference(x, ln_pre_w, w_in, conv_w, a_log_fwd, a_log_bwd, dt_bias_fwd, dt_bias_bwd, gdn_norm_w, w_proj_gdn, gk_w2_fwd, gk_b2_fwd, gk_w2_bwd, gk_b2_bwd, gla_norm_w, w_proj_gla, w_out, ln_post_w, loss_target, m_ln_pre_w, m_w_in, m_conv_w, m_a_log_fwd, m_a_log_bwd, m_dt_bias_fwd, m_dt_bias_bwd, m_gdn_norm_w, m_w_proj_gdn, m_gk_w2_fwd, m_gk_b2_fwd, m_gk_w2_bwd, m_gk_b2_bwd, m_gla_norm_w, m_w_proj_gla, m_w_out, m_ln_post_w, v_ln_pre_w, v_w_in, v_conv_w, v_a_log_fwd, v_a_log_bwd, v_dt_bias_fwd, v_dt_bias_bwd, v_gdn_norm_w, v_w_proj_gdn, v_gk_w2_fwd, v_gk_b2_fwd, v_gk_w2_bwd, v_gk_b2_bwd, v_gla_norm_w, v_w_proj_gla, v_w_out, v_ln_post_w):
    given = dict(x=x, ln_pre_w=ln_pre_w, w_in=w_in, conv_w=conv_w, a_log_fwd=a_log_fwd, a_log_bwd=a_log_bwd, dt_bias_fwd=dt_bias_fwd, dt_bias_bwd=dt_bias_bwd, gdn_norm_w=gdn_norm_w, w_proj_gdn=w_proj_gdn, gk_w2_fwd=gk_w2_fwd, gk_b2_fwd=gk_b2_fwd, gk_w2_bwd=gk_w2_bwd, gk_b2_bwd=gk_b2_bwd, gla_norm_w=gla_norm_w, w_proj_gla=w_proj_gla, w_out=w_out, ln_post_w=ln_post_w, loss_target=loss_target, m_ln_pre_w=m_ln_pre_w, m_w_in=m_w_in, m_conv_w=m_conv_w, m_a_log_fwd=m_a_log_fwd, m_a_log_bwd=m_a_log_bwd, m_dt_bias_fwd=m_dt_bias_fwd, m_dt_bias_bwd=m_dt_bias_bwd, m_gdn_norm_w=m_gdn_norm_w, m_w_proj_gdn=m_w_proj_gdn, m_gk_w2_fwd=m_gk_w2_fwd, m_gk_b2_fwd=m_gk_b2_fwd, m_gk_w2_bwd=m_gk_w2_bwd, m_gk_b2_bwd=m_gk_b2_bwd, m_gla_norm_w=m_gla_norm_w, m_w_proj_gla=m_w_proj_gla, m_w_out=m_w_out, m_ln_post_w=m_ln_post_w, v_ln_pre_w=v_ln_pre_w, v_w_in=v_w_in, v_conv_w=v_conv_w, v_a_log_fwd=v_a_log_fwd, v_a_log_bwd=v_a_log_bwd, v_dt_bias_fwd=v_dt_bias_fwd, v_dt_bias_bwd=v_dt_bias_bwd, v_gdn_norm_w=v_gdn_norm_w, v_w_proj_gdn=v_w_proj_gdn, v_gk_w2_fwd=v_gk_w2_fwd, v_gk_b2_fwd=v_gk_b2_fwd, v_gk_w2_bwd=v_gk_w2_bwd, v_gk_b2_bwd=v_gk_b2_bwd, v_gla_norm_w=v_gla_norm_w, v_w_proj_gla=v_w_proj_gla, v_w_out=v_w_out, v_ln_post_w=v_ln_post_w)
    weights = {n: given[n] for n in TWIN_WEIGHTS}
    shared = {n: given[n] for n in SHARED_INPUTS}
    per_example = {n: given[n] for n in ['x']}
    grad_fn = _jax.value_and_grad(_loss, argnums=(0, 1))

    def one_microbatch(ex, loss_target):
        ex = dict(ex)
        diff = ex.pop(TWIN_DIFF_INPUT)
        return grad_fn(weights, diff, {**shared, **ex}, loss_target)

    if N_MICROBATCH == 1:
        loss, (grad_w, grad_x) = one_microbatch(per_example, given["loss_target"])
    else:
        def body(carry, xs):
            loss_sum, grad_sum = carry
            l_k, (gw_k, gx_k) = one_microbatch(xs[0], xs[1])
            with _jax.named_scope("update"):
                return (loss_sum + l_k, _jax.tree.map(_jnp.add, grad_sum, gw_k)), gx_k

        init = (_jnp.zeros((), _jnp.float32), _jax.tree.map(_jnp.zeros_like, weights))
        (loss, grad_w), grad_x = _jax.lax.scan(body, init, (per_example, given["loss_target"]))
    with _jax.named_scope("update"):
        delta_w, new_m, new_v = {}, {}, {}
        for n in TWIN_WEIGHTS:
            delta_w[n], new_m[n], new_v[n] = _adamw(weights[n], grad_w[n], given["m_" + n], given["v_" + n])
    return (loss, grad_x, *[grad_w[n] for n in TWIN_WEIGHTS], *[delta_w[n] for n in TWIN_WEIGHTS],
            *[new_m[n] for n in TWIN_WEIGHTS], *[new_v[n] for n in TWIN_WEIGHTS])
```

```python
import functools
import math

import jax
import jax.numpy as jnp
from jax import lax
from jax.experimental import pallas as pl
from jax.experimental.pallas import tpu as pltpu

f32 = jnp.float32
MXU_DTYPE = jnp.bfloat16
HI = lax.Precision.HIGHEST

D_MODEL = 1024
CHUNK = 64
NORM_EPS = 1e-6
GDN_HEADS, GDN_DIM = 8, 128
GLA_HEADS, GLA_DK, GLA_DV = 4, 128, 256
GATE_NORMALIZER = 16.0
CONV_K = 5
N_IN = 9280
N_DEV = 8
SHARD_IN = N_IN // N_DEV

N_CAT = 9728
SMALL_OFF = 9216
LANES = 128

ADAM_LR, ADAM_B1, ADAM_B2, ADAM_EPS, ADAM_WD, ADAM_STEP = 0.001, 0.9, 0.999, 1e-08, 0.01, 10

VMEM_LIMIT = 56 * 1024 * 1024


def _cparams(sem=None):
    return pltpu.CompilerParams(dimension_semantics=sem, vmem_limit_bytes=VMEM_LIMIT)


_DN = {"nn": (((1,), (0,)), ((), ())), "nt": (((1,), (1,)), ((), ())), "tn": (((0,), (0,)), ((), ()))}


def _mx(a, b, kind):
    return lax.dot_general(a.astype(MXU_DTYPE), b.astype(MXU_DTYPE), _DN[kind], preferred_element_type=f32)


def _hi(a, b, kind):
    return lax.dot_general(a, b, _DN[kind], precision=HI, preferred_element_type=f32)


def _sigmoid(x):
    return jax.nn.sigmoid(x)


def _silu(x):
    return x * _sigmoid(x)


def _silu_grad(x):
    s = _sigmoid(x)
    return s * (1.0 + x * (1.0 - s))


def _softplus(x):
    u = jnp.exp(-jnp.abs(x))
    l1p = jnp.where(u < 1e-3, u * (1.0 - u * (0.5 - u * (1.0 / 3.0))), jnp.log(1.0 + u))
    return jnp.maximum(x, 0.0) + l1p


def _iota2(shape, axis):
    return lax.broadcasted_iota(jnp.int32, shape, axis)


def _matmul(a, b, kind, out_dtype, name, tm, tn, tk):
    if kind == "nn":
        (M, K), N = a.shape, b.shape[1]
    elif kind == "nt":
        (M, K), N = a.shape, b.shape[0]
    else:
        (K, M), N = a.shape, b.shape[1]
    tm, tn, tk = min(tm, M), min(tn, N), min(tk, K)
    assert M % tm == 0 and N % tn == 0 and K % tk == 0, (name, M, N, K)
    nk = K // tk
    if kind == "tn":
        a_spec = pl.BlockSpec((tk, tm), lambda i, j, k: (k, i))
    else:
        a_spec = pl.BlockSpec((tm, tk), lambda i, j, k: (i, k))
    if kind == "nt":
        b_spec = pl.BlockSpec((tn, tk), lambda i, j, k: (j, k))
    else:
        b_spec = pl.BlockSpec((tk, tn), lambda i, j, k: (k, j))

    def body(a_ref, b_ref, o_ref, *acc):
        p = _mx(a_ref[...], b_ref[...], kind)
        if nk == 1:
            o_ref[...] = p.astype(out_dtype)
        else:
            acc_ref = acc[0]
            k = pl.program_id(2)

            @pl.when(k == 0)
            def _():
                acc_ref[...] = p

            @pl.when(k > 0)
            def _():
                acc_ref[...] += p

            @pl.when(k == nk - 1)
            def _():
                o_ref[...] = acc_ref[...].astype(out_dtype)

    return pl.pallas_call(
        body,
        out_shape=jax.ShapeDtypeStruct((M, N), out_dtype),
        grid=(M // tm, N // tn, nk),
        in_specs=[a_spec, b_spec],
        out_specs=pl.BlockSpec((tm, tn), lambda i, j, k: (i, j)),
        scratch_shapes=[] if nk == 1 else [pltpu.VMEM((tm, tn), f32)],
        compiler_params=_cparams(("parallel", "parallel", "arbitrary")),
        name=name,
    )(a, b)


def _rms_pre(x, w, tb=256):
    T = x.shape[0]

    def body(x_ref, w_ref, h_ref):
        xv = x_ref[...]
        r = lax.rsqrt(jnp.mean(xv * xv, axis=1, keepdims=True) + NORM_EPS)
        h_ref[...] = (xv * r * w_ref[...]).astype(h_ref.dtype)

    return pl.pallas_call(
        body,
        out_shape=jax.ShapeDtypeStruct((T, D_MODEL), MXU_DTYPE),
        grid=(T // tb,),
        in_specs=[pl.BlockSpec((tb, D_MODEL), lambda i: (i, 0)), pl.BlockSpec((1, D_MODEL), lambda i: (0, 0))],
        out_specs=pl.BlockSpec((tb, D_MODEL), lambda i: (i, 0)),
        compiler_params=_cparams(("parallel",)),
        name="rms_pre",
    )(x, w)


def _rms_pre_bwd(x, w, dh, dy, tb=256):
    T = x.shape[0]

    def body(x_ref, w_ref, dh_ref, dy_ref, dx_ref, dw_ref):
        i = pl.program_id(0)
        xv, dhv = x_ref[...], dh_ref[...]
        r = lax.rsqrt(jnp.mean(xv * xv, axis=1, keepdims=True) + NORM_EPS)
        dhw = dhv * w_ref[...]
        dx_ref[...] = dy_ref[...] + r * dhw - xv * (r * r * r) * jnp.mean(dhw * xv, axis=1, keepdims=True)
        part = jnp.sum(dhv * xv * r, axis=0, keepdims=True)

        @pl.when(i == 0)
        def _():
            dw_ref[...] = part

        @pl.when(i > 0)
        def _():
            dw_ref[...] += part

    blk = pl.BlockSpec((tb, D_MODEL), lambda i: (i, 0))
    row = pl.BlockSpec((1, D_MODEL), lambda i: (0, 0))
    return pl.pallas_call(
        body,
        out_shape=(jax.ShapeDtypeStruct((T, D_MODEL), f32), jax.ShapeDtypeStruct((1, D_MODEL), f32)),
        grid=(T // tb,),
        in_specs=[blk, row, blk, blk],
        out_specs=(blk, row),
        compiler_params=_cparams(("arbitrary",)),
        name="rms_pre_bwd",
    )(x, w, dh, dy)


def _conv5(u, cw):
    T = u.shape[0]
    t = _iota2(u.shape, 0)
    acc = u * cw[2:3, :]
    for j in (0, 1, 3, 4):
        d = j - 2
        sh = pltpu.roll(u, (-d) % T, 0)
        valid = jnp.logical_and(t + d >= 0, t + d < T)
        acc = acc + jnp.where(valid, sh, 0.0) * cw[j : j + 1, :]
    return acc


def _gdn_prep(proj, conv_w):
    T = proj.shape[0]
    nh = GDN_HEADS

    def body(u_ref, cw_ref, o_ref):
        j = pl.program_id(0)
        s = _silu(_conv5(u_ref[...], cw_ref[...]))

        @pl.when(j < 2 * nh)
        def _():
            o_ref[0] = s * lax.rsqrt(jnp.sum(s * s, axis=1, keepdims=True) + NORM_EPS)

        @pl.when(j >= 2 * nh)
        def _():
            o_ref[0] = s

    return pl.pallas_call(
        body,
        out_shape=jax.ShapeDtypeStruct((3, T, GDN_HEADS * GDN_DIM), f32),
        grid=(3 * nh,),
        in_specs=[pl.BlockSpec((T, GDN_DIM), lambda j: (0, j)), pl.BlockSpec((CONV_K, GDN_DIM), lambda j: (0, j))],
        out_specs=pl.BlockSpec((1, T, GDN_DIM), lambda j: (j // nh, 0, j % nh)),
        compiler_params=_cparams(("parallel",)),
        name="gdn_prep",
    )(proj, conv_w)


def _gdn_prep_bwd(proj, conv_w, dqkv_f, dqkv_b):
    T = proj.shape[0]
    nh = GDN_HEADS

    def body(u_ref, cw_ref, df_ref, db_ref, du_ref, dcw_ref):
        j = pl.program_id(0)
        u, cw = u_ref[...], cw_ref[...]
        c = _conv5(u, cw)
        s = _silu(c)
        dn = df_ref[0] + db_ref[0]
        rinv = lax.rsqrt(jnp.sum(s * s, axis=1, keepdims=True) + NORM_EPS)
        ds_norm = rinv * dn - s * (rinv * rinv * rinv) * jnp.sum(dn * s, axis=1, keepdims=True)
        ds = jnp.where(j < 2 * nh, ds_norm, dn)
        dc = ds * _silu_grad(c)
        t = _iota2(u.shape, 0)
        du = dc * cw[2:3, :]
        rows = []
        for jj in range(CONV_K):
            d = jj - 2
            if d == 0:
                rows.append(jnp.sum(dc * u, axis=0, keepdims=True))
                continue
            ush = pltpu.roll(u, (-d) % T, 0)
            uvalid = jnp.logical_and(t + d >= 0, t + d < T)
            rows.append(jnp.sum(dc * jnp.where(uvalid, ush, 0.0), axis=0, keepdims=True))
            dsh = pltpu.roll(dc, d % T, 0)
            dvalid = jnp.logical_and(t - d >= 0, t - d < T)
            du = du + jnp.where(dvalid, dsh, 0.0) * cw[jj : jj + 1, :]
        du_ref[...] = du.astype(du_ref.dtype)
        for jj in range(CONV_K):
            dcw_ref[jj : jj + 1, :] = rows[jj]

    qspec = pl.BlockSpec((1, T, GDN_DIM), lambda j: (j // nh, 0, j % nh))
    return pl.pallas_call(
        body,
        out_shape=(jax.ShapeDtypeStruct((T, 3 * nh * GDN_DIM), MXU_DTYPE), jax.ShapeDtypeStruct((CONV_K, 3 * nh * GDN_DIM), f32)),
        grid=(3 * nh,),
        in_specs=[pl.BlockSpec((T, GDN_DIM), lambda j: (0, j)), pl.BlockSpec((CONV_K, GDN_DIM), lambda j: (0, j)), qspec, qspec],
        out_specs=(pl.BlockSpec((T, GDN_DIM), lambda j: (0, j)), pl.BlockSpec((CONV_K, GDN_DIM), lambda j: (0, j))),
        compiler_params=_cparams(("parallel",)),
        name="gdn_prep_bwd",
    )(proj, conv_w, dqkv_f, dqkv_b)


def _chunk_tri(n, rev):
    i, j = _iota2((n, n), 0), _iota2((n, n), 1)
    same = jnp.right_shift(i, 6) == jnp.right_shift(j, 6)
    order = (j >= i) if rev else (j <= i)
    return jnp.where(jnp.logical_and(same, order), 1.0, 0.0).astype(f32)


def _gdn_gates(proj, avec, dvec, tb=256):
    T = proj.shape[0]

    def body(s_ref, a_ref, d_ref, g_ref, gc_ref):
        small = s_ref[...]
        lane = _iota2(small.shape, 1)
        lg = -jnp.exp(a_ref[...]) * _softplus(small + d_ref[...])
        beta = _sigmoid(small)
        g_ref[...] = jnp.where(lane < 16, lg, jnp.where(lane < 32, beta, 0.0))
        lgm = jnp.where(lane < 16, lg, 0.0)
        gcf = _hi(_chunk_tri(tb, False), lgm, "nn")
        gcr = _hi(_chunk_tri(tb, True), lgm, "nn")
        gc_ref[...] = jnp.where(lane < 8, gcf, gcr)

    blk = pl.BlockSpec((tb, LANES), lambda i: (i, 0))
    row = pl.BlockSpec((1, LANES), lambda i: (0, 0))
    return pl.pallas_call(
        body,
        out_shape=(jax.ShapeDtypeStruct((T, LANES), f32), jax.ShapeDtypeStruct((T, LANES), f32)),
        grid=(T // tb,),
        in_specs=[pl.BlockSpec((tb, LANES), lambda i: (i, SMALL_OFF // LANES)), row, row],
        out_specs=(blk, blk),
        compiler_params=_cparams(("parallel",)),
        name="gdn_gates",
    )(proj, avec, dvec)


def _gdn_gates_bwd(proj, avec, dvec, dg_f, dg_b, dsmall_gla, tb=256):
    T = proj.shape[0]

    def body(s_ref, a_ref, d_ref, gf_ref, gb_ref, dl_ref, ds_ref, da_ref, dd_ref):
        i = pl.program_id(0)
        small = s_ref[...]
        lane = _iota2(small.shape, 1)
        dgate = gf_ref[...] + gb_ref[...]
        z = small + d_ref[...]
        nega = -jnp.exp(a_ref[...])
        dz = dgate * nega * _sigmoid(z)
        beta = _sigmoid(small)
        dsm = jnp.where(lane < 16, dz, jnp.where(lane < 32, dgate * beta * (1.0 - beta), 0.0))
        ds_ref[...] = (dsm + dl_ref[...]).astype(ds_ref.dtype)
        lg = nega * _softplus(z)
        pa = jnp.sum(jnp.where(lane < 16, dgate * lg, 0.0), axis=0, keepdims=True)
        pd = jnp.sum(jnp.where(lane < 16, dz, 0.0), axis=0, keepdims=True)

        @pl.when(i == 0)
        def _():
            da_ref[...] = pa
            dd_ref[...] = pd

        @pl.when(i > 0)
        def _():
            da_ref[...] += pa
            dd_ref[...] += pd

    blk = pl.BlockSpec((tb, LANES), lambda i: (i, 0))
    row = pl.BlockSpec((1, LANES), lambda i: (0, 0))
    return pl.pallas_call(
        body,
        out_shape=(jax.ShapeDtypeStruct((T, LANES), MXU_DTYPE), jax.ShapeDtypeStruct((1, LANES), f32), jax.ShapeDtypeStruct((1, LANES), f32)),
        grid=(T // tb,),
        in_specs=[pl.BlockSpec((tb, LANES), lambda i: (i, SMALL_OFF // LANES)), row, row, blk, blk, blk],
        out_specs=(blk, row, row),
        compiler_params=_cparams(("arbitrary",)),
        name="gdn_gates_bwd",
    )(proj, avec, dvec, dg_f, dg_b, dsmall_gla)


def _inv_unit_lower(L, eye):
    A = eye - L
    P = L
    for _ in range(5):
        P = _hi(P, P, "nn")
        A = A + _hi(A, P, "nn")
    return A


def _lane_col(blk, idx):
    lane = _iota2(blk.shape, 1)
    return jnp.sum(jnp.where(lane == idx, blk, 0.0), axis=1, keepdims=True)


class _GdnChunk:
    def __init__(self, q, k, v, gcol, beta, rev):
        Cn = q.shape[0]
        ii, jj = _iota2((Cn, Cn), 0), _iota2((Cn, Cn), 1)
        self.incl = (ii <= jj) if rev else (ii >= jj)
        self.strict = (ii < jj) if rev else (ii > jj)
        grow = jnp.broadcast_to(gcol, (Cn, LANES)).T[0:1, :]
        diff = gcol - grow
        self.decay = jnp.where(self.incl, jnp.exp(jnp.where(self.incl, diff, 0.0)), 0.0)
        self.k, self.v, self.beta, self.gcol = k, v, beta, gcol
        self.qs = q * (GDN_DIM**-0.5)
        self.kb = k * beta
        self.L = jnp.where(self.strict, _mx(self.kb, k, "nt") * self.decay, 0.0)
        eye = jnp.where(ii == jj, 1.0, 0.0).astype(f32)
        self.A = _inv_unit_lower(self.L, eye)
        self.eg = jnp.exp(gcol)
        self.kbg = self.kb * self.eg
        self.u = _hi(self.A, v * beta, "nn")
        self.w = _hi(self.A, self.kbg, "nn")
        self.attn = jnp.where(self.incl, _mx(self.qs, k, "nt") * self.decay, 0.0)
        self.qd = self.qs * self.eg
        last = 0 if rev else Cn - 1
        self.last = last
        gl = gcol[last : last + 1, :]
        self.ekd = jnp.exp(gl - gcol)
        self.kd = k * self.ekd
        self.egl = jnp.exp(gl)


def _gdn_scan_fwd(qkvn, gates, gc):
    T = qkvn.shape[1]
    N = T // CHUNK
    nh = GDN_HEADS

    def body(qf_ref, qb_ref, gf_ref, gb_ref, cf_ref, cb_ref, of_ref, ob_ref, sf_ref, sb_ref, s_scr):
        n, h = pl.program_id(0), pl.program_id(1)

        @pl.when(n == 0)
        def _():
            s_scr[0, h] = jnp.zeros((GDN_DIM, GDN_DIM), f32)
            s_scr[1, h] = jnp.zeros((GDN_DIM, GDN_DIM), f32)

        for d, (q_ref, g_ref, c_ref, o_ref, st_ref) in enumerate(
            ((qf_ref, gf_ref, cf_ref, of_ref, sf_ref), (qb_ref, gb_ref, cb_ref, ob_ref, sb_ref))
        ):
            rev = d == 1
            gcol = _lane_col(c_ref[...], h + 8 * d)
            beta = _lane_col(g_ref[...], 16 + h + 8 * d)
            ck = _GdnChunk(q_ref[0], q_ref[1], q_ref[2], gcol, beta, rev)
            S = s_scr[d, h]
            st_ref[0, 0] = S
            v_new = ck.u - _mx(ck.w, S, "nn")
            o_ref[...] = _mx(ck.qd, S, "nn") + _mx(ck.attn, v_new, "nn")
            s_scr[d, h] = S * ck.egl + _mx(ck.kd, v_new, "tn")

    qf = pl.BlockSpec((3, CHUNK, GDN_DIM), lambda n, h: (0, n, h))
    qb = pl.BlockSpec((3, CHUNK, GDN_DIM), lambda n, h: (0, N - 1 - n, h))
    gf = pl.BlockSpec((CHUNK, LANES), lambda n, h: (n, 0))
    gb = pl.BlockSpec((CHUNK, LANES), lambda n, h: (N - 1 - n, 0))
    of = pl.BlockSpec((CHUNK, GDN_DIM), lambda n, h: (n, h))
    ob = pl.BlockSpec((CHUNK, GDN_DIM), lambda n, h: (N - 1 - n, h))
    sf = pl.BlockSpec((1, 1, GDN_DIM, GDN_DIM), lambda n, h: (n, h, 0, 0))
    sb = pl.BlockSpec((1, 1, GDN_DIM, GDN_DIM), lambda n, h: (N - 1 - n, h, 0, 0))
    o_shape = jax.ShapeDtypeStruct((T, nh * GDN_DIM), f32)
    s_shape = jax.ShapeDtypeStruct((N, nh, GDN_DIM, GDN_DIM), f32)
    return pl.pallas_call(
        body,
        out_shape=(o_shape, o_shape, s_shape, s_shape),
        grid=(N, nh),
        in_specs=[qf, qb, gf, gb, gf, gb],
        out_specs=(of, ob, sf, sb),
        scratch_shapes=[pltpu.VMEM((2, nh, GDN_DIM, GDN_DIM), f32)],
        compiler_params=_cparams(("arbitrary", "arbitrary")),
        name="gdn_scan_fwd",
    )(qkvn, qkvn, gates, gates, gc, gc)


def _gdn_scan_bwd(qkvn, gates, gc, st_f, st_b, do):
    T = qkvn.shape[1]
    N = T // CHUNK
    nh = GDN_HEADS

    def body(qf_ref, qb_ref, gf_ref, gb_ref, cf_ref, cb_ref, sf_ref, sb_ref, dof_ref, dob_ref,
             dqf_ref, dqb_ref, dgf_ref, dgb_ref, ds_scr):
        n, h = pl.program_id(0), pl.program_id(1)

        @pl.when(n == 0)
        def _():
            ds_scr[0, h] = jnp.zeros((GDN_DIM, GDN_DIM), f32)
            ds_scr[1, h] = jnp.zeros((GDN_DIM, GDN_DIM), f32)

        @pl.when(h == 0)
        def _():
            dgf_ref[...] = jnp.zeros((CHUNK, LANES), f32)
            dgb_ref[...] = jnp.zeros((CHUNK, LANES), f32)

        for d, (q_ref, g_ref, c_ref, st_ref, do_ref, dq_ref, dg_ref) in enumerate(
            ((qf_ref, gf_ref, cf_ref, sf_ref, dof_ref, dqf_ref, dgf_ref),
             (qb_ref, gb_ref, cb_ref, sb_ref, dob_ref, dqb_ref, dgb_ref))
        ):
            rev = d == 1
            gcol = _lane_col(c_ref[...], h + 8 * d)
            beta = _lane_col(g_ref[...], 16 + h + 8 * d)
            ck = _GdnChunk(q_ref[0], q_ref[1], q_ref[2], gcol, beta, rev)
            k, v = ck.k, ck.v
            S = st_ref[0, 0]
            dS = ds_scr[d, h]
            dov = do_ref[...]
            v_new = ck.u - _mx(ck.w, S, "nn")
            d_vnew = _mx(ck.attn, dov, "tn") + _mx(ck.kd, dS, "nn")
            d_attn = jnp.where(ck.incl, _mx(dov, v_new, "nt"), 0.0)
            d_qd = _mx(dov, S, "nt")
            d_kd = _mx(v_new, dS, "nt")
            d_gl = jnp.sum(jnp.sum(dS * S, axis=1, keepdims=True), axis=0, keepdims=True) * ck.egl
            ds_scr[d, h] = dS * ck.egl + _mx(ck.qd, dov, "tn") - _mx(ck.w, d_vnew, "tn")
            d_w = -_mx(d_vnew, S, "nt")
            d_vb = _hi(ck.A, d_vnew, "tn")
            d_kbg = _hi(ck.A, d_w, "tn")
            dL = -jnp.where(ck.strict, _hi(d_vb, ck.u, "nt") + _hi(d_kbg, ck.w, "nt"), 0.0)
            P = dL * ck.decay
            Q = d_attn * ck.decay
            d_kb = _mx(P, k, "nn") + d_kbg * ck.eg
            d_qs = _mx(Q, k, "nn") + d_qd * ck.eg
            d_k = _mx(P, ck.kb, "tn") + _mx(Q, ck.qs, "tn") + d_kd * ck.ekd + d_kb * ck.beta
            E = dL * ck.L + d_attn * ck.attn
            colsum = _hi(E, jnp.ones((CHUNK, LANES), f32), "tn")[:, 0:1]
            kdsum = jnp.sum(d_kd * ck.kd, axis=1, keepdims=True)
            d_gc = (jnp.sum(E, axis=1, keepdims=True) - colsum
                    + jnp.sum(d_kbg * ck.kbg, axis=1, keepdims=True)
                    + jnp.sum(d_qd * ck.qd, axis=1, keepdims=True) - kdsum)
            d_gl = d_gl + jnp.sum(kdsum, axis=0, keepdims=True)
            row = _iota2((CHUNK, 1), 0)
            d_gc = d_gc + jnp.where(row == ck.last, d_gl, 0.0)
            d_beta = jnp.sum(d_vb * v, axis=1, keepdims=True) + jnp.sum(d_kb * k, axis=1, keepdims=True)
            dq_ref[0] = d_qs * (GDN_DIM**-0.5)
            dq_ref[1] = d_k
            dq_ref[2] = d_vb * ck.beta
            lane = _iota2((CHUNK, LANES), 1)
            acc = dg_ref[...]
            acc = jnp.where(lane == h + 8 * d, d_gc, acc)
            acc = jnp.where(lane == 16 + h + 8 * d, d_beta, acc)
            dg_ref[...] = acc

            @pl.when(h == nh - 1)
            def _():
                blk = dg_ref[...]
                dg_ref[...] = jnp.where(lane < 16, _hi(_chunk_tri(CHUNK, not rev), blk, "nn"), blk)

    qf = pl.BlockSpec((3, CHUNK, GDN_DIM), lambda n, h: (0, N - 1 - n, h))
    qb = pl.BlockSpec((3, CHUNK, GDN_DIM), lambda n, h: (0, n, h))
    gf = pl.BlockSpec((CHUNK, LANES), lambda n, h: (N - 1 - n, 0))
    gb = pl.BlockSpec((CHUNK, LANES), lambda n, h: (n, 0))
    sf = pl.BlockSpec((1, 1, GDN_DIM, GDN_DIM), lambda n, h: (N - 1 - n, h, 0, 0))
    sb = pl.BlockSpec((1, 1, GDN_DIM, GDN_DIM), lambda n, h: (n, h, 0, 0))
    dof = pl.BlockSpec((CHUNK, GDN_DIM), lambda n, h: (N - 1 - n, h))
    dob = pl.BlockSpec((CHUNK, GDN_DIM), lambda n, h: (n, h))
    dq_shape = jax.ShapeDtypeStruct((3, T, nh * GDN_DIM), f32)
    dg_shape = jax.ShapeDtypeStruct((T, LANES), f32)
    return pl.pallas_call(
        body,
        out_shape=(dq_shape, dq_shape, dg_shape, dg_shape),
        grid=(N, nh),
        in_specs=[qf, qb, gf, gb, gf, gb, sf, sb, dof, dob],
        out_specs=(qf, qb, gf, gb),
        scratch_shapes=[pltpu.VMEM((2, nh, GDN_DIM, GDN_DIM), f32)],
        compiler_params=_cparams(("arbitrary", "arbitrary")),
        name="gdn_scan_bwd",
    )(qkvn, qkvn, gates, gates, gc, gc, st_f, st_b, do, do)


def _out_norm(o_f, o_b, proj, gate_blk, w, hd, name, tb=256):
    T, W = o_f.shape
    nh = W // hd

    def body(of_ref, ob_ref, z_ref, w_ref, y_ref):
        wv = w_ref[...]
        for i in range(nh):
            sl = slice(i * hd, (i + 1) * hd)
            o = of_ref[:, sl] + ob_ref[:, sl]
            r = lax.rsqrt(jnp.mean(o * o, axis=1, keepdims=True) + NORM_EPS)
            y_ref[:, sl] = (o * r * wv * _silu(z_ref[:, sl])).astype(y_ref.dtype)

    blk = pl.BlockSpec((tb, W), lambda i: (i, 0))
    return pl.pallas_call(
        body,
        out_shape=jax.ShapeDtypeStruct((T, W), MXU_DTYPE),
        grid=(T // tb,),
        in_specs=[blk, blk, pl.BlockSpec((tb, W), lambda i: (i, gate_blk)), pl.BlockSpec((1, hd), lambda i: (0, 0))],
        out_specs=blk,
        compiler_params=_cparams(("parallel",)),
        name=name,
    )(o_f, o_b, proj, w)


def _out_norm_bwd(dy, o_f, o_b, proj, gate_blk, w, hd, name, tb=256):
    T, W = o_f.shape
    nh = W // hd

    def body(dy_ref, of_ref, ob_ref, z_ref, w_ref, do_ref, dz_ref, dw_ref):
        i = pl.program_id(0)
        wv = w_ref[...]
        dw = jnp.zeros((1, hd), f32)
        for a in range(nh):
            sl = slice(a * hd, (a + 1) * hd)
            o = of_ref[:, sl] + ob_ref[:, sl]
            z = z_ref[:, sl]
            dyv = dy_ref[:, sl]
            r = lax.rsqrt(jnp.mean(o * o, axis=1, keepdims=True) + NORM_EPS)
            orr = o * r
            dn = dyv * _silu(z)
            dz_ref[:, sl] = (dyv * orr * wv * _silu_grad(z)).astype(dz_ref.dtype)
            dw = dw + jnp.sum(dn * orr, axis=0, keepdims=True)
            dnw = dn * wv
            do_ref[:, sl] = r * dnw - o * (r * r * r) * jnp.mean(dnw * o, axis=1, keepdims=True)

        @pl.when(i == 0)
        def _():
            dw_ref[...] = dw

        @pl.when(i > 0)
        def _():
            dw_ref[...] += dw

    blk = pl.BlockSpec((tb, W), lambda i: (i, 0))
    row = pl.BlockSpec((1, hd), lambda i: (0, 0))
    return pl.pallas_call(
        body,
        out_shape=(jax.ShapeDtypeStruct((T, W), f32), jax.ShapeDtypeStruct((T, W), MXU_DTYPE), jax.ShapeDtypeStruct((1, hd), f32)),
        grid=(T // tb,),
        in_specs=[blk, blk, blk, pl.BlockSpec((tb, W), lambda i: (i, gate_blk)), row],
        out_specs=(blk, blk, row),
        compiler_params=_cparams(("arbitrary",)),
        name=name,
    )(dy, o_f, o_b, proj, w)


def _gla_prep(proj, w2f, w2b, b2f, b2b, tb=256):
    T = proj.shape[0]
    W = GLA_HEADS * GLA_DK

    def body(s_ref, wf_ref, wb_ref, bf_ref, bb_ref, gf_ref, gb_ref):
        small = s_ref[...]
        gkf = -_softplus(-(_mx(small, wf_ref[...], "nn") + bf_ref[...])) * (1.0 / GATE_NORMALIZER)
        gkb = -_softplus(-(_mx(small, wb_ref[...], "nn") + bb_ref[...])) * (1.0 / GATE_NORMALIZER)
        gf_ref[...] = _hi(_chunk_tri(tb, False), gkf, "nn")
        gb_ref[...] = _hi(_chunk_tri(tb, True), gkb, "nn")

    blk = pl.BlockSpec((tb, W), lambda i: (i, 0))
    wsp = pl.BlockSpec((LANES, W), lambda i: (0, 0))
    row = pl.BlockSpec((1, W), lambda i: (0, 0))
    return pl.pallas_call(
        body,
        out_shape=(jax.ShapeDtypeStruct((T, W), f32), jax.ShapeDtypeStruct((T, W), f32)),
        grid=(T // tb,),
        in_specs=[pl.BlockSpec((tb, LANES), lambda i: (i, SMALL_OFF // LANES)), wsp, wsp, row, row],
        out_specs=(blk, blk),
        compiler_params=_cparams(("parallel",)),
        name="gla_prep",
    )(proj, w2f, w2b, b2f, b2b)


def _gla_prep_bwd(proj, w2f, w2b, b2f, b2b, dgk_f, dgk_b, tb=256):
    T = proj.shape[0]
    W = GLA_HEADS * GLA_DK

    def body(s_ref, wf_ref, wb_ref, bf_ref, bb_ref, df_ref, db_ref, ds_ref, dwf_ref, dwb_ref, dbf_ref, dbb_ref):
        i = pl.program_id(0)
        small = s_ref[...]
        dsm = jnp.zeros((tb, LANES), f32)
        parts = []
        for w_ref, b_ref, d_ref in ((wf_ref, bf_ref, df_ref), (wb_ref, bb_ref, db_ref)):
            pre = _mx(small, w_ref[...], "nn") + b_ref[...]
            dpre = d_ref[...] * (1.0 / GATE_NORMALIZER) * _sigmoid(-pre)
            dsm = dsm + _mx(dpre, w_ref[...], "nt")
            parts.append((_mx(small, dpre, "tn"), jnp.sum(dpre, axis=0, keepdims=True)))
        ds_ref[...] = dsm

        @pl.when(i == 0)
        def _():
            dwf_ref[...], dbf_ref[...] = parts[0]
            dwb_ref[...], dbb_ref[...] = parts[1]

        @pl.when(i > 0)
        def _():
            dwf_ref[...] += parts[0][0]
            dbf_ref[...] += parts[0][1]
            dwb_ref[...] += parts[1][0]
            dbb_ref[...] += parts[1][1]

    blk = pl.BlockSpec((tb, W), lambda i: (i, 0))
    wsp = pl.BlockSpec((LANES, W), lambda i: (0, 0))
    row = pl.BlockSpec((1, W), lambda i: (0, 0))
    return pl.pallas_call(
        body,
        out_shape=(jax.ShapeDtypeStruct((T, LANES), f32), jax.ShapeDtypeStruct((LANES, W), f32), jax.ShapeDtypeStruct((LANES, W), f32),
                   jax.ShapeDtypeStruct((1, W), f32), jax.ShapeDtypeStruct((1, W), f32)),
        grid=(T // tb,),
        in_specs=[pl.BlockSpec((tb, LANES), lambda i: (i, SMALL_OFF // LANES)), wsp, wsp, row, row, blk, blk],
        out_specs=(pl.BlockSpec((tb, LANES), lambda i: (i, 0)), wsp, wsp, row, row),
        compiler_params=_cparams(("arbitrary",)),
        name="gla_prep_bwd",
    )(proj, w2f, w2b, b2f, b2b, dgk_f, dgk_b)


Q_B_BLK, K_B_BLK, V_B_BLK = 4096 // GLA_DK, 4608 // GLA_DK, 5120 // GLA_DV


class _GlaChunk:
    def __init__(self, q, k, G, rev):
        Cn = q.shape[0]
        ii, jj = _iota2((Cn, Cn), 0), _iota2((Cn, Cn), 1)
        self.incl = (ii <= jj) if rev else (ii >= jj)
        self.eG = jnp.exp(G)
        self.enG = jnp.exp(-G)
        self.qg = q * (GLA_DK**-0.5) * self.eG
        self.kg = k * self.enG
        self.attn = jnp.where(self.incl, _mx(self.qg, self.kg, "nt"), 0.0)
        last = 0 if rev else Cn - 1
        self.last = last
        gl = G[last : last + 1, :]
        self.ekd = jnp.exp(gl - G)
        self.kd = k * self.ekd
        self.egl = jnp.exp(gl)


def _gla_scan_fwd(proj, G_f, G_b):
    T = proj.shape[0]
    N = T // CHUNK
    nh = GLA_HEADS

    def body(qf_ref, kf_ref, vf_ref, gf_ref, qb_ref, kb_ref, vb_ref, gb_ref, of_ref, ob_ref, sf_ref, sb_ref, s_scr):
        n, h = pl.program_id(0), pl.program_id(1)

        @pl.when(n == 0)
        def _():
            s_scr[0, h] = jnp.zeros((GLA_DV, GLA_DK), f32)
            s_scr[1, h] = jnp.zeros((GLA_DV, GLA_DK), f32)

        for d, (q_ref, k_ref, v_ref, g_ref, o_ref, st_ref) in enumerate(
            ((qf_ref, kf_ref, vf_ref, gf_ref, of_ref, sf_ref), (qb_ref, kb_ref, vb_ref, gb_ref, ob_ref, sb_ref))
        ):
            ck = _GlaChunk(q_ref[...], k_ref[...], g_ref[...], d == 1)
            v = v_ref[...]
            St = s_scr[d, h]
            st_ref[0, 0] = St
            o_ref[...] = _mx(ck.qg, St, "nt") + _mx(ck.attn, v, "nn")
            s_scr[d, h] = St * ck.egl + _mx(v, ck.kd, "tn")

    def specs(cidx):
        return [
            pl.BlockSpec((CHUNK, GLA_DK), lambda n, h: (cidx(n), Q_B_BLK + h)),
            pl.BlockSpec((CHUNK, GLA_DK), lambda n, h: (cidx(n), K_B_BLK + h)),
            pl.BlockSpec((CHUNK, GLA_DV), lambda n, h: (cidx(n), V_B_BLK + h)),
            pl.BlockSpec((CHUNK, GLA_DK), lambda n, h: (cidx(n), h)),
        ]

    fw, bw = (lambda n: n), (lambda n: N - 1 - n)
    o_shape = jax.ShapeDtypeStruct((T, nh * GLA_DV), f32)
    s_shape = jax.ShapeDtypeStruct((N, nh, GLA_DV, GLA_DK), f32)
    return pl.pallas_call(
        body,
        out_shape=(o_shape, o_shape, s_shape, s_shape),
        grid=(N, nh),
        in_specs=specs(fw) + specs(bw),
        out_specs=(
            pl.BlockSpec((CHUNK, GLA_DV), lambda n, h: (n, h)),
            pl.BlockSpec((CHUNK, GLA_DV), lambda n, h: (N - 1 - n, h)),
            pl.BlockSpec((1, 1, GLA_DV, GLA_DK), lambda n, h: (n, h, 0, 0)),
            pl.BlockSpec((1, 1, GLA_DV, GLA_DK), lambda n, h: (N - 1 - n, h, 0, 0)),
        ),
        scratch_shapes=[pltpu.VMEM((2, nh, GLA_DV, GLA_DK), f32)],
        compiler_params=_cparams(("arbitrary", "arbitrary")),
        name="gla_scan_fwd",
    )(proj, proj, proj, G_f, proj, proj, proj, G_b)


def _gla_scan_bwd(proj, G_f, G_b, st_f, st_b, do):
    T = proj.shape[0]
    N = T // CHUNK
    nh = GLA_HEADS

    def body(qf_ref, kf_ref, vf_ref, gf_ref, sf_ref, dof_ref, qb_ref, kb_ref, vb_ref, gb_ref, sb_ref, dob_ref,
             dqf_ref, dkf_ref, dvf_ref, dgf_ref, dqb_ref, dkb_ref, dvb_ref, dgb_ref, ds_scr):
        n, h = pl.program_id(0), pl.program_id(1)

        @pl.when(n == 0)
        def _():
            ds_scr[0, h] = jnp.zeros((GLA_DV, GLA_DK), f32)
            ds_scr[1, h] = jnp.zeros((GLA_DV, GLA_DK), f32)

        for d, (q_ref, k_ref, v_ref, g_ref, st_ref, do_ref, dq_ref, dk_ref, dv_ref, dg_ref) in enumerate(
            ((qf_ref, kf_ref, vf_ref, gf_ref, sf_ref, dof_ref, dqf_ref, dkf_ref, dvf_ref, dgf_ref),
             (qb_ref, kb_ref, vb_ref, gb_ref, sb_ref, dob_ref, dqb_ref, dkb_ref, dvb_ref, dgb_ref))
        ):
            rev = d == 1
            ck = _GlaChunk(q_ref[...], k_ref[...], g_ref[...], rev)
            v = v_ref[...]
            St = st_ref[0, 0]
            dSt = ds_scr[d, h]
            dov = do_ref[...]
            d_attn = jnp.where(ck.incl, _mx(dov, v, "nt"), 0.0)
            d_qg = _mx(dov, St, "nn") + _mx(d_attn, ck.kg, "nn")
            d_kg = _mx(d_attn, ck.qg, "tn")
            dv_ref[...] = _mx(ck.attn, dov, "tn") + _mx(ck.kd, dSt, "nt")
            d_kd = _mx(v, dSt, "nn")
            ds_scr[d, h] = dSt * ck.egl + _mx(dov, ck.qg, "tn")
            kdd = d_kd * ck.kd
            d_gl = jnp.sum(dSt * St, axis=0, keepdims=True) * ck.egl + jnp.sum(kdd, axis=0, keepdims=True)
            dq_ref[...] = d_qg * ck.eG * (GLA_DK**-0.5)
            dk_ref[...] = d_kg * ck.enG + d_kd * ck.ekd
            d_G = d_qg * ck.qg - d_kg * ck.kg - kdd
            row = _iota2((CHUNK, GLA_DK), 0)
            d_G = d_G + jnp.where(row == ck.last, d_gl, 0.0)
            dg_ref[...] = _hi(_chunk_tri(CHUNK, not rev), d_G, "nn")

    def in_specs(cidx):
        return [
            pl.BlockSpec((CHUNK, GLA_DK), lambda n, h: (cidx(n), Q_B_BLK + h)),
            pl.BlockSpec((CHUNK, GLA_DK), lambda n, h: (cidx(n), K_B_BLK + h)),
            pl.BlockSpec((CHUNK, GLA_DV), lambda n, h: (cidx(n), V_B_BLK + h)),
            pl.BlockSpec((CHUNK, GLA_DK), lambda n, h: (cidx(n), h)),
            pl.BlockSpec((1, 1, GLA_DV, GLA_DK), lambda n, h: (cidx(n), h, 0, 0)),
            pl.BlockSpec((CHUNK, GLA_DV), lambda n, h: (cidx(n), h)),
        ]

    def out_specs(cidx):
        return [
            pl.BlockSpec((CHUNK, GLA_DK), lambda n, h: (cidx(n), h)),
            pl.BlockSpec((CHUNK, GLA_DK), lambda n, h: (cidx(n), h)),
            pl.BlockSpec((CHUNK, GLA_DV), lambda n, h: (cidx(n), h)),
            pl.BlockSpec((CHUNK, GLA_DK), lambda n, h: (cidx(n), h)),
        ]

    fw, bw = (lambda n: N - 1 - n), (lambda n: n)
    k_shape = jax.ShapeDtypeStruct((T, nh * GLA_DK), f32)
    v_shape = jax.ShapeDtypeStruct((T, nh * GLA_DV), f32)
    return pl.pallas_call(
        body,
        out_shape=(k_shape, k_shape, v_shape, k_shape) * 2,
        grid=(N, nh),
        in_specs=in_specs(fw) + in_specs(bw),
        out_specs=tuple(out_specs(fw) + out_specs(bw)),
        scratch_shapes=[pltpu.VMEM((2, nh, GLA_DV, GLA_DK), f32)],
        compiler_params=_cparams(("arbitrary", "arbitrary")),
        name="gla_scan_bwd",
    )(proj, proj, proj, G_f, st_f, do, proj, proj, proj, G_b, st_b, do)


GATE_A_BLK, GATE_B_BLK, Z_BLK, G_B_BLK = 7168 // 1024, 8192 // 1024, 3072 // 1024, 6144 // 1024


def _mid(og, obg, proj, x, target, wpg, wpl, wout, w_post, tb=256):
    T = x.shape[0]
    Dm = D_MODEL

    def body(og_ref, obg_ref, ga_ref, gb_ref, x_ref, t_ref, wpg_ref, wpl_ref, wo_ref, wp_ref,
             loss_ref, dwp_ref, dy_ref, m_ref, dout_ref, dya_ref, dyb_ref, dga_ref, dgb_ref, dog_ref, dobg_ref):
        i = pl.program_id(0)
        ya = _mx(og_ref[...], wpg_ref[...], "nn")
        yb = _mx(obg_ref[...], wpl_ref[...], "nn")
        sa, sb = _sigmoid(ga_ref[...]), _sigmoid(gb_ref[...])
        m = sa * ya + sb * yb
        m_ref[...] = m.astype(m_ref.dtype)
        out = _mx(m, wo_ref[...], "nn")
        r = lax.rsqrt(jnp.mean(out * out, axis=1, keepdims=True) + NORM_EPS)
        wp = wp_ref[...]
        e = x_ref[...] + out * r * wp - t_ref[...]
        dy = e * (1.0 / Dm)
        dy_ref[...] = dy
        lpart = jnp.sum(e * e, axis=0, keepdims=True)
        wpart = jnp.sum(dy * out * r, axis=0, keepdims=True)
        dyn = dy * wp
        dout = r * dyn - out * (r * r * r) * jnp.mean(dyn * out, axis=1, keepdims=True)
        dout_ref[...] = dout.astype(dout_ref.dtype)
        dm = _mx(dout, wo_ref[...], "nt")
        dya, dyb = dm * sa, dm * sb
        dya_ref[...] = dya.astype(dya_ref.dtype)
        dyb_ref[...] = dyb.astype(dyb_ref.dtype)
        dga_ref[...] = (dm * ya * sa * (1.0 - sa)).astype(dga_ref.dtype)
        dgb_ref[...] = (dm * yb * sb * (1.0 - sb)).astype(dgb_ref.dtype)
        dog_ref[...] = _mx(dya, wpg_ref[...], "nt")
        dobg_ref[...] = _mx(dyb, wpl_ref[...], "nt")

        @pl.when(i == 0)
        def _():
            loss_ref[...] = lpart
            dwp_ref[...] = wpart

        @pl.when(i > 0)
        def _():
            loss_ref[...] += lpart
            dwp_ref[...] += wpart

    blk = pl.BlockSpec((tb, Dm), lambda i: (i, 0))
    row = pl.BlockSpec((1, Dm), lambda i: (0, 0))
    wsp = pl.BlockSpec((Dm, Dm), lambda i: (0, 0))
    act = jax.ShapeDtypeStruct((T, Dm), MXU_DTYPE)
    big = jax.ShapeDtypeStruct((T, Dm), f32)
    vec = jax.ShapeDtypeStruct((1, Dm), f32)
    return pl.pallas_call(
        body,
        out_shape=(vec, vec, big, act, act, act, act, act, act, big, big),
        grid=(T // tb,),
        in_specs=[blk, blk, pl.BlockSpec((tb, Dm), lambda i: (i, GATE_A_BLK)), pl.BlockSpec((tb, Dm), lambda i: (i, GATE_B_BLK)),
                  blk, blk, wsp, wsp, wsp, row],
        out_specs=(row, row) + (blk,) * 9,
        compiler_params=_cparams(("arbitrary",)),
        name="mid",
    )(og, obg, proj, proj, x, target, wpg, wpl, wout, w_post)


def _local_step(x, target, wcat, conv_w, avec, dvec, gdn_norm_w, w2f, w2b, b2f, b2b, gla_norm_w, wpg, wpl, wout, w_pre, w_post):
    h = _rms_pre(x, w_pre)
    proj = _matmul(h, wcat, "nn", f32, "proj_in", 512, 512, 1024)
    qkvn = _gdn_prep(proj, conv_w)
    gates, gc = _gdn_gates(proj, avec, dvec)
    oa_f, oa_b, sa_f, sa_b = _gdn_scan_fwd(qkvn, gates, gc)
    og = _out_norm(oa_f, oa_b, proj, Z_BLK, gdn_norm_w, GDN_DIM, "gdn_out")
    G_f, G_b = _gla_prep(proj, w2f, w2b, b2f, b2b)
    ob_f, ob_b, sb_f, sb_b = _gla_scan_fwd(proj, G_f, G_b)
    obg = _out_norm(ob_f, ob_b, proj, G_B_BLK, gla_norm_w, GLA_DV, "gla_out")
    (loss_row, d_wpost, dy, m, dout, dya, dyb, dga, dgb, dog, dobg) = _mid(og, obg, proj, x, target, wpg, wpl, wout, w_post)

    d_wout = _matmul(m, dout, "tn", f32, "dw_out", 512, 512, 2048)
    d_wpg = _matmul(og, dya, "tn", f32, "dw_proj_gdn", 512, 512, 2048)
    d_wpl = _matmul(obg, dyb, "tn", f32, "dw_proj_gla", 512, 512, 2048)

    do_b, d_gb, d_gla_norm = _out_norm_bwd(dobg, ob_f, ob_b, proj, G_B_BLK, gla_norm_w, GLA_DV, "gla_out_bwd")
    dq_f, dk_f, dv_f, dgk_f, dq_b, dk_b, dv_b, dgk_b = _gla_scan_bwd(proj, G_f, G_b, sb_f, sb_b, do_b)
    dsmall_gla, d_w2f, d_w2b, d_b2f, d_b2b = _gla_prep_bwd(proj, w2f, w2b, b2f, b2b, dgk_f, dgk_b)

    do_a, d_z, d_gdn_norm = _out_norm_bwd(dog, oa_f, oa_b, proj, Z_BLK, gdn_norm_w, GDN_DIM, "gdn_out_bwd")
    dqkv_f, dqkv_b, dg_f, dg_b = _gdn_scan_bwd(qkvn, gates, gc, sa_f, sa_b, do_a)
    d_qkv, d_conv = _gdn_prep_bwd(proj, conv_w, dqkv_f, dqkv_b)
    d_small, d_alog, d_dtb = _gdn_gates_bwd(proj, avec, dvec, dg_f, dg_b, dsmall_gla)

    T = x.shape[0]
    dproj = jnp.concatenate(
        [d_qkv, d_z, (dq_f + dq_b).astype(MXU_DTYPE), (dk_f + dk_b).astype(MXU_DTYPE), (dv_f + dv_b).astype(MXU_DTYPE),
         d_gb, dga, dgb, d_small, jnp.zeros((T, N_CAT - SMALL_OFF - LANES), MXU_DTYPE)], axis=1)
    d_wcat = _matmul(h, dproj, "tn", f32, "dw_in", 512, 512, 2048)
    dh = _matmul(dproj, wcat, "nt", f32, "dh", 512, 1024, 512)
    grad_x, d_wpre = _rms_pre_bwd(x, w_pre, dh, dy)
    return dict(loss_row=loss_row, grad_x=grad_x, d_wcat=d_wcat, d_conv=d_conv, d_alog=d_alog, d_dtb=d_dtb,
                d_gdn_norm=d_gdn_norm, d_wpg=d_wpg, d_w2f=d_w2f, d_w2b=d_w2b, d_b2f=d_b2f, d_b2b=d_b2b,
                d_gla_norm=d_gla_norm, d_wpl=d_wpl, d_wout=d_wout, d_wpost=d_wpost, d_wpre=d_wpre)


def _to_cat(wf):
    pad = jnp.zeros(wf.shape[:-1] + (N_CAT - N_IN,), wf.dtype)
    return jnp.concatenate([wf[..., :4096], wf[..., 4128:7200], wf[..., 7232:9280], wf[..., 4096:4128], wf[..., 7200:7232], pad], axis=-1)


def _from_cat(wc):
    return jnp.concatenate([wc[..., :4096], wc[..., 9216:9248], wc[..., 4096:7168], wc[..., 9248:9280], wc[..., 7168:9216]], axis=-1)


def _lane_row(*pieces):
    row = jnp.concatenate(pieces, axis=1)
    return jnp.pad(row, ((0, 0), (0, LANES - row.shape[1])))


def _pad_w2(w2, row0):
    return jnp.pad(w2, ((row0, LANES - row0 - w2.shape[0]), (0, 0)))


SM_ROWS, SM_COLS = 32, 384
REP_ROWS = 32


def _pack_small_shard(conv, w2f, w2b):
    lead = conv.shape[:-2]
    z = lambda r, c: jnp.zeros(lead + (r, c), f32)
    top = jnp.concatenate([conv, z(8 - CONV_K, SM_COLS)], axis=-2)
    mid = jnp.concatenate([w2f, w2b, z(16, SM_COLS - 128)], axis=-1)
    return jnp.concatenate([top, mid, z(SM_ROWS - 24, SM_COLS)], axis=-2)


def _pack_rep(ln_pre, ln_post, b2f, b2b, gla_norm, gdn_norm, alog_f, alog_b, dt_f, dt_b):
    last = _lane_row(alog_f, alog_b, dt_f, dt_b)
    rows = [ln_pre.reshape(8, LANES), ln_post.reshape(8, LANES), b2f.reshape(4, LANES), b2b.reshape(4, LANES),
            gla_norm.reshape(2, LANES), gdn_norm.reshape(1, LANES), last, jnp.zeros((REP_ROWS - 28, LANES), f32)]
    return jnp.concatenate(rows, axis=0)


def _unpack_rep(p):
    return dict(ln_pre_w=p[0:8].reshape(1, 1024), ln_post_w=p[8:16].reshape(1, 1024), gk_b2_fwd=p[16:20].reshape(1, 512),
                gk_b2_bwd=p[20:24].reshape(1, 512), gla_norm_w=p[24:26].reshape(1, 256), gdn_norm_w=p[26:27],
                a_log_fwd=p[27:28, 0:8], a_log_bwd=p[27:28, 8:16], dt_bias_fwd=p[27:28, 16:24], dt_bias_bwd=p[27:28, 24:32])


_MESH = pl.DeviceIdType.MESH


def _gather_weights(w_in_s, wp_s, sm_s):
    shapes = (w_in_s.shape, wp_s.shape, sm_s.shape)
    dtypes = (MXU_DTYPE, MXU_DTYPE, f32)

    def body(win_ref, wp_ref, sm_ref, gin_ref, gp_ref, gsm_ref, send_sems, recv_sems):
        x, y, c = lax.axis_index("x"), lax.axis_index("y"), lax.axis_index("c")
        me, sibling = (x, y, c), (x, y, 1 - c)
        chips = [(1 - x, y), (x, 1 - y), (1 - x, 1 - y)]
        bufs = (gin_ref, gp_ref, gsm_ref)

        def idx(px, py, pc):
            return 4 * px + 2 * py + pc

        rows_in = w_in_s.shape[0] // 8
        for r in range(8):
            gin_ref[idx(*me), r * rows_in : (r + 1) * rows_in, :] = win_ref[r * rows_in : (r + 1) * rows_in, :].astype(MXU_DTYPE)
        gp_ref[idx(*me)] = wp_ref[...].astype(MXU_DTYPE)
        gsm_ref[idx(*me)] = sm_ref[...]

        def copy(a, k, block, to):
            blk = bufs[a].at[idx(*block)]
            return pltpu.make_async_remote_copy(src_ref=blk, dst_ref=blk, send_sem=send_sems.at[7 * a + k],
                                                recv_sem=recv_sems.at[7 * a + k], device_id=to, device_id_type=_MESH)

        first, passed = [], []
        for a in range(3):
            first.append(copy(a, 0, me, sibling))
            first += [copy(a, 1 + j, me, (*chip, c)) for j, chip in enumerate(chips)]
        for cp in first:
            cp.start()
        for j, chip in enumerate(chips):
            for a in range(3):
                copy(a, 1 + j, (*chip, c), me).wait_recv()
                cp = copy(a, 4 + j, (*chip, c), sibling)
                cp.start()
                passed.append(cp)
        for a in range(3):
            copy(a, 0, sibling, me).wait_recv()
            for j, chip in enumerate(chips):
                copy(a, 4 + j, (*chip, 1 - c), me).wait_recv()
        for cp in first + passed:
            cp.wait_send()

    vm = pl.BlockSpec(memory_space=pltpu.VMEM)
    return pl.pallas_call(
        body,
        out_shape=tuple(jax.ShapeDtypeStruct((N_DEV,) + s, d) for s, d in zip(shapes, dtypes)),
        in_specs=[vm, vm, vm],
        out_specs=(vm, vm, vm),
        scratch_shapes=[pltpu.SemaphoreType.DMA((21,)), pltpu.SemaphoreType.DMA((21,))],
        compiler_params=pltpu.CompilerParams(vmem_limit_bytes=VMEM_LIMIT),
        name="gather_weights",
    )(w_in_s, wp_s, sm_s)


def _exchange_grads(g_in, g_p, g_sm, g_rep):
    srcs = (g_in, g_p, g_sm, g_rep)

    def body(in0, in1, in2, in3, out0, out1, out2, out3, send_sems, recv_sems, local_sems):
        x, y, c = lax.axis_index("x"), lax.axis_index("y"), lax.axis_index("c")
        me = 4 * x + 2 * y + c
        ins, outs = (in0, in1, in2, in3), (out0, out1, out2, out3)

        def peer(k):
            px = (1 - x) if (k & 4) else x
            py = (1 - y) if (k & 2) else y
            pc = (1 - c) if (k & 1) else c
            return (px, py, pc), 4 * px + 2 * py + pc

        def copy(a, k):
            dev, p = peer(k)
            src = ins[a] if a == 3 else ins[a].at[p]
            return pltpu.make_async_remote_copy(src_ref=src, dst_ref=outs[a].at[me], send_sem=send_sems.at[7 * a + k - 1],
                                                recv_sem=recv_sems.at[7 * a + k - 1], device_id=dev, device_id_type=_MESH)

        def arrival(a, k):
            dev, p = peer(k)
            src = ins[a] if a == 3 else ins[a].at[p]
            return pltpu.make_async_remote_copy(src_ref=src, dst_ref=outs[a].at[p], send_sem=send_sems.at[7 * a + k - 1],
                                                recv_sem=recv_sems.at[7 * a + k - 1], device_id=dev, device_id_type=_MESH)

        sends = [copy(a, k) for a in range(4) for k in range(1, 8)]
        for cp in sends:
            cp.start()
        local = [pltpu.make_async_copy(ins[a] if a == 3 else ins[a].at[me], outs[a].at[me], local_sems.at[a]) for a in range(4)]
        for cp in local:
            cp.start()
        for a in range(4):
            for k in range(1, 8):
                arrival(a, k).wait_recv()
        for cp in sends:
            cp.wait_send()
        for cp in local:
            cp.wait()

    hbm = pl.BlockSpec(memory_space=pltpu.HBM)
    out_shape = tuple(jax.ShapeDtypeStruct(s.shape if i < 3 else (N_DEV,) + s.shape, s.dtype) for i, s in enumerate(srcs))
    return pl.pallas_call(
        body,
        out_shape=out_shape,
        in_specs=[hbm] * 4,
        out_specs=(hbm,) * 4,
        scratch_shapes=[pltpu.SemaphoreType.DMA((28,)), pltpu.SemaphoreType.DMA((28,)), pltpu.SemaphoreType.DMA((4,))],
        name="exchange_grads",
    )(*srcs)


def _sum_adam(parts, w, m, v, name, tb):
    R, Cc = w.shape
    tb = min(tb, R)
    assert R % tb == 0
    bc1 = 1.0 - ADAM_B1**ADAM_STEP
    bc2 = 1.0 - ADAM_B2**ADAM_STEP

    def body(p_ref, w_ref, m_ref, v_ref, g_ref, d_ref, nm_ref, nv_ref):
        g = p_ref[0].astype(f32)
        for j in range(1, N_DEV):
            g = g + p_ref[j].astype(f32)
        g_ref[...] = g
        m2 = ADAM_B1 * m_ref[...] + (1.0 - ADAM_B1) * g
        v2 = ADAM_B2 * v_ref[...] + (1.0 - ADAM_B2) * (g * g)
        nm_ref[...] = m2
        nv_ref[...] = v2
        d_ref[...] = -ADAM_LR * ((m2 / bc1) / (jnp.sqrt(v2 / bc2) + ADAM_EPS) + ADAM_WD * w_ref[...])

    blk = pl.BlockSpec((tb, Cc), lambda i: (i, 0))
    o = jax.ShapeDtypeStruct((R, Cc), f32)
    return pl.pallas_call(
        body,
        out_shape=(o, o, o, o),
        grid=(R // tb,),
        in_specs=[pl.BlockSpec((N_DEV, tb, Cc), lambda i: (0, i, 0)), blk, blk, blk],
        out_specs=(blk, blk, blk, blk),
        compiler_params=_cparams(("parallel",)),
        name=name,
    )(parts, w, m, v)


_WEIGHTS = ["ln_pre_w", "w_in", "conv_w", "a_log_fwd", "a_log_bwd", "dt_bias_fwd", "dt_bias_bwd", "gdn_norm_w", "w_proj_gdn",
            "gk_w2_fwd", "gk_b2_fwd", "gk_w2_bwd", "gk_b2_bwd", "gla_norm_w", "w_proj_gla", "w_out", "ln_post_w"]
_REP_ORDER = ["ln_pre_w", "ln_post_w", "gk_b2_fwd", "gk_b2_bwd", "gla_norm_w", "gdn_norm_w", "a_log_fwd", "a_log_bwd", "dt_bias_fwd", "dt_bias_bwd"]


def kernel(x, ln_pre_w, w_in, conv_w, a_log_fwd, a_log_bwd, dt_bias_fwd, dt_bias_bwd, gdn_norm_w, w_proj_gdn, gk_w2_fwd, gk_b2_fwd, gk_w2_bwd, gk_b2_bwd, gla_norm_w, w_proj_gla, w_out, ln_post_w, loss_target, m_ln_pre_w, m_w_in, m_conv_w, m_a_log_fwd, m_a_log_bwd, m_dt_bias_fwd, m_dt_bias_bwd, m_gdn_norm_w, m_w_proj_gdn, m_gk_w2_fwd, m_gk_b2_fwd, m_gk_w2_bwd, m_gk_b2_bwd, m_gla_norm_w, m_w_proj_gla, m_w_out, m_ln_post_w, v_ln_pre_w, v_w_in, v_conv_w, v_a_log_fwd, v_a_log_bwd, v_dt_bias_fwd, v_dt_bias_bwd, v_gdn_norm_w, v_w_proj_gdn, v_gk_w2_fwd, v_gk_b2_fwd, v_gk_w2_bwd, v_gk_b2_bwd, v_gla_norm_w, v_w_proj_gla, v_w_out, v_ln_post_w):
    args = locals()
    W = {n: args[n] for n in _WEIGHTS}
    M = {n: args["m_" + n] for n in _WEIGHTS}
    V = {n: args["v_" + n] for n in _WEIGHTS}

    wp_stack = lambda P: jnp.concatenate([P["w_proj_gdn"], P["w_proj_gla"], P["w_out"]], axis=0)
    sm_pack = lambda P: _pack_small_shard(P["conv_w"][0], P["gk_w2_fwd"][0], P["gk_w2_bwd"][0])
    g_in, g_p, g_sm = _gather_weights(w_in[0], wp_stack(W), sm_pack(W))
    wcat = _to_cat(g_in.transpose(1, 0, 2).reshape(D_MODEL, N_IN))
    wpg, wpl, wout = (g_p[:, i].reshape(D_MODEL, D_MODEL) for i in range(3))
    conv_full = g_sm[:, 0:CONV_K, :].transpose(1, 0, 2).reshape(CONV_K, N_DEV * SM_COLS)
    w2f_full = g_sm[:, 8:24, 0:64].transpose(1, 0, 2).reshape(16, 512)
    w2b_full = g_sm[:, 8:24, 64:128].transpose(1, 0, 2).reshape(16, 512)

    avec = _lane_row(a_log_fwd, a_log_bwd)
    dvec = _lane_row(dt_bias_fwd, dt_bias_bwd)
    G = _local_step(x[0], loss_target[0], wcat, conv_full, avec, dvec, gdn_norm_w, _pad_w2(w2f_full, 32), _pad_w2(w2b_full, 48),
                    gk_b2_fwd, gk_b2_bwd, gla_norm_w, wpg, wpl, wout, ln_pre_w, ln_post_w)
    loss = lax.psum(0.5 * jnp.sum(G["loss_row"]) / D_MODEL, ("x", "y", "c"))

    p_in = _from_cat(G["d_wcat"]).reshape(D_MODEL, N_DEV, SHARD_IN).transpose(1, 0, 2).astype(MXU_DTYPE)
    p_p = jnp.stack([G["d_wpg"], G["d_wpl"], G["d_wout"]]).reshape(3, N_DEV, D_MODEL // N_DEV, D_MODEL).transpose(1, 0, 2, 3).astype(MXU_DTYPE)
    p_sm = _pack_small_shard(G["d_conv"].reshape(CONV_K, N_DEV, SM_COLS).transpose(1, 0, 2),
                             G["d_w2f"][32:48].reshape(16, N_DEV, 64).transpose(1, 0, 2),
                             G["d_w2b"][48:64].reshape(16, N_DEV, 64).transpose(1, 0, 2))
    p_rep = _pack_rep(G["d_wpre"], G["d_wpost"], G["d_b2f"], G["d_b2b"], G["d_gla_norm"], G["d_gdn_norm"],
                      G["d_alog"][:, 0:8], G["d_alog"][:, 8:16], G["d_dtb"][:, 0:8], G["d_dtb"][:, 8:16])
    r_in, r_p, r_sm, r_rep = _exchange_grads(p_in, p_p, p_sm, p_rep)

    rows_p = 3 * D_MODEL // N_DEV
    o_in = _sum_adam(r_in, w_in[0], m_w_in[0], v_w_in[0], "adam_w_in", 128)
    o_p = _sum_adam(r_p.reshape(N_DEV, rows_p, D_MODEL), wp_stack(W).reshape(rows_p, D_MODEL), wp_stack(M).reshape(rows_p, D_MODEL),
                    wp_stack(V).reshape(rows_p, D_MODEL), "adam_w_proj", 128)
    o_sm = _sum_adam(r_sm, sm_pack(W), sm_pack(M), sm_pack(V), "adam_small", SM_ROWS)
    rep_pack = lambda P: _pack_rep(*[P[n] for n in _REP_ORDER])
    o_rep = _sum_adam(r_rep, rep_pack(W), rep_pack(M), rep_pack(V), "adam_rep", REP_ROWS)

    res = []
    for kind in range(4):
        rep = _unpack_rep(o_rep[kind])
        per = dict(rep)
        per["w_in"] = o_in[kind][None]
        pp = o_p[kind].reshape(3, 1, D_MODEL // N_DEV, D_MODEL)
        per["w_proj_gdn"], per["w_proj_gla"], per["w_out"] = pp[0], pp[1], pp[2]
        per["conv_w"] = o_sm[kind][None, 0:CONV_K, :]
        per["gk_w2_fwd"] = o_sm[kind][None, 8:24, 0:64]
        per["gk_w2_bwd"] = o_sm[kind][None, 8:24, 64:128]
        res.append([per[n] for n in _WEIGHTS])
    return (loss, G["grad_x"][None], *res[0], *res[1], *res[2], *res[3])
```

```python
import functools
import math

import jax
import jax.numpy as jnp
from jax import lax
from jax.experimental import pallas as pl
from jax.experimental.pallas import tpu as pltpu

f32 = jnp.float32
MXU_DTYPE = jnp.bfloat16
HI = lax.Precision.HIGHEST

D_MODEL = 1024
CHUNK = 64
NORM_EPS = 1e-6
GDN_HEADS, GDN_DIM = 8, 128
GLA_HEADS, GLA_DK, GLA_DV = 4, 128, 256
GATE_NORMALIZER = 16.0
CONV_K = 5
N_IN = 9280
N_DEV = 8
SHARD_IN = N_IN // N_DEV

N_CAT = 9728
SMALL_OFF = 9216
LANES = 128

ADAM_LR, ADAM_B1, ADAM_B2, ADAM_EPS, ADAM_WD, ADAM_STEP = 0.001, 0.9, 0.999, 1e-08, 0.01, 10

VMEM_LIMIT = 56 * 1024 * 1024


def _cparams(sem=None):
    return pltpu.CompilerParams(dimension_semantics=sem, vmem_limit_bytes=VMEM_LIMIT)


_DN = {"nn": (((1,), (0,)), ((), ())), "nt": (((1,), (1,)), ((), ())), "tn": (((0,), (0,)), ((), ()))}
_BDN = {"nn": (((2,), (1,)), ((0,), (0,))), "nt": (((2,), (2,)), ((0,), (0,))), "tn": (((1,), (1,)), ((0,), (0,)))}


def _dot16(a, b, kind):
    return lax.dot_general(a, b, (_DN if a.ndim == 2 else _BDN)[kind], preferred_element_type=f32)


def _mx(a, b, kind):
    return _dot16(a.astype(MXU_DTYPE), b.astype(MXU_DTYPE), kind)


def _split2(a):
    hi = a.astype(jnp.bfloat16)
    return hi, (a - hi.astype(f32)).astype(jnp.bfloat16)


def _split3(a):
    p1 = a.astype(jnp.bfloat16)
    r = a - p1.astype(f32)
    p2 = r.astype(jnp.bfloat16)
    return p1, p2, (r - p2.astype(f32)).astype(jnp.bfloat16)


def _h3s(a2, b2, kind):
    (ah, al), (bh, bl) = a2, b2
    return _dot16(ah, bh, kind) + (_dot16(ah, bl, kind) + _dot16(al, bh, kind))


def _h3(a, b, kind):
    return _h3s(_split2(a), _split2(b), kind)


def _exact01(t01, x, kind="nn"):
    t = t01.astype(jnp.bfloat16)
    x1, x2, x3 = _split3(x)
    return _dot16(t, x1, kind) + (_dot16(t, x2, kind) + _dot16(t, x3, kind))


def _colsum_col(e):
    ones = jnp.ones(e.shape[:-1] + (LANES,), jnp.bfloat16)
    e1, e2, e3 = _split3(e)
    return (_dot16(e1, ones, "tn") + (_dot16(e2, ones, "tn") + _dot16(e3, ones, "tn")))[..., 0:1]


def _iota3(shape, axis):
    return lax.broadcasted_iota(jnp.int32, shape, axis)


def _dir_masks(B, nf, Cn):
    shp = (B, Cn, Cn)
    bb, ii, jj = _iota3(shp, 0), _iota3(shp, 1), _iota3(shp, 2)
    fwd = bb < nf
    rev = jnp.logical_not(fwd)
    incl = jnp.logical_or(jnp.logical_and(fwd, ii >= jj), jnp.logical_and(rev, ii <= jj))
    strict = jnp.logical_and(incl, ii != jj)
    return incl, strict, ii == jj


def _last_row(x, nf):
    B, Cn = x.shape[0], x.shape[1]
    fwd = _iota3((B, 1, 1), 0) < nf
    return jnp.where(fwd, x[:, Cn - 1 : Cn, :], x[:, 0:1, :])


def _add_at_last_row(x, val, nf):
    B, Cn = x.shape[0], x.shape[1]
    bb, rr = _iota3((B, Cn, 1), 0), _iota3((B, Cn, 1), 1)
    at = jnp.logical_or(jnp.logical_and(bb < nf, rr == Cn - 1), jnp.logical_and(bb >= nf, rr == 0))
    return x + jnp.where(at, val, 0.0)


def _sigmoid(x):
    return jax.nn.sigmoid(x)


def _silu(x):
    return x * _sigmoid(x)


def _silu_grad(x):
    s = _sigmoid(x)
    return s * (1.0 + x * (1.0 - s))


def _softplus(x):
    u = jnp.exp(-jnp.abs(x))
    l1p = jnp.where(u < 1e-3, u * (1.0 - u * (0.5 - u * (1.0 / 3.0))), jnp.log(1.0 + u))
    return jnp.maximum(x, 0.0) + l1p


def _iota2(shape, axis):
    return lax.broadcasted_iota(jnp.int32, shape, axis)


def _matmul(a, b, kind, out_dtype, name, tm, tn, tk):
    if kind == "nn":
        (M, K), N = a.shape, b.shape[1]
    elif kind == "nt":
        (M, K), N = a.shape, b.shape[0]
    else:
        (K, M), N = a.shape, b.shape[1]
    tm, tn, tk = min(tm, M), min(tn, N), min(tk, K)
    assert M % tm == 0 and N % tn == 0 and K % tk == 0, (name, M, N, K)
    nk = K // tk
    if kind == "tn":
        a_spec = pl.BlockSpec((tk, tm), lambda i, j, k: (k, i))
    else:
        a_spec = pl.BlockSpec((tm, tk), lambda i, j, k: (i, k))
    if kind == "nt":
        b_spec = pl.BlockSpec((tn, tk), lambda i, j, k: (j, k))
    else:
        b_spec = pl.BlockSpec((tk, tn), lambda i, j, k: (k, j))

    def body(a_ref, b_ref, o_ref, *acc):
        p = _mx(a_ref[...], b_ref[...], kind)
        if nk == 1:
            o_ref[...] = p.astype(out_dtype)
        else:
            acc_ref = acc[0]
            k = pl.program_id(2)

            @pl.when(k == 0)
            def _():
                acc_ref[...] = p

            @pl.when(k > 0)
            def _():
                acc_ref[...] += p

            @pl.when(k == nk - 1)
            def _():
                o_ref[...] = acc_ref[...].astype(out_dtype)

    return pl.pallas_call(
        body,
        out_shape=jax.ShapeDtypeStruct((M, N), out_dtype),
        grid=(M // tm, N // tn, nk),
        in_specs=[a_spec, b_spec],
        out_specs=pl.BlockSpec((tm, tn), lambda i, j, k: (i, j)),
        scratch_shapes=[] if nk == 1 else [pltpu.VMEM((tm, tn), f32)],
        compiler_params=_cparams(("parallel", "parallel", "arbitrary")),
        name=name,
    )(a, b)


def _rms_pre(x, w, tb=256):
    T = x.shape[0]

    def body(x_ref, w_ref, h_ref):
        xv = x_ref[...]
        r = lax.rsqrt(jnp.mean(xv * xv, axis=1, keepdims=True) + NORM_EPS)
        h_ref[...] = (xv * r * w_ref[...]).astype(h_ref.dtype)

    return pl.pallas_call(
        body,
        out_shape=jax.ShapeDtypeStruct((T, D_MODEL), MXU_DTYPE),
        grid=(T // tb,),
        in_specs=[pl.BlockSpec((tb, D_MODEL), lambda i: (i, 0)), pl.BlockSpec((1, D_MODEL), lambda i: (0, 0))],
        out_specs=pl.BlockSpec((tb, D_MODEL), lambda i: (i, 0)),
        compiler_params=_cparams(("parallel",)),
        name="rms_pre",
    )(x, w)


def _rms_pre_bwd(x, w, dh, dy, tb=256):
    T = x.shape[0]

    def body(x_ref, w_ref, dh_ref, dy_ref, dx_ref, dw_ref):
        i = pl.program_id(0)
        xv, dhv = x_ref[...], dh_ref[...]
        r = lax.rsqrt(jnp.mean(xv * xv, axis=1, keepdims=True) + NORM_EPS)
        dhw = dhv * w_ref[...]
        dx_ref[...] = dy_ref[...] + r * dhw - xv * (r * r * r) * jnp.mean(dhw * xv, axis=1, keepdims=True)
        part = jnp.sum(dhv * xv * r, axis=0, keepdims=True)

        @pl.when(i == 0)
        def _():
            dw_ref[...] = part

        @pl.when(i > 0)
        def _():
            dw_ref[...] += part

    blk = pl.BlockSpec((tb, D_MODEL), lambda i: (i, 0))
    row = pl.BlockSpec((1, D_MODEL), lambda i: (0, 0))
    return pl.pallas_call(
        body,
        out_shape=(jax.ShapeDtypeStruct((T, D_MODEL), f32), jax.ShapeDtypeStruct((1, D_MODEL), f32)),
        grid=(T // tb,),
        in_specs=[blk, row, blk, blk],
        out_specs=(blk, row),
        compiler_params=_cparams(("arbitrary",)),
        name="rms_pre_bwd",
    )(x, w, dh, dy)


def _conv5(u, cw):
    T = u.shape[0]
    t = _iota2(u.shape, 0)
    acc = u * cw[2:3, :]
    for j in (0, 1, 3, 4):
        d = j - 2
        sh = pltpu.roll(u, (-d) % T, 0)
        valid = jnp.logical_and(t + d >= 0, t + d < T)
        acc = acc + jnp.where(valid, sh, 0.0) * cw[j : j + 1, :]
    return acc


def _gdn_prep(proj, conv_w):
    T = proj.shape[0]
    nh = GDN_HEADS

    def body(u_ref, cw_ref, o_ref):
        j = pl.program_id(0)
        s = _silu(_conv5(u_ref[...], cw_ref[...]))

        @pl.when(j < 2 * nh)
        def _():
            o_ref[0] = s * lax.rsqrt(jnp.sum(s * s, axis=1, keepdims=True) + NORM_EPS)

        @pl.when(j >= 2 * nh)
        def _():
            o_ref[0] = s

    return pl.pallas_call(
        body,
        out_shape=jax.ShapeDtypeStruct((3, T, GDN_HEADS * GDN_DIM), f32),
        grid=(3 * nh,),
        in_specs=[pl.BlockSpec((T, GDN_DIM), lambda j: (0, j)), pl.BlockSpec((CONV_K, GDN_DIM), lambda j: (0, j))],
        out_specs=pl.BlockSpec((1, T, GDN_DIM), lambda j: (j // nh, 0, j % nh)),
        compiler_params=_cparams(("parallel",)),
        name="gdn_prep",
    )(proj, conv_w)


def _gdn_prep_bwd(proj, conv_w, dqkv_f, dqkv_b):
    T = proj.shape[0]
    nh = GDN_HEADS

    def body(u_ref, cw_ref, df_ref, db_ref, du_ref, dcw_ref):
        j = pl.program_id(0)
        u, cw = u_ref[...], cw_ref[...]
        c = _conv5(u, cw)
        s = _silu(c)
        dn = df_ref[0] + db_ref[0]
        rinv = lax.rsqrt(jnp.sum(s * s, axis=1, keepdims=True) + NORM_EPS)
        ds_norm = rinv * dn - s * (rinv * rinv * rinv) * jnp.sum(dn * s, axis=1, keepdims=True)
        ds = jnp.where(j < 2 * nh, ds_norm, dn)
        dc = ds * _silu_grad(c)
        t = _iota2(u.shape, 0)
        du = dc * cw[2:3, :]
        rows = []
        for jj in range(CONV_K):
            d = jj - 2
            if d == 0:
                rows.append(jnp.sum(dc * u, axis=0, keepdims=True))
                continue
            ush = pltpu.roll(u, (-d) % T, 0)
            uvalid = jnp.logical_and(t + d >= 0, t + d < T)
            rows.append(jnp.sum(dc * jnp.where(uvalid, ush, 0.0), axis=0, keepdims=True))
            dsh = pltpu.roll(dc, d % T, 0)
            dvalid = jnp.logical_and(t - d >= 0, t - d < T)
            du = du + jnp.where(dvalid, dsh, 0.0) * cw[jj : jj + 1, :]
        du_ref[...] = du.astype(du_ref.dtype)
        for jj in range(CONV_K):
            dcw_ref[jj : jj + 1, :] = rows[jj]

    qspec = pl.BlockSpec((1, T, GDN_DIM), lambda j: (j // nh, 0, j % nh))
    return pl.pallas_call(
        body,
        out_shape=(jax.ShapeDtypeStruct((T, 3 * nh * GDN_DIM), MXU_DTYPE), jax.ShapeDtypeStruct((CONV_K, 3 * nh * GDN_DIM), f32)),
        grid=(3 * nh,),
        in_specs=[pl.BlockSpec((T, GDN_DIM), lambda j: (0, j)), pl.BlockSpec((CONV_K, GDN_DIM), lambda j: (0, j)), qspec, qspec],
        out_specs=(pl.BlockSpec((T, GDN_DIM), lambda j: (0, j)), pl.BlockSpec((CONV_K, GDN_DIM), lambda j: (0, j))),
        compiler_params=_cparams(("parallel",)),
        name="gdn_prep_bwd",
    )(proj, conv_w, dqkv_f, dqkv_b)


def _chunk_tri(n, rev):
    i, j = _iota2((n, n), 0), _iota2((n, n), 1)
    same = jnp.right_shift(i, 6) == jnp.right_shift(j, 6)
    order = (j >= i) if rev else (j <= i)
    return jnp.where(jnp.logical_and(same, order), 1.0, 0.0).astype(f32)


def _gdn_gates(proj, avec, dvec, tb=256):
    T = proj.shape[0]

    def body(s_ref, a_ref, d_ref, g_ref, gc_ref):
        small = s_ref[...]
        lane = _iota2(small.shape, 1)
        lg = -jnp.exp(a_ref[...]) * _softplus(small + d_ref[...])
        beta = _sigmoid(small)
        g_ref[...] = jnp.where(lane < 16, lg, jnp.where(lane < 32, beta, 0.0))
        lgm = jnp.where(lane < 16, lg, 0.0)
        gcf = _exact01(_chunk_tri(tb, False), lgm)
        gcr = _exact01(_chunk_tri(tb, True), lgm)
        gc_ref[...] = jnp.where(lane < 8, gcf, gcr)

    blk = pl.BlockSpec((tb, LANES), lambda i: (i, 0))
    row = pl.BlockSpec((1, LANES), lambda i: (0, 0))
    return pl.pallas_call(
        body,
        out_shape=(jax.ShapeDtypeStruct((T, LANES), f32), jax.ShapeDtypeStruct((T, LANES), f32)),
        grid=(T // tb,),
        in_specs=[pl.BlockSpec((tb, LANES), lambda i: (i, SMALL_OFF // LANES)), row, row],
        out_specs=(blk, blk),
        compiler_params=_cparams(("parallel",)),
        name="gdn_gates",
    )(proj, avec, dvec)


def _gdn_gates_bwd(proj, avec, dvec, dg_f, dg_b, dsmall_gla, tb=256):
    T = proj.shape[0]

    def body(s_ref, a_ref, d_ref, gf_ref, gb_ref, dl_ref, ds_ref, da_ref, dd_ref):
        i = pl.program_id(0)
        small = s_ref[...]
        lane = _iota2(small.shape, 1)
        dgate = gf_ref[...] + gb_ref[...]
        z = small + d_ref[...]
        nega = -jnp.exp(a_ref[...])
        dz = dgate * nega * _sigmoid(z)
        beta = _sigmoid(small)
        dsm = jnp.where(lane < 16, dz, jnp.where(lane < 32, dgate * beta * (1.0 - beta), 0.0))
        ds_ref[...] = (dsm + dl_ref[...]).astype(ds_ref.dtype)
        lg = nega * _softplus(z)
        pa = jnp.sum(jnp.where(lane < 16, dgate * lg, 0.0), axis=0, keepdims=True)
        pd = jnp.sum(jnp.where(lane < 16, dz, 0.0), axis=0, keepdims=True)

        @pl.when(i == 0)
        def _():
            da_ref[...] = pa
            dd_ref[...] = pd

        @pl.when(i > 0)
        def _():
            da_ref[...] += pa
            dd_ref[...] += pd

    blk = pl.BlockSpec((tb, LANES), lambda i: (i, 0))
    row = pl.BlockSpec((1, LANES), lambda i: (0, 0))
    return pl.pallas_call(
        body,
        out_shape=(jax.ShapeDtypeStruct((T, LANES), MXU_DTYPE), jax.ShapeDtypeStruct((1, LANES), f32), jax.ShapeDtypeStruct((1, LANES), f32)),
        grid=(T // tb,),
        in_specs=[pl.BlockSpec((tb, LANES), lambda i: (i, SMALL_OFF // LANES)), row, row, blk, blk, blk],
        out_specs=(blk, row, row),
        compiler_params=_cparams(("arbitrary",)),
        name="gdn_gates_bwd",
    )(proj, avec, dvec, dg_f, dg_b, dsmall_gla)


GDN_HB = 4


def _inv_unit_lower(L, eye):
    A = eye - L
    P2 = _split2(L)
    for _ in range(5):
        P2 = _split2(_h3s(P2, P2, "nn"))
        A = A + _h3s(_split2(A), P2, "nn")
    return A


def _lane_col(blk, idx):
    lane = _iota2(blk.shape, 1)
    return jnp.sum(jnp.where(lane == idx, blk, 0.0), axis=1, keepdims=True)


class _GdnChunk:
    def __init__(self, q, k, v, gcol, beta, rev, A=None):
        Cn = q.shape[0]
        ii, jj = _iota2((Cn, Cn), 0), _iota2((Cn, Cn), 1)
        self.incl = (ii <= jj) if rev else (ii >= jj)
        self.strict = (ii < jj) if rev else (ii > jj)
        grow = jnp.broadcast_to(gcol, (Cn, LANES)).T[0:1, :]
        diff = gcol - grow
        self.decay = jnp.where(self.incl, jnp.exp(jnp.where(self.incl, diff, 0.0)), 0.0)
        self.k, self.v, self.beta, self.gcol = k, v, beta, gcol
        self.qs = q * (GDN_DIM**-0.5)
        self.kb = k * beta
        self.L = jnp.where(self.strict, _mx(self.kb, k, "nt") * self.decay, 0.0)
        if A is None:
            A = _inv_unit_lower(self.L, jnp.where(ii == jj, 1.0, 0.0).astype(f32))
        self.A2 = _split2(A)
        self.A = A
        self.eg = jnp.exp(gcol)
        self.kbg = self.kb * self.eg
        self.sol = _h3s(self.A2, _split2(jnp.concatenate([v * beta, self.kbg], axis=1)), "nn")
        self.u = self.sol[:, :GDN_DIM]
        self.w = self.sol[:, GDN_DIM:]
        self.attn = jnp.where(self.incl, _mx(self.qs, k, "nt") * self.decay, 0.0)
        self.qd = self.qs * self.eg
        last = 0 if rev else Cn - 1
        self.last = last
        gl = gcol[last : last + 1, :]
        self.ekd = jnp.exp(gl - gcol)
        self.kd = k * self.ekd
        self.egl = jnp.exp(gl)


def _gdn_scan_fwd(qkvn, gates, gc):
    T = qkvn.shape[1]
    N = T // CHUNK
    nh, hb = GDN_HEADS, GDN_HB
    W = hb * GDN_DIM
    chains = [(d, hh) for d in range(2) for hh in range(hb)]

    def body(qf_ref, qb_ref, gf_ref, gb_ref, cf_ref, cb_ref, of_ref, ob_ref, sf_ref, sb_ref, af_ref, ab_ref, s_scr):
        n, b = pl.program_id(0), pl.program_id(1)

        @pl.when(n == 0)
        def _():
            for d, hh in chains:
                s_scr[d, b * hb + hh] = jnp.zeros((GDN_DIM, GDN_DIM), f32)

        dirs = ((qf_ref, gf_ref, cf_ref, of_ref, sf_ref, af_ref), (qb_ref, gb_ref, cb_ref, ob_ref, sb_ref, ab_ref))
        s_in = {ch: s_scr[ch[0], b * hb + ch[1]] for ch in chains}
        s_out = {}
        for d, hh in chains:
            q_ref, g_ref, c_ref, o_ref, st_ref, a_ref = dirs[d]
            h = b * hb + hh
            sl = slice(hh * GDN_DIM, (hh + 1) * GDN_DIM)
            gcol = _lane_col(c_ref[...], h + 8 * d)
            beta = _lane_col(g_ref[...], 16 + h + 8 * d)
            ck = _GdnChunk(q_ref[0, :, sl], q_ref[1, :, sl], q_ref[2, :, sl], gcol, beta, d == 1)
            S = s_in[d, hh]
            st_ref[0, hh] = S
            a_ref[0, hh] = ck.A
            v_new = ck.u - _mx(ck.w, S, "nn")
            o_ref[:, sl] = _mx(ck.qd, S, "nn") + _mx(ck.attn, v_new, "nn")
            s_out[d, hh] = S * ck.egl + _mx(ck.kd, v_new, "tn")
        for d, hh in chains:
            s_scr[d, b * hb + hh] = s_out[d, hh]

    fw, bw = (lambda n: n), (lambda n: N - 1 - n)
    qsp = lambda ci: pl.BlockSpec((3, CHUNK, W), lambda n, b: (0, ci(n), b))
    gsp = lambda ci: pl.BlockSpec((CHUNK, LANES), lambda n, b: (ci(n), 0))
    osp = lambda ci: pl.BlockSpec((CHUNK, W), lambda n, b: (ci(n), b))
    ssp = lambda ci: pl.BlockSpec((1, hb, GDN_DIM, GDN_DIM), lambda n, b: (ci(n), b, 0, 0))
    asp = lambda ci: pl.BlockSpec((1, hb, CHUNK, CHUNK), lambda n, b: (ci(n), b, 0, 0))
    o_shape = jax.ShapeDtypeStruct((T, nh * GDN_DIM), f32)
    s_shape = jax.ShapeDtypeStruct((N, nh, GDN_DIM, GDN_DIM), f32)
    a_shape = jax.ShapeDtypeStruct((N, nh, CHUNK, CHUNK), f32)
    return pl.pallas_call(
        body,
        out_shape=(o_shape, o_shape, s_shape, s_shape, a_shape, a_shape),
        grid=(N, nh // hb),
        in_specs=[qsp(fw), qsp(bw), gsp(fw), gsp(bw), gsp(fw), gsp(bw)],
        out_specs=(osp(fw), osp(bw), ssp(fw), ssp(bw), asp(fw), asp(bw)),
        scratch_shapes=[pltpu.VMEM((2, nh, GDN_DIM, GDN_DIM), f32)],
        compiler_params=_cparams(("arbitrary", "arbitrary")),
        name="gdn_scan_fwd",
    )(qkvn, qkvn, gates, gates, gc, gc)


def _gdn_scan_bwd(qkvn, gates, gc, st_f, st_b, a_f, a_b, do):
    T = qkvn.shape[1]
    N = T // CHUNK
    nh, hb = GDN_HEADS, GDN_HB
    nb = nh // hb
    W = hb * GDN_DIM
    chains = [(d, hh) for d in range(2) for hh in range(hb)]

    def body(qf_ref, qb_ref, gf_ref, gb_ref, cf_ref, cb_ref, sf_ref, sb_ref, af_ref, ab_ref, dof_ref, dob_ref,
             dqf_ref, dqb_ref, dgf_ref, dgb_ref, ds_scr):
        n, b = pl.program_id(0), pl.program_id(1)

        @pl.when(n == 0)
        def _():
            for d, hh in chains:
                ds_scr[d, b * hb + hh] = jnp.zeros((GDN_DIM, GDN_DIM), f32)

        @pl.when(b == 0)
        def _():
            dgf_ref[...] = jnp.zeros((CHUNK, LANES), f32)
            dgb_ref[...] = jnp.zeros((CHUNK, LANES), f32)

        dirs = ((qf_ref, gf_ref, cf_ref, sf_ref, af_ref, dof_ref, dqf_ref), (qb_ref, gb_ref, cb_ref, sb_ref, ab_ref, dob_ref, dqb_ref))
        ds_in = {ch: ds_scr[ch[0], b * hb + ch[1]] for ch in chains}
        ds_out = {}
        lane = _iota2((CHUNK, LANES), 1)
        acc = [dgf_ref[...], dgb_ref[...]]
        for d, hh in chains:
            q_ref, g_ref, c_ref, st_ref, a_ref, do_ref, dq_ref = dirs[d]
            h = b * hb + hh
            sl = slice(hh * GDN_DIM, (hh + 1) * GDN_DIM)
            gcol = _lane_col(c_ref[...], h + 8 * d)
            beta = _lane_col(g_ref[...], 16 + h + 8 * d)
            ck = _GdnChunk(q_ref[0, :, sl], q_ref[1, :, sl], q_ref[2, :, sl], gcol, beta, d == 1, A=a_ref[0, hh])
            k, v = ck.k, ck.v
            S = st_ref[0, hh]
            dS = ds_in[d, hh]
            dov = do_ref[:, sl]
            v_new = ck.u - _mx(ck.w, S, "nn")
            d_vnew = _mx(ck.attn, dov, "tn") + _mx(ck.kd, dS, "nn")
            d_attn = jnp.where(ck.incl, _mx(dov, v_new, "nt"), 0.0)
            d_qd = _mx(dov, S, "nt")
            d_kd = _mx(v_new, dS, "nt")
            d_gl = jnp.sum(jnp.sum(dS * S, axis=1, keepdims=True), axis=0, keepdims=True) * ck.egl
            ds_out[d, hh] = dS * ck.egl + _mx(ck.qd, dov, "tn") - _mx(ck.w, d_vnew, "tn")
            d_w = -_mx(d_vnew, S, "nt")
            d_rhs = _h3s(ck.A2, _split2(jnp.concatenate([d_vnew, d_w], axis=1)), "tn")
            d_vb, d_kbg = d_rhs[:, :GDN_DIM], d_rhs[:, GDN_DIM:]
            dL = -jnp.where(ck.strict, _h3(d_rhs, ck.sol, "nt"), 0.0)
            P = dL * ck.decay
            Q = d_attn * ck.decay
            d_kb = _mx(P, k, "nn") + d_kbg * ck.eg
            d_qs = _mx(Q, k, "nn") + d_qd * ck.eg
            d_k = _mx(P, ck.kb, "tn") + _mx(Q, ck.qs, "tn") + d_kd * ck.ekd + d_kb * ck.beta
            E = dL * ck.L + d_attn * ck.attn
            kdsum = jnp.sum(d_kd * ck.kd, axis=1, keepdims=True)
            d_gc = (jnp.sum(E, axis=1, keepdims=True) - _colsum_col(E)
                    + jnp.sum(d_kbg * ck.kbg, axis=1, keepdims=True)
                    + jnp.sum(d_qd * ck.qd, axis=1, keepdims=True) - kdsum)
            d_gl = d_gl + jnp.sum(kdsum, axis=0, keepdims=True)
            row = _iota2((CHUNK, 1), 0)
            d_gc = d_gc + jnp.where(row == ck.last, d_gl, 0.0)
            d_beta = jnp.sum(d_vb * v, axis=1, keepdims=True) + jnp.sum(d_kb * k, axis=1, keepdims=True)
            dq_ref[0, :, sl] = d_qs * (GDN_DIM**-0.5)
            dq_ref[1, :, sl] = d_k
            dq_ref[2, :, sl] = d_vb * ck.beta
            acc[d] = jnp.where(lane == h + 8 * d, d_gc, acc[d])
            acc[d] = jnp.where(lane == 16 + h + 8 * d, d_beta, acc[d])
        for d, hh in chains:
            ds_scr[d, b * hb + hh] = ds_out[d, hh]
        dgf_ref[...] = acc[0]
        dgb_ref[...] = acc[1]

        @pl.when(b == nb - 1)
        def _():
            for d, dg_ref in enumerate((dgf_ref, dgb_ref)):
                blk = dg_ref[...]
                dg_ref[...] = jnp.where(lane < 16, _exact01(_chunk_tri(CHUNK, d == 0), blk), blk)

    fw, bw = (lambda n: N - 1 - n), (lambda n: n)
    qsp = lambda ci: pl.BlockSpec((3, CHUNK, W), lambda n, b: (0, ci(n), b))
    gsp = lambda ci: pl.BlockSpec((CHUNK, LANES), lambda n, b: (ci(n), 0))
    osp = lambda ci: pl.BlockSpec((CHUNK, W), lambda n, b: (ci(n), b))
    ssp = lambda ci: pl.BlockSpec((1, hb, GDN_DIM, GDN_DIM), lambda n, b: (ci(n), b, 0, 0))
    asp = lambda ci: pl.BlockSpec((1, hb, CHUNK, CHUNK), lambda n, b: (ci(n), b, 0, 0))
    dq_shape = jax.ShapeDtypeStruct((3, T, nh * GDN_DIM), f32)
    dg_shape = jax.ShapeDtypeStruct((T, LANES), f32)
    return pl.pallas_call(
        body,
        out_shape=(dq_shape, dq_shape, dg_shape, dg_shape),
        grid=(N, nb),
        in_specs=[qsp(fw), qsp(bw), gsp(fw), gsp(bw), gsp(fw), gsp(bw), ssp(fw), ssp(bw), asp(fw), asp(bw), osp(fw), osp(bw)],
        out_specs=(qsp(fw), qsp(bw), gsp(fw), gsp(bw)),
        scratch_shapes=[pltpu.VMEM((2, nh, GDN_DIM, GDN_DIM), f32)],
        compiler_params=_cparams(("arbitrary", "arbitrary")),
        name="gdn_scan_bwd",
    )(qkvn, qkvn, gates, gates, gc, gc, st_f, st_b, a_f, a_b, do, do)


def _out_norm(o_f, o_b, proj, gate_blk, w, hd, name, tb=256):
    T, W = o_f.shape
    nh = W // hd

    def body(of_ref, ob_ref, z_ref, w_ref, y_ref):
        wv = w_ref[...]
        for i in range(nh):
            sl = slice(i * hd, (i + 1) * hd)
            o = of_ref[:, sl] + ob_ref[:, sl]
            r = lax.rsqrt(jnp.mean(o * o, axis=1, keepdims=True) + NORM_EPS)
            y_ref[:, sl] = (o * r * wv * _silu(z_ref[:, sl])).astype(y_ref.dtype)

    blk = pl.BlockSpec((tb, W), lambda i: (i, 0))
    return pl.pallas_call(
        body,
        out_shape=jax.ShapeDtypeStruct((T, W), MXU_DTYPE),
        grid=(T // tb,),
        in_specs=[blk, blk, pl.BlockSpec((tb, W), lambda i: (i, gate_blk)), pl.BlockSpec((1, hd), lambda i: (0, 0))],
        out_specs=blk,
        compiler_params=_cparams(("parallel",)),
        name=name,
    )(o_f, o_b, proj, w)


def _out_norm_bwd(dy, o_f, o_b, proj, gate_blk, w, hd, name, tb=256):
    T, W = o_f.shape
    nh = W // hd

    def body(dy_ref, of_ref, ob_ref, z_ref, w_ref, do_ref, dz_ref, dw_ref):
        i = pl.program_id(0)
        wv = w_ref[...]
        dw = jnp.zeros((1, hd), f32)
        for a in range(nh):
            sl = slice(a * hd, (a + 1) * hd)
            o = of_ref[:, sl] + ob_ref[:, sl]
            z = z_ref[:, sl]
            dyv = dy_ref[:, sl]
            r = lax.rsqrt(jnp.mean(o * o, axis=1, keepdims=True) + NORM_EPS)
            orr = o * r
            dn = dyv * _silu(z)
            dz_ref[:, sl] = (dyv * orr * wv * _silu_grad(z)).astype(dz_ref.dtype)
            dw = dw + jnp.sum(dn * orr, axis=0, keepdims=True)
            dnw = dn * wv
            do_ref[:, sl] = r * dnw - o * (r * r * r) * jnp.mean(dnw * o, axis=1, keepdims=True)

        @pl.when(i == 0)
        def _():
            dw_ref[...] = dw

        @pl.when(i > 0)
        def _():
            dw_ref[...] += dw

    blk = pl.BlockSpec((tb, W), lambda i: (i, 0))
    row = pl.BlockSpec((1, hd), lambda i: (0, 0))
    return pl.pallas_call(
        body,
        out_shape=(jax.ShapeDtypeStruct((T, W), f32), jax.ShapeDtypeStruct((T, W), MXU_DTYPE), jax.ShapeDtypeStruct((1, hd), f32)),
        grid=(T // tb,),
        in_specs=[blk, blk, blk, pl.BlockSpec((tb, W), lambda i: (i, gate_blk)), row],
        out_specs=(blk, blk, row),
        compiler_params=_cparams(("arbitrary",)),
        name=name,
    )(dy, o_f, o_b, proj, w)


def _gla_prep(proj, w2f, w2b, b2f, b2b, tb=256):
    T = proj.shape[0]
    W = GLA_HEADS * GLA_DK

    def body(s_ref, wf_ref, wb_ref, bf_ref, bb_ref, gf_ref, gb_ref):
        small = s_ref[...]
        gkf = -_softplus(-(_mx(small, wf_ref[...], "nn") + bf_ref[...])) * (1.0 / GATE_NORMALIZER)
        gkb = -_softplus(-(_mx(small, wb_ref[...], "nn") + bb_ref[...])) * (1.0 / GATE_NORMALIZER)
        gf_ref[...] = _exact01(_chunk_tri(tb, False), gkf)
        gb_ref[...] = _exact01(_chunk_tri(tb, True), gkb)

    blk = pl.BlockSpec((tb, W), lambda i: (i, 0))
    wsp = pl.BlockSpec((LANES, W), lambda i: (0, 0))
    row = pl.BlockSpec((1, W), lambda i: (0, 0))
    return pl.pallas_call(
        body,
        out_shape=(jax.ShapeDtypeStruct((T, W), f32), jax.ShapeDtypeStruct((T, W), f32)),
        grid=(T // tb,),
        in_specs=[pl.BlockSpec((tb, LANES), lambda i: (i, SMALL_OFF // LANES)), wsp, wsp, row, row],
        out_specs=(blk, blk),
        compiler_params=_cparams(("parallel",)),
        name="gla_prep",
    )(proj, w2f, w2b, b2f, b2b)


def _gla_prep_bwd(proj, w2f, w2b, b2f, b2b, dgk_f, dgk_b, tb=256):
    T = proj.shape[0]
    W = GLA_HEADS * GLA_DK

    def body(s_ref, wf_ref, wb_ref, bf_ref, bb_ref, df_ref, db_ref, ds_ref, dwf_ref, dwb_ref, dbf_ref, dbb_ref):
        i = pl.program_id(0)
        small = s_ref[...]
        dsm = jnp.zeros((tb, LANES), f32)
        parts = []
        for w_ref, b_ref, d_ref in ((wf_ref, bf_ref, df_ref), (wb_ref, bb_ref, db_ref)):
            pre = _mx(small, w_ref[...], "nn") + b_ref[...]
            dpre = d_ref[...] * (1.0 / GATE_NORMALIZER) * _sigmoid(-pre)
            dsm = dsm + _mx(dpre, w_ref[...], "nt")
            parts.append((_mx(small, dpre, "tn"), jnp.sum(dpre, axis=0, keepdims=True)))
        ds_ref[...] = dsm

        @pl.when(i == 0)
        def _():
            dwf_ref[...], dbf_ref[...] = parts[0]
            dwb_ref[...], dbb_ref[...] = parts[1]

        @pl.when(i > 0)
        def _():
            dwf_ref[...] += parts[0][0]
            dbf_ref[...] += parts[0][1]
            dwb_ref[...] += parts[1][0]
            dbb_ref[...] += parts[1][1]

    blk = pl.BlockSpec((tb, W), lambda i: (i, 0))
    wsp = pl.BlockSpec((LANES, W), lambda i: (0, 0))
    row = pl.BlockSpec((1, W), lambda i: (0, 0))
    return pl.pallas_call(
        body,
        out_shape=(jax.ShapeDtypeStruct((T, LANES), f32), jax.ShapeDtypeStruct((LANES, W), f32), jax.ShapeDtypeStruct((LANES, W), f32),
                   jax.ShapeDtypeStruct((1, W), f32), jax.ShapeDtypeStruct((1, W), f32)),
        grid=(T // tb,),
        in_specs=[pl.BlockSpec((tb, LANES), lambda i: (i, SMALL_OFF // LANES)), wsp, wsp, row, row, blk, blk],
        out_specs=(pl.BlockSpec((tb, LANES), lambda i: (i, 0)), wsp, wsp, row, row),
        compiler_params=_cparams(("arbitrary",)),
        name="gla_prep_bwd",
    )(proj, w2f, w2b, b2f, b2b, dgk_f, dgk_b)


Q_B_BLK, K_B_BLK, V_B_BLK = 4096 // GLA_DK, 4608 // GLA_DK, 5120 // GLA_DV


class _GlaChunk:
    def __init__(self, q, k, G, rev):
        Cn = q.shape[0]
        ii, jj = _iota2((Cn, Cn), 0), _iota2((Cn, Cn), 1)
        self.incl = (ii <= jj) if rev else (ii >= jj)
        self.eG = jnp.exp(G)
        self.enG = jnp.exp(-G)
        self.qg = q * (GLA_DK**-0.5) * self.eG
        self.kg = k * self.enG
        self.attn = jnp.where(self.incl, _mx(self.qg, self.kg, "nt"), 0.0)
        last = 0 if rev else Cn - 1
        self.last = last
        gl = G[last : last + 1, :]
        self.ekd = jnp.exp(gl - G)
        self.kd = k * self.ekd
        self.egl = jnp.exp(gl)


def _gla_scan_fwd(proj, G_f, G_b):
    T = proj.shape[0]
    N = T // CHUNK
    nh = GLA_HEADS

    def body(qf_ref, kf_ref, vf_ref, gf_ref, qb_ref, kb_ref, vb_ref, gb_ref, of_ref, ob_ref, sf_ref, sb_ref, s_scr):
        n, h = pl.program_id(0), pl.program_id(1)

        @pl.when(n == 0)
        def _():
            s_scr[0, h] = jnp.zeros((GLA_DV, GLA_DK), f32)
            s_scr[1, h] = jnp.zeros((GLA_DV, GLA_DK), f32)

        for d, (q_ref, k_ref, v_ref, g_ref, o_ref, st_ref) in enumerate(
            ((qf_ref, kf_ref, vf_ref, gf_ref, of_ref, sf_ref), (qb_ref, kb_ref, vb_ref, gb_ref, ob_ref, sb_ref))
        ):
            ck = _GlaChunk(q_ref[...], k_ref[...], g_ref[...], d == 1)
            v = v_ref[...]
            St = s_scr[d, h]
            st_ref[0, 0] = St
            o_ref[...] = _mx(ck.qg, St, "nt") + _mx(ck.attn, v, "nn")
            s_scr[d, h] = St * ck.egl + _mx(v, ck.kd, "tn")

    def specs(cidx):
        return [
            pl.BlockSpec((CHUNK, GLA_DK), lambda n, h: (cidx(n), Q_B_BLK + h)),
            pl.BlockSpec((CHUNK, GLA_DK), lambda n, h: (cidx(n), K_B_BLK + h)),
            pl.BlockSpec((CHUNK, GLA_DV), lambda n, h: (cidx(n), V_B_BLK + h)),
            pl.BlockSpec((CHUNK, GLA_DK), lambda n, h: (cidx(n), h)),
        ]

    fw, bw = (lambda n: n), (lambda n: N - 1 - n)
    o_shape = jax.ShapeDtypeStruct((T, nh * GLA_DV), f32)
    s_shape = jax.ShapeDtypeStruct((N, nh, GLA_DV, GLA_DK), f32)
    return pl.pallas_call(
        body,
        out_shape=(o_shape, o_shape, s_shape, s_shape),
        grid=(N, nh),
        in_specs=specs(fw) + specs(bw),
        out_specs=(
            pl.BlockSpec((CHUNK, GLA_DV), lambda n, h: (n, h)),
            pl.BlockSpec((CHUNK, GLA_DV), lambda n, h: (N - 1 - n, h)),
            pl.BlockSpec((1, 1, GLA_DV, GLA_DK), lambda n, h: (n, h, 0, 0)),
            pl.BlockSpec((1, 1, GLA_DV, GLA_DK), lambda n, h: (N - 1 - n, h, 0, 0)),
        ),
        scratch_shapes=[pltpu.VMEM((2, nh, GLA_DV, GLA_DK), f32)],
        compiler_params=_cparams(("arbitrary", "arbitrary")),
        name="gla_scan_fwd",
    )(proj, proj, proj, G_f, proj, proj, proj, G_b)


def _gla_scan_bwd(proj, G_f, G_b, st_f, st_b, do):
    T = proj.shape[0]
    N = T // CHUNK
    nh = GLA_HEADS

    def body(qf_ref, kf_ref, vf_ref, gf_ref, sf_ref, dof_ref, qb_ref, kb_ref, vb_ref, gb_ref, sb_ref, dob_ref,
             dqf_ref, dkf_ref, dvf_ref, dgf_ref, dqb_ref, dkb_ref, dvb_ref, dgb_ref, ds_scr):
        n, h = pl.program_id(0), pl.program_id(1)

        @pl.when(n == 0)
        def _():
            ds_scr[0, h] = jnp.zeros((GLA_DV, GLA_DK), f32)
            ds_scr[1, h] = jnp.zeros((GLA_DV, GLA_DK), f32)

        for d, (q_ref, k_ref, v_ref, g_ref, st_ref, do_ref, dq_ref, dk_ref, dv_ref, dg_ref) in enumerate(
            ((qf_ref, kf_ref, vf_ref, gf_ref, sf_ref, dof_ref, dqf_ref, dkf_ref, dvf_ref, dgf_ref),
             (qb_ref, kb_ref, vb_ref, gb_ref, sb_ref, dob_ref, dqb_ref, dkb_ref, dvb_ref, dgb_ref))
        ):
            rev = d == 1
            ck = _GlaChunk(q_ref[...], k_ref[...], g_ref[...], rev)
            v = v_ref[...]
            St = st_ref[0, 0]
            dSt = ds_scr[d, h]
            dov = do_ref[...]
            d_attn = jnp.where(ck.incl, _mx(dov, v, "nt"), 0.0)
            d_qg = _mx(dov, St, "nn") + _mx(d_attn, ck.kg, "nn")
            d_kg = _mx(d_attn, ck.qg, "tn")
            dv_ref[...] = _mx(ck.attn, dov, "tn") + _mx(ck.kd, dSt, "nt")
            d_kd = _mx(v, dSt, "nn")
            ds_scr[d, h] = dSt * ck.egl + _mx(dov, ck.qg, "tn")
            kdd = d_kd * ck.kd
            d_gl = jnp.sum(dSt * St, axis=0, keepdims=True) * ck.egl + jnp.sum(kdd, axis=0, keepdims=True)
            dq_ref[...] = d_qg * ck.eG * (GLA_DK**-0.5)
            dk_ref[...] = d_kg * ck.enG + d_kd * ck.ekd
            d_G = d_qg * ck.qg - d_kg * ck.kg - kdd
            row = _iota2((CHUNK, GLA_DK), 0)
            d_G = d_G + jnp.where(row == ck.last, d_gl, 0.0)
            dg_ref[...] = _exact01(_chunk_tri(CHUNK, not rev), d_G)

    def in_specs(cidx):
        return [
            pl.BlockSpec((CHUNK, GLA_DK), lambda n, h: (cidx(n), Q_B_BLK + h)),
            pl.BlockSpec((CHUNK, GLA_DK), lambda n, h: (cidx(n), K_B_BLK + h)),
            pl.BlockSpec((CHUNK, GLA_DV), lambda n, h: (cidx(n), V_B_BLK + h)),
            pl.BlockSpec((CHUNK, GLA_DK), lambda n, h: (cidx(n), h)),
            pl.BlockSpec((1, 1, GLA_DV, GLA_DK), lambda n, h: (cidx(n), h, 0, 0)),
            pl.BlockSpec((CHUNK, GLA_DV), lambda n, h: (cidx(n), h)),
        ]

    def out_specs(cidx):
        return [
            pl.BlockSpec((CHUNK, GLA_DK), lambda n, h: (cidx(n), h)),
            pl.BlockSpec((CHUNK, GLA_DK), lambda n, h: (cidx(n), h)),
            pl.BlockSpec((CHUNK, GLA_DV), lambda n, h: (cidx(n), h)),
            pl.BlockSpec((CHUNK, GLA_DK), lambda n, h: (cidx(n), h)),
        ]

    fw, bw = (lambda n: N - 1 - n), (lambda n: n)
    k_shape = jax.ShapeDtypeStruct((T, nh * GLA_DK), f32)
    v_shape = jax.ShapeDtypeStruct((T, nh * GLA_DV), f32)
    return pl.pallas_call(
        body,
        out_shape=(k_shape, k_shape, v_shape, k_shape) * 2,
        grid=(N, nh),
        in_specs=in_specs(fw) + in_specs(bw),
        out_specs=tuple(out_specs(fw) + out_specs(bw)),
        scratch_shapes=[pltpu.VMEM((2, nh, GLA_DV, GLA_DK), f32)],
        compiler_params=_cparams(("arbitrary", "arbitrary")),
        name="gla_scan_bwd",
    )(proj, proj, proj, G_f, st_f, do, proj, proj, proj, G_b, st_b, do)


GATE_A_BLK, GATE_B_BLK, Z_BLK, G_B_BLK = 7168 // 1024, 8192 // 1024, 3072 // 1024, 6144 // 1024


def _mid(og, obg, proj, x, target, wpg, wpl, wout, w_post, tb=256):
    T = x.shape[0]
    Dm = D_MODEL

    def body(og_ref, obg_ref, ga_ref, gb_ref, x_ref, t_ref, wpg_ref, wpl_ref, wo_ref, wp_ref,
             loss_ref, dwp_ref, dy_ref, m_ref, dout_ref, dya_ref, dyb_ref, dga_ref, dgb_ref, dog_ref, dobg_ref):
        i = pl.program_id(0)
        ya = _mx(og_ref[...], wpg_ref[...], "nn")
        yb = _mx(obg_ref[...], wpl_ref[...], "nn")
        sa, sb = _sigmoid(ga_ref[...]), _sigmoid(gb_ref[...])
        m = sa * ya + sb * yb
        m_ref[...] = m.astype(m_ref.dtype)
        out = _mx(m, wo_ref[...], "nn")
        r = lax.rsqrt(jnp.mean(out * out, axis=1, keepdims=True) + NORM_EPS)
        wp = wp_ref[...]
        e = x_ref[...] + out * r * wp - t_ref[...]
        dy = e * (1.0 / Dm)
        dy_ref[...] = dy
        lpart = jnp.sum(e * e, axis=0, keepdims=True)
        wpart = jnp.sum(dy * out * r, axis=0, keepdims=True)
        dyn = dy * wp
        dout = r * dyn - out * (r * r * r) * jnp.mean(dyn * out, axis=1, keepdims=True)
        dout_ref[...] = dout.astype(dout_ref.dtype)
        dm = _mx(dout, wo_ref[...], "nt")
        dya, dyb = dm * sa, dm * sb
        dya_ref[...] = dya.astype(dya_ref.dtype)
        dyb_ref[...] = dyb.astype(dyb_ref.dtype)
        dga_ref[...] = (dm * ya * sa * (1.0 - sa)).astype(dga_ref.dtype)
        dgb_ref[...] = (dm * yb * sb * (1.0 - sb)).astype(dgb_ref.dtype)
        dog_ref[...] = _mx(dya, wpg_ref[...], "nt")
        dobg_ref[...] = _mx(dyb, wpl_ref[...], "nt")

        @pl.when(i == 0)
        def _():
            loss_ref[...] = lpart
            dwp_ref[...] = wpart

        @pl.when(i > 0)
        def _():
            loss_ref[...] += lpart
            dwp_ref[...] += wpart

    blk = pl.BlockSpec((tb, Dm), lambda i: (i, 0))
    row = pl.BlockSpec((1, Dm), lambda i: (0, 0))
    wsp = pl.BlockSpec((Dm, Dm), lambda i: (0, 0))
    act = jax.ShapeDtypeStruct((T, Dm), MXU_DTYPE)
    big = jax.ShapeDtypeStruct((T, Dm), f32)
    vec = jax.ShapeDtypeStruct((1, Dm), f32)
    return pl.pallas_call(
        body,
        out_shape=(vec, vec, big, act, act, act, act, act, act, big, big),
        grid=(T // tb,),
        in_specs=[blk, blk, pl.BlockSpec((tb, Dm), lambda i: (i, GATE_A_BLK)), pl.BlockSpec((tb, Dm), lambda i: (i, GATE_B_BLK)),
                  blk, blk, wsp, wsp, wsp, row],
        out_specs=(row, row) + (blk,) * 9,
        compiler_params=_cparams(("arbitrary",)),
        name="mid",
    )(og, obg, proj, proj, x, target, wpg, wpl, wout, w_post)


def _heads(ref, lead, n, width):
    return [ref[lead + (slice(None), slice(i * width, (i + 1) * width))] for i in range(n)]


class _GdnBatch:
    def __init__(self, q, k, v, gcol, grow, beta, nf, A=None):
        B, Cn = q.shape[0], q.shape[1]
        self.nf = nf
        self.incl, self.strict, eye = _dir_masks(B, nf, Cn)
        self.decay = jnp.where(self.incl, jnp.exp(jnp.where(self.incl, gcol - grow, 0.0)), 0.0)
        self.k, self.v, self.beta = k, v, beta
        self.qs = q * (GDN_DIM**-0.5)
        self.kb = k * beta
        self.L = jnp.where(self.strict, _mx(self.kb, k, "nt") * self.decay, 0.0)
        if A is None:
            A = _inv_unit_lower(self.L, jnp.where(eye, 1.0, 0.0).astype(f32))
        self.A = A
        self.A2 = _split2(A)
        self.eg = jnp.exp(gcol)
        self.kbg = self.kb * self.eg
        self.sol = _h3s(self.A2, _split2(jnp.concatenate([v * beta, self.kbg], axis=2)), "nn")
        self.u = self.sol[:, :, :GDN_DIM]
        self.w = self.sol[:, :, GDN_DIM:]
        self.attn = jnp.where(self.incl, _mx(self.qs, k, "nt") * self.decay, 0.0)
        self.qd = self.qs * self.eg
        gl = _last_row(gcol, nf)
        self.ekd = jnp.exp(gl - gcol)
        self.kd = k * self.ekd
        self.egl = jnp.exp(gl)


def _gdn_chain_inputs(qf_ref, qb_ref, gf_ref, gb_ref, cf_ref, cb_ref, b, hb):
    qkv = [jnp.stack(_heads(qf_ref, (i,), hb, GDN_DIM) + _heads(qb_ref, (i,), hb, GDN_DIM)) for i in range(3)]
    cols, rows, betas = [], [], []
    for d, (g_ref, c_ref) in enumerate(((gf_ref, cf_ref), (gb_ref, cb_ref))):
        gates, gc = g_ref[...], c_ref[...]
        for hh in range(hb):
            h = b * hb + hh
            col = _lane_col(gc, h + 8 * d)
            cols.append(col)
            rows.append(jnp.broadcast_to(col, (CHUNK, LANES)).T[0:1, :])
            betas.append(_lane_col(gates, 16 + h + 8 * d))
    return qkv[0], qkv[1], qkv[2], jnp.stack(cols), jnp.stack(rows), jnp.stack(betas)


def _gdn_fwd(qkvn, gates, gc):
    T = qkvn.shape[1]
    N = T // CHUNK
    nh, hb = GDN_HEADS, GDN_HB
    W = hb * GDN_DIM

    def body(qf_ref, qb_ref, gf_ref, gb_ref, cf_ref, cb_ref, of_ref, ob_ref, sf_ref, sb_ref, af_ref, ab_ref, s_scr):
        n, b = pl.program_id(0), pl.program_id(1)
        hs = pl.ds(b * hb, hb)

        @pl.when(n == 0)
        def _():
            s_scr[0, hs] = jnp.zeros((hb, GDN_DIM, GDN_DIM), f32)
            s_scr[1, hs] = jnp.zeros((hb, GDN_DIM, GDN_DIM), f32)

        q, k, v, gcol, grow, beta = _gdn_chain_inputs(qf_ref, qb_ref, gf_ref, gb_ref, cf_ref, cb_ref, b, hb)
        ck = _GdnBatch(q, k, v, gcol, grow, beta, hb)
        S = jnp.concatenate([s_scr[0, hs], s_scr[1, hs]], axis=0)
        sf_ref[0], sb_ref[0] = S[:hb], S[hb:]
        af_ref[0], ab_ref[0] = ck.A[:hb], ck.A[hb:]
        v_new = ck.u - _mx(ck.w, S, "nn")
        o = _mx(ck.qd, S, "nn") + _mx(ck.attn, v_new, "nn")
        S_new = S * ck.egl + _mx(ck.kd, v_new, "tn")
        for hh in range(hb):
            sl = slice(hh * GDN_DIM, (hh + 1) * GDN_DIM)
            of_ref[:, sl] = o[hh]
            ob_ref[:, sl] = o[hb + hh]
        s_scr[0, hs] = S_new[:hb]
        s_scr[1, hs] = S_new[hb:]

    fw, bw = (lambda n: n), (lambda n: N - 1 - n)
    qsp = lambda ci: pl.BlockSpec((3, CHUNK, W), lambda n, b: (0, ci(n), b))
    gsp = lambda ci: pl.BlockSpec((CHUNK, LANES), lambda n, b: (ci(n), 0))
    osp = lambda ci: pl.BlockSpec((CHUNK, W), lambda n, b: (ci(n), b))
    ssp = lambda ci: pl.BlockSpec((1, hb, GDN_DIM, GDN_DIM), lambda n, b: (ci(n), b, 0, 0))
    asp = lambda ci: pl.BlockSpec((1, hb, CHUNK, CHUNK), lambda n, b: (ci(n), b, 0, 0))
    o_shape = jax.ShapeDtypeStruct((T, nh * GDN_DIM), f32)
    s_shape = jax.ShapeDtypeStruct((N, nh, GDN_DIM, GDN_DIM), f32)
    a_shape = jax.ShapeDtypeStruct((N, nh, CHUNK, CHUNK), f32)
    return pl.pallas_call(
        body,
        out_shape=(o_shape, o_shape, s_shape, s_shape, a_shape, a_shape),
        grid=(N, nh // hb),
        in_specs=[qsp(fw), qsp(bw), gsp(fw), gsp(bw), gsp(fw), gsp(bw)],
        out_specs=(osp(fw), osp(bw), ssp(fw), ssp(bw), asp(fw), asp(bw)),
        scratch_shapes=[pltpu.VMEM((2, nh, GDN_DIM, GDN_DIM), f32)],
        compiler_params=_cparams(("arbitrary", "arbitrary")),
        name="gdn_scan_fwd",
    )(qkvn, qkvn, gates, gates, gc, gc)


def _gdn_bwd(qkvn, gates, gc, st_f, st_b, a_f, a_b, do):
    T = qkvn.shape[1]
    N = T // CHUNK
    nh, hb = GDN_HEADS, GDN_HB
    nb = nh // hb
    W = hb * GDN_DIM

    def body(qf_ref, qb_ref, gf_ref, gb_ref, cf_ref, cb_ref, sf_ref, sb_ref, af_ref, ab_ref, dof_ref, dob_ref,
             dqf_ref, dqb_ref, dgf_ref, dgb_ref, ds_scr):
        n, b = pl.program_id(0), pl.program_id(1)
        hs = pl.ds(b * hb, hb)

        @pl.when(n == 0)
        def _():
            ds_scr[0, hs] = jnp.zeros((hb, GDN_DIM, GDN_DIM), f32)
            ds_scr[1, hs] = jnp.zeros((hb, GDN_DIM, GDN_DIM), f32)

        @pl.when(b == 0)
        def _():
            dgf_ref[...] = jnp.zeros((CHUNK, LANES), f32)
            dgb_ref[...] = jnp.zeros((CHUNK, LANES), f32)

        q, k, v, gcol, grow, beta = _gdn_chain_inputs(qf_ref, qb_ref, gf_ref, gb_ref, cf_ref, cb_ref, b, hb)
        ck = _GdnBatch(q, k, v, gcol, grow, beta, hb, A=jnp.concatenate([af_ref[0], ab_ref[0]], axis=0))
        S = jnp.concatenate([sf_ref[0], sb_ref[0]], axis=0)
        dS = jnp.concatenate([ds_scr[0, hs], ds_scr[1, hs]], axis=0)
        dov = jnp.stack(_heads(dof_ref, (), hb, GDN_DIM) + _heads(dob_ref, (), hb, GDN_DIM))
        v_new = ck.u - _mx(ck.w, S, "nn")
        d_vnew = _mx(ck.attn, dov, "tn") + _mx(ck.kd, dS, "nn")
        d_attn = jnp.where(ck.incl, _mx(dov, v_new, "nt"), 0.0)
        d_qd = _mx(dov, S, "nt")
        d_kd = _mx(v_new, dS, "nt")
        d_gl = jnp.sum(jnp.sum(dS * S, axis=2, keepdims=True), axis=1, keepdims=True) * ck.egl
        dS_new = dS * ck.egl + _mx(ck.qd, dov, "tn") - _mx(ck.w, d_vnew, "tn")
        ds_scr[0, hs] = dS_new[:hb]
        ds_scr[1, hs] = dS_new[hb:]
        d_w = -_mx(d_vnew, S, "nt")
        d_rhs = _h3s(ck.A2, _split2(jnp.concatenate([d_vnew, d_w], axis=2)), "tn")
        d_vb, d_kbg = d_rhs[:, :, :GDN_DIM], d_rhs[:, :, GDN_DIM:]
        dL = -jnp.where(ck.strict, _h3(d_rhs, ck.sol, "nt"), 0.0)
        P = dL * ck.decay
        Q = d_attn * ck.decay
        d_kb = _mx(P, k, "nn") + d_kbg * ck.eg
        d_qs = _mx(Q, k, "nn") + d_qd * ck.eg
        d_k = _mx(P, ck.kb, "tn") + _mx(Q, ck.qs, "tn") + d_kd * ck.ekd + d_kb * ck.beta
        E = dL * ck.L + d_attn * ck.attn
        kdsum = jnp.sum(d_kd * ck.kd, axis=2, keepdims=True)
        d_gc = (jnp.sum(E, axis=2, keepdims=True) - _colsum_col(E)
                + jnp.sum(d_kbg * ck.kbg, axis=2, keepdims=True)
                + jnp.sum(d_qd * ck.qd, axis=2, keepdims=True) - kdsum)
        d_gc = _add_at_last_row(d_gc, d_gl + jnp.sum(kdsum, axis=1, keepdims=True), hb)
        d_beta = jnp.sum(d_vb * v, axis=2, keepdims=True) + jnp.sum(d_kb * k, axis=2, keepdims=True)
        d_q = d_qs * (GDN_DIM**-0.5)
        d_v = d_vb * ck.beta
        lane = _iota2((CHUNK, LANES), 1)
        for d, (dq_ref, dg_ref) in enumerate(((dqf_ref, dgf_ref), (dqb_ref, dgb_ref))):
            acc = dg_ref[...]
            for hh in range(hb):
                c, h = d * hb + hh, b * hb + hh
                sl = slice(hh * GDN_DIM, (hh + 1) * GDN_DIM)
                dq_ref[0, :, sl] = d_q[c]
                dq_ref[1, :, sl] = d_k[c]
                dq_ref[2, :, sl] = d_v[c]
                acc = jnp.where(lane == h + 8 * d, d_gc[c], acc)
                acc = jnp.where(lane == 16 + h + 8 * d, d_beta[c], acc)
            dg_ref[...] = acc

        @pl.when(b == nb - 1)
        def _():
            for d, dg_ref in enumerate((dgf_ref, dgb_ref)):
                blk = dg_ref[...]
                dg_ref[...] = jnp.where(lane < 16, _exact01(_chunk_tri(CHUNK, d == 0), blk), blk)

    fw, bw = (lambda n: N - 1 - n), (lambda n: n)
    qsp = lambda ci: pl.BlockSpec((3, CHUNK, W), lambda n, b: (0, ci(n), b))
    gsp = lambda ci: pl.BlockSpec((CHUNK, LANES), lambda n, b: (ci(n), 0))
    osp = lambda ci: pl.BlockSpec((CHUNK, W), lambda n, b: (ci(n), b))
    ssp = lambda ci: pl.BlockSpec((1, hb, GDN_DIM, GDN_DIM), lambda n, b: (ci(n), b, 0, 0))
    asp = lambda ci: pl.BlockSpec((1, hb, CHUNK, CHUNK), lambda n, b: (ci(n), b, 0, 0))
    dq_shape = jax.ShapeDtypeStruct((3, T, nh * GDN_DIM), f32)
    dg_shape = jax.ShapeDtypeStruct((T, LANES), f32)
    return pl.pallas_call(
        body,
        out_shape=(dq_shape, dq_shape, dg_shape, dg_shape),
        grid=(N, nb),
        in_specs=[qsp(fw), qsp(bw), gsp(fw), gsp(bw), gsp(fw), gsp(bw), ssp(fw), ssp(bw), asp(fw), asp(bw), osp(fw), osp(bw)],
        out_specs=(qsp(fw), qsp(bw), gsp(fw), gsp(bw)),
        scratch_shapes=[pltpu.VMEM((2, nh, GDN_DIM, GDN_DIM), f32)],
        compiler_params=_cparams(("arbitrary", "arbitrary")),
        name="gdn_scan_bwd",
    )(qkvn, qkvn, gates, gates, gc, gc, st_f, st_b, a_f, a_b, do, do)


class _GlaBatch:
    def __init__(self, q, k, G, nf):
        B, Cn = q.shape[0], q.shape[1]
        self.incl, _, _ = _dir_masks(B, nf, Cn)
        self.eG = jnp.exp(G)
        self.enG = jnp.exp(-G)
        self.qg = q * (GLA_DK**-0.5) * self.eG
        self.kg = k * self.enG
        self.attn = jnp.where(self.incl, _mx(self.qg, self.kg, "nt"), 0.0)
        gl = _last_row(G, nf)
        self.ekd = jnp.exp(gl - G)
        self.kd = k * self.ekd
        self.egl = jnp.exp(gl)


def _gla_chain_inputs(refs_f, refs_b):
    nh = GLA_HEADS
    out = []
    for i, width in enumerate((GLA_DK, GLA_DK, GLA_DV, GLA_DK)):
        out.append(jnp.stack(_heads(refs_f[i], (), nh, width) + _heads(refs_b[i], (), nh, width)))
    return out


def _gla_specs(N, ci):
    nh = GLA_HEADS
    return [
        pl.BlockSpec((CHUNK, nh * GLA_DK), lambda n: (ci(n), 4096 // (nh * GLA_DK))),
        pl.BlockSpec((CHUNK, nh * GLA_DK), lambda n: (ci(n), 4608 // (nh * GLA_DK))),
        pl.BlockSpec((CHUNK, nh * GLA_DV), lambda n: (ci(n), 5120 // (nh * GLA_DV))),
        pl.BlockSpec((CHUNK, nh * GLA_DK), lambda n: (ci(n), 0)),
    ]


def _gla_fwd(proj, G_f, G_b):
    T = proj.shape[0]
    N = T // CHUNK
    nh = GLA_HEADS

    def body(qf, kf, vf, gf, qb, kb, vb, gb, of_ref, ob_ref, sf_ref, sb_ref, s_scr):
        n = pl.program_id(0)

        @pl.when(n == 0)
        def _():
            s_scr[...] = jnp.zeros((2 * nh, GLA_DV, GLA_DK), f32)

        q, k, v, G = _gla_chain_inputs((qf, kf, vf, gf), (qb, kb, vb, gb))
        ck = _GlaBatch(q, k, G, nh)
        St = s_scr[...]
        sf_ref[0], sb_ref[0] = St[:nh], St[nh:]
        o = _mx(ck.qg, St, "nt") + _mx(ck.attn, v, "nn")
        for i in range(nh):
            sl = slice(i * GLA_DV, (i + 1) * GLA_DV)
            of_ref[:, sl] = o[i]
            ob_ref[:, sl] = o[nh + i]
        s_scr[...] = St * ck.egl + _mx(v, ck.kd, "tn")

    fw, bw = (lambda n: n), (lambda n: N - 1 - n)
    osp = lambda ci: pl.BlockSpec((CHUNK, nh * GLA_DV), lambda n: (ci(n), 0))
    ssp = lambda ci: pl.BlockSpec((1, nh, GLA_DV, GLA_DK), lambda n: (ci(n), 0, 0, 0))
    o_shape = jax.ShapeDtypeStruct((T, nh * GLA_DV), f32)
    s_shape = jax.ShapeDtypeStruct((N, nh, GLA_DV, GLA_DK), f32)
    return pl.pallas_call(
        body,
        out_shape=(o_shape, o_shape, s_shape, s_shape),
        grid=(N,),
        in_specs=_gla_specs(N, fw) + _gla_specs(N, bw),
        out_specs=(osp(fw), osp(bw), ssp(fw), ssp(bw)),
        scratch_shapes=[pltpu.VMEM((2 * nh, GLA_DV, GLA_DK), f32)],
        compiler_params=_cparams(("arbitrary",)),
        name="gla_scan_fwd",
    )(proj, proj, proj, G_f, proj, proj, proj, G_b)


def _gla_bwd(proj, G_f, G_b, st_f, st_b, do):
    T = proj.shape[0]
    N = T // CHUNK
    nh = GLA_HEADS

    def body(qf, kf, vf, gf, sf_ref, dof_ref, qb, kb, vb, gb, sb_ref, dob_ref,
             dqf_ref, dkf_ref, dvf_ref, dgf_ref, dqb_ref, dkb_ref, dvb_ref, dgb_ref, ds_scr):
        n = pl.program_id(0)

        @pl.when(n == 0)
        def _():
            ds_scr[...] = jnp.zeros((2 * nh, GLA_DV, GLA_DK), f32)

        q, k, v, G = _gla_chain_inputs((qf, kf, vf, gf), (qb, kb, vb, gb))
        ck = _GlaBatch(q, k, G, nh)
        St = jnp.concatenate([sf_ref[0], sb_ref[0]], axis=0)
        dSt = ds_scr[...]
        dov = jnp.stack(_heads(dof_ref, (), nh, GLA_DV) + _heads(dob_ref, (), nh, GLA_DV))
        d_attn = jnp.where(ck.incl, _mx(dov, v, "nt"), 0.0)
        d_qg = _mx(dov, St, "nn") + _mx(d_attn, ck.kg, "nn")
        d_kg = _mx(d_attn, ck.qg, "tn")
        d_v = _mx(ck.attn, dov, "tn") + _mx(ck.kd, dSt, "nt")
        d_kd = _mx(v, dSt, "nn")
        ds_scr[...] = dSt * ck.egl + _mx(dov, ck.qg, "tn")
        kdd = d_kd * ck.kd
        d_gl = jnp.sum(dSt * St, axis=1, keepdims=True) * ck.egl + jnp.sum(kdd, axis=1, keepdims=True)
        d_q = d_qg * ck.eG * (GLA_DK**-0.5)
        d_k = d_kg * ck.enG + d_kd * ck.ekd
        d_G = _add_at_last_row(d_qg * ck.qg - d_kg * ck.kg - kdd, d_gl, nh)
        shp = (2 * nh, CHUNK, CHUNK)
        bb, ii, jj = _iota3(shp, 0), _iota3(shp, 1), _iota3(shp, 2)
        tri = jnp.logical_or(jnp.logical_and(bb < nh, jj >= ii), jnp.logical_and(bb >= nh, jj <= ii))
        d_gk = _exact01(jnp.where(tri, 1.0, 0.0), d_G)
        for d, (dq_ref, dk_ref, dv_ref, dg_ref) in enumerate(((dqf_ref, dkf_ref, dvf_ref, dgf_ref), (dqb_ref, dkb_ref, dvb_ref, dgb_ref))):
            for i in range(nh):
                c = d * nh + i
                ks, vs = slice(i * GLA_DK, (i + 1) * GLA_DK), slice(i * GLA_DV, (i + 1) * GLA_DV)
                dq_ref[:, ks] = d_q[c]
                dk_ref[:, ks] = d_k[c]
                dv_ref[:, vs] = d_v[c]
                dg_ref[:, ks] = d_gk[c]

    def in_specs(ci):
        return _gla_specs(N, ci) + [
            pl.BlockSpec((1, nh, GLA_DV, GLA_DK), lambda n: (ci(n), 0, 0, 0)),
            pl.BlockSpec((CHUNK, nh * GLA_DV), lambda n: (ci(n), 0)),
        ]

    def out_specs(ci):
        ksp = pl.BlockSpec((CHUNK, nh * GLA_DK), lambda n: (ci(n), 0))
        return [ksp, ksp, pl.BlockSpec((CHUNK, nh * GLA_DV), lambda n: (ci(n), 0)), ksp]

    fw, bw = (lambda n: N - 1 - n), (lambda n: n)
    k_shape = jax.ShapeDtypeStruct((T, nh * GLA_DK), f32)
    v_shape = jax.ShapeDtypeStruct((T, nh * GLA_DV), f32)
    return pl.pallas_call(
        body,
        out_shape=(k_shape, k_shape, v_shape, k_shape) * 2,
        grid=(N,),
        in_specs=in_specs(fw) + in_specs(bw),
        out_specs=tuple(out_specs(fw) + out_specs(bw)),
        scratch_shapes=[pltpu.VMEM((2 * nh, GLA_DV, GLA_DK), f32)],
        compiler_params=_cparams(("arbitrary",)),
        name="gla_scan_bwd",
    )(proj, proj, proj, G_f, st_f, do, proj, proj, proj, G_b, st_b, do)


def _local_step(x, target, wcat, conv_w, avec, dvec, gdn_norm_w, w2f, w2b, b2f, b2b, gla_norm_w, wpg, wpl, wout, w_pre, w_post):
    h = _rms_pre(x, w_pre)
    proj = _matmul(h, wcat, "nn", f32, "proj_in", 512, 512, 1024)
    qkvn = _gdn_prep(proj, conv_w)
    gates, gc = _gdn_gates(proj, avec, dvec)
    oa_f, oa_b, sa_f, sa_b, inv_f, inv_b = _gdn_fwd(qkvn, gates, gc)
    og = _out_norm(oa_f, oa_b, proj, Z_BLK, gdn_norm_w, GDN_DIM, "gdn_out")
    G_f, G_b = _gla_prep(proj, w2f, w2b, b2f, b2b)
    ob_f, ob_b, sb_f, sb_b = _gla_fwd(proj, G_f, G_b)
    obg = _out_norm(ob_f, ob_b, proj, G_B_BLK, gla_norm_w, GLA_DV, "gla_out")
    (loss_row, d_wpost, dy, m, dout, dya, dyb, dga, dgb, dog, dobg) = _mid(og, obg, proj, x, target, wpg, wpl, wout, w_post)

    d_wout = _matmul(m, dout, "tn", f32, "dw_out", 512, 512, 2048)
    d_wpg = _matmul(og, dya, "tn", f32, "dw_proj_gdn", 512, 512, 2048)
    d_wpl = _matmul(obg, dyb, "tn", f32, "dw_proj_gla", 512, 512, 2048)

    do_b, d_gb, d_gla_norm = _out_norm_bwd(dobg, ob_f, ob_b, proj, G_B_BLK, gla_norm_w, GLA_DV, "gla_out_bwd")
    dq_f, dk_f, dv_f, dgk_f, dq_b, dk_b, dv_b, dgk_b = _gla_bwd(proj, G_f, G_b, sb_f, sb_b, do_b)
    dsmall_gla, d_w2f, d_w2b, d_b2f, d_b2b = _gla_prep_bwd(proj, w2f, w2b, b2f, b2b, dgk_f, dgk_b)

    do_a, d_z, d_gdn_norm = _out_norm_bwd(dog, oa_f, oa_b, proj, Z_BLK, gdn_norm_w, GDN_DIM, "gdn_out_bwd")
    dqkv_f, dqkv_b, dg_f, dg_b = _gdn_bwd(qkvn, gates, gc, sa_f, sa_b, inv_f, inv_b, do_a)
    d_qkv, d_conv = _gdn_prep_bwd(proj, conv_w, dqkv_f, dqkv_b)
    d_small, d_alog, d_dtb = _gdn_gates_bwd(proj, avec, dvec, dg_f, dg_b, dsmall_gla)

    T = x.shape[0]
    dproj = jnp.concatenate(
        [d_qkv, d_z, (dq_f + dq_b).astype(MXU_DTYPE), (dk_f + dk_b).astype(MXU_DTYPE), (dv_f + dv_b).astype(MXU_DTYPE),
         d_gb, dga, dgb, d_small, jnp.zeros((T, N_CAT - SMALL_OFF - LANES), MXU_DTYPE)], axis=1)
    d_wcat = _matmul(h, dproj, "tn", f32, "dw_in", 512, 512, 2048)
    dh = _matmul(dproj, wcat, "nt", f32, "dh", 512, 1024, 512)
    grad_x, d_wpre = _rms_pre_bwd(x, w_pre, dh, dy)
    return dict(loss_row=loss_row, grad_x=grad_x, d_wcat=d_wcat, d_conv=d_conv, d_alog=d_alog, d_dtb=d_dtb,
                d_gdn_norm=d_gdn_norm, d_wpg=d_wpg, d_w2f=d_w2f, d_w2b=d_w2b, d_b2f=d_b2f, d_b2b=d_b2b,
                d_gla_norm=d_gla_norm, d_wpl=d_wpl, d_wout=d_wout, d_wpost=d_wpost, d_wpre=d_wpre)


def _to_cat(wf):
    pad = jnp.zeros(wf.shape[:-1] + (N_CAT - N_IN,), wf.dtype)
    return jnp.concatenate([wf[..., :4096], wf[..., 4128:7200], wf[..., 7232:9280], wf[..., 4096:4128], wf[..., 7200:7232], pad], axis=-1)


def _from_cat(wc):
    return jnp.concatenate([wc[..., :4096], wc[..., 9216:9248], wc[..., 4096:7168], wc[..., 9248:9280], wc[..., 7168:9216]], axis=-1)


def _lane_row(*pieces):
    row = jnp.concatenate(pieces, axis=1)
    return jnp.pad(row, ((0, 0), (0, LANES - row.shape[1])))


def _pad_w2(w2, row0):
    return jnp.pad(w2, ((row0, LANES - row0 - w2.shape[0]), (0, 0)))


SM_ROWS, SM_COLS = 32, 384
REP_ROWS = 32


def _pack_small_shard(conv, w2f, w2b):
    lead = conv.shape[:-2]
    z = lambda r, c: jnp.zeros(lead + (r, c), f32)
    top = jnp.concatenate([conv, z(8 - CONV_K, SM_COLS)], axis=-2)
    mid = jnp.concatenate([w2f, w2b, z(16, SM_COLS - 128)], axis=-1)
    return jnp.concatenate([top, mid, z(SM_ROWS - 24, SM_COLS)], axis=-2)


def _pack_rep(ln_pre, ln_post, b2f, b2b, gla_norm, gdn_norm, alog_f, alog_b, dt_f, dt_b):
    last = _lane_row(alog_f, alog_b, dt_f, dt_b)
    rows = [ln_pre.reshape(8, LANES), ln_post.reshape(8, LANES), b2f.reshape(4, LANES), b2b.reshape(4, LANES),
            gla_norm.reshape(2, LANES), gdn_norm.reshape(1, LANES), last, jnp.zeros((REP_ROWS - 28, LANES), f32)]
    return jnp.concatenate(rows, axis=0)


def _unpack_rep(p):
    return dict(ln_pre_w=p[0:8].reshape(1, 1024), ln_post_w=p[8:16].reshape(1, 1024), gk_b2_fwd=p[16:20].reshape(1, 512),
                gk_b2_bwd=p[20:24].reshape(1, 512), gla_norm_w=p[24:26].reshape(1, 256), gdn_norm_w=p[26:27],
                a_log_fwd=p[27:28, 0:8], a_log_bwd=p[27:28, 8:16], dt_bias_fwd=p[27:28, 16:24], dt_bias_bwd=p[27:28, 24:32])


_MESH = pl.DeviceIdType.MESH


def _gather_weights(w_in_s, wp_s, sm_s):
    shapes = (w_in_s.shape, wp_s.shape, sm_s.shape)
    dtypes = (MXU_DTYPE, MXU_DTYPE, f32)

    def body(win_ref, wp_ref, sm_ref, gin_ref, gp_ref, gsm_ref, send_sems, recv_sems):
        x, y, c = lax.axis_index("x"), lax.axis_index("y"), lax.axis_index("c")
        me, sibling = (x, y, c), (x, y, 1 - c)
        chips = [(1 - x, y), (x, 1 - y), (1 - x, 1 - y)]
        bufs = (gin_ref, gp_ref, gsm_ref)

        def idx(px, py, pc):
            return 4 * px + 2 * py + pc

        rows_in = w_in_s.shape[0] // 8
        for r in range(8):
            gin_ref[idx(*me), r * rows_in : (r + 1) * rows_in, :] = win_ref[r * rows_in : (r + 1) * rows_in, :].astype(MXU_DTYPE)
        gp_ref[idx(*me)] = wp_ref[...].astype(MXU_DTYPE)
        gsm_ref[idx(*me)] = sm_ref[...]

        def copy(a, k, block, to):
            blk = bufs[a].at[idx(*block)]
            return pltpu.make_async_remote_copy(src_ref=blk, dst_ref=blk, send_sem=send_sems.at[7 * a + k],
                                                recv_sem=recv_sems.at[7 * a + k], device_id=to, device_id_type=_MESH)

        first, passed = [], []
        for a in range(3):
            first.append(copy(a, 0, me, sibling))
            first += [copy(a, 1 + j, me, (*chip, c)) for j, chip in enumerate(chips)]
        for cp in first:
            cp.start()
        for j, chip in enumerate(chips):
            for a in range(3):
                copy(a, 1 + j, (*chip, c), me).wait_recv()
                cp = copy(a, 4 + j, (*chip, c), sibling)
                cp.start()
                passed.append(cp)
        for a in range(3):
            copy(a, 0, sibling, me).wait_recv()
            for j, chip in enumerate(chips):
                copy(a, 4 + j, (*chip, 1 - c), me).wait_recv()
        for cp in first + passed:
            cp.wait_send()

    vm = pl.BlockSpec(memory_space=pltpu.VMEM)
    return pl.pallas_call(
        body,
        out_shape=tuple(jax.ShapeDtypeStruct((N_DEV,) + s, d) for s, d in zip(shapes, dtypes)),
        in_specs=[vm, vm, vm],
        out_specs=(vm, vm, vm),
        scratch_shapes=[pltpu.SemaphoreType.DMA((21,)), pltpu.SemaphoreType.DMA((21,))],
        compiler_params=pltpu.CompilerParams(vmem_limit_bytes=VMEM_LIMIT),
        name="gather_weights",
    )(w_in_s, wp_s, sm_s)


def _exchange_grads(g_in, g_p, g_sm, g_rep):
    srcs = (g_in, g_p, g_sm, g_rep)

    def body(in0, in1, in2, in3, out0, out1, out2, out3, send_sems, recv_sems, local_sems):
        x, y, c = lax.axis_index("x"), lax.axis_index("y"), lax.axis_index("c")
        me = 4 * x + 2 * y + c
        ins, outs = (in0, in1, in2, in3), (out0, out1, out2, out3)

        def peer(k):
            px = (1 - x) if (k & 4) else x
            py = (1 - y) if (k & 2) else y
            pc = (1 - c) if (k & 1) else c
            return (px, py, pc), 4 * px + 2 * py + pc

        def copy(a, k):
            dev, p = peer(k)
            src = ins[a] if a == 3 else ins[a].at[p]
            return pltpu.make_async_remote_copy(src_ref=src, dst_ref=outs[a].at[me], send_sem=send_sems.at[7 * a + k - 1],
                                                recv_sem=recv_sems.at[7 * a + k - 1], device_id=dev, device_id_type=_MESH)

        def arrival(a, k):
            dev, p = peer(k)
            src = ins[a] if a == 3 else ins[a].at[p]
            return pltpu.make_async_remote_copy(src_ref=src, dst_ref=outs[a].at[p], send_sem=send_sems.at[7 * a + k - 1],
                                                recv_sem=recv_sems.at[7 * a + k - 1], device_id=dev, device_id_type=_MESH)

        sends = [copy(a, k) for a in range(4) for k in range(1, 8)]
        for cp in sends:
            cp.start()
        local = [pltpu.make_async_copy(ins[a] if a == 3 else ins[a].at[me], outs[a].at[me], local_sems.at[a]) for a in range(4)]
        for cp in local:
            cp.start()
        for a in range(4):
            for k in range(1, 8):
                arrival(a, k).wait_recv()
        for cp in sends:
            cp.wait_send()
        for cp in local:
            cp.wait()

    hbm = pl.BlockSpec(memory_space=pltpu.HBM)
    out_shape = tuple(jax.ShapeDtypeStruct(s.shape if i < 3 else (N_DEV,) + s.shape, s.dtype) for i, s in enumerate(srcs))
    return pl.pallas_call(
        body,
        out_shape=out_shape,
        in_specs=[hbm] * 4,
        out_specs=(hbm,) * 4,
        scratch_shapes=[pltpu.SemaphoreType.DMA((28,)), pltpu.SemaphoreType.DMA((28,)), pltpu.SemaphoreType.DMA((4,))],
        name="exchange_grads",
    )(*srcs)


def _sum_adam(parts, w, m, v, name, tb):
    R, Cc = w.shape
    tb = min(tb, R)
    assert R % tb == 0
    bc1 = 1.0 - ADAM_B1**ADAM_STEP
    bc2 = 1.0 - ADAM_B2**ADAM_STEP

    def body(p_ref, w_ref, m_ref, v_ref, g_ref, d_ref, nm_ref, nv_ref):
        g = p_ref[0].astype(f32)
        for j in range(1, N_DEV):
            g = g + p_ref[j].astype(f32)
        g_ref[...] = g
        m2 = ADAM_B1 * m_ref[...] + (1.0 - ADAM_B1) * g
        v2 = ADAM_B2 * v_ref[...] + (1.0 - ADAM_B2) * (g * g)
        nm_ref[...] = m2
        nv_ref[...] = v2
        d_ref[...] = -ADAM_LR * ((m2 / bc1) / (jnp.sqrt(v2 / bc2) + ADAM_EPS) + ADAM_WD * w_ref[...])

    blk = pl.BlockSpec((tb, Cc), lambda i: (i, 0))
    o = jax.ShapeDtypeStruct((R, Cc), f32)
    return pl.pallas_call(
        body,
        out_shape=(o, o, o, o),
        grid=(R // tb,),
        in_specs=[pl.BlockSpec((N_DEV, tb, Cc), lambda i: (0, i, 0)), blk, blk, blk],
        out_specs=(blk, blk, blk, blk),
        compiler_params=_cparams(("parallel",)),
        name=name,
    )(parts, w, m, v)


_WEIGHTS = ["ln_pre_w", "w_in", "conv_w", "a_log_fwd", "a_log_bwd", "dt_bias_fwd", "dt_bias_bwd", "gdn_norm_w", "w_proj_gdn",
            "gk_w2_fwd", "gk_b2_fwd", "gk_w2_bwd", "gk_b2_bwd", "gla_norm_w", "w_proj_gla", "w_out", "ln_post_w"]
_REP_ORDER = ["ln_pre_w", "ln_post_w", "gk_b2_fwd", "gk_b2_bwd", "gla_norm_w", "gdn_norm_w", "a_log_fwd", "a_log_bwd", "dt_bias_fwd", "dt_bias_bwd"]


def kernel(x, ln_pre_w, w_in, conv_w, a_log_fwd, a_log_bwd, dt_bias_fwd, dt_bias_bwd, gdn_norm_w, w_proj_gdn, gk_w2_fwd, gk_b2_fwd, gk_w2_bwd, gk_b2_bwd, gla_norm_w, w_proj_gla, w_out, ln_post_w, loss_target, m_ln_pre_w, m_w_in, m_conv_w, m_a_log_fwd, m_a_log_bwd, m_dt_bias_fwd, m_dt_bias_bwd, m_gdn_norm_w, m_w_proj_gdn, m_gk_w2_fwd, m_gk_b2_fwd, m_gk_w2_bwd, m_gk_b2_bwd, m_gla_norm_w, m_w_proj_gla, m_w_out, m_ln_post_w, v_ln_pre_w, v_w_in, v_conv_w, v_a_log_fwd, v_a_log_bwd, v_dt_bias_fwd, v_dt_bias_bwd, v_gdn_norm_w, v_w_proj_gdn, v_gk_w2_fwd, v_gk_b2_fwd, v_gk_w2_bwd, v_gk_b2_bwd, v_gla_norm_w, v_w_proj_gla, v_w_out, v_ln_post_w):
    args = locals()
    W = {n: args[n] for n in _WEIGHTS}
    M = {n: args["m_" + n] for n in _WEIGHTS}
    V = {n: args["v_" + n] for n in _WEIGHTS}

    wp_stack = lambda P: jnp.concatenate([P["w_proj_gdn"], P["w_proj_gla"], P["w_out"]], axis=0)
    sm_pack = lambda P: _pack_small_shard(P["conv_w"][0], P["gk_w2_fwd"][0], P["gk_w2_bwd"][0])
    g_in, g_p, g_sm = _gather_weights(w_in[0], wp_stack(W), sm_pack(W))
    wcat = _to_cat(g_in.transpose(1, 0, 2).reshape(D_MODEL, N_IN))
    wpg, wpl, wout = (g_p[:, i].reshape(D_MODEL, D_MODEL) for i in range(3))
    conv_full = g_sm[:, 0:CONV_K, :].transpose(1, 0, 2).reshape(CONV_K, N_DEV * SM_COLS)
    w2f_full = g_sm[:, 8:24, 0:64].transpose(1, 0, 2).reshape(16, 512)
    w2b_full = g_sm[:, 8:24, 64:128].transpose(1, 0, 2).reshape(16, 512)

    avec = _lane_row(a_log_fwd, a_log_bwd)
    dvec = _lane_row(dt_bias_fwd, dt_bias_bwd)
    G = _local_step(x[0], loss_target[0], wcat, conv_full, avec, dvec, gdn_norm_w, _pad_w2(w2f_full, 32), _pad_w2(w2b_full, 48),
                    gk_b2_fwd, gk_b2_bwd, gla_norm_w, wpg, wpl, wout, ln_pre_w, ln_post_w)
    loss = lax.psum(0.5 * jnp.sum(G["loss_row"]) / D_MODEL, ("x", "y", "c"))

    p_in = _from_cat(G["d_wcat"]).reshape(D_MODEL, N_DEV, SHARD_IN).transpose(1, 0, 2).astype(MXU_DTYPE)
    p_p = jnp.stack([G["d_wpg"], G["d_wpl"], G["d_wout"]]).reshape(3, N_DEV, D_MODEL // N_DEV, D_MODEL).transpose(1, 0, 2, 3).astype(MXU_DTYPE)
    p_sm = _pack_small_shard(G["d_conv"].reshape(CONV_K, N_DEV, SM_COLS).transpose(1, 0, 2),
                             G["d_w2f"][32:48].reshape(16, N_DEV, 64).transpose(1, 0, 2),
                             G["d_w2b"][48:64].reshape(16, N_DEV, 64).transpose(1, 0, 2))
    p_rep = _pack_rep(G["d_wpre"], G["d_wpost"], G["d_b2f"], G["d_b2b"], G["d_gla_norm"], G["d_gdn_norm"],
                      G["d_alog"][:, 0:8], G["d_alog"][:, 8:16], G["d_dtb"][:, 0:8], G["d_dtb"][:, 8:16])
    r_in, r_p, r_sm, r_rep = _exchange_grads(p_in, p_p, p_sm, p_rep)

    rows_p = 3 * D_MODEL // N_DEV
    o_in = _sum_adam(r_in, w_in[0], m_w_in[0], v_w_in[0], "adam_w_in", 128)
    o_p = _sum_adam(r_p.reshape(N_DEV, rows_p, D_MODEL), wp_stack(W).reshape(rows_p, D_MODEL), wp_stack(M).reshape(rows_p, D_MODEL),
                    wp_stack(V).reshape(rows_p, D_MODEL), "adam_w_proj", 128)
    o_sm = _sum_adam(r_sm, sm_pack(W), sm_pack(M), sm_pack(V), "adam_small", SM_ROWS)
    rep_pack = lambda P: _pack_rep(*[P[n] for n in _REP_ORDER])
    o_rep = _sum_adam(r_rep, rep_pack(W), rep_pack(M), rep_pack(V), "adam_rep", REP_ROWS)

    res = []
    for kind in range(4):
        rep = _unpack_rep(o_rep[kind])
        per = dict(rep)
        per["w_in"] = o_in[kind][None]
        pp = o_p[kind].reshape(3, 1, D_MODEL // N_DEV, D_MODEL)
        per["w_proj_gdn"], per["w_proj_gla"], per["w_out"] = pp[0], pp[1], pp[2]
        per["conv_w"] = o_sm[kind][None, 0:CONV_K, :]
        per["gk_w2_fwd"] = o_sm[kind][None, 8:24, 0:64]
        per["gk_w2_bwd"] = o_sm[kind][None, 8:24, 64:128]
        res.append([per[n] for n in _WEIGHTS])
    return (loss, G["grad_x"][None], *res[0], *res[1], *res[2], *res[3])
```

```python
import functools
import math

import jax
import jax.numpy as jnp
from jax import lax
from jax.experimental import pallas as pl
from jax.experimental.pallas import tpu as pltpu

f32 = jnp.float32
MXU_DTYPE = jnp.bfloat16
HI = lax.Precision.HIGHEST

D_MODEL = 1024
CHUNK = 64
NORM_EPS = 1e-6
GDN_HEADS, GDN_DIM = 8, 128
GLA_HEADS, GLA_DK, GLA_DV = 4, 128, 256
GATE_NORMALIZER = 16.0
CONV_K = 5
N_IN = 9280
N_DEV = 8
SHARD_IN = N_IN // N_DEV

N_CAT = 9728
SMALL_OFF = 9216
LANES = 128

ADAM_LR, ADAM_B1, ADAM_B2, ADAM_EPS, ADAM_WD, ADAM_STEP = 0.001, 0.9, 0.999, 1e-08, 0.01, 10

VMEM_LIMIT = 56 * 1024 * 1024


def _cparams(sem=None):
    return pltpu.CompilerParams(dimension_semantics=sem, vmem_limit_bytes=VMEM_LIMIT)


_DN = {"nn": (((1,), (0,)), ((), ())), "nt": (((1,), (1,)), ((), ())), "tn": (((0,), (0,)), ((), ()))}
_BDN = {"nn": (((2,), (1,)), ((0,), (0,))), "nt": (((2,), (2,)), ((0,), (0,))), "tn": (((1,), (1,)), ((0,), (0,)))}


def _dot16(a, b, kind):
    return lax.dot_general(a, b, (_DN if a.ndim == 2 else _BDN)[kind], preferred_element_type=f32)


def _mx(a, b, kind):
    return _dot16(a.astype(MXU_DTYPE), b.astype(MXU_DTYPE), kind)


def _split2(a):
    hi = a.astype(jnp.bfloat16)
    return hi, (a - hi.astype(f32)).astype(jnp.bfloat16)


def _split3(a):
    p1 = a.astype(jnp.bfloat16)
    r = a - p1.astype(f32)
    p2 = r.astype(jnp.bfloat16)
    return p1, p2, (r - p2.astype(f32)).astype(jnp.bfloat16)


def _h3s(a2, b2, kind):
    (ah, al), (bh, bl) = a2, b2
    return _dot16(ah, bh, kind) + (_dot16(ah, bl, kind) + _dot16(al, bh, kind))


def _h3(a, b, kind):
    return _h3s(_split2(a), _split2(b), kind)


def _exact01(t01, x, kind="nn"):
    t = t01.astype(jnp.bfloat16)
    x1, x2, x3 = _split3(x)
    return _dot16(t, x1, kind) + (_dot16(t, x2, kind) + _dot16(t, x3, kind))


def _colsum_col(e):
    ones = jnp.ones(e.shape[:-1] + (LANES,), jnp.bfloat16)
    e1, e2, e3 = _split3(e)
    return (_dot16(e1, ones, "tn") + (_dot16(e2, ones, "tn") + _dot16(e3, ones, "tn")))[..., 0:1]


def _iota3(shape, axis):
    return lax.broadcasted_iota(jnp.int32, shape, axis)


def _dir_masks(B, nf, Cn):
    shp = (B, Cn, Cn)
    bb, ii, jj = _iota3(shp, 0), _iota3(shp, 1), _iota3(shp, 2)
    fwd = bb < nf
    rev = jnp.logical_not(fwd)
    incl = jnp.logical_or(jnp.logical_and(fwd, ii >= jj), jnp.logical_and(rev, ii <= jj))
    strict = jnp.logical_and(incl, ii != jj)
    return incl, strict, ii == jj


def _last_row(x, nf):
    B, Cn = x.shape[0], x.shape[1]
    fwd = _iota3((B, 1, 1), 0) < nf
    return jnp.where(fwd, x[:, Cn - 1 : Cn, :], x[:, 0:1, :])


def _add_at_last_row(x, val, nf):
    B, Cn = x.shape[0], x.shape[1]
    bb, rr = _iota3((B, Cn, 1), 0), _iota3((B, Cn, 1), 1)
    at = jnp.logical_or(jnp.logical_and(bb < nf, rr == Cn - 1), jnp.logical_and(bb >= nf, rr == 0))
    return x + jnp.where(at, val, 0.0)


def _sigmoid(x):
    return jax.nn.sigmoid(x)


def _silu(x):
    return x * _sigmoid(x)


def _silu_grad(x):
    s = _sigmoid(x)
    return s * (1.0 + x * (1.0 - s))


def _softplus(x):
    u = jnp.exp(-jnp.abs(x))
    l1p = jnp.where(u < 1e-3, u * (1.0 - u * (0.5 - u * (1.0 / 3.0))), jnp.log(1.0 + u))
    return jnp.maximum(x, 0.0) + l1p


def _iota2(shape, axis):
    return lax.broadcasted_iota(jnp.int32, shape, axis)


def _matmul(a, b, kind, out_dtype, name, tm, tn, tk):
    if kind == "nn":
        (M, K), N = a.shape, b.shape[1]
    elif kind == "nt":
        (M, K), N = a.shape, b.shape[0]
    else:
        (K, M), N = a.shape, b.shape[1]
    tm, tn, tk = min(tm, M), min(tn, N), min(tk, K)
    assert M % tm == 0 and N % tn == 0 and K % tk == 0, (name, M, N, K)
    nk = K // tk
    if kind == "tn":
        a_spec = pl.BlockSpec((tk, tm), lambda i, j, k: (k, i))
    else:
        a_spec = pl.BlockSpec((tm, tk), lambda i, j, k: (i, k))
    if kind == "nt":
        b_spec = pl.BlockSpec((tn, tk), lambda i, j, k: (j, k))
    else:
        b_spec = pl.BlockSpec((tk, tn), lambda i, j, k: (k, j))

    def body(a_ref, b_ref, o_ref, *acc):
        p = _mx(a_ref[...], b_ref[...], kind)
        if nk == 1:
            o_ref[...] = p.astype(out_dtype)
        else:
            acc_ref = acc[0]
            k = pl.program_id(2)

            @pl.when(k == 0)
            def _():
                acc_ref[...] = p

            @pl.when(k > 0)
            def _():
                acc_ref[...] += p

            @pl.when(k == nk - 1)
            def _():
                o_ref[...] = acc_ref[...].astype(out_dtype)

    return pl.pallas_call(
        body,
        out_shape=jax.ShapeDtypeStruct((M, N), out_dtype),
        grid=(M // tm, N // tn, nk),
        in_specs=[a_spec, b_spec],
        out_specs=pl.BlockSpec((tm, tn), lambda i, j, k: (i, j)),
        scratch_shapes=[] if nk == 1 else [pltpu.VMEM((tm, tn), f32)],
        compiler_params=_cparams(("parallel", "parallel", "arbitrary")),
        name=name,
    )(a, b)


def _rms_pre(x, w, tb=256):
    T = x.shape[0]

    def body(x_ref, w_ref, h_ref):
        xv = x_ref[...]
        r = lax.rsqrt(jnp.mean(xv * xv, axis=1, keepdims=True) + NORM_EPS)
        h_ref[...] = (xv * r * w_ref[...]).astype(h_ref.dtype)

    return pl.pallas_call(
        body,
        out_shape=jax.ShapeDtypeStruct((T, D_MODEL), MXU_DTYPE),
        grid=(T // tb,),
        in_specs=[pl.BlockSpec((tb, D_MODEL), lambda i: (i, 0)), pl.BlockSpec((1, D_MODEL), lambda i: (0, 0))],
        out_specs=pl.BlockSpec((tb, D_MODEL), lambda i: (i, 0)),
        compiler_params=_cparams(("parallel",)),
        name="rms_pre",
    )(x, w)


def _rms_pre_bwd(x, w, dh, dy, tb=256):
    T = x.shape[0]

    def body(x_ref, w_ref, dh_ref, dy_ref, dx_ref, dw_ref):
        i = pl.program_id(0)
        xv, dhv = x_ref[...], dh_ref[...]
        r = lax.rsqrt(jnp.mean(xv * xv, axis=1, keepdims=True) + NORM_EPS)
        dhw = dhv * w_ref[...]
        dx_ref[...] = dy_ref[...] + r * dhw - xv * (r * r * r) * jnp.mean(dhw * xv, axis=1, keepdims=True)
        part = jnp.sum(dhv * xv * r, axis=0, keepdims=True)

        @pl.when(i == 0)
        def _():
            dw_ref[...] = part

        @pl.when(i > 0)
        def _():
            dw_ref[...] += part

    blk = pl.BlockSpec((tb, D_MODEL), lambda i: (i, 0))
    row = pl.BlockSpec((1, D_MODEL), lambda i: (0, 0))
    return pl.pallas_call(
        body,
        out_shape=(jax.ShapeDtypeStruct((T, D_MODEL), f32), jax.ShapeDtypeStruct((1, D_MODEL), f32)),
        grid=(T // tb,),
        in_specs=[blk, row, blk, blk],
        out_specs=(blk, row),
        compiler_params=_cparams(("arbitrary",)),
        name="rms_pre_bwd",
    )(x, w, dh, dy)


def _conv5(u, cw):
    T = u.shape[0]
    t = _iota2(u.shape, 0)
    acc = u * cw[2:3, :]
    for j in (0, 1, 3, 4):
        d = j - 2
        sh = pltpu.roll(u, (-d) % T, 0)
        valid = jnp.logical_and(t + d >= 0, t + d < T)
        acc = acc + jnp.where(valid, sh, 0.0) * cw[j : j + 1, :]
    return acc


def _gdn_prep(proj, conv_w):
    T = proj.shape[0]
    nh = GDN_HEADS

    def body(u_ref, cw_ref, o_ref):
        j = pl.program_id(0)
        s = _silu(_conv5(u_ref[...], cw_ref[...]))

        @pl.when(j < 2 * nh)
        def _():
            o_ref[0] = s * lax.rsqrt(jnp.sum(s * s, axis=1, keepdims=True) + NORM_EPS)

        @pl.when(j >= 2 * nh)
        def _():
            o_ref[0] = s

    return pl.pallas_call(
        body,
        out_shape=jax.ShapeDtypeStruct((3, T, GDN_HEADS * GDN_DIM), f32),
        grid=(3 * nh,),
        in_specs=[pl.BlockSpec((T, GDN_DIM), lambda j: (0, j)), pl.BlockSpec((CONV_K, GDN_DIM), lambda j: (0, j))],
        out_specs=pl.BlockSpec((1, T, GDN_DIM), lambda j: (j // nh, 0, j % nh)),
        compiler_params=_cparams(("parallel",)),
        name="gdn_prep",
    )(proj, conv_w)


def _gdn_prep_bwd(proj, conv_w, dqkv_f, dqkv_b):
    T = proj.shape[0]
    nh = GDN_HEADS

    def body(u_ref, cw_ref, df_ref, db_ref, du_ref, dcw_ref):
        j = pl.program_id(0)
        u, cw = u_ref[...], cw_ref[...]
        c = _conv5(u, cw)
        s = _silu(c)
        dn = df_ref[0] + db_ref[0]
        rinv = lax.rsqrt(jnp.sum(s * s, axis=1, keepdims=True) + NORM_EPS)
        ds_norm = rinv * dn - s * (rinv * rinv * rinv) * jnp.sum(dn * s, axis=1, keepdims=True)
        ds = jnp.where(j < 2 * nh, ds_norm, dn)
        dc = ds * _silu_grad(c)
        t = _iota2(u.shape, 0)
        du = dc * cw[2:3, :]
        rows = []
        for jj in range(CONV_K):
            d = jj - 2
            if d == 0:
                rows.append(jnp.sum(dc * u, axis=0, keepdims=True))
                continue
            ush = pltpu.roll(u, (-d) % T, 0)
            uvalid = jnp.logical_and(t + d >= 0, t + d < T)
            rows.append(jnp.sum(dc * jnp.where(uvalid, ush, 0.0), axis=0, keepdims=True))
            dsh = pltpu.roll(dc, d % T, 0)
            dvalid = jnp.logical_and(t - d >= 0, t - d < T)
            du = du + jnp.where(dvalid, dsh, 0.0) * cw[jj : jj + 1, :]
        du_ref[...] = du.astype(du_ref.dtype)
        for jj in range(CONV_K):
            dcw_ref[jj : jj + 1, :] = rows[jj]

    qspec = pl.BlockSpec((1, T, GDN_DIM), lambda j: (j // nh, 0, j % nh))
    return pl.pallas_call(
        body,
        out_shape=(jax.ShapeDtypeStruct((T, 3 * nh * GDN_DIM), MXU_DTYPE), jax.ShapeDtypeStruct((CONV_K, 3 * nh * GDN_DIM), f32)),
        grid=(3 * nh,),
        in_specs=[pl.BlockSpec((T, GDN_DIM), lambda j: (0, j)), pl.BlockSpec((CONV_K, GDN_DIM), lambda j: (0, j)), qspec, qspec],
        out_specs=(pl.BlockSpec((T, GDN_DIM), lambda j: (0, j)), pl.BlockSpec((CONV_K, GDN_DIM), lambda j: (0, j))),
        compiler_params=_cparams(("parallel",)),
        name="gdn_prep_bwd",
    )(proj, conv_w, dqkv_f, dqkv_b)


def _chunk_tri(n, rev):
    i, j = _iota2((n, n), 0), _iota2((n, n), 1)
    same = jnp.right_shift(i, 6) == jnp.right_shift(j, 6)
    order = (j >= i) if rev else (j <= i)
    return jnp.where(jnp.logical_and(same, order), 1.0, 0.0).astype(f32)


def _gdn_gates(proj, avec, dvec, tb=256):
    T = proj.shape[0]

    def body(s_ref, a_ref, d_ref, g_ref, gc_ref):
        small = s_ref[...]
        lane = _iota2(small.shape, 1)
        lg = -jnp.exp(a_ref[...]) * _softplus(small + d_ref[...])
        beta = _sigmoid(small)
        g_ref[...] = jnp.where(lane < 16, lg, jnp.where(lane < 32, beta, 0.0))
        lgm = jnp.where(lane < 16, lg, 0.0)
        gcf = _exact01(_chunk_tri(tb, False), lgm)
        gcr = _exact01(_chunk_tri(tb, True), lgm)
        gc_ref[...] = jnp.where(lane < 8, gcf, gcr)

    blk = pl.BlockSpec((tb, LANES), lambda i: (i, 0))
    row = pl.BlockSpec((1, LANES), lambda i: (0, 0))
    return pl.pallas_call(
        body,
        out_shape=(jax.ShapeDtypeStruct((T, LANES), f32), jax.ShapeDtypeStruct((T, LANES), f32)),
        grid=(T // tb,),
        in_specs=[pl.BlockSpec((tb, LANES), lambda i: (i, SMALL_OFF // LANES)), row, row],
        out_specs=(blk, blk),
        compiler_params=_cparams(("parallel",)),
        name="gdn_gates",
    )(proj, avec, dvec)


def _gdn_gates_bwd(proj, avec, dvec, dg_f, dg_b, dsmall_gla, tb=256):
    T = proj.shape[0]

    def body(s_ref, a_ref, d_ref, gf_ref, gb_ref, dl_ref, ds_ref, da_ref, dd_ref):
        i = pl.program_id(0)
        small = s_ref[...]
        lane = _iota2(small.shape, 1)
        dgate = gf_ref[...] + gb_ref[...]
        z = small + d_ref[...]
        nega = -jnp.exp(a_ref[...])
        dz = dgate * nega * _sigmoid(z)
        beta = _sigmoid(small)
        dsm = jnp.where(lane < 16, dz, jnp.where(lane < 32, dgate * beta * (1.0 - beta), 0.0))
        ds_ref[...] = (dsm + dl_ref[...]).astype(ds_ref.dtype)
        lg = nega * _softplus(z)
        pa = jnp.sum(jnp.where(lane < 16, dgate * lg, 0.0), axis=0, keepdims=True)
        pd = jnp.sum(jnp.where(lane < 16, dz, 0.0), axis=0, keepdims=True)

        @pl.when(i == 0)
        def _():
            da_ref[...] = pa
            dd_ref[...] = pd

        @pl.when(i > 0)
        def _():
            da_ref[...] += pa
            dd_ref[...] += pd

    blk = pl.BlockSpec((tb, LANES), lambda i: (i, 0))
    row = pl.BlockSpec((1, LANES), lambda i: (0, 0))
    return pl.pallas_call(
        body,
        out_shape=(jax.ShapeDtypeStruct((T, LANES), MXU_DTYPE), jax.ShapeDtypeStruct((1, LANES), f32), jax.ShapeDtypeStruct((1, LANES), f32)),
        grid=(T // tb,),
        in_specs=[pl.BlockSpec((tb, LANES), lambda i: (i, SMALL_OFF // LANES)), row, row, blk, blk, blk],
        out_specs=(blk, row, row),
        compiler_params=_cparams(("arbitrary",)),
        name="gdn_gates_bwd",
    )(proj, avec, dvec, dg_f, dg_b, dsmall_gla)


GDN_HB = 4


def _inv_unit_lower(L, eye):
    A = eye - L
    P2 = _split2(L)
    for _ in range(5):
        P2 = _split2(_h3s(P2, P2, "nn"))
        A = A + _h3s(_split2(A), P2, "nn")
    return A


def _lane_col(blk, idx):
    lane = _iota2(blk.shape, 1)
    return jnp.sum(jnp.where(lane == idx, blk, 0.0), axis=1, keepdims=True)


class _GdnChunk:
    def __init__(self, q, k, v, gcol, beta, rev, A=None):
        Cn = q.shape[0]
        ii, jj = _iota2((Cn, Cn), 0), _iota2((Cn, Cn), 1)
        self.incl = (ii <= jj) if rev else (ii >= jj)
        self.strict = (ii < jj) if rev else (ii > jj)
        grow = jnp.broadcast_to(gcol, (Cn, LANES)).T[0:1, :]
        diff = gcol - grow
        self.decay = jnp.where(self.incl, jnp.exp(jnp.where(self.incl, diff, 0.0)), 0.0)
        self.k, self.v, self.beta, self.gcol = k, v, beta, gcol
        self.qs = q * (GDN_DIM**-0.5)
        self.kb = k * beta
        self.L = jnp.where(self.strict, _mx(self.kb, k, "nt") * self.decay, 0.0)
        if A is None:
            A = _inv_unit_lower(self.L, jnp.where(ii == jj, 1.0, 0.0).astype(f32))
        self.A2 = _split2(A)
        self.A = A
        self.eg = jnp.exp(gcol)
        self.kbg = self.kb * self.eg
        self.sol = _h3s(self.A2, _split2(jnp.concatenate([v * beta, self.kbg], axis=1)), "nn")
        self.u = self.sol[:, :GDN_DIM]
        self.w = self.sol[:, GDN_DIM:]
        self.attn = jnp.where(self.incl, _mx(self.qs, k, "nt") * self.decay, 0.0)
        self.qd = self.qs * self.eg
        last = 0 if rev else Cn - 1
        self.last = last
        gl = gcol[last : last + 1, :]
        self.ekd = jnp.exp(gl - gcol)
        self.kd = k * self.ekd
        self.egl = jnp.exp(gl)


def _gdn_scan_fwd(qkvn, gates, gc):
    T = qkvn.shape[1]
    N = T // CHUNK
    nh, hb = GDN_HEADS, GDN_HB
    W = hb * GDN_DIM
    chains = [(d, hh) for d in range(2) for hh in range(hb)]

    def body(qf_ref, qb_ref, gf_ref, gb_ref, cf_ref, cb_ref, of_ref, ob_ref, sf_ref, sb_ref, af_ref, ab_ref, s_scr):
        n, b = pl.program_id(0), pl.program_id(1)

        @pl.when(n == 0)
        def _():
            for d, hh in chains:
                s_scr[d, b * hb + hh] = jnp.zeros((GDN_DIM, GDN_DIM), f32)

        dirs = ((qf_ref, gf_ref, cf_ref, of_ref, sf_ref, af_ref), (qb_ref, gb_ref, cb_ref, ob_ref, sb_ref, ab_ref))
        s_in = {ch: s_scr[ch[0], b * hb + ch[1]] for ch in chains}
        s_out = {}
        for d, hh in chains:
            q_ref, g_ref, c_ref, o_ref, st_ref, a_ref = dirs[d]
            h = b * hb + hh
            sl = slice(hh * GDN_DIM, (hh + 1) * GDN_DIM)
            gcol = _lane_col(c_ref[...], h + 8 * d)
            beta = _lane_col(g_ref[...], 16 + h + 8 * d)
            ck = _GdnChunk(q_ref[0, :, sl], q_ref[1, :, sl], q_ref[2, :, sl], gcol, beta, d == 1)
            S = s_in[d, hh]
            st_ref[0, hh] = S
            a_ref[0, hh] = ck.A
            v_new = ck.u - _mx(ck.w, S, "nn")
            o_ref[:, sl] = _mx(ck.qd, S, "nn") + _mx(ck.attn, v_new, "nn")
            s_out[d, hh] = S * ck.egl + _mx(ck.kd, v_new, "tn")
        for d, hh in chains:
            s_scr[d, b * hb + hh] = s_out[d, hh]

    fw, bw = (lambda n: n), (lambda n: N - 1 - n)
    qsp = lambda ci: pl.BlockSpec((3, CHUNK, W), lambda n, b: (0, ci(n), b))
    gsp = lambda ci: pl.BlockSpec((CHUNK, LANES), lambda n, b: (ci(n), 0))
    osp = lambda ci: pl.BlockSpec((CHUNK, W), lambda n, b: (ci(n), b))
    ssp = lambda ci: pl.BlockSpec((1, hb, GDN_DIM, GDN_DIM), lambda n, b: (ci(n), b, 0, 0))
    asp = lambda ci: pl.BlockSpec((1, hb, CHUNK, CHUNK), lambda n, b: (ci(n), b, 0, 0))
    o_shape = jax.ShapeDtypeStruct((T, nh * GDN_DIM), f32)
    s_shape = jax.ShapeDtypeStruct((N, nh, GDN_DIM, GDN_DIM), f32)
    a_shape = jax.ShapeDtypeStruct((N, nh, CHUNK, CHUNK), f32)
    return pl.pallas_call(
        body,
        out_shape=(o_shape, o_shape, s_shape, s_shape, a_shape, a_shape),
        grid=(N, nh // hb),
        in_specs=[qsp(fw), qsp(bw), gsp(fw), gsp(bw), gsp(fw), gsp(bw)],
        out_specs=(osp(fw), osp(bw), ssp(fw), ssp(bw), asp(fw), asp(bw)),
        scratch_shapes=[pltpu.VMEM((2, nh, GDN_DIM, GDN_DIM), f32)],
        compiler_params=_cparams(("arbitrary", "arbitrary")),
        name="gdn_scan_fwd",
    )(qkvn, qkvn, gates, gates, gc, gc)


def _gdn_scan_bwd(qkvn, gates, gc, st_f, st_b, a_f, a_b, do):
    T = qkvn.shape[1]
    N = T // CHUNK
    nh, hb = GDN_HEADS, GDN_HB
    nb = nh // hb
    W = hb * GDN_DIM
    chains = [(d, hh) for d in range(2) for hh in range(hb)]

    def body(qf_ref, qb_ref, gf_ref, gb_ref, cf_ref, cb_ref, sf_ref, sb_ref, af_ref, ab_ref, dof_ref, dob_ref,
             dqf_ref, dqb_ref, dgf_ref, dgb_ref, ds_scr):
        n, b = pl.program_id(0), pl.program_id(1)

        @pl.when(n == 0)
        def _():
            for d, hh in chains:
                ds_scr[d, b * hb + hh] = jnp.zeros((GDN_DIM, GDN_DIM), f32)

        @pl.when(b == 0)
        def _():
            dgf_ref[...] = jnp.zeros((CHUNK, LANES), f32)
            dgb_ref[...] = jnp.zeros((CHUNK, LANES), f32)

        dirs = ((qf_ref, gf_ref, cf_ref, sf_ref, af_ref, dof_ref, dqf_ref), (qb_ref, gb_ref, cb_ref, sb_ref, ab_ref, dob_ref, dqb_ref))
        ds_in = {ch: ds_scr[ch[0], b * hb + ch[1]] for ch in chains}
        ds_out = {}
        lane = _iota2((CHUNK, LANES), 1)
        acc = [dgf_ref[...], dgb_ref[...]]
        for d, hh in chains:
            q_ref, g_ref, c_ref, st_ref, a_ref, do_ref, dq_ref = dirs[d]
            h = b * hb + hh
            sl = slice(hh * GDN_DIM, (hh + 1) * GDN_DIM)
            gcol = _lane_col(c_ref[...], h + 8 * d)
            beta = _lane_col(g_ref[...], 16 + h + 8 * d)
            ck = _GdnChunk(q_ref[0, :, sl], q_ref[1, :, sl], q_ref[2, :, sl], gcol, beta, d == 1, A=a_ref[0, hh])
            k, v = ck.k, ck.v
            S = st_ref[0, hh]
            dS = ds_in[d, hh]
            dov = do_ref[:, sl]
            v_new = ck.u - _mx(ck.w, S, "nn")
            d_vnew = _mx(ck.attn, dov, "tn") + _mx(ck.kd, dS, "nn")
            d_attn = jnp.where(ck.incl, _mx(dov, v_new, "nt"), 0.0)
            d_qd = _mx(dov, S, "nt")
            d_kd = _mx(v_new, dS, "nt")
            d_gl = jnp.sum(jnp.sum(dS * S, axis=1, keepdims=True), axis=0, keepdims=True) * ck.egl
            ds_out[d, hh] = dS * ck.egl + _mx(ck.qd, dov, "tn") - _mx(ck.w, d_vnew, "tn")
            d_w = -_mx(d_vnew, S, "nt")
            d_rhs = _h3s(ck.A2, _split2(jnp.concatenate([d_vnew, d_w], axis=1)), "tn")
            d_vb, d_kbg = d_rhs[:, :GDN_DIM], d_rhs[:, GDN_DIM:]
            dL = -jnp.where(ck.strict, _h3(d_rhs, ck.sol, "nt"), 0.0)
            P = dL * ck.decay
            Q = d_attn * ck.decay
            d_kb = _mx(P, k, "nn") + d_kbg * ck.eg
            d_qs = _mx(Q, k, "nn") + d_qd * ck.eg
            d_k = _mx(P, ck.kb, "tn") + _mx(Q, ck.qs, "tn") + d_kd * ck.ekd + d_kb * ck.beta
            E = dL * ck.L + d_attn * ck.attn
            kdsum = jnp.sum(d_kd * ck.kd, axis=1, keepdims=True)
            d_gc = (jnp.sum(E, axis=1, keepdims=True) - _colsum_col(E)
                    + jnp.sum(d_kbg * ck.kbg, axis=1, keepdims=True)
                    + jnp.sum(d_qd * ck.qd, axis=1, keepdims=True) - kdsum)
            d_gl = d_gl + jnp.sum(kdsum, axis=0, keepdims=True)
            row = _iota2((CHUNK, 1), 0)
            d_gc = d_gc + jnp.where(row == ck.last, d_gl, 0.0)
            d_beta = jnp.sum(d_vb * v, axis=1, keepdims=True) + jnp.sum(d_kb * k, axis=1, keepdims=True)
            dq_ref[0, :, sl] = d_qs * (GDN_DIM**-0.5)
            dq_ref[1, :, sl] = d_k
            dq_ref[2, :, sl] = d_vb * ck.beta
            acc[d] = jnp.where(lane == h + 8 * d, d_gc, acc[d])
            acc[d] = jnp.where(lane == 16 + h + 8 * d, d_beta, acc[d])
        for d, hh in chains:
            ds_scr[d, b * hb + hh] = ds_out[d, hh]
        dgf_ref[...] = acc[0]
        dgb_ref[...] = acc[1]

        @pl.when(b == nb - 1)
        def _():
            for d, dg_ref in enumerate((dgf_ref, dgb_ref)):
                blk = dg_ref[...]
                dg_ref[...] = jnp.where(lane < 16, _exact01(_chunk_tri(CHUNK, d == 0), blk), blk)

    fw, bw = (lambda n: N - 1 - n), (lambda n: n)
    qsp = lambda ci: pl.BlockSpec((3, CHUNK, W), lambda n, b: (0, ci(n), b))
    gsp = lambda ci: pl.BlockSpec((CHUNK, LANES), lambda n, b: (ci(n), 0))
    osp = lambda ci: pl.BlockSpec((CHUNK, W), lambda n, b: (ci(n), b))
    ssp = lambda ci: pl.BlockSpec((1, hb, GDN_DIM, GDN_DIM), lambda n, b: (ci(n), b, 0, 0))
    asp = lambda ci: pl.BlockSpec((1, hb, CHUNK, CHUNK), lambda n, b: (ci(n), b, 0, 0))
    dq_shape = jax.ShapeDtypeStruct((3, T, nh * GDN_DIM), f32)
    dg_shape = jax.ShapeDtypeStruct((T, LANES), f32)
    return pl.pallas_call(
        body,
        out_shape=(dq_shape, dq_shape, dg_shape, dg_shape),
        grid=(N, nb),
        in_specs=[qsp(fw), qsp(bw), gsp(fw), gsp(bw), gsp(fw), gsp(bw), ssp(fw), ssp(bw), asp(fw), asp(bw), osp(fw), osp(bw)],
        out_specs=(qsp(fw), qsp(bw), gsp(fw), gsp(bw)),
        scratch_shapes=[pltpu.VMEM((2, nh, GDN_DIM, GDN_DIM), f32)],
        compiler_params=_cparams(("arbitrary", "arbitrary")),
        name="gdn_scan_bwd",
    )(qkvn, qkvn, gates, gates, gc, gc, st_f, st_b, a_f, a_b, do, do)


def _out_norm(o_f, o_b, proj, gate_blk, w, hd, name, tb=256):
    T, W = o_f.shape
    nh = W // hd

    def body(of_ref, ob_ref, z_ref, w_ref, y_ref):
        wv = w_ref[...]
        for i in range(nh):
            sl = slice(i * hd, (i + 1) * hd)
            o = of_ref[:, sl] + ob_ref[:, sl]
            r = lax.rsqrt(jnp.mean(o * o, axis=1, keepdims=True) + NORM_EPS)
            y_ref[:, sl] = (o * r * wv * _silu(z_ref[:, sl])).astype(y_ref.dtype)

    blk = pl.BlockSpec((tb, W), lambda i: (i, 0))
    return pl.pallas_call(
        body,
        out_shape=jax.ShapeDtypeStruct((T, W), MXU_DTYPE),
        grid=(T // tb,),
        in_specs=[blk, blk, pl.BlockSpec((tb, W), lambda i: (i, gate_blk)), pl.BlockSpec((1, hd), lambda i: (0, 0))],
        out_specs=blk,
        compiler_params=_cparams(("parallel",)),
        name=name,
    )(o_f, o_b, proj, w)


def _out_norm_bwd(dy, o_f, o_b, proj, gate_blk, w, hd, name, tb=256):
    T, W = o_f.shape
    nh = W // hd

    def body(dy_ref, of_ref, ob_ref, z_ref, w_ref, do_ref, dz_ref, dw_ref):
        i = pl.program_id(0)
        wv = w_ref[...]
        dw = jnp.zeros((1, hd), f32)
        for a in range(nh):
            sl = slice(a * hd, (a + 1) * hd)
            o = of_ref[:, sl] + ob_ref[:, sl]
            z = z_ref[:, sl]
            dyv = dy_ref[:, sl]
            r = lax.rsqrt(jnp.mean(o * o, axis=1, keepdims=True) + NORM_EPS)
            orr = o * r
            dn = dyv * _silu(z)
            dz_ref[:, sl] = (dyv * orr * wv * _silu_grad(z)).astype(dz_ref.dtype)
            dw = dw + jnp.sum(dn * orr, axis=0, keepdims=True)
            dnw = dn * wv
            do_ref[:, sl] = r * dnw - o * (r * r * r) * jnp.mean(dnw * o, axis=1, keepdims=True)

        @pl.when(i == 0)
        def _():
            dw_ref[...] = dw

        @pl.when(i > 0)
        def _():
            dw_ref[...] += dw

    blk = pl.BlockSpec((tb, W), lambda i: (i, 0))
    row = pl.BlockSpec((1, hd), lambda i: (0, 0))
    return pl.pallas_call(
        body,
        out_shape=(jax.ShapeDtypeStruct((T, W), f32), jax.ShapeDtypeStruct((T, W), MXU_DTYPE), jax.ShapeDtypeStruct((1, hd), f32)),
        grid=(T // tb,),
        in_specs=[blk, blk, blk, pl.BlockSpec((tb, W), lambda i: (i, gate_blk)), row],
        out_specs=(blk, blk, row),
        compiler_params=_cparams(("arbitrary",)),
        name=name,
    )(dy, o_f, o_b, proj, w)


def _gla_prep(proj, w2f, w2b, b2f, b2b, tb=256):
    T = proj.shape[0]
    W = GLA_HEADS * GLA_DK

    def body(s_ref, wf_ref, wb_ref, bf_ref, bb_ref, gf_ref, gb_ref):
        small = s_ref[...]
        gkf = -_softplus(-(_mx(small, wf_ref[...], "nn") + bf_ref[...])) * (1.0 / GATE_NORMALIZER)
        gkb = -_softplus(-(_mx(small, wb_ref[...], "nn") + bb_ref[...])) * (1.0 / GATE_NORMALIZER)
        gf_ref[...] = _exact01(_chunk_tri(tb, False), gkf)
        gb_ref[...] = _exact01(_chunk_tri(tb, True), gkb)

    blk = pl.BlockSpec((tb, W), lambda i: (i, 0))
    wsp = pl.BlockSpec((LANES, W), lambda i: (0, 0))
    row = pl.BlockSpec((1, W), lambda i: (0, 0))
    return pl.pallas_call(
        body,
        out_shape=(jax.ShapeDtypeStruct((T, W), f32), jax.ShapeDtypeStruct((T, W), f32)),
        grid=(T // tb,),
        in_specs=[pl.BlockSpec((tb, LANES), lambda i: (i, SMALL_OFF // LANES)), wsp, wsp, row, row],
        out_specs=(blk, blk),
        compiler_params=_cparams(("parallel",)),
        name="gla_prep",
    )(proj, w2f, w2b, b2f, b2b)


def _gla_prep_bwd(proj, w2f, w2b, b2f, b2b, dgk_f, dgk_b, tb=256):
    T = proj.shape[0]
    W = GLA_HEADS * GLA_DK

    def body(s_ref, wf_ref, wb_ref, bf_ref, bb_ref, df_ref, db_ref, ds_ref, dwf_ref, dwb_ref, dbf_ref, dbb_ref):
        i = pl.program_id(0)
        small = s_ref[...]
        dsm = jnp.zeros((tb, LANES), f32)
        parts = []
        for w_ref, b_ref, d_ref in ((wf_ref, bf_ref, df_ref), (wb_ref, bb_ref, db_ref)):
            pre = _mx(small, w_ref[...], "nn") + b_ref[...]
            dpre = d_ref[...] * (1.0 / GATE_NORMALIZER) * _sigmoid(-pre)
            dsm = dsm + _mx(dpre, w_ref[...], "nt")
            parts.append((_mx(small, dpre, "tn"), jnp.sum(dpre, axis=0, keepdims=True)))
        ds_ref[...] = dsm

        @pl.when(i == 0)
        def _():
            dwf_ref[...], dbf_ref[...] = parts[0]
            dwb_ref[...], dbb_ref[...] = parts[1]

        @pl.when(i > 0)
        def _():
            dwf_ref[...] += parts[0][0]
            dbf_ref[...] += parts[0][1]
            dwb_ref[...] += parts[1][0]
            dbb_ref[...] += parts[1][1]

    blk = pl.BlockSpec((tb, W), lambda i: (i, 0))
    wsp = pl.BlockSpec((LANES, W), lambda i: (0, 0))
    row = pl.BlockSpec((1, W), lambda i: (0, 0))
    return pl.pallas_call(
        body,
        out_shape=(jax.ShapeDtypeStruct((T, LANES), f32), jax.ShapeDtypeStruct((LANES, W), f32), jax.ShapeDtypeStruct((LANES, W), f32),
                   jax.ShapeDtypeStruct((1, W), f32), jax.ShapeDtypeStruct((1, W), f32)),
        grid=(T // tb,),
        in_specs=[pl.BlockSpec((tb, LANES), lambda i: (i, SMALL_OFF // LANES)), wsp, wsp, row, row, blk, blk],
        out_specs=(pl.BlockSpec((tb, LANES), lambda i: (i, 0)), wsp, wsp, row, row),
        compiler_params=_cparams(("arbitrary",)),
        name="gla_prep_bwd",
    )(proj, w2f, w2b, b2f, b2b, dgk_f, dgk_b)


Q_B_BLK, K_B_BLK, V_B_BLK = 4096 // GLA_DK, 4608 // GLA_DK, 5120 // GLA_DV


class _GlaChunk:
    def __init__(self, q, k, G, rev):
        Cn = q.shape[0]
        ii, jj = _iota2((Cn, Cn), 0), _iota2((Cn, Cn), 1)
        self.incl = (ii <= jj) if rev else (ii >= jj)
        self.eG = jnp.exp(G)
        self.enG = jnp.exp(-G)
        self.qg = q * (GLA_DK**-0.5) * self.eG
        self.kg = k * self.enG
        self.attn = jnp.where(self.incl, _mx(self.qg, self.kg, "nt"), 0.0)
        last = 0 if rev else Cn - 1
        self.last = last
        gl = G[last : last + 1, :]
        self.ekd = jnp.exp(gl - G)
        self.kd = k * self.ekd
        self.egl = jnp.exp(gl)


def _gla_scan_fwd(proj, G_f, G_b):
    T = proj.shape[0]
    N = T // CHUNK
    nh = GLA_HEADS

    def body(qf_ref, kf_ref, vf_ref, gf_ref, qb_ref, kb_ref, vb_ref, gb_ref, of_ref, ob_ref, sf_ref, sb_ref, s_scr):
        n, h = pl.program_id(0), pl.program_id(1)

        @pl.when(n == 0)
        def _():
            s_scr[0, h] = jnp.zeros((GLA_DV, GLA_DK), f32)
            s_scr[1, h] = jnp.zeros((GLA_DV, GLA_DK), f32)

        for d, (q_ref, k_ref, v_ref, g_ref, o_ref, st_ref) in enumerate(
            ((qf_ref, kf_ref, vf_ref, gf_ref, of_ref, sf_ref), (qb_ref, kb_ref, vb_ref, gb_ref, ob_ref, sb_ref))
        ):
            ck = _GlaChunk(q_ref[...], k_ref[...], g_ref[...], d == 1)
            v = v_ref[...]
            St = s_scr[d, h]
            st_ref[0, 0] = St
            o_ref[...] = _mx(ck.qg, St, "nt") + _mx(ck.attn, v, "nn")
            s_scr[d, h] = St * ck.egl + _mx(v, ck.kd, "tn")

    def specs(cidx):
        return [
            pl.BlockSpec((CHUNK, GLA_DK), lambda n, h: (cidx(n), Q_B_BLK + h)),
            pl.BlockSpec((CHUNK, GLA_DK), lambda n, h: (cidx(n), K_B_BLK + h)),
            pl.BlockSpec((CHUNK, GLA_DV), lambda n, h: (cidx(n), V_B_BLK + h)),
            pl.BlockSpec((CHUNK, GLA_DK), lambda n, h: (cidx(n), h)),
        ]

    fw, bw = (lambda n: n), (lambda n: N - 1 - n)
    o_shape = jax.ShapeDtypeStruct((T, nh * GLA_DV), f32)
    s_shape = jax.ShapeDtypeStruct((N, nh, GLA_DV, GLA_DK), f32)
    return pl.pallas_call(
        body,
        out_shape=(o_shape, o_shape, s_shape, s_shape),
        grid=(N, nh),
        in_specs=specs(fw) + specs(bw),
        out_specs=(
            pl.BlockSpec((CHUNK, GLA_DV), lambda n, h: (n, h)),
            pl.BlockSpec((CHUNK, GLA_DV), lambda n, h: (N - 1 - n, h)),
            pl.BlockSpec((1, 1, GLA_DV, GLA_DK), lambda n, h: (n, h, 0, 0)),
            pl.BlockSpec((1, 1, GLA_DV, GLA_DK), lambda n, h: (N - 1 - n, h, 0, 0)),
        ),
        scratch_shapes=[pltpu.VMEM((2, nh, GLA_DV, GLA_DK), f32)],
        compiler_params=_cparams(("arbitrary", "arbitrary")),
        name="gla_scan_fwd",
    )(proj, proj, proj, G_f, proj, proj, proj, G_b)


def _gla_scan_bwd(proj, G_f, G_b, st_f, st_b, do):
    T = proj.shape[0]
    N = T // CHUNK
    nh = GLA_HEADS

    def body(qf_ref, kf_ref, vf_ref, gf_ref, sf_ref, dof_ref, qb_ref, kb_ref, vb_ref, gb_ref, sb_ref, dob_ref,
             dqf_ref, dkf_ref, dvf_ref, dgf_ref, dqb_ref, dkb_ref, dvb_ref, dgb_ref, ds_scr):
        n, h = pl.program_id(0), pl.program_id(1)

        @pl.when(n == 0)
        def _():
            ds_scr[0, h] = jnp.zeros((GLA_DV, GLA_DK), f32)
            ds_scr[1, h] = jnp.zeros((GLA_DV, GLA_DK), f32)

        for d, (q_ref, k_ref, v_ref, g_ref, st_ref, do_ref, dq_ref, dk_ref, dv_ref, dg_ref) in enumerate(
            ((qf_ref, kf_ref, vf_ref, gf_ref, sf_ref, dof_ref, dqf_ref, dkf_ref, dvf_ref, dgf_ref),
             (qb_ref, kb_ref, vb_ref, gb_ref, sb_ref, dob_ref, dqb_ref, dkb_ref, dvb_ref, dgb_ref))
        ):
            rev = d == 1
            ck = _GlaChunk(q_ref[...], k_ref[...], g_ref[...], rev)
            v = v_ref[...]
            St = st_ref[0, 0]
            dSt = ds_scr[d, h]
            dov = do_ref[...]
            d_attn = jnp.where(ck.incl, _mx(dov, v, "nt"), 0.0)
            d_qg = _mx(dov, St, "nn") + _mx(d_attn, ck.kg, "nn")
            d_kg = _mx(d_attn, ck.qg, "tn")
            dv_ref[...] = _mx(ck.attn, dov, "tn") + _mx(ck.kd, dSt, "nt")
            d_kd = _mx(v, dSt, "nn")
            ds_scr[d, h] = dSt * ck.egl + _mx(dov, ck.qg, "tn")
            kdd = d_kd * ck.kd
            d_gl = jnp.sum(dSt * St, axis=0, keepdims=True) * ck.egl + jnp.sum(kdd, axis=0, keepdims=True)
            dq_ref[...] = d_qg * ck.eG * (GLA_DK**-0.5)
            dk_ref[...] = d_kg * ck.enG + d_kd * ck.ekd
            d_G = d_qg * ck.qg - d_kg * ck.kg - kdd
            row = _iota2((CHUNK, GLA_DK), 0)
            d_G = d_G + jnp.where(row == ck.last, d_gl, 0.0)
            dg_ref[...] = _exact01(_chunk_tri(CHUNK, not rev), d_G)

    def in_specs(cidx):
        return [
            pl.BlockSpec((CHUNK, GLA_DK), lambda n, h: (cidx(n), Q_B_BLK + h)),
            pl.BlockSpec((CHUNK, GLA_DK), lambda n, h: (cidx(n), K_B_BLK + h)),
            pl.BlockSpec((CHUNK, GLA_DV), lambda n, h: (cidx(n), V_B_BLK + h)),
            pl.BlockSpec((CHUNK, GLA_DK), lambda n, h: (cidx(n), h)),
            pl.BlockSpec((1, 1, GLA_DV, GLA_DK), lambda n, h: (cidx(n), h, 0, 0)),
            pl.BlockSpec((CHUNK, GLA_DV), lambda n, h: (cidx(n), h)),
        ]

    def out_specs(cidx):
        return [
            pl.BlockSpec((CHUNK, GLA_DK), lambda n, h: (cidx(n), h)),
            pl.BlockSpec((CHUNK, GLA_DK), lambda n, h: (cidx(n), h)),
            pl.BlockSpec((CHUNK, GLA_DV), lambda n, h: (cidx(n), h)),
            pl.BlockSpec((CHUNK, GLA_DK), lambda n, h: (cidx(n), h)),
        ]

    fw, bw = (lambda n: N - 1 - n), (lambda n: n)
    k_shape = jax.ShapeDtypeStruct((T, nh * GLA_DK), f32)
    v_shape = jax.ShapeDtypeStruct((T, nh * GLA_DV), f32)
    return pl.pallas_call(
        body,
        out_shape=(k_shape, k_shape, v_shape, k_shape) * 2,
        grid=(N, nh),
        in_specs=in_specs(fw) + in_specs(bw),
        out_specs=tuple(out_specs(fw) + out_specs(bw)),
        scratch_shapes=[pltpu.VMEM((2, nh, GLA_DV, GLA_DK), f32)],
        compiler_params=_cparams(("arbitrary", "arbitrary")),
        name="gla_scan_bwd",
    )(proj, proj, proj, G_f, st_f, do, proj, proj, proj, G_b, st_b, do)


GATE_A_BLK, GATE_B_BLK, Z_BLK, G_B_BLK = 7168 // 1024, 8192 // 1024, 3072 // 1024, 6144 // 1024


def _mid(og, obg, proj, x, target, wpg, wpl, wout, w_post, tb=256):
    T = x.shape[0]
    Dm = D_MODEL

    def body(og_ref, obg_ref, ga_ref, gb_ref, x_ref, t_ref, wpg_ref, wpl_ref, wo_ref, wp_ref,
             loss_ref, dwp_ref, dy_ref, m_ref, dout_ref, dya_ref, dyb_ref, dga_ref, dgb_ref, dog_ref, dobg_ref):
        i = pl.program_id(0)
        ya = _mx(og_ref[...], wpg_ref[...], "nn")
        yb = _mx(obg_ref[...], wpl_ref[...], "nn")
        sa, sb = _sigmoid(ga_ref[...]), _sigmoid(gb_ref[...])
        m = sa * ya + sb * yb
        m_ref[...] = m.astype(m_ref.dtype)
        out = _mx(m, wo_ref[...], "nn")
        r = lax.rsqrt(jnp.mean(out * out, axis=1, keepdims=True) + NORM_EPS)
        wp = wp_ref[...]
        e = x_ref[...] + out * r * wp - t_ref[...]
        dy = e * (1.0 / Dm)
        dy_ref[...] = dy
        lpart = jnp.sum(e * e, axis=0, keepdims=True)
        wpart = jnp.sum(dy * out * r, axis=0, keepdims=True)
        dyn = dy * wp
        dout = r * dyn - out * (r * r * r) * jnp.mean(dyn * out, axis=1, keepdims=True)
        dout_ref[...] = dout.astype(dout_ref.dtype)
        dm = _mx(dout, wo_ref[...], "nt")
        dya, dyb = dm * sa, dm * sb
        dya_ref[...] = dya.astype(dya_ref.dtype)
        dyb_ref[...] = dyb.astype(dyb_ref.dtype)
        dga_ref[...] = (dm * ya * sa * (1.0 - sa)).astype(dga_ref.dtype)
        dgb_ref[...] = (dm * yb * sb * (1.0 - sb)).astype(dgb_ref.dtype)
        dog_ref[...] = _mx(dya, wpg_ref[...], "nt")
        dobg_ref[...] = _mx(dyb, wpl_ref[...], "nt")

        @pl.when(i == 0)
        def _():
            loss_ref[...] = lpart
            dwp_ref[...] = wpart

        @pl.when(i > 0)
        def _():
            loss_ref[...] += lpart
            dwp_ref[...] += wpart

    blk = pl.BlockSpec((tb, Dm), lambda i: (i, 0))
    row = pl.BlockSpec((1, Dm), lambda i: (0, 0))
    wsp = pl.BlockSpec((Dm, Dm), lambda i: (0, 0))
    act = jax.ShapeDtypeStruct((T, Dm), MXU_DTYPE)
    big = jax.ShapeDtypeStruct((T, Dm), f32)
    vec = jax.ShapeDtypeStruct((1, Dm), f32)
    return pl.pallas_call(
        body,
        out_shape=(vec, vec, big, act, act, act, act, act, act, big, big),
        grid=(T // tb,),
        in_specs=[blk, blk, pl.BlockSpec((tb, Dm), lambda i: (i, GATE_A_BLK)), pl.BlockSpec((tb, Dm), lambda i: (i, GATE_B_BLK)),
                  blk, blk, wsp, wsp, wsp, row],
        out_specs=(row, row) + (blk,) * 9,
        compiler_params=_cparams(("arbitrary",)),
        name="mid",
    )(og, obg, proj, proj, x, target, wpg, wpl, wout, w_post)


def _heads(ref, lead, n, width):
    return [ref[lead + (slice(None), slice(i * width, (i + 1) * width))] for i in range(n)]


class _GdnBatch:
    def __init__(self, q, k, v, gcol, grow, beta, nf, A=None):
        B, Cn = q.shape[0], q.shape[1]
        self.nf = nf
        self.incl, self.strict, eye = _dir_masks(B, nf, Cn)
        self.decay = jnp.where(self.incl, jnp.exp(jnp.where(self.incl, gcol - grow, 0.0)), 0.0)
        self.k, self.v, self.beta = k, v, beta
        self.qs = q * (GDN_DIM**-0.5)
        self.kb = k * beta
        self.L = jnp.where(self.strict, _mx(self.kb, k, "nt") * self.decay, 0.0)
        if A is None:
            A = _inv_unit_lower(self.L, jnp.where(eye, 1.0, 0.0).astype(f32))
        self.A = A
        self.A2 = _split2(A)
        self.eg = jnp.exp(gcol)
        self.kbg = self.kb * self.eg
        self.sol = _h3s(self.A2, _split2(jnp.concatenate([v * beta, self.kbg], axis=2)), "nn")
        self.u = self.sol[:, :, :GDN_DIM]
        self.w = self.sol[:, :, GDN_DIM:]
        self.attn = jnp.where(self.incl, _mx(self.qs, k, "nt") * self.decay, 0.0)
        self.qd = self.qs * self.eg
        gl = _last_row(gcol, nf)
        self.ekd = jnp.exp(gl - gcol)
        self.kd = k * self.ekd
        self.egl = jnp.exp(gl)


def _gdn_chain_inputs(qf_ref, qb_ref, gf_ref, gb_ref, cf_ref, cb_ref, b, hb):
    qkv = [jnp.stack(_heads(qf_ref, (i,), hb, GDN_DIM) + _heads(qb_ref, (i,), hb, GDN_DIM)) for i in range(3)]
    cols, rows, betas = [], [], []
    for d, (g_ref, c_ref) in enumerate(((gf_ref, cf_ref), (gb_ref, cb_ref))):
        gates, gc = g_ref[...], c_ref[...]
        for hh in range(hb):
            h = b * hb + hh
            col = _lane_col(gc, h + 8 * d)
            cols.append(col)
            rows.append(jnp.broadcast_to(col, (CHUNK, LANES)).T[0:1, :])
            betas.append(_lane_col(gates, 16 + h + 8 * d))
    return qkv[0], qkv[1], qkv[2], jnp.stack(cols), jnp.stack(rows), jnp.stack(betas)


def _gdn_fwd(qkvn, gates, gc):
    T = qkvn.shape[1]
    N = T // CHUNK
    nh, hb = GDN_HEADS, GDN_HB
    W = hb * GDN_DIM

    def body(qf_ref, qb_ref, gf_ref, gb_ref, cf_ref, cb_ref, of_ref, ob_ref, sf_ref, sb_ref, af_ref, ab_ref, s_scr):
        n, b = pl.program_id(0), pl.program_id(1)
        hs = pl.ds(b * hb, hb)

        @pl.when(n == 0)
        def _():
            s_scr[0, hs] = jnp.zeros((hb, GDN_DIM, GDN_DIM), f32)
            s_scr[1, hs] = jnp.zeros((hb, GDN_DIM, GDN_DIM), f32)

        q, k, v, gcol, grow, beta = _gdn_chain_inputs(qf_ref, qb_ref, gf_ref, gb_ref, cf_ref, cb_ref, b, hb)
        ck = _GdnBatch(q, k, v, gcol, grow, beta, hb)
        S = jnp.concatenate([s_scr[0, hs], s_scr[1, hs]], axis=0)
        sf_ref[0], sb_ref[0] = S[:hb], S[hb:]
        af_ref[0], ab_ref[0] = ck.A[:hb], ck.A[hb:]
        v_new = ck.u - _mx(ck.w, S, "nn")
        o = _mx(ck.qd, S, "nn") + _mx(ck.attn, v_new, "nn")
        S_new = S * ck.egl + _mx(ck.kd, v_new, "tn")
        for hh in range(hb):
            sl = slice(hh * GDN_DIM, (hh + 1) * GDN_DIM)
            of_ref[:, sl] = o[hh]
            ob_ref[:, sl] = o[hb + hh]
        s_scr[0, hs] = S_new[:hb]
        s_scr[1, hs] = S_new[hb:]

    fw, bw = (lambda n: n), (lambda n: N - 1 - n)
    qsp = lambda ci: pl.BlockSpec((3, CHUNK, W), lambda n, b: (0, ci(n), b))
    gsp = lambda ci: pl.BlockSpec((CHUNK, LANES), lambda n, b: (ci(n), 0))
    osp = lambda ci: pl.BlockSpec((CHUNK, W), lambda n, b: (ci(n), b))
    ssp = lambda ci: pl.BlockSpec((1, hb, GDN_DIM, GDN_DIM), lambda n, b: (ci(n), b, 0, 0))
    asp = lambda ci: pl.BlockSpec((1, hb, CHUNK, CHUNK), lambda n, b: (ci(n), b, 0, 0))
    o_shape = jax.ShapeDtypeStruct((T, nh * GDN_DIM), f32)
    s_shape = jax.ShapeDtypeStruct((N, nh, GDN_DIM, GDN_DIM), f32)
    a_shape = jax.ShapeDtypeStruct((N, nh, CHUNK, CHUNK), f32)
    return pl.pallas_call(
        body,
        out_shape=(o_shape, o_shape, s_shape, s_shape, a_shape, a_shape),
        grid=(N, nh // hb),
        in_specs=[qsp(fw), qsp(bw), gsp(fw), gsp(bw), gsp(fw), gsp(bw)],
        out_specs=(osp(fw), osp(bw), ssp(fw), ssp(bw), asp(fw), asp(bw)),
        scratch_shapes=[pltpu.VMEM((2, nh, GDN_DIM, GDN_DIM), f32)],
        compiler_params=_cparams(("arbitrary", "arbitrary")),
        name="gdn_scan_fwd",
    )(qkvn, qkvn, gates, gates, gc, gc)


def _gdn_bwd(qkvn, gates, gc, st_f, st_b, a_f, a_b, do):
    T = qkvn.shape[1]
    N = T // CHUNK
    nh, hb = GDN_HEADS, GDN_HB
    nb = nh // hb
    W = hb * GDN_DIM

    def body(qf_ref, qb_ref, gf_ref, gb_ref, cf_ref, cb_ref, sf_ref, sb_ref, af_ref, ab_ref, dof_ref, dob_ref,
             dqf_ref, dqb_ref, dgf_ref, dgb_ref, ds_scr):
        n, b = pl.program_id(0), pl.program_id(1)
        hs = pl.ds(b * hb, hb)

        @pl.when(n == 0)
        def _():
            ds_scr[0, hs] = jnp.zeros((hb, GDN_DIM, GDN_DIM), f32)
            ds_scr[1, hs] = jnp.zeros((hb, GDN_DIM, GDN_DIM), f32)

        @pl.when(b == 0)
        def _():
            dgf_ref[...] = jnp.zeros((CHUNK, LANES), f32)
            dgb_ref[...] = jnp.zeros((CHUNK, LANES), f32)

        q, k, v, gcol, grow, beta = _gdn_chain_inputs(qf_ref, qb_ref, gf_ref, gb_ref, cf_ref, cb_ref, b, hb)
        ck = _GdnBatch(q, k, v, gcol, grow, beta, hb, A=jnp.concatenate([af_ref[0], ab_ref[0]], axis=0))
        S = jnp.concatenate([sf_ref[0], sb_ref[0]], axis=0)
        dS = jnp.concatenate([ds_scr[0, hs], ds_scr[1, hs]], axis=0)
        dov = jnp.stack(_heads(dof_ref, (), hb, GDN_DIM) + _heads(dob_ref, (), hb, GDN_DIM))
        v_new = ck.u - _mx(ck.w, S, "nn")
        d_vnew = _mx(ck.attn, dov, "tn") + _mx(ck.kd, dS, "nn")
        d_attn = jnp.where(ck.incl, _mx(dov, v_new, "nt"), 0.0)
        d_qd = _mx(dov, S, "nt")
        d_kd = _mx(v_new, dS, "nt")
        d_gl = jnp.sum(jnp.sum(dS * S, axis=2, keepdims=True), axis=1, keepdims=True) * ck.egl
        dS_new = dS * ck.egl + _mx(ck.qd, dov, "tn") - _mx(ck.w, d_vnew, "tn")
        ds_scr[0, hs] = dS_new[:hb]
        ds_scr[1, hs] = dS_new[hb:]
        d_w = -_mx(d_vnew, S, "nt")
        d_rhs = _h3s(ck.A2, _split2(jnp.concatenate([d_vnew, d_w], axis=2)), "tn")
        d_vb, d_kbg = d_rhs[:, :, :GDN_DIM], d_rhs[:, :, GDN_DIM:]
        dL = -jnp.where(ck.strict, _h3(d_rhs, ck.sol, "nt"), 0.0)
        P = dL * ck.decay
        Q = d_attn * ck.decay
        d_kb = _mx(P, k, "nn") + d_kbg * ck.eg
        d_qs = _mx(Q, k, "nn") + d_qd * ck.eg
        d_k = _mx(P, ck.kb, "tn") + _mx(Q, ck.qs, "tn") + d_kd * ck.ekd + d_kb * ck.beta
        E = dL * ck.L + d_attn * ck.attn
        kdsum = jnp.sum(d_kd * ck.kd, axis=2, keepdims=True)
        d_gc = (jnp.sum(E, axis=2, keepdims=True) - _colsum_col(E)
                + jnp.sum(d_kbg * ck.kbg, axis=2, keepdims=True)
                + jnp.sum(d_qd * ck.qd, axis=2, keepdims=True) - kdsum)
        d_gc = _add_at_last_row(d_gc, d_gl + jnp.sum(kdsum, axis=1, keepdims=True), hb)
        d_beta = jnp.sum(d_vb * v, axis=2, keepdims=True) + jnp.sum(d_kb * k, axis=2, keepdims=True)
        d_q = d_qs * (GDN_DIM**-0.5)
        d_v = d_vb * ck.beta
        lane = _iota2((CHUNK, LANES), 1)
        for d, (dq_ref, dg_ref) in enumerate(((dqf_ref, dgf_ref), (dqb_ref, dgb_ref))):
            acc = dg_ref[...]
            for hh in range(hb):
                c, h = d * hb + hh, b * hb + hh
                sl = slice(hh * GDN_DIM, (hh + 1) * GDN_DIM)
                dq_ref[0, :, sl] = d_q[c]
                dq_ref[1, :, sl] = d_k[c]
                dq_ref[2, :, sl] = d_v[c]
                acc = jnp.where(lane == h + 8 * d, d_gc[c], acc)
                acc = jnp.where(lane == 16 + h + 8 * d, d_beta[c], acc)
            dg_ref[...] = acc

        @pl.when(b == nb - 1)
        def _():
            for d, dg_ref in enumerate((dgf_ref, dgb_ref)):
                blk = dg_ref[...]
                dg_ref[...] = jnp.where(lane < 16, _exact01(_chunk_tri(CHUNK, d == 0), blk), blk)

    fw, bw = (lambda n: N - 1 - n), (lambda n: n)
    qsp = lambda ci: pl.BlockSpec((3, CHUNK, W), lambda n, b: (0, ci(n), b))
    gsp = lambda ci: pl.BlockSpec((CHUNK, LANES), lambda n, b: (ci(n), 0))
    osp = lambda ci: pl.BlockSpec((CHUNK, W), lambda n, b: (ci(n), b))
    ssp = lambda ci: pl.BlockSpec((1, hb, GDN_DIM, GDN_DIM), lambda n, b: (ci(n), b, 0, 0))
    asp = lambda ci: pl.BlockSpec((1, hb, CHUNK, CHUNK), lambda n, b: (ci(n), b, 0, 0))
    dq_shape = jax.ShapeDtypeStruct((3, T, nh * GDN_DIM), f32)
    dg_shape = jax.ShapeDtypeStruct((T, LANES), f32)
    return pl.pallas_call(
        body,
        out_shape=(dq_shape, dq_shape, dg_shape, dg_shape),
        grid=(N, nb),
        in_specs=[qsp(fw), qsp(bw), gsp(fw), gsp(bw), gsp(fw), gsp(bw), ssp(fw), ssp(bw), asp(fw), asp(bw), osp(fw), osp(bw)],
        out_specs=(qsp(fw), qsp(bw), gsp(fw), gsp(bw)),
        scratch_shapes=[pltpu.VMEM((2, nh, GDN_DIM, GDN_DIM), f32)],
        compiler_params=_cparams(("arbitrary", "arbitrary")),
        name="gdn_scan_bwd",
    )(qkvn, qkvn, gates, gates, gc, gc, st_f, st_b, a_f, a_b, do, do)


class _GlaBatch:
    def __init__(self, q, k, G, nf):
        B, Cn = q.shape[0], q.shape[1]
        self.incl, _, _ = _dir_masks(B, nf, Cn)
        self.eG = jnp.exp(G)
        self.enG = jnp.exp(-G)
        self.qg = q * (GLA_DK**-0.5) * self.eG
        self.kg = k * self.enG
        self.attn = jnp.where(self.incl, _mx(self.qg, self.kg, "nt"), 0.0)
        gl = _last_row(G, nf)
        self.ekd = jnp.exp(gl - G)
        self.kd = k * self.ekd
        self.egl = jnp.exp(gl)


def _gla_chain_inputs(refs_f, refs_b):
    nh = GLA_HEADS
    out = []
    for i, width in enumerate((GLA_DK, GLA_DK, GLA_DV, GLA_DK)):
        out.append(jnp.stack(_heads(refs_f[i], (), nh, width) + _heads(refs_b[i], (), nh, width)))
    return out


def _gla_specs(N, ci):
    nh = GLA_HEADS
    return [
        pl.BlockSpec((CHUNK, nh * GLA_DK), lambda n: (ci(n), 4096 // (nh * GLA_DK))),
        pl.BlockSpec((CHUNK, nh * GLA_DK), lambda n: (ci(n), 4608 // (nh * GLA_DK))),
        pl.BlockSpec((CHUNK, nh * GLA_DV), lambda n: (ci(n), 5120 // (nh * GLA_DV))),
        pl.BlockSpec((CHUNK, nh * GLA_DK), lambda n: (ci(n), 0)),
    ]


def _gla_fwd(proj, G_f, G_b):
    T = proj.shape[0]
    N = T // CHUNK
    nh = GLA_HEADS

    def body(qf, kf, vf, gf, qb, kb, vb, gb, of_ref, ob_ref, sf_ref, sb_ref, s_scr):
        n = pl.program_id(0)

        @pl.when(n == 0)
        def _():
            s_scr[...] = jnp.zeros((2 * nh, GLA_DV, GLA_DK), f32)

        q, k, v, G = _gla_chain_inputs((qf, kf, vf, gf), (qb, kb, vb, gb))
        ck = _GlaBatch(q, k, G, nh)
        St = s_scr[...]
        sf_ref[0], sb_ref[0] = St[:nh], St[nh:]
        o = _mx(ck.qg, St, "nt") + _mx(ck.attn, v, "nn")
        for i in range(nh):
            sl = slice(i * GLA_DV, (i + 1) * GLA_DV)
            of_ref[:, sl] = o[i]
            ob_ref[:, sl] = o[nh + i]
        s_scr[...] = St * ck.egl + _mx(v, ck.kd, "tn")

    fw, bw = (lambda n: n), (lambda n: N - 1 - n)
    osp = lambda ci: pl.BlockSpec((CHUNK, nh * GLA_DV), lambda n: (ci(n), 0))
    ssp = lambda ci: pl.BlockSpec((1, nh, GLA_DV, GLA_DK), lambda n: (ci(n), 0, 0, 0))
    o_shape = jax.ShapeDtypeStruct((T, nh * GLA_DV), f32)
    s_shape = jax.ShapeDtypeStruct((N, nh, GLA_DV, GLA_DK), f32)
    return pl.pallas_call(
        body,
        out_shape=(o_shape, o_shape, s_shape, s_shape),
        grid=(N,),
        in_specs=_gla_specs(N, fw) + _gla_specs(N, bw),
        out_specs=(osp(fw), osp(bw), ssp(fw), ssp(bw)),
        scratch_shapes=[pltpu.VMEM((2 * nh, GLA_DV, GLA_DK), f32)],
        compiler_params=_cparams(("arbitrary",)),
        name="gla_scan_fwd",
    )(proj, proj, proj, G_f, proj, proj, proj, G_b)


def _gla_bwd(proj, G_f, G_b, st_f, st_b, do):
    T = proj.shape[0]
    N = T // CHUNK
    nh = GLA_HEADS

    def body(qf, kf, vf, gf, sf_ref, dof_ref, qb, kb, vb, gb, sb_ref, dob_ref,
             dqf_ref, dkf_ref, dvf_ref, dgf_ref, dqb_ref, dkb_ref, dvb_ref, dgb_ref, ds_scr):
        n = pl.program_id(0)

        @pl.when(n == 0)
        def _():
            ds_scr[...] = jnp.zeros((2 * nh, GLA_DV, GLA_DK), f32)

        q, k, v, G = _gla_chain_inputs((qf, kf, vf, gf), (qb, kb, vb, gb))
        ck = _GlaBatch(q, k, G, nh)
        St = jnp.concatenate([sf_ref[0], sb_ref[0]], axis=0)
        dSt = ds_scr[...]
        dov = jnp.stack(_heads(dof_ref, (), nh, GLA_DV) + _heads(dob_ref, (), nh, GLA_DV))
        d_attn = jnp.where(ck.incl, _mx(dov, v, "nt"), 0.0)
        d_qg = _mx(dov, St, "nn") + _mx(d_attn, ck.kg, "nn")
        d_kg = _mx(d_attn, ck.qg, "tn")
        d_v = _mx(ck.attn, dov, "tn") + _mx(ck.kd, dSt, "nt")
        d_kd = _mx(v, dSt, "nn")
        ds_scr[...] = dSt * ck.egl + _mx(dov, ck.qg, "tn")
        kdd = d_kd * ck.kd
        d_gl = jnp.sum(dSt * St, axis=1, keepdims=True) * ck.egl + jnp.sum(kdd, axis=1, keepdims=True)
        d_q = d_qg * ck.eG * (GLA_DK**-0.5)
        d_k = d_kg * ck.enG + d_kd * ck.ekd
        d_G = _add_at_last_row(d_qg * ck.qg - d_kg * ck.kg - kdd, d_gl, nh)
        shp = (2 * nh, CHUNK, CHUNK)
        bb, ii, jj = _iota3(shp, 0), _iota3(shp, 1), _iota3(shp, 2)
        tri = jnp.logical_or(jnp.logical_and(bb < nh, jj >= ii), jnp.logical_and(bb >= nh, jj <= ii))
        d_gk = _exact01(jnp.where(tri, 1.0, 0.0), d_G)
        for d, (dq_ref, dk_ref, dv_ref, dg_ref) in enumerate(((dqf_ref, dkf_ref, dvf_ref, dgf_ref), (dqb_ref, dkb_ref, dvb_ref, dgb_ref))):
            for i in range(nh):
                c = d * nh + i
                ks, vs = slice(i * GLA_DK, (i + 1) * GLA_DK), slice(i * GLA_DV, (i + 1) * GLA_DV)
                dq_ref[:, ks] = d_q[c]
                dk_ref[:, ks] = d_k[c]
                dv_ref[:, vs] = d_v[c]
                dg_ref[:, ks] = d_gk[c]

    def in_specs(ci):
        return _gla_specs(N, ci) + [
            pl.BlockSpec((1, nh, GLA_DV, GLA_DK), lambda n: (ci(n), 0, 0, 0)),
            pl.BlockSpec((CHUNK, nh * GLA_DV), lambda n: (ci(n), 0)),
        ]

    def out_specs(ci):
        ksp = pl.BlockSpec((CHUNK, nh * GLA_DK), lambda n: (ci(n), 0))
        return [ksp, ksp, pl.BlockSpec((CHUNK, nh * GLA_DV), lambda n: (ci(n), 0)), ksp]

    fw, bw = (lambda n: N - 1 - n), (lambda n: n)
    k_shape = jax.ShapeDtypeStruct((T, nh * GLA_DK), f32)
    v_shape = jax.ShapeDtypeStruct((T, nh * GLA_DV), f32)
    return pl.pallas_call(
        body,
        out_shape=(k_shape, k_shape, v_shape, k_shape) * 2,
        grid=(N,),
        in_specs=in_specs(fw) + in_specs(bw),
        out_specs=tuple(out_specs(fw) + out_specs(bw)),
        scratch_shapes=[pltpu.VMEM((2 * nh, GLA_DV, GLA_DK), f32)],
        compiler_params=_cparams(("arbitrary",)),
        name="gla_scan_bwd",
    )(proj, proj, proj, G_f, st_f, do, proj, proj, proj, G_b, st_b, do)


def _local_step(x, target, wcat, conv_w, avec, dvec, gdn_norm_w, w2f, w2b, b2f, b2b, gla_norm_w, wpg, wpl, wout, w_pre, w_post):
    h = _rms_pre(x, w_pre)
    proj = _matmul(h, wcat, "nn", f32, "proj_in", 512, 512, 1024)
    qkvn = _gdn_prep(proj, conv_w)
    gates, gc = _gdn_gates(proj, avec, dvec)
    oa_f, oa_b, sa_f, sa_b, inv_f, inv_b = _gdn_fwd(qkvn, gates, gc)
    og = _out_norm(oa_f, oa_b, proj, Z_BLK, gdn_norm_w, GDN_DIM, "gdn_out")
    G_f, G_b = _gla_prep(proj, w2f, w2b, b2f, b2b)
    ob_f, ob_b, sb_f, sb_b = _gla_fwd(proj, G_f, G_b)
    obg = _out_norm(ob_f, ob_b, proj, G_B_BLK, gla_norm_w, GLA_DV, "gla_out")
    (loss_row, d_wpost, dy, m, dout, dya, dyb, dga, dgb, dog, dobg) = _mid(og, obg, proj, x, target, wpg, wpl, wout, w_post)

    d_wout = _matmul(m, dout, "tn", f32, "dw_out", 512, 512, 2048)
    d_wpg = _matmul(og, dya, "tn", f32, "dw_proj_gdn", 512, 512, 2048)
    d_wpl = _matmul(obg, dyb, "tn", f32, "dw_proj_gla", 512, 512, 2048)

    do_b, d_gb, d_gla_norm = _out_norm_bwd(dobg, ob_f, ob_b, proj, G_B_BLK, gla_norm_w, GLA_DV, "gla_out_bwd")
    dq_f, dk_f, dv_f, dgk_f, dq_b, dk_b, dv_b, dgk_b = _gla_bwd(proj, G_f, G_b, sb_f, sb_b, do_b)
    dsmall_gla, d_w2f, d_w2b, d_b2f, d_b2b = _gla_prep_bwd(proj, w2f, w2b, b2f, b2b, dgk_f, dgk_b)

    do_a, d_z, d_gdn_norm = _out_norm_bwd(dog, oa_f, oa_b, proj, Z_BLK, gdn_norm_w, GDN_DIM, "gdn_out_bwd")
    dqkv_f, dqkv_b, dg_f, dg_b = _gdn_bwd(qkvn, gates, gc, sa_f, sa_b, inv_f, inv_b, do_a)
    d_qkv, d_conv = _gdn_prep_bwd(proj, conv_w, dqkv_f, dqkv_b)
    d_small, d_alog, d_dtb = _gdn_gates_bwd(proj, avec, dvec, dg_f, dg_b, dsmall_gla)

    T = x.shape[0]
    dproj = jnp.concatenate(
        [d_qkv, d_z, (dq_f + dq_b).astype(MXU_DTYPE), (dk_f + dk_b).astype(MXU_DTYPE), (dv_f + dv_b).astype(MXU_DTYPE),
         d_gb, dga, dgb, d_small, jnp.zeros((T, N_CAT - SMALL_OFF - LANES), MXU_DTYPE)], axis=1)
    d_wcat = _matmul(h, dproj, "tn", f32, "dw_in", 512, 512, 2048)
    dh = _matmul(dproj, wcat, "nt", f32, "dh", 512, 1024, 512)
    grad_x, d_wpre = _rms_pre_bwd(x, w_pre, dh, dy)
    return dict(loss_row=loss_row, grad_x=grad_x, d_wcat=d_wcat, d_conv=d_conv, d_alog=d_alog, d_dtb=d_dtb,
                d_gdn_norm=d_gdn_norm, d_wpg=d_wpg, d_w2f=d_w2f, d_w2b=d_w2b, d_b2f=d_b2f, d_b2b=d_b2b,
                d_gla_norm=d_gla_norm, d_wpl=d_wpl, d_wout=d_wout, d_wpost=d_wpost, d_wpre=d_wpre)


def _to_cat(wf):
    pad = jnp.zeros(wf.shape[:-1] + (N_CAT - N_IN,), wf.dtype)
    return jnp.concatenate([wf[..., :4096], wf[..., 4128:7200], wf[..., 7232:9280], wf[..., 4096:4128], wf[..., 7200:7232], pad], axis=-1)


def _from_cat(wc):
    return jnp.concatenate([wc[..., :4096], wc[..., 9216:9248], wc[..., 4096:7168], wc[..., 9248:9280], wc[..., 7168:9216]], axis=-1)


def _lane_row(*pieces):
    row = jnp.concatenate(pieces, axis=1)
    return jnp.pad(row, ((0, 0), (0, LANES - row.shape[1])))


def _pad_w2(w2, row0):
    return jnp.pad(w2, ((row0, LANES - row0 - w2.shape[0]), (0, 0)))


SM_ROWS, SM_COLS = 32, 384
REP_ROWS = 32


def _pack_small_shard(conv, w2f, w2b):
    lead = conv.shape[:-2]
    z = lambda r, c: jnp.zeros(lead + (r, c), f32)
    top = jnp.concatenate([conv, z(8 - CONV_K, SM_COLS)], axis=-2)
    mid = jnp.concatenate([w2f, w2b, z(16, SM_COLS - 128)], axis=-1)
    return jnp.concatenate([top, mid, z(SM_ROWS - 24, SM_COLS)], axis=-2)


def _pack_rep(ln_pre, ln_post, b2f, b2b, gla_norm, gdn_norm, alog_f, alog_b, dt_f, dt_b):
    last = _lane_row(alog_f, alog_b, dt_f, dt_b)
    rows = [ln_pre.reshape(8, LANES), ln_post.reshape(8, LANES), b2f.reshape(4, LANES), b2b.reshape(4, LANES),
            gla_norm.reshape(2, LANES), gdn_norm.reshape(1, LANES), last, jnp.zeros((REP_ROWS - 28, LANES), f32)]
    return jnp.concatenate(rows, axis=0)


def _unpack_rep(p):
    return dict(ln_pre_w=p[0:8].reshape(1, 1024), ln_post_w=p[8:16].reshape(1, 1024), gk_b2_fwd=p[16:20].reshape(1, 512),
                gk_b2_bwd=p[20:24].reshape(1, 512), gla_norm_w=p[24:26].reshape(1, 256), gdn_norm_w=p[26:27],
                a_log_fwd=p[27:28, 0:8], a_log_bwd=p[27:28, 8:16], dt_bias_fwd=p[27:28, 16:24], dt_bias_bwd=p[27:28, 24:32])


_MESH = pl.DeviceIdType.MESH


def _gather_weights(w_in_s, wp_s, sm_s):
    shapes = (w_in_s.shape, wp_s.shape, sm_s.shape)
    dtypes = (MXU_DTYPE, MXU_DTYPE, f32)

    def body(win_ref, wp_ref, sm_ref, gin_ref, gp_ref, gsm_ref, send_sems, recv_sems):
        x, y, c = lax.axis_index("x"), lax.axis_index("y"), lax.axis_index("c")
        me, sibling = (x, y, c), (x, y, 1 - c)
        chips = [(1 - x, y), (x, 1 - y), (1 - x, 1 - y)]
        bufs = (gin_ref, gp_ref, gsm_ref)

        def idx(px, py, pc):
            return 4 * px + 2 * py + pc

        rows_in = w_in_s.shape[0] // 8
        for r in range(8):
            gin_ref[idx(*me), r * rows_in : (r + 1) * rows_in, :] = win_ref[r * rows_in : (r + 1) * rows_in, :].astype(MXU_DTYPE)
        gp_ref[idx(*me)] = wp_ref[...].astype(MXU_DTYPE)
        gsm_ref[idx(*me)] = sm_ref[...]

        def copy(a, k, block, to):
            blk = bufs[a].at[idx(*block)]
            return pltpu.make_async_remote_copy(src_ref=blk, dst_ref=blk, send_sem=send_sems.at[7 * a + k],
                                                recv_sem=recv_sems.at[7 * a + k], device_id=to, device_id_type=_MESH)

        first, passed = [], []
        for a in range(3):
            first.append(copy(a, 0, me, sibling))
            first += [copy(a, 1 + j, me, (*chip, c)) for j, chip in enumerate(chips)]
        for cp in first:
            cp.start()
        for j, chip in enumerate(chips):
            for a in range(3):
                copy(a, 1 + j, (*chip, c), me).wait_recv()
                cp = copy(a, 4 + j, (*chip, c), sibling)
                cp.start()
                passed.append(cp)
        for a in range(3):
            copy(a, 0, sibling, me).wait_recv()
            for j, chip in enumerate(chips):
                copy(a, 4 + j, (*chip, 1 - c), me).wait_recv()
        for cp in first + passed:
            cp.wait_send()

    vm = pl.BlockSpec(memory_space=pltpu.VMEM)
    return pl.pallas_call(
        body,
        out_shape=tuple(jax.ShapeDtypeStruct((N_DEV,) + s, d) for s, d in zip(shapes, dtypes)),
        in_specs=[vm, vm, vm],
        out_specs=(vm, vm, vm),
        scratch_shapes=[pltpu.SemaphoreType.DMA((21,)), pltpu.SemaphoreType.DMA((21,))],
        compiler_params=pltpu.CompilerParams(vmem_limit_bytes=VMEM_LIMIT),
        name="gather_weights",
    )(w_in_s, wp_s, sm_s)


N_CHIP = 4


def _exchange_sibling(g_in, g_p):
    srcs = (g_in, g_p)

    def body(in0, in1, out0, out1, send_sems, recv_sems):
        x, y, c = lax.axis_index("x"), lax.axis_index("y"), lax.axis_index("c")
        ins, outs = (in0, in1), (out0, out1)

        def copy(a, k):
            return pltpu.make_async_remote_copy(src_ref=ins[a].at[2 * k + 1 - c], dst_ref=outs[a].at[k], send_sem=send_sems.at[N_CHIP * a + k],
                                                recv_sem=recv_sems.at[N_CHIP * a + k], device_id=(x, y, 1 - c), device_id_type=_MESH)

        copies = [copy(a, k) for a in range(2) for k in range(N_CHIP)]
        for cp in copies:
            cp.start()
        for cp in copies:
            cp.wait_recv()
        for cp in copies:
            cp.wait_send()

    hbm = pl.BlockSpec(memory_space=pltpu.HBM)
    return pl.pallas_call(
        body,
        out_shape=tuple(jax.ShapeDtypeStruct((N_CHIP,) + s.shape[1:], s.dtype) for s in srcs),
        in_specs=[hbm] * 2,
        out_specs=(hbm,) * 2,
        scratch_shapes=[pltpu.SemaphoreType.DMA((2 * N_CHIP,)), pltpu.SemaphoreType.DMA((2 * N_CHIP,))],
        name="exchange_sibling",
    )(*srcs)


def _pair_sum(g, r, name, tb=128):
    _, R, Cc = g.shape

    def body(g_ref, r_ref, o_ref):
        o_ref[...] = (g_ref[...].astype(f32) + r_ref[...].astype(f32)).astype(o_ref.dtype)

    return pl.pallas_call(
        body,
        out_shape=jax.ShapeDtypeStruct(r.shape, r.dtype),
        grid=(N_CHIP, R // tb),
        in_specs=[pl.BlockSpec((1, tb, Cc), lambda k, i: (2 * k + lax.axis_index("c"), i, 0)), pl.BlockSpec((1, tb, Cc), lambda k, i: (k, i, 0))],
        out_specs=pl.BlockSpec((1, tb, Cc), lambda k, i: (k, i, 0)),
        compiler_params=_cparams(("parallel", "parallel")),
        name=name,
    )(g, r)


def _exchange_grads(q_in, q_p, g_sm, g_rep):
    srcs = (q_in, q_p, g_sm, g_rep)

    def body(in0, in1, in2, in3, out0, out1, out2, out3, send_sems, recv_sems, local_sems):
        x, y, c = lax.axis_index("x"), lax.axis_index("y"), lax.axis_index("c")
        me, my_chip = 4 * x + 2 * y + c, 2 * x + y
        ins, outs = (in0, in1, in2, in3), (out0, out1, out2, out3)

        def peer(k):
            px = (1 - x) if (k & 4) else x
            py = (1 - y) if (k & 2) else y
            pc = (1 - c) if (k & 1) else c
            return (px, py, pc)

        def slots(a, k):
            px, py, pc = peer(k)
            if a < 2:
                return ins[a].at[2 * px + py], my_chip, 2 * px + py
            p = 4 * px + 2 * py + pc
            return (ins[a].at[p] if a == 2 else ins[a]), me, p

        def copy(a, k, arrival):
            src, there, here = slots(a, k)
            sem = (7 * (a - 2) + k - 1) if a >= 2 else (14 + 3 * a + k // 2 - 1)
            return pltpu.make_async_remote_copy(src_ref=src, dst_ref=outs[a].at[here if arrival else there], send_sem=send_sems.at[sem],
                                                recv_sem=recv_sems.at[sem], device_id=peer(k), device_id_type=_MESH)

        pairs = [(a, k) for a in (0, 1) for k in (2, 4, 6)] + [(a, k) for a in (2, 3) for k in range(1, 8)]
        sends = [copy(a, k, False) for a, k in pairs]
        for cp in sends:
            cp.start()
        own = (ins[0].at[my_chip], ins[1].at[my_chip], ins[2].at[me], ins[3])
        own_dst = (outs[0].at[my_chip], outs[1].at[my_chip], outs[2].at[me], outs[3].at[me])
        local = [pltpu.make_async_copy(own[a], own_dst[a], local_sems.at[a]) for a in range(4)]
        for cp in local:
            cp.start()
        for a, k in pairs:
            copy(a, k, True).wait_recv()
        for cp in sends:
            cp.wait_send()
        for cp in local:
            cp.wait()

    hbm = pl.BlockSpec(memory_space=pltpu.HBM)
    out_shape = tuple(jax.ShapeDtypeStruct(s.shape if i < 3 else (N_DEV,) + s.shape, s.dtype) for i, s in enumerate(srcs))
    return pl.pallas_call(
        body,
        out_shape=out_shape,
        in_specs=[hbm] * 4,
        out_specs=(hbm,) * 4,
        scratch_shapes=[pltpu.SemaphoreType.DMA((20,)), pltpu.SemaphoreType.DMA((20,)), pltpu.SemaphoreType.DMA((4,))],
        name="exchange_grads",
    )(*srcs)


def _sum_adam(parts, w, m, v, name, tb):
    R, Cc = w.shape
    n_parts = parts.shape[0]
    tb = min(tb, R)
    assert R % tb == 0
    bc1 = 1.0 - ADAM_B1**ADAM_STEP
    bc2 = 1.0 - ADAM_B2**ADAM_STEP

    def body(p_ref, w_ref, m_ref, v_ref, g_ref, d_ref, nm_ref, nv_ref):
        g = p_ref[0].astype(f32)
        for j in range(1, n_parts):
            g = g + p_ref[j].astype(f32)
        g_ref[...] = g
        m2 = ADAM_B1 * m_ref[...] + (1.0 - ADAM_B1) * g
        v2 = ADAM_B2 * v_ref[...] + (1.0 - ADAM_B2) * (g * g)
        nm_ref[...] = m2
        nv_ref[...] = v2
        d_ref[...] = -ADAM_LR * ((m2 / bc1) / (jnp.sqrt(v2 / bc2) + ADAM_EPS) + ADAM_WD * w_ref[...])

    blk = pl.BlockSpec((tb, Cc), lambda i: (i, 0))
    o = jax.ShapeDtypeStruct((R, Cc), f32)
    return pl.pallas_call(
        body,
        out_shape=(o, o, o, o),
        grid=(R // tb,),
        in_specs=[pl.BlockSpec((n_parts, tb, Cc), lambda i: (0, i, 0)), blk, blk, blk],
        out_specs=(blk, blk, blk, blk),
        compiler_params=_cparams(("parallel",)),
        name=name,
    )(parts, w, m, v)


_WEIGHTS = ["ln_pre_w", "w_in", "conv_w", "a_log_fwd", "a_log_bwd", "dt_bias_fwd", "dt_bias_bwd", "gdn_norm_w", "w_proj_gdn",
            "gk_w2_fwd", "gk_b2_fwd", "gk_w2_bwd", "gk_b2_bwd", "gla_norm_w", "w_proj_gla", "w_out", "ln_post_w"]
_REP_ORDER = ["ln_pre_w", "ln_post_w", "gk_b2_fwd", "gk_b2_bwd", "gla_norm_w", "gdn_norm_w", "a_log_fwd", "a_log_bwd", "dt_bias_fwd", "dt_bias_bwd"]


def kernel(x, ln_pre_w, w_in, conv_w, a_log_fwd, a_log_bwd, dt_bias_fwd, dt_bias_bwd, gdn_norm_w, w_proj_gdn, gk_w2_fwd, gk_b2_fwd, gk_w2_bwd, gk_b2_bwd, gla_norm_w, w_proj_gla, w_out, ln_post_w, loss_target, m_ln_pre_w, m_w_in, m_conv_w, m_a_log_fwd, m_a_log_bwd, m_dt_bias_fwd, m_dt_bias_bwd, m_gdn_norm_w, m_w_proj_gdn, m_gk_w2_fwd, m_gk_b2_fwd, m_gk_w2_bwd, m_gk_b2_bwd, m_gla_norm_w, m_w_proj_gla, m_w_out, m_ln_post_w, v_ln_pre_w, v_w_in, v_conv_w, v_a_log_fwd, v_a_log_bwd, v_dt_bias_fwd, v_dt_bias_bwd, v_gdn_norm_w, v_w_proj_gdn, v_gk_w2_fwd, v_gk_b2_fwd, v_gk_w2_bwd, v_gk_b2_bwd, v_gla_norm_w, v_w_proj_gla, v_w_out, v_ln_post_w):
    args = locals()
    W = {n: args[n] for n in _WEIGHTS}
    M = {n: args["m_" + n] for n in _WEIGHTS}
    V = {n: args["v_" + n] for n in _WEIGHTS}

    wp_stack = lambda P: jnp.concatenate([P["w_proj_gdn"], P["w_proj_gla"], P["w_out"]], axis=0)
    sm_pack = lambda P: _pack_small_shard(P["conv_w"][0], P["gk_w2_fwd"][0], P["gk_w2_bwd"][0])
    g_in, g_p, g_sm = _gather_weights(w_in[0], wp_stack(W), sm_pack(W))
    wcat = _to_cat(g_in.transpose(1, 0, 2).reshape(D_MODEL, N_IN))
    wpg, wpl, wout = (g_p[:, i].reshape(D_MODEL, D_MODEL) for i in range(3))
    conv_full = g_sm[:, 0:CONV_K, :].transpose(1, 0, 2).reshape(CONV_K, N_DEV * SM_COLS)
    w2f_full = g_sm[:, 8:24, 0:64].transpose(1, 0, 2).reshape(16, 512)
    w2b_full = g_sm[:, 8:24, 64:128].transpose(1, 0, 2).reshape(16, 512)

    avec = _lane_row(a_log_fwd, a_log_bwd)
    dvec = _lane_row(dt_bias_fwd, dt_bias_bwd)
    G = _local_step(x[0], loss_target[0], wcat, conv_full, avec, dvec, gdn_norm_w, _pad_w2(w2f_full, 32), _pad_w2(w2b_full, 48),
                    gk_b2_fwd, gk_b2_bwd, gla_norm_w, wpg, wpl, wout, ln_pre_w, ln_post_w)
    loss = lax.psum(0.5 * jnp.sum(G["loss_row"]) / D_MODEL, ("x", "y", "c"))

    p_in = _from_cat(G["d_wcat"]).reshape(D_MODEL, N_DEV, SHARD_IN).transpose(1, 0, 2).astype(MXU_DTYPE)
    p_p = jnp.stack([G["d_wpg"], G["d_wpl"], G["d_wout"]]).reshape(3, N_DEV, D_MODEL // N_DEV, D_MODEL).transpose(1, 0, 2, 3).astype(MXU_DTYPE)
    p_sm = _pack_small_shard(G["d_conv"].reshape(CONV_K, N_DEV, SM_COLS).transpose(1, 0, 2),
                             G["d_w2f"][32:48].reshape(16, N_DEV, 64).transpose(1, 0, 2),
                             G["d_w2b"][48:64].reshape(16, N_DEV, 64).transpose(1, 0, 2))
    p_rep = _pack_rep(G["d_wpre"], G["d_wpost"], G["d_b2f"], G["d_b2b"], G["d_gla_norm"], G["d_gdn_norm"],
                      G["d_alog"][:, 0:8], G["d_alog"][:, 8:16], G["d_dtb"][:, 0:8], G["d_dtb"][:, 8:16])
    rows_p = 3 * D_MODEL // N_DEV
    p_p = p_p.reshape(N_DEV, rows_p, D_MODEL)
    sib_in, sib_p = _exchange_sibling(p_in, p_p)
    q_in = _pair_sum(p_in, sib_in, "pair_sum_w_in")
    q_p = _pair_sum(p_p, sib_p, "pair_sum_w_proj")
    r_in, r_p, r_sm, r_rep = _exchange_grads(q_in, q_p, p_sm, p_rep)

    o_in = _sum_adam(r_in, w_in[0], m_w_in[0], v_w_in[0], "adam_w_in", 128)
    o_p = _sum_adam(r_p, wp_stack(W).reshape(rows_p, D_MODEL), wp_stack(M).reshape(rows_p, D_MODEL),
                    wp_stack(V).reshape(rows_p, D_MODEL), "adam_w_proj", 128)
    o_sm = _sum_adam(r_sm, sm_pack(W), sm_pack(M), sm_pack(V), "adam_small", SM_ROWS)
    rep_pack = lambda P: _pack_rep(*[P[n] for n in _REP_ORDER])
    o_rep = _sum_adam(r_rep, rep_pack(W), rep_pack(M), rep_pack(V), "adam_rep", REP_ROWS)

    res = []
    for kind in range(4):
        rep = _unpack_rep(o_rep[kind])
        per = dict(rep)
        per["w_in"] = o_in[kind][None]
        pp = o_p[kind].reshape(3, 1, D_MODEL // N_DEV, D_MODEL)
        per["w_proj_gdn"], per["w_proj_gla"], per["w_out"] = pp[0], pp[1], pp[2]
        per["conv_w"] = o_sm[kind][None, 0:CONV_K, :]
        per["gk_w2_fwd"] = o_sm[kind][None, 8:24, 0:64]
        per["gk_w2_bwd"] = o_sm[kind][None, 8:24, 64:128]
        res.append([per[n] for n in _WEIGHTS])
    return (loss, G["grad_x"][None], *res[0], *res[1], *res[2], *res[3])
```

```python
import functools
import math

import jax
import jax.numpy as jnp
from jax import lax
from jax.experimental import pallas as pl
from jax.experimental.pallas import tpu as pltpu

f32 = jnp.float32
MXU_DTYPE = jnp.bfloat16
HI = lax.Precision.HIGHEST

D_MODEL = 1024
CHUNK = 64
NORM_EPS = 1e-6
GDN_HEADS, GDN_DIM = 8, 128
GLA_HEADS, GLA_DK, GLA_DV = 4, 128, 256
GATE_NORMALIZER = 16.0
CONV_K = 5
N_IN = 9280
N_DEV = 8
SHARD_IN = N_IN // N_DEV

N_CAT = 9728
SMALL_OFF = 9216
LANES = 128

ADAM_LR, ADAM_B1, ADAM_B2, ADAM_EPS, ADAM_WD, ADAM_STEP = 0.001, 0.9, 0.999, 1e-08, 0.01, 10

VMEM_LIMIT = 56 * 1024 * 1024


def _cparams(sem=None):
    return pltpu.CompilerParams(dimension_semantics=sem, vmem_limit_bytes=VMEM_LIMIT)


_DN = {"nn": (((1,), (0,)), ((), ())), "nt": (((1,), (1,)), ((), ())), "tn": (((0,), (0,)), ((), ()))}
_BDN = {"nn": (((2,), (1,)), ((0,), (0,))), "nt": (((2,), (2,)), ((0,), (0,))), "tn": (((1,), (1,)), ((0,), (0,)))}


def _dot16(a, b, kind):
    return lax.dot_general(a, b, (_DN if a.ndim == 2 else _BDN)[kind], preferred_element_type=f32)


def _mx(a, b, kind):
    return _dot16(a.astype(MXU_DTYPE), b.astype(MXU_DTYPE), kind)


def _split2(a):
    hi = a.astype(jnp.bfloat16)
    return hi, (a - hi.astype(f32)).astype(jnp.bfloat16)


def _split3(a):
    p1 = a.astype(jnp.bfloat16)
    r = a - p1.astype(f32)
    p2 = r.astype(jnp.bfloat16)
    return p1, p2, (r - p2.astype(f32)).astype(jnp.bfloat16)


def _h3s(a2, b2, kind):
    (ah, al), (bh, bl) = a2, b2
    return _dot16(ah, bh, kind) + (_dot16(ah, bl, kind) + _dot16(al, bh, kind))


def _h3(a, b, kind):
    return _h3s(_split2(a), _split2(b), kind)


def _exact01(t01, x, kind="nn"):
    t = t01.astype(jnp.bfloat16)
    x1, x2, x3 = _split3(x)
    return _dot16(t, x1, kind) + (_dot16(t, x2, kind) + _dot16(t, x3, kind))


def _colsum_col(e):
    ones = jnp.ones(e.shape[:-1] + (LANES,), jnp.bfloat16)
    e1, e2, e3 = _split3(e)
    return (_dot16(e1, ones, "tn") + (_dot16(e2, ones, "tn") + _dot16(e3, ones, "tn")))[..., 0:1]


def _iota3(shape, axis):
    return lax.broadcasted_iota(jnp.int32, shape, axis)


def _dir_masks(B, nf, Cn):
    shp = (B, Cn, Cn)
    bb, ii, jj = _iota3(shp, 0), _iota3(shp, 1), _iota3(shp, 2)
    fwd = bb < nf
    rev = jnp.logical_not(fwd)
    incl = jnp.logical_or(jnp.logical_and(fwd, ii >= jj), jnp.logical_and(rev, ii <= jj))
    strict = jnp.logical_and(incl, ii != jj)
    return incl, strict, ii == jj


def _last_row(x, nf):
    B, Cn = x.shape[0], x.shape[1]
    fwd = _iota3((B, 1, 1), 0) < nf
    return jnp.where(fwd, x[:, Cn - 1 : Cn, :], x[:, 0:1, :])


def _add_at_last_row(x, val, nf):
    B, Cn = x.shape[0], x.shape[1]
    bb, rr = _iota3((B, Cn, 1), 0), _iota3((B, Cn, 1), 1)
    at = jnp.logical_or(jnp.logical_and(bb < nf, rr == Cn - 1), jnp.logical_and(bb >= nf, rr == 0))
    return x + jnp.where(at, val, 0.0)


def _sigmoid(x):
    return jax.nn.sigmoid(x)


def _silu(x):
    return x * _sigmoid(x)


def _silu_grad(x):
    s = _sigmoid(x)
    return s * (1.0 + x * (1.0 - s))


def _softplus(x):
    u = jnp.exp(-jnp.abs(x))
    l1p = jnp.where(u < 1e-3, u * (1.0 - u * (0.5 - u * (1.0 / 3.0))), jnp.log(1.0 + u))
    return jnp.maximum(x, 0.0) + l1p


def _iota2(shape, axis):
    return lax.broadcasted_iota(jnp.int32, shape, axis)


def _matmul(a, b, kind, out_dtype, name, tm, tn, tk):
    if kind == "nn":
        (M, K), N = a.shape, b.shape[1]
    elif kind == "nt":
        (M, K), N = a.shape, b.shape[0]
    else:
        (K, M), N = a.shape, b.shape[1]
    tm, tn, tk = min(tm, M), min(tn, N), min(tk, K)
    assert M % tm == 0 and N % tn == 0 and K % tk == 0, (name, M, N, K)
    nk = K // tk
    if kind == "tn":
        a_spec = pl.BlockSpec((tk, tm), lambda i, j, k: (k, i))
    else:
        a_spec = pl.BlockSpec((tm, tk), lambda i, j, k: (i, k))
    if kind == "nt":
        b_spec = pl.BlockSpec((tn, tk), lambda i, j, k: (j, k))
    else:
        b_spec = pl.BlockSpec((tk, tn), lambda i, j, k: (k, j))

    def body(a_ref, b_ref, o_ref, *acc):
        p = _mx(a_ref[...], b_ref[...], kind)
        if nk == 1:
            o_ref[...] = p.astype(out_dtype)
        else:
            acc_ref = acc[0]
            k = pl.program_id(2)

            @pl.when(k == 0)
            def _():
                acc_ref[...] = p

            @pl.when(k > 0)
            def _():
                acc_ref[...] += p

            @pl.when(k == nk - 1)
            def _():
                o_ref[...] = acc_ref[...].astype(out_dtype)

    return pl.pallas_call(
        body,
        out_shape=jax.ShapeDtypeStruct((M, N), out_dtype),
        grid=(M // tm, N // tn, nk),
        in_specs=[a_spec, b_spec],
        out_specs=pl.BlockSpec((tm, tn), lambda i, j, k: (i, j)),
        scratch_shapes=[] if nk == 1 else [pltpu.VMEM((tm, tn), f32)],
        compiler_params=_cparams(("parallel", "parallel", "arbitrary")),
        name=name,
    )(a, b)


def _rms_pre(x, w, tb=256):
    T = x.shape[0]

    def body(x_ref, w_ref, h_ref, ht_ref):
        xv = x_ref[...]
        r = lax.rsqrt(jnp.mean(xv * xv, axis=1, keepdims=True) + NORM_EPS)
        h = xv * r * w_ref[...]
        h_ref[...] = h.astype(h_ref.dtype)
        ht_ref[...] = h.T.astype(ht_ref.dtype)

    return pl.pallas_call(
        body,
        out_shape=(jax.ShapeDtypeStruct((T, D_MODEL), MXU_DTYPE), jax.ShapeDtypeStruct((D_MODEL, T), MXU_DTYPE)),
        grid=(T // tb,),
        in_specs=[pl.BlockSpec((tb, D_MODEL), lambda i: (i, 0)), pl.BlockSpec((1, D_MODEL), lambda i: (0, 0))],
        out_specs=(pl.BlockSpec((tb, D_MODEL), lambda i: (i, 0)), pl.BlockSpec((D_MODEL, tb), lambda i: (0, i))),
        compiler_params=_cparams(("parallel",)),
        name="rms_pre",
    )(x, w)


def _rms_pre_bwd(x, w, dh, dy, tb=256):
    T = x.shape[0]

    def body(x_ref, w_ref, dh_ref, dy_ref, dx_ref, dw_ref):
        i = pl.program_id(0)
        xv, dhv = x_ref[...], dh_ref[...]
        r = lax.rsqrt(jnp.mean(xv * xv, axis=1, keepdims=True) + NORM_EPS)
        dhw = dhv * w_ref[...]
        dx_ref[...] = dy_ref[...] + r * dhw - xv * (r * r * r) * jnp.mean(dhw * xv, axis=1, keepdims=True)
        part = jnp.sum(dhv * xv * r, axis=0, keepdims=True)

        @pl.when(i == 0)
        def _():
            dw_ref[...] = part

        @pl.when(i > 0)
        def _():
            dw_ref[...] += part

    blk = pl.BlockSpec((tb, D_MODEL), lambda i: (i, 0))
    row = pl.BlockSpec((1, D_MODEL), lambda i: (0, 0))
    return pl.pallas_call(
        body,
        out_shape=(jax.ShapeDtypeStruct((T, D_MODEL), f32), jax.ShapeDtypeStruct((1, D_MODEL), f32)),
        grid=(T // tb,),
        in_specs=[blk, row, blk, blk],
        out_specs=(blk, row),
        compiler_params=_cparams(("arbitrary",)),
        name="rms_pre_bwd",
    )(x, w, dh, dy)


def _conv5(u, cw):
    T = u.shape[0]
    t = _iota2(u.shape, 0)
    acc = u * cw[2:3, :]
    for j in (0, 1, 3, 4):
        d = j - 2
        sh = pltpu.roll(u, (-d) % T, 0)
        valid = jnp.logical_and(t + d >= 0, t + d < T)
        acc = acc + jnp.where(valid, sh, 0.0) * cw[j : j + 1, :]
    return acc


def _gdn_prep(proj, conv_w):
    T = proj.shape[0]
    nh = GDN_HEADS

    def body(u_ref, cw_ref, o_ref):
        j = pl.program_id(0)
        s = _silu(_conv5(u_ref[...], cw_ref[...]))

        @pl.when(j < 2 * nh)
        def _():
            o_ref[0] = s * lax.rsqrt(jnp.sum(s * s, axis=1, keepdims=True) + NORM_EPS)

        @pl.when(j >= 2 * nh)
        def _():
            o_ref[0] = s

    return pl.pallas_call(
        body,
        out_shape=jax.ShapeDtypeStruct((3, T, GDN_HEADS * GDN_DIM), f32),
        grid=(3 * nh,),
        in_specs=[pl.BlockSpec((T, GDN_DIM), lambda j: (0, j)), pl.BlockSpec((CONV_K, GDN_DIM), lambda j: (0, j))],
        out_specs=pl.BlockSpec((1, T, GDN_DIM), lambda j: (j // nh, 0, j % nh)),
        compiler_params=_cparams(("parallel",)),
        name="gdn_prep",
    )(proj, conv_w)


def _gdn_prep_bwd(proj, conv_w, dqkv_f, dqkv_b):
    T = proj.shape[0]
    nh = GDN_HEADS

    def body(u_ref, cw_ref, df_ref, db_ref, du_ref, dcw_ref):
        j = pl.program_id(0)
        u, cw = u_ref[...], cw_ref[...]
        c = _conv5(u, cw)
        s = _silu(c)
        dn = df_ref[0] + db_ref[0]
        rinv = lax.rsqrt(jnp.sum(s * s, axis=1, keepdims=True) + NORM_EPS)
        ds_norm = rinv * dn - s * (rinv * rinv * rinv) * jnp.sum(dn * s, axis=1, keepdims=True)
        ds = jnp.where(j < 2 * nh, ds_norm, dn)
        dc = ds * _silu_grad(c)
        t = _iota2(u.shape, 0)
        du = dc * cw[2:3, :]
        rows = []
        for jj in range(CONV_K):
            d = jj - 2
            if d == 0:
                rows.append(jnp.sum(dc * u, axis=0, keepdims=True))
                continue
            ush = pltpu.roll(u, (-d) % T, 0)
            uvalid = jnp.logical_and(t + d >= 0, t + d < T)
            rows.append(jnp.sum(dc * jnp.where(uvalid, ush, 0.0), axis=0, keepdims=True))
            dsh = pltpu.roll(dc, d % T, 0)
            dvalid = jnp.logical_and(t - d >= 0, t - d < T)
            du = du + jnp.where(dvalid, dsh, 0.0) * cw[jj : jj + 1, :]
        du_ref[...] = du.astype(du_ref.dtype)
        for jj in range(CONV_K):
            dcw_ref[jj : jj + 1, :] = rows[jj]

    qspec = pl.BlockSpec((1, T, GDN_DIM), lambda j: (j // nh, 0, j % nh))
    return pl.pallas_call(
        body,
        out_shape=(jax.ShapeDtypeStruct((T, 3 * nh * GDN_DIM), MXU_DTYPE), jax.ShapeDtypeStruct((CONV_K, 3 * nh * GDN_DIM), f32)),
        grid=(3 * nh,),
        in_specs=[pl.BlockSpec((T, GDN_DIM), lambda j: (0, j)), pl.BlockSpec((CONV_K, GDN_DIM), lambda j: (0, j)), qspec, qspec],
        out_specs=(pl.BlockSpec((T, GDN_DIM), lambda j: (0, j)), pl.BlockSpec((CONV_K, GDN_DIM), lambda j: (0, j))),
        compiler_params=_cparams(("parallel",)),
        name="gdn_prep_bwd",
    )(proj, conv_w, dqkv_f, dqkv_b)


def _chunk_tri(n, rev):
    i, j = _iota2((n, n), 0), _iota2((n, n), 1)
    same = jnp.right_shift(i, 6) == jnp.right_shift(j, 6)
    order = (j >= i) if rev else (j <= i)
    return jnp.where(jnp.logical_and(same, order), 1.0, 0.0).astype(f32)


def _gdn_gates(proj, avec, dvec, tb=256):
    T = proj.shape[0]

    def body(s_ref, a_ref, d_ref, g_ref, gc_ref):
        small = s_ref[...]
        lane = _iota2(small.shape, 1)
        lg = -jnp.exp(a_ref[...]) * _softplus(small + d_ref[...])
        beta = _sigmoid(small)
        g_ref[...] = jnp.where(lane < 16, lg, jnp.where(lane < 32, beta, 0.0))
        lgm = jnp.where(lane < 16, lg, 0.0)
        gcf = _exact01(_chunk_tri(tb, False), lgm)
        gcr = _exact01(_chunk_tri(tb, True), lgm)
        gc_ref[...] = jnp.where(lane < 8, gcf, gcr)

    blk = pl.BlockSpec((tb, LANES), lambda i: (i, 0))
    row = pl.BlockSpec((1, LANES), lambda i: (0, 0))
    return pl.pallas_call(
        body,
        out_shape=(jax.ShapeDtypeStruct((T, LANES), f32), jax.ShapeDtypeStruct((T, LANES), f32)),
        grid=(T // tb,),
        in_specs=[pl.BlockSpec((tb, LANES), lambda i: (i, SMALL_OFF // LANES)), row, row],
        out_specs=(blk, blk),
        compiler_params=_cparams(("parallel",)),
        name="gdn_gates",
    )(proj, avec, dvec)


def _gdn_gates_bwd(proj, avec, dvec, dg_f, dg_b, dsmall_gla, tb=256):
    T = proj.shape[0]

    def body(s_ref, a_ref, d_ref, gf_ref, gb_ref, dl_ref, ds_ref, da_ref, dd_ref):
        i = pl.program_id(0)
        small = s_ref[...]
        lane = _iota2(small.shape, 1)
        dgate = gf_ref[...] + gb_ref[...]
        z = small + d_ref[...]
        nega = -jnp.exp(a_ref[...])
        dz = dgate * nega * _sigmoid(z)
        beta = _sigmoid(small)
        dsm = jnp.where(lane < 16, dz, jnp.where(lane < 32, dgate * beta * (1.0 - beta), 0.0))
        ds_ref[...] = (dsm + dl_ref[...]).astype(ds_ref.dtype)
        lg = nega * _softplus(z)
        pa = jnp.sum(jnp.where(lane < 16, dgate * lg, 0.0), axis=0, keepdims=True)
        pd = jnp.sum(jnp.where(lane < 16, dz, 0.0), axis=0, keepdims=True)

        @pl.when(i == 0)
        def _():
            da_ref[...] = pa
            dd_ref[...] = pd

        @pl.when(i > 0)
        def _():
            da_ref[...] += pa
            dd_ref[...] += pd

    blk = pl.BlockSpec((tb, LANES), lambda i: (i, 0))
    row = pl.BlockSpec((1, LANES), lambda i: (0, 0))
    return pl.pallas_call(
        body,
        out_shape=(jax.ShapeDtypeStruct((T, LANES), MXU_DTYPE), jax.ShapeDtypeStruct((1, LANES), f32), jax.ShapeDtypeStruct((1, LANES), f32)),
        grid=(T // tb,),
        in_specs=[pl.BlockSpec((tb, LANES), lambda i: (i, SMALL_OFF // LANES)), row, row, blk, blk, blk],
        out_specs=(blk, row, row),
        compiler_params=_cparams(("arbitrary",)),
        name="gdn_gates_bwd",
    )(proj, avec, dvec, dg_f, dg_b, dsmall_gla)


GDN_HB = 8


def _inv_unit_lower(L, eye):
    A = eye - L
    P2 = _split2(L)
    for _ in range(5):
        P2 = _split2(_h3s(P2, P2, "nn"))
        A = A + _h3s(_split2(A), P2, "nn")
    return A


def _lane_col(blk, idx):
    lane = _iota2(blk.shape, 1)
    return jnp.sum(jnp.where(lane == idx, blk, 0.0), axis=1, keepdims=True)


class _GdnChunk:
    def __init__(self, q, k, v, gcol, beta, rev, A=None):
        Cn = q.shape[0]
        ii, jj = _iota2((Cn, Cn), 0), _iota2((Cn, Cn), 1)
        self.incl = (ii <= jj) if rev else (ii >= jj)
        self.strict = (ii < jj) if rev else (ii > jj)
        grow = jnp.broadcast_to(gcol, (Cn, LANES)).T[0:1, :]
        diff = gcol - grow
        self.decay = jnp.where(self.incl, jnp.exp(jnp.where(self.incl, diff, 0.0)), 0.0)
        self.k, self.v, self.beta, self.gcol = k, v, beta, gcol
        self.qs = q * (GDN_DIM**-0.5)
        self.kb = k * beta
        self.L = jnp.where(self.strict, _mx(self.kb, k, "nt") * self.decay, 0.0)
        if A is None:
            A = _inv_unit_lower(self.L, jnp.where(ii == jj, 1.0, 0.0).astype(f32))
        self.A2 = _split2(A)
        self.A = A
        self.eg = jnp.exp(gcol)
        self.kbg = self.kb * self.eg
        self.sol = _h3s(self.A2, _split2(jnp.concatenate([v * beta, self.kbg], axis=1)), "nn")
        self.u = self.sol[:, :GDN_DIM]
        self.w = self.sol[:, GDN_DIM:]
        self.attn = jnp.where(self.incl, _mx(self.qs, k, "nt") * self.decay, 0.0)
        self.qd = self.qs * self.eg
        last = 0 if rev else Cn - 1
        self.last = last
        gl = gcol[last : last + 1, :]
        self.ekd = jnp.exp(gl - gcol)
        self.kd = k * self.ekd
        self.egl = jnp.exp(gl)


def _gdn_scan_fwd(qkvn, gates, gc):
    T = qkvn.shape[1]
    N = T // CHUNK
    nh, hb = GDN_HEADS, GDN_HB
    W = hb * GDN_DIM
    chains = [(d, hh) for d in range(2) for hh in range(hb)]

    def body(qf_ref, qb_ref, gf_ref, gb_ref, cf_ref, cb_ref, of_ref, ob_ref, sf_ref, sb_ref, af_ref, ab_ref, s_scr):
        n, b = pl.program_id(0), pl.program_id(1)

        @pl.when(n == 0)
        def _():
            for d, hh in chains:
                s_scr[d, b * hb + hh] = jnp.zeros((GDN_DIM, GDN_DIM), f32)

        dirs = ((qf_ref, gf_ref, cf_ref, of_ref, sf_ref, af_ref), (qb_ref, gb_ref, cb_ref, ob_ref, sb_ref, ab_ref))
        s_in = {ch: s_scr[ch[0], b * hb + ch[1]] for ch in chains}
        s_out = {}
        for d, hh in chains:
            q_ref, g_ref, c_ref, o_ref, st_ref, a_ref = dirs[d]
            h = b * hb + hh
            sl = slice(hh * GDN_DIM, (hh + 1) * GDN_DIM)
            gcol = _lane_col(c_ref[...], h + 8 * d)
            beta = _lane_col(g_ref[...], 16 + h + 8 * d)
            ck = _GdnChunk(q_ref[0, :, sl], q_ref[1, :, sl], q_ref[2, :, sl], gcol, beta, d == 1)
            S = s_in[d, hh]
            st_ref[0, hh] = S
            a_ref[0, hh] = ck.A
            v_new = ck.u - _mx(ck.w, S, "nn")
            o_ref[:, sl] = _mx(ck.qd, S, "nn") + _mx(ck.attn, v_new, "nn")
            s_out[d, hh] = S * ck.egl + _mx(ck.kd, v_new, "tn")
        for d, hh in chains:
            s_scr[d, b * hb + hh] = s_out[d, hh]

    fw, bw = (lambda n: n), (lambda n: N - 1 - n)
    qsp = lambda ci: pl.BlockSpec((3, CHUNK, W), lambda n, b: (0, ci(n), b))
    gsp = lambda ci: pl.BlockSpec((CHUNK, LANES), lambda n, b: (ci(n), 0))
    osp = lambda ci: pl.BlockSpec((CHUNK, W), lambda n, b: (ci(n), b))
    ssp = lambda ci: pl.BlockSpec((1, hb, GDN_DIM, GDN_DIM), lambda n, b: (ci(n), b, 0, 0))
    asp = lambda ci: pl.BlockSpec((1, hb, CHUNK, CHUNK), lambda n, b: (ci(n), b, 0, 0))
    o_shape = jax.ShapeDtypeStruct((T, nh * GDN_DIM), f32)
    s_shape = jax.ShapeDtypeStruct((N, nh, GDN_DIM, GDN_DIM), f32)
    a_shape = jax.ShapeDtypeStruct((N, nh, CHUNK, CHUNK), f32)
    return pl.pallas_call(
        body,
        out_shape=(o_shape, o_shape, s_shape, s_shape, a_shape, a_shape),
        grid=(N, nh // hb),
        in_specs=[qsp(fw), qsp(bw), gsp(fw), gsp(bw), gsp(fw), gsp(bw)],
        out_specs=(osp(fw), osp(bw), ssp(fw), ssp(bw), asp(fw), asp(bw)),
        scratch_shapes=[pltpu.VMEM((2, nh, GDN_DIM, GDN_DIM), f32)],
        compiler_params=_cparams(("arbitrary", "arbitrary")),
        name="gdn_scan_fwd",
    )(qkvn, qkvn, gates, gates, gc, gc)


def _gdn_scan_bwd(qkvn, gates, gc, st_f, st_b, a_f, a_b, do):
    T = qkvn.shape[1]
    N = T // CHUNK
    nh, hb = GDN_HEADS, GDN_HB
    nb = nh // hb
    W = hb * GDN_DIM
    chains = [(d, hh) for d in range(2) for hh in range(hb)]

    def body(qf_ref, qb_ref, gf_ref, gb_ref, cf_ref, cb_ref, sf_ref, sb_ref, af_ref, ab_ref, dof_ref, dob_ref,
             dqf_ref, dqb_ref, dgf_ref, dgb_ref, ds_scr):
        n, b = pl.program_id(0), pl.program_id(1)

        @pl.when(n == 0)
        def _():
            for d, hh in chains:
                ds_scr[d, b * hb + hh] = jnp.zeros((GDN_DIM, GDN_DIM), f32)

        @pl.when(b == 0)
        def _():
            dgf_ref[...] = jnp.zeros((CHUNK, LANES), f32)
            dgb_ref[...] = jnp.zeros((CHUNK, LANES), f32)

        dirs = ((qf_ref, gf_ref, cf_ref, sf_ref, af_ref, dof_ref, dqf_ref), (qb_ref, gb_ref, cb_ref, sb_ref, ab_ref, dob_ref, dqb_ref))
        ds_in = {ch: ds_scr[ch[0], b * hb + ch[1]] for ch in chains}
        ds_out = {}
        lane = _iota2((CHUNK, LANES), 1)
        acc = [dgf_ref[...], dgb_ref[...]]
        for d, hh in chains:
            q_ref, g_ref, c_ref, st_ref, a_ref, do_ref, dq_ref = dirs[d]
            h = b * hb + hh
            sl = slice(hh * GDN_DIM, (hh + 1) * GDN_DIM)
            gcol = _lane_col(c_ref[...], h + 8 * d)
            beta = _lane_col(g_ref[...], 16 + h + 8 * d)
            ck = _GdnChunk(q_ref[0, :, sl], q_ref[1, :, sl], q_ref[2, :, sl], gcol, beta, d == 1, A=a_ref[0, hh])
            k, v = ck.k, ck.v
            S = st_ref[0, hh]
            dS = ds_in[d, hh]
            dov = do_ref[:, sl]
            v_new = ck.u - _mx(ck.w, S, "nn")
            d_vnew = _mx(ck.attn, dov, "tn") + _mx(ck.kd, dS, "nn")
            d_attn = jnp.where(ck.incl, _mx(dov, v_new, "nt"), 0.0)
            d_qd = _mx(dov, S, "nt")
            d_kd = _mx(v_new, dS, "nt")
            d_gl = jnp.sum(jnp.sum(dS * S, axis=1, keepdims=True), axis=0, keepdims=True) * ck.egl
            ds_out[d, hh] = dS * ck.egl + _mx(ck.qd, dov, "tn") - _mx(ck.w, d_vnew, "tn")
            d_w = -_mx(d_vnew, S, "nt")
            d_rhs = _h3s(ck.A2, _split2(jnp.concatenate([d_vnew, d_w], axis=1)), "tn")
            d_vb, d_kbg = d_rhs[:, :GDN_DIM], d_rhs[:, GDN_DIM:]
            dL = -jnp.where(ck.strict, _h3(d_rhs, ck.sol, "nt"), 0.0)
            P = dL * ck.decay
            Q = d_attn * ck.decay
            d_kb = _mx(P, k, "nn") + d_kbg * ck.eg
            d_qs = _mx(Q, k, "nn") + d_qd * ck.eg
            d_k = _mx(P, ck.kb, "tn") + _mx(Q, ck.qs, "tn") + d_kd * ck.ekd + d_kb * ck.beta
            E = dL * ck.L + d_attn * ck.attn
            kdsum = jnp.sum(d_kd * ck.kd, axis=1, keepdims=True)
            d_gc = (jnp.sum(E, axis=1, keepdims=True) - _colsum_col(E)
                    + jnp.sum(d_kbg * ck.kbg, axis=1, keepdims=True)
                    + jnp.sum(d_qd * ck.qd, axis=1, keepdims=True) - kdsum)
            d_gl = d_gl + jnp.sum(kdsum, axis=0, keepdims=True)
            row = _iota2((CHUNK, 1), 0)
            d_gc = d_gc + jnp.where(row == ck.last, d_gl, 0.0)
            d_beta = jnp.sum(d_vb * v, axis=1, keepdims=True) + jnp.sum(d_kb * k, axis=1, keepdims=True)
            dq_ref[0, :, sl] = d_qs * (GDN_DIM**-0.5)
            dq_ref[1, :, sl] = d_k
            dq_ref[2, :, sl] = d_vb * ck.beta
            acc[d] = jnp.where(lane == h + 8 * d, d_gc, acc[d])
            acc[d] = jnp.where(lane == 16 + h + 8 * d, d_beta, acc[d])
        for d, hh in chains:
            ds_scr[d, b * hb + hh] = ds_out[d, hh]
        dgf_ref[...] = acc[0]
        dgb_ref[...] = acc[1]

        @pl.when(b == nb - 1)
        def _():
            for d, dg_ref in enumerate((dgf_ref, dgb_ref)):
                blk = dg_ref[...]
                dg_ref[...] = jnp.where(lane < 16, _exact01(_chunk_tri(CHUNK, d == 0), blk), blk)

    fw, bw = (lambda n: N - 1 - n), (lambda n: n)
    qsp = lambda ci: pl.BlockSpec((3, CHUNK, W), lambda n, b: (0, ci(n), b))
    gsp = lambda ci: pl.BlockSpec((CHUNK, LANES), lambda n, b: (ci(n), 0))
    osp = lambda ci: pl.BlockSpec((CHUNK, W), lambda n, b: (ci(n), b))
    ssp = lambda ci: pl.BlockSpec((1, hb, GDN_DIM, GDN_DIM), lambda n, b: (ci(n), b, 0, 0))
    asp = lambda ci: pl.BlockSpec((1, hb, CHUNK, CHUNK), lambda n, b: (ci(n), b, 0, 0))
    dq_shape = jax.ShapeDtypeStruct((3, T, nh * GDN_DIM), f32)
    dg_shape = jax.ShapeDtypeStruct((T, LANES), f32)
    return pl.pallas_call(
        body,
        out_shape=(dq_shape, dq_shape, dg_shape, dg_shape),
        grid=(N, nb),
        in_specs=[qsp(fw), qsp(bw), gsp(fw), gsp(bw), gsp(fw), gsp(bw), ssp(fw), ssp(bw), asp(fw), asp(bw), osp(fw), osp(bw)],
        out_specs=(qsp(fw), qsp(bw), gsp(fw), gsp(bw)),
        scratch_shapes=[pltpu.VMEM((2, nh, GDN_DIM, GDN_DIM), f32)],
        compiler_params=_cparams(("arbitrary", "arbitrary")),
        name="gdn_scan_bwd",
    )(qkvn, qkvn, gates, gates, gc, gc, st_f, st_b, a_f, a_b, do, do)


def _out_norm(o_f, o_b, proj, gate_blk, w, hd, name, tb=256):
    T, W = o_f.shape
    nh = W // hd

    def body(of_ref, ob_ref, z_ref, w_ref, y_ref):
        wv = w_ref[...]
        for i in range(nh):
            sl = slice(i * hd, (i + 1) * hd)
            o = of_ref[:, sl] + ob_ref[:, sl]
            r = lax.rsqrt(jnp.mean(o * o, axis=1, keepdims=True) + NORM_EPS)
            y_ref[:, sl] = (o * r * wv * _silu(z_ref[:, sl])).astype(y_ref.dtype)

    blk = pl.BlockSpec((tb, W), lambda i: (i, 0))
    return pl.pallas_call(
        body,
        out_shape=jax.ShapeDtypeStruct((T, W), MXU_DTYPE),
        grid=(T // tb,),
        in_specs=[blk, blk, pl.BlockSpec((tb, W), lambda i: (i, gate_blk)), pl.BlockSpec((1, hd), lambda i: (0, 0))],
        out_specs=blk,
        compiler_params=_cparams(("parallel",)),
        name=name,
    )(o_f, o_b, proj, w)


def _out_norm_bwd(dy, o_f, o_b, proj, gate_blk, w, hd, name, tb=256):
    T, W = o_f.shape
    nh = W // hd

    def body(dy_ref, of_ref, ob_ref, z_ref, w_ref, do_ref, dz_ref, dw_ref):
        i = pl.program_id(0)
        wv = w_ref[...]
        dw = jnp.zeros((1, hd), f32)
        for a in range(nh):
            sl = slice(a * hd, (a + 1) * hd)
            o = of_ref[:, sl] + ob_ref[:, sl]
            z = z_ref[:, sl]
            dyv = dy_ref[:, sl]
            r = lax.rsqrt(jnp.mean(o * o, axis=1, keepdims=True) + NORM_EPS)
            orr = o * r
            dn = dyv * _silu(z)
            dz_ref[:, sl] = (dyv * orr * wv * _silu_grad(z)).astype(dz_ref.dtype)
            dw = dw + jnp.sum(dn * orr, axis=0, keepdims=True)
            dnw = dn * wv
            do_ref[:, sl] = r * dnw - o * (r * r * r) * jnp.mean(dnw * o, axis=1, keepdims=True)

        @pl.when(i == 0)
        def _():
            dw_ref[...] = dw

        @pl.when(i > 0)
        def _():
            dw_ref[...] += dw

    blk = pl.BlockSpec((tb, W), lambda i: (i, 0))
    row = pl.BlockSpec((1, hd), lambda i: (0, 0))
    return pl.pallas_call(
        body,
        out_shape=(jax.ShapeDtypeStruct((T, W), f32), jax.ShapeDtypeStruct((T, W), MXU_DTYPE), jax.ShapeDtypeStruct((1, hd), f32)),
        grid=(T // tb,),
        in_specs=[blk, blk, blk, pl.BlockSpec((tb, W), lambda i: (i, gate_blk)), row],
        out_specs=(blk, blk, row),
        compiler_params=_cparams(("arbitrary",)),
        name=name,
    )(dy, o_f, o_b, proj, w)


def _gla_prep(proj, w2f, w2b, b2f, b2b, tb=256):
    T = proj.shape[0]
    W = GLA_HEADS * GLA_DK

    def body(s_ref, wf_ref, wb_ref, bf_ref, bb_ref, gf_ref, gb_ref):
        small = s_ref[...]
        gkf = -_softplus(-(_mx(small, wf_ref[...], "nn") + bf_ref[...])) * (1.0 / GATE_NORMALIZER)
        gkb = -_softplus(-(_mx(small, wb_ref[...], "nn") + bb_ref[...])) * (1.0 / GATE_NORMALIZER)
        gf_ref[...] = _exact01(_chunk_tri(tb, False), gkf)
        gb_ref[...] = _exact01(_chunk_tri(tb, True), gkb)

    blk = pl.BlockSpec((tb, W), lambda i: (i, 0))
    wsp = pl.BlockSpec((LANES, W), lambda i: (0, 0))
    row = pl.BlockSpec((1, W), lambda i: (0, 0))
    return pl.pallas_call(
        body,
        out_shape=(jax.ShapeDtypeStruct((T, W), f32), jax.ShapeDtypeStruct((T, W), f32)),
        grid=(T // tb,),
        in_specs=[pl.BlockSpec((tb, LANES), lambda i: (i, SMALL_OFF // LANES)), wsp, wsp, row, row],
        out_specs=(blk, blk),
        compiler_params=_cparams(("parallel",)),
        name="gla_prep",
    )(proj, w2f, w2b, b2f, b2b)


def _gla_prep_bwd(proj, w2f, w2b, b2f, b2b, dgk_f, dgk_b, tb=256):
    T = proj.shape[0]
    W = GLA_HEADS * GLA_DK

    def body(s_ref, wf_ref, wb_ref, bf_ref, bb_ref, df_ref, db_ref, ds_ref, dwf_ref, dwb_ref, dbf_ref, dbb_ref):
        i = pl.program_id(0)
        small = s_ref[...]
        dsm = jnp.zeros((tb, LANES), f32)
        parts = []
        for w_ref, b_ref, d_ref in ((wf_ref, bf_ref, df_ref), (wb_ref, bb_ref, db_ref)):
            pre = _mx(small, w_ref[...], "nn") + b_ref[...]
            dpre = d_ref[...] * (1.0 / GATE_NORMALIZER) * _sigmoid(-pre)
            dsm = dsm + _mx(dpre, w_ref[...], "nt")
            parts.append((_mx(small, dpre, "tn"), jnp.sum(dpre, axis=0, keepdims=True)))
        ds_ref[...] = dsm

        @pl.when(i == 0)
        def _():
            dwf_ref[...], dbf_ref[...] = parts[0]
            dwb_ref[...], dbb_ref[...] = parts[1]

        @pl.when(i > 0)
        def _():
            dwf_ref[...] += parts[0][0]
            dbf_ref[...] += parts[0][1]
            dwb_ref[...] += parts[1][0]
            dbb_ref[...] += parts[1][1]

    blk = pl.BlockSpec((tb, W), lambda i: (i, 0))
    wsp = pl.BlockSpec((LANES, W), lambda i: (0, 0))
    row = pl.BlockSpec((1, W), lambda i: (0, 0))
    return pl.pallas_call(
        body,
        out_shape=(jax.ShapeDtypeStruct((T, LANES), f32), jax.ShapeDtypeStruct((LANES, W), f32), jax.ShapeDtypeStruct((LANES, W), f32),
                   jax.ShapeDtypeStruct((1, W), f32), jax.ShapeDtypeStruct((1, W), f32)),
        grid=(T // tb,),
        in_specs=[pl.BlockSpec((tb, LANES), lambda i: (i, SMALL_OFF // LANES)), wsp, wsp, row, row, blk, blk],
        out_specs=(pl.BlockSpec((tb, LANES), lambda i: (i, 0)), wsp, wsp, row, row),
        compiler_params=_cparams(("arbitrary",)),
        name="gla_prep_bwd",
    )(proj, w2f, w2b, b2f, b2b, dgk_f, dgk_b)


Q_B_BLK, K_B_BLK, V_B_BLK = 4096 // GLA_DK, 4608 // GLA_DK, 5120 // GLA_DV


class _GlaChunk:
    def __init__(self, q, k, G, rev):
        Cn = q.shape[0]
        ii, jj = _iota2((Cn, Cn), 0), _iota2((Cn, Cn), 1)
        self.incl = (ii <= jj) if rev else (ii >= jj)
        self.eG = jnp.exp(G)
        self.enG = jnp.exp(-G)
        self.qg = q * (GLA_DK**-0.5) * self.eG
        self.kg = k * self.enG
        self.attn = jnp.where(self.incl, _mx(self.qg, self.kg, "nt"), 0.0)
        last = 0 if rev else Cn - 1
        self.last = last
        gl = G[last : last + 1, :]
        self.ekd = jnp.exp(gl - G)
        self.kd = k * self.ekd
        self.egl = jnp.exp(gl)


def _gla_scan_fwd(proj, G_f, G_b):
    T = proj.shape[0]
    N = T // CHUNK
    nh = GLA_HEADS

    def body(qf_ref, kf_ref, vf_ref, gf_ref, qb_ref, kb_ref, vb_ref, gb_ref, of_ref, ob_ref, sf_ref, sb_ref, s_scr):
        n, h = pl.program_id(0), pl.program_id(1)

        @pl.when(n == 0)
        def _():
            s_scr[0, h] = jnp.zeros((GLA_DV, GLA_DK), f32)
            s_scr[1, h] = jnp.zeros((GLA_DV, GLA_DK), f32)

        for d, (q_ref, k_ref, v_ref, g_ref, o_ref, st_ref) in enumerate(
            ((qf_ref, kf_ref, vf_ref, gf_ref, of_ref, sf_ref), (qb_ref, kb_ref, vb_ref, gb_ref, ob_ref, sb_ref))
        ):
            ck = _GlaChunk(q_ref[...], k_ref[...], g_ref[...], d == 1)
            v = v_ref[...]
            St = s_scr[d, h]
            st_ref[0, 0] = St
            o_ref[...] = _mx(ck.qg, St, "nt") + _mx(ck.attn, v, "nn")
            s_scr[d, h] = St * ck.egl + _mx(v, ck.kd, "tn")

    def specs(cidx):
        return [
            pl.BlockSpec((CHUNK, GLA_DK), lambda n, h: (cidx(n), Q_B_BLK + h)),
            pl.BlockSpec((CHUNK, GLA_DK), lambda n, h: (cidx(n), K_B_BLK + h)),
            pl.BlockSpec((CHUNK, GLA_DV), lambda n, h: (cidx(n), V_B_BLK + h)),
            pl.BlockSpec((CHUNK, GLA_DK), lambda n, h: (cidx(n), h)),
        ]

    fw, bw = (lambda n: n), (lambda n: N - 1 - n)
    o_shape = jax.ShapeDtypeStruct((T, nh * GLA_DV), f32)
    s_shape = jax.ShapeDtypeStruct((N, nh, GLA_DV, GLA_DK), f32)
    return pl.pallas_call(
        body,
        out_shape=(o_shape, o_shape, s_shape, s_shape),
        grid=(N, nh),
        in_specs=specs(fw) + specs(bw),
        out_specs=(
            pl.BlockSpec((CHUNK, GLA_DV), lambda n, h: (n, h)),
            pl.BlockSpec((CHUNK, GLA_DV), lambda n, h: (N - 1 - n, h)),
            pl.BlockSpec((1, 1, GLA_DV, GLA_DK), lambda n, h: (n, h, 0, 0)),
            pl.BlockSpec((1, 1, GLA_DV, GLA_DK), lambda n, h: (N - 1 - n, h, 0, 0)),
        ),
        scratch_shapes=[pltpu.VMEM((2, nh, GLA_DV, GLA_DK), f32)],
        compiler_params=_cparams(("arbitrary", "arbitrary")),
        name="gla_scan_fwd",
    )(proj, proj, proj, G_f, proj, proj, proj, G_b)


def _gla_scan_bwd(proj, G_f, G_b, st_f, st_b, do):
    T = proj.shape[0]
    N = T // CHUNK
    nh = GLA_HEADS

    def body(qf_ref, kf_ref, vf_ref, gf_ref, sf_ref, dof_ref, qb_ref, kb_ref, vb_ref, gb_ref, sb_ref, dob_ref,
             dqf_ref, dkf_ref, dvf_ref, dgf_ref, dqb_ref, dkb_ref, dvb_ref, dgb_ref, ds_scr):
        n, h = pl.program_id(0), pl.program_id(1)

        @pl.when(n == 0)
        def _():
            ds_scr[0, h] = jnp.zeros((GLA_DV, GLA_DK), f32)
            ds_scr[1, h] = jnp.zeros((GLA_DV, GLA_DK), f32)

        for d, (q_ref, k_ref, v_ref, g_ref, st_ref, do_ref, dq_ref, dk_ref, dv_ref, dg_ref) in enumerate(
            ((qf_ref, kf_ref, vf_ref, gf_ref, sf_ref, dof_ref, dqf_ref, dkf_ref, dvf_ref, dgf_ref),
             (qb_ref, kb_ref, vb_ref, gb_ref, sb_ref, dob_ref, dqb_ref, dkb_ref, dvb_ref, dgb_ref))
        ):
            rev = d == 1
            ck = _GlaChunk(q_ref[...], k_ref[...], g_ref[...], rev)
            v = v_ref[...]
            St = st_ref[0, 0]
            dSt = ds_scr[d, h]
            dov = do_ref[...]
            d_attn = jnp.where(ck.incl, _mx(dov, v, "nt"), 0.0)
            d_qg = _mx(dov, St, "nn") + _mx(d_attn, ck.kg, "nn")
            d_kg = _mx(d_attn, ck.qg, "tn")
            dv_ref[...] = _mx(ck.attn, dov, "tn") + _mx(ck.kd, dSt, "nt")
            d_kd = _mx(v, dSt, "nn")
            ds_scr[d, h] = dSt * ck.egl + _mx(dov, ck.qg, "tn")
            kdd = d_kd * ck.kd
            d_gl = jnp.sum(dSt * St, axis=0, keepdims=True) * ck.egl + jnp.sum(kdd, axis=0, keepdims=True)
            dq_ref[...] = d_qg * ck.eG * (GLA_DK**-0.5)
            dk_ref[...] = d_kg * ck.enG + d_kd * ck.ekd
            d_G = d_qg * ck.qg - d_kg * ck.kg - kdd
            row = _iota2((CHUNK, GLA_DK), 0)
            d_G = d_G + jnp.where(row == ck.last, d_gl, 0.0)
            dg_ref[...] = _exact01(_chunk_tri(CHUNK, not rev), d_G)

    def in_specs(cidx):
        return [
            pl.BlockSpec((CHUNK, GLA_DK), lambda n, h: (cidx(n), Q_B_BLK + h)),
            pl.BlockSpec((CHUNK, GLA_DK), lambda n, h: (cidx(n), K_B_BLK + h)),
            pl.BlockSpec((CHUNK, GLA_DV), lambda n, h: (cidx(n), V_B_BLK + h)),
            pl.BlockSpec((CHUNK, GLA_DK), lambda n, h: (cidx(n), h)),
            pl.BlockSpec((1, 1, GLA_DV, GLA_DK), lambda n, h: (cidx(n), h, 0, 0)),
            pl.BlockSpec((CHUNK, GLA_DV), lambda n, h: (cidx(n), h)),
        ]

    def out_specs(cidx):
        return [
            pl.BlockSpec((CHUNK, GLA_DK), lambda n, h: (cidx(n), h)),
            pl.BlockSpec((CHUNK, GLA_DK), lambda n, h: (cidx(n), h)),
            pl.BlockSpec((CHUNK, GLA_DV), lambda n, h: (cidx(n), h)),
            pl.BlockSpec((CHUNK, GLA_DK), lambda n, h: (cidx(n), h)),
        ]

    fw, bw = (lambda n: N - 1 - n), (lambda n: n)
    k_shape = jax.ShapeDtypeStruct((T, nh * GLA_DK), f32)
    v_shape = jax.ShapeDtypeStruct((T, nh * GLA_DV), f32)
    return pl.pallas_call(
        body,
        out_shape=(k_shape, k_shape, v_shape, k_shape) * 2,
        grid=(N, nh),
        in_specs=in_specs(fw) + in_specs(bw),
        out_specs=tuple(out_specs(fw) + out_specs(bw)),
        scratch_shapes=[pltpu.VMEM((2, nh, GLA_DV, GLA_DK), f32)],
        compiler_params=_cparams(("arbitrary", "arbitrary")),
        name="gla_scan_bwd",
    )(proj, proj, proj, G_f, st_f, do, proj, proj, proj, G_b, st_b, do)


GATE_A_BLK, GATE_B_BLK, Z_BLK, G_B_BLK = 7168 // 1024, 8192 // 1024, 3072 // 1024, 6144 // 1024


def _mid(og, obg, proj, x, target, wpg, wpl, wout, w_post, tb=256):
    T = x.shape[0]
    Dm = D_MODEL

    def body(og_ref, obg_ref, ga_ref, gb_ref, x_ref, t_ref, wpg_ref, wpl_ref, wo_ref, wp_ref,
             loss_ref, dwp_ref, dy_ref, m_ref, dout_ref, dya_ref, dyb_ref, dga_ref, dgb_ref, dog_ref, dobg_ref):
        i = pl.program_id(0)
        ya = _mx(og_ref[...], wpg_ref[...], "nn")
        yb = _mx(obg_ref[...], wpl_ref[...], "nn")
        sa, sb = _sigmoid(ga_ref[...]), _sigmoid(gb_ref[...])
        m = sa * ya + sb * yb
        m_ref[...] = m.astype(m_ref.dtype)
        out = _mx(m, wo_ref[...], "nn")
        r = lax.rsqrt(jnp.mean(out * out, axis=1, keepdims=True) + NORM_EPS)
        wp = wp_ref[...]
        e = x_ref[...] + out * r * wp - t_ref[...]
        dy = e * (1.0 / Dm)
        dy_ref[...] = dy
        lpart = jnp.sum(e * e, axis=0, keepdims=True)
        wpart = jnp.sum(dy * out * r, axis=0, keepdims=True)
        dyn = dy * wp
        dout = r * dyn - out * (r * r * r) * jnp.mean(dyn * out, axis=1, keepdims=True)
        dout_ref[...] = dout.astype(dout_ref.dtype)
        dm = _mx(dout, wo_ref[...], "nt")
        dya, dyb = dm * sa, dm * sb
        dya_ref[...] = dya.astype(dya_ref.dtype)
        dyb_ref[...] = dyb.astype(dyb_ref.dtype)
        dga_ref[...] = (dm * ya * sa * (1.0 - sa)).astype(dga_ref.dtype)
        dgb_ref[...] = (dm * yb * sb * (1.0 - sb)).astype(dgb_ref.dtype)
        dog_ref[...] = _mx(dya, wpg_ref[...], "nt")
        dobg_ref[...] = _mx(dyb, wpl_ref[...], "nt")

        @pl.when(i == 0)
        def _():
            loss_ref[...] = lpart
            dwp_ref[...] = wpart

        @pl.when(i > 0)
        def _():
            loss_ref[...] += lpart
            dwp_ref[...] += wpart

    blk = pl.BlockSpec((tb, Dm), lambda i: (i, 0))
    row = pl.BlockSpec((1, Dm), lambda i: (0, 0))
    wsp = pl.BlockSpec((Dm, Dm), lambda i: (0, 0))
    act = jax.ShapeDtypeStruct((T, Dm), MXU_DTYPE)
    big = jax.ShapeDtypeStruct((T, Dm), f32)
    vec = jax.ShapeDtypeStruct((1, Dm), f32)
    return pl.pallas_call(
        body,
        out_shape=(vec, vec, big, act, act, act, act, act, act, big, big),
        grid=(T // tb,),
        in_specs=[blk, blk, pl.BlockSpec((tb, Dm), lambda i: (i, GATE_A_BLK)), pl.BlockSpec((tb, Dm), lambda i: (i, GATE_B_BLK)),
                  blk, blk, wsp, wsp, wsp, row],
        out_specs=(row, row) + (blk,) * 9,
        compiler_params=_cparams(("arbitrary",)),
        name="mid",
    )(og, obg, proj, proj, x, target, wpg, wpl, wout, w_post)


def _heads(ref, lead, n, width):
    return [ref[lead + (slice(None), slice(i * width, (i + 1) * width))] for i in range(n)]


class _GdnBatch:
    def __init__(self, q, k, v, gcol, grow, beta, nf, A=None):
        B, Cn = q.shape[0], q.shape[1]
        self.nf = nf
        self.incl, self.strict, eye = _dir_masks(B, nf, Cn)
        self.decay = jnp.where(self.incl, jnp.exp(jnp.where(self.incl, gcol - grow, 0.0)), 0.0)
        self.k, self.v, self.beta = k, v, beta
        self.qs = q * (GDN_DIM**-0.5)
        self.kb = k * beta
        self.L = jnp.where(self.strict, _mx(self.kb, k, "nt") * self.decay, 0.0)
        if A is None:
            A = _inv_unit_lower(self.L, jnp.where(eye, 1.0, 0.0).astype(f32))
        self.A = A
        self.A2 = _split2(A)
        self.eg = jnp.exp(gcol)
        self.kbg = self.kb * self.eg
        self.sol = _h3s(self.A2, _split2(jnp.concatenate([v * beta, self.kbg], axis=2)), "nn")
        self.u = self.sol[:, :, :GDN_DIM]
        self.w = self.sol[:, :, GDN_DIM:]
        self.attn = jnp.where(self.incl, _mx(self.qs, k, "nt") * self.decay, 0.0)
        self.qd = self.qs * self.eg
        gl = _last_row(gcol, nf)
        self.ekd = jnp.exp(gl - gcol)
        self.kd = k * self.ekd
        self.egl = jnp.exp(gl)


def _gdn_chain_inputs(qf_ref, qb_ref, gf_ref, gb_ref, cf_ref, cb_ref, b, hb):
    qkv = [jnp.stack(_heads(qf_ref, (i,), hb, GDN_DIM) + _heads(qb_ref, (i,), hb, GDN_DIM)) for i in range(3)]
    cols, rows, betas = [], [], []
    for d, (g_ref, c_ref) in enumerate(((gf_ref, cf_ref), (gb_ref, cb_ref))):
        gates, gc = g_ref[...], c_ref[...]
        for hh in range(hb):
            h = b * hb + hh
            col = _lane_col(gc, h + 8 * d)
            cols.append(col)
            rows.append(jnp.broadcast_to(col, (CHUNK, LANES)).T[0:1, :])
            betas.append(_lane_col(gates, 16 + h + 8 * d))
    return qkv[0], qkv[1], qkv[2], jnp.stack(cols), jnp.stack(rows), jnp.stack(betas)


def _gdn_fwd(qkvn, gates, gc):
    T = qkvn.shape[1]
    N = T // CHUNK
    nh, hb = GDN_HEADS, GDN_HB
    W = hb * GDN_DIM

    def body(qf_ref, qb_ref, gf_ref, gb_ref, cf_ref, cb_ref, of_ref, ob_ref, sf_ref, sb_ref, af_ref, ab_ref, s_scr):
        n, b = pl.program_id(0), pl.program_id(1)
        hs = pl.ds(b * hb, hb)

        @pl.when(n == 0)
        def _():
            s_scr[0, hs] = jnp.zeros((hb, GDN_DIM, GDN_DIM), f32)
            s_scr[1, hs] = jnp.zeros((hb, GDN_DIM, GDN_DIM), f32)

        q, k, v, gcol, grow, beta = _gdn_chain_inputs(qf_ref, qb_ref, gf_ref, gb_ref, cf_ref, cb_ref, b, hb)
        ck = _GdnBatch(q, k, v, gcol, grow, beta, hb)
        S = jnp.concatenate([s_scr[0, hs], s_scr[1, hs]], axis=0)
        sf_ref[0], sb_ref[0] = S[:hb], S[hb:]
        af_ref[0], ab_ref[0] = ck.A[:hb], ck.A[hb:]
        v_new = ck.u - _mx(ck.w, S, "nn")
        o = _mx(ck.qd, S, "nn") + _mx(ck.attn, v_new, "nn")
        S_new = S * ck.egl + _mx(ck.kd, v_new, "tn")
        for hh in range(hb):
            sl = slice(hh * GDN_DIM, (hh + 1) * GDN_DIM)
            of_ref[:, sl] = o[hh]
            ob_ref[:, sl] = o[hb + hh]
        s_scr[0, hs] = S_new[:hb]
        s_scr[1, hs] = S_new[hb:]

    fw, bw = (lambda n: n), (lambda n: N - 1 - n)
    qsp = lambda ci: pl.BlockSpec((3, CHUNK, W), lambda n, b: (0, ci(n), b))
    gsp = lambda ci: pl.BlockSpec((CHUNK, LANES), lambda n, b: (ci(n), 0))
    osp = lambda ci: pl.BlockSpec((CHUNK, W), lambda n, b: (ci(n), b))
    ssp = lambda ci: pl.BlockSpec((1, hb, GDN_DIM, GDN_DIM), lambda n, b: (ci(n), b, 0, 0))
    asp = lambda ci: pl.BlockSpec((1, hb, CHUNK, CHUNK), lambda n, b: (ci(n), b, 0, 0))
    o_shape = jax.ShapeDtypeStruct((T, nh * GDN_DIM), f32)
    s_shape = jax.ShapeDtypeStruct((N, nh, GDN_DIM, GDN_DIM), f32)
    a_shape = jax.ShapeDtypeStruct((N, nh, CHUNK, CHUNK), f32)
    return pl.pallas_call(
        body,
        out_shape=(o_shape, o_shape, s_shape, s_shape, a_shape, a_shape),
        grid=(N, nh // hb),
        in_specs=[qsp(fw), qsp(bw), gsp(fw), gsp(bw), gsp(fw), gsp(bw)],
        out_specs=(osp(fw), osp(bw), ssp(fw), ssp(bw), asp(fw), asp(bw)),
        scratch_shapes=[pltpu.VMEM((2, nh, GDN_DIM, GDN_DIM), f32)],
        compiler_params=_cparams(("arbitrary", "arbitrary")),
        name="gdn_scan_fwd",
    )(qkvn, qkvn, gates, gates, gc, gc)


def _gdn_bwd(qkvn, gates, gc, st_f, st_b, a_f, a_b, do):
    T = qkvn.shape[1]
    N = T // CHUNK
    nh, hb = GDN_HEADS, GDN_HB
    nb = nh // hb
    W = hb * GDN_DIM

    def body(qf_ref, qb_ref, gf_ref, gb_ref, cf_ref, cb_ref, sf_ref, sb_ref, af_ref, ab_ref, dof_ref, dob_ref,
             dqf_ref, dqb_ref, dgf_ref, dgb_ref, ds_scr):
        n, b = pl.program_id(0), pl.program_id(1)
        hs = pl.ds(b * hb, hb)

        @pl.when(n == 0)
        def _():
            ds_scr[0, hs] = jnp.zeros((hb, GDN_DIM, GDN_DIM), f32)
            ds_scr[1, hs] = jnp.zeros((hb, GDN_DIM, GDN_DIM), f32)

        @pl.when(b == 0)
        def _():
            dgf_ref[...] = jnp.zeros((CHUNK, LANES), f32)
            dgb_ref[...] = jnp.zeros((CHUNK, LANES), f32)

        q, k, v, gcol, grow, beta = _gdn_chain_inputs(qf_ref, qb_ref, gf_ref, gb_ref, cf_ref, cb_ref, b, hb)
        ck = _GdnBatch(q, k, v, gcol, grow, beta, hb, A=jnp.concatenate([af_ref[0], ab_ref[0]], axis=0))
        S = jnp.concatenate([sf_ref[0], sb_ref[0]], axis=0)
        dS = jnp.concatenate([ds_scr[0, hs], ds_scr[1, hs]], axis=0)
        dov = jnp.stack(_heads(dof_ref, (), hb, GDN_DIM) + _heads(dob_ref, (), hb, GDN_DIM))
        v_new = ck.u - _mx(ck.w, S, "nn")
        d_vnew = _mx(ck.attn, dov, "tn") + _mx(ck.kd, dS, "nn")
        d_attn = jnp.where(ck.incl, _mx(dov, v_new, "nt"), 0.0)
        d_qd = _mx(dov, S, "nt")
        d_kd = _mx(v_new, dS, "nt")
        d_gl = jnp.sum(jnp.sum(dS * S, axis=2, keepdims=True), axis=1, keepdims=True) * ck.egl
        dS_new = dS * ck.egl + _mx(ck.qd, dov, "tn") - _mx(ck.w, d_vnew, "tn")
        ds_scr[0, hs] = dS_new[:hb]
        ds_scr[1, hs] = dS_new[hb:]
        d_w = -_mx(d_vnew, S, "nt")
        d_rhs = _h3s(ck.A2, _split2(jnp.concatenate([d_vnew, d_w], axis=2)), "tn")
        d_vb, d_kbg = d_rhs[:, :, :GDN_DIM], d_rhs[:, :, GDN_DIM:]
        dL = -jnp.where(ck.strict, _h3(d_rhs, ck.sol, "nt"), 0.0)
        P = dL * ck.decay
        Q = d_attn * ck.decay
        d_kb = _mx(P, k, "nn") + d_kbg * ck.eg
        d_qs = _mx(Q, k, "nn") + d_qd * ck.eg
        d_k = _mx(P, ck.kb, "tn") + _mx(Q, ck.qs, "tn") + d_kd * ck.ekd + d_kb * ck.beta
        E = dL * ck.L + d_attn * ck.attn
        kdsum = jnp.sum(d_kd * ck.kd, axis=2, keepdims=True)
        d_gc = (jnp.sum(E, axis=2, keepdims=True) - _colsum_col(E)
                + jnp.sum(d_kbg * ck.kbg, axis=2, keepdims=True)
                + jnp.sum(d_qd * ck.qd, axis=2, keepdims=True) - kdsum)
        d_gc = _add_at_last_row(d_gc, d_gl + jnp.sum(kdsum, axis=1, keepdims=True), hb)
        d_beta = jnp.sum(d_vb * v, axis=2, keepdims=True) + jnp.sum(d_kb * k, axis=2, keepdims=True)
        d_q = d_qs * (GDN_DIM**-0.5)
        d_v = d_vb * ck.beta
        lane = _iota2((CHUNK, LANES), 1)
        for d, (dq_ref, dg_ref) in enumerate(((dqf_ref, dgf_ref), (dqb_ref, dgb_ref))):
            acc = dg_ref[...]
            for hh in range(hb):
                c, h = d * hb + hh, b * hb + hh
                sl = slice(hh * GDN_DIM, (hh + 1) * GDN_DIM)
                dq_ref[0, :, sl] = d_q[c]
                dq_ref[1, :, sl] = d_k[c]
                dq_ref[2, :, sl] = d_v[c]
                acc = jnp.where(lane == h + 8 * d, d_gc[c], acc)
                acc = jnp.where(lane == 16 + h + 8 * d, d_beta[c], acc)
            dg_ref[...] = acc

        @pl.when(b == nb - 1)
        def _():
            for d, dg_ref in enumerate((dgf_ref, dgb_ref)):
                blk = dg_ref[...]
                dg_ref[...] = jnp.where(lane < 16, _exact01(_chunk_tri(CHUNK, d == 0), blk), blk)

    fw, bw = (lambda n: N - 1 - n), (lambda n: n)
    qsp = lambda ci: pl.BlockSpec((3, CHUNK, W), lambda n, b: (0, ci(n), b))
    gsp = lambda ci: pl.BlockSpec((CHUNK, LANES), lambda n, b: (ci(n), 0))
    osp = lambda ci: pl.BlockSpec((CHUNK, W), lambda n, b: (ci(n), b))
    ssp = lambda ci: pl.BlockSpec((1, hb, GDN_DIM, GDN_DIM), lambda n, b: (ci(n), b, 0, 0))
    asp = lambda ci: pl.BlockSpec((1, hb, CHUNK, CHUNK), lambda n, b: (ci(n), b, 0, 0))
    dq_shape = jax.ShapeDtypeStruct((3, T, nh * GDN_DIM), f32)
    dg_shape = jax.ShapeDtypeStruct((T, LANES), f32)
    return pl.pallas_call(
        body,
        out_shape=(dq_shape, dq_shape, dg_shape, dg_shape),
        grid=(N, nb),
        in_specs=[qsp(fw), qsp(bw), gsp(fw), gsp(bw), gsp(fw), gsp(bw), ssp(fw), ssp(bw), asp(fw), asp(bw), osp(fw), osp(bw)],
        out_specs=(qsp(fw), qsp(bw), gsp(fw), gsp(bw)),
        scratch_shapes=[pltpu.VMEM((2, nh, GDN_DIM, GDN_DIM), f32)],
        compiler_params=_cparams(("arbitrary", "arbitrary")),
        name="gdn_scan_bwd",
    )(qkvn, qkvn, gates, gates, gc, gc, st_f, st_b, a_f, a_b, do, do)


class _GlaBatch:
    def __init__(self, q, k, G, nf):
        B, Cn = q.shape[0], q.shape[1]
        self.incl, _, _ = _dir_masks(B, nf, Cn)
        self.eG = jnp.exp(G)
        self.enG = jnp.exp(-G)
        self.qg = q * (GLA_DK**-0.5) * self.eG
        self.kg = k * self.enG
        self.attn = jnp.where(self.incl, _mx(self.qg, self.kg, "nt"), 0.0)
        gl = _last_row(G, nf)
        self.ekd = jnp.exp(gl - G)
        self.kd = k * self.ekd
        self.egl = jnp.exp(gl)


def _gla_chain_inputs(refs_f, refs_b):
    nh = GLA_HEADS
    out = []
    for i, width in enumerate((GLA_DK, GLA_DK, GLA_DV, GLA_DK)):
        out.append(jnp.stack(_heads(refs_f[i], (), nh, width) + _heads(refs_b[i], (), nh, width)))
    return out


def _gla_specs(N, ci):
    nh = GLA_HEADS
    return [
        pl.BlockSpec((CHUNK, nh * GLA_DK), lambda n: (ci(n), 4096 // (nh * GLA_DK))),
        pl.BlockSpec((CHUNK, nh * GLA_DK), lambda n: (ci(n), 4608 // (nh * GLA_DK))),
        pl.BlockSpec((CHUNK, nh * GLA_DV), lambda n: (ci(n), 5120 // (nh * GLA_DV))),
        pl.BlockSpec((CHUNK, nh * GLA_DK), lambda n: (ci(n), 0)),
    ]


def _gla_fwd(proj, G_f, G_b):
    T = proj.shape[0]
    N = T // CHUNK
    nh = GLA_HEADS

    def body(qf, kf, vf, gf, qb, kb, vb, gb, of_ref, ob_ref, sf_ref, sb_ref, s_scr):
        n = pl.program_id(0)

        @pl.when(n == 0)
        def _():
            s_scr[...] = jnp.zeros((2 * nh, GLA_DV, GLA_DK), f32)

        q, k, v, G = _gla_chain_inputs((qf, kf, vf, gf), (qb, kb, vb, gb))
        ck = _GlaBatch(q, k, G, nh)
        St = s_scr[...]
        sf_ref[0], sb_ref[0] = St[:nh], St[nh:]
        o = _mx(ck.qg, St, "nt") + _mx(ck.attn, v, "nn")
        for i in range(nh):
            sl = slice(i * GLA_DV, (i + 1) * GLA_DV)
            of_ref[:, sl] = o[i]
            ob_ref[:, sl] = o[nh + i]
        s_scr[...] = St * ck.egl + _mx(v, ck.kd, "tn")

    fw, bw = (lambda n: n), (lambda n: N - 1 - n)
    osp = lambda ci: pl.BlockSpec((CHUNK, nh * GLA_DV), lambda n: (ci(n), 0))
    ssp = lambda ci: pl.BlockSpec((1, nh, GLA_DV, GLA_DK), lambda n: (ci(n), 0, 0, 0))
    o_shape = jax.ShapeDtypeStruct((T, nh * GLA_DV), f32)
    s_shape = jax.ShapeDtypeStruct((N, nh, GLA_DV, GLA_DK), f32)
    return pl.pallas_call(
        body,
        out_shape=(o_shape, o_shape, s_shape, s_shape),
        grid=(N,),
        in_specs=_gla_specs(N, fw) + _gla_specs(N, bw),
        out_specs=(osp(fw), osp(bw), ssp(fw), ssp(bw)),
        scratch_shapes=[pltpu.VMEM((2 * nh, GLA_DV, GLA_DK), f32)],
        compiler_params=_cparams(("arbitrary",)),
        name="gla_scan_fwd",
    )(proj, proj, proj, G_f, proj, proj, proj, G_b)


def _gla_bwd(proj, G_f, G_b, st_f, st_b, do):
    T = proj.shape[0]
    N = T // CHUNK
    nh = GLA_HEADS

    def body(qf, kf, vf, gf, sf_ref, dof_ref, qb, kb, vb, gb, sb_ref, dob_ref,
             dqf_ref, dkf_ref, dvf_ref, dgf_ref, dqb_ref, dkb_ref, dvb_ref, dgb_ref, ds_scr):
        n = pl.program_id(0)

        @pl.when(n == 0)
        def _():
            ds_scr[...] = jnp.zeros((2 * nh, GLA_DV, GLA_DK), f32)

        q, k, v, G = _gla_chain_inputs((qf, kf, vf, gf), (qb, kb, vb, gb))
        ck = _GlaBatch(q, k, G, nh)
        St = jnp.concatenate([sf_ref[0], sb_ref[0]], axis=0)
        dSt = ds_scr[...]
        dov = jnp.stack(_heads(dof_ref, (), nh, GLA_DV) + _heads(dob_ref, (), nh, GLA_DV))
        d_attn = jnp.where(ck.incl, _mx(dov, v, "nt"), 0.0)
        d_qg = _mx(dov, St, "nn") + _mx(d_attn, ck.kg, "nn")
        d_kg = _mx(d_attn, ck.qg, "tn")
        d_v = _mx(ck.attn, dov, "tn") + _mx(ck.kd, dSt, "nt")
        d_kd = _mx(v, dSt, "nn")
        ds_scr[...] = dSt * ck.egl + _mx(dov, ck.qg, "tn")
        kdd = d_kd * ck.kd
        d_gl = jnp.sum(dSt * St, axis=1, keepdims=True) * ck.egl + jnp.sum(kdd, axis=1, keepdims=True)
        d_q = d_qg * ck.eG * (GLA_DK**-0.5)
        d_k = d_kg * ck.enG + d_kd * ck.ekd
        d_G = _add_at_last_row(d_qg * ck.qg - d_kg * ck.kg - kdd, d_gl, nh)
        shp = (2 * nh, CHUNK, CHUNK)
        bb, ii, jj = _iota3(shp, 0), _iota3(shp, 1), _iota3(shp, 2)
        tri = jnp.logical_or(jnp.logical_and(bb < nh, jj >= ii), jnp.logical_and(bb >= nh, jj <= ii))
        d_gk = _exact01(jnp.where(tri, 1.0, 0.0), d_G)
        for d, (dq_ref, dk_ref, dv_ref, dg_ref) in enumerate(((dqf_ref, dkf_ref, dvf_ref, dgf_ref), (dqb_ref, dkb_ref, dvb_ref, dgb_ref))):
            for i in range(nh):
                c = d * nh + i
                ks, vs = slice(i * GLA_DK, (i + 1) * GLA_DK), slice(i * GLA_DV, (i + 1) * GLA_DV)
                dq_ref[:, ks] = d_q[c]
                dk_ref[:, ks] = d_k[c]
                dv_ref[:, vs] = d_v[c]
                dg_ref[:, ks] = d_gk[c]

    def in_specs(ci):
        return _gla_specs(N, ci) + [
            pl.BlockSpec((1, nh, GLA_DV, GLA_DK), lambda n: (ci(n), 0, 0, 0)),
            pl.BlockSpec((CHUNK, nh * GLA_DV), lambda n: (ci(n), 0)),
        ]

    def out_specs(ci):
        ksp = pl.BlockSpec((CHUNK, nh * GLA_DK), lambda n: (ci(n), 0))
        return [ksp, ksp, pl.BlockSpec((CHUNK, nh * GLA_DV), lambda n: (ci(n), 0)), ksp]

    fw, bw = (lambda n: N - 1 - n), (lambda n: n)
    k_shape = jax.ShapeDtypeStruct((T, nh * GLA_DK), f32)
    v_shape = jax.ShapeDtypeStruct((T, nh * GLA_DV), f32)
    return pl.pallas_call(
        body,
        out_shape=(k_shape, k_shape, v_shape, k_shape) * 2,
        grid=(N,),
        in_specs=in_specs(fw) + in_specs(bw),
        out_specs=tuple(out_specs(fw) + out_specs(bw)),
        scratch_shapes=[pltpu.VMEM((2 * nh, GLA_DV, GLA_DK), f32)],
        compiler_params=_cparams(("arbitrary",)),
        name="gla_scan_bwd",
    )(proj, proj, proj, G_f, st_f, do, proj, proj, proj, G_b, st_b, do)


def _local_step(x, target, wcat, conv_w, avec, dvec, gdn_norm_w, w2f, w2b, b2f, b2b, gla_norm_w, wpg, wpl, wout, w_pre, w_post):
    h, h_t = _rms_pre(x, w_pre)
    proj = _matmul(h, wcat, "nn", f32, "proj_in", 2048, 512, 1024)
    qkvn = _gdn_prep(proj, conv_w)
    gates, gc = _gdn_gates(proj, avec, dvec)
    oa_f, oa_b, sa_f, sa_b, inv_f, inv_b = _gdn_fwd(qkvn, gates, gc)
    og = _out_norm(oa_f, oa_b, proj, Z_BLK, gdn_norm_w, GDN_DIM, "gdn_out")
    G_f, G_b = _gla_prep(proj, w2f, w2b, b2f, b2b)
    ob_f, ob_b, sb_f, sb_b = _gla_fwd(proj, G_f, G_b)
    obg = _out_norm(ob_f, ob_b, proj, G_B_BLK, gla_norm_w, GLA_DV, "gla_out")
    (loss_row, d_wpost, dy, m, dout, dya, dyb, dga, dgb, dog, dobg) = _mid(og, obg, proj, x, target, wpg, wpl, wout, w_post)

    d_wout = _matmul(m, dout, "tn", MXU_DTYPE, "dw_out", 512, 512, 2048)
    d_wpg = _matmul(og, dya, "tn", MXU_DTYPE, "dw_proj_gdn", 512, 512, 2048)
    d_wpl = _matmul(obg, dyb, "tn", MXU_DTYPE, "dw_proj_gla", 512, 512, 2048)

    do_b, d_gb, d_gla_norm = _out_norm_bwd(dobg, ob_f, ob_b, proj, G_B_BLK, gla_norm_w, GLA_DV, "gla_out_bwd")
    dq_f, dk_f, dv_f, dgk_f, dq_b, dk_b, dv_b, dgk_b = _gla_bwd(proj, G_f, G_b, sb_f, sb_b, do_b)
    dsmall_gla, d_w2f, d_w2b, d_b2f, d_b2b = _gla_prep_bwd(proj, w2f, w2b, b2f, b2b, dgk_f, dgk_b)

    do_a, d_z, d_gdn_norm = _out_norm_bwd(dog, oa_f, oa_b, proj, Z_BLK, gdn_norm_w, GDN_DIM, "gdn_out_bwd")
    dqkv_f, dqkv_b, dg_f, dg_b = _gdn_bwd(qkvn, gates, gc, sa_f, sa_b, inv_f, inv_b, do_a)
    d_qkv, d_conv = _gdn_prep_bwd(proj, conv_w, dqkv_f, dqkv_b)
    d_small, d_alog, d_dtb = _gdn_gates_bwd(proj, avec, dvec, dg_f, dg_b, dsmall_gla)

    T = x.shape[0]
    dproj = jnp.concatenate(
        [d_qkv, d_z, (dq_f + dq_b).astype(MXU_DTYPE), (dk_f + dk_b).astype(MXU_DTYPE), (dv_f + dv_b).astype(MXU_DTYPE),
         d_gb, dga, dgb, d_small, jnp.zeros((T, N_CAT - SMALL_OFF - LANES), MXU_DTYPE)], axis=1)
    d_wcat = _matmul(h_t, dproj, "nn", MXU_DTYPE, "dw_in", 1024, 512, 2048)
    dh = _matmul(dproj, wcat, "nt", f32, "dh", 1024, 1024, 2432)
    grad_x, d_wpre = _rms_pre_bwd(x, w_pre, dh, dy)
    return dict(loss_row=loss_row, grad_x=grad_x, d_wcat=d_wcat, d_conv=d_conv, d_alog=d_alog, d_dtb=d_dtb,
                d_gdn_norm=d_gdn_norm, d_wpg=d_wpg, d_w2f=d_w2f, d_w2b=d_w2b, d_b2f=d_b2f, d_b2b=d_b2b,
                d_gla_norm=d_gla_norm, d_wpl=d_wpl, d_wout=d_wout, d_wpost=d_wpost, d_wpre=d_wpre)


def _to_cat(wf):
    pad = jnp.zeros(wf.shape[:-1] + (N_CAT - N_IN,), wf.dtype)
    return jnp.concatenate([wf[..., :4096], wf[..., 4128:7200], wf[..., 7232:9280], wf[..., 4096:4128], wf[..., 7200:7232], pad], axis=-1)


def _from_cat(wc):
    return jnp.concatenate([wc[..., :4096], wc[..., 9216:9248], wc[..., 4096:7168], wc[..., 9248:9280], wc[..., 7168:9216]], axis=-1)


def _lane_row(*pieces):
    row = jnp.concatenate(pieces, axis=1)
    return jnp.pad(row, ((0, 0), (0, LANES - row.shape[1])))


def _pad_w2(w2, row0):
    return jnp.pad(w2, ((row0, LANES - row0 - w2.shape[0]), (0, 0)))


SM_ROWS, SM_COLS = 32, 384
REP_ROWS = 32


def _pack_small_shard(conv, w2f, w2b):
    lead = conv.shape[:-2]
    z = lambda r, c: jnp.zeros(lead + (r, c), f32)
    top = jnp.concatenate([conv, z(8 - CONV_K, SM_COLS)], axis=-2)
    mid = jnp.concatenate([w2f, w2b, z(16, SM_COLS - 128)], axis=-1)
    return jnp.concatenate([top, mid, z(SM_ROWS - 24, SM_COLS)], axis=-2)


def _pack_rep(ln_pre, ln_post, b2f, b2b, gla_norm, gdn_norm, alog_f, alog_b, dt_f, dt_b):
    last = _lane_row(alog_f, alog_b, dt_f, dt_b)
    rows = [ln_pre.reshape(8, LANES), ln_post.reshape(8, LANES), b2f.reshape(4, LANES), b2b.reshape(4, LANES),
            gla_norm.reshape(2, LANES), gdn_norm.reshape(1, LANES), last, jnp.zeros((REP_ROWS - 28, LANES), f32)]
    return jnp.concatenate(rows, axis=0)


def _unpack_rep(p):
    return dict(ln_pre_w=p[0:8].reshape(1, 1024), ln_post_w=p[8:16].reshape(1, 1024), gk_b2_fwd=p[16:20].reshape(1, 512),
                gk_b2_bwd=p[20:24].reshape(1, 512), gla_norm_w=p[24:26].reshape(1, 256), gdn_norm_w=p[26:27],
                a_log_fwd=p[27:28, 0:8], a_log_bwd=p[27:28, 8:16], dt_bias_fwd=p[27:28, 16:24], dt_bias_bwd=p[27:28, 24:32])


_MESH = pl.DeviceIdType.MESH


def _gather_weights(w_in_s, wp_s, sm_s):
    shapes = (w_in_s.shape, wp_s.shape, sm_s.shape)
    dtypes = (MXU_DTYPE, MXU_DTYPE, f32)

    def body(win_ref, wp_ref, sm_ref, gin_ref, gp_ref, gsm_ref, send_sems, recv_sems):
        x, y, c = lax.axis_index("x"), lax.axis_index("y"), lax.axis_index("c")
        me, sibling = (x, y, c), (x, y, 1 - c)
        chips = [(1 - x, y), (x, 1 - y), (1 - x, 1 - y)]
        bufs = (gin_ref, gp_ref, gsm_ref)

        def idx(px, py, pc):
            return 4 * px + 2 * py + pc

        rows_in = w_in_s.shape[0] // 8
        for r in range(8):
            gin_ref[idx(*me), r * rows_in : (r + 1) * rows_in, :] = win_ref[r * rows_in : (r + 1) * rows_in, :].astype(MXU_DTYPE)
        gp_ref[idx(*me)] = wp_ref[...].astype(MXU_DTYPE)
        gsm_ref[idx(*me)] = sm_ref[...]

        def copy(a, k, block, to):
            blk = bufs[a].at[idx(*block)]
            return pltpu.make_async_remote_copy(src_ref=blk, dst_ref=blk, send_sem=send_sems.at[7 * a + k],
                                                recv_sem=recv_sems.at[7 * a + k], device_id=to, device_id_type=_MESH)

        first, passed = [], []
        for a in range(3):
            first.append(copy(a, 0, me, sibling))
            first += [copy(a, 1 + j, me, (*chip, c)) for j, chip in enumerate(chips)]
        for cp in first:
            cp.start()
        for j, chip in enumerate(chips):
            for a in range(3):
                copy(a, 1 + j, (*chip, c), me).wait_recv()
                cp = copy(a, 4 + j, (*chip, c), sibling)
                cp.start()
                passed.append(cp)
        for a in range(3):
            copy(a, 0, sibling, me).wait_recv()
            for j, chip in enumerate(chips):
                copy(a, 4 + j, (*chip, 1 - c), me).wait_recv()
        for cp in first + passed:
            cp.wait_send()

    vm = pl.BlockSpec(memory_space=pltpu.VMEM)
    return pl.pallas_call(
        body,
        out_shape=tuple(jax.ShapeDtypeStruct((N_DEV,) + s, d) for s, d in zip(shapes, dtypes)),
        in_specs=[vm, vm, vm],
        out_specs=(vm, vm, vm),
        scratch_shapes=[pltpu.SemaphoreType.DMA((21,)), pltpu.SemaphoreType.DMA((21,))],
        compiler_params=pltpu.CompilerParams(vmem_limit_bytes=VMEM_LIMIT),
        name="gather_weights",
    )(w_in_s, wp_s, sm_s)


N_CHIP = 4


def _exchange_sibling(g_in, g_p):
    srcs = (g_in, g_p)

    def body(in0, in1, out0, out1, send_sems, recv_sems):
        x, y, c = lax.axis_index("x"), lax.axis_index("y"), lax.axis_index("c")
        ins, outs = (in0, in1), (out0, out1)

        def copy(a, k):
            return pltpu.make_async_remote_copy(src_ref=ins[a].at[2 * k + 1 - c], dst_ref=outs[a].at[k], send_sem=send_sems.at[N_CHIP * a + k],
                                                recv_sem=recv_sems.at[N_CHIP * a + k], device_id=(x, y, 1 - c), device_id_type=_MESH)

        copies = [copy(a, k) for a in range(2) for k in range(N_CHIP)]
        for cp in copies:
            cp.start()
        for cp in copies:
            cp.wait_recv()
        for cp in copies:
            cp.wait_send()

    hbm = pl.BlockSpec(memory_space=pltpu.HBM)
    return pl.pallas_call(
        body,
        out_shape=tuple(jax.ShapeDtypeStruct((N_CHIP,) + s.shape[1:], s.dtype) for s in srcs),
        in_specs=[hbm] * 2,
        out_specs=(hbm,) * 2,
        scratch_shapes=[pltpu.SemaphoreType.DMA((2 * N_CHIP,)), pltpu.SemaphoreType.DMA((2 * N_CHIP,))],
        name="exchange_sibling",
    )(*srcs)


def _pair_sum(g, r, name, tb=128):
    _, R, Cc = g.shape

    def body(g_ref, r_ref, o_ref):
        o_ref[...] = (g_ref[...].astype(f32) + r_ref[...].astype(f32)).astype(o_ref.dtype)

    return pl.pallas_call(
        body,
        out_shape=jax.ShapeDtypeStruct(r.shape, r.dtype),
        grid=(N_CHIP, R // tb),
        in_specs=[pl.BlockSpec((1, tb, Cc), lambda k, i: (2 * k + lax.axis_index("c"), i, 0)), pl.BlockSpec((1, tb, Cc), lambda k, i: (k, i, 0))],
        out_specs=pl.BlockSpec((1, tb, Cc), lambda k, i: (k, i, 0)),
        compiler_params=_cparams(("parallel", "parallel")),
        name=name,
    )(g, r)


def _exchange_grads(q_in, q_p, g_sm, g_rep):
    srcs = (q_in, q_p, g_sm, g_rep)

    def body(in0, in1, in2, in3, out0, out1, out2, out3, send_sems, recv_sems, local_sems):
        x, y, c = lax.axis_index("x"), lax.axis_index("y"), lax.axis_index("c")
        me, my_chip = 4 * x + 2 * y + c, 2 * x + y
        ins, outs = (in0, in1, in2, in3), (out0, out1, out2, out3)

        def peer(k):
            px = (1 - x) if (k & 4) else x
            py = (1 - y) if (k & 2) else y
            pc = (1 - c) if (k & 1) else c
            return (px, py, pc)

        def slots(a, k):
            px, py, pc = peer(k)
            if a < 2:
                return ins[a].at[2 * px + py], my_chip, 2 * px + py
            p = 4 * px + 2 * py + pc
            return (ins[a].at[p] if a == 2 else ins[a]), me, p

        def copy(a, k, arrival):
            src, there, here = slots(a, k)
            sem = (7 * (a - 2) + k - 1) if a >= 2 else (14 + 3 * a + k // 2 - 1)
            return pltpu.make_async_remote_copy(src_ref=src, dst_ref=outs[a].at[here if arrival else there], send_sem=send_sems.at[sem],
                                                recv_sem=recv_sems.at[sem], device_id=peer(k), device_id_type=_MESH)

        pairs = [(a, k) for a in (0, 1) for k in (2, 4, 6)] + [(a, k) for a in (2, 3) for k in range(1, 8)]
        sends = [copy(a, k, False) for a, k in pairs]
        for cp in sends:
            cp.start()
        own = (ins[0].at[my_chip], ins[1].at[my_chip], ins[2].at[me], ins[3])
        own_dst = (outs[0].at[my_chip], outs[1].at[my_chip], outs[2].at[me], outs[3].at[me])
        local = [pltpu.make_async_copy(own[a], own_dst[a], local_sems.at[a]) for a in range(4)]
        for cp in local:
            cp.start()
        for a, k in pairs:
            copy(a, k, True).wait_recv()
        for cp in sends:
            cp.wait_send()
        for cp in local:
            cp.wait()

    hbm = pl.BlockSpec(memory_space=pltpu.HBM)
    out_shape = tuple(jax.ShapeDtypeStruct(s.shape if i < 3 else (N_DEV,) + s.shape, s.dtype) for i, s in enumerate(srcs))
    return pl.pallas_call(
        body,
        out_shape=out_shape,
        in_specs=[hbm] * 4,
        out_specs=(hbm,) * 4,
        scratch_shapes=[pltpu.SemaphoreType.DMA((20,)), pltpu.SemaphoreType.DMA((20,)), pltpu.SemaphoreType.DMA((4,))],
        name="exchange_grads",
    )(*srcs)


def _sum_adam(parts, w, m, v, name, tb):
    R, Cc = w.shape
    n_parts = parts.shape[0]
    tb = min(tb, R)
    assert R % tb == 0
    bc1 = 1.0 - ADAM_B1**ADAM_STEP
    bc2 = 1.0 - ADAM_B2**ADAM_STEP

    def body(p_ref, w_ref, m_ref, v_ref, g_ref, d_ref, nm_ref, nv_ref):
        g = p_ref[0].astype(f32)
        for j in range(1, n_parts):
            g = g + p_ref[j].astype(f32)
        g_ref[...] = g
        m2 = ADAM_B1 * m_ref[...] + (1.0 - ADAM_B1) * g
        v2 = ADAM_B2 * v_ref[...] + (1.0 - ADAM_B2) * (g * g)
        nm_ref[...] = m2
        nv_ref[...] = v2
        d_ref[...] = -ADAM_LR * ((m2 / bc1) / (jnp.sqrt(v2 / bc2) + ADAM_EPS) + ADAM_WD * w_ref[...])

    blk = pl.BlockSpec((tb, Cc), lambda i: (i, 0))
    o = jax.ShapeDtypeStruct((R, Cc), f32)
    return pl.pallas_call(
        body,
        out_shape=(o, o, o, o),
        grid=(R // tb,),
        in_specs=[pl.BlockSpec((n_parts, tb, Cc), lambda i: (0, i, 0)), blk, blk, blk],
        out_specs=(blk, blk, blk, blk),
        compiler_params=_cparams(("parallel",)),
        name=name,
    )(parts, w, m, v)


_WEIGHTS = ["ln_pre_w", "w_in", "conv_w", "a_log_fwd", "a_log_bwd", "dt_bias_fwd", "dt_bias_bwd", "gdn_norm_w", "w_proj_gdn",
            "gk_w2_fwd", "gk_b2_fwd", "gk_w2_bwd", "gk_b2_bwd", "gla_norm_w", "w_proj_gla", "w_out", "ln_post_w"]
_REP_ORDER = ["ln_pre_w", "ln_post_w", "gk_b2_fwd", "gk_b2_bwd", "gla_norm_w", "gdn_norm_w", "a_log_fwd", "a_log_bwd", "dt_bias_fwd", "dt_bias_bwd"]


def kernel(x, ln_pre_w, w_in, conv_w, a_log_fwd, a_log_bwd, dt_bias_fwd, dt_bias_bwd, gdn_norm_w, w_proj_gdn, gk_w2_fwd, gk_b2_fwd, gk_w2_bwd, gk_b2_bwd, gla_norm_w, w_proj_gla, w_out, ln_post_w, loss_target, m_ln_pre_w, m_w_in, m_conv_w, m_a_log_fwd, m_a_log_bwd, m_dt_bias_fwd, m_dt_bias_bwd, m_gdn_norm_w, m_w_proj_gdn, m_gk_w2_fwd, m_gk_b2_fwd, m_gk_w2_bwd, m_gk_b2_bwd, m_gla_norm_w, m_w_proj_gla, m_w_out, m_ln_post_w, v_ln_pre_w, v_w_in, v_conv_w, v_a_log_fwd, v_a_log_bwd, v_dt_bias_fwd, v_dt_bias_bwd, v_gdn_norm_w, v_w_proj_gdn, v_gk_w2_fwd, v_gk_b2_fwd, v_gk_w2_bwd, v_gk_b2_bwd, v_gla_norm_w, v_w_proj_gla, v_w_out, v_ln_post_w):
    args = locals()
    W = {n: args[n] for n in _WEIGHTS}
    M = {n: args["m_" + n] for n in _WEIGHTS}
    V = {n: args["v_" + n] for n in _WEIGHTS}

    wp_stack = lambda P: jnp.concatenate([P["w_proj_gdn"], P["w_proj_gla"], P["w_out"]], axis=0)
    sm_pack = lambda P: _pack_small_shard(P["conv_w"][0], P["gk_w2_fwd"][0], P["gk_w2_bwd"][0])
    g_in, g_p, g_sm = _gather_weights(w_in[0], wp_stack(W), sm_pack(W))
    wcat = _to_cat(g_in.transpose(1, 0, 2).reshape(D_MODEL, N_IN))
    wpg, wpl, wout = (g_p[:, i].reshape(D_MODEL, D_MODEL) for i in range(3))
    conv_full = g_sm[:, 0:CONV_K, :].transpose(1, 0, 2).reshape(CONV_K, N_DEV * SM_COLS)
    w2f_full = g_sm[:, 8:24, 0:64].transpose(1, 0, 2).reshape(16, 512)
    w2b_full = g_sm[:, 8:24, 64:128].transpose(1, 0, 2).reshape(16, 512)

    avec = _lane_row(a_log_fwd, a_log_bwd)
    dvec = _lane_row(dt_bias_fwd, dt_bias_bwd)
    G = _local_step(x[0], loss_target[0], wcat, conv_full, avec, dvec, gdn_norm_w, _pad_w2(w2f_full, 32), _pad_w2(w2b_full, 48),
                    gk_b2_fwd, gk_b2_bwd, gla_norm_w, wpg, wpl, wout, ln_pre_w, ln_post_w)
    loss = lax.psum(0.5 * jnp.sum(G["loss_row"]) / D_MODEL, ("x", "y", "c"))

    p_in = _from_cat(G["d_wcat"]).reshape(D_MODEL, N_DEV, SHARD_IN).transpose(1, 0, 2).astype(MXU_DTYPE)
    p_p = jnp.stack([G["d_wpg"], G["d_wpl"], G["d_wout"]]).reshape(3, N_DEV, D_MODEL // N_DEV, D_MODEL).transpose(1, 0, 2, 3).astype(MXU_DTYPE)
    p_sm = _pack_small_shard(G["d_conv"].reshape(CONV_K, N_DEV, SM_COLS).transpose(1, 0, 2),
                             G["d_w2f"][32:48].reshape(16, N_DEV, 64).transpose(1, 0, 2),
                             G["d_w2b"][48:64].reshape(16, N_DEV, 64).transpose(1, 0, 2))
    p_rep = _pack_rep(G["d_wpre"], G["d_wpost"], G["d_b2f"], G["d_b2b"], G["d_gla_norm"], G["d_gdn_norm"],
                      G["d_alog"][:, 0:8], G["d_alog"][:, 8:16], G["d_dtb"][:, 0:8], G["d_dtb"][:, 8:16])
    rows_p = 3 * D_MODEL // N_DEV
    p_p = p_p.reshape(N_DEV, rows_p, D_MODEL)
    sib_in, sib_p = _exchange_sibling(p_in, p_p)
    q_in = _pair_sum(p_in, sib_in, "pair_sum_w_in")
    q_p = _pair_sum(p_p, sib_p, "pair_sum_w_proj")
    r_in, r_p, r_sm, r_rep = _exchange_grads(q_in, q_p, p_sm, p_rep)

    o_in = _sum_adam(r_in, w_in[0], m_w_in[0], v_w_in[0], "adam_w_in", 128)
    o_p = _sum_adam(r_p, wp_stack(W).reshape(rows_p, D_MODEL), wp_stack(M).reshape(rows_p, D_MODEL),
                    wp_stack(V).reshape(rows_p, D_MODEL), "adam_w_proj", 128)
    o_sm = _sum_adam(r_sm, sm_pack(W), sm_pack(M), sm_pack(V), "adam_small", SM_ROWS)
    rep_pack = lambda P: _pack_rep(*[P[n] for n in _REP_ORDER])
    o_rep = _sum_adam(r_rep, rep_pack(W), rep_pack(M), rep_pack(V), "adam_rep", REP_ROWS)

    res = []
    for kind in range(4):
        rep = _unpack_rep(o_rep[kind])
        per = dict(rep)
        per["w_in"] = o_in[kind][None]
        pp = o_p[kind].reshape(3, 1, D_MODEL // N_DEV, D_MODEL)
        per["w_proj_gdn"], per["w_proj_gla"], per["w_out"] = pp[0], pp[1], pp[2]
        per["conv_w"] = o_sm[kind][None, 0:CONV_K, :]
        per["gk_w2_fwd"] = o_sm[kind][None, 8:24, 0:64]
        per["gk_w2_bwd"] = o_sm[kind][None, 8:24, 64:128]
        res.append([per[n] for n in _WEIGHTS])
    return (loss, G["grad_x"][None], *res[0], *res[1], *res[2], *res[3])
```

```python
import functools
import math

import jax
import jax.numpy as jnp
from jax import lax
from jax.experimental import pallas as pl
from jax.experimental.pallas import tpu as pltpu

f32 = jnp.float32
MXU_DTYPE = jnp.bfloat16
HI = lax.Precision.HIGHEST

D_MODEL = 1024
CHUNK = 64
NORM_EPS = 1e-6
GDN_HEADS, GDN_DIM = 8, 128
GLA_HEADS, GLA_DK, GLA_DV = 4, 128, 256
GATE_NORMALIZER = 16.0
CONV_K = 5
N_IN = 9280
N_DEV = 8
SHARD_IN = N_IN // N_DEV

N_CAT = 9728
SMALL_OFF = 9216
LANES = 128

ADAM_LR, ADAM_B1, ADAM_B2, ADAM_EPS, ADAM_WD, ADAM_STEP = 0.001, 0.9, 0.999, 1e-08, 0.01, 10

VMEM_LIMIT = 56 * 1024 * 1024


def _cparams(sem=None):
    return pltpu.CompilerParams(dimension_semantics=sem, vmem_limit_bytes=VMEM_LIMIT)


_DN = {"nn": (((1,), (0,)), ((), ())), "nt": (((1,), (1,)), ((), ())), "tn": (((0,), (0,)), ((), ()))}
_BDN = {"nn": (((2,), (1,)), ((0,), (0,))), "nt": (((2,), (2,)), ((0,), (0,))), "tn": (((1,), (1,)), ((0,), (0,)))}


def _dot16(a, b, kind):
    return lax.dot_general(a, b, (_DN if a.ndim == 2 else _BDN)[kind], preferred_element_type=f32)


def _mx(a, b, kind):
    return _dot16(a.astype(MXU_DTYPE), b.astype(MXU_DTYPE), kind)


def _split2(a):
    hi = a.astype(jnp.bfloat16)
    return hi, (a - hi.astype(f32)).astype(jnp.bfloat16)


def _split3(a):
    p1 = a.astype(jnp.bfloat16)
    r = a - p1.astype(f32)
    p2 = r.astype(jnp.bfloat16)
    return p1, p2, (r - p2.astype(f32)).astype(jnp.bfloat16)


def _h3s(a2, b2, kind):
    (ah, al), (bh, bl) = a2, b2
    return _dot16(ah, bh, kind) + (_dot16(ah, bl, kind) + _dot16(al, bh, kind))


def _h3(a, b, kind):
    return _h3s(_split2(a), _split2(b), kind)


def _exact01(t01, x, kind="nn"):
    t = t01.astype(jnp.bfloat16)
    x1, x2, x3 = _split3(x)
    return _dot16(t, x1, kind) + (_dot16(t, x2, kind) + _dot16(t, x3, kind))


def _colsum_col(e):
    ones = jnp.ones(e.shape[:-1] + (LANES,), jnp.bfloat16)
    e1, e2, e3 = _split3(e)
    return (_dot16(e1, ones, "tn") + (_dot16(e2, ones, "tn") + _dot16(e3, ones, "tn")))[..., 0:1]


def _iota3(shape, axis):
    return lax.broadcasted_iota(jnp.int32, shape, axis)


def _dir_masks(B, nf, Cn):
    shp = (B, Cn, Cn)
    bb, ii, jj = _iota3(shp, 0), _iota3(shp, 1), _iota3(shp, 2)
    fwd = bb < nf
    rev = jnp.logical_not(fwd)
    incl = jnp.logical_or(jnp.logical_and(fwd, ii >= jj), jnp.logical_and(rev, ii <= jj))
    strict = jnp.logical_and(incl, ii != jj)
    return incl, strict, ii == jj


def _last_row(x, nf):
    B, Cn = x.shape[0], x.shape[1]
    fwd = _iota3((B, 1, 1), 0) < nf
    return jnp.where(fwd, x[:, Cn - 1 : Cn, :], x[:, 0:1, :])


def _add_at_last_row(x, val, nf):
    B, Cn = x.shape[0], x.shape[1]
    bb, rr = _iota3((B, Cn, 1), 0), _iota3((B, Cn, 1), 1)
    at = jnp.logical_or(jnp.logical_and(bb < nf, rr == Cn - 1), jnp.logical_and(bb >= nf, rr == 0))
    return x + jnp.where(at, val, 0.0)


def _sigmoid(x):
    return jax.nn.sigmoid(x)


def _silu(x):
    return x * _sigmoid(x)


def _silu_grad(x):
    s = _sigmoid(x)
    return s * (1.0 + x * (1.0 - s))


def _softplus(x):
    u = jnp.exp(-jnp.abs(x))
    l1p = jnp.where(u < 1e-3, u * (1.0 - u * (0.5 - u * (1.0 / 3.0))), jnp.log(1.0 + u))
    return jnp.maximum(x, 0.0) + l1p


def _iota2(shape, axis):
    return lax.broadcasted_iota(jnp.int32, shape, axis)


def _matmul(a, b, kind, out_dtype, name, tm, tn, tk):
    if kind == "nn":
        (M, K), N = a.shape, b.shape[1]
    elif kind == "nt":
        (M, K), N = a.shape, b.shape[0]
    else:
        (K, M), N = a.shape, b.shape[1]
    tm, tn, tk = min(tm, M), min(tn, N), min(tk, K)
    assert M % tm == 0 and N % tn == 0 and K % tk == 0, (name, M, N, K)
    nk = K // tk
    if kind == "tn":
        a_spec = pl.BlockSpec((tk, tm), lambda i, j, k: (k, i))
    else:
        a_spec = pl.BlockSpec((tm, tk), lambda i, j, k: (i, k))
    if kind == "nt":
        b_spec = pl.BlockSpec((tn, tk), lambda i, j, k: (j, k))
    else:
        b_spec = pl.BlockSpec((tk, tn), lambda i, j, k: (k, j))

    def body(a_ref, b_ref, o_ref, *acc):
        p = _mx(a_ref[...], b_ref[...], kind)
        if nk == 1:
            o_ref[...] = p.astype(out_dtype)
        else:
            acc_ref = acc[0]
            k = pl.program_id(2)

            @pl.when(k == 0)
            def _():
                acc_ref[...] = p

            @pl.when(k > 0)
            def _():
                acc_ref[...] += p

            @pl.when(k == nk - 1)
            def _():
                o_ref[...] = acc_ref[...].astype(out_dtype)

    return pl.pallas_call(
        body,
        out_shape=jax.ShapeDtypeStruct((M, N), out_dtype),
        grid=(M // tm, N // tn, nk),
        in_specs=[a_spec, b_spec],
        out_specs=pl.BlockSpec((tm, tn), lambda i, j, k: (i, j)),
        scratch_shapes=[] if nk == 1 else [pltpu.VMEM((tm, tn), f32)],
        compiler_params=_cparams(("parallel", "parallel", "arbitrary")),
        name=name,
    )(a, b)


def _rms_pre(x, w, tb=256):
    T = x.shape[0]

    def body(x_ref, w_ref, h_ref, ht_ref):
        xv = x_ref[...]
        r = lax.rsqrt(jnp.mean(xv * xv, axis=1, keepdims=True) + NORM_EPS)
        h = xv * r * w_ref[...]
        h_ref[...] = h.astype(h_ref.dtype)
        ht_ref[...] = h.T.astype(ht_ref.dtype)

    return pl.pallas_call(
        body,
        out_shape=(jax.ShapeDtypeStruct((T, D_MODEL), MXU_DTYPE), jax.ShapeDtypeStruct((D_MODEL, T), MXU_DTYPE)),
        grid=(T // tb,),
        in_specs=[pl.BlockSpec((tb, D_MODEL), lambda i: (i, 0)), pl.BlockSpec((1, D_MODEL), lambda i: (0, 0))],
        out_specs=(pl.BlockSpec((tb, D_MODEL), lambda i: (i, 0)), pl.BlockSpec((D_MODEL, tb), lambda i: (0, i))),
        compiler_params=_cparams(("parallel",)),
        name="rms_pre",
    )(x, w)


def _rms_pre_bwd(x, w, dh, dy, tb=256):
    T = x.shape[0]

    def body(x_ref, w_ref, dh_ref, dy_ref, dx_ref, dw_ref):
        i = pl.program_id(0)
        xv, dhv = x_ref[...], dh_ref[...]
        r = lax.rsqrt(jnp.mean(xv * xv, axis=1, keepdims=True) + NORM_EPS)
        dhw = dhv * w_ref[...]
        dx_ref[...] = dy_ref[...] + r * dhw - xv * (r * r * r) * jnp.mean(dhw * xv, axis=1, keepdims=True)
        part = jnp.sum(dhv * xv * r, axis=0, keepdims=True)

        @pl.when(i == 0)
        def _():
            dw_ref[...] = part

        @pl.when(i > 0)
        def _():
            dw_ref[...] += part

    blk = pl.BlockSpec((tb, D_MODEL), lambda i: (i, 0))
    row = pl.BlockSpec((1, D_MODEL), lambda i: (0, 0))
    return pl.pallas_call(
        body,
        out_shape=(jax.ShapeDtypeStruct((T, D_MODEL), f32), jax.ShapeDtypeStruct((1, D_MODEL), f32)),
        grid=(T // tb,),
        in_specs=[blk, row, blk, blk],
        out_specs=(blk, row),
        compiler_params=_cparams(("arbitrary",)),
        name="rms_pre_bwd",
    )(x, w, dh, dy)


def _conv5(u, cw):
    T = u.shape[0]
    t = _iota2(u.shape, 0)
    acc = u * cw[2:3, :]
    for j in (0, 1, 3, 4):
        d = j - 2
        sh = pltpu.roll(u, (-d) % T, 0)
        valid = jnp.logical_and(t + d >= 0, t + d < T)
        acc = acc + jnp.where(valid, sh, 0.0) * cw[j : j + 1, :]
    return acc


def _gdn_prep(proj, conv_w):
    T = proj.shape[0]
    nh = GDN_HEADS

    def body(u_ref, cw_ref, o_ref):
        j = pl.program_id(0)
        s = _silu(_conv5(u_ref[...], cw_ref[...]))

        @pl.when(j < 2 * nh)
        def _():
            o_ref[0] = s * lax.rsqrt(jnp.sum(s * s, axis=1, keepdims=True) + NORM_EPS)

        @pl.when(j >= 2 * nh)
        def _():
            o_ref[0] = s

    return pl.pallas_call(
        body,
        out_shape=jax.ShapeDtypeStruct((3, T, GDN_HEADS * GDN_DIM), f32),
        grid=(3 * nh,),
        in_specs=[pl.BlockSpec((T, GDN_DIM), lambda j: (0, j)), pl.BlockSpec((CONV_K, GDN_DIM), lambda j: (0, j))],
        out_specs=pl.BlockSpec((1, T, GDN_DIM), lambda j: (j // nh, 0, j % nh)),
        compiler_params=_cparams(("parallel",)),
        name="gdn_prep",
    )(proj, conv_w)


def _gdn_prep_bwd(proj, conv_w, dqkv_f, dqkv_b):
    T = proj.shape[0]
    nh = GDN_HEADS

    def body(u_ref, cw_ref, df_ref, db_ref, du_ref, dcw_ref):
        j = pl.program_id(0)
        u, cw = u_ref[...], cw_ref[...]
        c = _conv5(u, cw)
        s = _silu(c)
        dn = df_ref[0] + db_ref[0]
        rinv = lax.rsqrt(jnp.sum(s * s, axis=1, keepdims=True) + NORM_EPS)
        ds_norm = rinv * dn - s * (rinv * rinv * rinv) * jnp.sum(dn * s, axis=1, keepdims=True)
        ds = jnp.where(j < 2 * nh, ds_norm, dn)
        dc = ds * _silu_grad(c)
        t = _iota2(u.shape, 0)
        du = dc * cw[2:3, :]
        rows = []
        for jj in range(CONV_K):
            d = jj - 2
            if d == 0:
                rows.append(jnp.sum(dc * u, axis=0, keepdims=True))
                continue
            ush = pltpu.roll(u, (-d) % T, 0)
            uvalid = jnp.logical_and(t + d >= 0, t + d < T)
            rows.append(jnp.sum(dc * jnp.where(uvalid, ush, 0.0), axis=0, keepdims=True))
            dsh = pltpu.roll(dc, d % T, 0)
            dvalid = jnp.logical_and(t - d >= 0, t - d < T)
            du = du + jnp.where(dvalid, dsh, 0.0) * cw[jj : jj + 1, :]
        du_ref[...] = du.astype(du_ref.dtype)
        for jj in range(CONV_K):
            dcw_ref[jj : jj + 1, :] = rows[jj]

    qspec = pl.BlockSpec((1, T, GDN_DIM), lambda j: (j // nh, 0, j % nh))
    return pl.pallas_call(
        body,
        out_shape=(jax.ShapeDtypeStruct((T, 3 * nh * GDN_DIM), MXU_DTYPE), jax.ShapeDtypeStruct((CONV_K, 3 * nh * GDN_DIM), f32)),
        grid=(3 * nh,),
        in_specs=[pl.BlockSpec((T, GDN_DIM), lambda j: (0, j)), pl.BlockSpec((CONV_K, GDN_DIM), lambda j: (0, j)), qspec, qspec],
        out_specs=(pl.BlockSpec((T, GDN_DIM), lambda j: (0, j)), pl.BlockSpec((CONV_K, GDN_DIM), lambda j: (0, j))),
        compiler_params=_cparams(("parallel",)),
        name="gdn_prep_bwd",
    )(proj, conv_w, dqkv_f, dqkv_b)


def _chunk_tri(n, rev):
    i, j = _iota2((n, n), 0), _iota2((n, n), 1)
    same = jnp.right_shift(i, 6) == jnp.right_shift(j, 6)
    order = (j >= i) if rev else (j <= i)
    return jnp.where(jnp.logical_and(same, order), 1.0, 0.0).astype(f32)


def _gdn_gates(proj, avec, dvec, tb=256):
    T = proj.shape[0]

    def body(s_ref, a_ref, d_ref, g_ref, gc_ref):
        small = s_ref[...]
        lane = _iota2(small.shape, 1)
        lg = -jnp.exp(a_ref[...]) * _softplus(small + d_ref[...])
        beta = _sigmoid(small)
        g_ref[...] = jnp.where(lane < 16, lg, jnp.where(lane < 32, beta, 0.0))
        lgm = jnp.where(lane < 16, lg, 0.0)
        gcf = _exact01(_chunk_tri(tb, False), lgm)
        gcr = _exact01(_chunk_tri(tb, True), lgm)
        gc_ref[...] = jnp.where(lane < 8, gcf, gcr)

    blk = pl.BlockSpec((tb, LANES), lambda i: (i, 0))
    row = pl.BlockSpec((1, LANES), lambda i: (0, 0))
    return pl.pallas_call(
        body,
        out_shape=(jax.ShapeDtypeStruct((T, LANES), f32), jax.ShapeDtypeStruct((T, LANES), f32)),
        grid=(T // tb,),
        in_specs=[pl.BlockSpec((tb, LANES), lambda i: (i, SMALL_OFF // LANES)), row, row],
        out_specs=(blk, blk),
        compiler_params=_cparams(("parallel",)),
        name="gdn_gates",
    )(proj, avec, dvec)


def _gdn_gates_bwd(proj, avec, dvec, dg_f, dg_b, dsmall_gla, tb=256):
    T = proj.shape[0]

    def body(s_ref, a_ref, d_ref, gf_ref, gb_ref, dl_ref, ds_ref, da_ref, dd_ref):
        i = pl.program_id(0)
        small = s_ref[...]
        lane = _iota2(small.shape, 1)
        dgate = gf_ref[...] + gb_ref[...]
        z = small + d_ref[...]
        nega = -jnp.exp(a_ref[...])
        dz = dgate * nega * _sigmoid(z)
        beta = _sigmoid(small)
        dsm = jnp.where(lane < 16, dz, jnp.where(lane < 32, dgate * beta * (1.0 - beta), 0.0))
        ds_ref[...] = (dsm + dl_ref[...]).astype(ds_ref.dtype)
        lg = nega * _softplus(z)
        pa = jnp.sum(jnp.where(lane < 16, dgate * lg, 0.0), axis=0, keepdims=True)
        pd = jnp.sum(jnp.where(lane < 16, dz, 0.0), axis=0, keepdims=True)

        @pl.when(i == 0)
        def _():
            da_ref[...] = pa
            dd_ref[...] = pd

        @pl.when(i > 0)
        def _():
            da_ref[...] += pa
            dd_ref[...] += pd

    blk = pl.BlockSpec((tb, LANES), lambda i: (i, 0))
    row = pl.BlockSpec((1, LANES), lambda i: (0, 0))
    return pl.pallas_call(
        body,
        out_shape=(jax.ShapeDtypeStruct((T, LANES), MXU_DTYPE), jax.ShapeDtypeStruct((1, LANES), f32), jax.ShapeDtypeStruct((1, LANES), f32)),
        grid=(T // tb,),
        in_specs=[pl.BlockSpec((tb, LANES), lambda i: (i, SMALL_OFF // LANES)), row, row, blk, blk, blk],
        out_specs=(blk, row, row),
        compiler_params=_cparams(("arbitrary",)),
        name="gdn_gates_bwd",
    )(proj, avec, dvec, dg_f, dg_b, dsmall_gla)


GDN_HB = 8


def _inv_unit_lower(L, eye):
    A = eye - L
    P2 = _split2(L)
    for _ in range(5):
        P2 = _split2(_h3s(P2, P2, "nn"))
        A = A + _h3s(_split2(A), P2, "nn")
    return A


def _lane_col(blk, idx):
    lane = _iota2(blk.shape, 1)
    return jnp.sum(jnp.where(lane == idx, blk, 0.0), axis=1, keepdims=True)


class _GdnChunk:
    def __init__(self, q, k, v, gcol, beta, rev, A=None):
        Cn = q.shape[0]
        ii, jj = _iota2((Cn, Cn), 0), _iota2((Cn, Cn), 1)
        self.incl = (ii <= jj) if rev else (ii >= jj)
        self.strict = (ii < jj) if rev else (ii > jj)
        grow = jnp.broadcast_to(gcol, (Cn, LANES)).T[0:1, :]
        diff = gcol - grow
        self.decay = jnp.where(self.incl, jnp.exp(jnp.where(self.incl, diff, 0.0)), 0.0)
        self.k, self.v, self.beta, self.gcol = k, v, beta, gcol
        self.qs = q * (GDN_DIM**-0.5)
        self.kb = k * beta
        self.L = jnp.where(self.strict, _mx(self.kb, k, "nt") * self.decay, 0.0)
        if A is None:
            A = _inv_unit_lower(self.L, jnp.where(ii == jj, 1.0, 0.0).astype(f32))
        self.A2 = _split2(A)
        self.A = A
        self.eg = jnp.exp(gcol)
        self.kbg = self.kb * self.eg
        self.sol = _h3s(self.A2, _split2(jnp.concatenate([v * beta, self.kbg], axis=1)), "nn")
        self.u = self.sol[:, :GDN_DIM]
        self.w = self.sol[:, GDN_DIM:]
        self.attn = jnp.where(self.incl, _mx(self.qs, k, "nt") * self.decay, 0.0)
        self.qd = self.qs * self.eg
        last = 0 if rev else Cn - 1
        self.last = last
        gl = gcol[last : last + 1, :]
        self.ekd = jnp.exp(gl - gcol)
        self.kd = k * self.ekd
        self.egl = jnp.exp(gl)


def _gdn_scan_fwd(qkvn, gates, gc):
    T = qkvn.shape[1]
    N = T // CHUNK
    nh, hb = GDN_HEADS, GDN_HB
    W = hb * GDN_DIM
    chains = [(d, hh) for d in range(2) for hh in range(hb)]

    def body(qf_ref, qb_ref, gf_ref, gb_ref, cf_ref, cb_ref, of_ref, ob_ref, sf_ref, sb_ref, af_ref, ab_ref, s_scr):
        n, b = pl.program_id(0), pl.program_id(1)

        @pl.when(n == 0)
        def _():
            for d, hh in chains:
                s_scr[d, b * hb + hh] = jnp.zeros((GDN_DIM, GDN_DIM), f32)

        dirs = ((qf_ref, gf_ref, cf_ref, of_ref, sf_ref, af_ref), (qb_ref, gb_ref, cb_ref, ob_ref, sb_ref, ab_ref))
        s_in = {ch: s_scr[ch[0], b * hb + ch[1]] for ch in chains}
        s_out = {}
        for d, hh in chains:
            q_ref, g_ref, c_ref, o_ref, st_ref, a_ref = dirs[d]
            h = b * hb + hh
            sl = slice(hh * GDN_DIM, (hh + 1) * GDN_DIM)
            gcol = _lane_col(c_ref[...], h + 8 * d)
            beta = _lane_col(g_ref[...], 16 + h + 8 * d)
            ck = _GdnChunk(q_ref[0, :, sl], q_ref[1, :, sl], q_ref[2, :, sl], gcol, beta, d == 1)
            S = s_in[d, hh]
            st_ref[0, hh] = S
            a_ref[0, hh] = ck.A
            v_new = ck.u - _mx(ck.w, S, "nn")
            o_ref[:, sl] = _mx(ck.qd, S, "nn") + _mx(ck.attn, v_new, "nn")
            s_out[d, hh] = S * ck.egl + _mx(ck.kd, v_new, "tn")
        for d, hh in chains:
            s_scr[d, b * hb + hh] = s_out[d, hh]

    fw, bw = (lambda n: n), (lambda n: N - 1 - n)
    qsp = lambda ci: pl.BlockSpec((3, CHUNK, W), lambda n, b: (0, ci(n), b))
    gsp = lambda ci: pl.BlockSpec((CHUNK, LANES), lambda n, b: (ci(n), 0))
    osp = lambda ci: pl.BlockSpec((CHUNK, W), lambda n, b: (ci(n), b))
    ssp = lambda ci: pl.BlockSpec((1, hb, GDN_DIM, GDN_DIM), lambda n, b: (ci(n), b, 0, 0))
    asp = lambda ci: pl.BlockSpec((1, hb, CHUNK, CHUNK), lambda n, b: (ci(n), b, 0, 0))
    o_shape = jax.ShapeDtypeStruct((T, nh * GDN_DIM), f32)
    s_shape = jax.ShapeDtypeStruct((N, nh, GDN_DIM, GDN_DIM), f32)
    a_shape = jax.ShapeDtypeStruct((N, nh, CHUNK, CHUNK), f32)
    return pl.pallas_call(
        body,
        out_shape=(o_shape, o_shape, s_shape, s_shape, a_shape, a_shape),
        grid=(N, nh // hb),
        in_specs=[qsp(fw), qsp(bw), gsp(fw), gsp(bw), gsp(fw), gsp(bw)],
        out_specs=(osp(fw), osp(bw), ssp(fw), ssp(bw), asp(fw), asp(bw)),
        scratch_shapes=[pltpu.VMEM((2, nh, GDN_DIM, GDN_DIM), f32)],
        compiler_params=_cparams(("arbitrary", "arbitrary")),
        name="gdn_scan_fwd",
    )(qkvn, qkvn, gates, gates, gc, gc)


def _gdn_scan_bwd(qkvn, gates, gc, st_f, st_b, a_f, a_b, do):
    T = qkvn.shape[1]
    N = T // CHUNK
    nh, hb = GDN_HEADS, GDN_HB
    nb = nh // hb
    W = hb * GDN_DIM
    chains = [(d, hh) for d in range(2) for hh in range(hb)]

    def body(qf_ref, qb_ref, gf_ref, gb_ref, cf_ref, cb_ref, sf_ref, sb_ref, af_ref, ab_ref, dof_ref, dob_ref,
             dqf_ref, dqb_ref, dgf_ref, dgb_ref, ds_scr):
        n, b = pl.program_id(0), pl.program_id(1)

        @pl.when(n == 0)
        def _():
            for d, hh in chains:
                ds_scr[d, b * hb + hh] = jnp.zeros((GDN_DIM, GDN_DIM), f32)

        @pl.when(b == 0)
        def _():
            dgf_ref[...] = jnp.zeros((CHUNK, LANES), f32)
            dgb_ref[...] = jnp.zeros((CHUNK, LANES), f32)

        dirs = ((qf_ref, gf_ref, cf_ref, sf_ref, af_ref, dof_ref, dqf_ref), (qb_ref, gb_ref, cb_ref, sb_ref, ab_ref, dob_ref, dqb_ref))
        ds_in = {ch: ds_scr[ch[0], b * hb + ch[1]] for ch in chains}
        ds_out = {}
        lane = _iota2((CHUNK, LANES), 1)
        acc = [dgf_ref[...], dgb_ref[...]]
        for d, hh in chains:
            q_ref, g_ref, c_ref, st_ref, a_ref, do_ref, dq_ref = dirs[d]
            h = b * hb + hh
            sl = slice(hh * GDN_DIM, (hh + 1) * GDN_DIM)
            gcol = _lane_col(c_ref[...], h + 8 * d)
            beta = _lane_col(g_ref[...], 16 + h + 8 * d)
            ck = _GdnChunk(q_ref[0, :, sl], q_ref[1, :, sl], q_ref[2, :, sl], gcol, beta, d == 1, A=a_ref[0, hh])
            k, v = ck.k, ck.v
            S = st_ref[0, hh]
            dS = ds_in[d, hh]
            dov = do_ref[:, sl]
            v_new = ck.u - _mx(ck.w, S, "nn")
            d_vnew = _mx(ck.attn, dov, "tn") + _mx(ck.kd, dS, "nn")
            d_attn = jnp.where(ck.incl, _mx(dov, v_new, "nt"), 0.0)
            d_qd = _mx(dov, S, "nt")
            d_kd = _mx(v_new, dS, "nt")
            d_gl = jnp.sum(jnp.sum(dS * S, axis=1, keepdims=True), axis=0, keepdims=True) * ck.egl
            ds_out[d, hh] = dS * ck.egl + _mx(ck.qd, dov, "tn") - _mx(ck.w, d_vnew, "tn")
            d_w = -_mx(d_vnew, S, "nt")
            d_rhs = _h3s(ck.A2, _split2(jnp.concatenate([d_vnew, d_w], axis=1)), "tn")
            d_vb, d_kbg = d_rhs[:, :GDN_DIM], d_rhs[:, GDN_DIM:]
            dL = -jnp.where(ck.strict, _h3(d_rhs, ck.sol, "nt"), 0.0)
            P = dL * ck.decay
            Q = d_attn * ck.decay
            d_kb = _mx(P, k, "nn") + d_kbg * ck.eg
            d_qs = _mx(Q, k, "nn") + d_qd * ck.eg
            d_k = _mx(P, ck.kb, "tn") + _mx(Q, ck.qs, "tn") + d_kd * ck.ekd + d_kb * ck.beta
            E = dL * ck.L + d_attn * ck.attn
            kdsum = jnp.sum(d_kd * ck.kd, axis=1, keepdims=True)
            d_gc = (jnp.sum(E, axis=1, keepdims=True) - _colsum_col(E)
                    + jnp.sum(d_kbg * ck.kbg, axis=1, keepdims=True)
                    + jnp.sum(d_qd * ck.qd, axis=1, keepdims=True) - kdsum)
            d_gl = d_gl + jnp.sum(kdsum, axis=0, keepdims=True)
            row = _iota2((CHUNK, 1), 0)
            d_gc = d_gc + jnp.where(row == ck.last, d_gl, 0.0)
            d_beta = jnp.sum(d_vb * v, axis=1, keepdims=True) + jnp.sum(d_kb * k, axis=1, keepdims=True)
            dq_ref[0, :, sl] = d_qs * (GDN_DIM**-0.5)
            dq_ref[1, :, sl] = d_k
            dq_ref[2, :, sl] = d_vb * ck.beta
            acc[d] = jnp.where(lane == h + 8 * d, d_gc, acc[d])
            acc[d] = jnp.where(lane == 16 + h + 8 * d, d_beta, acc[d])
        for d, hh in chains:
            ds_scr[d, b * hb + hh] = ds_out[d, hh]
        dgf_ref[...] = acc[0]
        dgb_ref[...] = acc[1]

        @pl.when(b == nb - 1)
        def _():
            for d, dg_ref in enumerate((dgf_ref, dgb_ref)):
                blk = dg_ref[...]
                dg_ref[...] = jnp.where(lane < 16, _exact01(_chunk_tri(CHUNK, d == 0), blk), blk)

    fw, bw = (lambda n: N - 1 - n), (lambda n: n)
    qsp = lambda ci: pl.BlockSpec((3, CHUNK, W), lambda n, b: (0, ci(n), b))
    gsp = lambda ci: pl.BlockSpec((CHUNK, LANES), lambda n, b: (ci(n), 0))
    osp = lambda ci: pl.BlockSpec((CHUNK, W), lambda n, b: (ci(n), b))
    ssp = lambda ci: pl.BlockSpec((1, hb, GDN_DIM, GDN_DIM), lambda n, b: (ci(n), b, 0, 0))
    asp = lambda ci: pl.BlockSpec((1, hb, CHUNK, CHUNK), lambda n, b: (ci(n), b, 0, 0))
    dq_shape = jax.ShapeDtypeStruct((3, T, nh * GDN_DIM), f32)
    dg_shape = jax.ShapeDtypeStruct((T, LANES), f32)
    return pl.pallas_call(
        body,
        out_shape=(dq_shape, dq_shape, dg_shape, dg_shape),
        grid=(N, nb),
        in_specs=[qsp(fw), qsp(bw), gsp(fw), gsp(bw), gsp(fw), gsp(bw), ssp(fw), ssp(bw), asp(fw), asp(bw), osp(fw), osp(bw)],
        out_specs=(qsp(fw), qsp(bw), gsp(fw), gsp(bw)),
        scratch_shapes=[pltpu.VMEM((2, nh, GDN_DIM, GDN_DIM), f32)],
        compiler_params=_cparams(("arbitrary", "arbitrary")),
        name="gdn_scan_bwd",
    )(qkvn, qkvn, gates, gates, gc, gc, st_f, st_b, a_f, a_b, do, do)


def _out_norm(o_f, o_b, proj, gate_blk, w, hd, name, tb=256):
    T, W = o_f.shape
    nh = W // hd

    def body(of_ref, ob_ref, z_ref, w_ref, y_ref):
        wv = w_ref[...]
        for i in range(nh):
            sl = slice(i * hd, (i + 1) * hd)
            o = of_ref[:, sl] + ob_ref[:, sl]
            r = lax.rsqrt(jnp.mean(o * o, axis=1, keepdims=True) + NORM_EPS)
            y_ref[:, sl] = (o * r * wv * _silu(z_ref[:, sl])).astype(y_ref.dtype)

    blk = pl.BlockSpec((tb, W), lambda i: (i, 0))
    return pl.pallas_call(
        body,
        out_shape=jax.ShapeDtypeStruct((T, W), MXU_DTYPE),
        grid=(T // tb,),
        in_specs=[blk, blk, pl.BlockSpec((tb, W), lambda i: (i, gate_blk)), pl.BlockSpec((1, hd), lambda i: (0, 0))],
        out_specs=blk,
        compiler_params=_cparams(("parallel",)),
        name=name,
    )(o_f, o_b, proj, w)


def _out_norm_bwd(dy, o_f, o_b, proj, gate_blk, w, hd, name, tb=256):
    T, W = o_f.shape
    nh = W // hd

    def body(dy_ref, of_ref, ob_ref, z_ref, w_ref, do_ref, dz_ref, dw_ref):
        i = pl.program_id(0)
        wv = w_ref[...]
        dw = jnp.zeros((1, hd), f32)
        for a in range(nh):
            sl = slice(a * hd, (a + 1) * hd)
            o = of_ref[:, sl] + ob_ref[:, sl]
            z = z_ref[:, sl]
            dyv = dy_ref[:, sl]
            r = lax.rsqrt(jnp.mean(o * o, axis=1, keepdims=True) + NORM_EPS)
            orr = o * r
            dn = dyv * _silu(z)
            dz_ref[:, sl] = (dyv * orr * wv * _silu_grad(z)).astype(dz_ref.dtype)
            dw = dw + jnp.sum(dn * orr, axis=0, keepdims=True)
            dnw = dn * wv
            do_ref[:, sl] = r * dnw - o * (r * r * r) * jnp.mean(dnw * o, axis=1, keepdims=True)

        @pl.when(i == 0)
        def _():
            dw_ref[...] = dw

        @pl.when(i > 0)
        def _():
            dw_ref[...] += dw

    blk = pl.BlockSpec((tb, W), lambda i: (i, 0))
    row = pl.BlockSpec((1, hd), lambda i: (0, 0))
    return pl.pallas_call(
        body,
        out_shape=(jax.ShapeDtypeStruct((T, W), f32), jax.ShapeDtypeStruct((T, W), MXU_DTYPE), jax.ShapeDtypeStruct((1, hd), f32)),
        grid=(T // tb,),
        in_specs=[blk, blk, blk, pl.BlockSpec((tb, W), lambda i: (i, gate_blk)), row],
        out_specs=(blk, blk, row),
        compiler_params=_cparams(("arbitrary",)),
        name=name,
    )(dy, o_f, o_b, proj, w)


def _gla_prep(proj, w2f, w2b, b2f, b2b, tb=256):
    T = proj.shape[0]
    W = GLA_HEADS * GLA_DK

    def body(s_ref, wf_ref, wb_ref, bf_ref, bb_ref, gf_ref, gb_ref):
        small = s_ref[...]
        gkf = -_softplus(-(_mx(small, wf_ref[...], "nn") + bf_ref[...])) * (1.0 / GATE_NORMALIZER)
        gkb = -_softplus(-(_mx(small, wb_ref[...], "nn") + bb_ref[...])) * (1.0 / GATE_NORMALIZER)
        gf_ref[...] = _exact01(_chunk_tri(tb, False), gkf)
        gb_ref[...] = _exact01(_chunk_tri(tb, True), gkb)

    blk = pl.BlockSpec((tb, W), lambda i: (i, 0))
    wsp = pl.BlockSpec((LANES, W), lambda i: (0, 0))
    row = pl.BlockSpec((1, W), lambda i: (0, 0))
    return pl.pallas_call(
        body,
        out_shape=(jax.ShapeDtypeStruct((T, W), f32), jax.ShapeDtypeStruct((T, W), f32)),
        grid=(T // tb,),
        in_specs=[pl.BlockSpec((tb, LANES), lambda i: (i, SMALL_OFF // LANES)), wsp, wsp, row, row],
        out_specs=(blk, blk),
        compiler_params=_cparams(("parallel",)),
        name="gla_prep",
    )(proj, w2f, w2b, b2f, b2b)


def _gla_prep_bwd(proj, w2f, w2b, b2f, b2b, dgk_f, dgk_b, tb=256):
    T = proj.shape[0]
    W = GLA_HEADS * GLA_DK

    def body(s_ref, wf_ref, wb_ref, bf_ref, bb_ref, df_ref, db_ref, ds_ref, dwf_ref, dwb_ref, dbf_ref, dbb_ref):
        i = pl.program_id(0)
        small = s_ref[...]
        dsm = jnp.zeros((tb, LANES), f32)
        parts = []
        for w_ref, b_ref, d_ref in ((wf_ref, bf_ref, df_ref), (wb_ref, bb_ref, db_ref)):
            pre = _mx(small, w_ref[...], "nn") + b_ref[...]
            dpre = d_ref[...] * (1.0 / GATE_NORMALIZER) * _sigmoid(-pre)
            dsm = dsm + _mx(dpre, w_ref[...], "nt")
            parts.append((_mx(small, dpre, "tn"), jnp.sum(dpre, axis=0, keepdims=True)))
        ds_ref[...] = dsm

        @pl.when(i == 0)
        def _():
            dwf_ref[...], dbf_ref[...] = parts[0]
            dwb_ref[...], dbb_ref[...] = parts[1]

        @pl.when(i > 0)
        def _():
            dwf_ref[...] += parts[0][0]
            dbf_ref[...] += parts[0][1]
            dwb_ref[...] += parts[1][0]
            dbb_ref[...] += parts[1][1]

    blk = pl.BlockSpec((tb, W), lambda i: (i, 0))
    wsp = pl.BlockSpec((LANES, W), lambda i: (0, 0))
    row = pl.BlockSpec((1, W), lambda i: (0, 0))
    return pl.pallas_call(
        body,
        out_shape=(jax.ShapeDtypeStruct((T, LANES), f32), jax.ShapeDtypeStruct((LANES, W), f32), jax.ShapeDtypeStruct((LANES, W), f32),
                   jax.ShapeDtypeStruct((1, W), f32), jax.ShapeDtypeStruct((1, W), f32)),
        grid=(T // tb,),
        in_specs=[pl.BlockSpec((tb, LANES), lambda i: (i, SMALL_OFF // LANES)), wsp, wsp, row, row, blk, blk],
        out_specs=(pl.BlockSpec((tb, LANES), lambda i: (i, 0)), wsp, wsp, row, row),
        compiler_params=_cparams(("arbitrary",)),
        name="gla_prep_bwd",
    )(proj, w2f, w2b, b2f, b2b, dgk_f, dgk_b)


Q_B_BLK, K_B_BLK, V_B_BLK = 4096 // GLA_DK, 4608 // GLA_DK, 5120 // GLA_DV


class _GlaChunk:
    def __init__(self, q, k, G, rev):
        Cn = q.shape[0]
        ii, jj = _iota2((Cn, Cn), 0), _iota2((Cn, Cn), 1)
        self.incl = (ii <= jj) if rev else (ii >= jj)
        self.eG = jnp.exp(G)
        self.enG = jnp.exp(-G)
        self.qg = q * (GLA_DK**-0.5) * self.eG
        self.kg = k * self.enG
        self.attn = jnp.where(self.incl, _mx(self.qg, self.kg, "nt"), 0.0)
        last = 0 if rev else Cn - 1
        self.last = last
        gl = G[last : last + 1, :]
        self.ekd = jnp.exp(gl - G)
        self.kd = k * self.ekd
        self.egl = jnp.exp(gl)


def _gla_scan_fwd(proj, G_f, G_b):
    T = proj.shape[0]
    N = T // CHUNK
    nh = GLA_HEADS

    def body(qf_ref, kf_ref, vf_ref, gf_ref, qb_ref, kb_ref, vb_ref, gb_ref, of_ref, ob_ref, sf_ref, sb_ref, s_scr):
        n, h = pl.program_id(0), pl.program_id(1)

        @pl.when(n == 0)
        def _():
            s_scr[0, h] = jnp.zeros((GLA_DV, GLA_DK), f32)
            s_scr[1, h] = jnp.zeros((GLA_DV, GLA_DK), f32)

        for d, (q_ref, k_ref, v_ref, g_ref, o_ref, st_ref) in enumerate(
            ((qf_ref, kf_ref, vf_ref, gf_ref, of_ref, sf_ref), (qb_ref, kb_ref, vb_ref, gb_ref, ob_ref, sb_ref))
        ):
            ck = _GlaChunk(q_ref[...], k_ref[...], g_ref[...], d == 1)
            v = v_ref[...]
            St = s_scr[d, h]
            st_ref[0, 0] = St
            o_ref[...] = _mx(ck.qg, St, "nt") + _mx(ck.attn, v, "nn")
            s_scr[d, h] = St * ck.egl + _mx(v, ck.kd, "tn")

    def specs(cidx):
        return [
            pl.BlockSpec((CHUNK, GLA_DK), lambda n, h: (cidx(n), Q_B_BLK + h)),
            pl.BlockSpec((CHUNK, GLA_DK), lambda n, h: (cidx(n), K_B_BLK + h)),
            pl.BlockSpec((CHUNK, GLA_DV), lambda n, h: (cidx(n), V_B_BLK + h)),
            pl.BlockSpec((CHUNK, GLA_DK), lambda n, h: (cidx(n), h)),
        ]

    fw, bw = (lambda n: n), (lambda n: N - 1 - n)
    o_shape = jax.ShapeDtypeStruct((T, nh * GLA_DV), f32)
    s_shape = jax.ShapeDtypeStruct((N, nh, GLA_DV, GLA_DK), f32)
    return pl.pallas_call(
        body,
        out_shape=(o_shape, o_shape, s_shape, s_shape),
        grid=(N, nh),
        in_specs=specs(fw) + specs(bw),
        out_specs=(
            pl.BlockSpec((CHUNK, GLA_DV), lambda n, h: (n, h)),
            pl.BlockSpec((CHUNK, GLA_DV), lambda n, h: (N - 1 - n, h)),
            pl.BlockSpec((1, 1, GLA_DV, GLA_DK), lambda n, h: (n, h, 0, 0)),
            pl.BlockSpec((1, 1, GLA_DV, GLA_DK), lambda n, h: (N - 1 - n, h, 0, 0)),
        ),
        scratch_shapes=[pltpu.VMEM((2, nh, GLA_DV, GLA_DK), f32)],
        compiler_params=_cparams(("arbitrary", "arbitrary")),
        name="gla_scan_fwd",
    )(proj, proj, proj, G_f, proj, proj, proj, G_b)


def _gla_scan_bwd(proj, G_f, G_b, st_f, st_b, do):
    T = proj.shape[0]
    N = T // CHUNK
    nh = GLA_HEADS

    def body(qf_ref, kf_ref, vf_ref, gf_ref, sf_ref, dof_ref, qb_ref, kb_ref, vb_ref, gb_ref, sb_ref, dob_ref,
             dqf_ref, dkf_ref, dvf_ref, dgf_ref, dqb_ref, dkb_ref, dvb_ref, dgb_ref, ds_scr):
        n, h = pl.program_id(0), pl.program_id(1)

        @pl.when(n == 0)
        def _():
            ds_scr[0, h] = jnp.zeros((GLA_DV, GLA_DK), f32)
            ds_scr[1, h] = jnp.zeros((GLA_DV, GLA_DK), f32)

        for d, (q_ref, k_ref, v_ref, g_ref, st_ref, do_ref, dq_ref, dk_ref, dv_ref, dg_ref) in enumerate(
            ((qf_ref, kf_ref, vf_ref, gf_ref, sf_ref, dof_ref, dqf_ref, dkf_ref, dvf_ref, dgf_ref),
             (qb_ref, kb_ref, vb_ref, gb_ref, sb_ref, dob_ref, dqb_ref, dkb_ref, dvb_ref, dgb_ref))
        ):
            rev = d == 1
            ck = _GlaChunk(q_ref[...], k_ref[...], g_ref[...], rev)
            v = v_ref[...]
            St = st_ref[0, 0]
            dSt = ds_scr[d, h]
            dov = do_ref[...]
            d_attn = jnp.where(ck.incl, _mx(dov, v, "nt"), 0.0)
            d_qg = _mx(dov, St, "nn") + _mx(d_attn, ck.kg, "nn")
            d_kg = _mx(d_attn, ck.qg, "tn")
            dv_ref[...] = _mx(ck.attn, dov, "tn") + _mx(ck.kd, dSt, "nt")
            d_kd = _mx(v, dSt, "nn")
            ds_scr[d, h] = dSt * ck.egl + _mx(dov, ck.qg, "tn")
            kdd = d_kd * ck.kd
            d_gl = jnp.sum(dSt * St, axis=0, keepdims=True) * ck.egl + jnp.sum(kdd, axis=0, keepdims=True)
            dq_ref[...] = d_qg * ck.eG * (GLA_DK**-0.5)
            dk_ref[...] = d_kg * ck.enG + d_kd * ck.ekd
            d_G = d_qg * ck.qg - d_kg * ck.kg - kdd
            row = _iota2((CHUNK, GLA_DK), 0)
            d_G = d_G + jnp.where(row == ck.last, d_gl, 0.0)
            dg_ref[...] = _exact01(_chunk_tri(CHUNK, not rev), d_G)

    def in_specs(cidx):
        return [
            pl.BlockSpec((CHUNK, GLA_DK), lambda n, h: (cidx(n), Q_B_BLK + h)),
            pl.BlockSpec((CHUNK, GLA_DK), lambda n, h: (cidx(n), K_B_BLK + h)),
            pl.BlockSpec((CHUNK, GLA_DV), lambda n, h: (cidx(n), V_B_BLK + h)),
            pl.BlockSpec((CHUNK, GLA_DK), lambda n, h: (cidx(n), h)),
            pl.BlockSpec((1, 1, GLA_DV, GLA_DK), lambda n, h: (cidx(n), h, 0, 0)),
            pl.BlockSpec((CHUNK, GLA_DV), lambda n, h: (cidx(n), h)),
        ]

    def out_specs(cidx):
        return [
            pl.BlockSpec((CHUNK, GLA_DK), lambda n, h: (cidx(n), h)),
            pl.BlockSpec((CHUNK, GLA_DK), lambda n, h: (cidx(n), h)),
            pl.BlockSpec((CHUNK, GLA_DV), lambda n, h: (cidx(n), h)),
            pl.BlockSpec((CHUNK, GLA_DK), lambda n, h: (cidx(n), h)),
        ]

    fw, bw = (lambda n: N - 1 - n), (lambda n: n)
    k_shape = jax.ShapeDtypeStruct((T, nh * GLA_DK), f32)
    v_shape = jax.ShapeDtypeStruct((T, nh * GLA_DV), f32)
    return pl.pallas_call(
        body,
        out_shape=(k_shape, k_shape, v_shape, k_shape) * 2,
        grid=(N, nh),
        in_specs=in_specs(fw) + in_specs(bw),
        out_specs=tuple(out_specs(fw) + out_specs(bw)),
        scratch_shapes=[pltpu.VMEM((2, nh, GLA_DV, GLA_DK), f32)],
        compiler_params=_cparams(("arbitrary", "arbitrary")),
        name="gla_scan_bwd",
    )(proj, proj, proj, G_f, st_f, do, proj, proj, proj, G_b, st_b, do)


GATE_A_BLK, GATE_B_BLK, Z_BLK, G_B_BLK = 7168 // 1024, 8192 // 1024, 3072 // 1024, 6144 // 1024


def _mid(og, obg, proj, x, target, wpg, wpl, wout, w_post, tb=256):
    T = x.shape[0]
    Dm = D_MODEL

    def body(og_ref, obg_ref, ga_ref, gb_ref, x_ref, t_ref, wpg_ref, wpl_ref, wo_ref, wp_ref,
             loss_ref, dwp_ref, dy_ref, m_ref, dout_ref, dya_ref, dyb_ref, dga_ref, dgb_ref, dog_ref, dobg_ref):
        i = pl.program_id(0)
        ya = _mx(og_ref[...], wpg_ref[...], "nn")
        yb = _mx(obg_ref[...], wpl_ref[...], "nn")
        sa, sb = _sigmoid(ga_ref[...]), _sigmoid(gb_ref[...])
        m = sa * ya + sb * yb
        m_ref[...] = m.astype(m_ref.dtype)
        out = _mx(m, wo_ref[...], "nn")
        r = lax.rsqrt(jnp.mean(out * out, axis=1, keepdims=True) + NORM_EPS)
        wp = wp_ref[...]
        e = x_ref[...] + out * r * wp - t_ref[...]
        dy = e * (1.0 / Dm)
        dy_ref[...] = dy
        lpart = jnp.sum(e * e, axis=0, keepdims=True)
        wpart = jnp.sum(dy * out * r, axis=0, keepdims=True)
        dyn = dy * wp
        dout = r * dyn - out * (r * r * r) * jnp.mean(dyn * out, axis=1, keepdims=True)
        dout_ref[...] = dout.astype(dout_ref.dtype)
        dm = _mx(dout, wo_ref[...], "nt")
        dya, dyb = dm * sa, dm * sb
        dya_ref[...] = dya.astype(dya_ref.dtype)
        dyb_ref[...] = dyb.astype(dyb_ref.dtype)
        dga_ref[...] = (dm * ya * sa * (1.0 - sa)).astype(dga_ref.dtype)
        dgb_ref[...] = (dm * yb * sb * (1.0 - sb)).astype(dgb_ref.dtype)
        dog_ref[...] = _mx(dya, wpg_ref[...], "nt")
        dobg_ref[...] = _mx(dyb, wpl_ref[...], "nt")

        @pl.when(i == 0)
        def _():
            loss_ref[...] = lpart
            dwp_ref[...] = wpart

        @pl.when(i > 0)
        def _():
            loss_ref[...] += lpart
            dwp_ref[...] += wpart

    blk = pl.BlockSpec((tb, Dm), lambda i: (i, 0))
    row = pl.BlockSpec((1, Dm), lambda i: (0, 0))
    wsp = pl.BlockSpec((Dm, Dm), lambda i: (0, 0))
    act = jax.ShapeDtypeStruct((T, Dm), MXU_DTYPE)
    big = jax.ShapeDtypeStruct((T, Dm), f32)
    vec = jax.ShapeDtypeStruct((1, Dm), f32)
    return pl.pallas_call(
        body,
        out_shape=(vec, vec, big, act, act, act, act, act, act, big, big),
        grid=(T // tb,),
        in_specs=[blk, blk, pl.BlockSpec((tb, Dm), lambda i: (i, GATE_A_BLK)), pl.BlockSpec((tb, Dm), lambda i: (i, GATE_B_BLK)),
                  blk, blk, wsp, wsp, wsp, row],
        out_specs=(row, row) + (blk,) * 9,
        compiler_params=_cparams(("arbitrary",)),
        name="mid",
    )(og, obg, proj, proj, x, target, wpg, wpl, wout, w_post)


def _heads(ref, lead, n, width):
    return [ref[lead + (slice(None), slice(i * width, (i + 1) * width))] for i in range(n)]


class _GdnBatch:
    def __init__(self, q, k, v, gcol, grow, beta, nf, A=None):
        B, Cn = q.shape[0], q.shape[1]
        self.nf = nf
        self.incl, self.strict, eye = _dir_masks(B, nf, Cn)
        self.decay = jnp.where(self.incl, jnp.exp(jnp.where(self.incl, gcol - grow, 0.0)), 0.0)
        self.k, self.v, self.beta = k, v, beta
        self.qs = q * (GDN_DIM**-0.5)
        self.kb = k * beta
        self.L = jnp.where(self.strict, _mx(self.kb, k, "nt") * self.decay, 0.0)
        if A is None:
            A = _inv_unit_lower(self.L, jnp.where(eye, 1.0, 0.0).astype(f32))
        self.A = A
        self.A2 = _split2(A)
        self.eg = jnp.exp(gcol)
        self.kbg = self.kb * self.eg
        self.sol = _h3s(self.A2, _split2(jnp.concatenate([v * beta, self.kbg], axis=2)), "nn")
        self.u = self.sol[:, :, :GDN_DIM]
        self.w = self.sol[:, :, GDN_DIM:]
        self.attn = jnp.where(self.incl, _mx(self.qs, k, "nt") * self.decay, 0.0)
        self.qd = self.qs * self.eg
        gl = _last_row(gcol, nf)
        self.ekd = jnp.exp(gl - gcol)
        self.kd = k * self.ekd
        self.egl = jnp.exp(gl)


def _gdn_chain_inputs(qf_ref, qb_ref, gf_ref, gb_ref, cf_ref, cb_ref, b, hb):
    qkv = [jnp.stack(_heads(qf_ref, (i,), hb, GDN_DIM) + _heads(qb_ref, (i,), hb, GDN_DIM)) for i in range(3)]
    cols, rows, betas = [], [], []
    for d, (g_ref, c_ref) in enumerate(((gf_ref, cf_ref), (gb_ref, cb_ref))):
        gates, gc = g_ref[...], c_ref[...]
        for hh in range(hb):
            h = b * hb + hh
            col = _lane_col(gc, h + 8 * d)
            cols.append(col)
            rows.append(jnp.broadcast_to(col, (CHUNK, LANES)).T[0:1, :])
            betas.append(_lane_col(gates, 16 + h + 8 * d))
    return qkv[0], qkv[1], qkv[2], jnp.stack(cols), jnp.stack(rows), jnp.stack(betas)


def _gdn_fwd(qkvn, gates, gc):
    T = qkvn.shape[1]
    N = T // CHUNK
    nh, hb = GDN_HEADS, GDN_HB
    W = hb * GDN_DIM

    def body(qf_ref, qb_ref, gf_ref, gb_ref, cf_ref, cb_ref, of_ref, ob_ref, sf_ref, sb_ref, af_ref, ab_ref, s_scr):
        n, b = pl.program_id(0), pl.program_id(1)
        hs = pl.ds(b * hb, hb)

        @pl.when(n == 0)
        def _():
            s_scr[0, hs] = jnp.zeros((hb, GDN_DIM, GDN_DIM), f32)
            s_scr[1, hs] = jnp.zeros((hb, GDN_DIM, GDN_DIM), f32)

        q, k, v, gcol, grow, beta = _gdn_chain_inputs(qf_ref, qb_ref, gf_ref, gb_ref, cf_ref, cb_ref, b, hb)
        ck = _GdnBatch(q, k, v, gcol, grow, beta, hb)
        S = jnp.concatenate([s_scr[0, hs], s_scr[1, hs]], axis=0)
        sf_ref[0], sb_ref[0] = S[:hb], S[hb:]
        af_ref[0], ab_ref[0] = ck.A[:hb], ck.A[hb:]
        v_new = ck.u - _mx(ck.w, S, "nn")
        o = _mx(ck.qd, S, "nn") + _mx(ck.attn, v_new, "nn")
        S_new = S * ck.egl + _mx(ck.kd, v_new, "tn")
        for hh in range(hb):
            sl = slice(hh * GDN_DIM, (hh + 1) * GDN_DIM)
            of_ref[:, sl] = o[hh]
            ob_ref[:, sl] = o[hb + hh]
        s_scr[0, hs] = S_new[:hb]
        s_scr[1, hs] = S_new[hb:]

    fw, bw = (lambda n: n), (lambda n: N - 1 - n)
    qsp = lambda ci: pl.BlockSpec((3, CHUNK, W), lambda n, b: (0, ci(n), b))
    gsp = lambda ci: pl.BlockSpec((CHUNK, LANES), lambda n, b: (ci(n), 0))
    osp = lambda ci: pl.BlockSpec((CHUNK, W), lambda n, b: (ci(n), b))
    ssp = lambda ci: pl.BlockSpec((1, hb, GDN_DIM, GDN_DIM), lambda n, b: (ci(n), b, 0, 0))
    asp = lambda ci: pl.BlockSpec((1, hb, CHUNK, CHUNK), lambda n, b: (ci(n), b, 0, 0))
    o_shape = jax.ShapeDtypeStruct((T, nh * GDN_DIM), f32)
    s_shape = jax.ShapeDtypeStruct((N, nh, GDN_DIM, GDN_DIM), f32)
    a_shape = jax.ShapeDtypeStruct((N, nh, CHUNK, CHUNK), f32)
    return pl.pallas_call(
        body,
        out_shape=(o_shape, o_shape, s_shape, s_shape, a_shape, a_shape),
        grid=(N, nh // hb),
        in_specs=[qsp(fw), qsp(bw), gsp(fw), gsp(bw), gsp(fw), gsp(bw)],
        out_specs=(osp(fw), osp(bw), ssp(fw), ssp(bw), asp(fw), asp(bw)),
        scratch_shapes=[pltpu.VMEM((2, nh, GDN_DIM, GDN_DIM), f32)],
        compiler_params=_cparams(("arbitrary", "arbitrary")),
        name="gdn_scan_fwd",
    )(qkvn, qkvn, gates, gates, gc, gc)


def _gdn_bwd(qkvn, gates, gc, st_f, st_b, a_f, a_b, do):
    T = qkvn.shape[1]
    N = T // CHUNK
    nh, hb = GDN_HEADS, GDN_HB
    nb = nh // hb
    W = hb * GDN_DIM

    def body(qf_ref, qb_ref, gf_ref, gb_ref, cf_ref, cb_ref, sf_ref, sb_ref, af_ref, ab_ref, dof_ref, dob_ref,
             dqf_ref, dqb_ref, dgf_ref, dgb_ref, ds_scr):
        n, b = pl.program_id(0), pl.program_id(1)
        hs = pl.ds(b * hb, hb)

        @pl.when(n == 0)
        def _():
            ds_scr[0, hs] = jnp.zeros((hb, GDN_DIM, GDN_DIM), f32)
            ds_scr[1, hs] = jnp.zeros((hb, GDN_DIM, GDN_DIM), f32)

        @pl.when(b == 0)
        def _():
            dgf_ref[...] = jnp.zeros((CHUNK, LANES), f32)
            dgb_ref[...] = jnp.zeros((CHUNK, LANES), f32)

        q, k, v, gcol, grow, beta = _gdn_chain_inputs(qf_ref, qb_ref, gf_ref, gb_ref, cf_ref, cb_ref, b, hb)
        ck = _GdnBatch(q, k, v, gcol, grow, beta, hb, A=jnp.concatenate([af_ref[0], ab_ref[0]], axis=0))
        S = jnp.concatenate([sf_ref[0], sb_ref[0]], axis=0)
        dS = jnp.concatenate([ds_scr[0, hs], ds_scr[1, hs]], axis=0)
        dov = jnp.stack(_heads(dof_ref, (), hb, GDN_DIM) + _heads(dob_ref, (), hb, GDN_DIM))
        v_new = ck.u - _mx(ck.w, S, "nn")
        d_vnew = _mx(ck.attn, dov, "tn") + _mx(ck.kd, dS, "nn")
        d_attn = jnp.where(ck.incl, _mx(dov, v_new, "nt"), 0.0)
        d_qd = _mx(dov, S, "nt")
        d_kd = _mx(v_new, dS, "nt")
        d_gl = jnp.sum(jnp.sum(dS * S, axis=2, keepdims=True), axis=1, keepdims=True) * ck.egl
        dS_new = dS * ck.egl + _mx(ck.qd, dov, "tn") - _mx(ck.w, d_vnew, "tn")
        ds_scr[0, hs] = dS_new[:hb]
        ds_scr[1, hs] = dS_new[hb:]
        d_w = -_mx(d_vnew, S, "nt")
        d_rhs = _h3s(ck.A2, _split2(jnp.concatenate([d_vnew, d_w], axis=2)), "tn")
        d_vb, d_kbg = d_rhs[:, :, :GDN_DIM], d_rhs[:, :, GDN_DIM:]
        dL = -jnp.where(ck.strict, _h3(d_rhs, ck.sol, "nt"), 0.0)
        P = dL * ck.decay
        Q = d_attn * ck.decay
        d_kb = _mx(P, k, "nn") + d_kbg * ck.eg
        d_qs = _mx(Q, k, "nn") + d_qd * ck.eg
        d_k = _mx(P, ck.kb, "tn") + _mx(Q, ck.qs, "tn") + d_kd * ck.ekd + d_kb * ck.beta
        E = dL * ck.L + d_attn * ck.attn
        kdsum = jnp.sum(d_kd * ck.kd, axis=2, keepdims=True)
        d_gc = (jnp.sum(E, axis=2, keepdims=True) - _colsum_col(E)
                + jnp.sum(d_kbg * ck.kbg, axis=2, keepdims=True)
                + jnp.sum(d_qd * ck.qd, axis=2, keepdims=True) - kdsum)
        d_gc = _add_at_last_row(d_gc, d_gl + jnp.sum(kdsum, axis=1, keepdims=True), hb)
        d_beta = jnp.sum(d_vb * v, axis=2, keepdims=True) + jnp.sum(d_kb * k, axis=2, keepdims=True)
        d_q = d_qs * (GDN_DIM**-0.5)
        d_v = d_vb * ck.beta
        lane = _iota2((CHUNK, LANES), 1)
        for d, (dq_ref, dg_ref) in enumerate(((dqf_ref, dgf_ref), (dqb_ref, dgb_ref))):
            acc = dg_ref[...]
            for hh in range(hb):
                c, h = d * hb + hh, b * hb + hh
                sl = slice(hh * GDN_DIM, (hh + 1) * GDN_DIM)
                dq_ref[0, :, sl] = d_q[c]
                dq_ref[1, :, sl] = d_k[c]
                dq_ref[2, :, sl] = d_v[c]
                acc = jnp.where(lane == h + 8 * d, d_gc[c], acc)
                acc = jnp.where(lane == 16 + h + 8 * d, d_beta[c], acc)
            dg_ref[...] = acc

        @pl.when(b == nb - 1)
        def _():
            for d, dg_ref in enumerate((dgf_ref, dgb_ref)):
                blk = dg_ref[...]
                dg_ref[...] = jnp.where(lane < 16, _exact01(_chunk_tri(CHUNK, d == 0), blk), blk)

    fw, bw = (lambda n: N - 1 - n), (lambda n: n)
    qsp = lambda ci: pl.BlockSpec((3, CHUNK, W), lambda n, b: (0, ci(n), b))
    gsp = lambda ci: pl.BlockSpec((CHUNK, LANES), lambda n, b: (ci(n), 0))
    osp = lambda ci: pl.BlockSpec((CHUNK, W), lambda n, b: (ci(n), b))
    ssp = lambda ci: pl.BlockSpec((1, hb, GDN_DIM, GDN_DIM), lambda n, b: (ci(n), b, 0, 0))
    asp = lambda ci: pl.BlockSpec((1, hb, CHUNK, CHUNK), lambda n, b: (ci(n), b, 0, 0))
    dq_shape = jax.ShapeDtypeStruct((3, T, nh * GDN_DIM), f32)
    dg_shape = jax.ShapeDtypeStruct((T, LANES), f32)
    return pl.pallas_call(
        body,
        out_shape=(dq_shape, dq_shape, dg_shape, dg_shape),
        grid=(N, nb),
        in_specs=[qsp(fw), qsp(bw), gsp(fw), gsp(bw), gsp(fw), gsp(bw), ssp(fw), ssp(bw), asp(fw), asp(bw), osp(fw), osp(bw)],
        out_specs=(qsp(fw), qsp(bw), gsp(fw), gsp(bw)),
        scratch_shapes=[pltpu.VMEM((2, nh, GDN_DIM, GDN_DIM), f32)],
        compiler_params=_cparams(("arbitrary", "arbitrary")),
        name="gdn_scan_bwd",
    )(qkvn, qkvn, gates, gates, gc, gc, st_f, st_b, a_f, a_b, do, do)


class _GlaBatch:
    def __init__(self, q, k, G, nf):
        B, Cn = q.shape[0], q.shape[1]
        self.incl, _, _ = _dir_masks(B, nf, Cn)
        self.eG = jnp.exp(G)
        self.enG = jnp.exp(-G)
        self.qg = q * (GLA_DK**-0.5) * self.eG
        self.kg = k * self.enG
        self.attn = jnp.where(self.incl, _mx(self.qg, self.kg, "nt"), 0.0)
        gl = _last_row(G, nf)
        self.ekd = jnp.exp(gl - G)
        self.kd = k * self.ekd
        self.egl = jnp.exp(gl)


def _gla_chain_inputs(refs_f, refs_b):
    nh = GLA_HEADS
    out = []
    for i, width in enumerate((GLA_DK, GLA_DK, GLA_DV, GLA_DK)):
        out.append(jnp.stack(_heads(refs_f[i], (), nh, width) + _heads(refs_b[i], (), nh, width)))
    return out


def _gla_specs(N, ci):
    nh = GLA_HEADS
    return [
        pl.BlockSpec((CHUNK, nh * GLA_DK), lambda n: (ci(n), 4096 // (nh * GLA_DK))),
        pl.BlockSpec((CHUNK, nh * GLA_DK), lambda n: (ci(n), 4608 // (nh * GLA_DK))),
        pl.BlockSpec((CHUNK, nh * GLA_DV), lambda n: (ci(n), 5120 // (nh * GLA_DV))),
        pl.BlockSpec((CHUNK, nh * GLA_DK), lambda n: (ci(n), 0)),
    ]


def _gla_fwd(proj, G_f, G_b):
    T = proj.shape[0]
    N = T // CHUNK
    nh = GLA_HEADS

    def body(qf, kf, vf, gf, qb, kb, vb, gb, of_ref, ob_ref, sf_ref, sb_ref, s_scr):
        n = pl.program_id(0)

        @pl.when(n == 0)
        def _():
            s_scr[...] = jnp.zeros((2 * nh, GLA_DV, GLA_DK), f32)

        q, k, v, G = _gla_chain_inputs((qf, kf, vf, gf), (qb, kb, vb, gb))
        ck = _GlaBatch(q, k, G, nh)
        St = s_scr[...]
        sf_ref[0], sb_ref[0] = St[:nh], St[nh:]
        o = _mx(ck.qg, St, "nt") + _mx(ck.attn, v, "nn")
        for i in range(nh):
            sl = slice(i * GLA_DV, (i + 1) * GLA_DV)
            of_ref[:, sl] = o[i]
            ob_ref[:, sl] = o[nh + i]
        s_scr[...] = St * ck.egl + _mx(v, ck.kd, "tn")

    fw, bw = (lambda n: n), (lambda n: N - 1 - n)
    osp = lambda ci: pl.BlockSpec((CHUNK, nh * GLA_DV), lambda n: (ci(n), 0))
    ssp = lambda ci: pl.BlockSpec((1, nh, GLA_DV, GLA_DK), lambda n: (ci(n), 0, 0, 0))
    o_shape = jax.ShapeDtypeStruct((T, nh * GLA_DV), f32)
    s_shape = jax.ShapeDtypeStruct((N, nh, GLA_DV, GLA_DK), f32)
    return pl.pallas_call(
        body,
        out_shape=(o_shape, o_shape, s_shape, s_shape),
        grid=(N,),
        in_specs=_gla_specs(N, fw) + _gla_specs(N, bw),
        out_specs=(osp(fw), osp(bw), ssp(fw), ssp(bw)),
        scratch_shapes=[pltpu.VMEM((2 * nh, GLA_DV, GLA_DK), f32)],
        compiler_params=_cparams(("arbitrary",)),
        name="gla_scan_fwd",
    )(proj, proj, proj, G_f, proj, proj, proj, G_b)


def _gla_bwd(proj, G_f, G_b, st_f, st_b, do):
    T = proj.shape[0]
    N = T // CHUNK
    nh = GLA_HEADS

    def body(qf, kf, vf, gf, sf_ref, dof_ref, qb, kb, vb, gb, sb_ref, dob_ref,
             dqf_ref, dkf_ref, dvf_ref, dgf_ref, dqb_ref, dkb_ref, dvb_ref, dgb_ref, ds_scr):
        n = pl.program_id(0)

        @pl.when(n == 0)
        def _():
            ds_scr[...] = jnp.zeros((2 * nh, GLA_DV, GLA_DK), f32)

        q, k, v, G = _gla_chain_inputs((qf, kf, vf, gf), (qb, kb, vb, gb))
        ck = _GlaBatch(q, k, G, nh)
        St = jnp.concatenate([sf_ref[0], sb_ref[0]], axis=0)
        dSt = ds_scr[...]
        dov = jnp.stack(_heads(dof_ref, (), nh, GLA_DV) + _heads(dob_ref, (), nh, GLA_DV))
        d_attn = jnp.where(ck.incl, _mx(dov, v, "nt"), 0.0)
        d_qg = _mx(dov, St, "nn") + _mx(d_attn, ck.kg, "nn")
        d_kg = _mx(d_attn, ck.qg, "tn")
        d_v = _mx(ck.attn, dov, "tn") + _mx(ck.kd, dSt, "nt")
        d_kd = _mx(v, dSt, "nn")
        ds_scr[...] = dSt * ck.egl + _mx(dov, ck.qg, "tn")
        kdd = d_kd * ck.kd
        d_gl = jnp.sum(dSt * St, axis=1, keepdims=True) * ck.egl + jnp.sum(kdd, axis=1, keepdims=True)
        d_q = d_qg * ck.eG * (GLA_DK**-0.5)
        d_k = d_kg * ck.enG + d_kd * ck.ekd
        d_G = _add_at_last_row(d_qg * ck.qg - d_kg * ck.kg - kdd, d_gl, nh)
        shp = (2 * nh, CHUNK, CHUNK)
        bb, ii, jj = _iota3(shp, 0), _iota3(shp, 1), _iota3(shp, 2)
        tri = jnp.logical_or(jnp.logical_and(bb < nh, jj >= ii), jnp.logical_and(bb >= nh, jj <= ii))
        d_gk = _exact01(jnp.where(tri, 1.0, 0.0), d_G)
        for d, (dq_ref, dk_ref, dv_ref, dg_ref) in enumerate(((dqf_ref, dkf_ref, dvf_ref, dgf_ref), (dqb_ref, dkb_ref, dvb_ref, dgb_ref))):
            for i in range(nh):
                c = d * nh + i
                ks, vs = slice(i * GLA_DK, (i + 1) * GLA_DK), slice(i * GLA_DV, (i + 1) * GLA_DV)
                dq_ref[:, ks] = d_q[c]
                dk_ref[:, ks] = d_k[c]
                dv_ref[:, vs] = d_v[c]
                dg_ref[:, ks] = d_gk[c]

    def in_specs(ci):
        return _gla_specs(N, ci) + [
            pl.BlockSpec((1, nh, GLA_DV, GLA_DK), lambda n: (ci(n), 0, 0, 0)),
            pl.BlockSpec((CHUNK, nh * GLA_DV), lambda n: (ci(n), 0)),
        ]

    def out_specs(ci):
        ksp = pl.BlockSpec((CHUNK, nh * GLA_DK), lambda n: (ci(n), 0))
        return [ksp, ksp, pl.BlockSpec((CHUNK, nh * GLA_DV), lambda n: (ci(n), 0)), ksp]

    fw, bw = (lambda n: N - 1 - n), (lambda n: n)
    k_shape = jax.ShapeDtypeStruct((T, nh * GLA_DK), f32)
    v_shape = jax.ShapeDtypeStruct((T, nh * GLA_DV), f32)
    return pl.pallas_call(
        body,
        out_shape=(k_shape, k_shape, v_shape, k_shape) * 2,
        grid=(N,),
        in_specs=in_specs(fw) + in_specs(bw),
        out_specs=tuple(out_specs(fw) + out_specs(bw)),
        scratch_shapes=[pltpu.VMEM((2 * nh, GLA_DV, GLA_DK), f32)],
        compiler_params=_cparams(("arbitrary",)),
        name="gla_scan_bwd",
    )(proj, proj, proj, G_f, st_f, do, proj, proj, proj, G_b, st_b, do)


def _local_step(x, target, wcat, conv_w, avec, dvec, gdn_norm_w, w2f, w2b, b2f, b2b, gla_norm_w, proj_weights, w_pre, w_post,
                send_proj_grads=None):
    h, h_t = _rms_pre(x, w_pre)
    proj = _matmul(h, wcat, "nn", f32, "proj_in", 2048, 512, 1024)
    qkvn = _gdn_prep(proj, conv_w)
    gates, gc = _gdn_gates(proj, avec, dvec)
    oa_f, oa_b, sa_f, sa_b, inv_f, inv_b = _gdn_fwd(qkvn, gates, gc)
    og = _out_norm(oa_f, oa_b, proj, Z_BLK, gdn_norm_w, GDN_DIM, "gdn_out")
    G_f, G_b = _gla_prep(proj, w2f, w2b, b2f, b2b)
    ob_f, ob_b, sb_f, sb_b = _gla_fwd(proj, G_f, G_b)
    obg = _out_norm(ob_f, ob_b, proj, G_B_BLK, gla_norm_w, GLA_DV, "gla_out")
    wpg, wpl, wout = proj_weights(obg)
    (loss_row, d_wpost, dy, m, dout, dya, dyb, dga, dgb, dog, dobg) = _mid(og, obg, proj, x, target, wpg, wpl, wout, w_post)

    d_wout = _matmul(m, dout, "tn", MXU_DTYPE, "dw_out", 512, 512, 2048)
    d_wpg = _matmul(og, dya, "tn", MXU_DTYPE, "dw_proj_gdn", 512, 512, 2048)
    d_wpl = _matmul(obg, dyb, "tn", MXU_DTYPE, "dw_proj_gla", 512, 512, 2048)
    if send_proj_grads is not None:
        gla_norm_w = gla_norm_w + send_proj_grads(d_wpg, d_wpl, d_wout)

    do_b, d_gb, d_gla_norm = _out_norm_bwd(dobg, ob_f, ob_b, proj, G_B_BLK, gla_norm_w, GLA_DV, "gla_out_bwd")
    dq_f, dk_f, dv_f, dgk_f, dq_b, dk_b, dv_b, dgk_b = _gla_bwd(proj, G_f, G_b, sb_f, sb_b, do_b)
    dsmall_gla, d_w2f, d_w2b, d_b2f, d_b2b = _gla_prep_bwd(proj, w2f, w2b, b2f, b2b, dgk_f, dgk_b)

    do_a, d_z, d_gdn_norm = _out_norm_bwd(dog, oa_f, oa_b, proj, Z_BLK, gdn_norm_w, GDN_DIM, "gdn_out_bwd")
    dqkv_f, dqkv_b, dg_f, dg_b = _gdn_bwd(qkvn, gates, gc, sa_f, sa_b, inv_f, inv_b, do_a)
    d_qkv, d_conv = _gdn_prep_bwd(proj, conv_w, dqkv_f, dqkv_b)
    d_small, d_alog, d_dtb = _gdn_gates_bwd(proj, avec, dvec, dg_f, dg_b, dsmall_gla)

    T = x.shape[0]
    dproj = jnp.concatenate(
        [d_qkv, d_z, (dq_f + dq_b).astype(MXU_DTYPE), (dk_f + dk_b).astype(MXU_DTYPE), (dv_f + dv_b).astype(MXU_DTYPE),
         d_gb, dga, dgb, d_small, jnp.zeros((T, N_CAT - SMALL_OFF - LANES), MXU_DTYPE)], axis=1)
    d_wcat = _matmul(h_t, dproj, "nn", MXU_DTYPE, "dw_in", 1024, 512, 2048)
    dh = _matmul(dproj, wcat, "nt", f32, "dh", 1024, 1024, 2432)
    grad_x, d_wpre = _rms_pre_bwd(x, w_pre, dh, dy)
    return dict(loss_row=loss_row, grad_x=grad_x, d_wcat=d_wcat, d_conv=d_conv, d_alog=d_alog, d_dtb=d_dtb,
                d_gdn_norm=d_gdn_norm, d_wpg=d_wpg, d_w2f=d_w2f, d_w2b=d_w2b, d_b2f=d_b2f, d_b2b=d_b2b,
                d_gla_norm=d_gla_norm, d_wpl=d_wpl, d_wout=d_wout, d_wpost=d_wpost, d_wpre=d_wpre)


def _to_cat(wf):
    pad = jnp.zeros(wf.shape[:-1] + (N_CAT - N_IN,), wf.dtype)
    return jnp.concatenate([wf[..., :4096], wf[..., 4128:7200], wf[..., 7232:9280], wf[..., 4096:4128], wf[..., 7200:7232], pad], axis=-1)


def _from_cat(wc):
    return jnp.concatenate([wc[..., :4096], wc[..., 9216:9248], wc[..., 4096:7168], wc[..., 9248:9280], wc[..., 7168:9216]], axis=-1)


def _lane_row(*pieces):
    row = jnp.concatenate(pieces, axis=1)
    return jnp.pad(row, ((0, 0), (0, LANES - row.shape[1])))


def _pad_w2(w2, row0):
    return jnp.pad(w2, ((row0, LANES - row0 - w2.shape[0]), (0, 0)))


SM_ROWS, SM_COLS = 32, 384
REP_ROWS = 32


def _pack_small_shard(conv, w2f, w2b):
    lead = conv.shape[:-2]
    z = lambda r, c: jnp.zeros(lead + (r, c), f32)
    top = jnp.concatenate([conv, z(8 - CONV_K, SM_COLS)], axis=-2)
    mid = jnp.concatenate([w2f, w2b, z(16, SM_COLS - 128)], axis=-1)
    return jnp.concatenate([top, mid, z(SM_ROWS - 24, SM_COLS)], axis=-2)


def _pack_rep(ln_pre, ln_post, b2f, b2b, gla_norm, gdn_norm, alog_f, alog_b, dt_f, dt_b):
    last = _lane_row(alog_f, alog_b, dt_f, dt_b)
    rows = [ln_pre.reshape(8, LANES), ln_post.reshape(8, LANES), b2f.reshape(4, LANES), b2b.reshape(4, LANES),
            gla_norm.reshape(2, LANES), gdn_norm.reshape(1, LANES), last, jnp.zeros((REP_ROWS - 28, LANES), f32)]
    return jnp.concatenate(rows, axis=0)


def _unpack_rep(p):
    return dict(ln_pre_w=p[0:8].reshape(1, 1024), ln_post_w=p[8:16].reshape(1, 1024), gk_b2_fwd=p[16:20].reshape(1, 512),
                gk_b2_bwd=p[20:24].reshape(1, 512), gla_norm_w=p[24:26].reshape(1, 256), gdn_norm_w=p[26:27],
                a_log_fwd=p[27:28, 0:8], a_log_bwd=p[27:28, 8:16], dt_bias_fwd=p[27:28, 16:24], dt_bias_bwd=p[27:28, 24:32])


_MESH = pl.DeviceIdType.MESH


N_GATHERED = 2


def _gather_weights(w_in_s, sm_s):
    shapes = (w_in_s.shape, sm_s.shape)
    dtypes = (MXU_DTYPE, f32)

    def body(win_ref, sm_ref, gin_ref, gsm_ref, send_sems, recv_sems):
        x, y, c = lax.axis_index("x"), lax.axis_index("y"), lax.axis_index("c")
        me, sibling = (x, y, c), (x, y, 1 - c)
        chips = [(1 - x, y), (x, 1 - y), (1 - x, 1 - y)]
        bufs = (gin_ref, gsm_ref)

        def idx(px, py, pc):
            return 4 * px + 2 * py + pc

        rows_in = w_in_s.shape[0] // 8
        for r in range(8):
            gin_ref[idx(*me), r * rows_in : (r + 1) * rows_in, :] = win_ref[r * rows_in : (r + 1) * rows_in, :].astype(MXU_DTYPE)
        gsm_ref[idx(*me)] = sm_ref[...]

        def copy(a, k, block, to):
            blk = bufs[a].at[idx(*block)]
            return pltpu.make_async_remote_copy(src_ref=blk, dst_ref=blk, send_sem=send_sems.at[7 * a + k],
                                                recv_sem=recv_sems.at[7 * a + k], device_id=to, device_id_type=_MESH)

        first, passed = [], []
        for a in range(N_GATHERED):
            first.append(copy(a, 0, me, sibling))
            first += [copy(a, 1 + j, me, (*chip, c)) for j, chip in enumerate(chips)]
        for cp in first:
            cp.start()
        for j, chip in enumerate(chips):
            for a in range(N_GATHERED):
                copy(a, 1 + j, (*chip, c), me).wait_recv()
                cp = copy(a, 4 + j, (*chip, c), sibling)
                cp.start()
                passed.append(cp)
        for a in range(N_GATHERED):
            copy(a, 0, sibling, me).wait_recv()
            for j, chip in enumerate(chips):
                copy(a, 4 + j, (*chip, 1 - c), me).wait_recv()
        for cp in first + passed:
            cp.wait_send()

    vm = pl.BlockSpec(memory_space=pltpu.VMEM)
    return pl.pallas_call(
        body,
        out_shape=tuple(jax.ShapeDtypeStruct((N_DEV,) + s, d) for s, d in zip(shapes, dtypes)),
        in_specs=[vm, vm],
        out_specs=(vm, vm),
        scratch_shapes=[pltpu.SemaphoreType.DMA((7 * N_GATHERED,)), pltpu.SemaphoreType.DMA((7 * N_GATHERED,))],
        compiler_params=pltpu.CompilerParams(vmem_limit_bytes=VMEM_LIMIT),
        name="gather_weights",
    )(w_in_s, sm_s)


N_CHIP = 4
_EFFECT = pltpu.SideEffectType.DATAFLOW_SIDE_EFFECTING
_HBM_SPEC = pl.BlockSpec(memory_space=pltpu.HBM)
_SEM_SPEC = pl.BlockSpec(memory_space=pltpu.SEMAPHORE)


def _relation_peer(k, x, y, c):
    px = (1 - x) if (k & 4) else x
    py = (1 - y) if (k & 2) else y
    pc = (1 - c) if (k & 1) else c
    return (px, py, pc), 4 * px + 2 * py + pc


def _copies_to_all(src_ref, land_ref, send_sems, recv_sems, blocked, arrival):
    x, y, c = lax.axis_index("x"), lax.axis_index("y"), lax.axis_index("c")
    me = 4 * x + 2 * y + c
    out = []
    for k in range(1, N_DEV):
        dev, p = _relation_peer(k, x, y, c)
        out.append(pltpu.make_async_remote_copy(
            src_ref=src_ref.at[p] if blocked else src_ref, dst_ref=land_ref.at[p if arrival else me],
            send_sem=send_sems.at[k - 1], recv_sem=recv_sems.at[k - 1], device_id=dev, device_id_type=_MESH))
    return out


def _exchange_start(src, blocked, name):
    land = lax.empty((N_DEV,) + (src.shape[1:] if blocked else src.shape), src.dtype)

    def body(src_ref, land_ref, send_sems, recv_sems, src_thru, land_thru, token):
        for cp in _copies_to_all(src_ref, land_ref, send_sems, recv_sems, blocked, False):
            cp.start()
        token[...] = jnp.zeros_like(token)

    return pl.pallas_call(
        body,
        name=name,
        out_shape=(pltpu.SemaphoreType.DMA((N_DEV - 1,)), pltpu.SemaphoreType.DMA((N_DEV - 1,)), pltpu.HBM(src.shape, src.dtype),
                   pltpu.HBM(land.shape, land.dtype), jax.ShapeDtypeStruct((8, LANES), f32)),
        in_specs=(_HBM_SPEC, _HBM_SPEC),
        out_specs=(_SEM_SPEC, _SEM_SPEC, _HBM_SPEC, _HBM_SPEC, pl.BlockSpec(memory_space=pltpu.VMEM)),
        input_output_aliases={0: 2, 1: 3},
        compiler_params=pltpu.CompilerParams(has_side_effects=_EFFECT),
    )(pltpu.with_memory_space_constraint(src, pltpu.HBM), pltpu.with_memory_space_constraint(land, pltpu.HBM))


def _exchange_wait(started, after, blocked, name):
    send_sems, recv_sems, src_thru, land_thru, _ = started

    def body(src_ref, land_ref, send_sems, recv_sems, after_ref, src_dead, got_ref):
        for cp in _copies_to_all(src_ref, land_ref, send_sems, recv_sems, blocked, True):
            cp.wait_send()
            cp.wait_recv()

    return pl.pallas_call(
        body,
        name=name,
        out_shape=(pltpu.HBM(src_thru.shape, src_thru.dtype), pltpu.HBM(land_thru.shape, land_thru.dtype)),
        in_specs=(_HBM_SPEC, _HBM_SPEC, _SEM_SPEC, _SEM_SPEC, pl.BlockSpec(memory_space=pl.ANY)),
        out_specs=(_HBM_SPEC, _HBM_SPEC),
        input_output_aliases={0: 0, 1: 1},
        compiler_params=pltpu.CompilerParams(has_side_effects=_EFFECT),
    )(src_thru, land_thru, send_sems, recv_sems, after)


def _exchange_sibling(g_in):
    def body(in_ref, out_ref, send_sems, recv_sems):
        x, y, c = lax.axis_index("x"), lax.axis_index("y"), lax.axis_index("c")

        def copy(k):
            return pltpu.make_async_remote_copy(src_ref=in_ref.at[2 * k + 1 - c], dst_ref=out_ref.at[k], send_sem=send_sems.at[k],
                                                recv_sem=recv_sems.at[k], device_id=(x, y, 1 - c), device_id_type=_MESH)

        copies = [copy(k) for k in range(N_CHIP)]
        for cp in copies:
            cp.start()
        for cp in copies:
            cp.wait_recv()
        for cp in copies:
            cp.wait_send()

    return pl.pallas_call(
        body,
        out_shape=jax.ShapeDtypeStruct((N_CHIP,) + g_in.shape[1:], g_in.dtype),
        in_specs=[_HBM_SPEC],
        out_specs=_HBM_SPEC,
        scratch_shapes=[pltpu.SemaphoreType.DMA((N_CHIP,)), pltpu.SemaphoreType.DMA((N_CHIP,))],
        name="exchange_sibling",
    )(g_in)


def _pair_sum(g, r, name, tb=128):
    _, R, Cc = g.shape

    def body(g_ref, r_ref, o_ref):
        o_ref[...] = (g_ref[...].astype(f32) + r_ref[...].astype(f32)).astype(o_ref.dtype)

    return pl.pallas_call(
        body,
        out_shape=jax.ShapeDtypeStruct(r.shape, r.dtype),
        grid=(N_CHIP, R // tb),
        in_specs=[pl.BlockSpec((1, tb, Cc), lambda k, i: (2 * k + lax.axis_index("c"), i, 0)), pl.BlockSpec((1, tb, Cc), lambda k, i: (k, i, 0))],
        out_specs=pl.BlockSpec((1, tb, Cc), lambda k, i: (k, i, 0)),
        compiler_params=_cparams(("parallel", "parallel")),
        name=name,
    )(g, r)


def _exchange_grads(q_in, g_sm, g_rep):
    srcs = (q_in, g_sm, g_rep)

    def body(in0, in1, in2, out0, out1, out2, send_sems, recv_sems, local_sems):
        x, y, c = lax.axis_index("x"), lax.axis_index("y"), lax.axis_index("c")
        me, my_chip = 4 * x + 2 * y + c, 2 * x + y
        ins, outs = (in0, in1, in2), (out0, out1, out2)

        def slots(a, k):
            (px, py, pc), p = _relation_peer(k, x, y, c)
            if a == 0:
                return ins[0].at[2 * px + py], my_chip, 2 * px + py
            return (ins[a].at[p] if a == 1 else ins[a]), me, p

        def copy(a, k, arrival):
            src, there, here = slots(a, k)
            sem = (7 * (a - 1) + k - 1) if a >= 1 else (14 + k // 2 - 1)
            return pltpu.make_async_remote_copy(src_ref=src, dst_ref=outs[a].at[here if arrival else there], send_sem=send_sems.at[sem],
                                                recv_sem=recv_sems.at[sem], device_id=_relation_peer(k, x, y, c)[0], device_id_type=_MESH)

        pairs = [(0, k) for k in (2, 4, 6)] + [(a, k) for a in (1, 2) for k in range(1, 8)]
        sends = [copy(a, k, False) for a, k in pairs]
        for cp in sends:
            cp.start()
        own = (ins[0].at[my_chip], ins[1].at[me], ins[2])
        own_dst = (outs[0].at[my_chip], outs[1].at[me], outs[2].at[me])
        local = [pltpu.make_async_copy(own[a], own_dst[a], local_sems.at[a]) for a in range(3)]
        for cp in local:
            cp.start()
        for a, k in pairs:
            copy(a, k, True).wait_recv()
        for cp in sends:
            cp.wait_send()
        for cp in local:
            cp.wait()

    out_shape = tuple(jax.ShapeDtypeStruct(s.shape if i < 2 else (N_DEV,) + s.shape, s.dtype) for i, s in enumerate(srcs))
    return pl.pallas_call(
        body,
        out_shape=out_shape,
        in_specs=[_HBM_SPEC] * 3,
        out_specs=(_HBM_SPEC,) * 3,
        scratch_shapes=[pltpu.SemaphoreType.DMA((17,)), pltpu.SemaphoreType.DMA((17,)), pltpu.SemaphoreType.DMA((3,))],
        name="exchange_grads",
    )(*srcs)


def _sum_adam(parts, w, m, v, name, tb):
    R, Cc = w.shape
    n_parts = parts.shape[0]
    tb = min(tb, R)
    assert R % tb == 0
    bc1 = 1.0 - ADAM_B1**ADAM_STEP
    bc2 = 1.0 - ADAM_B2**ADAM_STEP

    def body(p_ref, w_ref, m_ref, v_ref, g_ref, d_ref, nm_ref, nv_ref):
        g = p_ref[0].astype(f32)
        for j in range(1, n_parts):
            g = g + p_ref[j].astype(f32)
        g_ref[...] = g
        m2 = ADAM_B1 * m_ref[...] + (1.0 - ADAM_B1) * g
        v2 = ADAM_B2 * v_ref[...] + (1.0 - ADAM_B2) * (g * g)
        nm_ref[...] = m2
        nv_ref[...] = v2
        d_ref[...] = -ADAM_LR * ((m2 / bc1) / (jnp.sqrt(v2 / bc2) + ADAM_EPS) + ADAM_WD * w_ref[...])

    blk = pl.BlockSpec((tb, Cc), lambda i: (i, 0))
    o = jax.ShapeDtypeStruct((R, Cc), f32)
    return pl.pallas_call(
        body,
        out_shape=(o, o, o, o),
        grid=(R // tb,),
        in_specs=[pl.BlockSpec((n_parts, tb, Cc), lambda i: (0, i, 0)), blk, blk, blk],
        out_specs=(blk, blk, blk, blk),
        compiler_params=_cparams(("parallel",)),
        name=name,
    )(parts, w, m, v)


_WEIGHTS = ["ln_pre_w", "w_in", "conv_w", "a_log_fwd", "a_log_bwd", "dt_bias_fwd", "dt_bias_bwd", "gdn_norm_w", "w_proj_gdn",
            "gk_w2_fwd", "gk_b2_fwd", "gk_w2_bwd", "gk_b2_bwd", "gla_norm_w", "w_proj_gla", "w_out", "ln_post_w"]
_REP_ORDER = ["ln_pre_w", "ln_post_w", "gk_b2_fwd", "gk_b2_bwd", "gla_norm_w", "gdn_norm_w", "a_log_fwd", "a_log_bwd", "dt_bias_fwd", "dt_bias_bwd"]


def kernel(x, ln_pre_w, w_in, conv_w, a_log_fwd, a_log_bwd, dt_bias_fwd, dt_bias_bwd, gdn_norm_w, w_proj_gdn, gk_w2_fwd, gk_b2_fwd, gk_w2_bwd, gk_b2_bwd, gla_norm_w, w_proj_gla, w_out, ln_post_w, loss_target, m_ln_pre_w, m_w_in, m_conv_w, m_a_log_fwd, m_a_log_bwd, m_dt_bias_fwd, m_dt_bias_bwd, m_gdn_norm_w, m_w_proj_gdn, m_gk_w2_fwd, m_gk_b2_fwd, m_gk_w2_bwd, m_gk_b2_bwd, m_gla_norm_w, m_w_proj_gla, m_w_out, m_ln_post_w, v_ln_pre_w, v_w_in, v_conv_w, v_a_log_fwd, v_a_log_bwd, v_dt_bias_fwd, v_dt_bias_bwd, v_gdn_norm_w, v_w_proj_gdn, v_gk_w2_fwd, v_gk_b2_fwd, v_gk_w2_bwd, v_gk_b2_bwd, v_gla_norm_w, v_w_proj_gla, v_w_out, v_ln_post_w):
    args = locals()
    W = {n: args[n] for n in _WEIGHTS}
    M = {n: args["m_" + n] for n in _WEIGHTS}
    V = {n: args["v_" + n] for n in _WEIGHTS}

    wp_stack = lambda P: jnp.concatenate([P["w_proj_gdn"], P["w_proj_gla"], P["w_out"]], axis=0)
    sm_pack = lambda P: _pack_small_shard(P["conv_w"][0], P["gk_w2_fwd"][0], P["gk_w2_bwd"][0])
    me = 4 * lax.axis_index("x") + 2 * lax.axis_index("y") + lax.axis_index("c")
    g_in, g_sm = _gather_weights(w_in[0], sm_pack(W))
    wp_own = wp_stack(W).astype(MXU_DTYPE)
    wp_started = _exchange_start(wp_own, False, "gather_proj_start")

    def proj_weights(after):
        own, land = _exchange_wait(wp_started, after, False, "gather_proj_wait")
        g_p = lax.dynamic_update_slice(land, own[None], (me, 0, 0, 0))
        return tuple(g_p[:, i].reshape(D_MODEL, D_MODEL) for i in range(3))

    rows_p = 3 * D_MODEL // N_DEV
    sent = {}

    def send_proj_grads(d_wpg, d_wpl, d_wout):
        p_p = jnp.stack([d_wpg, d_wpl, d_wout]).reshape(3, N_DEV, D_MODEL // N_DEV, D_MODEL).transpose(1, 0, 2, 3)
        sent["p_p"] = p_p.reshape(N_DEV, rows_p, D_MODEL)
        sent["started"] = _exchange_start(sent["p_p"], True, "exchange_proj_start")
        return sent["started"][4][0:1, 0:1]

    wcat = _to_cat(g_in.transpose(1, 0, 2).reshape(D_MODEL, N_IN))
    conv_full = g_sm[:, 0:CONV_K, :].transpose(1, 0, 2).reshape(CONV_K, N_DEV * SM_COLS)
    w2f_full = g_sm[:, 8:24, 0:64].transpose(1, 0, 2).reshape(16, 512)
    w2b_full = g_sm[:, 8:24, 64:128].transpose(1, 0, 2).reshape(16, 512)

    avec = _lane_row(a_log_fwd, a_log_bwd)
    dvec = _lane_row(dt_bias_fwd, dt_bias_bwd)
    G = _local_step(x[0], loss_target[0], wcat, conv_full, avec, dvec, gdn_norm_w, _pad_w2(w2f_full, 32), _pad_w2(w2b_full, 48),
                    gk_b2_fwd, gk_b2_bwd, gla_norm_w, proj_weights, ln_pre_w + wp_started[4][0:1, 0:1], ln_post_w, send_proj_grads)
    loss = lax.psum(0.5 * jnp.sum(G["loss_row"]) / D_MODEL, ("x", "y", "c"))

    p_in = _from_cat(G["d_wcat"]).reshape(D_MODEL, N_DEV, SHARD_IN).transpose(1, 0, 2).astype(MXU_DTYPE)
    p_sm = _pack_small_shard(G["d_conv"].reshape(CONV_K, N_DEV, SM_COLS).transpose(1, 0, 2),
                             G["d_w2f"][32:48].reshape(16, N_DEV, 64).transpose(1, 0, 2),
                             G["d_w2b"][48:64].reshape(16, N_DEV, 64).transpose(1, 0, 2))
    p_rep = _pack_rep(G["d_wpre"], G["d_wpost"], G["d_b2f"], G["d_b2b"], G["d_gla_norm"], G["d_gdn_norm"],
                      G["d_alog"][:, 0:8], G["d_alog"][:, 8:16], G["d_dtb"][:, 0:8], G["d_dtb"][:, 8:16])
    q_in = _pair_sum(p_in, _exchange_sibling(p_in), "pair_sum_w_in")
    r_in, r_sm, r_rep = _exchange_grads(q_in, p_sm, p_rep)
    p_p, land_p = _exchange_wait(sent["started"], r_in, True, "exchange_proj_wait")
    r_p = lax.dynamic_update_slice(land_p, lax.dynamic_index_in_dim(p_p, me, 0, keepdims=True), (me, 0, 0))

    o_in = _sum_adam(r_in, w_in[0], m_w_in[0], v_w_in[0], "adam_w_in", 128)
    o_p = _sum_adam(r_p, wp_stack(W).reshape(rows_p, D_MODEL), wp_stack(M).reshape(rows_p, D_MODEL),
                    wp_stack(V).reshape(rows_p, D_MODEL), "adam_w_proj", 128)
    o_sm = _sum_adam(r_sm, sm_pack(W), sm_pack(M), sm_pack(V), "adam_small", SM_ROWS)
    rep_pack = lambda P: _pack_rep(*[P[n] for n in _REP_ORDER])
    o_rep = _sum_adam(r_rep, rep_pack(W), rep_pack(M), rep_pack(V), "adam_rep", REP_ROWS)

    res = []
    for kind in range(4):
        rep = _unpack_rep(o_rep[kind])
        per = dict(rep)
        per["w_in"] = o_in[kind][None]
        pp = o_p[kind].reshape(3, 1, D_MODEL // N_DEV, D_MODEL)
        per["w_proj_gdn"], per["w_proj_gla"], per["w_out"] = pp[0], pp[1], pp[2]
        per["conv_w"] = o_sm[kind][None, 0:CONV_K, :]
        per["gk_w2_fwd"] = o_sm[kind][None, 8:24, 0:64]
        per["gk_w2_bwd"] = o_sm[kind][None, 8:24, 64:128]
        res.append([per[n] for n in _WEIGHTS])
    return (loss, G["grad_x"][None], *res[0], *res[1], *res[2], *res[3])
```

```python
import functools
import math

import jax
import jax.numpy as jnp
from jax import lax
from jax.experimental import pallas as pl
from jax.experimental.pallas import tpu as pltpu

f32 = jnp.float32
MXU_DTYPE = jnp.bfloat16
HI = lax.Precision.HIGHEST

D_MODEL = 1024
CHUNK = 64
NORM_EPS = 1e-6
GDN_HEADS, GDN_DIM = 8, 128
GLA_HEADS, GLA_DK, GLA_DV = 4, 128, 256
GATE_NORMALIZER = 16.0
CONV_K = 5
N_IN = 9280
N_DEV = 8
SHARD_IN = N_IN // N_DEV

N_CAT = 9728
SMALL_OFF = 9216
LANES = 128

ADAM_LR, ADAM_B1, ADAM_B2, ADAM_EPS, ADAM_WD, ADAM_STEP = 0.001, 0.9, 0.999, 1e-08, 0.01, 10

VMEM_LIMIT = 56 * 1024 * 1024


def _cparams(sem=None):
    return pltpu.CompilerParams(dimension_semantics=sem, vmem_limit_bytes=VMEM_LIMIT)


_DN = {"nn": (((1,), (0,)), ((), ())), "nt": (((1,), (1,)), ((), ())), "tn": (((0,), (0,)), ((), ()))}
_BDN = {"nn": (((2,), (1,)), ((0,), (0,))), "nt": (((2,), (2,)), ((0,), (0,))), "tn": (((1,), (1,)), ((0,), (0,)))}


def _dot16(a, b, kind):
    return lax.dot_general(a, b, (_DN if a.ndim == 2 else _BDN)[kind], preferred_element_type=f32)


def _mx(a, b, kind):
    return _dot16(a.astype(MXU_DTYPE), b.astype(MXU_DTYPE), kind)


def _split2(a):
    hi = a.astype(jnp.bfloat16)
    return hi, (a - hi.astype(f32)).astype(jnp.bfloat16)


def _split3(a):
    p1 = a.astype(jnp.bfloat16)
    r = a - p1.astype(f32)
    p2 = r.astype(jnp.bfloat16)
    return p1, p2, (r - p2.astype(f32)).astype(jnp.bfloat16)


def _h3s(a2, b2, kind):
    (ah, al), (bh, bl) = a2, b2
    return _dot16(ah, bh, kind) + (_dot16(ah, bl, kind) + _dot16(al, bh, kind))


def _h3(a, b, kind):
    return _h3s(_split2(a), _split2(b), kind)


def _exact01(t01, x, kind="nn"):
    t = t01.astype(jnp.bfloat16)
    x1, x2, x3 = _split3(x)
    return _dot16(t, x1, kind) + (_dot16(t, x2, kind) + _dot16(t, x3, kind))


def _colsum_col(e):
    ones = jnp.ones(e.shape[:-1] + (LANES,), jnp.bfloat16)
    e1, e2, e3 = _split3(e)
    return (_dot16(e1, ones, "tn") + (_dot16(e2, ones, "tn") + _dot16(e3, ones, "tn")))[..., 0:1]


def _iota3(shape, axis):
    return lax.broadcasted_iota(jnp.int32, shape, axis)


def _dir_masks(B, nf, Cn):
    shp = (B, Cn, Cn)
    bb, ii, jj = _iota3(shp, 0), _iota3(shp, 1), _iota3(shp, 2)
    fwd = bb < nf
    rev = jnp.logical_not(fwd)
    incl = jnp.logical_or(jnp.logical_and(fwd, ii >= jj), jnp.logical_and(rev, ii <= jj))
    strict = jnp.logical_and(incl, ii != jj)
    return incl, strict, ii == jj


def _last_row(x, nf):
    B, Cn = x.shape[0], x.shape[1]
    fwd = _iota3((B, 1, 1), 0) < nf
    return jnp.where(fwd, x[:, Cn - 1 : Cn, :], x[:, 0:1, :])


def _add_at_last_row(x, val, nf):
    B, Cn = x.shape[0], x.shape[1]
    bb, rr = _iota3((B, Cn, 1), 0), _iota3((B, Cn, 1), 1)
    at = jnp.logical_or(jnp.logical_and(bb < nf, rr == Cn - 1), jnp.logical_and(bb >= nf, rr == 0))
    return x + jnp.where(at, val, 0.0)


def _sigmoid(x):
    return jax.nn.sigmoid(x)


def _silu(x):
    return x * _sigmoid(x)


def _silu_grad(x):
    s = _sigmoid(x)
    return s * (1.0 + x * (1.0 - s))


def _softplus(x):
    u = jnp.exp(-jnp.abs(x))
    l1p = jnp.where(u < 1e-3, u * (1.0 - u * (0.5 - u * (1.0 / 3.0))), jnp.log(1.0 + u))
    return jnp.maximum(x, 0.0) + l1p


def _iota2(shape, axis):
    return lax.broadcasted_iota(jnp.int32, shape, axis)


def _matmul(a, b, kind, out_dtype, name, tm, tn, tk):
    if kind == "nn":
        (M, K), N = a.shape, b.shape[1]
    elif kind == "nt":
        (M, K), N = a.shape, b.shape[0]
    else:
        (K, M), N = a.shape, b.shape[1]
    tm, tn, tk = min(tm, M), min(tn, N), min(tk, K)
    assert M % tm == 0 and N % tn == 0 and K % tk == 0, (name, M, N, K)
    nk = K // tk
    if kind == "tn":
        a_spec = pl.BlockSpec((tk, tm), lambda i, j, k: (k, i))
    else:
        a_spec = pl.BlockSpec((tm, tk), lambda i, j, k: (i, k))
    if kind == "nt":
        b_spec = pl.BlockSpec((tn, tk), lambda i, j, k: (j, k))
    else:
        b_spec = pl.BlockSpec((tk, tn), lambda i, j, k: (k, j))

    def body(a_ref, b_ref, o_ref, *acc):
        p = _mx(a_ref[...], b_ref[...], kind)
        if nk == 1:
            o_ref[...] = p.astype(out_dtype)
        else:
            acc_ref = acc[0]
            k = pl.program_id(2)

            @pl.when(k == 0)
            def _():
                acc_ref[...] = p

            @pl.when(k > 0)
            def _():
                acc_ref[...] += p

            @pl.when(k == nk - 1)
            def _():
                o_ref[...] = acc_ref[...].astype(out_dtype)

    return pl.pallas_call(
        body,
        out_shape=jax.ShapeDtypeStruct((M, N), out_dtype),
        grid=(M // tm, N // tn, nk),
        in_specs=[a_spec, b_spec],
        out_specs=pl.BlockSpec((tm, tn), lambda i, j, k: (i, j)),
        scratch_shapes=[] if nk == 1 else [pltpu.VMEM((tm, tn), f32)],
        compiler_params=_cparams(("parallel", "parallel", "arbitrary")),
        name=name,
    )(a, b)


def _rms_pre(x, w, tb=256):
    T = x.shape[0]

    def body(x_ref, w_ref, h_ref, ht_ref):
        xv = x_ref[...]
        r = lax.rsqrt(jnp.mean(xv * xv, axis=1, keepdims=True) + NORM_EPS)
        h = xv * r * w_ref[...]
        h_ref[...] = h.astype(h_ref.dtype)
        ht_ref[...] = h.T.astype(ht_ref.dtype)

    return pl.pallas_call(
        body,
        out_shape=(jax.ShapeDtypeStruct((T, D_MODEL), MXU_DTYPE), jax.ShapeDtypeStruct((D_MODEL, T), MXU_DTYPE)),
        grid=(T // tb,),
        in_specs=[pl.BlockSpec((tb, D_MODEL), lambda i: (i, 0)), pl.BlockSpec((1, D_MODEL), lambda i: (0, 0))],
        out_specs=(pl.BlockSpec((tb, D_MODEL), lambda i: (i, 0)), pl.BlockSpec((D_MODEL, tb), lambda i: (0, i))),
        compiler_params=_cparams(("parallel",)),
        name="rms_pre",
    )(x, w)


def _rms_pre_bwd(x, w, dh, dy, tb=256):
    T = x.shape[0]

    def body(x_ref, w_ref, dh_ref, dy_ref, dx_ref, dw_ref):
        i = pl.program_id(0)
        xv, dhv = x_ref[...], dh_ref[...]
        r = lax.rsqrt(jnp.mean(xv * xv, axis=1, keepdims=True) + NORM_EPS)
        dhw = dhv * w_ref[...]
        dx_ref[...] = dy_ref[...] + r * dhw - xv * (r * r * r) * jnp.mean(dhw * xv, axis=1, keepdims=True)
        part = jnp.sum(dhv * xv * r, axis=0, keepdims=True)

        @pl.when(i == 0)
        def _():
            dw_ref[...] = part

        @pl.when(i > 0)
        def _():
            dw_ref[...] += part

    blk = pl.BlockSpec((tb, D_MODEL), lambda i: (i, 0))
    row = pl.BlockSpec((1, D_MODEL), lambda i: (0, 0))
    return pl.pallas_call(
        body,
        out_shape=(jax.ShapeDtypeStruct((T, D_MODEL), f32), jax.ShapeDtypeStruct((1, D_MODEL), f32)),
        grid=(T // tb,),
        in_specs=[blk, row, blk, blk],
        out_specs=(blk, row),
        compiler_params=_cparams(("arbitrary",)),
        name="rms_pre_bwd",
    )(x, w, dh, dy)


def _conv5(u, cw):
    T = u.shape[0]
    t = _iota2(u.shape, 0)
    acc = u * cw[2:3, :]
    for j in (0, 1, 3, 4):
        d = j - 2
        sh = pltpu.roll(u, (-d) % T, 0)
        valid = jnp.logical_and(t + d >= 0, t + d < T)
        acc = acc + jnp.where(valid, sh, 0.0) * cw[j : j + 1, :]
    return acc


def _gdn_prep(proj, conv_w):
    T = proj.shape[0]
    nh = GDN_HEADS

    def body(u_ref, cw_ref, o_ref):
        j = pl.program_id(0)
        s = _silu(_conv5(u_ref[...], cw_ref[...]))

        @pl.when(j < 2 * nh)
        def _():
            o_ref[0] = s * lax.rsqrt(jnp.sum(s * s, axis=1, keepdims=True) + NORM_EPS)

        @pl.when(j >= 2 * nh)
        def _():
            o_ref[0] = s

    return pl.pallas_call(
        body,
        out_shape=jax.ShapeDtypeStruct((3, T, GDN_HEADS * GDN_DIM), f32),
        grid=(3 * nh,),
        in_specs=[pl.BlockSpec((T, GDN_DIM), lambda j: (0, j)), pl.BlockSpec((CONV_K, GDN_DIM), lambda j: (0, j))],
        out_specs=pl.BlockSpec((1, T, GDN_DIM), lambda j: (j // nh, 0, j % nh)),
        compiler_params=_cparams(("parallel",)),
        name="gdn_prep",
    )(proj, conv_w)


def _gdn_prep_bwd(proj, conv_w, dqkv_f, dqkv_b, dproj):
    T = proj.shape[0]
    nh = GDN_HEADS

    def body(u_ref, cw_ref, df_ref, db_ref, _, du_ref, dcw_ref):
        j = pl.program_id(0)
        u, cw = u_ref[...], cw_ref[...]
        c = _conv5(u, cw)
        s = _silu(c)
        dn = df_ref[0] + db_ref[0]
        rinv = lax.rsqrt(jnp.sum(s * s, axis=1, keepdims=True) + NORM_EPS)
        ds_norm = rinv * dn - s * (rinv * rinv * rinv) * jnp.sum(dn * s, axis=1, keepdims=True)
        ds = jnp.where(j < 2 * nh, ds_norm, dn)
        dc = ds * _silu_grad(c)
        t = _iota2(u.shape, 0)
        du = dc * cw[2:3, :]
        rows = []
        for jj in range(CONV_K):
            d = jj - 2
            if d == 0:
                rows.append(jnp.sum(dc * u, axis=0, keepdims=True))
                continue
            ush = pltpu.roll(u, (-d) % T, 0)
            uvalid = jnp.logical_and(t + d >= 0, t + d < T)
            rows.append(jnp.sum(dc * jnp.where(uvalid, ush, 0.0), axis=0, keepdims=True))
            dsh = pltpu.roll(dc, d % T, 0)
            dvalid = jnp.logical_and(t - d >= 0, t - d < T)
            du = du + jnp.where(dvalid, dsh, 0.0) * cw[jj : jj + 1, :]
        du_ref[...] = du.astype(du_ref.dtype)
        for jj in range(CONV_K):
            dcw_ref[jj : jj + 1, :] = rows[jj]

    qspec = pl.BlockSpec((1, T, GDN_DIM), lambda j: (j // nh, 0, j % nh))
    return pl.pallas_call(
        body,
        out_shape=(jax.ShapeDtypeStruct(dproj.shape, dproj.dtype), jax.ShapeDtypeStruct((CONV_K, 3 * nh * GDN_DIM), f32)),
        grid=(3 * nh,),
        in_specs=[pl.BlockSpec((T, GDN_DIM), lambda j: (0, j)), pl.BlockSpec((CONV_K, GDN_DIM), lambda j: (0, j)), qspec, qspec, _ANY_SPEC],
        out_specs=(pl.BlockSpec((T, GDN_DIM), lambda j: (0, j)), pl.BlockSpec((CONV_K, GDN_DIM), lambda j: (0, j))),
        input_output_aliases={4: 0},
        compiler_params=_cparams(("parallel",)),
        name="gdn_prep_bwd",
    )(proj, conv_w, dqkv_f, dqkv_b, dproj)


def _chunk_tri(n, rev):
    i, j = _iota2((n, n), 0), _iota2((n, n), 1)
    same = jnp.right_shift(i, 6) == jnp.right_shift(j, 6)
    order = (j >= i) if rev else (j <= i)
    return jnp.where(jnp.logical_and(same, order), 1.0, 0.0).astype(f32)


def _gdn_gates(proj, avec, dvec, tb=256):
    T = proj.shape[0]

    def body(s_ref, a_ref, d_ref, g_ref, gc_ref):
        small = s_ref[...]
        lane = _iota2(small.shape, 1)
        lg = -jnp.exp(a_ref[...]) * _softplus(small + d_ref[...])
        beta = _sigmoid(small)
        g_ref[...] = jnp.where(lane < 16, lg, jnp.where(lane < 32, beta, 0.0))
        lgm = jnp.where(lane < 16, lg, 0.0)
        gcf = _exact01(_chunk_tri(tb, False), lgm)
        gcr = _exact01(_chunk_tri(tb, True), lgm)
        gc_ref[...] = jnp.where(lane < 8, gcf, gcr)

    blk = pl.BlockSpec((tb, LANES), lambda i: (i, 0))
    row = pl.BlockSpec((1, LANES), lambda i: (0, 0))
    return pl.pallas_call(
        body,
        out_shape=(jax.ShapeDtypeStruct((T, LANES), f32), jax.ShapeDtypeStruct((T, LANES), f32)),
        grid=(T // tb,),
        in_specs=[pl.BlockSpec((tb, LANES), lambda i: (i, SMALL_OFF // LANES)), row, row],
        out_specs=(blk, blk),
        compiler_params=_cparams(("parallel",)),
        name="gdn_gates",
    )(proj, avec, dvec)


def _gdn_gates_bwd(proj, avec, dvec, dg_f, dg_b, dsmall_gla, dproj, tb=256):
    T = proj.shape[0]
    pad = N_CAT - SMALL_OFF

    def body(s_ref, a_ref, d_ref, gf_ref, gb_ref, dl_ref, _, ds_ref, da_ref, dd_ref):
        i = pl.program_id(0)
        small = s_ref[...]
        lane = _iota2(small.shape, 1)
        dgate = gf_ref[...] + gb_ref[...]
        z = small + d_ref[...]
        nega = -jnp.exp(a_ref[...])
        dz = dgate * nega * _sigmoid(z)
        beta = _sigmoid(small)
        dsm = jnp.where(lane < 16, dz, jnp.where(lane < 32, dgate * beta * (1.0 - beta), 0.0))
        ds_ref[:, :LANES] = (dsm + dl_ref[...]).astype(ds_ref.dtype)
        ds_ref[:, LANES:] = jnp.zeros((tb, pad - LANES), ds_ref.dtype)
        lg = nega * _softplus(z)
        pa = jnp.sum(jnp.where(lane < 16, dgate * lg, 0.0), axis=0, keepdims=True)
        pd = jnp.sum(jnp.where(lane < 16, dz, 0.0), axis=0, keepdims=True)

        @pl.when(i == 0)
        def _():
            da_ref[...] = pa
            dd_ref[...] = pd

        @pl.when(i > 0)
        def _():
            da_ref[...] += pa
            dd_ref[...] += pd

    blk = pl.BlockSpec((tb, LANES), lambda i: (i, 0))
    row = pl.BlockSpec((1, LANES), lambda i: (0, 0))
    return pl.pallas_call(
        body,
        out_shape=(jax.ShapeDtypeStruct(dproj.shape, dproj.dtype), jax.ShapeDtypeStruct((1, LANES), f32), jax.ShapeDtypeStruct((1, LANES), f32)),
        grid=(T // tb,),
        in_specs=[pl.BlockSpec((tb, LANES), lambda i: (i, SMALL_OFF // LANES)), row, row, blk, blk, blk, _ANY_SPEC],
        out_specs=(pl.BlockSpec((tb, pad), lambda i: (i, DP_SMALL_BLK)), row, row),
        input_output_aliases={6: 0},
        compiler_params=_cparams(("arbitrary",)),
        name="gdn_gates_bwd",
    )(proj, avec, dvec, dg_f, dg_b, dsmall_gla, dproj)


GDN_HB = 8


def _inv_unit_lower(L, eye):
    A = eye - L
    P2 = _split2(L)
    for _ in range(5):
        P2 = _split2(_h3s(P2, P2, "nn"))
        A = A + _h3s(_split2(A), P2, "nn")
    return A


def _lane_col(blk, idx):
    lane = _iota2(blk.shape, 1)
    return jnp.sum(jnp.where(lane == idx, blk, 0.0), axis=1, keepdims=True)


class _GdnChunk:
    def __init__(self, q, k, v, gcol, beta, rev, A=None):
        Cn = q.shape[0]
        ii, jj = _iota2((Cn, Cn), 0), _iota2((Cn, Cn), 1)
        self.incl = (ii <= jj) if rev else (ii >= jj)
        self.strict = (ii < jj) if rev else (ii > jj)
        grow = jnp.broadcast_to(gcol, (Cn, LANES)).T[0:1, :]
        diff = gcol - grow
        self.decay = jnp.where(self.incl, jnp.exp(jnp.where(self.incl, diff, 0.0)), 0.0)
        self.k, self.v, self.beta, self.gcol = k, v, beta, gcol
        self.qs = q * (GDN_DIM**-0.5)
        self.kb = k * beta
        self.L = jnp.where(self.strict, _mx(self.kb, k, "nt") * self.decay, 0.0)
        if A is None:
            A = _inv_unit_lower(self.L, jnp.where(ii == jj, 1.0, 0.0).astype(f32))
        self.A2 = _split2(A)
        self.A = A
        self.eg = jnp.exp(gcol)
        self.kbg = self.kb * self.eg
        self.sol = _h3s(self.A2, _split2(jnp.concatenate([v * beta, self.kbg], axis=1)), "nn")
        self.u = self.sol[:, :GDN_DIM]
        self.w = self.sol[:, GDN_DIM:]
        self.attn = jnp.where(self.incl, _mx(self.qs, k, "nt") * self.decay, 0.0)
        self.qd = self.qs * self.eg
        last = 0 if rev else Cn - 1
        self.last = last
        gl = gcol[last : last + 1, :]
        self.ekd = jnp.exp(gl - gcol)
        self.kd = k * self.ekd
        self.egl = jnp.exp(gl)


def _gdn_scan_fwd(qkvn, gates, gc):
    T = qkvn.shape[1]
    N = T // CHUNK
    nh, hb = GDN_HEADS, GDN_HB
    W = hb * GDN_DIM
    chains = [(d, hh) for d in range(2) for hh in range(hb)]

    def body(qf_ref, qb_ref, gf_ref, gb_ref, cf_ref, cb_ref, of_ref, ob_ref, sf_ref, sb_ref, af_ref, ab_ref, s_scr):
        n, b = pl.program_id(0), pl.program_id(1)

        @pl.when(n == 0)
        def _():
            for d, hh in chains:
                s_scr[d, b * hb + hh] = jnp.zeros((GDN_DIM, GDN_DIM), f32)

        dirs = ((qf_ref, gf_ref, cf_ref, of_ref, sf_ref, af_ref), (qb_ref, gb_ref, cb_ref, ob_ref, sb_ref, ab_ref))
        s_in = {ch: s_scr[ch[0], b * hb + ch[1]] for ch in chains}
        s_out = {}
        for d, hh in chains:
            q_ref, g_ref, c_ref, o_ref, st_ref, a_ref = dirs[d]
            h = b * hb + hh
            sl = slice(hh * GDN_DIM, (hh + 1) * GDN_DIM)
            gcol = _lane_col(c_ref[...], h + 8 * d)
            beta = _lane_col(g_ref[...], 16 + h + 8 * d)
            ck = _GdnChunk(q_ref[0, :, sl], q_ref[1, :, sl], q_ref[2, :, sl], gcol, beta, d == 1)
            S = s_in[d, hh]
            st_ref[0, hh] = S
            a_ref[0, hh] = ck.A
            v_new = ck.u - _mx(ck.w, S, "nn")
            o_ref[:, sl] = _mx(ck.qd, S, "nn") + _mx(ck.attn, v_new, "nn")
            s_out[d, hh] = S * ck.egl + _mx(ck.kd, v_new, "tn")
        for d, hh in chains:
            s_scr[d, b * hb + hh] = s_out[d, hh]

    fw, bw = (lambda n: n), (lambda n: N - 1 - n)
    qsp = lambda ci: pl.BlockSpec((3, CHUNK, W), lambda n, b: (0, ci(n), b))
    gsp = lambda ci: pl.BlockSpec((CHUNK, LANES), lambda n, b: (ci(n), 0))
    osp = lambda ci: pl.BlockSpec((CHUNK, W), lambda n, b: (ci(n), b))
    ssp = lambda ci: pl.BlockSpec((1, hb, GDN_DIM, GDN_DIM), lambda n, b: (ci(n), b, 0, 0))
    asp = lambda ci: pl.BlockSpec((1, hb, CHUNK, CHUNK), lambda n, b: (ci(n), b, 0, 0))
    o_shape = jax.ShapeDtypeStruct((T, nh * GDN_DIM), f32)
    s_shape = jax.ShapeDtypeStruct((N, nh, GDN_DIM, GDN_DIM), f32)
    a_shape = jax.ShapeDtypeStruct((N, nh, CHUNK, CHUNK), f32)
    return pl.pallas_call(
        body,
        out_shape=(o_shape, o_shape, s_shape, s_shape, a_shape, a_shape),
        grid=(N, nh // hb),
        in_specs=[qsp(fw), qsp(bw), gsp(fw), gsp(bw), gsp(fw), gsp(bw)],
        out_specs=(osp(fw), osp(bw), ssp(fw), ssp(bw), asp(fw), asp(bw)),
        scratch_shapes=[pltpu.VMEM((2, nh, GDN_DIM, GDN_DIM), f32)],
        compiler_params=_cparams(("arbitrary", "arbitrary")),
        name="gdn_scan_fwd",
    )(qkvn, qkvn, gates, gates, gc, gc)


def _gdn_scan_bwd(qkvn, gates, gc, st_f, st_b, a_f, a_b, do):
    T = qkvn.shape[1]
    N = T // CHUNK
    nh, hb = GDN_HEADS, GDN_HB
    nb = nh // hb
    W = hb * GDN_DIM
    chains = [(d, hh) for d in range(2) for hh in range(hb)]

    def body(qf_ref, qb_ref, gf_ref, gb_ref, cf_ref, cb_ref, sf_ref, sb_ref, af_ref, ab_ref, dof_ref, dob_ref,
             dqf_ref, dqb_ref, dgf_ref, dgb_ref, ds_scr):
        n, b = pl.program_id(0), pl.program_id(1)

        @pl.when(n == 0)
        def _():
            for d, hh in chains:
                ds_scr[d, b * hb + hh] = jnp.zeros((GDN_DIM, GDN_DIM), f32)

        @pl.when(b == 0)
        def _():
            dgf_ref[...] = jnp.zeros((CHUNK, LANES), f32)
            dgb_ref[...] = jnp.zeros((CHUNK, LANES), f32)

        dirs = ((qf_ref, gf_ref, cf_ref, sf_ref, af_ref, dof_ref, dqf_ref), (qb_ref, gb_ref, cb_ref, sb_ref, ab_ref, dob_ref, dqb_ref))
        ds_in = {ch: ds_scr[ch[0], b * hb + ch[1]] for ch in chains}
        ds_out = {}
        lane = _iota2((CHUNK, LANES), 1)
        acc = [dgf_ref[...], dgb_ref[...]]
        for d, hh in chains:
            q_ref, g_ref, c_ref, st_ref, a_ref, do_ref, dq_ref = dirs[d]
            h = b * hb + hh
            sl = slice(hh * GDN_DIM, (hh + 1) * GDN_DIM)
            gcol = _lane_col(c_ref[...], h + 8 * d)
            beta = _lane_col(g_ref[...], 16 + h + 8 * d)
            ck = _GdnChunk(q_ref[0, :, sl], q_ref[1, :, sl], q_ref[2, :, sl], gcol, beta, d == 1, A=a_ref[0, hh])
            k, v = ck.k, ck.v
            S = st_ref[0, hh]
            dS = ds_in[d, hh]
            dov = do_ref[:, sl]
            v_new = ck.u - _mx(ck.w, S, "nn")
            d_vnew = _mx(ck.attn, dov, "tn") + _mx(ck.kd, dS, "nn")
            d_attn = jnp.where(ck.incl, _mx(dov, v_new, "nt"), 0.0)
            d_qd = _mx(dov, S, "nt")
            d_kd = _mx(v_new, dS, "nt")
            d_gl = jnp.sum(jnp.sum(dS * S, axis=1, keepdims=True), axis=0, keepdims=True) * ck.egl
            ds_out[d, hh] = dS * ck.egl + _mx(ck.qd, dov, "tn") - _mx(ck.w, d_vnew, "tn")
            d_w = -_mx(d_vnew, S, "nt")
            d_rhs = _h3s(ck.A2, _split2(jnp.concatenate([d_vnew, d_w], axis=1)), "tn")
            d_vb, d_kbg = d_rhs[:, :GDN_DIM], d_rhs[:, GDN_DIM:]
            dL = -jnp.where(ck.strict, _h3(d_rhs, ck.sol, "nt"), 0.0)
            P = dL * ck.decay
            Q = d_attn * ck.decay
            d_kb = _mx(P, k, "nn") + d_kbg * ck.eg
            d_qs = _mx(Q, k, "nn") + d_qd * ck.eg
            d_k = _mx(P, ck.kb, "tn") + _mx(Q, ck.qs, "tn") + d_kd * ck.ekd + d_kb * ck.beta
            E = dL * ck.L + d_attn * ck.attn
            kdsum = jnp.sum(d_kd * ck.kd, axis=1, keepdims=True)
            d_gc = (jnp.sum(E, axis=1, keepdims=True) - _colsum_col(E)
                    + jnp.sum(d_kbg * ck.kbg, axis=1, keepdims=True)
                    + jnp.sum(d_qd * ck.qd, axis=1, keepdims=True) - kdsum)
            d_gl = d_gl + jnp.sum(kdsum, axis=0, keepdims=True)
            row = _iota2((CHUNK, 1), 0)
            d_gc = d_gc + jnp.where(row == ck.last, d_gl, 0.0)
            d_beta = jnp.sum(d_vb * v, axis=1, keepdims=True) + jnp.sum(d_kb * k, axis=1, keepdims=True)
            dq_ref[0, :, sl] = d_qs * (GDN_DIM**-0.5)
            dq_ref[1, :, sl] = d_k
            dq_ref[2, :, sl] = d_vb * ck.beta
            acc[d] = jnp.where(lane == h + 8 * d, d_gc, acc[d])
            acc[d] = jnp.where(lane == 16 + h + 8 * d, d_beta, acc[d])
        for d, hh in chains:
            ds_scr[d, b * hb + hh] = ds_out[d, hh]
        dgf_ref[...] = acc[0]
        dgb_ref[...] = acc[1]

        @pl.when(b == nb - 1)
        def _():
            for d, dg_ref in enumerate((dgf_ref, dgb_ref)):
                blk = dg_ref[...]
                dg_ref[...] = jnp.where(lane < 16, _exact01(_chunk_tri(CHUNK, d == 0), blk), blk)

    fw, bw = (lambda n: N - 1 - n), (lambda n: n)
    qsp = lambda ci: pl.BlockSpec((3, CHUNK, W), lambda n, b: (0, ci(n), b))
    gsp = lambda ci: pl.BlockSpec((CHUNK, LANES), lambda n, b: (ci(n), 0))
    osp = lambda ci: pl.BlockSpec((CHUNK, W), lambda n, b: (ci(n), b))
    ssp = lambda ci: pl.BlockSpec((1, hb, GDN_DIM, GDN_DIM), lambda n, b: (ci(n), b, 0, 0))
    asp = lambda ci: pl.BlockSpec((1, hb, CHUNK, CHUNK), lambda n, b: (ci(n), b, 0, 0))
    dq_shape = jax.ShapeDtypeStruct((3, T, nh * GDN_DIM), f32)
    dg_shape = jax.ShapeDtypeStruct((T, LANES), f32)
    return pl.pallas_call(
        body,
        out_shape=(dq_shape, dq_shape, dg_shape, dg_shape),
        grid=(N, nb),
        in_specs=[qsp(fw), qsp(bw), gsp(fw), gsp(bw), gsp(fw), gsp(bw), ssp(fw), ssp(bw), asp(fw), asp(bw), osp(fw), osp(bw)],
        out_specs=(qsp(fw), qsp(bw), gsp(fw), gsp(bw)),
        scratch_shapes=[pltpu.VMEM((2, nh, GDN_DIM, GDN_DIM), f32)],
        compiler_params=_cparams(("arbitrary", "arbitrary")),
        name="gdn_scan_bwd",
    )(qkvn, qkvn, gates, gates, gc, gc, st_f, st_b, a_f, a_b, do, do)


def _out_norm(o_f, o_b, proj, gate_blk, w, hd, name, tb=256):
    T, W = o_f.shape
    nh = W // hd

    def body(of_ref, ob_ref, z_ref, w_ref, y_ref):
        wv = w_ref[...]
        for i in range(nh):
            sl = slice(i * hd, (i + 1) * hd)
            o = of_ref[:, sl] + ob_ref[:, sl]
            r = lax.rsqrt(jnp.mean(o * o, axis=1, keepdims=True) + NORM_EPS)
            y_ref[:, sl] = (o * r * wv * _silu(z_ref[:, sl])).astype(y_ref.dtype)

    blk = pl.BlockSpec((tb, W), lambda i: (i, 0))
    return pl.pallas_call(
        body,
        out_shape=jax.ShapeDtypeStruct((T, W), MXU_DTYPE),
        grid=(T // tb,),
        in_specs=[blk, blk, pl.BlockSpec((tb, W), lambda i: (i, gate_blk)), pl.BlockSpec((1, hd), lambda i: (0, 0))],
        out_specs=blk,
        compiler_params=_cparams(("parallel",)),
        name=name,
    )(o_f, o_b, proj, w)


def _out_norm_bwd(dy, o_f, o_b, proj, gate_blk, w, hd, name, dproj, tb=256):
    T, W = o_f.shape
    nh = W // hd

    def body(dy_ref, of_ref, ob_ref, z_ref, w_ref, _, do_ref, dz_ref, dw_ref):
        i = pl.program_id(0)
        wv = w_ref[...]
        dw = jnp.zeros((1, hd), f32)
        for a in range(nh):
            sl = slice(a * hd, (a + 1) * hd)
            o = of_ref[:, sl] + ob_ref[:, sl]
            z = z_ref[:, sl]
            dyv = dy_ref[:, sl]
            r = lax.rsqrt(jnp.mean(o * o, axis=1, keepdims=True) + NORM_EPS)
            orr = o * r
            dn = dyv * _silu(z)
            dz_ref[:, sl] = (dyv * orr * wv * _silu_grad(z)).astype(dz_ref.dtype)
            dw = dw + jnp.sum(dn * orr, axis=0, keepdims=True)
            dnw = dn * wv
            do_ref[:, sl] = r * dnw - o * (r * r * r) * jnp.mean(dnw * o, axis=1, keepdims=True)

        @pl.when(i == 0)
        def _():
            dw_ref[...] = dw

        @pl.when(i > 0)
        def _():
            dw_ref[...] += dw

    blk = pl.BlockSpec((tb, W), lambda i: (i, 0))
    row = pl.BlockSpec((1, hd), lambda i: (0, 0))
    return pl.pallas_call(
        body,
        out_shape=(jax.ShapeDtypeStruct((T, W), f32), jax.ShapeDtypeStruct(dproj.shape, dproj.dtype), jax.ShapeDtypeStruct((1, hd), f32)),
        grid=(T // tb,),
        in_specs=[blk, blk, blk, pl.BlockSpec((tb, W), lambda i: (i, gate_blk)), row, _ANY_SPEC],
        out_specs=(blk, pl.BlockSpec((tb, W), lambda i: (i, gate_blk)), row),
        input_output_aliases={5: 1},
        compiler_params=_cparams(("arbitrary",)),
        name=name,
    )(dy, o_f, o_b, proj, w, dproj)


def _gla_prep(proj, w2f, w2b, b2f, b2b, tb=256):
    T = proj.shape[0]
    W = GLA_HEADS * GLA_DK

    def body(s_ref, wf_ref, wb_ref, bf_ref, bb_ref, gf_ref, gb_ref):
        small = s_ref[...]
        gkf = -_softplus(-(_mx(small, wf_ref[...], "nn") + bf_ref[...])) * (1.0 / GATE_NORMALIZER)
        gkb = -_softplus(-(_mx(small, wb_ref[...], "nn") + bb_ref[...])) * (1.0 / GATE_NORMALIZER)
        gf_ref[...] = _exact01(_chunk_tri(tb, False), gkf)
        gb_ref[...] = _exact01(_chunk_tri(tb, True), gkb)

    blk = pl.BlockSpec((tb, W), lambda i: (i, 0))
    wsp = pl.BlockSpec((LANES, W), lambda i: (0, 0))
    row = pl.BlockSpec((1, W), lambda i: (0, 0))
    return pl.pallas_call(
        body,
        out_shape=(jax.ShapeDtypeStruct((T, W), f32), jax.ShapeDtypeStruct((T, W), f32)),
        grid=(T // tb,),
        in_specs=[pl.BlockSpec((tb, LANES), lambda i: (i, SMALL_OFF // LANES)), wsp, wsp, row, row],
        out_specs=(blk, blk),
        compiler_params=_cparams(("parallel",)),
        name="gla_prep",
    )(proj, w2f, w2b, b2f, b2b)


def _gla_prep_bwd(proj, w2f, w2b, b2f, b2b, dgk_f, dgk_b, tb=256):
    T = proj.shape[0]
    W = GLA_HEADS * GLA_DK

    def body(s_ref, wf_ref, wb_ref, bf_ref, bb_ref, df_ref, db_ref, ds_ref, dwf_ref, dwb_ref, dbf_ref, dbb_ref):
        i = pl.program_id(0)
        small = s_ref[...]
        dsm = jnp.zeros((tb, LANES), f32)
        parts = []
        for w_ref, b_ref, d_ref in ((wf_ref, bf_ref, df_ref), (wb_ref, bb_ref, db_ref)):
            pre = _mx(small, w_ref[...], "nn") + b_ref[...]
            dpre = d_ref[...] * (1.0 / GATE_NORMALIZER) * _sigmoid(-pre)
            dsm = dsm + _mx(dpre, w_ref[...], "nt")
            parts.append((_mx(small, dpre, "tn"), jnp.sum(dpre, axis=0, keepdims=True)))
        ds_ref[...] = dsm

        @pl.when(i == 0)
        def _():
            dwf_ref[...], dbf_ref[...] = parts[0]
            dwb_ref[...], dbb_ref[...] = parts[1]

        @pl.when(i > 0)
        def _():
            dwf_ref[...] += parts[0][0]
            dbf_ref[...] += parts[0][1]
            dwb_ref[...] += parts[1][0]
            dbb_ref[...] += parts[1][1]

    blk = pl.BlockSpec((tb, W), lambda i: (i, 0))
    wsp = pl.BlockSpec((LANES, W), lambda i: (0, 0))
    row = pl.BlockSpec((1, W), lambda i: (0, 0))
    return pl.pallas_call(
        body,
        out_shape=(jax.ShapeDtypeStruct((T, LANES), f32), jax.ShapeDtypeStruct((LANES, W), f32), jax.ShapeDtypeStruct((LANES, W), f32),
                   jax.ShapeDtypeStruct((1, W), f32), jax.ShapeDtypeStruct((1, W), f32)),
        grid=(T // tb,),
        in_specs=[pl.BlockSpec((tb, LANES), lambda i: (i, SMALL_OFF // LANES)), wsp, wsp, row, row, blk, blk],
        out_specs=(pl.BlockSpec((tb, LANES), lambda i: (i, 0)), wsp, wsp, row, row),
        compiler_params=_cparams(("arbitrary",)),
        name="gla_prep_bwd",
    )(proj, w2f, w2b, b2f, b2b, dgk_f, dgk_b)


Q_B_BLK, K_B_BLK, V_B_BLK = 4096 // GLA_DK, 4608 // GLA_DK, 5120 // GLA_DV


class _GlaChunk:
    def __init__(self, q, k, G, rev):
        Cn = q.shape[0]
        ii, jj = _iota2((Cn, Cn), 0), _iota2((Cn, Cn), 1)
        self.incl = (ii <= jj) if rev else (ii >= jj)
        self.eG = jnp.exp(G)
        self.enG = jnp.exp(-G)
        self.qg = q * (GLA_DK**-0.5) * self.eG
        self.kg = k * self.enG
        self.attn = jnp.where(self.incl, _mx(self.qg, self.kg, "nt"), 0.0)
        last = 0 if rev else Cn - 1
        self.last = last
        gl = G[last : last + 1, :]
        self.ekd = jnp.exp(gl - G)
        self.kd = k * self.ekd
        self.egl = jnp.exp(gl)


def _gla_scan_fwd(proj, G_f, G_b):
    T = proj.shape[0]
    N = T // CHUNK
    nh = GLA_HEADS

    def body(qf_ref, kf_ref, vf_ref, gf_ref, qb_ref, kb_ref, vb_ref, gb_ref, of_ref, ob_ref, sf_ref, sb_ref, s_scr):
        n, h = pl.program_id(0), pl.program_id(1)

        @pl.when(n == 0)
        def _():
            s_scr[0, h] = jnp.zeros((GLA_DV, GLA_DK), f32)
            s_scr[1, h] = jnp.zeros((GLA_DV, GLA_DK), f32)

        for d, (q_ref, k_ref, v_ref, g_ref, o_ref, st_ref) in enumerate(
            ((qf_ref, kf_ref, vf_ref, gf_ref, of_ref, sf_ref), (qb_ref, kb_ref, vb_ref, gb_ref, ob_ref, sb_ref))
        ):
            ck = _GlaChunk(q_ref[...], k_ref[...], g_ref[...], d == 1)
            v = v_ref[...]
            St = s_scr[d, h]
            st_ref[0, 0] = St
            o_ref[...] = _mx(ck.qg, St, "nt") + _mx(ck.attn, v, "nn")
            s_scr[d, h] = St * ck.egl + _mx(v, ck.kd, "tn")

    def specs(cidx):
        return [
            pl.BlockSpec((CHUNK, GLA_DK), lambda n, h: (cidx(n), Q_B_BLK + h)),
            pl.BlockSpec((CHUNK, GLA_DK), lambda n, h: (cidx(n), K_B_BLK + h)),
            pl.BlockSpec((CHUNK, GLA_DV), lambda n, h: (cidx(n), V_B_BLK + h)),
            pl.BlockSpec((CHUNK, GLA_DK), lambda n, h: (cidx(n), h)),
        ]

    fw, bw = (lambda n: n), (lambda n: N - 1 - n)
    o_shape = jax.ShapeDtypeStruct((T, nh * GLA_DV), f32)
    s_shape = jax.ShapeDtypeStruct((N, nh, GLA_DV, GLA_DK), f32)
    return pl.pallas_call(
        body,
        out_shape=(o_shape, o_shape, s_shape, s_shape),
        grid=(N, nh),
        in_specs=specs(fw) + specs(bw),
        out_specs=(
            pl.BlockSpec((CHUNK, GLA_DV), lambda n, h: (n, h)),
            pl.BlockSpec((CHUNK, GLA_DV), lambda n, h: (N - 1 - n, h)),
            pl.BlockSpec((1, 1, GLA_DV, GLA_DK), lambda n, h: (n, h, 0, 0)),
            pl.BlockSpec((1, 1, GLA_DV, GLA_DK), lambda n, h: (N - 1 - n, h, 0, 0)),
        ),
        scratch_shapes=[pltpu.VMEM((2, nh, GLA_DV, GLA_DK), f32)],
        compiler_params=_cparams(("arbitrary", "arbitrary")),
        name="gla_scan_fwd",
    )(proj, proj, proj, G_f, proj, proj, proj, G_b)


def _gla_scan_bwd(proj, G_f, G_b, st_f, st_b, do):
    T = proj.shape[0]
    N = T // CHUNK
    nh = GLA_HEADS

    def body(qf_ref, kf_ref, vf_ref, gf_ref, sf_ref, dof_ref, qb_ref, kb_ref, vb_ref, gb_ref, sb_ref, dob_ref,
             dqf_ref, dkf_ref, dvf_ref, dgf_ref, dqb_ref, dkb_ref, dvb_ref, dgb_ref, ds_scr):
        n, h = pl.program_id(0), pl.program_id(1)

        @pl.when(n == 0)
        def _():
            ds_scr[0, h] = jnp.zeros((GLA_DV, GLA_DK), f32)
            ds_scr[1, h] = jnp.zeros((GLA_DV, GLA_DK), f32)

        for d, (q_ref, k_ref, v_ref, g_ref, st_ref, do_ref, dq_ref, dk_ref, dv_ref, dg_ref) in enumerate(
            ((qf_ref, kf_ref, vf_ref, gf_ref, sf_ref, dof_ref, dqf_ref, dkf_ref, dvf_ref, dgf_ref),
             (qb_ref, kb_ref, vb_ref, gb_ref, sb_ref, dob_ref, dqb_ref, dkb_ref, dvb_ref, dgb_ref))
        ):
            rev = d == 1
            ck = _GlaChunk(q_ref[...], k_ref[...], g_ref[...], rev)
            v = v_ref[...]
            St = st_ref[0, 0]
            dSt = ds_scr[d, h]
            dov = do_ref[...]
            d_attn = jnp.where(ck.incl, _mx(dov, v, "nt"), 0.0)
            d_qg = _mx(dov, St, "nn") + _mx(d_attn, ck.kg, "nn")
            d_kg = _mx(d_attn, ck.qg, "tn")
            dv_ref[...] = _mx(ck.attn, dov, "tn") + _mx(ck.kd, dSt, "nt")
            d_kd = _mx(v, dSt, "nn")
            ds_scr[d, h] = dSt * ck.egl + _mx(dov, ck.qg, "tn")
            kdd = d_kd * ck.kd
            d_gl = jnp.sum(dSt * St, axis=0, keepdims=True) * ck.egl + jnp.sum(kdd, axis=0, keepdims=True)
            dq_ref[...] = d_qg * ck.eG * (GLA_DK**-0.5)
            dk_ref[...] = d_kg * ck.enG + d_kd * ck.ekd
            d_G = d_qg * ck.qg - d_kg * ck.kg - kdd
            row = _iota2((CHUNK, GLA_DK), 0)
            d_G = d_G + jnp.where(row == ck.last, d_gl, 0.0)
            dg_ref[...] = _exact01(_chunk_tri(CHUNK, not rev), d_G)

    def in_specs(cidx):
        return [
            pl.BlockSpec((CHUNK, GLA_DK), lambda n, h: (cidx(n), Q_B_BLK + h)),
            pl.BlockSpec((CHUNK, GLA_DK), lambda n, h: (cidx(n), K_B_BLK + h)),
            pl.BlockSpec((CHUNK, GLA_DV), lambda n, h: (cidx(n), V_B_BLK + h)),
            pl.BlockSpec((CHUNK, GLA_DK), lambda n, h: (cidx(n), h)),
            pl.BlockSpec((1, 1, GLA_DV, GLA_DK), lambda n, h: (cidx(n), h, 0, 0)),
            pl.BlockSpec((CHUNK, GLA_DV), lambda n, h: (cidx(n), h)),
        ]

    def out_specs(cidx):
        return [
            pl.BlockSpec((CHUNK, GLA_DK), lambda n, h: (cidx(n), h)),
            pl.BlockSpec((CHUNK, GLA_DK), lambda n, h: (cidx(n), h)),
            pl.BlockSpec((CHUNK, GLA_DV), lambda n, h: (cidx(n), h)),
            pl.BlockSpec((CHUNK, GLA_DK), lambda n, h: (cidx(n), h)),
        ]

    fw, bw = (lambda n: N - 1 - n), (lambda n: n)
    k_shape = jax.ShapeDtypeStruct((T, nh * GLA_DK), f32)
    v_shape = jax.ShapeDtypeStruct((T, nh * GLA_DV), f32)
    return pl.pallas_call(
        body,
        out_shape=(k_shape, k_shape, v_shape, k_shape) * 2,
        grid=(N, nh),
        in_specs=in_specs(fw) + in_specs(bw),
        out_specs=tuple(out_specs(fw) + out_specs(bw)),
        scratch_shapes=[pltpu.VMEM((2, nh, GLA_DV, GLA_DK), f32)],
        compiler_params=_cparams(("arbitrary", "arbitrary")),
        name="gla_scan_bwd",
    )(proj, proj, proj, G_f, st_f, do, proj, proj, proj, G_b, st_b, do)


GATE_A_BLK, GATE_B_BLK, Z_BLK, G_B_BLK = 6144 // 1024, 7168 // 1024, 3072 // 1024, 8192 // 1024
DP_GATES_BLK, DP_GLA_QKV_BLK, DP_SMALL_BLK = 6144 // 2048, 4096 // 2048, SMALL_OFF // 512
_ANY_SPEC = pl.BlockSpec(memory_space=pl.ANY)


def _mid(og, obg, proj, x, target, wpg, wpl, wout, w_post, tb=256):
    T = x.shape[0]
    Dm = D_MODEL

    def body(og_ref, obg_ref, ga_ref, gb_ref, x_ref, t_ref, wpg_ref, wpl_ref, wo_ref, wp_ref,
             loss_ref, dwp_ref, dy_ref, m_ref, dout_ref, dya_ref, dyb_ref, dgates_ref, dog_ref, dobg_ref):
        i = pl.program_id(0)
        ya = _mx(og_ref[...], wpg_ref[...], "nn")
        yb = _mx(obg_ref[...], wpl_ref[...], "nn")
        sa, sb = _sigmoid(ga_ref[...]), _sigmoid(gb_ref[...])
        m = sa * ya + sb * yb
        m_ref[...] = m.astype(m_ref.dtype)
        out = _mx(m, wo_ref[...], "nn")
        r = lax.rsqrt(jnp.mean(out * out, axis=1, keepdims=True) + NORM_EPS)
        wp = wp_ref[...]
        e = x_ref[...] + out * r * wp - t_ref[...]
        dy = e * (1.0 / Dm)
        dy_ref[...] = dy
        lpart = jnp.sum(e * e, axis=0, keepdims=True)
        wpart = jnp.sum(dy * out * r, axis=0, keepdims=True)
        dyn = dy * wp
        dout = r * dyn - out * (r * r * r) * jnp.mean(dyn * out, axis=1, keepdims=True)
        dout_ref[...] = dout.astype(dout_ref.dtype)
        dm = _mx(dout, wo_ref[...], "nt")
        dya, dyb = dm * sa, dm * sb
        dya_ref[...] = dya.astype(dya_ref.dtype)
        dyb_ref[...] = dyb.astype(dyb_ref.dtype)
        dgates_ref[:, :Dm] = (dm * ya * sa * (1.0 - sa)).astype(dgates_ref.dtype)
        dgates_ref[:, Dm:] = (dm * yb * sb * (1.0 - sb)).astype(dgates_ref.dtype)
        dog_ref[...] = _mx(dya, wpg_ref[...], "nt")
        dobg_ref[...] = _mx(dyb, wpl_ref[...], "nt")

        @pl.when(i == 0)
        def _():
            loss_ref[...] = lpart
            dwp_ref[...] = wpart

        @pl.when(i > 0)
        def _():
            loss_ref[...] += lpart
            dwp_ref[...] += wpart

    blk = pl.BlockSpec((tb, Dm), lambda i: (i, 0))
    row = pl.BlockSpec((1, Dm), lambda i: (0, 0))
    wsp = pl.BlockSpec((Dm, Dm), lambda i: (0, 0))
    act = jax.ShapeDtypeStruct((T, Dm), MXU_DTYPE)
    big = jax.ShapeDtypeStruct((T, Dm), f32)
    vec = jax.ShapeDtypeStruct((1, Dm), f32)
    return pl.pallas_call(
        body,
        out_shape=(vec, vec, big, act, act, act, act, jax.ShapeDtypeStruct((T, N_CAT), MXU_DTYPE), big, big),
        grid=(T // tb,),
        in_specs=[blk, blk, pl.BlockSpec((tb, Dm), lambda i: (i, GATE_A_BLK)), pl.BlockSpec((tb, Dm), lambda i: (i, GATE_B_BLK)),
                  blk, blk, wsp, wsp, wsp, row],
        out_specs=(row, row) + (blk,) * 5 + (pl.BlockSpec((tb, 2 * Dm), lambda i: (i, DP_GATES_BLK)), blk, blk),
        compiler_params=_cparams(("arbitrary",)),
        name="mid",
    )(og, obg, proj, proj, x, target, wpg, wpl, wout, w_post)


def _heads(ref, lead, n, width):
    return [ref[lead + (slice(None), slice(i * width, (i + 1) * width))] for i in range(n)]


class _GdnBatch:
    def __init__(self, q, k, v, gcol, grow, beta, nf, A=None):
        B, Cn = q.shape[0], q.shape[1]
        self.nf = nf
        self.incl, self.strict, eye = _dir_masks(B, nf, Cn)
        self.decay = jnp.where(self.incl, jnp.exp(jnp.where(self.incl, gcol - grow, 0.0)), 0.0)
        self.k, self.v, self.beta = k, v, beta
        self.qs = q * (GDN_DIM**-0.5)
        self.kb = k * beta
        self.L = jnp.where(self.strict, _mx(self.kb, k, "nt") * self.decay, 0.0)
        if A is None:
            A = _inv_unit_lower(self.L, jnp.where(eye, 1.0, 0.0).astype(f32))
        self.A = A
        self.A2 = _split2(A)
        self.eg = jnp.exp(gcol)
        self.kbg = self.kb * self.eg
        self.sol = _h3s(self.A2, _split2(jnp.concatenate([v * beta, self.kbg], axis=2)), "nn")
        self.u = self.sol[:, :, :GDN_DIM]
        self.w = self.sol[:, :, GDN_DIM:]
        self.attn = jnp.where(self.incl, _mx(self.qs, k, "nt") * self.decay, 0.0)
        self.qd = self.qs * self.eg
        gl = _last_row(gcol, nf)
        self.ekd = jnp.exp(gl - gcol)
        self.kd = k * self.ekd
        self.egl = jnp.exp(gl)


def _gdn_chain_inputs(qf_ref, qb_ref, gf_ref, gb_ref, cf_ref, cb_ref, b, hb):
    qkv = [jnp.stack(_heads(qf_ref, (i,), hb, GDN_DIM) + _heads(qb_ref, (i,), hb, GDN_DIM)) for i in range(3)]
    cols, rows, betas = [], [], []
    for d, (g_ref, c_ref) in enumerate(((gf_ref, cf_ref), (gb_ref, cb_ref))):
        gates, gc = g_ref[...], c_ref[...]
        for hh in range(hb):
            h = b * hb + hh
            col = _lane_col(gc, h + 8 * d)
            cols.append(col)
            rows.append(jnp.broadcast_to(col, (CHUNK, LANES)).T[0:1, :])
            betas.append(_lane_col(gates, 16 + h + 8 * d))
    return qkv[0], qkv[1], qkv[2], jnp.stack(cols), jnp.stack(rows), jnp.stack(betas)


def _gdn_fwd(qkvn, gates, gc):
    T = qkvn.shape[1]
    N = T // CHUNK
    nh, hb = GDN_HEADS, GDN_HB
    W = hb * GDN_DIM

    def body(qf_ref, qb_ref, gf_ref, gb_ref, cf_ref, cb_ref, of_ref, ob_ref, sf_ref, sb_ref, af_ref, ab_ref, s_scr):
        n, b = pl.program_id(0), pl.program_id(1)
        hs = pl.ds(b * hb, hb)

        @pl.when(n == 0)
        def _():
            s_scr[0, hs] = jnp.zeros((hb, GDN_DIM, GDN_DIM), f32)
            s_scr[1, hs] = jnp.zeros((hb, GDN_DIM, GDN_DIM), f32)

        q, k, v, gcol, grow, beta = _gdn_chain_inputs(qf_ref, qb_ref, gf_ref, gb_ref, cf_ref, cb_ref, b, hb)
        ck = _GdnBatch(q, k, v, gcol, grow, beta, hb)
        S = jnp.concatenate([s_scr[0, hs], s_scr[1, hs]], axis=0)
        sf_ref[0], sb_ref[0] = S[:hb], S[hb:]
        af_ref[0], ab_ref[0] = ck.A[:hb], ck.A[hb:]
        v_new = ck.u - _mx(ck.w, S, "nn")
        o = _mx(ck.qd, S, "nn") + _mx(ck.attn, v_new, "nn")
        S_new = S * ck.egl + _mx(ck.kd, v_new, "tn")
        for hh in range(hb):
            sl = slice(hh * GDN_DIM, (hh + 1) * GDN_DIM)
            of_ref[:, sl] = o[hh]
            ob_ref[:, sl] = o[hb + hh]
        s_scr[0, hs] = S_new[:hb]
        s_scr[1, hs] = S_new[hb:]

    fw, bw = (lambda n: n), (lambda n: N - 1 - n)
    qsp = lambda ci: pl.BlockSpec((3, CHUNK, W), lambda n, b: (0, ci(n), b))
    gsp = lambda ci: pl.BlockSpec((CHUNK, LANES), lambda n, b: (ci(n), 0))
    osp = lambda ci: pl.BlockSpec((CHUNK, W), lambda n, b: (ci(n), b))
    ssp = lambda ci: pl.BlockSpec((1, hb, GDN_DIM, GDN_DIM), lambda n, b: (ci(n), b, 0, 0))
    asp = lambda ci: pl.BlockSpec((1, hb, CHUNK, CHUNK), lambda n, b: (ci(n), b, 0, 0))
    o_shape = jax.ShapeDtypeStruct((T, nh * GDN_DIM), f32)
    s_shape = jax.ShapeDtypeStruct((N, nh, GDN_DIM, GDN_DIM), f32)
    a_shape = jax.ShapeDtypeStruct((N, nh, CHUNK, CHUNK), f32)
    return pl.pallas_call(
        body,
        out_shape=(o_shape, o_shape, s_shape, s_shape, a_shape, a_shape),
        grid=(N, nh // hb),
        in_specs=[qsp(fw), qsp(bw), gsp(fw), gsp(bw), gsp(fw), gsp(bw)],
        out_specs=(osp(fw), osp(bw), ssp(fw), ssp(bw), asp(fw), asp(bw)),
        scratch_shapes=[pltpu.VMEM((2, nh, GDN_DIM, GDN_DIM), f32)],
        compiler_params=_cparams(("arbitrary", "arbitrary")),
        name="gdn_scan_fwd",
    )(qkvn, qkvn, gates, gates, gc, gc)


def _gdn_bwd(qkvn, gates, gc, st_f, st_b, a_f, a_b, do):
    T = qkvn.shape[1]
    N = T // CHUNK
    nh, hb = GDN_HEADS, GDN_HB
    nb = nh // hb
    W = hb * GDN_DIM

    def body(qf_ref, qb_ref, gf_ref, gb_ref, cf_ref, cb_ref, sf_ref, sb_ref, af_ref, ab_ref, dof_ref, dob_ref,
             dqf_ref, dqb_ref, dgf_ref, dgb_ref, ds_scr):
        n, b = pl.program_id(0), pl.program_id(1)
        hs = pl.ds(b * hb, hb)

        @pl.when(n == 0)
        def _():
            ds_scr[0, hs] = jnp.zeros((hb, GDN_DIM, GDN_DIM), f32)
            ds_scr[1, hs] = jnp.zeros((hb, GDN_DIM, GDN_DIM), f32)

        @pl.when(b == 0)
        def _():
            dgf_ref[...] = jnp.zeros((CHUNK, LANES), f32)
            dgb_ref[...] = jnp.zeros((CHUNK, LANES), f32)

        q, k, v, gcol, grow, beta = _gdn_chain_inputs(qf_ref, qb_ref, gf_ref, gb_ref, cf_ref, cb_ref, b, hb)
        ck = _GdnBatch(q, k, v, gcol, grow, beta, hb, A=jnp.concatenate([af_ref[0], ab_ref[0]], axis=0))
        S = jnp.concatenate([sf_ref[0], sb_ref[0]], axis=0)
        dS = jnp.concatenate([ds_scr[0, hs], ds_scr[1, hs]], axis=0)
        dov = jnp.stack(_heads(dof_ref, (), hb, GDN_DIM) + _heads(dob_ref, (), hb, GDN_DIM))
        v_new = ck.u - _mx(ck.w, S, "nn")
        d_vnew = _mx(ck.attn, dov, "tn") + _mx(ck.kd, dS, "nn")
        d_attn = jnp.where(ck.incl, _mx(dov, v_new, "nt"), 0.0)
        d_qd = _mx(dov, S, "nt")
        d_kd = _mx(v_new, dS, "nt")
        d_gl = jnp.sum(jnp.sum(dS * S, axis=2, keepdims=True), axis=1, keepdims=True) * ck.egl
        dS_new = dS * ck.egl + _mx(ck.qd, dov, "tn") - _mx(ck.w, d_vnew, "tn")
        ds_scr[0, hs] = dS_new[:hb]
        ds_scr[1, hs] = dS_new[hb:]
        d_w = -_mx(d_vnew, S, "nt")
        d_rhs = _h3s(ck.A2, _split2(jnp.concatenate([d_vnew, d_w], axis=2)), "tn")
        d_vb, d_kbg = d_rhs[:, :, :GDN_DIM], d_rhs[:, :, GDN_DIM:]
        dL = -jnp.where(ck.strict, _h3(d_rhs, ck.sol, "nt"), 0.0)
        P = dL * ck.decay
        Q = d_attn * ck.decay
        d_kb = _mx(P, k, "nn") + d_kbg * ck.eg
        d_qs = _mx(Q, k, "nn") + d_qd * ck.eg
        d_k = _mx(P, ck.kb, "tn") + _mx(Q, ck.qs, "tn") + d_kd * ck.ekd + d_kb * ck.beta
        E = dL * ck.L + d_attn * ck.attn
        kdsum = jnp.sum(d_kd * ck.kd, axis=2, keepdims=True)
        d_gc = (jnp.sum(E, axis=2, keepdims=True) - _colsum_col(E)
                + jnp.sum(d_kbg * ck.kbg, axis=2, keepdims=True)
                + jnp.sum(d_qd * ck.qd, axis=2, keepdims=True) - kdsum)
        d_gc = _add_at_last_row(d_gc, d_gl + jnp.sum(kdsum, axis=1, keepdims=True), hb)
        d_beta = jnp.sum(d_vb * v, axis=2, keepdims=True) + jnp.sum(d_kb * k, axis=2, keepdims=True)
        d_q = d_qs * (GDN_DIM**-0.5)
        d_v = d_vb * ck.beta
        lane = _iota2((CHUNK, LANES), 1)
        for d, (dq_ref, dg_ref) in enumerate(((dqf_ref, dgf_ref), (dqb_ref, dgb_ref))):
            acc = dg_ref[...]
            for hh in range(hb):
                c, h = d * hb + hh, b * hb + hh
                sl = slice(hh * GDN_DIM, (hh + 1) * GDN_DIM)
                dq_ref[0, :, sl] = d_q[c]
                dq_ref[1, :, sl] = d_k[c]
                dq_ref[2, :, sl] = d_v[c]
                acc = jnp.where(lane == h + 8 * d, d_gc[c], acc)
                acc = jnp.where(lane == 16 + h + 8 * d, d_beta[c], acc)
            dg_ref[...] = acc

        @pl.when(b == nb - 1)
        def _():
            for d, dg_ref in enumerate((dgf_ref, dgb_ref)):
                blk = dg_ref[...]
                dg_ref[...] = jnp.where(lane < 16, _exact01(_chunk_tri(CHUNK, d == 0), blk), blk)

    fw, bw = (lambda n: N - 1 - n), (lambda n: n)
    qsp = lambda ci: pl.BlockSpec((3, CHUNK, W), lambda n, b: (0, ci(n), b))
    gsp = lambda ci: pl.BlockSpec((CHUNK, LANES), lambda n, b: (ci(n), 0))
    osp = lambda ci: pl.BlockSpec((CHUNK, W), lambda n, b: (ci(n), b))
    ssp = lambda ci: pl.BlockSpec((1, hb, GDN_DIM, GDN_DIM), lambda n, b: (ci(n), b, 0, 0))
    asp = lambda ci: pl.BlockSpec((1, hb, CHUNK, CHUNK), lambda n, b: (ci(n), b, 0, 0))
    dq_shape = jax.ShapeDtypeStruct((3, T, nh * GDN_DIM), f32)
    dg_shape = jax.ShapeDtypeStruct((T, LANES), f32)
    return pl.pallas_call(
        body,
        out_shape=(dq_shape, dq_shape, dg_shape, dg_shape),
        grid=(N, nb),
        in_specs=[qsp(fw), qsp(bw), gsp(fw), gsp(bw), gsp(fw), gsp(bw), ssp(fw), ssp(bw), asp(fw), asp(bw), osp(fw), osp(bw)],
        out_specs=(qsp(fw), qsp(bw), gsp(fw), gsp(bw)),
        scratch_shapes=[pltpu.VMEM((2, nh, GDN_DIM, GDN_DIM), f32)],
        compiler_params=_cparams(("arbitrary", "arbitrary")),
        name="gdn_scan_bwd",
    )(qkvn, qkvn, gates, gates, gc, gc, st_f, st_b, a_f, a_b, do, do)


class _GlaBatch:
    def __init__(self, q, k, G, nf):
        B, Cn = q.shape[0], q.shape[1]
        self.incl, _, _ = _dir_masks(B, nf, Cn)
        self.eG = jnp.exp(G)
        self.enG = jnp.exp(-G)
        self.qg = q * (GLA_DK**-0.5) * self.eG
        self.kg = k * self.enG
        self.attn = jnp.where(self.incl, _mx(self.qg, self.kg, "nt"), 0.0)
        gl = _last_row(G, nf)
        self.ekd = jnp.exp(gl - G)
        self.kd = k * self.ekd
        self.egl = jnp.exp(gl)


def _gla_chain_inputs(refs_f, refs_b):
    nh = GLA_HEADS
    out = []
    for i, width in enumerate((GLA_DK, GLA_DK, GLA_DV, GLA_DK)):
        out.append(jnp.stack(_heads(refs_f[i], (), nh, width) + _heads(refs_b[i], (), nh, width)))
    return out


def _gla_specs(N, ci):
    nh = GLA_HEADS
    return [
        pl.BlockSpec((CHUNK, nh * GLA_DK), lambda n: (ci(n), 4096 // (nh * GLA_DK))),
        pl.BlockSpec((CHUNK, nh * GLA_DK), lambda n: (ci(n), 4608 // (nh * GLA_DK))),
        pl.BlockSpec((CHUNK, nh * GLA_DV), lambda n: (ci(n), 5120 // (nh * GLA_DV))),
        pl.BlockSpec((CHUNK, nh * GLA_DK), lambda n: (ci(n), 0)),
    ]


def _gla_fwd(proj, G_f, G_b):
    T = proj.shape[0]
    N = T // CHUNK
    nh = GLA_HEADS

    def body(qf, kf, vf, gf, qb, kb, vb, gb, of_ref, ob_ref, sf_ref, sb_ref, s_scr):
        n = pl.program_id(0)

        @pl.when(n == 0)
        def _():
            s_scr[...] = jnp.zeros((2 * nh, GLA_DV, GLA_DK), f32)

        q, k, v, G = _gla_chain_inputs((qf, kf, vf, gf), (qb, kb, vb, gb))
        ck = _GlaBatch(q, k, G, nh)
        St = s_scr[...]
        sf_ref[0], sb_ref[0] = St[:nh], St[nh:]
        o = _mx(ck.qg, St, "nt") + _mx(ck.attn, v, "nn")
        for i in range(nh):
            sl = slice(i * GLA_DV, (i + 1) * GLA_DV)
            of_ref[:, sl] = o[i]
            ob_ref[:, sl] = o[nh + i]
        s_scr[...] = St * ck.egl + _mx(v, ck.kd, "tn")

    fw, bw = (lambda n: n), (lambda n: N - 1 - n)
    osp = lambda ci: pl.BlockSpec((CHUNK, nh * GLA_DV), lambda n: (ci(n), 0))
    ssp = lambda ci: pl.BlockSpec((1, nh, GLA_DV, GLA_DK), lambda n: (ci(n), 0, 0, 0))
    o_shape = jax.ShapeDtypeStruct((T, nh * GLA_DV), f32)
    s_shape = jax.ShapeDtypeStruct((N, nh, GLA_DV, GLA_DK), f32)
    return pl.pallas_call(
        body,
        out_shape=(o_shape, o_shape, s_shape, s_shape),
        grid=(N,),
        in_specs=_gla_specs(N, fw) + _gla_specs(N, bw),
        out_specs=(osp(fw), osp(bw), ssp(fw), ssp(bw)),
        scratch_shapes=[pltpu.VMEM((2 * nh, GLA_DV, GLA_DK), f32)],
        compiler_params=_cparams(("arbitrary",)),
        name="gla_scan_fwd",
    )(proj, proj, proj, G_f, proj, proj, proj, G_b)


def _gla_bwd(proj, G_f, G_b, st_f, st_b, do):
    T = proj.shape[0]
    N = T // CHUNK
    nh = GLA_HEADS

    def body(qf, kf, vf, gf, sf_ref, dof_ref, qb, kb, vb, gb, sb_ref, dob_ref,
             dqf_ref, dkf_ref, dvf_ref, dgf_ref, dqb_ref, dkb_ref, dvb_ref, dgb_ref, ds_scr):
        n = pl.program_id(0)

        @pl.when(n == 0)
        def _():
            ds_scr[...] = jnp.zeros((2 * nh, GLA_DV, GLA_DK), f32)

        q, k, v, G = _gla_chain_inputs((qf, kf, vf, gf), (qb, kb, vb, gb))
        ck = _GlaBatch(q, k, G, nh)
        St = jnp.concatenate([sf_ref[0], sb_ref[0]], axis=0)
        dSt = ds_scr[...]
        dov = jnp.stack(_heads(dof_ref, (), nh, GLA_DV) + _heads(dob_ref, (), nh, GLA_DV))
        d_attn = jnp.where(ck.incl, _mx(dov, v, "nt"), 0.0)
        d_qg = _mx(dov, St, "nn") + _mx(d_attn, ck.kg, "nn")
        d_kg = _mx(d_attn, ck.qg, "tn")
        d_v = _mx(ck.attn, dov, "tn") + _mx(ck.kd, dSt, "nt")
        d_kd = _mx(v, dSt, "nn")
        ds_scr[...] = dSt * ck.egl + _mx(dov, ck.qg, "tn")
        kdd = d_kd * ck.kd
        d_gl = jnp.sum(dSt * St, axis=1, keepdims=True) * ck.egl + jnp.sum(kdd, axis=1, keepdims=True)
        d_q = d_qg * ck.eG * (GLA_DK**-0.5)
        d_k = d_kg * ck.enG + d_kd * ck.ekd
        d_G = _add_at_last_row(d_qg * ck.qg - d_kg * ck.kg - kdd, d_gl, nh)
        shp = (2 * nh, CHUNK, CHUNK)
        bb, ii, jj = _iota3(shp, 0), _iota3(shp, 1), _iota3(shp, 2)
        tri = jnp.logical_or(jnp.logical_and(bb < nh, jj >= ii), jnp.logical_and(bb >= nh, jj <= ii))
        d_gk = _exact01(jnp.where(tri, 1.0, 0.0), d_G)
        for d, (dq_ref, dk_ref, dv_ref, dg_ref) in enumerate(((dqf_ref, dkf_ref, dvf_ref, dgf_ref), (dqb_ref, dkb_ref, dvb_ref, dgb_ref))):
            for i in range(nh):
                c = d * nh + i
                ks, vs = slice(i * GLA_DK, (i + 1) * GLA_DK), slice(i * GLA_DV, (i + 1) * GLA_DV)
                dq_ref[:, ks] = d_q[c]
                dk_ref[:, ks] = d_k[c]
                dv_ref[:, vs] = d_v[c]
                dg_ref[:, ks] = d_gk[c]

    def in_specs(ci):
        return _gla_specs(N, ci) + [
            pl.BlockSpec((1, nh, GLA_DV, GLA_DK), lambda n: (ci(n), 0, 0, 0)),
            pl.BlockSpec((CHUNK, nh * GLA_DV), lambda n: (ci(n), 0)),
        ]

    def out_specs(ci):
        ksp = pl.BlockSpec((CHUNK, nh * GLA_DK), lambda n: (ci(n), 0))
        return [ksp, ksp, pl.BlockSpec((CHUNK, nh * GLA_DV), lambda n: (ci(n), 0)), ksp]

    fw, bw = (lambda n: N - 1 - n), (lambda n: n)
    k_shape = jax.ShapeDtypeStruct((T, nh * GLA_DK), f32)
    v_shape = jax.ShapeDtypeStruct((T, nh * GLA_DV), f32)
    return pl.pallas_call(
        body,
        out_shape=(k_shape, k_shape, v_shape, k_shape) * 2,
        grid=(N,),
        in_specs=in_specs(fw) + in_specs(bw),
        out_specs=tuple(out_specs(fw) + out_specs(bw)),
        scratch_shapes=[pltpu.VMEM((2 * nh, GLA_DV, GLA_DK), f32)],
        compiler_params=_cparams(("arbitrary",)),
        name="gla_scan_bwd",
    )(proj, proj, proj, G_f, st_f, do, proj, proj, proj, G_b, st_b, do)


def _gla_dsum(dq_f, dq_b, dk_f, dk_b, dv_f, dv_b, dproj, tb=256):
    T = dq_f.shape[0]
    wk, wv = GLA_HEADS * GLA_DK, GLA_HEADS * GLA_DV

    def body(qf, qb, kf, kb, vf, vb, _, o_ref):
        o_ref[:, :wk] = (qf[...] + qb[...]).astype(o_ref.dtype)
        o_ref[:, wk : 2 * wk] = (kf[...] + kb[...]).astype(o_ref.dtype)
        o_ref[:, 2 * wk :] = (vf[...] + vb[...]).astype(o_ref.dtype)

    ksp = pl.BlockSpec((tb, wk), lambda i: (i, 0))
    vsp = pl.BlockSpec((tb, wv), lambda i: (i, 0))
    return pl.pallas_call(
        body,
        out_shape=jax.ShapeDtypeStruct(dproj.shape, dproj.dtype),
        grid=(T // tb,),
        in_specs=[ksp, ksp, ksp, ksp, vsp, vsp, _ANY_SPEC],
        out_specs=pl.BlockSpec((tb, 2 * wk + wv), lambda i: (i, DP_GLA_QKV_BLK)),
        input_output_aliases={6: 0},
        compiler_params=_cparams(("parallel",)),
        name="gla_dsum",
    )(dq_f, dq_b, dk_f, dk_b, dv_f, dv_b, dproj)


def _local_step(x, target, wcat, conv_w, avec, dvec, gdn_norm_w, w2f, w2b, b2f, b2b, gla_norm_w, proj_weights, w_pre, w_post,
                send_proj_grads=None):
    h, h_t = _rms_pre(x, w_pre)
    proj = _matmul(h, wcat, "nn", f32, "proj_in", 2048, 512, 1024)
    qkvn = _gdn_prep(proj, conv_w)
    gates, gc = _gdn_gates(proj, avec, dvec)
    oa_f, oa_b, sa_f, sa_b, inv_f, inv_b = _gdn_fwd(qkvn, gates, gc)
    og = _out_norm(oa_f, oa_b, proj, Z_BLK, gdn_norm_w, GDN_DIM, "gdn_out")
    G_f, G_b = _gla_prep(proj, w2f, w2b, b2f, b2b)
    ob_f, ob_b, sb_f, sb_b = _gla_fwd(proj, G_f, G_b)
    obg = _out_norm(ob_f, ob_b, proj, G_B_BLK, gla_norm_w, GLA_DV, "gla_out")
    wpg, wpl, wout = proj_weights(obg)
    (loss_row, d_wpost, dy, m, dout, dya, dyb, dproj, dog, dobg) = _mid(og, obg, proj, x, target, wpg, wpl, wout, w_post)

    d_wout = _matmul(m, dout, "tn", MXU_DTYPE, "dw_out", 512, 512, 2048)
    d_wpg = _matmul(og, dya, "tn", MXU_DTYPE, "dw_proj_gdn", 512, 512, 2048)
    d_wpl = _matmul(obg, dyb, "tn", MXU_DTYPE, "dw_proj_gla", 512, 512, 2048)
    if send_proj_grads is not None:
        gla_norm_w = gla_norm_w + send_proj_grads(d_wpg, d_wpl, d_wout)

    do_b, dproj, d_gla_norm = _out_norm_bwd(dobg, ob_f, ob_b, proj, G_B_BLK, gla_norm_w, GLA_DV, "gla_out_bwd", dproj)
    dq_f, dk_f, dv_f, dgk_f, dq_b, dk_b, dv_b, dgk_b = _gla_bwd(proj, G_f, G_b, sb_f, sb_b, do_b)
    dproj = _gla_dsum(dq_f, dq_b, dk_f, dk_b, dv_f, dv_b, dproj)
    dsmall_gla, d_w2f, d_w2b, d_b2f, d_b2b = _gla_prep_bwd(proj, w2f, w2b, b2f, b2b, dgk_f, dgk_b)

    do_a, dproj, d_gdn_norm = _out_norm_bwd(dog, oa_f, oa_b, proj, Z_BLK, gdn_norm_w, GDN_DIM, "gdn_out_bwd", dproj)
    dqkv_f, dqkv_b, dg_f, dg_b = _gdn_bwd(qkvn, gates, gc, sa_f, sa_b, inv_f, inv_b, do_a)
    dproj, d_conv = _gdn_prep_bwd(proj, conv_w, dqkv_f, dqkv_b, dproj)
    dproj, d_alog, d_dtb = _gdn_gates_bwd(proj, avec, dvec, dg_f, dg_b, dsmall_gla, dproj)

    d_wcat = _matmul(h_t, dproj, "nn", MXU_DTYPE, "dw_in", 1024, 512, 2048)
    dh = _matmul(dproj, wcat, "nt", f32, "dh", 1024, 1024, 2432)
    grad_x, d_wpre = _rms_pre_bwd(x, w_pre, dh, dy)
    return dict(loss_row=loss_row, grad_x=grad_x, d_wcat=d_wcat, d_conv=d_conv, d_alog=d_alog, d_dtb=d_dtb,
                d_gdn_norm=d_gdn_norm, d_wpg=d_wpg, d_w2f=d_w2f, d_w2b=d_w2b, d_b2f=d_b2f, d_b2b=d_b2b,
                d_gla_norm=d_gla_norm, d_wpl=d_wpl, d_wout=d_wout, d_wpost=d_wpost, d_wpre=d_wpre)


def _to_cat(wf):
    pad = jnp.zeros(wf.shape[:-1] + (N_CAT - N_IN,), wf.dtype)
    return jnp.concatenate([wf[..., :4096], wf[..., 4128:6176], wf[..., 7232:9280], wf[..., 6176:7200], wf[..., 4096:4128],
                            wf[..., 7200:7232], pad], axis=-1)


def _from_cat(wc):
    return jnp.concatenate([wc[..., :4096], wc[..., 9216:9248], wc[..., 4096:6144], wc[..., 8192:9216], wc[..., 9248:9280],
                            wc[..., 6144:8192]], axis=-1)


def _lane_row(*pieces):
    row = jnp.concatenate(pieces, axis=1)
    return jnp.pad(row, ((0, 0), (0, LANES - row.shape[1])))


def _pad_w2(w2, row0):
    return jnp.pad(w2, ((row0, LANES - row0 - w2.shape[0]), (0, 0)))


SM_ROWS, SM_COLS = 32, 384
REP_ROWS = 32


def _pack_small_shard(conv, w2f, w2b):
    lead = conv.shape[:-2]
    z = lambda r, c: jnp.zeros(lead + (r, c), f32)
    top = jnp.concatenate([conv, z(8 - CONV_K, SM_COLS)], axis=-2)
    mid = jnp.concatenate([w2f, w2b, z(16, SM_COLS - 128)], axis=-1)
    return jnp.concatenate([top, mid, z(SM_ROWS - 24, SM_COLS)], axis=-2)


LOSS_ROW = 28


def _pack_rep(ln_pre, ln_post, b2f, b2b, gla_norm, gdn_norm, alog_f, alog_b, dt_f, dt_b, loss=None):
    last = _lane_row(alog_f, alog_b, dt_f, dt_b)
    extra = jnp.zeros((1, LANES), f32) if loss is None else _lane_row(loss)
    rows = [ln_pre.reshape(8, LANES), ln_post.reshape(8, LANES), b2f.reshape(4, LANES), b2b.reshape(4, LANES),
            gla_norm.reshape(2, LANES), gdn_norm.reshape(1, LANES), last, extra, jnp.zeros((REP_ROWS - LOSS_ROW - 1, LANES), f32)]
    return jnp.concatenate(rows, axis=0)


def _unpack_rep(p):
    return dict(ln_pre_w=p[0:8].reshape(1, 1024), ln_post_w=p[8:16].reshape(1, 1024), gk_b2_fwd=p[16:20].reshape(1, 512),
                gk_b2_bwd=p[20:24].reshape(1, 512), gla_norm_w=p[24:26].reshape(1, 256), gdn_norm_w=p[26:27],
                a_log_fwd=p[27:28, 0:8], a_log_bwd=p[27:28, 8:16], dt_bias_fwd=p[27:28, 16:24], dt_bias_bwd=p[27:28, 24:32])


_MESH = pl.DeviceIdType.MESH


N_GATHERED = 2


def _gather_weights(w_in_s, sm_s):
    shapes = (w_in_s.shape, sm_s.shape)
    dtypes = (MXU_DTYPE, f32)

    def body(win_ref, sm_ref, gin_ref, gsm_ref, send_sems, recv_sems):
        x, y, c = lax.axis_index("x"), lax.axis_index("y"), lax.axis_index("c")
        me, sibling = (x, y, c), (x, y, 1 - c)
        chips = [(1 - x, y), (x, 1 - y), (1 - x, 1 - y)]
        bufs = (gin_ref, gsm_ref)

        def idx(px, py, pc):
            return 4 * px + 2 * py + pc

        for r in range(w_in_s.shape[1] // LANES):
            cols = slice(r * LANES, (r + 1) * LANES)
            gin_ref[idx(*me), :, cols] = win_ref[:, cols].astype(MXU_DTYPE)
        gsm_ref[idx(*me)] = sm_ref[...]

        def copy(a, k, block, to):
            blk = bufs[a].at[idx(*block)]
            return pltpu.make_async_remote_copy(src_ref=blk, dst_ref=blk, send_sem=send_sems.at[7 * a + k],
                                                recv_sem=recv_sems.at[7 * a + k], device_id=to, device_id_type=_MESH)

        first, passed = [], []
        for a in range(N_GATHERED):
            first.append(copy(a, 0, me, sibling))
            first += [copy(a, 1 + j, me, (*chip, c)) for j, chip in enumerate(chips)]
        for cp in first:
            cp.start()
        for j, chip in enumerate(chips):
            for a in range(N_GATHERED):
                copy(a, 1 + j, (*chip, c), me).wait_recv()
                cp = copy(a, 4 + j, (*chip, c), sibling)
                cp.start()
                passed.append(cp)
        for a in range(N_GATHERED):
            copy(a, 0, sibling, me).wait_recv()
            for j, chip in enumerate(chips):
                copy(a, 4 + j, (*chip, 1 - c), me).wait_recv()
        for cp in first + passed:
            cp.wait_send()

    vm = pl.BlockSpec(memory_space=pltpu.VMEM)
    return pl.pallas_call(
        body,
        out_shape=tuple(jax.ShapeDtypeStruct((N_DEV,) + s, d) for s, d in zip(shapes, dtypes)),
        in_specs=[vm, vm],
        out_specs=(vm, vm),
        scratch_shapes=[pltpu.SemaphoreType.DMA((7 * N_GATHERED,)), pltpu.SemaphoreType.DMA((7 * N_GATHERED,))],
        compiler_params=pltpu.CompilerParams(vmem_limit_bytes=VMEM_LIMIT),
        name="gather_weights",
    )(w_in_s, sm_s)


N_CHIP = 4
_EFFECT = pltpu.SideEffectType.DATAFLOW_SIDE_EFFECTING
_HBM_SPEC = pl.BlockSpec(memory_space=pltpu.HBM)
_SEM_SPEC = pl.BlockSpec(memory_space=pltpu.SEMAPHORE)


def _relation_peer(k, x, y, c):
    px = (1 - x) if (k & 4) else x
    py = (1 - y) if (k & 2) else y
    pc = (1 - c) if (k & 1) else c
    return (px, py, pc), 4 * px + 2 * py + pc


def _copies_to_all(src_ref, land_ref, send_sems, recv_sems, blocked, arrival):
    x, y, c = lax.axis_index("x"), lax.axis_index("y"), lax.axis_index("c")
    me = 4 * x + 2 * y + c
    out = []
    for k in range(1, N_DEV):
        dev, p = _relation_peer(k, x, y, c)
        out.append(pltpu.make_async_remote_copy(
            src_ref=src_ref.at[p] if blocked else src_ref, dst_ref=land_ref.at[p if arrival else me],
            send_sem=send_sems.at[k - 1], recv_sem=recv_sems.at[k - 1], device_id=dev, device_id_type=_MESH))
    return out


def _exchange_start(src, blocked, name):
    land = lax.empty((N_DEV,) + (src.shape[1:] if blocked else src.shape), src.dtype)

    def body(src_ref, land_ref, send_sems, recv_sems, src_thru, land_thru, token):
        for cp in _copies_to_all(src_ref, land_ref, send_sems, recv_sems, blocked, False):
            cp.start()
        token[...] = jnp.zeros_like(token)

    return pl.pallas_call(
        body,
        name=name,
        out_shape=(pltpu.SemaphoreType.DMA((N_DEV - 1,)), pltpu.SemaphoreType.DMA((N_DEV - 1,)), pltpu.HBM(src.shape, src.dtype),
                   pltpu.HBM(land.shape, land.dtype), jax.ShapeDtypeStruct((8, LANES), f32)),
        in_specs=(_HBM_SPEC, _HBM_SPEC),
        out_specs=(_SEM_SPEC, _SEM_SPEC, _HBM_SPEC, _HBM_SPEC, pl.BlockSpec(memory_space=pltpu.VMEM)),
        input_output_aliases={0: 2, 1: 3},
        compiler_params=pltpu.CompilerParams(has_side_effects=_EFFECT),
    )(pltpu.with_memory_space_constraint(src, pltpu.HBM), pltpu.with_memory_space_constraint(land, pltpu.HBM))


def _exchange_wait(started, after, blocked, name):
    send_sems, recv_sems, src_thru, land_thru, _ = started

    def body(src_ref, land_ref, send_sems, recv_sems, after_ref, src_dead, got_ref):
        for cp in _copies_to_all(src_ref, land_ref, send_sems, recv_sems, blocked, True):
            cp.wait_send()
            cp.wait_recv()

    return pl.pallas_call(
        body,
        name=name,
        out_shape=(pltpu.HBM(src_thru.shape, src_thru.dtype), pltpu.HBM(land_thru.shape, land_thru.dtype)),
        in_specs=(_HBM_SPEC, _HBM_SPEC, _SEM_SPEC, _SEM_SPEC, pl.BlockSpec(memory_space=pl.ANY)),
        out_specs=(_HBM_SPEC, _HBM_SPEC),
        input_output_aliases={0: 0, 1: 1},
        compiler_params=pltpu.CompilerParams(has_side_effects=_EFFECT),
    )(src_thru, land_thru, send_sems, recv_sems, after)


def _exchange_sibling(g_in):
    def body(in_ref, out_ref, send_sems, recv_sems):
        x, y, c = lax.axis_index("x"), lax.axis_index("y"), lax.axis_index("c")

        def copy(k):
            return pltpu.make_async_remote_copy(src_ref=in_ref.at[2 * k + 1 - c], dst_ref=out_ref.at[k], send_sem=send_sems.at[k],
                                                recv_sem=recv_sems.at[k], device_id=(x, y, 1 - c), device_id_type=_MESH)

        copies = [copy(k) for k in range(N_CHIP)]
        for cp in copies:
            cp.start()
        for cp in copies:
            cp.wait_recv()
        for cp in copies:
            cp.wait_send()

    return pl.pallas_call(
        body,
        out_shape=jax.ShapeDtypeStruct((N_CHIP,) + g_in.shape[1:], g_in.dtype),
        in_specs=[_HBM_SPEC],
        out_specs=_HBM_SPEC,
        scratch_shapes=[pltpu.SemaphoreType.DMA((N_CHIP,)), pltpu.SemaphoreType.DMA((N_CHIP,))],
        name="exchange_sibling",
    )(g_in)


def _pair_sum(g, r, name, tc=LANES):
    _, R, Cc = g.shape

    def body(g_ref, r_ref, o_ref):
        o_ref[...] = (g_ref[...].astype(f32) + r_ref[...].astype(f32)).astype(o_ref.dtype)

    return pl.pallas_call(
        body,
        out_shape=jax.ShapeDtypeStruct(r.shape, r.dtype),
        grid=(N_CHIP, Cc // tc),
        in_specs=[pl.BlockSpec((1, R, tc), lambda k, i: (2 * k + lax.axis_index("c"), 0, i)), pl.BlockSpec((1, R, tc), lambda k, i: (k, 0, i))],
        out_specs=pl.BlockSpec((1, R, tc), lambda k, i: (k, 0, i)),
        compiler_params=_cparams(("parallel", "parallel")),
        name=name,
    )(g, r)


def _exchange_grads(q_in, g_sm, g_rep):
    srcs = (q_in, g_sm, g_rep)

    def body(in0, in1, in2, out0, out1, out2, send_sems, recv_sems, local_sems):
        x, y, c = lax.axis_index("x"), lax.axis_index("y"), lax.axis_index("c")
        me, my_chip = 4 * x + 2 * y + c, 2 * x + y
        ins, outs = (in0, in1, in2), (out0, out1, out2)

        def slots(a, k):
            (px, py, pc), p = _relation_peer(k, x, y, c)
            if a == 0:
                return ins[0].at[2 * px + py], my_chip, 2 * px + py
            return (ins[a].at[p] if a == 1 else ins[a]), me, p

        def copy(a, k, arrival):
            src, there, here = slots(a, k)
            sem = (7 * (a - 1) + k - 1) if a >= 1 else (14 + k // 2 - 1)
            return pltpu.make_async_remote_copy(src_ref=src, dst_ref=outs[a].at[here if arrival else there], send_sem=send_sems.at[sem],
                                                recv_sem=recv_sems.at[sem], device_id=_relation_peer(k, x, y, c)[0], device_id_type=_MESH)

        pairs = [(0, k) for k in (2, 4, 6)] + [(a, k) for a in (1, 2) for k in range(1, 8)]
        sends = [copy(a, k, False) for a, k in pairs]
        for cp in sends:
            cp.start()
        own = (ins[0].at[my_chip], ins[1].at[me], ins[2])
        own_dst = (outs[0].at[my_chip], outs[1].at[me], outs[2].at[me])
        local = [pltpu.make_async_copy(own[a], own_dst[a], local_sems.at[a]) for a in range(3)]
        for cp in local:
            cp.start()
        for a, k in pairs:
            copy(a, k, True).wait_recv()
        for cp in sends:
            cp.wait_send()
        for cp in local:
            cp.wait()

    out_shape = tuple(jax.ShapeDtypeStruct(s.shape if i < 2 else (N_DEV,) + s.shape, s.dtype) for i, s in enumerate(srcs))
    return pl.pallas_call(
        body,
        out_shape=out_shape,
        in_specs=[_HBM_SPEC] * 3,
        out_specs=(_HBM_SPEC,) * 3,
        scratch_shapes=[pltpu.SemaphoreType.DMA((17,)), pltpu.SemaphoreType.DMA((17,)), pltpu.SemaphoreType.DMA((3,))],
        name="exchange_grads",
    )(*srcs)


def _sum_adam(parts, w, m, v, name, tb, tc=None):
    R, Cc = w.shape
    n_parts = parts.shape[0]
    tb = R if tc else min(tb, R)
    tc = tc or Cc
    assert R % tb == 0 and Cc % tc == 0
    bc1 = 1.0 - ADAM_B1**ADAM_STEP
    bc2 = 1.0 - ADAM_B2**ADAM_STEP

    def body(p_ref, w_ref, m_ref, v_ref, g_ref, d_ref, nm_ref, nv_ref):
        g = p_ref[0].astype(f32)
        for j in range(1, n_parts):
            g = g + p_ref[j].astype(f32)
        g_ref[...] = g
        m2 = ADAM_B1 * m_ref[...] + (1.0 - ADAM_B1) * g
        v2 = ADAM_B2 * v_ref[...] + (1.0 - ADAM_B2) * (g * g)
        nm_ref[...] = m2
        nv_ref[...] = v2
        d_ref[...] = -ADAM_LR * ((m2 / bc1) / (jnp.sqrt(v2 / bc2) + ADAM_EPS) + ADAM_WD * w_ref[...])

    blk = pl.BlockSpec((tb, tc), lambda i, j: (i, j))
    o = jax.ShapeDtypeStruct((R, Cc), f32)
    return pl.pallas_call(
        body,
        out_shape=(o, o, o, o),
        grid=(R // tb, Cc // tc),
        in_specs=[pl.BlockSpec((n_parts, tb, tc), lambda i, j: (0, i, j)), blk, blk, blk],
        out_specs=(blk, blk, blk, blk),
        compiler_params=_cparams(("parallel", "parallel")),
        name=name,
    )(parts, w, m, v)


_WEIGHTS = ["ln_pre_w", "w_in", "conv_w", "a_log_fwd", "a_log_bwd", "dt_bias_fwd", "dt_bias_bwd", "gdn_norm_w", "w_proj_gdn",
            "gk_w2_fwd", "gk_b2_fwd", "gk_w2_bwd", "gk_b2_bwd", "gla_norm_w", "w_proj_gla", "w_out", "ln_post_w"]
_REP_ORDER = ["ln_pre_w", "ln_post_w", "gk_b2_fwd", "gk_b2_bwd", "gla_norm_w", "gdn_norm_w", "a_log_fwd", "a_log_bwd", "dt_bias_fwd", "dt_bias_bwd"]


def kernel(x, ln_pre_w, w_in, conv_w, a_log_fwd, a_log_bwd, dt_bias_fwd, dt_bias_bwd, gdn_norm_w, w_proj_gdn, gk_w2_fwd, gk_b2_fwd, gk_w2_bwd, gk_b2_bwd, gla_norm_w, w_proj_gla, w_out, ln_post_w, loss_target, m_ln_pre_w, m_w_in, m_conv_w, m_a_log_fwd, m_a_log_bwd, m_dt_bias_fwd, m_dt_bias_bwd, m_gdn_norm_w, m_w_proj_gdn, m_gk_w2_fwd, m_gk_b2_fwd, m_gk_w2_bwd, m_gk_b2_bwd, m_gla_norm_w, m_w_proj_gla, m_w_out, m_ln_post_w, v_ln_pre_w, v_w_in, v_conv_w, v_a_log_fwd, v_a_log_bwd, v_dt_bias_fwd, v_dt_bias_bwd, v_gdn_norm_w, v_w_proj_gdn, v_gk_w2_fwd, v_gk_b2_fwd, v_gk_w2_bwd, v_gk_b2_bwd, v_gla_norm_w, v_w_proj_gla, v_w_out, v_ln_post_w):
    args = locals()
    W = {n: args[n] for n in _WEIGHTS}
    M = {n: args["m_" + n] for n in _WEIGHTS}
    V = {n: args["v_" + n] for n in _WEIGHTS}

    wp_stack = lambda P: jnp.concatenate([P["w_proj_gdn"], P["w_proj_gla"], P["w_out"]], axis=0)
    sm_pack = lambda P: _pack_small_shard(P["conv_w"][0], P["gk_w2_fwd"][0], P["gk_w2_bwd"][0])
    me = 4 * lax.axis_index("x") + 2 * lax.axis_index("y") + lax.axis_index("c")
    w_in_t, m_in_t, v_in_t = (jnp.transpose(a[0]) for a in (w_in, m_w_in, v_w_in))
    g_in, g_sm = _gather_weights(w_in_t, sm_pack(W))
    wp_own = wp_stack(W).astype(MXU_DTYPE)
    wp_started = _exchange_start(wp_own, False, "gather_proj_start")

    def proj_weights(after):
        own, land = _exchange_wait(wp_started, after, False, "gather_proj_wait")
        g_p = lax.dynamic_update_slice(land, own[None], (me, 0, 0, 0))
        return tuple(g_p[:, i].reshape(D_MODEL, D_MODEL) for i in range(3))

    rows_p = 3 * D_MODEL // N_DEV
    sent = {}

    def send_proj_grads(d_wpg, d_wpl, d_wout):
        p_p = jnp.stack([d_wpg, d_wpl, d_wout]).reshape(3, N_DEV, D_MODEL // N_DEV, D_MODEL).transpose(1, 0, 2, 3)
        sent["p_p"] = p_p.reshape(N_DEV, rows_p, D_MODEL)
        sent["started"] = _exchange_start(sent["p_p"], True, "exchange_proj_start")
        return sent["started"][4][0:1, 0:1]

    wcat = _to_cat(jnp.transpose(g_in.reshape(N_IN, D_MODEL)))
    conv_full = g_sm[:, 0:CONV_K, :].transpose(1, 0, 2).reshape(CONV_K, N_DEV * SM_COLS)
    w2f_full = g_sm[:, 8:24, 0:64].transpose(1, 0, 2).reshape(16, 512)
    w2b_full = g_sm[:, 8:24, 64:128].transpose(1, 0, 2).reshape(16, 512)

    avec = _lane_row(a_log_fwd, a_log_bwd)
    dvec = _lane_row(dt_bias_fwd, dt_bias_bwd)
    G = _local_step(x[0], loss_target[0], wcat, conv_full, avec, dvec, gdn_norm_w, _pad_w2(w2f_full, 32), _pad_w2(w2b_full, 48),
                    gk_b2_fwd, gk_b2_bwd, gla_norm_w, proj_weights, ln_pre_w + wp_started[4][0:1, 0:1], ln_post_w, send_proj_grads)

    p_in = jnp.transpose(_from_cat(G["d_wcat"])).reshape(N_DEV, SHARD_IN, D_MODEL)
    p_sm = _pack_small_shard(G["d_conv"].reshape(CONV_K, N_DEV, SM_COLS).transpose(1, 0, 2),
                             G["d_w2f"][32:48].reshape(16, N_DEV, 64).transpose(1, 0, 2),
                             G["d_w2b"][48:64].reshape(16, N_DEV, 64).transpose(1, 0, 2))
    p_rep = _pack_rep(G["d_wpre"], G["d_wpost"], G["d_b2f"], G["d_b2b"], G["d_gla_norm"], G["d_gdn_norm"],
                      G["d_alog"][:, 0:8], G["d_alog"][:, 8:16], G["d_dtb"][:, 0:8], G["d_dtb"][:, 8:16],
                      loss=0.5 * jnp.sum(G["loss_row"], axis=1, keepdims=True) / D_MODEL)
    q_in = _pair_sum(p_in, _exchange_sibling(p_in), "pair_sum_w_in")
    r_in, r_sm, r_rep = _exchange_grads(q_in, p_sm, p_rep)
    p_p, land_p = _exchange_wait(sent["started"], r_in, True, "exchange_proj_wait")
    r_p = lax.dynamic_update_slice(land_p, lax.dynamic_index_in_dim(p_p, me, 0, keepdims=True), (me, 0, 0))

    o_in = _sum_adam(r_in, w_in_t, m_in_t, v_in_t, "adam_w_in", SHARD_IN, tc=LANES)
    o_p = _sum_adam(r_p, wp_stack(W).reshape(rows_p, D_MODEL), wp_stack(M).reshape(rows_p, D_MODEL),
                    wp_stack(V).reshape(rows_p, D_MODEL), "adam_w_proj", 128)
    o_sm = _sum_adam(r_sm, sm_pack(W), sm_pack(M), sm_pack(V), "adam_small", SM_ROWS)
    rep_pack = lambda P: _pack_rep(*[P[n] for n in _REP_ORDER])
    o_rep = _sum_adam(r_rep, rep_pack(W), rep_pack(M), rep_pack(V), "adam_rep", REP_ROWS)

    res = []
    for kind in range(4):
        rep = _unpack_rep(o_rep[kind])
        per = dict(rep)
        per["w_in"] = jnp.transpose(o_in[kind])[None]
        pp = o_p[kind].reshape(3, 1, D_MODEL // N_DEV, D_MODEL)
        per["w_proj_gdn"], per["w_proj_gla"], per["w_out"] = pp[0], pp[1], pp[2]
        per["conv_w"] = o_sm[kind][None, 0:CONV_K, :]
        per["gk_w2_fwd"] = o_sm[kind][None, 8:24, 0:64]
        per["gk_w2_bwd"] = o_sm[kind][None, 8:24, 64:128]
        res.append([per[n] for n in _WEIGHTS])
    loss = o_rep[0][LOSS_ROW, 0]
    return (loss, G["grad_x"][None], *res[0], *res[1], *res[2], *res[3])
```

```python
import functools
import math

import jax
import jax.numpy as jnp
from jax import lax
from jax.experimental import pallas as pl
from jax.experimental.pallas import tpu as pltpu

f32 = jnp.float32
MXU_DTYPE = jnp.bfloat16
HI = lax.Precision.HIGHEST

D_MODEL = 1024
CHUNK = 64
NORM_EPS = 1e-6
GDN_HEADS, GDN_DIM = 8, 128
GLA_HEADS, GLA_DK, GLA_DV = 4, 128, 256
GATE_NORMALIZER = 16.0
CONV_K = 5
N_IN = 9280
N_DEV = 8
SHARD_IN = N_IN // N_DEV

N_CAT = 9728
SMALL_OFF = 9216
LANES = 128

ADAM_LR, ADAM_B1, ADAM_B2, ADAM_EPS, ADAM_WD, ADAM_STEP = 0.001, 0.9, 0.999, 1e-08, 0.01, 10

VMEM_LIMIT = 56 * 1024 * 1024


def _cparams(sem=None):
    return pltpu.CompilerParams(dimension_semantics=sem, vmem_limit_bytes=VMEM_LIMIT)


_DN = {"nn": (((1,), (0,)), ((), ())), "nt": (((1,), (1,)), ((), ())), "tn": (((0,), (0,)), ((), ()))}
_BDN = {"nn": (((2,), (1,)), ((0,), (0,))), "nt": (((2,), (2,)), ((0,), (0,))), "tn": (((1,), (1,)), ((0,), (0,)))}


def _dot16(a, b, kind):
    return lax.dot_general(a, b, (_DN if a.ndim == 2 else _BDN)[kind], preferred_element_type=f32)


def _mx(a, b, kind):
    return _dot16(a.astype(MXU_DTYPE), b.astype(MXU_DTYPE), kind)


def _split2(a):
    hi = a.astype(jnp.bfloat16)
    return hi, (a - hi.astype(f32)).astype(jnp.bfloat16)


def _split3(a):
    p1 = a.astype(jnp.bfloat16)
    r = a - p1.astype(f32)
    p2 = r.astype(jnp.bfloat16)
    return p1, p2, (r - p2.astype(f32)).astype(jnp.bfloat16)


def _h3s(a2, b2, kind):
    (ah, al), (bh, bl) = a2, b2
    return _dot16(ah, bh, kind) + (_dot16(ah, bl, kind) + _dot16(al, bh, kind))


def _h3(a, b, kind):
    return _h3s(_split2(a), _split2(b), kind)


def _exact01(t01, x, kind="nn"):
    t = t01.astype(jnp.bfloat16)
    x1, x2, x3 = _split3(x)
    return _dot16(t, x1, kind) + (_dot16(t, x2, kind) + _dot16(t, x3, kind))


def _colsum_col(e):
    ones = jnp.ones(e.shape[:-1] + (LANES,), jnp.bfloat16)
    e1, e2, e3 = _split3(e)
    return (_dot16(e1, ones, "tn") + (_dot16(e2, ones, "tn") + _dot16(e3, ones, "tn")))[..., 0:1]


def _iota3(shape, axis):
    return lax.broadcasted_iota(jnp.int32, shape, axis)


def _dir_masks(B, nf, Cn):
    shp = (B, Cn, Cn)
    bb, ii, jj = _iota3(shp, 0), _iota3(shp, 1), _iota3(shp, 2)
    fwd = bb < nf
    rev = jnp.logical_not(fwd)
    incl = jnp.logical_or(jnp.logical_and(fwd, ii >= jj), jnp.logical_and(rev, ii <= jj))
    strict = jnp.logical_and(incl, ii != jj)
    return incl, strict, ii == jj


def _last_row(x, nf):
    B, Cn = x.shape[0], x.shape[1]
    fwd = _iota3((B, 1, 1), 0) < nf
    return jnp.where(fwd, x[:, Cn - 1 : Cn, :], x[:, 0:1, :])


def _add_at_last_row(x, val, nf):
    B, Cn = x.shape[0], x.shape[1]
    bb, rr = _iota3((B, Cn, 1), 0), _iota3((B, Cn, 1), 1)
    at = jnp.logical_or(jnp.logical_and(bb < nf, rr == Cn - 1), jnp.logical_and(bb >= nf, rr == 0))
    return x + jnp.where(at, val, 0.0)


def _sigmoid(x):
    return jax.nn.sigmoid(x)


def _silu(x):
    return x * _sigmoid(x)


def _silu_grad(x):
    s = _sigmoid(x)
    return s * (1.0 + x * (1.0 - s))


def _softplus(x):
    u = jnp.exp(-jnp.abs(x))
    l1p = jnp.where(u < 1e-3, u * (1.0 - u * (0.5 - u * (1.0 / 3.0))), jnp.log(1.0 + u))
    return jnp.maximum(x, 0.0) + l1p


def _iota2(shape, axis):
    return lax.broadcasted_iota(jnp.int32, shape, axis)


def _matmul(a, b, kind, out_dtype, name, tm, tn, tk, out_t=False):
    if kind == "nn":
        (M, K), N = a.shape, b.shape[1]
    elif kind == "nt":
        (M, K), N = a.shape, b.shape[0]
    else:
        (K, M), N = a.shape, b.shape[1]
    tm, tn, tk = min(tm, M), min(tn, N), min(tk, K)
    assert M % tm == 0 and N % tn == 0 and K % tk == 0, (name, M, N, K)
    nk = K // tk
    if kind == "tn":
        a_spec = pl.BlockSpec((tk, tm), lambda i, j, k: (k, i))
    else:
        a_spec = pl.BlockSpec((tm, tk), lambda i, j, k: (i, k))
    if kind == "nt":
        b_spec = pl.BlockSpec((tn, tk), lambda i, j, k: (j, k))
    else:
        b_spec = pl.BlockSpec((tk, tn), lambda i, j, k: (k, j))

    def body(a_ref, b_ref, o_ref, *acc):
        p = _mx(a_ref[...], b_ref[...], kind)
        if nk == 1:
            o_ref[...] = (p.T if out_t else p).astype(out_dtype)
        else:
            assert not out_t
            acc_ref = acc[0]
            k = pl.program_id(2)

            @pl.when(k == 0)
            def _():
                acc_ref[...] = p

            @pl.when(k > 0)
            def _():
                acc_ref[...] += p

            @pl.when(k == nk - 1)
            def _():
                o_ref[...] = acc_ref[...].astype(out_dtype)

    return pl.pallas_call(
        body,
        out_shape=jax.ShapeDtypeStruct((N, M) if out_t else (M, N), out_dtype),
        grid=(M // tm, N // tn, nk),
        in_specs=[a_spec, b_spec],
        out_specs=pl.BlockSpec((tn, tm), lambda i, j, k: (j, i)) if out_t else pl.BlockSpec((tm, tn), lambda i, j, k: (i, j)),
        scratch_shapes=[] if nk == 1 else [pltpu.VMEM((tm, tn), f32)],
        compiler_params=_cparams(("parallel", "parallel", "arbitrary")),
        name=name,
    )(a, b)


def _rms_pre(x, w, tb=256):
    T = x.shape[0]

    def body(x_ref, w_ref, h_ref, ht_ref):
        xv = x_ref[...]
        r = lax.rsqrt(jnp.mean(xv * xv, axis=1, keepdims=True) + NORM_EPS)
        h = xv * r * w_ref[...]
        h_ref[...] = h.astype(h_ref.dtype)
        ht_ref[...] = h.T.astype(ht_ref.dtype)

    return pl.pallas_call(
        body,
        out_shape=(jax.ShapeDtypeStruct((T, D_MODEL), MXU_DTYPE), jax.ShapeDtypeStruct((D_MODEL, T), MXU_DTYPE)),
        grid=(T // tb,),
        in_specs=[pl.BlockSpec((tb, D_MODEL), lambda i: (i, 0)), pl.BlockSpec((1, D_MODEL), lambda i: (0, 0))],
        out_specs=(pl.BlockSpec((tb, D_MODEL), lambda i: (i, 0)), pl.BlockSpec((D_MODEL, tb), lambda i: (0, i))),
        compiler_params=_cparams(("parallel",)),
        name="rms_pre",
    )(x, w)


def _rms_pre_bwd(x, w, dh, dy, tb=256):
    T = x.shape[0]

    def body(x_ref, w_ref, dh_ref, dy_ref, dx_ref, dw_ref):
        i = pl.program_id(0)
        xv, dhv = x_ref[...], dh_ref[...]
        r = lax.rsqrt(jnp.mean(xv * xv, axis=1, keepdims=True) + NORM_EPS)
        dhw = dhv * w_ref[...]
        dx_ref[...] = dy_ref[...] + r * dhw - xv * (r * r * r) * jnp.mean(dhw * xv, axis=1, keepdims=True)
        part = jnp.sum(dhv * xv * r, axis=0, keepdims=True)

        @pl.when(i == 0)
        def _():
            dw_ref[...] = part

        @pl.when(i > 0)
        def _():
            dw_ref[...] += part

    blk = pl.BlockSpec((tb, D_MODEL), lambda i: (i, 0))
    row = pl.BlockSpec((1, D_MODEL), lambda i: (0, 0))
    return pl.pallas_call(
        body,
        out_shape=(jax.ShapeDtypeStruct((T, D_MODEL), f32), jax.ShapeDtypeStruct((1, D_MODEL), f32)),
        grid=(T // tb,),
        in_specs=[blk, row, blk, blk],
        out_specs=(blk, row),
        compiler_params=_cparams(("arbitrary",)),
        name="rms_pre_bwd",
    )(x, w, dh, dy)


def _conv5(u, cw):
    T = u.shape[0]
    t = _iota2(u.shape, 0)
    acc = u * cw[2:3, :]
    for j in (0, 1, 3, 4):
        d = j - 2
        sh = pltpu.roll(u, (-d) % T, 0)
        valid = jnp.logical_and(t + d >= 0, t + d < T)
        acc = acc + jnp.where(valid, sh, 0.0) * cw[j : j + 1, :]
    return acc


def _gdn_prep(proj, conv_w):
    T = proj.shape[0]
    nh = GDN_HEADS

    def body(u_ref, cw_ref, o_ref):
        j = pl.program_id(0)
        s = _silu(_conv5(u_ref[...], cw_ref[...]))

        @pl.when(j < 2 * nh)
        def _():
            o_ref[0] = s * lax.rsqrt(jnp.sum(s * s, axis=1, keepdims=True) + NORM_EPS)

        @pl.when(j >= 2 * nh)
        def _():
            o_ref[0] = s

    return pl.pallas_call(
        body,
        out_shape=jax.ShapeDtypeStruct((3, T, GDN_HEADS * GDN_DIM), f32),
        grid=(3 * nh,),
        in_specs=[pl.BlockSpec((T, GDN_DIM), lambda j: (0, j)), pl.BlockSpec((CONV_K, GDN_DIM), lambda j: (0, j))],
        out_specs=pl.BlockSpec((1, T, GDN_DIM), lambda j: (j // nh, 0, j % nh)),
        compiler_params=_cparams(("parallel",)),
        name="gdn_prep",
    )(proj, conv_w)


def _gdn_prep_bwd(proj, conv_w, dqkv_f, dqkv_b, dproj):
    T = proj.shape[0]
    nh = GDN_HEADS

    def body(u_ref, cw_ref, df_ref, db_ref, _, du_ref, dcw_ref):
        j = pl.program_id(0)
        u, cw = u_ref[...], cw_ref[...]
        c = _conv5(u, cw)
        s = _silu(c)
        dn = df_ref[0] + db_ref[0]
        rinv = lax.rsqrt(jnp.sum(s * s, axis=1, keepdims=True) + NORM_EPS)
        ds_norm = rinv * dn - s * (rinv * rinv * rinv) * jnp.sum(dn * s, axis=1, keepdims=True)
        ds = jnp.where(j < 2 * nh, ds_norm, dn)
        dc = ds * _silu_grad(c)
        t = _iota2(u.shape, 0)
        du = dc * cw[2:3, :]
        rows = []
        for jj in range(CONV_K):
            d = jj - 2
            if d == 0:
                rows.append(jnp.sum(dc * u, axis=0, keepdims=True))
                continue
            ush = pltpu.roll(u, (-d) % T, 0)
            uvalid = jnp.logical_and(t + d >= 0, t + d < T)
            rows.append(jnp.sum(dc * jnp.where(uvalid, ush, 0.0), axis=0, keepdims=True))
            dsh = pltpu.roll(dc, d % T, 0)
            dvalid = jnp.logical_and(t - d >= 0, t - d < T)
            du = du + jnp.where(dvalid, dsh, 0.0) * cw[jj : jj + 1, :]
        du_ref[...] = du.astype(du_ref.dtype)
        for jj in range(CONV_K):
            dcw_ref[jj : jj + 1, :] = rows[jj]

    qspec = pl.BlockSpec((1, T, GDN_DIM), lambda j: (j // nh, 0, j % nh))
    return pl.pallas_call(
        body,
        out_shape=(jax.ShapeDtypeStruct(dproj.shape, dproj.dtype), jax.ShapeDtypeStruct((CONV_K, 3 * nh * GDN_DIM), f32)),
        grid=(3 * nh,),
        in_specs=[pl.BlockSpec((T, GDN_DIM), lambda j: (0, j)), pl.BlockSpec((CONV_K, GDN_DIM), lambda j: (0, j)), qspec, qspec, _ANY_SPEC],
        out_specs=(pl.BlockSpec((T, GDN_DIM), lambda j: (0, j)), pl.BlockSpec((CONV_K, GDN_DIM), lambda j: (0, j))),
        input_output_aliases={4: 0},
        compiler_params=_cparams(("parallel",)),
        name="gdn_prep_bwd",
    )(proj, conv_w, dqkv_f, dqkv_b, dproj)


def _chunk_tri(n, rev):
    i, j = _iota2((n, n), 0), _iota2((n, n), 1)
    same = jnp.right_shift(i, 6) == jnp.right_shift(j, 6)
    order = (j >= i) if rev else (j <= i)
    return jnp.where(jnp.logical_and(same, order), 1.0, 0.0).astype(f32)


def _gdn_gates(proj, avec, dvec, tb=256):
    T = proj.shape[0]

    def body(s_ref, a_ref, d_ref, g_ref, gc_ref):
        small = s_ref[...]
        lane = _iota2(small.shape, 1)
        lg = -jnp.exp(a_ref[...]) * _softplus(small + d_ref[...])
        beta = _sigmoid(small)
        g_ref[...] = jnp.where(lane < 16, lg, jnp.where(lane < 32, beta, 0.0))
        lgm = jnp.where(lane < 16, lg, 0.0)
        gcf = _exact01(_chunk_tri(tb, False), lgm)
        gcr = _exact01(_chunk_tri(tb, True), lgm)
        gc_ref[...] = jnp.where(lane < 8, gcf, gcr)

    blk = pl.BlockSpec((tb, LANES), lambda i: (i, 0))
    row = pl.BlockSpec((1, LANES), lambda i: (0, 0))
    return pl.pallas_call(
        body,
        out_shape=(jax.ShapeDtypeStruct((T, LANES), f32), jax.ShapeDtypeStruct((T, LANES), f32)),
        grid=(T // tb,),
        in_specs=[pl.BlockSpec((tb, LANES), lambda i: (i, SMALL_OFF // LANES)), row, row],
        out_specs=(blk, blk),
        compiler_params=_cparams(("parallel",)),
        name="gdn_gates",
    )(proj, avec, dvec)


def _gdn_gates_bwd(proj, avec, dvec, dg_f, dg_b, dsmall_gla, dproj, tb=256):
    T = proj.shape[0]
    pad = N_CAT - SMALL_OFF

    def body(s_ref, a_ref, d_ref, gf_ref, gb_ref, dl_ref, _, ds_ref, da_ref, dd_ref):
        i = pl.program_id(0)
        small = s_ref[...]
        lane = _iota2(small.shape, 1)
        dgate = gf_ref[...] + gb_ref[...]
        z = small + d_ref[...]
        nega = -jnp.exp(a_ref[...])
        dz = dgate * nega * _sigmoid(z)
        beta = _sigmoid(small)
        dsm = jnp.where(lane < 16, dz, jnp.where(lane < 32, dgate * beta * (1.0 - beta), 0.0))
        ds_ref[:, :LANES] = (dsm + dl_ref[...]).astype(ds_ref.dtype)
        ds_ref[:, LANES:] = jnp.zeros((tb, pad - LANES), ds_ref.dtype)
        lg = nega * _softplus(z)
        pa = jnp.sum(jnp.where(lane < 16, dgate * lg, 0.0), axis=0, keepdims=True)
        pd = jnp.sum(jnp.where(lane < 16, dz, 0.0), axis=0, keepdims=True)

        @pl.when(i == 0)
        def _():
            da_ref[...] = pa
            dd_ref[...] = pd

        @pl.when(i > 0)
        def _():
            da_ref[...] += pa
            dd_ref[...] += pd

    blk = pl.BlockSpec((tb, LANES), lambda i: (i, 0))
    row = pl.BlockSpec((1, LANES), lambda i: (0, 0))
    return pl.pallas_call(
        body,
        out_shape=(jax.ShapeDtypeStruct(dproj.shape, dproj.dtype), jax.ShapeDtypeStruct((1, LANES), f32), jax.ShapeDtypeStruct((1, LANES), f32)),
        grid=(T // tb,),
        in_specs=[pl.BlockSpec((tb, LANES), lambda i: (i, SMALL_OFF // LANES)), row, row, blk, blk, blk, _ANY_SPEC],
        out_specs=(pl.BlockSpec((tb, pad), lambda i: (i, DP_SMALL_BLK)), row, row),
        input_output_aliases={6: 0},
        compiler_params=_cparams(("arbitrary",)),
        name="gdn_gates_bwd",
    )(proj, avec, dvec, dg_f, dg_b, dsmall_gla, dproj)


GDN_HB = 8


def _inv_unit_lower(L, eye):
    A = eye - L
    P2 = _split2(L)
    for _ in range(5):
        P2 = _split2(_h3s(P2, P2, "nn"))
        A = A + _h3s(_split2(A), P2, "nn")
    return A


def _lane_col(blk, idx):
    lane = _iota2(blk.shape, 1)
    return jnp.sum(jnp.where(lane == idx, blk, 0.0), axis=1, keepdims=True)


class _GdnChunk:
    def __init__(self, q, k, v, gcol, beta, rev, A=None):
        Cn = q.shape[0]
        ii, jj = _iota2((Cn, Cn), 0), _iota2((Cn, Cn), 1)
        self.incl = (ii <= jj) if rev else (ii >= jj)
        self.strict = (ii < jj) if rev else (ii > jj)
        grow = jnp.broadcast_to(gcol, (Cn, LANES)).T[0:1, :]
        diff = gcol - grow
        self.decay = jnp.where(self.incl, jnp.exp(jnp.where(self.incl, diff, 0.0)), 0.0)
        self.k, self.v, self.beta, self.gcol = k, v, beta, gcol
        self.qs = q * (GDN_DIM**-0.5)
        self.kb = k * beta
        self.L = jnp.where(self.strict, _mx(self.kb, k, "nt") * self.decay, 0.0)
        if A is None:
            A = _inv_unit_lower(self.L, jnp.where(ii == jj, 1.0, 0.0).astype(f32))
        self.A2 = _split2(A)
        self.A = A
        self.eg = jnp.exp(gcol)
        self.kbg = self.kb * self.eg
        self.sol = _h3s(self.A2, _split2(jnp.concatenate([v * beta, self.kbg], axis=1)), "nn")
        self.u = self.sol[:, :GDN_DIM]
        self.w = self.sol[:, GDN_DIM:]
        self.attn = jnp.where(self.incl, _mx(self.qs, k, "nt") * self.decay, 0.0)
        self.qd = self.qs * self.eg
        last = 0 if rev else Cn - 1
        self.last = last
        gl = gcol[last : last + 1, :]
        self.ekd = jnp.exp(gl - gcol)
        self.kd = k * self.ekd
        self.egl = jnp.exp(gl)


def _gdn_scan_fwd(qkvn, gates, gc):
    T = qkvn.shape[1]
    N = T // CHUNK
    nh, hb = GDN_HEADS, GDN_HB
    W = hb * GDN_DIM
    chains = [(d, hh) for d in range(2) for hh in range(hb)]

    def body(qf_ref, qb_ref, gf_ref, gb_ref, cf_ref, cb_ref, of_ref, ob_ref, sf_ref, sb_ref, af_ref, ab_ref, s_scr):
        n, b = pl.program_id(0), pl.program_id(1)

        @pl.when(n == 0)
        def _():
            for d, hh in chains:
                s_scr[d, b * hb + hh] = jnp.zeros((GDN_DIM, GDN_DIM), f32)

        dirs = ((qf_ref, gf_ref, cf_ref, of_ref, sf_ref, af_ref), (qb_ref, gb_ref, cb_ref, ob_ref, sb_ref, ab_ref))
        s_in = {ch: s_scr[ch[0], b * hb + ch[1]] for ch in chains}
        s_out = {}
        for d, hh in chains:
            q_ref, g_ref, c_ref, o_ref, st_ref, a_ref = dirs[d]
            h = b * hb + hh
            sl = slice(hh * GDN_DIM, (hh + 1) * GDN_DIM)
            gcol = _lane_col(c_ref[...], h + 8 * d)
            beta = _lane_col(g_ref[...], 16 + h + 8 * d)
            ck = _GdnChunk(q_ref[0, :, sl], q_ref[1, :, sl], q_ref[2, :, sl], gcol, beta, d == 1)
            S = s_in[d, hh]
            st_ref[0, hh] = S
            a_ref[0, hh] = ck.A
            v_new = ck.u - _mx(ck.w, S, "nn")
            o_ref[:, sl] = _mx(ck.qd, S, "nn") + _mx(ck.attn, v_new, "nn")
            s_out[d, hh] = S * ck.egl + _mx(ck.kd, v_new, "tn")
        for d, hh in chains:
            s_scr[d, b * hb + hh] = s_out[d, hh]

    fw, bw = (lambda n: n), (lambda n: N - 1 - n)
    qsp = lambda ci: pl.BlockSpec((3, CHUNK, W), lambda n, b: (0, ci(n), b))
    gsp = lambda ci: pl.BlockSpec((CHUNK, LANES), lambda n, b: (ci(n), 0))
    osp = lambda ci: pl.BlockSpec((CHUNK, W), lambda n, b: (ci(n), b))
    ssp = lambda ci: pl.BlockSpec((1, hb, GDN_DIM, GDN_DIM), lambda n, b: (ci(n), b, 0, 0))
    asp = lambda ci: pl.BlockSpec((1, hb, CHUNK, CHUNK), lambda n, b: (ci(n), b, 0, 0))
    o_shape = jax.ShapeDtypeStruct((T, nh * GDN_DIM), f32)
    s_shape = jax.ShapeDtypeStruct((N, nh, GDN_DIM, GDN_DIM), f32)
    a_shape = jax.ShapeDtypeStruct((N, nh, CHUNK, CHUNK), f32)
    return pl.pallas_call(
        body,
        out_shape=(o_shape, o_shape, s_shape, s_shape, a_shape, a_shape),
        grid=(N, nh // hb),
        in_specs=[qsp(fw), qsp(bw), gsp(fw), gsp(bw), gsp(fw), gsp(bw)],
        out_specs=(osp(fw), osp(bw), ssp(fw), ssp(bw), asp(fw), asp(bw)),
        scratch_shapes=[pltpu.VMEM((2, nh, GDN_DIM, GDN_DIM), f32)],
        compiler_params=_cparams(("arbitrary", "arbitrary")),
        name="gdn_scan_fwd",
    )(qkvn, qkvn, gates, gates, gc, gc)


def _gdn_scan_bwd(qkvn, gates, gc, st_f, st_b, a_f, a_b, do):
    T = qkvn.shape[1]
    N = T // CHUNK
    nh, hb = GDN_HEADS, GDN_HB
    nb = nh // hb
    W = hb * GDN_DIM
    chains = [(d, hh) for d in range(2) for hh in range(hb)]

    def body(qf_ref, qb_ref, gf_ref, gb_ref, cf_ref, cb_ref, sf_ref, sb_ref, af_ref, ab_ref, dof_ref, dob_ref,
             dqf_ref, dqb_ref, dgf_ref, dgb_ref, ds_scr):
        n, b = pl.program_id(0), pl.program_id(1)

        @pl.when(n == 0)
        def _():
            for d, hh in chains:
                ds_scr[d, b * hb + hh] = jnp.zeros((GDN_DIM, GDN_DIM), f32)

        @pl.when(b == 0)
        def _():
            dgf_ref[...] = jnp.zeros((CHUNK, LANES), f32)
            dgb_ref[...] = jnp.zeros((CHUNK, LANES), f32)

        dirs = ((qf_ref, gf_ref, cf_ref, sf_ref, af_ref, dof_ref, dqf_ref), (qb_ref, gb_ref, cb_ref, sb_ref, ab_ref, dob_ref, dqb_ref))
        ds_in = {ch: ds_scr[ch[0], b * hb + ch[1]] for ch in chains}
        ds_out = {}
        lane = _iota2((CHUNK, LANES), 1)
        acc = [dgf_ref[...], dgb_ref[...]]
        for d, hh in chains:
            q_ref, g_ref, c_ref, st_ref, a_ref, do_ref, dq_ref = dirs[d]
            h = b * hb + hh
            sl = slice(hh * GDN_DIM, (hh + 1) * GDN_DIM)
            gcol = _lane_col(c_ref[...], h + 8 * d)
            beta = _lane_col(g_ref[...], 16 + h + 8 * d)
            ck = _GdnChunk(q_ref[0, :, sl], q_ref[1, :, sl], q_ref[2, :, sl], gcol, beta, d == 1, A=a_ref[0, hh])
            k, v = ck.k, ck.v
            S = st_ref[0, hh]
            dS = ds_in[d, hh]
            dov = do_ref[:, sl]
            v_new = ck.u - _mx(ck.w, S, "nn")
            d_vnew = _mx(ck.attn, dov, "tn") + _mx(ck.kd, dS, "nn")
            d_attn = jnp.where(ck.incl, _mx(dov, v_new, "nt"), 0.0)
            d_qd = _mx(dov, S, "nt")
            d_kd = _mx(v_new, dS, "nt")
            d_gl = jnp.sum(jnp.sum(dS * S, axis=1, keepdims=True), axis=0, keepdims=True) * ck.egl
            ds_out[d, hh] = dS * ck.egl + _mx(ck.qd, dov, "tn") - _mx(ck.w, d_vnew, "tn")
            d_w = -_mx(d_vnew, S, "nt")
            d_rhs = _h3s(ck.A2, _split2(jnp.concatenate([d_vnew, d_w], axis=1)), "tn")
            d_vb, d_kbg = d_rhs[:, :GDN_DIM], d_rhs[:, GDN_DIM:]
            dL = -jnp.where(ck.strict, _h3(d_rhs, ck.sol, "nt"), 0.0)
            P = dL * ck.decay
            Q = d_attn * ck.decay
            d_kb = _mx(P, k, "nn") + d_kbg * ck.eg
            d_qs = _mx(Q, k, "nn") + d_qd * ck.eg
            d_k = _mx(P, ck.kb, "tn") + _mx(Q, ck.qs, "tn") + d_kd * ck.ekd + d_kb * ck.beta
            E = dL * ck.L + d_attn * ck.attn
            kdsum = jnp.sum(d_kd * ck.kd, axis=1, keepdims=True)
            d_gc = (jnp.sum(E, axis=1, keepdims=True) - _colsum_col(E)
                    + jnp.sum(d_kbg * ck.kbg, axis=1, keepdims=True)
                    + jnp.sum(d_qd * ck.qd, axis=1, keepdims=True) - kdsum)
            d_gl = d_gl + jnp.sum(kdsum, axis=0, keepdims=True)
            row = _iota2((CHUNK, 1), 0)
            d_gc = d_gc + jnp.where(row == ck.last, d_gl, 0.0)
            d_beta = jnp.sum(d_vb * v, axis=1, keepdims=True) + jnp.sum(d_kb * k, axis=1, keepdims=True)
            dq_ref[0, :, sl] = d_qs * (GDN_DIM**-0.5)
            dq_ref[1, :, sl] = d_k
            dq_ref[2, :, sl] = d_vb * ck.beta
            acc[d] = jnp.where(lane == h + 8 * d, d_gc, acc[d])
            acc[d] = jnp.where(lane == 16 + h + 8 * d, d_beta, acc[d])
        for d, hh in chains:
            ds_scr[d, b * hb + hh] = ds_out[d, hh]
        dgf_ref[...] = acc[0]
        dgb_ref[...] = acc[1]

        @pl.when(b == nb - 1)
        def _():
            for d, dg_ref in enumerate((dgf_ref, dgb_ref)):
                blk = dg_ref[...]
                dg_ref[...] = jnp.where(lane < 16, _exact01(_chunk_tri(CHUNK, d == 0), blk), blk)

    fw, bw = (lambda n: N - 1 - n), (lambda n: n)
    qsp = lambda ci: pl.BlockSpec((3, CHUNK, W), lambda n, b: (0, ci(n), b))
    gsp = lambda ci: pl.BlockSpec((CHUNK, LANES), lambda n, b: (ci(n), 0))
    osp = lambda ci: pl.BlockSpec((CHUNK, W), lambda n, b: (ci(n), b))
    ssp = lambda ci: pl.BlockSpec((1, hb, GDN_DIM, GDN_DIM), lambda n, b: (ci(n), b, 0, 0))
    asp = lambda ci: pl.BlockSpec((1, hb, CHUNK, CHUNK), lambda n, b: (ci(n), b, 0, 0))
    dq_shape = jax.ShapeDtypeStruct((3, T, nh * GDN_DIM), f32)
    dg_shape = jax.ShapeDtypeStruct((T, LANES), f32)
    return pl.pallas_call(
        body,
        out_shape=(dq_shape, dq_shape, dg_shape, dg_shape),
        grid=(N, nb),
        in_specs=[qsp(fw), qsp(bw), gsp(fw), gsp(bw), gsp(fw), gsp(bw), ssp(fw), ssp(bw), asp(fw), asp(bw), osp(fw), osp(bw)],
        out_specs=(qsp(fw), qsp(bw), gsp(fw), gsp(bw)),
        scratch_shapes=[pltpu.VMEM((2, nh, GDN_DIM, GDN_DIM), f32)],
        compiler_params=_cparams(("arbitrary", "arbitrary")),
        name="gdn_scan_bwd",
    )(qkvn, qkvn, gates, gates, gc, gc, st_f, st_b, a_f, a_b, do, do)


def _out_norm(o_f, o_b, proj, gate_blk, w, hd, name, tb=256):
    T, W = o_f.shape
    nh = W // hd

    def body(of_ref, ob_ref, z_ref, w_ref, y_ref):
        wv = w_ref[...]
        for i in range(nh):
            sl = slice(i * hd, (i + 1) * hd)
            o = of_ref[:, sl] + ob_ref[:, sl]
            r = lax.rsqrt(jnp.mean(o * o, axis=1, keepdims=True) + NORM_EPS)
            y_ref[:, sl] = (o * r * wv * _silu(z_ref[:, sl])).astype(y_ref.dtype)

    blk = pl.BlockSpec((tb, W), lambda i: (i, 0))
    return pl.pallas_call(
        body,
        out_shape=jax.ShapeDtypeStruct((T, W), MXU_DTYPE),
        grid=(T // tb,),
        in_specs=[blk, blk, pl.BlockSpec((tb, W), lambda i: (i, gate_blk)), pl.BlockSpec((1, hd), lambda i: (0, 0))],
        out_specs=blk,
        compiler_params=_cparams(("parallel",)),
        name=name,
    )(o_f, o_b, proj, w)


def _out_norm_bwd(dy, o_f, o_b, proj, gate_blk, w, hd, name, dproj, tb=256):
    T, W = o_f.shape
    nh = W // hd

    def body(dy_ref, of_ref, ob_ref, z_ref, w_ref, _, do_ref, dz_ref, dw_ref):
        i = pl.program_id(0)
        wv = w_ref[...]
        dw = jnp.zeros((1, hd), f32)
        for a in range(nh):
            sl = slice(a * hd, (a + 1) * hd)
            o = of_ref[:, sl] + ob_ref[:, sl]
            z = z_ref[:, sl]
            dyv = dy_ref[:, sl]
            r = lax.rsqrt(jnp.mean(o * o, axis=1, keepdims=True) + NORM_EPS)
            orr = o * r
            dn = dyv * _silu(z)
            dz_ref[:, sl] = (dyv * orr * wv * _silu_grad(z)).astype(dz_ref.dtype)
            dw = dw + jnp.sum(dn * orr, axis=0, keepdims=True)
            dnw = dn * wv
            do_ref[:, sl] = r * dnw - o * (r * r * r) * jnp.mean(dnw * o, axis=1, keepdims=True)

        @pl.when(i == 0)
        def _():
            dw_ref[...] = dw

        @pl.when(i > 0)
        def _():
            dw_ref[...] += dw

    blk = pl.BlockSpec((tb, W), lambda i: (i, 0))
    row = pl.BlockSpec((1, hd), lambda i: (0, 0))
    return pl.pallas_call(
        body,
        out_shape=(jax.ShapeDtypeStruct((T, W), f32), jax.ShapeDtypeStruct(dproj.shape, dproj.dtype), jax.ShapeDtypeStruct((1, hd), f32)),
        grid=(T // tb,),
        in_specs=[blk, blk, blk, pl.BlockSpec((tb, W), lambda i: (i, gate_blk)), row, _ANY_SPEC],
        out_specs=(blk, pl.BlockSpec((tb, W), lambda i: (i, gate_blk)), row),
        input_output_aliases={5: 1},
        compiler_params=_cparams(("arbitrary",)),
        name=name,
    )(dy, o_f, o_b, proj, w, dproj)


def _gla_prep(proj, w2f, w2b, b2f, b2b, tb=256):
    T = proj.shape[0]
    W = GLA_HEADS * GLA_DK

    def body(s_ref, wf_ref, wb_ref, bf_ref, bb_ref, gf_ref, gb_ref):
        small = s_ref[...]
        gkf = -_softplus(-(_mx(small, wf_ref[...], "nn") + bf_ref[...])) * (1.0 / GATE_NORMALIZER)
        gkb = -_softplus(-(_mx(small, wb_ref[...], "nn") + bb_ref[...])) * (1.0 / GATE_NORMALIZER)
        gf_ref[...] = _exact01(_chunk_tri(tb, False), gkf)
        gb_ref[...] = _exact01(_chunk_tri(tb, True), gkb)

    blk = pl.BlockSpec((tb, W), lambda i: (i, 0))
    wsp = pl.BlockSpec((LANES, W), lambda i: (0, 0))
    row = pl.BlockSpec((1, W), lambda i: (0, 0))
    return pl.pallas_call(
        body,
        out_shape=(jax.ShapeDtypeStruct((T, W), f32), jax.ShapeDtypeStruct((T, W), f32)),
        grid=(T // tb,),
        in_specs=[pl.BlockSpec((tb, LANES), lambda i: (i, SMALL_OFF // LANES)), wsp, wsp, row, row],
        out_specs=(blk, blk),
        compiler_params=_cparams(("parallel",)),
        name="gla_prep",
    )(proj, w2f, w2b, b2f, b2b)


def _gla_prep_bwd(proj, w2f, w2b, b2f, b2b, dgk_f, dgk_b, tb=256):
    T = proj.shape[0]
    W = GLA_HEADS * GLA_DK

    def body(s_ref, wf_ref, wb_ref, bf_ref, bb_ref, df_ref, db_ref, ds_ref, dwf_ref, dwb_ref, dbf_ref, dbb_ref):
        i = pl.program_id(0)
        small = s_ref[...]
        dsm = jnp.zeros((tb, LANES), f32)
        parts = []
        for w_ref, b_ref, d_ref in ((wf_ref, bf_ref, df_ref), (wb_ref, bb_ref, db_ref)):
            pre = _mx(small, w_ref[...], "nn") + b_ref[...]
            dpre = d_ref[...] * (1.0 / GATE_NORMALIZER) * _sigmoid(-pre)
            dsm = dsm + _mx(dpre, w_ref[...], "nt")
            parts.append((_mx(small, dpre, "tn"), jnp.sum(dpre, axis=0, keepdims=True)))
        ds_ref[...] = dsm

        @pl.when(i == 0)
        def _():
            dwf_ref[...], dbf_ref[...] = parts[0]
            dwb_ref[...], dbb_ref[...] = parts[1]

        @pl.when(i > 0)
        def _():
            dwf_ref[...] += parts[0][0]
            dbf_ref[...] += parts[0][1]
            dwb_ref[...] += parts[1][0]
            dbb_ref[...] += parts[1][1]

    blk = pl.BlockSpec((tb, W), lambda i: (i, 0))
    wsp = pl.BlockSpec((LANES, W), lambda i: (0, 0))
    row = pl.BlockSpec((1, W), lambda i: (0, 0))
    return pl.pallas_call(
        body,
        out_shape=(jax.ShapeDtypeStruct((T, LANES), f32), jax.ShapeDtypeStruct((LANES, W), f32), jax.ShapeDtypeStruct((LANES, W), f32),
                   jax.ShapeDtypeStruct((1, W), f32), jax.ShapeDtypeStruct((1, W), f32)),
        grid=(T // tb,),
        in_specs=[pl.BlockSpec((tb, LANES), lambda i: (i, SMALL_OFF // LANES)), wsp, wsp, row, row, blk, blk],
        out_specs=(pl.BlockSpec((tb, LANES), lambda i: (i, 0)), wsp, wsp, row, row),
        compiler_params=_cparams(("arbitrary",)),
        name="gla_prep_bwd",
    )(proj, w2f, w2b, b2f, b2b, dgk_f, dgk_b)


Q_B_BLK, K_B_BLK, V_B_BLK = 4096 // GLA_DK, 4608 // GLA_DK, 5120 // GLA_DV


class _GlaChunk:
    def __init__(self, q, k, G, rev):
        Cn = q.shape[0]
        ii, jj = _iota2((Cn, Cn), 0), _iota2((Cn, Cn), 1)
        self.incl = (ii <= jj) if rev else (ii >= jj)
        self.eG = jnp.exp(G)
        self.enG = jnp.exp(-G)
        self.qg = q * (GLA_DK**-0.5) * self.eG
        self.kg = k * self.enG
        self.attn = jnp.where(self.incl, _mx(self.qg, self.kg, "nt"), 0.0)
        last = 0 if rev else Cn - 1
        self.last = last
        gl = G[last : last + 1, :]
        self.ekd = jnp.exp(gl - G)
        self.kd = k * self.ekd
        self.egl = jnp.exp(gl)


def _gla_scan_fwd(proj, G_f, G_b):
    T = proj.shape[0]
    N = T // CHUNK
    nh = GLA_HEADS

    def body(qf_ref, kf_ref, vf_ref, gf_ref, qb_ref, kb_ref, vb_ref, gb_ref, of_ref, ob_ref, sf_ref, sb_ref, s_scr):
        n, h = pl.program_id(0), pl.program_id(1)

        @pl.when(n == 0)
        def _():
            s_scr[0, h] = jnp.zeros((GLA_DV, GLA_DK), f32)
            s_scr[1, h] = jnp.zeros((GLA_DV, GLA_DK), f32)

        for d, (q_ref, k_ref, v_ref, g_ref, o_ref, st_ref) in enumerate(
            ((qf_ref, kf_ref, vf_ref, gf_ref, of_ref, sf_ref), (qb_ref, kb_ref, vb_ref, gb_ref, ob_ref, sb_ref))
        ):
            ck = _GlaChunk(q_ref[...], k_ref[...], g_ref[...], d == 1)
            v = v_ref[...]
            St = s_scr[d, h]
            st_ref[0, 0] = St
            o_ref[...] = _mx(ck.qg, St, "nt") + _mx(ck.attn, v, "nn")
            s_scr[d, h] = St * ck.egl + _mx(v, ck.kd, "tn")

    def specs(cidx):
        return [
            pl.BlockSpec((CHUNK, GLA_DK), lambda n, h: (cidx(n), Q_B_BLK + h)),
            pl.BlockSpec((CHUNK, GLA_DK), lambda n, h: (cidx(n), K_B_BLK + h)),
            pl.BlockSpec((CHUNK, GLA_DV), lambda n, h: (cidx(n), V_B_BLK + h)),
            pl.BlockSpec((CHUNK, GLA_DK), lambda n, h: (cidx(n), h)),
        ]

    fw, bw = (lambda n: n), (lambda n: N - 1 - n)
    o_shape = jax.ShapeDtypeStruct((T, nh * GLA_DV), f32)
    s_shape = jax.ShapeDtypeStruct((N, nh, GLA_DV, GLA_DK), f32)
    return pl.pallas_call(
        body,
        out_shape=(o_shape, o_shape, s_shape, s_shape),
        grid=(N, nh),
        in_specs=specs(fw) + specs(bw),
        out_specs=(
            pl.BlockSpec((CHUNK, GLA_DV), lambda n, h: (n, h)),
            pl.BlockSpec((CHUNK, GLA_DV), lambda n, h: (N - 1 - n, h)),
            pl.BlockSpec((1, 1, GLA_DV, GLA_DK), lambda n, h: (n, h, 0, 0)),
            pl.BlockSpec((1, 1, GLA_DV, GLA_DK), lambda n, h: (N - 1 - n, h, 0, 0)),
        ),
        scratch_shapes=[pltpu.VMEM((2, nh, GLA_DV, GLA_DK), f32)],
        compiler_params=_cparams(("arbitrary", "arbitrary")),
        name="gla_scan_fwd",
    )(proj, proj, proj, G_f, proj, proj, proj, G_b)


def _gla_scan_bwd(proj, G_f, G_b, st_f, st_b, do):
    T = proj.shape[0]
    N = T // CHUNK
    nh = GLA_HEADS

    def body(qf_ref, kf_ref, vf_ref, gf_ref, sf_ref, dof_ref, qb_ref, kb_ref, vb_ref, gb_ref, sb_ref, dob_ref,
             dqf_ref, dkf_ref, dvf_ref, dgf_ref, dqb_ref, dkb_ref, dvb_ref, dgb_ref, ds_scr):
        n, h = pl.program_id(0), pl.program_id(1)

        @pl.when(n == 0)
        def _():
            ds_scr[0, h] = jnp.zeros((GLA_DV, GLA_DK), f32)
            ds_scr[1, h] = jnp.zeros((GLA_DV, GLA_DK), f32)

        for d, (q_ref, k_ref, v_ref, g_ref, st_ref, do_ref, dq_ref, dk_ref, dv_ref, dg_ref) in enumerate(
            ((qf_ref, kf_ref, vf_ref, gf_ref, sf_ref, dof_ref, dqf_ref, dkf_ref, dvf_ref, dgf_ref),
             (qb_ref, kb_ref, vb_ref, gb_ref, sb_ref, dob_ref, dqb_ref, dkb_ref, dvb_ref, dgb_ref))
        ):
            rev = d == 1
            ck = _GlaChunk(q_ref[...], k_ref[...], g_ref[...], rev)
            v = v_ref[...]
            St = st_ref[0, 0]
            dSt = ds_scr[d, h]
            dov = do_ref[...]
            d_attn = jnp.where(ck.incl, _mx(dov, v, "nt"), 0.0)
            d_qg = _mx(dov, St, "nn") + _mx(d_attn, ck.kg, "nn")
            d_kg = _mx(d_attn, ck.qg, "tn")
            dv_ref[...] = _mx(ck.attn, dov, "tn") + _mx(ck.kd, dSt, "nt")
            d_kd = _mx(v, dSt, "nn")
            ds_scr[d, h] = dSt * ck.egl + _mx(dov, ck.qg, "tn")
            kdd = d_kd * ck.kd
            d_gl = jnp.sum(dSt * St, axis=0, keepdims=True) * ck.egl + jnp.sum(kdd, axis=0, keepdims=True)
            dq_ref[...] = d_qg * ck.eG * (GLA_DK**-0.5)
            dk_ref[...] = d_kg * ck.enG + d_kd * ck.ekd
            d_G = d_qg * ck.qg - d_kg * ck.kg - kdd
            row = _iota2((CHUNK, GLA_DK), 0)
            d_G = d_G + jnp.where(row == ck.last, d_gl, 0.0)
            dg_ref[...] = _exact01(_chunk_tri(CHUNK, not rev), d_G)

    def in_specs(cidx):
        return [
            pl.BlockSpec((CHUNK, GLA_DK), lambda n, h: (cidx(n), Q_B_BLK + h)),
            pl.BlockSpec((CHUNK, GLA_DK), lambda n, h: (cidx(n), K_B_BLK + h)),
            pl.BlockSpec((CHUNK, GLA_DV), lambda n, h: (cidx(n), V_B_BLK + h)),
            pl.BlockSpec((CHUNK, GLA_DK), lambda n, h: (cidx(n), h)),
            pl.BlockSpec((1, 1, GLA_DV, GLA_DK), lambda n, h: (cidx(n), h, 0, 0)),
            pl.BlockSpec((CHUNK, GLA_DV), lambda n, h: (cidx(n), h)),
        ]

    def out_specs(cidx):
        return [
            pl.BlockSpec((CHUNK, GLA_DK), lambda n, h: (cidx(n), h)),
            pl.BlockSpec((CHUNK, GLA_DK), lambda n, h: (cidx(n), h)),
            pl.BlockSpec((CHUNK, GLA_DV), lambda n, h: (cidx(n), h)),
            pl.BlockSpec((CHUNK, GLA_DK), lambda n, h: (cidx(n), h)),
        ]

    fw, bw = (lambda n: N - 1 - n), (lambda n: n)
    k_shape = jax.ShapeDtypeStruct((T, nh * GLA_DK), f32)
    v_shape = jax.ShapeDtypeStruct((T, nh * GLA_DV), f32)
    return pl.pallas_call(
        body,
        out_shape=(k_shape, k_shape, v_shape, k_shape) * 2,
        grid=(N, nh),
        in_specs=in_specs(fw) + in_specs(bw),
        out_specs=tuple(out_specs(fw) + out_specs(bw)),
        scratch_shapes=[pltpu.VMEM((2, nh, GLA_DV, GLA_DK), f32)],
        compiler_params=_cparams(("arbitrary", "arbitrary")),
        name="gla_scan_bwd",
    )(proj, proj, proj, G_f, st_f, do, proj, proj, proj, G_b, st_b, do)


GATE_A_BLK, GATE_B_BLK, Z_BLK, G_B_BLK = 6144 // 1024, 7168 // 1024, 3072 // 1024, 8192 // 1024
DP_GATES_BLK, DP_GLA_QKV_BLK, DP_SMALL_BLK = 6144 // 2048, 4096 // 2048, SMALL_OFF // 512
_ANY_SPEC = pl.BlockSpec(memory_space=pl.ANY)


def _mid(og, obg, proj, x, target, wpg, wpl, wout, w_post, tb=256):
    T = x.shape[0]
    Dm = D_MODEL

    def body(og_ref, obg_ref, ga_ref, gb_ref, x_ref, t_ref, wpg_ref, wpl_ref, wo_ref, wp_ref,
             loss_ref, dwp_ref, dy_ref, m_ref, dout_ref, dya_ref, dyb_ref, dgates_ref, dog_ref, dobg_ref):
        i = pl.program_id(0)
        ya = _mx(og_ref[...], wpg_ref[...], "nn")
        yb = _mx(obg_ref[...], wpl_ref[...], "nn")
        sa, sb = _sigmoid(ga_ref[...]), _sigmoid(gb_ref[...])
        m = sa * ya + sb * yb
        m_ref[...] = m.astype(m_ref.dtype)
        out = _mx(m, wo_ref[...], "nn")
        r = lax.rsqrt(jnp.mean(out * out, axis=1, keepdims=True) + NORM_EPS)
        wp = wp_ref[...]
        e = x_ref[...] + out * r * wp - t_ref[...]
        dy = e * (1.0 / Dm)
        dy_ref[...] = dy
        lpart = jnp.sum(e * e, axis=0, keepdims=True)
        wpart = jnp.sum(dy * out * r, axis=0, keepdims=True)
        dyn = dy * wp
        dout = r * dyn - out * (r * r * r) * jnp.mean(dyn * out, axis=1, keepdims=True)
        dout_ref[...] = dout.astype(dout_ref.dtype)
        dm = _mx(dout, wo_ref[...], "nt")
        dya, dyb = dm * sa, dm * sb
        dya_ref[...] = dya.astype(dya_ref.dtype)
        dyb_ref[...] = dyb.astype(dyb_ref.dtype)
        dgates_ref[:, :Dm] = (dm * ya * sa * (1.0 - sa)).astype(dgates_ref.dtype)
        dgates_ref[:, Dm:] = (dm * yb * sb * (1.0 - sb)).astype(dgates_ref.dtype)
        dog_ref[...] = _mx(dya, wpg_ref[...], "nt")
        dobg_ref[...] = _mx(dyb, wpl_ref[...], "nt")

        @pl.when(i == 0)
        def _():
            loss_ref[...] = lpart
            dwp_ref[...] = wpart

        @pl.when(i > 0)
        def _():
            loss_ref[...] += lpart
            dwp_ref[...] += wpart

    blk = pl.BlockSpec((tb, Dm), lambda i: (i, 0))
    row = pl.BlockSpec((1, Dm), lambda i: (0, 0))
    wsp = pl.BlockSpec((Dm, Dm), lambda i: (0, 0))
    act = jax.ShapeDtypeStruct((T, Dm), MXU_DTYPE)
    big = jax.ShapeDtypeStruct((T, Dm), f32)
    vec = jax.ShapeDtypeStruct((1, Dm), f32)
    return pl.pallas_call(
        body,
        out_shape=(vec, vec, big, act, act, act, act, jax.ShapeDtypeStruct((T, N_CAT), MXU_DTYPE), big, big),
        grid=(T // tb,),
        in_specs=[blk, blk, pl.BlockSpec((tb, Dm), lambda i: (i, GATE_A_BLK)), pl.BlockSpec((tb, Dm), lambda i: (i, GATE_B_BLK)),
                  blk, blk, wsp, wsp, wsp, row],
        out_specs=(row, row) + (blk,) * 5 + (pl.BlockSpec((tb, 2 * Dm), lambda i: (i, DP_GATES_BLK)), blk, blk),
        compiler_params=_cparams(("arbitrary",)),
        name="mid",
    )(og, obg, proj, proj, x, target, wpg, wpl, wout, w_post)


def _heads(ref, lead, n, width):
    return [ref[lead + (slice(None), slice(i * width, (i + 1) * width))] for i in range(n)]


class _GdnBatch:
    def __init__(self, q, k, v, gcol, grow, beta, nf, A=None):
        B, Cn = q.shape[0], q.shape[1]
        self.nf = nf
        self.incl, self.strict, eye = _dir_masks(B, nf, Cn)
        self.decay = jnp.where(self.incl, jnp.exp(jnp.where(self.incl, gcol - grow, 0.0)), 0.0)
        self.k, self.v, self.beta = k, v, beta
        self.qs = q * (GDN_DIM**-0.5)
        self.kb = k * beta
        self.L = jnp.where(self.strict, _mx(self.kb, k, "nt") * self.decay, 0.0)
        if A is None:
            A = _inv_unit_lower(self.L, jnp.where(eye, 1.0, 0.0).astype(f32))
        self.A = A
        self.A2 = _split2(A)
        self.eg = jnp.exp(gcol)
        self.kbg = self.kb * self.eg
        self.sol = _h3s(self.A2, _split2(jnp.concatenate([v * beta, self.kbg], axis=2)), "nn")
        self.u = self.sol[:, :, :GDN_DIM]
        self.w = self.sol[:, :, GDN_DIM:]
        self.attn = jnp.where(self.incl, _mx(self.qs, k, "nt") * self.decay, 0.0)
        self.qd = self.qs * self.eg
        gl = _last_row(gcol, nf)
        self.ekd = jnp.exp(gl - gcol)
        self.kd = k * self.ekd
        self.egl = jnp.exp(gl)


def _gdn_chain_inputs(qf_ref, qb_ref, gf_ref, gb_ref, cf_ref, cb_ref, b, hb):
    qkv = [jnp.stack(_heads(qf_ref, (i,), hb, GDN_DIM) + _heads(qb_ref, (i,), hb, GDN_DIM)) for i in range(3)]
    cols, rows, betas = [], [], []
    for d, (g_ref, c_ref) in enumerate(((gf_ref, cf_ref), (gb_ref, cb_ref))):
        gates, gc = g_ref[...], c_ref[...]
        for hh in range(hb):
            h = b * hb + hh
            col = _lane_col(gc, h + 8 * d)
            cols.append(col)
            rows.append(jnp.broadcast_to(col, (CHUNK, LANES)).T[0:1, :])
            betas.append(_lane_col(gates, 16 + h + 8 * d))
    return qkv[0], qkv[1], qkv[2], jnp.stack(cols), jnp.stack(rows), jnp.stack(betas)


def _gdn_fwd(qkvn, gates, gc):
    T = qkvn.shape[1]
    N = T // CHUNK
    nh, hb = GDN_HEADS, GDN_HB
    W = hb * GDN_DIM

    def body(qf_ref, qb_ref, gf_ref, gb_ref, cf_ref, cb_ref, of_ref, ob_ref, sf_ref, sb_ref, af_ref, ab_ref, s_scr):
        n, b = pl.program_id(0), pl.program_id(1)
        hs = pl.ds(b * hb, hb)

        @pl.when(n == 0)
        def _():
            s_scr[0, hs] = jnp.zeros((hb, GDN_DIM, GDN_DIM), f32)
            s_scr[1, hs] = jnp.zeros((hb, GDN_DIM, GDN_DIM), f32)

        q, k, v, gcol, grow, beta = _gdn_chain_inputs(qf_ref, qb_ref, gf_ref, gb_ref, cf_ref, cb_ref, b, hb)
        ck = _GdnBatch(q, k, v, gcol, grow, beta, hb)
        S = jnp.concatenate([s_scr[0, hs], s_scr[1, hs]], axis=0)
        sf_ref[0], sb_ref[0] = S[:hb], S[hb:]
        af_ref[0], ab_ref[0] = ck.A[:hb], ck.A[hb:]
        v_new = ck.u - _mx(ck.w, S, "nn")
        o = _mx(ck.qd, S, "nn") + _mx(ck.attn, v_new, "nn")
        S_new = S * ck.egl + _mx(ck.kd, v_new, "tn")
        for hh in range(hb):
            sl = slice(hh * GDN_DIM, (hh + 1) * GDN_DIM)
            of_ref[:, sl] = o[hh]
            ob_ref[:, sl] = o[hb + hh]
        s_scr[0, hs] = S_new[:hb]
        s_scr[1, hs] = S_new[hb:]

    fw, bw = (lambda n: n), (lambda n: N - 1 - n)
    qsp = lambda ci: pl.BlockSpec((3, CHUNK, W), lambda n, b: (0, ci(n), b))
    gsp = lambda ci: pl.BlockSpec((CHUNK, LANES), lambda n, b: (ci(n), 0))
    osp = lambda ci: pl.BlockSpec((CHUNK, W), lambda n, b: (ci(n), b))
    ssp = lambda ci: pl.BlockSpec((1, hb, GDN_DIM, GDN_DIM), lambda n, b: (ci(n), b, 0, 0))
    asp = lambda ci: pl.BlockSpec((1, hb, CHUNK, CHUNK), lambda n, b: (ci(n), b, 0, 0))
    o_shape = jax.ShapeDtypeStruct((T, nh * GDN_DIM), f32)
    s_shape = jax.ShapeDtypeStruct((N, nh, GDN_DIM, GDN_DIM), f32)
    a_shape = jax.ShapeDtypeStruct((N, nh, CHUNK, CHUNK), f32)
    return pl.pallas_call(
        body,
        out_shape=(o_shape, o_shape, s_shape, s_shape, a_shape, a_shape),
        grid=(N, nh // hb),
        in_specs=[qsp(fw), qsp(bw), gsp(fw), gsp(bw), gsp(fw), gsp(bw)],
        out_specs=(osp(fw), osp(bw), ssp(fw), ssp(bw), asp(fw), asp(bw)),
        scratch_shapes=[pltpu.VMEM((2, nh, GDN_DIM, GDN_DIM), f32)],
        compiler_params=_cparams(("arbitrary", "arbitrary")),
        name="gdn_scan_fwd",
    )(qkvn, qkvn, gates, gates, gc, gc)


def _gdn_bwd(qkvn, gates, gc, st_f, st_b, a_f, a_b, do):
    T = qkvn.shape[1]
    N = T // CHUNK
    nh, hb = GDN_HEADS, GDN_HB
    nb = nh // hb
    W = hb * GDN_DIM

    def body(qf_ref, qb_ref, gf_ref, gb_ref, cf_ref, cb_ref, sf_ref, sb_ref, af_ref, ab_ref, dof_ref, dob_ref,
             dqf_ref, dqb_ref, dgf_ref, dgb_ref, ds_scr):
        n, b = pl.program_id(0), pl.program_id(1)
        hs = pl.ds(b * hb, hb)

        @pl.when(n == 0)
        def _():
            ds_scr[0, hs] = jnp.zeros((hb, GDN_DIM, GDN_DIM), f32)
            ds_scr[1, hs] = jnp.zeros((hb, GDN_DIM, GDN_DIM), f32)

        @pl.when(b == 0)
        def _():
            dgf_ref[...] = jnp.zeros((CHUNK, LANES), f32)
            dgb_ref[...] = jnp.zeros((CHUNK, LANES), f32)

        q, k, v, gcol, grow, beta = _gdn_chain_inputs(qf_ref, qb_ref, gf_ref, gb_ref, cf_ref, cb_ref, b, hb)
        ck = _GdnBatch(q, k, v, gcol, grow, beta, hb, A=jnp.concatenate([af_ref[0], ab_ref[0]], axis=0))
        S = jnp.concatenate([sf_ref[0], sb_ref[0]], axis=0)
        dS = jnp.concatenate([ds_scr[0, hs], ds_scr[1, hs]], axis=0)
        dov = jnp.stack(_heads(dof_ref, (), hb, GDN_DIM) + _heads(dob_ref, (), hb, GDN_DIM))
        v_new = ck.u - _mx(ck.w, S, "nn")
        d_vnew = _mx(ck.attn, dov, "tn") + _mx(ck.kd, dS, "nn")
        d_attn = jnp.where(ck.incl, _mx(dov, v_new, "nt"), 0.0)
        d_qd = _mx(dov, S, "nt")
        d_kd = _mx(v_new, dS, "nt")
        d_gl = jnp.sum(jnp.sum(dS * S, axis=2, keepdims=True), axis=1, keepdims=True) * ck.egl
        dS_new = dS * ck.egl + _mx(ck.qd, dov, "tn") - _mx(ck.w, d_vnew, "tn")
        ds_scr[0, hs] = dS_new[:hb]
        ds_scr[1, hs] = dS_new[hb:]
        d_w = -_mx(d_vnew, S, "nt")
        d_rhs = _h3s(ck.A2, _split2(jnp.concatenate([d_vnew, d_w], axis=2)), "tn")
        d_vb, d_kbg = d_rhs[:, :, :GDN_DIM], d_rhs[:, :, GDN_DIM:]
        dL = -jnp.where(ck.strict, _h3(d_rhs, ck.sol, "nt"), 0.0)
        P = dL * ck.decay
        Q = d_attn * ck.decay
        d_kb = _mx(P, k, "nn") + d_kbg * ck.eg
        d_qs = _mx(Q, k, "nn") + d_qd * ck.eg
        d_k = _mx(P, ck.kb, "tn") + _mx(Q, ck.qs, "tn") + d_kd * ck.ekd + d_kb * ck.beta
        E = dL * ck.L + d_attn * ck.attn
        kdsum = jnp.sum(d_kd * ck.kd, axis=2, keepdims=True)
        d_gc = (jnp.sum(E, axis=2, keepdims=True) - _colsum_col(E)
                + jnp.sum(d_kbg * ck.kbg, axis=2, keepdims=True)
                + jnp.sum(d_qd * ck.qd, axis=2, keepdims=True) - kdsum)
        d_gc = _add_at_last_row(d_gc, d_gl + jnp.sum(kdsum, axis=1, keepdims=True), hb)
        d_beta = jnp.sum(d_vb * v, axis=2, keepdims=True) + jnp.sum(d_kb * k, axis=2, keepdims=True)
        d_q = d_qs * (GDN_DIM**-0.5)
        d_v = d_vb * ck.beta
        lane = _iota2((CHUNK, LANES), 1)
        for d, (dq_ref, dg_ref) in enumerate(((dqf_ref, dgf_ref), (dqb_ref, dgb_ref))):
            acc = dg_ref[...]
            for hh in range(hb):
                c, h = d * hb + hh, b * hb + hh
                sl = slice(hh * GDN_DIM, (hh + 1) * GDN_DIM)
                dq_ref[0, :, sl] = d_q[c]
                dq_ref[1, :, sl] = d_k[c]
                dq_ref[2, :, sl] = d_v[c]
                acc = jnp.where(lane == h + 8 * d, d_gc[c], acc)
                acc = jnp.where(lane == 16 + h + 8 * d, d_beta[c], acc)
            dg_ref[...] = acc

        @pl.when(b == nb - 1)
        def _():
            for d, dg_ref in enumerate((dgf_ref, dgb_ref)):
                blk = dg_ref[...]
                dg_ref[...] = jnp.where(lane < 16, _exact01(_chunk_tri(CHUNK, d == 0), blk), blk)

    fw, bw = (lambda n: N - 1 - n), (lambda n: n)
    qsp = lambda ci: pl.BlockSpec((3, CHUNK, W), lambda n, b: (0, ci(n), b))
    gsp = lambda ci: pl.BlockSpec((CHUNK, LANES), lambda n, b: (ci(n), 0))
    osp = lambda ci: pl.BlockSpec((CHUNK, W), lambda n, b: (ci(n), b))
    ssp = lambda ci: pl.BlockSpec((1, hb, GDN_DIM, GDN_DIM), lambda n, b: (ci(n), b, 0, 0))
    asp = lambda ci: pl.BlockSpec((1, hb, CHUNK, CHUNK), lambda n, b: (ci(n), b, 0, 0))
    dq_shape = jax.ShapeDtypeStruct((3, T, nh * GDN_DIM), f32)
    dg_shape = jax.ShapeDtypeStruct((T, LANES), f32)
    return pl.pallas_call(
        body,
        out_shape=(dq_shape, dq_shape, dg_shape, dg_shape),
        grid=(N, nb),
        in_specs=[qsp(fw), qsp(bw), gsp(fw), gsp(bw), gsp(fw), gsp(bw), ssp(fw), ssp(bw), asp(fw), asp(bw), osp(fw), osp(bw)],
        out_specs=(qsp(fw), qsp(bw), gsp(fw), gsp(bw)),
        scratch_shapes=[pltpu.VMEM((2, nh, GDN_DIM, GDN_DIM), f32)],
        compiler_params=_cparams(("arbitrary", "arbitrary")),
        name="gdn_scan_bwd",
    )(qkvn, qkvn, gates, gates, gc, gc, st_f, st_b, a_f, a_b, do, do)


class _GlaBatch:
    def __init__(self, q, k, G, nf):
        B, Cn = q.shape[0], q.shape[1]
        self.incl, _, _ = _dir_masks(B, nf, Cn)
        self.eG = jnp.exp(G)
        self.enG = jnp.exp(-G)
        self.qg = q * (GLA_DK**-0.5) * self.eG
        self.kg = k * self.enG
        self.attn = jnp.where(self.incl, _mx(self.qg, self.kg, "nt"), 0.0)
        gl = _last_row(G, nf)
        self.ekd = jnp.exp(gl - G)
        self.kd = k * self.ekd
        self.egl = jnp.exp(gl)


def _gla_chain_inputs(refs_f, refs_b):
    nh = GLA_HEADS
    out = []
    for i, width in enumerate((GLA_DK, GLA_DK, GLA_DV, GLA_DK)):
        out.append(jnp.stack(_heads(refs_f[i], (), nh, width) + _heads(refs_b[i], (), nh, width)))
    return out


def _gla_specs(N, ci):
    nh = GLA_HEADS
    return [
        pl.BlockSpec((CHUNK, nh * GLA_DK), lambda n: (ci(n), 4096 // (nh * GLA_DK))),
        pl.BlockSpec((CHUNK, nh * GLA_DK), lambda n: (ci(n), 4608 // (nh * GLA_DK))),
        pl.BlockSpec((CHUNK, nh * GLA_DV), lambda n: (ci(n), 5120 // (nh * GLA_DV))),
        pl.BlockSpec((CHUNK, nh * GLA_DK), lambda n: (ci(n), 0)),
    ]


def _gla_fwd(proj, G_f, G_b):
    T = proj.shape[0]
    N = T // CHUNK
    nh = GLA_HEADS

    def body(qf, kf, vf, gf, qb, kb, vb, gb, of_ref, ob_ref, sf_ref, sb_ref, s_scr):
        n = pl.program_id(0)

        @pl.when(n == 0)
        def _():
            s_scr[...] = jnp.zeros((2 * nh, GLA_DV, GLA_DK), f32)

        q, k, v, G = _gla_chain_inputs((qf, kf, vf, gf), (qb, kb, vb, gb))
        ck = _GlaBatch(q, k, G, nh)
        St = s_scr[...]
        sf_ref[0], sb_ref[0] = St[:nh], St[nh:]
        o = _mx(ck.qg, St, "nt") + _mx(ck.attn, v, "nn")
        for i in range(nh):
            sl = slice(i * GLA_DV, (i + 1) * GLA_DV)
            of_ref[:, sl] = o[i]
            ob_ref[:, sl] = o[nh + i]
        s_scr[...] = St * ck.egl + _mx(v, ck.kd, "tn")

    fw, bw = (lambda n: n), (lambda n: N - 1 - n)
    osp = lambda ci: pl.BlockSpec((CHUNK, nh * GLA_DV), lambda n: (ci(n), 0))
    ssp = lambda ci: pl.BlockSpec((1, nh, GLA_DV, GLA_DK), lambda n: (ci(n), 0, 0, 0))
    o_shape = jax.ShapeDtypeStruct((T, nh * GLA_DV), f32)
    s_shape = jax.ShapeDtypeStruct((N, nh, GLA_DV, GLA_DK), f32)
    return pl.pallas_call(
        body,
        out_shape=(o_shape, o_shape, s_shape, s_shape),
        grid=(N,),
        in_specs=_gla_specs(N, fw) + _gla_specs(N, bw),
        out_specs=(osp(fw), osp(bw), ssp(fw), ssp(bw)),
        scratch_shapes=[pltpu.VMEM((2 * nh, GLA_DV, GLA_DK), f32)],
        compiler_params=_cparams(("arbitrary",)),
        name="gla_scan_fwd",
    )(proj, proj, proj, G_f, proj, proj, proj, G_b)


def _gla_bwd(proj, G_f, G_b, st_f, st_b, do):
    T = proj.shape[0]
    N = T // CHUNK
    nh = GLA_HEADS

    def body(qf, kf, vf, gf, sf_ref, dof_ref, qb, kb, vb, gb, sb_ref, dob_ref,
             dqf_ref, dkf_ref, dvf_ref, dgf_ref, dqb_ref, dkb_ref, dvb_ref, dgb_ref, ds_scr):
        n = pl.program_id(0)

        @pl.when(n == 0)
        def _():
            ds_scr[...] = jnp.zeros((2 * nh, GLA_DV, GLA_DK), f32)

        q, k, v, G = _gla_chain_inputs((qf, kf, vf, gf), (qb, kb, vb, gb))
        ck = _GlaBatch(q, k, G, nh)
        St = jnp.concatenate([sf_ref[0], sb_ref[0]], axis=0)
        dSt = ds_scr[...]
        dov = jnp.stack(_heads(dof_ref, (), nh, GLA_DV) + _heads(dob_ref, (), nh, GLA_DV))
        d_attn = jnp.where(ck.incl, _mx(dov, v, "nt"), 0.0)
        d_qg = _mx(dov, St, "nn") + _mx(d_attn, ck.kg, "nn")
        d_kg = _mx(d_attn, ck.qg, "tn")
        d_v = _mx(ck.attn, dov, "tn") + _mx(ck.kd, dSt, "nt")
        d_kd = _mx(v, dSt, "nn")
        ds_scr[...] = dSt * ck.egl + _mx(dov, ck.qg, "tn")
        kdd = d_kd * ck.kd
        d_gl = jnp.sum(dSt * St, axis=1, keepdims=True) * ck.egl + jnp.sum(kdd, axis=1, keepdims=True)
        d_q = d_qg * ck.eG * (GLA_DK**-0.5)
        d_k = d_kg * ck.enG + d_kd * ck.ekd
        d_G = _add_at_last_row(d_qg * ck.qg - d_kg * ck.kg - kdd, d_gl, nh)
        shp = (2 * nh, CHUNK, CHUNK)
        bb, ii, jj = _iota3(shp, 0), _iota3(shp, 1), _iota3(shp, 2)
        tri = jnp.logical_or(jnp.logical_and(bb < nh, jj >= ii), jnp.logical_and(bb >= nh, jj <= ii))
        d_gk = _exact01(jnp.where(tri, 1.0, 0.0), d_G)
        for d, (dq_ref, dk_ref, dv_ref, dg_ref) in enumerate(((dqf_ref, dkf_ref, dvf_ref, dgf_ref), (dqb_ref, dkb_ref, dvb_ref, dgb_ref))):
            for i in range(nh):
                c = d * nh + i
                ks, vs = slice(i * GLA_DK, (i + 1) * GLA_DK), slice(i * GLA_DV, (i + 1) * GLA_DV)
                dq_ref[:, ks] = d_q[c]
                dk_ref[:, ks] = d_k[c]
                dv_ref[:, vs] = d_v[c]
                dg_ref[:, ks] = d_gk[c]

    def in_specs(ci):
        return _gla_specs(N, ci) + [
            pl.BlockSpec((1, nh, GLA_DV, GLA_DK), lambda n: (ci(n), 0, 0, 0)),
            pl.BlockSpec((CHUNK, nh * GLA_DV), lambda n: (ci(n), 0)),
        ]

    def out_specs(ci):
        ksp = pl.BlockSpec((CHUNK, nh * GLA_DK), lambda n: (ci(n), 0))
        return [ksp, ksp, pl.BlockSpec((CHUNK, nh * GLA_DV), lambda n: (ci(n), 0)), ksp]

    fw, bw = (lambda n: N - 1 - n), (lambda n: n)
    k_shape = jax.ShapeDtypeStruct((T, nh * GLA_DK), f32)
    v_shape = jax.ShapeDtypeStruct((T, nh * GLA_DV), f32)
    return pl.pallas_call(
        body,
        out_shape=(k_shape, k_shape, v_shape, k_shape) * 2,
        grid=(N,),
        in_specs=in_specs(fw) + in_specs(bw),
        out_specs=tuple(out_specs(fw) + out_specs(bw)),
        scratch_shapes=[pltpu.VMEM((2 * nh, GLA_DV, GLA_DK), f32)],
        compiler_params=_cparams(("arbitrary",)),
        name="gla_scan_bwd",
    )(proj, proj, proj, G_f, st_f, do, proj, proj, proj, G_b, st_b, do)


def _gla_dsum(dq_f, dq_b, dk_f, dk_b, dv_f, dv_b, dproj, tb=256):
    T = dq_f.shape[0]
    wk, wv = GLA_HEADS * GLA_DK, GLA_HEADS * GLA_DV

    def body(qf, qb, kf, kb, vf, vb, _, o_ref):
        o_ref[:, :wk] = (qf[...] + qb[...]).astype(o_ref.dtype)
        o_ref[:, wk : 2 * wk] = (kf[...] + kb[...]).astype(o_ref.dtype)
        o_ref[:, 2 * wk :] = (vf[...] + vb[...]).astype(o_ref.dtype)

    ksp = pl.BlockSpec((tb, wk), lambda i: (i, 0))
    vsp = pl.BlockSpec((tb, wv), lambda i: (i, 0))
    return pl.pallas_call(
        body,
        out_shape=jax.ShapeDtypeStruct(dproj.shape, dproj.dtype),
        grid=(T // tb,),
        in_specs=[ksp, ksp, ksp, ksp, vsp, vsp, _ANY_SPEC],
        out_specs=pl.BlockSpec((tb, 2 * wk + wv), lambda i: (i, DP_GLA_QKV_BLK)),
        input_output_aliases={6: 0},
        compiler_params=_cparams(("parallel",)),
        name="gla_dsum",
    )(dq_f, dq_b, dk_f, dk_b, dv_f, dv_b, dproj)


def _local_step(x, target, wcat, conv_w, avec, dvec, gdn_norm_w, w2f, w2b, b2f, b2b, gla_norm_w, proj_weights, w_pre, w_post,
                send_proj_grads=None):
    h, h_t = _rms_pre(x, w_pre)
    proj = _matmul(h, wcat, "nt", f32, "proj_in", 2048, 512, 1024)
    qkvn = _gdn_prep(proj, conv_w)
    gates, gc = _gdn_gates(proj, avec, dvec)
    oa_f, oa_b, sa_f, sa_b, inv_f, inv_b = _gdn_fwd(qkvn, gates, gc)
    og = _out_norm(oa_f, oa_b, proj, Z_BLK, gdn_norm_w, GDN_DIM, "gdn_out")
    G_f, G_b = _gla_prep(proj, w2f, w2b, b2f, b2b)
    ob_f, ob_b, sb_f, sb_b = _gla_fwd(proj, G_f, G_b)
    obg = _out_norm(ob_f, ob_b, proj, G_B_BLK, gla_norm_w, GLA_DV, "gla_out")
    wpg, wpl, wout = proj_weights(obg)
    (loss_row, d_wpost, dy, m, dout, dya, dyb, dproj, dog, dobg) = _mid(og, obg, proj, x, target, wpg, wpl, wout, w_post)

    d_wout = _matmul(m, dout, "tn", MXU_DTYPE, "dw_out", 512, 512, 2048)
    d_wpg = _matmul(og, dya, "tn", MXU_DTYPE, "dw_proj_gdn", 512, 512, 2048)
    d_wpl = _matmul(obg, dyb, "tn", MXU_DTYPE, "dw_proj_gla", 512, 512, 2048)
    if send_proj_grads is not None:
        gla_norm_w = gla_norm_w + send_proj_grads(d_wpg, d_wpl, d_wout)

    do_b, dproj, d_gla_norm = _out_norm_bwd(dobg, ob_f, ob_b, proj, G_B_BLK, gla_norm_w, GLA_DV, "gla_out_bwd", dproj)
    dq_f, dk_f, dv_f, dgk_f, dq_b, dk_b, dv_b, dgk_b = _gla_bwd(proj, G_f, G_b, sb_f, sb_b, do_b)
    dproj = _gla_dsum(dq_f, dq_b, dk_f, dk_b, dv_f, dv_b, dproj)
    dsmall_gla, d_w2f, d_w2b, d_b2f, d_b2b = _gla_prep_bwd(proj, w2f, w2b, b2f, b2b, dgk_f, dgk_b)

    do_a, dproj, d_gdn_norm = _out_norm_bwd(dog, oa_f, oa_b, proj, Z_BLK, gdn_norm_w, GDN_DIM, "gdn_out_bwd", dproj)
    dqkv_f, dqkv_b, dg_f, dg_b = _gdn_bwd(qkvn, gates, gc, sa_f, sa_b, inv_f, inv_b, do_a)
    dproj, d_conv = _gdn_prep_bwd(proj, conv_w, dqkv_f, dqkv_b, dproj)
    dproj, d_alog, d_dtb = _gdn_gates_bwd(proj, avec, dvec, dg_f, dg_b, dsmall_gla, dproj)

    d_wcat = _matmul(h_t, dproj, "nn", MXU_DTYPE, "dw_in", 1024, 512, 2048, out_t=True)
    dh = _matmul(dproj, wcat, "nn", f32, "dh", 1024, 1024, 2432)
    grad_x, d_wpre = _rms_pre_bwd(x, w_pre, dh, dy)
    return dict(loss_row=loss_row, grad_x=grad_x, d_wcat=d_wcat, d_conv=d_conv, d_alog=d_alog, d_dtb=d_dtb,
                d_gdn_norm=d_gdn_norm, d_wpg=d_wpg, d_w2f=d_w2f, d_w2b=d_w2b, d_b2f=d_b2f, d_b2b=d_b2b,
                d_gla_norm=d_gla_norm, d_wpl=d_wpl, d_wout=d_wout, d_wpost=d_wpost, d_wpre=d_wpre)


def _to_cat(wf):
    pad = jnp.zeros((N_CAT - N_IN,) + wf.shape[1:], wf.dtype)
    return jnp.concatenate([wf[:4096], wf[4128:6176], wf[7232:9280], wf[6176:7200], wf[4096:4128], wf[7200:7232], pad], axis=0)


def _from_cat(wc):
    return jnp.concatenate([wc[:4096], wc[9216:9248], wc[4096:6144], wc[8192:9216], wc[9248:9280], wc[6144:8192]], axis=0)


def _lane_row(*pieces):
    row = jnp.concatenate(pieces, axis=1)
    return jnp.pad(row, ((0, 0), (0, LANES - row.shape[1])))


def _pad_w2(w2, row0):
    return jnp.pad(w2, ((row0, LANES - row0 - w2.shape[0]), (0, 0)))


SM_ROWS, SM_COLS = 32, 384
REP_ROWS = 32


def _pack_small_shard(conv, w2f, w2b):
    lead = conv.shape[:-2]
    z = lambda r, c: jnp.zeros(lead + (r, c), f32)
    top = jnp.concatenate([conv, z(8 - CONV_K, SM_COLS)], axis=-2)
    mid = jnp.concatenate([w2f, w2b, z(16, SM_COLS - 128)], axis=-1)
    return jnp.concatenate([top, mid, z(SM_ROWS - 24, SM_COLS)], axis=-2)


LOSS_ROW = 28


def _pack_rep(ln_pre, ln_post, b2f, b2b, gla_norm, gdn_norm, alog_f, alog_b, dt_f, dt_b, loss=None):
    last = _lane_row(alog_f, alog_b, dt_f, dt_b)
    extra = jnp.zeros((1, LANES), f32) if loss is None else _lane_row(loss)
    rows = [ln_pre.reshape(8, LANES), ln_post.reshape(8, LANES), b2f.reshape(4, LANES), b2b.reshape(4, LANES),
            gla_norm.reshape(2, LANES), gdn_norm.reshape(1, LANES), last, extra, jnp.zeros((REP_ROWS - LOSS_ROW - 1, LANES), f32)]
    return jnp.concatenate(rows, axis=0)


def _unpack_rep(p):
    return dict(ln_pre_w=p[0:8].reshape(1, 1024), ln_post_w=p[8:16].reshape(1, 1024), gk_b2_fwd=p[16:20].reshape(1, 512),
                gk_b2_bwd=p[20:24].reshape(1, 512), gla_norm_w=p[24:26].reshape(1, 256), gdn_norm_w=p[26:27],
                a_log_fwd=p[27:28, 0:8], a_log_bwd=p[27:28, 8:16], dt_bias_fwd=p[27:28, 16:24], dt_bias_bwd=p[27:28, 24:32])


_MESH = pl.DeviceIdType.MESH


N_GATHERED = 2


def _gather_weights(w_in_s, sm_s):
    shapes = (w_in_s.shape, sm_s.shape)
    dtypes = (MXU_DTYPE, f32)

    def body(win_ref, sm_ref, gin_ref, gsm_ref, send_sems, recv_sems):
        x, y, c = lax.axis_index("x"), lax.axis_index("y"), lax.axis_index("c")
        me, sibling = (x, y, c), (x, y, 1 - c)
        chips = [(1 - x, y), (x, 1 - y), (1 - x, 1 - y)]
        bufs = (gin_ref, gsm_ref)

        def idx(px, py, pc):
            return 4 * px + 2 * py + pc

        for r in range(w_in_s.shape[1] // LANES):
            cols = slice(r * LANES, (r + 1) * LANES)
            gin_ref[idx(*me), :, cols] = win_ref[:, cols].astype(MXU_DTYPE)
        gsm_ref[idx(*me)] = sm_ref[...]

        def copy(a, k, block, to):
            blk = bufs[a].at[idx(*block)]
            return pltpu.make_async_remote_copy(src_ref=blk, dst_ref=blk, send_sem=send_sems.at[7 * a + k],
                                                recv_sem=recv_sems.at[7 * a + k], device_id=to, device_id_type=_MESH)

        first, passed = [], []
        for a in range(N_GATHERED):
            first.append(copy(a, 0, me, sibling))
            first += [copy(a, 1 + j, me, (*chip, c)) for j, chip in enumerate(chips)]
        for cp in first:
            cp.start()
        for j, chip in enumerate(chips):
            for a in range(N_GATHERED):
                copy(a, 1 + j, (*chip, c), me).wait_recv()
                cp = copy(a, 4 + j, (*chip, c), sibling)
                cp.start()
                passed.append(cp)
        for a in range(N_GATHERED):
            copy(a, 0, sibling, me).wait_recv()
            for j, chip in enumerate(chips):
                copy(a, 4 + j, (*chip, 1 - c), me).wait_recv()
        for cp in first + passed:
            cp.wait_send()

    vm = pl.BlockSpec(memory_space=pltpu.VMEM)
    return pl.pallas_call(
        body,
        out_shape=tuple(jax.ShapeDtypeStruct((N_DEV,) + s, d) for s, d in zip(shapes, dtypes)),
        in_specs=[vm, vm],
        out_specs=(vm, vm),
        scratch_shapes=[pltpu.SemaphoreType.DMA((7 * N_GATHERED,)), pltpu.SemaphoreType.DMA((7 * N_GATHERED,))],
        compiler_params=pltpu.CompilerParams(vmem_limit_bytes=VMEM_LIMIT),
        name="gather_weights",
    )(w_in_s, sm_s)


N_CHIP = 4
_EFFECT = pltpu.SideEffectType.DATAFLOW_SIDE_EFFECTING
_HBM_SPEC = pl.BlockSpec(memory_space=pltpu.HBM)
_SEM_SPEC = pl.BlockSpec(memory_space=pltpu.SEMAPHORE)


def _relation_peer(k, x, y, c):
    px = (1 - x) if (k & 4) else x
    py = (1 - y) if (k & 2) else y
    pc = (1 - c) if (k & 1) else c
    return (px, py, pc), 4 * px + 2 * py + pc


def _copies_to_all(src_ref, land_ref, send_sems, recv_sems, blocked, arrival):
    x, y, c = lax.axis_index("x"), lax.axis_index("y"), lax.axis_index("c")
    me = 4 * x + 2 * y + c
    out = []
    for k in range(1, N_DEV):
        dev, p = _relation_peer(k, x, y, c)
        out.append(pltpu.make_async_remote_copy(
            src_ref=src_ref.at[p] if blocked else src_ref, dst_ref=land_ref.at[p if arrival else me],
            send_sem=send_sems.at[k - 1], recv_sem=recv_sems.at[k - 1], device_id=dev, device_id_type=_MESH))
    return out


def _exchange_start(src, blocked, name):
    land = lax.empty((N_DEV,) + (src.shape[1:] if blocked else src.shape), src.dtype)

    def body(src_ref, land_ref, send_sems, recv_sems, src_thru, land_thru, token):
        for cp in _copies_to_all(src_ref, land_ref, send_sems, recv_sems, blocked, False):
            cp.start()
        token[...] = jnp.zeros_like(token)

    return pl.pallas_call(
        body,
        name=name,
        out_shape=(pltpu.SemaphoreType.DMA((N_DEV - 1,)), pltpu.SemaphoreType.DMA((N_DEV - 1,)), pltpu.HBM(src.shape, src.dtype),
                   pltpu.HBM(land.shape, land.dtype), jax.ShapeDtypeStruct((8, LANES), f32)),
        in_specs=(_HBM_SPEC, _HBM_SPEC),
        out_specs=(_SEM_SPEC, _SEM_SPEC, _HBM_SPEC, _HBM_SPEC, pl.BlockSpec(memory_space=pltpu.VMEM)),
        input_output_aliases={0: 2, 1: 3},
        compiler_params=pltpu.CompilerParams(has_side_effects=_EFFECT),
    )(pltpu.with_memory_space_constraint(src, pltpu.HBM), pltpu.with_memory_space_constraint(land, pltpu.HBM))


def _exchange_wait(started, after, blocked, name):
    send_sems, recv_sems, src_thru, land_thru, _ = started

    def body(src_ref, land_ref, send_sems, recv_sems, after_ref, src_dead, got_ref):
        for cp in _copies_to_all(src_ref, land_ref, send_sems, recv_sems, blocked, True):
            cp.wait_send()
            cp.wait_recv()

    return pl.pallas_call(
        body,
        name=name,
        out_shape=(pltpu.HBM(src_thru.shape, src_thru.dtype), pltpu.HBM(land_thru.shape, land_thru.dtype)),
        in_specs=(_HBM_SPEC, _HBM_SPEC, _SEM_SPEC, _SEM_SPEC, pl.BlockSpec(memory_space=pl.ANY)),
        out_specs=(_HBM_SPEC, _HBM_SPEC),
        input_output_aliases={0: 0, 1: 1},
        compiler_params=pltpu.CompilerParams(has_side_effects=_EFFECT),
    )(src_thru, land_thru, send_sems, recv_sems, after)


def _exchange_sibling(g_in):
    def body(in_ref, out_ref, send_sems, recv_sems):
        x, y, c = lax.axis_index("x"), lax.axis_index("y"), lax.axis_index("c")

        def copy(k):
            return pltpu.make_async_remote_copy(src_ref=in_ref.at[2 * k + 1 - c], dst_ref=out_ref.at[k], send_sem=send_sems.at[k],
                                                recv_sem=recv_sems.at[k], device_id=(x, y, 1 - c), device_id_type=_MESH)

        copies = [copy(k) for k in range(N_CHIP)]
        for cp in copies:
            cp.start()
        for cp in copies:
            cp.wait_recv()
        for cp in copies:
            cp.wait_send()

    return pl.pallas_call(
        body,
        out_shape=jax.ShapeDtypeStruct((N_CHIP,) + g_in.shape[1:], g_in.dtype),
        in_specs=[_HBM_SPEC],
        out_specs=_HBM_SPEC,
        scratch_shapes=[pltpu.SemaphoreType.DMA((N_CHIP,)), pltpu.SemaphoreType.DMA((N_CHIP,))],
        name="exchange_sibling",
    )(g_in)


def _pair_sum(g, r, name, tc=LANES):
    _, R, Cc = g.shape

    def body(g_ref, r_ref, o_ref):
        o_ref[...] = (g_ref[...].astype(f32) + r_ref[...].astype(f32)).astype(o_ref.dtype)

    return pl.pallas_call(
        body,
        out_shape=jax.ShapeDtypeStruct(r.shape, r.dtype),
        grid=(N_CHIP, Cc // tc),
        in_specs=[pl.BlockSpec((1, R, tc), lambda k, i: (2 * k + lax.axis_index("c"), 0, i)), pl.BlockSpec((1, R, tc), lambda k, i: (k, 0, i))],
        out_specs=pl.BlockSpec((1, R, tc), lambda k, i: (k, 0, i)),
        compiler_params=_cparams(("parallel", "parallel")),
        name=name,
    )(g, r)


def _exchange_grads(q_in, g_sm, g_rep):
    srcs = (q_in, g_sm, g_rep)

    def body(in0, in1, in2, out0, out1, out2, send_sems, recv_sems, local_sems):
        x, y, c = lax.axis_index("x"), lax.axis_index("y"), lax.axis_index("c")
        me, my_chip = 4 * x + 2 * y + c, 2 * x + y
        ins, outs = (in0, in1, in2), (out0, out1, out2)

        def slots(a, k):
            (px, py, pc), p = _relation_peer(k, x, y, c)
            if a == 0:
                return ins[0].at[2 * px + py], my_chip, 2 * px + py
            return (ins[a].at[p] if a == 1 else ins[a]), me, p

        def copy(a, k, arrival):
            src, there, here = slots(a, k)
            sem = (7 * (a - 1) + k - 1) if a >= 1 else (14 + k // 2 - 1)
            return pltpu.make_async_remote_copy(src_ref=src, dst_ref=outs[a].at[here if arrival else there], send_sem=send_sems.at[sem],
                                                recv_sem=recv_sems.at[sem], device_id=_relation_peer(k, x, y, c)[0], device_id_type=_MESH)

        pairs = [(0, k) for k in (2, 4, 6)] + [(a, k) for a in (1, 2) for k in range(1, 8)]
        sends = [copy(a, k, False) for a, k in pairs]
        for cp in sends:
            cp.start()
        own = (ins[0].at[my_chip], ins[1].at[me], ins[2])
        own_dst = (outs[0].at[my_chip], outs[1].at[me], outs[2].at[me])
        local = [pltpu.make_async_copy(own[a], own_dst[a], local_sems.at[a]) for a in range(3)]
        for cp in local:
            cp.start()
        for a, k in pairs:
            copy(a, k, True).wait_recv()
        for cp in sends:
            cp.wait_send()
        for cp in local:
            cp.wait()

    out_shape = tuple(jax.ShapeDtypeStruct(s.shape if i < 2 else (N_DEV,) + s.shape, s.dtype) for i, s in enumerate(srcs))
    return pl.pallas_call(
        body,
        out_shape=out_shape,
        in_specs=[_HBM_SPEC] * 3,
        out_specs=(_HBM_SPEC,) * 3,
        scratch_shapes=[pltpu.SemaphoreType.DMA((17,)), pltpu.SemaphoreType.DMA((17,)), pltpu.SemaphoreType.DMA((3,))],
        name="exchange_grads",
    )(*srcs)


def _sum_adam(parts, w, m, v, name, tb, tc=None):
    R, Cc = w.shape
    n_parts = parts.shape[0]
    tb = R if tc else min(tb, R)
    tc = tc or Cc
    assert R % tb == 0 and Cc % tc == 0
    bc1 = 1.0 - ADAM_B1**ADAM_STEP
    bc2 = 1.0 - ADAM_B2**ADAM_STEP

    def body(p_ref, w_ref, m_ref, v_ref, g_ref, d_ref, nm_ref, nv_ref):
        g = p_ref[0].astype(f32)
        for j in range(1, n_parts):
            g = g + p_ref[j].astype(f32)
        g_ref[...] = g
        m2 = ADAM_B1 * m_ref[...] + (1.0 - ADAM_B1) * g
        v2 = ADAM_B2 * v_ref[...] + (1.0 - ADAM_B2) * (g * g)
        nm_ref[...] = m2
        nv_ref[...] = v2
        d_ref[...] = -ADAM_LR * ((m2 / bc1) / (jnp.sqrt(v2 / bc2) + ADAM_EPS) + ADAM_WD * w_ref[...])

    blk = pl.BlockSpec((tb, tc), lambda i, j: (i, j))
    o = jax.ShapeDtypeStruct((R, Cc), f32)
    return pl.pallas_call(
        body,
        out_shape=(o, o, o, o),
        grid=(R // tb, Cc // tc),
        in_specs=[pl.BlockSpec((n_parts, tb, tc), lambda i, j: (0, i, j)), blk, blk, blk],
        out_specs=(blk, blk, blk, blk),
        compiler_params=_cparams(("parallel", "parallel")),
        name=name,
    )(parts, w, m, v)


_WEIGHTS = ["ln_pre_w", "w_in", "conv_w", "a_log_fwd", "a_log_bwd", "dt_bias_fwd", "dt_bias_bwd", "gdn_norm_w", "w_proj_gdn",
            "gk_w2_fwd", "gk_b2_fwd", "gk_w2_bwd", "gk_b2_bwd", "gla_norm_w", "w_proj_gla", "w_out", "ln_post_w"]
_REP_ORDER = ["ln_pre_w", "ln_post_w", "gk_b2_fwd", "gk_b2_bwd", "gla_norm_w", "gdn_norm_w", "a_log_fwd", "a_log_bwd", "dt_bias_fwd", "dt_bias_bwd"]


def kernel(x, ln_pre_w, w_in, conv_w, a_log_fwd, a_log_bwd, dt_bias_fwd, dt_bias_bwd, gdn_norm_w, w_proj_gdn, gk_w2_fwd, gk_b2_fwd, gk_w2_bwd, gk_b2_bwd, gla_norm_w, w_proj_gla, w_out, ln_post_w, loss_target, m_ln_pre_w, m_w_in, m_conv_w, m_a_log_fwd, m_a_log_bwd, m_dt_bias_fwd, m_dt_bias_bwd, m_gdn_norm_w, m_w_proj_gdn, m_gk_w2_fwd, m_gk_b2_fwd, m_gk_w2_bwd, m_gk_b2_bwd, m_gla_norm_w, m_w_proj_gla, m_w_out, m_ln_post_w, v_ln_pre_w, v_w_in, v_conv_w, v_a_log_fwd, v_a_log_bwd, v_dt_bias_fwd, v_dt_bias_bwd, v_gdn_norm_w, v_w_proj_gdn, v_gk_w2_fwd, v_gk_b2_fwd, v_gk_w2_bwd, v_gk_b2_bwd, v_gla_norm_w, v_w_proj_gla, v_w_out, v_ln_post_w):
    args = locals()
    W = {n: args[n] for n in _WEIGHTS}
    M = {n: args["m_" + n] for n in _WEIGHTS}
    V = {n: args["v_" + n] for n in _WEIGHTS}

    wp_stack = lambda P: jnp.concatenate([P["w_proj_gdn"], P["w_proj_gla"], P["w_out"]], axis=0)
    sm_pack = lambda P: _pack_small_shard(P["conv_w"][0], P["gk_w2_fwd"][0], P["gk_w2_bwd"][0])
    me = 4 * lax.axis_index("x") + 2 * lax.axis_index("y") + lax.axis_index("c")
    w_in_t, m_in_t, v_in_t = (jnp.transpose(a[0]) for a in (w_in, m_w_in, v_w_in))
    g_in, g_sm = _gather_weights(w_in_t, sm_pack(W))
    wp_own = wp_stack(W).astype(MXU_DTYPE)
    wp_started = _exchange_start(wp_own, False, "gather_proj_start")

    def proj_weights(after):
        own, land = _exchange_wait(wp_started, after, False, "gather_proj_wait")
        g_p = lax.dynamic_update_slice(land, own[None], (me, 0, 0, 0))
        return tuple(g_p[:, i].reshape(D_MODEL, D_MODEL) for i in range(3))

    rows_p = 3 * D_MODEL // N_DEV
    sent = {}

    def send_proj_grads(d_wpg, d_wpl, d_wout):
        p_p = jnp.stack([d_wpg, d_wpl, d_wout]).reshape(3, N_DEV, D_MODEL // N_DEV, D_MODEL).transpose(1, 0, 2, 3)
        sent["p_p"] = p_p.reshape(N_DEV, rows_p, D_MODEL)
        sent["started"] = _exchange_start(sent["p_p"], True, "exchange_proj_start")
        return sent["started"][4][0:1, 0:1]

    wcat = _to_cat(g_in.reshape(N_IN, D_MODEL))
    conv_full = g_sm[:, 0:CONV_K, :].transpose(1, 0, 2).reshape(CONV_K, N_DEV * SM_COLS)
    w2f_full = g_sm[:, 8:24, 0:64].transpose(1, 0, 2).reshape(16, 512)
    w2b_full = g_sm[:, 8:24, 64:128].transpose(1, 0, 2).reshape(16, 512)

    avec = _lane_row(a_log_fwd, a_log_bwd)
    dvec = _lane_row(dt_bias_fwd, dt_bias_bwd)
    G = _local_step(x[0], loss_target[0], wcat, conv_full, avec, dvec, gdn_norm_w, _pad_w2(w2f_full, 32), _pad_w2(w2b_full, 48),
                    gk_b2_fwd, gk_b2_bwd, gla_norm_w, proj_weights, ln_pre_w + wp_started[4][0:1, 0:1], ln_post_w, send_proj_grads)

    p_in = _from_cat(G["d_wcat"]).reshape(N_DEV, SHARD_IN, D_MODEL)
    p_sm = _pack_small_shard(G["d_conv"].reshape(CONV_K, N_DEV, SM_COLS).transpose(1, 0, 2),
                             G["d_w2f"][32:48].reshape(16, N_DEV, 64).transpose(1, 0, 2),
                             G["d_w2b"][48:64].reshape(16, N_DEV, 64).transpose(1, 0, 2))
    p_rep = _pack_rep(G["d_wpre"], G["d_wpost"], G["d_b2f"], G["d_b2b"], G["d_gla_norm"], G["d_gdn_norm"],
                      G["d_alog"][:, 0:8], G["d_alog"][:, 8:16], G["d_dtb"][:, 0:8], G["d_dtb"][:, 8:16],
                      loss=0.5 * jnp.sum(G["loss_row"], axis=1, keepdims=True) / D_MODEL)
    q_in = _pair_sum(p_in, _exchange_sibling(p_in), "pair_sum_w_in")
    r_in, r_sm, r_rep = _exchange_grads(q_in, p_sm, p_rep)
    p_p, land_p = _exchange_wait(sent["started"], r_in, True, "exchange_proj_wait")
    r_p = lax.dynamic_update_slice(land_p, lax.dynamic_index_in_dim(p_p, me, 0, keepdims=True), (me, 0, 0))

    o_in = _sum_adam(r_in, w_in_t, m_in_t, v_in_t, "adam_w_in", SHARD_IN, tc=LANES)
    o_p = _sum_adam(r_p, wp_stack(W).reshape(rows_p, D_MODEL), wp_stack(M).reshape(rows_p, D_MODEL),
                    wp_stack(V).reshape(rows_p, D_MODEL), "adam_w_proj", 128)
    o_sm = _sum_adam(r_sm, sm_pack(W), sm_pack(M), sm_pack(V), "adam_small", SM_ROWS)
    rep_pack = lambda P: _pack_rep(*[P[n] for n in _REP_ORDER])
    o_rep = _sum_adam(r_rep, rep_pack(W), rep_pack(M), rep_pack(V), "adam_rep", REP_ROWS)

    res = []
    for kind in range(4):
        rep = _unpack_rep(o_rep[kind])
        per = dict(rep)
        per["w_in"] = jnp.transpose(o_in[kind])[None]
        pp = o_p[kind].reshape(3, 1, D_MODEL // N_DEV, D_MODEL)
        per["w_proj_gdn"], per["w_proj_gla"], per["w_out"] = pp[0], pp[1], pp[2]
        per["conv_w"] = o_sm[kind][None, 0:CONV_K, :]
        per["gk_w2_fwd"] = o_sm[kind][None, 8:24, 0:64]
        per["gk_w2_bwd"] = o_sm[kind][None, 8:24, 64:128]
        res.append([per[n] for n in _WEIGHTS])
    loss = o_rep[0][LOSS_ROW, 0]
    return (loss, G["grad_x"][None], *res[0], *res[1], *res[2], *res[3])
```

```python
import functools
import math

import jax
import jax.numpy as jnp
from jax import lax
from jax.experimental import pallas as pl
from jax.experimental.pallas import tpu as pltpu

f32 = jnp.float32
MXU_DTYPE = jnp.bfloat16
HI = lax.Precision.HIGHEST

D_MODEL = 1024
CHUNK = 64
NORM_EPS = 1e-6
GDN_HEADS, GDN_DIM = 8, 128
GLA_HEADS, GLA_DK, GLA_DV = 4, 128, 256
GATE_NORMALIZER = 16.0
CONV_K = 5
N_IN = 9280
N_DEV = 8
SHARD_IN = N_IN // N_DEV

N_CAT = 9728
SMALL_OFF = 9216
LANES = 128

ADAM_LR, ADAM_B1, ADAM_B2, ADAM_EPS, ADAM_WD, ADAM_STEP = 0.001, 0.9, 0.999, 1e-08, 0.01, 10

VMEM_LIMIT = 56 * 1024 * 1024


def _cparams(sem=None):
    return pltpu.CompilerParams(dimension_semantics=sem, vmem_limit_bytes=VMEM_LIMIT)


_DN = {"nn": (((1,), (0,)), ((), ())), "nt": (((1,), (1,)), ((), ())), "tn": (((0,), (0,)), ((), ()))}
_BDN = {"nn": (((2,), (1,)), ((0,), (0,))), "nt": (((2,), (2,)), ((0,), (0,))), "tn": (((1,), (1,)), ((0,), (0,)))}


def _dot16(a, b, kind):
    return lax.dot_general(a, b, (_DN if a.ndim == 2 else _BDN)[kind], preferred_element_type=f32)


def _mx(a, b, kind):
    return _dot16(a.astype(MXU_DTYPE), b.astype(MXU_DTYPE), kind)


def _split2(a):
    hi = a.astype(jnp.bfloat16)
    return hi, (a - hi.astype(f32)).astype(jnp.bfloat16)


def _split3(a):
    p1 = a.astype(jnp.bfloat16)
    r = a - p1.astype(f32)
    p2 = r.astype(jnp.bfloat16)
    return p1, p2, (r - p2.astype(f32)).astype(jnp.bfloat16)


def _h3s(a2, b2, kind):
    (ah, al), (bh, bl) = a2, b2
    return _dot16(ah, bh, kind) + (_dot16(ah, bl, kind) + _dot16(al, bh, kind))


def _h3(a, b, kind):
    return _h3s(_split2(a), _split2(b), kind)


def _exact01(t01, x, kind="nn"):
    t = t01.astype(jnp.bfloat16)
    x1, x2, x3 = _split3(x)
    return _dot16(t, x1, kind) + (_dot16(t, x2, kind) + _dot16(t, x3, kind))


def _colsum_col(e):
    ones = jnp.ones(e.shape[:-1] + (LANES,), jnp.bfloat16)
    e1, e2, e3 = _split3(e)
    return (_dot16(e1, ones, "tn") + (_dot16(e2, ones, "tn") + _dot16(e3, ones, "tn")))[..., 0:1]


def _iota3(shape, axis):
    return lax.broadcasted_iota(jnp.int32, shape, axis)


def _dir_masks(B, nf, Cn):
    shp = (B, Cn, Cn)
    bb, ii, jj = _iota3(shp, 0), _iota3(shp, 1), _iota3(shp, 2)
    fwd = bb < nf
    rev = jnp.logical_not(fwd)
    incl = jnp.logical_or(jnp.logical_and(fwd, ii >= jj), jnp.logical_and(rev, ii <= jj))
    strict = jnp.logical_and(incl, ii != jj)
    return incl, strict, ii == jj


def _last_row(x, nf):
    B, Cn = x.shape[0], x.shape[1]
    fwd = _iota3((B, 1, 1), 0) < nf
    return jnp.where(fwd, x[:, Cn - 1 : Cn, :], x[:, 0:1, :])


def _add_at_last_row(x, val, nf):
    B, Cn = x.shape[0], x.shape[1]
    bb, rr = _iota3((B, Cn, 1), 0), _iota3((B, Cn, 1), 1)
    at = jnp.logical_or(jnp.logical_and(bb < nf, rr == Cn - 1), jnp.logical_and(bb >= nf, rr == 0))
    return x + jnp.where(at, val, 0.0)


def _sigmoid(x):
    return jax.nn.sigmoid(x)


def _silu(x):
    return x * _sigmoid(x)


def _silu_grad(x):
    s = _sigmoid(x)
    return s * (1.0 + x * (1.0 - s))


def _softplus(x):
    u = jnp.exp(-jnp.abs(x))
    l1p = jnp.where(u < 1e-3, u * (1.0 - u * (0.5 - u * (1.0 / 3.0))), jnp.log(1.0 + u))
    return jnp.maximum(x, 0.0) + l1p


def _iota2(shape, axis):
    return lax.broadcasted_iota(jnp.int32, shape, axis)


def _matmul(a, b, kind, out_dtype, name, tm, tn, tk, out_t=False, after=None):
    extra = [] if after is None else [after]
    if kind == "nn":
        (M, K), N = a.shape, b.shape[1]
    elif kind == "nt":
        (M, K), N = a.shape, b.shape[0]
    else:
        (K, M), N = a.shape, b.shape[1]
    tm, tn, tk = min(tm, M), min(tn, N), min(tk, K)
    assert M % tm == 0 and N % tn == 0 and K % tk == 0, (name, M, N, K)
    nk = K // tk
    if kind == "tn":
        a_spec = pl.BlockSpec((tk, tm), lambda i, j, k: (k, i))
    else:
        a_spec = pl.BlockSpec((tm, tk), lambda i, j, k: (i, k))
    if kind == "nt":
        b_spec = pl.BlockSpec((tn, tk), lambda i, j, k: (j, k))
    else:
        b_spec = pl.BlockSpec((tk, tn), lambda i, j, k: (k, j))

    def body(a_ref, b_ref, *rest):
        o_ref, acc = rest[len(extra)], rest[len(extra) + 1 :]
        p = _mx(a_ref[...], b_ref[...], kind)
        if nk == 1:
            o_ref[...] = (p.T if out_t else p).astype(out_dtype)
        else:
            assert not out_t
            acc_ref = acc[0]
            k = pl.program_id(2)

            @pl.when(k == 0)
            def _():
                acc_ref[...] = p

            @pl.when(k > 0)
            def _():
                acc_ref[...] += p

            @pl.when(k == nk - 1)
            def _():
                o_ref[...] = acc_ref[...].astype(out_dtype)

    return pl.pallas_call(
        body,
        out_shape=jax.ShapeDtypeStruct((N, M) if out_t else (M, N), out_dtype),
        grid=(M // tm, N // tn, nk),
        in_specs=[a_spec, b_spec] + [pl.BlockSpec((8, LANES), lambda i, j, k: (0, 0))] * len(extra),
        out_specs=pl.BlockSpec((tn, tm), lambda i, j, k: (j, i)) if out_t else pl.BlockSpec((tm, tn), lambda i, j, k: (i, j)),
        scratch_shapes=[] if nk == 1 else [pltpu.VMEM((tm, tn), f32)],
        compiler_params=_cparams(("parallel", "parallel", "arbitrary")),
        name=name,
    )(a, b, *extra)


def _rms_pre(x, w, tb=256):
    T = x.shape[0]

    def body(x_ref, w_ref, h_ref, ht_ref):
        xv = x_ref[...]
        r = lax.rsqrt(jnp.mean(xv * xv, axis=1, keepdims=True) + NORM_EPS)
        h = xv * r * w_ref[...]
        h_ref[...] = h.astype(h_ref.dtype)
        ht_ref[...] = h.T.astype(ht_ref.dtype)

    return pl.pallas_call(
        body,
        out_shape=(jax.ShapeDtypeStruct((T, D_MODEL), MXU_DTYPE), jax.ShapeDtypeStruct((D_MODEL, T), MXU_DTYPE)),
        grid=(T // tb,),
        in_specs=[pl.BlockSpec((tb, D_MODEL), lambda i: (i, 0)), pl.BlockSpec((1, D_MODEL), lambda i: (0, 0))],
        out_specs=(pl.BlockSpec((tb, D_MODEL), lambda i: (i, 0)), pl.BlockSpec((D_MODEL, tb), lambda i: (0, i))),
        compiler_params=_cparams(("parallel",)),
        name="rms_pre",
    )(x, w)


def _rms_pre_bwd(x, w, dh, dy, tb=256):
    T = x.shape[0]

    def body(x_ref, w_ref, dh_ref, dy_ref, dx_ref, dw_ref):
        i = pl.program_id(0)
        xv, dhv = x_ref[...], dh_ref[...]
        r = lax.rsqrt(jnp.mean(xv * xv, axis=1, keepdims=True) + NORM_EPS)
        dhw = dhv * w_ref[...]
        dx_ref[...] = dy_ref[...] + r * dhw - xv * (r * r * r) * jnp.mean(dhw * xv, axis=1, keepdims=True)
        part = jnp.sum(dhv * xv * r, axis=0, keepdims=True)

        @pl.when(i == 0)
        def _():
            dw_ref[...] = part

        @pl.when(i > 0)
        def _():
            dw_ref[...] += part

    blk = pl.BlockSpec((tb, D_MODEL), lambda i: (i, 0))
    row = pl.BlockSpec((1, D_MODEL), lambda i: (0, 0))
    return pl.pallas_call(
        body,
        out_shape=(jax.ShapeDtypeStruct((T, D_MODEL), f32), jax.ShapeDtypeStruct((1, D_MODEL), f32)),
        grid=(T // tb,),
        in_specs=[blk, row, blk, blk],
        out_specs=(blk, row),
        compiler_params=_cparams(("arbitrary",)),
        name="rms_pre_bwd",
    )(x, w, dh, dy)


def _conv5(u, cw):
    T = u.shape[0]
    t = _iota2(u.shape, 0)
    acc = u * cw[2:3, :]
    for j in (0, 1, 3, 4):
        d = j - 2
        sh = pltpu.roll(u, (-d) % T, 0)
        valid = jnp.logical_and(t + d >= 0, t + d < T)
        acc = acc + jnp.where(valid, sh, 0.0) * cw[j : j + 1, :]
    return acc


def _gdn_prep(proj, conv_w):
    T = proj.shape[0]
    nh = GDN_HEADS

    def body(u_ref, cw_ref, o_ref):
        j = pl.program_id(0)
        s = _silu(_conv5(u_ref[...], cw_ref[...]))

        @pl.when(j < 2 * nh)
        def _():
            o_ref[0] = s * lax.rsqrt(jnp.sum(s * s, axis=1, keepdims=True) + NORM_EPS)

        @pl.when(j >= 2 * nh)
        def _():
            o_ref[0] = s

    return pl.pallas_call(
        body,
        out_shape=jax.ShapeDtypeStruct((3, T, GDN_HEADS * GDN_DIM), f32),
        grid=(3 * nh,),
        in_specs=[pl.BlockSpec((T, GDN_DIM), lambda j: (0, j)), pl.BlockSpec((CONV_K, GDN_DIM), lambda j: (0, j))],
        out_specs=pl.BlockSpec((1, T, GDN_DIM), lambda j: (j // nh, 0, j % nh)),
        compiler_params=_cparams(("parallel",)),
        name="gdn_prep",
    )(proj, conv_w)


def _gdn_prep_bwd(proj, conv_w, dqkv_f, dqkv_b, dproj):
    T = proj.shape[0]
    nh = GDN_HEADS

    def body(u_ref, cw_ref, df_ref, db_ref, _, du_ref, dcw_ref):
        j = pl.program_id(0)
        u, cw = u_ref[...], cw_ref[...]
        c = _conv5(u, cw)
        s = _silu(c)
        dn = df_ref[0] + db_ref[0]
        rinv = lax.rsqrt(jnp.sum(s * s, axis=1, keepdims=True) + NORM_EPS)
        ds_norm = rinv * dn - s * (rinv * rinv * rinv) * jnp.sum(dn * s, axis=1, keepdims=True)
        ds = jnp.where(j < 2 * nh, ds_norm, dn)
        dc = ds * _silu_grad(c)
        t = _iota2(u.shape, 0)
        du = dc * cw[2:3, :]
        rows = []
        for jj in range(CONV_K):
            d = jj - 2
            if d == 0:
                rows.append(jnp.sum(dc * u, axis=0, keepdims=True))
                continue
            ush = pltpu.roll(u, (-d) % T, 0)
            uvalid = jnp.logical_and(t + d >= 0, t + d < T)
            rows.append(jnp.sum(dc * jnp.where(uvalid, ush, 0.0), axis=0, keepdims=True))
            dsh = pltpu.roll(dc, d % T, 0)
            dvalid = jnp.logical_and(t - d >= 0, t - d < T)
            du = du + jnp.where(dvalid, dsh, 0.0) * cw[jj : jj + 1, :]
        du_ref[...] = du.astype(du_ref.dtype)
        for jj in range(CONV_K):
            dcw_ref[jj : jj + 1, :] = rows[jj]

    qspec = pl.BlockSpec((1, T, GDN_DIM), lambda j: (j // nh, 0, j % nh))
    return pl.pallas_call(
        body,
        out_shape=(jax.ShapeDtypeStruct(dproj.shape, dproj.dtype), jax.ShapeDtypeStruct((CONV_K, 3 * nh * GDN_DIM), f32)),
        grid=(3 * nh,),
        in_specs=[pl.BlockSpec((T, GDN_DIM), lambda j: (0, j)), pl.BlockSpec((CONV_K, GDN_DIM), lambda j: (0, j)), qspec, qspec, _ANY_SPEC],
        out_specs=(pl.BlockSpec((T, GDN_DIM), lambda j: (0, j)), pl.BlockSpec((CONV_K, GDN_DIM), lambda j: (0, j))),
        input_output_aliases={4: 0},
        compiler_params=_cparams(("parallel",)),
        name="gdn_prep_bwd",
    )(proj, conv_w, dqkv_f, dqkv_b, dproj)


def _chunk_tri(n, rev):
    i, j = _iota2((n, n), 0), _iota2((n, n), 1)
    same = jnp.right_shift(i, 6) == jnp.right_shift(j, 6)
    order = (j >= i) if rev else (j <= i)
    return jnp.where(jnp.logical_and(same, order), 1.0, 0.0).astype(f32)


def _gdn_gates(proj, avec, dvec, tb=256):
    T = proj.shape[0]

    def body(s_ref, a_ref, d_ref, g_ref, gc_ref):
        small = s_ref[...]
        lane = _iota2(small.shape, 1)
        lg = -jnp.exp(a_ref[...]) * _softplus(small + d_ref[...])
        beta = _sigmoid(small)
        g_ref[...] = jnp.where(lane < 16, lg, jnp.where(lane < 32, beta, 0.0))
        lgm = jnp.where(lane < 16, lg, 0.0)
        gcf = _exact01(_chunk_tri(tb, False), lgm)
        gcr = _exact01(_chunk_tri(tb, True), lgm)
        gc_ref[...] = jnp.where(lane < 8, gcf, gcr)

    blk = pl.BlockSpec((tb, LANES), lambda i: (i, 0))
    row = pl.BlockSpec((1, LANES), lambda i: (0, 0))
    return pl.pallas_call(
        body,
        out_shape=(jax.ShapeDtypeStruct((T, LANES), f32), jax.ShapeDtypeStruct((T, LANES), f32)),
        grid=(T // tb,),
        in_specs=[pl.BlockSpec((tb, LANES), lambda i: (i, SMALL_OFF // LANES)), row, row],
        out_specs=(blk, blk),
        compiler_params=_cparams(("parallel",)),
        name="gdn_gates",
    )(proj, avec, dvec)


def _gdn_gates_bwd(proj, avec, dvec, dg_f, dg_b, dsmall_gla, dproj, tb=256):
    T = proj.shape[0]
    pad = N_CAT - SMALL_OFF

    def body(s_ref, a_ref, d_ref, gf_ref, gb_ref, dl_ref, _, ds_ref, da_ref, dd_ref):
        i = pl.program_id(0)
        small = s_ref[...]
        lane = _iota2(small.shape, 1)
        dgate = gf_ref[...] + gb_ref[...]
        z = small + d_ref[...]
        nega = -jnp.exp(a_ref[...])
        dz = dgate * nega * _sigmoid(z)
        beta = _sigmoid(small)
        dsm = jnp.where(lane < 16, dz, jnp.where(lane < 32, dgate * beta * (1.0 - beta), 0.0))
        ds_ref[:, :LANES] = (dsm + dl_ref[...]).astype(ds_ref.dtype)
        ds_ref[:, LANES:] = jnp.zeros((tb, pad - LANES), ds_ref.dtype)
        lg = nega * _softplus(z)
        pa = jnp.sum(jnp.where(lane < 16, dgate * lg, 0.0), axis=0, keepdims=True)
        pd = jnp.sum(jnp.where(lane < 16, dz, 0.0), axis=0, keepdims=True)

        @pl.when(i == 0)
        def _():
            da_ref[...] = pa
            dd_ref[...] = pd

        @pl.when(i > 0)
        def _():
            da_ref[...] += pa
            dd_ref[...] += pd

    blk = pl.BlockSpec((tb, LANES), lambda i: (i, 0))
    row = pl.BlockSpec((1, LANES), lambda i: (0, 0))
    return pl.pallas_call(
        body,
        out_shape=(jax.ShapeDtypeStruct(dproj.shape, dproj.dtype), jax.ShapeDtypeStruct((1, LANES), f32), jax.ShapeDtypeStruct((1, LANES), f32)),
        grid=(T // tb,),
        in_specs=[pl.BlockSpec((tb, LANES), lambda i: (i, SMALL_OFF // LANES)), row, row, blk, blk, blk, _ANY_SPEC],
        out_specs=(pl.BlockSpec((tb, pad), lambda i: (i, DP_SMALL_BLK)), row, row),
        input_output_aliases={6: 0},
        compiler_params=_cparams(("arbitrary",)),
        name="gdn_gates_bwd",
    )(proj, avec, dvec, dg_f, dg_b, dsmall_gla, dproj)


GDN_HB = 8


def _inv_unit_lower(L, eye):
    A = eye - L
    P2 = _split2(L)
    for _ in range(5):
        P2 = _split2(_h3s(P2, P2, "nn"))
        A = A + _h3s(_split2(A), P2, "nn")
    return A


def _lane_col(blk, idx):
    lane = _iota2(blk.shape, 1)
    return jnp.sum(jnp.where(lane == idx, blk, 0.0), axis=1, keepdims=True)


class _GdnChunk:
    def __init__(self, q, k, v, gcol, beta, rev, A=None):
        Cn = q.shape[0]
        ii, jj = _iota2((Cn, Cn), 0), _iota2((Cn, Cn), 1)
        self.incl = (ii <= jj) if rev else (ii >= jj)
        self.strict = (ii < jj) if rev else (ii > jj)
        grow = jnp.broadcast_to(gcol, (Cn, LANES)).T[0:1, :]
        diff = gcol - grow
        self.decay = jnp.where(self.incl, jnp.exp(jnp.where(self.incl, diff, 0.0)), 0.0)
        self.k, self.v, self.beta, self.gcol = k, v, beta, gcol
        self.qs = q * (GDN_DIM**-0.5)
        self.kb = k * beta
        self.L = jnp.where(self.strict, _mx(self.kb, k, "nt") * self.decay, 0.0)
        if A is None:
            A = _inv_unit_lower(self.L, jnp.where(ii == jj, 1.0, 0.0).astype(f32))
        self.A2 = _split2(A)
        self.A = A
        self.eg = jnp.exp(gcol)
        self.kbg = self.kb * self.eg
        self.sol = _h3s(self.A2, _split2(jnp.concatenate([v * beta, self.kbg], axis=1)), "nn")
        self.u = self.sol[:, :GDN_DIM]
        self.w = self.sol[:, GDN_DIM:]
        self.attn = jnp.where(self.incl, _mx(self.qs, k, "nt") * self.decay, 0.0)
        self.qd = self.qs * self.eg
        last = 0 if rev else Cn - 1
        self.last = last
        gl = gcol[last : last + 1, :]
        self.ekd = jnp.exp(gl - gcol)
        self.kd = k * self.ekd
        self.egl = jnp.exp(gl)


def _gdn_scan_fwd(qkvn, gates, gc):
    T = qkvn.shape[1]
    N = T // CHUNK
    nh, hb = GDN_HEADS, GDN_HB
    W = hb * GDN_DIM
    chains = [(d, hh) for d in range(2) for hh in range(hb)]

    def body(qf_ref, qb_ref, gf_ref, gb_ref, cf_ref, cb_ref, of_ref, ob_ref, sf_ref, sb_ref, af_ref, ab_ref, s_scr):
        n, b = pl.program_id(0), pl.program_id(1)

        @pl.when(n == 0)
        def _():
            for d, hh in chains:
                s_scr[d, b * hb + hh] = jnp.zeros((GDN_DIM, GDN_DIM), f32)

        dirs = ((qf_ref, gf_ref, cf_ref, of_ref, sf_ref, af_ref), (qb_ref, gb_ref, cb_ref, ob_ref, sb_ref, ab_ref))
        s_in = {ch: s_scr[ch[0], b * hb + ch[1]] for ch in chains}
        s_out = {}
        for d, hh in chains:
            q_ref, g_ref, c_ref, o_ref, st_ref, a_ref = dirs[d]
            h = b * hb + hh
            sl = slice(hh * GDN_DIM, (hh + 1) * GDN_DIM)
            gcol = _lane_col(c_ref[...], h + 8 * d)
            beta = _lane_col(g_ref[...], 16 + h + 8 * d)
            ck = _GdnChunk(q_ref[0, :, sl], q_ref[1, :, sl], q_ref[2, :, sl], gcol, beta, d == 1)
            S = s_in[d, hh]
            st_ref[0, hh] = S
            a_ref[0, hh] = ck.A
            v_new = ck.u - _mx(ck.w, S, "nn")
            o_ref[:, sl] = _mx(ck.qd, S, "nn") + _mx(ck.attn, v_new, "nn")
            s_out[d, hh] = S * ck.egl + _mx(ck.kd, v_new, "tn")
        for d, hh in chains:
            s_scr[d, b * hb + hh] = s_out[d, hh]

    fw, bw = (lambda n: n), (lambda n: N - 1 - n)
    qsp = lambda ci: pl.BlockSpec((3, CHUNK, W), lambda n, b: (0, ci(n), b))
    gsp = lambda ci: pl.BlockSpec((CHUNK, LANES), lambda n, b: (ci(n), 0))
    osp = lambda ci: pl.BlockSpec((CHUNK, W), lambda n, b: (ci(n), b))
    ssp = lambda ci: pl.BlockSpec((1, hb, GDN_DIM, GDN_DIM), lambda n, b: (ci(n), b, 0, 0))
    asp = lambda ci: pl.BlockSpec((1, hb, CHUNK, CHUNK), lambda n, b: (ci(n), b, 0, 0))
    o_shape = jax.ShapeDtypeStruct((T, nh * GDN_DIM), f32)
    s_shape = jax.ShapeDtypeStruct((N, nh, GDN_DIM, GDN_DIM), f32)
    a_shape = jax.ShapeDtypeStruct((N, nh, CHUNK, CHUNK), f32)
    return pl.pallas_call(
        body,
        out_shape=(o_shape, o_shape, s_shape, s_shape, a_shape, a_shape),
        grid=(N, nh // hb),
        in_specs=[qsp(fw), qsp(bw), gsp(fw), gsp(bw), gsp(fw), gsp(bw)],
        out_specs=(osp(fw), osp(bw), ssp(fw), ssp(bw), asp(fw), asp(bw)),
        scratch_shapes=[pltpu.VMEM((2, nh, GDN_DIM, GDN_DIM), f32)],
        compiler_params=_cparams(("arbitrary", "arbitrary")),
        name="gdn_scan_fwd",
    )(qkvn, qkvn, gates, gates, gc, gc)


def _gdn_scan_bwd(qkvn, gates, gc, st_f, st_b, a_f, a_b, do):
    T = qkvn.shape[1]
    N = T // CHUNK
    nh, hb = GDN_HEADS, GDN_HB
    nb = nh // hb
    W = hb * GDN_DIM
    chains = [(d, hh) for d in range(2) for hh in range(hb)]

    def body(qf_ref, qb_ref, gf_ref, gb_ref, cf_ref, cb_ref, sf_ref, sb_ref, af_ref, ab_ref, dof_ref, dob_ref,
             dqf_ref, dqb_ref, dgf_ref, dgb_ref, ds_scr):
        n, b = pl.program_id(0), pl.program_id(1)

        @pl.when(n == 0)
        def _():
            for d, hh in chains:
                ds_scr[d, b * hb + hh] = jnp.zeros((GDN_DIM, GDN_DIM), f32)

        @pl.when(b == 0)
        def _():
            dgf_ref[...] = jnp.zeros((CHUNK, LANES), f32)
            dgb_ref[...] = jnp.zeros((CHUNK, LANES), f32)

        dirs = ((qf_ref, gf_ref, cf_ref, sf_ref, af_ref, dof_ref, dqf_ref), (qb_ref, gb_ref, cb_ref, sb_ref, ab_ref, dob_ref, dqb_ref))
        ds_in = {ch: ds_scr[ch[0], b * hb + ch[1]] for ch in chains}
        ds_out = {}
        lane = _iota2((CHUNK, LANES), 1)
        acc = [dgf_ref[...], dgb_ref[...]]
        for d, hh in chains:
            q_ref, g_ref, c_ref, st_ref, a_ref, do_ref, dq_ref = dirs[d]
            h = b * hb + hh
            sl = slice(hh * GDN_DIM, (hh + 1) * GDN_DIM)
            gcol = _lane_col(c_ref[...], h + 8 * d)
            beta = _lane_col(g_ref[...], 16 + h + 8 * d)
            ck = _GdnChunk(q_ref[0, :, sl], q_ref[1, :, sl], q_ref[2, :, sl], gcol, beta, d == 1, A=a_ref[0, hh])
            k, v = ck.k, ck.v
            S = st_ref[0, hh]
            dS = ds_in[d, hh]
            dov = do_ref[:, sl]
            v_new = ck.u - _mx(ck.w, S, "nn")
            d_vnew = _mx(ck.attn, dov, "tn") + _mx(ck.kd, dS, "nn")
            d_attn = jnp.where(ck.incl, _mx(dov, v_new, "nt"), 0.0)
            d_qd = _mx(dov, S, "nt")
            d_kd = _mx(v_new, dS, "nt")
            d_gl = jnp.sum(jnp.sum(dS * S, axis=1, keepdims=True), axis=0, keepdims=True) * ck.egl
            ds_out[d, hh] = dS * ck.egl + _mx(ck.qd, dov, "tn") - _mx(ck.w, d_vnew, "tn")
            d_w = -_mx(d_vnew, S, "nt")
            d_rhs = _h3s(ck.A2, _split2(jnp.concatenate([d_vnew, d_w], axis=1)), "tn")
            d_vb, d_kbg = d_rhs[:, :GDN_DIM], d_rhs[:, GDN_DIM:]
            dL = -jnp.where(ck.strict, _h3(d_rhs, ck.sol, "nt"), 0.0)
            P = dL * ck.decay
            Q = d_attn * ck.decay
            d_kb = _mx(P, k, "nn") + d_kbg * ck.eg
            d_qs = _mx(Q, k, "nn") + d_qd * ck.eg
            d_k = _mx(P, ck.kb, "tn") + _mx(Q, ck.qs, "tn") + d_kd * ck.ekd + d_kb * ck.beta
            E = dL * ck.L + d_attn * ck.attn
            kdsum = jnp.sum(d_kd * ck.kd, axis=1, keepdims=True)
            d_gc = (jnp.sum(E, axis=1, keepdims=True) - _colsum_col(E)
                    + jnp.sum(d_kbg * ck.kbg, axis=1, keepdims=True)
                    + jnp.sum(d_qd * ck.qd, axis=1, keepdims=True) - kdsum)
            d_gl = d_gl + jnp.sum(kdsum, axis=0, keepdims=True)
            row = _iota2((CHUNK, 1), 0)
            d_gc = d_gc + jnp.where(row == ck.last, d_gl, 0.0)
            d_beta = jnp.sum(d_vb * v, axis=1, keepdims=True) + jnp.sum(d_kb * k, axis=1, keepdims=True)
            dq_ref[0, :, sl] = d_qs * (GDN_DIM**-0.5)
            dq_ref[1, :, sl] = d_k
            dq_ref[2, :, sl] = d_vb * ck.beta
            acc[d] = jnp.where(lane == h + 8 * d, d_gc, acc[d])
            acc[d] = jnp.where(lane == 16 + h + 8 * d, d_beta, acc[d])
        for d, hh in chains:
            ds_scr[d, b * hb + hh] = ds_out[d, hh]
        dgf_ref[...] = acc[0]
        dgb_ref[...] = acc[1]

        @pl.when(b == nb - 1)
        def _():
            for d, dg_ref in enumerate((dgf_ref, dgb_ref)):
                blk = dg_ref[...]
                dg_ref[...] = jnp.where(lane < 16, _exact01(_chunk_tri(CHUNK, d == 0), blk), blk)

    fw, bw = (lambda n: N - 1 - n), (lambda n: n)
    qsp = lambda ci: pl.BlockSpec((3, CHUNK, W), lambda n, b: (0, ci(n), b))
    gsp = lambda ci: pl.BlockSpec((CHUNK, LANES), lambda n, b: (ci(n), 0))
    osp = lambda ci: pl.BlockSpec((CHUNK, W), lambda n, b: (ci(n), b))
    ssp = lambda ci: pl.BlockSpec((1, hb, GDN_DIM, GDN_DIM), lambda n, b: (ci(n), b, 0, 0))
    asp = lambda ci: pl.BlockSpec((1, hb, CHUNK, CHUNK), lambda n, b: (ci(n), b, 0, 0))
    dq_shape = jax.ShapeDtypeStruct((3, T, nh * GDN_DIM), f32)
    dg_shape = jax.ShapeDtypeStruct((T, LANES), f32)
    return pl.pallas_call(
        body,
        out_shape=(dq_shape, dq_shape, dg_shape, dg_shape),
        grid=(N, nb),
        in_specs=[qsp(fw), qsp(bw), gsp(fw), gsp(bw), gsp(fw), gsp(bw), ssp(fw), ssp(bw), asp(fw), asp(bw), osp(fw), osp(bw)],
        out_specs=(qsp(fw), qsp(bw), gsp(fw), gsp(bw)),
        scratch_shapes=[pltpu.VMEM((2, nh, GDN_DIM, GDN_DIM), f32)],
        compiler_params=_cparams(("arbitrary", "arbitrary")),
        name="gdn_scan_bwd",
    )(qkvn, qkvn, gates, gates, gc, gc, st_f, st_b, a_f, a_b, do, do)


def _out_norm(o_f, o_b, proj, gate_blk, w, hd, name, tb=256):
    T, W = o_f.shape
    nh = W // hd

    def body(of_ref, ob_ref, z_ref, w_ref, y_ref):
        wv = w_ref[...]
        for i in range(nh):
            sl = slice(i * hd, (i + 1) * hd)
            o = of_ref[:, sl] + ob_ref[:, sl]
            r = lax.rsqrt(jnp.mean(o * o, axis=1, keepdims=True) + NORM_EPS)
            y_ref[:, sl] = (o * r * wv * _silu(z_ref[:, sl])).astype(y_ref.dtype)

    blk = pl.BlockSpec((tb, W), lambda i: (i, 0))
    return pl.pallas_call(
        body,
        out_shape=jax.ShapeDtypeStruct((T, W), MXU_DTYPE),
        grid=(T // tb,),
        in_specs=[blk, blk, pl.BlockSpec((tb, W), lambda i: (i, gate_blk)), pl.BlockSpec((1, hd), lambda i: (0, 0))],
        out_specs=blk,
        compiler_params=_cparams(("parallel",)),
        name=name,
    )(o_f, o_b, proj, w)


def _out_norm_bwd(dy, o_f, o_b, proj, gate_blk, w, hd, name, dproj, tb=256):
    T, W = o_f.shape
    nh = W // hd

    def body(dy_ref, of_ref, ob_ref, z_ref, w_ref, _, do_ref, dz_ref, dw_ref):
        i = pl.program_id(0)
        wv = w_ref[...]
        dw = jnp.zeros((1, hd), f32)
        for a in range(nh):
            sl = slice(a * hd, (a + 1) * hd)
            o = of_ref[:, sl] + ob_ref[:, sl]
            z = z_ref[:, sl]
            dyv = dy_ref[:, sl]
            r = lax.rsqrt(jnp.mean(o * o, axis=1, keepdims=True) + NORM_EPS)
            orr = o * r
            dn = dyv * _silu(z)
            dz_ref[:, sl] = (dyv * orr * wv * _silu_grad(z)).astype(dz_ref.dtype)
            dw = dw + jnp.sum(dn * orr, axis=0, keepdims=True)
            dnw = dn * wv
            do_ref[:, sl] = r * dnw - o * (r * r * r) * jnp.mean(dnw * o, axis=1, keepdims=True)

        @pl.when(i == 0)
        def _():
            dw_ref[...] = dw

        @pl.when(i > 0)
        def _():
            dw_ref[...] += dw

    blk = pl.BlockSpec((tb, W), lambda i: (i, 0))
    row = pl.BlockSpec((1, hd), lambda i: (0, 0))
    return pl.pallas_call(
        body,
        out_shape=(jax.ShapeDtypeStruct((T, W), f32), jax.ShapeDtypeStruct(dproj.shape, dproj.dtype), jax.ShapeDtypeStruct((1, hd), f32)),
        grid=(T // tb,),
        in_specs=[blk, blk, blk, pl.BlockSpec((tb, W), lambda i: (i, gate_blk)), row, _ANY_SPEC],
        out_specs=(blk, pl.BlockSpec((tb, W), lambda i: (i, gate_blk)), row),
        input_output_aliases={5: 1},
        compiler_params=_cparams(("arbitrary",)),
        name=name,
    )(dy, o_f, o_b, proj, w, dproj)


def _gla_prep(proj, w2f, w2b, b2f, b2b, tb=256):
    T = proj.shape[0]
    W = GLA_HEADS * GLA_DK

    def body(s_ref, wf_ref, wb_ref, bf_ref, bb_ref, gf_ref, gb_ref):
        small = s_ref[...]
        gkf = -_softplus(-(_mx(small, wf_ref[...], "nn") + bf_ref[...])) * (1.0 / GATE_NORMALIZER)
        gkb = -_softplus(-(_mx(small, wb_ref[...], "nn") + bb_ref[...])) * (1.0 / GATE_NORMALIZER)
        gf_ref[...] = _exact01(_chunk_tri(tb, False), gkf)
        gb_ref[...] = _exact01(_chunk_tri(tb, True), gkb)

    blk = pl.BlockSpec((tb, W), lambda i: (i, 0))
    wsp = pl.BlockSpec((LANES, W), lambda i: (0, 0))
    row = pl.BlockSpec((1, W), lambda i: (0, 0))
    return pl.pallas_call(
        body,
        out_shape=(jax.ShapeDtypeStruct((T, W), f32), jax.ShapeDtypeStruct((T, W), f32)),
        grid=(T // tb,),
        in_specs=[pl.BlockSpec((tb, LANES), lambda i: (i, SMALL_OFF // LANES)), wsp, wsp, row, row],
        out_specs=(blk, blk),
        compiler_params=_cparams(("parallel",)),
        name="gla_prep",
    )(proj, w2f, w2b, b2f, b2b)


def _gla_prep_bwd(proj, w2f, w2b, b2f, b2b, dgk_f, dgk_b, tb=256):
    T = proj.shape[0]
    W = GLA_HEADS * GLA_DK

    def body(s_ref, wf_ref, wb_ref, bf_ref, bb_ref, df_ref, db_ref, ds_ref, dwf_ref, dwb_ref, dbf_ref, dbb_ref):
        i = pl.program_id(0)
        small = s_ref[...]
        dsm = jnp.zeros((tb, LANES), f32)
        parts = []
        for w_ref, b_ref, d_ref in ((wf_ref, bf_ref, df_ref), (wb_ref, bb_ref, db_ref)):
            pre = _mx(small, w_ref[...], "nn") + b_ref[...]
            dpre = d_ref[...] * (1.0 / GATE_NORMALIZER) * _sigmoid(-pre)
            dsm = dsm + _mx(dpre, w_ref[...], "nt")
            parts.append((_mx(small, dpre, "tn"), jnp.sum(dpre, axis=0, keepdims=True)))
        ds_ref[...] = dsm

        @pl.when(i == 0)
        def _():
            dwf_ref[...], dbf_ref[...] = parts[0]
            dwb_ref[...], dbb_ref[...] = parts[1]

        @pl.when(i > 0)
        def _():
            dwf_ref[...] += parts[0][0]
            dbf_ref[...] += parts[0][1]
            dwb_ref[...] += parts[1][0]
            dbb_ref[...] += parts[1][1]

    blk = pl.BlockSpec((tb, W), lambda i: (i, 0))
    wsp = pl.BlockSpec((LANES, W), lambda i: (0, 0))
    row = pl.BlockSpec((1, W), lambda i: (0, 0))
    return pl.pallas_call(
        body,
        out_shape=(jax.ShapeDtypeStruct((T, LANES), f32), jax.ShapeDtypeStruct((LANES, W), f32), jax.ShapeDtypeStruct((LANES, W), f32),
                   jax.ShapeDtypeStruct((1, W), f32), jax.ShapeDtypeStruct((1, W), f32)),
        grid=(T // tb,),
        in_specs=[pl.BlockSpec((tb, LANES), lambda i: (i, SMALL_OFF // LANES)), wsp, wsp, row, row, blk, blk],
        out_specs=(pl.BlockSpec((tb, LANES), lambda i: (i, 0)), wsp, wsp, row, row),
        compiler_params=_cparams(("arbitrary",)),
        name="gla_prep_bwd",
    )(proj, w2f, w2b, b2f, b2b, dgk_f, dgk_b)


Q_B_BLK, K_B_BLK, V_B_BLK = 4096 // GLA_DK, 4608 // GLA_DK, 5120 // GLA_DV


class _GlaChunk:
    def __init__(self, q, k, G, rev):
        Cn = q.shape[0]
        ii, jj = _iota2((Cn, Cn), 0), _iota2((Cn, Cn), 1)
        self.incl = (ii <= jj) if rev else (ii >= jj)
        self.eG = jnp.exp(G)
        self.enG = jnp.exp(-G)
        self.qg = q * (GLA_DK**-0.5) * self.eG
        self.kg = k * self.enG
        self.attn = jnp.where(self.incl, _mx(self.qg, self.kg, "nt"), 0.0)
        last = 0 if rev else Cn - 1
        self.last = last
        gl = G[last : last + 1, :]
        self.ekd = jnp.exp(gl - G)
        self.kd = k * self.ekd
        self.egl = jnp.exp(gl)


def _gla_scan_fwd(proj, G_f, G_b):
    T = proj.shape[0]
    N = T // CHUNK
    nh = GLA_HEADS

    def body(qf_ref, kf_ref, vf_ref, gf_ref, qb_ref, kb_ref, vb_ref, gb_ref, of_ref, ob_ref, sf_ref, sb_ref, s_scr):
        n, h = pl.program_id(0), pl.program_id(1)

        @pl.when(n == 0)
        def _():
            s_scr[0, h] = jnp.zeros((GLA_DV, GLA_DK), f32)
            s_scr[1, h] = jnp.zeros((GLA_DV, GLA_DK), f32)

        for d, (q_ref, k_ref, v_ref, g_ref, o_ref, st_ref) in enumerate(
            ((qf_ref, kf_ref, vf_ref, gf_ref, of_ref, sf_ref), (qb_ref, kb_ref, vb_ref, gb_ref, ob_ref, sb_ref))
        ):
            ck = _GlaChunk(q_ref[...], k_ref[...], g_ref[...], d == 1)
            v = v_ref[...]
            St = s_scr[d, h]
            st_ref[0, 0] = St
            o_ref[...] = _mx(ck.qg, St, "nt") + _mx(ck.attn, v, "nn")
            s_scr[d, h] = St * ck.egl + _mx(v, ck.kd, "tn")

    def specs(cidx):
        return [
            pl.BlockSpec((CHUNK, GLA_DK), lambda n, h: (cidx(n), Q_B_BLK + h)),
            pl.BlockSpec((CHUNK, GLA_DK), lambda n, h: (cidx(n), K_B_BLK + h)),
            pl.BlockSpec((CHUNK, GLA_DV), lambda n, h: (cidx(n), V_B_BLK + h)),
            pl.BlockSpec((CHUNK, GLA_DK), lambda n, h: (cidx(n), h)),
        ]

    fw, bw = (lambda n: n), (lambda n: N - 1 - n)
    o_shape = jax.ShapeDtypeStruct((T, nh * GLA_DV), f32)
    s_shape = jax.ShapeDtypeStruct((N, nh, GLA_DV, GLA_DK), f32)
    return pl.pallas_call(
        body,
        out_shape=(o_shape, o_shape, s_shape, s_shape),
        grid=(N, nh),
        in_specs=specs(fw) + specs(bw),
        out_specs=(
            pl.BlockSpec((CHUNK, GLA_DV), lambda n, h: (n, h)),
            pl.BlockSpec((CHUNK, GLA_DV), lambda n, h: (N - 1 - n, h)),
            pl.BlockSpec((1, 1, GLA_DV, GLA_DK), lambda n, h: (n, h, 0, 0)),
            pl.BlockSpec((1, 1, GLA_DV, GLA_DK), lambda n, h: (N - 1 - n, h, 0, 0)),
        ),
        scratch_shapes=[pltpu.VMEM((2, nh, GLA_DV, GLA_DK), f32)],
        compiler_params=_cparams(("arbitrary", "arbitrary")),
        name="gla_scan_fwd",
    )(proj, proj, proj, G_f, proj, proj, proj, G_b)


def _gla_scan_bwd(proj, G_f, G_b, st_f, st_b, do):
    T = proj.shape[0]
    N = T // CHUNK
    nh = GLA_HEADS

    def body(qf_ref, kf_ref, vf_ref, gf_ref, sf_ref, dof_ref, qb_ref, kb_ref, vb_ref, gb_ref, sb_ref, dob_ref,
             dqf_ref, dkf_ref, dvf_ref, dgf_ref, dqb_ref, dkb_ref, dvb_ref, dgb_ref, ds_scr):
        n, h = pl.program_id(0), pl.program_id(1)

        @pl.when(n == 0)
        def _():
            ds_scr[0, h] = jnp.zeros((GLA_DV, GLA_DK), f32)
            ds_scr[1, h] = jnp.zeros((GLA_DV, GLA_DK), f32)

        for d, (q_ref, k_ref, v_ref, g_ref, st_ref, do_ref, dq_ref, dk_ref, dv_ref, dg_ref) in enumerate(
            ((qf_ref, kf_ref, vf_ref, gf_ref, sf_ref, dof_ref, dqf_ref, dkf_ref, dvf_ref, dgf_ref),
             (qb_ref, kb_ref, vb_ref, gb_ref, sb_ref, dob_ref, dqb_ref, dkb_ref, dvb_ref, dgb_ref))
        ):
            rev = d == 1
            ck = _GlaChunk(q_ref[...], k_ref[...], g_ref[...], rev)
            v = v_ref[...]
            St = st_ref[0, 0]
            dSt = ds_scr[d, h]
            dov = do_ref[...]
            d_attn = jnp.where(ck.incl, _mx(dov, v, "nt"), 0.0)
            d_qg = _mx(dov, St, "nn") + _mx(d_attn, ck.kg, "nn")
            d_kg = _mx(d_attn, ck.qg, "tn")
            dv_ref[...] = _mx(ck.attn, dov, "tn") + _mx(ck.kd, dSt, "nt")
            d_kd = _mx(v, dSt, "nn")
            ds_scr[d, h] = dSt * ck.egl + _mx(dov, ck.qg, "tn")
            kdd = d_kd * ck.kd
            d_gl = jnp.sum(dSt * St, axis=0, keepdims=True) * ck.egl + jnp.sum(kdd, axis=0, keepdims=True)
            dq_ref[...] = d_qg * ck.eG * (GLA_DK**-0.5)
            dk_ref[...] = d_kg * ck.enG + d_kd * ck.ekd
            d_G = d_qg * ck.qg - d_kg * ck.kg - kdd
            row = _iota2((CHUNK, GLA_DK), 0)
            d_G = d_G + jnp.where(row == ck.last, d_gl, 0.0)
            dg_ref[...] = _exact01(_chunk_tri(CHUNK, not rev), d_G)

    def in_specs(cidx):
        return [
            pl.BlockSpec((CHUNK, GLA_DK), lambda n, h: (cidx(n), Q_B_BLK + h)),
            pl.BlockSpec((CHUNK, GLA_DK), lambda n, h: (cidx(n), K_B_BLK + h)),
            pl.BlockSpec((CHUNK, GLA_DV), lambda n, h: (cidx(n), V_B_BLK + h)),
            pl.BlockSpec((CHUNK, GLA_DK), lambda n, h: (cidx(n), h)),
            pl.BlockSpec((1, 1, GLA_DV, GLA_DK), lambda n, h: (cidx(n), h, 0, 0)),
            pl.BlockSpec((CHUNK, GLA_DV), lambda n, h: (cidx(n), h)),
        ]

    def out_specs(cidx):
        return [
            pl.BlockSpec((CHUNK, GLA_DK), lambda n, h: (cidx(n), h)),
            pl.BlockSpec((CHUNK, GLA_DK), lambda n, h: (cidx(n), h)),
            pl.BlockSpec((CHUNK, GLA_DV), lambda n, h: (cidx(n), h)),
            pl.BlockSpec((CHUNK, GLA_DK), lambda n, h: (cidx(n), h)),
        ]

    fw, bw = (lambda n: N - 1 - n), (lambda n: n)
    k_shape = jax.ShapeDtypeStruct((T, nh * GLA_DK), f32)
    v_shape = jax.ShapeDtypeStruct((T, nh * GLA_DV), f32)
    return pl.pallas_call(
        body,
        out_shape=(k_shape, k_shape, v_shape, k_shape) * 2,
        grid=(N, nh),
        in_specs=in_specs(fw) + in_specs(bw),
        out_specs=tuple(out_specs(fw) + out_specs(bw)),
        scratch_shapes=[pltpu.VMEM((2, nh, GLA_DV, GLA_DK), f32)],
        compiler_params=_cparams(("arbitrary", "arbitrary")),
        name="gla_scan_bwd",
    )(proj, proj, proj, G_f, st_f, do, proj, proj, proj, G_b, st_b, do)


GATE_A_BLK, GATE_B_BLK, Z_BLK, G_B_BLK = 6144 // 1024, 7168 // 1024, 3072 // 1024, 8192 // 1024
DP_GATES_BLK, DP_GLA_QKV_BLK, DP_SMALL_BLK = 6144 // 2048, 4096 // 2048, SMALL_OFF // 512
_ANY_SPEC = pl.BlockSpec(memory_space=pl.ANY)


def _mid(og, obg, proj, x, target, wpg, wpl, wout, w_post, tb=256):
    T = x.shape[0]
    Dm = D_MODEL

    def body(og_ref, obg_ref, ga_ref, gb_ref, x_ref, t_ref, wpg_ref, wpl_ref, wo_ref, wp_ref,
             loss_ref, dwp_ref, dy_ref, m_ref, dout_ref, dya_ref, dyb_ref, dgates_ref, dog_ref, dobg_ref):
        i = pl.program_id(0)
        ya = _mx(og_ref[...], wpg_ref[...], "nn")
        yb = _mx(obg_ref[...], wpl_ref[...], "nn")
        sa, sb = _sigmoid(ga_ref[...]), _sigmoid(gb_ref[...])
        m = sa * ya + sb * yb
        m_ref[...] = m.astype(m_ref.dtype)
        out = _mx(m, wo_ref[...], "nn")
        r = lax.rsqrt(jnp.mean(out * out, axis=1, keepdims=True) + NORM_EPS)
        wp = wp_ref[...]
        e = x_ref[...] + out * r * wp - t_ref[...]
        dy = e * (1.0 / Dm)
        dy_ref[...] = dy
        lpart = jnp.sum(e * e, axis=0, keepdims=True)
        wpart = jnp.sum(dy * out * r, axis=0, keepdims=True)
        dyn = dy * wp
        dout = r * dyn - out * (r * r * r) * jnp.mean(dyn * out, axis=1, keepdims=True)
        dout_ref[...] = dout.astype(dout_ref.dtype)
        dm = _mx(dout, wo_ref[...], "nt")
        dya, dyb = dm * sa, dm * sb
        dya_ref[...] = dya.astype(dya_ref.dtype)
        dyb_ref[...] = dyb.astype(dyb_ref.dtype)
        dgates_ref[:, :Dm] = (dm * ya * sa * (1.0 - sa)).astype(dgates_ref.dtype)
        dgates_ref[:, Dm:] = (dm * yb * sb * (1.0 - sb)).astype(dgates_ref.dtype)
        dog_ref[...] = _mx(dya, wpg_ref[...], "nt")
        dobg_ref[...] = _mx(dyb, wpl_ref[...], "nt")

        @pl.when(i == 0)
        def _():
            loss_ref[...] = lpart
            dwp_ref[...] = wpart

        @pl.when(i > 0)
        def _():
            loss_ref[...] += lpart
            dwp_ref[...] += wpart

    blk = pl.BlockSpec((tb, Dm), lambda i: (i, 0))
    row = pl.BlockSpec((1, Dm), lambda i: (0, 0))
    wsp = pl.BlockSpec((Dm, Dm), lambda i: (0, 0))
    act = jax.ShapeDtypeStruct((T, Dm), MXU_DTYPE)
    big = jax.ShapeDtypeStruct((T, Dm), f32)
    vec = jax.ShapeDtypeStruct((1, Dm), f32)
    return pl.pallas_call(
        body,
        out_shape=(vec, vec, big, act, act, act, act, jax.ShapeDtypeStruct((T, N_CAT), MXU_DTYPE), big, big),
        grid=(T // tb,),
        in_specs=[blk, blk, pl.BlockSpec((tb, Dm), lambda i: (i, GATE_A_BLK)), pl.BlockSpec((tb, Dm), lambda i: (i, GATE_B_BLK)),
                  blk, blk, wsp, wsp, wsp, row],
        out_specs=(row, row) + (blk,) * 5 + (pl.BlockSpec((tb, 2 * Dm), lambda i: (i, DP_GATES_BLK)), blk, blk),
        compiler_params=_cparams(("arbitrary",)),
        name="mid",
    )(og, obg, proj, proj, x, target, wpg, wpl, wout, w_post)


def _heads(ref, lead, n, width):
    return [ref[lead + (slice(None), slice(i * width, (i + 1) * width))] for i in range(n)]


class _GdnBatch:
    def __init__(self, q, k, v, gcol, grow, beta, nf, A=None):
        B, Cn = q.shape[0], q.shape[1]
        self.nf = nf
        self.incl, self.strict, eye = _dir_masks(B, nf, Cn)
        self.decay = jnp.where(self.incl, jnp.exp(jnp.where(self.incl, gcol - grow, 0.0)), 0.0)
        self.k, self.v, self.beta = k, v, beta
        self.qs = q * (GDN_DIM**-0.5)
        self.kb = k * beta
        self.L = jnp.where(self.strict, _mx(self.kb, k, "nt") * self.decay, 0.0)
        if A is None:
            A = _inv_unit_lower(self.L, jnp.where(eye, 1.0, 0.0).astype(f32))
        self.A = A
        self.A2 = _split2(A)
        self.eg = jnp.exp(gcol)
        self.kbg = self.kb * self.eg
        self.sol = _h3s(self.A2, _split2(jnp.concatenate([v * beta, self.kbg], axis=2)), "nn")
        self.u = self.sol[:, :, :GDN_DIM]
        self.w = self.sol[:, :, GDN_DIM:]
        self.attn = jnp.where(self.incl, _mx(self.qs, k, "nt") * self.decay, 0.0)
        self.qd = self.qs * self.eg
        gl = _last_row(gcol, nf)
        self.ekd = jnp.exp(gl - gcol)
        self.kd = k * self.ekd
        self.egl = jnp.exp(gl)


def _gdn_chain_inputs(qf_ref, qb_ref, gf_ref, gb_ref, cf_ref, cb_ref, b, hb):
    qkv = [jnp.stack(_heads(qf_ref, (i,), hb, GDN_DIM) + _heads(qb_ref, (i,), hb, GDN_DIM)) for i in range(3)]
    cols, rows, betas = [], [], []
    for d, (g_ref, c_ref) in enumerate(((gf_ref, cf_ref), (gb_ref, cb_ref))):
        gates, gc = g_ref[...], c_ref[...]
        for hh in range(hb):
            h = b * hb + hh
            col = _lane_col(gc, h + 8 * d)
            cols.append(col)
            rows.append(jnp.broadcast_to(col, (CHUNK, LANES)).T[0:1, :])
            betas.append(_lane_col(gates, 16 + h + 8 * d))
    return qkv[0], qkv[1], qkv[2], jnp.stack(cols), jnp.stack(rows), jnp.stack(betas)


def _gdn_fwd(qkvn, gates, gc):
    T = qkvn.shape[1]
    N = T // CHUNK
    nh, hb = GDN_HEADS, GDN_HB
    W = hb * GDN_DIM

    def body(qf_ref, qb_ref, gf_ref, gb_ref, cf_ref, cb_ref, of_ref, ob_ref, sf_ref, sb_ref, af_ref, ab_ref, s_scr):
        n, b = pl.program_id(0), pl.program_id(1)
        hs = pl.ds(b * hb, hb)

        @pl.when(n == 0)
        def _():
            s_scr[0, hs] = jnp.zeros((hb, GDN_DIM, GDN_DIM), f32)
            s_scr[1, hs] = jnp.zeros((hb, GDN_DIM, GDN_DIM), f32)

        q, k, v, gcol, grow, beta = _gdn_chain_inputs(qf_ref, qb_ref, gf_ref, gb_ref, cf_ref, cb_ref, b, hb)
        ck = _GdnBatch(q, k, v, gcol, grow, beta, hb)
        S = jnp.concatenate([s_scr[0, hs], s_scr[1, hs]], axis=0)
        sf_ref[0], sb_ref[0] = S[:hb], S[hb:]
        af_ref[0], ab_ref[0] = ck.A[:hb], ck.A[hb:]
        v_new = ck.u - _mx(ck.w, S, "nn")
        o = _mx(ck.qd, S, "nn") + _mx(ck.attn, v_new, "nn")
        S_new = S * ck.egl + _mx(ck.kd, v_new, "tn")
        for hh in range(hb):
            sl = slice(hh * GDN_DIM, (hh + 1) * GDN_DIM)
            of_ref[:, sl] = o[hh]
            ob_ref[:, sl] = o[hb + hh]
        s_scr[0, hs] = S_new[:hb]
        s_scr[1, hs] = S_new[hb:]

    fw, bw = (lambda n: n), (lambda n: N - 1 - n)
    qsp = lambda ci: pl.BlockSpec((3, CHUNK, W), lambda n, b: (0, ci(n), b))
    gsp = lambda ci: pl.BlockSpec((CHUNK, LANES), lambda n, b: (ci(n), 0))
    osp = lambda ci: pl.BlockSpec((CHUNK, W), lambda n, b: (ci(n), b))
    ssp = lambda ci: pl.BlockSpec((1, hb, GDN_DIM, GDN_DIM), lambda n, b: (ci(n), b, 0, 0))
    asp = lambda ci: pl.BlockSpec((1, hb, CHUNK, CHUNK), lambda n, b: (ci(n), b, 0, 0))
    o_shape = jax.ShapeDtypeStruct((T, nh * GDN_DIM), f32)
    s_shape = jax.ShapeDtypeStruct((N, nh, GDN_DIM, GDN_DIM), f32)
    a_shape = jax.ShapeDtypeStruct((N, nh, CHUNK, CHUNK), f32)
    return pl.pallas_call(
        body,
        out_shape=(o_shape, o_shape, s_shape, s_shape, a_shape, a_shape),
        grid=(N, nh // hb),
        in_specs=[qsp(fw), qsp(bw), gsp(fw), gsp(bw), gsp(fw), gsp(bw)],
        out_specs=(osp(fw), osp(bw), ssp(fw), ssp(bw), asp(fw), asp(bw)),
        scratch_shapes=[pltpu.VMEM((2, nh, GDN_DIM, GDN_DIM), f32)],
        compiler_params=_cparams(("arbitrary", "arbitrary")),
        name="gdn_scan_fwd",
    )(qkvn, qkvn, gates, gates, gc, gc)


def _gdn_bwd(qkvn, gates, gc, st_f, st_b, a_f, a_b, do):
    T = qkvn.shape[1]
    N = T // CHUNK
    nh, hb = GDN_HEADS, GDN_HB
    nb = nh // hb
    W = hb * GDN_DIM

    def body(qf_ref, qb_ref, gf_ref, gb_ref, cf_ref, cb_ref, sf_ref, sb_ref, af_ref, ab_ref, dof_ref, dob_ref,
             dqf_ref, dqb_ref, dgf_ref, dgb_ref, ds_scr):
        n, b = pl.program_id(0), pl.program_id(1)
        hs = pl.ds(b * hb, hb)

        @pl.when(n == 0)
        def _():
            ds_scr[0, hs] = jnp.zeros((hb, GDN_DIM, GDN_DIM), f32)
            ds_scr[1, hs] = jnp.zeros((hb, GDN_DIM, GDN_DIM), f32)

        @pl.when(b == 0)
        def _():
            dgf_ref[...] = jnp.zeros((CHUNK, LANES), f32)
            dgb_ref[...] = jnp.zeros((CHUNK, LANES), f32)

        q, k, v, gcol, grow, beta = _gdn_chain_inputs(qf_ref, qb_ref, gf_ref, gb_ref, cf_ref, cb_ref, b, hb)
        ck = _GdnBatch(q, k, v, gcol, grow, beta, hb, A=jnp.concatenate([af_ref[0], ab_ref[0]], axis=0))
        S = jnp.concatenate([sf_ref[0], sb_ref[0]], axis=0)
        dS = jnp.concatenate([ds_scr[0, hs], ds_scr[1, hs]], axis=0)
        dov = jnp.stack(_heads(dof_ref, (), hb, GDN_DIM) + _heads(dob_ref, (), hb, GDN_DIM))
        v_new = ck.u - _mx(ck.w, S, "nn")
        d_vnew = _mx(ck.attn, dov, "tn") + _mx(ck.kd, dS, "nn")
        d_attn = jnp.where(ck.incl, _mx(dov, v_new, "nt"), 0.0)
        d_qd = _mx(dov, S, "nt")
        d_kd = _mx(v_new, dS, "nt")
        d_gl = jnp.sum(jnp.sum(dS * S, axis=2, keepdims=True), axis=1, keepdims=True) * ck.egl
        dS_new = dS * ck.egl + _mx(ck.qd, dov, "tn") - _mx(ck.w, d_vnew, "tn")
        ds_scr[0, hs] = dS_new[:hb]
        ds_scr[1, hs] = dS_new[hb:]
        d_w = -_mx(d_vnew, S, "nt")
        d_rhs = _h3s(ck.A2, _split2(jnp.concatenate([d_vnew, d_w], axis=2)), "tn")
        d_vb, d_kbg = d_rhs[:, :, :GDN_DIM], d_rhs[:, :, GDN_DIM:]
        dL = -jnp.where(ck.strict, _h3(d_rhs, ck.sol, "nt"), 0.0)
        P = dL * ck.decay
        Q = d_attn * ck.decay
        d_kb = _mx(P, k, "nn") + d_kbg * ck.eg
        d_qs = _mx(Q, k, "nn") + d_qd * ck.eg
        d_k = _mx(P, ck.kb, "tn") + _mx(Q, ck.qs, "tn") + d_kd * ck.ekd + d_kb * ck.beta
        E = dL * ck.L + d_attn * ck.attn
        kdsum = jnp.sum(d_kd * ck.kd, axis=2, keepdims=True)
        d_gc = (jnp.sum(E, axis=2, keepdims=True) - _colsum_col(E)
                + jnp.sum(d_kbg * ck.kbg, axis=2, keepdims=True)
                + jnp.sum(d_qd * ck.qd, axis=2, keepdims=True) - kdsum)
        d_gc = _add_at_last_row(d_gc, d_gl + jnp.sum(kdsum, axis=1, keepdims=True), hb)
        d_beta = jnp.sum(d_vb * v, axis=2, keepdims=True) + jnp.sum(d_kb * k, axis=2, keepdims=True)
        d_q = d_qs * (GDN_DIM**-0.5)
        d_v = d_vb * ck.beta
        lane = _iota2((CHUNK, LANES), 1)
        for d, (dq_ref, dg_ref) in enumerate(((dqf_ref, dgf_ref), (dqb_ref, dgb_ref))):
            acc = dg_ref[...]
            for hh in range(hb):
                c, h = d * hb + hh, b * hb + hh
                sl = slice(hh * GDN_DIM, (hh + 1) * GDN_DIM)
                dq_ref[0, :, sl] = d_q[c]
                dq_ref[1, :, sl] = d_k[c]
                dq_ref[2, :, sl] = d_v[c]
                acc = jnp.where(lane == h + 8 * d, d_gc[c], acc)
                acc = jnp.where(lane == 16 + h + 8 * d, d_beta[c], acc)
            dg_ref[...] = acc

        @pl.when(b == nb - 1)
        def _():
            for d, dg_ref in enumerate((dgf_ref, dgb_ref)):
                blk = dg_ref[...]
                dg_ref[...] = jnp.where(lane < 16, _exact01(_chunk_tri(CHUNK, d == 0), blk), blk)

    fw, bw = (lambda n: N - 1 - n), (lambda n: n)
    qsp = lambda ci: pl.BlockSpec((3, CHUNK, W), lambda n, b: (0, ci(n), b))
    gsp = lambda ci: pl.BlockSpec((CHUNK, LANES), lambda n, b: (ci(n), 0))
    osp = lambda ci: pl.BlockSpec((CHUNK, W), lambda n, b: (ci(n), b))
    ssp = lambda ci: pl.BlockSpec((1, hb, GDN_DIM, GDN_DIM), lambda n, b: (ci(n), b, 0, 0))
    asp = lambda ci: pl.BlockSpec((1, hb, CHUNK, CHUNK), lambda n, b: (ci(n), b, 0, 0))
    dq_shape = jax.ShapeDtypeStruct((3, T, nh * GDN_DIM), f32)
    dg_shape = jax.ShapeDtypeStruct((T, LANES), f32)
    return pl.pallas_call(
        body,
        out_shape=(dq_shape, dq_shape, dg_shape, dg_shape),
        grid=(N, nb),
        in_specs=[qsp(fw), qsp(bw), gsp(fw), gsp(bw), gsp(fw), gsp(bw), ssp(fw), ssp(bw), asp(fw), asp(bw), osp(fw), osp(bw)],
        out_specs=(qsp(fw), qsp(bw), gsp(fw), gsp(bw)),
        scratch_shapes=[pltpu.VMEM((2, nh, GDN_DIM, GDN_DIM), f32)],
        compiler_params=_cparams(("arbitrary", "arbitrary")),
        name="gdn_scan_bwd",
    )(qkvn, qkvn, gates, gates, gc, gc, st_f, st_b, a_f, a_b, do, do)


class _GlaBatch:
    def __init__(self, q, k, G, nf):
        B, Cn = q.shape[0], q.shape[1]
        self.incl, _, _ = _dir_masks(B, nf, Cn)
        self.eG = jnp.exp(G)
        self.enG = jnp.exp(-G)
        self.qg = q * (GLA_DK**-0.5) * self.eG
        self.kg = k * self.enG
        self.attn = jnp.where(self.incl, _mx(self.qg, self.kg, "nt"), 0.0)
        gl = _last_row(G, nf)
        self.ekd = jnp.exp(gl - G)
        self.kd = k * self.ekd
        self.egl = jnp.exp(gl)


def _gla_chain_inputs(refs_f, refs_b):
    nh = GLA_HEADS
    out = []
    for i, width in enumerate((GLA_DK, GLA_DK, GLA_DV, GLA_DK)):
        out.append(jnp.stack(_heads(refs_f[i], (), nh, width) + _heads(refs_b[i], (), nh, width)))
    return out


def _gla_specs(N, ci):
    nh = GLA_HEADS
    return [
        pl.BlockSpec((CHUNK, nh * GLA_DK), lambda n: (ci(n), 4096 // (nh * GLA_DK))),
        pl.BlockSpec((CHUNK, nh * GLA_DK), lambda n: (ci(n), 4608 // (nh * GLA_DK))),
        pl.BlockSpec((CHUNK, nh * GLA_DV), lambda n: (ci(n), 5120 // (nh * GLA_DV))),
        pl.BlockSpec((CHUNK, nh * GLA_DK), lambda n: (ci(n), 0)),
    ]


def _gla_fwd(proj, G_f, G_b):
    T = proj.shape[0]
    N = T // CHUNK
    nh = GLA_HEADS

    def body(qf, kf, vf, gf, qb, kb, vb, gb, of_ref, ob_ref, sf_ref, sb_ref, s_scr):
        n = pl.program_id(0)

        @pl.when(n == 0)
        def _():
            s_scr[...] = jnp.zeros((2 * nh, GLA_DV, GLA_DK), f32)

        q, k, v, G = _gla_chain_inputs((qf, kf, vf, gf), (qb, kb, vb, gb))
        ck = _GlaBatch(q, k, G, nh)
        St = s_scr[...]
        sf_ref[0], sb_ref[0] = St[:nh], St[nh:]
        o = _mx(ck.qg, St, "nt") + _mx(ck.attn, v, "nn")
        for i in range(nh):
            sl = slice(i * GLA_DV, (i + 1) * GLA_DV)
            of_ref[:, sl] = o[i]
            ob_ref[:, sl] = o[nh + i]
        s_scr[...] = St * ck.egl + _mx(v, ck.kd, "tn")

    fw, bw = (lambda n: n), (lambda n: N - 1 - n)
    osp = lambda ci: pl.BlockSpec((CHUNK, nh * GLA_DV), lambda n: (ci(n), 0))
    ssp = lambda ci: pl.BlockSpec((1, nh, GLA_DV, GLA_DK), lambda n: (ci(n), 0, 0, 0))
    o_shape = jax.ShapeDtypeStruct((T, nh * GLA_DV), f32)
    s_shape = jax.ShapeDtypeStruct((N, nh, GLA_DV, GLA_DK), f32)
    return pl.pallas_call(
        body,
        out_shape=(o_shape, o_shape, s_shape, s_shape),
        grid=(N,),
        in_specs=_gla_specs(N, fw) + _gla_specs(N, bw),
        out_specs=(osp(fw), osp(bw), ssp(fw), ssp(bw)),
        scratch_shapes=[pltpu.VMEM((2 * nh, GLA_DV, GLA_DK), f32)],
        compiler_params=_cparams(("arbitrary",)),
        name="gla_scan_fwd",
    )(proj, proj, proj, G_f, proj, proj, proj, G_b)


def _gla_bwd(proj, G_f, G_b, st_f, st_b, do):
    T = proj.shape[0]
    N = T // CHUNK
    nh = GLA_HEADS

    def body(qf, kf, vf, gf, sf_ref, dof_ref, qb, kb, vb, gb, sb_ref, dob_ref,
             dqf_ref, dkf_ref, dvf_ref, dgf_ref, dqb_ref, dkb_ref, dvb_ref, dgb_ref, ds_scr):
        n = pl.program_id(0)

        @pl.when(n == 0)
        def _():
            ds_scr[...] = jnp.zeros((2 * nh, GLA_DV, GLA_DK), f32)

        q, k, v, G = _gla_chain_inputs((qf, kf, vf, gf), (qb, kb, vb, gb))
        ck = _GlaBatch(q, k, G, nh)
        St = jnp.concatenate([sf_ref[0], sb_ref[0]], axis=0)
        dSt = ds_scr[...]
        dov = jnp.stack(_heads(dof_ref, (), nh, GLA_DV) + _heads(dob_ref, (), nh, GLA_DV))
        d_attn = jnp.where(ck.incl, _mx(dov, v, "nt"), 0.0)
        d_qg = _mx(dov, St, "nn") + _mx(d_attn, ck.kg, "nn")
        d_kg = _mx(d_attn, ck.qg, "tn")
        d_v = _mx(ck.attn, dov, "tn") + _mx(ck.kd, dSt, "nt")
        d_kd = _mx(v, dSt, "nn")
        ds_scr[...] = dSt * ck.egl + _mx(dov, ck.qg, "tn")
        kdd = d_kd * ck.kd
        d_gl = jnp.sum(dSt * St, axis=1, keepdims=True) * ck.egl + jnp.sum(kdd, axis=1, keepdims=True)
        d_q = d_qg * ck.eG * (GLA_DK**-0.5)
        d_k = d_kg * ck.enG + d_kd * ck.ekd
        d_G = _add_at_last_row(d_qg * ck.qg - d_kg * ck.kg - kdd, d_gl, nh)
        shp = (2 * nh, CHUNK, CHUNK)
        bb, ii, jj = _iota3(shp, 0), _iota3(shp, 1), _iota3(shp, 2)
        tri = jnp.logical_or(jnp.logical_and(bb < nh, jj >= ii), jnp.logical_and(bb >= nh, jj <= ii))
        d_gk = _exact01(jnp.where(tri, 1.0, 0.0), d_G)
        for d, (dq_ref, dk_ref, dv_ref, dg_ref) in enumerate(((dqf_ref, dkf_ref, dvf_ref, dgf_ref), (dqb_ref, dkb_ref, dvb_ref, dgb_ref))):
            for i in range(nh):
                c = d * nh + i
                ks, vs = slice(i * GLA_DK, (i + 1) * GLA_DK), slice(i * GLA_DV, (i + 1) * GLA_DV)
                dq_ref[:, ks] = d_q[c]
                dk_ref[:, ks] = d_k[c]
                dv_ref[:, vs] = d_v[c]
                dg_ref[:, ks] = d_gk[c]

    def in_specs(ci):
        return _gla_specs(N, ci) + [
            pl.BlockSpec((1, nh, GLA_DV, GLA_DK), lambda n: (ci(n), 0, 0, 0)),
            pl.BlockSpec((CHUNK, nh * GLA_DV), lambda n: (ci(n), 0)),
        ]

    def out_specs(ci):
        ksp = pl.BlockSpec((CHUNK, nh * GLA_DK), lambda n: (ci(n), 0))
        return [ksp, ksp, pl.BlockSpec((CHUNK, nh * GLA_DV), lambda n: (ci(n), 0)), ksp]

    fw, bw = (lambda n: N - 1 - n), (lambda n: n)
    k_shape = jax.ShapeDtypeStruct((T, nh * GLA_DK), f32)
    v_shape = jax.ShapeDtypeStruct((T, nh * GLA_DV), f32)
    return pl.pallas_call(
        body,
        out_shape=(k_shape, k_shape, v_shape, k_shape) * 2,
        grid=(N,),
        in_specs=in_specs(fw) + in_specs(bw),
        out_specs=tuple(out_specs(fw) + out_specs(bw)),
        scratch_shapes=[pltpu.VMEM((2 * nh, GLA_DV, GLA_DK), f32)],
        compiler_params=_cparams(("arbitrary",)),
        name="gla_scan_bwd",
    )(proj, proj, proj, G_f, st_f, do, proj, proj, proj, G_b, st_b, do)


def _gla_dsum(dq_f, dq_b, dk_f, dk_b, dv_f, dv_b, dproj, tb=256):
    T = dq_f.shape[0]
    wk, wv = GLA_HEADS * GLA_DK, GLA_HEADS * GLA_DV

    def body(qf, qb, kf, kb, vf, vb, _, o_ref):
        o_ref[:, :wk] = (qf[...] + qb[...]).astype(o_ref.dtype)
        o_ref[:, wk : 2 * wk] = (kf[...] + kb[...]).astype(o_ref.dtype)
        o_ref[:, 2 * wk :] = (vf[...] + vb[...]).astype(o_ref.dtype)

    ksp = pl.BlockSpec((tb, wk), lambda i: (i, 0))
    vsp = pl.BlockSpec((tb, wv), lambda i: (i, 0))
    return pl.pallas_call(
        body,
        out_shape=jax.ShapeDtypeStruct(dproj.shape, dproj.dtype),
        grid=(T // tb,),
        in_specs=[ksp, ksp, ksp, ksp, vsp, vsp, _ANY_SPEC],
        out_specs=pl.BlockSpec((tb, 2 * wk + wv), lambda i: (i, DP_GLA_QKV_BLK)),
        input_output_aliases={6: 0},
        compiler_params=_cparams(("parallel",)),
        name="gla_dsum",
    )(dq_f, dq_b, dk_f, dk_b, dv_f, dv_b, dproj)


def _local_step(x, target, wcat, conv_w, avec, dvec, gdn_norm_w, w2f, w2b, b2f, b2b, gla_norm_w, proj_weights, w_pre, w_post,
                send_proj_grads=None, send_in_grads=None):
    h, h_t = _rms_pre(x, w_pre)
    proj = _matmul(h, wcat, "nt", f32, "proj_in", 2048, 512, 1024)
    qkvn = _gdn_prep(proj, conv_w)
    gates, gc = _gdn_gates(proj, avec, dvec)
    oa_f, oa_b, sa_f, sa_b, inv_f, inv_b = _gdn_fwd(qkvn, gates, gc)
    og = _out_norm(oa_f, oa_b, proj, Z_BLK, gdn_norm_w, GDN_DIM, "gdn_out")
    G_f, G_b = _gla_prep(proj, w2f, w2b, b2f, b2b)
    ob_f, ob_b, sb_f, sb_b = _gla_fwd(proj, G_f, G_b)
    obg = _out_norm(ob_f, ob_b, proj, G_B_BLK, gla_norm_w, GLA_DV, "gla_out")
    wpg, wpl, wout = proj_weights(obg)
    (loss_row, d_wpost, dy, m, dout, dya, dyb, dproj, dog, dobg) = _mid(og, obg, proj, x, target, wpg, wpl, wout, w_post)

    d_wout = _matmul(m, dout, "tn", MXU_DTYPE, "dw_out", 512, 512, 2048)
    d_wpg = _matmul(og, dya, "tn", MXU_DTYPE, "dw_proj_gdn", 512, 512, 2048)
    d_wpl = _matmul(obg, dyb, "tn", MXU_DTYPE, "dw_proj_gla", 512, 512, 2048)
    if send_proj_grads is not None:
        gla_norm_w = gla_norm_w + send_proj_grads(d_wpg, d_wpl, d_wout)

    do_b, dproj, d_gla_norm = _out_norm_bwd(dobg, ob_f, ob_b, proj, G_B_BLK, gla_norm_w, GLA_DV, "gla_out_bwd", dproj)
    dq_f, dk_f, dv_f, dgk_f, dq_b, dk_b, dv_b, dgk_b = _gla_bwd(proj, G_f, G_b, sb_f, sb_b, do_b)
    dproj = _gla_dsum(dq_f, dq_b, dk_f, dk_b, dv_f, dv_b, dproj)
    dsmall_gla, d_w2f, d_w2b, d_b2f, d_b2b = _gla_prep_bwd(proj, w2f, w2b, b2f, b2b, dgk_f, dgk_b)

    do_a, dproj, d_gdn_norm = _out_norm_bwd(dog, oa_f, oa_b, proj, Z_BLK, gdn_norm_w, GDN_DIM, "gdn_out_bwd", dproj)
    dqkv_f, dqkv_b, dg_f, dg_b = _gdn_bwd(qkvn, gates, gc, sa_f, sa_b, inv_f, inv_b, do_a)
    dproj, d_conv = _gdn_prep_bwd(proj, conv_w, dqkv_f, dqkv_b, dproj)
    dproj, d_alog, d_dtb = _gdn_gates_bwd(proj, avec, dvec, dg_f, dg_b, dsmall_gla, dproj)

    d_wcat = _matmul(h_t, dproj, "nn", MXU_DTYPE, "dw_in", 1024, 512, 2048, out_t=True)
    started = None if send_in_grads is None else send_in_grads(d_wcat)
    dh = _matmul(dproj, wcat, "nn", f32, "dh", 1024, 1024, 2432, after=started)
    grad_x, d_wpre = _rms_pre_bwd(x, w_pre, dh, dy)
    return dict(loss_row=loss_row, grad_x=grad_x, d_wcat=d_wcat, d_conv=d_conv, d_alog=d_alog, d_dtb=d_dtb,
                d_gdn_norm=d_gdn_norm, d_wpg=d_wpg, d_w2f=d_w2f, d_w2b=d_w2b, d_b2f=d_b2f, d_b2b=d_b2b,
                d_gla_norm=d_gla_norm, d_wpl=d_wpl, d_wout=d_wout, d_wpost=d_wpost, d_wpre=d_wpre)


def _to_cat(wf):
    pad = jnp.zeros((N_CAT - N_IN,) + wf.shape[1:], wf.dtype)
    return jnp.concatenate([wf[:4096], wf[4128:6176], wf[7232:9280], wf[6176:7200], wf[4096:4128], wf[7200:7232], pad], axis=0)


def _from_cat(wc):
    return jnp.concatenate([wc[:4096], wc[9216:9248], wc[4096:6144], wc[8192:9216], wc[9248:9280], wc[6144:8192]], axis=0)


def _lane_row(*pieces):
    row = jnp.concatenate(pieces, axis=1)
    return jnp.pad(row, ((0, 0), (0, LANES - row.shape[1])))


def _pad_w2(w2, row0):
    return jnp.pad(w2, ((row0, LANES - row0 - w2.shape[0]), (0, 0)))


SM_ROWS, SM_COLS = 32, 384
REP_ROWS = 32


def _pack_small_shard(conv, w2f, w2b):
    lead = conv.shape[:-2]
    z = lambda r, c: jnp.zeros(lead + (r, c), f32)
    top = jnp.concatenate([conv, z(8 - CONV_K, SM_COLS)], axis=-2)
    mid = jnp.concatenate([w2f, w2b, z(16, SM_COLS - 128)], axis=-1)
    return jnp.concatenate([top, mid, z(SM_ROWS - 24, SM_COLS)], axis=-2)


LOSS_ROW = 28


def _pack_rep(ln_pre, ln_post, b2f, b2b, gla_norm, gdn_norm, alog_f, alog_b, dt_f, dt_b, loss=None):
    last = _lane_row(alog_f, alog_b, dt_f, dt_b)
    extra = jnp.zeros((1, LANES), f32) if loss is None else _lane_row(loss)
    rows = [ln_pre.reshape(8, LANES), ln_post.reshape(8, LANES), b2f.reshape(4, LANES), b2b.reshape(4, LANES),
            gla_norm.reshape(2, LANES), gdn_norm.reshape(1, LANES), last, extra, jnp.zeros((REP_ROWS - LOSS_ROW - 1, LANES), f32)]
    return jnp.concatenate(rows, axis=0)


def _unpack_rep(p):
    return dict(ln_pre_w=p[0:8].reshape(1, 1024), ln_post_w=p[8:16].reshape(1, 1024), gk_b2_fwd=p[16:20].reshape(1, 512),
                gk_b2_bwd=p[20:24].reshape(1, 512), gla_norm_w=p[24:26].reshape(1, 256), gdn_norm_w=p[26:27],
                a_log_fwd=p[27:28, 0:8], a_log_bwd=p[27:28, 8:16], dt_bias_fwd=p[27:28, 16:24], dt_bias_bwd=p[27:28, 24:32])


_MESH = pl.DeviceIdType.MESH


N_GATHERED = 2


def _gather_weights(w_in_s, sm_s):
    shapes = (w_in_s.shape, sm_s.shape)
    dtypes = (MXU_DTYPE, f32)

    def body(win_ref, sm_ref, gin_ref, gsm_ref, send_sems, recv_sems):
        x, y, c = lax.axis_index("x"), lax.axis_index("y"), lax.axis_index("c")
        me, sibling = (x, y, c), (x, y, 1 - c)
        chips = [(1 - x, y), (x, 1 - y), (1 - x, 1 - y)]
        bufs = (gin_ref, gsm_ref)

        def idx(px, py, pc):
            return 4 * px + 2 * py + pc

        for r in range(w_in_s.shape[1] // LANES):
            cols = slice(r * LANES, (r + 1) * LANES)
            gin_ref[idx(*me), :, cols] = win_ref[:, cols].astype(MXU_DTYPE)
        gsm_ref[idx(*me)] = sm_ref[...]

        def copy(a, k, block, to):
            blk = bufs[a].at[idx(*block)]
            return pltpu.make_async_remote_copy(src_ref=blk, dst_ref=blk, send_sem=send_sems.at[7 * a + k],
                                                recv_sem=recv_sems.at[7 * a + k], device_id=to, device_id_type=_MESH)

        first, passed = [], []
        for a in range(N_GATHERED):
            first.append(copy(a, 0, me, sibling))
            first += [copy(a, 1 + j, me, (*chip, c)) for j, chip in enumerate(chips)]
        for cp in first:
            cp.start()
        for j, chip in enumerate(chips):
            for a in range(N_GATHERED):
                copy(a, 1 + j, (*chip, c), me).wait_recv()
                cp = copy(a, 4 + j, (*chip, c), sibling)
                cp.start()
                passed.append(cp)
        for a in range(N_GATHERED):
            copy(a, 0, sibling, me).wait_recv()
            for j, chip in enumerate(chips):
                copy(a, 4 + j, (*chip, 1 - c), me).wait_recv()
        for cp in first + passed:
            cp.wait_send()

    vm = pl.BlockSpec(memory_space=pltpu.VMEM)
    return pl.pallas_call(
        body,
        out_shape=tuple(jax.ShapeDtypeStruct((N_DEV,) + s, d) for s, d in zip(shapes, dtypes)),
        in_specs=[vm, vm],
        out_specs=(vm, vm),
        scratch_shapes=[pltpu.SemaphoreType.DMA((7 * N_GATHERED,)), pltpu.SemaphoreType.DMA((7 * N_GATHERED,))],
        compiler_params=pltpu.CompilerParams(vmem_limit_bytes=VMEM_LIMIT),
        name="gather_weights",
    )(w_in_s, sm_s)


N_CHIP = 4
_EFFECT = pltpu.SideEffectType.DATAFLOW_SIDE_EFFECTING
_HBM_SPEC = pl.BlockSpec(memory_space=pltpu.HBM)
_SEM_SPEC = pl.BlockSpec(memory_space=pltpu.SEMAPHORE)


def _relation_peer(k, x, y, c):
    px = (1 - x) if (k & 4) else x
    py = (1 - y) if (k & 2) else y
    pc = (1 - c) if (k & 1) else c
    return (px, py, pc), 4 * px + 2 * py + pc


_PLAN_COPIES = {"gather": N_DEV - 1, "blocks": N_DEV - 1, "sibling": 4, "chips": 3}
_PLAN_SLOTS = {"gather": N_DEV, "blocks": N_DEV, "sibling": 4, "chips": 4}


def _plan_copies(plan, src_ref, land_ref, send_sems, recv_sems, arrival):
    x, y, c = lax.axis_index("x"), lax.axis_index("y"), lax.axis_index("c")
    me, my_chip = 4 * x + 2 * y + c, 2 * x + y
    out = []

    def add(i, src, there, here, dev):
        out.append(pltpu.make_async_remote_copy(src_ref=src, dst_ref=land_ref.at[here if arrival else there], send_sem=send_sems.at[i],
                                                recv_sem=recv_sems.at[i], device_id=dev, device_id_type=_MESH))

    if plan in ("gather", "blocks"):
        for k in range(1, N_DEV):
            dev, p = _relation_peer(k, x, y, c)
            add(k - 1, src_ref.at[p] if plan == "blocks" else src_ref, me, p, dev)
    elif plan == "sibling":
        for j in range(4):
            add(j, src_ref.at[2 * j + 1 - c], j, j, (x, y, 1 - c))
    else:
        for i, k in enumerate((2, 4, 6)):
            (px, py, pc), _ = _relation_peer(k, x, y, c)
            add(i, src_ref.at[2 * px + py], my_chip, 2 * px + py, (px, py, pc))
    return out


def _exchange_start(src, plan, name):
    n = _PLAN_COPIES[plan]
    land = lax.empty((_PLAN_SLOTS[plan],) + (src.shape if plan == "gather" else src.shape[1:]), src.dtype)

    def body(src_ref, land_ref, send_sems, recv_sems, src_thru, land_thru, token):
        for cp in _plan_copies(plan, src_ref, land_ref, send_sems, recv_sems, False):
            cp.start()
        token[...] = jnp.zeros_like(token)

    return pl.pallas_call(
        body,
        name=name,
        out_shape=(pltpu.SemaphoreType.DMA((n,)), pltpu.SemaphoreType.DMA((n,)), pltpu.HBM(src.shape, src.dtype),
                   pltpu.HBM(land.shape, land.dtype), jax.ShapeDtypeStruct((8, LANES), f32)),
        in_specs=(_HBM_SPEC, _HBM_SPEC),
        out_specs=(_SEM_SPEC, _SEM_SPEC, _HBM_SPEC, _HBM_SPEC, pl.BlockSpec(memory_space=pltpu.VMEM)),
        input_output_aliases={0: 2, 1: 3},
        compiler_params=pltpu.CompilerParams(has_side_effects=_EFFECT),
    )(pltpu.with_memory_space_constraint(src, pltpu.HBM), pltpu.with_memory_space_constraint(land, pltpu.HBM))


def _exchange_wait(started, after, plan, name):
    send_sems, recv_sems, src_thru, land_thru, _ = started

    def body(src_ref, land_ref, send_sems, recv_sems, after_ref, src_dead, got_ref):
        for cp in _plan_copies(plan, src_ref, land_ref, send_sems, recv_sems, True):
            cp.wait_send()
            cp.wait_recv()

    return pl.pallas_call(
        body,
        name=name,
        out_shape=(pltpu.HBM(src_thru.shape, src_thru.dtype), pltpu.HBM(land_thru.shape, land_thru.dtype)),
        in_specs=(_HBM_SPEC, _HBM_SPEC, _SEM_SPEC, _SEM_SPEC, pl.BlockSpec(memory_space=pl.ANY)),
        out_specs=(_HBM_SPEC, _HBM_SPEC),
        input_output_aliases={0: 0, 1: 1},
        compiler_params=pltpu.CompilerParams(has_side_effects=_EFFECT),
    )(src_thru, land_thru, send_sems, recv_sems, after)


def _pair_sum(g, r, name, tc=LANES):
    _, R, Cc = g.shape

    def body(g_ref, r_ref, o_ref):
        o_ref[...] = (g_ref[...].astype(f32) + r_ref[...].astype(f32)).astype(o_ref.dtype)

    return pl.pallas_call(
        body,
        out_shape=jax.ShapeDtypeStruct(r.shape, r.dtype),
        grid=(N_CHIP, Cc // tc),
        in_specs=[pl.BlockSpec((1, R, tc), lambda k, i: (2 * k + lax.axis_index("c"), 0, i)), pl.BlockSpec((1, R, tc), lambda k, i: (k, 0, i))],
        out_specs=pl.BlockSpec((1, R, tc), lambda k, i: (k, 0, i)),
        compiler_params=_cparams(("parallel", "parallel")),
        name=name,
    )(g, r)


def _exchange_small(g_sm, g_rep):
    srcs = (g_sm, g_rep)

    def body(in0, in1, out0, out1, send_sems, recv_sems, local_sems):
        x, y, c = lax.axis_index("x"), lax.axis_index("y"), lax.axis_index("c")
        me = 4 * x + 2 * y + c
        ins, outs = (in0, in1), (out0, out1)

        def copy(a, k, arrival):
            dev, p = _relation_peer(k, x, y, c)
            return pltpu.make_async_remote_copy(src_ref=ins[a].at[p] if a == 0 else ins[a], dst_ref=outs[a].at[p if arrival else me],
                                                send_sem=send_sems.at[7 * a + k - 1], recv_sem=recv_sems.at[7 * a + k - 1],
                                                device_id=dev, device_id_type=_MESH)

        pairs = [(a, k) for a in (0, 1) for k in range(1, N_DEV)]
        sends = [copy(a, k, False) for a, k in pairs]
        for cp in sends:
            cp.start()
        local = [pltpu.make_async_copy(ins[0].at[me], outs[0].at[me], local_sems.at[0]),
                 pltpu.make_async_copy(ins[1], outs[1].at[me], local_sems.at[1])]
        for cp in local:
            cp.start()
        for a, k in pairs:
            copy(a, k, True).wait_recv()
        for cp in sends:
            cp.wait_send()
        for cp in local:
            cp.wait()

    return pl.pallas_call(
        body,
        out_shape=(jax.ShapeDtypeStruct(g_sm.shape, g_sm.dtype), jax.ShapeDtypeStruct((N_DEV,) + g_rep.shape, g_rep.dtype)),
        in_specs=[_HBM_SPEC] * 2,
        out_specs=(_HBM_SPEC,) * 2,
        scratch_shapes=[pltpu.SemaphoreType.DMA((14,)), pltpu.SemaphoreType.DMA((14,)), pltpu.SemaphoreType.DMA((2,))],
        name="exchange_small",
    )(*srcs)


def _sum_adam(parts, w, m, v, name, tb, tc=None):
    R, Cc = w.shape
    n_parts = parts.shape[0]
    tb = R if tc else min(tb, R)
    tc = tc or Cc
    assert R % tb == 0 and Cc % tc == 0
    bc1 = 1.0 - ADAM_B1**ADAM_STEP
    bc2 = 1.0 - ADAM_B2**ADAM_STEP

    def body(p_ref, w_ref, m_ref, v_ref, g_ref, d_ref, nm_ref, nv_ref):
        g = p_ref[0].astype(f32)
        for j in range(1, n_parts):
            g = g + p_ref[j].astype(f32)
        g_ref[...] = g
        m2 = ADAM_B1 * m_ref[...] + (1.0 - ADAM_B1) * g
        v2 = ADAM_B2 * v_ref[...] + (1.0 - ADAM_B2) * (g * g)
        nm_ref[...] = m2
        nv_ref[...] = v2
        d_ref[...] = -ADAM_LR * ((m2 / bc1) / (jnp.sqrt(v2 / bc2) + ADAM_EPS) + ADAM_WD * w_ref[...])

    blk = pl.BlockSpec((tb, tc), lambda i, j: (i, j))
    o = jax.ShapeDtypeStruct((R, Cc), f32)
    return pl.pallas_call(
        body,
        out_shape=(o, o, o, o),
        grid=(R // tb, Cc // tc),
        in_specs=[pl.BlockSpec((n_parts, tb, tc), lambda i, j: (0, i, j)), blk, blk, blk],
        out_specs=(blk, blk, blk, blk),
        compiler_params=_cparams(("parallel", "parallel")),
        name=name,
    )(parts, w, m, v)


_WEIGHTS = ["ln_pre_w", "w_in", "conv_w", "a_log_fwd", "a_log_bwd", "dt_bias_fwd", "dt_bias_bwd", "gdn_norm_w", "w_proj_gdn",
            "gk_w2_fwd", "gk_b2_fwd", "gk_w2_bwd", "gk_b2_bwd", "gla_norm_w", "w_proj_gla", "w_out", "ln_post_w"]
_REP_ORDER = ["ln_pre_w", "ln_post_w", "gk_b2_fwd", "gk_b2_bwd", "gla_norm_w", "gdn_norm_w", "a_log_fwd", "a_log_bwd", "dt_bias_fwd", "dt_bias_bwd"]


def kernel(x, ln_pre_w, w_in, conv_w, a_log_fwd, a_log_bwd, dt_bias_fwd, dt_bias_bwd, gdn_norm_w, w_proj_gdn, gk_w2_fwd, gk_b2_fwd, gk_w2_bwd, gk_b2_bwd, gla_norm_w, w_proj_gla, w_out, ln_post_w, loss_target, m_ln_pre_w, m_w_in, m_conv_w, m_a_log_fwd, m_a_log_bwd, m_dt_bias_fwd, m_dt_bias_bwd, m_gdn_norm_w, m_w_proj_gdn, m_gk_w2_fwd, m_gk_b2_fwd, m_gk_w2_bwd, m_gk_b2_bwd, m_gla_norm_w, m_w_proj_gla, m_w_out, m_ln_post_w, v_ln_pre_w, v_w_in, v_conv_w, v_a_log_fwd, v_a_log_bwd, v_dt_bias_fwd, v_dt_bias_bwd, v_gdn_norm_w, v_w_proj_gdn, v_gk_w2_fwd, v_gk_b2_fwd, v_gk_w2_bwd, v_gk_b2_bwd, v_gla_norm_w, v_w_proj_gla, v_w_out, v_ln_post_w):
    args = locals()
    W = {n: args[n] for n in _WEIGHTS}
    M = {n: args["m_" + n] for n in _WEIGHTS}
    V = {n: args["v_" + n] for n in _WEIGHTS}

    wp_stack = lambda P: jnp.concatenate([P["w_proj_gdn"], P["w_proj_gla"], P["w_out"]], axis=0)
    sm_pack = lambda P: _pack_small_shard(P["conv_w"][0], P["gk_w2_fwd"][0], P["gk_w2_bwd"][0])
    me = 4 * lax.axis_index("x") + 2 * lax.axis_index("y") + lax.axis_index("c")
    w_in_t, m_in_t, v_in_t = (jnp.transpose(a[0]) for a in (w_in, m_w_in, v_w_in))
    g_in, g_sm = _gather_weights(w_in_t, sm_pack(W))
    wp_own = wp_stack(W).astype(MXU_DTYPE)
    wp_started = _exchange_start(wp_own, "gather", "gather_proj_start")

    def proj_weights(after):
        own, land = _exchange_wait(wp_started, after, "gather", "gather_proj_wait")
        g_p = lax.dynamic_update_slice(land, own[None], (me, 0, 0, 0))
        return tuple(g_p[:, i].reshape(D_MODEL, D_MODEL) for i in range(3))

    rows_p = 3 * D_MODEL // N_DEV
    sent = {}

    def send_proj_grads(d_wpg, d_wpl, d_wout):
        p_p = jnp.stack([d_wpg, d_wpl, d_wout]).reshape(3, N_DEV, D_MODEL // N_DEV, D_MODEL).transpose(1, 0, 2, 3)
        sent["p_p"] = p_p.reshape(N_DEV, rows_p, D_MODEL)
        sent["started"] = _exchange_start(sent["p_p"], "blocks", "exchange_proj_start")
        return sent["started"][4][0:1, 0:1]

    my_chip = 2 * lax.axis_index("x") + lax.axis_index("y")

    def send_in_grads(d_wcat):
        p_in = _from_cat(d_wcat).reshape(N_DEV, SHARD_IN, D_MODEL)
        sib = _exchange_start(p_in, "sibling", "exchange_in_sibling_start")
        p_p, land_p = _exchange_wait(sent["started"], sib[4], "blocks", "exchange_proj_wait")
        r_p = lax.dynamic_update_slice(land_p, lax.dynamic_index_in_dim(p_p, me, 0, keepdims=True), (me, 0, 0))
        sent["o_p"] = _sum_adam(r_p, wp_stack(W).reshape(rows_p, D_MODEL), wp_stack(M).reshape(rows_p, D_MODEL),
                                wp_stack(V).reshape(rows_p, D_MODEL), "adam_w_proj", 128)
        p_in, sib_land = _exchange_wait(sib, sent["o_p"][0], "sibling", "exchange_in_sibling_wait")
        sent["chips"] = _exchange_start(_pair_sum(p_in, sib_land, "pair_sum_w_in"), "chips", "exchange_in_chips_start")
        return sent["chips"][4]

    wcat = _to_cat(g_in.reshape(N_IN, D_MODEL))
    conv_full = g_sm[:, 0:CONV_K, :].transpose(1, 0, 2).reshape(CONV_K, N_DEV * SM_COLS)
    w2f_full = g_sm[:, 8:24, 0:64].transpose(1, 0, 2).reshape(16, 512)
    w2b_full = g_sm[:, 8:24, 64:128].transpose(1, 0, 2).reshape(16, 512)

    avec = _lane_row(a_log_fwd, a_log_bwd)
    dvec = _lane_row(dt_bias_fwd, dt_bias_bwd)
    G = _local_step(x[0], loss_target[0], wcat, conv_full, avec, dvec, gdn_norm_w, _pad_w2(w2f_full, 32), _pad_w2(w2b_full, 48),
                    gk_b2_fwd, gk_b2_bwd, gla_norm_w, proj_weights, ln_pre_w + wp_started[4][0:1, 0:1], ln_post_w, send_proj_grads,
                    send_in_grads)

    p_sm = _pack_small_shard(G["d_conv"].reshape(CONV_K, N_DEV, SM_COLS).transpose(1, 0, 2),
                             G["d_w2f"][32:48].reshape(16, N_DEV, 64).transpose(1, 0, 2),
                             G["d_w2b"][48:64].reshape(16, N_DEV, 64).transpose(1, 0, 2))
    p_rep = _pack_rep(G["d_wpre"], G["d_wpost"], G["d_b2f"], G["d_b2b"], G["d_gla_norm"], G["d_gdn_norm"],
                      G["d_alog"][:, 0:8], G["d_alog"][:, 8:16], G["d_dtb"][:, 0:8], G["d_dtb"][:, 8:16],
                      loss=0.5 * jnp.sum(G["loss_row"], axis=1, keepdims=True) / D_MODEL)
    r_sm, r_rep = _exchange_small(p_sm, p_rep)
    q_in, land_in = _exchange_wait(sent["chips"], r_rep, "chips", "exchange_in_chips_wait")
    r_in = lax.dynamic_update_slice(land_in, lax.dynamic_index_in_dim(q_in, my_chip, 0, keepdims=True), (my_chip, 0, 0))

    o_in = _sum_adam(r_in, w_in_t, m_in_t, v_in_t, "adam_w_in", SHARD_IN, tc=LANES)
    o_p = sent["o_p"]
    o_sm = _sum_adam(r_sm, sm_pack(W), sm_pack(M), sm_pack(V), "adam_small", SM_ROWS)
    rep_pack = lambda P: _pack_rep(*[P[n] for n in _REP_ORDER])
    o_rep = _sum_adam(r_rep, rep_pack(W), rep_pack(M), rep_pack(V), "adam_rep", REP_ROWS)

    res = []
    for kind in range(4):
        rep = _unpack_rep(o_rep[kind])
        per = dict(rep)
        per["w_in"] = jnp.transpose(o_in[kind])[None]
        pp = o_p[kind].reshape(3, 1, D_MODEL // N_DEV, D_MODEL)
        per["w_proj_gdn"], per["w_proj_gla"], per["w_out"] = pp[0], pp[1], pp[2]
        per["conv_w"] = o_sm[kind][None, 0:CONV_K, :]
        per["gk_w2_fwd"] = o_sm[kind][None, 8:24, 0:64]
        per["gk_w2_bwd"] = o_sm[kind][None, 8:24, 64:128]
        res.append([per[n] for n in _WEIGHTS])
    loss = o_rep[0][LOSS_ROW, 0]
    return (loss, G["grad_x"][None], *res[0], *res[1], *res[2], *res[3])
```

```python
import functools
import math

import jax
import jax.numpy as jnp
from jax import lax
from jax.experimental import pallas as pl
from jax.experimental.pallas import tpu as pltpu

f32 = jnp.float32
MXU_DTYPE = jnp.bfloat16
HI = lax.Precision.HIGHEST

D_MODEL = 1024
CHUNK = 64
NORM_EPS = 1e-6
GDN_HEADS, GDN_DIM = 8, 128
GLA_HEADS, GLA_DK, GLA_DV = 4, 128, 256
GATE_NORMALIZER = 16.0
CONV_K = 5
N_IN = 9280
N_DEV = 8
SHARD_IN = N_IN // N_DEV

N_CAT = 9728
SMALL_OFF = 9216
LANES = 128

ADAM_LR, ADAM_B1, ADAM_B2, ADAM_EPS, ADAM_WD, ADAM_STEP = 0.001, 0.9, 0.999, 1e-08, 0.01, 10

VMEM_LIMIT = 56 * 1024 * 1024


def _cparams(sem=None):
    return pltpu.CompilerParams(dimension_semantics=sem, vmem_limit_bytes=VMEM_LIMIT)


_DN = {"nn": (((1,), (0,)), ((), ())), "nt": (((1,), (1,)), ((), ())), "tn": (((0,), (0,)), ((), ()))}
_BDN = {"nn": (((2,), (1,)), ((0,), (0,))), "nt": (((2,), (2,)), ((0,), (0,))), "tn": (((1,), (1,)), ((0,), (0,)))}


def _dot16(a, b, kind):
    return lax.dot_general(a, b, (_DN if a.ndim == 2 else _BDN)[kind], preferred_element_type=f32)


def _mx(a, b, kind):
    return _dot16(a.astype(MXU_DTYPE), b.astype(MXU_DTYPE), kind)


def _split2(a):
    hi = a.astype(jnp.bfloat16)
    return hi, (a - hi.astype(f32)).astype(jnp.bfloat16)


def _split3(a):
    p1 = a.astype(jnp.bfloat16)
    r = a - p1.astype(f32)
    p2 = r.astype(jnp.bfloat16)
    return p1, p2, (r - p2.astype(f32)).astype(jnp.bfloat16)


def _h3s(a2, b2, kind):
    (ah, al), (bh, bl) = a2, b2
    return _dot16(ah, bh, kind) + (_dot16(ah, bl, kind) + _dot16(al, bh, kind))


def _h3(a, b, kind):
    return _h3s(_split2(a), _split2(b), kind)


def _exact01(t01, x, kind="nn"):
    t = t01.astype(jnp.bfloat16)
    x1, x2, x3 = _split3(x)
    return _dot16(t, x1, kind) + (_dot16(t, x2, kind) + _dot16(t, x3, kind))


def _colsum_col(e):
    ones = jnp.ones(e.shape[:-1] + (LANES,), jnp.bfloat16)
    e1, e2, e3 = _split3(e)
    return (_dot16(e1, ones, "tn") + (_dot16(e2, ones, "tn") + _dot16(e3, ones, "tn")))[..., 0:1]


def _iota3(shape, axis):
    return lax.broadcasted_iota(jnp.int32, shape, axis)


def _dir_masks(B, nf, Cn):
    shp = (B, Cn, Cn)
    bb, ii, jj = _iota3(shp, 0), _iota3(shp, 1), _iota3(shp, 2)
    fwd = bb < nf
    rev = jnp.logical_not(fwd)
    incl = jnp.logical_or(jnp.logical_and(fwd, ii >= jj), jnp.logical_and(rev, ii <= jj))
    strict = jnp.logical_and(incl, ii != jj)
    return incl, strict, ii == jj


def _last_row(x, nf):
    B, Cn = x.shape[0], x.shape[1]
    fwd = _iota3((B, 1, 1), 0) < nf
    return jnp.where(fwd, x[:, Cn - 1 : Cn, :], x[:, 0:1, :])


def _add_at_last_row(x, val, nf):
    B, Cn = x.shape[0], x.shape[1]
    bb, rr = _iota3((B, Cn, 1), 0), _iota3((B, Cn, 1), 1)
    at = jnp.logical_or(jnp.logical_and(bb < nf, rr == Cn - 1), jnp.logical_and(bb >= nf, rr == 0))
    return x + jnp.where(at, val, 0.0)


def _sigmoid(x):
    return jax.nn.sigmoid(x)


def _silu(x):
    return x * _sigmoid(x)


def _silu_grad(x):
    s = _sigmoid(x)
    return s * (1.0 + x * (1.0 - s))


def _softplus(x):
    u = jnp.exp(-jnp.abs(x))
    l1p = jnp.where(u < 1e-3, u * (1.0 - u * (0.5 - u * (1.0 / 3.0))), jnp.log(1.0 + u))
    return jnp.maximum(x, 0.0) + l1p


def _iota2(shape, axis):
    return lax.broadcasted_iota(jnp.int32, shape, axis)


def _matmul(a, b, kind, out_dtype, name, tm, tn, tk, out_t=False, after=None):
    extra = [] if after is None else [after]
    if kind == "nn":
        (M, K), N = a.shape, b.shape[1]
    elif kind == "nt":
        (M, K), N = a.shape, b.shape[0]
    else:
        (K, M), N = a.shape, b.shape[1]
    tm, tn, tk = min(tm, M), min(tn, N), min(tk, K)
    assert M % tm == 0 and N % tn == 0 and K % tk == 0, (name, M, N, K)
    nk = K // tk
    if kind == "tn":
        a_spec = pl.BlockSpec((tk, tm), lambda i, j, k: (k, i))
    else:
        a_spec = pl.BlockSpec((tm, tk), lambda i, j, k: (i, k))
    if kind == "nt":
        b_spec = pl.BlockSpec((tn, tk), lambda i, j, k: (j, k))
    else:
        b_spec = pl.BlockSpec((tk, tn), lambda i, j, k: (k, j))

    def body(a_ref, b_ref, *rest):
        o_ref, acc = rest[len(extra)], rest[len(extra) + 1 :]
        p = _mx(a_ref[...], b_ref[...], kind)
        if nk == 1:
            o_ref[...] = (p.T if out_t else p).astype(out_dtype)
        else:
            assert not out_t
            acc_ref = acc[0]
            k = pl.program_id(2)

            @pl.when(k == 0)
            def _():
                acc_ref[...] = p

            @pl.when(k > 0)
            def _():
                acc_ref[...] += p

            @pl.when(k == nk - 1)
            def _():
                o_ref[...] = acc_ref[...].astype(out_dtype)

    return pl.pallas_call(
        body,
        out_shape=jax.ShapeDtypeStruct((N, M) if out_t else (M, N), out_dtype),
        grid=(M // tm, N // tn, nk),
        in_specs=[a_spec, b_spec] + [pl.BlockSpec((8, LANES), lambda i, j, k: (0, 0))] * len(extra),
        out_specs=pl.BlockSpec((tn, tm), lambda i, j, k: (j, i)) if out_t else pl.BlockSpec((tm, tn), lambda i, j, k: (i, j)),
        scratch_shapes=[] if nk == 1 else [pltpu.VMEM((tm, tn), f32)],
        compiler_params=_cparams(("parallel", "parallel", "arbitrary")),
        name=name,
    )(a, b, *extra)


def _rms_pre(x, w, tb=256):
    T = x.shape[0]

    def body(x_ref, w_ref, h_ref, ht_ref):
        xv = x_ref[...]
        r = lax.rsqrt(jnp.mean(xv * xv, axis=1, keepdims=True) + NORM_EPS)
        h = xv * r * w_ref[...]
        h_ref[...] = h.astype(h_ref.dtype)
        ht_ref[...] = h.T.astype(ht_ref.dtype)

    return pl.pallas_call(
        body,
        out_shape=(jax.ShapeDtypeStruct((T, D_MODEL), MXU_DTYPE), jax.ShapeDtypeStruct((D_MODEL, T), MXU_DTYPE)),
        grid=(T // tb,),
        in_specs=[pl.BlockSpec((tb, D_MODEL), lambda i: (i, 0)), pl.BlockSpec((1, D_MODEL), lambda i: (0, 0))],
        out_specs=(pl.BlockSpec((tb, D_MODEL), lambda i: (i, 0)), pl.BlockSpec((D_MODEL, tb), lambda i: (0, i))),
        compiler_params=_cparams(("parallel",)),
        name="rms_pre",
    )(x, w)


def _rms_pre_bwd(x, w, dh, dy, tb=256):
    T = x.shape[0]

    def body(x_ref, w_ref, dh_ref, dy_ref, dx_ref, dw_ref):
        i = pl.program_id(0)
        xv, dhv = x_ref[...], dh_ref[...]
        r = lax.rsqrt(jnp.mean(xv * xv, axis=1, keepdims=True) + NORM_EPS)
        dhw = dhv * w_ref[...]
        dx_ref[...] = dy_ref[...] + r * dhw - xv * (r * r * r) * jnp.mean(dhw * xv, axis=1, keepdims=True)
        part = jnp.sum(dhv * xv * r, axis=0, keepdims=True)

        @pl.when(i == 0)
        def _():
            dw_ref[...] = part

        @pl.when(i > 0)
        def _():
            dw_ref[...] += part

    blk = pl.BlockSpec((tb, D_MODEL), lambda i: (i, 0))
    row = pl.BlockSpec((1, D_MODEL), lambda i: (0, 0))
    return pl.pallas_call(
        body,
        out_shape=(jax.ShapeDtypeStruct((T, D_MODEL), f32), jax.ShapeDtypeStruct((1, D_MODEL), f32)),
        grid=(T // tb,),
        in_specs=[blk, row, blk, blk],
        out_specs=(blk, row),
        compiler_params=_cparams(("arbitrary",)),
        name="rms_pre_bwd",
    )(x, w, dh, dy)


def _conv5(u, cw):
    T = u.shape[0]
    t = _iota2(u.shape, 0)
    acc = u * cw[2:3, :]
    for j in (0, 1, 3, 4):
        d = j - 2
        sh = pltpu.roll(u, (-d) % T, 0)
        valid = jnp.logical_and(t + d >= 0, t + d < T)
        acc = acc + jnp.where(valid, sh, 0.0) * cw[j : j + 1, :]
    return acc


def _gdn_prep(proj, conv_w):
    T = proj.shape[0]
    nh = GDN_HEADS

    def body(u_ref, cw_ref, o_ref):
        j = pl.program_id(0)
        s = _silu(_conv5(u_ref[...], cw_ref[...]))

        @pl.when(j < 2 * nh)
        def _():
            o_ref[0] = s * lax.rsqrt(jnp.sum(s * s, axis=1, keepdims=True) + NORM_EPS)

        @pl.when(j >= 2 * nh)
        def _():
            o_ref[0] = s

    return pl.pallas_call(
        body,
        out_shape=jax.ShapeDtypeStruct((3, T, GDN_HEADS * GDN_DIM), f32),
        grid=(3 * nh,),
        in_specs=[pl.BlockSpec((T, GDN_DIM), lambda j: (0, j)), pl.BlockSpec((CONV_K, GDN_DIM), lambda j: (0, j))],
        out_specs=pl.BlockSpec((1, T, GDN_DIM), lambda j: (j // nh, 0, j % nh)),
        compiler_params=_cparams(("parallel",)),
        name="gdn_prep",
    )(proj, conv_w)


def _gdn_prep_bwd(proj, conv_w, dqkv_f, dqkv_b, dproj):
    T = proj.shape[0]
    nh = GDN_HEADS

    def body(u_ref, cw_ref, df_ref, db_ref, _, du_ref, dcw_ref):
        j = pl.program_id(0)
        u, cw = u_ref[...], cw_ref[...]
        c = _conv5(u, cw)
        s = _silu(c)
        dn = df_ref[0] + db_ref[0]
        rinv = lax.rsqrt(jnp.sum(s * s, axis=1, keepdims=True) + NORM_EPS)
        ds_norm = rinv * dn - s * (rinv * rinv * rinv) * jnp.sum(dn * s, axis=1, keepdims=True)
        ds = jnp.where(j < 2 * nh, ds_norm, dn)
        dc = ds * _silu_grad(c)
        t = _iota2(u.shape, 0)
        du = dc * cw[2:3, :]
        rows = []
        for jj in range(CONV_K):
            d = jj - 2
            if d == 0:
                rows.append(jnp.sum(dc * u, axis=0, keepdims=True))
                continue
            ush = pltpu.roll(u, (-d) % T, 0)
            uvalid = jnp.logical_and(t + d >= 0, t + d < T)
            rows.append(jnp.sum(dc * jnp.where(uvalid, ush, 0.0), axis=0, keepdims=True))
            dsh = pltpu.roll(dc, d % T, 0)
            dvalid = jnp.logical_and(t - d >= 0, t - d < T)
            du = du + jnp.where(dvalid, dsh, 0.0) * cw[jj : jj + 1, :]
        du_ref[...] = du.astype(du_ref.dtype)
        for jj in range(CONV_K):
            dcw_ref[jj : jj + 1, :] = rows[jj]

    qspec = pl.BlockSpec((1, T, GDN_DIM), lambda j: (j // nh, 0, j % nh))
    return pl.pallas_call(
        body,
        out_shape=(jax.ShapeDtypeStruct(dproj.shape, dproj.dtype), jax.ShapeDtypeStruct((CONV_K, 3 * nh * GDN_DIM), f32)),
        grid=(3 * nh,),
        in_specs=[pl.BlockSpec((T, GDN_DIM), lambda j: (0, j)), pl.BlockSpec((CONV_K, GDN_DIM), lambda j: (0, j)), qspec, qspec, _ANY_SPEC],
        out_specs=(pl.BlockSpec((T, GDN_DIM), lambda j: (0, j)), pl.BlockSpec((CONV_K, GDN_DIM), lambda j: (0, j))),
        input_output_aliases={4: 0},
        compiler_params=_cparams(("parallel",)),
        name="gdn_prep_bwd",
    )(proj, conv_w, dqkv_f, dqkv_b, dproj)


def _chunk_tri(n, rev):
    i, j = _iota2((n, n), 0), _iota2((n, n), 1)
    same = jnp.right_shift(i, 6) == jnp.right_shift(j, 6)
    order = (j >= i) if rev else (j <= i)
    return jnp.where(jnp.logical_and(same, order), 1.0, 0.0).astype(f32)


def _gdn_gates(proj, avec, dvec, tb=256):
    T = proj.shape[0]

    def body(s_ref, a_ref, d_ref, g_ref, gc_ref):
        small = s_ref[...]
        lane = _iota2(small.shape, 1)
        lg = -jnp.exp(a_ref[...]) * _softplus(small + d_ref[...])
        beta = _sigmoid(small)
        g_ref[...] = jnp.where(lane < 16, lg, jnp.where(lane < 32, beta, 0.0))
        lgm = jnp.where(lane < 16, lg, 0.0)
        gcf = _exact01(_chunk_tri(tb, False), lgm)
        gcr = _exact01(_chunk_tri(tb, True), lgm)
        gc_ref[...] = jnp.where(lane < 8, gcf, gcr)

    blk = pl.BlockSpec((tb, LANES), lambda i: (i, 0))
    row = pl.BlockSpec((1, LANES), lambda i: (0, 0))
    return pl.pallas_call(
        body,
        out_shape=(jax.ShapeDtypeStruct((T, LANES), f32), jax.ShapeDtypeStruct((T, LANES), f32)),
        grid=(T // tb,),
        in_specs=[pl.BlockSpec((tb, LANES), lambda i: (i, SMALL_OFF // LANES)), row, row],
        out_specs=(blk, blk),
        compiler_params=_cparams(("parallel",)),
        name="gdn_gates",
    )(proj, avec, dvec)


def _gdn_gates_bwd(proj, avec, dvec, dg_f, dg_b, dsmall_gla, dproj, tb=256):
    T = proj.shape[0]
    pad = N_CAT - SMALL_OFF

    def body(s_ref, a_ref, d_ref, gf_ref, gb_ref, dl_ref, _, ds_ref, da_ref, dd_ref):
        i = pl.program_id(0)
        small = s_ref[...]
        lane = _iota2(small.shape, 1)
        dgate = gf_ref[...] + gb_ref[...]
        z = small + d_ref[...]
        nega = -jnp.exp(a_ref[...])
        dz = dgate * nega * _sigmoid(z)
        beta = _sigmoid(small)
        dsm = jnp.where(lane < 16, dz, jnp.where(lane < 32, dgate * beta * (1.0 - beta), 0.0))
        ds_ref[:, :LANES] = (dsm + dl_ref[...]).astype(ds_ref.dtype)
        ds_ref[:, LANES:] = jnp.zeros((tb, pad - LANES), ds_ref.dtype)
        lg = nega * _softplus(z)
        pa = jnp.sum(jnp.where(lane < 16, dgate * lg, 0.0), axis=0, keepdims=True)
        pd = jnp.sum(jnp.where(lane < 16, dz, 0.0), axis=0, keepdims=True)

        @pl.when(i == 0)
        def _():
            da_ref[...] = pa
            dd_ref[...] = pd

        @pl.when(i > 0)
        def _():
            da_ref[...] += pa
            dd_ref[...] += pd

    blk = pl.BlockSpec((tb, LANES), lambda i: (i, 0))
    row = pl.BlockSpec((1, LANES), lambda i: (0, 0))
    return pl.pallas_call(
        body,
        out_shape=(jax.ShapeDtypeStruct(dproj.shape, dproj.dtype), jax.ShapeDtypeStruct((1, LANES), f32), jax.ShapeDtypeStruct((1, LANES), f32)),
        grid=(T // tb,),
        in_specs=[pl.BlockSpec((tb, LANES), lambda i: (i, SMALL_OFF // LANES)), row, row, blk, blk, blk, _ANY_SPEC],
        out_specs=(pl.BlockSpec((tb, pad), lambda i: (i, DP_SMALL_BLK)), row, row),
        input_output_aliases={6: 0},
        compiler_params=_cparams(("arbitrary",)),
        name="gdn_gates_bwd",
    )(proj, avec, dvec, dg_f, dg_b, dsmall_gla, dproj)


GDN_HB = 8


def _inv_unit_lower(L, eye):
    A = eye - L
    P2 = _split2(L)
    for _ in range(5):
        P2 = _split2(_h3s(P2, P2, "nn"))
        A = A + _h3s(_split2(A), P2, "nn")
    return A


def _lane_col(blk, idx):
    lane = _iota2(blk.shape, 1)
    return jnp.sum(jnp.where(lane == idx, blk, 0.0), axis=1, keepdims=True)


class _GdnChunk:
    def __init__(self, q, k, v, gcol, beta, rev, A=None):
        Cn = q.shape[0]
        ii, jj = _iota2((Cn, Cn), 0), _iota2((Cn, Cn), 1)
        self.incl = (ii <= jj) if rev else (ii >= jj)
        self.strict = (ii < jj) if rev else (ii > jj)
        grow = jnp.broadcast_to(gcol, (Cn, LANES)).T[0:1, :]
        diff = gcol - grow
        self.decay = jnp.where(self.incl, jnp.exp(jnp.where(self.incl, diff, 0.0)), 0.0)
        self.k, self.v, self.beta, self.gcol = k, v, beta, gcol
        self.qs = q * (GDN_DIM**-0.5)
        self.kb = k * beta
        self.L = jnp.where(self.strict, _mx(self.kb, k, "nt") * self.decay, 0.0)
        if A is None:
            A = _inv_unit_lower(self.L, jnp.where(ii == jj, 1.0, 0.0).astype(f32))
        self.A2 = _split2(A)
        self.A = A
        self.eg = jnp.exp(gcol)
        self.kbg = self.kb * self.eg
        self.sol = _h3s(self.A2, _split2(jnp.concatenate([v * beta, self.kbg], axis=1)), "nn")
        self.u = self.sol[:, :GDN_DIM]
        self.w = self.sol[:, GDN_DIM:]
        self.attn = jnp.where(self.incl, _mx(self.qs, k, "nt") * self.decay, 0.0)
        self.qd = self.qs * self.eg
        last = 0 if rev else Cn - 1
        self.last = last
        gl = gcol[last : last + 1, :]
        self.ekd = jnp.exp(gl - gcol)
        self.kd = k * self.ekd
        self.egl = jnp.exp(gl)


def _gdn_scan_fwd(qkvn, gates, gc):
    T = qkvn.shape[1]
    N = T // CHUNK
    nh, hb = GDN_HEADS, GDN_HB
    W = hb * GDN_DIM
    chains = [(d, hh) for d in range(2) for hh in range(hb)]

    def body(qf_ref, qb_ref, gf_ref, gb_ref, cf_ref, cb_ref, of_ref, ob_ref, sf_ref, sb_ref, af_ref, ab_ref, s_scr):
        n, b = pl.program_id(0), pl.program_id(1)

        @pl.when(n == 0)
        def _():
            for d, hh in chains:
                s_scr[d, b * hb + hh] = jnp.zeros((GDN_DIM, GDN_DIM), f32)

        dirs = ((qf_ref, gf_ref, cf_ref, of_ref, sf_ref, af_ref), (qb_ref, gb_ref, cb_ref, ob_ref, sb_ref, ab_ref))
        s_in = {ch: s_scr[ch[0], b * hb + ch[1]] for ch in chains}
        s_out = {}
        for d, hh in chains:
            q_ref, g_ref, c_ref, o_ref, st_ref, a_ref = dirs[d]
            h = b * hb + hh
            sl = slice(hh * GDN_DIM, (hh + 1) * GDN_DIM)
            gcol = _lane_col(c_ref[...], h + 8 * d)
            beta = _lane_col(g_ref[...], 16 + h + 8 * d)
            ck = _GdnChunk(q_ref[0, :, sl], q_ref[1, :, sl], q_ref[2, :, sl], gcol, beta, d == 1)
            S = s_in[d, hh]
            st_ref[0, hh] = S
            a_ref[0, hh] = ck.A
            v_new = ck.u - _mx(ck.w, S, "nn")
            o_ref[:, sl] = _mx(ck.qd, S, "nn") + _mx(ck.attn, v_new, "nn")
            s_out[d, hh] = S * ck.egl + _mx(ck.kd, v_new, "tn")
        for d, hh in chains:
            s_scr[d, b * hb + hh] = s_out[d, hh]

    fw, bw = (lambda n: n), (lambda n: N - 1 - n)
    qsp = lambda ci: pl.BlockSpec((3, CHUNK, W), lambda n, b: (0, ci(n), b))
    gsp = lambda ci: pl.BlockSpec((CHUNK, LANES), lambda n, b: (ci(n), 0))
    osp = lambda ci: pl.BlockSpec((CHUNK, W), lambda n, b: (ci(n), b))
    ssp = lambda ci: pl.BlockSpec((1, hb, GDN_DIM, GDN_DIM), lambda n, b: (ci(n), b, 0, 0))
    asp = lambda ci: pl.BlockSpec((1, hb, CHUNK, CHUNK), lambda n, b: (ci(n), b, 0, 0))
    o_shape = jax.ShapeDtypeStruct((T, nh * GDN_DIM), f32)
    s_shape = jax.ShapeDtypeStruct((N, nh, GDN_DIM, GDN_DIM), f32)
    a_shape = jax.ShapeDtypeStruct((N, nh, CHUNK, CHUNK), f32)
    return pl.pallas_call(
        body,
        out_shape=(o_shape, o_shape, s_shape, s_shape, a_shape, a_shape),
        grid=(N, nh // hb),
        in_specs=[qsp(fw), qsp(bw), gsp(fw), gsp(bw), gsp(fw), gsp(bw)],
        out_specs=(osp(fw), osp(bw), ssp(fw), ssp(bw), asp(fw), asp(bw)),
        scratch_shapes=[pltpu.VMEM((2, nh, GDN_DIM, GDN_DIM), f32)],
        compiler_params=_cparams(("arbitrary", "arbitrary")),
        name="gdn_scan_fwd",
    )(qkvn, qkvn, gates, gates, gc, gc)


def _gdn_scan_bwd(qkvn, gates, gc, st_f, st_b, a_f, a_b, do):
    T = qkvn.shape[1]
    N = T // CHUNK
    nh, hb = GDN_HEADS, GDN_HB
    nb = nh // hb
    W = hb * GDN_DIM
    chains = [(d, hh) for d in range(2) for hh in range(hb)]

    def body(qf_ref, qb_ref, gf_ref, gb_ref, cf_ref, cb_ref, sf_ref, sb_ref, af_ref, ab_ref, dof_ref, dob_ref,
             dqf_ref, dqb_ref, dgf_ref, dgb_ref, ds_scr):
        n, b = pl.program_id(0), pl.program_id(1)

        @pl.when(n == 0)
        def _():
            for d, hh in chains:
                ds_scr[d, b * hb + hh] = jnp.zeros((GDN_DIM, GDN_DIM), f32)

        @pl.when(b == 0)
        def _():
            dgf_ref[...] = jnp.zeros((CHUNK, LANES), f32)
            dgb_ref[...] = jnp.zeros((CHUNK, LANES), f32)

        dirs = ((qf_ref, gf_ref, cf_ref, sf_ref, af_ref, dof_ref, dqf_ref), (qb_ref, gb_ref, cb_ref, sb_ref, ab_ref, dob_ref, dqb_ref))
        ds_in = {ch: ds_scr[ch[0], b * hb + ch[1]] for ch in chains}
        ds_out = {}
        lane = _iota2((CHUNK, LANES), 1)
        acc = [dgf_ref[...], dgb_ref[...]]
        for d, hh in chains:
            q_ref, g_ref, c_ref, st_ref, a_ref, do_ref, dq_ref = dirs[d]
            h = b * hb + hh
            sl = slice(hh * GDN_DIM, (hh + 1) * GDN_DIM)
            gcol = _lane_col(c_ref[...], h + 8 * d)
            beta = _lane_col(g_ref[...], 16 + h + 8 * d)
            ck = _GdnChunk(q_ref[0, :, sl], q_ref[1, :, sl], q_ref[2, :, sl], gcol, beta, d == 1, A=a_ref[0, hh])
            k, v = ck.k, ck.v
            S = st_ref[0, hh]
            dS = ds_in[d, hh]
            dov = do_ref[:, sl]
            v_new = ck.u - _mx(ck.w, S, "nn")
            d_vnew = _mx(ck.attn, dov, "tn") + _mx(ck.kd, dS, "nn")
            d_attn = jnp.where(ck.incl, _mx(dov, v_new, "nt"), 0.0)
            d_qd = _mx(dov, S, "nt")
            d_kd = _mx(v_new, dS, "nt")
            d_gl = jnp.sum(jnp.sum(dS * S, axis=1, keepdims=True), axis=0, keepdims=True) * ck.egl
            ds_out[d, hh] = dS * ck.egl + _mx(ck.qd, dov, "tn") - _mx(ck.w, d_vnew, "tn")
            d_w = -_mx(d_vnew, S, "nt")
            d_rhs = _h3s(ck.A2, _split2(jnp.concatenate([d_vnew, d_w], axis=1)), "tn")
            d_vb, d_kbg = d_rhs[:, :GDN_DIM], d_rhs[:, GDN_DIM:]
            dL = -jnp.where(ck.strict, _h3(d_rhs, ck.sol, "nt"), 0.0)
            P = dL * ck.decay
            Q = d_attn * ck.decay
            d_kb = _mx(P, k, "nn") + d_kbg * ck.eg
            d_qs = _mx(Q, k, "nn") + d_qd * ck.eg
            d_k = _mx(P, ck.kb, "tn") + _mx(Q, ck.qs, "tn") + d_kd * ck.ekd + d_kb * ck.beta
            E = dL * ck.L + d_attn * ck.attn
            kdsum = jnp.sum(d_kd * ck.kd, axis=1, keepdims=True)
            d_gc = (jnp.sum(E, axis=1, keepdims=True) - _colsum_col(E)
                    + jnp.sum(d_kbg * ck.kbg, axis=1, keepdims=True)
                    + jnp.sum(d_qd * ck.qd, axis=1, keepdims=True) - kdsum)
            d_gl = d_gl + jnp.sum(kdsum, axis=0, keepdims=True)
            row = _iota2((CHUNK, 1), 0)
            d_gc = d_gc + jnp.where(row == ck.last, d_gl, 0.0)
            d_beta = jnp.sum(d_vb * v, axis=1, keepdims=True) + jnp.sum(d_kb * k, axis=1, keepdims=True)
            dq_ref[0, :, sl] = d_qs * (GDN_DIM**-0.5)
            dq_ref[1, :, sl] = d_k
            dq_ref[2, :, sl] = d_vb * ck.beta
            acc[d] = jnp.where(lane == h + 8 * d, d_gc, acc[d])
            acc[d] = jnp.where(lane == 16 + h + 8 * d, d_beta, acc[d])
        for d, hh in chains:
            ds_scr[d, b * hb + hh] = ds_out[d, hh]
        dgf_ref[...] = acc[0]
        dgb_ref[...] = acc[1]

        @pl.when(b == nb - 1)
        def _():
            for d, dg_ref in enumerate((dgf_ref, dgb_ref)):
                blk = dg_ref[...]
                dg_ref[...] = jnp.where(lane < 16, _exact01(_chunk_tri(CHUNK, d == 0), blk), blk)

    fw, bw = (lambda n: N - 1 - n), (lambda n: n)
    qsp = lambda ci: pl.BlockSpec((3, CHUNK, W), lambda n, b: (0, ci(n), b))
    gsp = lambda ci: pl.BlockSpec((CHUNK, LANES), lambda n, b: (ci(n), 0))
    osp = lambda ci: pl.BlockSpec((CHUNK, W), lambda n, b: (ci(n), b))
    ssp = lambda ci: pl.BlockSpec((1, hb, GDN_DIM, GDN_DIM), lambda n, b: (ci(n), b, 0, 0))
    asp = lambda ci: pl.BlockSpec((1, hb, CHUNK, CHUNK), lambda n, b: (ci(n), b, 0, 0))
    dq_shape = jax.ShapeDtypeStruct((3, T, nh * GDN_DIM), f32)
    dg_shape = jax.ShapeDtypeStruct((T, LANES), f32)
    return pl.pallas_call(
        body,
        out_shape=(dq_shape, dq_shape, dg_shape, dg_shape),
        grid=(N, nb),
        in_specs=[qsp(fw), qsp(bw), gsp(fw), gsp(bw), gsp(fw), gsp(bw), ssp(fw), ssp(bw), asp(fw), asp(bw), osp(fw), osp(bw)],
        out_specs=(qsp(fw), qsp(bw), gsp(fw), gsp(bw)),
        scratch_shapes=[pltpu.VMEM((2, nh, GDN_DIM, GDN_DIM), f32)],
        compiler_params=_cparams(("arbitrary", "arbitrary")),
        name="gdn_scan_bwd",
    )(qkvn, qkvn, gates, gates, gc, gc, st_f, st_b, a_f, a_b, do, do)


def _out_norm(o_f, o_b, proj, gate_blk, w, hd, name, tb=256):
    T, W = o_f.shape
    nh = W // hd

    def body(of_ref, ob_ref, z_ref, w_ref, y_ref):
        wv = w_ref[...]
        for i in range(nh):
            sl = slice(i * hd, (i + 1) * hd)
            o = of_ref[:, sl] + ob_ref[:, sl]
            r = lax.rsqrt(jnp.mean(o * o, axis=1, keepdims=True) + NORM_EPS)
            y_ref[:, sl] = (o * r * wv * _silu(z_ref[:, sl])).astype(y_ref.dtype)

    blk = pl.BlockSpec((tb, W), lambda i: (i, 0))
    return pl.pallas_call(
        body,
        out_shape=jax.ShapeDtypeStruct((T, W), MXU_DTYPE),
        grid=(T // tb,),
        in_specs=[blk, blk, pl.BlockSpec((tb, W), lambda i: (i, gate_blk)), pl.BlockSpec((1, hd), lambda i: (0, 0))],
        out_specs=blk,
        compiler_params=_cparams(("parallel",)),
        name=name,
    )(o_f, o_b, proj, w)


def _out_norm_bwd(dy, o_f, o_b, proj, gate_blk, w, hd, name, dproj, tb=256):
    T, W = o_f.shape
    nh = W // hd

    def body(dy_ref, of_ref, ob_ref, z_ref, w_ref, _, do_ref, dz_ref, dw_ref):
        i = pl.program_id(0)
        wv = w_ref[...]
        dw = jnp.zeros((1, hd), f32)
        for a in range(nh):
            sl = slice(a * hd, (a + 1) * hd)
            o = of_ref[:, sl] + ob_ref[:, sl]
            z = z_ref[:, sl]
            dyv = dy_ref[:, sl]
            r = lax.rsqrt(jnp.mean(o * o, axis=1, keepdims=True) + NORM_EPS)
            orr = o * r
            dn = dyv * _silu(z)
            dz_ref[:, sl] = (dyv * orr * wv * _silu_grad(z)).astype(dz_ref.dtype)
            dw = dw + jnp.sum(dn * orr, axis=0, keepdims=True)
            dnw = dn * wv
            do_ref[:, sl] = r * dnw - o * (r * r * r) * jnp.mean(dnw * o, axis=1, keepdims=True)

        @pl.when(i == 0)
        def _():
            dw_ref[...] = dw

        @pl.when(i > 0)
        def _():
            dw_ref[...] += dw

    blk = pl.BlockSpec((tb, W), lambda i: (i, 0))
    row = pl.BlockSpec((1, hd), lambda i: (0, 0))
    return pl.pallas_call(
        body,
        out_shape=(jax.ShapeDtypeStruct((T, W), f32), jax.ShapeDtypeStruct(dproj.shape, dproj.dtype), jax.ShapeDtypeStruct((1, hd), f32)),
        grid=(T // tb,),
        in_specs=[blk, blk, blk, pl.BlockSpec((tb, W), lambda i: (i, gate_blk)), row, _ANY_SPEC],
        out_specs=(blk, pl.BlockSpec((tb, W), lambda i: (i, gate_blk)), row),
        input_output_aliases={5: 1},
        compiler_params=_cparams(("arbitrary",)),
        name=name,
    )(dy, o_f, o_b, proj, w, dproj)


def _gla_prep(proj, w2f, w2b, b2f, b2b, tb=256):
    T = proj.shape[0]
    W = GLA_HEADS * GLA_DK

    def body(s_ref, wf_ref, wb_ref, bf_ref, bb_ref, gf_ref, gb_ref):
        small = s_ref[...]
        gkf = -_softplus(-(_mx(small, wf_ref[...], "nn") + bf_ref[...])) * (1.0 / GATE_NORMALIZER)
        gkb = -_softplus(-(_mx(small, wb_ref[...], "nn") + bb_ref[...])) * (1.0 / GATE_NORMALIZER)
        gf_ref[...] = _exact01(_chunk_tri(tb, False), gkf)
        gb_ref[...] = _exact01(_chunk_tri(tb, True), gkb)

    blk = pl.BlockSpec((tb, W), lambda i: (i, 0))
    wsp = pl.BlockSpec((LANES, W), lambda i: (0, 0))
    row = pl.BlockSpec((1, W), lambda i: (0, 0))
    return pl.pallas_call(
        body,
        out_shape=(jax.ShapeDtypeStruct((T, W), f32), jax.ShapeDtypeStruct((T, W), f32)),
        grid=(T // tb,),
        in_specs=[pl.BlockSpec((tb, LANES), lambda i: (i, SMALL_OFF // LANES)), wsp, wsp, row, row],
        out_specs=(blk, blk),
        compiler_params=_cparams(("parallel",)),
        name="gla_prep",
    )(proj, w2f, w2b, b2f, b2b)


def _gla_prep_bwd(proj, w2f, w2b, b2f, b2b, dgk_f, dgk_b, tb=256):
    T = proj.shape[0]
    W = GLA_HEADS * GLA_DK

    def body(s_ref, wf_ref, wb_ref, bf_ref, bb_ref, df_ref, db_ref, ds_ref, dwf_ref, dwb_ref, dbf_ref, dbb_ref):
        i = pl.program_id(0)
        small = s_ref[...]
        dsm = jnp.zeros((tb, LANES), f32)
        parts = []
        for w_ref, b_ref, d_ref in ((wf_ref, bf_ref, df_ref), (wb_ref, bb_ref, db_ref)):
            pre = _mx(small, w_ref[...], "nn") + b_ref[...]
            dpre = d_ref[...] * (1.0 / GATE_NORMALIZER) * _sigmoid(-pre)
            dsm = dsm + _mx(dpre, w_ref[...], "nt")
            parts.append((_mx(small, dpre, "tn"), jnp.sum(dpre, axis=0, keepdims=True)))
        ds_ref[...] = dsm

        @pl.when(i == 0)
        def _():
            dwf_ref[...], dbf_ref[...] = parts[0]
            dwb_ref[...], dbb_ref[...] = parts[1]

        @pl.when(i > 0)
        def _():
            dwf_ref[...] += parts[0][0]
            dbf_ref[...] += parts[0][1]
            dwb_ref[...] += parts[1][0]
            dbb_ref[...] += parts[1][1]

    blk = pl.BlockSpec((tb, W), lambda i: (i, 0))
    wsp = pl.BlockSpec((LANES, W), lambda i: (0, 0))
    row = pl.BlockSpec((1, W), lambda i: (0, 0))
    return pl.pallas_call(
        body,
        out_shape=(jax.ShapeDtypeStruct((T, LANES), f32), jax.ShapeDtypeStruct((LANES, W), f32), jax.ShapeDtypeStruct((LANES, W), f32),
                   jax.ShapeDtypeStruct((1, W), f32), jax.ShapeDtypeStruct((1, W), f32)),
        grid=(T // tb,),
        in_specs=[pl.BlockSpec((tb, LANES), lambda i: (i, SMALL_OFF // LANES)), wsp, wsp, row, row, blk, blk],
        out_specs=(pl.BlockSpec((tb, LANES), lambda i: (i, 0)), wsp, wsp, row, row),
        compiler_params=_cparams(("arbitrary",)),
        name="gla_prep_bwd",
    )(proj, w2f, w2b, b2f, b2b, dgk_f, dgk_b)


Q_B_BLK, K_B_BLK, V_B_BLK = 4096 // GLA_DK, 4608 // GLA_DK, 5120 // GLA_DV


class _GlaChunk:
    def __init__(self, q, k, G, rev):
        Cn = q.shape[0]
        ii, jj = _iota2((Cn, Cn), 0), _iota2((Cn, Cn), 1)
        self.incl = (ii <= jj) if rev else (ii >= jj)
        self.eG = jnp.exp(G)
        self.enG = jnp.exp(-G)
        self.qg = q * (GLA_DK**-0.5) * self.eG
        self.kg = k * self.enG
        self.attn = jnp.where(self.incl, _mx(self.qg, self.kg, "nt"), 0.0)
        last = 0 if rev else Cn - 1
        self.last = last
        gl = G[last : last + 1, :]
        self.ekd = jnp.exp(gl - G)
        self.kd = k * self.ekd
        self.egl = jnp.exp(gl)


def _gla_scan_fwd(proj, G_f, G_b):
    T = proj.shape[0]
    N = T // CHUNK
    nh = GLA_HEADS

    def body(qf_ref, kf_ref, vf_ref, gf_ref, qb_ref, kb_ref, vb_ref, gb_ref, of_ref, ob_ref, sf_ref, sb_ref, s_scr):
        n, h = pl.program_id(0), pl.program_id(1)

        @pl.when(n == 0)
        def _():
            s_scr[0, h] = jnp.zeros((GLA_DV, GLA_DK), f32)
            s_scr[1, h] = jnp.zeros((GLA_DV, GLA_DK), f32)

        for d, (q_ref, k_ref, v_ref, g_ref, o_ref, st_ref) in enumerate(
            ((qf_ref, kf_ref, vf_ref, gf_ref, of_ref, sf_ref), (qb_ref, kb_ref, vb_ref, gb_ref, ob_ref, sb_ref))
        ):
            ck = _GlaChunk(q_ref[...], k_ref[...], g_ref[...], d == 1)
            v = v_ref[...]
            St = s_scr[d, h]
            st_ref[0, 0] = St
            o_ref[...] = _mx(ck.qg, St, "nt") + _mx(ck.attn, v, "nn")
            s_scr[d, h] = St * ck.egl + _mx(v, ck.kd, "tn")

    def specs(cidx):
        return [
            pl.BlockSpec((CHUNK, GLA_DK), lambda n, h: (cidx(n), Q_B_BLK + h)),
            pl.BlockSpec((CHUNK, GLA_DK), lambda n, h: (cidx(n), K_B_BLK + h)),
            pl.BlockSpec((CHUNK, GLA_DV), lambda n, h: (cidx(n), V_B_BLK + h)),
            pl.BlockSpec((CHUNK, GLA_DK), lambda n, h: (cidx(n), h)),
        ]

    fw, bw = (lambda n: n), (lambda n: N - 1 - n)
    o_shape = jax.ShapeDtypeStruct((T, nh * GLA_DV), f32)
    s_shape = jax.ShapeDtypeStruct((N, nh, GLA_DV, GLA_DK), f32)
    return pl.pallas_call(
        body,
        out_shape=(o_shape, o_shape, s_shape, s_shape),
        grid=(N, nh),
        in_specs=specs(fw) + specs(bw),
        out_specs=(
            pl.BlockSpec((CHUNK, GLA_DV), lambda n, h: (n, h)),
            pl.BlockSpec((CHUNK, GLA_DV), lambda n, h: (N - 1 - n, h)),
            pl.BlockSpec((1, 1, GLA_DV, GLA_DK), lambda n, h: (n, h, 0, 0)),
            pl.BlockSpec((1, 1, GLA_DV, GLA_DK), lambda n, h: (N - 1 - n, h, 0, 0)),
        ),
        scratch_shapes=[pltpu.VMEM((2, nh, GLA_DV, GLA_DK), f32)],
        compiler_params=_cparams(("arbitrary", "arbitrary")),
        name="gla_scan_fwd",
    )(proj, proj, proj, G_f, proj, proj, proj, G_b)


def _gla_scan_bwd(proj, G_f, G_b, st_f, st_b, do):
    T = proj.shape[0]
    N = T // CHUNK
    nh = GLA_HEADS

    def body(qf_ref, kf_ref, vf_ref, gf_ref, sf_ref, dof_ref, qb_ref, kb_ref, vb_ref, gb_ref, sb_ref, dob_ref,
             dqf_ref, dkf_ref, dvf_ref, dgf_ref, dqb_ref, dkb_ref, dvb_ref, dgb_ref, ds_scr):
        n, h = pl.program_id(0), pl.program_id(1)

        @pl.when(n == 0)
        def _():
            ds_scr[0, h] = jnp.zeros((GLA_DV, GLA_DK), f32)
            ds_scr[1, h] = jnp.zeros((GLA_DV, GLA_DK), f32)

        for d, (q_ref, k_ref, v_ref, g_ref, st_ref, do_ref, dq_ref, dk_ref, dv_ref, dg_ref) in enumerate(
            ((qf_ref, kf_ref, vf_ref, gf_ref, sf_ref, dof_ref, dqf_ref, dkf_ref, dvf_ref, dgf_ref),
             (qb_ref, kb_ref, vb_ref, gb_ref, sb_ref, dob_ref, dqb_ref, dkb_ref, dvb_ref, dgb_ref))
        ):
            rev = d == 1
            ck = _GlaChunk(q_ref[...], k_ref[...], g_ref[...], rev)
            v = v_ref[...]
            St = st_ref[0, 0]
            dSt = ds_scr[d, h]
            dov = do_ref[...]
            d_attn = jnp.where(ck.incl, _mx(dov, v, "nt"), 0.0)
            d_qg = _mx(dov, St, "nn") + _mx(d_attn, ck.kg, "nn")
            d_kg = _mx(d_attn, ck.qg, "tn")
            dv_ref[...] = _mx(ck.attn, dov, "tn") + _mx(ck.kd, dSt, "nt")
            d_kd = _mx(v, dSt, "nn")
            ds_scr[d, h] = dSt * ck.egl + _mx(dov, ck.qg, "tn")
            kdd = d_kd * ck.kd
            d_gl = jnp.sum(dSt * St, axis=0, keepdims=True) * ck.egl + jnp.sum(kdd, axis=0, keepdims=True)
            dq_ref[...] = d_qg * ck.eG * (GLA_DK**-0.5)
            dk_ref[...] = d_kg * ck.enG + d_kd * ck.ekd
            d_G = d_qg * ck.qg - d_kg * ck.kg - kdd
            row = _iota2((CHUNK, GLA_DK), 0)
            d_G = d_G + jnp.where(row == ck.last, d_gl, 0.0)
            dg_ref[...] = _exact01(_chunk_tri(CHUNK, not rev), d_G)

    def in_specs(cidx):
        return [
            pl.BlockSpec((CHUNK, GLA_DK), lambda n, h: (cidx(n), Q_B_BLK + h)),
            pl.BlockSpec((CHUNK, GLA_DK), lambda n, h: (cidx(n), K_B_BLK + h)),
            pl.BlockSpec((CHUNK, GLA_DV), lambda n, h: (cidx(n), V_B_BLK + h)),
            pl.BlockSpec((CHUNK, GLA_DK), lambda n, h: (cidx(n), h)),
            pl.BlockSpec((1, 1, GLA_DV, GLA_DK), lambda n, h: (cidx(n), h, 0, 0)),
            pl.BlockSpec((CHUNK, GLA_DV), lambda n, h: (cidx(n), h)),
        ]

    def out_specs(cidx):
        return [
            pl.BlockSpec((CHUNK, GLA_DK), lambda n, h: (cidx(n), h)),
            pl.BlockSpec((CHUNK, GLA_DK), lambda n, h: (cidx(n), h)),
            pl.BlockSpec((CHUNK, GLA_DV), lambda n, h: (cidx(n), h)),
            pl.BlockSpec((CHUNK, GLA_DK), lambda n, h: (cidx(n), h)),
        ]

    fw, bw = (lambda n: N - 1 - n), (lambda n: n)
    k_shape = jax.ShapeDtypeStruct((T, nh * GLA_DK), f32)
    v_shape = jax.ShapeDtypeStruct((T, nh * GLA_DV), f32)
    return pl.pallas_call(
        body,
        out_shape=(k_shape, k_shape, v_shape, k_shape) * 2,
        grid=(N, nh),
        in_specs=in_specs(fw) + in_specs(bw),
        out_specs=tuple(out_specs(fw) + out_specs(bw)),
        scratch_shapes=[pltpu.VMEM((2, nh, GLA_DV, GLA_DK), f32)],
        compiler_params=_cparams(("arbitrary", "arbitrary")),
        name="gla_scan_bwd",
    )(proj, proj, proj, G_f, st_f, do, proj, proj, proj, G_b, st_b, do)


GATE_A_BLK, GATE_B_BLK, Z_BLK, G_B_BLK = 6144 // 1024, 7168 // 1024, 3072 // 1024, 8192 // 1024
DP_GATES_BLK, DP_GLA_QKV_BLK, DP_SMALL_BLK = 6144 // 2048, 4096 // 2048, SMALL_OFF // 512
_ANY_SPEC = pl.BlockSpec(memory_space=pl.ANY)


def _mid(og, obg, proj, x, target, wpg, wpl, wout, w_post, tb=256):
    T = x.shape[0]
    Dm = D_MODEL

    def body(og_ref, obg_ref, ga_ref, gb_ref, x_ref, t_ref, wpg_ref, wpl_ref, wo_ref, wp_ref,
             loss_ref, dwp_ref, dy_ref, m_ref, dout_ref, dya_ref, dyb_ref, dgates_ref, dog_ref, dobg_ref):
        i = pl.program_id(0)
        ya = _mx(og_ref[...], wpg_ref[...], "nn")
        yb = _mx(obg_ref[...], wpl_ref[...], "nn")
        sa, sb = _sigmoid(ga_ref[...]), _sigmoid(gb_ref[...])
        m = sa * ya + sb * yb
        m_ref[...] = m.astype(m_ref.dtype)
        out = _mx(m, wo_ref[...], "nn")
        r = lax.rsqrt(jnp.mean(out * out, axis=1, keepdims=True) + NORM_EPS)
        wp = wp_ref[...]
        e = x_ref[...] + out * r * wp - t_ref[...]
        dy = e * (1.0 / Dm)
        dy_ref[...] = dy
        lpart = jnp.sum(e * e, axis=0, keepdims=True)
        wpart = jnp.sum(dy * out * r, axis=0, keepdims=True)
        dyn = dy * wp
        dout = r * dyn - out * (r * r * r) * jnp.mean(dyn * out, axis=1, keepdims=True)
        dout_ref[...] = dout.astype(dout_ref.dtype)
        dm = _mx(dout, wo_ref[...], "nt")
        dya, dyb = dm * sa, dm * sb
        dya_ref[...] = dya.astype(dya_ref.dtype)
        dyb_ref[...] = dyb.astype(dyb_ref.dtype)
        dgates_ref[:, :Dm] = (dm * ya * sa * (1.0 - sa)).astype(dgates_ref.dtype)
        dgates_ref[:, Dm:] = (dm * yb * sb * (1.0 - sb)).astype(dgates_ref.dtype)
        dog_ref[...] = _mx(dya, wpg_ref[...], "nt")
        dobg_ref[...] = _mx(dyb, wpl_ref[...], "nt")

        @pl.when(i == 0)
        def _():
            loss_ref[...] = lpart
            dwp_ref[...] = wpart

        @pl.when(i > 0)
        def _():
            loss_ref[...] += lpart
            dwp_ref[...] += wpart

    blk = pl.BlockSpec((tb, Dm), lambda i: (i, 0))
    row = pl.BlockSpec((1, Dm), lambda i: (0, 0))
    wsp = pl.BlockSpec((Dm, Dm), lambda i: (0, 0))
    act = jax.ShapeDtypeStruct((T, Dm), MXU_DTYPE)
    big = jax.ShapeDtypeStruct((T, Dm), f32)
    vec = jax.ShapeDtypeStruct((1, Dm), f32)
    return pl.pallas_call(
        body,
        out_shape=(vec, vec, big, act, act, act, act, jax.ShapeDtypeStruct((T, N_CAT), MXU_DTYPE), big, big),
        grid=(T // tb,),
        in_specs=[blk, blk, pl.BlockSpec((tb, Dm), lambda i: (i, GATE_A_BLK)), pl.BlockSpec((tb, Dm), lambda i: (i, GATE_B_BLK)),
                  blk, blk, wsp, wsp, wsp, row],
        out_specs=(row, row) + (blk,) * 5 + (pl.BlockSpec((tb, 2 * Dm), lambda i: (i, DP_GATES_BLK)), blk, blk),
        compiler_params=_cparams(("arbitrary",)),
        name="mid",
    )(og, obg, proj, proj, x, target, wpg, wpl, wout, w_post)


def _heads(ref, lead, n, width):
    return [ref[lead + (slice(None), slice(i * width, (i + 1) * width))] for i in range(n)]


class _GdnBatch:
    def __init__(self, q, k, v, gcol, grow, beta, nf, A=None):
        B, Cn = q.shape[0], q.shape[1]
        self.nf = nf
        self.incl, self.strict, eye = _dir_masks(B, nf, Cn)
        self.decay = jnp.where(self.incl, jnp.exp(jnp.where(self.incl, gcol - grow, 0.0)), 0.0)
        self.k, self.v, self.beta = k, v, beta
        self.qs = q * (GDN_DIM**-0.5)
        self.kb = k * beta
        self.L = jnp.where(self.strict, _mx(self.kb, k, "nt") * self.decay, 0.0)
        if A is None:
            A = _inv_unit_lower(self.L, jnp.where(eye, 1.0, 0.0).astype(f32))
        self.A = A
        self.A2 = _split2(A)
        self.eg = jnp.exp(gcol)
        self.kbg = self.kb * self.eg
        self.sol = _h3s(self.A2, _split2(jnp.concatenate([v * beta, self.kbg], axis=2)), "nn")
        self.u = self.sol[:, :, :GDN_DIM]
        self.w = self.sol[:, :, GDN_DIM:]
        self.attn = jnp.where(self.incl, _mx(self.qs, k, "nt") * self.decay, 0.0)
        self.qd = self.qs * self.eg
        gl = _last_row(gcol, nf)
        self.ekd = jnp.exp(gl - gcol)
        self.kd = k * self.ekd
        self.egl = jnp.exp(gl)


def _gdn_chain_inputs(qf_ref, qb_ref, gf_ref, gb_ref, cf_ref, cb_ref, b, hb):
    qkv = [jnp.stack(_heads(qf_ref, (i,), hb, GDN_DIM) + _heads(qb_ref, (i,), hb, GDN_DIM)) for i in range(3)]
    cols, rows, betas = [], [], []
    for d, (g_ref, c_ref) in enumerate(((gf_ref, cf_ref), (gb_ref, cb_ref))):
        gates, gc = g_ref[...], c_ref[...]
        for hh in range(hb):
            h = b * hb + hh
            col = _lane_col(gc, h + 8 * d)
            cols.append(col)
            rows.append(jnp.broadcast_to(col, (CHUNK, LANES)).T[0:1, :])
            betas.append(_lane_col(gates, 16 + h + 8 * d))
    return qkv[0], qkv[1], qkv[2], jnp.stack(cols), jnp.stack(rows), jnp.stack(betas)


def _gdn_fwd(qkvn, gates, gc):
    T = qkvn.shape[1]
    N = T // CHUNK
    nh, hb = GDN_HEADS, GDN_HB
    W = hb * GDN_DIM

    def body(qf_ref, qb_ref, gf_ref, gb_ref, cf_ref, cb_ref, of_ref, ob_ref, sf_ref, sb_ref, af_ref, ab_ref, s_scr):
        n, b = pl.program_id(0), pl.program_id(1)
        hs = pl.ds(b * hb, hb)

        @pl.when(n == 0)
        def _():
            s_scr[0, hs] = jnp.zeros((hb, GDN_DIM, GDN_DIM), f32)
            s_scr[1, hs] = jnp.zeros((hb, GDN_DIM, GDN_DIM), f32)

        q, k, v, gcol, grow, beta = _gdn_chain_inputs(qf_ref, qb_ref, gf_ref, gb_ref, cf_ref, cb_ref, b, hb)
        ck = _GdnBatch(q, k, v, gcol, grow, beta, hb)
        S = jnp.concatenate([s_scr[0, hs], s_scr[1, hs]], axis=0)
        sf_ref[0], sb_ref[0] = S[:hb], S[hb:]
        af_ref[0], ab_ref[0] = ck.A[:hb], ck.A[hb:]
        v_new = ck.u - _mx(ck.w, S, "nn")
        o = _mx(ck.qd, S, "nn") + _mx(ck.attn, v_new, "nn")
        S_new = S * ck.egl + _mx(ck.kd, v_new, "tn")
        for hh in range(hb):
            sl = slice(hh * GDN_DIM, (hh + 1) * GDN_DIM)
            of_ref[:, sl] = o[hh]
            ob_ref[:, sl] = o[hb + hh]
        s_scr[0, hs] = S_new[:hb]
        s_scr[1, hs] = S_new[hb:]

    fw, bw = (lambda n: n), (lambda n: N - 1 - n)
    qsp = lambda ci: pl.BlockSpec((3, CHUNK, W), lambda n, b: (0, ci(n), b))
    gsp = lambda ci: pl.BlockSpec((CHUNK, LANES), lambda n, b: (ci(n), 0))
    osp = lambda ci: pl.BlockSpec((CHUNK, W), lambda n, b: (ci(n), b))
    ssp = lambda ci: pl.BlockSpec((1, hb, GDN_DIM, GDN_DIM), lambda n, b: (ci(n), b, 0, 0))
    asp = lambda ci: pl.BlockSpec((1, hb, CHUNK, CHUNK), lambda n, b: (ci(n), b, 0, 0))
    o_shape = jax.ShapeDtypeStruct((T, nh * GDN_DIM), f32)
    s_shape = jax.ShapeDtypeStruct((N, nh, GDN_DIM, GDN_DIM), f32)
    a_shape = jax.ShapeDtypeStruct((N, nh, CHUNK, CHUNK), f32)
    return pl.pallas_call(
        body,
        out_shape=(o_shape, o_shape, s_shape, s_shape, a_shape, a_shape),
        grid=(N, nh // hb),
        in_specs=[qsp(fw), qsp(bw), gsp(fw), gsp(bw), gsp(fw), gsp(bw)],
        out_specs=(osp(fw), osp(bw), ssp(fw), ssp(bw), asp(fw), asp(bw)),
        scratch_shapes=[pltpu.VMEM((2, nh, GDN_DIM, GDN_DIM), f32)],
        compiler_params=_cparams(("arbitrary", "arbitrary")),
        name="gdn_scan_fwd",
    )(qkvn, qkvn, gates, gates, gc, gc)


def _gdn_bwd(qkvn, gates, gc, st_f, st_b, a_f, a_b, do):
    T = qkvn.shape[1]
    N = T // CHUNK
    nh, hb = GDN_HEADS, GDN_HB
    nb = nh // hb
    W = hb * GDN_DIM

    def body(qf_ref, qb_ref, gf_ref, gb_ref, cf_ref, cb_ref, sf_ref, sb_ref, af_ref, ab_ref, dof_ref, dob_ref,
             dqf_ref, dqb_ref, dgf_ref, dgb_ref, ds_scr):
        n, b = pl.program_id(0), pl.program_id(1)
        hs = pl.ds(b * hb, hb)

        @pl.when(n == 0)
        def _():
            ds_scr[0, hs] = jnp.zeros((hb, GDN_DIM, GDN_DIM), f32)
            ds_scr[1, hs] = jnp.zeros((hb, GDN_DIM, GDN_DIM), f32)

        @pl.when(b == 0)
        def _():
            dgf_ref[...] = jnp.zeros((CHUNK, LANES), f32)
            dgb_ref[...] = jnp.zeros((CHUNK, LANES), f32)

        q, k, v, gcol, grow, beta = _gdn_chain_inputs(qf_ref, qb_ref, gf_ref, gb_ref, cf_ref, cb_ref, b, hb)
        ck = _GdnBatch(q, k, v, gcol, grow, beta, hb, A=jnp.concatenate([af_ref[0], ab_ref[0]], axis=0))
        S = jnp.concatenate([sf_ref[0], sb_ref[0]], axis=0)
        dS = jnp.concatenate([ds_scr[0, hs], ds_scr[1, hs]], axis=0)
        dov = jnp.stack(_heads(dof_ref, (), hb, GDN_DIM) + _heads(dob_ref, (), hb, GDN_DIM))
        v_new = ck.u - _mx(ck.w, S, "nn")
        d_vnew = _mx(ck.attn, dov, "tn") + _mx(ck.kd, dS, "nn")
        d_attn = jnp.where(ck.incl, _mx(dov, v_new, "nt"), 0.0)
        d_qd = _mx(dov, S, "nt")
        d_kd = _mx(v_new, dS, "nt")
        d_gl = jnp.sum(jnp.sum(dS * S, axis=2, keepdims=True), axis=1, keepdims=True) * ck.egl
        dS_new = dS * ck.egl + _mx(ck.qd, dov, "tn") - _mx(ck.w, d_vnew, "tn")
        ds_scr[0, hs] = dS_new[:hb]
        ds_scr[1, hs] = dS_new[hb:]
        d_w = -_mx(d_vnew, S, "nt")
        d_rhs = _h3s(ck.A2, _split2(jnp.concatenate([d_vnew, d_w], axis=2)), "tn")
        d_vb, d_kbg = d_rhs[:, :, :GDN_DIM], d_rhs[:, :, GDN_DIM:]
        dL = -jnp.where(ck.strict, _h3(d_rhs, ck.sol, "nt"), 0.0)
        P = dL * ck.decay
        Q = d_attn * ck.decay
        d_kb = _mx(P, k, "nn") + d_kbg * ck.eg
        d_qs = _mx(Q, k, "nn") + d_qd * ck.eg
        d_k = _mx(P, ck.kb, "tn") + _mx(Q, ck.qs, "tn") + d_kd * ck.ekd + d_kb * ck.beta
        E = dL * ck.L + d_attn * ck.attn
        kdsum = jnp.sum(d_kd * ck.kd, axis=2, keepdims=True)
        d_gc = (jnp.sum(E, axis=2, keepdims=True) - _colsum_col(E)
                + jnp.sum(d_kbg * ck.kbg, axis=2, keepdims=True)
                + jnp.sum(d_qd * ck.qd, axis=2, keepdims=True) - kdsum)
        d_gc = _add_at_last_row(d_gc, d_gl + jnp.sum(kdsum, axis=1, keepdims=True), hb)
        d_beta = jnp.sum(d_vb * v, axis=2, keepdims=True) + jnp.sum(d_kb * k, axis=2, keepdims=True)
        d_q = d_qs * (GDN_DIM**-0.5)
        d_v = d_vb * ck.beta
        lane = _iota2((CHUNK, LANES), 1)
        for d, (dq_ref, dg_ref) in enumerate(((dqf_ref, dgf_ref), (dqb_ref, dgb_ref))):
            acc = dg_ref[...]
            for hh in range(hb):
                c, h = d * hb + hh, b * hb + hh
                sl = slice(hh * GDN_DIM, (hh + 1) * GDN_DIM)
                dq_ref[0, :, sl] = d_q[c]
                dq_ref[1, :, sl] = d_k[c]
                dq_ref[2, :, sl] = d_v[c]
                acc = jnp.where(lane == h + 8 * d, d_gc[c], acc)
                acc = jnp.where(lane == 16 + h + 8 * d, d_beta[c], acc)
            dg_ref[...] = acc

        @pl.when(b == nb - 1)
        def _():
            for d, dg_ref in enumerate((dgf_ref, dgb_ref)):
                blk = dg_ref[...]
                dg_ref[...] = jnp.where(lane < 16, _exact01(_chunk_tri(CHUNK, d == 0), blk), blk)

    fw, bw = (lambda n: N - 1 - n), (lambda n: n)
    qsp = lambda ci: pl.BlockSpec((3, CHUNK, W), lambda n, b: (0, ci(n), b))
    gsp = lambda ci: pl.BlockSpec((CHUNK, LANES), lambda n, b: (ci(n), 0))
    osp = lambda ci: pl.BlockSpec((CHUNK, W), lambda n, b: (ci(n), b))
    ssp = lambda ci: pl.BlockSpec((1, hb, GDN_DIM, GDN_DIM), lambda n, b: (ci(n), b, 0, 0))
    asp = lambda ci: pl.BlockSpec((1, hb, CHUNK, CHUNK), lambda n, b: (ci(n), b, 0, 0))
    dq_shape = jax.ShapeDtypeStruct((3, T, nh * GDN_DIM), f32)
    dg_shape = jax.ShapeDtypeStruct((T, LANES), f32)
    return pl.pallas_call(
        body,
        out_shape=(dq_shape, dq_shape, dg_shape, dg_shape),
        grid=(N, nb),
        in_specs=[qsp(fw), qsp(bw), gsp(fw), gsp(bw), gsp(fw), gsp(bw), ssp(fw), ssp(bw), asp(fw), asp(bw), osp(fw), osp(bw)],
        out_specs=(qsp(fw), qsp(bw), gsp(fw), gsp(bw)),
        scratch_shapes=[pltpu.VMEM((2, nh, GDN_DIM, GDN_DIM), f32)],
        compiler_params=_cparams(("arbitrary", "arbitrary")),
        name="gdn_scan_bwd",
    )(qkvn, qkvn, gates, gates, gc, gc, st_f, st_b, a_f, a_b, do, do)


class _GlaBatch:
    def __init__(self, q, k, G, nf):
        B, Cn = q.shape[0], q.shape[1]
        self.incl, _, _ = _dir_masks(B, nf, Cn)
        self.eG = jnp.exp(G)
        self.enG = jnp.exp(-G)
        self.qg = q * (GLA_DK**-0.5) * self.eG
        self.kg = k * self.enG
        self.attn = jnp.where(self.incl, _mx(self.qg, self.kg, "nt"), 0.0)
        gl = _last_row(G, nf)
        self.ekd = jnp.exp(gl - G)
        self.kd = k * self.ekd
        self.egl = jnp.exp(gl)


def _gla_chain_inputs(refs_f, refs_b):
    nh = GLA_HEADS
    out = []
    for i, width in enumerate((GLA_DK, GLA_DK, GLA_DV, GLA_DK)):
        out.append(jnp.stack(_heads(refs_f[i], (), nh, width) + _heads(refs_b[i], (), nh, width)))
    return out


def _gla_specs(N, ci):
    nh = GLA_HEADS
    return [
        pl.BlockSpec((CHUNK, nh * GLA_DK), lambda n: (ci(n), 4096 // (nh * GLA_DK))),
        pl.BlockSpec((CHUNK, nh * GLA_DK), lambda n: (ci(n), 4608 // (nh * GLA_DK))),
        pl.BlockSpec((CHUNK, nh * GLA_DV), lambda n: (ci(n), 5120 // (nh * GLA_DV))),
        pl.BlockSpec((CHUNK, nh * GLA_DK), lambda n: (ci(n), 0)),
    ]


def _gla_fwd(proj, G_f, G_b):
    T = proj.shape[0]
    N = T // CHUNK
    nh = GLA_HEADS

    def body(qf, kf, vf, gf, qb, kb, vb, gb, of_ref, ob_ref, sf_ref, sb_ref, s_scr):
        n = pl.program_id(0)

        @pl.when(n == 0)
        def _():
            s_scr[...] = jnp.zeros((2 * nh, GLA_DV, GLA_DK), f32)

        q, k, v, G = _gla_chain_inputs((qf, kf, vf, gf), (qb, kb, vb, gb))
        ck = _GlaBatch(q, k, G, nh)
        St = s_scr[...]
        sf_ref[0], sb_ref[0] = St[:nh], St[nh:]
        o = _mx(ck.qg, St, "nt") + _mx(ck.attn, v, "nn")
        for i in range(nh):
            sl = slice(i * GLA_DV, (i + 1) * GLA_DV)
            of_ref[:, sl] = o[i]
            ob_ref[:, sl] = o[nh + i]
        s_scr[...] = St * ck.egl + _mx(v, ck.kd, "tn")

    fw, bw = (lambda n: n), (lambda n: N - 1 - n)
    osp = lambda ci: pl.BlockSpec((CHUNK, nh * GLA_DV), lambda n: (ci(n), 0))
    ssp = lambda ci: pl.BlockSpec((1, nh, GLA_DV, GLA_DK), lambda n: (ci(n), 0, 0, 0))
    o_shape = jax.ShapeDtypeStruct((T, nh * GLA_DV), f32)
    s_shape = jax.ShapeDtypeStruct((N, nh, GLA_DV, GLA_DK), f32)
    return pl.pallas_call(
        body,
        out_shape=(o_shape, o_shape, s_shape, s_shape),
        grid=(N,),
        in_specs=_gla_specs(N, fw) + _gla_specs(N, bw),
        out_specs=(osp(fw), osp(bw), ssp(fw), ssp(bw)),
        scratch_shapes=[pltpu.VMEM((2 * nh, GLA_DV, GLA_DK), f32)],
        compiler_params=_cparams(("arbitrary",)),
        name="gla_scan_fwd",
    )(proj, proj, proj, G_f, proj, proj, proj, G_b)


def _gla_bwd(proj, G_f, G_b, st_f, st_b, do):
    T = proj.shape[0]
    N = T // CHUNK
    nh = GLA_HEADS

    def body(qf, kf, vf, gf, sf_ref, dof_ref, qb, kb, vb, gb, sb_ref, dob_ref,
             dqf_ref, dkf_ref, dvf_ref, dgf_ref, dqb_ref, dkb_ref, dvb_ref, dgb_ref, ds_scr):
        n = pl.program_id(0)

        @pl.when(n == 0)
        def _():
            ds_scr[...] = jnp.zeros((2 * nh, GLA_DV, GLA_DK), f32)

        q, k, v, G = _gla_chain_inputs((qf, kf, vf, gf), (qb, kb, vb, gb))
        ck = _GlaBatch(q, k, G, nh)
        St = jnp.concatenate([sf_ref[0], sb_ref[0]], axis=0)
        dSt = ds_scr[...]
        dov = jnp.stack(_heads(dof_ref, (), nh, GLA_DV) + _heads(dob_ref, (), nh, GLA_DV))
        d_attn = jnp.where(ck.incl, _mx(dov, v, "nt"), 0.0)
        d_qg = _mx(dov, St, "nn") + _mx(d_attn, ck.kg, "nn")
        d_kg = _mx(d_attn, ck.qg, "tn")
        d_v = _mx(ck.attn, dov, "tn") + _mx(ck.kd, dSt, "nt")
        d_kd = _mx(v, dSt, "nn")
        ds_scr[...] = dSt * ck.egl + _mx(dov, ck.qg, "tn")
        kdd = d_kd * ck.kd
        d_gl = jnp.sum(dSt * St, axis=1, keepdims=True) * ck.egl + jnp.sum(kdd, axis=1, keepdims=True)
        d_q = d_qg * ck.eG * (GLA_DK**-0.5)
        d_k = d_kg * ck.enG + d_kd * ck.ekd
        d_G = _add_at_last_row(d_qg * ck.qg - d_kg * ck.kg - kdd, d_gl, nh)
        shp = (2 * nh, CHUNK, CHUNK)
        bb, ii, jj = _iota3(shp, 0), _iota3(shp, 1), _iota3(shp, 2)
        tri = jnp.logical_or(jnp.logical_and(bb < nh, jj >= ii), jnp.logical_and(bb >= nh, jj <= ii))
        d_gk = _exact01(jnp.where(tri, 1.0, 0.0), d_G)
        for d, (dq_ref, dk_ref, dv_ref, dg_ref) in enumerate(((dqf_ref, dkf_ref, dvf_ref, dgf_ref), (dqb_ref, dkb_ref, dvb_ref, dgb_ref))):
            for i in range(nh):
                c = d * nh + i
                ks, vs = slice(i * GLA_DK, (i + 1) * GLA_DK), slice(i * GLA_DV, (i + 1) * GLA_DV)
                dq_ref[:, ks] = d_q[c]
                dk_ref[:, ks] = d_k[c]
                dv_ref[:, vs] = d_v[c]
                dg_ref[:, ks] = d_gk[c]

    def in_specs(ci):
        return _gla_specs(N, ci) + [
            pl.BlockSpec((1, nh, GLA_DV, GLA_DK), lambda n: (ci(n), 0, 0, 0)),
            pl.BlockSpec((CHUNK, nh * GLA_DV), lambda n: (ci(n), 0)),
        ]

    def out_specs(ci):
        ksp = pl.BlockSpec((CHUNK, nh * GLA_DK), lambda n: (ci(n), 0))
        return [ksp, ksp, pl.BlockSpec((CHUNK, nh * GLA_DV), lambda n: (ci(n), 0)), ksp]

    fw, bw = (lambda n: N - 1 - n), (lambda n: n)
    k_shape = jax.ShapeDtypeStruct((T, nh * GLA_DK), f32)
    v_shape = jax.ShapeDtypeStruct((T, nh * GLA_DV), f32)
    return pl.pallas_call(
        body,
        out_shape=(k_shape, k_shape, v_shape, k_shape) * 2,
        grid=(N,),
        in_specs=in_specs(fw) + in_specs(bw),
        out_specs=tuple(out_specs(fw) + out_specs(bw)),
        scratch_shapes=[pltpu.VMEM((2 * nh, GLA_DV, GLA_DK), f32)],
        compiler_params=_cparams(("arbitrary",)),
        name="gla_scan_bwd",
    )(proj, proj, proj, G_f, st_f, do, proj, proj, proj, G_b, st_b, do)


def _gla_dsum(dq_f, dq_b, dk_f, dk_b, dv_f, dv_b, dproj, tb=256):
    T = dq_f.shape[0]
    wk, wv = GLA_HEADS * GLA_DK, GLA_HEADS * GLA_DV

    def body(qf, qb, kf, kb, vf, vb, _, o_ref):
        o_ref[:, :wk] = (qf[...] + qb[...]).astype(o_ref.dtype)
        o_ref[:, wk : 2 * wk] = (kf[...] + kb[...]).astype(o_ref.dtype)
        o_ref[:, 2 * wk :] = (vf[...] + vb[...]).astype(o_ref.dtype)

    ksp = pl.BlockSpec((tb, wk), lambda i: (i, 0))
    vsp = pl.BlockSpec((tb, wv), lambda i: (i, 0))
    return pl.pallas_call(
        body,
        out_shape=jax.ShapeDtypeStruct(dproj.shape, dproj.dtype),
        grid=(T // tb,),
        in_specs=[ksp, ksp, ksp, ksp, vsp, vsp, _ANY_SPEC],
        out_specs=pl.BlockSpec((tb, 2 * wk + wv), lambda i: (i, DP_GLA_QKV_BLK)),
        input_output_aliases={6: 0},
        compiler_params=_cparams(("parallel",)),
        name="gla_dsum",
    )(dq_f, dq_b, dk_f, dk_b, dv_f, dv_b, dproj)


def _local_step(x, target, wcat, conv_w, avec, dvec, gdn_norm_w, w2f, w2b, b2f, b2b, gla_norm_w, proj_weights, w_pre, w_post,
                send_proj_grads=None, send_in_grads=None):
    h, h_t = _rms_pre(x, w_pre)
    proj = _matmul(h, wcat, "nt", f32, "proj_in", 2048, 512, 1024)
    qkvn = _gdn_prep(proj, conv_w)
    gates, gc = _gdn_gates(proj, avec, dvec)
    oa_f, oa_b, sa_f, sa_b, inv_f, inv_b = _gdn_fwd(qkvn, gates, gc)
    og = _out_norm(oa_f, oa_b, proj, Z_BLK, gdn_norm_w, GDN_DIM, "gdn_out")
    G_f, G_b = _gla_prep(proj, w2f, w2b, b2f, b2b)
    ob_f, ob_b, sb_f, sb_b = _gla_fwd(proj, G_f, G_b)
    obg = _out_norm(ob_f, ob_b, proj, G_B_BLK, gla_norm_w, GLA_DV, "gla_out")
    wpg, wpl, wout = proj_weights(obg)
    (loss_row, d_wpost, dy, m, dout, dya, dyb, dproj, dog, dobg) = _mid(og, obg, proj, x, target, wpg, wpl, wout, w_post)

    d_wout = _matmul(m, dout, "tn", MXU_DTYPE, "dw_out", 512, 512, 2048)
    d_wpg = _matmul(og, dya, "tn", MXU_DTYPE, "dw_proj_gdn", 512, 512, 2048)
    d_wpl = _matmul(obg, dyb, "tn", MXU_DTYPE, "dw_proj_gla", 512, 512, 2048)
    if send_proj_grads is not None:
        gla_norm_w = gla_norm_w + send_proj_grads(d_wpg, d_wpl, d_wout)

    do_b, dproj, d_gla_norm = _out_norm_bwd(dobg, ob_f, ob_b, proj, G_B_BLK, gla_norm_w, GLA_DV, "gla_out_bwd", dproj)
    dq_f, dk_f, dv_f, dgk_f, dq_b, dk_b, dv_b, dgk_b = _gla_bwd(proj, G_f, G_b, sb_f, sb_b, do_b)
    dproj = _gla_dsum(dq_f, dq_b, dk_f, dk_b, dv_f, dv_b, dproj)
    dsmall_gla, d_w2f, d_w2b, d_b2f, d_b2b = _gla_prep_bwd(proj, w2f, w2b, b2f, b2b, dgk_f, dgk_b)

    do_a, dproj, d_gdn_norm = _out_norm_bwd(dog, oa_f, oa_b, proj, Z_BLK, gdn_norm_w, GDN_DIM, "gdn_out_bwd", dproj)
    dqkv_f, dqkv_b, dg_f, dg_b = _gdn_bwd(qkvn, gates, gc, sa_f, sa_b, inv_f, inv_b, do_a)
    dproj, d_conv = _gdn_prep_bwd(proj, conv_w, dqkv_f, dqkv_b, dproj)
    dproj, d_alog, d_dtb = _gdn_gates_bwd(proj, avec, dvec, dg_f, dg_b, dsmall_gla, dproj)

    d_wcat = _matmul(h_t, dproj, "nn", MXU_DTYPE, "dw_in", 1024, 512, 2048, out_t=True)
    started = None if send_in_grads is None else send_in_grads(d_wcat)
    dh = _matmul(dproj, wcat, "nn", f32, "dh", 1024, 1024, 2432, after=started)
    grad_x, d_wpre = _rms_pre_bwd(x, w_pre, dh, dy)
    return dict(loss_row=loss_row, grad_x=grad_x, d_wcat=d_wcat, d_conv=d_conv, d_alog=d_alog, d_dtb=d_dtb,
                d_gdn_norm=d_gdn_norm, d_wpg=d_wpg, d_w2f=d_w2f, d_w2b=d_w2b, d_b2f=d_b2f, d_b2b=d_b2b,
                d_gla_norm=d_gla_norm, d_wpl=d_wpl, d_wout=d_wout, d_wpost=d_wpost, d_wpre=d_wpre)


def _to_cat(wf):
    pad = jnp.zeros((N_CAT - N_IN,) + wf.shape[1:], wf.dtype)
    return jnp.concatenate([wf[:4096], wf[4128:6176], wf[7232:9280], wf[6176:7200], wf[4096:4128], wf[7200:7232], pad], axis=0)


def _from_cat(wc):
    return jnp.concatenate([wc[:4096], wc[9216:9248], wc[4096:6144], wc[8192:9216], wc[9248:9280], wc[6144:8192]], axis=0)


def _lane_row(*pieces):
    row = jnp.concatenate(pieces, axis=1)
    return jnp.pad(row, ((0, 0), (0, LANES - row.shape[1])))


def _pad_w2(w2, row0):
    return jnp.pad(w2, ((row0, LANES - row0 - w2.shape[0]), (0, 0)))


SM_ROWS, SM_COLS = 32, 384
REP_ROWS = 32


def _pack_small_shard(conv, w2f, w2b):
    lead = conv.shape[:-2]
    z = lambda r, c: jnp.zeros(lead + (r, c), f32)
    top = jnp.concatenate([conv, z(8 - CONV_K, SM_COLS)], axis=-2)
    mid = jnp.concatenate([w2f, w2b, z(16, SM_COLS - 128)], axis=-1)
    return jnp.concatenate([top, mid, z(SM_ROWS - 24, SM_COLS)], axis=-2)


LOSS_ROW = 28


def _pack_rep(ln_pre, ln_post, b2f, b2b, gla_norm, gdn_norm, alog_f, alog_b, dt_f, dt_b, loss=None):
    last = _lane_row(alog_f, alog_b, dt_f, dt_b)
    extra = jnp.zeros((1, LANES), f32) if loss is None else _lane_row(loss)
    rows = [ln_pre.reshape(8, LANES), ln_post.reshape(8, LANES), b2f.reshape(4, LANES), b2b.reshape(4, LANES),
            gla_norm.reshape(2, LANES), gdn_norm.reshape(1, LANES), last, extra, jnp.zeros((REP_ROWS - LOSS_ROW - 1, LANES), f32)]
    return jnp.concatenate(rows, axis=0)


def _unpack_rep(p):
    return dict(ln_pre_w=p[0:8].reshape(1, 1024), ln_post_w=p[8:16].reshape(1, 1024), gk_b2_fwd=p[16:20].reshape(1, 512),
                gk_b2_bwd=p[20:24].reshape(1, 512), gla_norm_w=p[24:26].reshape(1, 256), gdn_norm_w=p[26:27],
                a_log_fwd=p[27:28, 0:8], a_log_bwd=p[27:28, 8:16], dt_bias_fwd=p[27:28, 16:24], dt_bias_bwd=p[27:28, 24:32])


_MESH = pl.DeviceIdType.MESH


N_GATHERED = 2


def _gather_weights(w_in_s, sm_s):
    shapes = (w_in_s.shape, sm_s.shape)
    dtypes = (MXU_DTYPE, f32)

    def body(win_ref, sm_ref, gin_ref, gsm_ref, send_sems, recv_sems):
        x, y, c = lax.axis_index("x"), lax.axis_index("y"), lax.axis_index("c")
        me, sibling = (x, y, c), (x, y, 1 - c)
        chips = [(1 - x, y), (x, 1 - y), (1 - x, 1 - y)]
        bufs = (gin_ref, gsm_ref)

        def idx(px, py, pc):
            return 4 * px + 2 * py + pc

        for r in range(w_in_s.shape[1] // LANES):
            cols = slice(r * LANES, (r + 1) * LANES)
            gin_ref[idx(*me), :, cols] = win_ref[:, cols].astype(MXU_DTYPE)
        gsm_ref[idx(*me)] = sm_ref[...]

        def copy(a, k, block, to):
            blk = bufs[a].at[idx(*block)]
            return pltpu.make_async_remote_copy(src_ref=blk, dst_ref=blk, send_sem=send_sems.at[7 * a + k],
                                                recv_sem=recv_sems.at[7 * a + k], device_id=to, device_id_type=_MESH)

        first, passed = [], []
        for a in range(N_GATHERED):
            first.append(copy(a, 0, me, sibling))
            first += [copy(a, 1 + j, me, (*chip, c)) for j, chip in enumerate(chips)]
        for cp in first:
            cp.start()
        for j, chip in enumerate(chips):
            for a in range(N_GATHERED):
                copy(a, 1 + j, (*chip, c), me).wait_recv()
                cp = copy(a, 4 + j, (*chip, c), sibling)
                cp.start()
                passed.append(cp)
        for a in range(N_GATHERED):
            copy(a, 0, sibling, me).wait_recv()
            for j, chip in enumerate(chips):
                copy(a, 4 + j, (*chip, 1 - c), me).wait_recv()
        for cp in first + passed:
            cp.wait_send()

    vm = pl.BlockSpec(memory_space=pltpu.VMEM)
    return pl.pallas_call(
        body,
        out_shape=tuple(jax.ShapeDtypeStruct((N_DEV,) + s, d) for s, d in zip(shapes, dtypes)),
        in_specs=[vm, vm],
        out_specs=(vm, vm),
        scratch_shapes=[pltpu.SemaphoreType.DMA((7 * N_GATHERED,)), pltpu.SemaphoreType.DMA((7 * N_GATHERED,))],
        compiler_params=pltpu.CompilerParams(vmem_limit_bytes=VMEM_LIMIT),
        name="gather_weights",
    )(w_in_s, sm_s)


N_CHIP = 4
_EFFECT = pltpu.SideEffectType.DATAFLOW_SIDE_EFFECTING
_HBM_SPEC = pl.BlockSpec(memory_space=pltpu.HBM)
_SEM_SPEC = pl.BlockSpec(memory_space=pltpu.SEMAPHORE)


def _relation_peer(k, x, y, c):
    px = (1 - x) if (k & 4) else x
    py = (1 - y) if (k & 2) else y
    pc = (1 - c) if (k & 1) else c
    return (px, py, pc), 4 * px + 2 * py + pc


_PLAN_COPIES = {"gather": N_DEV - 1, "blocks": N_DEV - 1, "sibling": 4, "chips": 3}
_PLAN_SLOTS = {"gather": N_DEV, "blocks": N_DEV, "sibling": 4, "chips": 4}


def _plan_copies(plan, src_ref, land_ref, send_sems, recv_sems, arrival):
    x, y, c = lax.axis_index("x"), lax.axis_index("y"), lax.axis_index("c")
    me, my_chip = 4 * x + 2 * y + c, 2 * x + y
    out = []

    def add(i, src, there, here, dev):
        out.append(pltpu.make_async_remote_copy(src_ref=src, dst_ref=land_ref.at[here if arrival else there], send_sem=send_sems.at[i],
                                                recv_sem=recv_sems.at[i], device_id=dev, device_id_type=_MESH))

    if plan in ("gather", "blocks"):
        for k in range(1, N_DEV):
            dev, p = _relation_peer(k, x, y, c)
            add(k - 1, src_ref.at[p] if plan == "blocks" else src_ref, me, p, dev)
    elif plan == "sibling":
        for j in range(4):
            add(j, src_ref.at[2 * j + 1 - c], j, j, (x, y, 1 - c))
    else:
        for i, k in enumerate((2, 4, 6)):
            (px, py, pc), _ = _relation_peer(k, x, y, c)
            add(i, src_ref.at[2 * px + py], my_chip, 2 * px + py, (px, py, pc))
    return out


def _exchange_start(src, plan, name):
    n = _PLAN_COPIES[plan]
    land = lax.empty((_PLAN_SLOTS[plan],) + (src.shape if plan == "gather" else src.shape[1:]), src.dtype)

    def body(src_ref, land_ref, send_sems, recv_sems, src_thru, land_thru, token):
        for cp in _plan_copies(plan, src_ref, land_ref, send_sems, recv_sems, False):
            cp.start()
        token[...] = jnp.zeros_like(token)

    return pl.pallas_call(
        body,
        name=name,
        out_shape=(pltpu.SemaphoreType.DMA((n,)), pltpu.SemaphoreType.DMA((n,)), pltpu.HBM(src.shape, src.dtype),
                   pltpu.HBM(land.shape, land.dtype), jax.ShapeDtypeStruct((8, LANES), f32)),
        in_specs=(_HBM_SPEC, _HBM_SPEC),
        out_specs=(_SEM_SPEC, _SEM_SPEC, _HBM_SPEC, _HBM_SPEC, pl.BlockSpec(memory_space=pltpu.VMEM)),
        input_output_aliases={0: 2, 1: 3},
        compiler_params=pltpu.CompilerParams(has_side_effects=_EFFECT),
    )(pltpu.with_memory_space_constraint(src, pltpu.HBM), pltpu.with_memory_space_constraint(land, pltpu.HBM))


def _exchange_wait(started, after, plan, name):
    send_sems, recv_sems, src_thru, land_thru, _ = started

    def body(src_ref, land_ref, send_sems, recv_sems, after_ref, src_dead, got_ref):
        for cp in _plan_copies(plan, src_ref, land_ref, send_sems, recv_sems, True):
            cp.wait_send()
            cp.wait_recv()

    return pl.pallas_call(
        body,
        name=name,
        out_shape=(pltpu.HBM(src_thru.shape, src_thru.dtype), pltpu.HBM(land_thru.shape, land_thru.dtype)),
        in_specs=(_HBM_SPEC, _HBM_SPEC, _SEM_SPEC, _SEM_SPEC, pl.BlockSpec(memory_space=pl.ANY)),
        out_specs=(_HBM_SPEC, _HBM_SPEC),
        input_output_aliases={0: 0, 1: 1},
        compiler_params=pltpu.CompilerParams(has_side_effects=_EFFECT),
    )(src_thru, land_thru, send_sems, recv_sems, after)


def _pair_sum(g, r, name, tc=LANES):
    _, R, Cc = g.shape

    def body(g_ref, r_ref, o_ref):
        o_ref[...] = (g_ref[...].astype(f32) + r_ref[...].astype(f32)).astype(o_ref.dtype)

    return pl.pallas_call(
        body,
        out_shape=jax.ShapeDtypeStruct(r.shape, r.dtype),
        grid=(N_CHIP, Cc // tc),
        in_specs=[pl.BlockSpec((1, R, tc), lambda k, i: (2 * k + lax.axis_index("c"), 0, i)), pl.BlockSpec((1, R, tc), lambda k, i: (k, 0, i))],
        out_specs=pl.BlockSpec((1, R, tc), lambda k, i: (k, 0, i)),
        compiler_params=_cparams(("parallel", "parallel")),
        name=name,
    )(g, r)


def _exchange_small(g_sm, g_rep):
    srcs = (g_sm, g_rep)

    def body(in0, in1, out0, out1, send_sems, recv_sems, local_sems):
        x, y, c = lax.axis_index("x"), lax.axis_index("y"), lax.axis_index("c")
        me = 4 * x + 2 * y + c
        ins, outs = (in0, in1), (out0, out1)

        def copy(a, k, arrival):
            dev, p = _relation_peer(k, x, y, c)
            return pltpu.make_async_remote_copy(src_ref=ins[a].at[p] if a == 0 else ins[a], dst_ref=outs[a].at[p if arrival else me],
                                                send_sem=send_sems.at[7 * a + k - 1], recv_sem=recv_sems.at[7 * a + k - 1],
                                                device_id=dev, device_id_type=_MESH)

        pairs = [(a, k) for a in (0, 1) for k in range(1, N_DEV)]
        sends = [copy(a, k, False) for a, k in pairs]
        for cp in sends:
            cp.start()
        local = [pltpu.make_async_copy(ins[0].at[me], outs[0].at[me], local_sems.at[0]),
                 pltpu.make_async_copy(ins[1], outs[1].at[me], local_sems.at[1])]
        for cp in local:
            cp.start()
        for a, k in pairs:
            copy(a, k, True).wait_recv()
        for cp in sends:
            cp.wait_send()
        for cp in local:
            cp.wait()

    return pl.pallas_call(
        body,
        out_shape=(jax.ShapeDtypeStruct(g_sm.shape, g_sm.dtype), jax.ShapeDtypeStruct((N_DEV,) + g_rep.shape, g_rep.dtype)),
        in_specs=[_HBM_SPEC] * 2,
        out_specs=(_HBM_SPEC,) * 2,
        scratch_shapes=[pltpu.SemaphoreType.DMA((14,)), pltpu.SemaphoreType.DMA((14,)), pltpu.SemaphoreType.DMA((2,))],
        name="exchange_small",
    )(*srcs)


def _sum_adam(parts, w, m, v, name, tb, tc=None, own=None, own_slot=None):
    R, Cc = w.shape
    n_parts = parts.shape[0]
    tb = R if tc else min(tb, R)
    tc = tc or Cc
    assert R % tb == 0 and Cc % tc == 0
    bc1 = 1.0 - ADAM_B1**ADAM_STEP
    bc2 = 1.0 - ADAM_B2**ADAM_STEP
    extra = [] if own is None else [own]

    def body(p_ref, w_ref, m_ref, v_ref, *rest):
        g_ref, d_ref, nm_ref, nv_ref = rest[len(extra) :]

        def part(j):
            pj = p_ref[j].astype(f32)
            return pj if own is None else jnp.where(own_slot() == j, rest[0][0].astype(f32), pj)

        g = part(0)
        for j in range(1, n_parts):
            g = g + part(j)
        g_ref[...] = g
        m2 = ADAM_B1 * m_ref[...] + (1.0 - ADAM_B1) * g
        v2 = ADAM_B2 * v_ref[...] + (1.0 - ADAM_B2) * (g * g)
        nm_ref[...] = m2
        nv_ref[...] = v2
        d_ref[...] = -ADAM_LR * ((m2 / bc1) / (jnp.sqrt(v2 / bc2) + ADAM_EPS) + ADAM_WD * w_ref[...])

    blk = pl.BlockSpec((tb, tc), lambda i, j: (i, j))
    o = jax.ShapeDtypeStruct((R, Cc), f32)
    return pl.pallas_call(
        body,
        out_shape=(o, o, o, o),
        grid=(R // tb, Cc // tc),
        in_specs=[pl.BlockSpec((n_parts, tb, tc), lambda i, j: (0, i, j)), blk, blk, blk]
        + [pl.BlockSpec((1, tb, tc), lambda i, j: (own_slot(), i, j))] * len(extra),
        out_specs=(blk, blk, blk, blk),
        compiler_params=_cparams(("parallel", "parallel")),
        name=name,
    )(parts, w, m, v, *extra)


def _my_device():
    return 4 * lax.axis_index("x") + 2 * lax.axis_index("y") + lax.axis_index("c")


def _my_chip():
    return 2 * lax.axis_index("x") + lax.axis_index("y")


_WEIGHTS = ["ln_pre_w", "w_in", "conv_w", "a_log_fwd", "a_log_bwd", "dt_bias_fwd", "dt_bias_bwd", "gdn_norm_w", "w_proj_gdn",
            "gk_w2_fwd", "gk_b2_fwd", "gk_w2_bwd", "gk_b2_bwd", "gla_norm_w", "w_proj_gla", "w_out", "ln_post_w"]
_REP_ORDER = ["ln_pre_w", "ln_post_w", "gk_b2_fwd", "gk_b2_bwd", "gla_norm_w", "gdn_norm_w", "a_log_fwd", "a_log_bwd", "dt_bias_fwd", "dt_bias_bwd"]


def kernel(x, ln_pre_w, w_in, conv_w, a_log_fwd, a_log_bwd, dt_bias_fwd, dt_bias_bwd, gdn_norm_w, w_proj_gdn, gk_w2_fwd, gk_b2_fwd, gk_w2_bwd, gk_b2_bwd, gla_norm_w, w_proj_gla, w_out, ln_post_w, loss_target, m_ln_pre_w, m_w_in, m_conv_w, m_a_log_fwd, m_a_log_bwd, m_dt_bias_fwd, m_dt_bias_bwd, m_gdn_norm_w, m_w_proj_gdn, m_gk_w2_fwd, m_gk_b2_fwd, m_gk_w2_bwd, m_gk_b2_bwd, m_gla_norm_w, m_w_proj_gla, m_w_out, m_ln_post_w, v_ln_pre_w, v_w_in, v_conv_w, v_a_log_fwd, v_a_log_bwd, v_dt_bias_fwd, v_dt_bias_bwd, v_gdn_norm_w, v_w_proj_gdn, v_gk_w2_fwd, v_gk_b2_fwd, v_gk_w2_bwd, v_gk_b2_bwd, v_gla_norm_w, v_w_proj_gla, v_w_out, v_ln_post_w):
    args = locals()
    W = {n: args[n] for n in _WEIGHTS}
    M = {n: args["m_" + n] for n in _WEIGHTS}
    V = {n: args["v_" + n] for n in _WEIGHTS}

    wp_stack = lambda P: jnp.concatenate([P["w_proj_gdn"], P["w_proj_gla"], P["w_out"]], axis=0)
    sm_pack = lambda P: _pack_small_shard(P["conv_w"][0], P["gk_w2_fwd"][0], P["gk_w2_bwd"][0])
    me = 4 * lax.axis_index("x") + 2 * lax.axis_index("y") + lax.axis_index("c")
    w_in_t, m_in_t, v_in_t = (jnp.transpose(a[0]) for a in (w_in, m_w_in, v_w_in))
    g_in, g_sm = _gather_weights(w_in_t, sm_pack(W))
    wp_own = wp_stack(W).astype(MXU_DTYPE)
    wp_started = _exchange_start(wp_own, "gather", "gather_proj_start")

    def proj_weights(after):
        own, land = _exchange_wait(wp_started, after, "gather", "gather_proj_wait")
        g_p = lax.dynamic_update_slice(land, own[None], (me, 0, 0, 0))
        return tuple(g_p[:, i].reshape(D_MODEL, D_MODEL) for i in range(3))

    rows_p = 3 * D_MODEL // N_DEV
    sent = {}

    def send_proj_grads(d_wpg, d_wpl, d_wout):
        p_p = jnp.stack([d_wpg, d_wpl, d_wout]).reshape(3, N_DEV, D_MODEL // N_DEV, D_MODEL).transpose(1, 0, 2, 3)
        sent["p_p"] = p_p.reshape(N_DEV, rows_p, D_MODEL)
        sent["started"] = _exchange_start(sent["p_p"], "blocks", "exchange_proj_start")
        return sent["started"][4][0:1, 0:1]

    def send_in_grads(d_wcat):
        p_in = _from_cat(d_wcat).reshape(N_DEV, SHARD_IN, D_MODEL)
        sib = _exchange_start(p_in, "sibling", "exchange_in_sibling_start")
        p_p, land_p = _exchange_wait(sent["started"], sib[4], "blocks", "exchange_proj_wait")
        sent["o_p"] = _sum_adam(land_p, wp_stack(W).reshape(rows_p, D_MODEL), wp_stack(M).reshape(rows_p, D_MODEL),
                                wp_stack(V).reshape(rows_p, D_MODEL), "adam_w_proj", 128, own=p_p, own_slot=_my_device)
        p_in, sib_land = _exchange_wait(sib, sent["o_p"][0], "sibling", "exchange_in_sibling_wait")
        sent["chips"] = _exchange_start(_pair_sum(p_in, sib_land, "pair_sum_w_in", tc=512), "chips", "exchange_in_chips_start")
        return sent["chips"][4]

    wcat = _to_cat(g_in.reshape(N_IN, D_MODEL))
    conv_full = g_sm[:, 0:CONV_K, :].transpose(1, 0, 2).reshape(CONV_K, N_DEV * SM_COLS)
    w2f_full = g_sm[:, 8:24, 0:64].transpose(1, 0, 2).reshape(16, 512)
    w2b_full = g_sm[:, 8:24, 64:128].transpose(1, 0, 2).reshape(16, 512)

    avec = _lane_row(a_log_fwd, a_log_bwd)
    dvec = _lane_row(dt_bias_fwd, dt_bias_bwd)
    G = _local_step(x[0], loss_target[0], wcat, conv_full, avec, dvec, gdn_norm_w, _pad_w2(w2f_full, 32), _pad_w2(w2b_full, 48),
                    gk_b2_fwd, gk_b2_bwd, gla_norm_w, proj_weights, ln_pre_w + wp_started[4][0:1, 0:1], ln_post_w, send_proj_grads,
                    send_in_grads)

    p_sm = _pack_small_shard(G["d_conv"].reshape(CONV_K, N_DEV, SM_COLS).transpose(1, 0, 2),
                             G["d_w2f"][32:48].reshape(16, N_DEV, 64).transpose(1, 0, 2),
                             G["d_w2b"][48:64].reshape(16, N_DEV, 64).transpose(1, 0, 2))
    p_rep = _pack_rep(G["d_wpre"], G["d_wpost"], G["d_b2f"], G["d_b2b"], G["d_gla_norm"], G["d_gdn_norm"],
                      G["d_alog"][:, 0:8], G["d_alog"][:, 8:16], G["d_dtb"][:, 0:8], G["d_dtb"][:, 8:16],
                      loss=0.5 * jnp.sum(G["loss_row"], axis=1, keepdims=True) / D_MODEL)
    r_sm, r_rep = _exchange_small(p_sm, p_rep)
    q_in, land_in = _exchange_wait(sent["chips"], r_rep, "chips", "exchange_in_chips_wait")
    o_in = _sum_adam(land_in, w_in_t, m_in_t, v_in_t, "adam_w_in", SHARD_IN, tc=256, own=q_in, own_slot=_my_chip)
    o_p = sent["o_p"]
    o_sm = _sum_adam(r_sm, sm_pack(W), sm_pack(M), sm_pack(V), "adam_small", SM_ROWS)
    rep_pack = lambda P: _pack_rep(*[P[n] for n in _REP_ORDER])
    o_rep = _sum_adam(r_rep, rep_pack(W), rep_pack(M), rep_pack(V), "adam_rep", REP_ROWS)

    res = []
    for kind in range(4):
        rep = _unpack_rep(o_rep[kind])
        per = dict(rep)
        per["w_in"] = jnp.transpose(o_in[kind])[None]
        pp = o_p[kind].reshape(3, 1, D_MODEL // N_DEV, D_MODEL)
        per["w_proj_gdn"], per["w_proj_gla"], per["w_out"] = pp[0], pp[1], pp[2]
        per["conv_w"] = o_sm[kind][None, 0:CONV_K, :]
        per["gk_w2_fwd"] = o_sm[kind][None, 8:24, 0:64]
        per["gk_w2_bwd"] = o_sm[kind][None, 8:24, 64:128]
        res.append([per[n] for n in _WEIGHTS])
    loss = o_rep[0][LOSS_ROW, 0]
    return (loss, G["grad_x"][None], *res[0], *res[1], *res[2], *res[3])
```

```python
import functools
import math

import jax
import jax.numpy as jnp
from jax import lax
from jax.experimental import pallas as pl
from jax.experimental.pallas import tpu as pltpu

f32 = jnp.float32
MXU_DTYPE = jnp.bfloat16
HI = lax.Precision.HIGHEST

D_MODEL = 1024
CHUNK = 64
NORM_EPS = 1e-6
GDN_HEADS, GDN_DIM = 8, 128
GLA_HEADS, GLA_DK, GLA_DV = 4, 128, 256
GATE_NORMALIZER = 16.0
CONV_K = 5
N_IN = 9280
N_DEV = 8
SHARD_IN = N_IN // N_DEV

N_CAT = 9728
SMALL_OFF = 9216
LANES = 128

ADAM_LR, ADAM_B1, ADAM_B2, ADAM_EPS, ADAM_WD, ADAM_STEP = 0.001, 0.9, 0.999, 1e-08, 0.01, 10

VMEM_LIMIT = 56 * 1024 * 1024


def _cparams(sem=None):
    return pltpu.CompilerParams(dimension_semantics=sem, vmem_limit_bytes=VMEM_LIMIT)


_DN = {"nn": (((1,), (0,)), ((), ())), "nt": (((1,), (1,)), ((), ())), "tn": (((0,), (0,)), ((), ()))}
_BDN = {"nn": (((2,), (1,)), ((0,), (0,))), "nt": (((2,), (2,)), ((0,), (0,))), "tn": (((1,), (1,)), ((0,), (0,)))}


def _dot16(a, b, kind):
    return lax.dot_general(a, b, (_DN if a.ndim == 2 else _BDN)[kind], preferred_element_type=f32)


def _mx(a, b, kind):
    return _dot16(a.astype(MXU_DTYPE), b.astype(MXU_DTYPE), kind)


def _split2(a):
    hi = a.astype(jnp.bfloat16)
    return hi, (a - hi.astype(f32)).astype(jnp.bfloat16)


def _split3(a):
    p1 = a.astype(jnp.bfloat16)
    r = a - p1.astype(f32)
    p2 = r.astype(jnp.bfloat16)
    return p1, p2, (r - p2.astype(f32)).astype(jnp.bfloat16)


def _h3s(a2, b2, kind):
    (ah, al), (bh, bl) = a2, b2
    return _dot16(ah, bh, kind) + (_dot16(ah, bl, kind) + _dot16(al, bh, kind))


def _h3(a, b, kind):
    return _h3s(_split2(a), _split2(b), kind)


def _exact01(t01, x, kind="nn"):
    t = t01.astype(jnp.bfloat16)
    x1, x2, x3 = _split3(x)
    return _dot16(t, x1, kind) + (_dot16(t, x2, kind) + _dot16(t, x3, kind))


def _colsum_col(e):
    ones = jnp.ones(e.shape[:-1] + (LANES,), jnp.bfloat16)
    e1, e2, e3 = _split3(e)
    return (_dot16(e1, ones, "tn") + (_dot16(e2, ones, "tn") + _dot16(e3, ones, "tn")))[..., 0:1]


def _iota3(shape, axis):
    return lax.broadcasted_iota(jnp.int32, shape, axis)


def _dir_masks(B, nf, Cn):
    shp = (B, Cn, Cn)
    bb, ii, jj = _iota3(shp, 0), _iota3(shp, 1), _iota3(shp, 2)
    fwd = bb < nf
    rev = jnp.logical_not(fwd)
    incl = jnp.logical_or(jnp.logical_and(fwd, ii >= jj), jnp.logical_and(rev, ii <= jj))
    strict = jnp.logical_and(incl, ii != jj)
    return incl, strict, ii == jj


def _last_row(x, nf):
    B, Cn = x.shape[0], x.shape[1]
    fwd = _iota3((B, 1, 1), 0) < nf
    return jnp.where(fwd, x[:, Cn - 1 : Cn, :], x[:, 0:1, :])


def _add_at_last_row(x, val, nf):
    B, Cn = x.shape[0], x.shape[1]
    bb, rr = _iota3((B, Cn, 1), 0), _iota3((B, Cn, 1), 1)
    at = jnp.logical_or(jnp.logical_and(bb < nf, rr == Cn - 1), jnp.logical_and(bb >= nf, rr == 0))
    return x + jnp.where(at, val, 0.0)


def _sigmoid(x):
    return jax.nn.sigmoid(x)


def _silu(x):
    return x * _sigmoid(x)


def _silu_grad(x):
    s = _sigmoid(x)
    return s * (1.0 + x * (1.0 - s))


def _softplus(x):
    u = jnp.exp(-jnp.abs(x))
    l1p = jnp.where(u < 1e-3, u * (1.0 - u * (0.5 - u * (1.0 / 3.0))), jnp.log(1.0 + u))
    return jnp.maximum(x, 0.0) + l1p


def _mxr(x):
    return x.astype(MXU_DTYPE).astype(f32)


def _iota2(shape, axis):
    return lax.broadcasted_iota(jnp.int32, shape, axis)


def _matmul(a, b, kind, out_dtype, name, tm, tn, tk, out_t=False, after=None):
    extra = [] if after is None else [after]
    if kind == "nn":
        (M, K), N = a.shape, b.shape[1]
    elif kind == "nt":
        (M, K), N = a.shape, b.shape[0]
    else:
        (K, M), N = a.shape, b.shape[1]
    tm, tn, tk = min(tm, M), min(tn, N), min(tk, K)
    assert M % tm == 0 and N % tn == 0 and K % tk == 0, (name, M, N, K)
    nk = K // tk
    if kind == "tn":
        a_spec = pl.BlockSpec((tk, tm), lambda i, j, k: (k, i))
    else:
        a_spec = pl.BlockSpec((tm, tk), lambda i, j, k: (i, k))
    if kind == "nt":
        b_spec = pl.BlockSpec((tn, tk), lambda i, j, k: (j, k))
    else:
        b_spec = pl.BlockSpec((tk, tn), lambda i, j, k: (k, j))

    def body(a_ref, b_ref, *rest):
        o_ref, acc = rest[len(extra)], rest[len(extra) + 1 :]
        p = _mx(a_ref[...], b_ref[...], kind)
        if nk == 1:
            o_ref[...] = (p.T if out_t else p).astype(out_dtype)
        else:
            assert not out_t
            acc_ref = acc[0]
            k = pl.program_id(2)

            @pl.when(k == 0)
            def _():
                acc_ref[...] = p

            @pl.when(k > 0)
            def _():
                acc_ref[...] += p

            @pl.when(k == nk - 1)
            def _():
                o_ref[...] = acc_ref[...].astype(out_dtype)

    return pl.pallas_call(
        body,
        out_shape=jax.ShapeDtypeStruct((N, M) if out_t else (M, N), out_dtype),
        grid=(M // tm, N // tn, nk),
        in_specs=[a_spec, b_spec] + [pl.BlockSpec((8, LANES), lambda i, j, k: (0, 0))] * len(extra),
        out_specs=pl.BlockSpec((tn, tm), lambda i, j, k: (j, i)) if out_t else pl.BlockSpec((tm, tn), lambda i, j, k: (i, j)),
        scratch_shapes=[] if nk == 1 else [pltpu.VMEM((tm, tn), f32)],
        compiler_params=_cparams(("parallel", "parallel", "arbitrary")),
        name=name,
    )(a, b, *extra)


def _rms_pre(x, w, tb=256):
    T = x.shape[0]

    def body(x_ref, w_ref, h_ref, ht_ref):
        xv = x_ref[...]
        r = lax.rsqrt(jnp.mean(xv * xv, axis=1, keepdims=True) + NORM_EPS)
        h = xv * r * w_ref[...]
        h_ref[...] = h.astype(h_ref.dtype)
        ht_ref[...] = h.T.astype(ht_ref.dtype)

    return pl.pallas_call(
        body,
        out_shape=(jax.ShapeDtypeStruct((T, D_MODEL), MXU_DTYPE), jax.ShapeDtypeStruct((D_MODEL, T), MXU_DTYPE)),
        grid=(T // tb,),
        in_specs=[pl.BlockSpec((tb, D_MODEL), lambda i: (i, 0)), pl.BlockSpec((1, D_MODEL), lambda i: (0, 0))],
        out_specs=(pl.BlockSpec((tb, D_MODEL), lambda i: (i, 0)), pl.BlockSpec((D_MODEL, tb), lambda i: (0, i))),
        compiler_params=_cparams(("parallel",)),
        name="rms_pre",
    )(x, w)


def _rms_pre_bwd(x, w, dh, dy, tb=256):
    T = x.shape[0]

    def body(x_ref, w_ref, dh_ref, dy_ref, dx_ref, dw_ref):
        i = pl.program_id(0)
        xv, dhv = x_ref[...], dh_ref[...]
        r = lax.rsqrt(jnp.mean(xv * xv, axis=1, keepdims=True) + NORM_EPS)
        dhw = dhv * w_ref[...]
        dx_ref[...] = dy_ref[...] + r * dhw - xv * (r * r * r) * jnp.mean(dhw * xv, axis=1, keepdims=True)
        part = jnp.sum(dhv * xv * r, axis=0, keepdims=True)

        @pl.when(i == 0)
        def _():
            dw_ref[...] = part

        @pl.when(i > 0)
        def _():
            dw_ref[...] += part

    blk = pl.BlockSpec((tb, D_MODEL), lambda i: (i, 0))
    row = pl.BlockSpec((1, D_MODEL), lambda i: (0, 0))
    return pl.pallas_call(
        body,
        out_shape=(jax.ShapeDtypeStruct((T, D_MODEL), f32), jax.ShapeDtypeStruct((1, D_MODEL), f32)),
        grid=(T // tb,),
        in_specs=[blk, row, blk, blk],
        out_specs=(blk, row),
        compiler_params=_cparams(("arbitrary",)),
        name="rms_pre_bwd",
    )(x, w, dh, dy)


def _conv5(u, cw):
    T = u.shape[0]
    t = _iota2(u.shape, 0)
    acc = u * cw[2:3, :]
    for j in (0, 1, 3, 4):
        d = j - 2
        sh = pltpu.roll(u, (-d) % T, 0)
        valid = jnp.logical_and(t + d >= 0, t + d < T)
        acc = acc + jnp.where(valid, sh, 0.0) * cw[j : j + 1, :]
    return acc


def _gdn_prep(proj, conv_w):
    T = proj.shape[0]
    nh = GDN_HEADS

    def body(u_ref, cw_ref, o_ref):
        j = pl.program_id(0)
        s = _silu(_conv5(_mxr(u_ref[...]), _mxr(cw_ref[...])))

        @pl.when(j < 2 * nh)
        def _():
            o_ref[0] = s * lax.rsqrt(jnp.sum(s * s, axis=1, keepdims=True) + NORM_EPS)

        @pl.when(j >= 2 * nh)
        def _():
            o_ref[0] = s

    return pl.pallas_call(
        body,
        out_shape=jax.ShapeDtypeStruct((3, T, GDN_HEADS * GDN_DIM), f32),
        grid=(3 * nh,),
        in_specs=[pl.BlockSpec((T, GDN_DIM), lambda j: (0, j)), pl.BlockSpec((CONV_K, GDN_DIM), lambda j: (0, j))],
        out_specs=pl.BlockSpec((1, T, GDN_DIM), lambda j: (j // nh, 0, j % nh)),
        compiler_params=_cparams(("parallel",)),
        name="gdn_prep",
    )(proj, conv_w)


def _gdn_prep_bwd(proj, conv_w, dqkv_f, dqkv_b, dproj):
    T = proj.shape[0]
    nh = GDN_HEADS

    def body(u_ref, cw_ref, df_ref, db_ref, _, du_ref, dcw_ref):
        j = pl.program_id(0)
        u, cw = _mxr(u_ref[...]), _mxr(cw_ref[...])
        c = _conv5(u, cw)
        s = _silu(c)
        dn = df_ref[0] + db_ref[0]
        rinv = lax.rsqrt(jnp.sum(s * s, axis=1, keepdims=True) + NORM_EPS)
        ds_norm = rinv * dn - s * (rinv * rinv * rinv) * jnp.sum(dn * s, axis=1, keepdims=True)
        ds = jnp.where(j < 2 * nh, ds_norm, dn)
        dc = _mxr(ds * _silu_grad(c))
        t = _iota2(u.shape, 0)
        du = dc * cw[2:3, :]
        rows = []
        for jj in range(CONV_K):
            d = jj - 2
            if d == 0:
                rows.append(jnp.sum(dc * u, axis=0, keepdims=True))
                continue
            ush = pltpu.roll(u, (-d) % T, 0)
            uvalid = jnp.logical_and(t + d >= 0, t + d < T)
            rows.append(jnp.sum(dc * jnp.where(uvalid, ush, 0.0), axis=0, keepdims=True))
            dsh = pltpu.roll(dc, d % T, 0)
            dvalid = jnp.logical_and(t - d >= 0, t - d < T)
            du = du + jnp.where(dvalid, dsh, 0.0) * cw[jj : jj + 1, :]
        du_ref[...] = du.astype(du_ref.dtype)
        for jj in range(CONV_K):
            dcw_ref[jj : jj + 1, :] = rows[jj]

    qspec = pl.BlockSpec((1, T, GDN_DIM), lambda j: (j // nh, 0, j % nh))
    return pl.pallas_call(
        body,
        out_shape=(jax.ShapeDtypeStruct(dproj.shape, dproj.dtype), jax.ShapeDtypeStruct((CONV_K, 3 * nh * GDN_DIM), f32)),
        grid=(3 * nh,),
        in_specs=[pl.BlockSpec((T, GDN_DIM), lambda j: (0, j)), pl.BlockSpec((CONV_K, GDN_DIM), lambda j: (0, j)), qspec, qspec, _ANY_SPEC],
        out_specs=(pl.BlockSpec((T, GDN_DIM), lambda j: (0, j)), pl.BlockSpec((CONV_K, GDN_DIM), lambda j: (0, j))),
        input_output_aliases={4: 0},
        compiler_params=_cparams(("parallel",)),
        name="gdn_prep_bwd",
    )(proj, conv_w, dqkv_f, dqkv_b, dproj)


def _chunk_tri(n, rev):
    i, j = _iota2((n, n), 0), _iota2((n, n), 1)
    same = jnp.right_shift(i, 6) == jnp.right_shift(j, 6)
    order = (j >= i) if rev else (j <= i)
    return jnp.where(jnp.logical_and(same, order), 1.0, 0.0).astype(f32)


def _gdn_gates(proj, avec, dvec, tb=256):
    T = proj.shape[0]

    def body(s_ref, a_ref, d_ref, g_ref, gc_ref):
        small = s_ref[...]
        lane = _iota2(small.shape, 1)
        lg = -jnp.exp(a_ref[...]) * _softplus(small + d_ref[...])
        beta = _sigmoid(small)
        g_ref[...] = jnp.where(lane < 16, lg, jnp.where(lane < 32, beta, 0.0))
        lgm = jnp.where(lane < 16, lg, 0.0)
        gcf = _exact01(_chunk_tri(tb, False), lgm)
        gcr = _exact01(_chunk_tri(tb, True), lgm)
        gc_ref[...] = jnp.where(lane < 8, gcf, gcr)

    blk = pl.BlockSpec((tb, LANES), lambda i: (i, 0))
    row = pl.BlockSpec((1, LANES), lambda i: (0, 0))
    return pl.pallas_call(
        body,
        out_shape=(jax.ShapeDtypeStruct((T, LANES), f32), jax.ShapeDtypeStruct((T, LANES), f32)),
        grid=(T // tb,),
        in_specs=[pl.BlockSpec((tb, LANES), lambda i: (i, SMALL_OFF // LANES)), row, row],
        out_specs=(blk, blk),
        compiler_params=_cparams(("parallel",)),
        name="gdn_gates",
    )(proj, avec, dvec)


def _gdn_gates_bwd(proj, avec, dvec, dg_f, dg_b, dsmall_gla, dproj, tb=256):
    T = proj.shape[0]
    pad = N_CAT - SMALL_OFF

    def body(s_ref, a_ref, d_ref, gf_ref, gb_ref, dl_ref, _, ds_ref, da_ref, dd_ref):
        i = pl.program_id(0)
        small = s_ref[...]
        lane = _iota2(small.shape, 1)
        dgate = gf_ref[...] + gb_ref[...]
        z = small + d_ref[...]
        nega = -jnp.exp(a_ref[...])
        dz = dgate * nega * _sigmoid(z)
        beta = _sigmoid(small)
        dsm = jnp.where(lane < 16, dz, jnp.where(lane < 32, dgate * beta * (1.0 - beta), 0.0))
        ds_ref[:, :LANES] = (dsm + dl_ref[...]).astype(ds_ref.dtype)
        ds_ref[:, LANES:] = jnp.zeros((tb, pad - LANES), ds_ref.dtype)
        lg = nega * _softplus(z)
        pa = jnp.sum(jnp.where(lane < 16, dgate * lg, 0.0), axis=0, keepdims=True)
        pd = jnp.sum(jnp.where(lane < 16, dz, 0.0), axis=0, keepdims=True)

        @pl.when(i == 0)
        def _():
            da_ref[...] = pa
            dd_ref[...] = pd

        @pl.when(i > 0)
        def _():
            da_ref[...] += pa
            dd_ref[...] += pd

    blk = pl.BlockSpec((tb, LANES), lambda i: (i, 0))
    row = pl.BlockSpec((1, LANES), lambda i: (0, 0))
    return pl.pallas_call(
        body,
        out_shape=(jax.ShapeDtypeStruct(dproj.shape, dproj.dtype), jax.ShapeDtypeStruct((1, LANES), f32), jax.ShapeDtypeStruct((1, LANES), f32)),
        grid=(T // tb,),
        in_specs=[pl.BlockSpec((tb, LANES), lambda i: (i, SMALL_OFF // LANES)), row, row, blk, blk, blk, _ANY_SPEC],
        out_specs=(pl.BlockSpec((tb, pad), lambda i: (i, DP_SMALL_BLK)), row, row),
        input_output_aliases={6: 0},
        compiler_params=_cparams(("arbitrary",)),
        name="gdn_gates_bwd",
    )(proj, avec, dvec, dg_f, dg_b, dsmall_gla, dproj)


GDN_HB = 8


def _inv_unit_lower(L, eye):
    A = eye - L
    P2 = _split2(L)
    for _ in range(5):
        P2 = _split2(_h3s(P2, P2, "nn"))
        A = A + _h3s(_split2(A), P2, "nn")
    return A


def _lane_col(blk, idx):
    lane = _iota2(blk.shape, 1)
    return jnp.sum(jnp.where(lane == idx, blk, 0.0), axis=1, keepdims=True)


class _GdnChunk:
    def __init__(self, q, k, v, gcol, beta, rev, A=None):
        Cn = q.shape[0]
        ii, jj = _iota2((Cn, Cn), 0), _iota2((Cn, Cn), 1)
        self.incl = (ii <= jj) if rev else (ii >= jj)
        self.strict = (ii < jj) if rev else (ii > jj)
        grow = jnp.broadcast_to(gcol, (Cn, LANES)).T[0:1, :]
        diff = gcol - grow
        self.decay = jnp.where(self.incl, jnp.exp(jnp.where(self.incl, diff, 0.0)), 0.0)
        self.k, self.v, self.beta, self.gcol = k, v, beta, gcol
        self.qs = q * (GDN_DIM**-0.5)
        self.kb = k * beta
        self.L = jnp.where(self.strict, _mx(self.kb, k, "nt") * self.decay, 0.0)
        if A is None:
            A = _inv_unit_lower(self.L, jnp.where(ii == jj, 1.0, 0.0).astype(f32))
        self.A2 = _split2(A)
        self.A = A
        self.eg = jnp.exp(gcol)
        self.kbg = self.kb * self.eg
        self.sol = _h3s(self.A2, _split2(jnp.concatenate([v * beta, self.kbg], axis=1)), "nn")
        self.u = self.sol[:, :GDN_DIM]
        self.w = self.sol[:, GDN_DIM:]
        self.attn = jnp.where(self.incl, _mx(self.qs, k, "nt") * self.decay, 0.0)
        self.qd = self.qs * self.eg
        last = 0 if rev else Cn - 1
        self.last = last
        gl = gcol[last : last + 1, :]
        self.ekd = jnp.exp(gl - gcol)
        self.kd = k * self.ekd
        self.egl = jnp.exp(gl)


def _gdn_scan_fwd(qkvn, gates, gc):
    T = qkvn.shape[1]
    N = T // CHUNK
    nh, hb = GDN_HEADS, GDN_HB
    W = hb * GDN_DIM
    chains = [(d, hh) for d in range(2) for hh in range(hb)]

    def body(qf_ref, qb_ref, gf_ref, gb_ref, cf_ref, cb_ref, of_ref, ob_ref, sf_ref, sb_ref, af_ref, ab_ref, s_scr):
        n, b = pl.program_id(0), pl.program_id(1)

        @pl.when(n == 0)
        def _():
            for d, hh in chains:
                s_scr[d, b * hb + hh] = jnp.zeros((GDN_DIM, GDN_DIM), f32)

        dirs = ((qf_ref, gf_ref, cf_ref, of_ref, sf_ref, af_ref), (qb_ref, gb_ref, cb_ref, ob_ref, sb_ref, ab_ref))
        s_in = {ch: s_scr[ch[0], b * hb + ch[1]] for ch in chains}
        s_out = {}
        for d, hh in chains:
            q_ref, g_ref, c_ref, o_ref, st_ref, a_ref = dirs[d]
            h = b * hb + hh
            sl = slice(hh * GDN_DIM, (hh + 1) * GDN_DIM)
            gcol = _lane_col(c_ref[...], h + 8 * d)
            beta = _lane_col(g_ref[...], 16 + h + 8 * d)
            ck = _GdnChunk(q_ref[0, :, sl], q_ref[1, :, sl], q_ref[2, :, sl], gcol, beta, d == 1)
            S = s_in[d, hh]
            st_ref[0, hh] = S
            a_ref[0, hh] = ck.A
            v_new = ck.u - _mx(ck.w, S, "nn")
            o_ref[:, sl] = _mx(ck.qd, S, "nn") + _mx(ck.attn, v_new, "nn")
            s_out[d, hh] = S * ck.egl + _mx(ck.kd, v_new, "tn")
        for d, hh in chains:
            s_scr[d, b * hb + hh] = s_out[d, hh]

    fw, bw = (lambda n: n), (lambda n: N - 1 - n)
    qsp = lambda ci: pl.BlockSpec((3, CHUNK, W), lambda n, b: (0, ci(n), b))
    gsp = lambda ci: pl.BlockSpec((CHUNK, LANES), lambda n, b: (ci(n), 0))
    osp = lambda ci: pl.BlockSpec((CHUNK, W), lambda n, b: (ci(n), b))
    ssp = lambda ci: pl.BlockSpec((1, hb, GDN_DIM, GDN_DIM), lambda n, b: (ci(n), b, 0, 0))
    asp = lambda ci: pl.BlockSpec((1, hb, CHUNK, CHUNK), lambda n, b: (ci(n), b, 0, 0))
    o_shape = jax.ShapeDtypeStruct((T, nh * GDN_DIM), f32)
    s_shape = jax.ShapeDtypeStruct((N, nh, GDN_DIM, GDN_DIM), f32)
    a_shape = jax.ShapeDtypeStruct((N, nh, CHUNK, CHUNK), f32)
    return pl.pallas_call(
        body,
        out_shape=(o_shape, o_shape, s_shape, s_shape, a_shape, a_shape),
        grid=(N, nh // hb),
        in_specs=[qsp(fw), qsp(bw), gsp(fw), gsp(bw), gsp(fw), gsp(bw)],
        out_specs=(osp(fw), osp(bw), ssp(fw), ssp(bw), asp(fw), asp(bw)),
        scratch_shapes=[pltpu.VMEM((2, nh, GDN_DIM, GDN_DIM), f32)],
        compiler_params=_cparams(("arbitrary", "arbitrary")),
        name="gdn_scan_fwd",
    )(qkvn, qkvn, gates, gates, gc, gc)


def _gdn_scan_bwd(qkvn, gates, gc, st_f, st_b, a_f, a_b, do):
    T = qkvn.shape[1]
    N = T // CHUNK
    nh, hb = GDN_HEADS, GDN_HB
    nb = nh // hb
    W = hb * GDN_DIM
    chains = [(d, hh) for d in range(2) for hh in range(hb)]

    def body(qf_ref, qb_ref, gf_ref, gb_ref, cf_ref, cb_ref, sf_ref, sb_ref, af_ref, ab_ref, dof_ref, dob_ref,
             dqf_ref, dqb_ref, dgf_ref, dgb_ref, ds_scr):
        n, b = pl.program_id(0), pl.program_id(1)

        @pl.when(n == 0)
        def _():
            for d, hh in chains:
                ds_scr[d, b * hb + hh] = jnp.zeros((GDN_DIM, GDN_DIM), f32)

        @pl.when(b == 0)
        def _():
            dgf_ref[...] = jnp.zeros((CHUNK, LANES), f32)
            dgb_ref[...] = jnp.zeros((CHUNK, LANES), f32)

        dirs = ((qf_ref, gf_ref, cf_ref, sf_ref, af_ref, dof_ref, dqf_ref), (qb_ref, gb_ref, cb_ref, sb_ref, ab_ref, dob_ref, dqb_ref))
        ds_in = {ch: ds_scr[ch[0], b * hb + ch[1]] for ch in chains}
        ds_out = {}
        lane = _iota2((CHUNK, LANES), 1)
        acc = [dgf_ref[...], dgb_ref[...]]
        for d, hh in chains:
            q_ref, g_ref, c_ref, st_ref, a_ref, do_ref, dq_ref = dirs[d]
            h = b * hb + hh
            sl = slice(hh * GDN_DIM, (hh + 1) * GDN_DIM)
            gcol = _lane_col(c_ref[...], h + 8 * d)
            beta = _lane_col(g_ref[...], 16 + h + 8 * d)
            ck = _GdnChunk(q_ref[0, :, sl], q_ref[1, :, sl], q_ref[2, :, sl], gcol, beta, d == 1, A=a_ref[0, hh])
            k, v = ck.k, ck.v
            S = st_ref[0, hh]
            dS = ds_in[d, hh]
            dov = do_ref[:, sl]
            v_new = ck.u - _mx(ck.w, S, "nn")
            d_vnew = _mx(ck.attn, dov, "tn") + _mx(ck.kd, dS, "nn")
            d_attn = jnp.where(ck.incl, _mx(dov, v_new, "nt"), 0.0)
            d_qd = _mx(dov, S, "nt")
            d_kd = _mx(v_new, dS, "nt")
            d_gl = jnp.sum(jnp.sum(dS * S, axis=1, keepdims=True), axis=0, keepdims=True) * ck.egl
            ds_out[d, hh] = dS * ck.egl + _mx(ck.qd, dov, "tn") - _mx(ck.w, d_vnew, "tn")
            d_w = -_mx(d_vnew, S, "nt")
            d_rhs = _h3s(ck.A2, _split2(jnp.concatenate([d_vnew, d_w], axis=1)), "tn")
            d_vb, d_kbg = d_rhs[:, :GDN_DIM], d_rhs[:, GDN_DIM:]
            dL = -jnp.where(ck.strict, _h3(d_rhs, ck.sol, "nt"), 0.0)
            P = dL * ck.decay
            Q = d_attn * ck.decay
            d_kb = _mx(P, k, "nn") + d_kbg * ck.eg
            d_qs = _mx(Q, k, "nn") + d_qd * ck.eg
            d_k = _mx(P, ck.kb, "tn") + _mx(Q, ck.qs, "tn") + d_kd * ck.ekd + d_kb * ck.beta
            E = dL * ck.L + d_attn * ck.attn
            kdsum = jnp.sum(d_kd * ck.kd, axis=1, keepdims=True)
            d_gc = (jnp.sum(E, axis=1, keepdims=True) - _colsum_col(E)
                    + jnp.sum(d_kbg * ck.kbg, axis=1, keepdims=True)
                    + jnp.sum(d_qd * ck.qd, axis=1, keepdims=True) - kdsum)
            d_gl = d_gl + jnp.sum(kdsum, axis=0, keepdims=True)
            row = _iota2((CHUNK, 1), 0)
            d_gc = d_gc + jnp.where(row == ck.last, d_gl, 0.0)
            d_beta = jnp.sum(d_vb * v, axis=1, keepdims=True) + jnp.sum(d_kb * k, axis=1, keepdims=True)
            dq_ref[0, :, sl] = d_qs * (GDN_DIM**-0.5)
            dq_ref[1, :, sl] = d_k
            dq_ref[2, :, sl] = d_vb * ck.beta
            acc[d] = jnp.where(lane == h + 8 * d, d_gc, acc[d])
            acc[d] = jnp.where(lane == 16 + h + 8 * d, d_beta, acc[d])
        for d, hh in chains:
            ds_scr[d, b * hb + hh] = ds_out[d, hh]
        dgf_ref[...] = acc[0]
        dgb_ref[...] = acc[1]

        @pl.when(b == nb - 1)
        def _():
            for d, dg_ref in enumerate((dgf_ref, dgb_ref)):
                blk = dg_ref[...]
                dg_ref[...] = jnp.where(lane < 16, _exact01(_chunk_tri(CHUNK, d == 0), blk), blk)

    fw, bw = (lambda n: N - 1 - n), (lambda n: n)
    qsp = lambda ci: pl.BlockSpec((3, CHUNK, W), lambda n, b: (0, ci(n), b))
    gsp = lambda ci: pl.BlockSpec((CHUNK, LANES), lambda n, b: (ci(n), 0))
    osp = lambda ci: pl.BlockSpec((CHUNK, W), lambda n, b: (ci(n), b))
    ssp = lambda ci: pl.BlockSpec((1, hb, GDN_DIM, GDN_DIM), lambda n, b: (ci(n), b, 0, 0))
    asp = lambda ci: pl.BlockSpec((1, hb, CHUNK, CHUNK), lambda n, b: (ci(n), b, 0, 0))
    dq_shape = jax.ShapeDtypeStruct((3, T, nh * GDN_DIM), f32)
    dg_shape = jax.ShapeDtypeStruct((T, LANES), f32)
    return pl.pallas_call(
        body,
        out_shape=(dq_shape, dq_shape, dg_shape, dg_shape),
        grid=(N, nb),
        in_specs=[qsp(fw), qsp(bw), gsp(fw), gsp(bw), gsp(fw), gsp(bw), ssp(fw), ssp(bw), asp(fw), asp(bw), osp(fw), osp(bw)],
        out_specs=(qsp(fw), qsp(bw), gsp(fw), gsp(bw)),
        scratch_shapes=[pltpu.VMEM((2, nh, GDN_DIM, GDN_DIM), f32)],
        compiler_params=_cparams(("arbitrary", "arbitrary")),
        name="gdn_scan_bwd",
    )(qkvn, qkvn, gates, gates, gc, gc, st_f, st_b, a_f, a_b, do, do)


def _out_norm(o_f, o_b, proj, gate_blk, w, hd, name, tb=256):
    T, W = o_f.shape
    nh = W // hd

    def body(of_ref, ob_ref, z_ref, w_ref, y_ref):
        wv = w_ref[...]
        for i in range(nh):
            sl = slice(i * hd, (i + 1) * hd)
            o = of_ref[:, sl] + ob_ref[:, sl]
            r = lax.rsqrt(jnp.mean(o * o, axis=1, keepdims=True) + NORM_EPS)
            y_ref[:, sl] = (o * r * wv * _silu(z_ref[:, sl])).astype(y_ref.dtype)

    blk = pl.BlockSpec((tb, W), lambda i: (i, 0))
    return pl.pallas_call(
        body,
        out_shape=jax.ShapeDtypeStruct((T, W), MXU_DTYPE),
        grid=(T // tb,),
        in_specs=[blk, blk, pl.BlockSpec((tb, W), lambda i: (i, gate_blk)), pl.BlockSpec((1, hd), lambda i: (0, 0))],
        out_specs=blk,
        compiler_params=_cparams(("parallel",)),
        name=name,
    )(o_f, o_b, proj, w)


def _out_norm_bwd(dy, o_f, o_b, proj, gate_blk, w, hd, name, dproj, tb=256):
    T, W = o_f.shape
    nh = W // hd

    def body(dy_ref, of_ref, ob_ref, z_ref, w_ref, _, do_ref, dz_ref, dw_ref):
        i = pl.program_id(0)
        wv = w_ref[...]
        dw = jnp.zeros((1, hd), f32)
        for a in range(nh):
            sl = slice(a * hd, (a + 1) * hd)
            o = of_ref[:, sl] + ob_ref[:, sl]
            z = z_ref[:, sl]
            dyv = dy_ref[:, sl]
            r = lax.rsqrt(jnp.mean(o * o, axis=1, keepdims=True) + NORM_EPS)
            orr = o * r
            dn = dyv * _silu(z)
            dz_ref[:, sl] = (dyv * orr * wv * _silu_grad(z)).astype(dz_ref.dtype)
            dw = dw + jnp.sum(dn * orr, axis=0, keepdims=True)
            dnw = dn * wv
            do_ref[:, sl] = r * dnw - o * (r * r * r) * jnp.mean(dnw * o, axis=1, keepdims=True)

        @pl.when(i == 0)
        def _():
            dw_ref[...] = dw

        @pl.when(i > 0)
        def _():
            dw_ref[...] += dw

    blk = pl.BlockSpec((tb, W), lambda i: (i, 0))
    row = pl.BlockSpec((1, hd), lambda i: (0, 0))
    return pl.pallas_call(
        body,
        out_shape=(jax.ShapeDtypeStruct((T, W), f32), jax.ShapeDtypeStruct(dproj.shape, dproj.dtype), jax.ShapeDtypeStruct((1, hd), f32)),
        grid=(T // tb,),
        in_specs=[blk, blk, blk, pl.BlockSpec((tb, W), lambda i: (i, gate_blk)), row, _ANY_SPEC],
        out_specs=(blk, pl.BlockSpec((tb, W), lambda i: (i, gate_blk)), row),
        input_output_aliases={5: 1},
        compiler_params=_cparams(("arbitrary",)),
        name=name,
    )(dy, o_f, o_b, proj, w, dproj)


def _gla_prep(proj, w2f, w2b, b2f, b2b, tb=256):
    T = proj.shape[0]
    W = GLA_HEADS * GLA_DK

    def body(s_ref, wf_ref, wb_ref, bf_ref, bb_ref, gf_ref, gb_ref):
        small = s_ref[...]
        gkf = -_softplus(-(_mx(small, wf_ref[...], "nn") + bf_ref[...])) * (1.0 / GATE_NORMALIZER)
        gkb = -_softplus(-(_mx(small, wb_ref[...], "nn") + bb_ref[...])) * (1.0 / GATE_NORMALIZER)
        gf_ref[...] = _exact01(_chunk_tri(tb, False), gkf)
        gb_ref[...] = _exact01(_chunk_tri(tb, True), gkb)

    blk = pl.BlockSpec((tb, W), lambda i: (i, 0))
    wsp = pl.BlockSpec((LANES, W), lambda i: (0, 0))
    row = pl.BlockSpec((1, W), lambda i: (0, 0))
    return pl.pallas_call(
        body,
        out_shape=(jax.ShapeDtypeStruct((T, W), f32), jax.ShapeDtypeStruct((T, W), f32)),
        grid=(T // tb,),
        in_specs=[pl.BlockSpec((tb, LANES), lambda i: (i, SMALL_OFF // LANES)), wsp, wsp, row, row],
        out_specs=(blk, blk),
        compiler_params=_cparams(("parallel",)),
        name="gla_prep",
    )(proj, w2f, w2b, b2f, b2b)


def _gla_prep_bwd(proj, w2f, w2b, b2f, b2b, dgk_f, dgk_b, tb=256):
    T = proj.shape[0]
    W = GLA_HEADS * GLA_DK

    def body(s_ref, wf_ref, wb_ref, bf_ref, bb_ref, df_ref, db_ref, ds_ref, dwf_ref, dwb_ref, dbf_ref, dbb_ref):
        i = pl.program_id(0)
        small = s_ref[...]
        dsm = jnp.zeros((tb, LANES), f32)
        parts = []
        for w_ref, b_ref, d_ref in ((wf_ref, bf_ref, df_ref), (wb_ref, bb_ref, db_ref)):
            pre = _mx(small, w_ref[...], "nn") + b_ref[...]
            dpre = d_ref[...] * (1.0 / GATE_NORMALIZER) * _sigmoid(-pre)
            dsm = dsm + _mx(dpre, w_ref[...], "nt")
            parts.append((_mx(small, dpre, "tn"), jnp.sum(dpre, axis=0, keepdims=True)))
        ds_ref[...] = dsm

        @pl.when(i == 0)
        def _():
            dwf_ref[...], dbf_ref[...] = parts[0]
            dwb_ref[...], dbb_ref[...] = parts[1]

        @pl.when(i > 0)
        def _():
            dwf_ref[...] += parts[0][0]
            dbf_ref[...] += parts[0][1]
            dwb_ref[...] += parts[1][0]
            dbb_ref[...] += parts[1][1]

    blk = pl.BlockSpec((tb, W), lambda i: (i, 0))
    wsp = pl.BlockSpec((LANES, W), lambda i: (0, 0))
    row = pl.BlockSpec((1, W), lambda i: (0, 0))
    return pl.pallas_call(
        body,
        out_shape=(jax.ShapeDtypeStruct((T, LANES), f32), jax.ShapeDtypeStruct((LANES, W), f32), jax.ShapeDtypeStruct((LANES, W), f32),
                   jax.ShapeDtypeStruct((1, W), f32), jax.ShapeDtypeStruct((1, W), f32)),
        grid=(T // tb,),
        in_specs=[pl.BlockSpec((tb, LANES), lambda i: (i, SMALL_OFF // LANES)), wsp, wsp, row, row, blk, blk],
        out_specs=(pl.BlockSpec((tb, LANES), lambda i: (i, 0)), wsp, wsp, row, row),
        compiler_params=_cparams(("arbitrary",)),
        name="gla_prep_bwd",
    )(proj, w2f, w2b, b2f, b2b, dgk_f, dgk_b)


Q_B_BLK, K_B_BLK, V_B_BLK = 4096 // GLA_DK, 4608 // GLA_DK, 5120 // GLA_DV


class _GlaChunk:
    def __init__(self, q, k, G, rev):
        Cn = q.shape[0]
        ii, jj = _iota2((Cn, Cn), 0), _iota2((Cn, Cn), 1)
        self.incl = (ii <= jj) if rev else (ii >= jj)
        self.eG = jnp.exp(G)
        self.enG = jnp.exp(-G)
        self.qg = q * (GLA_DK**-0.5) * self.eG
        self.kg = k * self.enG
        self.attn = jnp.where(self.incl, _mx(self.qg, self.kg, "nt"), 0.0)
        last = 0 if rev else Cn - 1
        self.last = last
        gl = G[last : last + 1, :]
        self.ekd = jnp.exp(gl - G)
        self.kd = k * self.ekd
        self.egl = jnp.exp(gl)


def _gla_scan_fwd(proj, G_f, G_b):
    T = proj.shape[0]
    N = T // CHUNK
    nh = GLA_HEADS

    def body(qf_ref, kf_ref, vf_ref, gf_ref, qb_ref, kb_ref, vb_ref, gb_ref, of_ref, ob_ref, sf_ref, sb_ref, s_scr):
        n, h = pl.program_id(0), pl.program_id(1)

        @pl.when(n == 0)
        def _():
            s_scr[0, h] = jnp.zeros((GLA_DV, GLA_DK), f32)
            s_scr[1, h] = jnp.zeros((GLA_DV, GLA_DK), f32)

        for d, (q_ref, k_ref, v_ref, g_ref, o_ref, st_ref) in enumerate(
            ((qf_ref, kf_ref, vf_ref, gf_ref, of_ref, sf_ref), (qb_ref, kb_ref, vb_ref, gb_ref, ob_ref, sb_ref))
        ):
            ck = _GlaChunk(q_ref[...], k_ref[...], g_ref[...], d == 1)
            v = v_ref[...]
            St = s_scr[d, h]
            st_ref[0, 0] = St
            o_ref[...] = _mx(ck.qg, St, "nt") + _mx(ck.attn, v, "nn")
            s_scr[d, h] = St * ck.egl + _mx(v, ck.kd, "tn")

    def specs(cidx):
        return [
            pl.BlockSpec((CHUNK, GLA_DK), lambda n, h: (cidx(n), Q_B_BLK + h)),
            pl.BlockSpec((CHUNK, GLA_DK), lambda n, h: (cidx(n), K_B_BLK + h)),
            pl.BlockSpec((CHUNK, GLA_DV), lambda n, h: (cidx(n), V_B_BLK + h)),
            pl.BlockSpec((CHUNK, GLA_DK), lambda n, h: (cidx(n), h)),
        ]

    fw, bw = (lambda n: n), (lambda n: N - 1 - n)
    o_shape = jax.ShapeDtypeStruct((T, nh * GLA_DV), f32)
    s_shape = jax.ShapeDtypeStruct((N, nh, GLA_DV, GLA_DK), f32)
    return pl.pallas_call(
        body,
        out_shape=(o_shape, o_shape, s_shape, s_shape),
        grid=(N, nh),
        in_specs=specs(fw) + specs(bw),
        out_specs=(
            pl.BlockSpec((CHUNK, GLA_DV), lambda n, h: (n, h)),
            pl.BlockSpec((CHUNK, GLA_DV), lambda n, h: (N - 1 - n, h)),
            pl.BlockSpec((1, 1, GLA_DV, GLA_DK), lambda n, h: (n, h, 0, 0)),
            pl.BlockSpec((1, 1, GLA_DV, GLA_DK), lambda n, h: (N - 1 - n, h, 0, 0)),
        ),
        scratch_shapes=[pltpu.VMEM((2, nh, GLA_DV, GLA_DK), f32)],
        compiler_params=_cparams(("arbitrary", "arbitrary")),
        name="gla_scan_fwd",
    )(proj, proj, proj, G_f, proj, proj, proj, G_b)


def _gla_scan_bwd(proj, G_f, G_b, st_f, st_b, do):
    T = proj.shape[0]
    N = T // CHUNK
    nh = GLA_HEADS

    def body(qf_ref, kf_ref, vf_ref, gf_ref, sf_ref, dof_ref, qb_ref, kb_ref, vb_ref, gb_ref, sb_ref, dob_ref,
             dqf_ref, dkf_ref, dvf_ref, dgf_ref, dqb_ref, dkb_ref, dvb_ref, dgb_ref, ds_scr):
        n, h = pl.program_id(0), pl.program_id(1)

        @pl.when(n == 0)
        def _():
            ds_scr[0, h] = jnp.zeros((GLA_DV, GLA_DK), f32)
            ds_scr[1, h] = jnp.zeros((GLA_DV, GLA_DK), f32)

        for d, (q_ref, k_ref, v_ref, g_ref, st_ref, do_ref, dq_ref, dk_ref, dv_ref, dg_ref) in enumerate(
            ((qf_ref, kf_ref, vf_ref, gf_ref, sf_ref, dof_ref, dqf_ref, dkf_ref, dvf_ref, dgf_ref),
             (qb_ref, kb_ref, vb_ref, gb_ref, sb_ref, dob_ref, dqb_ref, dkb_ref, dvb_ref, dgb_ref))
        ):
            rev = d == 1
            ck = _GlaChunk(q_ref[...], k_ref[...], g_ref[...], rev)
            v = v_ref[...]
            St = st_ref[0, 0]
            dSt = ds_scr[d, h]
            dov = do_ref[...]
            d_attn = jnp.where(ck.incl, _mx(dov, v, "nt"), 0.0)
            d_qg = _mx(dov, St, "nn") + _mx(d_attn, ck.kg, "nn")
            d_kg = _mx(d_attn, ck.qg, "tn")
            dv_ref[...] = _mx(ck.attn, dov, "tn") + _mx(ck.kd, dSt, "nt")
            d_kd = _mx(v, dSt, "nn")
            ds_scr[d, h] = dSt * ck.egl + _mx(dov, ck.qg, "tn")
            kdd = d_kd * ck.kd
            d_gl = jnp.sum(dSt * St, axis=0, keepdims=True) * ck.egl + jnp.sum(kdd, axis=0, keepdims=True)
            dq_ref[...] = d_qg * ck.eG * (GLA_DK**-0.5)
            dk_ref[...] = d_kg * ck.enG + d_kd * ck.ekd
            d_G = d_qg * ck.qg - d_kg * ck.kg - kdd
            row = _iota2((CHUNK, GLA_DK), 0)
            d_G = d_G + jnp.where(row == ck.last, d_gl, 0.0)
            dg_ref[...] = _exact01(_chunk_tri(CHUNK, not rev), d_G)

    def in_specs(cidx):
        return [
            pl.BlockSpec((CHUNK, GLA_DK), lambda n, h: (cidx(n), Q_B_BLK + h)),
            pl.BlockSpec((CHUNK, GLA_DK), lambda n, h: (cidx(n), K_B_BLK + h)),
            pl.BlockSpec((CHUNK, GLA_DV), lambda n, h: (cidx(n), V_B_BLK + h)),
            pl.BlockSpec((CHUNK, GLA_DK), lambda n, h: (cidx(n), h)),
            pl.BlockSpec((1, 1, GLA_DV, GLA_DK), lambda n, h: (cidx(n), h, 0, 0)),
            pl.BlockSpec((CHUNK, GLA_DV), lambda n, h: (cidx(n), h)),
        ]

    def out_specs(cidx):
        return [
            pl.BlockSpec((CHUNK, GLA_DK), lambda n, h: (cidx(n), h)),
            pl.BlockSpec((CHUNK, GLA_DK), lambda n, h: (cidx(n), h)),
            pl.BlockSpec((CHUNK, GLA_DV), lambda n, h: (cidx(n), h)),
            pl.BlockSpec((CHUNK, GLA_DK), lambda n, h: (cidx(n), h)),
        ]

    fw, bw = (lambda n: N - 1 - n), (lambda n: n)
    k_shape = jax.ShapeDtypeStruct((T, nh * GLA_DK), f32)
    v_shape = jax.ShapeDtypeStruct((T, nh * GLA_DV), f32)
    return pl.pallas_call(
        body,
        out_shape=(k_shape, k_shape, v_shape, k_shape) * 2,
        grid=(N, nh),
        in_specs=in_specs(fw) + in_specs(bw),
        out_specs=tuple(out_specs(fw) + out_specs(bw)),
        scratch_shapes=[pltpu.VMEM((2, nh, GLA_DV, GLA_DK), f32)],
        compiler_params=_cparams(("arbitrary", "arbitrary")),
        name="gla_scan_bwd",
    )(proj, proj, proj, G_f, st_f, do, proj, proj, proj, G_b, st_b, do)


GATE_A_BLK, GATE_B_BLK, Z_BLK, G_B_BLK = 6144 // 1024, 7168 // 1024, 3072 // 1024, 8192 // 1024
DP_GATES_BLK, DP_GLA_QKV_BLK, DP_SMALL_BLK = 6144 // 2048, 4096 // 2048, SMALL_OFF // 512
_ANY_SPEC = pl.BlockSpec(memory_space=pl.ANY)


def _mid(og, obg, proj, x, target, wpg, wpl, wout, w_post, tb=256):
    T = x.shape[0]
    Dm = D_MODEL

    def body(og_ref, obg_ref, ga_ref, gb_ref, x_ref, t_ref, wpg_ref, wpl_ref, wo_ref, wp_ref,
             loss_ref, dwp_ref, dy_ref, m_ref, dout_ref, dya_ref, dyb_ref, dgates_ref, dog_ref, dobg_ref):
        i = pl.program_id(0)
        ya = _mx(og_ref[...], wpg_ref[...], "nn")
        yb = _mx(obg_ref[...], wpl_ref[...], "nn")
        sa, sb = _sigmoid(ga_ref[...]), _sigmoid(gb_ref[...])
        m = sa * ya + sb * yb
        m_ref[...] = m.astype(m_ref.dtype)
        out = _mx(m, wo_ref[...], "nn")
        r = lax.rsqrt(jnp.mean(out * out, axis=1, keepdims=True) + NORM_EPS)
        wp = wp_ref[...]
        e = x_ref[...] + out * r * wp - t_ref[...]
        dy = e * (1.0 / Dm)
        dy_ref[...] = dy
        lpart = jnp.sum(e * e, axis=0, keepdims=True)
        wpart = jnp.sum(dy * out * r, axis=0, keepdims=True)
        dyn = dy * wp
        dout = r * dyn - out * (r * r * r) * jnp.mean(dyn * out, axis=1, keepdims=True)
        dout_ref[...] = dout.astype(dout_ref.dtype)
        dm = _mx(dout, wo_ref[...], "nt")
        dya, dyb = dm * sa, dm * sb
        dya_ref[...] = dya.astype(dya_ref.dtype)
        dyb_ref[...] = dyb.astype(dyb_ref.dtype)
        dgates_ref[:, :Dm] = (dm * ya * sa * (1.0 - sa)).astype(dgates_ref.dtype)
        dgates_ref[:, Dm:] = (dm * yb * sb * (1.0 - sb)).astype(dgates_ref.dtype)
        dog_ref[...] = _mx(dya, wpg_ref[...], "nt")
        dobg_ref[...] = _mx(dyb, wpl_ref[...], "nt")

        @pl.when(i == 0)
        def _():
            loss_ref[...] = lpart
            dwp_ref[...] = wpart

        @pl.when(i > 0)
        def _():
            loss_ref[...] += lpart
            dwp_ref[...] += wpart

    blk = pl.BlockSpec((tb, Dm), lambda i: (i, 0))
    row = pl.BlockSpec((1, Dm), lambda i: (0, 0))
    wsp = pl.BlockSpec((Dm, Dm), lambda i: (0, 0))
    act = jax.ShapeDtypeStruct((T, Dm), MXU_DTYPE)
    big = jax.ShapeDtypeStruct((T, Dm), f32)
    vec = jax.ShapeDtypeStruct((1, Dm), f32)
    return pl.pallas_call(
        body,
        out_shape=(vec, vec, big, act, act, act, act, jax.ShapeDtypeStruct((T, N_CAT), MXU_DTYPE), big, big),
        grid=(T // tb,),
        in_specs=[blk, blk, pl.BlockSpec((tb, Dm), lambda i: (i, GATE_A_BLK)), pl.BlockSpec((tb, Dm), lambda i: (i, GATE_B_BLK)),
                  blk, blk, wsp, wsp, wsp, row],
        out_specs=(row, row) + (blk,) * 5 + (pl.BlockSpec((tb, 2 * Dm), lambda i: (i, DP_GATES_BLK)), blk, blk),
        compiler_params=_cparams(("arbitrary",)),
        name="mid",
    )(og, obg, proj, proj, x, target, wpg, wpl, wout, w_post)


def _heads(ref, lead, n, width):
    return [ref[lead + (slice(None), slice(i * width, (i + 1) * width))] for i in range(n)]


class _GdnBatch:
    def __init__(self, q, k, v, gcol, grow, beta, nf, A=None):
        B, Cn = q.shape[0], q.shape[1]
        self.nf = nf
        self.incl, self.strict, eye = _dir_masks(B, nf, Cn)
        self.decay = jnp.where(self.incl, jnp.exp(jnp.where(self.incl, gcol - grow, 0.0)), 0.0)
        self.k, self.v, self.beta = k, v, beta
        self.qs = q * (GDN_DIM**-0.5)
        self.kb = k * beta
        self.L = jnp.where(self.strict, _mx(self.kb, k, "nt") * self.decay, 0.0)
        if A is None:
            A = _inv_unit_lower(self.L, jnp.where(eye, 1.0, 0.0).astype(f32))
        self.A = A
        self.A2 = _split2(A)
        self.eg = jnp.exp(gcol)
        self.kbg = self.kb * self.eg
        self.sol = _h3s(self.A2, _split2(jnp.concatenate([v * beta, self.kbg], axis=2)), "nn")
        self.u = self.sol[:, :, :GDN_DIM]
        self.w = self.sol[:, :, GDN_DIM:]
        self.attn = jnp.where(self.incl, _mx(self.qs, k, "nt") * self.decay, 0.0)
        self.qd = self.qs * self.eg
        gl = _last_row(gcol, nf)
        self.ekd = jnp.exp(gl - gcol)
        self.kd = k * self.ekd
        self.egl = jnp.exp(gl)


def _gdn_chain_inputs(qf_ref, qb_ref, gf_ref, gb_ref, cf_ref, cb_ref, b, hb):
    qkv = [jnp.stack(_heads(qf_ref, (i,), hb, GDN_DIM) + _heads(qb_ref, (i,), hb, GDN_DIM)) for i in range(3)]
    cols, rows, betas = [], [], []
    for d, (g_ref, c_ref) in enumerate(((gf_ref, cf_ref), (gb_ref, cb_ref))):
        gates, gc = g_ref[...], c_ref[...]
        for hh in range(hb):
            h = b * hb + hh
            col = _lane_col(gc, h + 8 * d)
            cols.append(col)
            rows.append(jnp.broadcast_to(col, (CHUNK, LANES)).T[0:1, :])
            betas.append(_lane_col(gates, 16 + h + 8 * d))
    return qkv[0], qkv[1], qkv[2], jnp.stack(cols), jnp.stack(rows), jnp.stack(betas)


def _gdn_fwd(qkvn, gates, gc):
    T = qkvn.shape[1]
    N = T // CHUNK
    nh, hb = GDN_HEADS, GDN_HB
    W = hb * GDN_DIM

    def body(qf_ref, qb_ref, gf_ref, gb_ref, cf_ref, cb_ref, of_ref, ob_ref, sf_ref, sb_ref, af_ref, ab_ref, s_scr):
        n, b = pl.program_id(0), pl.program_id(1)
        hs = pl.ds(b * hb, hb)

        @pl.when(n == 0)
        def _():
            s_scr[0, hs] = jnp.zeros((hb, GDN_DIM, GDN_DIM), f32)
            s_scr[1, hs] = jnp.zeros((hb, GDN_DIM, GDN_DIM), f32)

        q, k, v, gcol, grow, beta = _gdn_chain_inputs(qf_ref, qb_ref, gf_ref, gb_ref, cf_ref, cb_ref, b, hb)
        ck = _GdnBatch(q, k, v, gcol, grow, beta, hb)
        S = jnp.concatenate([s_scr[0, hs], s_scr[1, hs]], axis=0)
        sf_ref[0], sb_ref[0] = S[:hb], S[hb:]
        af_ref[0], ab_ref[0] = ck.A[:hb], ck.A[hb:]
        v_new = ck.u - _mx(ck.w, S, "nn")
        o = _mx(ck.qd, S, "nn") + _mx(ck.attn, v_new, "nn")
        S_new = S * ck.egl + _mx(ck.kd, v_new, "tn")
        for hh in range(hb):
            sl = slice(hh * GDN_DIM, (hh + 1) * GDN_DIM)
            of_ref[:, sl] = o[hh]
            ob_ref[:, sl] = o[hb + hh]
        s_scr[0, hs] = S_new[:hb]
        s_scr[1, hs] = S_new[hb:]

    fw, bw = (lambda n: n), (lambda n: N - 1 - n)
    qsp = lambda ci: pl.BlockSpec((3, CHUNK, W), lambda n, b: (0, ci(n), b))
    gsp = lambda ci: pl.BlockSpec((CHUNK, LANES), lambda n, b: (ci(n), 0))
    osp = lambda ci: pl.BlockSpec((CHUNK, W), lambda n, b: (ci(n), b))
    ssp = lambda ci: pl.BlockSpec((1, hb, GDN_DIM, GDN_DIM), lambda n, b: (ci(n), b, 0, 0))
    asp = lambda ci: pl.BlockSpec((1, hb, CHUNK, CHUNK), lambda n, b: (ci(n), b, 0, 0))
    o_shape = jax.ShapeDtypeStruct((T, nh * GDN_DIM), f32)
    s_shape = jax.ShapeDtypeStruct((N, nh, GDN_DIM, GDN_DIM), f32)
    a_shape = jax.ShapeDtypeStruct((N, nh, CHUNK, CHUNK), f32)
    return pl.pallas_call(
        body,
        out_shape=(o_shape, o_shape, s_shape, s_shape, a_shape, a_shape),
        grid=(N, nh // hb),
        in_specs=[qsp(fw), qsp(bw), gsp(fw), gsp(bw), gsp(fw), gsp(bw)],
        out_specs=(osp(fw), osp(bw), ssp(fw), ssp(bw), asp(fw), asp(bw)),
        scratch_shapes=[pltpu.VMEM((2, nh, GDN_DIM, GDN_DIM), f32)],
        compiler_params=_cparams(("arbitrary", "arbitrary")),
        name="gdn_scan_fwd",
    )(qkvn, qkvn, gates, gates, gc, gc)


def _gdn_bwd(qkvn, gates, gc, st_f, st_b, a_f, a_b, do):
    T = qkvn.shape[1]
    N = T // CHUNK
    nh, hb = GDN_HEADS, GDN_HB
    nb = nh // hb
    W = hb * GDN_DIM

    def body(qf_ref, qb_ref, gf_ref, gb_ref, cf_ref, cb_ref, sf_ref, sb_ref, af_ref, ab_ref, dof_ref, dob_ref,
             dqf_ref, dqb_ref, dgf_ref, dgb_ref, ds_scr):
        n, b = pl.program_id(0), pl.program_id(1)
        hs = pl.ds(b * hb, hb)

        @pl.when(n == 0)
        def _():
            ds_scr[0, hs] = jnp.zeros((hb, GDN_DIM, GDN_DIM), f32)
            ds_scr[1, hs] = jnp.zeros((hb, GDN_DIM, GDN_DIM), f32)

        @pl.when(b == 0)
        def _():
            dgf_ref[...] = jnp.zeros((CHUNK, LANES), f32)
            dgb_ref[...] = jnp.zeros((CHUNK, LANES), f32)

        q, k, v, gcol, grow, beta = _gdn_chain_inputs(qf_ref, qb_ref, gf_ref, gb_ref, cf_ref, cb_ref, b, hb)
        ck = _GdnBatch(q, k, v, gcol, grow, beta, hb, A=jnp.concatenate([af_ref[0], ab_ref[0]], axis=0))
        S = jnp.concatenate([sf_ref[0], sb_ref[0]], axis=0)
        dS = jnp.concatenate([ds_scr[0, hs], ds_scr[1, hs]], axis=0)
        dov = jnp.stack(_heads(dof_ref, (), hb, GDN_DIM) + _heads(dob_ref, (), hb, GDN_DIM))
        v_new = ck.u - _mx(ck.w, S, "nn")
        d_vnew = _mx(ck.attn, dov, "tn") + _mx(ck.kd, dS, "nn")
        d_attn = jnp.where(ck.incl, _mx(dov, v_new, "nt"), 0.0)
        d_qd = _mx(dov, S, "nt")
        d_kd = _mx(v_new, dS, "nt")
        d_gl = jnp.sum(jnp.sum(dS * S, axis=2, keepdims=True), axis=1, keepdims=True) * ck.egl
        dS_new = dS * ck.egl + _mx(ck.qd, dov, "tn") - _mx(ck.w, d_vnew, "tn")
        ds_scr[0, hs] = dS_new[:hb]
        ds_scr[1, hs] = dS_new[hb:]
        d_w = -_mx(d_vnew, S, "nt")
        d_rhs = _h3s(ck.A2, _split2(jnp.concatenate([d_vnew, d_w], axis=2)), "tn")
        d_vb, d_kbg = d_rhs[:, :, :GDN_DIM], d_rhs[:, :, GDN_DIM:]
        dL = -jnp.where(ck.strict, _h3(d_rhs, ck.sol, "nt"), 0.0)
        P = dL * ck.decay
        Q = d_attn * ck.decay
        d_kb = _mx(P, k, "nn") + d_kbg * ck.eg
        d_qs = _mx(Q, k, "nn") + d_qd * ck.eg
        d_k = _mx(P, ck.kb, "tn") + _mx(Q, ck.qs, "tn") + d_kd * ck.ekd + d_kb * ck.beta
        E = dL * ck.L + d_attn * ck.attn
        kdsum = jnp.sum(d_kd * ck.kd, axis=2, keepdims=True)
        d_gc = (jnp.sum(E, axis=2, keepdims=True) - _colsum_col(E)
                + jnp.sum(d_kbg * ck.kbg, axis=2, keepdims=True)
                + jnp.sum(d_qd * ck.qd, axis=2, keepdims=True) - kdsum)
        d_gc = _add_at_last_row(d_gc, d_gl + jnp.sum(kdsum, axis=1, keepdims=True), hb)
        d_beta = jnp.sum(d_vb * v, axis=2, keepdims=True) + jnp.sum(d_kb * k, axis=2, keepdims=True)
        d_q = d_qs * (GDN_DIM**-0.5)
        d_v = d_vb * ck.beta
        lane = _iota2((CHUNK, LANES), 1)
        for d, (dq_ref, dg_ref) in enumerate(((dqf_ref, dgf_ref), (dqb_ref, dgb_ref))):
            acc = dg_ref[...]
            for hh in range(hb):
                c, h = d * hb + hh, b * hb + hh
                sl = slice(hh * GDN_DIM, (hh + 1) * GDN_DIM)
                dq_ref[0, :, sl] = d_q[c]
                dq_ref[1, :, sl] = d_k[c]
                dq_ref[2, :, sl] = d_v[c]
                acc = jnp.where(lane == h + 8 * d, d_gc[c], acc)
                acc = jnp.where(lane == 16 + h + 8 * d, d_beta[c], acc)
            dg_ref[...] = acc

        @pl.when(b == nb - 1)
        def _():
            for d, dg_ref in enumerate((dgf_ref, dgb_ref)):
                blk = dg_ref[...]
                dg_ref[...] = jnp.where(lane < 16, _exact01(_chunk_tri(CHUNK, d == 0), blk), blk)

    fw, bw = (lambda n: N - 1 - n), (lambda n: n)
    qsp = lambda ci: pl.BlockSpec((3, CHUNK, W), lambda n, b: (0, ci(n), b))
    gsp = lambda ci: pl.BlockSpec((CHUNK, LANES), lambda n, b: (ci(n), 0))
    osp = lambda ci: pl.BlockSpec((CHUNK, W), lambda n, b: (ci(n), b))
    ssp = lambda ci: pl.BlockSpec((1, hb, GDN_DIM, GDN_DIM), lambda n, b: (ci(n), b, 0, 0))
    asp = lambda ci: pl.BlockSpec((1, hb, CHUNK, CHUNK), lambda n, b: (ci(n), b, 0, 0))
    dq_shape = jax.ShapeDtypeStruct((3, T, nh * GDN_DIM), f32)
    dg_shape = jax.ShapeDtypeStruct((T, LANES), f32)
    return pl.pallas_call(
        body,
        out_shape=(dq_shape, dq_shape, dg_shape, dg_shape),
        grid=(N, nb),
        in_specs=[qsp(fw), qsp(bw), gsp(fw), gsp(bw), gsp(fw), gsp(bw), ssp(fw), ssp(bw), asp(fw), asp(bw), osp(fw), osp(bw)],
        out_specs=(qsp(fw), qsp(bw), gsp(fw), gsp(bw)),
        scratch_shapes=[pltpu.VMEM((2, nh, GDN_DIM, GDN_DIM), f32)],
        compiler_params=_cparams(("arbitrary", "arbitrary")),
        name="gdn_scan_bwd",
    )(qkvn, qkvn, gates, gates, gc, gc, st_f, st_b, a_f, a_b, do, do)


class _GlaBatch:
    def __init__(self, q, k, G, nf):
        B, Cn = q.shape[0], q.shape[1]
        self.incl, _, _ = _dir_masks(B, nf, Cn)
        self.eG = jnp.exp(G)
        self.enG = jnp.exp(-G)
        self.qg = q * (GLA_DK**-0.5) * self.eG
        self.kg = k * self.enG
        self.attn = jnp.where(self.incl, _mx(self.qg, self.kg, "nt"), 0.0)
        gl = _last_row(G, nf)
        self.ekd = jnp.exp(gl - G)
        self.kd = k * self.ekd
        self.egl = jnp.exp(gl)


def _gla_chain_inputs(refs_f, refs_b):
    nh = GLA_HEADS
    out = []
    for i, width in enumerate((GLA_DK, GLA_DK, GLA_DV, GLA_DK)):
        out.append(jnp.stack(_heads(refs_f[i], (), nh, width) + _heads(refs_b[i], (), nh, width)))
    return out


def _gla_specs(N, ci):
    nh = GLA_HEADS
    return [
        pl.BlockSpec((CHUNK, nh * GLA_DK), lambda n: (ci(n), 4096 // (nh * GLA_DK))),
        pl.BlockSpec((CHUNK, nh * GLA_DK), lambda n: (ci(n), 4608 // (nh * GLA_DK))),
        pl.BlockSpec((CHUNK, nh * GLA_DV), lambda n: (ci(n), 5120 // (nh * GLA_DV))),
        pl.BlockSpec((CHUNK, nh * GLA_DK), lambda n: (ci(n), 0)),
    ]


def _gla_fwd(proj, G_f, G_b):
    T = proj.shape[0]
    N = T // CHUNK
    nh = GLA_HEADS

    def body(qf, kf, vf, gf, qb, kb, vb, gb, of_ref, ob_ref, sf_ref, sb_ref, s_scr):
        n = pl.program_id(0)

        @pl.when(n == 0)
        def _():
            s_scr[...] = jnp.zeros((2 * nh, GLA_DV, GLA_DK), f32)

        q, k, v, G = _gla_chain_inputs((qf, kf, vf, gf), (qb, kb, vb, gb))
        ck = _GlaBatch(q, k, G, nh)
        St = s_scr[...]
        sf_ref[0], sb_ref[0] = St[:nh], St[nh:]
        o = _mx(ck.qg, St, "nt") + _mx(ck.attn, v, "nn")
        for i in range(nh):
            sl = slice(i * GLA_DV, (i + 1) * GLA_DV)
            of_ref[:, sl] = o[i]
            ob_ref[:, sl] = o[nh + i]
        s_scr[...] = St * ck.egl + _mx(v, ck.kd, "tn")

    fw, bw = (lambda n: n), (lambda n: N - 1 - n)
    osp = lambda ci: pl.BlockSpec((CHUNK, nh * GLA_DV), lambda n: (ci(n), 0))
    ssp = lambda ci: pl.BlockSpec((1, nh, GLA_DV, GLA_DK), lambda n: (ci(n), 0, 0, 0))
    o_shape = jax.ShapeDtypeStruct((T, nh * GLA_DV), f32)
    s_shape = jax.ShapeDtypeStruct((N, nh, GLA_DV, GLA_DK), f32)
    return pl.pallas_call(
        body,
        out_shape=(o_shape, o_shape, s_shape, s_shape),
        grid=(N,),
        in_specs=_gla_specs(N, fw) + _gla_specs(N, bw),
        out_specs=(osp(fw), osp(bw), ssp(fw), ssp(bw)),
        scratch_shapes=[pltpu.VMEM((2 * nh, GLA_DV, GLA_DK), f32)],
        compiler_params=_cparams(("arbitrary",)),
        name="gla_scan_fwd",
    )(proj, proj, proj, G_f, proj, proj, proj, G_b)


def _gla_bwd(proj, G_f, G_b, st_f, st_b, do):
    T = proj.shape[0]
    N = T // CHUNK
    nh = GLA_HEADS

    def body(qf, kf, vf, gf, sf_ref, dof_ref, qb, kb, vb, gb, sb_ref, dob_ref,
             dqf_ref, dkf_ref, dvf_ref, dgf_ref, dqb_ref, dkb_ref, dvb_ref, dgb_ref, ds_scr):
        n = pl.program_id(0)

        @pl.when(n == 0)
        def _():
            ds_scr[...] = jnp.zeros((2 * nh, GLA_DV, GLA_DK), f32)

        q, k, v, G = _gla_chain_inputs((qf, kf, vf, gf), (qb, kb, vb, gb))
        ck = _GlaBatch(q, k, G, nh)
        St = jnp.concatenate([sf_ref[0], sb_ref[0]], axis=0)
        dSt = ds_scr[...]
        dov = jnp.stack(_heads(dof_ref, (), nh, GLA_DV) + _heads(dob_ref, (), nh, GLA_DV))
        d_attn = jnp.where(ck.incl, _mx(dov, v, "nt"), 0.0)
        d_qg = _mx(dov, St, "nn") + _mx(d_attn, ck.kg, "nn")
        d_kg = _mx(d_attn, ck.qg, "tn")
        d_v = _mx(ck.attn, dov, "tn") + _mx(ck.kd, dSt, "nt")
        d_kd = _mx(v, dSt, "nn")
        ds_scr[...] = dSt * ck.egl + _mx(dov, ck.qg, "tn")
        kdd = d_kd * ck.kd
        d_gl = jnp.sum(dSt * St, axis=1, keepdims=True) * ck.egl + jnp.sum(kdd, axis=1, keepdims=True)
        d_q = d_qg * ck.eG * (GLA_DK**-0.5)
        d_k = d_kg * ck.enG + d_kd * ck.ekd
        d_G = _add_at_last_row(d_qg * ck.qg - d_kg * ck.kg - kdd, d_gl, nh)
        shp = (2 * nh, CHUNK, CHUNK)
        bb, ii, jj = _iota3(shp, 0), _iota3(shp, 1), _iota3(shp, 2)
        tri = jnp.logical_or(jnp.logical_and(bb < nh, jj >= ii), jnp.logical_and(bb >= nh, jj <= ii))
        d_gk = _exact01(jnp.where(tri, 1.0, 0.0), d_G)
        for d, (dq_ref, dk_ref, dv_ref, dg_ref) in enumerate(((dqf_ref, dkf_ref, dvf_ref, dgf_ref), (dqb_ref, dkb_ref, dvb_ref, dgb_ref))):
            for i in range(nh):
                c = d * nh + i
                ks, vs = slice(i * GLA_DK, (i + 1) * GLA_DK), slice(i * GLA_DV, (i + 1) * GLA_DV)
                dq_ref[:, ks] = d_q[c]
                dk_ref[:, ks] = d_k[c]
                dv_ref[:, vs] = d_v[c]
                dg_ref[:, ks] = d_gk[c]

    def in_specs(ci):
        return _gla_specs(N, ci) + [
            pl.BlockSpec((1, nh, GLA_DV, GLA_DK), lambda n: (ci(n), 0, 0, 0)),
            pl.BlockSpec((CHUNK, nh * GLA_DV), lambda n: (ci(n), 0)),
        ]

    def out_specs(ci):
        ksp = pl.BlockSpec((CHUNK, nh * GLA_DK), lambda n: (ci(n), 0))
        return [ksp, ksp, pl.BlockSpec((CHUNK, nh * GLA_DV), lambda n: (ci(n), 0)), ksp]

    fw, bw = (lambda n: N - 1 - n), (lambda n: n)
    k_shape = jax.ShapeDtypeStruct((T, nh * GLA_DK), f32)
    v_shape = jax.ShapeDtypeStruct((T, nh * GLA_DV), f32)
    return pl.pallas_call(
        body,
        out_shape=(k_shape, k_shape, v_shape, k_shape) * 2,
        grid=(N,),
        in_specs=in_specs(fw) + in_specs(bw),
        out_specs=tuple(out_specs(fw) + out_specs(bw)),
        scratch_shapes=[pltpu.VMEM((2 * nh, GLA_DV, GLA_DK), f32)],
        compiler_params=_cparams(("arbitrary",)),
        name="gla_scan_bwd",
    )(proj, proj, proj, G_f, st_f, do, proj, proj, proj, G_b, st_b, do)


def _gla_dsum(dq_f, dq_b, dk_f, dk_b, dv_f, dv_b, dproj, tb=256):
    T = dq_f.shape[0]
    wk, wv = GLA_HEADS * GLA_DK, GLA_HEADS * GLA_DV

    def body(qf, qb, kf, kb, vf, vb, _, o_ref):
        o_ref[:, :wk] = (qf[...] + qb[...]).astype(o_ref.dtype)
        o_ref[:, wk : 2 * wk] = (kf[...] + kb[...]).astype(o_ref.dtype)
        o_ref[:, 2 * wk :] = (vf[...] + vb[...]).astype(o_ref.dtype)

    ksp = pl.BlockSpec((tb, wk), lambda i: (i, 0))
    vsp = pl.BlockSpec((tb, wv), lambda i: (i, 0))
    return pl.pallas_call(
        body,
        out_shape=jax.ShapeDtypeStruct(dproj.shape, dproj.dtype),
        grid=(T // tb,),
        in_specs=[ksp, ksp, ksp, ksp, vsp, vsp, _ANY_SPEC],
        out_specs=pl.BlockSpec((tb, 2 * wk + wv), lambda i: (i, DP_GLA_QKV_BLK)),
        input_output_aliases={6: 0},
        compiler_params=_cparams(("parallel",)),
        name="gla_dsum",
    )(dq_f, dq_b, dk_f, dk_b, dv_f, dv_b, dproj)


def _local_step(x, target, wcat, conv_w, avec, dvec, gdn_norm_w, w2f, w2b, b2f, b2b, gla_norm_w, proj_weights, w_pre, w_post,
                send_proj_grads=None, send_in_grads=None):
    h, h_t = _rms_pre(x, w_pre)
    proj = _matmul(h, wcat, "nt", f32, "proj_in", 2048, 512, 1024)
    qkvn = _gdn_prep(proj, conv_w)
    gates, gc = _gdn_gates(proj, avec, dvec)
    oa_f, oa_b, sa_f, sa_b, inv_f, inv_b = _gdn_fwd(qkvn, gates, gc)
    og = _out_norm(oa_f, oa_b, proj, Z_BLK, gdn_norm_w, GDN_DIM, "gdn_out")
    G_f, G_b = _gla_prep(proj, w2f, w2b, b2f, b2b)
    ob_f, ob_b, sb_f, sb_b = _gla_fwd(proj, G_f, G_b)
    obg = _out_norm(ob_f, ob_b, proj, G_B_BLK, gla_norm_w, GLA_DV, "gla_out")
    wpg, wpl, wout = proj_weights(obg)
    (loss_row, d_wpost, dy, m, dout, dya, dyb, dproj, dog, dobg) = _mid(og, obg, proj, x, target, wpg, wpl, wout, w_post)

    d_wout = _matmul(m, dout, "tn", MXU_DTYPE, "dw_out", 512, 512, 2048)
    d_wpg = _matmul(og, dya, "tn", MXU_DTYPE, "dw_proj_gdn", 512, 512, 2048)
    d_wpl = _matmul(obg, dyb, "tn", MXU_DTYPE, "dw_proj_gla", 512, 512, 2048)
    if send_proj_grads is not None:
        gla_norm_w = gla_norm_w + send_proj_grads(d_wpg, d_wpl, d_wout)

    do_b, dproj, d_gla_norm = _out_norm_bwd(dobg, ob_f, ob_b, proj, G_B_BLK, gla_norm_w, GLA_DV, "gla_out_bwd", dproj)
    dq_f, dk_f, dv_f, dgk_f, dq_b, dk_b, dv_b, dgk_b = _gla_bwd(proj, G_f, G_b, sb_f, sb_b, do_b)
    dproj = _gla_dsum(dq_f, dq_b, dk_f, dk_b, dv_f, dv_b, dproj)
    dsmall_gla, d_w2f, d_w2b, d_b2f, d_b2b = _gla_prep_bwd(proj, w2f, w2b, b2f, b2b, dgk_f, dgk_b)

    do_a, dproj, d_gdn_norm = _out_norm_bwd(dog, oa_f, oa_b, proj, Z_BLK, gdn_norm_w, GDN_DIM, "gdn_out_bwd", dproj)
    dqkv_f, dqkv_b, dg_f, dg_b = _gdn_bwd(qkvn, gates, gc, sa_f, sa_b, inv_f, inv_b, do_a)
    dproj, d_conv = _gdn_prep_bwd(proj, conv_w, dqkv_f, dqkv_b, dproj)
    dproj, d_alog, d_dtb = _gdn_gates_bwd(proj, avec, dvec, dg_f, dg_b, dsmall_gla, dproj)

    d_wcat = _matmul(h_t, dproj, "nn", MXU_DTYPE, "dw_in", 1024, 512, 2048, out_t=True)
    started = None if send_in_grads is None else send_in_grads(d_wcat)
    dh = _matmul(dproj, wcat, "nn", f32, "dh", 1024, 1024, 2432, after=started)
    grad_x, d_wpre = _rms_pre_bwd(x, w_pre, dh, dy)
    return dict(loss_row=loss_row, grad_x=grad_x, d_wcat=d_wcat, d_conv=d_conv, d_alog=d_alog, d_dtb=d_dtb,
                d_gdn_norm=d_gdn_norm, d_wpg=d_wpg, d_w2f=d_w2f, d_w2b=d_w2b, d_b2f=d_b2f, d_b2b=d_b2b,
                d_gla_norm=d_gla_norm, d_wpl=d_wpl, d_wout=d_wout, d_wpost=d_wpost, d_wpre=d_wpre)


def _to_cat(wf):
    pad = jnp.zeros((N_CAT - N_IN,) + wf.shape[1:], wf.dtype)
    return jnp.concatenate([wf[:4096], wf[4128:6176], wf[7232:9280], wf[6176:7200], wf[4096:4128], wf[7200:7232], pad], axis=0)


def _from_cat(wc):
    return jnp.concatenate([wc[:4096], wc[9216:9248], wc[4096:6144], wc[8192:9216], wc[9248:9280], wc[6144:8192]], axis=0)


def _lane_row(*pieces):
    row = jnp.concatenate(pieces, axis=1)
    return jnp.pad(row, ((0, 0), (0, LANES - row.shape[1])))


def _pad_w2(w2, row0):
    return jnp.pad(w2, ((row0, LANES - row0 - w2.shape[0]), (0, 0)))


SM_ROWS, SM_COLS = 32, 384
REP_ROWS = 32


def _pack_small_shard(conv, w2f, w2b):
    lead = conv.shape[:-2]
    z = lambda r, c: jnp.zeros(lead + (r, c), f32)
    top = jnp.concatenate([conv, z(8 - CONV_K, SM_COLS)], axis=-2)
    mid = jnp.concatenate([w2f, w2b, z(16, SM_COLS - 128)], axis=-1)
    return jnp.concatenate([top, mid, z(SM_ROWS - 24, SM_COLS)], axis=-2)


LOSS_ROW = 28


def _pack_rep(ln_pre, ln_post, b2f, b2b, gla_norm, gdn_norm, alog_f, alog_b, dt_f, dt_b, loss=None):
    last = _lane_row(alog_f, alog_b, dt_f, dt_b)
    extra = jnp.zeros((1, LANES), f32) if loss is None else _lane_row(loss)
    rows = [ln_pre.reshape(8, LANES), ln_post.reshape(8, LANES), b2f.reshape(4, LANES), b2b.reshape(4, LANES),
            gla_norm.reshape(2, LANES), gdn_norm.reshape(1, LANES), last, extra, jnp.zeros((REP_ROWS - LOSS_ROW - 1, LANES), f32)]
    return jnp.concatenate(rows, axis=0)


def _unpack_rep(p):
    return dict(ln_pre_w=p[0:8].reshape(1, 1024), ln_post_w=p[8:16].reshape(1, 1024), gk_b2_fwd=p[16:20].reshape(1, 512),
                gk_b2_bwd=p[20:24].reshape(1, 512), gla_norm_w=p[24:26].reshape(1, 256), gdn_norm_w=p[26:27],
                a_log_fwd=p[27:28, 0:8], a_log_bwd=p[27:28, 8:16], dt_bias_fwd=p[27:28, 16:24], dt_bias_bwd=p[27:28, 24:32])


_MESH = pl.DeviceIdType.MESH


N_GATHERED = 2


def _gather_weights(w_in_s, sm_s):
    shapes = (w_in_s.shape, sm_s.shape)
    dtypes = (MXU_DTYPE, f32)

    def body(win_ref, sm_ref, gin_ref, gsm_ref, send_sems, recv_sems):
        x, y, c = lax.axis_index("x"), lax.axis_index("y"), lax.axis_index("c")
        me, sibling = (x, y, c), (x, y, 1 - c)
        chips = [(1 - x, y), (x, 1 - y), (1 - x, 1 - y)]
        bufs = (gin_ref, gsm_ref)

        def idx(px, py, pc):
            return 4 * px + 2 * py + pc

        for r in range(w_in_s.shape[1] // LANES):
            cols = slice(r * LANES, (r + 1) * LANES)
            gin_ref[idx(*me), :, cols] = win_ref[:, cols].astype(MXU_DTYPE)
        gsm_ref[idx(*me)] = sm_ref[...]

        def copy(a, k, block, to):
            blk = bufs[a].at[idx(*block)]
            return pltpu.make_async_remote_copy(src_ref=blk, dst_ref=blk, send_sem=send_sems.at[7 * a + k],
                                                recv_sem=recv_sems.at[7 * a + k], device_id=to, device_id_type=_MESH)

        first, passed = [], []
        for a in range(N_GATHERED):
            first.append(copy(a, 0, me, sibling))
            first += [copy(a, 1 + j, me, (*chip, c)) for j, chip in enumerate(chips)]
        for cp in first:
            cp.start()
        for j, chip in enumerate(chips):
            for a in range(N_GATHERED):
                copy(a, 1 + j, (*chip, c), me).wait_recv()
                cp = copy(a, 4 + j, (*chip, c), sibling)
                cp.start()
                passed.append(cp)
        for a in range(N_GATHERED):
            copy(a, 0, sibling, me).wait_recv()
            for j, chip in enumerate(chips):
                copy(a, 4 + j, (*chip, 1 - c), me).wait_recv()
        for cp in first + passed:
            cp.wait_send()

    vm = pl.BlockSpec(memory_space=pltpu.VMEM)
    return pl.pallas_call(
        body,
        out_shape=tuple(jax.ShapeDtypeStruct((N_DEV,) + s, d) for s, d in zip(shapes, dtypes)),
        in_specs=[vm, vm],
        out_specs=(vm, vm),
        scratch_shapes=[pltpu.SemaphoreType.DMA((7 * N_GATHERED,)), pltpu.SemaphoreType.DMA((7 * N_GATHERED,))],
        compiler_params=pltpu.CompilerParams(vmem_limit_bytes=VMEM_LIMIT),
        name="gather_weights",
    )(w_in_s, sm_s)


N_CHIP = 4
_EFFECT = pltpu.SideEffectType.DATAFLOW_SIDE_EFFECTING
_HBM_SPEC = pl.BlockSpec(memory_space=pltpu.HBM)
_SEM_SPEC = pl.BlockSpec(memory_space=pltpu.SEMAPHORE)


def _relation_peer(k, x, y, c):
    px = (1 - x) if (k & 4) else x
    py = (1 - y) if (k & 2) else y
    pc = (1 - c) if (k & 1) else c
    return (px, py, pc), 4 * px + 2 * py + pc


_PLAN_COPIES = {"gather": N_DEV - 1, "blocks": N_DEV - 1, "sibling": 4, "chips": 3}
_PLAN_SLOTS = {"gather": N_DEV, "blocks": N_DEV, "sibling": 4, "chips": 4}


def _plan_copies(plan, src_ref, land_ref, send_sems, recv_sems, arrival):
    x, y, c = lax.axis_index("x"), lax.axis_index("y"), lax.axis_index("c")
    me, my_chip = 4 * x + 2 * y + c, 2 * x + y
    out = []

    def add(i, src, there, here, dev):
        out.append(pltpu.make_async_remote_copy(src_ref=src, dst_ref=land_ref.at[here if arrival else there], send_sem=send_sems.at[i],
                                                recv_sem=recv_sems.at[i], device_id=dev, device_id_type=_MESH))

    if plan in ("gather", "blocks"):
        for k in range(1, N_DEV):
            dev, p = _relation_peer(k, x, y, c)
            add(k - 1, src_ref.at[p] if plan == "blocks" else src_ref, me, p, dev)
    elif plan == "sibling":
        for j in range(4):
            add(j, src_ref.at[2 * j + 1 - c], j, j, (x, y, 1 - c))
    else:
        for i, k in enumerate((2, 4, 6)):
            (px, py, pc), _ = _relation_peer(k, x, y, c)
            add(i, src_ref.at[2 * px + py], my_chip, 2 * px + py, (px, py, pc))
    return out


def _exchange_start(src, plan, name):
    n = _PLAN_COPIES[plan]
    land = lax.empty((_PLAN_SLOTS[plan],) + (src.shape if plan == "gather" else src.shape[1:]), src.dtype)

    def body(src_ref, land_ref, send_sems, recv_sems, src_thru, land_thru, token):
        for cp in _plan_copies(plan, src_ref, land_ref, send_sems, recv_sems, False):
            cp.start()
        token[...] = jnp.zeros_like(token)

    return pl.pallas_call(
        body,
        name=name,
        out_shape=(pltpu.SemaphoreType.DMA((n,)), pltpu.SemaphoreType.DMA((n,)), pltpu.HBM(src.shape, src.dtype),
                   pltpu.HBM(land.shape, land.dtype), jax.ShapeDtypeStruct((8, LANES), f32)),
        in_specs=(_HBM_SPEC, _HBM_SPEC),
        out_specs=(_SEM_SPEC, _SEM_SPEC, _HBM_SPEC, _HBM_SPEC, pl.BlockSpec(memory_space=pltpu.VMEM)),
        input_output_aliases={0: 2, 1: 3},
        compiler_params=pltpu.CompilerParams(has_side_effects=_EFFECT),
    )(pltpu.with_memory_space_constraint(src, pltpu.HBM), pltpu.with_memory_space_constraint(land, pltpu.HBM))


def _exchange_wait(started, after, plan, name):
    send_sems, recv_sems, src_thru, land_thru, _ = started

    def body(src_ref, land_ref, send_sems, recv_sems, after_ref, src_dead, got_ref):
        for cp in _plan_copies(plan, src_ref, land_ref, send_sems, recv_sems, True):
            cp.wait_send()
            cp.wait_recv()

    return pl.pallas_call(
        body,
        name=name,
        out_shape=(pltpu.HBM(src_thru.shape, src_thru.dtype), pltpu.HBM(land_thru.shape, land_thru.dtype)),
        in_specs=(_HBM_SPEC, _HBM_SPEC, _SEM_SPEC, _SEM_SPEC, pl.BlockSpec(memory_space=pl.ANY)),
        out_specs=(_HBM_SPEC, _HBM_SPEC),
        input_output_aliases={0: 0, 1: 1},
        compiler_params=pltpu.CompilerParams(has_side_effects=_EFFECT),
    )(src_thru, land_thru, send_sems, recv_sems, after)


def _pair_sum(g, r, name, tc=LANES):
    _, R, Cc = g.shape

    def body(g_ref, r_ref, o_ref):
        o_ref[...] = (g_ref[...].astype(f32) + r_ref[...].astype(f32)).astype(o_ref.dtype)

    return pl.pallas_call(
        body,
        out_shape=jax.ShapeDtypeStruct(r.shape, r.dtype),
        grid=(N_CHIP, Cc // tc),
        in_specs=[pl.BlockSpec((1, R, tc), lambda k, i: (2 * k + lax.axis_index("c"), 0, i)), pl.BlockSpec((1, R, tc), lambda k, i: (k, 0, i))],
        out_specs=pl.BlockSpec((1, R, tc), lambda k, i: (k, 0, i)),
        compiler_params=_cparams(("parallel", "parallel")),
        name=name,
    )(g, r)


def _exchange_small(g_sm, g_rep):
    srcs = (g_sm, g_rep)

    def body(in0, in1, out0, out1, send_sems, recv_sems, local_sems):
        x, y, c = lax.axis_index("x"), lax.axis_index("y"), lax.axis_index("c")
        me = 4 * x + 2 * y + c
        ins, outs = (in0, in1), (out0, out1)

        def copy(a, k, arrival):
            dev, p = _relation_peer(k, x, y, c)
            return pltpu.make_async_remote_copy(src_ref=ins[a].at[p] if a == 0 else ins[a], dst_ref=outs[a].at[p if arrival else me],
                                                send_sem=send_sems.at[7 * a + k - 1], recv_sem=recv_sems.at[7 * a + k - 1],
                                                device_id=dev, device_id_type=_MESH)

        pairs = [(a, k) for a in (0, 1) for k in range(1, N_DEV)]
        sends = [copy(a, k, False) for a, k in pairs]
        for cp in sends:
            cp.start()
        local = [pltpu.make_async_copy(ins[0].at[me], outs[0].at[me], local_sems.at[0]),
                 pltpu.make_async_copy(ins[1], outs[1].at[me], local_sems.at[1])]
        for cp in local:
            cp.start()
        for a, k in pairs:
            copy(a, k, True).wait_recv()
        for cp in sends:
            cp.wait_send()
        for cp in local:
            cp.wait()

    return pl.pallas_call(
        body,
        out_shape=(jax.ShapeDtypeStruct(g_sm.shape, g_sm.dtype), jax.ShapeDtypeStruct((N_DEV,) + g_rep.shape, g_rep.dtype)),
        in_specs=[_HBM_SPEC] * 2,
        out_specs=(_HBM_SPEC,) * 2,
        scratch_shapes=[pltpu.SemaphoreType.DMA((14,)), pltpu.SemaphoreType.DMA((14,)), pltpu.SemaphoreType.DMA((2,))],
        name="exchange_small",
    )(*srcs)


def _sum_adam(parts, w, m, v, name, tb, tc=None, own=None, own_slot=None):
    R, Cc = w.shape
    n_parts = parts.shape[0]
    tb = R if tc else min(tb, R)
    tc = tc or Cc
    assert R % tb == 0 and Cc % tc == 0
    bc1 = 1.0 - ADAM_B1**ADAM_STEP
    bc2 = 1.0 - ADAM_B2**ADAM_STEP
    extra = [] if own is None else [own]

    def body(p_ref, w_ref, m_ref, v_ref, *rest):
        g_ref, d_ref, nm_ref, nv_ref = rest[len(extra) :]

        def part(j):
            pj = p_ref[j].astype(f32)
            return pj if own is None else jnp.where(own_slot() == j, rest[0][0].astype(f32), pj)

        g = part(0)
        for j in range(1, n_parts):
            g = g + part(j)
        g_ref[...] = g
        m2 = ADAM_B1 * m_ref[...] + (1.0 - ADAM_B1) * g
        v2 = ADAM_B2 * v_ref[...] + (1.0 - ADAM_B2) * (g * g)
        nm_ref[...] = m2
        nv_ref[...] = v2
        d_ref[...] = -ADAM_LR * ((m2 / bc1) / (jnp.sqrt(v2 / bc2) + ADAM_EPS) + ADAM_WD * w_ref[...])

    blk = pl.BlockSpec((tb, tc), lambda i, j: (i, j))
    o = jax.ShapeDtypeStruct((R, Cc), f32)
    return pl.pallas_call(
        body,
        out_shape=(o, o, o, o),
        grid=(R // tb, Cc // tc),
        in_specs=[pl.BlockSpec((n_parts, tb, tc), lambda i, j: (0, i, j)), blk, blk, blk]
        + [pl.BlockSpec((1, tb, tc), lambda i, j: (own_slot(), i, j))] * len(extra),
        out_specs=(blk, blk, blk, blk),
        compiler_params=_cparams(("parallel", "parallel")),
        name=name,
    )(parts, w, m, v, *extra)


def _my_device():
    return 4 * lax.axis_index("x") + 2 * lax.axis_index("y") + lax.axis_index("c")


def _my_chip():
    return 2 * lax.axis_index("x") + lax.axis_index("y")


_WEIGHTS = ["ln_pre_w", "w_in", "conv_w", "a_log_fwd", "a_log_bwd", "dt_bias_fwd", "dt_bias_bwd", "gdn_norm_w", "w_proj_gdn",
            "gk_w2_fwd", "gk_b2_fwd", "gk_w2_bwd", "gk_b2_bwd", "gla_norm_w", "w_proj_gla", "w_out", "ln_post_w"]
_REP_ORDER = ["ln_pre_w", "ln_post_w", "gk_b2_fwd", "gk_b2_bwd", "gla_norm_w", "gdn_norm_w", "a_log_fwd", "a_log_bwd", "dt_bias_fwd", "dt_bias_bwd"]


def kernel(x, ln_pre_w, w_in, conv_w, a_log_fwd, a_log_bwd, dt_bias_fwd, dt_bias_bwd, gdn_norm_w, w_proj_gdn, gk_w2_fwd, gk_b2_fwd, gk_w2_bwd, gk_b2_bwd, gla_norm_w, w_proj_gla, w_out, ln_post_w, loss_target, m_ln_pre_w, m_w_in, m_conv_w, m_a_log_fwd, m_a_log_bwd, m_dt_bias_fwd, m_dt_bias_bwd, m_gdn_norm_w, m_w_proj_gdn, m_gk_w2_fwd, m_gk_b2_fwd, m_gk_w2_bwd, m_gk_b2_bwd, m_gla_norm_w, m_w_proj_gla, m_w_out, m_ln_post_w, v_ln_pre_w, v_w_in, v_conv_w, v_a_log_fwd, v_a_log_bwd, v_dt_bias_fwd, v_dt_bias_bwd, v_gdn_norm_w, v_w_proj_gdn, v_gk_w2_fwd, v_gk_b2_fwd, v_gk_w2_bwd, v_gk_b2_bwd, v_gla_norm_w, v_w_proj_gla, v_w_out, v_ln_post_w):
    args = locals()
    W = {n: args[n] for n in _WEIGHTS}
    M = {n: args["m_" + n] for n in _WEIGHTS}
    V = {n: args["v_" + n] for n in _WEIGHTS}

    wp_stack = lambda P: jnp.concatenate([P["w_proj_gdn"], P["w_proj_gla"], P["w_out"]], axis=0)
    sm_pack = lambda P: _pack_small_shard(P["conv_w"][0], P["gk_w2_fwd"][0], P["gk_w2_bwd"][0])
    me = 4 * lax.axis_index("x") + 2 * lax.axis_index("y") + lax.axis_index("c")
    w_in_t, m_in_t, v_in_t = (jnp.transpose(a[0]) for a in (w_in, m_w_in, v_w_in))
    g_in, g_sm = _gather_weights(w_in_t, sm_pack(W))
    wp_own = wp_stack(W).astype(MXU_DTYPE)
    wp_started = _exchange_start(wp_own, "gather", "gather_proj_start")

    def proj_weights(after):
        own, land = _exchange_wait(wp_started, after, "gather", "gather_proj_wait")
        g_p = lax.dynamic_update_slice(land, own[None], (me, 0, 0, 0))
        return tuple(g_p[:, i].reshape(D_MODEL, D_MODEL) for i in range(3))

    rows_p = 3 * D_MODEL // N_DEV
    sent = {}

    def send_proj_grads(d_wpg, d_wpl, d_wout):
        p_p = jnp.stack([d_wpg, d_wpl, d_wout]).reshape(3, N_DEV, D_MODEL // N_DEV, D_MODEL).transpose(1, 0, 2, 3)
        sent["p_p"] = p_p.reshape(N_DEV, rows_p, D_MODEL)
        sent["started"] = _exchange_start(sent["p_p"], "blocks", "exchange_proj_start")
        return sent["started"][4][0:1, 0:1]

    def send_in_grads(d_wcat):
        p_in = _from_cat(d_wcat).reshape(N_DEV, SHARD_IN, D_MODEL)
        sib = _exchange_start(p_in, "sibling", "exchange_in_sibling_start")
        p_p, land_p = _exchange_wait(sent["started"], sib[4], "blocks", "exchange_proj_wait")
        sent["o_p"] = _sum_adam(land_p, wp_stack(W).reshape(rows_p, D_MODEL), wp_stack(M).reshape(rows_p, D_MODEL),
                                wp_stack(V).reshape(rows_p, D_MODEL), "adam_w_proj", 128, own=p_p, own_slot=_my_device)
        p_in, sib_land = _exchange_wait(sib, sent["o_p"][0], "sibling", "exchange_in_sibling_wait")
        sent["chips"] = _exchange_start(_pair_sum(p_in, sib_land, "pair_sum_w_in", tc=512), "chips", "exchange_in_chips_start")
        return sent["chips"][4]

    wcat = _to_cat(g_in.reshape(N_IN, D_MODEL))
    conv_full = g_sm[:, 0:CONV_K, :].transpose(1, 0, 2).reshape(CONV_K, N_DEV * SM_COLS)
    w2f_full = g_sm[:, 8:24, 0:64].transpose(1, 0, 2).reshape(16, 512)
    w2b_full = g_sm[:, 8:24, 64:128].transpose(1, 0, 2).reshape(16, 512)

    avec = _lane_row(a_log_fwd, a_log_bwd)
    dvec = _lane_row(dt_bias_fwd, dt_bias_bwd)
    G = _local_step(x[0], loss_target[0], wcat, conv_full, avec, dvec, gdn_norm_w, _pad_w2(w2f_full, 32), _pad_w2(w2b_full, 48),
                    gk_b2_fwd, gk_b2_bwd, gla_norm_w, proj_weights, ln_pre_w + wp_started[4][0:1, 0:1], ln_post_w, send_proj_grads,
                    send_in_grads)

    p_sm = _pack_small_shard(G["d_conv"].reshape(CONV_K, N_DEV, SM_COLS).transpose(1, 0, 2),
                             G["d_w2f"][32:48].reshape(16, N_DEV, 64).transpose(1, 0, 2),
                             G["d_w2b"][48:64].reshape(16, N_DEV, 64).transpose(1, 0, 2))
    p_rep = _pack_rep(G["d_wpre"], G["d_wpost"], G["d_b2f"], G["d_b2b"], G["d_gla_norm"], G["d_gdn_norm"],
                      G["d_alog"][:, 0:8], G["d_alog"][:, 8:16], G["d_dtb"][:, 0:8], G["d_dtb"][:, 8:16],
                      loss=0.5 * jnp.sum(G["loss_row"], axis=1, keepdims=True) / D_MODEL)
    r_sm, r_rep = _exchange_small(p_sm, p_rep)
    q_in, land_in = _exchange_wait(sent["chips"], r_rep, "chips", "exchange_in_chips_wait")
    o_in = _sum_adam(land_in, w_in_t, m_in_t, v_in_t, "adam_w_in", SHARD_IN, tc=256, own=q_in, own_slot=_my_chip)
    o_p = sent["o_p"]
    o_sm = _sum_adam(r_sm, sm_pack(W), sm_pack(M), sm_pack(V), "adam_small", SM_ROWS)
    rep_pack = lambda P: _pack_rep(*[P[n] for n in _REP_ORDER])
    o_rep = _sum_adam(r_rep, rep_pack(W), rep_pack(M), rep_pack(V), "adam_rep", REP_ROWS)

    res = []
    for kind in range(4):
        rep = _unpack_rep(o_rep[kind])
        per = dict(rep)
        per["w_in"] = jnp.transpose(o_in[kind])[None]
        pp = o_p[kind].reshape(3, 1, D_MODEL // N_DEV, D_MODEL)
        per["w_proj_gdn"], per["w_proj_gla"], per["w_out"] = pp[0], pp[1], pp[2]
        per["conv_w"] = o_sm[kind][None, 0:CONV_K, :]
        per["gk_w2_fwd"] = o_sm[kind][None, 8:24, 0:64]
        per["gk_w2_bwd"] = o_sm[kind][None, 8:24, 64:128]
        res.append([per[n] for n in _WEIGHTS])
    loss = o_rep[0][LOSS_ROW, 0]
    return (loss, G["grad_x"][None], *res[0], *res[1], *res[2], *res[3])
```

```python
import functools
import math

import jax
import jax.numpy as jnp
from jax import lax
from jax.experimental import pallas as pl
from jax.experimental.pallas import tpu as pltpu

f32 = jnp.float32
MXU_DTYPE = jnp.bfloat16
HI = lax.Precision.HIGHEST

D_MODEL = 1024
CHUNK = 64
NORM_EPS = 1e-6
GDN_HEADS, GDN_DIM = 8, 128
GLA_HEADS, GLA_DK, GLA_DV = 4, 128, 256
GATE_NORMALIZER = 16.0
CONV_K = 5
N_IN = 9280
N_DEV = 8
SHARD_IN = N_IN // N_DEV

N_CAT = 9728
SMALL_OFF = 9216
LANES = 128

ADAM_LR, ADAM_B1, ADAM_B2, ADAM_EPS, ADAM_WD, ADAM_STEP = 0.001, 0.9, 0.999, 1e-08, 0.01, 10

VMEM_LIMIT = 56 * 1024 * 1024


def _cparams(sem=None):
    return pltpu.CompilerParams(dimension_semantics=sem, vmem_limit_bytes=VMEM_LIMIT)


_DN = {"nn": (((1,), (0,)), ((), ())), "nt": (((1,), (1,)), ((), ())), "tn": (((0,), (0,)), ((), ()))}
_BDN = {"nn": (((2,), (1,)), ((0,), (0,))), "nt": (((2,), (2,)), ((0,), (0,))), "tn": (((1,), (1,)), ((0,), (0,)))}


def _dot16(a, b, kind):
    return lax.dot_general(a, b, (_DN if a.ndim == 2 else _BDN)[kind], preferred_element_type=f32)


def _mx(a, b, kind):
    return _dot16(a.astype(MXU_DTYPE), b.astype(MXU_DTYPE), kind)


def _split2(a):
    hi = a.astype(jnp.bfloat16)
    return hi, (a - hi.astype(f32)).astype(jnp.bfloat16)


def _split3(a):
    p1 = a.astype(jnp.bfloat16)
    r = a - p1.astype(f32)
    p2 = r.astype(jnp.bfloat16)
    return p1, p2, (r - p2.astype(f32)).astype(jnp.bfloat16)


def _h3s(a2, b2, kind):
    (ah, al), (bh, bl) = a2, b2
    return _dot16(ah, bh, kind) + (_dot16(ah, bl, kind) + _dot16(al, bh, kind))


def _h3(a, b, kind):
    return _h3s(_split2(a), _split2(b), kind)


def _exact01(t01, x, kind="nn"):
    t = t01.astype(jnp.bfloat16)
    x1, x2, x3 = _split3(x)
    return _dot16(t, x1, kind) + (_dot16(t, x2, kind) + _dot16(t, x3, kind))


def _colsum_col(e):
    ones = jnp.ones(e.shape[:-1] + (LANES,), jnp.bfloat16)
    e1, e2, e3 = _split3(e)
    return (_dot16(e1, ones, "tn") + (_dot16(e2, ones, "tn") + _dot16(e3, ones, "tn")))[..., 0:1]


def _iota3(shape, axis):
    return lax.broadcasted_iota(jnp.int32, shape, axis)


def _dir_masks(B, nf, Cn):
    shp = (B, Cn, Cn)
    bb, ii, jj = _iota3(shp, 0), _iota3(shp, 1), _iota3(shp, 2)
    fwd = bb < nf
    rev = jnp.logical_not(fwd)
    incl = jnp.logical_or(jnp.logical_and(fwd, ii >= jj), jnp.logical_and(rev, ii <= jj))
    strict = jnp.logical_and(incl, ii != jj)
    return incl, strict, ii == jj


def _last_row(x, nf):
    B, Cn = x.shape[0], x.shape[1]
    fwd = _iota3((B, 1, 1), 0) < nf
    return jnp.where(fwd, x[:, Cn - 1 : Cn, :], x[:, 0:1, :])


def _add_at_last_row(x, val, nf):
    B, Cn = x.shape[0], x.shape[1]
    bb, rr = _iota3((B, Cn, 1), 0), _iota3((B, Cn, 1), 1)
    at = jnp.logical_or(jnp.logical_and(bb < nf, rr == Cn - 1), jnp.logical_and(bb >= nf, rr == 0))
    return x + jnp.where(at, val, 0.0)


def _sigmoid(x):
    return jax.nn.sigmoid(x)


def _silu(x):
    return x * _sigmoid(x)


def _silu_grad(x):
    s = _sigmoid(x)
    return s * (1.0 + x * (1.0 - s))


def _softplus(x):
    u = jnp.exp(-jnp.abs(x))
    l1p = jnp.where(u < 1e-3, u * (1.0 - u * (0.5 - u * (1.0 / 3.0))), jnp.log(1.0 + u))
    return jnp.maximum(x, 0.0) + l1p


def _mxr(x):
    return x.astype(MXU_DTYPE).astype(f32)


def _iota2(shape, axis):
    return lax.broadcasted_iota(jnp.int32, shape, axis)


def _matmul(a, b, kind, out_dtype, name, tm, tn, tk, out_t=False, after=None):
    extra = [] if after is None else [after]
    if kind == "nn":
        (M, K), N = a.shape, b.shape[1]
    elif kind == "nt":
        (M, K), N = a.shape, b.shape[0]
    else:
        (K, M), N = a.shape, b.shape[1]
    tm, tn, tk = min(tm, M), min(tn, N), min(tk, K)
    assert M % tm == 0 and N % tn == 0 and K % tk == 0, (name, M, N, K)
    nk = K // tk
    if kind == "tn":
        a_spec = pl.BlockSpec((tk, tm), lambda i, j, k: (k, i))
    else:
        a_spec = pl.BlockSpec((tm, tk), lambda i, j, k: (i, k))
    if kind == "nt":
        b_spec = pl.BlockSpec((tn, tk), lambda i, j, k: (j, k))
    else:
        b_spec = pl.BlockSpec((tk, tn), lambda i, j, k: (k, j))

    def body(a_ref, b_ref, *rest):
        o_ref, acc = rest[len(extra)], rest[len(extra) + 1 :]
        p = _mx(a_ref[...], b_ref[...], kind)
        if nk == 1:
            o_ref[...] = (p.T if out_t else p).astype(out_dtype)
        else:
            assert not out_t
            acc_ref = acc[0]
            k = pl.program_id(2)

            @pl.when(k == 0)
            def _():
                acc_ref[...] = p

            @pl.when(k > 0)
            def _():
                acc_ref[...] += p

            @pl.when(k == nk - 1)
            def _():
                o_ref[...] = acc_ref[...].astype(out_dtype)

    return pl.pallas_call(
        body,
        out_shape=jax.ShapeDtypeStruct((N, M) if out_t else (M, N), out_dtype),
        grid=(M // tm, N // tn, nk),
        in_specs=[a_spec, b_spec] + [pl.BlockSpec((8, LANES), lambda i, j, k: (0, 0))] * len(extra),
        out_specs=pl.BlockSpec((tn, tm), lambda i, j, k: (j, i)) if out_t else pl.BlockSpec((tm, tn), lambda i, j, k: (i, j)),
        scratch_shapes=[] if nk == 1 else [pltpu.VMEM((tm, tn), f32)],
        compiler_params=_cparams(("parallel", "parallel", "arbitrary")),
        name=name,
    )(a, b, *extra)


def _rms_pre(x, w, tb=256):
    T = x.shape[0]

    def body(x_ref, w_ref, h_ref, ht_ref):
        xv = x_ref[...]
        r = lax.rsqrt(jnp.mean(xv * xv, axis=1, keepdims=True) + NORM_EPS)
        h = xv * r * w_ref[...]
        h_ref[...] = h.astype(h_ref.dtype)
        ht_ref[...] = h.T.astype(ht_ref.dtype)

    return pl.pallas_call(
        body,
        out_shape=(jax.ShapeDtypeStruct((T, D_MODEL), MXU_DTYPE), jax.ShapeDtypeStruct((D_MODEL, T), MXU_DTYPE)),
        grid=(T // tb,),
        in_specs=[pl.BlockSpec((tb, D_MODEL), lambda i: (i, 0)), pl.BlockSpec((1, D_MODEL), lambda i: (0, 0))],
        out_specs=(pl.BlockSpec((tb, D_MODEL), lambda i: (i, 0)), pl.BlockSpec((D_MODEL, tb), lambda i: (0, i))),
        compiler_params=_cparams(("parallel",)),
        name="rms_pre",
    )(x, w)


def _rms_pre_bwd(x, w, dh, dy, tb=256):
    T = x.shape[0]

    def body(x_ref, w_ref, dh_ref, dy_ref, dx_ref, dw_ref):
        i = pl.program_id(0)
        xv, dhv = x_ref[...], dh_ref[...]
        r = lax.rsqrt(jnp.mean(xv * xv, axis=1, keepdims=True) + NORM_EPS)
        dhw = dhv * w_ref[...]
        dx_ref[...] = dy_ref[...] + r * dhw - xv * (r * r * r) * jnp.mean(dhw * xv, axis=1, keepdims=True)
        part = jnp.sum(dhv * xv * r, axis=0, keepdims=True)

        @pl.when(i == 0)
        def _():
            dw_ref[...] = part

        @pl.when(i > 0)
        def _():
            dw_ref[...] += part

    blk = pl.BlockSpec((tb, D_MODEL), lambda i: (i, 0))
    row = pl.BlockSpec((1, D_MODEL), lambda i: (0, 0))
    return pl.pallas_call(
        body,
        out_shape=(jax.ShapeDtypeStruct((T, D_MODEL), f32), jax.ShapeDtypeStruct((1, D_MODEL), f32)),
        grid=(T // tb,),
        in_specs=[blk, row, blk, blk],
        out_specs=(blk, row),
        compiler_params=_cparams(("arbitrary",)),
        name="rms_pre_bwd",
    )(x, w, dh, dy)


def _conv5(u, cw):
    T = u.shape[0]
    t = _iota2(u.shape, 0)
    acc = u * cw[2:3, :]
    for j in (0, 1, 3, 4):
        d = j - 2
        sh = pltpu.roll(u, (-d) % T, 0)
        valid = jnp.logical_and(t + d >= 0, t + d < T)
        acc = acc + jnp.where(valid, sh, 0.0) * cw[j : j + 1, :]
    return acc


def _gdn_prep(proj, conv_w):
    T = proj.shape[0]
    nh = GDN_HEADS

    def body(u_ref, cw_ref, o_ref):
        j = pl.program_id(0)
        s = _silu(_conv5(_mxr(u_ref[...]), _mxr(cw_ref[...])))

        @pl.when(j < 2 * nh)
        def _():
            o_ref[0] = s * lax.rsqrt(jnp.sum(s * s, axis=1, keepdims=True) + NORM_EPS)

        @pl.when(j >= 2 * nh)
        def _():
            o_ref[0] = s

    return pl.pallas_call(
        body,
        out_shape=jax.ShapeDtypeStruct((3, T, GDN_HEADS * GDN_DIM), f32),
        grid=(3 * nh,),
        in_specs=[pl.BlockSpec((T, GDN_DIM), lambda j: (0, j)), pl.BlockSpec((CONV_K, GDN_DIM), lambda j: (0, j))],
        out_specs=pl.BlockSpec((1, T, GDN_DIM), lambda j: (j // nh, 0, j % nh)),
        compiler_params=_cparams(("parallel",)),
        name="gdn_prep",
    )(proj, conv_w)


def _gdn_prep_bwd(proj, conv_w, dqkv_f, dqkv_b, dproj):
    T = proj.shape[0]
    nh = GDN_HEADS

    def body(u_ref, cw_ref, df_ref, db_ref, _, du_ref, dcw_ref):
        j = pl.program_id(0)
        u, cw = _mxr(u_ref[...]), _mxr(cw_ref[...])
        c = _conv5(u, cw)
        s = _silu(c)
        dn = df_ref[0] + db_ref[0]
        rinv = lax.rsqrt(jnp.sum(s * s, axis=1, keepdims=True) + NORM_EPS)
        ds_norm = rinv * dn - s * (rinv * rinv * rinv) * jnp.sum(dn * s, axis=1, keepdims=True)
        ds = jnp.where(j < 2 * nh, ds_norm, dn)
        dc = _mxr(ds * _silu_grad(c))
        t = _iota2(u.shape, 0)
        du = dc * cw[2:3, :]
        rows = []
        for jj in range(CONV_K):
            d = jj - 2
            if d == 0:
                rows.append(jnp.sum(dc * u, axis=0, keepdims=True))
                continue
            ush = pltpu.roll(u, (-d) % T, 0)
            uvalid = jnp.logical_and(t + d >= 0, t + d < T)
            rows.append(jnp.sum(dc * jnp.where(uvalid, ush, 0.0), axis=0, keepdims=True))
            dsh = pltpu.roll(dc, d % T, 0)
            dvalid = jnp.logical_and(t - d >= 0, t - d < T)
            du = du + jnp.where(dvalid, dsh, 0.0) * cw[jj : jj + 1, :]
        du_ref[...] = du.astype(du_ref.dtype)
        for jj in range(CONV_K):
            dcw_ref[jj : jj + 1, :] = rows[jj]

    qspec = pl.BlockSpec((1, T, GDN_DIM), lambda j: (j // nh, 0, j % nh))
    return pl.pallas_call(
        body,
        out_shape=(jax.ShapeDtypeStruct(dproj.shape, dproj.dtype), jax.ShapeDtypeStruct((CONV_K, 3 * nh * GDN_DIM), f32)),
        grid=(3 * nh,),
        in_specs=[pl.BlockSpec((T, GDN_DIM), lambda j: (0, j)), pl.BlockSpec((CONV_K, GDN_DIM), lambda j: (0, j)), qspec, qspec, _ANY_SPEC],
        out_specs=(pl.BlockSpec((T, GDN_DIM), lambda j: (0, j)), pl.BlockSpec((CONV_K, GDN_DIM), lambda j: (0, j))),
        input_output_aliases={4: 0},
        compiler_params=_cparams(("parallel",)),
        name="gdn_prep_bwd",
    )(proj, conv_w, dqkv_f, dqkv_b, dproj)


def _chunk_tri(n, rev):
    i, j = _iota2((n, n), 0), _iota2((n, n), 1)
    same = jnp.right_shift(i, 6) == jnp.right_shift(j, 6)
    order = (j >= i) if rev else (j <= i)
    return jnp.where(jnp.logical_and(same, order), 1.0, 0.0).astype(f32)


def _gdn_gates(proj, avec, dvec, tb=256):
    T = proj.shape[0]

    def body(s_ref, a_ref, d_ref, g_ref, gc_ref):
        small = s_ref[...]
        lane = _iota2(small.shape, 1)
        lg = -jnp.exp(a_ref[...]) * _softplus(small + d_ref[...])
        beta = _sigmoid(small)
        g_ref[...] = jnp.where(lane < 16, lg, jnp.where(lane < 32, beta, 0.0))
        lgm = jnp.where(lane < 16, lg, 0.0)
        gcf = _exact01(_chunk_tri(tb, False), lgm)
        gcr = _exact01(_chunk_tri(tb, True), lgm)
        gc_ref[...] = jnp.where(lane < 8, gcf, gcr)

    blk = pl.BlockSpec((tb, LANES), lambda i: (i, 0))
    row = pl.BlockSpec((1, LANES), lambda i: (0, 0))
    return pl.pallas_call(
        body,
        out_shape=(jax.ShapeDtypeStruct((T, LANES), f32), jax.ShapeDtypeStruct((T, LANES), f32)),
        grid=(T // tb,),
        in_specs=[pl.BlockSpec((tb, LANES), lambda i: (i, SMALL_OFF // LANES)), row, row],
        out_specs=(blk, blk),
        compiler_params=_cparams(("parallel",)),
        name="gdn_gates",
    )(proj, avec, dvec)


def _gdn_gates_bwd(proj, avec, dvec, dg_f, dg_b, dsmall_gla, dproj, tb=256):
    T = proj.shape[0]
    pad = N_CAT - SMALL_OFF

    def body(s_ref, a_ref, d_ref, gf_ref, gb_ref, dl_ref, _, ds_ref, da_ref, dd_ref):
        i = pl.program_id(0)
        small = s_ref[...]
        lane = _iota2(small.shape, 1)
        dgate = gf_ref[...] + gb_ref[...]
        z = small + d_ref[...]
        nega = -jnp.exp(a_ref[...])
        dz = dgate * nega * _sigmoid(z)
        beta = _sigmoid(small)
        dsm = jnp.where(lane < 16, dz, jnp.where(lane < 32, dgate * beta * (1.0 - beta), 0.0))
        ds_ref[:, :LANES] = (dsm + dl_ref[...]).astype(ds_ref.dtype)
        ds_ref[:, LANES:] = jnp.zeros((tb, pad - LANES), ds_ref.dtype)
        lg = nega * _softplus(z)
        pa = jnp.sum(jnp.where(lane < 16, dgate * lg, 0.0), axis=0, keepdims=True)
        pd = jnp.sum(jnp.where(lane < 16, dz, 0.0), axis=0, keepdims=True)

        @pl.when(i == 0)
        def _():
            da_ref[...] = pa
            dd_ref[...] = pd

        @pl.when(i > 0)
        def _():
            da_ref[...] += pa
            dd_ref[...] += pd

    blk = pl.BlockSpec((tb, LANES), lambda i: (i, 0))
    row = pl.BlockSpec((1, LANES), lambda i: (0, 0))
    return pl.pallas_call(
        body,
        out_shape=(jax.ShapeDtypeStruct(dproj.shape, dproj.dtype), jax.ShapeDtypeStruct((1, LANES), f32), jax.ShapeDtypeStruct((1, LANES), f32)),
        grid=(T // tb,),
        in_specs=[pl.BlockSpec((tb, LANES), lambda i: (i, SMALL_OFF // LANES)), row, row, blk, blk, blk, _ANY_SPEC],
        out_specs=(pl.BlockSpec((tb, pad), lambda i: (i, DP_SMALL_BLK)), row, row),
        input_output_aliases={6: 0},
        compiler_params=_cparams(("arbitrary",)),
        name="gdn_gates_bwd",
    )(proj, avec, dvec, dg_f, dg_b, dsmall_gla, dproj)


GDN_HB = 8


def _inv_unit_lower(L, eye):
    A = eye - L
    P2 = _split2(L)
    for _ in range(3):
        P2 = _split2(_h3s(P2, P2, "nn"))
        A = A + _h3s(_split2(A), P2, "nn")
    Ph = P2[0]
    for _ in range(2):
        Ph = _dot16(Ph, Ph, "nn").astype(jnp.bfloat16)
        A = A + _dot16(A.astype(jnp.bfloat16), Ph, "nn")
    return A


def _lane_col(blk, idx):
    lane = _iota2(blk.shape, 1)
    return jnp.sum(jnp.where(lane == idx, blk, 0.0), axis=1, keepdims=True)


class _GdnChunk:
    def __init__(self, q, k, v, gcol, beta, rev, A=None):
        Cn = q.shape[0]
        ii, jj = _iota2((Cn, Cn), 0), _iota2((Cn, Cn), 1)
        self.incl = (ii <= jj) if rev else (ii >= jj)
        self.strict = (ii < jj) if rev else (ii > jj)
        grow = jnp.broadcast_to(gcol, (Cn, LANES)).T[0:1, :]
        diff = gcol - grow
        self.decay = jnp.where(self.incl, jnp.exp(jnp.where(self.incl, diff, 0.0)), 0.0)
        self.k, self.v, self.beta, self.gcol = k, v, beta, gcol
        self.qs = q * (GDN_DIM**-0.5)
        self.kb = k * beta
        self.L = jnp.where(self.strict, _mx(self.kb, k, "nt") * self.decay, 0.0)
        if A is None:
            A = _inv_unit_lower(self.L, jnp.where(ii == jj, 1.0, 0.0).astype(f32))
        self.A2 = _split2(A)
        self.A = A
        self.eg = jnp.exp(gcol)
        self.kbg = self.kb * self.eg
        self.sol = _h3s(self.A2, _split2(jnp.concatenate([v * beta, self.kbg], axis=1)), "nn")
        self.u = self.sol[:, :GDN_DIM]
        self.w = self.sol[:, GDN_DIM:]
        self.attn = jnp.where(self.incl, _mx(self.qs, k, "nt") * self.decay, 0.0)
        self.qd = self.qs * self.eg
        last = 0 if rev else Cn - 1
        self.last = last
        gl = gcol[last : last + 1, :]
        self.ekd = jnp.exp(gl - gcol)
        self.kd = k * self.ekd
        self.egl = jnp.exp(gl)


def _gdn_scan_fwd(qkvn, gates, gc):
    T = qkvn.shape[1]
    N = T // CHUNK
    nh, hb = GDN_HEADS, GDN_HB
    W = hb * GDN_DIM
    chains = [(d, hh) for d in range(2) for hh in range(hb)]

    def body(qf_ref, qb_ref, gf_ref, gb_ref, cf_ref, cb_ref, of_ref, ob_ref, sf_ref, sb_ref, af_ref, ab_ref, s_scr):
        n, b = pl.program_id(0), pl.program_id(1)

        @pl.when(n == 0)
        def _():
            for d, hh in chains:
                s_scr[d, b * hb + hh] = jnp.zeros((GDN_DIM, GDN_DIM), f32)

        dirs = ((qf_ref, gf_ref, cf_ref, of_ref, sf_ref, af_ref), (qb_ref, gb_ref, cb_ref, ob_ref, sb_ref, ab_ref))
        s_in = {ch: s_scr[ch[0], b * hb + ch[1]] for ch in chains}
        s_out = {}
        for d, hh in chains:
            q_ref, g_ref, c_ref, o_ref, st_ref, a_ref = dirs[d]
            h = b * hb + hh
            sl = slice(hh * GDN_DIM, (hh + 1) * GDN_DIM)
            gcol = _lane_col(c_ref[...], h + 8 * d)
            beta = _lane_col(g_ref[...], 16 + h + 8 * d)
            ck = _GdnChunk(q_ref[0, :, sl], q_ref[1, :, sl], q_ref[2, :, sl], gcol, beta, d == 1)
            S = s_in[d, hh]
            st_ref[0, hh] = S
            a_ref[0, hh] = ck.A
            v_new = ck.u - _mx(ck.w, S, "nn")
            o_ref[:, sl] = _mx(ck.qd, S, "nn") + _mx(ck.attn, v_new, "nn")
            s_out[d, hh] = S * ck.egl + _mx(ck.kd, v_new, "tn")
        for d, hh in chains:
            s_scr[d, b * hb + hh] = s_out[d, hh]

    fw, bw = (lambda n: n), (lambda n: N - 1 - n)
    qsp = lambda ci: pl.BlockSpec((3, CHUNK, W), lambda n, b: (0, ci(n), b))
    gsp = lambda ci: pl.BlockSpec((CHUNK, LANES), lambda n, b: (ci(n), 0))
    osp = lambda ci: pl.BlockSpec((CHUNK, W), lambda n, b: (ci(n), b))
    ssp = lambda ci: pl.BlockSpec((1, hb, GDN_DIM, GDN_DIM), lambda n, b: (ci(n), b, 0, 0))
    asp = lambda ci: pl.BlockSpec((1, hb, CHUNK, CHUNK), lambda n, b: (ci(n), b, 0, 0))
    o_shape = jax.ShapeDtypeStruct((T, nh * GDN_DIM), f32)
    s_shape = jax.ShapeDtypeStruct((N, nh, GDN_DIM, GDN_DIM), f32)
    a_shape = jax.ShapeDtypeStruct((N, nh, CHUNK, CHUNK), f32)
    return pl.pallas_call(
        body,
        out_shape=(o_shape, o_shape, s_shape, s_shape, a_shape, a_shape),
        grid=(N, nh // hb),
        in_specs=[qsp(fw), qsp(bw), gsp(fw), gsp(bw), gsp(fw), gsp(bw)],
        out_specs=(osp(fw), osp(bw), ssp(fw), ssp(bw), asp(fw), asp(bw)),
        scratch_shapes=[pltpu.VMEM((2, nh, GDN_DIM, GDN_DIM), f32)],
        compiler_params=_cparams(("arbitrary", "arbitrary")),
        name="gdn_scan_fwd",
    )(qkvn, qkvn, gates, gates, gc, gc)


def _gdn_scan_bwd(qkvn, gates, gc, st_f, st_b, a_f, a_b, do):
    T = qkvn.shape[1]
    N = T // CHUNK
    nh, hb = GDN_HEADS, GDN_HB
    nb = nh // hb
    W = hb * GDN_DIM
    chains = [(d, hh) for d in range(2) for hh in range(hb)]

    def body(qf_ref, qb_ref, gf_ref, gb_ref, cf_ref, cb_ref, sf_ref, sb_ref, af_ref, ab_ref, dof_ref, dob_ref,
             dqf_ref, dqb_ref, dgf_ref, dgb_ref, ds_scr):
        n, b = pl.program_id(0), pl.program_id(1)

        @pl.when(n == 0)
        def _():
            for d, hh in chains:
                ds_scr[d, b * hb + hh] = jnp.zeros((GDN_DIM, GDN_DIM), f32)

        @pl.when(b == 0)
        def _():
            dgf_ref[...] = jnp.zeros((CHUNK, LANES), f32)
            dgb_ref[...] = jnp.zeros((CHUNK, LANES), f32)

        dirs = ((qf_ref, gf_ref, cf_ref, sf_ref, af_ref, dof_ref, dqf_ref), (qb_ref, gb_ref, cb_ref, sb_ref, ab_ref, dob_ref, dqb_ref))
        ds_in = {ch: ds_scr[ch[0], b * hb + ch[1]] for ch in chains}
        ds_out = {}
        lane = _iota2((CHUNK, LANES), 1)
        acc = [dgf_ref[...], dgb_ref[...]]
        for d, hh in chains:
            q_ref, g_ref, c_ref, st_ref, a_ref, do_ref, dq_ref = dirs[d]
            h = b * hb + hh
            sl = slice(hh * GDN_DIM, (hh + 1) * GDN_DIM)
            gcol = _lane_col(c_ref[...], h + 8 * d)
            beta = _lane_col(g_ref[...], 16 + h + 8 * d)
            ck = _GdnChunk(q_ref[0, :, sl], q_ref[1, :, sl], q_ref[2, :, sl], gcol, beta, d == 1, A=a_ref[0, hh])
            k, v = ck.k, ck.v
            S = st_ref[0, hh]
            dS = ds_in[d, hh]
            dov = do_ref[:, sl]
            v_new = ck.u - _mx(ck.w, S, "nn")
            d_vnew = _mx(ck.attn, dov, "tn") + _mx(ck.kd, dS, "nn")
            d_attn = jnp.where(ck.incl, _mx(dov, v_new, "nt"), 0.0)
            d_qd = _mx(dov, S, "nt")
            d_kd = _mx(v_new, dS, "nt")
            d_gl = jnp.sum(jnp.sum(dS * S, axis=1, keepdims=True), axis=0, keepdims=True) * ck.egl
            ds_out[d, hh] = dS * ck.egl + _mx(ck.qd, dov, "tn") - _mx(ck.w, d_vnew, "tn")
            d_w = -_mx(d_vnew, S, "nt")
            d_rhs = _h3s(ck.A2, _split2(jnp.concatenate([d_vnew, d_w], axis=1)), "tn")
            d_vb, d_kbg = d_rhs[:, :GDN_DIM], d_rhs[:, GDN_DIM:]
            dL = -jnp.where(ck.strict, _h3(d_rhs, ck.sol, "nt"), 0.0)
            P = dL * ck.decay
            Q = d_attn * ck.decay
            d_kb = _mx(P, k, "nn") + d_kbg * ck.eg
            d_qs = _mx(Q, k, "nn") + d_qd * ck.eg
            d_k = _mx(P, ck.kb, "tn") + _mx(Q, ck.qs, "tn") + d_kd * ck.ekd + d_kb * ck.beta
            E = dL * ck.L + d_attn * ck.attn
            kdsum = jnp.sum(d_kd * ck.kd, axis=1, keepdims=True)
            d_gc = (jnp.sum(E, axis=1, keepdims=True) - _colsum_col(E)
                    + jnp.sum(d_kbg * ck.kbg, axis=1, keepdims=True)
                    + jnp.sum(d_qd * ck.qd, axis=1, keepdims=True) - kdsum)
            d_gl = d_gl + jnp.sum(kdsum, axis=0, keepdims=True)
            row = _iota2((CHUNK, 1), 0)
            d_gc = d_gc + jnp.where(row == ck.last, d_gl, 0.0)
            d_beta = jnp.sum(d_vb * v, axis=1, keepdims=True) + jnp.sum(d_kb * k, axis=1, keepdims=True)
            dq_ref[0, :, sl] = d_qs * (GDN_DIM**-0.5)
            dq_ref[1, :, sl] = d_k
            dq_ref[2, :, sl] = d_vb * ck.beta
            acc[d] = jnp.where(lane == h + 8 * d, d_gc, acc[d])
            acc[d] = jnp.where(lane == 16 + h + 8 * d, d_beta, acc[d])
        for d, hh in chains:
            ds_scr[d, b * hb + hh] = ds_out[d, hh]
        dgf_ref[...] = acc[0]
        dgb_ref[...] = acc[1]

        @pl.when(b == nb - 1)
        def _():
            for d, dg_ref in enumerate((dgf_ref, dgb_ref)):
                blk = dg_ref[...]
                dg_ref[...] = jnp.where(lane < 16, _exact01(_chunk_tri(CHUNK, d == 0), blk), blk)

    fw, bw = (lambda n: N - 1 - n), (lambda n: n)
    qsp = lambda ci: pl.BlockSpec((3, CHUNK, W), lambda n, b: (0, ci(n), b))
    gsp = lambda ci: pl.BlockSpec((CHUNK, LANES), lambda n, b: (ci(n), 0))
    osp = lambda ci: pl.BlockSpec((CHUNK, W), lambda n, b: (ci(n), b))
    ssp = lambda ci: pl.BlockSpec((1, hb, GDN_DIM, GDN_DIM), lambda n, b: (ci(n), b, 0, 0))
    asp = lambda ci: pl.BlockSpec((1, hb, CHUNK, CHUNK), lambda n, b: (ci(n), b, 0, 0))
    dq_shape = jax.ShapeDtypeStruct((3, T, nh * GDN_DIM), f32)
    dg_shape = jax.ShapeDtypeStruct((T, LANES), f32)
    return pl.pallas_call(
        body,
        out_shape=(dq_shape, dq_shape, dg_shape, dg_shape),
        grid=(N, nb),
        in_specs=[qsp(fw), qsp(bw), gsp(fw), gsp(bw), gsp(fw), gsp(bw), ssp(fw), ssp(bw), asp(fw), asp(bw), osp(fw), osp(bw)],
        out_specs=(qsp(fw), qsp(bw), gsp(fw), gsp(bw)),
        scratch_shapes=[pltpu.VMEM((2, nh, GDN_DIM, GDN_DIM), f32)],
        compiler_params=_cparams(("arbitrary", "arbitrary")),
        name="gdn_scan_bwd",
    )(qkvn, qkvn, gates, gates, gc, gc, st_f, st_b, a_f, a_b, do, do)


def _out_norm(o_f, o_b, proj, gate_blk, w, hd, name, tb=256):
    T, W = o_f.shape
    nh = W // hd

    def body(of_ref, ob_ref, z_ref, w_ref, y_ref):
        wv = w_ref[...]
        for i in range(nh):
            sl = slice(i * hd, (i + 1) * hd)
            o = of_ref[:, sl] + ob_ref[:, sl]
            r = lax.rsqrt(jnp.mean(o * o, axis=1, keepdims=True) + NORM_EPS)
            y_ref[:, sl] = (o * r * wv * _silu(z_ref[:, sl])).astype(y_ref.dtype)

    blk = pl.BlockSpec((tb, W), lambda i: (i, 0))
    return pl.pallas_call(
        body,
        out_shape=jax.ShapeDtypeStruct((T, W), MXU_DTYPE),
        grid=(T // tb,),
        in_specs=[blk, blk, pl.BlockSpec((tb, W), lambda i: (i, gate_blk)), pl.BlockSpec((1, hd), lambda i: (0, 0))],
        out_specs=blk,
        compiler_params=_cparams(("parallel",)),
        name=name,
    )(o_f, o_b, proj, w)


def _out_norm_bwd(dy, o_f, o_b, proj, gate_blk, w, hd, name, dproj, tb=256):
    T, W = o_f.shape
    nh = W // hd

    def body(dy_ref, of_ref, ob_ref, z_ref, w_ref, _, do_ref, dz_ref, dw_ref):
        i = pl.program_id(0)
        wv = w_ref[...]
        dw = jnp.zeros((1, hd), f32)
        for a in range(nh):
            sl = slice(a * hd, (a + 1) * hd)
            o = of_ref[:, sl] + ob_ref[:, sl]
            z = z_ref[:, sl]
            dyv = dy_ref[:, sl]
            r = lax.rsqrt(jnp.mean(o * o, axis=1, keepdims=True) + NORM_EPS)
            orr = o * r
            dn = dyv * _silu(z)
            dz_ref[:, sl] = (dyv * orr * wv * _silu_grad(z)).astype(dz_ref.dtype)
            dw = dw + jnp.sum(dn * orr, axis=0, keepdims=True)
            dnw = dn * wv
            do_ref[:, sl] = r * dnw - o * (r * r * r) * jnp.mean(dnw * o, axis=1, keepdims=True)

        @pl.when(i == 0)
        def _():
            dw_ref[...] = dw

        @pl.when(i > 0)
        def _():
            dw_ref[...] += dw

    blk = pl.BlockSpec((tb, W), lambda i: (i, 0))
    row = pl.BlockSpec((1, hd), lambda i: (0, 0))
    return pl.pallas_call(
        body,
        out_shape=(jax.ShapeDtypeStruct((T, W), f32), jax.ShapeDtypeStruct(dproj.shape, dproj.dtype), jax.ShapeDtypeStruct((1, hd), f32)),
        grid=(T // tb,),
        in_specs=[blk, blk, blk, pl.BlockSpec((tb, W), lambda i: (i, gate_blk)), row, _ANY_SPEC],
        out_specs=(blk, pl.BlockSpec((tb, W), lambda i: (i, gate_blk)), row),
        input_output_aliases={5: 1},
        compiler_params=_cparams(("arbitrary",)),
        name=name,
    )(dy, o_f, o_b, proj, w, dproj)


def _gla_prep(proj, w2f, w2b, b2f, b2b, tb=256):
    T = proj.shape[0]
    W = GLA_HEADS * GLA_DK

    def body(s_ref, wf_ref, wb_ref, bf_ref, bb_ref, gf_ref, gb_ref):
        small = s_ref[...]
        gkf = -_softplus(-(_mx(small, wf_ref[...], "nn") + bf_ref[...])) * (1.0 / GATE_NORMALIZER)
        gkb = -_softplus(-(_mx(small, wb_ref[...], "nn") + bb_ref[...])) * (1.0 / GATE_NORMALIZER)
        gf_ref[...] = _exact01(_chunk_tri(tb, False), gkf)
        gb_ref[...] = _exact01(_chunk_tri(tb, True), gkb)

    blk = pl.BlockSpec((tb, W), lambda i: (i, 0))
    wsp = pl.BlockSpec((LANES, W), lambda i: (0, 0))
    row = pl.BlockSpec((1, W), lambda i: (0, 0))
    return pl.pallas_call(
        body,
        out_shape=(jax.ShapeDtypeStruct((T, W), f32), jax.ShapeDtypeStruct((T, W), f32)),
        grid=(T // tb,),
        in_specs=[pl.BlockSpec((tb, LANES), lambda i: (i, SMALL_OFF // LANES)), wsp, wsp, row, row],
        out_specs=(blk, blk),
        compiler_params=_cparams(("parallel",)),
        name="gla_prep",
    )(proj, w2f, w2b, b2f, b2b)


def _gla_prep_bwd(proj, w2f, w2b, b2f, b2b, dgk_f, dgk_b, tb=256):
    T = proj.shape[0]
    W = GLA_HEADS * GLA_DK

    def body(s_ref, wf_ref, wb_ref, bf_ref, bb_ref, df_ref, db_ref, ds_ref, dwf_ref, dwb_ref, dbf_ref, dbb_ref):
        i = pl.program_id(0)
        small = s_ref[...]
        dsm = jnp.zeros((tb, LANES), f32)
        parts = []
        for w_ref, b_ref, d_ref in ((wf_ref, bf_ref, df_ref), (wb_ref, bb_ref, db_ref)):
            pre = _mx(small, w_ref[...], "nn") + b_ref[...]
            dpre = d_ref[...] * (1.0 / GATE_NORMALIZER) * _sigmoid(-pre)
            dsm = dsm + _mx(dpre, w_ref[...], "nt")
            parts.append((_mx(small, dpre, "tn"), jnp.sum(dpre, axis=0, keepdims=True)))
        ds_ref[...] = dsm

        @pl.when(i == 0)
        def _():
            dwf_ref[...], dbf_ref[...] = parts[0]
            dwb_ref[...], dbb_ref[...] = parts[1]

        @pl.when(i > 0)
        def _():
            dwf_ref[...] += parts[0][0]
            dbf_ref[...] += parts[0][1]
            dwb_ref[...] += parts[1][0]
            dbb_ref[...] += parts[1][1]

    blk = pl.BlockSpec((tb, W), lambda i: (i, 0))
    wsp = pl.BlockSpec((LANES, W), lambda i: (0, 0))
    row = pl.BlockSpec((1, W), lambda i: (0, 0))
    return pl.pallas_call(
        body,
        out_shape=(jax.ShapeDtypeStruct((T, LANES), f32), jax.ShapeDtypeStruct((LANES, W), f32), jax.ShapeDtypeStruct((LANES, W), f32),
                   jax.ShapeDtypeStruct((1, W), f32), jax.ShapeDtypeStruct((1, W), f32)),
        grid=(T // tb,),
        in_specs=[pl.BlockSpec((tb, LANES), lambda i: (i, SMALL_OFF // LANES)), wsp, wsp, row, row, blk, blk],
        out_specs=(pl.BlockSpec((tb, LANES), lambda i: (i, 0)), wsp, wsp, row, row),
        compiler_params=_cparams(("arbitrary",)),
        name="gla_prep_bwd",
    )(proj, w2f, w2b, b2f, b2b, dgk_f, dgk_b)


Q_B_BLK, K_B_BLK, V_B_BLK = 4096 // GLA_DK, 4608 // GLA_DK, 5120 // GLA_DV


class _GlaChunk:
    def __init__(self, q, k, G, rev):
        Cn = q.shape[0]
        ii, jj = _iota2((Cn, Cn), 0), _iota2((Cn, Cn), 1)
        self.incl = (ii <= jj) if rev else (ii >= jj)
        self.eG = jnp.exp(G)
        self.enG = jnp.exp(-G)
        self.qg = q * (GLA_DK**-0.5) * self.eG
        self.kg = k * self.enG
        self.attn = jnp.where(self.incl, _mx(self.qg, self.kg, "nt"), 0.0)
        last = 0 if rev else Cn - 1
        self.last = last
        gl = G[last : last + 1, :]
        self.ekd = jnp.exp(gl - G)
        self.kd = k * self.ekd
        self.egl = jnp.exp(gl)


def _gla_scan_fwd(proj, G_f, G_b):
    T = proj.shape[0]
    N = T // CHUNK
    nh = GLA_HEADS

    def body(qf_ref, kf_ref, vf_ref, gf_ref, qb_ref, kb_ref, vb_ref, gb_ref, of_ref, ob_ref, sf_ref, sb_ref, s_scr):
        n, h = pl.program_id(0), pl.program_id(1)

        @pl.when(n == 0)
        def _():
            s_scr[0, h] = jnp.zeros((GLA_DV, GLA_DK), f32)
            s_scr[1, h] = jnp.zeros((GLA_DV, GLA_DK), f32)

        for d, (q_ref, k_ref, v_ref, g_ref, o_ref, st_ref) in enumerate(
            ((qf_ref, kf_ref, vf_ref, gf_ref, of_ref, sf_ref), (qb_ref, kb_ref, vb_ref, gb_ref, ob_ref, sb_ref))
        ):
            ck = _GlaChunk(q_ref[...], k_ref[...], g_ref[...], d == 1)
            v = v_ref[...]
            St = s_scr[d, h]
            st_ref[0, 0] = St
            o_ref[...] = _mx(ck.qg, St, "nt") + _mx(ck.attn, v, "nn")
            s_scr[d, h] = St * ck.egl + _mx(v, ck.kd, "tn")

    def specs(cidx):
        return [
            pl.BlockSpec((CHUNK, GLA_DK), lambda n, h: (cidx(n), Q_B_BLK + h)),
            pl.BlockSpec((CHUNK, GLA_DK), lambda n, h: (cidx(n), K_B_BLK + h)),
            pl.BlockSpec((CHUNK, GLA_DV), lambda n, h: (cidx(n), V_B_BLK + h)),
            pl.BlockSpec((CHUNK, GLA_DK), lambda n, h: (cidx(n), h)),
        ]

    fw, bw = (lambda n: n), (lambda n: N - 1 - n)
    o_shape = jax.ShapeDtypeStruct((T, nh * GLA_DV), f32)
    s_shape = jax.ShapeDtypeStruct((N, nh, GLA_DV, GLA_DK), f32)
    return pl.pallas_call(
        body,
        out_shape=(o_shape, o_shape, s_shape, s_shape),
        grid=(N, nh),
        in_specs=specs(fw) + specs(bw),
        out_specs=(
            pl.BlockSpec((CHUNK, GLA_DV), lambda n, h: (n, h)),
            pl.BlockSpec((CHUNK, GLA_DV), lambda n, h: (N - 1 - n, h)),
            pl.BlockSpec((1, 1, GLA_DV, GLA_DK), lambda n, h: (n, h, 0, 0)),
            pl.BlockSpec((1, 1, GLA_DV, GLA_DK), lambda n, h: (N - 1 - n, h, 0, 0)),
        ),
        scratch_shapes=[pltpu.VMEM((2, nh, GLA_DV, GLA_DK), f32)],
        compiler_params=_cparams(("arbitrary", "arbitrary")),
        name="gla_scan_fwd",
    )(proj, proj, proj, G_f, proj, proj, proj, G_b)


def _gla_scan_bwd(proj, G_f, G_b, st_f, st_b, do):
    T = proj.shape[0]
    N = T // CHUNK
    nh = GLA_HEADS

    def body(qf_ref, kf_ref, vf_ref, gf_ref, sf_ref, dof_ref, qb_ref, kb_ref, vb_ref, gb_ref, sb_ref, dob_ref,
             dqf_ref, dkf_ref, dvf_ref, dgf_ref, dqb_ref, dkb_ref, dvb_ref, dgb_ref, ds_scr):
        n, h = pl.program_id(0), pl.program_id(1)

        @pl.when(n == 0)
        def _():
            ds_scr[0, h] = jnp.zeros((GLA_DV, GLA_DK), f32)
            ds_scr[1, h] = jnp.zeros((GLA_DV, GLA_DK), f32)

        for d, (q_ref, k_ref, v_ref, g_ref, st_ref, do_ref, dq_ref, dk_ref, dv_ref, dg_ref) in enumerate(
            ((qf_ref, kf_ref, vf_ref, gf_ref, sf_ref, dof_ref, dqf_ref, dkf_ref, dvf_ref, dgf_ref),
             (qb_ref, kb_ref, vb_ref, gb_ref, sb_ref, dob_ref, dqb_ref, dkb_ref, dvb_ref, dgb_ref))
        ):
            rev = d == 1
            ck = _GlaChunk(q_ref[...], k_ref[...], g_ref[...], rev)
            v = v_ref[...]
            St = st_ref[0, 0]
            dSt = ds_scr[d, h]
            dov = do_ref[...]
            d_attn = jnp.where(ck.incl, _mx(dov, v, "nt"), 0.0)
            d_qg = _mx(dov, St, "nn") + _mx(d_attn, ck.kg, "nn")
            d_kg = _mx(d_attn, ck.qg, "tn")
            dv_ref[...] = _mx(ck.attn, dov, "tn") + _mx(ck.kd, dSt, "nt")
            d_kd = _mx(v, dSt, "nn")
            ds_scr[d, h] = dSt * ck.egl + _mx(dov, ck.qg, "tn")
            kdd = d_kd * ck.kd
            d_gl = jnp.sum(dSt * St, axis=0, keepdims=True) * ck.egl + jnp.sum(kdd, axis=0, keepdims=True)
            dq_ref[...] = d_qg * ck.eG * (GLA_DK**-0.5)
            dk_ref[...] = d_kg * ck.enG + d_kd * ck.ekd
            d_G = d_qg * ck.qg - d_kg * ck.kg - kdd
            row = _iota2((CHUNK, GLA_DK), 0)
            d_G = d_G + jnp.where(row == ck.last, d_gl, 0.0)
            dg_ref[...] = _exact01(_chunk_tri(CHUNK, not rev), d_G)

    def in_specs(cidx):
        return [
            pl.BlockSpec((CHUNK, GLA_DK), lambda n, h: (cidx(n), Q_B_BLK + h)),
            pl.BlockSpec((CHUNK, GLA_DK), lambda n, h: (cidx(n), K_B_BLK + h)),
            pl.BlockSpec((CHUNK, GLA_DV), lambda n, h: (cidx(n), V_B_BLK + h)),
            pl.BlockSpec((CHUNK, GLA_DK), lambda n, h: (cidx(n), h)),
            pl.BlockSpec((1, 1, GLA_DV, GLA_DK), lambda n, h: (cidx(n), h, 0, 0)),
            pl.BlockSpec((CHUNK, GLA_DV), lambda n, h: (cidx(n), h)),
        ]

    def out_specs(cidx):
        return [
            pl.BlockSpec((CHUNK, GLA_DK), lambda n, h: (cidx(n), h)),
            pl.BlockSpec((CHUNK, GLA_DK), lambda n, h: (cidx(n), h)),
            pl.BlockSpec((CHUNK, GLA_DV), lambda n, h: (cidx(n), h)),
            pl.BlockSpec((CHUNK, GLA_DK), lambda n, h: (cidx(n), h)),
        ]

    fw, bw = (lambda n: N - 1 - n), (lambda n: n)
    k_shape = jax.ShapeDtypeStruct((T, nh * GLA_DK), f32)
    v_shape = jax.ShapeDtypeStruct((T, nh * GLA_DV), f32)
    return pl.pallas_call(
        body,
        out_shape=(k_shape, k_shape, v_shape, k_shape) * 2,
        grid=(N, nh),
        in_specs=in_specs(fw) + in_specs(bw),
        out_specs=tuple(out_specs(fw) + out_specs(bw)),
        scratch_shapes=[pltpu.VMEM((2, nh, GLA_DV, GLA_DK), f32)],
        compiler_params=_cparams(("arbitrary", "arbitrary")),
        name="gla_scan_bwd",
    )(proj, proj, proj, G_f, st_f, do, proj, proj, proj, G_b, st_b, do)


GATE_A_BLK, GATE_B_BLK, Z_BLK, G_B_BLK = 6144 // 1024, 7168 // 1024, 3072 // 1024, 8192 // 1024
DP_GATES_BLK, DP_GLA_QKV_BLK, DP_SMALL_BLK = 6144 // 2048, 4096 // 2048, SMALL_OFF // 512
_ANY_SPEC = pl.BlockSpec(memory_space=pl.ANY)


def _mid(og, obg, proj, x, target, wpg, wpl, wout, w_post, tb=256):
    T = x.shape[0]
    Dm = D_MODEL

    def body(og_ref, obg_ref, ga_ref, gb_ref, x_ref, t_ref, wpg_ref, wpl_ref, wo_ref, wp_ref,
             loss_ref, dwp_ref, dy_ref, m_ref, dout_ref, dya_ref, dyb_ref, dgates_ref, dog_ref, dobg_ref):
        i = pl.program_id(0)
        ya = _mx(og_ref[...], wpg_ref[...], "nn")
        yb = _mx(obg_ref[...], wpl_ref[...], "nn")
        sa, sb = _sigmoid(ga_ref[...]), _sigmoid(gb_ref[...])
        m = sa * ya + sb * yb
        m_ref[...] = m.astype(m_ref.dtype)
        out = _mx(m, wo_ref[...], "nn")
        r = lax.rsqrt(jnp.mean(out * out, axis=1, keepdims=True) + NORM_EPS)
        wp = wp_ref[...]
        e = x_ref[...] + out * r * wp - t_ref[...]
        dy = e * (1.0 / Dm)
        dy_ref[...] = dy
        lpart = jnp.sum(e * e, axis=0, keepdims=True)
        wpart = jnp.sum(dy * out * r, axis=0, keepdims=True)
        dyn = dy * wp
        dout = r * dyn - out * (r * r * r) * jnp.mean(dyn * out, axis=1, keepdims=True)
        dout_ref[...] = dout.astype(dout_ref.dtype)
        dm = _mx(dout, wo_ref[...], "nt")
        dya, dyb = dm * sa, dm * sb
        dya_ref[...] = dya.astype(dya_ref.dtype)
        dyb_ref[...] = dyb.astype(dyb_ref.dtype)
        dgates_ref[:, :Dm] = (dm * ya * sa * (1.0 - sa)).astype(dgates_ref.dtype)
        dgates_ref[:, Dm:] = (dm * yb * sb * (1.0 - sb)).astype(dgates_ref.dtype)
        dog_ref[...] = _mx(dya, wpg_ref[...], "nt")
        dobg_ref[...] = _mx(dyb, wpl_ref[...], "nt")

        @pl.when(i == 0)
        def _():
            loss_ref[...] = lpart
            dwp_ref[...] = wpart

        @pl.when(i > 0)
        def _():
            loss_ref[...] += lpart
            dwp_ref[...] += wpart

    blk = pl.BlockSpec((tb, Dm), lambda i: (i, 0))
    row = pl.BlockSpec((1, Dm), lambda i: (0, 0))
    wsp = pl.BlockSpec((Dm, Dm), lambda i: (0, 0))
    act = jax.ShapeDtypeStruct((T, Dm), MXU_DTYPE)
    big = jax.ShapeDtypeStruct((T, Dm), f32)
    vec = jax.ShapeDtypeStruct((1, Dm), f32)
    return pl.pallas_call(
        body,
        out_shape=(vec, vec, big, act, act, act, act, jax.ShapeDtypeStruct((T, N_CAT), MXU_DTYPE), big, big),
        grid=(T // tb,),
        in_specs=[blk, blk, pl.BlockSpec((tb, Dm), lambda i: (i, GATE_A_BLK)), pl.BlockSpec((tb, Dm), lambda i: (i, GATE_B_BLK)),
                  blk, blk, wsp, wsp, wsp, row],
        out_specs=(row, row) + (blk,) * 5 + (pl.BlockSpec((tb, 2 * Dm), lambda i: (i, DP_GATES_BLK)), blk, blk),
        compiler_params=_cparams(("arbitrary",)),
        name="mid",
    )(og, obg, proj, proj, x, target, wpg, wpl, wout, w_post)


def _heads(ref, lead, n, width):
    return [ref[lead + (slice(None), slice(i * width, (i + 1) * width))] for i in range(n)]


class _GdnBatch:
    def __init__(self, q, k, v, gcol, grow, beta, nf, A=None):
        B, Cn = q.shape[0], q.shape[1]
        self.nf = nf
        self.incl, self.strict, eye = _dir_masks(B, nf, Cn)
        self.decay = jnp.where(self.incl, jnp.exp(jnp.where(self.incl, gcol - grow, 0.0)), 0.0)
        self.k, self.v, self.beta = k, v, beta
        self.qs = q * (GDN_DIM**-0.5)
        self.kb = k * beta
        self.L = jnp.where(self.strict, _mx(self.kb, k, "nt") * self.decay, 0.0)
        if A is None:
            A = _inv_unit_lower(self.L, jnp.where(eye, 1.0, 0.0).astype(f32))
        self.A = A
        self.A2 = _split2(A)
        self.eg = jnp.exp(gcol)
        self.kbg = self.kb * self.eg
        self.sol = _h3s(self.A2, _split2(jnp.concatenate([v * beta, self.kbg], axis=2)), "nn")
        self.u = self.sol[:, :, :GDN_DIM]
        self.w = self.sol[:, :, GDN_DIM:]
        self.attn = jnp.where(self.incl, _mx(self.qs, k, "nt") * self.decay, 0.0)
        self.qd = self.qs * self.eg
        gl = _last_row(gcol, nf)
        self.ekd = jnp.exp(gl - gcol)
        self.kd = k * self.ekd
        self.egl = jnp.exp(gl)


def _gdn_chain_inputs(qf_ref, qb_ref, gf_ref, gb_ref, cf_ref, cb_ref, b, hb):
    qkv = [jnp.stack(_heads(qf_ref, (i,), hb, GDN_DIM) + _heads(qb_ref, (i,), hb, GDN_DIM)) for i in range(3)]
    cols, rows, betas = [], [], []
    for d, (g_ref, c_ref) in enumerate(((gf_ref, cf_ref), (gb_ref, cb_ref))):
        gates, gc = g_ref[...], c_ref[...]
        for hh in range(hb):
            h = b * hb + hh
            col = _lane_col(gc, h + 8 * d)
            cols.append(col)
            rows.append(jnp.broadcast_to(col, (CHUNK, LANES)).T[0:1, :])
            betas.append(_lane_col(gates, 16 + h + 8 * d))
    return qkv[0], qkv[1], qkv[2], jnp.stack(cols), jnp.stack(rows), jnp.stack(betas)


def _gdn_fwd(qkvn, gates, gc):
    T = qkvn.shape[1]
    N = T // CHUNK
    nh, hb = GDN_HEADS, GDN_HB
    W = hb * GDN_DIM

    def body(qf_ref, qb_ref, gf_ref, gb_ref, cf_ref, cb_ref, of_ref, ob_ref, sf_ref, sb_ref, af_ref, ab_ref, s_scr):
        n, b = pl.program_id(0), pl.program_id(1)
        hs = pl.ds(b * hb, hb)

        @pl.when(n == 0)
        def _():
            s_scr[0, hs] = jnp.zeros((hb, GDN_DIM, GDN_DIM), f32)
            s_scr[1, hs] = jnp.zeros((hb, GDN_DIM, GDN_DIM), f32)

        q, k, v, gcol, grow, beta = _gdn_chain_inputs(qf_ref, qb_ref, gf_ref, gb_ref, cf_ref, cb_ref, b, hb)
        ck = _GdnBatch(q, k, v, gcol, grow, beta, hb)
        S = jnp.concatenate([s_scr[0, hs], s_scr[1, hs]], axis=0)
        sf_ref[0], sb_ref[0] = S[:hb], S[hb:]
        af_ref[0], ab_ref[0] = ck.A[:hb], ck.A[hb:]
        v_new = ck.u - _mx(ck.w, S, "nn")
        o = _mx(ck.qd, S, "nn") + _mx(ck.attn, v_new, "nn")
        S_new = S * ck.egl + _mx(ck.kd, v_new, "tn")
        for hh in range(hb):
            sl = slice(hh * GDN_DIM, (hh + 1) * GDN_DIM)
            of_ref[:, sl] = o[hh]
            ob_ref[:, sl] = o[hb + hh]
        s_scr[0, hs] = S_new[:hb]
        s_scr[1, hs] = S_new[hb:]

    fw, bw = (lambda n: n), (lambda n: N - 1 - n)
    qsp = lambda ci: pl.BlockSpec((3, CHUNK, W), lambda n, b: (0, ci(n), b))
    gsp = lambda ci: pl.BlockSpec((CHUNK, LANES), lambda n, b: (ci(n), 0))
    osp = lambda ci: pl.BlockSpec((CHUNK, W), lambda n, b: (ci(n), b))
    ssp = lambda ci: pl.BlockSpec((1, hb, GDN_DIM, GDN_DIM), lambda n, b: (ci(n), b, 0, 0))
    asp = lambda ci: pl.BlockSpec((1, hb, CHUNK, CHUNK), lambda n, b: (ci(n), b, 0, 0))
    o_shape = jax.ShapeDtypeStruct((T, nh * GDN_DIM), f32)
    s_shape = jax.ShapeDtypeStruct((N, nh, GDN_DIM, GDN_DIM), f32)
    a_shape = jax.ShapeDtypeStruct((N, nh, CHUNK, CHUNK), f32)
    return pl.pallas_call(
        body,
        out_shape=(o_shape, o_shape, s_shape, s_shape, a_shape, a_shape),
        grid=(N, nh // hb),
        in_specs=[qsp(fw), qsp(bw), gsp(fw), gsp(bw), gsp(fw), gsp(bw)],
        out_specs=(osp(fw), osp(bw), ssp(fw), ssp(bw), asp(fw), asp(bw)),
        scratch_shapes=[pltpu.VMEM((2, nh, GDN_DIM, GDN_DIM), f32)],
        compiler_params=_cparams(("arbitrary", "arbitrary")),
        name="gdn_scan_fwd",
    )(qkvn, qkvn, gates, gates, gc, gc)


def _gdn_bwd(qkvn, gates, gc, st_f, st_b, a_f, a_b, do):
    T = qkvn.shape[1]
    N = T // CHUNK
    nh, hb = GDN_HEADS, GDN_HB
    nb = nh // hb
    W = hb * GDN_DIM

    def body(qf_ref, qb_ref, gf_ref, gb_ref, cf_ref, cb_ref, sf_ref, sb_ref, af_ref, ab_ref, dof_ref, dob_ref,
             dqf_ref, dqb_ref, dgf_ref, dgb_ref, ds_scr):
        n, b = pl.program_id(0), pl.program_id(1)
        hs = pl.ds(b * hb, hb)

        @pl.when(n == 0)
        def _():
            ds_scr[0, hs] = jnp.zeros((hb, GDN_DIM, GDN_DIM), f32)
            ds_scr[1, hs] = jnp.zeros((hb, GDN_DIM, GDN_DIM), f32)

        @pl.when(b == 0)
        def _():
            dgf_ref[...] = jnp.zeros((CHUNK, LANES), f32)
            dgb_ref[...] = jnp.zeros((CHUNK, LANES), f32)

        q, k, v, gcol, grow, beta = _gdn_chain_inputs(qf_ref, qb_ref, gf_ref, gb_ref, cf_ref, cb_ref, b, hb)
        ck = _GdnBatch(q, k, v, gcol, grow, beta, hb, A=jnp.concatenate([af_ref[0], ab_ref[0]], axis=0))
        S = jnp.concatenate([sf_ref[0], sb_ref[0]], axis=0)
        dS = jnp.concatenate([ds_scr[0, hs], ds_scr[1, hs]], axis=0)
        dov = jnp.stack(_heads(dof_ref, (), hb, GDN_DIM) + _heads(dob_ref, (), hb, GDN_DIM))
        v_new = ck.u - _mx(ck.w, S, "nn")
        d_vnew = _mx(ck.attn, dov, "tn") + _mx(ck.kd, dS, "nn")
        d_attn = jnp.where(ck.incl, _mx(dov, v_new, "nt"), 0.0)
        d_qd = _mx(dov, S, "nt")
        d_kd = _mx(v_new, dS, "nt")
        d_gl = jnp.sum(jnp.sum(dS * S, axis=2, keepdims=True), axis=1, keepdims=True) * ck.egl
        dS_new = dS * ck.egl + _mx(ck.qd, dov, "tn") - _mx(ck.w, d_vnew, "tn")
        ds_scr[0, hs] = dS_new[:hb]
        ds_scr[1, hs] = dS_new[hb:]
        d_w = -_mx(d_vnew, S, "nt")
        d_rhs = _h3s(ck.A2, _split2(jnp.concatenate([d_vnew, d_w], axis=2)), "tn")
        d_vb, d_kbg = d_rhs[:, :, :GDN_DIM], d_rhs[:, :, GDN_DIM:]
        dL = -jnp.where(ck.strict, _h3(d_rhs, ck.sol, "nt"), 0.0)
        P = dL * ck.decay
        Q = d_attn * ck.decay
        d_kb = _mx(P, k, "nn") + d_kbg * ck.eg
        d_qs = _mx(Q, k, "nn") + d_qd * ck.eg
        d_k = _mx(P, ck.kb, "tn") + _mx(Q, ck.qs, "tn") + d_kd * ck.ekd + d_kb * ck.beta
        E = dL * ck.L + d_attn * ck.attn
        kdsum = jnp.sum(d_kd * ck.kd, axis=2, keepdims=True)
        d_gc = (jnp.sum(E, axis=2, keepdims=True) - _colsum_col(E)
                + jnp.sum(d_kbg * ck.kbg, axis=2, keepdims=True)
                + jnp.sum(d_qd * ck.qd, axis=2, keepdims=True) - kdsum)
        d_gc = _add_at_last_row(d_gc, d_gl + jnp.sum(kdsum, axis=1, keepdims=True), hb)
        d_beta = jnp.sum(d_vb * v, axis=2, keepdims=True) + jnp.sum(d_kb * k, axis=2, keepdims=True)
        d_q = d_qs * (GDN_DIM**-0.5)
        d_v = d_vb * ck.beta
        lane = _iota2((CHUNK, LANES), 1)
        for d, (dq_ref, dg_ref) in enumerate(((dqf_ref, dgf_ref), (dqb_ref, dgb_ref))):
            acc = dg_ref[...]
            for hh in range(hb):
                c, h = d * hb + hh, b * hb + hh
                sl = slice(hh * GDN_DIM, (hh + 1) * GDN_DIM)
                dq_ref[0, :, sl] = d_q[c]
                dq_ref[1, :, sl] = d_k[c]
                dq_ref[2, :, sl] = d_v[c]
                acc = jnp.where(lane == h + 8 * d, d_gc[c], acc)
                acc = jnp.where(lane == 16 + h + 8 * d, d_beta[c], acc)
            dg_ref[...] = acc

        @pl.when(b == nb - 1)
        def _():
            for d, dg_ref in enumerate((dgf_ref, dgb_ref)):
                blk = dg_ref[...]
                dg_ref[...] = jnp.where(lane < 16, _exact01(_chunk_tri(CHUNK, d == 0), blk), blk)

    fw, bw = (lambda n: N - 1 - n), (lambda n: n)
    qsp = lambda ci: pl.BlockSpec((3, CHUNK, W), lambda n, b: (0, ci(n), b))
    gsp = lambda ci: pl.BlockSpec((CHUNK, LANES), lambda n, b: (ci(n), 0))
    osp = lambda ci: pl.BlockSpec((CHUNK, W), lambda n, b: (ci(n), b))
    ssp = lambda ci: pl.BlockSpec((1, hb, GDN_DIM, GDN_DIM), lambda n, b: (ci(n), b, 0, 0))
    asp = lambda ci: pl.BlockSpec((1, hb, CHUNK, CHUNK), lambda n, b: (ci(n), b, 0, 0))
    dq_shape = jax.ShapeDtypeStruct((3, T, nh * GDN_DIM), f32)
    dg_shape = jax.ShapeDtypeStruct((T, LANES), f32)
    return pl.pallas_call(
        body,
        out_shape=(dq_shape, dq_shape, dg_shape, dg_shape),
        grid=(N, nb),
        in_specs=[qsp(fw), qsp(bw), gsp(fw), gsp(bw), gsp(fw), gsp(bw), ssp(fw), ssp(bw), asp(fw), asp(bw), osp(fw), osp(bw)],
        out_specs=(qsp(fw), qsp(bw), gsp(fw), gsp(bw)),
        scratch_shapes=[pltpu.VMEM((2, nh, GDN_DIM, GDN_DIM), f32)],
        compiler_params=_cparams(("arbitrary", "arbitrary")),
        name="gdn_scan_bwd",
    )(qkvn, qkvn, gates, gates, gc, gc, st_f, st_b, a_f, a_b, do, do)


class _GlaBatch:
    def __init__(self, q, k, G, nf):
        B, Cn = q.shape[0], q.shape[1]
        self.incl, _, _ = _dir_masks(B, nf, Cn)
        self.eG = jnp.exp(G)
        self.enG = jnp.exp(-G)
        self.qg = q * (GLA_DK**-0.5) * self.eG
        self.kg = k * self.enG
        self.attn = jnp.where(self.incl, _mx(self.qg, self.kg, "nt"), 0.0)
        gl = _last_row(G, nf)
        self.ekd = jnp.exp(gl - G)
        self.kd = k * self.ekd
        self.egl = jnp.exp(gl)


def _gla_chain_inputs(refs_f, refs_b):
    nh = GLA_HEADS
    out = []
    for i, width in enumerate((GLA_DK, GLA_DK, GLA_DV, GLA_DK)):
        out.append(jnp.stack(_heads(refs_f[i], (), nh, width) + _heads(refs_b[i], (), nh, width)))
    return out


def _gla_specs(N, ci):
    nh = GLA_HEADS
    return [
        pl.BlockSpec((CHUNK, nh * GLA_DK), lambda n: (ci(n), 4096 // (nh * GLA_DK))),
        pl.BlockSpec((CHUNK, nh * GLA_DK), lambda n: (ci(n), 4608 // (nh * GLA_DK))),
        pl.BlockSpec((CHUNK, nh * GLA_DV), lambda n: (ci(n), 5120 // (nh * GLA_DV))),
        pl.BlockSpec((CHUNK, nh * GLA_DK), lambda n: (ci(n), 0)),
    ]


def _gla_fwd(proj, G_f, G_b):
    T = proj.shape[0]
    N = T // CHUNK
    nh = GLA_HEADS

    def body(qf, kf, vf, gf, qb, kb, vb, gb, of_ref, ob_ref, sf_ref, sb_ref, s_scr):
        n = pl.program_id(0)

        @pl.when(n == 0)
        def _():
            s_scr[...] = jnp.zeros((2 * nh, GLA_DV, GLA_DK), f32)

        q, k, v, G = _gla_chain_inputs((qf, kf, vf, gf), (qb, kb, vb, gb))
        ck = _GlaBatch(q, k, G, nh)
        St = s_scr[...]
        sf_ref[0], sb_ref[0] = St[:nh], St[nh:]
        o = _mx(ck.qg, St, "nt") + _mx(ck.attn, v, "nn")
        for i in range(nh):
            sl = slice(i * GLA_DV, (i + 1) * GLA_DV)
            of_ref[:, sl] = o[i]
            ob_ref[:, sl] = o[nh + i]
        s_scr[...] = St * ck.egl + _mx(v, ck.kd, "tn")

    fw, bw = (lambda n: n), (lambda n: N - 1 - n)
    osp = lambda ci: pl.BlockSpec((CHUNK, nh * GLA_DV), lambda n: (ci(n), 0))
    ssp = lambda ci: pl.BlockSpec((1, nh, GLA_DV, GLA_DK), lambda n: (ci(n), 0, 0, 0))
    o_shape = jax.ShapeDtypeStruct((T, nh * GLA_DV), f32)
    s_shape = jax.ShapeDtypeStruct((N, nh, GLA_DV, GLA_DK), f32)
    return pl.pallas_call(
        body,
        out_shape=(o_shape, o_shape, s_shape, s_shape),
        grid=(N,),
        in_specs=_gla_specs(N, fw) + _gla_specs(N, bw),
        out_specs=(osp(fw), osp(bw), ssp(fw), ssp(bw)),
        scratch_shapes=[pltpu.VMEM((2 * nh, GLA_DV, GLA_DK), f32)],
        compiler_params=_cparams(("arbitrary",)),
        name="gla_scan_fwd",
    )(proj, proj, proj, G_f, proj, proj, proj, G_b)


def _gla_bwd(proj, G_f, G_b, st_f, st_b, do):
    T = proj.shape[0]
    N = T // CHUNK
    nh = GLA_HEADS

    def body(qf, kf, vf, gf, sf_ref, dof_ref, qb, kb, vb, gb, sb_ref, dob_ref,
             dqf_ref, dkf_ref, dvf_ref, dgf_ref, dqb_ref, dkb_ref, dvb_ref, dgb_ref, ds_scr):
        n = pl.program_id(0)

        @pl.when(n == 0)
        def _():
            ds_scr[...] = jnp.zeros((2 * nh, GLA_DV, GLA_DK), f32)

        q, k, v, G = _gla_chain_inputs((qf, kf, vf, gf), (qb, kb, vb, gb))
        ck = _GlaBatch(q, k, G, nh)
        St = jnp.concatenate([sf_ref[0], sb_ref[0]], axis=0)
        dSt = ds_scr[...]
        dov = jnp.stack(_heads(dof_ref, (), nh, GLA_DV) + _heads(dob_ref, (), nh, GLA_DV))
        d_attn = jnp.where(ck.incl, _mx(dov, v, "nt"), 0.0)
        d_qg = _mx(dov, St, "nn") + _mx(d_attn, ck.kg, "nn")
        d_kg = _mx(d_attn, ck.qg, "tn")
        d_v = _mx(ck.attn, dov, "tn") + _mx(ck.kd, dSt, "nt")
        d_kd = _mx(v, dSt, "nn")
        ds_scr[...] = dSt * ck.egl + _mx(dov, ck.qg, "tn")
        kdd = d_kd * ck.kd
        d_gl = jnp.sum(dSt * St, axis=1, keepdims=True) * ck.egl + jnp.sum(kdd, axis=1, keepdims=True)
        d_q = d_qg * ck.eG * (GLA_DK**-0.5)
        d_k = d_kg * ck.enG + d_kd * ck.ekd
        d_G = _add_at_last_row(d_qg * ck.qg - d_kg * ck.kg - kdd, d_gl, nh)
        shp = (2 * nh, CHUNK, CHUNK)
        bb, ii, jj = _iota3(shp, 0), _iota3(shp, 1), _iota3(shp, 2)
        tri = jnp.logical_or(jnp.logical_and(bb < nh, jj >= ii), jnp.logical_and(bb >= nh, jj <= ii))
        d_gk = _exact01(jnp.where(tri, 1.0, 0.0), d_G)
        for d, (dq_ref, dk_ref, dv_ref, dg_ref) in enumerate(((dqf_ref, dkf_ref, dvf_ref, dgf_ref), (dqb_ref, dkb_ref, dvb_ref, dgb_ref))):
            for i in range(nh):
                c = d * nh + i
                ks, vs = slice(i * GLA_DK, (i + 1) * GLA_DK), slice(i * GLA_DV, (i + 1) * GLA_DV)
                dq_ref[:, ks] = d_q[c]
                dk_ref[:, ks] = d_k[c]
                dv_ref[:, vs] = d_v[c]
                dg_ref[:, ks] = d_gk[c]

    def in_specs(ci):
        return _gla_specs(N, ci) + [
            pl.BlockSpec((1, nh, GLA_DV, GLA_DK), lambda n: (ci(n), 0, 0, 0)),
            pl.BlockSpec((CHUNK, nh * GLA_DV), lambda n: (ci(n), 0)),
        ]

    def out_specs(ci):
        ksp = pl.BlockSpec((CHUNK, nh * GLA_DK), lambda n: (ci(n), 0))
        return [ksp, ksp, pl.BlockSpec((CHUNK, nh * GLA_DV), lambda n: (ci(n), 0)), ksp]

    fw, bw = (lambda n: N - 1 - n), (lambda n: n)
    k_shape = jax.ShapeDtypeStruct((T, nh * GLA_DK), f32)
    v_shape = jax.ShapeDtypeStruct((T, nh * GLA_DV), f32)
    return pl.pallas_call(
        body,
        out_shape=(k_shape, k_shape, v_shape, k_shape) * 2,
        grid=(N,),
        in_specs=in_specs(fw) + in_specs(bw),
        out_specs=tuple(out_specs(fw) + out_specs(bw)),
        scratch_shapes=[pltpu.VMEM((2 * nh, GLA_DV, GLA_DK), f32)],
        compiler_params=_cparams(("arbitrary",)),
        name="gla_scan_bwd",
    )(proj, proj, proj, G_f, st_f, do, proj, proj, proj, G_b, st_b, do)


def _gla_dsum(dq_f, dq_b, dk_f, dk_b, dv_f, dv_b, dproj, tb=256):
    T = dq_f.shape[0]
    wk, wv = GLA_HEADS * GLA_DK, GLA_HEADS * GLA_DV

    def body(qf, qb, kf, kb, vf, vb, _, o_ref):
        o_ref[:, :wk] = (qf[...] + qb[...]).astype(o_ref.dtype)
        o_ref[:, wk : 2 * wk] = (kf[...] + kb[...]).astype(o_ref.dtype)
        o_ref[:, 2 * wk :] = (vf[...] + vb[...]).astype(o_ref.dtype)

    ksp = pl.BlockSpec((tb, wk), lambda i: (i, 0))
    vsp = pl.BlockSpec((tb, wv), lambda i: (i, 0))
    return pl.pallas_call(
        body,
        out_shape=jax.ShapeDtypeStruct(dproj.shape, dproj.dtype),
        grid=(T // tb,),
        in_specs=[ksp, ksp, ksp, ksp, vsp, vsp, _ANY_SPEC],
        out_specs=pl.BlockSpec((tb, 2 * wk + wv), lambda i: (i, DP_GLA_QKV_BLK)),
        input_output_aliases={6: 0},
        compiler_params=_cparams(("parallel",)),
        name="gla_dsum",
    )(dq_f, dq_b, dk_f, dk_b, dv_f, dv_b, dproj)


def _local_step(x, target, wcat, conv_w, avec, dvec, gdn_norm_w, w2f, w2b, b2f, b2b, gla_norm_w, proj_weights, w_pre, w_post,
                send_proj_grads=None, send_in_grads=None):
    h, h_t = _rms_pre(x, w_pre)
    proj = _matmul(h, wcat, "nt", f32, "proj_in", 2048, 512, 1024)
    qkvn = _gdn_prep(proj, conv_w)
    gates, gc = _gdn_gates(proj, avec, dvec)
    oa_f, oa_b, sa_f, sa_b, inv_f, inv_b = _gdn_fwd(qkvn, gates, gc)
    og = _out_norm(oa_f, oa_b, proj, Z_BLK, gdn_norm_w, GDN_DIM, "gdn_out")
    G_f, G_b = _gla_prep(proj, w2f, w2b, b2f, b2b)
    ob_f, ob_b, sb_f, sb_b = _gla_fwd(proj, G_f, G_b)
    obg = _out_norm(ob_f, ob_b, proj, G_B_BLK, gla_norm_w, GLA_DV, "gla_out")
    wpg, wpl, wout = proj_weights(obg)
    (loss_row, d_wpost, dy, m, dout, dya, dyb, dproj, dog, dobg) = _mid(og, obg, proj, x, target, wpg, wpl, wout, w_post)

    d_wout = _matmul(m, dout, "tn", MXU_DTYPE, "dw_out", 512, 512, 2048)
    d_wpg = _matmul(og, dya, "tn", MXU_DTYPE, "dw_proj_gdn", 512, 512, 2048)
    d_wpl = _matmul(obg, dyb, "tn", MXU_DTYPE, "dw_proj_gla", 512, 512, 2048)
    if send_proj_grads is not None:
        gla_norm_w = gla_norm_w + send_proj_grads(d_wpg, d_wpl, d_wout)

    do_b, dproj, d_gla_norm = _out_norm_bwd(dobg, ob_f, ob_b, proj, G_B_BLK, gla_norm_w, GLA_DV, "gla_out_bwd", dproj)
    dq_f, dk_f, dv_f, dgk_f, dq_b, dk_b, dv_b, dgk_b = _gla_bwd(proj, G_f, G_b, sb_f, sb_b, do_b)
    dproj = _gla_dsum(dq_f, dq_b, dk_f, dk_b, dv_f, dv_b, dproj)
    dsmall_gla, d_w2f, d_w2b, d_b2f, d_b2b = _gla_prep_bwd(proj, w2f, w2b, b2f, b2b, dgk_f, dgk_b)

    do_a, dproj, d_gdn_norm = _out_norm_bwd(dog, oa_f, oa_b, proj, Z_BLK, gdn_norm_w, GDN_DIM, "gdn_out_bwd", dproj)
    dqkv_f, dqkv_b, dg_f, dg_b = _gdn_bwd(qkvn, gates, gc, sa_f, sa_b, inv_f, inv_b, do_a)
    dproj, d_conv = _gdn_prep_bwd(proj, conv_w, dqkv_f, dqkv_b, dproj)
    dproj, d_alog, d_dtb = _gdn_gates_bwd(proj, avec, dvec, dg_f, dg_b, dsmall_gla, dproj)

    d_wcat = _matmul(h_t, dproj, "nn", MXU_DTYPE, "dw_in", 1024, 512, 2048, out_t=True)
    started = None if send_in_grads is None else send_in_grads(d_wcat)
    dh = _matmul(dproj, wcat, "nn", f32, "dh", 1024, 1024, 2432, after=started)
    grad_x, d_wpre = _rms_pre_bwd(x, w_pre, dh, dy)
    return dict(loss_row=loss_row, grad_x=grad_x, d_wcat=d_wcat, d_conv=d_conv, d_alog=d_alog, d_dtb=d_dtb,
                d_gdn_norm=d_gdn_norm, d_wpg=d_wpg, d_w2f=d_w2f, d_w2b=d_w2b, d_b2f=d_b2f, d_b2b=d_b2b,
                d_gla_norm=d_gla_norm, d_wpl=d_wpl, d_wout=d_wout, d_wpost=d_wpost, d_wpre=d_wpre)


def _to_cat(wf):
    pad = jnp.zeros((N_CAT - N_IN,) + wf.shape[1:], wf.dtype)
    return jnp.concatenate([wf[:4096], wf[4128:6176], wf[7232:9280], wf[6176:7200], wf[4096:4128], wf[7200:7232], pad], axis=0)


def _from_cat(wc):
    return jnp.concatenate([wc[:4096], wc[9216:9248], wc[4096:6144], wc[8192:9216], wc[9248:9280], wc[6144:8192]], axis=0)


def _lane_row(*pieces):
    row = jnp.concatenate(pieces, axis=1)
    return jnp.pad(row, ((0, 0), (0, LANES - row.shape[1])))


def _pad_w2(w2, row0):
    return jnp.pad(w2, ((row0, LANES - row0 - w2.shape[0]), (0, 0)))


SM_ROWS, SM_COLS = 32, 384
REP_ROWS = 32


def _pack_small_shard(conv, w2f, w2b):
    lead = conv.shape[:-2]
    z = lambda r, c: jnp.zeros(lead + (r, c), f32)
    top = jnp.concatenate([conv, z(8 - CONV_K, SM_COLS)], axis=-2)
    mid = jnp.concatenate([w2f, w2b, z(16, SM_COLS - 128)], axis=-1)
    return jnp.concatenate([top, mid, z(SM_ROWS - 24, SM_COLS)], axis=-2)


LOSS_ROW = 28


def _pack_rep(ln_pre, ln_post, b2f, b2b, gla_norm, gdn_norm, alog_f, alog_b, dt_f, dt_b, loss=None):
    last = _lane_row(alog_f, alog_b, dt_f, dt_b)
    extra = jnp.zeros((1, LANES), f32) if loss is None else _lane_row(loss)
    rows = [ln_pre.reshape(8, LANES), ln_post.reshape(8, LANES), b2f.reshape(4, LANES), b2b.reshape(4, LANES),
            gla_norm.reshape(2, LANES), gdn_norm.reshape(1, LANES), last, extra, jnp.zeros((REP_ROWS - LOSS_ROW - 1, LANES), f32)]
    return jnp.concatenate(rows, axis=0)


def _unpack_rep(p):
    return dict(ln_pre_w=p[0:8].reshape(1, 1024), ln_post_w=p[8:16].reshape(1, 1024), gk_b2_fwd=p[16:20].reshape(1, 512),
                gk_b2_bwd=p[20:24].reshape(1, 512), gla_norm_w=p[24:26].reshape(1, 256), gdn_norm_w=p[26:27],
                a_log_fwd=p[27:28, 0:8], a_log_bwd=p[27:28, 8:16], dt_bias_fwd=p[27:28, 16:24], dt_bias_bwd=p[27:28, 24:32])


_MESH = pl.DeviceIdType.MESH


N_BIG_COPIES, N_SMALL_COPIES = 9, 7


def _gather_weights(w_in_s, sm_s):
    shapes = (w_in_s.shape, sm_s.shape)
    dtypes = (MXU_DTYPE, f32)

    def body(win_ref, sm_ref, gin_ref, gsm_ref, send_sems, recv_sems):
        x, y, c = lax.axis_index("x"), lax.axis_index("y"), lax.axis_index("c")
        me, sibling = (x, y, c), (x, y, 1 - c)
        chips = [(1 - x, y), (x, 1 - y), (1 - x, 1 - y)]
        bufs = (gin_ref, gsm_ref)

        def idx(px, py, pc):
            return 4 * px + 2 * py + pc

        for r in range(w_in_s.shape[1] // LANES):
            cols = slice(r * LANES, (r + 1) * LANES)
            gin_ref[idx(*me), :, cols] = win_ref[:, cols].astype(MXU_DTYPE)
        gsm_ref[idx(*me)] = sm_ref[...]

        def copy(sem, blk, to):
            return pltpu.make_async_remote_copy(src_ref=blk, dst_ref=blk, send_sem=send_sems.at[sem], recv_sem=recv_sems.at[sem],
                                                device_id=to, device_id_type=_MESH)

        def small(k, block, to):
            return copy(N_BIG_COPIES + k, gsm_ref.at[idx(*block)], to)

        half = w_in_s.shape[1] // 2
        xn, yn, dg = (1 - x, y), (x, 1 - y), (1 - x, 1 - y)

        def big(sem, block, to, part=None):
            blk = gin_ref.at[idx(*block)]
            return copy(sem, blk if part is None else blk.at[:, pl.ds(part * half, half)], to)

        started = [big(0, me, sibling), big(1, me, (*xn, c)), big(2, me, (*yn, c)), small(0, me, sibling)]
        started += [small(1 + j, me, (*chip, c)) for j, chip in enumerate(chips)]
        for cp in started:
            cp.start()

        def then(arrived, forwards):
            arrived.wait_recv()
            for cp in forwards:
                cp.start()
            started.extend(forwards)

        then(big(1, (*xn, c), me), [big(3, (*xn, c), (*yn, c), 1), big(5, (*xn, c), sibling)])
        then(big(2, (*yn, c), me), [big(4, (*yn, c), (*xn, c), 0), big(6, (*yn, c), sibling)])
        then(big(4, (*dg, c), me, 0), [big(7, (*dg, c), sibling, 0)])
        then(big(3, (*dg, c), me, 1), [big(8, (*dg, c), sibling, 1)])
        for j, chip in enumerate(chips):
            then(small(1 + j, (*chip, c), me), [small(4 + j, (*chip, c), sibling)])
        big(0, sibling, me).wait_recv()
        big(5, (*xn, 1 - c), me).wait_recv()
        big(6, (*yn, 1 - c), me).wait_recv()
        big(7, (*dg, 1 - c), me, 0).wait_recv()
        big(8, (*dg, 1 - c), me, 1).wait_recv()
        small(0, sibling, me).wait_recv()
        for j, chip in enumerate(chips):
            small(4 + j, (*chip, 1 - c), me).wait_recv()
        for cp in started:
            cp.wait_send()

    vm = pl.BlockSpec(memory_space=pltpu.VMEM)
    return pl.pallas_call(
        body,
        out_shape=tuple(jax.ShapeDtypeStruct((N_DEV,) + s, d) for s, d in zip(shapes, dtypes)),
        in_specs=[vm, vm],
        out_specs=(vm, vm),
        scratch_shapes=[pltpu.SemaphoreType.DMA((N_BIG_COPIES + N_SMALL_COPIES,)), pltpu.SemaphoreType.DMA((N_BIG_COPIES + N_SMALL_COPIES,))],
        compiler_params=pltpu.CompilerParams(vmem_limit_bytes=VMEM_LIMIT),
        name="gather_weights",
    )(w_in_s, sm_s)


N_CHIP = 4
_EFFECT = pltpu.SideEffectType.DATAFLOW_SIDE_EFFECTING
_HBM_SPEC = pl.BlockSpec(memory_space=pltpu.HBM)
_SEM_SPEC = pl.BlockSpec(memory_space=pltpu.SEMAPHORE)


def _relation_peer(k, x, y, c):
    px = (1 - x) if (k & 4) else x
    py = (1 - y) if (k & 2) else y
    pc = (1 - c) if (k & 1) else c
    return (px, py, pc), 4 * px + 2 * py + pc


_PLAN_COPIES = {"gather": N_DEV - 1, "blocks": N_DEV - 1, "sibling": 4, "chips": 3}
_PLAN_SLOTS = {"gather": N_DEV, "blocks": N_DEV, "sibling": 4, "chips": 4}


def _plan_copies(plan, src_ref, land_ref, send_sems, recv_sems, arrival):
    x, y, c = lax.axis_index("x"), lax.axis_index("y"), lax.axis_index("c")
    me, my_chip = 4 * x + 2 * y + c, 2 * x + y
    out = []

    def add(i, src, there, here, dev):
        out.append(pltpu.make_async_remote_copy(src_ref=src, dst_ref=land_ref.at[here if arrival else there], send_sem=send_sems.at[i],
                                                recv_sem=recv_sems.at[i], device_id=dev, device_id_type=_MESH))

    if plan in ("gather", "blocks"):
        for k in range(1, N_DEV):
            dev, p = _relation_peer(k, x, y, c)
            add(k - 1, src_ref.at[p] if plan == "blocks" else src_ref, me, p, dev)
    elif plan == "sibling":
        for j in range(4):
            add(j, src_ref.at[2 * j + 1 - c], j, j, (x, y, 1 - c))
    else:
        for i, k in enumerate((2, 4, 6)):
            (px, py, pc), _ = _relation_peer(k, x, y, c)
            add(i, src_ref.at[2 * px + py], my_chip, 2 * px + py, (px, py, pc))
    return out


def _exchange_start(src, plan, name):
    n = _PLAN_COPIES[plan]
    land = lax.empty((_PLAN_SLOTS[plan],) + (src.shape if plan == "gather" else src.shape[1:]), src.dtype)

    def body(src_ref, land_ref, send_sems, recv_sems, src_thru, land_thru, token):
        for cp in _plan_copies(plan, src_ref, land_ref, send_sems, recv_sems, False):
            cp.start()
        token[...] = jnp.zeros_like(token)

    return pl.pallas_call(
        body,
        name=name,
        out_shape=(pltpu.SemaphoreType.DMA((n,)), pltpu.SemaphoreType.DMA((n,)), pltpu.HBM(src.shape, src.dtype),
                   pltpu.HBM(land.shape, land.dtype), jax.ShapeDtypeStruct((8, LANES), f32)),
        in_specs=(_HBM_SPEC, _HBM_SPEC),
        out_specs=(_SEM_SPEC, _SEM_SPEC, _HBM_SPEC, _HBM_SPEC, pl.BlockSpec(memory_space=pltpu.VMEM)),
        input_output_aliases={0: 2, 1: 3},
        compiler_params=pltpu.CompilerParams(has_side_effects=_EFFECT),
    )(pltpu.with_memory_space_constraint(src, pltpu.HBM), pltpu.with_memory_space_constraint(land, pltpu.HBM))


def _exchange_wait(started, after, plan, name):
    send_sems, recv_sems, src_thru, land_thru, _ = started

    def body(src_ref, land_ref, send_sems, recv_sems, after_ref, src_dead, got_ref):
        for cp in _plan_copies(plan, src_ref, land_ref, send_sems, recv_sems, True):
            cp.wait_send()
            cp.wait_recv()

    return pl.pallas_call(
        body,
        name=name,
        out_shape=(pltpu.HBM(src_thru.shape, src_thru.dtype), pltpu.HBM(land_thru.shape, land_thru.dtype)),
        in_specs=(_HBM_SPEC, _HBM_SPEC, _SEM_SPEC, _SEM_SPEC, pl.BlockSpec(memory_space=pl.ANY)),
        out_specs=(_HBM_SPEC, _HBM_SPEC),
        input_output_aliases={0: 0, 1: 1},
        compiler_params=pltpu.CompilerParams(has_side_effects=_EFFECT),
    )(src_thru, land_thru, send_sems, recv_sems, after)


def _pair_sum(g, r, name, tc=LANES):
    _, R, Cc = g.shape

    def body(g_ref, r_ref, o_ref):
        o_ref[...] = (g_ref[...].astype(f32) + r_ref[...].astype(f32)).astype(o_ref.dtype)

    return pl.pallas_call(
        body,
        out_shape=jax.ShapeDtypeStruct(r.shape, r.dtype),
        grid=(N_CHIP, Cc // tc),
        in_specs=[pl.BlockSpec((1, R, tc), lambda k, i: (2 * k + lax.axis_index("c"), 0, i)), pl.BlockSpec((1, R, tc), lambda k, i: (k, 0, i))],
        out_specs=pl.BlockSpec((1, R, tc), lambda k, i: (k, 0, i)),
        compiler_params=_cparams(("parallel", "parallel")),
        name=name,
    )(g, r)


def _exchange_small(g_sm, g_rep):
    srcs = (g_sm, g_rep)

    def body(in0, in1, out0, out1, send_sems, recv_sems, local_sems):
        x, y, c = lax.axis_index("x"), lax.axis_index("y"), lax.axis_index("c")
        me = 4 * x + 2 * y + c
        ins, outs = (in0, in1), (out0, out1)

        def copy(a, k, arrival):
            dev, p = _relation_peer(k, x, y, c)
            return pltpu.make_async_remote_copy(src_ref=ins[a].at[p] if a == 0 else ins[a], dst_ref=outs[a].at[p if arrival else me],
                                                send_sem=send_sems.at[7 * a + k - 1], recv_sem=recv_sems.at[7 * a + k - 1],
                                                device_id=dev, device_id_type=_MESH)

        pairs = [(a, k) for a in (0, 1) for k in range(1, N_DEV)]
        sends = [copy(a, k, False) for a, k in pairs]
        for cp in sends:
            cp.start()
        local = [pltpu.make_async_copy(ins[0].at[me], outs[0].at[me], local_sems.at[0]),
                 pltpu.make_async_copy(ins[1], outs[1].at[me], local_sems.at[1])]
        for cp in local:
            cp.start()
        for a, k in pairs:
            copy(a, k, True).wait_recv()
        for cp in sends:
            cp.wait_send()
        for cp in local:
            cp.wait()

    return pl.pallas_call(
        body,
        out_shape=(jax.ShapeDtypeStruct(g_sm.shape, g_sm.dtype), jax.ShapeDtypeStruct((N_DEV,) + g_rep.shape, g_rep.dtype)),
        in_specs=[_HBM_SPEC] * 2,
        out_specs=(_HBM_SPEC,) * 2,
        scratch_shapes=[pltpu.SemaphoreType.DMA((14,)), pltpu.SemaphoreType.DMA((14,)), pltpu.SemaphoreType.DMA((2,))],
        name="exchange_small",
    )(*srcs)


def _sum_adam(parts, w, m, v, name, tb, tc=None, own=None, own_slot=None):
    R, Cc = w.shape
    n_parts = parts.shape[0]
    tb = R if tc else min(tb, R)
    tc = tc or Cc
    assert R % tb == 0 and Cc % tc == 0
    bc1 = 1.0 - ADAM_B1**ADAM_STEP
    bc2 = 1.0 - ADAM_B2**ADAM_STEP
    extra = [] if own is None else [own]

    def body(p_ref, w_ref, m_ref, v_ref, *rest):
        g_ref, d_ref, nm_ref, nv_ref = rest[len(extra) :]

        def part(j):
            pj = p_ref[j].astype(f32)
            return pj if own is None else jnp.where(own_slot() == j, rest[0][0].astype(f32), pj)

        g = part(0)
        for j in range(1, n_parts):
            g = g + part(j)
        g_ref[...] = g
        m2 = ADAM_B1 * m_ref[...] + (1.0 - ADAM_B1) * g
        v2 = ADAM_B2 * v_ref[...] + (1.0 - ADAM_B2) * (g * g)
        nm_ref[...] = m2
        nv_ref[...] = v2
        d_ref[...] = -ADAM_LR * ((m2 / bc1) / (jnp.sqrt(v2 / bc2) + ADAM_EPS) + ADAM_WD * w_ref[...])

    blk = pl.BlockSpec((tb, tc), lambda i, j: (i, j))
    o = jax.ShapeDtypeStruct((R, Cc), f32)
    return pl.pallas_call(
        body,
        out_shape=(o, o, o, o),
        grid=(R // tb, Cc // tc),
        in_specs=[pl.BlockSpec((n_parts, tb, tc), lambda i, j: (0, i, j)), blk, blk, blk]
        + [pl.BlockSpec((1, tb, tc), lambda i, j: (own_slot(), i, j))] * len(extra),
        out_specs=(blk, blk, blk, blk),
        compiler_params=_cparams(("parallel", "parallel")),
        name=name,
    )(parts, w, m, v, *extra)


def _my_device():
    return 4 * lax.axis_index("x") + 2 * lax.axis_index("y") + lax.axis_index("c")


def _my_chip():
    return 2 * lax.axis_index("x") + lax.axis_index("y")


_WEIGHTS = ["ln_pre_w", "w_in", "conv_w", "a_log_fwd", "a_log_bwd", "dt_bias_fwd", "dt_bias_bwd", "gdn_norm_w", "w_proj_gdn",
            "gk_w2_fwd", "gk_b2_fwd", "gk_w2_bwd", "gk_b2_bwd", "gla_norm_w", "w_proj_gla", "w_out", "ln_post_w"]
_REP_ORDER = ["ln_pre_w", "ln_post_w", "gk_b2_fwd", "gk_b2_bwd", "gla_norm_w", "gdn_norm_w", "a_log_fwd", "a_log_bwd", "dt_bias_fwd", "dt_bias_bwd"]


def kernel(x, ln_pre_w, w_in, conv_w, a_log_fwd, a_log_bwd, dt_bias_fwd, dt_bias_bwd, gdn_norm_w, w_proj_gdn, gk_w2_fwd, gk_b2_fwd, gk_w2_bwd, gk_b2_bwd, gla_norm_w, w_proj_gla, w_out, ln_post_w, loss_target, m_ln_pre_w, m_w_in, m_conv_w, m_a_log_fwd, m_a_log_bwd, m_dt_bias_fwd, m_dt_bias_bwd, m_gdn_norm_w, m_w_proj_gdn, m_gk_w2_fwd, m_gk_b2_fwd, m_gk_w2_bwd, m_gk_b2_bwd, m_gla_norm_w, m_w_proj_gla, m_w_out, m_ln_post_w, v_ln_pre_w, v_w_in, v_conv_w, v_a_log_fwd, v_a_log_bwd, v_dt_bias_fwd, v_dt_bias_bwd, v_gdn_norm_w, v_w_proj_gdn, v_gk_w2_fwd, v_gk_b2_fwd, v_gk_w2_bwd, v_gk_b2_bwd, v_gla_norm_w, v_w_proj_gla, v_w_out, v_ln_post_w):
    args = locals()
    W = {n: args[n] for n in _WEIGHTS}
    M = {n: args["m_" + n] for n in _WEIGHTS}
    V = {n: args["v_" + n] for n in _WEIGHTS}

    wp_stack = lambda P: jnp.concatenate([P["w_proj_gdn"], P["w_proj_gla"], P["w_out"]], axis=0)
    sm_pack = lambda P: _pack_small_shard(P["conv_w"][0], P["gk_w2_fwd"][0], P["gk_w2_bwd"][0])
    me = 4 * lax.axis_index("x") + 2 * lax.axis_index("y") + lax.axis_index("c")
    w_in_t, m_in_t, v_in_t = (jnp.transpose(a[0]) for a in (w_in, m_w_in, v_w_in))
    g_in, g_sm = _gather_weights(w_in_t, sm_pack(W))
    wp_own = wp_stack(W).astype(MXU_DTYPE)
    wp_started = _exchange_start(wp_own, "gather", "gather_proj_start")

    def proj_weights(after):
        own, land = _exchange_wait(wp_started, after, "gather", "gather_proj_wait")
        g_p = lax.dynamic_update_slice(land, own[None], (me, 0, 0, 0))
        return tuple(g_p[:, i].reshape(D_MODEL, D_MODEL) for i in range(3))

    rows_p = 3 * D_MODEL // N_DEV
    sent = {}

    def send_proj_grads(d_wpg, d_wpl, d_wout):
        p_p = jnp.stack([d_wpg, d_wpl, d_wout]).reshape(3, N_DEV, D_MODEL // N_DEV, D_MODEL).transpose(1, 0, 2, 3)
        sent["p_p"] = p_p.reshape(N_DEV, rows_p, D_MODEL)
        sent["started"] = _exchange_start(sent["p_p"], "blocks", "exchange_proj_start")
        return sent["started"][4][0:1, 0:1]

    def send_in_grads(d_wcat):
        p_in = _from_cat(d_wcat).reshape(N_DEV, SHARD_IN, D_MODEL)
        sib = _exchange_start(p_in, "sibling", "exchange_in_sibling_start")
        p_p, land_p = _exchange_wait(sent["started"], sib[4], "blocks", "exchange_proj_wait")
        sent["o_p"] = _sum_adam(land_p, wp_stack(W).reshape(rows_p, D_MODEL), wp_stack(M).reshape(rows_p, D_MODEL),
                                wp_stack(V).reshape(rows_p, D_MODEL), "adam_w_proj", 128, own=p_p, own_slot=_my_device)
        p_in, sib_land = _exchange_wait(sib, sent["o_p"][0], "sibling", "exchange_in_sibling_wait")
        sent["chips"] = _exchange_start(_pair_sum(p_in, sib_land, "pair_sum_w_in", tc=512), "chips", "exchange_in_chips_start")
        return sent["chips"][4]

    wcat = _to_cat(g_in.reshape(N_IN, D_MODEL))
    conv_full = g_sm[:, 0:CONV_K, :].transpose(1, 0, 2).reshape(CONV_K, N_DEV * SM_COLS)
    w2f_full = g_sm[:, 8:24, 0:64].transpose(1, 0, 2).reshape(16, 512)
    w2b_full = g_sm[:, 8:24, 64:128].transpose(1, 0, 2).reshape(16, 512)

    avec = _lane_row(a_log_fwd, a_log_bwd)
    dvec = _lane_row(dt_bias_fwd, dt_bias_bwd)
    G = _local_step(x[0], loss_target[0], wcat, conv_full, avec, dvec, gdn_norm_w, _pad_w2(w2f_full, 32), _pad_w2(w2b_full, 48),
                    gk_b2_fwd, gk_b2_bwd, gla_norm_w, proj_weights, ln_pre_w + wp_started[4][0:1, 0:1], ln_post_w, send_proj_grads,
                    send_in_grads)

    p_sm = _pack_small_shard(G["d_conv"].reshape(CONV_K, N_DEV, SM_COLS).transpose(1, 0, 2),
                             G["d_w2f"][32:48].reshape(16, N_DEV, 64).transpose(1, 0, 2),
                             G["d_w2b"][48:64].reshape(16, N_DEV, 64).transpose(1, 0, 2))
    p_rep = _pack_rep(G["d_wpre"], G["d_wpost"], G["d_b2f"], G["d_b2b"], G["d_gla_norm"], G["d_gdn_norm"],
                      G["d_alog"][:, 0:8], G["d_alog"][:, 8:16], G["d_dtb"][:, 0:8], G["d_dtb"][:, 8:16],
                      loss=0.5 * jnp.sum(G["loss_row"], axis=1, keepdims=True) / D_MODEL)
    r_sm, r_rep = _exchange_small(p_sm, p_rep)
    q_in, land_in = _exchange_wait(sent["chips"], r_rep, "chips", "exchange_in_chips_wait")
    o_in = _sum_adam(land_in, w_in_t, m_in_t, v_in_t, "adam_w_in", SHARD_IN, tc=256, own=q_in, own_slot=_my_chip)
    o_p = sent["o_p"]
    o_sm = _sum_adam(r_sm, sm_pack(W), sm_pack(M), sm_pack(V), "adam_small", SM_ROWS)
    rep_pack = lambda P: _pack_rep(*[P[n] for n in _REP_ORDER])
    o_rep = _sum_adam(r_rep, rep_pack(W), rep_pack(M), rep_pack(V), "adam_rep", REP_ROWS)

    res = []
    for kind in range(4):
        rep = _unpack_rep(o_rep[kind])
        per = dict(rep)
        per["w_in"] = jnp.transpose(o_in[kind])[None]
        pp = o_p[kind].reshape(3, 1, D_MODEL // N_DEV, D_MODEL)
        per["w_proj_gdn"], per["w_proj_gla"], per["w_out"] = pp[0], pp[1], pp[2]
        per["conv_w"] = o_sm[kind][None, 0:CONV_K, :]
        per["gk_w2_fwd"] = o_sm[kind][None, 8:24, 0:64]
        per["gk_w2_bwd"] = o_sm[kind][None, 8:24, 64:128]
        res.append([per[n] for n in _WEIGHTS])
    loss = o_rep[0][LOSS_ROW, 0]
    return (loss, G["grad_x"][None], *res[0], *res[1], *res[2], *res[3])
```

```python
import functools
import math

import jax
import jax.numpy as jnp
from jax import lax
from jax.experimental import pallas as pl
from jax.experimental.pallas import tpu as pltpu

f32 = jnp.float32
MXU_DTYPE = jnp.bfloat16
HI = lax.Precision.HIGHEST

D_MODEL = 1024
CHUNK = 64
NORM_EPS = 1e-6
GDN_HEADS, GDN_DIM = 8, 128
GLA_HEADS, GLA_DK, GLA_DV = 4, 128, 256
GATE_NORMALIZER = 16.0
CONV_K = 5
N_IN = 9280
N_DEV = 8
SHARD_IN = N_IN // N_DEV

N_CAT = 9728
SMALL_OFF = 9216
LANES = 128

ADAM_LR, ADAM_B1, ADAM_B2, ADAM_EPS, ADAM_WD, ADAM_STEP = 0.001, 0.9, 0.999, 1e-08, 0.01, 10

VMEM_LIMIT = 56 * 1024 * 1024


def _cparams(sem=None):
    return pltpu.CompilerParams(dimension_semantics=sem, vmem_limit_bytes=VMEM_LIMIT)


_DN = {"nn": (((1,), (0,)), ((), ())), "nt": (((1,), (1,)), ((), ())), "tn": (((0,), (0,)), ((), ()))}
_BDN = {"nn": (((2,), (1,)), ((0,), (0,))), "nt": (((2,), (2,)), ((0,), (0,))), "tn": (((1,), (1,)), ((0,), (0,)))}


def _dot16(a, b, kind):
    return lax.dot_general(a, b, (_DN if a.ndim == 2 else _BDN)[kind], preferred_element_type=f32)


def _mx(a, b, kind):
    return _dot16(a.astype(MXU_DTYPE), b.astype(MXU_DTYPE), kind)


def _split2(a):
    hi = a.astype(jnp.bfloat16)
    return hi, (a - hi.astype(f32)).astype(jnp.bfloat16)


def _split3(a):
    p1 = a.astype(jnp.bfloat16)
    r = a - p1.astype(f32)
    p2 = r.astype(jnp.bfloat16)
    return p1, p2, (r - p2.astype(f32)).astype(jnp.bfloat16)


def _h3s(a2, b2, kind):
    (ah, al), (bh, bl) = a2, b2
    return _dot16(ah, bh, kind) + (_dot16(ah, bl, kind) + _dot16(al, bh, kind))


def _h3(a, b, kind):
    return _h3s(_split2(a), _split2(b), kind)


def _exact01(t01, x, kind="nn"):
    t = t01.astype(jnp.bfloat16)
    x1, x2, x3 = _split3(x)
    return _dot16(t, x1, kind) + (_dot16(t, x2, kind) + _dot16(t, x3, kind))


def _colsum_col(e):
    ones = jnp.ones(e.shape[:-1] + (LANES,), jnp.bfloat16)
    e1, e2, e3 = _split3(e)
    return (_dot16(e1, ones, "tn") + (_dot16(e2, ones, "tn") + _dot16(e3, ones, "tn")))[..., 0:1]


def _iota3(shape, axis):
    return lax.broadcasted_iota(jnp.int32, shape, axis)


def _dir_masks(B, nf, Cn):
    shp = (B, Cn, Cn)
    bb, ii, jj = _iota3(shp, 0), _iota3(shp, 1), _iota3(shp, 2)
    fwd = bb < nf
    rev = jnp.logical_not(fwd)
    incl = jnp.logical_or(jnp.logical_and(fwd, ii >= jj), jnp.logical_and(rev, ii <= jj))
    strict = jnp.logical_and(incl, ii != jj)
    return incl, strict, ii == jj


def _last_row(x, nf):
    B, Cn = x.shape[0], x.shape[1]
    fwd = _iota3((B, 1, 1), 0) < nf
    return jnp.where(fwd, x[:, Cn - 1 : Cn, :], x[:, 0:1, :])


def _add_at_last_row(x, val, nf):
    B, Cn = x.shape[0], x.shape[1]
    bb, rr = _iota3((B, Cn, 1), 0), _iota3((B, Cn, 1), 1)
    at = jnp.logical_or(jnp.logical_and(bb < nf, rr == Cn - 1), jnp.logical_and(bb >= nf, rr == 0))
    return x + jnp.where(at, val, 0.0)


def _sigmoid(x):
    return jax.nn.sigmoid(x)


def _silu(x):
    return x * _sigmoid(x)


def _silu_grad(x):
    s = _sigmoid(x)
    return s * (1.0 + x * (1.0 - s))


def _softplus(x):
    u = jnp.exp(-jnp.abs(x))
    l1p = jnp.where(u < 1e-3, u * (1.0 - u * (0.5 - u * (1.0 / 3.0))), jnp.log(1.0 + u))
    return jnp.maximum(x, 0.0) + l1p


def _mxr(x):
    return x.astype(MXU_DTYPE).astype(f32)


def _iota2(shape, axis):
    return lax.broadcasted_iota(jnp.int32, shape, axis)


def _matmul(a, b, kind, out_dtype, name, tm, tn, tk, out_t=False, after=None):
    extra = [] if after is None else [after]
    if kind == "nn":
        (M, K), N = a.shape, b.shape[1]
    elif kind == "nt":
        (M, K), N = a.shape, b.shape[0]
    else:
        (K, M), N = a.shape, b.shape[1]
    tm, tn, tk = min(tm, M), min(tn, N), min(tk, K)
    assert M % tm == 0 and N % tn == 0 and K % tk == 0, (name, M, N, K)
    nk = K // tk
    if kind == "tn":
        a_spec = pl.BlockSpec((tk, tm), lambda i, j, k: (k, i))
    else:
        a_spec = pl.BlockSpec((tm, tk), lambda i, j, k: (i, k))
    if kind == "nt":
        b_spec = pl.BlockSpec((tn, tk), lambda i, j, k: (j, k))
    else:
        b_spec = pl.BlockSpec((tk, tn), lambda i, j, k: (k, j))

    def body(a_ref, b_ref, *rest):
        o_ref, acc = rest[len(extra)], rest[len(extra) + 1 :]
        p = _mx(a_ref[...], b_ref[...], kind)
        if nk == 1:
            o_ref[...] = (p.T if out_t else p).astype(out_dtype)
        else:
            assert not out_t
            acc_ref = acc[0]
            k = pl.program_id(2)

            @pl.when(k == 0)
            def _():
                acc_ref[...] = p

            @pl.when(k > 0)
            def _():
                acc_ref[...] += p

            @pl.when(k == nk - 1)
            def _():
                o_ref[...] = acc_ref[...].astype(out_dtype)

    return pl.pallas_call(
        body,
        out_shape=jax.ShapeDtypeStruct((N, M) if out_t else (M, N), out_dtype),
        grid=(M // tm, N // tn, nk),
        in_specs=[a_spec, b_spec] + [pl.BlockSpec((8, LANES), lambda i, j, k: (0, 0))] * len(extra),
        out_specs=pl.BlockSpec((tn, tm), lambda i, j, k: (j, i)) if out_t else pl.BlockSpec((tm, tn), lambda i, j, k: (i, j)),
        scratch_shapes=[] if nk == 1 else [pltpu.VMEM((tm, tn), f32)],
        compiler_params=_cparams(("parallel", "parallel", "arbitrary")),
        name=name,
    )(a, b, *extra)


def _rms_pre(x, w, tb=256):
    T = x.shape[0]

    def body(x_ref, w_ref, h_ref, ht_ref):
        xv = x_ref[...]
        r = lax.rsqrt(jnp.mean(xv * xv, axis=1, keepdims=True) + NORM_EPS)
        h = xv * r * w_ref[...]
        h_ref[...] = h.astype(h_ref.dtype)
        ht_ref[...] = h.T.astype(ht_ref.dtype)

    return pl.pallas_call(
        body,
        out_shape=(jax.ShapeDtypeStruct((T, D_MODEL), MXU_DTYPE), jax.ShapeDtypeStruct((D_MODEL, T), MXU_DTYPE)),
        grid=(T // tb,),
        in_specs=[pl.BlockSpec((tb, D_MODEL), lambda i: (i, 0)), pl.BlockSpec((1, D_MODEL), lambda i: (0, 0))],
        out_specs=(pl.BlockSpec((tb, D_MODEL), lambda i: (i, 0)), pl.BlockSpec((D_MODEL, tb), lambda i: (0, i))),
        compiler_params=_cparams(("parallel",)),
        name="rms_pre",
    )(x, w)


def _rms_pre_bwd(x, w, dh, dy, tb=256):
    T = x.shape[0]

    def body(x_ref, w_ref, dh_ref, dy_ref, dx_ref, dw_ref):
        i = pl.program_id(0)
        xv, dhv = x_ref[...], dh_ref[...]
        r = lax.rsqrt(jnp.mean(xv * xv, axis=1, keepdims=True) + NORM_EPS)
        dhw = dhv * w_ref[...]
        dx_ref[...] = dy_ref[...] + r * dhw - xv * (r * r * r) * jnp.mean(dhw * xv, axis=1, keepdims=True)
        part = jnp.sum(dhv * xv * r, axis=0, keepdims=True)

        @pl.when(i == 0)
        def _():
            dw_ref[...] = part

        @pl.when(i > 0)
        def _():
            dw_ref[...] += part

    blk = pl.BlockSpec((tb, D_MODEL), lambda i: (i, 0))
    row = pl.BlockSpec((1, D_MODEL), lambda i: (0, 0))
    return pl.pallas_call(
        body,
        out_shape=(jax.ShapeDtypeStruct((T, D_MODEL), f32), jax.ShapeDtypeStruct((1, D_MODEL), f32)),
        grid=(T // tb,),
        in_specs=[blk, row, blk, blk],
        out_specs=(blk, row),
        compiler_params=_cparams(("arbitrary",)),
        name="rms_pre_bwd",
    )(x, w, dh, dy)


SHIFTS = (-2, -1, 1, 2)


def _shifted(x):
    T = x.shape[0]
    t = _iota2(x.shape, 0)
    out = {}
    for d in SHIFTS:
        valid = (t < T - d) if d > 0 else (t >= -d)
        out[d] = jnp.where(valid, pltpu.roll(x, (-d) % T, 0), 0.0)
    return out


def _conv5(u, ush, cw):
    acc = u * cw[2:3, :]
    for j in (0, 1, 3, 4):
        acc = acc + ush[j - 2] * cw[j : j + 1, :]
    return acc


def _gdn_prep(proj, conv_w):
    T = proj.shape[0]
    nh = GDN_HEADS

    def body(u_ref, cw_ref, o_ref):
        j = pl.program_id(0)
        u = _mxr(u_ref[...])
        s = _silu(_conv5(u, _shifted(u), _mxr(cw_ref[...])))

        @pl.when(j < 2 * nh)
        def _():
            o_ref[0] = s * lax.rsqrt(jnp.sum(s * s, axis=1, keepdims=True) + NORM_EPS)

        @pl.when(j >= 2 * nh)
        def _():
            o_ref[0] = s

    return pl.pallas_call(
        body,
        out_shape=jax.ShapeDtypeStruct((3, T, GDN_HEADS * GDN_DIM), f32),
        grid=(3 * nh,),
        in_specs=[pl.BlockSpec((T, GDN_DIM), lambda j: (0, j)), pl.BlockSpec((CONV_K, GDN_DIM), lambda j: (0, j))],
        out_specs=pl.BlockSpec((1, T, GDN_DIM), lambda j: (j // nh, 0, j % nh)),
        compiler_params=_cparams(("parallel",)),
        name="gdn_prep",
    )(proj, conv_w)


def _gdn_prep_bwd(proj, conv_w, dqkv_f, dqkv_b, dproj):
    T = proj.shape[0]
    nh = GDN_HEADS


    def body(u_ref, cw_ref, df_ref, db_ref, _, du_ref, dcw_ref):
        j = pl.program_id(0)
        u, cw = _mxr(u_ref[...]), _mxr(cw_ref[...])
        ush = _shifted(u)
        c = _conv5(u, ush, cw)
        s = _silu(c)
        dn = df_ref[0] + db_ref[0]
        rinv = lax.rsqrt(jnp.sum(s * s, axis=1, keepdims=True) + NORM_EPS)
        ds_norm = rinv * dn - s * (rinv * rinv * rinv) * jnp.sum(dn * s, axis=1, keepdims=True)
        ds = jnp.where(j < 2 * nh, ds_norm, dn)
        dc = _mxr(ds * _silu_grad(c))
        dcsh = _shifted(dc)
        du = dc * cw[2:3, :]
        for jj in range(CONV_K):
            d = jj - 2
            dcw_ref[jj : jj + 1, :] = jnp.sum(dc * (u if d == 0 else ush[d]), axis=0, keepdims=True)
            if d != 0:
                du = du + dcsh[-d] * cw[jj : jj + 1, :]
        du_ref[...] = du.astype(du_ref.dtype)

    qspec = pl.BlockSpec((1, T, GDN_DIM), lambda j: (j // nh, 0, j % nh))
    return pl.pallas_call(
        body,
        out_shape=(jax.ShapeDtypeStruct(dproj.shape, dproj.dtype), jax.ShapeDtypeStruct((CONV_K, 3 * nh * GDN_DIM), f32)),
        grid=(3 * nh,),
        in_specs=[pl.BlockSpec((T, GDN_DIM), lambda j: (0, j)), pl.BlockSpec((CONV_K, GDN_DIM), lambda j: (0, j)), qspec, qspec, _ANY_SPEC],
        out_specs=(pl.BlockSpec((T, GDN_DIM), lambda j: (0, j)), pl.BlockSpec((CONV_K, GDN_DIM), lambda j: (0, j))),
        input_output_aliases={4: 0},
        compiler_params=_cparams(("parallel",)),
        name="gdn_prep_bwd",
    )(proj, conv_w, dqkv_f, dqkv_b, dproj)


def _chunk_tri(n, rev):
    i, j = _iota2((n, n), 0), _iota2((n, n), 1)
    same = jnp.right_shift(i, 6) == jnp.right_shift(j, 6)
    order = (j >= i) if rev else (j <= i)
    return jnp.where(jnp.logical_and(same, order), 1.0, 0.0).astype(f32)


def _gdn_gates(proj, avec, dvec, tb=256):
    T = proj.shape[0]

    def body(s_ref, a_ref, d_ref, g_ref, gc_ref):
        small = s_ref[...]
        lane = _iota2(small.shape, 1)
        lg = -jnp.exp(a_ref[...]) * _softplus(small + d_ref[...])
        beta = _sigmoid(small)
        g_ref[...] = jnp.where(lane < 16, lg, jnp.where(lane < 32, beta, 0.0))
        lgm = jnp.where(lane < 16, lg, 0.0)
        gcf = _exact01(_chunk_tri(tb, False), lgm)
        gcr = _exact01(_chunk_tri(tb, True), lgm)
        gc_ref[...] = jnp.where(lane < 8, gcf, gcr)

    blk = pl.BlockSpec((tb, LANES), lambda i: (i, 0))
    row = pl.BlockSpec((1, LANES), lambda i: (0, 0))
    return pl.pallas_call(
        body,
        out_shape=(jax.ShapeDtypeStruct((T, LANES), f32), jax.ShapeDtypeStruct((T, LANES), f32)),
        grid=(T // tb,),
        in_specs=[pl.BlockSpec((tb, LANES), lambda i: (i, SMALL_OFF // LANES)), row, row],
        out_specs=(blk, blk),
        compiler_params=_cparams(("parallel",)),
        name="gdn_gates",
    )(proj, avec, dvec)


def _gdn_gates_bwd(proj, avec, dvec, dg_f, dg_b, dsmall_gla, dproj, tb=256):
    T = proj.shape[0]
    pad = N_CAT - SMALL_OFF

    def body(s_ref, a_ref, d_ref, gf_ref, gb_ref, dl_ref, _, ds_ref, da_ref, dd_ref):
        i = pl.program_id(0)
        small = s_ref[...]
        lane = _iota2(small.shape, 1)
        dgate = gf_ref[...] + gb_ref[...]
        z = small + d_ref[...]
        nega = -jnp.exp(a_ref[...])
        dz = dgate * nega * _sigmoid(z)
        beta = _sigmoid(small)
        dsm = jnp.where(lane < 16, dz, jnp.where(lane < 32, dgate * beta * (1.0 - beta), 0.0))
        ds_ref[:, :LANES] = (dsm + dl_ref[...]).astype(ds_ref.dtype)
        ds_ref[:, LANES:] = jnp.zeros((tb, pad - LANES), ds_ref.dtype)
        lg = nega * _softplus(z)
        pa = jnp.sum(jnp.where(lane < 16, dgate * lg, 0.0), axis=0, keepdims=True)
        pd = jnp.sum(jnp.where(lane < 16, dz, 0.0), axis=0, keepdims=True)

        @pl.when(i == 0)
        def _():
            da_ref[...] = pa
            dd_ref[...] = pd

        @pl.when(i > 0)
        def _():
            da_ref[...] += pa
            dd_ref[...] += pd

    blk = pl.BlockSpec((tb, LANES), lambda i: (i, 0))
    row = pl.BlockSpec((1, LANES), lambda i: (0, 0))
    return pl.pallas_call(
        body,
        out_shape=(jax.ShapeDtypeStruct(dproj.shape, dproj.dtype), jax.ShapeDtypeStruct((1, LANES), f32), jax.ShapeDtypeStruct((1, LANES), f32)),
        grid=(T // tb,),
        in_specs=[pl.BlockSpec((tb, LANES), lambda i: (i, SMALL_OFF // LANES)), row, row, blk, blk, blk, _ANY_SPEC],
        out_specs=(pl.BlockSpec((tb, pad), lambda i: (i, DP_SMALL_BLK)), row, row),
        input_output_aliases={6: 0},
        compiler_params=_cparams(("arbitrary",)),
        name="gdn_gates_bwd",
    )(proj, avec, dvec, dg_f, dg_b, dsmall_gla, dproj)


GDN_HB = 8


def _inv_unit_lower(L, eye):
    A = eye - L
    P2 = _split2(L)
    for _ in range(3):
        P2 = _split2(_h3s(P2, P2, "nn"))
        A = A + _h3s(_split2(A), P2, "nn")
    Ph = P2[0]
    for _ in range(2):
        Ph = _dot16(Ph, Ph, "nn").astype(jnp.bfloat16)
        A = A + _dot16(A.astype(jnp.bfloat16), Ph, "nn")
    return A


def _lane_col(blk, idx):
    lane = _iota2(blk.shape, 1)
    return jnp.sum(jnp.where(lane == idx, blk, 0.0), axis=1, keepdims=True)


class _GdnChunk:
    def __init__(self, q, k, v, gcol, beta, rev, A=None):
        Cn = q.shape[0]
        ii, jj = _iota2((Cn, Cn), 0), _iota2((Cn, Cn), 1)
        self.incl = (ii <= jj) if rev else (ii >= jj)
        self.strict = (ii < jj) if rev else (ii > jj)
        grow = jnp.broadcast_to(gcol, (Cn, LANES)).T[0:1, :]
        diff = gcol - grow
        self.decay = jnp.where(self.incl, jnp.exp(jnp.where(self.incl, diff, 0.0)), 0.0)
        self.k, self.v, self.beta, self.gcol = k, v, beta, gcol
        self.qs = q * (GDN_DIM**-0.5)
        self.kb = k * beta
        self.L = jnp.where(self.strict, _mx(self.kb, k, "nt") * self.decay, 0.0)
        if A is None:
            A = _inv_unit_lower(self.L, jnp.where(ii == jj, 1.0, 0.0).astype(f32))
        self.A2 = _split2(A)
        self.A = A
        self.eg = jnp.exp(gcol)
        self.kbg = self.kb * self.eg
        self.sol = _h3s(self.A2, _split2(jnp.concatenate([v * beta, self.kbg], axis=1)), "nn")
        self.u = self.sol[:, :GDN_DIM]
        self.w = self.sol[:, GDN_DIM:]
        self.attn = jnp.where(self.incl, _mx(self.qs, k, "nt") * self.decay, 0.0)
        self.qd = self.qs * self.eg
        last = 0 if rev else Cn - 1
        self.last = last
        gl = gcol[last : last + 1, :]
        self.ekd = jnp.exp(gl - gcol)
        self.kd = k * self.ekd
        self.egl = jnp.exp(gl)


def _gdn_scan_fwd(qkvn, gates, gc):
    T = qkvn.shape[1]
    N = T // CHUNK
    nh, hb = GDN_HEADS, GDN_HB
    W = hb * GDN_DIM
    chains = [(d, hh) for d in range(2) for hh in range(hb)]

    def body(qf_ref, qb_ref, gf_ref, gb_ref, cf_ref, cb_ref, of_ref, ob_ref, sf_ref, sb_ref, af_ref, ab_ref, s_scr):
        n, b = pl.program_id(0), pl.program_id(1)

        @pl.when(n == 0)
        def _():
            for d, hh in chains:
                s_scr[d, b * hb + hh] = jnp.zeros((GDN_DIM, GDN_DIM), f32)

        dirs = ((qf_ref, gf_ref, cf_ref, of_ref, sf_ref, af_ref), (qb_ref, gb_ref, cb_ref, ob_ref, sb_ref, ab_ref))
        s_in = {ch: s_scr[ch[0], b * hb + ch[1]] for ch in chains}
        s_out = {}
        for d, hh in chains:
            q_ref, g_ref, c_ref, o_ref, st_ref, a_ref = dirs[d]
            h = b * hb + hh
            sl = slice(hh * GDN_DIM, (hh + 1) * GDN_DIM)
            gcol = _lane_col(c_ref[...], h + 8 * d)
            beta = _lane_col(g_ref[...], 16 + h + 8 * d)
            ck = _GdnChunk(q_ref[0, :, sl], q_ref[1, :, sl], q_ref[2, :, sl], gcol, beta, d == 1)
            S = s_in[d, hh]
            st_ref[0, hh] = S
            a_ref[0, hh] = ck.A
            v_new = ck.u - _mx(ck.w, S, "nn")
            o_ref[:, sl] = _mx(ck.qd, S, "nn") + _mx(ck.attn, v_new, "nn")
            s_out[d, hh] = S * ck.egl + _mx(ck.kd, v_new, "tn")
        for d, hh in chains:
            s_scr[d, b * hb + hh] = s_out[d, hh]

    fw, bw = (lambda n: n), (lambda n: N - 1 - n)
    qsp = lambda ci: pl.BlockSpec((3, CHUNK, W), lambda n, b: (0, ci(n), b))
    gsp = lambda ci: pl.BlockSpec((CHUNK, LANES), lambda n, b: (ci(n), 0))
    osp = lambda ci: pl.BlockSpec((CHUNK, W), lambda n, b: (ci(n), b))
    ssp = lambda ci: pl.BlockSpec((1, hb, GDN_DIM, GDN_DIM), lambda n, b: (ci(n), b, 0, 0))
    asp = lambda ci: pl.BlockSpec((1, hb, CHUNK, CHUNK), lambda n, b: (ci(n), b, 0, 0))
    o_shape = jax.ShapeDtypeStruct((T, nh * GDN_DIM), f32)
    s_shape = jax.ShapeDtypeStruct((N, nh, GDN_DIM, GDN_DIM), f32)
    a_shape = jax.ShapeDtypeStruct((N, nh, CHUNK, CHUNK), f32)
    return pl.pallas_call(
        body,
        out_shape=(o_shape, o_shape, s_shape, s_shape, a_shape, a_shape),
        grid=(N, nh // hb),
        in_specs=[qsp(fw), qsp(bw), gsp(fw), gsp(bw), gsp(fw), gsp(bw)],
        out_specs=(osp(fw), osp(bw), ssp(fw), ssp(bw), asp(fw), asp(bw)),
        scratch_shapes=[pltpu.VMEM((2, nh, GDN_DIM, GDN_DIM), f32)],
        compiler_params=_cparams(("arbitrary", "arbitrary")),
        name="gdn_scan_fwd",
    )(qkvn, qkvn, gates, gates, gc, gc)


def _gdn_scan_bwd(qkvn, gates, gc, st_f, st_b, a_f, a_b, do):
    T = qkvn.shape[1]
    N = T // CHUNK
    nh, hb = GDN_HEADS, GDN_HB
    nb = nh // hb
    W = hb * GDN_DIM
    chains = [(d, hh) for d in range(2) for hh in range(hb)]

    def body(qf_ref, qb_ref, gf_ref, gb_ref, cf_ref, cb_ref, sf_ref, sb_ref, af_ref, ab_ref, dof_ref, dob_ref,
             dqf_ref, dqb_ref, dgf_ref, dgb_ref, ds_scr):
        n, b = pl.program_id(0), pl.program_id(1)

        @pl.when(n == 0)
        def _():
            for d, hh in chains:
                ds_scr[d, b * hb + hh] = jnp.zeros((GDN_DIM, GDN_DIM), f32)

        @pl.when(b == 0)
        def _():
            dgf_ref[...] = jnp.zeros((CHUNK, LANES), f32)
            dgb_ref[...] = jnp.zeros((CHUNK, LANES), f32)

        dirs = ((qf_ref, gf_ref, cf_ref, sf_ref, af_ref, dof_ref, dqf_ref), (qb_ref, gb_ref, cb_ref, sb_ref, ab_ref, dob_ref, dqb_ref))
        ds_in = {ch: ds_scr[ch[0], b * hb + ch[1]] for ch in chains}
        ds_out = {}
        lane = _iota2((CHUNK, LANES), 1)
        acc = [dgf_ref[...], dgb_ref[...]]
        for d, hh in chains:
            q_ref, g_ref, c_ref, st_ref, a_ref, do_ref, dq_ref = dirs[d]
            h = b * hb + hh
            sl = slice(hh * GDN_DIM, (hh + 1) * GDN_DIM)
            gcol = _lane_col(c_ref[...], h + 8 * d)
            beta = _lane_col(g_ref[...], 16 + h + 8 * d)
            ck = _GdnChunk(q_ref[0, :, sl], q_ref[1, :, sl], q_ref[2, :, sl], gcol, beta, d == 1, A=a_ref[0, hh])
            k, v = ck.k, ck.v
            S = st_ref[0, hh]
            dS = ds_in[d, hh]
            dov = do_ref[:, sl]
            v_new = ck.u - _mx(ck.w, S, "nn")
            d_vnew = _mx(ck.attn, dov, "tn") + _mx(ck.kd, dS, "nn")
            d_attn = jnp.where(ck.incl, _mx(dov, v_new, "nt"), 0.0)
            d_qd = _mx(dov, S, "nt")
            d_kd = _mx(v_new, dS, "nt")
            d_gl = jnp.sum(jnp.sum(dS * S, axis=1, keepdims=True), axis=0, keepdims=True) * ck.egl
            ds_out[d, hh] = dS * ck.egl + _mx(ck.qd, dov, "tn") - _mx(ck.w, d_vnew, "tn")
            d_w = -_mx(d_vnew, S, "nt")
            d_rhs = _h3s(ck.A2, _split2(jnp.concatenate([d_vnew, d_w], axis=1)), "tn")
            d_vb, d_kbg = d_rhs[:, :GDN_DIM], d_rhs[:, GDN_DIM:]
            dL = -jnp.where(ck.strict, _h3(d_rhs, ck.sol, "nt"), 0.0)
            P = dL * ck.decay
            Q = d_attn * ck.decay
            d_kb = _mx(P, k, "nn") + d_kbg * ck.eg
            d_qs = _mx(Q, k, "nn") + d_qd * ck.eg
            d_k = _mx(P, ck.kb, "tn") + _mx(Q, ck.qs, "tn") + d_kd * ck.ekd + d_kb * ck.beta
            E = dL * ck.L + d_attn * ck.attn
            kdsum = jnp.sum(d_kd * ck.kd, axis=1, keepdims=True)
            d_gc = (jnp.sum(E, axis=1, keepdims=True) - _colsum_col(E)
                    + jnp.sum(d_kbg * ck.kbg, axis=1, keepdims=True)
                    + jnp.sum(d_qd * ck.qd, axis=1, keepdims=True) - kdsum)
            d_gl = d_gl + jnp.sum(kdsum, axis=0, keepdims=True)
            row = _iota2((CHUNK, 1), 0)
            d_gc = d_gc + jnp.where(row == ck.last, d_gl, 0.0)
            d_beta = jnp.sum(d_vb * v, axis=1, keepdims=True) + jnp.sum(d_kb * k, axis=1, keepdims=True)
            dq_ref[0, :, sl] = d_qs * (GDN_DIM**-0.5)
            dq_ref[1, :, sl] = d_k
            dq_ref[2, :, sl] = d_vb * ck.beta
            acc[d] = jnp.where(lane == h + 8 * d, d_gc, acc[d])
            acc[d] = jnp.where(lane == 16 + h + 8 * d, d_beta, acc[d])
        for d, hh in chains:
            ds_scr[d, b * hb + hh] = ds_out[d, hh]
        dgf_ref[...] = acc[0]
        dgb_ref[...] = acc[1]

        @pl.when(b == nb - 1)
        def _():
            for d, dg_ref in enumerate((dgf_ref, dgb_ref)):
                blk = dg_ref[...]
                dg_ref[...] = jnp.where(lane < 16, _exact01(_chunk_tri(CHUNK, d == 0), blk), blk)

    fw, bw = (lambda n: N - 1 - n), (lambda n: n)
    qsp = lambda ci: pl.BlockSpec((3, CHUNK, W), lambda n, b: (0, ci(n), b))
    gsp = lambda ci: pl.BlockSpec((CHUNK, LANES), lambda n, b: (ci(n), 0))
    osp = lambda ci: pl.BlockSpec((CHUNK, W), lambda n, b: (ci(n), b))
    ssp = lambda ci: pl.BlockSpec((1, hb, GDN_DIM, GDN_DIM), lambda n, b: (ci(n), b, 0, 0))
    asp = lambda ci: pl.BlockSpec((1, hb, CHUNK, CHUNK), lambda n, b: (ci(n), b, 0, 0))
    dq_shape = jax.ShapeDtypeStruct((3, T, nh * GDN_DIM), f32)
    dg_shape = jax.ShapeDtypeStruct((T, LANES), f32)
    return pl.pallas_call(
        body,
        out_shape=(dq_shape, dq_shape, dg_shape, dg_shape),
        grid=(N, nb),
        in_specs=[qsp(fw), qsp(bw), gsp(fw), gsp(bw), gsp(fw), gsp(bw), ssp(fw), ssp(bw), asp(fw), asp(bw), osp(fw), osp(bw)],
        out_specs=(qsp(fw), qsp(bw), gsp(fw), gsp(bw)),
        scratch_shapes=[pltpu.VMEM((2, nh, GDN_DIM, GDN_DIM), f32)],
        compiler_params=_cparams(("arbitrary", "arbitrary")),
        name="gdn_scan_bwd",
    )(qkvn, qkvn, gates, gates, gc, gc, st_f, st_b, a_f, a_b, do, do)


def _out_norm(o_f, o_b, proj, gate_blk, w, hd, name, tb=256):
    T, W = o_f.shape
    nh = W // hd

    def body(of_ref, ob_ref, z_ref, w_ref, y_ref):
        wv = w_ref[...]
        for i in range(nh):
            sl = slice(i * hd, (i + 1) * hd)
            o = of_ref[:, sl] + ob_ref[:, sl]
            r = lax.rsqrt(jnp.mean(o * o, axis=1, keepdims=True) + NORM_EPS)
            y_ref[:, sl] = (o * r * wv * _silu(z_ref[:, sl])).astype(y_ref.dtype)

    blk = pl.BlockSpec((tb, W), lambda i: (i, 0))
    return pl.pallas_call(
        body,
        out_shape=jax.ShapeDtypeStruct((T, W), MXU_DTYPE),
        grid=(T // tb,),
        in_specs=[blk, blk, pl.BlockSpec((tb, W), lambda i: (i, gate_blk)), pl.BlockSpec((1, hd), lambda i: (0, 0))],
        out_specs=blk,
        compiler_params=_cparams(("parallel",)),
        name=name,
    )(o_f, o_b, proj, w)


def _out_norm_bwd(dy, o_f, o_b, proj, gate_blk, w, hd, name, dproj, tb=256):
    T, W = o_f.shape
    nh = W // hd

    def body(dy_ref, of_ref, ob_ref, z_ref, w_ref, _, do_ref, dz_ref, dw_ref):
        i = pl.program_id(0)
        wv = w_ref[...]
        dw = jnp.zeros((1, hd), f32)
        for a in range(nh):
            sl = slice(a * hd, (a + 1) * hd)
            o = of_ref[:, sl] + ob_ref[:, sl]
            z = z_ref[:, sl]
            dyv = dy_ref[:, sl]
            r = lax.rsqrt(jnp.mean(o * o, axis=1, keepdims=True) + NORM_EPS)
            orr = o * r
            dn = dyv * _silu(z)
            dz_ref[:, sl] = (dyv * orr * wv * _silu_grad(z)).astype(dz_ref.dtype)
            dw = dw + jnp.sum(dn * orr, axis=0, keepdims=True)
            dnw = dn * wv
            do_ref[:, sl] = r * dnw - o * (r * r * r) * jnp.mean(dnw * o, axis=1, keepdims=True)

        @pl.when(i == 0)
        def _():
            dw_ref[...] = dw

        @pl.when(i > 0)
        def _():
            dw_ref[...] += dw

    blk = pl.BlockSpec((tb, W), lambda i: (i, 0))
    row = pl.BlockSpec((1, hd), lambda i: (0, 0))
    return pl.pallas_call(
        body,
        out_shape=(jax.ShapeDtypeStruct((T, W), f32), jax.ShapeDtypeStruct(dproj.shape, dproj.dtype), jax.ShapeDtypeStruct((1, hd), f32)),
        grid=(T // tb,),
        in_specs=[blk, blk, blk, pl.BlockSpec((tb, W), lambda i: (i, gate_blk)), row, _ANY_SPEC],
        out_specs=(blk, pl.BlockSpec((tb, W), lambda i: (i, gate_blk)), row),
        input_output_aliases={5: 1},
        compiler_params=_cparams(("arbitrary",)),
        name=name,
    )(dy, o_f, o_b, proj, w, dproj)


def _gla_prep(proj, w2f, w2b, b2f, b2b, tb=256):
    T = proj.shape[0]
    W = GLA_HEADS * GLA_DK

    def body(s_ref, wf_ref, wb_ref, bf_ref, bb_ref, gf_ref, gb_ref):
        small = s_ref[...]
        gkf = -_softplus(-(_mx(small, wf_ref[...], "nn") + bf_ref[...])) * (1.0 / GATE_NORMALIZER)
        gkb = -_softplus(-(_mx(small, wb_ref[...], "nn") + bb_ref[...])) * (1.0 / GATE_NORMALIZER)
        gf_ref[...] = _exact01(_chunk_tri(tb, False), gkf)
        gb_ref[...] = _exact01(_chunk_tri(tb, True), gkb)

    blk = pl.BlockSpec((tb, W), lambda i: (i, 0))
    wsp = pl.BlockSpec((LANES, W), lambda i: (0, 0))
    row = pl.BlockSpec((1, W), lambda i: (0, 0))
    return pl.pallas_call(
        body,
        out_shape=(jax.ShapeDtypeStruct((T, W), f32), jax.ShapeDtypeStruct((T, W), f32)),
        grid=(T // tb,),
        in_specs=[pl.BlockSpec((tb, LANES), lambda i: (i, SMALL_OFF // LANES)), wsp, wsp, row, row],
        out_specs=(blk, blk),
        compiler_params=_cparams(("parallel",)),
        name="gla_prep",
    )(proj, w2f, w2b, b2f, b2b)


def _gla_prep_bwd(proj, w2f, w2b, b2f, b2b, dgk_f, dgk_b, tb=256):
    T = proj.shape[0]
    W = GLA_HEADS * GLA_DK

    def body(s_ref, wf_ref, wb_ref, bf_ref, bb_ref, df_ref, db_ref, ds_ref, dwf_ref, dwb_ref, dbf_ref, dbb_ref):
        i = pl.program_id(0)
        small = s_ref[...]
        dsm = jnp.zeros((tb, LANES), f32)
        parts = []
        for w_ref, b_ref, d_ref in ((wf_ref, bf_ref, df_ref), (wb_ref, bb_ref, db_ref)):
            pre = _mx(small, w_ref[...], "nn") + b_ref[...]
            dpre = d_ref[...] * (1.0 / GATE_NORMALIZER) * _sigmoid(-pre)
            dsm = dsm + _mx(dpre, w_ref[...], "nt")
            parts.append((_mx(small, dpre, "tn"), jnp.sum(dpre, axis=0, keepdims=True)))
        ds_ref[...] = dsm

        @pl.when(i == 0)
        def _():
            dwf_ref[...], dbf_ref[...] = parts[0]
            dwb_ref[...], dbb_ref[...] = parts[1]

        @pl.when(i > 0)
        def _():
            dwf_ref[...] += parts[0][0]
            dbf_ref[...] += parts[0][1]
            dwb_ref[...] += parts[1][0]
            dbb_ref[...] += parts[1][1]

    blk = pl.BlockSpec((tb, W), lambda i: (i, 0))
    wsp = pl.BlockSpec((LANES, W), lambda i: (0, 0))
    row = pl.BlockSpec((1, W), lambda i: (0, 0))
    return pl.pallas_call(
        body,
        out_shape=(jax.ShapeDtypeStruct((T, LANES), f32), jax.ShapeDtypeStruct((LANES, W), f32), jax.ShapeDtypeStruct((LANES, W), f32),
                   jax.ShapeDtypeStruct((1, W), f32), jax.ShapeDtypeStruct((1, W), f32)),
        grid=(T // tb,),
        in_specs=[pl.BlockSpec((tb, LANES), lambda i: (i, SMALL_OFF // LANES)), wsp, wsp, row, row, blk, blk],
        out_specs=(pl.BlockSpec((tb, LANES), lambda i: (i, 0)), wsp, wsp, row, row),
        compiler_params=_cparams(("arbitrary",)),
        name="gla_prep_bwd",
    )(proj, w2f, w2b, b2f, b2b, dgk_f, dgk_b)


Q_B_BLK, K_B_BLK, V_B_BLK = 4096 // GLA_DK, 4608 // GLA_DK, 5120 // GLA_DV


class _GlaChunk:
    def __init__(self, q, k, G, rev):
        Cn = q.shape[0]
        ii, jj = _iota2((Cn, Cn), 0), _iota2((Cn, Cn), 1)
        self.incl = (ii <= jj) if rev else (ii >= jj)
        self.eG = jnp.exp(G)
        self.enG = jnp.exp(-G)
        self.qg = q * (GLA_DK**-0.5) * self.eG
        self.kg = k * self.enG
        self.attn = jnp.where(self.incl, _mx(self.qg, self.kg, "nt"), 0.0)
        last = 0 if rev else Cn - 1
        self.last = last
        gl = G[last : last + 1, :]
        self.ekd = jnp.exp(gl - G)
        self.kd = k * self.ekd
        self.egl = jnp.exp(gl)


def _gla_scan_fwd(proj, G_f, G_b):
    T = proj.shape[0]
    N = T // CHUNK
    nh = GLA_HEADS

    def body(qf_ref, kf_ref, vf_ref, gf_ref, qb_ref, kb_ref, vb_ref, gb_ref, of_ref, ob_ref, sf_ref, sb_ref, s_scr):
        n, h = pl.program_id(0), pl.program_id(1)

        @pl.when(n == 0)
        def _():
            s_scr[0, h] = jnp.zeros((GLA_DV, GLA_DK), f32)
            s_scr[1, h] = jnp.zeros((GLA_DV, GLA_DK), f32)

        for d, (q_ref, k_ref, v_ref, g_ref, o_ref, st_ref) in enumerate(
            ((qf_ref, kf_ref, vf_ref, gf_ref, of_ref, sf_ref), (qb_ref, kb_ref, vb_ref, gb_ref, ob_ref, sb_ref))
        ):
            ck = _GlaChunk(q_ref[...], k_ref[...], g_ref[...], d == 1)
            v = v_ref[...]
            St = s_scr[d, h]
            st_ref[0, 0] = St
            o_ref[...] = _mx(ck.qg, St, "nt") + _mx(ck.attn, v, "nn")
            s_scr[d, h] = St * ck.egl + _mx(v, ck.kd, "tn")

    def specs(cidx):
        return [
            pl.BlockSpec((CHUNK, GLA_DK), lambda n, h: (cidx(n), Q_B_BLK + h)),
            pl.BlockSpec((CHUNK, GLA_DK), lambda n, h: (cidx(n), K_B_BLK + h)),
            pl.BlockSpec((CHUNK, GLA_DV), lambda n, h: (cidx(n), V_B_BLK + h)),
            pl.BlockSpec((CHUNK, GLA_DK), lambda n, h: (cidx(n), h)),
        ]

    fw, bw = (lambda n: n), (lambda n: N - 1 - n)
    o_shape = jax.ShapeDtypeStruct((T, nh * GLA_DV), f32)
    s_shape = jax.ShapeDtypeStruct((N, nh, GLA_DV, GLA_DK), f32)
    return pl.pallas_call(
        body,
        out_shape=(o_shape, o_shape, s_shape, s_shape),
        grid=(N, nh),
        in_specs=specs(fw) + specs(bw),
        out_specs=(
            pl.BlockSpec((CHUNK, GLA_DV), lambda n, h: (n, h)),
            pl.BlockSpec((CHUNK, GLA_DV), lambda n, h: (N - 1 - n, h)),
            pl.BlockSpec((1, 1, GLA_DV, GLA_DK), lambda n, h: (n, h, 0, 0)),
            pl.BlockSpec((1, 1, GLA_DV, GLA_DK), lambda n, h: (N - 1 - n, h, 0, 0)),
        ),
        scratch_shapes=[pltpu.VMEM((2, nh, GLA_DV, GLA_DK), f32)],
        compiler_params=_cparams(("arbitrary", "arbitrary")),
        name="gla_scan_fwd",
    )(proj, proj, proj, G_f, proj, proj, proj, G_b)


def _gla_scan_bwd(proj, G_f, G_b, st_f, st_b, do):
    T = proj.shape[0]
    N = T // CHUNK
    nh = GLA_HEADS

    def body(qf_ref, kf_ref, vf_ref, gf_ref, sf_ref, dof_ref, qb_ref, kb_ref, vb_ref, gb_ref, sb_ref, dob_ref,
             dqf_ref, dkf_ref, dvf_ref, dgf_ref, dqb_ref, dkb_ref, dvb_ref, dgb_ref, ds_scr):
        n, h = pl.program_id(0), pl.program_id(1)

        @pl.when(n == 0)
        def _():
            ds_scr[0, h] = jnp.zeros((GLA_DV, GLA_DK), f32)
            ds_scr[1, h] = jnp.zeros((GLA_DV, GLA_DK), f32)

        for d, (q_ref, k_ref, v_ref, g_ref, st_ref, do_ref, dq_ref, dk_ref, dv_ref, dg_ref) in enumerate(
            ((qf_ref, kf_ref, vf_ref, gf_ref, sf_ref, dof_ref, dqf_ref, dkf_ref, dvf_ref, dgf_ref),
             (qb_ref, kb_ref, vb_ref, gb_ref, sb_ref, dob_ref, dqb_ref, dkb_ref, dvb_ref, dgb_ref))
        ):
            rev = d == 1
            ck = _GlaChunk(q_ref[...], k_ref[...], g_ref[...], rev)
            v = v_ref[...]
            St = st_ref[0, 0]
            dSt = ds_scr[d, h]
            dov = do_ref[...]
            d_attn = jnp.where(ck.incl, _mx(dov, v, "nt"), 0.0)
            d_qg = _mx(dov, St, "nn") + _mx(d_attn, ck.kg, "nn")
            d_kg = _mx(d_attn, ck.qg, "tn")
            dv_ref[...] = _mx(ck.attn, dov, "tn") + _mx(ck.kd, dSt, "nt")
            d_kd = _mx(v, dSt, "nn")
            ds_scr[d, h] = dSt * ck.egl + _mx(dov, ck.qg, "tn")
            kdd = d_kd * ck.kd
            d_gl = jnp.sum(dSt * St, axis=0, keepdims=True) * ck.egl + jnp.sum(kdd, axis=0, keepdims=True)
            dq_ref[...] = d_qg * ck.eG * (GLA_DK**-0.5)
            dk_ref[...] = d_kg * ck.enG + d_kd * ck.ekd
            d_G = d_qg * ck.qg - d_kg * ck.kg - kdd
            row = _iota2((CHUNK, GLA_DK), 0)
            d_G = d_G + jnp.where(row == ck.last, d_gl, 0.0)
            dg_ref[...] = _exact01(_chunk_tri(CHUNK, not rev), d_G)

    def in_specs(cidx):
        return [
            pl.BlockSpec((CHUNK, GLA_DK), lambda n, h: (cidx(n), Q_B_BLK + h)),
            pl.BlockSpec((CHUNK, GLA_DK), lambda n, h: (cidx(n), K_B_BLK + h)),
            pl.BlockSpec((CHUNK, GLA_DV), lambda n, h: (cidx(n), V_B_BLK + h)),
            pl.BlockSpec((CHUNK, GLA_DK), lambda n, h: (cidx(n), h)),
            pl.BlockSpec((1, 1, GLA_DV, GLA_DK), lambda n, h: (cidx(n), h, 0, 0)),
            pl.BlockSpec((CHUNK, GLA_DV), lambda n, h: (cidx(n), h)),
        ]

    def out_specs(cidx):
        return [
            pl.BlockSpec((CHUNK, GLA_DK), lambda n, h: (cidx(n), h)),
            pl.BlockSpec((CHUNK, GLA_DK), lambda n, h: (cidx(n), h)),
            pl.BlockSpec((CHUNK, GLA_DV), lambda n, h: (cidx(n), h)),
            pl.BlockSpec((CHUNK, GLA_DK), lambda n, h: (cidx(n), h)),
        ]

    fw, bw = (lambda n: N - 1 - n), (lambda n: n)
    k_shape = jax.ShapeDtypeStruct((T, nh * GLA_DK), f32)
    v_shape = jax.ShapeDtypeStruct((T, nh * GLA_DV), f32)
    return pl.pallas_call(
        body,
        out_shape=(k_shape, k_shape, v_shape, k_shape) * 2,
        grid=(N, nh),
        in_specs=in_specs(fw) + in_specs(bw),
        out_specs=tuple(out_specs(fw) + out_specs(bw)),
        scratch_shapes=[pltpu.VMEM((2, nh, GLA_DV, GLA_DK), f32)],
        compiler_params=_cparams(("arbitrary", "arbitrary")),
        name="gla_scan_bwd",
    )(proj, proj, proj, G_f, st_f, do, proj, proj, proj, G_b, st_b, do)


GATE_A_BLK, GATE_B_BLK, Z_BLK, G_B_BLK = 6144 // 1024, 7168 // 1024, 3072 // 1024, 8192 // 1024
DP_GATES_BLK, DP_GLA_QKV_BLK, DP_SMALL_BLK = 6144 // 2048, 4096 // 2048, SMALL_OFF // 512
_ANY_SPEC = pl.BlockSpec(memory_space=pl.ANY)


def _mid(og, obg, proj, x, target, wpg, wpl, wout, w_post, tb=256):
    T = x.shape[0]
    Dm = D_MODEL

    def body(og_ref, obg_ref, ga_ref, gb_ref, x_ref, t_ref, wpg_ref, wpl_ref, wo_ref, wp_ref,
             loss_ref, dwp_ref, dy_ref, m_ref, dout_ref, dya_ref, dyb_ref, dgates_ref, dog_ref, dobg_ref):
        i = pl.program_id(0)
        ya = _mx(og_ref[...], wpg_ref[...], "nn")
        yb = _mx(obg_ref[...], wpl_ref[...], "nn")
        sa, sb = _sigmoid(ga_ref[...]), _sigmoid(gb_ref[...])
        m = sa * ya + sb * yb
        m_ref[...] = m.astype(m_ref.dtype)
        out = _mx(m, wo_ref[...], "nn")
        r = lax.rsqrt(jnp.mean(out * out, axis=1, keepdims=True) + NORM_EPS)
        wp = wp_ref[...]
        e = x_ref[...] + out * r * wp - t_ref[...]
        dy = e * (1.0 / Dm)
        dy_ref[...] = dy
        lpart = jnp.sum(e * e, axis=0, keepdims=True)
        wpart = jnp.sum(dy * out * r, axis=0, keepdims=True)
        dyn = dy * wp
        dout = r * dyn - out * (r * r * r) * jnp.mean(dyn * out, axis=1, keepdims=True)
        dout_ref[...] = dout.astype(dout_ref.dtype)
        dm = _mx(dout, wo_ref[...], "nt")
        dya, dyb = dm * sa, dm * sb
        dya_ref[...] = dya.astype(dya_ref.dtype)
        dyb_ref[...] = dyb.astype(dyb_ref.dtype)
        dgates_ref[:, :Dm] = (dm * ya * sa * (1.0 - sa)).astype(dgates_ref.dtype)
        dgates_ref[:, Dm:] = (dm * yb * sb * (1.0 - sb)).astype(dgates_ref.dtype)
        dog_ref[...] = _mx(dya, wpg_ref[...], "nt")
        dobg_ref[...] = _mx(dyb, wpl_ref[...], "nt")

        @pl.when(i == 0)
        def _():
            loss_ref[...] = lpart
            dwp_ref[...] = wpart

        @pl.when(i > 0)
        def _():
            loss_ref[...] += lpart
            dwp_ref[...] += wpart

    blk = pl.BlockSpec((tb, Dm), lambda i: (i, 0))
    row = pl.BlockSpec((1, Dm), lambda i: (0, 0))
    wsp = pl.BlockSpec((Dm, Dm), lambda i: (0, 0))
    act = jax.ShapeDtypeStruct((T, Dm), MXU_DTYPE)
    big = jax.ShapeDtypeStruct((T, Dm), f32)
    vec = jax.ShapeDtypeStruct((1, Dm), f32)
    return pl.pallas_call(
        body,
        out_shape=(vec, vec, big, act, act, act, act, jax.ShapeDtypeStruct((T, N_CAT), MXU_DTYPE), big, big),
        grid=(T // tb,),
        in_specs=[blk, blk, pl.BlockSpec((tb, Dm), lambda i: (i, GATE_A_BLK)), pl.BlockSpec((tb, Dm), lambda i: (i, GATE_B_BLK)),
                  blk, blk, wsp, wsp, wsp, row],
        out_specs=(row, row) + (blk,) * 5 + (pl.BlockSpec((tb, 2 * Dm), lambda i: (i, DP_GATES_BLK)), blk, blk),
        compiler_params=_cparams(("arbitrary",)),
        name="mid",
    )(og, obg, proj, proj, x, target, wpg, wpl, wout, w_post)


def _heads(ref, lead, n, width):
    return [ref[lead + (slice(None), slice(i * width, (i + 1) * width))] for i in range(n)]


class _GdnBatch:
    def __init__(self, q, k, v, gcol, grow, beta, nf, A=None):
        B, Cn = q.shape[0], q.shape[1]
        self.nf = nf
        self.incl, self.strict, eye = _dir_masks(B, nf, Cn)
        self.decay = jnp.where(self.incl, jnp.exp(jnp.where(self.incl, gcol - grow, 0.0)), 0.0)
        self.k, self.v, self.beta = k, v, beta
        self.qs = q * (GDN_DIM**-0.5)
        self.kb = k * beta
        self.L = jnp.where(self.strict, _mx(self.kb, k, "nt") * self.decay, 0.0)
        if A is None:
            A = _inv_unit_lower(self.L, jnp.where(eye, 1.0, 0.0).astype(f32))
        self.A = A
        self.A2 = _split2(A)
        self.eg = jnp.exp(gcol)
        self.kbg = self.kb * self.eg
        self.sol = _h3s(self.A2, _split2(jnp.concatenate([v * beta, self.kbg], axis=2)), "nn")
        self.u = self.sol[:, :, :GDN_DIM]
        self.w = self.sol[:, :, GDN_DIM:]
        self.attn = jnp.where(self.incl, _mx(self.qs, k, "nt") * self.decay, 0.0)
        self.qd = self.qs * self.eg
        gl = _last_row(gcol, nf)
        self.ekd = jnp.exp(gl - gcol)
        self.kd = k * self.ekd
        self.egl = jnp.exp(gl)


def _gdn_chain_inputs(qf_ref, qb_ref, gf_ref, gb_ref, cf_ref, cb_ref, b, hb):
    qkv = [jnp.stack(_heads(qf_ref, (i,), hb, GDN_DIM) + _heads(qb_ref, (i,), hb, GDN_DIM)) for i in range(3)]
    cols, rows, betas = [], [], []
    for d, (g_ref, c_ref) in enumerate(((gf_ref, cf_ref), (gb_ref, cb_ref))):
        gates, gc = g_ref[...], c_ref[...]
        for hh in range(hb):
            h = b * hb + hh
            col = _lane_col(gc, h + 8 * d)
            cols.append(col)
            rows.append(jnp.broadcast_to(col, (CHUNK, LANES)).T[0:1, :])
            betas.append(_lane_col(gates, 16 + h + 8 * d))
    return qkv[0], qkv[1], qkv[2], jnp.stack(cols), jnp.stack(rows), jnp.stack(betas)


def _gdn_fwd(qkvn, gates, gc):
    T = qkvn.shape[1]
    N = T // CHUNK
    nh, hb = GDN_HEADS, GDN_HB
    W = hb * GDN_DIM

    def body(qf_ref, qb_ref, gf_ref, gb_ref, cf_ref, cb_ref, of_ref, ob_ref, sf_ref, sb_ref, af_ref, ab_ref, s_scr):
        n, b = pl.program_id(0), pl.program_id(1)
        hs = pl.ds(b * hb, hb)

        @pl.when(n == 0)
        def _():
            s_scr[0, hs] = jnp.zeros((hb, GDN_DIM, GDN_DIM), f32)
            s_scr[1, hs] = jnp.zeros((hb, GDN_DIM, GDN_DIM), f32)

        q, k, v, gcol, grow, beta = _gdn_chain_inputs(qf_ref, qb_ref, gf_ref, gb_ref, cf_ref, cb_ref, b, hb)
        ck = _GdnBatch(q, k, v, gcol, grow, beta, hb)
        S = jnp.concatenate([s_scr[0, hs], s_scr[1, hs]], axis=0)
        sf_ref[0], sb_ref[0] = S[:hb], S[hb:]
        af_ref[0], ab_ref[0] = ck.A[:hb], ck.A[hb:]
        v_new = ck.u - _mx(ck.w, S, "nn")
        o = _mx(ck.qd, S, "nn") + _mx(ck.attn, v_new, "nn")
        S_new = S * ck.egl + _mx(ck.kd, v_new, "tn")
        for hh in range(hb):
            sl = slice(hh * GDN_DIM, (hh + 1) * GDN_DIM)
            of_ref[:, sl] = o[hh]
            ob_ref[:, sl] = o[hb + hh]
        s_scr[0, hs] = S_new[:hb]
        s_scr[1, hs] = S_new[hb:]

    fw, bw = (lambda n: n), (lambda n: N - 1 - n)
    qsp = lambda ci: pl.BlockSpec((3, CHUNK, W), lambda n, b: (0, ci(n), b))
    gsp = lambda ci: pl.BlockSpec((CHUNK, LANES), lambda n, b: (ci(n), 0))
    osp = lambda ci: pl.BlockSpec((CHUNK, W), lambda n, b: (ci(n), b))
    ssp = lambda ci: pl.BlockSpec((1, hb, GDN_DIM, GDN_DIM), lambda n, b: (ci(n), b, 0, 0))
    asp = lambda ci: pl.BlockSpec((1, hb, CHUNK, CHUNK), lambda n, b: (ci(n), b, 0, 0))
    o_shape = jax.ShapeDtypeStruct((T, nh * GDN_DIM), f32)
    s_shape = jax.ShapeDtypeStruct((N, nh, GDN_DIM, GDN_DIM), f32)
    a_shape = jax.ShapeDtypeStruct((N, nh, CHUNK, CHUNK), f32)
    return pl.pallas_call(
        body,
        out_shape=(o_shape, o_shape, s_shape, s_shape, a_shape, a_shape),
        grid=(N, nh // hb),
        in_specs=[qsp(fw), qsp(bw), gsp(fw), gsp(bw), gsp(fw), gsp(bw)],
        out_specs=(osp(fw), osp(bw), ssp(fw), ssp(bw), asp(fw), asp(bw)),
        scratch_shapes=[pltpu.VMEM((2, nh, GDN_DIM, GDN_DIM), f32)],
        compiler_params=_cparams(("arbitrary", "arbitrary")),
        name="gdn_scan_fwd",
    )(qkvn, qkvn, gates, gates, gc, gc)


def _gdn_bwd(qkvn, gates, gc, st_f, st_b, a_f, a_b, do):
    T = qkvn.shape[1]
    N = T // CHUNK
    nh, hb = GDN_HEADS, GDN_HB
    nb = nh // hb
    W = hb * GDN_DIM

    def body(qf_ref, qb_ref, gf_ref, gb_ref, cf_ref, cb_ref, sf_ref, sb_ref, af_ref, ab_ref, dof_ref, dob_ref,
             dqf_ref, dqb_ref, dgf_ref, dgb_ref, ds_scr):
        n, b = pl.program_id(0), pl.program_id(1)
        hs = pl.ds(b * hb, hb)

        @pl.when(n == 0)
        def _():
            ds_scr[0, hs] = jnp.zeros((hb, GDN_DIM, GDN_DIM), f32)
            ds_scr[1, hs] = jnp.zeros((hb, GDN_DIM, GDN_DIM), f32)

        @pl.when(b == 0)
        def _():
            dgf_ref[...] = jnp.zeros((CHUNK, LANES), f32)
            dgb_ref[...] = jnp.zeros((CHUNK, LANES), f32)

        q, k, v, gcol, grow, beta = _gdn_chain_inputs(qf_ref, qb_ref, gf_ref, gb_ref, cf_ref, cb_ref, b, hb)
        ck = _GdnBatch(q, k, v, gcol, grow, beta, hb, A=jnp.concatenate([af_ref[0], ab_ref[0]], axis=0))
        S = jnp.concatenate([sf_ref[0], sb_ref[0]], axis=0)
        dS = jnp.concatenate([ds_scr[0, hs], ds_scr[1, hs]], axis=0)
        dov = jnp.stack(_heads(dof_ref, (), hb, GDN_DIM) + _heads(dob_ref, (), hb, GDN_DIM))
        v_new = ck.u - _mx(ck.w, S, "nn")
        d_vnew = _mx(ck.attn, dov, "tn") + _mx(ck.kd, dS, "nn")
        d_attn = jnp.where(ck.incl, _mx(dov, v_new, "nt"), 0.0)
        d_qd = _mx(dov, S, "nt")
        d_kd = _mx(v_new, dS, "nt")
        d_gl = jnp.sum(jnp.sum(dS * S, axis=2, keepdims=True), axis=1, keepdims=True) * ck.egl
        dS_new = dS * ck.egl + _mx(ck.qd, dov, "tn") - _mx(ck.w, d_vnew, "tn")
        ds_scr[0, hs] = dS_new[:hb]
        ds_scr[1, hs] = dS_new[hb:]
        d_w = -_mx(d_vnew, S, "nt")
        d_rhs = _h3s(ck.A2, _split2(jnp.concatenate([d_vnew, d_w], axis=2)), "tn")
        d_vb, d_kbg = d_rhs[:, :, :GDN_DIM], d_rhs[:, :, GDN_DIM:]
        dL = -jnp.where(ck.strict, _h3(d_rhs, ck.sol, "nt"), 0.0)
        P = dL * ck.decay
        Q = d_attn * ck.decay
        d_kb = _mx(P, k, "nn") + d_kbg * ck.eg
        d_qs = _mx(Q, k, "nn") + d_qd * ck.eg
        d_k = _mx(P, ck.kb, "tn") + _mx(Q, ck.qs, "tn") + d_kd * ck.ekd + d_kb * ck.beta
        E = dL * ck.L + d_attn * ck.attn
        kdsum = jnp.sum(d_kd * ck.kd, axis=2, keepdims=True)
        d_gc = (jnp.sum(E, axis=2, keepdims=True) - _colsum_col(E)
                + jnp.sum(d_kbg * ck.kbg, axis=2, keepdims=True)
                + jnp.sum(d_qd * ck.qd, axis=2, keepdims=True) - kdsum)
        d_gc = _add_at_last_row(d_gc, d_gl + jnp.sum(kdsum, axis=1, keepdims=True), hb)
        d_beta = jnp.sum(d_vb * v, axis=2, keepdims=True) + jnp.sum(d_kb * k, axis=2, keepdims=True)
        d_q = d_qs * (GDN_DIM**-0.5)
        d_v = d_vb * ck.beta
        lane = _iota2((CHUNK, LANES), 1)
        for d, (dq_ref, dg_ref) in enumerate(((dqf_ref, dgf_ref), (dqb_ref, dgb_ref))):
            acc = dg_ref[...]
            for hh in range(hb):
                c, h = d * hb + hh, b * hb + hh
                sl = slice(hh * GDN_DIM, (hh + 1) * GDN_DIM)
                dq_ref[0, :, sl] = d_q[c]
                dq_ref[1, :, sl] = d_k[c]
                dq_ref[2, :, sl] = d_v[c]
                acc = jnp.where(lane == h + 8 * d, d_gc[c], acc)
                acc = jnp.where(lane == 16 + h + 8 * d, d_beta[c], acc)
            dg_ref[...] = acc

        @pl.when(b == nb - 1)
        def _():
            for d, dg_ref in enumerate((dgf_ref, dgb_ref)):
                blk = dg_ref[...]
                dg_ref[...] = jnp.where(lane < 16, _exact01(_chunk_tri(CHUNK, d == 0), blk), blk)

    fw, bw = (lambda n: N - 1 - n), (lambda n: n)
    qsp = lambda ci: pl.BlockSpec((3, CHUNK, W), lambda n, b: (0, ci(n), b))
    gsp = lambda ci: pl.BlockSpec((CHUNK, LANES), lambda n, b: (ci(n), 0))
    osp = lambda ci: pl.BlockSpec((CHUNK, W), lambda n, b: (ci(n), b))
    ssp = lambda ci: pl.BlockSpec((1, hb, GDN_DIM, GDN_DIM), lambda n, b: (ci(n), b, 0, 0))
    asp = lambda ci: pl.BlockSpec((1, hb, CHUNK, CHUNK), lambda n, b: (ci(n), b, 0, 0))
    dq_shape = jax.ShapeDtypeStruct((3, T, nh * GDN_DIM), f32)
    dg_shape = jax.ShapeDtypeStruct((T, LANES), f32)
    return pl.pallas_call(
        body,
        out_shape=(dq_shape, dq_shape, dg_shape, dg_shape),
        grid=(N, nb),
        in_specs=[qsp(fw), qsp(bw), gsp(fw), gsp(bw), gsp(fw), gsp(bw), ssp(fw), ssp(bw), asp(fw), asp(bw), osp(fw), osp(bw)],
        out_specs=(qsp(fw), qsp(bw), gsp(fw), gsp(bw)),
        scratch_shapes=[pltpu.VMEM((2, nh, GDN_DIM, GDN_DIM), f32)],
        compiler_params=_cparams(("arbitrary", "arbitrary")),
        name="gdn_scan_bwd",
    )(qkvn, qkvn, gates, gates, gc, gc, st_f, st_b, a_f, a_b, do, do)


class _GlaBatch:
    def __init__(self, q, k, G, nf):
        B, Cn = q.shape[0], q.shape[1]
        self.incl, _, _ = _dir_masks(B, nf, Cn)
        self.eG = jnp.exp(G)
        self.enG = jnp.exp(-G)
        self.qg = q * (GLA_DK**-0.5) * self.eG
        self.kg = k * self.enG
        self.attn = jnp.where(self.incl, _mx(self.qg, self.kg, "nt"), 0.0)
        gl = _last_row(G, nf)
        self.ekd = jnp.exp(gl - G)
        self.kd = k * self.ekd
        self.egl = jnp.exp(gl)


def _gla_chain_inputs(refs_f, refs_b):
    nh = GLA_HEADS
    out = []
    for i, width in enumerate((GLA_DK, GLA_DK, GLA_DV, GLA_DK)):
        out.append(jnp.stack(_heads(refs_f[i], (), nh, width) + _heads(refs_b[i], (), nh, width)))
    return out


def _gla_specs(N, ci):
    nh = GLA_HEADS
    return [
        pl.BlockSpec((CHUNK, nh * GLA_DK), lambda n: (ci(n), 4096 // (nh * GLA_DK))),
        pl.BlockSpec((CHUNK, nh * GLA_DK), lambda n: (ci(n), 4608 // (nh * GLA_DK))),
        pl.BlockSpec((CHUNK, nh * GLA_DV), lambda n: (ci(n), 5120 // (nh * GLA_DV))),
        pl.BlockSpec((CHUNK, nh * GLA_DK), lambda n: (ci(n), 0)),
    ]


def _gla_fwd(proj, G_f, G_b):
    T = proj.shape[0]
    N = T // CHUNK
    nh = GLA_HEADS

    def body(qf, kf, vf, gf, qb, kb, vb, gb, of_ref, ob_ref, sf_ref, sb_ref, s_scr):
        n = pl.program_id(0)

        @pl.when(n == 0)
        def _():
            s_scr[...] = jnp.zeros((2 * nh, GLA_DV, GLA_DK), f32)

        q, k, v, G = _gla_chain_inputs((qf, kf, vf, gf), (qb, kb, vb, gb))
        ck = _GlaBatch(q, k, G, nh)
        St = s_scr[...]
        sf_ref[0], sb_ref[0] = St[:nh], St[nh:]
        o = _mx(ck.qg, St, "nt") + _mx(ck.attn, v, "nn")
        for i in range(nh):
            sl = slice(i * GLA_DV, (i + 1) * GLA_DV)
            of_ref[:, sl] = o[i]
            ob_ref[:, sl] = o[nh + i]
        s_scr[...] = St * ck.egl + _mx(v, ck.kd, "tn")

    fw, bw = (lambda n: n), (lambda n: N - 1 - n)
    osp = lambda ci: pl.BlockSpec((CHUNK, nh * GLA_DV), lambda n: (ci(n), 0))
    ssp = lambda ci: pl.BlockSpec((1, nh, GLA_DV, GLA_DK), lambda n: (ci(n), 0, 0, 0))
    o_shape = jax.ShapeDtypeStruct((T, nh * GLA_DV), f32)
    s_shape = jax.ShapeDtypeStruct((N, nh, GLA_DV, GLA_DK), f32)
    return pl.pallas_call(
        body,
        out_shape=(o_shape, o_shape, s_shape, s_shape),
        grid=(N,),
        in_specs=_gla_specs(N, fw) + _gla_specs(N, bw),
        out_specs=(osp(fw), osp(bw), ssp(fw), ssp(bw)),
        scratch_shapes=[pltpu.VMEM((2 * nh, GLA_DV, GLA_DK), f32)],
        compiler_params=_cparams(("arbitrary",)),
        name="gla_scan_fwd",
    )(proj, proj, proj, G_f, proj, proj, proj, G_b)


def _gla_bwd(proj, G_f, G_b, st_f, st_b, do):
    T = proj.shape[0]
    N = T // CHUNK
    nh = GLA_HEADS

    def body(qf, kf, vf, gf, sf_ref, dof_ref, qb, kb, vb, gb, sb_ref, dob_ref,
             dqf_ref, dkf_ref, dvf_ref, dgf_ref, dqb_ref, dkb_ref, dvb_ref, dgb_ref, ds_scr):
        n = pl.program_id(0)

        @pl.when(n == 0)
        def _():
            ds_scr[...] = jnp.zeros((2 * nh, GLA_DV, GLA_DK), f32)

        q, k, v, G = _gla_chain_inputs((qf, kf, vf, gf), (qb, kb, vb, gb))
        ck = _GlaBatch(q, k, G, nh)
        St = jnp.concatenate([sf_ref[0], sb_ref[0]], axis=0)
        dSt = ds_scr[...]
        dov = jnp.stack(_heads(dof_ref, (), nh, GLA_DV) + _heads(dob_ref, (), nh, GLA_DV))
        d_attn = jnp.where(ck.incl, _mx(dov, v, "nt"), 0.0)
        d_qg = _mx(dov, St, "nn") + _mx(d_attn, ck.kg, "nn")
        d_kg = _mx(d_attn, ck.qg, "tn")
        d_v = _mx(ck.attn, dov, "tn") + _mx(ck.kd, dSt, "nt")
        d_kd = _mx(v, dSt, "nn")
        ds_scr[...] = dSt * ck.egl + _mx(dov, ck.qg, "tn")
        kdd = d_kd * ck.kd
        d_gl = jnp.sum(dSt * St, axis=1, keepdims=True) * ck.egl + jnp.sum(kdd, axis=1, keepdims=True)
        d_q = d_qg * ck.eG * (GLA_DK**-0.5)
        d_k = d_kg * ck.enG + d_kd * ck.ekd
        d_G = _add_at_last_row(d_qg * ck.qg - d_kg * ck.kg - kdd, d_gl, nh)
        shp = (2 * nh, CHUNK, CHUNK)
        bb, ii, jj = _iota3(shp, 0), _iota3(shp, 1), _iota3(shp, 2)
        tri = jnp.logical_or(jnp.logical_and(bb < nh, jj >= ii), jnp.logical_and(bb >= nh, jj <= ii))
        d_gk = _exact01(jnp.where(tri, 1.0, 0.0), d_G)
        for d, (dq_ref, dk_ref, dv_ref, dg_ref) in enumerate(((dqf_ref, dkf_ref, dvf_ref, dgf_ref), (dqb_ref, dkb_ref, dvb_ref, dgb_ref))):
            for i in range(nh):
                c = d * nh + i
                ks, vs = slice(i * GLA_DK, (i + 1) * GLA_DK), slice(i * GLA_DV, (i + 1) * GLA_DV)
                dq_ref[:, ks] = d_q[c]
                dk_ref[:, ks] = d_k[c]
                dv_ref[:, vs] = d_v[c]
                dg_ref[:, ks] = d_gk[c]

    def in_specs(ci):
        return _gla_specs(N, ci) + [
            pl.BlockSpec((1, nh, GLA_DV, GLA_DK), lambda n: (ci(n), 0, 0, 0)),
            pl.BlockSpec((CHUNK, nh * GLA_DV), lambda n: (ci(n), 0)),
        ]

    def out_specs(ci):
        ksp = pl.BlockSpec((CHUNK, nh * GLA_DK), lambda n: (ci(n), 0))
        return [ksp, ksp, pl.BlockSpec((CHUNK, nh * GLA_DV), lambda n: (ci(n), 0)), ksp]

    fw, bw = (lambda n: N - 1 - n), (lambda n: n)
    k_shape = jax.ShapeDtypeStruct((T, nh * GLA_DK), f32)
    v_shape = jax.ShapeDtypeStruct((T, nh * GLA_DV), f32)
    return pl.pallas_call(
        body,
        out_shape=(k_shape, k_shape, v_shape, k_shape) * 2,
        grid=(N,),
        in_specs=in_specs(fw) + in_specs(bw),
        out_specs=tuple(out_specs(fw) + out_specs(bw)),
        scratch_shapes=[pltpu.VMEM((2 * nh, GLA_DV, GLA_DK), f32)],
        compiler_params=_cparams(("arbitrary",)),
        name="gla_scan_bwd",
    )(proj, proj, proj, G_f, st_f, do, proj, proj, proj, G_b, st_b, do)


def _gla_dsum(dq_f, dq_b, dk_f, dk_b, dv_f, dv_b, dproj, tb=256):
    T = dq_f.shape[0]
    wk, wv = GLA_HEADS * GLA_DK, GLA_HEADS * GLA_DV

    def body(qf, qb, kf, kb, vf, vb, _, o_ref):
        o_ref[:, :wk] = (qf[...] + qb[...]).astype(o_ref.dtype)
        o_ref[:, wk : 2 * wk] = (kf[...] + kb[...]).astype(o_ref.dtype)
        o_ref[:, 2 * wk :] = (vf[...] + vb[...]).astype(o_ref.dtype)

    ksp = pl.BlockSpec((tb, wk), lambda i: (i, 0))
    vsp = pl.BlockSpec((tb, wv), lambda i: (i, 0))
    return pl.pallas_call(
        body,
        out_shape=jax.ShapeDtypeStruct(dproj.shape, dproj.dtype),
        grid=(T // tb,),
        in_specs=[ksp, ksp, ksp, ksp, vsp, vsp, _ANY_SPEC],
        out_specs=pl.BlockSpec((tb, 2 * wk + wv), lambda i: (i, DP_GLA_QKV_BLK)),
        input_output_aliases={6: 0},
        compiler_params=_cparams(("parallel",)),
        name="gla_dsum",
    )(dq_f, dq_b, dk_f, dk_b, dv_f, dv_b, dproj)


def _local_step(x, target, wcat, conv_w, avec, dvec, gdn_norm_w, w2f, w2b, b2f, b2b, gla_norm_w, proj_weights, w_pre, w_post,
                send_proj_grads=None, send_in_grads=None):
    h, h_t = _rms_pre(x, w_pre)
    proj = _matmul(h, wcat, "nt", f32, "proj_in", 2048, 512, 1024)
    qkvn = _gdn_prep(proj, conv_w)
    gates, gc = _gdn_gates(proj, avec, dvec)
    oa_f, oa_b, sa_f, sa_b, inv_f, inv_b = _gdn_fwd(qkvn, gates, gc)
    og = _out_norm(oa_f, oa_b, proj, Z_BLK, gdn_norm_w, GDN_DIM, "gdn_out")
    G_f, G_b = _gla_prep(proj, w2f, w2b, b2f, b2b)
    ob_f, ob_b, sb_f, sb_b = _gla_fwd(proj, G_f, G_b)
    obg = _out_norm(ob_f, ob_b, proj, G_B_BLK, gla_norm_w, GLA_DV, "gla_out")
    wpg, wpl, wout = proj_weights(obg)
    (loss_row, d_wpost, dy, m, dout, dya, dyb, dproj, dog, dobg) = _mid(og, obg, proj, x, target, wpg, wpl, wout, w_post)

    d_wout = _matmul(m, dout, "tn", MXU_DTYPE, "dw_out", 1024, 512, 2048)
    d_wpg = _matmul(og, dya, "tn", MXU_DTYPE, "dw_proj_gdn", 1024, 512, 2048)
    d_wpl = _matmul(obg, dyb, "tn", MXU_DTYPE, "dw_proj_gla", 1024, 512, 2048)
    if send_proj_grads is not None:
        gla_norm_w = gla_norm_w + send_proj_grads(d_wpg, d_wpl, d_wout)

    do_b, dproj, d_gla_norm = _out_norm_bwd(dobg, ob_f, ob_b, proj, G_B_BLK, gla_norm_w, GLA_DV, "gla_out_bwd", dproj)
    dq_f, dk_f, dv_f, dgk_f, dq_b, dk_b, dv_b, dgk_b = _gla_bwd(proj, G_f, G_b, sb_f, sb_b, do_b)
    dproj = _gla_dsum(dq_f, dq_b, dk_f, dk_b, dv_f, dv_b, dproj)
    dsmall_gla, d_w2f, d_w2b, d_b2f, d_b2b = _gla_prep_bwd(proj, w2f, w2b, b2f, b2b, dgk_f, dgk_b)

    do_a, dproj, d_gdn_norm = _out_norm_bwd(dog, oa_f, oa_b, proj, Z_BLK, gdn_norm_w, GDN_DIM, "gdn_out_bwd", dproj)
    dqkv_f, dqkv_b, dg_f, dg_b = _gdn_bwd(qkvn, gates, gc, sa_f, sa_b, inv_f, inv_b, do_a)
    dproj, d_conv = _gdn_prep_bwd(proj, conv_w, dqkv_f, dqkv_b, dproj)
    dproj, d_alog, d_dtb = _gdn_gates_bwd(proj, avec, dvec, dg_f, dg_b, dsmall_gla, dproj)

    d_wcat = _matmul(h_t, dproj, "nn", MXU_DTYPE, "dw_in", 1024, 512, 2048, out_t=True)
    started = None if send_in_grads is None else send_in_grads(d_wcat)
    dh = _matmul(dproj, wcat, "nn", f32, "dh", 1024, 1024, 2432, after=started)
    grad_x, d_wpre = _rms_pre_bwd(x, w_pre, dh, dy)
    return dict(loss_row=loss_row, grad_x=grad_x, d_wcat=d_wcat, d_conv=d_conv, d_alog=d_alog, d_dtb=d_dtb,
                d_gdn_norm=d_gdn_norm, d_wpg=d_wpg, d_w2f=d_w2f, d_w2b=d_w2b, d_b2f=d_b2f, d_b2b=d_b2b,
                d_gla_norm=d_gla_norm, d_wpl=d_wpl, d_wout=d_wout, d_wpost=d_wpost, d_wpre=d_wpre)


def _to_cat(wf):
    pad = jnp.zeros((N_CAT - N_IN,) + wf.shape[1:], wf.dtype)
    return jnp.concatenate([wf[:4096], wf[4128:6176], wf[7232:9280], wf[6176:7200], wf[4096:4128], wf[7200:7232], pad], axis=0)


def _from_cat(wc):
    return jnp.concatenate([wc[:4096], wc[9216:9248], wc[4096:6144], wc[8192:9216], wc[9248:9280], wc[6144:8192]], axis=0)


def _lane_row(*pieces):
    row = jnp.concatenate(pieces, axis=1)
    return jnp.pad(row, ((0, 0), (0, LANES - row.shape[1])))


def _pad_w2(w2, row0):
    return jnp.pad(w2, ((row0, LANES - row0 - w2.shape[0]), (0, 0)))


SM_ROWS, SM_COLS = 32, 384
REP_ROWS = 32


def _pack_small_shard(conv, w2f, w2b):
    lead = conv.shape[:-2]
    z = lambda r, c: jnp.zeros(lead + (r, c), f32)
    top = jnp.concatenate([conv, z(8 - CONV_K, SM_COLS)], axis=-2)
    mid = jnp.concatenate([w2f, w2b, z(16, SM_COLS - 128)], axis=-1)
    return jnp.concatenate([top, mid, z(SM_ROWS - 24, SM_COLS)], axis=-2)


LOSS_ROW = 28


def _pack_rep(ln_pre, ln_post, b2f, b2b, gla_norm, gdn_norm, alog_f, alog_b, dt_f, dt_b, loss=None):
    last = _lane_row(alog_f, alog_b, dt_f, dt_b)
    extra = jnp.zeros((1, LANES), f32) if loss is None else _lane_row(loss)
    rows = [ln_pre.reshape(8, LANES), ln_post.reshape(8, LANES), b2f.reshape(4, LANES), b2b.reshape(4, LANES),
            gla_norm.reshape(2, LANES), gdn_norm.reshape(1, LANES), last, extra, jnp.zeros((REP_ROWS - LOSS_ROW - 1, LANES), f32)]
    return jnp.concatenate(rows, axis=0)


def _unpack_rep(p):
    return dict(ln_pre_w=p[0:8].reshape(1, 1024), ln_post_w=p[8:16].reshape(1, 1024), gk_b2_fwd=p[16:20].reshape(1, 512),
                gk_b2_bwd=p[20:24].reshape(1, 512), gla_norm_w=p[24:26].reshape(1, 256), gdn_norm_w=p[26:27],
                a_log_fwd=p[27:28, 0:8], a_log_bwd=p[27:28, 8:16], dt_bias_fwd=p[27:28, 16:24], dt_bias_bwd=p[27:28, 24:32])


_MESH = pl.DeviceIdType.MESH


N_BIG_COPIES, N_SMALL_COPIES = 9, 7


def _gather_weights(w_in_s, sm_s):
    shapes = (w_in_s.shape, sm_s.shape)
    dtypes = (MXU_DTYPE, f32)

    def body(win_ref, sm_ref, gin_ref, gsm_ref, send_sems, recv_sems):
        x, y, c = lax.axis_index("x"), lax.axis_index("y"), lax.axis_index("c")
        me, sibling = (x, y, c), (x, y, 1 - c)
        chips = [(1 - x, y), (x, 1 - y), (1 - x, 1 - y)]
        bufs = (gin_ref, gsm_ref)

        def idx(px, py, pc):
            return 4 * px + 2 * py + pc

        for r in range(w_in_s.shape[1] // LANES):
            cols = slice(r * LANES, (r + 1) * LANES)
            gin_ref[idx(*me), :, cols] = win_ref[:, cols].astype(MXU_DTYPE)
        gsm_ref[idx(*me)] = sm_ref[...]

        def copy(sem, blk, to):
            return pltpu.make_async_remote_copy(src_ref=blk, dst_ref=blk, send_sem=send_sems.at[sem], recv_sem=recv_sems.at[sem],
                                                device_id=to, device_id_type=_MESH)

        def small(k, block, to):
            return copy(N_BIG_COPIES + k, gsm_ref.at[idx(*block)], to)

        half = w_in_s.shape[1] // 2
        xn, yn, dg = (1 - x, y), (x, 1 - y), (1 - x, 1 - y)

        def big(sem, block, to, part=None):
            blk = gin_ref.at[idx(*block)]
            return copy(sem, blk if part is None else blk.at[:, pl.ds(part * half, half)], to)

        started = [big(0, me, sibling), big(1, me, (*xn, c)), big(2, me, (*yn, c)), small(0, me, sibling)]
        started += [small(1 + j, me, (*chip, c)) for j, chip in enumerate(chips)]
        for cp in started:
            cp.start()

        def then(arrived, forwards):
            arrived.wait_recv()
            for cp in forwards:
                cp.start()
            started.extend(forwards)

        then(big(1, (*xn, c), me), [big(3, (*xn, c), (*yn, c), 1), big(5, (*xn, c), sibling)])
        then(big(2, (*yn, c), me), [big(4, (*yn, c), (*xn, c), 0), big(6, (*yn, c), sibling)])
        then(big(4, (*dg, c), me, 0), [big(7, (*dg, c), sibling, 0)])
        then(big(3, (*dg, c), me, 1), [big(8, (*dg, c), sibling, 1)])
        for j, chip in enumerate(chips):
            then(small(1 + j, (*chip, c), me), [small(4 + j, (*chip, c), sibling)])
        big(0, sibling, me).wait_recv()
        big(5, (*xn, 1 - c), me).wait_recv()
        big(6, (*yn, 1 - c), me).wait_recv()
        big(7, (*dg, 1 - c), me, 0).wait_recv()
        big(8, (*dg, 1 - c), me, 1).wait_recv()
        small(0, sibling, me).wait_recv()
        for j, chip in enumerate(chips):
            small(4 + j, (*chip, 1 - c), me).wait_recv()
        for cp in started:
            cp.wait_send()

    vm = pl.BlockSpec(memory_space=pltpu.VMEM)
    return pl.pallas_call(
        body,
        out_shape=tuple(jax.ShapeDtypeStruct((N_DEV,) + s, d) for s, d in zip(shapes, dtypes)),
        in_specs=[vm, vm],
        out_specs=(vm, vm),
        scratch_shapes=[pltpu.SemaphoreType.DMA((N_BIG_COPIES + N_SMALL_COPIES,)), pltpu.SemaphoreType.DMA((N_BIG_COPIES + N_SMALL_COPIES,))],
        compiler_params=pltpu.CompilerParams(vmem_limit_bytes=VMEM_LIMIT),
        name="gather_weights",
    )(w_in_s, sm_s)


N_CHIP = 4
_EFFECT = pltpu.SideEffectType.DATAFLOW_SIDE_EFFECTING
_HBM_SPEC = pl.BlockSpec(memory_space=pltpu.HBM)
_SEM_SPEC = pl.BlockSpec(memory_space=pltpu.SEMAPHORE)


def _relation_peer(k, x, y, c):
    px = (1 - x) if (k & 4) else x
    py = (1 - y) if (k & 2) else y
    pc = (1 - c) if (k & 1) else c
    return (px, py, pc), 4 * px + 2 * py + pc


_PLAN_COPIES = {"gather": N_DEV - 1, "blocks": N_DEV - 1, "sibling": 4, "chips": 3}
_PLAN_SLOTS = {"gather": N_DEV, "blocks": N_DEV, "sibling": 4, "chips": 4}


def _plan_copies(plan, src_ref, land_ref, send_sems, recv_sems, arrival):
    x, y, c = lax.axis_index("x"), lax.axis_index("y"), lax.axis_index("c")
    me, my_chip = 4 * x + 2 * y + c, 2 * x + y
    out = []

    def add(i, src, there, here, dev):
        out.append(pltpu.make_async_remote_copy(src_ref=src, dst_ref=land_ref.at[here if arrival else there], send_sem=send_sems.at[i],
                                                recv_sem=recv_sems.at[i], device_id=dev, device_id_type=_MESH))

    if plan in ("gather", "blocks"):
        for k in range(1, N_DEV):
            dev, p = _relation_peer(k, x, y, c)
            add(k - 1, src_ref.at[p] if plan == "blocks" else src_ref, me, p, dev)
    elif plan == "sibling":
        for j in range(4):
            add(j, src_ref.at[2 * j + 1 - c], j, j, (x, y, 1 - c))
    else:
        for i, k in enumerate((2, 4, 6)):
            (px, py, pc), _ = _relation_peer(k, x, y, c)
            add(i, src_ref.at[2 * px + py], my_chip, 2 * px + py, (px, py, pc))
    return out


def _exchange_start(src, plan, name):
    n = _PLAN_COPIES[plan]
    land = lax.empty((_PLAN_SLOTS[plan],) + (src.shape if plan == "gather" else src.shape[1:]), src.dtype)

    def body(src_ref, land_ref, send_sems, recv_sems, src_thru, land_thru, token):
        for cp in _plan_copies(plan, src_ref, land_ref, send_sems, recv_sems, False):
            cp.start()
        token[...] = jnp.zeros_like(token)

    return pl.pallas_call(
        body,
        name=name,
        out_shape=(pltpu.SemaphoreType.DMA((n,)), pltpu.SemaphoreType.DMA((n,)), pltpu.HBM(src.shape, src.dtype),
                   pltpu.HBM(land.shape, land.dtype), jax.ShapeDtypeStruct((8, LANES), f32)),
        in_specs=(_HBM_SPEC, _HBM_SPEC),
        out_specs=(_SEM_SPEC, _SEM_SPEC, _HBM_SPEC, _HBM_SPEC, pl.BlockSpec(memory_space=pltpu.VMEM)),
        input_output_aliases={0: 2, 1: 3},
        compiler_params=pltpu.CompilerParams(has_side_effects=_EFFECT),
    )(pltpu.with_memory_space_constraint(src, pltpu.HBM), pltpu.with_memory_space_constraint(land, pltpu.HBM))


def _exchange_wait(started, after, plan, name):
    send_sems, recv_sems, src_thru, land_thru, _ = started

    def body(src_ref, land_ref, send_sems, recv_sems, after_ref, src_dead, got_ref):
        for cp in _plan_copies(plan, src_ref, land_ref, send_sems, recv_sems, True):
            cp.wait_send()
            cp.wait_recv()

    return pl.pallas_call(
        body,
        name=name,
        out_shape=(pltpu.HBM(src_thru.shape, src_thru.dtype), pltpu.HBM(land_thru.shape, land_thru.dtype)),
        in_specs=(_HBM_SPEC, _HBM_SPEC, _SEM_SPEC, _SEM_SPEC, pl.BlockSpec(memory_space=pl.ANY)),
        out_specs=(_HBM_SPEC, _HBM_SPEC),
        input_output_aliases={0: 0, 1: 1},
        compiler_params=pltpu.CompilerParams(has_side_effects=_EFFECT),
    )(src_thru, land_thru, send_sems, recv_sems, after)


def _pair_sum(g, r, name, tc=LANES):
    _, R, Cc = g.shape

    def body(g_ref, r_ref, o_ref):
        o_ref[...] = (g_ref[...].astype(f32) + r_ref[...].astype(f32)).astype(o_ref.dtype)

    return pl.pallas_call(
        body,
        out_shape=jax.ShapeDtypeStruct(r.shape, r.dtype),
        grid=(N_CHIP, Cc // tc),
        in_specs=[pl.BlockSpec((1, R, tc), lambda k, i: (2 * k + lax.axis_index("c"), 0, i)), pl.BlockSpec((1, R, tc), lambda k, i: (k, 0, i))],
        out_specs=pl.BlockSpec((1, R, tc), lambda k, i: (k, 0, i)),
        compiler_params=_cparams(("parallel", "parallel")),
        name=name,
    )(g, r)


def _exchange_small(g_sm, g_rep):
    srcs = (g_sm, g_rep)

    def body(in0, in1, out0, out1, send_sems, recv_sems, local_sems):
        x, y, c = lax.axis_index("x"), lax.axis_index("y"), lax.axis_index("c")
        me = 4 * x + 2 * y + c
        ins, outs = (in0, in1), (out0, out1)

        def copy(a, k, arrival):
            dev, p = _relation_peer(k, x, y, c)
            return pltpu.make_async_remote_copy(src_ref=ins[a].at[p] if a == 0 else ins[a], dst_ref=outs[a].at[p if arrival else me],
                                                send_sem=send_sems.at[7 * a + k - 1], recv_sem=recv_sems.at[7 * a + k - 1],
                                                device_id=dev, device_id_type=_MESH)

        pairs = [(a, k) for a in (0, 1) for k in range(1, N_DEV)]
        sends = [copy(a, k, False) for a, k in pairs]
        for cp in sends:
            cp.start()
        local = [pltpu.make_async_copy(ins[0].at[me], outs[0].at[me], local_sems.at[0]),
                 pltpu.make_async_copy(ins[1], outs[1].at[me], local_sems.at[1])]
        for cp in local:
            cp.start()
        for a, k in pairs:
            copy(a, k, True).wait_recv()
        for cp in sends:
            cp.wait_send()
        for cp in local:
            cp.wait()

    return pl.pallas_call(
        body,
        out_shape=(jax.ShapeDtypeStruct(g_sm.shape, g_sm.dtype), jax.ShapeDtypeStruct((N_DEV,) + g_rep.shape, g_rep.dtype)),
        in_specs=[_HBM_SPEC] * 2,
        out_specs=(_HBM_SPEC,) * 2,
        scratch_shapes=[pltpu.SemaphoreType.DMA((14,)), pltpu.SemaphoreType.DMA((14,)), pltpu.SemaphoreType.DMA((2,))],
        name="exchange_small",
    )(*srcs)


def _sum_adam(parts, w, m, v, name, tb, tc=None, own=None, own_slot=None):
    R, Cc = w.shape
    n_parts = parts.shape[0]
    tb = R if tc else min(tb, R)
    tc = tc or Cc
    assert R % tb == 0 and Cc % tc == 0
    bc1 = 1.0 - ADAM_B1**ADAM_STEP
    bc2 = 1.0 - ADAM_B2**ADAM_STEP
    extra = [] if own is None else [own]

    def body(p_ref, w_ref, m_ref, v_ref, *rest):
        g_ref, d_ref, nm_ref, nv_ref = rest[len(extra) :]

        def part(j):
            pj = p_ref[j].astype(f32)
            return pj if own is None else jnp.where(own_slot() == j, rest[0][0].astype(f32), pj)

        g = part(0)
        for j in range(1, n_parts):
            g = g + part(j)
        g_ref[...] = g
        m2 = ADAM_B1 * m_ref[...] + (1.0 - ADAM_B1) * g
        v2 = ADAM_B2 * v_ref[...] + (1.0 - ADAM_B2) * (g * g)
        nm_ref[...] = m2
        nv_ref[...] = v2
        d_ref[...] = -ADAM_LR * ((m2 / bc1) / (jnp.sqrt(v2 / bc2) + ADAM_EPS) + ADAM_WD * w_ref[...])

    blk = pl.BlockSpec((tb, tc), lambda i, j: (i, j))
    o = jax.ShapeDtypeStruct((R, Cc), f32)
    return pl.pallas_call(
        body,
        out_shape=(o, o, o, o),
        grid=(R // tb, Cc // tc),
        in_specs=[pl.BlockSpec((n_parts, tb, tc), lambda i, j: (0, i, j)), blk, blk, blk]
        + [pl.BlockSpec((1, tb, tc), lambda i, j: (own_slot(), i, j))] * len(extra),
        out_specs=(blk, blk, blk, blk),
        compiler_params=_cparams(("parallel", "parallel")),
        name=name,
    )(parts, w, m, v, *extra)


def _my_device():
    return 4 * lax.axis_index("x") + 2 * lax.axis_index("y") + lax.axis_index("c")


def _my_chip():
    return 2 * lax.axis_index("x") + lax.axis_index("y")


_WEIGHTS = ["ln_pre_w", "w_in", "conv_w", "a_log_fwd", "a_log_bwd", "dt_bias_fwd", "dt_bias_bwd", "gdn_norm_w", "w_proj_gdn",
            "gk_w2_fwd", "gk_b2_fwd", "gk_w2_bwd", "gk_b2_bwd", "gla_norm_w", "w_proj_gla", "w_out", "ln_post_w"]
_REP_ORDER = ["ln_pre_w", "ln_post_w", "gk_b2_fwd", "gk_b2_bwd", "gla_norm_w", "gdn_norm_w", "a_log_fwd", "a_log_bwd", "dt_bias_fwd", "dt_bias_bwd"]


def kernel(x, ln_pre_w, w_in, conv_w, a_log_fwd, a_log_bwd, dt_bias_fwd, dt_bias_bwd, gdn_norm_w, w_proj_gdn, gk_w2_fwd, gk_b2_fwd, gk_w2_bwd, gk_b2_bwd, gla_norm_w, w_proj_gla, w_out, ln_post_w, loss_target, m_ln_pre_w, m_w_in, m_conv_w, m_a_log_fwd, m_a_log_bwd, m_dt_bias_fwd, m_dt_bias_bwd, m_gdn_norm_w, m_w_proj_gdn, m_gk_w2_fwd, m_gk_b2_fwd, m_gk_w2_bwd, m_gk_b2_bwd, m_gla_norm_w, m_w_proj_gla, m_w_out, m_ln_post_w, v_ln_pre_w, v_w_in, v_conv_w, v_a_log_fwd, v_a_log_bwd, v_dt_bias_fwd, v_dt_bias_bwd, v_gdn_norm_w, v_w_proj_gdn, v_gk_w2_fwd, v_gk_b2_fwd, v_gk_w2_bwd, v_gk_b2_bwd, v_gla_norm_w, v_w_proj_gla, v_w_out, v_ln_post_w):
    args = locals()
    W = {n: args[n] for n in _WEIGHTS}
    M = {n: args["m_" + n] for n in _WEIGHTS}
    V = {n: args["v_" + n] for n in _WEIGHTS}

    wp_stack = lambda P: jnp.concatenate([P["w_proj_gdn"], P["w_proj_gla"], P["w_out"]], axis=0)
    sm_pack = lambda P: _pack_small_shard(P["conv_w"][0], P["gk_w2_fwd"][0], P["gk_w2_bwd"][0])
    me = 4 * lax.axis_index("x") + 2 * lax.axis_index("y") + lax.axis_index("c")
    w_in_t, m_in_t, v_in_t = (jnp.transpose(a[0]) for a in (w_in, m_w_in, v_w_in))
    g_in, g_sm = _gather_weights(w_in_t, sm_pack(W))
    wp_own = wp_stack(W).astype(MXU_DTYPE)
    wp_started = _exchange_start(wp_own, "gather", "gather_proj_start")

    def proj_weights(after):
        own, land = _exchange_wait(wp_started, after, "gather", "gather_proj_wait")
        g_p = lax.dynamic_update_slice(land, own[None], (me, 0, 0, 0))
        return tuple(g_p[:, i].reshape(D_MODEL, D_MODEL) for i in range(3))

    rows_p = 3 * D_MODEL // N_DEV
    sent = {}

    def send_proj_grads(d_wpg, d_wpl, d_wout):
        p_p = jnp.stack([d_wpg, d_wpl, d_wout]).reshape(3, N_DEV, D_MODEL // N_DEV, D_MODEL).transpose(1, 0, 2, 3)
        sent["p_p"] = p_p.reshape(N_DEV, rows_p, D_MODEL)
        sent["started"] = _exchange_start(sent["p_p"], "blocks", "exchange_proj_start")
        return sent["started"][4][0:1, 0:1]

    def send_in_grads(d_wcat):
        p_in = _from_cat(d_wcat).reshape(N_DEV, SHARD_IN, D_MODEL)
        sib = _exchange_start(p_in, "sibling", "exchange_in_sibling_start")
        p_p, land_p = _exchange_wait(sent["started"], sib[4], "blocks", "exchange_proj_wait")
        sent["o_p"] = _sum_adam(land_p, wp_stack(W).reshape(rows_p, D_MODEL), wp_stack(M).reshape(rows_p, D_MODEL),
                                wp_stack(V).reshape(rows_p, D_MODEL), "adam_w_proj", 128, own=p_p, own_slot=_my_device)
        p_in, sib_land = _exchange_wait(sib, sent["o_p"][0], "sibling", "exchange_in_sibling_wait")
        sent["chips"] = _exchange_start(_pair_sum(p_in, sib_land, "pair_sum_w_in", tc=512), "chips", "exchange_in_chips_start")
        return sent["chips"][4]

    wcat = _to_cat(g_in.reshape(N_IN, D_MODEL))
    conv_full = g_sm[:, 0:CONV_K, :].transpose(1, 0, 2).reshape(CONV_K, N_DEV * SM_COLS)
    w2f_full = g_sm[:, 8:24, 0:64].transpose(1, 0, 2).reshape(16, 512)
    w2b_full = g_sm[:, 8:24, 64:128].transpose(1, 0, 2).reshape(16, 512)

    avec = _lane_row(a_log_fwd, a_log_bwd)
    dvec = _lane_row(dt_bias_fwd, dt_bias_bwd)
    G = _local_step(x[0], loss_target[0], wcat, conv_full, avec, dvec, gdn_norm_w, _pad_w2(w2f_full, 32), _pad_w2(w2b_full, 48),
                    gk_b2_fwd, gk_b2_bwd, gla_norm_w, proj_weights, ln_pre_w + wp_started[4][0:1, 0:1], ln_post_w, send_proj_grads,
                    send_in_grads)

    p_sm = _pack_small_shard(G["d_conv"].reshape(CONV_K, N_DEV, SM_COLS).transpose(1, 0, 2),
                             G["d_w2f"][32:48].reshape(16, N_DEV, 64).transpose(1, 0, 2),
                             G["d_w2b"][48:64].reshape(16, N_DEV, 64).transpose(1, 0, 2))
    p_rep = _pack_rep(G["d_wpre"], G["d_wpost"], G["d_b2f"], G["d_b2b"], G["d_gla_norm"], G["d_gdn_norm"],
                      G["d_alog"][:, 0:8], G["d_alog"][:, 8:16], G["d_dtb"][:, 0:8], G["d_dtb"][:, 8:16],
                      loss=0.5 * jnp.sum(G["loss_row"], axis=1, keepdims=True) / D_MODEL)
    p_small = jnp.concatenate([p_sm, jnp.broadcast_to(p_rep[None], (N_DEV,) + p_rep.shape)], axis=2)
    small_started = _exchange_start(p_small, "blocks", "exchange_small_start")
    q_in, land_in = _exchange_wait(sent["chips"], small_started[4], "chips", "exchange_in_chips_wait")
    o_in = _sum_adam(land_in, w_in_t, m_in_t, v_in_t, "adam_w_in", SHARD_IN, tc=256, own=q_in, own_slot=_my_chip)
    o_p = sent["o_p"]
    p_small, land_small = _exchange_wait(small_started, o_in[0], "blocks", "exchange_small_wait")
    rep_pack = lambda P: _pack_rep(*[P[n] for n in _REP_ORDER])
    small_pack = lambda P: jnp.concatenate([sm_pack(P), rep_pack(P)], axis=1)
    o_small = _sum_adam(land_small, small_pack(W), small_pack(M), small_pack(V), "adam_small", SM_ROWS, own=p_small, own_slot=_my_device)
    o_sm, o_rep = [o[:, :SM_COLS] for o in o_small], [o[:, SM_COLS:] for o in o_small]

    res = []
    for kind in range(4):
        rep = _unpack_rep(o_rep[kind])
        per = dict(rep)
        per["w_in"] = jnp.transpose(o_in[kind])[None]
        pp = o_p[kind].reshape(3, 1, D_MODEL // N_DEV, D_MODEL)
        per["w_proj_gdn"], per["w_proj_gla"], per["w_out"] = pp[0], pp[1], pp[2]
        per["conv_w"] = o_sm[kind][None, 0:CONV_K, :]
        per["gk_w2_fwd"] = o_sm[kind][None, 8:24, 0:64]
        per["gk_w2_bwd"] = o_sm[kind][None, 8:24, 64:128]
        res.append([per[n] for n in _WEIGHTS])
    loss = o_rep[0][LOSS_ROW, 0]
    return (loss, G["grad_x"][None], *res[0], *res[1], *res[2], *res[3])
```

```python
import functools
import math

import jax
import jax.numpy as jnp
from jax import lax
from jax.experimental import pallas as pl
from jax.experimental.pallas import tpu as pltpu

f32 = jnp.float32
MXU_DTYPE = jnp.bfloat16
HI = lax.Precision.HIGHEST

D_MODEL = 1024
CHUNK = 64
NORM_EPS = 1e-6
GDN_HEADS, GDN_DIM = 8, 128
GLA_HEADS, GLA_DK, GLA_DV = 4, 128, 256
GATE_NORMALIZER = 16.0
CONV_K = 5
N_IN = 9280
N_DEV = 8
SHARD_IN = N_IN // N_DEV

N_CAT = 9728
SMALL_OFF = 9216
LANES = 128

ADAM_LR, ADAM_B1, ADAM_B2, ADAM_EPS, ADAM_WD, ADAM_STEP = 0.001, 0.9, 0.999, 1e-08, 0.01, 10

VMEM_LIMIT = 56 * 1024 * 1024


def _cparams(sem=None):
    return pltpu.CompilerParams(dimension_semantics=sem, vmem_limit_bytes=VMEM_LIMIT)


_DN = {"nn": (((1,), (0,)), ((), ())), "nt": (((1,), (1,)), ((), ())), "tn": (((0,), (0,)), ((), ()))}
_BDN = {"nn": (((2,), (1,)), ((0,), (0,))), "nt": (((2,), (2,)), ((0,), (0,))), "tn": (((1,), (1,)), ((0,), (0,)))}


def _dot16(a, b, kind):
    return lax.dot_general(a, b, (_DN if a.ndim == 2 else _BDN)[kind], preferred_element_type=f32)


def _mx(a, b, kind):
    return _dot16(a.astype(MXU_DTYPE), b.astype(MXU_DTYPE), kind)


def _split2(a):
    hi = a.astype(jnp.bfloat16)
    return hi, (a - hi.astype(f32)).astype(jnp.bfloat16)


def _split3(a):
    p1 = a.astype(jnp.bfloat16)
    r = a - p1.astype(f32)
    p2 = r.astype(jnp.bfloat16)
    return p1, p2, (r - p2.astype(f32)).astype(jnp.bfloat16)


def _h3s(a2, b2, kind):
    (ah, al), (bh, bl) = a2, b2
    return _dot16(ah, bh, kind) + (_dot16(ah, bl, kind) + _dot16(al, bh, kind))


def _h3(a, b, kind):
    return _h3s(_split2(a), _split2(b), kind)


def _exact01(t01, x, kind="nn"):
    t = t01.astype(jnp.bfloat16)
    x1, x2, x3 = _split3(x)
    return _dot16(t, x1, kind) + (_dot16(t, x2, kind) + _dot16(t, x3, kind))


def _colsum_col(e):
    ones = jnp.ones(e.shape[:-1] + (LANES,), jnp.bfloat16)
    e1, e2, e3 = _split3(e)
    return (_dot16(e1, ones, "tn") + (_dot16(e2, ones, "tn") + _dot16(e3, ones, "tn")))[..., 0:1]


def _iota3(shape, axis):
    return lax.broadcasted_iota(jnp.int32, shape, axis)


def _dir_masks(B, nf, Cn):
    shp = (B, Cn, Cn)
    bb, ii, jj = _iota3(shp, 0), _iota3(shp, 1), _iota3(shp, 2)
    fwd = bb < nf
    rev = jnp.logical_not(fwd)
    incl = jnp.logical_or(jnp.logical_and(fwd, ii >= jj), jnp.logical_and(rev, ii <= jj))
    strict = jnp.logical_and(incl, ii != jj)
    return incl, strict, ii == jj


def _last_row(x, nf):
    B, Cn = x.shape[0], x.shape[1]
    fwd = _iota3((B, 1, 1), 0) < nf
    return jnp.where(fwd, x[:, Cn - 1 : Cn, :], x[:, 0:1, :])


def _add_at_last_row(x, val, nf):
    B, Cn = x.shape[0], x.shape[1]
    bb, rr = _iota3((B, Cn, 1), 0), _iota3((B, Cn, 1), 1)
    at = jnp.logical_or(jnp.logical_and(bb < nf, rr == Cn - 1), jnp.logical_and(bb >= nf, rr == 0))
    return x + jnp.where(at, val, 0.0)


def _sigmoid(x):
    return jax.nn.sigmoid(x)


def _silu(x):
    return x * _sigmoid(x)


def _silu_grad(x):
    s = _sigmoid(x)
    return s * (1.0 + x * (1.0 - s))


def _softplus(x):
    u = jnp.exp(-jnp.abs(x))
    l1p = jnp.where(u < 1e-3, u * (1.0 - u * (0.5 - u * (1.0 / 3.0))), jnp.log(1.0 + u))
    return jnp.maximum(x, 0.0) + l1p


def _mxr(x):
    return x.astype(MXU_DTYPE).astype(f32)


def _iota2(shape, axis):
    return lax.broadcasted_iota(jnp.int32, shape, axis)


def _matmul(a, b, kind, out_dtype, name, tm, tn, tk, out_t=False, after=None):
    extra = [] if after is None else [after]
    if kind == "nn":
        (M, K), N = a.shape, b.shape[1]
    elif kind == "nt":
        (M, K), N = a.shape, b.shape[0]
    else:
        (K, M), N = a.shape, b.shape[1]
    tm, tn, tk = min(tm, M), min(tn, N), min(tk, K)
    assert M % tm == 0 and N % tn == 0 and K % tk == 0, (name, M, N, K)
    nk = K // tk
    if kind == "tn":
        a_spec = pl.BlockSpec((tk, tm), lambda i, j, k: (k, i))
    else:
        a_spec = pl.BlockSpec((tm, tk), lambda i, j, k: (i, k))
    if kind == "nt":
        b_spec = pl.BlockSpec((tn, tk), lambda i, j, k: (j, k))
    else:
        b_spec = pl.BlockSpec((tk, tn), lambda i, j, k: (k, j))

    def body(a_ref, b_ref, *rest):
        o_ref, acc = rest[len(extra)], rest[len(extra) + 1 :]
        p = _mx(a_ref[...], b_ref[...], kind)
        if nk == 1:
            o_ref[...] = (p.T if out_t else p).astype(out_dtype)
        else:
            assert not out_t
            acc_ref = acc[0]
            k = pl.program_id(2)

            @pl.when(k == 0)
            def _():
                acc_ref[...] = p

            @pl.when(k > 0)
            def _():
                acc_ref[...] += p

            @pl.when(k == nk - 1)
            def _():
                o_ref[...] = acc_ref[...].astype(out_dtype)

    return pl.pallas_call(
        body,
        out_shape=jax.ShapeDtypeStruct((N, M) if out_t else (M, N), out_dtype),
        grid=(M // tm, N // tn, nk),
        in_specs=[a_spec, b_spec] + [pl.BlockSpec((8, LANES), lambda i, j, k: (0, 0))] * len(extra),
        out_specs=pl.BlockSpec((tn, tm), lambda i, j, k: (j, i)) if out_t else pl.BlockSpec((tm, tn), lambda i, j, k: (i, j)),
        scratch_shapes=[] if nk == 1 else [pltpu.VMEM((tm, tn), f32)],
        compiler_params=_cparams(("parallel", "parallel", "arbitrary")),
        name=name,
    )(a, b, *extra)


def _rms_pre(x, w, tb=256):
    T = x.shape[0]

    def body(x_ref, w_ref, h_ref, ht_ref):
        xv = x_ref[...]
        r = lax.rsqrt(jnp.mean(xv * xv, axis=1, keepdims=True) + NORM_EPS)
        h = xv * r * w_ref[...]
        h_ref[...] = h.astype(h_ref.dtype)
        ht_ref[...] = h.T.astype(ht_ref.dtype)

    return pl.pallas_call(
        body,
        out_shape=(jax.ShapeDtypeStruct((T, D_MODEL), MXU_DTYPE), jax.ShapeDtypeStruct((D_MODEL, T), MXU_DTYPE)),
        grid=(T // tb,),
        in_specs=[pl.BlockSpec((tb, D_MODEL), lambda i: (i, 0)), pl.BlockSpec((1, D_MODEL), lambda i: (0, 0))],
        out_specs=(pl.BlockSpec((tb, D_MODEL), lambda i: (i, 0)), pl.BlockSpec((D_MODEL, tb), lambda i: (0, i))),
        compiler_params=_cparams(("parallel",)),
        name="rms_pre",
    )(x, w)


def _rms_pre_bwd(x, w, dh, dy, tb=256):
    T = x.shape[0]

    def body(x_ref, w_ref, dh_ref, dy_ref, dx_ref, dw_ref):
        i = pl.program_id(0)
        xv, dhv = x_ref[...], dh_ref[...]
        r = lax.rsqrt(jnp.mean(xv * xv, axis=1, keepdims=True) + NORM_EPS)
        dhw = dhv * w_ref[...]
        dx_ref[...] = dy_ref[...] + r * dhw - xv * (r * r * r) * jnp.mean(dhw * xv, axis=1, keepdims=True)
        part = jnp.sum(dhv * xv * r, axis=0, keepdims=True)

        @pl.when(i == 0)
        def _():
            dw_ref[...] = part

        @pl.when(i > 0)
        def _():
            dw_ref[...] += part

    blk = pl.BlockSpec((tb, D_MODEL), lambda i: (i, 0))
    row = pl.BlockSpec((1, D_MODEL), lambda i: (0, 0))
    return pl.pallas_call(
        body,
        out_shape=(jax.ShapeDtypeStruct((T, D_MODEL), f32), jax.ShapeDtypeStruct((1, D_MODEL), f32)),
        grid=(T // tb,),
        in_specs=[blk, row, blk, blk],
        out_specs=(blk, row),
        compiler_params=_cparams(("arbitrary",)),
        name="rms_pre_bwd",
    )(x, w, dh, dy)


SHIFTS = (-2, -1, 1, 2)


def _shifted(x):
    T = x.shape[0]
    t = _iota2(x.shape, 0)
    out = {}
    for d in SHIFTS:
        valid = (t < T - d) if d > 0 else (t >= -d)
        out[d] = jnp.where(valid, pltpu.roll(x, (-d) % T, 0), 0.0)
    return out


def _conv5(u, ush, cw):
    acc = u * cw[2:3, :]
    for j in (0, 1, 3, 4):
        acc = acc + ush[j - 2] * cw[j : j + 1, :]
    return acc


def _gdn_prep(proj, conv_w):
    T = proj.shape[0]
    nh = GDN_HEADS

    def body(u_ref, cw_ref, o_ref):
        j = pl.program_id(0)
        u = _mxr(u_ref[...])
        s = _silu(_conv5(u, _shifted(u), _mxr(cw_ref[...])))

        @pl.when(j < 2 * nh)
        def _():
            o_ref[0] = s * lax.rsqrt(jnp.sum(s * s, axis=1, keepdims=True) + NORM_EPS)

        @pl.when(j >= 2 * nh)
        def _():
            o_ref[0] = s

    return pl.pallas_call(
        body,
        out_shape=jax.ShapeDtypeStruct((3, T, GDN_HEADS * GDN_DIM), f32),
        grid=(3 * nh,),
        in_specs=[pl.BlockSpec((T, GDN_DIM), lambda j: (0, j)), pl.BlockSpec((CONV_K, GDN_DIM), lambda j: (0, j))],
        out_specs=pl.BlockSpec((1, T, GDN_DIM), lambda j: (j // nh, 0, j % nh)),
        compiler_params=_cparams(("parallel",)),
        name="gdn_prep",
    )(proj, conv_w)


def _gdn_prep_bwd(proj, conv_w, dqkv_f, dqkv_b, dproj):
    T = proj.shape[0]
    nh = GDN_HEADS


    def body(u_ref, cw_ref, df_ref, db_ref, _, du_ref, dcw_ref):
        j = pl.program_id(0)
        u, cw = _mxr(u_ref[...]), _mxr(cw_ref[...])
        ush = _shifted(u)
        c = _conv5(u, ush, cw)
        s = _silu(c)
        dn = df_ref[0] + db_ref[0]
        rinv = lax.rsqrt(jnp.sum(s * s, axis=1, keepdims=True) + NORM_EPS)
        ds_norm = rinv * dn - s * (rinv * rinv * rinv) * jnp.sum(dn * s, axis=1, keepdims=True)
        ds = jnp.where(j < 2 * nh, ds_norm, dn)
        dc = _mxr(ds * _silu_grad(c))
        dcsh = _shifted(dc)
        du = dc * cw[2:3, :]
        for jj in range(CONV_K):
            d = jj - 2
            dcw_ref[jj : jj + 1, :] = jnp.sum(dc * (u if d == 0 else ush[d]), axis=0, keepdims=True)
            if d != 0:
                du = du + dcsh[-d] * cw[jj : jj + 1, :]
        du_ref[...] = du.astype(du_ref.dtype)

    qspec = pl.BlockSpec((1, T, GDN_DIM), lambda j: (j // nh, 0, j % nh))
    return pl.pallas_call(
        body,
        out_shape=(jax.ShapeDtypeStruct(dproj.shape, dproj.dtype), jax.ShapeDtypeStruct((CONV_K, 3 * nh * GDN_DIM), f32)),
        grid=(3 * nh,),
        in_specs=[pl.BlockSpec((T, GDN_DIM), lambda j: (0, j)), pl.BlockSpec((CONV_K, GDN_DIM), lambda j: (0, j)), qspec, qspec, _ANY_SPEC],
        out_specs=(pl.BlockSpec((T, GDN_DIM), lambda j: (0, j)), pl.BlockSpec((CONV_K, GDN_DIM), lambda j: (0, j))),
        input_output_aliases={4: 0},
        compiler_params=_cparams(("parallel",)),
        name="gdn_prep_bwd",
    )(proj, conv_w, dqkv_f, dqkv_b, dproj)


def _chunk_tri(n, rev):
    i, j = _iota2((n, n), 0), _iota2((n, n), 1)
    same = jnp.right_shift(i, 6) == jnp.right_shift(j, 6)
    order = (j >= i) if rev else (j <= i)
    return jnp.where(jnp.logical_and(same, order), 1.0, 0.0).astype(f32)


def _gdn_gates(proj, avec, dvec, tb=256):
    T = proj.shape[0]

    def body(s_ref, a_ref, d_ref, g_ref, gc_ref):
        small = s_ref[...]
        lane = _iota2(small.shape, 1)
        lg = -jnp.exp(a_ref[...]) * _softplus(small + d_ref[...])
        beta = _sigmoid(small)
        g_ref[...] = jnp.where(lane < 16, lg, jnp.where(lane < 32, beta, 0.0))
        lgm = jnp.where(lane < 16, lg, 0.0)
        gcf = _exact01(_chunk_tri(tb, False), lgm)
        gcr = _exact01(_chunk_tri(tb, True), lgm)
        gc_ref[...] = jnp.where(lane < 8, gcf, gcr)

    blk = pl.BlockSpec((tb, LANES), lambda i: (i, 0))
    row = pl.BlockSpec((1, LANES), lambda i: (0, 0))
    return pl.pallas_call(
        body,
        out_shape=(jax.ShapeDtypeStruct((T, LANES), f32), jax.ShapeDtypeStruct((T, LANES), f32)),
        grid=(T // tb,),
        in_specs=[pl.BlockSpec((tb, LANES), lambda i: (i, SMALL_OFF // LANES)), row, row],
        out_specs=(blk, blk),
        compiler_params=_cparams(("parallel",)),
        name="gdn_gates",
    )(proj, avec, dvec)


def _gdn_gates_bwd(proj, avec, dvec, dg_f, dg_b, dsmall_gla, dproj, tb=256):
    T = proj.shape[0]
    pad = N_CAT - SMALL_OFF

    def body(s_ref, a_ref, d_ref, gf_ref, gb_ref, dl_ref, _, ds_ref, da_ref, dd_ref):
        i = pl.program_id(0)
        small = s_ref[...]
        lane = _iota2(small.shape, 1)
        dgate = gf_ref[...] + gb_ref[...]
        z = small + d_ref[...]
        nega = -jnp.exp(a_ref[...])
        dz = dgate * nega * _sigmoid(z)
        beta = _sigmoid(small)
        dsm = jnp.where(lane < 16, dz, jnp.where(lane < 32, dgate * beta * (1.0 - beta), 0.0))
        ds_ref[:, :LANES] = (dsm + dl_ref[...]).astype(ds_ref.dtype)
        ds_ref[:, LANES:] = jnp.zeros((tb, pad - LANES), ds_ref.dtype)
        lg = nega * _softplus(z)
        pa = jnp.sum(jnp.where(lane < 16, dgate * lg, 0.0), axis=0, keepdims=True)
        pd = jnp.sum(jnp.where(lane < 16, dz, 0.0), axis=0, keepdims=True)

        @pl.when(i == 0)
        def _():
            da_ref[...] = pa
            dd_ref[...] = pd

        @pl.when(i > 0)
        def _():
            da_ref[...] += pa
            dd_ref[...] += pd

    blk = pl.BlockSpec((tb, LANES), lambda i: (i, 0))
    row = pl.BlockSpec((1, LANES), lambda i: (0, 0))
    return pl.pallas_call(
        body,
        out_shape=(jax.ShapeDtypeStruct(dproj.shape, dproj.dtype), jax.ShapeDtypeStruct((1, LANES), f32), jax.ShapeDtypeStruct((1, LANES), f32)),
        grid=(T // tb,),
        in_specs=[pl.BlockSpec((tb, LANES), lambda i: (i, SMALL_OFF // LANES)), row, row, blk, blk, blk, _ANY_SPEC],
        out_specs=(pl.BlockSpec((tb, pad), lambda i: (i, DP_SMALL_BLK)), row, row),
        input_output_aliases={6: 0},
        compiler_params=_cparams(("arbitrary",)),
        name="gdn_gates_bwd",
    )(proj, avec, dvec, dg_f, dg_b, dsmall_gla, dproj)


GDN_HB = 8


def _inv_unit_lower(L, eye):
    A = eye - L
    P2 = _split2(L)
    for _ in range(3):
        P2 = _split2(_h3s(P2, P2, "nn"))
        A = A + _h3s(_split2(A), P2, "nn")
    Ph = P2[0]
    for _ in range(2):
        Ph = _dot16(Ph, Ph, "nn").astype(jnp.bfloat16)
        A = A + _dot16(A.astype(jnp.bfloat16), Ph, "nn")
    return A


def _lane_col(blk, idx):
    lane = _iota2(blk.shape, 1)
    return jnp.sum(jnp.where(lane == idx, blk, 0.0), axis=1, keepdims=True)


class _GdnChunk:
    def __init__(self, q, k, v, gcol, beta, rev, A=None):
        Cn = q.shape[0]
        ii, jj = _iota2((Cn, Cn), 0), _iota2((Cn, Cn), 1)
        self.incl = (ii <= jj) if rev else (ii >= jj)
        self.strict = (ii < jj) if rev else (ii > jj)
        grow = jnp.broadcast_to(gcol, (Cn, LANES)).T[0:1, :]
        diff = gcol - grow
        self.decay = jnp.where(self.incl, jnp.exp(jnp.where(self.incl, diff, 0.0)), 0.0)
        self.k, self.v, self.beta, self.gcol = k, v, beta, gcol
        self.qs = q * (GDN_DIM**-0.5)
        self.kb = k * beta
        self.L = jnp.where(self.strict, _mx(self.kb, k, "nt") * self.decay, 0.0)
        if A is None:
            A = _inv_unit_lower(self.L, jnp.where(ii == jj, 1.0, 0.0).astype(f32))
        self.A2 = _split2(A)
        self.A = A
        self.eg = jnp.exp(gcol)
        self.kbg = self.kb * self.eg
        self.sol = _h3s(self.A2, _split2(jnp.concatenate([v * beta, self.kbg], axis=1)), "nn")
        self.u = self.sol[:, :GDN_DIM]
        self.w = self.sol[:, GDN_DIM:]
        self.attn = jnp.where(self.incl, _mx(self.qs, k, "nt") * self.decay, 0.0)
        self.qd = self.qs * self.eg
        last = 0 if rev else Cn - 1
        self.last = last
        gl = gcol[last : last + 1, :]
        self.ekd = jnp.exp(gl - gcol)
        self.kd = k * self.ekd
        self.egl = jnp.exp(gl)


def _gdn_scan_fwd(qkvn, gates, gc):
    T = qkvn.shape[1]
    N = T // CHUNK
    nh, hb = GDN_HEADS, GDN_HB
    W = hb * GDN_DIM
    chains = [(d, hh) for d in range(2) for hh in range(hb)]

    def body(qf_ref, qb_ref, gf_ref, gb_ref, cf_ref, cb_ref, of_ref, ob_ref, sf_ref, sb_ref, af_ref, ab_ref, s_scr):
        n, b = pl.program_id(0), pl.program_id(1)

        @pl.when(n == 0)
        def _():
            for d, hh in chains:
                s_scr[d, b * hb + hh] = jnp.zeros((GDN_DIM, GDN_DIM), f32)

        dirs = ((qf_ref, gf_ref, cf_ref, of_ref, sf_ref, af_ref), (qb_ref, gb_ref, cb_ref, ob_ref, sb_ref, ab_ref))
        s_in = {ch: s_scr[ch[0], b * hb + ch[1]] for ch in chains}
        s_out = {}
        for d, hh in chains:
            q_ref, g_ref, c_ref, o_ref, st_ref, a_ref = dirs[d]
            h = b * hb + hh
            sl = slice(hh * GDN_DIM, (hh + 1) * GDN_DIM)
            gcol = _lane_col(c_ref[...], h + 8 * d)
            beta = _lane_col(g_ref[...], 16 + h + 8 * d)
            ck = _GdnChunk(q_ref[0, :, sl], q_ref[1, :, sl], q_ref[2, :, sl], gcol, beta, d == 1)
            S = s_in[d, hh]
            st_ref[0, hh] = S
            a_ref[0, hh] = ck.A
            v_new = ck.u - _mx(ck.w, S, "nn")
            o_ref[:, sl] = _mx(ck.qd, S, "nn") + _mx(ck.attn, v_new, "nn")
            s_out[d, hh] = S * ck.egl + _mx(ck.kd, v_new, "tn")
        for d, hh in chains:
            s_scr[d, b * hb + hh] = s_out[d, hh]

    fw, bw = (lambda n: n), (lambda n: N - 1 - n)
    qsp = lambda ci: pl.BlockSpec((3, CHUNK, W), lambda n, b: (0, ci(n), b))
    gsp = lambda ci: pl.BlockSpec((CHUNK, LANES), lambda n, b: (ci(n), 0))
    osp = lambda ci: pl.BlockSpec((CHUNK, W), lambda n, b: (ci(n), b))
    ssp = lambda ci: pl.BlockSpec((1, hb, GDN_DIM, GDN_DIM), lambda n, b: (ci(n), b, 0, 0))
    asp = lambda ci: pl.BlockSpec((1, hb, CHUNK, CHUNK), lambda n, b: (ci(n), b, 0, 0))
    o_shape = jax.ShapeDtypeStruct((T, nh * GDN_DIM), f32)
    s_shape = jax.ShapeDtypeStruct((N, nh, GDN_DIM, GDN_DIM), f32)
    a_shape = jax.ShapeDtypeStruct((N, nh, CHUNK, CHUNK), f32)
    return pl.pallas_call(
        body,
        out_shape=(o_shape, o_shape, s_shape, s_shape, a_shape, a_shape),
        grid=(N, nh // hb),
        in_specs=[qsp(fw), qsp(bw), gsp(fw), gsp(bw), gsp(fw), gsp(bw)],
        out_specs=(osp(fw), osp(bw), ssp(fw), ssp(bw), asp(fw), asp(bw)),
        scratch_shapes=[pltpu.VMEM((2, nh, GDN_DIM, GDN_DIM), f32)],
        compiler_params=_cparams(("arbitrary", "arbitrary")),
        name="gdn_scan_fwd",
    )(qkvn, qkvn, gates, gates, gc, gc)


def _gdn_scan_bwd(qkvn, gates, gc, st_f, st_b, a_f, a_b, do):
    T = qkvn.shape[1]
    N = T // CHUNK
    nh, hb = GDN_HEADS, GDN_HB
    nb = nh // hb
    W = hb * GDN_DIM
    chains = [(d, hh) for d in range(2) for hh in range(hb)]

    def body(qf_ref, qb_ref, gf_ref, gb_ref, cf_ref, cb_ref, sf_ref, sb_ref, af_ref, ab_ref, dof_ref, dob_ref,
             dqf_ref, dqb_ref, dgf_ref, dgb_ref, ds_scr):
        n, b = pl.program_id(0), pl.program_id(1)

        @pl.when(n == 0)
        def _():
            for d, hh in chains:
                ds_scr[d, b * hb + hh] = jnp.zeros((GDN_DIM, GDN_DIM), f32)

        @pl.when(b == 0)
        def _():
            dgf_ref[...] = jnp.zeros((CHUNK, LANES), f32)
            dgb_ref[...] = jnp.zeros((CHUNK, LANES), f32)

        dirs = ((qf_ref, gf_ref, cf_ref, sf_ref, af_ref, dof_ref, dqf_ref), (qb_ref, gb_ref, cb_ref, sb_ref, ab_ref, dob_ref, dqb_ref))
        ds_in = {ch: ds_scr[ch[0], b * hb + ch[1]] for ch in chains}
        ds_out = {}
        lane = _iota2((CHUNK, LANES), 1)
        acc = [dgf_ref[...], dgb_ref[...]]
        for d, hh in chains:
            q_ref, g_ref, c_ref, st_ref, a_ref, do_ref, dq_ref = dirs[d]
            h = b * hb + hh
            sl = slice(hh * GDN_DIM, (hh + 1) * GDN_DIM)
            gcol = _lane_col(c_ref[...], h + 8 * d)
            beta = _lane_col(g_ref[...], 16 + h + 8 * d)
            ck = _GdnChunk(q_ref[0, :, sl], q_ref[1, :, sl], q_ref[2, :, sl], gcol, beta, d == 1, A=a_ref[0, hh])
            k, v = ck.k, ck.v
            S = st_ref[0, hh]
            dS = ds_in[d, hh]
            dov = do_ref[:, sl]
            v_new = ck.u - _mx(ck.w, S, "nn")
            d_vnew = _mx(ck.attn, dov, "tn") + _mx(ck.kd, dS, "nn")
            d_attn = jnp.where(ck.incl, _mx(dov, v_new, "nt"), 0.0)
            d_qd = _mx(dov, S, "nt")
            d_kd = _mx(v_new, dS, "nt")
            d_gl = jnp.sum(jnp.sum(dS * S, axis=1, keepdims=True), axis=0, keepdims=True) * ck.egl
            ds_out[d, hh] = dS * ck.egl + _mx(ck.qd, dov, "tn") - _mx(ck.w, d_vnew, "tn")
            d_w = -_mx(d_vnew, S, "nt")
            d_rhs = _h3s(ck.A2, _split2(jnp.concatenate([d_vnew, d_w], axis=1)), "tn")
            d_vb, d_kbg = d_rhs[:, :GDN_DIM], d_rhs[:, GDN_DIM:]
            dL = -jnp.where(ck.strict, _h3(d_rhs, ck.sol, "nt"), 0.0)
            P = dL * ck.decay
            Q = d_attn * ck.decay
            d_kb = _mx(P, k, "nn") + d_kbg * ck.eg
            d_qs = _mx(Q, k, "nn") + d_qd * ck.eg
            d_k = _mx(P, ck.kb, "tn") + _mx(Q, ck.qs, "tn") + d_kd * ck.ekd + d_kb * ck.beta
            E = dL * ck.L + d_attn * ck.attn
            kdsum = jnp.sum(d_kd * ck.kd, axis=1, keepdims=True)
            d_gc = (jnp.sum(E, axis=1, keepdims=True) - _colsum_col(E)
                    + jnp.sum(d_kbg * ck.kbg, axis=1, keepdims=True)
                    + jnp.sum(d_qd * ck.qd, axis=1, keepdims=True) - kdsum)
            d_gl = d_gl + jnp.sum(kdsum, axis=0, keepdims=True)
            row = _iota2((CHUNK, 1), 0)
            d_gc = d_gc + jnp.where(row == ck.last, d_gl, 0.0)
            d_beta = jnp.sum(d_vb * v, axis=1, keepdims=True) + jnp.sum(d_kb * k, axis=1, keepdims=True)
            dq_ref[0, :, sl] = d_qs * (GDN_DIM**-0.5)
            dq_ref[1, :, sl] = d_k
            dq_ref[2, :, sl] = d_vb * ck.beta
            acc[d] = jnp.where(lane == h + 8 * d, d_gc, acc[d])
            acc[d] = jnp.where(lane == 16 + h + 8 * d, d_beta, acc[d])
        for d, hh in chains:
            ds_scr[d, b * hb + hh] = ds_out[d, hh]
        dgf_ref[...] = acc[0]
        dgb_ref[...] = acc[1]

        @pl.when(b == nb - 1)
        def _():
            for d, dg_ref in enumerate((dgf_ref, dgb_ref)):
                blk = dg_ref[...]
                dg_ref[...] = jnp.where(lane < 16, _exact01(_chunk_tri(CHUNK, d == 0), blk), blk)

    fw, bw = (lambda n: N - 1 - n), (lambda n: n)
    qsp = lambda ci: pl.BlockSpec((3, CHUNK, W), lambda n, b: (0, ci(n), b))
    gsp = lambda ci: pl.BlockSpec((CHUNK, LANES), lambda n, b: (ci(n), 0))
    osp = lambda ci: pl.BlockSpec((CHUNK, W), lambda n, b: (ci(n), b))
    ssp = lambda ci: pl.BlockSpec((1, hb, GDN_DIM, GDN_DIM), lambda n, b: (ci(n), b, 0, 0))
    asp = lambda ci: pl.BlockSpec((1, hb, CHUNK, CHUNK), lambda n, b: (ci(n), b, 0, 0))
    dq_shape = jax.ShapeDtypeStruct((3, T, nh * GDN_DIM), f32)
    dg_shape = jax.ShapeDtypeStruct((T, LANES), f32)
    return pl.pallas_call(
        body,
        out_shape=(dq_shape, dq_shape, dg_shape, dg_shape),
        grid=(N, nb),
        in_specs=[qsp(fw), qsp(bw), gsp(fw), gsp(bw), gsp(fw), gsp(bw), ssp(fw), ssp(bw), asp(fw), asp(bw), osp(fw), osp(bw)],
        out_specs=(qsp(fw), qsp(bw), gsp(fw), gsp(bw)),
        scratch_shapes=[pltpu.VMEM((2, nh, GDN_DIM, GDN_DIM), f32)],
        compiler_params=_cparams(("arbitrary", "arbitrary")),
        name="gdn_scan_bwd",
    )(qkvn, qkvn, gates, gates, gc, gc, st_f, st_b, a_f, a_b, do, do)


def _out_norm(o_f, o_b, proj, gate_blk, w, hd, name, tb=256):
    T, W = o_f.shape
    nh = W // hd

    def body(of_ref, ob_ref, z_ref, w_ref, y_ref):
        wv = w_ref[...]
        for i in range(nh):
            sl = slice(i * hd, (i + 1) * hd)
            o = of_ref[:, sl] + ob_ref[:, sl]
            r = lax.rsqrt(jnp.mean(o * o, axis=1, keepdims=True) + NORM_EPS)
            y_ref[:, sl] = (o * r * wv * _silu(z_ref[:, sl])).astype(y_ref.dtype)

    blk = pl.BlockSpec((tb, W), lambda i: (i, 0))
    return pl.pallas_call(
        body,
        out_shape=jax.ShapeDtypeStruct((T, W), MXU_DTYPE),
        grid=(T // tb,),
        in_specs=[blk, blk, pl.BlockSpec((tb, W), lambda i: (i, gate_blk)), pl.BlockSpec((1, hd), lambda i: (0, 0))],
        out_specs=blk,
        compiler_params=_cparams(("parallel",)),
        name=name,
    )(o_f, o_b, proj, w)


def _out_norm_bwd(dy, o_f, o_b, proj, gate_blk, w, hd, name, dproj, tb=256):
    T, W = o_f.shape
    nh = W // hd

    def body(dy_ref, of_ref, ob_ref, z_ref, w_ref, _, do_ref, dz_ref, dw_ref):
        i = pl.program_id(0)
        wv = w_ref[...]
        dw = jnp.zeros((1, hd), f32)
        for a in range(nh):
            sl = slice(a * hd, (a + 1) * hd)
            o = of_ref[:, sl] + ob_ref[:, sl]
            z = z_ref[:, sl]
            dyv = dy_ref[:, sl]
            r = lax.rsqrt(jnp.mean(o * o, axis=1, keepdims=True) + NORM_EPS)
            orr = o * r
            dn = dyv * _silu(z)
            dz_ref[:, sl] = (dyv * orr * wv * _silu_grad(z)).astype(dz_ref.dtype)
            dw = dw + jnp.sum(dn * orr, axis=0, keepdims=True)
            dnw = dn * wv
            do_ref[:, sl] = r * dnw - o * (r * r * r) * jnp.mean(dnw * o, axis=1, keepdims=True)

        @pl.when(i == 0)
        def _():
            dw_ref[...] = dw

        @pl.when(i > 0)
        def _():
            dw_ref[...] += dw

    blk = pl.BlockSpec((tb, W), lambda i: (i, 0))
    row = pl.BlockSpec((1, hd), lambda i: (0, 0))
    return pl.pallas_call(
        body,
        out_shape=(jax.ShapeDtypeStruct((T, W), f32), jax.ShapeDtypeStruct(dproj.shape, dproj.dtype), jax.ShapeDtypeStruct((1, hd), f32)),
        grid=(T // tb,),
        in_specs=[blk, blk, blk, pl.BlockSpec((tb, W), lambda i: (i, gate_blk)), row, _ANY_SPEC],
        out_specs=(blk, pl.BlockSpec((tb, W), lambda i: (i, gate_blk)), row),
        input_output_aliases={5: 1},
        compiler_params=_cparams(("arbitrary",)),
        name=name,
    )(dy, o_f, o_b, proj, w, dproj)


def _gla_prep(proj, w2f, w2b, b2f, b2b, tb=256):
    T = proj.shape[0]
    W = GLA_HEADS * GLA_DK

    def body(s_ref, wf_ref, wb_ref, bf_ref, bb_ref, gf_ref, gb_ref):
        small = s_ref[...]
        gkf = -_softplus(-(_mx(small, wf_ref[...], "nn") + bf_ref[...])) * (1.0 / GATE_NORMALIZER)
        gkb = -_softplus(-(_mx(small, wb_ref[...], "nn") + bb_ref[...])) * (1.0 / GATE_NORMALIZER)
        gf_ref[...] = _exact01(_chunk_tri(tb, False), gkf)
        gb_ref[...] = _exact01(_chunk_tri(tb, True), gkb)

    blk = pl.BlockSpec((tb, W), lambda i: (i, 0))
    wsp = pl.BlockSpec((LANES, W), lambda i: (0, 0))
    row = pl.BlockSpec((1, W), lambda i: (0, 0))
    return pl.pallas_call(
        body,
        out_shape=(jax.ShapeDtypeStruct((T, W), f32), jax.ShapeDtypeStruct((T, W), f32)),
        grid=(T // tb,),
        in_specs=[pl.BlockSpec((tb, LANES), lambda i: (i, SMALL_OFF // LANES)), wsp, wsp, row, row],
        out_specs=(blk, blk),
        compiler_params=_cparams(("parallel",)),
        name="gla_prep",
    )(proj, w2f, w2b, b2f, b2b)


def _gla_prep_bwd(proj, w2f, w2b, b2f, b2b, dgk_f, dgk_b, tb=256):
    T = proj.shape[0]
    W = GLA_HEADS * GLA_DK

    def body(s_ref, wf_ref, wb_ref, bf_ref, bb_ref, df_ref, db_ref, ds_ref, dwf_ref, dwb_ref, dbf_ref, dbb_ref):
        i = pl.program_id(0)
        small = s_ref[...]
        dsm = jnp.zeros((tb, LANES), f32)
        parts = []
        for w_ref, b_ref, d_ref in ((wf_ref, bf_ref, df_ref), (wb_ref, bb_ref, db_ref)):
            pre = _mx(small, w_ref[...], "nn") + b_ref[...]
            dpre = d_ref[...] * (1.0 / GATE_NORMALIZER) * _sigmoid(-pre)
            dsm = dsm + _mx(dpre, w_ref[...], "nt")
            parts.append((_mx(small, dpre, "tn"), jnp.sum(dpre, axis=0, keepdims=True)))
        ds_ref[...] = dsm

        @pl.when(i == 0)
        def _():
            dwf_ref[...], dbf_ref[...] = parts[0]
            dwb_ref[...], dbb_ref[...] = parts[1]

        @pl.when(i > 0)
        def _():
            dwf_ref[...] += parts[0][0]
            dbf_ref[...] += parts[0][1]
            dwb_ref[...] += parts[1][0]
            dbb_ref[...] += parts[1][1]

    blk = pl.BlockSpec((tb, W), lambda i: (i, 0))
    wsp = pl.BlockSpec((LANES, W), lambda i: (0, 0))
    row = pl.BlockSpec((1, W), lambda i: (0, 0))
    return pl.pallas_call(
        body,
        out_shape=(jax.ShapeDtypeStruct((T, LANES), f32), jax.ShapeDtypeStruct((LANES, W), f32), jax.ShapeDtypeStruct((LANES, W), f32),
                   jax.ShapeDtypeStruct((1, W), f32), jax.ShapeDtypeStruct((1, W), f32)),
        grid=(T // tb,),
        in_specs=[pl.BlockSpec((tb, LANES), lambda i: (i, SMALL_OFF // LANES)), wsp, wsp, row, row, blk, blk],
        out_specs=(pl.BlockSpec((tb, LANES), lambda i: (i, 0)), wsp, wsp, row, row),
        compiler_params=_cparams(("arbitrary",)),
        name="gla_prep_bwd",
    )(proj, w2f, w2b, b2f, b2b, dgk_f, dgk_b)


Q_B_BLK, K_B_BLK, V_B_BLK = 4096 // GLA_DK, 4608 // GLA_DK, 5120 // GLA_DV


class _GlaChunk:
    def __init__(self, q, k, G, rev):
        Cn = q.shape[0]
        ii, jj = _iota2((Cn, Cn), 0), _iota2((Cn, Cn), 1)
        self.incl = (ii <= jj) if rev else (ii >= jj)
        self.eG = jnp.exp(G)
        self.enG = jnp.exp(-G)
        self.qg = q * (GLA_DK**-0.5) * self.eG
        self.kg = k * self.enG
        self.attn = jnp.where(self.incl, _mx(self.qg, self.kg, "nt"), 0.0)
        last = 0 if rev else Cn - 1
        self.last = last
        gl = G[last : last + 1, :]
        self.ekd = jnp.exp(gl - G)
        self.kd = k * self.ekd
        self.egl = jnp.exp(gl)


def _gla_scan_fwd(proj, G_f, G_b):
    T = proj.shape[0]
    N = T // CHUNK
    nh = GLA_HEADS

    def body(qf_ref, kf_ref, vf_ref, gf_ref, qb_ref, kb_ref, vb_ref, gb_ref, of_ref, ob_ref, sf_ref, sb_ref, s_scr):
        n, h = pl.program_id(0), pl.program_id(1)

        @pl.when(n == 0)
        def _():
            s_scr[0, h] = jnp.zeros((GLA_DV, GLA_DK), f32)
            s_scr[1, h] = jnp.zeros((GLA_DV, GLA_DK), f32)

        for d, (q_ref, k_ref, v_ref, g_ref, o_ref, st_ref) in enumerate(
            ((qf_ref, kf_ref, vf_ref, gf_ref, of_ref, sf_ref), (qb_ref, kb_ref, vb_ref, gb_ref, ob_ref, sb_ref))
        ):
            ck = _GlaChunk(q_ref[...], k_ref[...], g_ref[...], d == 1)
            v = v_ref[...]
            St = s_scr[d, h]
            st_ref[0, 0] = St
            o_ref[...] = _mx(ck.qg, St, "nt") + _mx(ck.attn, v, "nn")
            s_scr[d, h] = St * ck.egl + _mx(v, ck.kd, "tn")

    def specs(cidx):
        return [
            pl.BlockSpec((CHUNK, GLA_DK), lambda n, h: (cidx(n), Q_B_BLK + h)),
            pl.BlockSpec((CHUNK, GLA_DK), lambda n, h: (cidx(n), K_B_BLK + h)),
            pl.BlockSpec((CHUNK, GLA_DV), lambda n, h: (cidx(n), V_B_BLK + h)),
            pl.BlockSpec((CHUNK, GLA_DK), lambda n, h: (cidx(n), h)),
        ]

    fw, bw = (lambda n: n), (lambda n: N - 1 - n)
    o_shape = jax.ShapeDtypeStruct((T, nh * GLA_DV), f32)
    s_shape = jax.ShapeDtypeStruct((N, nh, GLA_DV, GLA_DK), f32)
    return pl.pallas_call(
        body,
        out_shape=(o_shape, o_shape, s_shape, s_shape),
        grid=(N, nh),
        in_specs=specs(fw) + specs(bw),
        out_specs=(
            pl.BlockSpec((CHUNK, GLA_DV), lambda n, h: (n, h)),
            pl.BlockSpec((CHUNK, GLA_DV), lambda n, h: (N - 1 - n, h)),
            pl.BlockSpec((1, 1, GLA_DV, GLA_DK), lambda n, h: (n, h, 0, 0)),
            pl.BlockSpec((1, 1, GLA_DV, GLA_DK), lambda n, h: (N - 1 - n, h, 0, 0)),
        ),
        scratch_shapes=[pltpu.VMEM((2, nh, GLA_DV, GLA_DK), f32)],
        compiler_params=_cparams(("arbitrary", "arbitrary")),
        name="gla_scan_fwd",
    )(proj, proj, proj, G_f, proj, proj, proj, G_b)


def _gla_scan_bwd(proj, G_f, G_b, st_f, st_b, do):
    T = proj.shape[0]
    N = T // CHUNK
    nh = GLA_HEADS

    def body(qf_ref, kf_ref, vf_ref, gf_ref, sf_ref, dof_ref, qb_ref, kb_ref, vb_ref, gb_ref, sb_ref, dob_ref,
             dqf_ref, dkf_ref, dvf_ref, dgf_ref, dqb_ref, dkb_ref, dvb_ref, dgb_ref, ds_scr):
        n, h = pl.program_id(0), pl.program_id(1)

        @pl.when(n == 0)
        def _():
            ds_scr[0, h] = jnp.zeros((GLA_DV, GLA_DK), f32)
            ds_scr[1, h] = jnp.zeros((GLA_DV, GLA_DK), f32)

        for d, (q_ref, k_ref, v_ref, g_ref, st_ref, do_ref, dq_ref, dk_ref, dv_ref, dg_ref) in enumerate(
            ((qf_ref, kf_ref, vf_ref, gf_ref, sf_ref, dof_ref, dqf_ref, dkf_ref, dvf_ref, dgf_ref),
             (qb_ref, kb_ref, vb_ref, gb_ref, sb_ref, dob_ref, dqb_ref, dkb_ref, dvb_ref, dgb_ref))
        ):
            rev = d == 1
            ck = _GlaChunk(q_ref[...], k_ref[...], g_ref[...], rev)
            v = v_ref[...]
            St = st_ref[0, 0]
            dSt = ds_scr[d, h]
            dov = do_ref[...]
            d_attn = jnp.where(ck.incl, _mx(dov, v, "nt"), 0.0)
            d_qg = _mx(dov, St, "nn") + _mx(d_attn, ck.kg, "nn")
            d_kg = _mx(d_attn, ck.qg, "tn")
            dv_ref[...] = _mx(ck.attn, dov, "tn") + _mx(ck.kd, dSt, "nt")
            d_kd = _mx(v, dSt, "nn")
            ds_scr[d, h] = dSt * ck.egl + _mx(dov, ck.qg, "tn")
            kdd = d_kd * ck.kd
            d_gl = jnp.sum(dSt * St, axis=0, keepdims=True) * ck.egl + jnp.sum(kdd, axis=0, keepdims=True)
            dq_ref[...] = d_qg * ck.eG * (GLA_DK**-0.5)
            dk_ref[...] = d_kg * ck.enG + d_kd * ck.ekd
            d_G = d_qg * ck.qg - d_kg * ck.kg - kdd
            row = _iota2((CHUNK, GLA_DK), 0)
            d_G = d_G + jnp.where(row == ck.last, d_gl, 0.0)
            dg_ref[...] = _exact01(_chunk_tri(CHUNK, not rev), d_G)

    def in_specs(cidx):
        return [
            pl.BlockSpec((CHUNK, GLA_DK), lambda n, h: (cidx(n), Q_B_BLK + h)),
            pl.BlockSpec((CHUNK, GLA_DK), lambda n, h: (cidx(n), K_B_BLK + h)),
            pl.BlockSpec((CHUNK, GLA_DV), lambda n, h: (cidx(n), V_B_BLK + h)),
            pl.BlockSpec((CHUNK, GLA_DK), lambda n, h: (cidx(n), h)),
            pl.BlockSpec((1, 1, GLA_DV, GLA_DK), lambda n, h: (cidx(n), h, 0, 0)),
            pl.BlockSpec((CHUNK, GLA_DV), lambda n, h: (cidx(n), h)),
        ]

    def out_specs(cidx):
        return [
            pl.BlockSpec((CHUNK, GLA_DK), lambda n, h: (cidx(n), h)),
            pl.BlockSpec((CHUNK, GLA_DK), lambda n, h: (cidx(n), h)),
            pl.BlockSpec((CHUNK, GLA_DV), lambda n, h: (cidx(n), h)),
            pl.BlockSpec((CHUNK, GLA_DK), lambda n, h: (cidx(n), h)),
        ]

    fw, bw = (lambda n: N - 1 - n), (lambda n: n)
    k_shape = jax.ShapeDtypeStruct((T, nh * GLA_DK), f32)
    v_shape = jax.ShapeDtypeStruct((T, nh * GLA_DV), f32)
    return pl.pallas_call(
        body,
        out_shape=(k_shape, k_shape, v_shape, k_shape) * 2,
        grid=(N, nh),
        in_specs=in_specs(fw) + in_specs(bw),
        out_specs=tuple(out_specs(fw) + out_specs(bw)),
        scratch_shapes=[pltpu.VMEM((2, nh, GLA_DV, GLA_DK), f32)],
        compiler_params=_cparams(("arbitrary", "arbitrary")),
        name="gla_scan_bwd",
    )(proj, proj, proj, G_f, st_f, do, proj, proj, proj, G_b, st_b, do)


GATE_A_BLK, GATE_B_BLK, Z_BLK, G_B_BLK = 6144 // 1024, 7168 // 1024, 3072 // 1024, 8192 // 1024
DP_GATES_BLK, DP_GLA_QKV_BLK, DP_SMALL_BLK = 6144 // 2048, 4096 // 2048, SMALL_OFF // 512
_ANY_SPEC = pl.BlockSpec(memory_space=pl.ANY)


def _mid(og, obg, proj, x, target, wpg, wpl, wout, w_post, tb=256):
    T = x.shape[0]
    Dm = D_MODEL

    def body(og_ref, obg_ref, ga_ref, gb_ref, x_ref, t_ref, wpg_ref, wpl_ref, wo_ref, wp_ref,
             loss_ref, dwp_ref, dy_ref, m_ref, dout_ref, dya_ref, dyb_ref, dgates_ref, dog_ref, dobg_ref):
        i = pl.program_id(0)
        ya = _mx(og_ref[...], wpg_ref[...], "nn")
        yb = _mx(obg_ref[...], wpl_ref[...], "nn")
        sa, sb = _sigmoid(ga_ref[...]), _sigmoid(gb_ref[...])
        m = sa * ya + sb * yb
        m_ref[...] = m.astype(m_ref.dtype)
        out = _mx(m, wo_ref[...], "nn")
        r = lax.rsqrt(jnp.mean(out * out, axis=1, keepdims=True) + NORM_EPS)
        wp = wp_ref[...]
        e = x_ref[...] + out * r * wp - t_ref[...]
        dy = e * (1.0 / Dm)
        dy_ref[...] = dy
        lpart = jnp.sum(e * e, axis=0, keepdims=True)
        wpart = jnp.sum(dy * out * r, axis=0, keepdims=True)
        dyn = dy * wp
        dout = r * dyn - out * (r * r * r) * jnp.mean(dyn * out, axis=1, keepdims=True)
        dout_ref[...] = dout.astype(dout_ref.dtype)
        dm = _mx(dout, wo_ref[...], "nt")
        dya, dyb = dm * sa, dm * sb
        dya_ref[...] = dya.astype(dya_ref.dtype)
        dyb_ref[...] = dyb.astype(dyb_ref.dtype)
        dgates_ref[:, :Dm] = (dm * ya * sa * (1.0 - sa)).astype(dgates_ref.dtype)
        dgates_ref[:, Dm:] = (dm * yb * sb * (1.0 - sb)).astype(dgates_ref.dtype)
        dog_ref[...] = _mx(dya, wpg_ref[...], "nt")
        dobg_ref[...] = _mx(dyb, wpl_ref[...], "nt")

        @pl.when(i == 0)
        def _():
            loss_ref[...] = lpart
            dwp_ref[...] = wpart

        @pl.when(i > 0)
        def _():
            loss_ref[...] += lpart
            dwp_ref[...] += wpart

    blk = pl.BlockSpec((tb, Dm), lambda i: (i, 0))
    row = pl.BlockSpec((1, Dm), lambda i: (0, 0))
    wsp = pl.BlockSpec((Dm, Dm), lambda i: (0, 0))
    act = jax.ShapeDtypeStruct((T, Dm), MXU_DTYPE)
    big = jax.ShapeDtypeStruct((T, Dm), f32)
    vec = jax.ShapeDtypeStruct((1, Dm), f32)
    return pl.pallas_call(
        body,
        out_shape=(vec, vec, big, act, act, act, act, jax.ShapeDtypeStruct((T, N_CAT), MXU_DTYPE), big, big),
        grid=(T // tb,),
        in_specs=[blk, blk, pl.BlockSpec((tb, Dm), lambda i: (i, GATE_A_BLK)), pl.BlockSpec((tb, Dm), lambda i: (i, GATE_B_BLK)),
                  blk, blk, wsp, wsp, wsp, row],
        out_specs=(row, row) + (blk,) * 5 + (pl.BlockSpec((tb, 2 * Dm), lambda i: (i, DP_GATES_BLK)), blk, blk),
        compiler_params=_cparams(("arbitrary",)),
        name="mid",
    )(og, obg, proj, proj, x, target, wpg, wpl, wout, w_post)


def _heads(ref, lead, n, width):
    return [ref[lead + (slice(None), slice(i * width, (i + 1) * width))] for i in range(n)]


class _GdnBatch:
    def __init__(self, q, k, v, gcol, grow, beta, nf, A=None):
        B, Cn = q.shape[0], q.shape[1]
        self.nf = nf
        self.incl, self.strict, eye = _dir_masks(B, nf, Cn)
        self.decay = jnp.where(self.incl, jnp.exp(jnp.where(self.incl, gcol - grow, 0.0)), 0.0)
        self.k, self.v, self.beta = k, v, beta
        self.qs = q * (GDN_DIM**-0.5)
        self.kb = k * beta
        self.L = jnp.where(self.strict, _mx(self.kb, k, "nt") * self.decay, 0.0)
        if A is None:
            A = _inv_unit_lower(self.L, jnp.where(eye, 1.0, 0.0).astype(f32))
        self.A = A
        self.A2 = _split2(A)
        self.eg = jnp.exp(gcol)
        self.kbg = self.kb * self.eg
        self.sol = _h3s(self.A2, _split2(jnp.concatenate([v * beta, self.kbg], axis=2)), "nn")
        self.u = self.sol[:, :, :GDN_DIM]
        self.w = self.sol[:, :, GDN_DIM:]
        self.attn = jnp.where(self.incl, _mx(self.qs, k, "nt") * self.decay, 0.0)
        self.qd = self.qs * self.eg
        gl = _last_row(gcol, nf)
        self.ekd = jnp.exp(gl - gcol)
        self.kd = k * self.ekd
        self.egl = jnp.exp(gl)


def _gdn_chain_inputs(qf_ref, qb_ref, gf_ref, gb_ref, cf_ref, cb_ref, b, hb):
    qkv = [jnp.stack(_heads(qf_ref, (i,), hb, GDN_DIM) + _heads(qb_ref, (i,), hb, GDN_DIM)) for i in range(3)]
    cols, rows, betas = [], [], []
    for d, (g_ref, c_ref) in enumerate(((gf_ref, cf_ref), (gb_ref, cb_ref))):
        gates, gc = g_ref[...], c_ref[...]
        for hh in range(hb):
            h = b * hb + hh
            col = _lane_col(gc, h + 8 * d)
            cols.append(col)
            rows.append(jnp.broadcast_to(col, (CHUNK, LANES)).T[0:1, :])
            betas.append(_lane_col(gates, 16 + h + 8 * d))
    return qkv[0], qkv[1], qkv[2], jnp.stack(cols), jnp.stack(rows), jnp.stack(betas)


def _gdn_fwd(qkvn, gates, gc):
    T = qkvn.shape[1]
    N = T // CHUNK
    nh, hb = GDN_HEADS, GDN_HB
    W = hb * GDN_DIM

    def body(qf_ref, qb_ref, gf_ref, gb_ref, cf_ref, cb_ref, of_ref, ob_ref, sf_ref, sb_ref, af_ref, ab_ref, s_scr):
        n, b = pl.program_id(0), pl.program_id(1)
        hs = pl.ds(b * hb, hb)

        @pl.when(n == 0)
        def _():
            s_scr[0, hs] = jnp.zeros((hb, GDN_DIM, GDN_DIM), f32)
            s_scr[1, hs] = jnp.zeros((hb, GDN_DIM, GDN_DIM), f32)

        q, k, v, gcol, grow, beta = _gdn_chain_inputs(qf_ref, qb_ref, gf_ref, gb_ref, cf_ref, cb_ref, b, hb)
        ck = _GdnBatch(q, k, v, gcol, grow, beta, hb)
        S = jnp.concatenate([s_scr[0, hs], s_scr[1, hs]], axis=0)
        sf_ref[0], sb_ref[0] = S[:hb], S[hb:]
        af_ref[0], ab_ref[0] = ck.A[:hb], ck.A[hb:]
        v_new = ck.u - _mx(ck.w, S, "nn")
        o = _mx(ck.qd, S, "nn") + _mx(ck.attn, v_new, "nn")
        S_new = S * ck.egl + _mx(ck.kd, v_new, "tn")
        for hh in range(hb):
            sl = slice(hh * GDN_DIM, (hh + 1) * GDN_DIM)
            of_ref[:, sl] = o[hh]
            ob_ref[:, sl] = o[hb + hh]
        s_scr[0, hs] = S_new[:hb]
        s_scr[1, hs] = S_new[hb:]

    fw, bw = (lambda n: n), (lambda n: N - 1 - n)
    qsp = lambda ci: pl.BlockSpec((3, CHUNK, W), lambda n, b: (0, ci(n), b))
    gsp = lambda ci: pl.BlockSpec((CHUNK, LANES), lambda n, b: (ci(n), 0))
    osp = lambda ci: pl.BlockSpec((CHUNK, W), lambda n, b: (ci(n), b))
    ssp = lambda ci: pl.BlockSpec((1, hb, GDN_DIM, GDN_DIM), lambda n, b: (ci(n), b, 0, 0))
    asp = lambda ci: pl.BlockSpec((1, hb, CHUNK, CHUNK), lambda n, b: (ci(n), b, 0, 0))
    o_shape = jax.ShapeDtypeStruct((T, nh * GDN_DIM), f32)
    s_shape = jax.ShapeDtypeStruct((N, nh, GDN_DIM, GDN_DIM), f32)
    a_shape = jax.ShapeDtypeStruct((N, nh, CHUNK, CHUNK), f32)
    return pl.pallas_call(
        body,
        out_shape=(o_shape, o_shape, s_shape, s_shape, a_shape, a_shape),
        grid=(N, nh // hb),
        in_specs=[qsp(fw), qsp(bw), gsp(fw), gsp(bw), gsp(fw), gsp(bw)],
        out_specs=(osp(fw), osp(bw), ssp(fw), ssp(bw), asp(fw), asp(bw)),
        scratch_shapes=[pltpu.VMEM((2, nh, GDN_DIM, GDN_DIM), f32)],
        compiler_params=_cparams(("arbitrary", "arbitrary")),
        name="gdn_scan_fwd",
    )(qkvn, qkvn, gates, gates, gc, gc)


def _gdn_bwd(qkvn, gates, gc, st_f, st_b, a_f, a_b, do):
    T = qkvn.shape[1]
    N = T // CHUNK
    nh, hb = GDN_HEADS, GDN_HB
    nb = nh // hb
    W = hb * GDN_DIM

    def body(qf_ref, qb_ref, gf_ref, gb_ref, cf_ref, cb_ref, sf_ref, sb_ref, af_ref, ab_ref, dof_ref, dob_ref,
             dqf_ref, dqb_ref, dgf_ref, dgb_ref, ds_scr):
        n, b = pl.program_id(0), pl.program_id(1)
        hs = pl.ds(b * hb, hb)

        @pl.when(n == 0)
        def _():
            ds_scr[0, hs] = jnp.zeros((hb, GDN_DIM, GDN_DIM), f32)
            ds_scr[1, hs] = jnp.zeros((hb, GDN_DIM, GDN_DIM), f32)

        @pl.when(b == 0)
        def _():
            dgf_ref[...] = jnp.zeros((CHUNK, LANES), f32)
            dgb_ref[...] = jnp.zeros((CHUNK, LANES), f32)

        q, k, v, gcol, grow, beta = _gdn_chain_inputs(qf_ref, qb_ref, gf_ref, gb_ref, cf_ref, cb_ref, b, hb)
        ck = _GdnBatch(q, k, v, gcol, grow, beta, hb, A=jnp.concatenate([af_ref[0], ab_ref[0]], axis=0))
        S = jnp.concatenate([sf_ref[0], sb_ref[0]], axis=0)
        dS = jnp.concatenate([ds_scr[0, hs], ds_scr[1, hs]], axis=0)
        dov = jnp.stack(_heads(dof_ref, (), hb, GDN_DIM) + _heads(dob_ref, (), hb, GDN_DIM))
        v_new = ck.u - _mx(ck.w, S, "nn")
        d_vnew = _mx(ck.attn, dov, "tn") + _mx(ck.kd, dS, "nn")
        d_attn = jnp.where(ck.incl, _mx(dov, v_new, "nt"), 0.0)
        d_qd = _mx(dov, S, "nt")
        d_kd = _mx(v_new, dS, "nt")
        d_gl = jnp.sum(jnp.sum(dS * S, axis=2, keepdims=True), axis=1, keepdims=True) * ck.egl
        dS_new = dS * ck.egl + _mx(ck.qd, dov, "tn") - _mx(ck.w, d_vnew, "tn")
        ds_scr[0, hs] = dS_new[:hb]
        ds_scr[1, hs] = dS_new[hb:]
        d_w = -_mx(d_vnew, S, "nt")
        d_rhs = _h3s(ck.A2, _split2(jnp.concatenate([d_vnew, d_w], axis=2)), "tn")
        d_vb, d_kbg = d_rhs[:, :, :GDN_DIM], d_rhs[:, :, GDN_DIM:]
        dL = -jnp.where(ck.strict, _h3(d_rhs, ck.sol, "nt"), 0.0)
        P = dL * ck.decay
        Q = d_attn * ck.decay
        d_kb = _mx(P, k, "nn") + d_kbg * ck.eg
        d_qs = _mx(Q, k, "nn") + d_qd * ck.eg
        d_k = _mx(P, ck.kb, "tn") + _mx(Q, ck.qs, "tn") + d_kd * ck.ekd + d_kb * ck.beta
        E = dL * ck.L + d_attn * ck.attn
        kdsum = jnp.sum(d_kd * ck.kd, axis=2, keepdims=True)
        d_gc = (jnp.sum(E, axis=2, keepdims=True) - _colsum_col(E)
                + jnp.sum(d_kbg * ck.kbg, axis=2, keepdims=True)
                + jnp.sum(d_qd * ck.qd, axis=2, keepdims=True) - kdsum)
        d_gc = _add_at_last_row(d_gc, d_gl + jnp.sum(kdsum, axis=1, keepdims=True), hb)
        d_beta = jnp.sum(d_vb * v, axis=2, keepdims=True) + jnp.sum(d_kb * k, axis=2, keepdims=True)
        d_q = d_qs * (GDN_DIM**-0.5)
        d_v = d_vb * ck.beta
        lane = _iota2((CHUNK, LANES), 1)
        for d, (dq_ref, dg_ref) in enumerate(((dqf_ref, dgf_ref), (dqb_ref, dgb_ref))):
            acc = dg_ref[...]
            for hh in range(hb):
                c, h = d * hb + hh, b * hb + hh
                sl = slice(hh * GDN_DIM, (hh + 1) * GDN_DIM)
                dq_ref[0, :, sl] = d_q[c]
                dq_ref[1, :, sl] = d_k[c]
                dq_ref[2, :, sl] = d_v[c]
                acc = jnp.where(lane == h + 8 * d, d_gc[c], acc)
                acc = jnp.where(lane == 16 + h + 8 * d, d_beta[c], acc)
            dg_ref[...] = acc

        @pl.when(b == nb - 1)
        def _():
            for d, dg_ref in enumerate((dgf_ref, dgb_ref)):
                blk = dg_ref[...]
                dg_ref[...] = jnp.where(lane < 16, _exact01(_chunk_tri(CHUNK, d == 0), blk), blk)

    fw, bw = (lambda n: N - 1 - n), (lambda n: n)
    qsp = lambda ci: pl.BlockSpec((3, CHUNK, W), lambda n, b: (0, ci(n), b))
    gsp = lambda ci: pl.BlockSpec((CHUNK, LANES), lambda n, b: (ci(n), 0))
    osp = lambda ci: pl.BlockSpec((CHUNK, W), lambda n, b: (ci(n), b))
    ssp = lambda ci: pl.BlockSpec((1, hb, GDN_DIM, GDN_DIM), lambda n, b: (ci(n), b, 0, 0))
    asp = lambda ci: pl.BlockSpec((1, hb, CHUNK, CHUNK), lambda n, b: (ci(n), b, 0, 0))
    dq_shape = jax.ShapeDtypeStruct((3, T, nh * GDN_DIM), f32)
    dg_shape = jax.ShapeDtypeStruct((T, LANES), f32)
    return pl.pallas_call(
        body,
        out_shape=(dq_shape, dq_shape, dg_shape, dg_shape),
        grid=(N, nb),
        in_specs=[qsp(fw), qsp(bw), gsp(fw), gsp(bw), gsp(fw), gsp(bw), ssp(fw), ssp(bw), asp(fw), asp(bw), osp(fw), osp(bw)],
        out_specs=(qsp(fw), qsp(bw), gsp(fw), gsp(bw)),
        scratch_shapes=[pltpu.VMEM((2, nh, GDN_DIM, GDN_DIM), f32)],
        compiler_params=_cparams(("arbitrary", "arbitrary")),
        name="gdn_scan_bwd",
    )(qkvn, qkvn, gates, gates, gc, gc, st_f, st_b, a_f, a_b, do, do)


class _GlaBatch:
    def __init__(self, q, k, G, nf):
        B, Cn = q.shape[0], q.shape[1]
        self.incl, _, _ = _dir_masks(B, nf, Cn)
        self.eG = jnp.exp(G)
        self.enG = jnp.exp(-G)
        self.qg = q * (GLA_DK**-0.5) * self.eG
        self.kg = k * self.enG
        self.attn = jnp.where(self.incl, _mx(self.qg, self.kg, "nt"), 0.0)
        gl = _last_row(G, nf)
        self.ekd = jnp.exp(gl - G)
        self.kd = k * self.ekd
        self.egl = jnp.exp(gl)


def _gla_chain_inputs(refs_f, refs_b):
    nh = GLA_HEADS
    out = []
    for i, width in enumerate((GLA_DK, GLA_DK, GLA_DV, GLA_DK)):
        out.append(jnp.stack(_heads(refs_f[i], (), nh, width) + _heads(refs_b[i], (), nh, width)))
    return out


def _gla_specs(N, ci):
    nh = GLA_HEADS
    return [
        pl.BlockSpec((CHUNK, nh * GLA_DK), lambda n: (ci(n), 4096 // (nh * GLA_DK))),
        pl.BlockSpec((CHUNK, nh * GLA_DK), lambda n: (ci(n), 4608 // (nh * GLA_DK))),
        pl.BlockSpec((CHUNK, nh * GLA_DV), lambda n: (ci(n), 5120 // (nh * GLA_DV))),
        pl.BlockSpec((CHUNK, nh * GLA_DK), lambda n: (ci(n), 0)),
    ]


def _gla_fwd(proj, G_f, G_b):
    T = proj.shape[0]
    N = T // CHUNK
    nh = GLA_HEADS

    def body(qf, kf, vf, gf, qb, kb, vb, gb, of_ref, ob_ref, sf_ref, sb_ref, s_scr):
        n = pl.program_id(0)

        @pl.when(n == 0)
        def _():
            s_scr[...] = jnp.zeros((2 * nh, GLA_DV, GLA_DK), f32)

        q, k, v, G = _gla_chain_inputs((qf, kf, vf, gf), (qb, kb, vb, gb))
        ck = _GlaBatch(q, k, G, nh)
        St = s_scr[...]
        sf_ref[0], sb_ref[0] = St[:nh], St[nh:]
        o = _mx(ck.qg, St, "nt") + _mx(ck.attn, v, "nn")
        for i in range(nh):
            sl = slice(i * GLA_DV, (i + 1) * GLA_DV)
            of_ref[:, sl] = o[i]
            ob_ref[:, sl] = o[nh + i]
        s_scr[...] = St * ck.egl + _mx(v, ck.kd, "tn")

    fw, bw = (lambda n: n), (lambda n: N - 1 - n)
    osp = lambda ci: pl.BlockSpec((CHUNK, nh * GLA_DV), lambda n: (ci(n), 0))
    ssp = lambda ci: pl.BlockSpec((1, nh, GLA_DV, GLA_DK), lambda n: (ci(n), 0, 0, 0))
    o_shape = jax.ShapeDtypeStruct((T, nh * GLA_DV), f32)
    s_shape = jax.ShapeDtypeStruct((N, nh, GLA_DV, GLA_DK), f32)
    return pl.pallas_call(
        body,
        out_shape=(o_shape, o_shape, s_shape, s_shape),
        grid=(N,),
        in_specs=_gla_specs(N, fw) + _gla_specs(N, bw),
        out_specs=(osp(fw), osp(bw), ssp(fw), ssp(bw)),
        scratch_shapes=[pltpu.VMEM((2 * nh, GLA_DV, GLA_DK), f32)],
        compiler_params=_cparams(("arbitrary",)),
        name="gla_scan_fwd",
    )(proj, proj, proj, G_f, proj, proj, proj, G_b)


def _gla_bwd(proj, G_f, G_b, st_f, st_b, do):
    T = proj.shape[0]
    N = T // CHUNK
    nh = GLA_HEADS

    def body(qf, kf, vf, gf, sf_ref, dof_ref, qb, kb, vb, gb, sb_ref, dob_ref,
             dqf_ref, dkf_ref, dvf_ref, dgf_ref, dqb_ref, dkb_ref, dvb_ref, dgb_ref, ds_scr):
        n = pl.program_id(0)

        @pl.when(n == 0)
        def _():
            ds_scr[...] = jnp.zeros((2 * nh, GLA_DV, GLA_DK), f32)

        q, k, v, G = _gla_chain_inputs((qf, kf, vf, gf), (qb, kb, vb, gb))
        ck = _GlaBatch(q, k, G, nh)
        St = jnp.concatenate([sf_ref[0], sb_ref[0]], axis=0)
        dSt = ds_scr[...]
        dov = jnp.stack(_heads(dof_ref, (), nh, GLA_DV) + _heads(dob_ref, (), nh, GLA_DV))
        d_attn = jnp.where(ck.incl, _mx(dov, v, "nt"), 0.0)
        d_qg = _mx(dov, St, "nn") + _mx(d_attn, ck.kg, "nn")
        d_kg = _mx(d_attn, ck.qg, "tn")
        d_v = _mx(ck.attn, dov, "tn") + _mx(ck.kd, dSt, "nt")
        d_kd = _mx(v, dSt, "nn")
        ds_scr[...] = dSt * ck.egl + _mx(dov, ck.qg, "tn")
        kdd = d_kd * ck.kd
        d_gl = jnp.sum(dSt * St, axis=1, keepdims=True) * ck.egl + jnp.sum(kdd, axis=1, keepdims=True)
        d_q = d_qg * ck.eG * (GLA_DK**-0.5)
        d_k = d_kg * ck.enG + d_kd * ck.ekd
        d_G = _add_at_last_row(d_qg * ck.qg - d_kg * ck.kg - kdd, d_gl, nh)
        shp = (2 * nh, CHUNK, CHUNK)
        bb, ii, jj = _iota3(shp, 0), _iota3(shp, 1), _iota3(shp, 2)
        tri = jnp.logical_or(jnp.logical_and(bb < nh, jj >= ii), jnp.logical_and(bb >= nh, jj <= ii))
        d_gk = _exact01(jnp.where(tri, 1.0, 0.0), d_G)
        for d, (dq_ref, dk_ref, dv_ref, dg_ref) in enumerate(((dqf_ref, dkf_ref, dvf_ref, dgf_ref), (dqb_ref, dkb_ref, dvb_ref, dgb_ref))):
            for i in range(nh):
                c = d * nh + i
                ks, vs = slice(i * GLA_DK, (i + 1) * GLA_DK), slice(i * GLA_DV, (i + 1) * GLA_DV)
                dq_ref[:, ks] = d_q[c]
                dk_ref[:, ks] = d_k[c]
                dv_ref[:, vs] = d_v[c]
                dg_ref[:, ks] = d_gk[c]

    def in_specs(ci):
        return _gla_specs(N, ci) + [
            pl.BlockSpec((1, nh, GLA_DV, GLA_DK), lambda n: (ci(n), 0, 0, 0)),
            pl.BlockSpec((CHUNK, nh * GLA_DV), lambda n: (ci(n), 0)),
        ]

    def out_specs(ci):
        ksp = pl.BlockSpec((CHUNK, nh * GLA_DK), lambda n: (ci(n), 0))
        return [ksp, ksp, pl.BlockSpec((CHUNK, nh * GLA_DV), lambda n: (ci(n), 0)), ksp]

    fw, bw = (lambda n: N - 1 - n), (lambda n: n)
    k_shape = jax.ShapeDtypeStruct((T, nh * GLA_DK), f32)
    v_shape = jax.ShapeDtypeStruct((T, nh * GLA_DV), f32)
    return pl.pallas_call(
        body,
        out_shape=(k_shape, k_shape, v_shape, k_shape) * 2,
        grid=(N,),
        in_specs=in_specs(fw) + in_specs(bw),
        out_specs=tuple(out_specs(fw) + out_specs(bw)),
        scratch_shapes=[pltpu.VMEM((2 * nh, GLA_DV, GLA_DK), f32)],
        compiler_params=_cparams(("arbitrary",)),
        name="gla_scan_bwd",
    )(proj, proj, proj, G_f, st_f, do, proj, proj, proj, G_b, st_b, do)


def _gla_dsum(dq_f, dq_b, dk_f, dk_b, dv_f, dv_b, dproj, tb=256):
    T = dq_f.shape[0]
    wk, wv = GLA_HEADS * GLA_DK, GLA_HEADS * GLA_DV

    def body(qf, qb, kf, kb, vf, vb, _, o_ref):
        o_ref[:, :wk] = (qf[...] + qb[...]).astype(o_ref.dtype)
        o_ref[:, wk : 2 * wk] = (kf[...] + kb[...]).astype(o_ref.dtype)
        o_ref[:, 2 * wk :] = (vf[...] + vb[...]).astype(o_ref.dtype)

    ksp = pl.BlockSpec((tb, wk), lambda i: (i, 0))
    vsp = pl.BlockSpec((tb, wv), lambda i: (i, 0))
    return pl.pallas_call(
        body,
        out_shape=jax.ShapeDtypeStruct(dproj.shape, dproj.dtype),
        grid=(T // tb,),
        in_specs=[ksp, ksp, ksp, ksp, vsp, vsp, _ANY_SPEC],
        out_specs=pl.BlockSpec((tb, 2 * wk + wv), lambda i: (i, DP_GLA_QKV_BLK)),
        input_output_aliases={6: 0},
        compiler_params=_cparams(("parallel",)),
        name="gla_dsum",
    )(dq_f, dq_b, dk_f, dk_b, dv_f, dv_b, dproj)


def _local_step(x, target, wcat, conv_w, avec, dvec, gdn_norm_w, w2f, w2b, b2f, b2b, gla_norm_w, proj_weights, w_pre, w_post,
                send_proj_grads=None, send_in_grads=None):
    h, h_t = _rms_pre(x, w_pre)
    proj = _matmul(h, wcat, "nt", f32, "proj_in", 2048, 512, 1024)
    qkvn = _gdn_prep(proj, conv_w)
    gates, gc = _gdn_gates(proj, avec, dvec)
    oa_f, oa_b, sa_f, sa_b, inv_f, inv_b = _gdn_fwd(qkvn, gates, gc)
    og = _out_norm(oa_f, oa_b, proj, Z_BLK, gdn_norm_w, GDN_DIM, "gdn_out")
    G_f, G_b = _gla_prep(proj, w2f, w2b, b2f, b2b)
    ob_f, ob_b, sb_f, sb_b = _gla_fwd(proj, G_f, G_b)
    obg = _out_norm(ob_f, ob_b, proj, G_B_BLK, gla_norm_w, GLA_DV, "gla_out")
    wpg, wpl, wout = proj_weights(obg)
    (loss_row, d_wpost, dy, m, dout, dya, dyb, dproj, dog, dobg) = _mid(og, obg, proj, x, target, wpg, wpl, wout, w_post)

    d_wout = _matmul(m, dout, "tn", MXU_DTYPE, "dw_out", 1024, 512, 2048)
    d_wpg = _matmul(og, dya, "tn", MXU_DTYPE, "dw_proj_gdn", 1024, 512, 2048)
    d_wpl = _matmul(obg, dyb, "tn", MXU_DTYPE, "dw_proj_gla", 1024, 512, 2048)
    if send_proj_grads is not None:
        gla_norm_w = gla_norm_w + send_proj_grads(d_wpg, d_wpl, d_wout)

    do_b, dproj, d_gla_norm = _out_norm_bwd(dobg, ob_f, ob_b, proj, G_B_BLK, gla_norm_w, GLA_DV, "gla_out_bwd", dproj)
    dq_f, dk_f, dv_f, dgk_f, dq_b, dk_b, dv_b, dgk_b = _gla_bwd(proj, G_f, G_b, sb_f, sb_b, do_b)
    dproj = _gla_dsum(dq_f, dq_b, dk_f, dk_b, dv_f, dv_b, dproj)
    dsmall_gla, d_w2f, d_w2b, d_b2f, d_b2b = _gla_prep_bwd(proj, w2f, w2b, b2f, b2b, dgk_f, dgk_b)

    do_a, dproj, d_gdn_norm = _out_norm_bwd(dog, oa_f, oa_b, proj, Z_BLK, gdn_norm_w, GDN_DIM, "gdn_out_bwd", dproj)
    dqkv_f, dqkv_b, dg_f, dg_b = _gdn_bwd(qkvn, gates, gc, sa_f, sa_b, inv_f, inv_b, do_a)
    dproj, d_conv = _gdn_prep_bwd(proj, conv_w, dqkv_f, dqkv_b, dproj)
    dproj, d_alog, d_dtb = _gdn_gates_bwd(proj, avec, dvec, dg_f, dg_b, dsmall_gla, dproj)

    d_wcat = _matmul(h_t, dproj, "nn", MXU_DTYPE, "dw_in", 1024, 512, 2048, out_t=True)
    started = None if send_in_grads is None else send_in_grads(d_wcat)
    dh = _matmul(dproj, wcat, "nn", f32, "dh", 1024, 1024, 2432, after=started)
    grad_x, d_wpre = _rms_pre_bwd(x, w_pre, dh, dy)
    return dict(loss_row=loss_row, grad_x=grad_x, d_wcat=d_wcat, d_conv=d_conv, d_alog=d_alog, d_dtb=d_dtb,
                d_gdn_norm=d_gdn_norm, d_wpg=d_wpg, d_w2f=d_w2f, d_w2b=d_w2b, d_b2f=d_b2f, d_b2b=d_b2b,
                d_gla_norm=d_gla_norm, d_wpl=d_wpl, d_wout=d_wout, d_wpost=d_wpost, d_wpre=d_wpre)


def _to_cat(wf):
    pad = jnp.zeros((N_CAT - N_IN,) + wf.shape[1:], wf.dtype)
    return jnp.concatenate([wf[:4096], wf[4128:6176], wf[7232:9280], wf[6176:7200], wf[4096:4128], wf[7200:7232], pad], axis=0)


def _from_cat(wc):
    return jnp.concatenate([wc[:4096], wc[9216:9248], wc[4096:6144], wc[8192:9216], wc[9248:9280], wc[6144:8192]], axis=0)


def _lane_row(*pieces):
    row = jnp.concatenate(pieces, axis=1)
    return jnp.pad(row, ((0, 0), (0, LANES - row.shape[1])))


def _pad_w2(w2, row0):
    return jnp.pad(w2, ((row0, LANES - row0 - w2.shape[0]), (0, 0)))


SM_ROWS, SM_COLS = 32, 384
REP_ROWS = 32


def _pack_small_shard(conv, w2f, w2b):
    lead = conv.shape[:-2]
    z = lambda r, c: jnp.zeros(lead + (r, c), f32)
    top = jnp.concatenate([conv, z(8 - CONV_K, SM_COLS)], axis=-2)
    mid = jnp.concatenate([w2f, w2b, z(16, SM_COLS - 128)], axis=-1)
    return jnp.concatenate([top, mid, z(SM_ROWS - 24, SM_COLS)], axis=-2)


LOSS_ROW = 28


def _pack_rep(ln_pre, ln_post, b2f, b2b, gla_norm, gdn_norm, alog_f, alog_b, dt_f, dt_b, loss=None):
    last = _lane_row(alog_f, alog_b, dt_f, dt_b)
    extra = jnp.zeros((1, LANES), f32) if loss is None else _lane_row(loss)
    rows = [ln_pre.reshape(8, LANES), ln_post.reshape(8, LANES), b2f.reshape(4, LANES), b2b.reshape(4, LANES),
            gla_norm.reshape(2, LANES), gdn_norm.reshape(1, LANES), last, extra, jnp.zeros((REP_ROWS - LOSS_ROW - 1, LANES), f32)]
    return jnp.concatenate(rows, axis=0)


def _unpack_rep(p):
    return dict(ln_pre_w=p[0:8].reshape(1, 1024), ln_post_w=p[8:16].reshape(1, 1024), gk_b2_fwd=p[16:20].reshape(1, 512),
                gk_b2_bwd=p[20:24].reshape(1, 512), gla_norm_w=p[24:26].reshape(1, 256), gdn_norm_w=p[26:27],
                a_log_fwd=p[27:28, 0:8], a_log_bwd=p[27:28, 8:16], dt_bias_fwd=p[27:28, 16:24], dt_bias_bwd=p[27:28, 24:32])


_MESH = pl.DeviceIdType.MESH


N_BIG_COPIES, N_SMALL_COPIES = 9, 7


def _gather_weights(w_in_s, sm_s):
    shapes = (w_in_s.shape, sm_s.shape)
    dtypes = (MXU_DTYPE, f32)

    def body(win_ref, sm_ref, gin_ref, gsm_ref, send_sems, recv_sems):
        x, y, c = lax.axis_index("x"), lax.axis_index("y"), lax.axis_index("c")
        me, sibling = (x, y, c), (x, y, 1 - c)
        chips = [(1 - x, y), (x, 1 - y), (1 - x, 1 - y)]
        bufs = (gin_ref, gsm_ref)

        def idx(px, py, pc):
            return 4 * px + 2 * py + pc

        for r in range(w_in_s.shape[1] // LANES):
            cols = slice(r * LANES, (r + 1) * LANES)
            gin_ref[idx(*me), :, cols] = win_ref[:, cols].astype(MXU_DTYPE)
        gsm_ref[idx(*me)] = sm_ref[...]

        def copy(sem, blk, to):
            return pltpu.make_async_remote_copy(src_ref=blk, dst_ref=blk, send_sem=send_sems.at[sem], recv_sem=recv_sems.at[sem],
                                                device_id=to, device_id_type=_MESH)

        def small(k, block, to):
            return copy(N_BIG_COPIES + k, gsm_ref.at[idx(*block)], to)

        half = w_in_s.shape[1] // 2
        xn, yn, dg = (1 - x, y), (x, 1 - y), (1 - x, 1 - y)

        def big(sem, block, to, part=None):
            blk = gin_ref.at[idx(*block)]
            return copy(sem, blk if part is None else blk.at[:, pl.ds(part * half, half)], to)

        started = [big(0, me, sibling), big(1, me, (*xn, c)), big(2, me, (*yn, c)), small(0, me, sibling)]
        started += [small(1 + j, me, (*chip, c)) for j, chip in enumerate(chips)]
        for cp in started:
            cp.start()

        def then(arrived, forwards):
            arrived.wait_recv()
            for cp in forwards:
                cp.start()
            started.extend(forwards)

        then(big(1, (*xn, c), me), [big(3, (*xn, c), (*yn, c), 1), big(5, (*xn, c), sibling)])
        then(big(2, (*yn, c), me), [big(4, (*yn, c), (*xn, c), 0), big(6, (*yn, c), sibling)])
        then(big(4, (*dg, c), me, 0), [big(7, (*dg, c), sibling, 0)])
        then(big(3, (*dg, c), me, 1), [big(8, (*dg, c), sibling, 1)])
        for j, chip in enumerate(chips):
            then(small(1 + j, (*chip, c), me), [small(4 + j, (*chip, c), sibling)])
        big(0, sibling, me).wait_recv()
        big(5, (*xn, 1 - c), me).wait_recv()
        big(6, (*yn, 1 - c), me).wait_recv()
        big(7, (*dg, 1 - c), me, 0).wait_recv()
        big(8, (*dg, 1 - c), me, 1).wait_recv()
        small(0, sibling, me).wait_recv()
        for j, chip in enumerate(chips):
            small(4 + j, (*chip, 1 - c), me).wait_recv()
        for cp in started:
            cp.wait_send()

    vm = pl.BlockSpec(memory_space=pltpu.VMEM)
    return pl.pallas_call(
        body,
        out_shape=tuple(jax.ShapeDtypeStruct((N_DEV,) + s, d) for s, d in zip(shapes, dtypes)),
        in_specs=[vm, vm],
        out_specs=(vm, vm),
        scratch_shapes=[pltpu.SemaphoreType.DMA((N_BIG_COPIES + N_SMALL_COPIES,)), pltpu.SemaphoreType.DMA((N_BIG_COPIES + N_SMALL_COPIES,))],
        compiler_params=pltpu.CompilerParams(vmem_limit_bytes=VMEM_LIMIT),
        name="gather_weights",
    )(w_in_s, sm_s)


N_CHIP = 4
_EFFECT = pltpu.SideEffectType.DATAFLOW_SIDE_EFFECTING
_HBM_SPEC = pl.BlockSpec(memory_space=pltpu.HBM)
_SEM_SPEC = pl.BlockSpec(memory_space=pltpu.SEMAPHORE)


def _relation_peer(k, x, y, c):
    px = (1 - x) if (k & 4) else x
    py = (1 - y) if (k & 2) else y
    pc = (1 - c) if (k & 1) else c
    return (px, py, pc), 4 * px + 2 * py + pc


_PLAN_COPIES = {"gather": N_DEV - 1, "blocks": N_DEV - 1, "sibling": 4, "chips": 3}
_PLAN_SLOTS = {"gather": N_DEV, "blocks": N_DEV, "sibling": 4, "chips": 4}


def _plan_copies(plan, src_ref, land_ref, send_sems, recv_sems, arrival):
    x, y, c = lax.axis_index("x"), lax.axis_index("y"), lax.axis_index("c")
    me, my_chip = 4 * x + 2 * y + c, 2 * x + y
    out = []

    def add(i, src, there, here, dev):
        out.append(pltpu.make_async_remote_copy(src_ref=src, dst_ref=land_ref.at[here if arrival else there], send_sem=send_sems.at[i],
                                                recv_sem=recv_sems.at[i], device_id=dev, device_id_type=_MESH))

    if plan in ("gather", "blocks"):
        for k in range(1, N_DEV):
            dev, p = _relation_peer(k, x, y, c)
            add(k - 1, src_ref.at[p] if plan == "blocks" else src_ref, me, p, dev)
    elif plan == "sibling":
        for j in range(4):
            add(j, src_ref.at[2 * j + 1 - c], j, j, (x, y, 1 - c))
    else:
        for i, k in enumerate((2, 4, 6)):
            (px, py, pc), _ = _relation_peer(k, x, y, c)
            add(i, src_ref.at[2 * px + py], my_chip, 2 * px + py, (px, py, pc))
    return out


def _exchange_start(src, plan, name, after=None):
    n = _PLAN_COPIES[plan]
    land = lax.empty((_PLAN_SLOTS[plan],) + (src.shape if plan == "gather" else src.shape[1:]), src.dtype)
    extra = [] if after is None else [after]

    def body(src_ref, land_ref, *rest):
        send_sems, recv_sems, _, _, token = rest[len(extra) :]
        for cp in _plan_copies(plan, src_ref, land_ref, send_sems, recv_sems, False):
            cp.start()
        token[...] = jnp.zeros_like(token)

    return pl.pallas_call(
        body,
        name=name,
        out_shape=(pltpu.SemaphoreType.DMA((n,)), pltpu.SemaphoreType.DMA((n,)), pltpu.HBM(src.shape, src.dtype),
                   pltpu.HBM(land.shape, land.dtype), jax.ShapeDtypeStruct((8, LANES), f32)),
        in_specs=(_HBM_SPEC, _HBM_SPEC) + (_ANY_SPEC,) * len(extra),
        out_specs=(_SEM_SPEC, _SEM_SPEC, _HBM_SPEC, _HBM_SPEC, pl.BlockSpec(memory_space=pltpu.VMEM)),
        input_output_aliases={0: 2, 1: 3},
        compiler_params=pltpu.CompilerParams(has_side_effects=_EFFECT),
    )(pltpu.with_memory_space_constraint(src, pltpu.HBM), pltpu.with_memory_space_constraint(land, pltpu.HBM), *extra)


def _exchange_wait(started, after, plan, name):
    send_sems, recv_sems, src_thru, land_thru, _ = started

    def body(src_ref, land_ref, send_sems, recv_sems, after_ref, src_dead, got_ref):
        for cp in _plan_copies(plan, src_ref, land_ref, send_sems, recv_sems, True):
            cp.wait_send()
            cp.wait_recv()

    return pl.pallas_call(
        body,
        name=name,
        out_shape=(pltpu.HBM(src_thru.shape, src_thru.dtype), pltpu.HBM(land_thru.shape, land_thru.dtype)),
        in_specs=(_HBM_SPEC, _HBM_SPEC, _SEM_SPEC, _SEM_SPEC, pl.BlockSpec(memory_space=pl.ANY)),
        out_specs=(_HBM_SPEC, _HBM_SPEC),
        input_output_aliases={0: 0, 1: 1},
        compiler_params=pltpu.CompilerParams(has_side_effects=_EFFECT),
    )(src_thru, land_thru, send_sems, recv_sems, after)


def _pair_sum(g, r, name, tc=LANES):
    _, R, Cc = g.shape

    def body(g_ref, r_ref, o_ref):
        o_ref[...] = (g_ref[...].astype(f32) + r_ref[...].astype(f32)).astype(o_ref.dtype)

    return pl.pallas_call(
        body,
        out_shape=jax.ShapeDtypeStruct(r.shape, r.dtype),
        grid=(N_CHIP, Cc // tc),
        in_specs=[pl.BlockSpec((1, R, tc), lambda k, i: (2 * k + lax.axis_index("c"), 0, i)), pl.BlockSpec((1, R, tc), lambda k, i: (k, 0, i))],
        out_specs=pl.BlockSpec((1, R, tc), lambda k, i: (k, 0, i)),
        compiler_params=_cparams(("parallel", "parallel")),
        name=name,
    )(g, r)


def _exchange_small(g_sm, g_rep):
    srcs = (g_sm, g_rep)

    def body(in0, in1, out0, out1, send_sems, recv_sems, local_sems):
        x, y, c = lax.axis_index("x"), lax.axis_index("y"), lax.axis_index("c")
        me = 4 * x + 2 * y + c
        ins, outs = (in0, in1), (out0, out1)

        def copy(a, k, arrival):
            dev, p = _relation_peer(k, x, y, c)
            return pltpu.make_async_remote_copy(src_ref=ins[a].at[p] if a == 0 else ins[a], dst_ref=outs[a].at[p if arrival else me],
                                                send_sem=send_sems.at[7 * a + k - 1], recv_sem=recv_sems.at[7 * a + k - 1],
                                                device_id=dev, device_id_type=_MESH)

        pairs = [(a, k) for a in (0, 1) for k in range(1, N_DEV)]
        sends = [copy(a, k, False) for a, k in pairs]
        for cp in sends:
            cp.start()
        local = [pltpu.make_async_copy(ins[0].at[me], outs[0].at[me], local_sems.at[0]),
                 pltpu.make_async_copy(ins[1], outs[1].at[me], local_sems.at[1])]
        for cp in local:
            cp.start()
        for a, k in pairs:
            copy(a, k, True).wait_recv()
        for cp in sends:
            cp.wait_send()
        for cp in local:
            cp.wait()

    return pl.pallas_call(
        body,
        out_shape=(jax.ShapeDtypeStruct(g_sm.shape, g_sm.dtype), jax.ShapeDtypeStruct((N_DEV,) + g_rep.shape, g_rep.dtype)),
        in_specs=[_HBM_SPEC] * 2,
        out_specs=(_HBM_SPEC,) * 2,
        scratch_shapes=[pltpu.SemaphoreType.DMA((14,)), pltpu.SemaphoreType.DMA((14,)), pltpu.SemaphoreType.DMA((2,))],
        name="exchange_small",
    )(*srcs)


def _sum_adam(parts, w, m, v, name, tb, tc=None, own=None, own_slot=None):
    R, Cc = w.shape
    n_parts = parts.shape[0]
    tb = R if tc else min(tb, R)
    tc = tc or Cc
    assert R % tb == 0 and Cc % tc == 0
    bc1 = 1.0 - ADAM_B1**ADAM_STEP
    bc2 = 1.0 - ADAM_B2**ADAM_STEP
    extra = [] if own is None else [own]

    def body(p_ref, w_ref, m_ref, v_ref, *rest):
        g_ref, d_ref, nm_ref, nv_ref = rest[len(extra) :]

        def part(j):
            pj = p_ref[j].astype(f32)
            return pj if own is None else jnp.where(own_slot() == j, rest[0][0].astype(f32), pj)

        g = part(0)
        for j in range(1, n_parts):
            g = g + part(j)
        g_ref[...] = g
        m2 = ADAM_B1 * m_ref[...] + (1.0 - ADAM_B1) * g
        v2 = ADAM_B2 * v_ref[...] + (1.0 - ADAM_B2) * (g * g)
        nm_ref[...] = m2
        nv_ref[...] = v2
        d_ref[...] = -ADAM_LR * ((m2 / bc1) / (jnp.sqrt(v2 / bc2) + ADAM_EPS) + ADAM_WD * w_ref[...])

    blk = pl.BlockSpec((tb, tc), lambda i, j: (i, j))
    o = jax.ShapeDtypeStruct((R, Cc), f32)
    return pl.pallas_call(
        body,
        out_shape=(o, o, o, o),
        grid=(R // tb, Cc // tc),
        in_specs=[pl.BlockSpec((n_parts, tb, tc), lambda i, j: (0, i, j)), blk, blk, blk]
        + [pl.BlockSpec((1, tb, tc), lambda i, j: (own_slot(), i, j))] * len(extra),
        out_specs=(blk, blk, blk, blk),
        compiler_params=_cparams(("parallel", "parallel")),
        name=name,
    )(parts, w, m, v, *extra)


def _my_device():
    return 4 * lax.axis_index("x") + 2 * lax.axis_index("y") + lax.axis_index("c")


def _my_chip():
    return 2 * lax.axis_index("x") + lax.axis_index("y")


_WEIGHTS = ["ln_pre_w", "w_in", "conv_w", "a_log_fwd", "a_log_bwd", "dt_bias_fwd", "dt_bias_bwd", "gdn_norm_w", "w_proj_gdn",
            "gk_w2_fwd", "gk_b2_fwd", "gk_w2_bwd", "gk_b2_bwd", "gla_norm_w", "w_proj_gla", "w_out", "ln_post_w"]
_REP_ORDER = ["ln_pre_w", "ln_post_w", "gk_b2_fwd", "gk_b2_bwd", "gla_norm_w", "gdn_norm_w", "a_log_fwd", "a_log_bwd", "dt_bias_fwd", "dt_bias_bwd"]


def kernel(x, ln_pre_w, w_in, conv_w, a_log_fwd, a_log_bwd, dt_bias_fwd, dt_bias_bwd, gdn_norm_w, w_proj_gdn, gk_w2_fwd, gk_b2_fwd, gk_w2_bwd, gk_b2_bwd, gla_norm_w, w_proj_gla, w_out, ln_post_w, loss_target, m_ln_pre_w, m_w_in, m_conv_w, m_a_log_fwd, m_a_log_bwd, m_dt_bias_fwd, m_dt_bias_bwd, m_gdn_norm_w, m_w_proj_gdn, m_gk_w2_fwd, m_gk_b2_fwd, m_gk_w2_bwd, m_gk_b2_bwd, m_gla_norm_w, m_w_proj_gla, m_w_out, m_ln_post_w, v_ln_pre_w, v_w_in, v_conv_w, v_a_log_fwd, v_a_log_bwd, v_dt_bias_fwd, v_dt_bias_bwd, v_gdn_norm_w, v_w_proj_gdn, v_gk_w2_fwd, v_gk_b2_fwd, v_gk_w2_bwd, v_gk_b2_bwd, v_gla_norm_w, v_w_proj_gla, v_w_out, v_ln_post_w):
    args = locals()
    W = {n: args[n] for n in _WEIGHTS}
    M = {n: args["m_" + n] for n in _WEIGHTS}
    V = {n: args["v_" + n] for n in _WEIGHTS}

    wp_stack = lambda P: jnp.concatenate([P["w_proj_gdn"], P["w_proj_gla"], P["w_out"]], axis=0)
    sm_pack = lambda P: _pack_small_shard(P["conv_w"][0], P["gk_w2_fwd"][0], P["gk_w2_bwd"][0])
    me = 4 * lax.axis_index("x") + 2 * lax.axis_index("y") + lax.axis_index("c")
    w_in_t, m_in_t, v_in_t = (jnp.transpose(a[0]) for a in (w_in, m_w_in, v_w_in))
    g_in, g_sm = _gather_weights(w_in_t, sm_pack(W))
    wp_own = wp_stack(W).astype(MXU_DTYPE)
    wp_started = _exchange_start(wp_own, "gather", "gather_proj_start", after=g_sm)

    def proj_weights(after):
        own, land = _exchange_wait(wp_started, after, "gather", "gather_proj_wait")
        g_p = lax.dynamic_update_slice(land, own[None], (me, 0, 0, 0))
        return tuple(g_p[:, i].reshape(D_MODEL, D_MODEL) for i in range(3))

    rows_p = 3 * D_MODEL // N_DEV
    sent = {}

    def send_proj_grads(d_wpg, d_wpl, d_wout):
        p_p = jnp.stack([d_wpg, d_wpl, d_wout]).reshape(3, N_DEV, D_MODEL // N_DEV, D_MODEL).transpose(1, 0, 2, 3)
        sent["p_p"] = p_p.reshape(N_DEV, rows_p, D_MODEL)
        sent["started"] = _exchange_start(sent["p_p"], "blocks", "exchange_proj_start")
        return sent["started"][4][0:1, 0:1]

    def send_in_grads(d_wcat):
        p_in = _from_cat(d_wcat).reshape(N_DEV, SHARD_IN, D_MODEL)
        sib = _exchange_start(p_in, "sibling", "exchange_in_sibling_start")
        p_p, land_p = _exchange_wait(sent["started"], sib[4], "blocks", "exchange_proj_wait")
        sent["o_p"] = _sum_adam(land_p, wp_stack(W).reshape(rows_p, D_MODEL), wp_stack(M).reshape(rows_p, D_MODEL),
                                wp_stack(V).reshape(rows_p, D_MODEL), "adam_w_proj", 128, own=p_p, own_slot=_my_device)
        p_in, sib_land = _exchange_wait(sib, sent["o_p"][0], "sibling", "exchange_in_sibling_wait")
        sent["chips"] = _exchange_start(_pair_sum(p_in, sib_land, "pair_sum_w_in", tc=512), "chips", "exchange_in_chips_start")
        return sent["chips"][4]

    wcat = _to_cat(g_in.reshape(N_IN, D_MODEL))
    conv_full = g_sm[:, 0:CONV_K, :].transpose(1, 0, 2).reshape(CONV_K, N_DEV * SM_COLS)
    w2f_full = g_sm[:, 8:24, 0:64].transpose(1, 0, 2).reshape(16, 512)
    w2b_full = g_sm[:, 8:24, 64:128].transpose(1, 0, 2).reshape(16, 512)

    avec = _lane_row(a_log_fwd, a_log_bwd)
    dvec = _lane_row(dt_bias_fwd, dt_bias_bwd)
    G = _local_step(x[0], loss_target[0], wcat, conv_full, avec, dvec, gdn_norm_w, _pad_w2(w2f_full, 32), _pad_w2(w2b_full, 48),
                    gk_b2_fwd, gk_b2_bwd, gla_norm_w, proj_weights, ln_pre_w + wp_started[4][0:1, 0:1], ln_post_w, send_proj_grads,
                    send_in_grads)

    p_sm = _pack_small_shard(G["d_conv"].reshape(CONV_K, N_DEV, SM_COLS).transpose(1, 0, 2),
                             G["d_w2f"][32:48].reshape(16, N_DEV, 64).transpose(1, 0, 2),
                             G["d_w2b"][48:64].reshape(16, N_DEV, 64).transpose(1, 0, 2))
    p_rep = _pack_rep(G["d_wpre"], G["d_wpost"], G["d_b2f"], G["d_b2b"], G["d_gla_norm"], G["d_gdn_norm"],
                      G["d_alog"][:, 0:8], G["d_alog"][:, 8:16], G["d_dtb"][:, 0:8], G["d_dtb"][:, 8:16],
                      loss=0.5 * jnp.sum(G["loss_row"], axis=1, keepdims=True) / D_MODEL)
    p_small = jnp.concatenate([p_sm, jnp.broadcast_to(p_rep[None], (N_DEV,) + p_rep.shape)], axis=2)
    small_started = _exchange_start(p_small, "blocks", "exchange_small_start")
    q_in, land_in = _exchange_wait(sent["chips"], small_started[4], "chips", "exchange_in_chips_wait")
    o_in = _sum_adam(land_in, w_in_t, m_in_t, v_in_t, "adam_w_in", SHARD_IN, tc=256, own=q_in, own_slot=_my_chip)
    o_p = sent["o_p"]
    p_small, land_small = _exchange_wait(small_started, o_in[0], "blocks", "exchange_small_wait")
    rep_pack = lambda P: _pack_rep(*[P[n] for n in _REP_ORDER])
    small_pack = lambda P: jnp.concatenate([sm_pack(P), rep_pack(P)], axis=1)
    o_small = _sum_adam(land_small, small_pack(W), small_pack(M), small_pack(V), "adam_small", SM_ROWS, own=p_small, own_slot=_my_device)
    o_sm, o_rep = [o[:, :SM_COLS] for o in o_small], [o[:, SM_COLS:] for o in o_small]

    res = []
    for kind in range(4):
        rep = _unpack_rep(o_rep[kind])
        per = dict(rep)
        per["w_in"] = jnp.transpose(o_in[kind])[None]
        pp = o_p[kind].reshape(3, 1, D_MODEL // N_DEV, D_MODEL)
        per["w_proj_gdn"], per["w_proj_gla"], per["w_out"] = pp[0], pp[1], pp[2]
        per["conv_w"] = o_sm[kind][None, 0:CONV_K, :]
        per["gk_w2_fwd"] = o_sm[kind][None, 8:24, 0:64]
        per["gk_w2_bwd"] = o_sm[kind][None, 8:24, 64:128]
        res.append([per[n] for n in _WEIGHTS])
    loss = o_rep[0][LOSS_ROW, 0]
    return (loss, G["grad_x"][None], *res[0], *res[1], *res[2], *res[3])
```

```python
import functools
import math

import jax
import jax.numpy as jnp
from jax import lax
from jax.experimental import pallas as pl
from jax.experimental.pallas import tpu as pltpu

f32 = jnp.float32
MXU_DTYPE = jnp.bfloat16
HI = lax.Precision.HIGHEST

D_MODEL = 1024
CHUNK = 64
NORM_EPS = 1e-6
GDN_HEADS, GDN_DIM = 8, 128
GLA_HEADS, GLA_DK, GLA_DV = 4, 128, 256
GATE_NORMALIZER = 16.0
CONV_K = 5
N_IN = 9280
N_DEV = 8
SHARD_IN = N_IN // N_DEV

N_CAT = 9728
SMALL_OFF = 9216
LANES = 128

ADAM_LR, ADAM_B1, ADAM_B2, ADAM_EPS, ADAM_WD, ADAM_STEP = 0.001, 0.9, 0.999, 1e-08, 0.01, 10

VMEM_LIMIT = 56 * 1024 * 1024


def _cparams(sem=None):
    return pltpu.CompilerParams(dimension_semantics=sem, vmem_limit_bytes=VMEM_LIMIT)


_DN = {"nn": (((1,), (0,)), ((), ())), "nt": (((1,), (1,)), ((), ())), "tn": (((0,), (0,)), ((), ()))}
_BDN = {"nn": (((2,), (1,)), ((0,), (0,))), "nt": (((2,), (2,)), ((0,), (0,))), "tn": (((1,), (1,)), ((0,), (0,)))}


def _dot16(a, b, kind):
    return lax.dot_general(a, b, (_DN if a.ndim == 2 else _BDN)[kind], preferred_element_type=f32)


def _mx(a, b, kind):
    return _dot16(a.astype(MXU_DTYPE), b.astype(MXU_DTYPE), kind)


def _split2(a):
    hi = a.astype(jnp.bfloat16)
    return hi, (a - hi.astype(f32)).astype(jnp.bfloat16)


def _split3(a):
    p1 = a.astype(jnp.bfloat16)
    r = a - p1.astype(f32)
    p2 = r.astype(jnp.bfloat16)
    return p1, p2, (r - p2.astype(f32)).astype(jnp.bfloat16)


def _h3s(a2, b2, kind):
    (ah, al), (bh, bl) = a2, b2
    return _dot16(ah, bh, kind) + (_dot16(ah, bl, kind) + _dot16(al, bh, kind))


def _h3(a, b, kind):
    return _h3s(_split2(a), _split2(b), kind)


def _exact01(t01, x, kind="nn"):
    t = t01.astype(jnp.bfloat16)
    x1, x2, x3 = _split3(x)
    return _dot16(t, x1, kind) + (_dot16(t, x2, kind) + _dot16(t, x3, kind))


def _colsum_col(e):
    ones = jnp.ones(e.shape[:-1] + (LANES,), jnp.bfloat16)
    e1, e2, e3 = _split3(e)
    return (_dot16(e1, ones, "tn") + (_dot16(e2, ones, "tn") + _dot16(e3, ones, "tn")))[..., 0:1]


def _iota3(shape, axis):
    return lax.broadcasted_iota(jnp.int32, shape, axis)


def _dir_masks(B, nf, Cn):
    shp = (B, Cn, Cn)
    bb, ii, jj = _iota3(shp, 0), _iota3(shp, 1), _iota3(shp, 2)
    fwd = bb < nf
    rev = jnp.logical_not(fwd)
    incl = jnp.logical_or(jnp.logical_and(fwd, ii >= jj), jnp.logical_and(rev, ii <= jj))
    strict = jnp.logical_and(incl, ii != jj)
    return incl, strict, ii == jj


def _last_row(x, nf):
    B, Cn = x.shape[0], x.shape[1]
    fwd = _iota3((B, 1, 1), 0) < nf
    return jnp.where(fwd, x[:, Cn - 1 : Cn, :], x[:, 0:1, :])


def _add_at_last_row(x, val, nf):
    B, Cn = x.shape[0], x.shape[1]
    bb, rr = _iota3((B, Cn, 1), 0), _iota3((B, Cn, 1), 1)
    at = jnp.logical_or(jnp.logical_and(bb < nf, rr == Cn - 1), jnp.logical_and(bb >= nf, rr == 0))
    return x + jnp.where(at, val, 0.0)


def _sigmoid(x):
    return jax.nn.sigmoid(x)


def _silu(x):
    return x * _sigmoid(x)


def _silu_grad(x):
    s = _sigmoid(x)
    return s * (1.0 + x * (1.0 - s))


def _softplus(x):
    u = jnp.exp(-jnp.abs(x))
    l1p = jnp.where(u < 1e-3, u * (1.0 - u * (0.5 - u * (1.0 / 3.0))), jnp.log(1.0 + u))
    return jnp.maximum(x, 0.0) + l1p


def _mxr(x):
    return x.astype(MXU_DTYPE).astype(f32)


def _iota2(shape, axis):
    return lax.broadcasted_iota(jnp.int32, shape, axis)


def _matmul(a, b, kind, out_dtype, name, tm, tn, tk, out_t=False, after=None):
    extra = [] if after is None else [after]
    if kind == "nn":
        (M, K), N = a.shape, b.shape[1]
    elif kind == "nt":
        (M, K), N = a.shape, b.shape[0]
    else:
        (K, M), N = a.shape, b.shape[1]
    tm, tn, tk = min(tm, M), min(tn, N), min(tk, K)
    assert M % tm == 0 and N % tn == 0 and K % tk == 0, (name, M, N, K)
    nk = K // tk
    if kind == "tn":
        a_spec = pl.BlockSpec((tk, tm), lambda i, j, k: (k, i))
    else:
        a_spec = pl.BlockSpec((tm, tk), lambda i, j, k: (i, k))
    if kind == "nt":
        b_spec = pl.BlockSpec((tn, tk), lambda i, j, k: (j, k))
    else:
        b_spec = pl.BlockSpec((tk, tn), lambda i, j, k: (k, j))

    def body(a_ref, b_ref, *rest):
        o_ref, acc = rest[len(extra)], rest[len(extra) + 1 :]
        p = _mx(a_ref[...], b_ref[...], kind)
        if nk == 1:
            o_ref[...] = (p.T if out_t else p).astype(out_dtype)
        else:
            assert not out_t
            acc_ref = acc[0]
            k = pl.program_id(2)

            @pl.when(k == 0)
            def _():
                acc_ref[...] = p

            @pl.when(k > 0)
            def _():
                acc_ref[...] += p

            @pl.when(k == nk - 1)
            def _():
                o_ref[...] = acc_ref[...].astype(out_dtype)

    return pl.pallas_call(
        body,
        out_shape=jax.ShapeDtypeStruct((N, M) if out_t else (M, N), out_dtype),
        grid=(M // tm, N // tn, nk),
        in_specs=[a_spec, b_spec] + [pl.BlockSpec((8, LANES), lambda i, j, k: (0, 0))] * len(extra),
        out_specs=pl.BlockSpec((tn, tm), lambda i, j, k: (j, i)) if out_t else pl.BlockSpec((tm, tn), lambda i, j, k: (i, j)),
        scratch_shapes=[] if nk == 1 else [pltpu.VMEM((tm, tn), f32)],
        compiler_params=_cparams(("parallel", "parallel", "arbitrary")),
        name=name,
    )(a, b, *extra)


def _rms_pre(x, w, tb=256):
    T = x.shape[0]

    def body(x_ref, w_ref, h_ref, ht_ref):
        xv = x_ref[...]
        r = lax.rsqrt(jnp.mean(xv * xv, axis=1, keepdims=True) + NORM_EPS)
        h = xv * r * w_ref[...]
        h_ref[...] = h.astype(h_ref.dtype)
        ht_ref[...] = h.T.astype(ht_ref.dtype)

    return pl.pallas_call(
        body,
        out_shape=(jax.ShapeDtypeStruct((T, D_MODEL), MXU_DTYPE), jax.ShapeDtypeStruct((D_MODEL, T), MXU_DTYPE)),
        grid=(T // tb,),
        in_specs=[pl.BlockSpec((tb, D_MODEL), lambda i: (i, 0)), pl.BlockSpec((1, D_MODEL), lambda i: (0, 0))],
        out_specs=(pl.BlockSpec((tb, D_MODEL), lambda i: (i, 0)), pl.BlockSpec((D_MODEL, tb), lambda i: (0, i))),
        compiler_params=_cparams(("parallel",)),
        name="rms_pre",
    )(x, w)


def _rms_pre_bwd(x, w, dh, dy, tb=256):
    T = x.shape[0]

    def body(x_ref, w_ref, dh_ref, dy_ref, dx_ref, dw_ref):
        i = pl.program_id(0)
        xv, dhv = x_ref[...], dh_ref[...]
        r = lax.rsqrt(jnp.mean(xv * xv, axis=1, keepdims=True) + NORM_EPS)
        dhw = dhv * w_ref[...]
        dx_ref[...] = dy_ref[...] + r * dhw - xv * (r * r * r) * jnp.mean(dhw * xv, axis=1, keepdims=True)
        part = jnp.sum(dhv * xv * r, axis=0, keepdims=True)

        @pl.when(i == 0)
        def _():
            dw_ref[...] = part

        @pl.when(i > 0)
        def _():
            dw_ref[...] += part

    blk = pl.BlockSpec((tb, D_MODEL), lambda i: (i, 0))
    row = pl.BlockSpec((1, D_MODEL), lambda i: (0, 0))
    return pl.pallas_call(
        body,
        out_shape=(jax.ShapeDtypeStruct((T, D_MODEL), f32), jax.ShapeDtypeStruct((1, D_MODEL), f32)),
        grid=(T // tb,),
        in_specs=[blk, row, blk, blk],
        out_specs=(blk, row),
        compiler_params=_cparams(("arbitrary",)),
        name="rms_pre_bwd",
    )(x, w, dh, dy)


SHIFTS = (-2, -1, 1, 2)


def _shifted(x):
    T = x.shape[0]
    t = _iota2(x.shape, 0)
    out = {}
    for d in SHIFTS:
        valid = (t < T - d) if d > 0 else (t >= -d)
        out[d] = jnp.where(valid, pltpu.roll(x, (-d) % T, 0), 0.0)
    return out


def _conv5(u, ush, cw):
    acc = u * cw[2:3, :]
    for j in (0, 1, 3, 4):
        acc = acc + ush[j - 2] * cw[j : j + 1, :]
    return acc


def _gdn_prep(proj, conv_w):
    T = proj.shape[0]
    nh = GDN_HEADS

    def body(u_ref, cw_ref, o_ref):
        j = pl.program_id(0)
        u = _mxr(u_ref[...])
        s = _silu(_conv5(u, _shifted(u), _mxr(cw_ref[...])))

        @pl.when(j < 2 * nh)
        def _():
            o_ref[0] = s * lax.rsqrt(jnp.sum(s * s, axis=1, keepdims=True) + NORM_EPS)

        @pl.when(j >= 2 * nh)
        def _():
            o_ref[0] = s

    return pl.pallas_call(
        body,
        out_shape=jax.ShapeDtypeStruct((3, T, GDN_HEADS * GDN_DIM), f32),
        grid=(3 * nh,),
        in_specs=[pl.BlockSpec((T, GDN_DIM), lambda j: (0, j)), pl.BlockSpec((CONV_K, GDN_DIM), lambda j: (0, j))],
        out_specs=pl.BlockSpec((1, T, GDN_DIM), lambda j: (j // nh, 0, j % nh)),
        compiler_params=_cparams(("parallel",)),
        name="gdn_prep",
    )(proj, conv_w)


def _gdn_prep_bwd(proj, conv_w, dqkv_f, dqkv_b, dproj):
    T = proj.shape[0]
    nh = GDN_HEADS


    def body(u_ref, cw_ref, df_ref, db_ref, _, du_ref, dcw_ref):
        j = pl.program_id(0)
        u, cw = _mxr(u_ref[...]), _mxr(cw_ref[...])
        ush = _shifted(u)
        c = _conv5(u, ush, cw)
        s = _silu(c)
        dn = df_ref[0] + db_ref[0]
        rinv = lax.rsqrt(jnp.sum(s * s, axis=1, keepdims=True) + NORM_EPS)
        ds_norm = rinv * dn - s * (rinv * rinv * rinv) * jnp.sum(dn * s, axis=1, keepdims=True)
        ds = jnp.where(j < 2 * nh, ds_norm, dn)
        dc = _mxr(ds * _silu_grad(c))
        dcsh = _shifted(dc)
        du = dc * cw[2:3, :]
        for jj in range(CONV_K):
            d = jj - 2
            dcw_ref[jj : jj + 1, :] = jnp.sum(dc * (u if d == 0 else ush[d]), axis=0, keepdims=True)
            if d != 0:
                du = du + dcsh[-d] * cw[jj : jj + 1, :]
        du_ref[...] = du.astype(du_ref.dtype)

    qspec = pl.BlockSpec((1, T, GDN_DIM), lambda j: (j // nh, 0, j % nh))
    return pl.pallas_call(
        body,
        out_shape=(jax.ShapeDtypeStruct(dproj.shape, dproj.dtype), jax.ShapeDtypeStruct((CONV_K, 3 * nh * GDN_DIM), f32)),
        grid=(3 * nh,),
        in_specs=[pl.BlockSpec((T, GDN_DIM), lambda j: (0, j)), pl.BlockSpec((CONV_K, GDN_DIM), lambda j: (0, j)), qspec, qspec, _ANY_SPEC],
        out_specs=(pl.BlockSpec((T, GDN_DIM), lambda j: (0, j)), pl.BlockSpec((CONV_K, GDN_DIM), lambda j: (0, j))),
        input_output_aliases={4: 0},
        compiler_params=_cparams(("parallel",)),
        name="gdn_prep_bwd",
    )(proj, conv_w, dqkv_f, dqkv_b, dproj)


def _chunk_tri(n, rev):
    i, j = _iota2((n, n), 0), _iota2((n, n), 1)
    same = jnp.right_shift(i, 6) == jnp.right_shift(j, 6)
    order = (j >= i) if rev else (j <= i)
    return jnp.where(jnp.logical_and(same, order), 1.0, 0.0).astype(f32)


def _gdn_gates(proj, avec, dvec, tb=256):
    T = proj.shape[0]

    def body(s_ref, a_ref, d_ref, g_ref, gc_ref):
        small = s_ref[...]
        lane = _iota2(small.shape, 1)
        lg = -jnp.exp(a_ref[...]) * _softplus(small + d_ref[...])
        beta = _sigmoid(small)
        g_ref[...] = jnp.where(lane < 16, lg, jnp.where(lane < 32, beta, 0.0))
        lgm = jnp.where(lane < 16, lg, 0.0)
        gcf = _exact01(_chunk_tri(tb, False), lgm)
        gcr = _exact01(_chunk_tri(tb, True), lgm)
        gc_ref[...] = jnp.where(lane < 8, gcf, gcr)

    blk = pl.BlockSpec((tb, LANES), lambda i: (i, 0))
    row = pl.BlockSpec((1, LANES), lambda i: (0, 0))
    return pl.pallas_call(
        body,
        out_shape=(jax.ShapeDtypeStruct((T, LANES), f32), jax.ShapeDtypeStruct((T, LANES), f32)),
        grid=(T // tb,),
        in_specs=[pl.BlockSpec((tb, LANES), lambda i: (i, SMALL_OFF // LANES)), row, row],
        out_specs=(blk, blk),
        compiler_params=_cparams(("parallel",)),
        name="gdn_gates",
    )(proj, avec, dvec)


def _gdn_gates_bwd(proj, avec, dvec, dg_f, dg_b, dsmall_gla, dproj, tb=256):
    T = proj.shape[0]
    pad = N_CAT - SMALL_OFF

    def body(s_ref, a_ref, d_ref, gf_ref, gb_ref, dl_ref, _, ds_ref, da_ref, dd_ref):
        i = pl.program_id(0)
        small = s_ref[...]
        lane = _iota2(small.shape, 1)
        dgate = gf_ref[...] + gb_ref[...]
        z = small + d_ref[...]
        nega = -jnp.exp(a_ref[...])
        dz = dgate * nega * _sigmoid(z)
        beta = _sigmoid(small)
        dsm = jnp.where(lane < 16, dz, jnp.where(lane < 32, dgate * beta * (1.0 - beta), 0.0))
        ds_ref[:, :LANES] = (dsm + dl_ref[...]).astype(ds_ref.dtype)
        ds_ref[:, LANES:] = jnp.zeros((tb, pad - LANES), ds_ref.dtype)
        lg = nega * _softplus(z)
        pa = jnp.sum(jnp.where(lane < 16, dgate * lg, 0.0), axis=0, keepdims=True)
        pd = jnp.sum(jnp.where(lane < 16, dz, 0.0), axis=0, keepdims=True)

        @pl.when(i == 0)
        def _():
            da_ref[...] = pa
            dd_ref[...] = pd

        @pl.when(i > 0)
        def _():
            da_ref[...] += pa
            dd_ref[...] += pd

    blk = pl.BlockSpec((tb, LANES), lambda i: (i, 0))
    row = pl.BlockSpec((1, LANES), lambda i: (0, 0))
    return pl.pallas_call(
        body,
        out_shape=(jax.ShapeDtypeStruct(dproj.shape, dproj.dtype), jax.ShapeDtypeStruct((1, LANES), f32), jax.ShapeDtypeStruct((1, LANES), f32)),
        grid=(T // tb,),
        in_specs=[pl.BlockSpec((tb, LANES), lambda i: (i, SMALL_OFF // LANES)), row, row, blk, blk, blk, _ANY_SPEC],
        out_specs=(pl.BlockSpec((tb, pad), lambda i: (i, DP_SMALL_BLK)), row, row),
        input_output_aliases={6: 0},
        compiler_params=_cparams(("arbitrary",)),
        name="gdn_gates_bwd",
    )(proj, avec, dvec, dg_f, dg_b, dsmall_gla, dproj)


GDN_HB = 8


def _inv_unit_lower(L, eye):
    A = eye - L
    P2 = _split2(L)
    for _ in range(3):
        P2 = _split2(_h3s(P2, P2, "nn"))
        A = A + _h3s(_split2(A), P2, "nn")
    Ph = P2[0]
    for _ in range(2):
        Ph = _dot16(Ph, Ph, "nn").astype(jnp.bfloat16)
        A = A + _dot16(A.astype(jnp.bfloat16), Ph, "nn")
    return A


def _lane_col(blk, idx):
    lane = _iota2(blk.shape, 1)
    return jnp.sum(jnp.where(lane == idx, blk, 0.0), axis=1, keepdims=True)


class _GdnChunk:
    def __init__(self, q, k, v, gcol, beta, rev, A=None):
        Cn = q.shape[0]
        ii, jj = _iota2((Cn, Cn), 0), _iota2((Cn, Cn), 1)
        self.incl = (ii <= jj) if rev else (ii >= jj)
        self.strict = (ii < jj) if rev else (ii > jj)
        grow = jnp.broadcast_to(gcol, (Cn, LANES)).T[0:1, :]
        diff = gcol - grow
        self.decay = jnp.where(self.incl, jnp.exp(jnp.where(self.incl, diff, 0.0)), 0.0)
        self.k, self.v, self.beta, self.gcol = k, v, beta, gcol
        self.qs = q * (GDN_DIM**-0.5)
        self.kb = k * beta
        self.L = jnp.where(self.strict, _mx(self.kb, k, "nt") * self.decay, 0.0)
        if A is None:
            A = _inv_unit_lower(self.L, jnp.where(ii == jj, 1.0, 0.0).astype(f32))
        self.A2 = _split2(A)
        self.A = A
        self.eg = jnp.exp(gcol)
        self.kbg = self.kb * self.eg
        self.sol = _h3s(self.A2, _split2(jnp.concatenate([v * beta, self.kbg], axis=1)), "nn")
        self.u = self.sol[:, :GDN_DIM]
        self.w = self.sol[:, GDN_DIM:]
        self.attn = jnp.where(self.incl, _mx(self.qs, k, "nt") * self.decay, 0.0)
        self.qd = self.qs * self.eg
        last = 0 if rev else Cn - 1
        self.last = last
        gl = gcol[last : last + 1, :]
        self.ekd = jnp.exp(gl - gcol)
        self.kd = k * self.ekd
        self.egl = jnp.exp(gl)


def _gdn_scan_fwd(qkvn, gates, gc):
    T = qkvn.shape[1]
    N = T // CHUNK
    nh, hb = GDN_HEADS, GDN_HB
    W = hb * GDN_DIM
    chains = [(d, hh) for d in range(2) for hh in range(hb)]

    def body(qf_ref, qb_ref, gf_ref, gb_ref, cf_ref, cb_ref, of_ref, ob_ref, sf_ref, sb_ref, af_ref, ab_ref, s_scr):
        n, b = pl.program_id(0), pl.program_id(1)

        @pl.when(n == 0)
        def _():
            for d, hh in chains:
                s_scr[d, b * hb + hh] = jnp.zeros((GDN_DIM, GDN_DIM), f32)

        dirs = ((qf_ref, gf_ref, cf_ref, of_ref, sf_ref, af_ref), (qb_ref, gb_ref, cb_ref, ob_ref, sb_ref, ab_ref))
        s_in = {ch: s_scr[ch[0], b * hb + ch[1]] for ch in chains}
        s_out = {}
        for d, hh in chains:
            q_ref, g_ref, c_ref, o_ref, st_ref, a_ref = dirs[d]
            h = b * hb + hh
            sl = slice(hh * GDN_DIM, (hh + 1) * GDN_DIM)
            gcol = _lane_col(c_ref[...], h + 8 * d)
            beta = _lane_col(g_ref[...], 16 + h + 8 * d)
            ck = _GdnChunk(q_ref[0, :, sl], q_ref[1, :, sl], q_ref[2, :, sl], gcol, beta, d == 1)
            S = s_in[d, hh]
            st_ref[0, hh] = S
            a_ref[0, hh] = ck.A
            v_new = ck.u - _mx(ck.w, S, "nn")
            o_ref[:, sl] = _mx(ck.qd, S, "nn") + _mx(ck.attn, v_new, "nn")
            s_out[d, hh] = S * ck.egl + _mx(ck.kd, v_new, "tn")
        for d, hh in chains:
            s_scr[d, b * hb + hh] = s_out[d, hh]

    fw, bw = (lambda n: n), (lambda n: N - 1 - n)
    qsp = lambda ci: pl.BlockSpec((3, CHUNK, W), lambda n, b: (0, ci(n), b))
    gsp = lambda ci: pl.BlockSpec((CHUNK, LANES), lambda n, b: (ci(n), 0))
    osp = lambda ci: pl.BlockSpec((CHUNK, W), lambda n, b: (ci(n), b))
    ssp = lambda ci: pl.BlockSpec((1, hb, GDN_DIM, GDN_DIM), lambda n, b: (ci(n), b, 0, 0))
    asp = lambda ci: pl.BlockSpec((1, hb, CHUNK, CHUNK), lambda n, b: (ci(n), b, 0, 0))
    o_shape = jax.ShapeDtypeStruct((T, nh * GDN_DIM), f32)
    s_shape = jax.ShapeDtypeStruct((N, nh, GDN_DIM, GDN_DIM), f32)
    a_shape = jax.ShapeDtypeStruct((N, nh, CHUNK, CHUNK), f32)
    return pl.pallas_call(
        body,
        out_shape=(o_shape, o_shape, s_shape, s_shape, a_shape, a_shape),
        grid=(N, nh // hb),
        in_specs=[qsp(fw), qsp(bw), gsp(fw), gsp(bw), gsp(fw), gsp(bw)],
        out_specs=(osp(fw), osp(bw), ssp(fw), ssp(bw), asp(fw), asp(bw)),
        scratch_shapes=[pltpu.VMEM((2, nh, GDN_DIM, GDN_DIM), f32)],
        compiler_params=_cparams(("arbitrary", "arbitrary")),
        name="gdn_scan_fwd",
    )(qkvn, qkvn, gates, gates, gc, gc)


def _gdn_scan_bwd(qkvn, gates, gc, st_f, st_b, a_f, a_b, do):
    T = qkvn.shape[1]
    N = T // CHUNK
    nh, hb = GDN_HEADS, GDN_HB
    nb = nh // hb
    W = hb * GDN_DIM
    chains = [(d, hh) for d in range(2) for hh in range(hb)]

    def body(qf_ref, qb_ref, gf_ref, gb_ref, cf_ref, cb_ref, sf_ref, sb_ref, af_ref, ab_ref, dof_ref, dob_ref,
             dqf_ref, dqb_ref, dgf_ref, dgb_ref, ds_scr):
        n, b = pl.program_id(0), pl.program_id(1)

        @pl.when(n == 0)
        def _():
            for d, hh in chains:
                ds_scr[d, b * hb + hh] = jnp.zeros((GDN_DIM, GDN_DIM), f32)

        @pl.when(b == 0)
        def _():
            dgf_ref[...] = jnp.zeros((CHUNK, LANES), f32)
            dgb_ref[...] = jnp.zeros((CHUNK, LANES), f32)

        dirs = ((qf_ref, gf_ref, cf_ref, sf_ref, af_ref, dof_ref, dqf_ref), (qb_ref, gb_ref, cb_ref, sb_ref, ab_ref, dob_ref, dqb_ref))
        ds_in = {ch: ds_scr[ch[0], b * hb + ch[1]] for ch in chains}
        ds_out = {}
        lane = _iota2((CHUNK, LANES), 1)
        acc = [dgf_ref[...], dgb_ref[...]]
        for d, hh in chains:
            q_ref, g_ref, c_ref, st_ref, a_ref, do_ref, dq_ref = dirs[d]
            h = b * hb + hh
            sl = slice(hh * GDN_DIM, (hh + 1) * GDN_DIM)
            gcol = _lane_col(c_ref[...], h + 8 * d)
            beta = _lane_col(g_ref[...], 16 + h + 8 * d)
            ck = _GdnChunk(q_ref[0, :, sl], q_ref[1, :, sl], q_ref[2, :, sl], gcol, beta, d == 1, A=a_ref[0, hh])
            k, v = ck.k, ck.v
            S = st_ref[0, hh]
            dS = ds_in[d, hh]
            dov = do_ref[:, sl]
            v_new = ck.u - _mx(ck.w, S, "nn")
            d_vnew = _mx(ck.attn, dov, "tn") + _mx(ck.kd, dS, "nn")
            d_attn = jnp.where(ck.incl, _mx(dov, v_new, "nt"), 0.0)
            d_qd = _mx(dov, S, "nt")
            d_kd = _mx(v_new, dS, "nt")
            d_gl = jnp.sum(jnp.sum(dS * S, axis=1, keepdims=True), axis=0, keepdims=True) * ck.egl
            ds_out[d, hh] = dS * ck.egl + _mx(ck.qd, dov, "tn") - _mx(ck.w, d_vnew, "tn")
            d_w = -_mx(d_vnew, S, "nt")
            d_rhs = _h3s(ck.A2, _split2(jnp.concatenate([d_vnew, d_w], axis=1)), "tn")
            d_vb, d_kbg = d_rhs[:, :GDN_DIM], d_rhs[:, GDN_DIM:]
            dL = -jnp.where(ck.strict, _h3(d_rhs, ck.sol, "nt"), 0.0)
            P = dL * ck.decay
            Q = d_attn * ck.decay
            d_kb = _mx(P, k, "nn") + d_kbg * ck.eg
            d_qs = _mx(Q, k, "nn") + d_qd * ck.eg
            d_k = _mx(P, ck.kb, "tn") + _mx(Q, ck.qs, "tn") + d_kd * ck.ekd + d_kb * ck.beta
            E = dL * ck.L + d_attn * ck.attn
            kdsum = jnp.sum(d_kd * ck.kd, axis=1, keepdims=True)
            d_gc = (jnp.sum(E, axis=1, keepdims=True) - _colsum_col(E)
                    + jnp.sum(d_kbg * ck.kbg, axis=1, keepdims=True)
                    + jnp.sum(d_qd * ck.qd, axis=1, keepdims=True) - kdsum)
            d_gl = d_gl + jnp.sum(kdsum, axis=0, keepdims=True)
            row = _iota2((CHUNK, 1), 0)
            d_gc = d_gc + jnp.where(row == ck.last, d_gl, 0.0)
            d_beta = jnp.sum(d_vb * v, axis=1, keepdims=True) + jnp.sum(d_kb * k, axis=1, keepdims=True)
            dq_ref[0, :, sl] = d_qs * (GDN_DIM**-0.5)
            dq_ref[1, :, sl] = d_k
            dq_ref[2, :, sl] = d_vb * ck.beta
            acc[d] = jnp.where(lane == h + 8 * d, d_gc, acc[d])
            acc[d] = jnp.where(lane == 16 + h + 8 * d, d_beta, acc[d])
        for d, hh in chains:
            ds_scr[d, b * hb + hh] = ds_out[d, hh]
        dgf_ref[...] = acc[0]
        dgb_ref[...] = acc[1]

        @pl.when(b == nb - 1)
        def _():
            for d, dg_ref in enumerate((dgf_ref, dgb_ref)):
                blk = dg_ref[...]
                dg_ref[...] = jnp.where(lane < 16, _exact01(_chunk_tri(CHUNK, d == 0), blk), blk)

    fw, bw = (lambda n: N - 1 - n), (lambda n: n)
    qsp = lambda ci: pl.BlockSpec((3, CHUNK, W), lambda n, b: (0, ci(n), b))
    gsp = lambda ci: pl.BlockSpec((CHUNK, LANES), lambda n, b: (ci(n), 0))
    osp = lambda ci: pl.BlockSpec((CHUNK, W), lambda n, b: (ci(n), b))
    ssp = lambda ci: pl.BlockSpec((1, hb, GDN_DIM, GDN_DIM), lambda n, b: (ci(n), b, 0, 0))
    asp = lambda ci: pl.BlockSpec((1, hb, CHUNK, CHUNK), lambda n, b: (ci(n), b, 0, 0))
    dq_shape = jax.ShapeDtypeStruct((3, T, nh * GDN_DIM), f32)
    dg_shape = jax.ShapeDtypeStruct((T, LANES), f32)
    return pl.pallas_call(
        body,
        out_shape=(dq_shape, dq_shape, dg_shape, dg_shape),
        grid=(N, nb),
        in_specs=[qsp(fw), qsp(bw), gsp(fw), gsp(bw), gsp(fw), gsp(bw), ssp(fw), ssp(bw), asp(fw), asp(bw), osp(fw), osp(bw)],
        out_specs=(qsp(fw), qsp(bw), gsp(fw), gsp(bw)),
        scratch_shapes=[pltpu.VMEM((2, nh, GDN_DIM, GDN_DIM), f32)],
        compiler_params=_cparams(("arbitrary", "arbitrary")),
        name="gdn_scan_bwd",
    )(qkvn, qkvn, gates, gates, gc, gc, st_f, st_b, a_f, a_b, do, do)


def _out_norm(o_f, o_b, proj, gate_blk, w, hd, name, tb=256):
    T, W = o_f.shape
    nh = W // hd

    def body(of_ref, ob_ref, z_ref, w_ref, y_ref):
        wv = w_ref[...]
        for i in range(nh):
            sl = slice(i * hd, (i + 1) * hd)
            o = of_ref[:, sl] + ob_ref[:, sl]
            r = lax.rsqrt(jnp.mean(o * o, axis=1, keepdims=True) + NORM_EPS)
            y_ref[:, sl] = (o * r * wv * _silu(z_ref[:, sl])).astype(y_ref.dtype)

    blk = pl.BlockSpec((tb, W), lambda i: (i, 0))
    return pl.pallas_call(
        body,
        out_shape=jax.ShapeDtypeStruct((T, W), MXU_DTYPE),
        grid=(T // tb,),
        in_specs=[blk, blk, pl.BlockSpec((tb, W), lambda i: (i, gate_blk)), pl.BlockSpec((1, hd), lambda i: (0, 0))],
        out_specs=blk,
        compiler_params=_cparams(("parallel",)),
        name=name,
    )(o_f, o_b, proj, w)


def _out_norm_bwd(dy, o_f, o_b, proj, gate_blk, w, hd, name, dproj, tb=256):
    T, W = o_f.shape
    nh = W // hd

    def body(dy_ref, of_ref, ob_ref, z_ref, w_ref, _, do_ref, dz_ref, dw_ref):
        i = pl.program_id(0)
        wv = w_ref[...]
        dw = jnp.zeros((1, hd), f32)
        for a in range(nh):
            sl = slice(a * hd, (a + 1) * hd)
            o = of_ref[:, sl] + ob_ref[:, sl]
            z = z_ref[:, sl]
            dyv = dy_ref[:, sl]
            r = lax.rsqrt(jnp.mean(o * o, axis=1, keepdims=True) + NORM_EPS)
            orr = o * r
            dn = dyv * _silu(z)
            dz_ref[:, sl] = (dyv * orr * wv * _silu_grad(z)).astype(dz_ref.dtype)
            dw = dw + jnp.sum(dn * orr, axis=0, keepdims=True)
            dnw = dn * wv
            do_ref[:, sl] = r * dnw - o * (r * r * r) * jnp.mean(dnw * o, axis=1, keepdims=True)

        @pl.when(i == 0)
        def _():
            dw_ref[...] = dw

        @pl.when(i > 0)
        def _():
            dw_ref[...] += dw

    blk = pl.BlockSpec((tb, W), lambda i: (i, 0))
    row = pl.BlockSpec((1, hd), lambda i: (0, 0))
    return pl.pallas_call(
        body,
        out_shape=(jax.ShapeDtypeStruct((T, W), f32), jax.ShapeDtypeStruct(dproj.shape, dproj.dtype), jax.ShapeDtypeStruct((1, hd), f32)),
        grid=(T // tb,),
        in_specs=[blk, blk, blk, pl.BlockSpec((tb, W), lambda i: (i, gate_blk)), row, _ANY_SPEC],
        out_specs=(blk, pl.BlockSpec((tb, W), lambda i: (i, gate_blk)), row),
        input_output_aliases={5: 1},
        compiler_params=_cparams(("arbitrary",)),
        name=name,
    )(dy, o_f, o_b, proj, w, dproj)


def _gla_prep(proj, w2f, w2b, b2f, b2b, tb=256):
    T = proj.shape[0]
    W = GLA_HEADS * GLA_DK

    def body(s_ref, wf_ref, wb_ref, bf_ref, bb_ref, gf_ref, gb_ref):
        small = s_ref[...]
        gkf = -_softplus(-(_mx(small, wf_ref[...], "nn") + bf_ref[...])) * (1.0 / GATE_NORMALIZER)
        gkb = -_softplus(-(_mx(small, wb_ref[...], "nn") + bb_ref[...])) * (1.0 / GATE_NORMALIZER)
        gf_ref[...] = _exact01(_chunk_tri(tb, False), gkf)
        gb_ref[...] = _exact01(_chunk_tri(tb, True), gkb)

    blk = pl.BlockSpec((tb, W), lambda i: (i, 0))
    wsp = pl.BlockSpec((LANES, W), lambda i: (0, 0))
    row = pl.BlockSpec((1, W), lambda i: (0, 0))
    return pl.pallas_call(
        body,
        out_shape=(jax.ShapeDtypeStruct((T, W), f32), jax.ShapeDtypeStruct((T, W), f32)),
        grid=(T // tb,),
        in_specs=[pl.BlockSpec((tb, LANES), lambda i: (i, SMALL_OFF // LANES)), wsp, wsp, row, row],
        out_specs=(blk, blk),
        compiler_params=_cparams(("parallel",)),
        name="gla_prep",
    )(proj, w2f, w2b, b2f, b2b)


def _gla_prep_bwd(proj, w2f, w2b, b2f, b2b, dgk_f, dgk_b, tb=256):
    T = proj.shape[0]
    W = GLA_HEADS * GLA_DK

    def body(s_ref, wf_ref, wb_ref, bf_ref, bb_ref, df_ref, db_ref, ds_ref, dwf_ref, dwb_ref, dbf_ref, dbb_ref):
        i = pl.program_id(0)
        small = s_ref[...]
        dsm = jnp.zeros((tb, LANES), f32)
        parts = []
        for w_ref, b_ref, d_ref in ((wf_ref, bf_ref, df_ref), (wb_ref, bb_ref, db_ref)):
            pre = _mx(small, w_ref[...], "nn") + b_ref[...]
            dpre = d_ref[...] * (1.0 / GATE_NORMALIZER) * _sigmoid(-pre)
            dsm = dsm + _mx(dpre, w_ref[...], "nt")
            parts.append((_mx(small, dpre, "tn"), jnp.sum(dpre, axis=0, keepdims=True)))
        ds_ref[...] = dsm

        @pl.when(i == 0)
        def _():
            dwf_ref[...], dbf_ref[...] = parts[0]
            dwb_ref[...], dbb_ref[...] = parts[1]

        @pl.when(i > 0)
        def _():
            dwf_ref[...] += parts[0][0]
            dbf_ref[...] += parts[0][1]
            dwb_ref[...] += parts[1][0]
            dbb_ref[...] += parts[1][1]

    blk = pl.BlockSpec((tb, W), lambda i: (i, 0))
    wsp = pl.BlockSpec((LANES, W), lambda i: (0, 0))
    row = pl.BlockSpec((1, W), lambda i: (0, 0))
    return pl.pallas_call(
        body,
        out_shape=(jax.ShapeDtypeStruct((T, LANES), f32), jax.ShapeDtypeStruct((LANES, W), f32), jax.ShapeDtypeStruct((LANES, W), f32),
                   jax.ShapeDtypeStruct((1, W), f32), jax.ShapeDtypeStruct((1, W), f32)),
        grid=(T // tb,),
        in_specs=[pl.BlockSpec((tb, LANES), lambda i: (i, SMALL_OFF // LANES)), wsp, wsp, row, row, blk, blk],
        out_specs=(pl.BlockSpec((tb, LANES), lambda i: (i, 0)), wsp, wsp, row, row),
        compiler_params=_cparams(("arbitrary",)),
        name="gla_prep_bwd",
    )(proj, w2f, w2b, b2f, b2b, dgk_f, dgk_b)


Q_B_BLK, K_B_BLK, V_B_BLK = 4096 // GLA_DK, 4608 // GLA_DK, 5120 // GLA_DV


class _GlaChunk:
    def __init__(self, q, k, G, rev):
        Cn = q.shape[0]
        ii, jj = _iota2((Cn, Cn), 0), _iota2((Cn, Cn), 1)
        self.incl = (ii <= jj) if rev else (ii >= jj)
        self.eG = jnp.exp(G)
        self.enG = jnp.exp(-G)
        self.qg = q * (GLA_DK**-0.5) * self.eG
        self.kg = k * self.enG
        self.attn = jnp.where(self.incl, _mx(self.qg, self.kg, "nt"), 0.0)
        last = 0 if rev else Cn - 1
        self.last = last
        gl = G[last : last + 1, :]
        self.ekd = jnp.exp(gl - G)
        self.kd = k * self.ekd
        self.egl = jnp.exp(gl)


def _gla_scan_fwd(proj, G_f, G_b):
    T = proj.shape[0]
    N = T // CHUNK
    nh = GLA_HEADS

    def body(qf_ref, kf_ref, vf_ref, gf_ref, qb_ref, kb_ref, vb_ref, gb_ref, of_ref, ob_ref, sf_ref, sb_ref, s_scr):
        n, h = pl.program_id(0), pl.program_id(1)

        @pl.when(n == 0)
        def _():
            s_scr[0, h] = jnp.zeros((GLA_DV, GLA_DK), f32)
            s_scr[1, h] = jnp.zeros((GLA_DV, GLA_DK), f32)

        for d, (q_ref, k_ref, v_ref, g_ref, o_ref, st_ref) in enumerate(
            ((qf_ref, kf_ref, vf_ref, gf_ref, of_ref, sf_ref), (qb_ref, kb_ref, vb_ref, gb_ref, ob_ref, sb_ref))
        ):
            ck = _GlaChunk(q_ref[...], k_ref[...], g_ref[...], d == 1)
            v = v_ref[...]
            St = s_scr[d, h]
            st_ref[0, 0] = St
            o_ref[...] = _mx(ck.qg, St, "nt") + _mx(ck.attn, v, "nn")
            s_scr[d, h] = St * ck.egl + _mx(v, ck.kd, "tn")

    def specs(cidx):
        return [
            pl.BlockSpec((CHUNK, GLA_DK), lambda n, h: (cidx(n), Q_B_BLK + h)),
            pl.BlockSpec((CHUNK, GLA_DK), lambda n, h: (cidx(n), K_B_BLK + h)),
            pl.BlockSpec((CHUNK, GLA_DV), lambda n, h: (cidx(n), V_B_BLK + h)),
            pl.BlockSpec((CHUNK, GLA_DK), lambda n, h: (cidx(n), h)),
        ]

    fw, bw = (lambda n: n), (lambda n: N - 1 - n)
    o_shape = jax.ShapeDtypeStruct((T, nh * GLA_DV), f32)
    s_shape = jax.ShapeDtypeStruct((N, nh, GLA_DV, GLA_DK), f32)
    return pl.pallas_call(
        body,
        out_shape=(o_shape, o_shape, s_shape, s_shape),
        grid=(N, nh),
        in_specs=specs(fw) + specs(bw),
        out_specs=(
            pl.BlockSpec((CHUNK, GLA_DV), lambda n, h: (n, h)),
            pl.BlockSpec((CHUNK, GLA_DV), lambda n, h: (N - 1 - n, h)),
            pl.BlockSpec((1, 1, GLA_DV, GLA_DK), lambda n, h: (n, h, 0, 0)),
            pl.BlockSpec((1, 1, GLA_DV, GLA_DK), lambda n, h: (N - 1 - n, h, 0, 0)),
        ),
        scratch_shapes=[pltpu.VMEM((2, nh, GLA_DV, GLA_DK), f32)],
        compiler_params=_cparams(("arbitrary", "arbitrary")),
        name="gla_scan_fwd",
    )(proj, proj, proj, G_f, proj, proj, proj, G_b)


def _gla_scan_bwd(proj, G_f, G_b, st_f, st_b, do):
    T = proj.shape[0]
    N = T // CHUNK
    nh = GLA_HEADS

    def body(qf_ref, kf_ref, vf_ref, gf_ref, sf_ref, dof_ref, qb_ref, kb_ref, vb_ref, gb_ref, sb_ref, dob_ref,
             dqf_ref, dkf_ref, dvf_ref, dgf_ref, dqb_ref, dkb_ref, dvb_ref, dgb_ref, ds_scr):
        n, h = pl.program_id(0), pl.program_id(1)

        @pl.when(n == 0)
        def _():
            ds_scr[0, h] = jnp.zeros((GLA_DV, GLA_DK), f32)
            ds_scr[1, h] = jnp.zeros((GLA_DV, GLA_DK), f32)

        for d, (q_ref, k_ref, v_ref, g_ref, st_ref, do_ref, dq_ref, dk_ref, dv_ref, dg_ref) in enumerate(
            ((qf_ref, kf_ref, vf_ref, gf_ref, sf_ref, dof_ref, dqf_ref, dkf_ref, dvf_ref, dgf_ref),
             (qb_ref, kb_ref, vb_ref, gb_ref, sb_ref, dob_ref, dqb_ref, dkb_ref, dvb_ref, dgb_ref))
        ):
            rev = d == 1
            ck = _GlaChunk(q_ref[...], k_ref[...], g_ref[...], rev)
            v = v_ref[...]
            St = st_ref[0, 0]
            dSt = ds_scr[d, h]
            dov = do_ref[...]
            d_attn = jnp.where(ck.incl, _mx(dov, v, "nt"), 0.0)
            d_qg = _mx(dov, St, "nn") + _mx(d_attn, ck.kg, "nn")
            d_kg = _mx(d_attn, ck.qg, "tn")
            dv_ref[...] = _mx(ck.attn, dov, "tn") + _mx(ck.kd, dSt, "nt")
            d_kd = _mx(v, dSt, "nn")
            ds_scr[d, h] = dSt * ck.egl + _mx(dov, ck.qg, "tn")
            kdd = d_kd * ck.kd
            d_gl = jnp.sum(dSt * St, axis=0, keepdims=True) * ck.egl + jnp.sum(kdd, axis=0, keepdims=True)
            dq_ref[...] = d_qg * ck.eG * (GLA_DK**-0.5)
            dk_ref[...] = d_kg * ck.enG + d_kd * ck.ekd
            d_G = d_qg * ck.qg - d_kg * ck.kg - kdd
            row = _iota2((CHUNK, GLA_DK), 0)
            d_G = d_G + jnp.where(row == ck.last, d_gl, 0.0)
            dg_ref[...] = _exact01(_chunk_tri(CHUNK, not rev), d_G)

    def in_specs(cidx):
        return [
            pl.BlockSpec((CHUNK, GLA_DK), lambda n, h: (cidx(n), Q_B_BLK + h)),
            pl.BlockSpec((CHUNK, GLA_DK), lambda n, h: (cidx(n), K_B_BLK + h)),
            pl.BlockSpec((CHUNK, GLA_DV), lambda n, h: (cidx(n), V_B_BLK + h)),
            pl.BlockSpec((CHUNK, GLA_DK), lambda n, h: (cidx(n), h)),
            pl.BlockSpec((1, 1, GLA_DV, GLA_DK), lambda n, h: (cidx(n), h, 0, 0)),
            pl.BlockSpec((CHUNK, GLA_DV), lambda n, h: (cidx(n), h)),
        ]

    def out_specs(cidx):
        return [
            pl.BlockSpec((CHUNK, GLA_DK), lambda n, h: (cidx(n), h)),
            pl.BlockSpec((CHUNK, GLA_DK), lambda n, h: (cidx(n), h)),
            pl.BlockSpec((CHUNK, GLA_DV), lambda n, h: (cidx(n), h)),
            pl.BlockSpec((CHUNK, GLA_DK), lambda n, h: (cidx(n), h)),
        ]

    fw, bw = (lambda n: N - 1 - n), (lambda n: n)
    k_shape = jax.ShapeDtypeStruct((T, nh * GLA_DK), f32)
    v_shape = jax.ShapeDtypeStruct((T, nh * GLA_DV), f32)
    return pl.pallas_call(
        body,
        out_shape=(k_shape, k_shape, v_shape, k_shape) * 2,
        grid=(N, nh),
        in_specs=in_specs(fw) + in_specs(bw),
        out_specs=tuple(out_specs(fw) + out_specs(bw)),
        scratch_shapes=[pltpu.VMEM((2, nh, GLA_DV, GLA_DK), f32)],
        compiler_params=_cparams(("arbitrary", "arbitrary")),
        name="gla_scan_bwd",
    )(proj, proj, proj, G_f, st_f, do, proj, proj, proj, G_b, st_b, do)


GATE_A_BLK, GATE_B_BLK, Z_BLK, G_B_BLK = 6144 // 1024, 7168 // 1024, 3072 // 1024, 8192 // 1024
DP_GATES_BLK, DP_GLA_QKV_BLK, DP_SMALL_BLK = 6144 // 2048, 4096 // 2048, SMALL_OFF // 512
_ANY_SPEC = pl.BlockSpec(memory_space=pl.ANY)


def _mid(og, obg, proj, x, target, wpg, wpl, wout, w_post, tb=256):
    T = x.shape[0]
    Dm = D_MODEL

    def body(og_ref, obg_ref, ga_ref, gb_ref, x_ref, t_ref, wpg_ref, wpl_ref, wo_ref, wp_ref,
             loss_ref, dwp_ref, dy_ref, m_ref, dout_ref, dya_ref, dyb_ref, dgates_ref, dog_ref, dobg_ref):
        i = pl.program_id(0)
        ya = _mx(og_ref[...], wpg_ref[...], "nn")
        yb = _mx(obg_ref[...], wpl_ref[...], "nn")
        sa, sb = _sigmoid(ga_ref[...]), _sigmoid(gb_ref[...])
        m = sa * ya + sb * yb
        m_ref[...] = m.astype(m_ref.dtype)
        out = _mx(m, wo_ref[...], "nn")
        r = lax.rsqrt(jnp.mean(out * out, axis=1, keepdims=True) + NORM_EPS)
        wp = wp_ref[...]
        e = x_ref[...] + out * r * wp - t_ref[...]
        dy = e * (1.0 / Dm)
        dy_ref[...] = dy
        lpart = jnp.sum(e * e, axis=0, keepdims=True)
        wpart = jnp.sum(dy * out * r, axis=0, keepdims=True)
        dyn = dy * wp
        dout = r * dyn - out * (r * r * r) * jnp.mean(dyn * out, axis=1, keepdims=True)
        dout_ref[...] = dout.astype(dout_ref.dtype)
        dm = _mx(dout, wo_ref[...], "nt")
        dya, dyb = dm * sa, dm * sb
        dya_ref[...] = dya.astype(dya_ref.dtype)
        dyb_ref[...] = dyb.astype(dyb_ref.dtype)
        dgates_ref[:, :Dm] = (dm * ya * sa * (1.0 - sa)).astype(dgates_ref.dtype)
        dgates_ref[:, Dm:] = (dm * yb * sb * (1.0 - sb)).astype(dgates_ref.dtype)
        dog_ref[...] = _mx(dya, wpg_ref[...], "nt")
        dobg_ref[...] = _mx(dyb, wpl_ref[...], "nt")

        @pl.when(i == 0)
        def _():
            loss_ref[...] = lpart
            dwp_ref[...] = wpart

        @pl.when(i > 0)
        def _():
            loss_ref[...] += lpart
            dwp_ref[...] += wpart

    blk = pl.BlockSpec((tb, Dm), lambda i: (i, 0))
    row = pl.BlockSpec((1, Dm), lambda i: (0, 0))
    wsp = pl.BlockSpec((Dm, Dm), lambda i: (0, 0))
    act = jax.ShapeDtypeStruct((T, Dm), MXU_DTYPE)
    big = jax.ShapeDtypeStruct((T, Dm), f32)
    vec = jax.ShapeDtypeStruct((1, Dm), f32)
    return pl.pallas_call(
        body,
        out_shape=(vec, vec, big, act, act, act, act, jax.ShapeDtypeStruct((T, N_CAT), MXU_DTYPE), big, big),
        grid=(T // tb,),
        in_specs=[blk, blk, pl.BlockSpec((tb, Dm), lambda i: (i, GATE_A_BLK)), pl.BlockSpec((tb, Dm), lambda i: (i, GATE_B_BLK)),
                  blk, blk, wsp, wsp, wsp, row],
        out_specs=(row, row) + (blk,) * 5 + (pl.BlockSpec((tb, 2 * Dm), lambda i: (i, DP_GATES_BLK)), blk, blk),
        compiler_params=_cparams(("arbitrary",)),
        name="mid",
    )(og, obg, proj, proj, x, target, wpg, wpl, wout, w_post)


def _heads(ref, lead, n, width):
    return [ref[lead + (slice(None), slice(i * width, (i + 1) * width))] for i in range(n)]


class _GdnBatch:
    def __init__(self, q, k, v, gcol, grow, beta, nf, A=None):
        B, Cn = q.shape[0], q.shape[1]
        self.nf = nf
        self.incl, self.strict, eye = _dir_masks(B, nf, Cn)
        self.decay = jnp.where(self.incl, jnp.exp(jnp.where(self.incl, gcol - grow, 0.0)), 0.0)
        self.k, self.v, self.beta = k, v, beta
        self.qs = q * (GDN_DIM**-0.5)
        self.kb = k * beta
        self.L = jnp.where(self.strict, _mx(self.kb, k, "nt") * self.decay, 0.0)
        if A is None:
            A = _inv_unit_lower(self.L, jnp.where(eye, 1.0, 0.0).astype(f32))
        self.A = A
        self.A2 = _split2(A)
        self.eg = jnp.exp(gcol)
        self.kbg = self.kb * self.eg
        self.sol = _h3s(self.A2, _split2(jnp.concatenate([v * beta, self.kbg], axis=2)), "nn")
        self.u = self.sol[:, :, :GDN_DIM]
        self.w = self.sol[:, :, GDN_DIM:]
        self.attn = jnp.where(self.incl, _mx(self.qs, k, "nt") * self.decay, 0.0)
        self.qd = self.qs * self.eg
        gl = _last_row(gcol, nf)
        self.ekd = jnp.exp(gl - gcol)
        self.kd = k * self.ekd
        self.egl = jnp.exp(gl)


def _gdn_chain_inputs(qf_ref, qb_ref, gf_ref, gb_ref, cf_ref, cb_ref, b, hb):
    qkv = [jnp.stack(_heads(qf_ref, (i,), hb, GDN_DIM) + _heads(qb_ref, (i,), hb, GDN_DIM)) for i in range(3)]
    cols, rows, betas = [], [], []
    for d, (g_ref, c_ref) in enumerate(((gf_ref, cf_ref), (gb_ref, cb_ref))):
        gates, gc = g_ref[...], c_ref[...]
        for hh in range(hb):
            h = b * hb + hh
            col = _lane_col(gc, h + 8 * d)
            cols.append(col)
            rows.append(jnp.broadcast_to(col, (CHUNK, LANES)).T[0:1, :])
            betas.append(_lane_col(gates, 16 + h + 8 * d))
    return qkv[0], qkv[1], qkv[2], jnp.stack(cols), jnp.stack(rows), jnp.stack(betas)


def _gdn_fwd(qkvn, gates, gc):
    T = qkvn.shape[1]
    N = T // CHUNK
    nh, hb = GDN_HEADS, GDN_HB
    W = hb * GDN_DIM

    def body(qf_ref, qb_ref, gf_ref, gb_ref, cf_ref, cb_ref, of_ref, ob_ref, sf_ref, sb_ref, af_ref, ab_ref, s_scr):
        n, b = pl.program_id(0), pl.program_id(1)
        hs = pl.ds(b * hb, hb)

        @pl.when(n == 0)
        def _():
            s_scr[0, hs] = jnp.zeros((hb, GDN_DIM, GDN_DIM), f32)
            s_scr[1, hs] = jnp.zeros((hb, GDN_DIM, GDN_DIM), f32)

        q, k, v, gcol, grow, beta = _gdn_chain_inputs(qf_ref, qb_ref, gf_ref, gb_ref, cf_ref, cb_ref, b, hb)
        ck = _GdnBatch(q, k, v, gcol, grow, beta, hb)
        S = jnp.concatenate([s_scr[0, hs], s_scr[1, hs]], axis=0)
        sf_ref[0], sb_ref[0] = S[:hb], S[hb:]
        af_ref[0], ab_ref[0] = ck.A[:hb], ck.A[hb:]
        v_new = ck.u - _mx(ck.w, S, "nn")
        o = _mx(ck.qd, S, "nn") + _mx(ck.attn, v_new, "nn")
        S_new = S * ck.egl + _mx(ck.kd, v_new, "tn")
        for hh in range(hb):
            sl = slice(hh * GDN_DIM, (hh + 1) * GDN_DIM)
            of_ref[:, sl] = o[hh]
            ob_ref[:, sl] = o[hb + hh]
        s_scr[0, hs] = S_new[:hb]
        s_scr[1, hs] = S_new[hb:]

    fw, bw = (lambda n: n), (lambda n: N - 1 - n)
    qsp = lambda ci: pl.BlockSpec((3, CHUNK, W), lambda n, b: (0, ci(n), b))
    gsp = lambda ci: pl.BlockSpec((CHUNK, LANES), lambda n, b: (ci(n), 0))
    osp = lambda ci: pl.BlockSpec((CHUNK, W), lambda n, b: (ci(n), b))
    ssp = lambda ci: pl.BlockSpec((1, hb, GDN_DIM, GDN_DIM), lambda n, b: (ci(n), b, 0, 0))
    asp = lambda ci: pl.BlockSpec((1, hb, CHUNK, CHUNK), lambda n, b: (ci(n), b, 0, 0))
    o_shape = jax.ShapeDtypeStruct((T, nh * GDN_DIM), f32)
    s_shape = jax.ShapeDtypeStruct((N, nh, GDN_DIM, GDN_DIM), f32)
    a_shape = jax.ShapeDtypeStruct((N, nh, CHUNK, CHUNK), f32)
    return pl.pallas_call(
        body,
        out_shape=(o_shape, o_shape, s_shape, s_shape, a_shape, a_shape),
        grid=(N, nh // hb),
        in_specs=[qsp(fw), qsp(bw), gsp(fw), gsp(bw), gsp(fw), gsp(bw)],
        out_specs=(osp(fw), osp(bw), ssp(fw), ssp(bw), asp(fw), asp(bw)),
        scratch_shapes=[pltpu.VMEM((2, nh, GDN_DIM, GDN_DIM), f32)],
        compiler_params=_cparams(("arbitrary", "arbitrary")),
        name="gdn_scan_fwd",
    )(qkvn, qkvn, gates, gates, gc, gc)


def _gdn_bwd(qkvn, gates, gc, st_f, st_b, a_f, a_b, do):
    T = qkvn.shape[1]
    N = T // CHUNK
    nh, hb = GDN_HEADS, GDN_HB
    nb = nh // hb
    W = hb * GDN_DIM

    def body(qf_ref, qb_ref, gf_ref, gb_ref, cf_ref, cb_ref, sf_ref, sb_ref, af_ref, ab_ref, dof_ref, dob_ref,
             dqf_ref, dqb_ref, dgf_ref, dgb_ref, ds_scr):
        n, b = pl.program_id(0), pl.program_id(1)
        hs = pl.ds(b * hb, hb)

        @pl.when(n == 0)
        def _():
            ds_scr[0, hs] = jnp.zeros((hb, GDN_DIM, GDN_DIM), f32)
            ds_scr[1, hs] = jnp.zeros((hb, GDN_DIM, GDN_DIM), f32)

        @pl.when(b == 0)
        def _():
            dgf_ref[...] = jnp.zeros((CHUNK, LANES), f32)
            dgb_ref[...] = jnp.zeros((CHUNK, LANES), f32)

        q, k, v, gcol, grow, beta = _gdn_chain_inputs(qf_ref, qb_ref, gf_ref, gb_ref, cf_ref, cb_ref, b, hb)
        ck = _GdnBatch(q, k, v, gcol, grow, beta, hb, A=jnp.concatenate([af_ref[0], ab_ref[0]], axis=0))
        S = jnp.concatenate([sf_ref[0], sb_ref[0]], axis=0)
        dS = jnp.concatenate([ds_scr[0, hs], ds_scr[1, hs]], axis=0)
        dov = jnp.stack(_heads(dof_ref, (), hb, GDN_DIM) + _heads(dob_ref, (), hb, GDN_DIM))
        v_new = ck.u - _mx(ck.w, S, "nn")
        d_vnew = _mx(ck.attn, dov, "tn") + _mx(ck.kd, dS, "nn")
        d_attn = jnp.where(ck.incl, _mx(dov, v_new, "nt"), 0.0)
        d_qd = _mx(dov, S, "nt")
        d_kd = _mx(v_new, dS, "nt")
        d_gl = jnp.sum(jnp.sum(dS * S, axis=2, keepdims=True), axis=1, keepdims=True) * ck.egl
        dS_new = dS * ck.egl + _mx(ck.qd, dov, "tn") - _mx(ck.w, d_vnew, "tn")
        ds_scr[0, hs] = dS_new[:hb]
        ds_scr[1, hs] = dS_new[hb:]
        d_w = -_mx(d_vnew, S, "nt")
        d_rhs = _h3s(ck.A2, _split2(jnp.concatenate([d_vnew, d_w], axis=2)), "tn")
        d_vb, d_kbg = d_rhs[:, :, :GDN_DIM], d_rhs[:, :, GDN_DIM:]
        dL = -jnp.where(ck.strict, _h3(d_rhs, ck.sol, "nt"), 0.0)
        P = dL * ck.decay
        Q = d_attn * ck.decay
        d_kb = _mx(P, k, "nn") + d_kbg * ck.eg
        d_qs = _mx(Q, k, "nn") + d_qd * ck.eg
        d_k = _mx(P, ck.kb, "tn") + _mx(Q, ck.qs, "tn") + d_kd * ck.ekd + d_kb * ck.beta
        E = dL * ck.L + d_attn * ck.attn
        kdsum = jnp.sum(d_kd * ck.kd, axis=2, keepdims=True)
        d_gc = (jnp.sum(E, axis=2, keepdims=True) - _colsum_col(E)
                + jnp.sum(d_kbg * ck.kbg, axis=2, keepdims=True)
                + jnp.sum(d_qd * ck.qd, axis=2, keepdims=True) - kdsum)
        d_gc = _add_at_last_row(d_gc, d_gl + jnp.sum(kdsum, axis=1, keepdims=True), hb)
        d_beta = jnp.sum(d_vb * v, axis=2, keepdims=True) + jnp.sum(d_kb * k, axis=2, keepdims=True)
        d_q = d_qs * (GDN_DIM**-0.5)
        d_v = d_vb * ck.beta
        lane = _iota2((CHUNK, LANES), 1)
        for d, (dq_ref, dg_ref) in enumerate(((dqf_ref, dgf_ref), (dqb_ref, dgb_ref))):
            acc = dg_ref[...]
            for hh in range(hb):
                c, h = d * hb + hh, b * hb + hh
                sl = slice(hh * GDN_DIM, (hh + 1) * GDN_DIM)
                dq_ref[0, :, sl] = d_q[c]
                dq_ref[1, :, sl] = d_k[c]
                dq_ref[2, :, sl] = d_v[c]
                acc = jnp.where(lane == h + 8 * d, d_gc[c], acc)
                acc = jnp.where(lane == 16 + h + 8 * d, d_beta[c], acc)
            dg_ref[...] = acc

        @pl.when(b == nb - 1)
        def _():
            for d, dg_ref in enumerate((dgf_ref, dgb_ref)):
                blk = dg_ref[...]
                dg_ref[...] = jnp.where(lane < 16, _exact01(_chunk_tri(CHUNK, d == 0), blk), blk)

    fw, bw = (lambda n: N - 1 - n), (lambda n: n)
    qsp = lambda ci: pl.BlockSpec((3, CHUNK, W), lambda n, b: (0, ci(n), b))
    gsp = lambda ci: pl.BlockSpec((CHUNK, LANES), lambda n, b: (ci(n), 0))
    osp = lambda ci: pl.BlockSpec((CHUNK, W), lambda n, b: (ci(n), b))
    ssp = lambda ci: pl.BlockSpec((1, hb, GDN_DIM, GDN_DIM), lambda n, b: (ci(n), b, 0, 0))
    asp = lambda ci: pl.BlockSpec((1, hb, CHUNK, CHUNK), lambda n, b: (ci(n), b, 0, 0))
    dq_shape = jax.ShapeDtypeStruct((3, T, nh * GDN_DIM), f32)
    dg_shape = jax.ShapeDtypeStruct((T, LANES), f32)
    return pl.pallas_call(
        body,
        out_shape=(dq_shape, dq_shape, dg_shape, dg_shape),
        grid=(N, nb),
        in_specs=[qsp(fw), qsp(bw), gsp(fw), gsp(bw), gsp(fw), gsp(bw), ssp(fw), ssp(bw), asp(fw), asp(bw), osp(fw), osp(bw)],
        out_specs=(qsp(fw), qsp(bw), gsp(fw), gsp(bw)),
        scratch_shapes=[pltpu.VMEM((2, nh, GDN_DIM, GDN_DIM), f32)],
        compiler_params=_cparams(("arbitrary", "arbitrary")),
        name="gdn_scan_bwd",
    )(qkvn, qkvn, gates, gates, gc, gc, st_f, st_b, a_f, a_b, do, do)


class _GlaBatch:
    def __init__(self, q, k, G, nf):
        B, Cn = q.shape[0], q.shape[1]
        self.incl, _, _ = _dir_masks(B, nf, Cn)
        self.eG = jnp.exp(G)
        self.enG = jnp.exp(-G)
        self.qg = q * (GLA_DK**-0.5) * self.eG
        self.kg = k * self.enG
        self.attn = jnp.where(self.incl, _mx(self.qg, self.kg, "nt"), 0.0)
        gl = _last_row(G, nf)
        self.ekd = jnp.exp(gl - G)
        self.kd = k * self.ekd
        self.egl = jnp.exp(gl)


def _gla_chain_inputs(refs_f, refs_b):
    nh = GLA_HEADS
    out = []
    for i, width in enumerate((GLA_DK, GLA_DK, GLA_DV, GLA_DK)):
        out.append(jnp.stack(_heads(refs_f[i], (), nh, width) + _heads(refs_b[i], (), nh, width)))
    return out


def _gla_specs(N, ci):
    nh = GLA_HEADS
    return [
        pl.BlockSpec((CHUNK, nh * GLA_DK), lambda n: (ci(n), 4096 // (nh * GLA_DK))),
        pl.BlockSpec((CHUNK, nh * GLA_DK), lambda n: (ci(n), 4608 // (nh * GLA_DK))),
        pl.BlockSpec((CHUNK, nh * GLA_DV), lambda n: (ci(n), 5120 // (nh * GLA_DV))),
        pl.BlockSpec((CHUNK, nh * GLA_DK), lambda n: (ci(n), 0)),
    ]


def _gla_fwd(proj, G_f, G_b):
    T = proj.shape[0]
    N = T // CHUNK
    nh = GLA_HEADS

    def body(qf, kf, vf, gf, qb, kb, vb, gb, of_ref, ob_ref, sf_ref, sb_ref, s_scr):
        n = pl.program_id(0)

        @pl.when(n == 0)
        def _():
            s_scr[...] = jnp.zeros((2 * nh, GLA_DV, GLA_DK), f32)

        q, k, v, G = _gla_chain_inputs((qf, kf, vf, gf), (qb, kb, vb, gb))
        ck = _GlaBatch(q, k, G, nh)
        St = s_scr[...]
        sf_ref[0], sb_ref[0] = St[:nh].astype(sf_ref.dtype), St[nh:].astype(sb_ref.dtype)
        o = _mx(ck.qg, St, "nt") + _mx(ck.attn, v, "nn")
        for i in range(nh):
            sl = slice(i * GLA_DV, (i + 1) * GLA_DV)
            of_ref[:, sl] = o[i]
            ob_ref[:, sl] = o[nh + i]
        s_scr[...] = St * ck.egl + _mx(v, ck.kd, "tn")

    fw, bw = (lambda n: n), (lambda n: N - 1 - n)
    osp = lambda ci: pl.BlockSpec((CHUNK, nh * GLA_DV), lambda n: (ci(n), 0))
    ssp = lambda ci: pl.BlockSpec((1, nh, GLA_DV, GLA_DK), lambda n: (ci(n), 0, 0, 0))
    o_shape = jax.ShapeDtypeStruct((T, nh * GLA_DV), f32)
    s_shape = jax.ShapeDtypeStruct((N, nh, GLA_DV, GLA_DK), MXU_DTYPE)
    return pl.pallas_call(
        body,
        out_shape=(o_shape, o_shape, s_shape, s_shape),
        grid=(N,),
        in_specs=_gla_specs(N, fw) + _gla_specs(N, bw),
        out_specs=(osp(fw), osp(bw), ssp(fw), ssp(bw)),
        scratch_shapes=[pltpu.VMEM((2 * nh, GLA_DV, GLA_DK), f32)],
        compiler_params=_cparams(("arbitrary",)),
        name="gla_scan_fwd",
    )(proj, proj, proj, G_f, proj, proj, proj, G_b)


def _gla_bwd(proj, G_f, G_b, st_f, st_b, do):
    T = proj.shape[0]
    N = T // CHUNK
    nh = GLA_HEADS

    def body(qf, kf, vf, gf, sf_ref, dof_ref, qb, kb, vb, gb, sb_ref, dob_ref,
             dqf_ref, dkf_ref, dvf_ref, dgf_ref, dqb_ref, dkb_ref, dvb_ref, dgb_ref, ds_scr):
        n = pl.program_id(0)

        @pl.when(n == 0)
        def _():
            ds_scr[...] = jnp.zeros((2 * nh, GLA_DV, GLA_DK), f32)

        q, k, v, G = _gla_chain_inputs((qf, kf, vf, gf), (qb, kb, vb, gb))
        ck = _GlaBatch(q, k, G, nh)
        St = jnp.concatenate([sf_ref[0], sb_ref[0]], axis=0).astype(f32)
        dSt = ds_scr[...]
        dov = jnp.stack(_heads(dof_ref, (), nh, GLA_DV) + _heads(dob_ref, (), nh, GLA_DV))
        d_attn = jnp.where(ck.incl, _mx(dov, v, "nt"), 0.0)
        d_qg = _mx(dov, St, "nn") + _mx(d_attn, ck.kg, "nn")
        d_kg = _mx(d_attn, ck.qg, "tn")
        d_v = _mx(ck.attn, dov, "tn") + _mx(ck.kd, dSt, "nt")
        d_kd = _mx(v, dSt, "nn")
        ds_scr[...] = dSt * ck.egl + _mx(dov, ck.qg, "tn")
        kdd = d_kd * ck.kd
        d_gl = jnp.sum(dSt * St, axis=1, keepdims=True) * ck.egl + jnp.sum(kdd, axis=1, keepdims=True)
        d_q = d_qg * ck.eG * (GLA_DK**-0.5)
        d_k = d_kg * ck.enG + d_kd * ck.ekd
        d_G = _add_at_last_row(d_qg * ck.qg - d_kg * ck.kg - kdd, d_gl, nh)
        shp = (2 * nh, CHUNK, CHUNK)
        bb, ii, jj = _iota3(shp, 0), _iota3(shp, 1), _iota3(shp, 2)
        tri = jnp.logical_or(jnp.logical_and(bb < nh, jj >= ii), jnp.logical_and(bb >= nh, jj <= ii))
        d_gk = _exact01(jnp.where(tri, 1.0, 0.0), d_G)
        for d, (dq_ref, dk_ref, dv_ref, dg_ref) in enumerate(((dqf_ref, dkf_ref, dvf_ref, dgf_ref), (dqb_ref, dkb_ref, dvb_ref, dgb_ref))):
            for i in range(nh):
                c = d * nh + i
                ks, vs = slice(i * GLA_DK, (i + 1) * GLA_DK), slice(i * GLA_DV, (i + 1) * GLA_DV)
                dq_ref[:, ks] = d_q[c]
                dk_ref[:, ks] = d_k[c]
                dv_ref[:, vs] = d_v[c]
                dg_ref[:, ks] = d_gk[c]

    def in_specs(ci):
        return _gla_specs(N, ci) + [
            pl.BlockSpec((1, nh, GLA_DV, GLA_DK), lambda n: (ci(n), 0, 0, 0)),
            pl.BlockSpec((CHUNK, nh * GLA_DV), lambda n: (ci(n), 0)),
        ]

    def out_specs(ci):
        ksp = pl.BlockSpec((CHUNK, nh * GLA_DK), lambda n: (ci(n), 0))
        return [ksp, ksp, pl.BlockSpec((CHUNK, nh * GLA_DV), lambda n: (ci(n), 0)), ksp]

    fw, bw = (lambda n: N - 1 - n), (lambda n: n)
    k_shape = jax.ShapeDtypeStruct((T, nh * GLA_DK), f32)
    v_shape = jax.ShapeDtypeStruct((T, nh * GLA_DV), f32)
    return pl.pallas_call(
        body,
        out_shape=(k_shape, k_shape, v_shape, k_shape) * 2,
        grid=(N,),
        in_specs=in_specs(fw) + in_specs(bw),
        out_specs=tuple(out_specs(fw) + out_specs(bw)),
        scratch_shapes=[pltpu.VMEM((2 * nh, GLA_DV, GLA_DK), f32)],
        compiler_params=_cparams(("arbitrary",)),
        name="gla_scan_bwd",
    )(proj, proj, proj, G_f, st_f, do, proj, proj, proj, G_b, st_b, do)


def _gla_dsum(dq_f, dq_b, dk_f, dk_b, dv_f, dv_b, dproj, tb=256):
    T = dq_f.shape[0]
    wk, wv = GLA_HEADS * GLA_DK, GLA_HEADS * GLA_DV

    def body(qf, qb, kf, kb, vf, vb, _, o_ref):
        o_ref[:, :wk] = (qf[...] + qb[...]).astype(o_ref.dtype)
        o_ref[:, wk : 2 * wk] = (kf[...] + kb[...]).astype(o_ref.dtype)
        o_ref[:, 2 * wk :] = (vf[...] + vb[...]).astype(o_ref.dtype)

    ksp = pl.BlockSpec((tb, wk), lambda i: (i, 0))
    vsp = pl.BlockSpec((tb, wv), lambda i: (i, 0))
    return pl.pallas_call(
        body,
        out_shape=jax.ShapeDtypeStruct(dproj.shape, dproj.dtype),
        grid=(T // tb,),
        in_specs=[ksp, ksp, ksp, ksp, vsp, vsp, _ANY_SPEC],
        out_specs=pl.BlockSpec((tb, 2 * wk + wv), lambda i: (i, DP_GLA_QKV_BLK)),
        input_output_aliases={6: 0},
        compiler_params=_cparams(("parallel",)),
        name="gla_dsum",
    )(dq_f, dq_b, dk_f, dk_b, dv_f, dv_b, dproj)


def _local_step(x, target, wcat, conv_w, avec, dvec, gdn_norm_w, w2f, w2b, b2f, b2b, gla_norm_w, proj_weights, w_pre, w_post,
                send_proj_grads=None, send_in_grads=None):
    h, h_t = _rms_pre(x, w_pre)
    proj = _matmul(h, wcat, "nt", f32, "proj_in", 2048, 512, 1024)
    qkvn = _gdn_prep(proj, conv_w)
    gates, gc = _gdn_gates(proj, avec, dvec)
    oa_f, oa_b, sa_f, sa_b, inv_f, inv_b = _gdn_fwd(qkvn, gates, gc)
    og = _out_norm(oa_f, oa_b, proj, Z_BLK, gdn_norm_w, GDN_DIM, "gdn_out")
    G_f, G_b = _gla_prep(proj, w2f, w2b, b2f, b2b)
    ob_f, ob_b, sb_f, sb_b = _gla_fwd(proj, G_f, G_b)
    obg = _out_norm(ob_f, ob_b, proj, G_B_BLK, gla_norm_w, GLA_DV, "gla_out")
    wpg, wpl, wout = proj_weights(obg)
    (loss_row, d_wpost, dy, m, dout, dya, dyb, dproj, dog, dobg) = _mid(og, obg, proj, x, target, wpg, wpl, wout, w_post)

    d_wout = _matmul(m, dout, "tn", MXU_DTYPE, "dw_out", 1024, 512, 2048)
    d_wpg = _matmul(og, dya, "tn", MXU_DTYPE, "dw_proj_gdn", 1024, 512, 2048)
    d_wpl = _matmul(obg, dyb, "tn", MXU_DTYPE, "dw_proj_gla", 1024, 512, 2048)
    if send_proj_grads is not None:
        gla_norm_w = gla_norm_w + send_proj_grads(d_wpg, d_wpl, d_wout)

    do_b, dproj, d_gla_norm = _out_norm_bwd(dobg, ob_f, ob_b, proj, G_B_BLK, gla_norm_w, GLA_DV, "gla_out_bwd", dproj)
    dq_f, dk_f, dv_f, dgk_f, dq_b, dk_b, dv_b, dgk_b = _gla_bwd(proj, G_f, G_b, sb_f, sb_b, do_b)
    dproj = _gla_dsum(dq_f, dq_b, dk_f, dk_b, dv_f, dv_b, dproj)
    dsmall_gla, d_w2f, d_w2b, d_b2f, d_b2b = _gla_prep_bwd(proj, w2f, w2b, b2f, b2b, dgk_f, dgk_b)

    do_a, dproj, d_gdn_norm = _out_norm_bwd(dog, oa_f, oa_b, proj, Z_BLK, gdn_norm_w, GDN_DIM, "gdn_out_bwd", dproj)
    dqkv_f, dqkv_b, dg_f, dg_b = _gdn_bwd(qkvn, gates, gc, sa_f, sa_b, inv_f, inv_b, do_a)
    dproj, d_conv = _gdn_prep_bwd(proj, conv_w, dqkv_f, dqkv_b, dproj)
    dproj, d_alog, d_dtb = _gdn_gates_bwd(proj, avec, dvec, dg_f, dg_b, dsmall_gla, dproj)

    d_wcat = _matmul(h_t, dproj, "nn", MXU_DTYPE, "dw_in", 1024, 512, 2048, out_t=True)
    started = None if send_in_grads is None else send_in_grads(d_wcat)
    dh = _matmul(dproj, wcat, "nn", f32, "dh", 1024, 1024, 2432, after=started)
    grad_x, d_wpre = _rms_pre_bwd(x, w_pre, dh, dy)
    return dict(loss_row=loss_row, grad_x=grad_x, d_wcat=d_wcat, d_conv=d_conv, d_alog=d_alog, d_dtb=d_dtb,
                d_gdn_norm=d_gdn_norm, d_wpg=d_wpg, d_w2f=d_w2f, d_w2b=d_w2b, d_b2f=d_b2f, d_b2b=d_b2b,
                d_gla_norm=d_gla_norm, d_wpl=d_wpl, d_wout=d_wout, d_wpost=d_wpost, d_wpre=d_wpre)


def _to_cat(wf):
    pad = jnp.zeros((N_CAT - N_IN,) + wf.shape[1:], wf.dtype)
    return jnp.concatenate([wf[:4096], wf[4128:6176], wf[7232:9280], wf[6176:7200], wf[4096:4128], wf[7200:7232], pad], axis=0)


def _from_cat(wc):
    return jnp.concatenate([wc[:4096], wc[9216:9248], wc[4096:6144], wc[8192:9216], wc[9248:9280], wc[6144:8192]], axis=0)


def _lane_row(*pieces):
    row = jnp.concatenate(pieces, axis=1)
    return jnp.pad(row, ((0, 0), (0, LANES - row.shape[1])))


def _pad_w2(w2, row0):
    return jnp.pad(w2, ((row0, LANES - row0 - w2.shape[0]), (0, 0)))


SM_ROWS, SM_COLS = 32, 384
REP_ROWS = 32


def _pack_small_shard(conv, w2f, w2b):
    lead = conv.shape[:-2]
    z = lambda r, c: jnp.zeros(lead + (r, c), f32)
    top = jnp.concatenate([conv, z(8 - CONV_K, SM_COLS)], axis=-2)
    mid = jnp.concatenate([w2f, w2b, z(16, SM_COLS - 128)], axis=-1)
    return jnp.concatenate([top, mid, z(SM_ROWS - 24, SM_COLS)], axis=-2)


LOSS_ROW = 28


def _pack_rep(ln_pre, ln_post, b2f, b2b, gla_norm, gdn_norm, alog_f, alog_b, dt_f, dt_b, loss=None):
    last = _lane_row(alog_f, alog_b, dt_f, dt_b)
    extra = jnp.zeros((1, LANES), f32) if loss is None else _lane_row(loss)
    rows = [ln_pre.reshape(8, LANES), ln_post.reshape(8, LANES), b2f.reshape(4, LANES), b2b.reshape(4, LANES),
            gla_norm.reshape(2, LANES), gdn_norm.reshape(1, LANES), last, extra, jnp.zeros((REP_ROWS - LOSS_ROW - 1, LANES), f32)]
    return jnp.concatenate(rows, axis=0)


def _unpack_rep(p):
    return dict(ln_pre_w=p[0:8].reshape(1, 1024), ln_post_w=p[8:16].reshape(1, 1024), gk_b2_fwd=p[16:20].reshape(1, 512),
                gk_b2_bwd=p[20:24].reshape(1, 512), gla_norm_w=p[24:26].reshape(1, 256), gdn_norm_w=p[26:27],
                a_log_fwd=p[27:28, 0:8], a_log_bwd=p[27:28, 8:16], dt_bias_fwd=p[27:28, 16:24], dt_bias_bwd=p[27:28, 24:32])


_MESH = pl.DeviceIdType.MESH


N_BIG_COPIES, N_SMALL_COPIES = 9, 7


def _gather_weights(w_in_s, sm_s):
    shapes = (w_in_s.shape, sm_s.shape)
    dtypes = (MXU_DTYPE, f32)

    def body(win_ref, sm_ref, gin_ref, gsm_ref, send_sems, recv_sems):
        x, y, c = lax.axis_index("x"), lax.axis_index("y"), lax.axis_index("c")
        me, sibling = (x, y, c), (x, y, 1 - c)
        chips = [(1 - x, y), (x, 1 - y), (1 - x, 1 - y)]
        bufs = (gin_ref, gsm_ref)

        def idx(px, py, pc):
            return 4 * px + 2 * py + pc

        for r in range(w_in_s.shape[1] // LANES):
            cols = slice(r * LANES, (r + 1) * LANES)
            gin_ref[idx(*me), :, cols] = win_ref[:, cols].astype(MXU_DTYPE)
        gsm_ref[idx(*me)] = sm_ref[...]

        def copy(sem, blk, to):
            return pltpu.make_async_remote_copy(src_ref=blk, dst_ref=blk, send_sem=send_sems.at[sem], recv_sem=recv_sems.at[sem],
                                                device_id=to, device_id_type=_MESH)

        def small(k, block, to):
            return copy(N_BIG_COPIES + k, gsm_ref.at[idx(*block)], to)

        half = w_in_s.shape[1] // 2
        xn, yn, dg = (1 - x, y), (x, 1 - y), (1 - x, 1 - y)

        def big(sem, block, to, part=None):
            blk = gin_ref.at[idx(*block)]
            return copy(sem, blk if part is None else blk.at[:, pl.ds(part * half, half)], to)

        started = [big(0, me, sibling), big(1, me, (*xn, c)), big(2, me, (*yn, c)), small(0, me, sibling)]
        started += [small(1 + j, me, (*chip, c)) for j, chip in enumerate(chips)]
        for cp in started:
            cp.start()

        def then(arrived, forwards):
            arrived.wait_recv()
            for cp in forwards:
                cp.start()
            started.extend(forwards)

        then(big(1, (*xn, c), me), [big(3, (*xn, c), (*yn, c), 1), big(5, (*xn, c), sibling)])
        then(big(2, (*yn, c), me), [big(4, (*yn, c), (*xn, c), 0), big(6, (*yn, c), sibling)])
        then(big(4, (*dg, c), me, 0), [big(7, (*dg, c), sibling, 0)])
        then(big(3, (*dg, c), me, 1), [big(8, (*dg, c), sibling, 1)])
        for j, chip in enumerate(chips):
            then(small(1 + j, (*chip, c), me), [small(4 + j, (*chip, c), sibling)])
        big(0, sibling, me).wait_recv()
        big(5, (*xn, 1 - c), me).wait_recv()
        big(6, (*yn, 1 - c), me).wait_recv()
        big(7, (*dg, 1 - c), me, 0).wait_recv()
        big(8, (*dg, 1 - c), me, 1).wait_recv()
        small(0, sibling, me).wait_recv()
        for j, chip in enumerate(chips):
            small(4 + j, (*chip, 1 - c), me).wait_recv()
        for cp in started:
            cp.wait_send()

    vm = pl.BlockSpec(memory_space=pltpu.VMEM)
    return pl.pallas_call(
        body,
        out_shape=tuple(jax.ShapeDtypeStruct((N_DEV,) + s, d) for s, d in zip(shapes, dtypes)),
        in_specs=[vm, vm],
        out_specs=(vm, vm),
        scratch_shapes=[pltpu.SemaphoreType.DMA((N_BIG_COPIES + N_SMALL_COPIES,)), pltpu.SemaphoreType.DMA((N_BIG_COPIES + N_SMALL_COPIES,))],
        compiler_params=pltpu.CompilerParams(vmem_limit_bytes=VMEM_LIMIT),
        name="gather_weights",
    )(w_in_s, sm_s)


N_CHIP = 4
_EFFECT = pltpu.SideEffectType.DATAFLOW_SIDE_EFFECTING
_HBM_SPEC = pl.BlockSpec(memory_space=pltpu.HBM)
_SEM_SPEC = pl.BlockSpec(memory_space=pltpu.SEMAPHORE)


def _relation_peer(k, x, y, c):
    px = (1 - x) if (k & 4) else x
    py = (1 - y) if (k & 2) else y
    pc = (1 - c) if (k & 1) else c
    return (px, py, pc), 4 * px + 2 * py + pc


_PLAN_COPIES = {"gather": N_DEV - 1, "blocks": N_DEV - 1, "sibling": 4, "chips": 3}
_PLAN_SLOTS = {"gather": N_DEV, "blocks": N_DEV, "sibling": 4, "chips": 4}


def _plan_copies(plan, src_ref, land_ref, send_sems, recv_sems, arrival):
    x, y, c = lax.axis_index("x"), lax.axis_index("y"), lax.axis_index("c")
    me, my_chip = 4 * x + 2 * y + c, 2 * x + y
    out = []

    def add(i, src, there, here, dev):
        out.append(pltpu.make_async_remote_copy(src_ref=src, dst_ref=land_ref.at[here if arrival else there], send_sem=send_sems.at[i],
                                                recv_sem=recv_sems.at[i], device_id=dev, device_id_type=_MESH))

    if plan in ("gather", "blocks"):
        for k in range(1, N_DEV):
            dev, p = _relation_peer(k, x, y, c)
            add(k - 1, src_ref.at[p] if plan == "blocks" else src_ref, me, p, dev)
    elif plan == "sibling":
        for j in range(4):
            add(j, src_ref.at[2 * j + 1 - c], j, j, (x, y, 1 - c))
    else:
        for i, k in enumerate((2, 4, 6)):
            (px, py, pc), _ = _relation_peer(k, x, y, c)
            add(i, src_ref.at[2 * px + py], my_chip, 2 * px + py, (px, py, pc))
    return out


def _exchange_start(src, plan, name, after=None):
    n = _PLAN_COPIES[plan]
    land = lax.empty((_PLAN_SLOTS[plan],) + (src.shape if plan == "gather" else src.shape[1:]), src.dtype)
    extra = [] if after is None else [after]

    def body(src_ref, land_ref, *rest):
        send_sems, recv_sems, _, _, token = rest[len(extra) :]
        for cp in _plan_copies(plan, src_ref, land_ref, send_sems, recv_sems, False):
            cp.start()
        token[...] = jnp.zeros_like(token)

    return pl.pallas_call(
        body,
        name=name,
        out_shape=(pltpu.SemaphoreType.DMA((n,)), pltpu.SemaphoreType.DMA((n,)), pltpu.HBM(src.shape, src.dtype),
                   pltpu.HBM(land.shape, land.dtype), jax.ShapeDtypeStruct((8, LANES), f32)),
        in_specs=(_HBM_SPEC, _HBM_SPEC) + (_ANY_SPEC,) * len(extra),
        out_specs=(_SEM_SPEC, _SEM_SPEC, _HBM_SPEC, _HBM_SPEC, pl.BlockSpec(memory_space=pltpu.VMEM)),
        input_output_aliases={0: 2, 1: 3},
        compiler_params=pltpu.CompilerParams(has_side_effects=_EFFECT),
    )(pltpu.with_memory_space_constraint(src, pltpu.HBM), pltpu.with_memory_space_constraint(land, pltpu.HBM), *extra)


def _exchange_wait(started, after, plan, name):
    send_sems, recv_sems, src_thru, land_thru, _ = started

    def body(src_ref, land_ref, send_sems, recv_sems, after_ref, src_dead, got_ref):
        for cp in _plan_copies(plan, src_ref, land_ref, send_sems, recv_sems, True):
            cp.wait_send()
            cp.wait_recv()

    return pl.pallas_call(
        body,
        name=name,
        out_shape=(pltpu.HBM(src_thru.shape, src_thru.dtype), pltpu.HBM(land_thru.shape, land_thru.dtype)),
        in_specs=(_HBM_SPEC, _HBM_SPEC, _SEM_SPEC, _SEM_SPEC, pl.BlockSpec(memory_space=pl.ANY)),
        out_specs=(_HBM_SPEC, _HBM_SPEC),
        input_output_aliases={0: 0, 1: 1},
        compiler_params=pltpu.CompilerParams(has_side_effects=_EFFECT),
    )(src_thru, land_thru, send_sems, recv_sems, after)


def _pair_sum(g, r, name, tc=LANES):
    _, R, Cc = g.shape

    def body(g_ref, r_ref, o_ref):
        o_ref[...] = (g_ref[...].astype(f32) + r_ref[...].astype(f32)).astype(o_ref.dtype)

    return pl.pallas_call(
        body,
        out_shape=jax.ShapeDtypeStruct(r.shape, r.dtype),
        grid=(N_CHIP, Cc // tc),
        in_specs=[pl.BlockSpec((1, R, tc), lambda k, i: (2 * k + lax.axis_index("c"), 0, i)), pl.BlockSpec((1, R, tc), lambda k, i: (k, 0, i))],
        out_specs=pl.BlockSpec((1, R, tc), lambda k, i: (k, 0, i)),
        compiler_params=_cparams(("parallel", "parallel")),
        name=name,
    )(g, r)


def _exchange_small(g_sm, g_rep):
    srcs = (g_sm, g_rep)

    def body(in0, in1, out0, out1, send_sems, recv_sems, local_sems):
        x, y, c = lax.axis_index("x"), lax.axis_index("y"), lax.axis_index("c")
        me = 4 * x + 2 * y + c
        ins, outs = (in0, in1), (out0, out1)

        def copy(a, k, arrival):
            dev, p = _relation_peer(k, x, y, c)
            return pltpu.make_async_remote_copy(src_ref=ins[a].at[p] if a == 0 else ins[a], dst_ref=outs[a].at[p if arrival else me],
                                                send_sem=send_sems.at[7 * a + k - 1], recv_sem=recv_sems.at[7 * a + k - 1],
                                                device_id=dev, device_id_type=_MESH)

        pairs = [(a, k) for a in (0, 1) for k in range(1, N_DEV)]
        sends = [copy(a, k, False) for a, k in pairs]
        for cp in sends:
            cp.start()
        local = [pltpu.make_async_copy(ins[0].at[me], outs[0].at[me], local_sems.at[0]),
                 pltpu.make_async_copy(ins[1], outs[1].at[me], local_sems.at[1])]
        for cp in local:
            cp.start()
        for a, k in pairs:
            copy(a, k, True).wait_recv()
        for cp in sends:
            cp.wait_send()
        for cp in local:
            cp.wait()

    return pl.pallas_call(
        body,
        out_shape=(jax.ShapeDtypeStruct(g_sm.shape, g_sm.dtype), jax.ShapeDtypeStruct((N_DEV,) + g_rep.shape, g_rep.dtype)),
        in_specs=[_HBM_SPEC] * 2,
        out_specs=(_HBM_SPEC,) * 2,
        scratch_shapes=[pltpu.SemaphoreType.DMA((14,)), pltpu.SemaphoreType.DMA((14,)), pltpu.SemaphoreType.DMA((2,))],
        name="exchange_small",
    )(*srcs)


def _sum_adam(parts, w, m, v, name, tb, tc=None, own=None, own_slot=None):
    R, Cc = w.shape
    n_parts = parts.shape[0]
    tb = R if tc else min(tb, R)
    tc = tc or Cc
    assert R % tb == 0 and Cc % tc == 0
    bc1 = 1.0 - ADAM_B1**ADAM_STEP
    bc2 = 1.0 - ADAM_B2**ADAM_STEP
    extra = [] if own is None else [own]

    def body(p_ref, w_ref, m_ref, v_ref, *rest):
        g_ref, d_ref, nm_ref, nv_ref = rest[len(extra) :]

        def part(j):
            pj = p_ref[j].astype(f32)
            return pj if own is None else jnp.where(own_slot() == j, rest[0][0].astype(f32), pj)

        g = part(0)
        for j in range(1, n_parts):
            g = g + part(j)
        g_ref[...] = g
        m2 = ADAM_B1 * m_ref[...] + (1.0 - ADAM_B1) * g
        v2 = ADAM_B2 * v_ref[...] + (1.0 - ADAM_B2) * (g * g)
        nm_ref[...] = m2
        nv_ref[...] = v2
        d_ref[...] = -ADAM_LR * ((m2 / bc1) / (jnp.sqrt(v2 / bc2) + ADAM_EPS) + ADAM_WD * w_ref[...])

    blk = pl.BlockSpec((tb, tc), lambda i, j: (i, j))
    o = jax.ShapeDtypeStruct((R, Cc), f32)
    return pl.pallas_call(
        body,
        out_shape=(o, o, o, o),
        grid=(R // tb, Cc // tc),
        in_specs=[pl.BlockSpec((n_parts, tb, tc), lambda i, j: (0, i, j)), blk, blk, blk]
        + [pl.BlockSpec((1, tb, tc), lambda i, j: (own_slot(), i, j))] * len(extra),
        out_specs=(blk, blk, blk, blk),
        compiler_params=_cparams(("parallel", "parallel")),
        name=name,
    )(parts, w, m, v, *extra)


def _my_device():
    return 4 * lax.axis_index("x") + 2 * lax.axis_index("y") + lax.axis_index("c")


def _my_chip():
    return 2 * lax.axis_index("x") + lax.axis_index("y")


_WEIGHTS = ["ln_pre_w", "w_in", "conv_w", "a_log_fwd", "a_log_bwd", "dt_bias_fwd", "dt_bias_bwd", "gdn_norm_w", "w_proj_gdn",
            "gk_w2_fwd", "gk_b2_fwd", "gk_w2_bwd", "gk_b2_bwd", "gla_norm_w", "w_proj_gla", "w_out", "ln_post_w"]
_REP_ORDER = ["ln_pre_w", "ln_post_w", "gk_b2_fwd", "gk_b2_bwd", "gla_norm_w", "gdn_norm_w", "a_log_fwd", "a_log_bwd", "dt_bias_fwd", "dt_bias_bwd"]


def kernel(x, ln_pre_w, w_in, conv_w, a_log_fwd, a_log_bwd, dt_bias_fwd, dt_bias_bwd, gdn_norm_w, w_proj_gdn, gk_w2_fwd, gk_b2_fwd, gk_w2_bwd, gk_b2_bwd, gla_norm_w, w_proj_gla, w_out, ln_post_w, loss_target, m_ln_pre_w, m_w_in, m_conv_w, m_a_log_fwd, m_a_log_bwd, m_dt_bias_fwd, m_dt_bias_bwd, m_gdn_norm_w, m_w_proj_gdn, m_gk_w2_fwd, m_gk_b2_fwd, m_gk_w2_bwd, m_gk_b2_bwd, m_gla_norm_w, m_w_proj_gla, m_w_out, m_ln_post_w, v_ln_pre_w, v_w_in, v_conv_w, v_a_log_fwd, v_a_log_bwd, v_dt_bias_fwd, v_dt_bias_bwd, v_gdn_norm_w, v_w_proj_gdn, v_gk_w2_fwd, v_gk_b2_fwd, v_gk_w2_bwd, v_gk_b2_bwd, v_gla_norm_w, v_w_proj_gla, v_w_out, v_ln_post_w):
    args = locals()
    W = {n: args[n] for n in _WEIGHTS}
    M = {n: args["m_" + n] for n in _WEIGHTS}
    V = {n: args["v_" + n] for n in _WEIGHTS}

    wp_stack = lambda P: jnp.concatenate([P["w_proj_gdn"], P["w_proj_gla"], P["w_out"]], axis=0)
    sm_pack = lambda P: _pack_small_shard(P["conv_w"][0], P["gk_w2_fwd"][0], P["gk_w2_bwd"][0])
    me = 4 * lax.axis_index("x") + 2 * lax.axis_index("y") + lax.axis_index("c")
    w_in_t, m_in_t, v_in_t = (jnp.transpose(a[0]) for a in (w_in, m_w_in, v_w_in))
    g_in, g_sm = _gather_weights(w_in_t, sm_pack(W))
    wp_own = wp_stack(W).astype(MXU_DTYPE)
    wp_started = _exchange_start(wp_own, "gather", "gather_proj_start", after=g_sm)

    def proj_weights(after):
        own, land = _exchange_wait(wp_started, after, "gather", "gather_proj_wait")
        g_p = lax.dynamic_update_slice(land, own[None], (me, 0, 0, 0))
        return tuple(g_p[:, i].reshape(D_MODEL, D_MODEL) for i in range(3))

    rows_p = 3 * D_MODEL // N_DEV
    sent = {}

    def send_proj_grads(d_wpg, d_wpl, d_wout):
        p_p = jnp.stack([d_wpg, d_wpl, d_wout]).reshape(3, N_DEV, D_MODEL // N_DEV, D_MODEL).transpose(1, 0, 2, 3)
        sent["p_p"] = p_p.reshape(N_DEV, rows_p, D_MODEL)
        sent["started"] = _exchange_start(sent["p_p"], "blocks", "exchange_proj_start")
        return sent["started"][4][0:1, 0:1]

    def send_in_grads(d_wcat):
        p_in = _from_cat(d_wcat).reshape(N_DEV, SHARD_IN, D_MODEL)
        sib = _exchange_start(p_in, "sibling", "exchange_in_sibling_start")
        p_p, land_p = _exchange_wait(sent["started"], sib[4], "blocks", "exchange_proj_wait")
        sent["o_p"] = _sum_adam(land_p, wp_stack(W).reshape(rows_p, D_MODEL), wp_stack(M).reshape(rows_p, D_MODEL),
                                wp_stack(V).reshape(rows_p, D_MODEL), "adam_w_proj", 128, own=p_p, own_slot=_my_device)
        p_in, sib_land = _exchange_wait(sib, sent["o_p"][0], "sibling", "exchange_in_sibling_wait")
        sent["chips"] = _exchange_start(_pair_sum(p_in, sib_land, "pair_sum_w_in", tc=512), "chips", "exchange_in_chips_start")
        return sent["chips"][4]

    wcat = _to_cat(g_in.reshape(N_IN, D_MODEL))
    conv_full = g_sm[:, 0:CONV_K, :].transpose(1, 0, 2).reshape(CONV_K, N_DEV * SM_COLS)
    w2f_full = g_sm[:, 8:24, 0:64].transpose(1, 0, 2).reshape(16, 512)
    w2b_full = g_sm[:, 8:24, 64:128].transpose(1, 0, 2).reshape(16, 512)

    avec = _lane_row(a_log_fwd, a_log_bwd)
    dvec = _lane_row(dt_bias_fwd, dt_bias_bwd)
    G = _local_step(x[0], loss_target[0], wcat, conv_full, avec, dvec, gdn_norm_w, _pad_w2(w2f_full, 32), _pad_w2(w2b_full, 48),
                    gk_b2_fwd, gk_b2_bwd, gla_norm_w, proj_weights, ln_pre_w + wp_started[4][0:1, 0:1], ln_post_w, send_proj_grads,
                    send_in_grads)

    p_sm = _pack_small_shard(G["d_conv"].reshape(CONV_K, N_DEV, SM_COLS).transpose(1, 0, 2),
                             G["d_w2f"][32:48].reshape(16, N_DEV, 64).transpose(1, 0, 2),
                             G["d_w2b"][48:64].reshape(16, N_DEV, 64).transpose(1, 0, 2))
    p_rep = _pack_rep(G["d_wpre"], G["d_wpost"], G["d_b2f"], G["d_b2b"], G["d_gla_norm"], G["d_gdn_norm"],
                      G["d_alog"][:, 0:8], G["d_alog"][:, 8:16], G["d_dtb"][:, 0:8], G["d_dtb"][:, 8:16],
                      loss=0.5 * jnp.sum(G["loss_row"], axis=1, keepdims=True) / D_MODEL)
    p_small = jnp.concatenate([p_sm, jnp.broadcast_to(p_rep[None], (N_DEV,) + p_rep.shape)], axis=2)
    small_started = _exchange_start(p_small, "blocks", "exchange_small_start")
    q_in, land_in = _exchange_wait(sent["chips"], small_started[4], "chips", "exchange_in_chips_wait")
    o_in = _sum_adam(land_in, w_in_t, m_in_t, v_in_t, "adam_w_in", SHARD_IN, tc=256, own=q_in, own_slot=_my_chip)
    o_p = sent["o_p"]
    p_small, land_small = _exchange_wait(small_started, o_in[0], "blocks", "exchange_small_wait")
    rep_pack = lambda P: _pack_rep(*[P[n] for n in _REP_ORDER])
    small_pack = lambda P: jnp.concatenate([sm_pack(P), rep_pack(P)], axis=1)
    o_small = _sum_adam(land_small, small_pack(W), small_pack(M), small_pack(V), "adam_small", SM_ROWS, own=p_small, own_slot=_my_device)
    o_sm, o_rep = [o[:, :SM_COLS] for o in o_small], [o[:, SM_COLS:] for o in o_small]

    res = []
    for kind in range(4):
        rep = _unpack_rep(o_rep[kind])
        per = dict(rep)
        per["w_in"] = jnp.transpose(o_in[kind])[None]
        pp = o_p[kind].reshape(3, 1, D_MODEL // N_DEV, D_MODEL)
        per["w_proj_gdn"], per["w_proj_gla"], per["w_out"] = pp[0], pp[1], pp[2]
        per["conv_w"] = o_sm[kind][None, 0:CONV_K, :]
        per["gk_w2_fwd"] = o_sm[kind][None, 8:24, 0:64]
        per["gk_w2_bwd"] = o_sm[kind][None, 8:24, 64:128]
        res.append([per[n] for n in _WEIGHTS])
    loss = o_rep[0][LOSS_ROW, 0]
    return (loss, G["grad_x"][None], *res[0], *res[1], *res[2], *res[3])
```

```python
import functools
import math

import jax
import jax.numpy as jnp
from jax import lax
from jax.experimental import pallas as pl
from jax.experimental.pallas import tpu as pltpu

f32 = jnp.float32
MXU_DTYPE = jnp.bfloat16
HI = lax.Precision.HIGHEST

D_MODEL = 1024
CHUNK = 64
NORM_EPS = 1e-6
GDN_HEADS, GDN_DIM = 8, 128
GLA_HEADS, GLA_DK, GLA_DV = 4, 128, 256
GATE_NORMALIZER = 16.0
CONV_K = 5
N_IN = 9280
N_DEV = 8
SHARD_IN = N_IN // N_DEV

N_CAT = 9728
SMALL_OFF = 9216
LANES = 128

ADAM_LR, ADAM_B1, ADAM_B2, ADAM_EPS, ADAM_WD, ADAM_STEP = 0.001, 0.9, 0.999, 1e-08, 0.01, 10

VMEM_LIMIT = 56 * 1024 * 1024


def _cparams(sem=None):
    return pltpu.CompilerParams(dimension_semantics=sem, vmem_limit_bytes=VMEM_LIMIT)


_DN = {"nn": (((1,), (0,)), ((), ())), "nt": (((1,), (1,)), ((), ())), "tn": (((0,), (0,)), ((), ()))}
_BDN = {"nn": (((2,), (1,)), ((0,), (0,))), "nt": (((2,), (2,)), ((0,), (0,))), "tn": (((1,), (1,)), ((0,), (0,)))}


def _dot16(a, b, kind):
    return lax.dot_general(a, b, (_DN if a.ndim == 2 else _BDN)[kind], preferred_element_type=f32)


def _mx(a, b, kind):
    return _dot16(a.astype(MXU_DTYPE), b.astype(MXU_DTYPE), kind)


def _split2(a):
    hi = a.astype(jnp.bfloat16)
    return hi, (a - hi.astype(f32)).astype(jnp.bfloat16)


def _split3(a):
    p1 = a.astype(jnp.bfloat16)
    r = a - p1.astype(f32)
    p2 = r.astype(jnp.bfloat16)
    return p1, p2, (r - p2.astype(f32)).astype(jnp.bfloat16)


def _h3s(a2, b2, kind):
    (ah, al), (bh, bl) = a2, b2
    return _dot16(ah, bh, kind) + (_dot16(ah, bl, kind) + _dot16(al, bh, kind))


def _h3(a, b, kind):
    return _h3s(_split2(a), _split2(b), kind)


def _exact01(t01, x, kind="nn"):
    t = t01.astype(jnp.bfloat16)
    x1, x2, x3 = _split3(x)
    return _dot16(t, x1, kind) + (_dot16(t, x2, kind) + _dot16(t, x3, kind))


def _colsum_col(e):
    ones = jnp.ones(e.shape[:-1] + (LANES,), jnp.bfloat16)
    e1, e2, e3 = _split3(e)
    return (_dot16(e1, ones, "tn") + (_dot16(e2, ones, "tn") + _dot16(e3, ones, "tn")))[..., 0:1]


def _iota3(shape, axis):
    return lax.broadcasted_iota(jnp.int32, shape, axis)


def _dir_masks(B, nf, Cn):
    shp = (B, Cn, Cn)
    bb, ii, jj = _iota3(shp, 0), _iota3(shp, 1), _iota3(shp, 2)
    fwd = bb < nf
    rev = jnp.logical_not(fwd)
    incl = jnp.logical_or(jnp.logical_and(fwd, ii >= jj), jnp.logical_and(rev, ii <= jj))
    strict = jnp.logical_and(incl, ii != jj)
    return incl, strict, ii == jj


def _last_row(x, nf):
    B, Cn = x.shape[0], x.shape[1]
    fwd = _iota3((B, 1, 1), 0) < nf
    return jnp.where(fwd, x[:, Cn - 1 : Cn, :], x[:, 0:1, :])


def _add_at_last_row(x, val, nf):
    B, Cn = x.shape[0], x.shape[1]
    bb, rr = _iota3((B, Cn, 1), 0), _iota3((B, Cn, 1), 1)
    at = jnp.logical_or(jnp.logical_and(bb < nf, rr == Cn - 1), jnp.logical_and(bb >= nf, rr == 0))
    return x + jnp.where(at, val, 0.0)


def _sigmoid(x):
    return jax.nn.sigmoid(x)


def _silu(x):
    return x * _sigmoid(x)


def _silu_grad(x):
    s = _sigmoid(x)
    return s * (1.0 + x * (1.0 - s))


def _softplus(x):
    u = jnp.exp(-jnp.abs(x))
    l1p = jnp.where(u < 1e-3, u * (1.0 - u * (0.5 - u * (1.0 / 3.0))), jnp.log(1.0 + u))
    return jnp.maximum(x, 0.0) + l1p


def _mxr(x):
    return x.astype(MXU_DTYPE).astype(f32)


def _iota2(shape, axis):
    return lax.broadcasted_iota(jnp.int32, shape, axis)


def _matmul(a, b, kind, out_dtype, name, tm, tn, tk, out_t=False, after=None):
    extra = [] if after is None else [after]
    if kind == "nn":
        (M, K), N = a.shape, b.shape[1]
    elif kind == "nt":
        (M, K), N = a.shape, b.shape[0]
    else:
        (K, M), N = a.shape, b.shape[1]
    tm, tn, tk = min(tm, M), min(tn, N), min(tk, K)
    assert M % tm == 0 and N % tn == 0 and K % tk == 0, (name, M, N, K)
    nk = K // tk
    if kind == "tn":
        a_spec = pl.BlockSpec((tk, tm), lambda i, j, k: (k, i))
    else:
        a_spec = pl.BlockSpec((tm, tk), lambda i, j, k: (i, k))
    if kind == "nt":
        b_spec = pl.BlockSpec((tn, tk), lambda i, j, k: (j, k))
    else:
        b_spec = pl.BlockSpec((tk, tn), lambda i, j, k: (k, j))

    def body(a_ref, b_ref, *rest):
        o_ref, acc = rest[len(extra)], rest[len(extra) + 1 :]
        p = _mx(a_ref[...], b_ref[...], kind)
        if nk == 1:
            o_ref[...] = (p.T if out_t else p).astype(out_dtype)
        else:
            assert not out_t
            acc_ref = acc[0]
            k = pl.program_id(2)

            @pl.when(k == 0)
            def _():
                acc_ref[...] = p

            @pl.when(k > 0)
            def _():
                acc_ref[...] += p

            @pl.when(k == nk - 1)
            def _():
                o_ref[...] = acc_ref[...].astype(out_dtype)

    return pl.pallas_call(
        body,
        out_shape=jax.ShapeDtypeStruct((N, M) if out_t else (M, N), out_dtype),
        grid=(M // tm, N // tn, nk),
        in_specs=[a_spec, b_spec] + [pl.BlockSpec((8, LANES), lambda i, j, k: (0, 0))] * len(extra),
        out_specs=pl.BlockSpec((tn, tm), lambda i, j, k: (j, i)) if out_t else pl.BlockSpec((tm, tn), lambda i, j, k: (i, j)),
        scratch_shapes=[] if nk == 1 else [pltpu.VMEM((tm, tn), f32)],
        compiler_params=_cparams(("parallel", "parallel", "arbitrary")),
        name=name,
    )(a, b, *extra)


def _rms_pre(x, w, tb=256):
    T = x.shape[0]

    def body(x_ref, w_ref, h_ref, ht_ref):
        xv = x_ref[...]
        r = lax.rsqrt(jnp.mean(xv * xv, axis=1, keepdims=True) + NORM_EPS)
        h = xv * r * w_ref[...]
        h_ref[...] = h.astype(h_ref.dtype)
        ht_ref[...] = h.T.astype(ht_ref.dtype)

    return pl.pallas_call(
        body,
        out_shape=(jax.ShapeDtypeStruct((T, D_MODEL), MXU_DTYPE), jax.ShapeDtypeStruct((D_MODEL, T), MXU_DTYPE)),
        grid=(T // tb,),
        in_specs=[pl.BlockSpec((tb, D_MODEL), lambda i: (i, 0)), pl.BlockSpec((1, D_MODEL), lambda i: (0, 0))],
        out_specs=(pl.BlockSpec((tb, D_MODEL), lambda i: (i, 0)), pl.BlockSpec((D_MODEL, tb), lambda i: (0, i))),
        compiler_params=_cparams(("parallel",)),
        name="rms_pre",
    )(x, w)


def _rms_pre_bwd(x, w, dh, dy, tb=256):
    T = x.shape[0]

    def body(x_ref, w_ref, dh_ref, dy_ref, dx_ref, dw_ref):
        i = pl.program_id(0)
        xv, dhv = x_ref[...], dh_ref[...]
        r = lax.rsqrt(jnp.mean(xv * xv, axis=1, keepdims=True) + NORM_EPS)
        dhw = dhv * w_ref[...]
        dx_ref[...] = dy_ref[...] + r * dhw - xv * (r * r * r) * jnp.mean(dhw * xv, axis=1, keepdims=True)
        part = jnp.sum(dhv * xv * r, axis=0, keepdims=True)

        @pl.when(i == 0)
        def _():
            dw_ref[...] = part

        @pl.when(i > 0)
        def _():
            dw_ref[...] += part

    blk = pl.BlockSpec((tb, D_MODEL), lambda i: (i, 0))
    row = pl.BlockSpec((1, D_MODEL), lambda i: (0, 0))
    return pl.pallas_call(
        body,
        out_shape=(jax.ShapeDtypeStruct((T, D_MODEL), f32), jax.ShapeDtypeStruct((1, D_MODEL), f32)),
        grid=(T // tb,),
        in_specs=[blk, row, blk, blk],
        out_specs=(blk, row),
        compiler_params=_cparams(("arbitrary",)),
        name="rms_pre_bwd",
    )(x, w, dh, dy)


SHIFTS = (-2, -1, 1, 2)


def _shifted(x):
    T = x.shape[0]
    t = _iota2(x.shape, 0)
    out = {}
    for d in SHIFTS:
        valid = (t < T - d) if d > 0 else (t >= -d)
        out[d] = jnp.where(valid, pltpu.roll(x, (-d) % T, 0), 0.0)
    return out


def _conv5(u, ush, cw):
    acc = u * cw[2:3, :]
    for j in (0, 1, 3, 4):
        acc = acc + ush[j - 2] * cw[j : j + 1, :]
    return acc


def _gdn_prep(proj, conv_w):
    T = proj.shape[0]
    nh = GDN_HEADS

    def body(u_ref, cw_ref, o_ref):
        j = pl.program_id(0)
        u = _mxr(u_ref[...])
        s = _silu(_conv5(u, _shifted(u), _mxr(cw_ref[...])))

        @pl.when(j < 2 * nh)
        def _():
            o_ref[0] = s * lax.rsqrt(jnp.sum(s * s, axis=1, keepdims=True) + NORM_EPS)

        @pl.when(j >= 2 * nh)
        def _():
            o_ref[0] = s

    return pl.pallas_call(
        body,
        out_shape=jax.ShapeDtypeStruct((3, T, GDN_HEADS * GDN_DIM), f32),
        grid=(3 * nh,),
        in_specs=[pl.BlockSpec((T, GDN_DIM), lambda j: (0, j)), pl.BlockSpec((CONV_K, GDN_DIM), lambda j: (0, j))],
        out_specs=pl.BlockSpec((1, T, GDN_DIM), lambda j: (j // nh, 0, j % nh)),
        compiler_params=_cparams(("parallel",)),
        name="gdn_prep",
    )(proj, conv_w)


def _gdn_prep_bwd(proj, conv_w, dqkv_f, dqkv_b, dproj):
    T = proj.shape[0]
    nh = GDN_HEADS


    def body(u_ref, cw_ref, df_ref, db_ref, _, du_ref, dcw_ref):
        j = pl.program_id(0)
        u, cw = _mxr(u_ref[...]), _mxr(cw_ref[...])
        ush = _shifted(u)
        c = _conv5(u, ush, cw)
        s = _silu(c)
        dn = df_ref[0] + db_ref[0]
        rinv = lax.rsqrt(jnp.sum(s * s, axis=1, keepdims=True) + NORM_EPS)
        ds_norm = rinv * dn - s * (rinv * rinv * rinv) * jnp.sum(dn * s, axis=1, keepdims=True)
        ds = jnp.where(j < 2 * nh, ds_norm, dn)
        dc = _mxr(ds * _silu_grad(c))
        dcsh = _shifted(dc)
        du = dc * cw[2:3, :]
        for jj in range(CONV_K):
            d = jj - 2
            dcw_ref[jj : jj + 1, :] = jnp.sum(dc * (u if d == 0 else ush[d]), axis=0, keepdims=True)
            if d != 0:
                du = du + dcsh[-d] * cw[jj : jj + 1, :]
        du_ref[...] = du.astype(du_ref.dtype)

    qspec = pl.BlockSpec((1, T, GDN_DIM), lambda j: (j // nh, 0, j % nh))
    return pl.pallas_call(
        body,
        out_shape=(jax.ShapeDtypeStruct(dproj.shape, dproj.dtype), jax.ShapeDtypeStruct((CONV_K, 3 * nh * GDN_DIM), f32)),
        grid=(3 * nh,),
        in_specs=[pl.BlockSpec((T, GDN_DIM), lambda j: (0, j)), pl.BlockSpec((CONV_K, GDN_DIM), lambda j: (0, j)), qspec, qspec, _ANY_SPEC],
        out_specs=(pl.BlockSpec((T, GDN_DIM), lambda j: (0, j)), pl.BlockSpec((CONV_K, GDN_DIM), lambda j: (0, j))),
        input_output_aliases={4: 0},
        compiler_params=_cparams(("parallel",)),
        name="gdn_prep_bwd",
    )(proj, conv_w, dqkv_f, dqkv_b, dproj)


def _chunk_tri(n, rev):
    i, j = _iota2((n, n), 0), _iota2((n, n), 1)
    same = jnp.right_shift(i, 6) == jnp.right_shift(j, 6)
    order = (j >= i) if rev else (j <= i)
    return jnp.where(jnp.logical_and(same, order), 1.0, 0.0).astype(f32)


def _gdn_gates(proj, avec, dvec, tb=256):
    T = proj.shape[0]

    def body(s_ref, a_ref, d_ref, g_ref, gc_ref):
        small = s_ref[...]
        lane = _iota2(small.shape, 1)
        lg = -jnp.exp(a_ref[...]) * _softplus(small + d_ref[...])
        beta = _sigmoid(small)
        g_ref[...] = jnp.where(lane < 16, lg, jnp.where(lane < 32, beta, 0.0))
        lgm = jnp.where(lane < 16, lg, 0.0)
        gcf = _exact01(_chunk_tri(tb, False), lgm)
        gcr = _exact01(_chunk_tri(tb, True), lgm)
        gc_ref[...] = jnp.where(lane < 8, gcf, gcr)

    blk = pl.BlockSpec((tb, LANES), lambda i: (i, 0))
    row = pl.BlockSpec((1, LANES), lambda i: (0, 0))
    return pl.pallas_call(
        body,
        out_shape=(jax.ShapeDtypeStruct((T, LANES), f32), jax.ShapeDtypeStruct((T, LANES), f32)),
        grid=(T // tb,),
        in_specs=[pl.BlockSpec((tb, LANES), lambda i: (i, SMALL_OFF // LANES)), row, row],
        out_specs=(blk, blk),
        compiler_params=_cparams(("parallel",)),
        name="gdn_gates",
    )(proj, avec, dvec)


def _gdn_gates_bwd(proj, avec, dvec, dg_f, dg_b, dsmall_gla, dproj, tb=256):
    T = proj.shape[0]
    pad = N_CAT - SMALL_OFF

    def body(s_ref, a_ref, d_ref, gf_ref, gb_ref, dl_ref, _, ds_ref, da_ref, dd_ref):
        i = pl.program_id(0)
        small = s_ref[...]
        lane = _iota2(small.shape, 1)
        dgate = gf_ref[...] + gb_ref[...]
        z = small + d_ref[...]
        nega = -jnp.exp(a_ref[...])
        dz = dgate * nega * _sigmoid(z)
        beta = _sigmoid(small)
        dsm = jnp.where(lane < 16, dz, jnp.where(lane < 32, dgate * beta * (1.0 - beta), 0.0))
        ds_ref[:, :LANES] = (dsm + dl_ref[...]).astype(ds_ref.dtype)
        ds_ref[:, LANES:] = jnp.zeros((tb, pad - LANES), ds_ref.dtype)
        lg = nega * _softplus(z)
        pa = jnp.sum(jnp.where(lane < 16, dgate * lg, 0.0), axis=0, keepdims=True)
        pd = jnp.sum(jnp.where(lane < 16, dz, 0.0), axis=0, keepdims=True)

        @pl.when(i == 0)
        def _():
            da_ref[...] = pa
            dd_ref[...] = pd

        @pl.when(i > 0)
        def _():
            da_ref[...] += pa
            dd_ref[...] += pd

    blk = pl.BlockSpec((tb, LANES), lambda i: (i, 0))
    row = pl.BlockSpec((1, LANES), lambda i: (0, 0))
    return pl.pallas_call(
        body,
        out_shape=(jax.ShapeDtypeStruct(dproj.shape, dproj.dtype), jax.ShapeDtypeStruct((1, LANES), f32), jax.ShapeDtypeStruct((1, LANES), f32)),
        grid=(T // tb,),
        in_specs=[pl.BlockSpec((tb, LANES), lambda i: (i, SMALL_OFF // LANES)), row, row, blk, blk, blk, _ANY_SPEC],
        out_specs=(pl.BlockSpec((tb, pad), lambda i: (i, DP_SMALL_BLK)), row, row),
        input_output_aliases={6: 0},
        compiler_params=_cparams(("arbitrary",)),
        name="gdn_gates_bwd",
    )(proj, avec, dvec, dg_f, dg_b, dsmall_gla, dproj)


GDN_HB = 8


def _inv_unit_lower(L, eye):
    A = eye - L
    P2 = _split2(L)
    for _ in range(3):
        P2 = _split2(_h3s(P2, P2, "nn"))
        A = A + _h3s(_split2(A), P2, "nn")
    Ph = P2[0]
    for _ in range(2):
        Ph = _dot16(Ph, Ph, "nn").astype(jnp.bfloat16)
        A = A + _dot16(A.astype(jnp.bfloat16), Ph, "nn")
    return A


def _lane_col(blk, idx):
    lane = _iota2(blk.shape, 1)
    return jnp.sum(jnp.where(lane == idx, blk, 0.0), axis=1, keepdims=True)


class _GdnChunk:
    def __init__(self, q, k, v, gcol, beta, rev, A=None):
        Cn = q.shape[0]
        ii, jj = _iota2((Cn, Cn), 0), _iota2((Cn, Cn), 1)
        self.incl = (ii <= jj) if rev else (ii >= jj)
        self.strict = (ii < jj) if rev else (ii > jj)
        grow = jnp.broadcast_to(gcol, (Cn, LANES)).T[0:1, :]
        diff = gcol - grow
        self.decay = jnp.where(self.incl, jnp.exp(jnp.where(self.incl, diff, 0.0)), 0.0)
        self.k, self.v, self.beta, self.gcol = k, v, beta, gcol
        self.qs = q * (GDN_DIM**-0.5)
        self.kb = k * beta
        self.L = jnp.where(self.strict, _mx(self.kb, k, "nt") * self.decay, 0.0)
        if A is None:
            A = _inv_unit_lower(self.L, jnp.where(ii == jj, 1.0, 0.0).astype(f32))
        self.A2 = _split2(A)
        self.A = A
        self.eg = jnp.exp(gcol)
        self.kbg = self.kb * self.eg
        self.sol = _h3s(self.A2, _split2(jnp.concatenate([v * beta, self.kbg], axis=1)), "nn")
        self.u = self.sol[:, :GDN_DIM]
        self.w = self.sol[:, GDN_DIM:]
        self.attn = jnp.where(self.incl, _mx(self.qs, k, "nt") * self.decay, 0.0)
        self.qd = self.qs * self.eg
        last = 0 if rev else Cn - 1
        self.last = last
        gl = gcol[last : last + 1, :]
        self.ekd = jnp.exp(gl - gcol)
        self.kd = k * self.ekd
        self.egl = jnp.exp(gl)


def _gdn_scan_fwd(qkvn, gates, gc):
    T = qkvn.shape[1]
    N = T // CHUNK
    nh, hb = GDN_HEADS, GDN_HB
    W = hb * GDN_DIM
    chains = [(d, hh) for d in range(2) for hh in range(hb)]

    def body(qf_ref, qb_ref, gf_ref, gb_ref, cf_ref, cb_ref, of_ref, ob_ref, sf_ref, sb_ref, af_ref, ab_ref, s_scr):
        n, b = pl.program_id(0), pl.program_id(1)

        @pl.when(n == 0)
        def _():
            for d, hh in chains:
                s_scr[d, b * hb + hh] = jnp.zeros((GDN_DIM, GDN_DIM), f32)

        dirs = ((qf_ref, gf_ref, cf_ref, of_ref, sf_ref, af_ref), (qb_ref, gb_ref, cb_ref, ob_ref, sb_ref, ab_ref))
        s_in = {ch: s_scr[ch[0], b * hb + ch[1]] for ch in chains}
        s_out = {}
        for d, hh in chains:
            q_ref, g_ref, c_ref, o_ref, st_ref, a_ref = dirs[d]
            h = b * hb + hh
            sl = slice(hh * GDN_DIM, (hh + 1) * GDN_DIM)
            gcol = _lane_col(c_ref[...], h + 8 * d)
            beta = _lane_col(g_ref[...], 16 + h + 8 * d)
            ck = _GdnChunk(q_ref[0, :, sl], q_ref[1, :, sl], q_ref[2, :, sl], gcol, beta, d == 1)
            S = s_in[d, hh]
            st_ref[0, hh] = S
            a_ref[0, hh] = ck.A
            v_new = ck.u - _mx(ck.w, S, "nn")
            o_ref[:, sl] = _mx(ck.qd, S, "nn") + _mx(ck.attn, v_new, "nn")
            s_out[d, hh] = S * ck.egl + _mx(ck.kd, v_new, "tn")
        for d, hh in chains:
            s_scr[d, b * hb + hh] = s_out[d, hh]

    fw, bw = (lambda n: n), (lambda n: N - 1 - n)
    qsp = lambda ci: pl.BlockSpec((3, CHUNK, W), lambda n, b: (0, ci(n), b))
    gsp = lambda ci: pl.BlockSpec((CHUNK, LANES), lambda n, b: (ci(n), 0))
    osp = lambda ci: pl.BlockSpec((CHUNK, W), lambda n, b: (ci(n), b))
    ssp = lambda ci: pl.BlockSpec((1, hb, GDN_DIM, GDN_DIM), lambda n, b: (ci(n), b, 0, 0))
    asp = lambda ci: pl.BlockSpec((1, hb, CHUNK, CHUNK), lambda n, b: (ci(n), b, 0, 0))
    o_shape = jax.ShapeDtypeStruct((T, nh * GDN_DIM), f32)
    s_shape = jax.ShapeDtypeStruct((N, nh, GDN_DIM, GDN_DIM), f32)
    a_shape = jax.ShapeDtypeStruct((N, nh, CHUNK, CHUNK), f32)
    return pl.pallas_call(
        body,
        out_shape=(o_shape, o_shape, s_shape, s_shape, a_shape, a_shape),
        grid=(N, nh // hb),
        in_specs=[qsp(fw), qsp(bw), gsp(fw), gsp(bw), gsp(fw), gsp(bw)],
        out_specs=(osp(fw), osp(bw), ssp(fw), ssp(bw), asp(fw), asp(bw)),
        scratch_shapes=[pltpu.VMEM((2, nh, GDN_DIM, GDN_DIM), f32)],
        compiler_params=_cparams(("arbitrary", "arbitrary")),
        name="gdn_scan_fwd",
    )(qkvn, qkvn, gates, gates, gc, gc)


def _gdn_scan_bwd(qkvn, gates, gc, st_f, st_b, a_f, a_b, do):
    T = qkvn.shape[1]
    N = T // CHUNK
    nh, hb = GDN_HEADS, GDN_HB
    nb = nh // hb
    W = hb * GDN_DIM
    chains = [(d, hh) for d in range(2) for hh in range(hb)]

    def body(qf_ref, qb_ref, gf_ref, gb_ref, cf_ref, cb_ref, sf_ref, sb_ref, af_ref, ab_ref, dof_ref, dob_ref,
             dqf_ref, dqb_ref, dgf_ref, dgb_ref, ds_scr):
        n, b = pl.program_id(0), pl.program_id(1)

        @pl.when(n == 0)
        def _():
            for d, hh in chains:
                ds_scr[d, b * hb + hh] = jnp.zeros((GDN_DIM, GDN_DIM), f32)

        @pl.when(b == 0)
        def _():
            dgf_ref[...] = jnp.zeros((CHUNK, LANES), f32)
            dgb_ref[...] = jnp.zeros((CHUNK, LANES), f32)

        dirs = ((qf_ref, gf_ref, cf_ref, sf_ref, af_ref, dof_ref, dqf_ref), (qb_ref, gb_ref, cb_ref, sb_ref, ab_ref, dob_ref, dqb_ref))
        ds_in = {ch: ds_scr[ch[0], b * hb + ch[1]] for ch in chains}
        ds_out = {}
        lane = _iota2((CHUNK, LANES), 1)
        acc = [dgf_ref[...], dgb_ref[...]]
        for d, hh in chains:
            q_ref, g_ref, c_ref, st_ref, a_ref, do_ref, dq_ref = dirs[d]
            h = b * hb + hh
            sl = slice(hh * GDN_DIM, (hh + 1) * GDN_DIM)
            gcol = _lane_col(c_ref[...], h + 8 * d)
            beta = _lane_col(g_ref[...], 16 + h + 8 * d)
            ck = _GdnChunk(q_ref[0, :, sl], q_ref[1, :, sl], q_ref[2, :, sl], gcol, beta, d == 1, A=a_ref[0, hh])
            k, v = ck.k, ck.v
            S = st_ref[0, hh]
            dS = ds_in[d, hh]
            dov = do_ref[:, sl]
            v_new = ck.u - _mx(ck.w, S, "nn")
            d_vnew = _mx(ck.attn, dov, "tn") + _mx(ck.kd, dS, "nn")
            d_attn = jnp.where(ck.incl, _mx(dov, v_new, "nt"), 0.0)
            d_qd = _mx(dov, S, "nt")
            d_kd = _mx(v_new, dS, "nt")
            d_gl = jnp.sum(jnp.sum(dS * S, axis=1, keepdims=True), axis=0, keepdims=True) * ck.egl
            ds_out[d, hh] = dS * ck.egl + _mx(ck.qd, dov, "tn") - _mx(ck.w, d_vnew, "tn")
            d_w = -_mx(d_vnew, S, "nt")
            d_rhs = _h3s(ck.A2, _split2(jnp.concatenate([d_vnew, d_w], axis=1)), "tn")
            d_vb, d_kbg = d_rhs[:, :GDN_DIM], d_rhs[:, GDN_DIM:]
            dL = -jnp.where(ck.strict, _h3(d_rhs, ck.sol, "nt"), 0.0)
            P = dL * ck.decay
            Q = d_attn * ck.decay
            d_kb = _mx(P, k, "nn") + d_kbg * ck.eg
            d_qs = _mx(Q, k, "nn") + d_qd * ck.eg
            d_k = _mx(P, ck.kb, "tn") + _mx(Q, ck.qs, "tn") + d_kd * ck.ekd + d_kb * ck.beta
            E = dL * ck.L + d_attn * ck.attn
            kdsum = jnp.sum(d_kd * ck.kd, axis=1, keepdims=True)
            d_gc = (jnp.sum(E, axis=1, keepdims=True) - _colsum_col(E)
                    + jnp.sum(d_kbg * ck.kbg, axis=1, keepdims=True)
                    + jnp.sum(d_qd * ck.qd, axis=1, keepdims=True) - kdsum)
            d_gl = d_gl + jnp.sum(kdsum, axis=0, keepdims=True)
            row = _iota2((CHUNK, 1), 0)
            d_gc = d_gc + jnp.where(row == ck.last, d_gl, 0.0)
            d_beta = jnp.sum(d_vb * v, axis=1, keepdims=True) + jnp.sum(d_kb * k, axis=1, keepdims=True)
            dq_ref[0, :, sl] = d_qs * (GDN_DIM**-0.5)
            dq_ref[1, :, sl] = d_k
            dq_ref[2, :, sl] = d_vb * ck.beta
            acc[d] = jnp.where(lane == h + 8 * d, d_gc, acc[d])
            acc[d] = jnp.where(lane == 16 + h + 8 * d, d_beta, acc[d])
        for d, hh in chains:
            ds_scr[d, b * hb + hh] = ds_out[d, hh]
        dgf_ref[...] = acc[0]
        dgb_ref[...] = acc[1]

        @pl.when(b == nb - 1)
        def _():
            for d, dg_ref in enumerate((dgf_ref, dgb_ref)):
                blk = dg_ref[...]
                dg_ref[...] = jnp.where(lane < 16, _exact01(_chunk_tri(CHUNK, d == 0), blk), blk)

    fw, bw = (lambda n: N - 1 - n), (lambda n: n)
    qsp = lambda ci: pl.BlockSpec((3, CHUNK, W), lambda n, b: (0, ci(n), b))
    gsp = lambda ci: pl.BlockSpec((CHUNK, LANES), lambda n, b: (ci(n), 0))
    osp = lambda ci: pl.BlockSpec((CHUNK, W), lambda n, b: (ci(n), b))
    ssp = lambda ci: pl.BlockSpec((1, hb, GDN_DIM, GDN_DIM), lambda n, b: (ci(n), b, 0, 0))
    asp = lambda ci: pl.BlockSpec((1, hb, CHUNK, CHUNK), lambda n, b: (ci(n), b, 0, 0))
    dq_shape = jax.ShapeDtypeStruct((3, T, nh * GDN_DIM), f32)
    dg_shape = jax.ShapeDtypeStruct((T, LANES), f32)
    return pl.pallas_call(
        body,
        out_shape=(dq_shape, dq_shape, dg_shape, dg_shape),
        grid=(N, nb),
        in_specs=[qsp(fw), qsp(bw), gsp(fw), gsp(bw), gsp(fw), gsp(bw), ssp(fw), ssp(bw), asp(fw), asp(bw), osp(fw), osp(bw)],
        out_specs=(qsp(fw), qsp(bw), gsp(fw), gsp(bw)),
        scratch_shapes=[pltpu.VMEM((2, nh, GDN_DIM, GDN_DIM), f32)],
        compiler_params=_cparams(("arbitrary", "arbitrary")),
        name="gdn_scan_bwd",
    )(qkvn, qkvn, gates, gates, gc, gc, st_f, st_b, a_f, a_b, do, do)


def _out_norm(o_f, o_b, proj, gate_blk, w, hd, name, tb=256):
    T, W = o_f.shape
    nh = W // hd

    def body(of_ref, ob_ref, z_ref, w_ref, y_ref):
        wv = w_ref[...]
        for i in range(nh):
            sl = slice(i * hd, (i + 1) * hd)
            o = of_ref[:, sl] + ob_ref[:, sl]
            r = lax.rsqrt(jnp.mean(o * o, axis=1, keepdims=True) + NORM_EPS)
            y_ref[:, sl] = (o * r * wv * _silu(z_ref[:, sl])).astype(y_ref.dtype)

    blk = pl.BlockSpec((tb, W), lambda i: (i, 0))
    return pl.pallas_call(
        body,
        out_shape=jax.ShapeDtypeStruct((T, W), MXU_DTYPE),
        grid=(T // tb,),
        in_specs=[blk, blk, pl.BlockSpec((tb, W), lambda i: (i, gate_blk)), pl.BlockSpec((1, hd), lambda i: (0, 0))],
        out_specs=blk,
        compiler_params=_cparams(("parallel",)),
        name=name,
    )(o_f, o_b, proj, w)


def _out_norm_bwd(dy, o_f, o_b, proj, gate_blk, w, hd, name, dproj, tb=256):
    T, W = o_f.shape
    nh = W // hd

    def body(dy_ref, of_ref, ob_ref, z_ref, w_ref, _, do_ref, dz_ref, dw_ref):
        i = pl.program_id(0)
        wv = w_ref[...]
        dw = jnp.zeros((1, hd), f32)
        for a in range(nh):
            sl = slice(a * hd, (a + 1) * hd)
            o = of_ref[:, sl] + ob_ref[:, sl]
            z = z_ref[:, sl]
            dyv = dy_ref[:, sl]
            r = lax.rsqrt(jnp.mean(o * o, axis=1, keepdims=True) + NORM_EPS)
            orr = o * r
            dn = dyv * _silu(z)
            dz_ref[:, sl] = (dyv * orr * wv * _silu_grad(z)).astype(dz_ref.dtype)
            dw = dw + jnp.sum(dn * orr, axis=0, keepdims=True)
            dnw = dn * wv
            do_ref[:, sl] = r * dnw - o * (r * r * r) * jnp.mean(dnw * o, axis=1, keepdims=True)

        @pl.when(i == 0)
        def _():
            dw_ref[...] = dw

        @pl.when(i > 0)
        def _():
            dw_ref[...] += dw

    blk = pl.BlockSpec((tb, W), lambda i: (i, 0))
    row = pl.BlockSpec((1, hd), lambda i: (0, 0))
    return pl.pallas_call(
        body,
        out_shape=(jax.ShapeDtypeStruct((T, W), f32), jax.ShapeDtypeStruct(dproj.shape, dproj.dtype), jax.ShapeDtypeStruct((1, hd), f32)),
        grid=(T // tb,),
        in_specs=[blk, blk, blk, pl.BlockSpec((tb, W), lambda i: (i, gate_blk)), row, _ANY_SPEC],
        out_specs=(blk, pl.BlockSpec((tb, W), lambda i: (i, gate_blk)), row),
        input_output_aliases={5: 1},
        compiler_params=_cparams(("arbitrary",)),
        name=name,
    )(dy, o_f, o_b, proj, w, dproj)


def _gla_prep(proj, w2f, w2b, b2f, b2b, tb=256):
    T = proj.shape[0]
    W = GLA_HEADS * GLA_DK

    def body(s_ref, wf_ref, wb_ref, bf_ref, bb_ref, gf_ref, gb_ref):
        small = s_ref[...]
        gkf = -_softplus(-(_mx(small, wf_ref[...], "nn") + bf_ref[...])) * (1.0 / GATE_NORMALIZER)
        gkb = -_softplus(-(_mx(small, wb_ref[...], "nn") + bb_ref[...])) * (1.0 / GATE_NORMALIZER)
        gf_ref[...] = _exact01(_chunk_tri(tb, False), gkf)
        gb_ref[...] = _exact01(_chunk_tri(tb, True), gkb)

    blk = pl.BlockSpec((tb, W), lambda i: (i, 0))
    wsp = pl.BlockSpec((LANES, W), lambda i: (0, 0))
    row = pl.BlockSpec((1, W), lambda i: (0, 0))
    return pl.pallas_call(
        body,
        out_shape=(jax.ShapeDtypeStruct((T, W), f32), jax.ShapeDtypeStruct((T, W), f32)),
        grid=(T // tb,),
        in_specs=[pl.BlockSpec((tb, LANES), lambda i: (i, SMALL_OFF // LANES)), wsp, wsp, row, row],
        out_specs=(blk, blk),
        compiler_params=_cparams(("parallel",)),
        name="gla_prep",
    )(proj, w2f, w2b, b2f, b2b)


def _gla_prep_bwd(proj, w2f, w2b, b2f, b2b, dgk_f, dgk_b, tb=256):
    T = proj.shape[0]
    W = GLA_HEADS * GLA_DK

    def body(s_ref, wf_ref, wb_ref, bf_ref, bb_ref, df_ref, db_ref, ds_ref, dwf_ref, dwb_ref, dbf_ref, dbb_ref):
        i = pl.program_id(0)
        small = s_ref[...]
        dsm = jnp.zeros((tb, LANES), f32)
        parts = []
        for w_ref, b_ref, d_ref in ((wf_ref, bf_ref, df_ref), (wb_ref, bb_ref, db_ref)):
            pre = _mx(small, w_ref[...], "nn") + b_ref[...]
            dpre = d_ref[...] * (1.0 / GATE_NORMALIZER) * _sigmoid(-pre)
            dsm = dsm + _mx(dpre, w_ref[...], "nt")
            parts.append((_mx(small, dpre, "tn"), jnp.sum(dpre, axis=0, keepdims=True)))
        ds_ref[...] = dsm

        @pl.when(i == 0)
        def _():
            dwf_ref[...], dbf_ref[...] = parts[0]
            dwb_ref[...], dbb_ref[...] = parts[1]

        @pl.when(i > 0)
        def _():
            dwf_ref[...] += parts[0][0]
            dbf_ref[...] += parts[0][1]
            dwb_ref[...] += parts[1][0]
            dbb_ref[...] += parts[1][1]

    blk = pl.BlockSpec((tb, W), lambda i: (i, 0))
    wsp = pl.BlockSpec((LANES, W), lambda i: (0, 0))
    row = pl.BlockSpec((1, W), lambda i: (0, 0))
    return pl.pallas_call(
        body,
        out_shape=(jax.ShapeDtypeStruct((T, LANES), f32), jax.ShapeDtypeStruct((LANES, W), f32), jax.ShapeDtypeStruct((LANES, W), f32),
                   jax.ShapeDtypeStruct((1, W), f32), jax.ShapeDtypeStruct((1, W), f32)),
        grid=(T // tb,),
        in_specs=[pl.BlockSpec((tb, LANES), lambda i: (i, SMALL_OFF // LANES)), wsp, wsp, row, row, blk, blk],
        out_specs=(pl.BlockSpec((tb, LANES), lambda i: (i, 0)), wsp, wsp, row, row),
        compiler_params=_cparams(("arbitrary",)),
        name="gla_prep_bwd",
    )(proj, w2f, w2b, b2f, b2b, dgk_f, dgk_b)


Q_B_BLK, K_B_BLK, V_B_BLK = 4096 // GLA_DK, 4608 // GLA_DK, 5120 // GLA_DV


class _GlaChunk:
    def __init__(self, q, k, G, rev):
        Cn = q.shape[0]
        ii, jj = _iota2((Cn, Cn), 0), _iota2((Cn, Cn), 1)
        self.incl = (ii <= jj) if rev else (ii >= jj)
        self.eG = jnp.exp(G)
        self.enG = jnp.exp(-G)
        self.qg = q * (GLA_DK**-0.5) * self.eG
        self.kg = k * self.enG
        self.attn = jnp.where(self.incl, _mx(self.qg, self.kg, "nt"), 0.0)
        last = 0 if rev else Cn - 1
        self.last = last
        gl = G[last : last + 1, :]
        self.ekd = jnp.exp(gl - G)
        self.kd = k * self.ekd
        self.egl = jnp.exp(gl)


def _gla_scan_fwd(proj, G_f, G_b):
    T = proj.shape[0]
    N = T // CHUNK
    nh = GLA_HEADS

    def body(qf_ref, kf_ref, vf_ref, gf_ref, qb_ref, kb_ref, vb_ref, gb_ref, of_ref, ob_ref, sf_ref, sb_ref, s_scr):
        n, h = pl.program_id(0), pl.program_id(1)

        @pl.when(n == 0)
        def _():
            s_scr[0, h] = jnp.zeros((GLA_DV, GLA_DK), f32)
            s_scr[1, h] = jnp.zeros((GLA_DV, GLA_DK), f32)

        for d, (q_ref, k_ref, v_ref, g_ref, o_ref, st_ref) in enumerate(
            ((qf_ref, kf_ref, vf_ref, gf_ref, of_ref, sf_ref), (qb_ref, kb_ref, vb_ref, gb_ref, ob_ref, sb_ref))
        ):
            ck = _GlaChunk(q_ref[...], k_ref[...], g_ref[...], d == 1)
            v = v_ref[...]
            St = s_scr[d, h]
            st_ref[0, 0] = St
            o_ref[...] = _mx(ck.qg, St, "nt") + _mx(ck.attn, v, "nn")
            s_scr[d, h] = St * ck.egl + _mx(v, ck.kd, "tn")

    def specs(cidx):
        return [
            pl.BlockSpec((CHUNK, GLA_DK), lambda n, h: (cidx(n), Q_B_BLK + h)),
            pl.BlockSpec((CHUNK, GLA_DK), lambda n, h: (cidx(n), K_B_BLK + h)),
            pl.BlockSpec((CHUNK, GLA_DV), lambda n, h: (cidx(n), V_B_BLK + h)),
            pl.BlockSpec((CHUNK, GLA_DK), lambda n, h: (cidx(n), h)),
        ]

    fw, bw = (lambda n: n), (lambda n: N - 1 - n)
    o_shape = jax.ShapeDtypeStruct((T, nh * GLA_DV), f32)
    s_shape = jax.ShapeDtypeStruct((N, nh, GLA_DV, GLA_DK), f32)
    return pl.pallas_call(
        body,
        out_shape=(o_shape, o_shape, s_shape, s_shape),
        grid=(N, nh),
        in_specs=specs(fw) + specs(bw),
        out_specs=(
            pl.BlockSpec((CHUNK, GLA_DV), lambda n, h: (n, h)),
            pl.BlockSpec((CHUNK, GLA_DV), lambda n, h: (N - 1 - n, h)),
            pl.BlockSpec((1, 1, GLA_DV, GLA_DK), lambda n, h: (n, h, 0, 0)),
            pl.BlockSpec((1, 1, GLA_DV, GLA_DK), lambda n, h: (N - 1 - n, h, 0, 0)),
        ),
        scratch_shapes=[pltpu.VMEM((2, nh, GLA_DV, GLA_DK), f32)],
        compiler_params=_cparams(("arbitrary", "arbitrary")),
        name="gla_scan_fwd",
    )(proj, proj, proj, G_f, proj, proj, proj, G_b)


def _gla_scan_bwd(proj, G_f, G_b, st_f, st_b, do):
    T = proj.shape[0]
    N = T // CHUNK
    nh = GLA_HEADS

    def body(qf_ref, kf_ref, vf_ref, gf_ref, sf_ref, dof_ref, qb_ref, kb_ref, vb_ref, gb_ref, sb_ref, dob_ref,
             dqf_ref, dkf_ref, dvf_ref, dgf_ref, dqb_ref, dkb_ref, dvb_ref, dgb_ref, ds_scr):
        n, h = pl.program_id(0), pl.program_id(1)

        @pl.when(n == 0)
        def _():
            ds_scr[0, h] = jnp.zeros((GLA_DV, GLA_DK), f32)
            ds_scr[1, h] = jnp.zeros((GLA_DV, GLA_DK), f32)

        for d, (q_ref, k_ref, v_ref, g_ref, st_ref, do_ref, dq_ref, dk_ref, dv_ref, dg_ref) in enumerate(
            ((qf_ref, kf_ref, vf_ref, gf_ref, sf_ref, dof_ref, dqf_ref, dkf_ref, dvf_ref, dgf_ref),
             (qb_ref, kb_ref, vb_ref, gb_ref, sb_ref, dob_ref, dqb_ref, dkb_ref, dvb_ref, dgb_ref))
        ):
            rev = d == 1
            ck = _GlaChunk(q_ref[...], k_ref[...], g_ref[...], rev)
            v = v_ref[...]
            St = st_ref[0, 0]
            dSt = ds_scr[d, h]
            dov = do_ref[...]
            d_attn = jnp.where(ck.incl, _mx(dov, v, "nt"), 0.0)
            d_qg = _mx(dov, St, "nn") + _mx(d_attn, ck.kg, "nn")
            d_kg = _mx(d_attn, ck.qg, "tn")
            dv_ref[...] = _mx(ck.attn, dov, "tn") + _mx(ck.kd, dSt, "nt")
            d_kd = _mx(v, dSt, "nn")
            ds_scr[d, h] = dSt * ck.egl + _mx(dov, ck.qg, "tn")
            kdd = d_kd * ck.kd
            d_gl = jnp.sum(dSt * St, axis=0, keepdims=True) * ck.egl + jnp.sum(kdd, axis=0, keepdims=True)
            dq_ref[...] = d_qg * ck.eG * (GLA_DK**-0.5)
            dk_ref[...] = d_kg * ck.enG + d_kd * ck.ekd
            d_G = d_qg * ck.qg - d_kg * ck.kg - kdd
            row = _iota2((CHUNK, GLA_DK), 0)
            d_G = d_G + jnp.where(row == ck.last, d_gl, 0.0)
            dg_ref[...] = _exact01(_chunk_tri(CHUNK, not rev), d_G)

    def in_specs(cidx):
        return [
            pl.BlockSpec((CHUNK, GLA_DK), lambda n, h: (cidx(n), Q_B_BLK + h)),
            pl.BlockSpec((CHUNK, GLA_DK), lambda n, h: (cidx(n), K_B_BLK + h)),
            pl.BlockSpec((CHUNK, GLA_DV), lambda n, h: (cidx(n), V_B_BLK + h)),
            pl.BlockSpec((CHUNK, GLA_DK), lambda n, h: (cidx(n), h)),
            pl.BlockSpec((1, 1, GLA_DV, GLA_DK), lambda n, h: (cidx(n), h, 0, 0)),
            pl.BlockSpec((CHUNK, GLA_DV), lambda n, h: (cidx(n), h)),
        ]

    def out_specs(cidx):
        return [
            pl.BlockSpec((CHUNK, GLA_DK), lambda n, h: (cidx(n), h)),
            pl.BlockSpec((CHUNK, GLA_DK), lambda n, h: (cidx(n), h)),
            pl.BlockSpec((CHUNK, GLA_DV), lambda n, h: (cidx(n), h)),
            pl.BlockSpec((CHUNK, GLA_DK), lambda n, h: (cidx(n), h)),
        ]

    fw, bw = (lambda n: N - 1 - n), (lambda n: n)
    k_shape = jax.ShapeDtypeStruct((T, nh * GLA_DK), f32)
    v_shape = jax.ShapeDtypeStruct((T, nh * GLA_DV), f32)
    return pl.pallas_call(
        body,
        out_shape=(k_shape, k_shape, v_shape, k_shape) * 2,
        grid=(N, nh),
        in_specs=in_specs(fw) + in_specs(bw),
        out_specs=tuple(out_specs(fw) + out_specs(bw)),
        scratch_shapes=[pltpu.VMEM((2, nh, GLA_DV, GLA_DK), f32)],
        compiler_params=_cparams(("arbitrary", "arbitrary")),
        name="gla_scan_bwd",
    )(proj, proj, proj, G_f, st_f, do, proj, proj, proj, G_b, st_b, do)


GATE_A_BLK, GATE_B_BLK, Z_BLK, G_B_BLK = 6144 // 1024, 7168 // 1024, 3072 // 1024, 8192 // 1024
DP_GATES_BLK, DP_GLA_QKV_BLK, DP_SMALL_BLK = 6144 // 2048, 4096 // 2048, SMALL_OFF // 512
_ANY_SPEC = pl.BlockSpec(memory_space=pl.ANY)


def _mid(og, obg, proj, x, target, wpg, wpl, wout, w_post, tb=256):
    T = x.shape[0]
    Dm = D_MODEL

    def body(og_ref, obg_ref, ga_ref, gb_ref, x_ref, t_ref, wpg_ref, wpl_ref, wo_ref, wp_ref,
             loss_ref, dwp_ref, dy_ref, m_ref, dout_ref, dya_ref, dyb_ref, dgates_ref, dog_ref, dobg_ref):
        i = pl.program_id(0)
        ya = _mx(og_ref[...], wpg_ref[...], "nn")
        yb = _mx(obg_ref[...], wpl_ref[...], "nn")
        sa, sb = _sigmoid(ga_ref[...]), _sigmoid(gb_ref[...])
        m = sa * ya + sb * yb
        m_ref[...] = m.astype(m_ref.dtype)
        out = _mx(m, wo_ref[...], "nn")
        r = lax.rsqrt(jnp.mean(out * out, axis=1, keepdims=True) + NORM_EPS)
        wp = wp_ref[...]
        e = x_ref[...] + out * r * wp - t_ref[...]
        dy = e * (1.0 / Dm)
        dy_ref[...] = dy
        lpart = jnp.sum(e * e, axis=0, keepdims=True)
        wpart = jnp.sum(dy * out * r, axis=0, keepdims=True)
        dyn = dy * wp
        dout = r * dyn - out * (r * r * r) * jnp.mean(dyn * out, axis=1, keepdims=True)
        dout_ref[...] = dout.astype(dout_ref.dtype)
        dm = _mx(dout, wo_ref[...], "nt")
        dya, dyb = dm * sa, dm * sb
        dya_ref[...] = dya.astype(dya_ref.dtype)
        dyb_ref[...] = dyb.astype(dyb_ref.dtype)
        dgates_ref[:, :Dm] = (dm * ya * sa * (1.0 - sa)).astype(dgates_ref.dtype)
        dgates_ref[:, Dm:] = (dm * yb * sb * (1.0 - sb)).astype(dgates_ref.dtype)
        dog_ref[...] = _mx(dya, wpg_ref[...], "nt")
        dobg_ref[...] = _mx(dyb, wpl_ref[...], "nt")

        @pl.when(i == 0)
        def _():
            loss_ref[...] = lpart
            dwp_ref[...] = wpart

        @pl.when(i > 0)
        def _():
            loss_ref[...] += lpart
            dwp_ref[...] += wpart

    blk = pl.BlockSpec((tb, Dm), lambda i: (i, 0))
    row = pl.BlockSpec((1, Dm), lambda i: (0, 0))
    wsp = pl.BlockSpec((Dm, Dm), lambda i: (0, 0))
    act = jax.ShapeDtypeStruct((T, Dm), MXU_DTYPE)
    big = jax.ShapeDtypeStruct((T, Dm), f32)
    vec = jax.ShapeDtypeStruct((1, Dm), f32)
    return pl.pallas_call(
        body,
        out_shape=(vec, vec, big, act, act, act, act, jax.ShapeDtypeStruct((T, N_CAT), MXU_DTYPE), big, big),
        grid=(T // tb,),
        in_specs=[blk, blk, pl.BlockSpec((tb, Dm), lambda i: (i, GATE_A_BLK)), pl.BlockSpec((tb, Dm), lambda i: (i, GATE_B_BLK)),
                  blk, blk, wsp, wsp, wsp, row],
        out_specs=(row, row) + (blk,) * 5 + (pl.BlockSpec((tb, 2 * Dm), lambda i: (i, DP_GATES_BLK)), blk, blk),
        compiler_params=_cparams(("arbitrary",)),
        name="mid",
    )(og, obg, proj, proj, x, target, wpg, wpl, wout, w_post)


def _heads(ref, lead, n, width, rows=slice(None)):
    return [ref[lead + (rows, slice(i * width, (i + 1) * width))] for i in range(n)]


class _GdnBatch:
    def __init__(self, q, k, v, gcol, grow, beta, nf, A=None):
        B, Cn = q.shape[0], q.shape[1]
        self.nf = nf
        self.incl, self.strict, eye = _dir_masks(B, nf, Cn)
        self.decay = jnp.where(self.incl, jnp.exp(jnp.where(self.incl, gcol - grow, 0.0)), 0.0)
        self.k, self.v, self.beta = k, v, beta
        self.qs = q * (GDN_DIM**-0.5)
        self.kb = k * beta
        self.L = jnp.where(self.strict, _mx(self.kb, k, "nt") * self.decay, 0.0)
        if A is None:
            A = _inv_unit_lower(self.L, jnp.where(eye, 1.0, 0.0).astype(f32))
        self.A = A
        self.A2 = _split2(A)
        self.eg = jnp.exp(gcol)
        self.kbg = self.kb * self.eg
        self.sol = _h3s(self.A2, _split2(jnp.concatenate([v * beta, self.kbg], axis=2)), "nn")
        self.u = self.sol[:, :, :GDN_DIM]
        self.w = self.sol[:, :, GDN_DIM:]
        self.attn = jnp.where(self.incl, _mx(self.qs, k, "nt") * self.decay, 0.0)
        self.qd = self.qs * self.eg
        gl = _last_row(gcol, nf)
        self.ekd = jnp.exp(gl - gcol)
        self.kd = k * self.ekd
        self.egl = jnp.exp(gl)


def _gdn_chain_inputs(qf_ref, qb_ref, gf_ref, gb_ref, cf_ref, cb_ref, b, hb):
    qkv = [jnp.stack(_heads(qf_ref, (i,), hb, GDN_DIM) + _heads(qb_ref, (i,), hb, GDN_DIM)) for i in range(3)]
    cols, rows, betas = [], [], []
    for d, (g_ref, c_ref) in enumerate(((gf_ref, cf_ref), (gb_ref, cb_ref))):
        gates, gc = g_ref[...], c_ref[...]
        for hh in range(hb):
            h = b * hb + hh
            col = _lane_col(gc, h + 8 * d)
            cols.append(col)
            rows.append(jnp.broadcast_to(col, (CHUNK, LANES)).T[0:1, :])
            betas.append(_lane_col(gates, 16 + h + 8 * d))
    return qkv[0], qkv[1], qkv[2], jnp.stack(cols), jnp.stack(rows), jnp.stack(betas)


def _gdn_fwd(qkvn, gates, gc):
    T = qkvn.shape[1]
    N = T // CHUNK
    nh, hb = GDN_HEADS, GDN_HB
    W = hb * GDN_DIM

    def body(qf_ref, qb_ref, gf_ref, gb_ref, cf_ref, cb_ref, of_ref, ob_ref, sf_ref, sb_ref, af_ref, ab_ref, s_scr):
        n, b = pl.program_id(0), pl.program_id(1)
        hs = pl.ds(b * hb, hb)

        @pl.when(n == 0)
        def _():
            s_scr[0, hs] = jnp.zeros((hb, GDN_DIM, GDN_DIM), f32)
            s_scr[1, hs] = jnp.zeros((hb, GDN_DIM, GDN_DIM), f32)

        q, k, v, gcol, grow, beta = _gdn_chain_inputs(qf_ref, qb_ref, gf_ref, gb_ref, cf_ref, cb_ref, b, hb)
        ck = _GdnBatch(q, k, v, gcol, grow, beta, hb)
        S = jnp.concatenate([s_scr[0, hs], s_scr[1, hs]], axis=0)
        sf_ref[0], sb_ref[0] = S[:hb], S[hb:]
        af_ref[0], ab_ref[0] = ck.A[:hb], ck.A[hb:]
        v_new = ck.u - _mx(ck.w, S, "nn")
        o = _mx(ck.qd, S, "nn") + _mx(ck.attn, v_new, "nn")
        S_new = S * ck.egl + _mx(ck.kd, v_new, "tn")
        for hh in range(hb):
            sl = slice(hh * GDN_DIM, (hh + 1) * GDN_DIM)
            of_ref[:, sl] = o[hh]
            ob_ref[:, sl] = o[hb + hh]
        s_scr[0, hs] = S_new[:hb]
        s_scr[1, hs] = S_new[hb:]

    fw, bw = (lambda n: n), (lambda n: N - 1 - n)
    qsp = lambda ci: pl.BlockSpec((3, CHUNK, W), lambda n, b: (0, ci(n), b))
    gsp = lambda ci: pl.BlockSpec((CHUNK, LANES), lambda n, b: (ci(n), 0))
    osp = lambda ci: pl.BlockSpec((CHUNK, W), lambda n, b: (ci(n), b))
    ssp = lambda ci: pl.BlockSpec((1, hb, GDN_DIM, GDN_DIM), lambda n, b: (ci(n), b, 0, 0))
    asp = lambda ci: pl.BlockSpec((1, hb, CHUNK, CHUNK), lambda n, b: (ci(n), b, 0, 0))
    o_shape = jax.ShapeDtypeStruct((T, nh * GDN_DIM), f32)
    s_shape = jax.ShapeDtypeStruct((N, nh, GDN_DIM, GDN_DIM), f32)
    a_shape = jax.ShapeDtypeStruct((N, nh, CHUNK, CHUNK), f32)
    return pl.pallas_call(
        body,
        out_shape=(o_shape, o_shape, s_shape, s_shape, a_shape, a_shape),
        grid=(N, nh // hb),
        in_specs=[qsp(fw), qsp(bw), gsp(fw), gsp(bw), gsp(fw), gsp(bw)],
        out_specs=(osp(fw), osp(bw), ssp(fw), ssp(bw), asp(fw), asp(bw)),
        scratch_shapes=[pltpu.VMEM((2, nh, GDN_DIM, GDN_DIM), f32)],
        compiler_params=_cparams(("arbitrary", "arbitrary")),
        name="gdn_scan_fwd",
    )(qkvn, qkvn, gates, gates, gc, gc)


def _gdn_bwd(qkvn, gates, gc, st_f, st_b, a_f, a_b, do):
    T = qkvn.shape[1]
    N = T // CHUNK
    nh, hb = GDN_HEADS, GDN_HB
    nb = nh // hb
    W = hb * GDN_DIM

    def body(qf_ref, qb_ref, gf_ref, gb_ref, cf_ref, cb_ref, sf_ref, sb_ref, af_ref, ab_ref, dof_ref, dob_ref,
             dqf_ref, dqb_ref, dgf_ref, dgb_ref, ds_scr):
        n, b = pl.program_id(0), pl.program_id(1)
        hs = pl.ds(b * hb, hb)

        @pl.when(n == 0)
        def _():
            ds_scr[0, hs] = jnp.zeros((hb, GDN_DIM, GDN_DIM), f32)
            ds_scr[1, hs] = jnp.zeros((hb, GDN_DIM, GDN_DIM), f32)

        @pl.when(b == 0)
        def _():
            dgf_ref[...] = jnp.zeros((CHUNK, LANES), f32)
            dgb_ref[...] = jnp.zeros((CHUNK, LANES), f32)

        q, k, v, gcol, grow, beta = _gdn_chain_inputs(qf_ref, qb_ref, gf_ref, gb_ref, cf_ref, cb_ref, b, hb)
        ck = _GdnBatch(q, k, v, gcol, grow, beta, hb, A=jnp.concatenate([af_ref[0], ab_ref[0]], axis=0))
        S = jnp.concatenate([sf_ref[0], sb_ref[0]], axis=0)
        dS = jnp.concatenate([ds_scr[0, hs], ds_scr[1, hs]], axis=0)
        dov = jnp.stack(_heads(dof_ref, (), hb, GDN_DIM) + _heads(dob_ref, (), hb, GDN_DIM))
        v_new = ck.u - _mx(ck.w, S, "nn")
        d_vnew = _mx(ck.attn, dov, "tn") + _mx(ck.kd, dS, "nn")
        d_attn = jnp.where(ck.incl, _mx(dov, v_new, "nt"), 0.0)
        d_qd = _mx(dov, S, "nt")
        d_kd = _mx(v_new, dS, "nt")
        d_gl = jnp.sum(jnp.sum(dS * S, axis=2, keepdims=True), axis=1, keepdims=True) * ck.egl
        dS_new = dS * ck.egl + _mx(ck.qd, dov, "tn") - _mx(ck.w, d_vnew, "tn")
        ds_scr[0, hs] = dS_new[:hb]
        ds_scr[1, hs] = dS_new[hb:]
        d_w = -_mx(d_vnew, S, "nt")
        d_rhs = _h3s(ck.A2, _split2(jnp.concatenate([d_vnew, d_w], axis=2)), "tn")
        d_vb, d_kbg = d_rhs[:, :, :GDN_DIM], d_rhs[:, :, GDN_DIM:]
        dL = -jnp.where(ck.strict, _h3(d_rhs, ck.sol, "nt"), 0.0)
        P = dL * ck.decay
        Q = d_attn * ck.decay
        d_kb = _mx(P, k, "nn") + d_kbg * ck.eg
        d_qs = _mx(Q, k, "nn") + d_qd * ck.eg
        d_k = _mx(P, ck.kb, "tn") + _mx(Q, ck.qs, "tn") + d_kd * ck.ekd + d_kb * ck.beta
        E = dL * ck.L + d_attn * ck.attn
        kdsum = jnp.sum(d_kd * ck.kd, axis=2, keepdims=True)
        d_gc = (jnp.sum(E, axis=2, keepdims=True) - _colsum_col(E)
                + jnp.sum(d_kbg * ck.kbg, axis=2, keepdims=True)
                + jnp.sum(d_qd * ck.qd, axis=2, keepdims=True) - kdsum)
        d_gc = _add_at_last_row(d_gc, d_gl + jnp.sum(kdsum, axis=1, keepdims=True), hb)
        d_beta = jnp.sum(d_vb * v, axis=2, keepdims=True) + jnp.sum(d_kb * k, axis=2, keepdims=True)
        d_q = d_qs * (GDN_DIM**-0.5)
        d_v = d_vb * ck.beta
        lane = _iota2((CHUNK, LANES), 1)
        for d, (dq_ref, dg_ref) in enumerate(((dqf_ref, dgf_ref), (dqb_ref, dgb_ref))):
            acc = dg_ref[...]
            for hh in range(hb):
                c, h = d * hb + hh, b * hb + hh
                sl = slice(hh * GDN_DIM, (hh + 1) * GDN_DIM)
                dq_ref[0, :, sl] = d_q[c]
                dq_ref[1, :, sl] = d_k[c]
                dq_ref[2, :, sl] = d_v[c]
                acc = jnp.where(lane == h + 8 * d, d_gc[c], acc)
                acc = jnp.where(lane == 16 + h + 8 * d, d_beta[c], acc)
            dg_ref[...] = acc

        @pl.when(b == nb - 1)
        def _():
            for d, dg_ref in enumerate((dgf_ref, dgb_ref)):
                blk = dg_ref[...]
                dg_ref[...] = jnp.where(lane < 16, _exact01(_chunk_tri(CHUNK, d == 0), blk), blk)

    fw, bw = (lambda n: N - 1 - n), (lambda n: n)
    qsp = lambda ci: pl.BlockSpec((3, CHUNK, W), lambda n, b: (0, ci(n), b))
    gsp = lambda ci: pl.BlockSpec((CHUNK, LANES), lambda n, b: (ci(n), 0))
    osp = lambda ci: pl.BlockSpec((CHUNK, W), lambda n, b: (ci(n), b))
    ssp = lambda ci: pl.BlockSpec((1, hb, GDN_DIM, GDN_DIM), lambda n, b: (ci(n), b, 0, 0))
    asp = lambda ci: pl.BlockSpec((1, hb, CHUNK, CHUNK), lambda n, b: (ci(n), b, 0, 0))
    dq_shape = jax.ShapeDtypeStruct((3, T, nh * GDN_DIM), f32)
    dg_shape = jax.ShapeDtypeStruct((T, LANES), f32)
    return pl.pallas_call(
        body,
        out_shape=(dq_shape, dq_shape, dg_shape, dg_shape),
        grid=(N, nb),
        in_specs=[qsp(fw), qsp(bw), gsp(fw), gsp(bw), gsp(fw), gsp(bw), ssp(fw), ssp(bw), asp(fw), asp(bw), osp(fw), osp(bw)],
        out_specs=(qsp(fw), qsp(bw), gsp(fw), gsp(bw)),
        scratch_shapes=[pltpu.VMEM((2, nh, GDN_DIM, GDN_DIM), f32)],
        compiler_params=_cparams(("arbitrary", "arbitrary")),
        name="gdn_scan_bwd",
    )(qkvn, qkvn, gates, gates, gc, gc, st_f, st_b, a_f, a_b, do, do)


class _GlaBatch:
    def __init__(self, q, k, G, nf):
        B, Cn = q.shape[0], q.shape[1]
        self.incl, _, _ = _dir_masks(B, nf, Cn)
        self.eG = jnp.exp(G)
        self.enG = jnp.exp(-G)
        self.qg = q * (GLA_DK**-0.5) * self.eG
        self.kg = k * self.enG
        self.attn = jnp.where(self.incl, _mx(self.qg, self.kg, "nt"), 0.0)
        gl = _last_row(G, nf)
        self.ekd = jnp.exp(gl - G)
        self.kd = k * self.ekd
        self.egl = jnp.exp(gl)


GLA_CPS = 4


def _chunk_rows(s):
    return slice(s * CHUNK, (s + 1) * CHUNK)


def _gla_chain_inputs(refs_f, refs_b, rows_f, rows_b):
    nh = GLA_HEADS
    out = []
    for i, width in enumerate((GLA_DK, GLA_DK, GLA_DV, GLA_DK)):
        out.append(jnp.stack(_heads(refs_f[i], (), nh, width, rows_f) + _heads(refs_b[i], (), nh, width, rows_b)))
    return out


def _gla_specs(rows, ci):
    nh = GLA_HEADS
    return [
        pl.BlockSpec((rows, nh * GLA_DK), lambda n: (ci(n), 4096 // (nh * GLA_DK))),
        pl.BlockSpec((rows, nh * GLA_DK), lambda n: (ci(n), 4608 // (nh * GLA_DK))),
        pl.BlockSpec((rows, nh * GLA_DV), lambda n: (ci(n), 5120 // (nh * GLA_DV))),
        pl.BlockSpec((rows, nh * GLA_DK), lambda n: (ci(n), 0)),
    ]


def _gla_fwd(proj, G_f, G_b):
    T = proj.shape[0]
    N = T // CHUNK
    nh = GLA_HEADS
    cps = min(GLA_CPS, N)
    NB = N // cps

    def body(qf, kf, vf, gf, qb, kb, vb, gb, of_ref, ob_ref, sf_ref, sb_ref, s_scr):
        n = pl.program_id(0)

        @pl.when(n == 0)
        def _():
            s_scr[...] = jnp.zeros((2 * nh, GLA_DV, GLA_DK), f32)

        for s in range(cps):
            sb = cps - 1 - s
            q, k, v, G = _gla_chain_inputs((qf, kf, vf, gf), (qb, kb, vb, gb), _chunk_rows(s), _chunk_rows(sb))
            ck = _GlaBatch(q, k, G, nh)
            St = s_scr[...]
            sf_ref[s], sb_ref[sb] = St[:nh].astype(sf_ref.dtype), St[nh:].astype(sb_ref.dtype)
            o = _mx(ck.qg, St, "nt") + _mx(ck.attn, v, "nn")
            for i in range(nh):
                sl = slice(i * GLA_DV, (i + 1) * GLA_DV)
                of_ref[_chunk_rows(s), sl] = o[i]
                ob_ref[_chunk_rows(sb), sl] = o[nh + i]
            s_scr[...] = St * ck.egl + _mx(v, ck.kd, "tn")

    fw, bw = (lambda n: n), (lambda n: NB - 1 - n)
    osp = lambda ci: pl.BlockSpec((cps * CHUNK, nh * GLA_DV), lambda n: (ci(n), 0))
    ssp = lambda ci: pl.BlockSpec((cps, nh, GLA_DV, GLA_DK), lambda n: (ci(n), 0, 0, 0))
    o_shape = jax.ShapeDtypeStruct((T, nh * GLA_DV), f32)
    s_shape = jax.ShapeDtypeStruct((N, nh, GLA_DV, GLA_DK), MXU_DTYPE)
    return pl.pallas_call(
        body,
        out_shape=(o_shape, o_shape, s_shape, s_shape),
        grid=(NB,),
        in_specs=_gla_specs(cps * CHUNK, fw) + _gla_specs(cps * CHUNK, bw),
        out_specs=(osp(fw), osp(bw), ssp(fw), ssp(bw)),
        scratch_shapes=[pltpu.VMEM((2 * nh, GLA_DV, GLA_DK), f32)],
        compiler_params=_cparams(("arbitrary",)),
        name="gla_scan_fwd",
    )(proj, proj, proj, G_f, proj, proj, proj, G_b)


def _gla_bwd(proj, G_f, G_b, st_f, st_b, do):
    T = proj.shape[0]
    N = T // CHUNK
    nh = GLA_HEADS
    cps = min(GLA_CPS, N)
    NB = N // cps

    def body(qf, kf, vf, gf, sf_ref, dof_ref, qb, kb, vb, gb, sb_ref, dob_ref,
             dqf_ref, dkf_ref, dvf_ref, dgf_ref, dqb_ref, dkb_ref, dvb_ref, dgb_ref, ds_scr):
        n = pl.program_id(0)

        @pl.when(n == 0)
        def _():
            ds_scr[...] = jnp.zeros((2 * nh, GLA_DV, GLA_DK), f32)

        shp = (2 * nh, CHUNK, CHUNK)
        bb, ii, jj = _iota3(shp, 0), _iota3(shp, 1), _iota3(shp, 2)
        tri = jnp.where(jnp.logical_or(jnp.logical_and(bb < nh, jj >= ii), jnp.logical_and(bb >= nh, jj <= ii)), 1.0, 0.0)
        for s in range(cps):
            sf = cps - 1 - s
            rows = (_chunk_rows(sf), _chunk_rows(s))
            q, k, v, G = _gla_chain_inputs((qf, kf, vf, gf), (qb, kb, vb, gb), *rows)
            ck = _GlaBatch(q, k, G, nh)
            St = jnp.concatenate([sf_ref[sf], sb_ref[s]], axis=0).astype(f32)
            dSt = ds_scr[...]
            dov = jnp.stack(_heads(dof_ref, (), nh, GLA_DV, rows[0]) + _heads(dob_ref, (), nh, GLA_DV, rows[1]))
            d_attn = jnp.where(ck.incl, _mx(dov, v, "nt"), 0.0)
            d_qg = _mx(dov, St, "nn") + _mx(d_attn, ck.kg, "nn")
            d_kg = _mx(d_attn, ck.qg, "tn")
            d_v = _mx(ck.attn, dov, "tn") + _mx(ck.kd, dSt, "nt")
            d_kd = _mx(v, dSt, "nn")
            ds_scr[...] = dSt * ck.egl + _mx(dov, ck.qg, "tn")
            kdd = d_kd * ck.kd
            d_gl = jnp.sum(dSt * St, axis=1, keepdims=True) * ck.egl + jnp.sum(kdd, axis=1, keepdims=True)
            d_q = d_qg * ck.eG * (GLA_DK**-0.5)
            d_k = d_kg * ck.enG + d_kd * ck.ekd
            d_gk = _exact01(tri, _add_at_last_row(d_qg * ck.qg - d_kg * ck.kg - kdd, d_gl, nh))
            for d, (dq_ref, dk_ref, dv_ref, dg_ref) in enumerate(((dqf_ref, dkf_ref, dvf_ref, dgf_ref), (dqb_ref, dkb_ref, dvb_ref, dgb_ref))):
                for i in range(nh):
                    c = d * nh + i
                    ks, vs = slice(i * GLA_DK, (i + 1) * GLA_DK), slice(i * GLA_DV, (i + 1) * GLA_DV)
                    dq_ref[rows[d], ks] = d_q[c]
                    dk_ref[rows[d], ks] = d_k[c]
                    dv_ref[rows[d], vs] = d_v[c]
                    dg_ref[rows[d], ks] = d_gk[c]

    R = cps * CHUNK

    def in_specs(ci):
        return _gla_specs(R, ci) + [
            pl.BlockSpec((cps, nh, GLA_DV, GLA_DK), lambda n: (ci(n), 0, 0, 0)),
            pl.BlockSpec((R, nh * GLA_DV), lambda n: (ci(n), 0)),
        ]

    def out_specs(ci):
        ksp = pl.BlockSpec((R, nh * GLA_DK), lambda n: (ci(n), 0))
        return [ksp, ksp, pl.BlockSpec((R, nh * GLA_DV), lambda n: (ci(n), 0)), ksp]

    fw, bw = (lambda n: NB - 1 - n), (lambda n: n)
    k_shape = jax.ShapeDtypeStruct((T, nh * GLA_DK), f32)
    v_shape = jax.ShapeDtypeStruct((T, nh * GLA_DV), f32)
    return pl.pallas_call(
        body,
        out_shape=(k_shape, k_shape, v_shape, k_shape) * 2,
        grid=(NB,),
        in_specs=in_specs(fw) + in_specs(bw),
        out_specs=tuple(out_specs(fw) + out_specs(bw)),
        scratch_shapes=[pltpu.VMEM((2 * nh, GLA_DV, GLA_DK), f32)],
        compiler_params=_cparams(("arbitrary",)),
        name="gla_scan_bwd",
    )(proj, proj, proj, G_f, st_f, do, proj, proj, proj, G_b, st_b, do)


def _gla_dsum(dq_f, dq_b, dk_f, dk_b, dv_f, dv_b, dproj, tb=256):
    T = dq_f.shape[0]
    wk, wv = GLA_HEADS * GLA_DK, GLA_HEADS * GLA_DV

    def body(qf, qb, kf, kb, vf, vb, _, o_ref):
        o_ref[:, :wk] = (qf[...] + qb[...]).astype(o_ref.dtype)
        o_ref[:, wk : 2 * wk] = (kf[...] + kb[...]).astype(o_ref.dtype)
        o_ref[:, 2 * wk :] = (vf[...] + vb[...]).astype(o_ref.dtype)

    ksp = pl.BlockSpec((tb, wk), lambda i: (i, 0))
    vsp = pl.BlockSpec((tb, wv), lambda i: (i, 0))
    return pl.pallas_call(
        body,
        out_shape=jax.ShapeDtypeStruct(dproj.shape, dproj.dtype),
        grid=(T // tb,),
        in_specs=[ksp, ksp, ksp, ksp, vsp, vsp, _ANY_SPEC],
        out_specs=pl.BlockSpec((tb, 2 * wk + wv), lambda i: (i, DP_GLA_QKV_BLK)),
        input_output_aliases={6: 0},
        compiler_params=_cparams(("parallel",)),
        name="gla_dsum",
    )(dq_f, dq_b, dk_f, dk_b, dv_f, dv_b, dproj)


def _local_step(x, target, wcat, conv_w, avec, dvec, gdn_norm_w, w2f, w2b, b2f, b2b, gla_norm_w, proj_weights, w_pre, w_post,
                send_proj_grads=None, send_in_grads=None):
    h, h_t = _rms_pre(x, w_pre)
    proj = _matmul(h, wcat, "nt", f32, "proj_in", 2048, 512, 1024)
    qkvn = _gdn_prep(proj, conv_w)
    gates, gc = _gdn_gates(proj, avec, dvec)
    oa_f, oa_b, sa_f, sa_b, inv_f, inv_b = _gdn_fwd(qkvn, gates, gc)
    og = _out_norm(oa_f, oa_b, proj, Z_BLK, gdn_norm_w, GDN_DIM, "gdn_out")
    G_f, G_b = _gla_prep(proj, w2f, w2b, b2f, b2b)
    ob_f, ob_b, sb_f, sb_b = _gla_fwd(proj, G_f, G_b)
    obg = _out_norm(ob_f, ob_b, proj, G_B_BLK, gla_norm_w, GLA_DV, "gla_out")
    wpg, wpl, wout = proj_weights(obg)
    (loss_row, d_wpost, dy, m, dout, dya, dyb, dproj, dog, dobg) = _mid(og, obg, proj, x, target, wpg, wpl, wout, w_post)

    d_wout = _matmul(m, dout, "tn", MXU_DTYPE, "dw_out", 1024, 512, 2048)
    d_wpg = _matmul(og, dya, "tn", MXU_DTYPE, "dw_proj_gdn", 1024, 512, 2048)
    d_wpl = _matmul(obg, dyb, "tn", MXU_DTYPE, "dw_proj_gla", 1024, 512, 2048)
    if send_proj_grads is not None:
        gla_norm_w = gla_norm_w + send_proj_grads(d_wpg, d_wpl, d_wout)

    do_b, dproj, d_gla_norm = _out_norm_bwd(dobg, ob_f, ob_b, proj, G_B_BLK, gla_norm_w, GLA_DV, "gla_out_bwd", dproj)
    dq_f, dk_f, dv_f, dgk_f, dq_b, dk_b, dv_b, dgk_b = _gla_bwd(proj, G_f, G_b, sb_f, sb_b, do_b)
    dproj = _gla_dsum(dq_f, dq_b, dk_f, dk_b, dv_f, dv_b, dproj)
    dsmall_gla, d_w2f, d_w2b, d_b2f, d_b2b = _gla_prep_bwd(proj, w2f, w2b, b2f, b2b, dgk_f, dgk_b)

    do_a, dproj, d_gdn_norm = _out_norm_bwd(dog, oa_f, oa_b, proj, Z_BLK, gdn_norm_w, GDN_DIM, "gdn_out_bwd", dproj)
    dqkv_f, dqkv_b, dg_f, dg_b = _gdn_bwd(qkvn, gates, gc, sa_f, sa_b, inv_f, inv_b, do_a)
    dproj, d_conv = _gdn_prep_bwd(proj, conv_w, dqkv_f, dqkv_b, dproj)
    dproj, d_alog, d_dtb = _gdn_gates_bwd(proj, avec, dvec, dg_f, dg_b, dsmall_gla, dproj)

    d_wcat = _matmul(h_t, dproj, "nn", MXU_DTYPE, "dw_in", 1024, 512, 2048, out_t=True)
    started = None if send_in_grads is None else send_in_grads(d_wcat)
    dh = _matmul(dproj, wcat, "nn", f32, "dh", 1024, 1024, 2432, after=started)
    grad_x, d_wpre = _rms_pre_bwd(x, w_pre, dh, dy)
    return dict(loss_row=loss_row, grad_x=grad_x, d_wcat=d_wcat, d_conv=d_conv, d_alog=d_alog, d_dtb=d_dtb,
                d_gdn_norm=d_gdn_norm, d_wpg=d_wpg, d_w2f=d_w2f, d_w2b=d_w2b, d_b2f=d_b2f, d_b2b=d_b2b,
                d_gla_norm=d_gla_norm, d_wpl=d_wpl, d_wout=d_wout, d_wpost=d_wpost, d_wpre=d_wpre)


def _to_cat(wf):
    pad = jnp.zeros((N_CAT - N_IN,) + wf.shape[1:], wf.dtype)
    return jnp.concatenate([wf[:4096], wf[4128:6176], wf[7232:9280], wf[6176:7200], wf[4096:4128], wf[7200:7232], pad], axis=0)


def _from_cat(wc):
    return jnp.concatenate([wc[:4096], wc[9216:9248], wc[4096:6144], wc[8192:9216], wc[9248:9280], wc[6144:8192]], axis=0)


def _lane_row(*pieces):
    row = jnp.concatenate(pieces, axis=1)
    return jnp.pad(row, ((0, 0), (0, LANES - row.shape[1])))


def _pad_w2(w2, row0):
    return jnp.pad(w2, ((row0, LANES - row0 - w2.shape[0]), (0, 0)))


SM_ROWS, SM_COLS = 32, 384
REP_ROWS = 32


def _pack_small_shard(conv, w2f, w2b):
    lead = conv.shape[:-2]
    z = lambda r, c: jnp.zeros(lead + (r, c), f32)
    top = jnp.concatenate([conv, z(8 - CONV_K, SM_COLS)], axis=-2)
    mid = jnp.concatenate([w2f, w2b, z(16, SM_COLS - 128)], axis=-1)
    return jnp.concatenate([top, mid, z(SM_ROWS - 24, SM_COLS)], axis=-2)


LOSS_ROW = 28


def _pack_rep(ln_pre, ln_post, b2f, b2b, gla_norm, gdn_norm, alog_f, alog_b, dt_f, dt_b, loss=None):
    last = _lane_row(alog_f, alog_b, dt_f, dt_b)
    extra = jnp.zeros((1, LANES), f32) if loss is None else _lane_row(loss)
    rows = [ln_pre.reshape(8, LANES), ln_post.reshape(8, LANES), b2f.reshape(4, LANES), b2b.reshape(4, LANES),
            gla_norm.reshape(2, LANES), gdn_norm.reshape(1, LANES), last, extra, jnp.zeros((REP_ROWS - LOSS_ROW - 1, LANES), f32)]
    return jnp.concatenate(rows, axis=0)


def _unpack_rep(p):
    return dict(ln_pre_w=p[0:8].reshape(1, 1024), ln_post_w=p[8:16].reshape(1, 1024), gk_b2_fwd=p[16:20].reshape(1, 512),
                gk_b2_bwd=p[20:24].reshape(1, 512), gla_norm_w=p[24:26].reshape(1, 256), gdn_norm_w=p[26:27],
                a_log_fwd=p[27:28, 0:8], a_log_bwd=p[27:28, 8:16], dt_bias_fwd=p[27:28, 16:24], dt_bias_bwd=p[27:28, 24:32])


_MESH = pl.DeviceIdType.MESH


N_BIG_COPIES, N_SMALL_COPIES = 9, 7


def _gather_weights(w_in_s, sm_s):
    shapes = (w_in_s.shape, sm_s.shape)
    dtypes = (MXU_DTYPE, f32)

    def body(win_ref, sm_ref, gin_ref, gsm_ref, send_sems, recv_sems):
        x, y, c = lax.axis_index("x"), lax.axis_index("y"), lax.axis_index("c")
        me, sibling = (x, y, c), (x, y, 1 - c)
        chips = [(1 - x, y), (x, 1 - y), (1 - x, 1 - y)]
        bufs = (gin_ref, gsm_ref)

        def idx(px, py, pc):
            return 4 * px + 2 * py + pc

        for r in range(w_in_s.shape[1] // LANES):
            cols = slice(r * LANES, (r + 1) * LANES)
            gin_ref[idx(*me), :, cols] = win_ref[:, cols].astype(MXU_DTYPE)
        gsm_ref[idx(*me)] = sm_ref[...]

        def copy(sem, blk, to):
            return pltpu.make_async_remote_copy(src_ref=blk, dst_ref=blk, send_sem=send_sems.at[sem], recv_sem=recv_sems.at[sem],
                                                device_id=to, device_id_type=_MESH)

        def small(k, block, to):
            return copy(N_BIG_COPIES + k, gsm_ref.at[idx(*block)], to)

        half = w_in_s.shape[1] // 2
        xn, yn, dg = (1 - x, y), (x, 1 - y), (1 - x, 1 - y)

        def big(sem, block, to, part=None):
            blk = gin_ref.at[idx(*block)]
            return copy(sem, blk if part is None else blk.at[:, pl.ds(part * half, half)], to)

        started = [big(0, me, sibling), big(1, me, (*xn, c)), big(2, me, (*yn, c)), small(0, me, sibling)]
        started += [small(1 + j, me, (*chip, c)) for j, chip in enumerate(chips)]
        for cp in started:
            cp.start()

        def then(arrived, forwards):
            arrived.wait_recv()
            for cp in forwards:
                cp.start()
            started.extend(forwards)

        then(big(1, (*xn, c), me), [big(3, (*xn, c), (*yn, c), 1), big(5, (*xn, c), sibling)])
        then(big(2, (*yn, c), me), [big(4, (*yn, c), (*xn, c), 0), big(6, (*yn, c), sibling)])
        then(big(4, (*dg, c), me, 0), [big(7, (*dg, c), sibling, 0)])
        then(big(3, (*dg, c), me, 1), [big(8, (*dg, c), sibling, 1)])
        for j, chip in enumerate(chips):
            then(small(1 + j, (*chip, c), me), [small(4 + j, (*chip, c), sibling)])
        big(0, sibling, me).wait_recv()
        big(5, (*xn, 1 - c), me).wait_recv()
        big(6, (*yn, 1 - c), me).wait_recv()
        big(7, (*dg, 1 - c), me, 0).wait_recv()
        big(8, (*dg, 1 - c), me, 1).wait_recv()
        small(0, sibling, me).wait_recv()
        for j, chip in enumerate(chips):
            small(4 + j, (*chip, 1 - c), me).wait_recv()
        for cp in started:
            cp.wait_send()

    vm = pl.BlockSpec(memory_space=pltpu.VMEM)
    return pl.pallas_call(
        body,
        out_shape=tuple(jax.ShapeDtypeStruct((N_DEV,) + s, d) for s, d in zip(shapes, dtypes)),
        in_specs=[vm, vm],
        out_specs=(vm, vm),
        scratch_shapes=[pltpu.SemaphoreType.DMA((N_BIG_COPIES + N_SMALL_COPIES,)), pltpu.SemaphoreType.DMA((N_BIG_COPIES + N_SMALL_COPIES,))],
        compiler_params=pltpu.CompilerParams(vmem_limit_bytes=VMEM_LIMIT),
        name="gather_weights",
    )(w_in_s, sm_s)


N_CHIP = 4
_EFFECT = pltpu.SideEffectType.DATAFLOW_SIDE_EFFECTING
_HBM_SPEC = pl.BlockSpec(memory_space=pltpu.HBM)
_SEM_SPEC = pl.BlockSpec(memory_space=pltpu.SEMAPHORE)


def _relation_peer(k, x, y, c):
    px = (1 - x) if (k & 4) else x
    py = (1 - y) if (k & 2) else y
    pc = (1 - c) if (k & 1) else c
    return (px, py, pc), 4 * px + 2 * py + pc


_PLAN_COPIES = {"gather": N_DEV - 1, "blocks": N_DEV - 1, "sibling": 4, "chips": 3}
_PLAN_SLOTS = {"gather": N_DEV, "blocks": N_DEV, "sibling": 4, "chips": 4}


def _plan_copies(plan, src_ref, land_ref, send_sems, recv_sems, arrival):
    x, y, c = lax.axis_index("x"), lax.axis_index("y"), lax.axis_index("c")
    me, my_chip = 4 * x + 2 * y + c, 2 * x + y
    out = []

    def add(i, src, there, here, dev):
        out.append(pltpu.make_async_remote_copy(src_ref=src, dst_ref=land_ref.at[here if arrival else there], send_sem=send_sems.at[i],
                                                recv_sem=recv_sems.at[i], device_id=dev, device_id_type=_MESH))

    if plan in ("gather", "blocks"):
        for k in range(1, N_DEV):
            dev, p = _relation_peer(k, x, y, c)
            add(k - 1, src_ref.at[p] if plan == "blocks" else src_ref, me, p, dev)
    elif plan == "sibling":
        for j in range(4):
            add(j, src_ref.at[2 * j + 1 - c], j, j, (x, y, 1 - c))
    else:
        for i, k in enumerate((2, 4, 6)):
            (px, py, pc), _ = _relation_peer(k, x, y, c)
            add(i, src_ref.at[2 * px + py], my_chip, 2 * px + py, (px, py, pc))
    return out


def _exchange_start(src, plan, name, after=None):
    n = _PLAN_COPIES[plan]
    land = lax.empty((_PLAN_SLOTS[plan],) + (src.shape if plan == "gather" else src.shape[1:]), src.dtype)
    extra = [] if after is None else [after]

    def body(src_ref, land_ref, *rest):
        send_sems, recv_sems, _, _, token = rest[len(extra) :]
        for cp in _plan_copies(plan, src_ref, land_ref, send_sems, recv_sems, False):
            cp.start()
        token[...] = jnp.zeros_like(token)

    return pl.pallas_call(
        body,
        name=name,
        out_shape=(pltpu.SemaphoreType.DMA((n,)), pltpu.SemaphoreType.DMA((n,)), pltpu.HBM(src.shape, src.dtype),
                   pltpu.HBM(land.shape, land.dtype), jax.ShapeDtypeStruct((8, LANES), f32)),
        in_specs=(_HBM_SPEC, _HBM_SPEC) + (_ANY_SPEC,) * len(extra),
        out_specs=(_SEM_SPEC, _SEM_SPEC, _HBM_SPEC, _HBM_SPEC, pl.BlockSpec(memory_space=pltpu.VMEM)),
        input_output_aliases={0: 2, 1: 3},
        compiler_params=pltpu.CompilerParams(has_side_effects=_EFFECT),
    )(pltpu.with_memory_space_constraint(src, pltpu.HBM), pltpu.with_memory_space_constraint(land, pltpu.HBM), *extra)


def _exchange_wait(started, after, plan, name):
    send_sems, recv_sems, src_thru, land_thru, _ = started

    def body(src_ref, land_ref, send_sems, recv_sems, after_ref, src_dead, got_ref):
        for cp in _plan_copies(plan, src_ref, land_ref, send_sems, recv_sems, True):
            cp.wait_send()
            cp.wait_recv()

    return pl.pallas_call(
        body,
        name=name,
        out_shape=(pltpu.HBM(src_thru.shape, src_thru.dtype), pltpu.HBM(land_thru.shape, land_thru.dtype)),
        in_specs=(_HBM_SPEC, _HBM_SPEC, _SEM_SPEC, _SEM_SPEC, pl.BlockSpec(memory_space=pl.ANY)),
        out_specs=(_HBM_SPEC, _HBM_SPEC),
        input_output_aliases={0: 0, 1: 1},
        compiler_params=pltpu.CompilerParams(has_side_effects=_EFFECT),
    )(src_thru, land_thru, send_sems, recv_sems, after)


def _pair_sum(g, r, name, tc=LANES):
    _, R, Cc = g.shape

    def body(g_ref, r_ref, o_ref):
        o_ref[...] = (g_ref[...].astype(f32) + r_ref[...].astype(f32)).astype(o_ref.dtype)

    return pl.pallas_call(
        body,
        out_shape=jax.ShapeDtypeStruct(r.shape, r.dtype),
        grid=(N_CHIP, Cc // tc),
        in_specs=[pl.BlockSpec((1, R, tc), lambda k, i: (2 * k + lax.axis_index("c"), 0, i)), pl.BlockSpec((1, R, tc), lambda k, i: (k, 0, i))],
        out_specs=pl.BlockSpec((1, R, tc), lambda k, i: (k, 0, i)),
        compiler_params=_cparams(("parallel", "parallel")),
        name=name,
    )(g, r)


def _exchange_small(g_sm, g_rep):
    srcs = (g_sm, g_rep)

    def body(in0, in1, out0, out1, send_sems, recv_sems, local_sems):
        x, y, c = lax.axis_index("x"), lax.axis_index("y"), lax.axis_index("c")
        me = 4 * x + 2 * y + c
        ins, outs = (in0, in1), (out0, out1)

        def copy(a, k, arrival):
            dev, p = _relation_peer(k, x, y, c)
            return pltpu.make_async_remote_copy(src_ref=ins[a].at[p] if a == 0 else ins[a], dst_ref=outs[a].at[p if arrival else me],
                                                send_sem=send_sems.at[7 * a + k - 1], recv_sem=recv_sems.at[7 * a + k - 1],
                                                device_id=dev, device_id_type=_MESH)

        pairs = [(a, k) for a in (0, 1) for k in range(1, N_DEV)]
        sends = [copy(a, k, False) for a, k in pairs]
        for cp in sends:
            cp.start()
        local = [pltpu.make_async_copy(ins[0].at[me], outs[0].at[me], local_sems.at[0]),
                 pltpu.make_async_copy(ins[1], outs[1].at[me], local_sems.at[1])]
        for cp in local:
            cp.start()
        for a, k in pairs:
            copy(a, k, True).wait_recv()
        for cp in sends:
            cp.wait_send()
        for cp in local:
            cp.wait()

    return pl.pallas_call(
        body,
        out_shape=(jax.ShapeDtypeStruct(g_sm.shape, g_sm.dtype), jax.ShapeDtypeStruct((N_DEV,) + g_rep.shape, g_rep.dtype)),
        in_specs=[_HBM_SPEC] * 2,
        out_specs=(_HBM_SPEC,) * 2,
        scratch_shapes=[pltpu.SemaphoreType.DMA((14,)), pltpu.SemaphoreType.DMA((14,)), pltpu.SemaphoreType.DMA((2,))],
        name="exchange_small",
    )(*srcs)


def _sum_adam(parts, w, m, v, name, tb, tc=None, own=None, own_slot=None):
    R, Cc = w.shape
    n_parts = parts.shape[0]
    tb = R if tc else min(tb, R)
    tc = tc or Cc
    assert R % tb == 0 and Cc % tc == 0
    bc1 = 1.0 - ADAM_B1**ADAM_STEP
    bc2 = 1.0 - ADAM_B2**ADAM_STEP
    extra = [] if own is None else [own]

    def body(p_ref, w_ref, m_ref, v_ref, *rest):
        g_ref, d_ref, nm_ref, nv_ref = rest[len(extra) :]

        def part(j):
            pj = p_ref[j].astype(f32)
            return pj if own is None else jnp.where(own_slot() == j, rest[0][0].astype(f32), pj)

        g = part(0)
        for j in range(1, n_parts):
            g = g + part(j)
        g_ref[...] = g
        m2 = ADAM_B1 * m_ref[...] + (1.0 - ADAM_B1) * g
        v2 = ADAM_B2 * v_ref[...] + (1.0 - ADAM_B2) * (g * g)
        nm_ref[...] = m2
        nv_ref[...] = v2
        d_ref[...] = -ADAM_LR * ((m2 / bc1) / (jnp.sqrt(v2 / bc2) + ADAM_EPS) + ADAM_WD * w_ref[...])

    blk = pl.BlockSpec((tb, tc), lambda i, j: (i, j))
    o = jax.ShapeDtypeStruct((R, Cc), f32)
    return pl.pallas_call(
        body,
        out_shape=(o, o, o, o),
        grid=(R // tb, Cc // tc),
        in_specs=[pl.BlockSpec((n_parts, tb, tc), lambda i, j: (0, i, j)), blk, blk, blk]
        + [pl.BlockSpec((1, tb, tc), lambda i, j: (own_slot(), i, j))] * len(extra),
        out_specs=(blk, blk, blk, blk),
        compiler_params=_cparams(("parallel", "parallel")),
        name=name,
    )(parts, w, m, v, *extra)


def _my_device():
    return 4 * lax.axis_index("x") + 2 * lax.axis_index("y") + lax.axis_index("c")


def _my_chip():
    return 2 * lax.axis_index("x") + lax.axis_index("y")


_WEIGHTS = ["ln_pre_w", "w_in", "conv_w", "a_log_fwd", "a_log_bwd", "dt_bias_fwd", "dt_bias_bwd", "gdn_norm_w", "w_proj_gdn",
            "gk_w2_fwd", "gk_b2_fwd", "gk_w2_bwd", "gk_b2_bwd", "gla_norm_w", "w_proj_gla", "w_out", "ln_post_w"]
_REP_ORDER = ["ln_pre_w", "ln_post_w", "gk_b2_fwd", "gk_b2_bwd", "gla_norm_w", "gdn_norm_w", "a_log_fwd", "a_log_bwd", "dt_bias_fwd", "dt_bias_bwd"]


def kernel(x, ln_pre_w, w_in, conv_w, a_log_fwd, a_log_bwd, dt_bias_fwd, dt_bias_bwd, gdn_norm_w, w_proj_gdn, gk_w2_fwd, gk_b2_fwd, gk_w2_bwd, gk_b2_bwd, gla_norm_w, w_proj_gla, w_out, ln_post_w, loss_target, m_ln_pre_w, m_w_in, m_conv_w, m_a_log_fwd, m_a_log_bwd, m_dt_bias_fwd, m_dt_bias_bwd, m_gdn_norm_w, m_w_proj_gdn, m_gk_w2_fwd, m_gk_b2_fwd, m_gk_w2_bwd, m_gk_b2_bwd, m_gla_norm_w, m_w_proj_gla, m_w_out, m_ln_post_w, v_ln_pre_w, v_w_in, v_conv_w, v_a_log_fwd, v_a_log_bwd, v_dt_bias_fwd, v_dt_bias_bwd, v_gdn_norm_w, v_w_proj_gdn, v_gk_w2_fwd, v_gk_b2_fwd, v_gk_w2_bwd, v_gk_b2_bwd, v_gla_norm_w, v_w_proj_gla, v_w_out, v_ln_post_w):
    args = locals()
    W = {n: args[n] for n in _WEIGHTS}
    M = {n: args["m_" + n] for n in _WEIGHTS}
    V = {n: args["v_" + n] for n in _WEIGHTS}

    wp_stack = lambda P: jnp.concatenate([P["w_proj_gdn"], P["w_proj_gla"], P["w_out"]], axis=0)
    sm_pack = lambda P: _pack_small_shard(P["conv_w"][0], P["gk_w2_fwd"][0], P["gk_w2_bwd"][0])
    me = 4 * lax.axis_index("x") + 2 * lax.axis_index("y") + lax.axis_index("c")
    w_in_t, m_in_t, v_in_t = (jnp.transpose(a[0]) for a in (w_in, m_w_in, v_w_in))
    g_in, g_sm = _gather_weights(w_in_t, sm_pack(W))
    wp_own = wp_stack(W).astype(MXU_DTYPE)
    wp_started = _exchange_start(wp_own, "gather", "gather_proj_start", after=g_sm)

    def proj_weights(after):
        own, land = _exchange_wait(wp_started, after, "gather", "gather_proj_wait")
        g_p = lax.dynamic_update_slice(land, own[None], (me, 0, 0, 0))
        return tuple(g_p[:, i].reshape(D_MODEL, D_MODEL) for i in range(3))

    rows_p = 3 * D_MODEL // N_DEV
    sent = {}

    def send_proj_grads(d_wpg, d_wpl, d_wout):
        p_p = jnp.stack([d_wpg, d_wpl, d_wout]).reshape(3, N_DEV, D_MODEL // N_DEV, D_MODEL).transpose(1, 0, 2, 3)
        sent["p_p"] = p_p.reshape(N_DEV, rows_p, D_MODEL)
        sent["started"] = _exchange_start(sent["p_p"], "blocks", "exchange_proj_start")
        return sent["started"][4][0:1, 0:1]

    def send_in_grads(d_wcat):
        p_in = _from_cat(d_wcat).reshape(N_DEV, SHARD_IN, D_MODEL)
        sib = _exchange_start(p_in, "sibling", "exchange_in_sibling_start")
        p_p, land_p = _exchange_wait(sent["started"], sib[4], "blocks", "exchange_proj_wait")
        sent["o_p"] = _sum_adam(land_p, wp_stack(W).reshape(rows_p, D_MODEL), wp_stack(M).reshape(rows_p, D_MODEL),
                                wp_stack(V).reshape(rows_p, D_MODEL), "adam_w_proj", 128, own=p_p, own_slot=_my_device)
        p_in, sib_land = _exchange_wait(sib, sent["o_p"][0], "sibling", "exchange_in_sibling_wait")
        sent["chips"] = _exchange_start(_pair_sum(p_in, sib_land, "pair_sum_w_in", tc=512), "chips", "exchange_in_chips_start")
        return sent["chips"][4]

    wcat = _to_cat(g_in.reshape(N_IN, D_MODEL))
    conv_full = g_sm[:, 0:CONV_K, :].transpose(1, 0, 2).reshape(CONV_K, N_DEV * SM_COLS)
    w2f_full = g_sm[:, 8:24, 0:64].transpose(1, 0, 2).reshape(16, 512)
    w2b_full = g_sm[:, 8:24, 64:128].transpose(1, 0, 2).reshape(16, 512)

    avec = _lane_row(a_log_fwd, a_log_bwd)
    dvec = _lane_row(dt_bias_fwd, dt_bias_bwd)
    G = _local_step(x[0], loss_target[0], wcat, conv_full, avec, dvec, gdn_norm_w, _pad_w2(w2f_full, 32), _pad_w2(w2b_full, 48),
                    gk_b2_fwd, gk_b2_bwd, gla_norm_w, proj_weights, ln_pre_w + wp_started[4][0:1, 0:1], ln_post_w, send_proj_grads,
                    send_in_grads)

    p_sm = _pack_small_shard(G["d_conv"].reshape(CONV_K, N_DEV, SM_COLS).transpose(1, 0, 2),
                             G["d_w2f"][32:48].reshape(16, N_DEV, 64).transpose(1, 0, 2),
                             G["d_w2b"][48:64].reshape(16, N_DEV, 64).transpose(1, 0, 2))
    p_rep = _pack_rep(G["d_wpre"], G["d_wpost"], G["d_b2f"], G["d_b2b"], G["d_gla_norm"], G["d_gdn_norm"],
                      G["d_alog"][:, 0:8], G["d_alog"][:, 8:16], G["d_dtb"][:, 0:8], G["d_dtb"][:, 8:16],
                      loss=0.5 * jnp.sum(G["loss_row"], axis=1, keepdims=True) / D_MODEL)
    p_small = jnp.concatenate([p_sm, jnp.broadcast_to(p_rep[None], (N_DEV,) + p_rep.shape)], axis=2)
    small_started = _exchange_start(p_small, "blocks", "exchange_small_start")
    q_in, land_in = _exchange_wait(sent["chips"], small_started[4], "chips", "exchange_in_chips_wait")
    o_in = _sum_adam(land_in, w_in_t, m_in_t, v_in_t, "adam_w_in", SHARD_IN, tc=256, own=q_in, own_slot=_my_chip)
    o_p = sent["o_p"]
    p_small, land_small = _exchange_wait(small_started, o_in[0], "blocks", "exchange_small_wait")
    rep_pack = lambda P: _pack_rep(*[P[n] for n in _REP_ORDER])
    small_pack = lambda P: jnp.concatenate([sm_pack(P), rep_pack(P)], axis=1)
    o_small = _sum_adam(land_small, small_pack(W), small_pack(M), small_pack(V), "adam_small", SM_ROWS, own=p_small, own_slot=_my_device)
    o_sm, o_rep = [o[:, :SM_COLS] for o in o_small], [o[:, SM_COLS:] for o in o_small]

    res = []
    for kind in range(4):
        rep = _unpack_rep(o_rep[kind])
        per = dict(rep)
        per["w_in"] = jnp.transpose(o_in[kind])[None]
        pp = o_p[kind].reshape(3, 1, D_MODEL // N_DEV, D_MODEL)
        per["w_proj_gdn"], per["w_proj_gla"], per["w_out"] = pp[0], pp[1], pp[2]
        per["conv_w"] = o_sm[kind][None, 0:CONV_K, :]
        per["gk_w2_fwd"] = o_sm[kind][None, 8:24, 0:64]
        per["gk_w2_bwd"] = o_sm[kind][None, 8:24, 64:128]
        res.append([per[n] for n in _WEIGHTS])
    loss = o_rep[0][LOSS_ROW, 0]
    return (loss, G["grad_x"][None], *res[0], *res[1], *res[2], *res[3])
```

```python
import functools
import math

import jax
import jax.numpy as jnp
from jax import lax
from jax.experimental import pallas as pl
from jax.experimental.pallas import tpu as pltpu

f32 = jnp.float32
MXU_DTYPE = jnp.bfloat16
HI = lax.Precision.HIGHEST

D_MODEL = 1024
CHUNK = 64
NORM_EPS = 1e-6
GDN_HEADS, GDN_DIM = 8, 128
GLA_HEADS, GLA_DK, GLA_DV = 4, 128, 256
GATE_NORMALIZER = 16.0
CONV_K = 5
N_IN = 9280
N_DEV = 8
SHARD_IN = N_IN // N_DEV

N_CAT = 9728
SMALL_OFF = 9216
LANES = 128

ADAM_LR, ADAM_B1, ADAM_B2, ADAM_EPS, ADAM_WD, ADAM_STEP = 0.001, 0.9, 0.999, 1e-08, 0.01, 10

VMEM_LIMIT = 56 * 1024 * 1024


def _cparams(sem=None):
    return pltpu.CompilerParams(dimension_semantics=sem, vmem_limit_bytes=VMEM_LIMIT)


_DN = {"nn": (((1,), (0,)), ((), ())), "nt": (((1,), (1,)), ((), ())), "tn": (((0,), (0,)), ((), ()))}
_BDN = {"nn": (((2,), (1,)), ((0,), (0,))), "nt": (((2,), (2,)), ((0,), (0,))), "tn": (((1,), (1,)), ((0,), (0,)))}


def _dot16(a, b, kind):
    return lax.dot_general(a, b, (_DN if a.ndim == 2 else _BDN)[kind], preferred_element_type=f32)


def _mx(a, b, kind):
    return _dot16(a.astype(MXU_DTYPE), b.astype(MXU_DTYPE), kind)


def _split2(a):
    hi = a.astype(jnp.bfloat16)
    return hi, (a - hi.astype(f32)).astype(jnp.bfloat16)


def _split3(a):
    p1 = a.astype(jnp.bfloat16)
    r = a - p1.astype(f32)
    p2 = r.astype(jnp.bfloat16)
    return p1, p2, (r - p2.astype(f32)).astype(jnp.bfloat16)


def _h3s(a2, b2, kind):
    (ah, al), (bh, bl) = a2, b2
    return _dot16(ah, bh, kind) + (_dot16(ah, bl, kind) + _dot16(al, bh, kind))


def _h3(a, b, kind):
    return _h3s(_split2(a), _split2(b), kind)


def _exact01(t01, x, kind="nn"):
    t = t01.astype(jnp.bfloat16)
    x1, x2, x3 = _split3(x)
    return _dot16(t, x1, kind) + (_dot16(t, x2, kind) + _dot16(t, x3, kind))


def _colsum_col(e):
    ones = jnp.ones(e.shape[:-1] + (LANES,), jnp.bfloat16)
    e1, e2, e3 = _split3(e)
    return (_dot16(e1, ones, "tn") + (_dot16(e2, ones, "tn") + _dot16(e3, ones, "tn")))[..., 0:1]


def _iota3(shape, axis):
    return lax.broadcasted_iota(jnp.int32, shape, axis)


def _dir_masks(B, nf, Cn):
    shp = (B, Cn, Cn)
    bb, ii, jj = _iota3(shp, 0), _iota3(shp, 1), _iota3(shp, 2)
    fwd = bb < nf
    rev = jnp.logical_not(fwd)
    incl = jnp.logical_or(jnp.logical_and(fwd, ii >= jj), jnp.logical_and(rev, ii <= jj))
    strict = jnp.logical_and(incl, ii != jj)
    return incl, strict, ii == jj


def _last_row(x, nf):
    B, Cn = x.shape[0], x.shape[1]
    fwd = _iota3((B, 1, 1), 0) < nf
    return jnp.where(fwd, x[:, Cn - 1 : Cn, :], x[:, 0:1, :])


def _add_at_last_row(x, val, nf):
    B, Cn = x.shape[0], x.shape[1]
    bb, rr = _iota3((B, Cn, 1), 0), _iota3((B, Cn, 1), 1)
    at = jnp.logical_or(jnp.logical_and(bb < nf, rr == Cn - 1), jnp.logical_and(bb >= nf, rr == 0))
    return x + jnp.where(at, val, 0.0)


def _sigmoid(x):
    return jax.nn.sigmoid(x)


def _silu(x):
    return x * _sigmoid(x)


def _silu_grad(x):
    s = _sigmoid(x)
    return s * (1.0 + x * (1.0 - s))


def _softplus(x):
    u = jnp.exp(-jnp.abs(x))
    l1p = jnp.where(u < 1e-3, u * (1.0 - u * (0.5 - u * (1.0 / 3.0))), jnp.log(1.0 + u))
    return jnp.maximum(x, 0.0) + l1p


def _mxr(x):
    return x.astype(MXU_DTYPE).astype(f32)


def _iota2(shape, axis):
    return lax.broadcasted_iota(jnp.int32, shape, axis)


def _matmul(a, b, kind, out_dtype, name, tm, tn, tk, out_t=False, after=None):
    extra = [] if after is None else [after]
    if kind == "nn":
        (M, K), N = a.shape, b.shape[1]
    elif kind == "nt":
        (M, K), N = a.shape, b.shape[0]
    else:
        (K, M), N = a.shape, b.shape[1]
    tm, tn, tk = min(tm, M), min(tn, N), min(tk, K)
    assert M % tm == 0 and N % tn == 0 and K % tk == 0, (name, M, N, K)
    nk = K // tk
    if kind == "tn":
        a_spec = pl.BlockSpec((tk, tm), lambda i, j, k: (k, i))
    else:
        a_spec = pl.BlockSpec((tm, tk), lambda i, j, k: (i, k))
    if kind == "nt":
        b_spec = pl.BlockSpec((tn, tk), lambda i, j, k: (j, k))
    else:
        b_spec = pl.BlockSpec((tk, tn), lambda i, j, k: (k, j))

    def body(a_ref, b_ref, *rest):
        o_ref, acc = rest[len(extra)], rest[len(extra) + 1 :]
        p = _mx(a_ref[...], b_ref[...], kind)
        if nk == 1:
            o_ref[...] = (p.T if out_t else p).astype(out_dtype)
        else:
            assert not out_t
            acc_ref = acc[0]
            k = pl.program_id(2)

            @pl.when(k == 0)
            def _():
                acc_ref[...] = p

            @pl.when(k > 0)
            def _():
                acc_ref[...] += p

            @pl.when(k == nk - 1)
            def _():
                o_ref[...] = acc_ref[...].astype(out_dtype)

    return pl.pallas_call(
        body,
        out_shape=jax.ShapeDtypeStruct((N, M) if out_t else (M, N), out_dtype),
        grid=(M // tm, N // tn, nk),
        in_specs=[a_spec, b_spec] + [pl.BlockSpec((8, LANES), lambda i, j, k: (0, 0))] * len(extra),
        out_specs=pl.BlockSpec((tn, tm), lambda i, j, k: (j, i)) if out_t else pl.BlockSpec((tm, tn), lambda i, j, k: (i, j)),
        scratch_shapes=[] if nk == 1 else [pltpu.VMEM((tm, tn), f32)],
        compiler_params=_cparams(("parallel", "parallel", "arbitrary")),
        name=name,
    )(a, b, *extra)


ROW_TILE = 512


def _rms_pre(x, w):
    T = x.shape[0]
    tb = min(ROW_TILE, T)

    def body(x_ref, w_ref, h_ref, ht_ref):
        xv = x_ref[...]
        r = lax.rsqrt(jnp.mean(xv * xv, axis=1, keepdims=True) + NORM_EPS)
        h = xv * r * w_ref[...]
        h_ref[...] = h.astype(h_ref.dtype)
        ht_ref[...] = h.T.astype(ht_ref.dtype)

    return pl.pallas_call(
        body,
        out_shape=(jax.ShapeDtypeStruct((T, D_MODEL), MXU_DTYPE), jax.ShapeDtypeStruct((D_MODEL, T), MXU_DTYPE)),
        grid=(T // tb,),
        in_specs=[pl.BlockSpec((tb, D_MODEL), lambda i: (i, 0)), pl.BlockSpec((1, D_MODEL), lambda i: (0, 0))],
        out_specs=(pl.BlockSpec((tb, D_MODEL), lambda i: (i, 0)), pl.BlockSpec((D_MODEL, tb), lambda i: (0, i))),
        compiler_params=_cparams(("parallel",)),
        name="rms_pre",
    )(x, w)


def _rms_pre_bwd(x, w, dh, dy):
    T = x.shape[0]
    tb = min(ROW_TILE, T)

    def body(x_ref, w_ref, dh_ref, dy_ref, dx_ref, dw_ref):
        i = pl.program_id(0)
        xv, dhv = x_ref[...], dh_ref[...]
        r = lax.rsqrt(jnp.mean(xv * xv, axis=1, keepdims=True) + NORM_EPS)
        dhw = dhv * w_ref[...]
        dx_ref[...] = dy_ref[...] + r * dhw - xv * (r * r * r) * jnp.mean(dhw * xv, axis=1, keepdims=True)
        part = jnp.sum(dhv * xv * r, axis=0, keepdims=True)

        @pl.when(i == 0)
        def _():
            dw_ref[...] = part

        @pl.when(i > 0)
        def _():
            dw_ref[...] += part

    blk = pl.BlockSpec((tb, D_MODEL), lambda i: (i, 0))
    row = pl.BlockSpec((1, D_MODEL), lambda i: (0, 0))
    return pl.pallas_call(
        body,
        out_shape=(jax.ShapeDtypeStruct((T, D_MODEL), f32), jax.ShapeDtypeStruct((1, D_MODEL), f32)),
        grid=(T // tb,),
        in_specs=[blk, row, blk, blk],
        out_specs=(blk, row),
        compiler_params=_cparams(("arbitrary",)),
        name="rms_pre_bwd",
    )(x, w, dh, dy)


SHIFTS = (-2, -1, 1, 2)


def _shifted(x):
    T = x.shape[0]
    t = _iota2(x.shape, 0)
    out = {}
    for d in SHIFTS:
        valid = (t < T - d) if d > 0 else (t >= -d)
        out[d] = jnp.where(valid, pltpu.roll(x, (-d) % T, 0), 0.0)
    return out


def _conv5(u, ush, cw):
    acc = u * cw[2:3, :]
    for j in (0, 1, 3, 4):
        acc = acc + ush[j - 2] * cw[j : j + 1, :]
    return acc


def _gdn_prep(proj, conv_w):
    T = proj.shape[0]
    nh = GDN_HEADS

    def body(u_ref, cw_ref, o_ref):
        j = pl.program_id(0)
        u = _mxr(u_ref[...])
        s = _silu(_conv5(u, _shifted(u), _mxr(cw_ref[...])))

        @pl.when(j < 2 * nh)
        def _():
            o_ref[0] = s * lax.rsqrt(jnp.sum(s * s, axis=1, keepdims=True) + NORM_EPS)

        @pl.when(j >= 2 * nh)
        def _():
            o_ref[0] = s

    return pl.pallas_call(
        body,
        out_shape=jax.ShapeDtypeStruct((3, T, GDN_HEADS * GDN_DIM), f32),
        grid=(3 * nh,),
        in_specs=[pl.BlockSpec((T, GDN_DIM), lambda j: (0, j)), pl.BlockSpec((CONV_K, GDN_DIM), lambda j: (0, j))],
        out_specs=pl.BlockSpec((1, T, GDN_DIM), lambda j: (j // nh, 0, j % nh)),
        compiler_params=_cparams(("parallel",)),
        name="gdn_prep",
    )(proj, conv_w)


def _gdn_prep_bwd(proj, conv_w, dqkv_f, dqkv_b, dproj):
    T = proj.shape[0]
    nh = GDN_HEADS


    def body(u_ref, cw_ref, df_ref, db_ref, _, du_ref, dcw_ref):
        j = pl.program_id(0)
        u, cw = _mxr(u_ref[...]), _mxr(cw_ref[...])
        ush = _shifted(u)
        c = _conv5(u, ush, cw)
        s = _silu(c)
        dn = df_ref[0] + db_ref[0]
        rinv = lax.rsqrt(jnp.sum(s * s, axis=1, keepdims=True) + NORM_EPS)
        ds_norm = rinv * dn - s * (rinv * rinv * rinv) * jnp.sum(dn * s, axis=1, keepdims=True)
        ds = jnp.where(j < 2 * nh, ds_norm, dn)
        dc = _mxr(ds * _silu_grad(c))
        dcsh = _shifted(dc)
        du = dc * cw[2:3, :]
        for jj in range(CONV_K):
            d = jj - 2
            dcw_ref[jj : jj + 1, :] = jnp.sum(dc * (u if d == 0 else ush[d]), axis=0, keepdims=True)
            if d != 0:
                du = du + dcsh[-d] * cw[jj : jj + 1, :]
        du_ref[...] = du.astype(du_ref.dtype)

    qspec = pl.BlockSpec((1, T, GDN_DIM), lambda j: (j // nh, 0, j % nh))
    return pl.pallas_call(
        body,
        out_shape=(jax.ShapeDtypeStruct(dproj.shape, dproj.dtype), jax.ShapeDtypeStruct((CONV_K, 3 * nh * GDN_DIM), f32)),
        grid=(3 * nh,),
        in_specs=[pl.BlockSpec((T, GDN_DIM), lambda j: (0, j)), pl.BlockSpec((CONV_K, GDN_DIM), lambda j: (0, j)), qspec, qspec, _ANY_SPEC],
        out_specs=(pl.BlockSpec((T, GDN_DIM), lambda j: (0, j)), pl.BlockSpec((CONV_K, GDN_DIM), lambda j: (0, j))),
        input_output_aliases={4: 0},
        compiler_params=_cparams(("parallel",)),
        name="gdn_prep_bwd",
    )(proj, conv_w, dqkv_f, dqkv_b, dproj)


def _chunk_tri(n, rev):
    i, j = _iota2((n, n), 0), _iota2((n, n), 1)
    same = jnp.right_shift(i, 6) == jnp.right_shift(j, 6)
    order = (j >= i) if rev else (j <= i)
    return jnp.where(jnp.logical_and(same, order), 1.0, 0.0).astype(f32)


def _gdn_gates(proj, avec, dvec, tb=256):
    T = proj.shape[0]

    def body(s_ref, a_ref, d_ref, g_ref, gc_ref):
        small = s_ref[...]
        lane = _iota2(small.shape, 1)
        lg = -jnp.exp(a_ref[...]) * _softplus(small + d_ref[...])
        beta = _sigmoid(small)
        g_ref[...] = jnp.where(lane < 16, lg, jnp.where(lane < 32, beta, 0.0))
        lgm = jnp.where(lane < 16, lg, 0.0)
        gcf = _exact01(_chunk_tri(tb, False), lgm)
        gcr = _exact01(_chunk_tri(tb, True), lgm)
        gc_ref[...] = jnp.where(lane < 8, gcf, gcr)

    blk = pl.BlockSpec((tb, LANES), lambda i: (i, 0))
    row = pl.BlockSpec((1, LANES), lambda i: (0, 0))
    return pl.pallas_call(
        body,
        out_shape=(jax.ShapeDtypeStruct((T, LANES), f32), jax.ShapeDtypeStruct((T, LANES), f32)),
        grid=(T // tb,),
        in_specs=[pl.BlockSpec((tb, LANES), lambda i: (i, SMALL_OFF // LANES)), row, row],
        out_specs=(blk, blk),
        compiler_params=_cparams(("parallel",)),
        name="gdn_gates",
    )(proj, avec, dvec)


def _gdn_gates_bwd(proj, avec, dvec, dg_f, dg_b, dsmall_gla, dproj):
    T = proj.shape[0]
    tb = min(ROW_TILE, T)
    pad = N_CAT - SMALL_OFF

    def body(s_ref, a_ref, d_ref, gf_ref, gb_ref, dl_ref, _, ds_ref, da_ref, dd_ref):
        i = pl.program_id(0)
        small = s_ref[...]
        lane = _iota2(small.shape, 1)
        dgate = gf_ref[...] + gb_ref[...]
        z = small + d_ref[...]
        nega = -jnp.exp(a_ref[...])
        dz = dgate * nega * _sigmoid(z)
        beta = _sigmoid(small)
        dsm = jnp.where(lane < 16, dz, jnp.where(lane < 32, dgate * beta * (1.0 - beta), 0.0))
        ds_ref[:, :LANES] = (dsm + dl_ref[...]).astype(ds_ref.dtype)
        ds_ref[:, LANES:] = jnp.zeros((tb, pad - LANES), ds_ref.dtype)
        lg = nega * _softplus(z)
        pa = jnp.sum(jnp.where(lane < 16, dgate * lg, 0.0), axis=0, keepdims=True)
        pd = jnp.sum(jnp.where(lane < 16, dz, 0.0), axis=0, keepdims=True)

        @pl.when(i == 0)
        def _():
            da_ref[...] = pa
            dd_ref[...] = pd

        @pl.when(i > 0)
        def _():
            da_ref[...] += pa
            dd_ref[...] += pd

    blk = pl.BlockSpec((tb, LANES), lambda i: (i, 0))
    row = pl.BlockSpec((1, LANES), lambda i: (0, 0))
    return pl.pallas_call(
        body,
        out_shape=(jax.ShapeDtypeStruct(dproj.shape, dproj.dtype), jax.ShapeDtypeStruct((1, LANES), f32), jax.ShapeDtypeStruct((1, LANES), f32)),
        grid=(T // tb,),
        in_specs=[pl.BlockSpec((tb, LANES), lambda i: (i, SMALL_OFF // LANES)), row, row, blk, blk, blk, _ANY_SPEC],
        out_specs=(pl.BlockSpec((tb, pad), lambda i: (i, DP_SMALL_BLK)), row, row),
        input_output_aliases={6: 0},
        compiler_params=_cparams(("arbitrary",)),
        name="gdn_gates_bwd",
    )(proj, avec, dvec, dg_f, dg_b, dsmall_gla, dproj)


GDN_HB = 8


def _inv_unit_lower(L, eye):
    A = eye - L
    P2 = _split2(L)
    for _ in range(3):
        P2 = _split2(_h3s(P2, P2, "nn"))
        A = A + _h3s(_split2(A), P2, "nn")
    Ph = P2[0]
    for _ in range(2):
        Ph = _dot16(Ph, Ph, "nn").astype(jnp.bfloat16)
        A = A + _dot16(A.astype(jnp.bfloat16), Ph, "nn")
    return A


def _lane_col(blk, idx):
    lane = _iota2(blk.shape, 1)
    return jnp.sum(jnp.where(lane == idx, blk, 0.0), axis=1, keepdims=True)


class _GdnChunk:
    def __init__(self, q, k, v, gcol, beta, rev, A=None):
        Cn = q.shape[0]
        ii, jj = _iota2((Cn, Cn), 0), _iota2((Cn, Cn), 1)
        self.incl = (ii <= jj) if rev else (ii >= jj)
        self.strict = (ii < jj) if rev else (ii > jj)
        grow = jnp.broadcast_to(gcol, (Cn, LANES)).T[0:1, :]
        diff = gcol - grow
        self.decay = jnp.where(self.incl, jnp.exp(jnp.where(self.incl, diff, 0.0)), 0.0)
        self.k, self.v, self.beta, self.gcol = k, v, beta, gcol
        self.qs = q * (GDN_DIM**-0.5)
        self.kb = k * beta
        self.L = jnp.where(self.strict, _mx(self.kb, k, "nt") * self.decay, 0.0)
        if A is None:
            A = _inv_unit_lower(self.L, jnp.where(ii == jj, 1.0, 0.0).astype(f32))
        self.A2 = _split2(A)
        self.A = A
        self.eg = jnp.exp(gcol)
        self.kbg = self.kb * self.eg
        self.sol = _h3s(self.A2, _split2(jnp.concatenate([v * beta, self.kbg], axis=1)), "nn")
        self.u = self.sol[:, :GDN_DIM]
        self.w = self.sol[:, GDN_DIM:]
        self.attn = jnp.where(self.incl, _mx(self.qs, k, "nt") * self.decay, 0.0)
        self.qd = self.qs * self.eg
        last = 0 if rev else Cn - 1
        self.last = last
        gl = gcol[last : last + 1, :]
        self.ekd = jnp.exp(gl - gcol)
        self.kd = k * self.ekd
        self.egl = jnp.exp(gl)


def _gdn_scan_fwd(qkvn, gates, gc):
    T = qkvn.shape[1]
    N = T // CHUNK
    nh, hb = GDN_HEADS, GDN_HB
    W = hb * GDN_DIM
    chains = [(d, hh) for d in range(2) for hh in range(hb)]

    def body(qf_ref, qb_ref, gf_ref, gb_ref, cf_ref, cb_ref, of_ref, ob_ref, sf_ref, sb_ref, af_ref, ab_ref, s_scr):
        n, b = pl.program_id(0), pl.program_id(1)

        @pl.when(n == 0)
        def _():
            for d, hh in chains:
                s_scr[d, b * hb + hh] = jnp.zeros((GDN_DIM, GDN_DIM), f32)

        dirs = ((qf_ref, gf_ref, cf_ref, of_ref, sf_ref, af_ref), (qb_ref, gb_ref, cb_ref, ob_ref, sb_ref, ab_ref))
        s_in = {ch: s_scr[ch[0], b * hb + ch[1]] for ch in chains}
        s_out = {}
        for d, hh in chains:
            q_ref, g_ref, c_ref, o_ref, st_ref, a_ref = dirs[d]
            h = b * hb + hh
            sl = slice(hh * GDN_DIM, (hh + 1) * GDN_DIM)
            gcol = _lane_col(c_ref[...], h + 8 * d)
            beta = _lane_col(g_ref[...], 16 + h + 8 * d)
            ck = _GdnChunk(q_ref[0, :, sl], q_ref[1, :, sl], q_ref[2, :, sl], gcol, beta, d == 1)
            S = s_in[d, hh]
            st_ref[0, hh] = S
            a_ref[0, hh] = ck.A
            v_new = ck.u - _mx(ck.w, S, "nn")
            o_ref[:, sl] = _mx(ck.qd, S, "nn") + _mx(ck.attn, v_new, "nn")
            s_out[d, hh] = S * ck.egl + _mx(ck.kd, v_new, "tn")
        for d, hh in chains:
            s_scr[d, b * hb + hh] = s_out[d, hh]

    fw, bw = (lambda n: n), (lambda n: N - 1 - n)
    qsp = lambda ci: pl.BlockSpec((3, CHUNK, W), lambda n, b: (0, ci(n), b))
    gsp = lambda ci: pl.BlockSpec((CHUNK, LANES), lambda n, b: (ci(n), 0))
    osp = lambda ci: pl.BlockSpec((CHUNK, W), lambda n, b: (ci(n), b))
    ssp = lambda ci: pl.BlockSpec((1, hb, GDN_DIM, GDN_DIM), lambda n, b: (ci(n), b, 0, 0))
    asp = lambda ci: pl.BlockSpec((1, hb, CHUNK, CHUNK), lambda n, b: (ci(n), b, 0, 0))
    o_shape = jax.ShapeDtypeStruct((T, nh * GDN_DIM), f32)
    s_shape = jax.ShapeDtypeStruct((N, nh, GDN_DIM, GDN_DIM), f32)
    a_shape = jax.ShapeDtypeStruct((N, nh, CHUNK, CHUNK), f32)
    return pl.pallas_call(
        body,
        out_shape=(o_shape, o_shape, s_shape, s_shape, a_shape, a_shape),
        grid=(N, nh // hb),
        in_specs=[qsp(fw), qsp(bw), gsp(fw), gsp(bw), gsp(fw), gsp(bw)],
        out_specs=(osp(fw), osp(bw), ssp(fw), ssp(bw), asp(fw), asp(bw)),
        scratch_shapes=[pltpu.VMEM((2, nh, GDN_DIM, GDN_DIM), f32)],
        compiler_params=_cparams(("arbitrary", "arbitrary")),
        name="gdn_scan_fwd",
    )(qkvn, qkvn, gates, gates, gc, gc)


def _gdn_scan_bwd(qkvn, gates, gc, st_f, st_b, a_f, a_b, do):
    T = qkvn.shape[1]
    N = T // CHUNK
    nh, hb = GDN_HEADS, GDN_HB
    nb = nh // hb
    W = hb * GDN_DIM
    chains = [(d, hh) for d in range(2) for hh in range(hb)]

    def body(qf_ref, qb_ref, gf_ref, gb_ref, cf_ref, cb_ref, sf_ref, sb_ref, af_ref, ab_ref, dof_ref, dob_ref,
             dqf_ref, dqb_ref, dgf_ref, dgb_ref, ds_scr):
        n, b = pl.program_id(0), pl.program_id(1)

        @pl.when(n == 0)
        def _():
            for d, hh in chains:
                ds_scr[d, b * hb + hh] = jnp.zeros((GDN_DIM, GDN_DIM), f32)

        @pl.when(b == 0)
        def _():
            dgf_ref[...] = jnp.zeros((CHUNK, LANES), f32)
            dgb_ref[...] = jnp.zeros((CHUNK, LANES), f32)

        dirs = ((qf_ref, gf_ref, cf_ref, sf_ref, af_ref, dof_ref, dqf_ref), (qb_ref, gb_ref, cb_ref, sb_ref, ab_ref, dob_ref, dqb_ref))
        ds_in = {ch: ds_scr[ch[0], b * hb + ch[1]] for ch in chains}
        ds_out = {}
        lane = _iota2((CHUNK, LANES), 1)
        acc = [dgf_ref[...], dgb_ref[...]]
        for d, hh in chains:
            q_ref, g_ref, c_ref, st_ref, a_ref, do_ref, dq_ref = dirs[d]
            h = b * hb + hh
            sl = slice(hh * GDN_DIM, (hh + 1) * GDN_DIM)
            gcol = _lane_col(c_ref[...], h + 8 * d)
            beta = _lane_col(g_ref[...], 16 + h + 8 * d)
            ck = _GdnChunk(q_ref[0, :, sl], q_ref[1, :, sl], q_ref[2, :, sl], gcol, beta, d == 1, A=a_ref[0, hh])
            k, v = ck.k, ck.v
            S = st_ref[0, hh]
            dS = ds_in[d, hh]
            dov = do_ref[:, sl]
            v_new = ck.u - _mx(ck.w, S, "nn")
            d_vnew = _mx(ck.attn, dov, "tn") + _mx(ck.kd, dS, "nn")
            d_attn = jnp.where(ck.incl, _mx(dov, v_new, "nt"), 0.0)
            d_qd = _mx(dov, S, "nt")
            d_kd = _mx(v_new, dS, "nt")
            d_gl = jnp.sum(jnp.sum(dS * S, axis=1, keepdims=True), axis=0, keepdims=True) * ck.egl
            ds_out[d, hh] = dS * ck.egl + _mx(ck.qd, dov, "tn") - _mx(ck.w, d_vnew, "tn")
            d_w = -_mx(d_vnew, S, "nt")
            d_rhs = _h3s(ck.A2, _split2(jnp.concatenate([d_vnew, d_w], axis=1)), "tn")
            d_vb, d_kbg = d_rhs[:, :GDN_DIM], d_rhs[:, GDN_DIM:]
            dL = -jnp.where(ck.strict, _h3(d_rhs, ck.sol, "nt"), 0.0)
            P = dL * ck.decay
            Q = d_attn * ck.decay
            d_kb = _mx(P, k, "nn") + d_kbg * ck.eg
            d_qs = _mx(Q, k, "nn") + d_qd * ck.eg
            d_k = _mx(P, ck.kb, "tn") + _mx(Q, ck.qs, "tn") + d_kd * ck.ekd + d_kb * ck.beta
            E = dL * ck.L + d_attn * ck.attn
            kdsum = jnp.sum(d_kd * ck.kd, axis=1, keepdims=True)
            d_gc = (jnp.sum(E, axis=1, keepdims=True) - _colsum_col(E)
                    + jnp.sum(d_kbg * ck.kbg, axis=1, keepdims=True)
                    + jnp.sum(d_qd * ck.qd, axis=1, keepdims=True) - kdsum)
            d_gl = d_gl + jnp.sum(kdsum, axis=0, keepdims=True)
            row = _iota2((CHUNK, 1), 0)
            d_gc = d_gc + jnp.where(row == ck.last, d_gl, 0.0)
            d_beta = jnp.sum(d_vb * v, axis=1, keepdims=True) + jnp.sum(d_kb * k, axis=1, keepdims=True)
            dq_ref[0, :, sl] = d_qs * (GDN_DIM**-0.5)
            dq_ref[1, :, sl] = d_k
            dq_ref[2, :, sl] = d_vb * ck.beta
            acc[d] = jnp.where(lane == h + 8 * d, d_gc, acc[d])
            acc[d] = jnp.where(lane == 16 + h + 8 * d, d_beta, acc[d])
        for d, hh in chains:
            ds_scr[d, b * hb + hh] = ds_out[d, hh]
        dgf_ref[...] = acc[0]
        dgb_ref[...] = acc[1]

        @pl.when(b == nb - 1)
        def _():
            for d, dg_ref in enumerate((dgf_ref, dgb_ref)):
                blk = dg_ref[...]
                dg_ref[...] = jnp.where(lane < 16, _exact01(_chunk_tri(CHUNK, d == 0), blk), blk)

    fw, bw = (lambda n: N - 1 - n), (lambda n: n)
    qsp = lambda ci: pl.BlockSpec((3, CHUNK, W), lambda n, b: (0, ci(n), b))
    gsp = lambda ci: pl.BlockSpec((CHUNK, LANES), lambda n, b: (ci(n), 0))
    osp = lambda ci: pl.BlockSpec((CHUNK, W), lambda n, b: (ci(n), b))
    ssp = lambda ci: pl.BlockSpec((1, hb, GDN_DIM, GDN_DIM), lambda n, b: (ci(n), b, 0, 0))
    asp = lambda ci: pl.BlockSpec((1, hb, CHUNK, CHUNK), lambda n, b: (ci(n), b, 0, 0))
    dq_shape = jax.ShapeDtypeStruct((3, T, nh * GDN_DIM), f32)
    dg_shape = jax.ShapeDtypeStruct((T, LANES), f32)
    return pl.pallas_call(
        body,
        out_shape=(dq_shape, dq_shape, dg_shape, dg_shape),
        grid=(N, nb),
        in_specs=[qsp(fw), qsp(bw), gsp(fw), gsp(bw), gsp(fw), gsp(bw), ssp(fw), ssp(bw), asp(fw), asp(bw), osp(fw), osp(bw)],
        out_specs=(qsp(fw), qsp(bw), gsp(fw), gsp(bw)),
        scratch_shapes=[pltpu.VMEM((2, nh, GDN_DIM, GDN_DIM), f32)],
        compiler_params=_cparams(("arbitrary", "arbitrary")),
        name="gdn_scan_bwd",
    )(qkvn, qkvn, gates, gates, gc, gc, st_f, st_b, a_f, a_b, do, do)


def _out_norm(o_f, o_b, proj, gate_blk, w, hd, name):
    T, W = o_f.shape
    tb = min(ROW_TILE, T)
    nh = W // hd

    def body(of_ref, ob_ref, z_ref, w_ref, y_ref):
        wv = w_ref[...]
        for i in range(nh):
            sl = slice(i * hd, (i + 1) * hd)
            o = of_ref[:, sl] + ob_ref[:, sl]
            r = lax.rsqrt(jnp.mean(o * o, axis=1, keepdims=True) + NORM_EPS)
            y_ref[:, sl] = (o * r * wv * _silu(z_ref[:, sl])).astype(y_ref.dtype)

    blk = pl.BlockSpec((tb, W), lambda i: (i, 0))
    return pl.pallas_call(
        body,
        out_shape=jax.ShapeDtypeStruct((T, W), MXU_DTYPE),
        grid=(T // tb,),
        in_specs=[blk, blk, pl.BlockSpec((tb, W), lambda i: (i, gate_blk)), pl.BlockSpec((1, hd), lambda i: (0, 0))],
        out_specs=blk,
        compiler_params=_cparams(("parallel",)),
        name=name,
    )(o_f, o_b, proj, w)


def _out_norm_bwd(dy, o_f, o_b, proj, gate_blk, w, hd, name, dproj):
    T, W = o_f.shape
    tb = min(ROW_TILE, T)
    nh = W // hd

    def body(dy_ref, of_ref, ob_ref, z_ref, w_ref, _, do_ref, dz_ref, dw_ref):
        i = pl.program_id(0)
        wv = w_ref[...]
        dw = jnp.zeros((1, hd), f32)
        for a in range(nh):
            sl = slice(a * hd, (a + 1) * hd)
            o = of_ref[:, sl] + ob_ref[:, sl]
            z = z_ref[:, sl]
            dyv = dy_ref[:, sl]
            r = lax.rsqrt(jnp.mean(o * o, axis=1, keepdims=True) + NORM_EPS)
            orr = o * r
            dn = dyv * _silu(z)
            dz_ref[:, sl] = (dyv * orr * wv * _silu_grad(z)).astype(dz_ref.dtype)
            dw = dw + jnp.sum(dn * orr, axis=0, keepdims=True)
            dnw = dn * wv
            do_ref[:, sl] = r * dnw - o * (r * r * r) * jnp.mean(dnw * o, axis=1, keepdims=True)

        @pl.when(i == 0)
        def _():
            dw_ref[...] = dw

        @pl.when(i > 0)
        def _():
            dw_ref[...] += dw

    blk = pl.BlockSpec((tb, W), lambda i: (i, 0))
    row = pl.BlockSpec((1, hd), lambda i: (0, 0))
    return pl.pallas_call(
        body,
        out_shape=(jax.ShapeDtypeStruct((T, W), f32), jax.ShapeDtypeStruct(dproj.shape, dproj.dtype), jax.ShapeDtypeStruct((1, hd), f32)),
        grid=(T // tb,),
        in_specs=[blk, blk, blk, pl.BlockSpec((tb, W), lambda i: (i, gate_blk)), row, _ANY_SPEC],
        out_specs=(blk, pl.BlockSpec((tb, W), lambda i: (i, gate_blk)), row),
        input_output_aliases={5: 1},
        compiler_params=_cparams(("arbitrary",)),
        name=name,
    )(dy, o_f, o_b, proj, w, dproj)


def _gla_prep(proj, w2f, w2b, b2f, b2b, tb=256):
    T = proj.shape[0]
    W = GLA_HEADS * GLA_DK

    def body(s_ref, wf_ref, wb_ref, bf_ref, bb_ref, gf_ref, gb_ref):
        small = s_ref[...]
        gkf = -_softplus(-(_mx(small, wf_ref[...], "nn") + bf_ref[...])) * (1.0 / GATE_NORMALIZER)
        gkb = -_softplus(-(_mx(small, wb_ref[...], "nn") + bb_ref[...])) * (1.0 / GATE_NORMALIZER)
        gf_ref[...] = _exact01(_chunk_tri(tb, False), gkf)
        gb_ref[...] = _exact01(_chunk_tri(tb, True), gkb)

    blk = pl.BlockSpec((tb, W), lambda i: (i, 0))
    wsp = pl.BlockSpec((LANES, W), lambda i: (0, 0))
    row = pl.BlockSpec((1, W), lambda i: (0, 0))
    return pl.pallas_call(
        body,
        out_shape=(jax.ShapeDtypeStruct((T, W), f32), jax.ShapeDtypeStruct((T, W), f32)),
        grid=(T // tb,),
        in_specs=[pl.BlockSpec((tb, LANES), lambda i: (i, SMALL_OFF // LANES)), wsp, wsp, row, row],
        out_specs=(blk, blk),
        compiler_params=_cparams(("parallel",)),
        name="gla_prep",
    )(proj, w2f, w2b, b2f, b2b)


def _gla_prep_bwd(proj, w2f, w2b, b2f, b2b, dgk_f, dgk_b):
    T = proj.shape[0]
    tb = min(ROW_TILE, T)
    W = GLA_HEADS * GLA_DK

    def body(s_ref, wf_ref, wb_ref, bf_ref, bb_ref, df_ref, db_ref, ds_ref, dwf_ref, dwb_ref, dbf_ref, dbb_ref):
        i = pl.program_id(0)
        small = s_ref[...]
        dsm = jnp.zeros((tb, LANES), f32)
        parts = []
        for w_ref, b_ref, d_ref in ((wf_ref, bf_ref, df_ref), (wb_ref, bb_ref, db_ref)):
            pre = _mx(small, w_ref[...], "nn") + b_ref[...]
            dpre = d_ref[...] * (1.0 / GATE_NORMALIZER) * _sigmoid(-pre)
            dsm = dsm + _mx(dpre, w_ref[...], "nt")
            parts.append((_mx(small, dpre, "tn"), jnp.sum(dpre, axis=0, keepdims=True)))
        ds_ref[...] = dsm

        @pl.when(i == 0)
        def _():
            dwf_ref[...], dbf_ref[...] = parts[0]
            dwb_ref[...], dbb_ref[...] = parts[1]

        @pl.when(i > 0)
        def _():
            dwf_ref[...] += parts[0][0]
            dbf_ref[...] += parts[0][1]
            dwb_ref[...] += parts[1][0]
            dbb_ref[...] += parts[1][1]

    blk = pl.BlockSpec((tb, W), lambda i: (i, 0))
    wsp = pl.BlockSpec((LANES, W), lambda i: (0, 0))
    row = pl.BlockSpec((1, W), lambda i: (0, 0))
    return pl.pallas_call(
        body,
        out_shape=(jax.ShapeDtypeStruct((T, LANES), f32), jax.ShapeDtypeStruct((LANES, W), f32), jax.ShapeDtypeStruct((LANES, W), f32),
                   jax.ShapeDtypeStruct((1, W), f32), jax.ShapeDtypeStruct((1, W), f32)),
        grid=(T // tb,),
        in_specs=[pl.BlockSpec((tb, LANES), lambda i: (i, SMALL_OFF // LANES)), wsp, wsp, row, row, blk, blk],
        out_specs=(pl.BlockSpec((tb, LANES), lambda i: (i, 0)), wsp, wsp, row, row),
        compiler_params=_cparams(("arbitrary",)),
        name="gla_prep_bwd",
    )(proj, w2f, w2b, b2f, b2b, dgk_f, dgk_b)


Q_B_BLK, K_B_BLK, V_B_BLK = 4096 // GLA_DK, 4608 // GLA_DK, 5120 // GLA_DV


class _GlaChunk:
    def __init__(self, q, k, G, rev):
        Cn = q.shape[0]
        ii, jj = _iota2((Cn, Cn), 0), _iota2((Cn, Cn), 1)
        self.incl = (ii <= jj) if rev else (ii >= jj)
        self.eG = jnp.exp(G)
        self.enG = jnp.exp(-G)
        self.qg = q * (GLA_DK**-0.5) * self.eG
        self.kg = k * self.enG
        self.attn = jnp.where(self.incl, _mx(self.qg, self.kg, "nt"), 0.0)
        last = 0 if rev else Cn - 1
        self.last = last
        gl = G[last : last + 1, :]
        self.ekd = jnp.exp(gl - G)
        self.kd = k * self.ekd
        self.egl = jnp.exp(gl)


def _gla_scan_fwd(proj, G_f, G_b):
    T = proj.shape[0]
    N = T // CHUNK
    nh = GLA_HEADS

    def body(qf_ref, kf_ref, vf_ref, gf_ref, qb_ref, kb_ref, vb_ref, gb_ref, of_ref, ob_ref, sf_ref, sb_ref, s_scr):
        n, h = pl.program_id(0), pl.program_id(1)

        @pl.when(n == 0)
        def _():
            s_scr[0, h] = jnp.zeros((GLA_DV, GLA_DK), f32)
            s_scr[1, h] = jnp.zeros((GLA_DV, GLA_DK), f32)

        for d, (q_ref, k_ref, v_ref, g_ref, o_ref, st_ref) in enumerate(
            ((qf_ref, kf_ref, vf_ref, gf_ref, of_ref, sf_ref), (qb_ref, kb_ref, vb_ref, gb_ref, ob_ref, sb_ref))
        ):
            ck = _GlaChunk(q_ref[...], k_ref[...], g_ref[...], d == 1)
            v = v_ref[...]
            St = s_scr[d, h]
            st_ref[0, 0] = St
            o_ref[...] = _mx(ck.qg, St, "nt") + _mx(ck.attn, v, "nn")
            s_scr[d, h] = St * ck.egl + _mx(v, ck.kd, "tn")

    def specs(cidx):
        return [
            pl.BlockSpec((CHUNK, GLA_DK), lambda n, h: (cidx(n), Q_B_BLK + h)),
            pl.BlockSpec((CHUNK, GLA_DK), lambda n, h: (cidx(n), K_B_BLK + h)),
            pl.BlockSpec((CHUNK, GLA_DV), lambda n, h: (cidx(n), V_B_BLK + h)),
            pl.BlockSpec((CHUNK, GLA_DK), lambda n, h: (cidx(n), h)),
        ]

    fw, bw = (lambda n: n), (lambda n: N - 1 - n)
    o_shape = jax.ShapeDtypeStruct((T, nh * GLA_DV), f32)
    s_shape = jax.ShapeDtypeStruct((N, nh, GLA_DV, GLA_DK), f32)
    return pl.pallas_call(
        body,
        out_shape=(o_shape, o_shape, s_shape, s_shape),
        grid=(N, nh),
        in_specs=specs(fw) + specs(bw),
        out_specs=(
            pl.BlockSpec((CHUNK, GLA_DV), lambda n, h: (n, h)),
            pl.BlockSpec((CHUNK, GLA_DV), lambda n, h: (N - 1 - n, h)),
            pl.BlockSpec((1, 1, GLA_DV, GLA_DK), lambda n, h: (n, h, 0, 0)),
            pl.BlockSpec((1, 1, GLA_DV, GLA_DK), lambda n, h: (N - 1 - n, h, 0, 0)),
        ),
        scratch_shapes=[pltpu.VMEM((2, nh, GLA_DV, GLA_DK), f32)],
        compiler_params=_cparams(("arbitrary", "arbitrary")),
        name="gla_scan_fwd",
    )(proj, proj, proj, G_f, proj, proj, proj, G_b)


def _gla_scan_bwd(proj, G_f, G_b, st_f, st_b, do):
    T = proj.shape[0]
    N = T // CHUNK
    nh = GLA_HEADS

    def body(qf_ref, kf_ref, vf_ref, gf_ref, sf_ref, dof_ref, qb_ref, kb_ref, vb_ref, gb_ref, sb_ref, dob_ref,
             dqf_ref, dkf_ref, dvf_ref, dgf_ref, dqb_ref, dkb_ref, dvb_ref, dgb_ref, ds_scr):
        n, h = pl.program_id(0), pl.program_id(1)

        @pl.when(n == 0)
        def _():
            ds_scr[0, h] = jnp.zeros((GLA_DV, GLA_DK), f32)
            ds_scr[1, h] = jnp.zeros((GLA_DV, GLA_DK), f32)

        for d, (q_ref, k_ref, v_ref, g_ref, st_ref, do_ref, dq_ref, dk_ref, dv_ref, dg_ref) in enumerate(
            ((qf_ref, kf_ref, vf_ref, gf_ref, sf_ref, dof_ref, dqf_ref, dkf_ref, dvf_ref, dgf_ref),
             (qb_ref, kb_ref, vb_ref, gb_ref, sb_ref, dob_ref, dqb_ref, dkb_ref, dvb_ref, dgb_ref))
        ):
            rev = d == 1
            ck = _GlaChunk(q_ref[...], k_ref[...], g_ref[...], rev)
            v = v_ref[...]
            St = st_ref[0, 0]
            dSt = ds_scr[d, h]
            dov = do_ref[...]
            d_attn = jnp.where(ck.incl, _mx(dov, v, "nt"), 0.0)
            d_qg = _mx(dov, St, "nn") + _mx(d_attn, ck.kg, "nn")
            d_kg = _mx(d_attn, ck.qg, "tn")
            dv_ref[...] = _mx(ck.attn, dov, "tn") + _mx(ck.kd, dSt, "nt")
            d_kd = _mx(v, dSt, "nn")
            ds_scr[d, h] = dSt * ck.egl + _mx(dov, ck.qg, "tn")
            kdd = d_kd * ck.kd
            d_gl = jnp.sum(dSt * St, axis=0, keepdims=True) * ck.egl + jnp.sum(kdd, axis=0, keepdims=True)
            dq_ref[...] = d_qg * ck.eG * (GLA_DK**-0.5)
            dk_ref[...] = d_kg * ck.enG + d_kd * ck.ekd
            d_G = d_qg * ck.qg - d_kg * ck.kg - kdd
            row = _iota2((CHUNK, GLA_DK), 0)
            d_G = d_G + jnp.where(row == ck.last, d_gl, 0.0)
            dg_ref[...] = _exact01(_chunk_tri(CHUNK, not rev), d_G)

    def in_specs(cidx):
        return [
            pl.BlockSpec((CHUNK, GLA_DK), lambda n, h: (cidx(n), Q_B_BLK + h)),
            pl.BlockSpec((CHUNK, GLA_DK), lambda n, h: (cidx(n), K_B_BLK + h)),
            pl.BlockSpec((CHUNK, GLA_DV), lambda n, h: (cidx(n), V_B_BLK + h)),
            pl.BlockSpec((CHUNK, GLA_DK), lambda n, h: (cidx(n), h)),
            pl.BlockSpec((1, 1, GLA_DV, GLA_DK), lambda n, h: (cidx(n), h, 0, 0)),
            pl.BlockSpec((CHUNK, GLA_DV), lambda n, h: (cidx(n), h)),
        ]

    def out_specs(cidx):
        return [
            pl.BlockSpec((CHUNK, GLA_DK), lambda n, h: (cidx(n), h)),
            pl.BlockSpec((CHUNK, GLA_DK), lambda n, h: (cidx(n), h)),
            pl.BlockSpec((CHUNK, GLA_DV), lambda n, h: (cidx(n), h)),
            pl.BlockSpec((CHUNK, GLA_DK), lambda n, h: (cidx(n), h)),
        ]

    fw, bw = (lambda n: N - 1 - n), (lambda n: n)
    k_shape = jax.ShapeDtypeStruct((T, nh * GLA_DK), f32)
    v_shape = jax.ShapeDtypeStruct((T, nh * GLA_DV), f32)
    return pl.pallas_call(
        body,
        out_shape=(k_shape, k_shape, v_shape, k_shape) * 2,
        grid=(N, nh),
        in_specs=in_specs(fw) + in_specs(bw),
        out_specs=tuple(out_specs(fw) + out_specs(bw)),
        scratch_shapes=[pltpu.VMEM((2, nh, GLA_DV, GLA_DK), f32)],
        compiler_params=_cparams(("arbitrary", "arbitrary")),
        name="gla_scan_bwd",
    )(proj, proj, proj, G_f, st_f, do, proj, proj, proj, G_b, st_b, do)


GATE_A_BLK, GATE_B_BLK, Z_BLK, G_B_BLK = 6144 // 1024, 7168 // 1024, 3072 // 1024, 8192 // 1024
DP_GATES_BLK, DP_GLA_QKV_BLK, DP_SMALL_BLK = 6144 // 2048, 4096 // 2048, SMALL_OFF // 512
_ANY_SPEC = pl.BlockSpec(memory_space=pl.ANY)


def _mid(og, obg, proj, x, target, wpg, wpl, wout, w_post, tb=256):
    T = x.shape[0]
    Dm = D_MODEL

    def body(og_ref, obg_ref, ga_ref, gb_ref, x_ref, t_ref, wpg_ref, wpl_ref, wo_ref, wp_ref,
             loss_ref, dwp_ref, dy_ref, m_ref, dout_ref, dya_ref, dyb_ref, dgates_ref, dog_ref, dobg_ref):
        i = pl.program_id(0)
        ya = _mx(og_ref[...], wpg_ref[...], "nn")
        yb = _mx(obg_ref[...], wpl_ref[...], "nn")
        sa, sb = _sigmoid(ga_ref[...]), _sigmoid(gb_ref[...])
        m = sa * ya + sb * yb
        m_ref[...] = m.astype(m_ref.dtype)
        out = _mx(m, wo_ref[...], "nn")
        r = lax.rsqrt(jnp.mean(out * out, axis=1, keepdims=True) + NORM_EPS)
        wp = wp_ref[...]
        e = x_ref[...] + out * r * wp - t_ref[...]
        dy = e * (1.0 / Dm)
        dy_ref[...] = dy
        lpart = jnp.sum(e * e, axis=0, keepdims=True)
        wpart = jnp.sum(dy * out * r, axis=0, keepdims=True)
        dyn = dy * wp
        dout = r * dyn - out * (r * r * r) * jnp.mean(dyn * out, axis=1, keepdims=True)
        dout_ref[...] = dout.astype(dout_ref.dtype)
        dm = _mx(dout, wo_ref[...], "nt")
        dya, dyb = dm * sa, dm * sb
        dya_ref[...] = dya.astype(dya_ref.dtype)
        dyb_ref[...] = dyb.astype(dyb_ref.dtype)
        dgates_ref[:, :Dm] = (dm * ya * sa * (1.0 - sa)).astype(dgates_ref.dtype)
        dgates_ref[:, Dm:] = (dm * yb * sb * (1.0 - sb)).astype(dgates_ref.dtype)
        dog_ref[...] = _mx(dya, wpg_ref[...], "nt")
        dobg_ref[...] = _mx(dyb, wpl_ref[...], "nt")

        @pl.when(i == 0)
        def _():
            loss_ref[...] = lpart
            dwp_ref[...] = wpart

        @pl.when(i > 0)
        def _():
            loss_ref[...] += lpart
            dwp_ref[...] += wpart

    blk = pl.BlockSpec((tb, Dm), lambda i: (i, 0))
    row = pl.BlockSpec((1, Dm), lambda i: (0, 0))
    wsp = pl.BlockSpec((Dm, Dm), lambda i: (0, 0))
    act = jax.ShapeDtypeStruct((T, Dm), MXU_DTYPE)
    big = jax.ShapeDtypeStruct((T, Dm), f32)
    vec = jax.ShapeDtypeStruct((1, Dm), f32)
    return pl.pallas_call(
        body,
        out_shape=(vec, vec, big, act, act, act, act, jax.ShapeDtypeStruct((T, N_CAT), MXU_DTYPE), big, big),
        grid=(T // tb,),
        in_specs=[blk, blk, pl.BlockSpec((tb, Dm), lambda i: (i, GATE_A_BLK)), pl.BlockSpec((tb, Dm), lambda i: (i, GATE_B_BLK)),
                  blk, blk, wsp, wsp, wsp, row],
        out_specs=(row, row) + (blk,) * 5 + (pl.BlockSpec((tb, 2 * Dm), lambda i: (i, DP_GATES_BLK)), blk, blk),
        compiler_params=_cparams(("arbitrary",)),
        name="mid",
    )(og, obg, proj, proj, x, target, wpg, wpl, wout, w_post)


def _heads(ref, lead, n, width, rows=slice(None)):
    return [ref[lead + (rows, slice(i * width, (i + 1) * width))] for i in range(n)]


class _GdnBatch:
    def __init__(self, q, k, v, gcol, grow, beta, nf, A=None):
        B, Cn = q.shape[0], q.shape[1]
        self.nf = nf
        self.incl, self.strict, eye = _dir_masks(B, nf, Cn)
        self.decay = jnp.where(self.incl, jnp.exp(jnp.where(self.incl, gcol - grow, 0.0)), 0.0)
        self.k, self.v, self.beta = k, v, beta
        self.qs = q * (GDN_DIM**-0.5)
        self.kb = k * beta
        self.L = jnp.where(self.strict, _mx(self.kb, k, "nt") * self.decay, 0.0)
        if A is None:
            A = _inv_unit_lower(self.L, jnp.where(eye, 1.0, 0.0).astype(f32))
        self.A = A
        self.A2 = _split2(A)
        self.eg = jnp.exp(gcol)
        self.kbg = self.kb * self.eg
        self.sol = _h3s(self.A2, _split2(jnp.concatenate([v * beta, self.kbg], axis=2)), "nn")
        self.u = self.sol[:, :, :GDN_DIM]
        self.w = self.sol[:, :, GDN_DIM:]
        self.attn = jnp.where(self.incl, _mx(self.qs, k, "nt") * self.decay, 0.0)
        self.qd = self.qs * self.eg
        gl = _last_row(gcol, nf)
        self.ekd = jnp.exp(gl - gcol)
        self.kd = k * self.ekd
        self.egl = jnp.exp(gl)


def _gdn_chain_inputs(qf_ref, qb_ref, gf_ref, gb_ref, cf_ref, cb_ref, b, hb):
    qkv = [jnp.stack(_heads(qf_ref, (i,), hb, GDN_DIM) + _heads(qb_ref, (i,), hb, GDN_DIM)) for i in range(3)]
    cols, rows, betas = [], [], []
    for d, (g_ref, c_ref) in enumerate(((gf_ref, cf_ref), (gb_ref, cb_ref))):
        gates, gc = g_ref[...], c_ref[...]
        for hh in range(hb):
            h = b * hb + hh
            col = _lane_col(gc, h + 8 * d)
            cols.append(col)
            rows.append(jnp.broadcast_to(col, (CHUNK, LANES)).T[0:1, :])
            betas.append(_lane_col(gates, 16 + h + 8 * d))
    return qkv[0], qkv[1], qkv[2], jnp.stack(cols), jnp.stack(rows), jnp.stack(betas)


def _gdn_fwd(qkvn, gates, gc):
    T = qkvn.shape[1]
    N = T // CHUNK
    nh, hb = GDN_HEADS, GDN_HB
    W = hb * GDN_DIM

    def body(qf_ref, qb_ref, gf_ref, gb_ref, cf_ref, cb_ref, of_ref, ob_ref, sf_ref, sb_ref, af_ref, ab_ref, s_scr):
        n, b = pl.program_id(0), pl.program_id(1)
        hs = pl.ds(b * hb, hb)

        @pl.when(n == 0)
        def _():
            s_scr[0, hs] = jnp.zeros((hb, GDN_DIM, GDN_DIM), f32)
            s_scr[1, hs] = jnp.zeros((hb, GDN_DIM, GDN_DIM), f32)

        q, k, v, gcol, grow, beta = _gdn_chain_inputs(qf_ref, qb_ref, gf_ref, gb_ref, cf_ref, cb_ref, b, hb)
        ck = _GdnBatch(q, k, v, gcol, grow, beta, hb)
        S = jnp.concatenate([s_scr[0, hs], s_scr[1, hs]], axis=0)
        sf_ref[0], sb_ref[0] = S[:hb], S[hb:]
        af_ref[0], ab_ref[0] = ck.A[:hb], ck.A[hb:]
        v_new = ck.u - _mx(ck.w, S, "nn")
        o = _mx(ck.qd, S, "nn") + _mx(ck.attn, v_new, "nn")
        S_new = S * ck.egl + _mx(ck.kd, v_new, "tn")
        for hh in range(hb):
            sl = slice(hh * GDN_DIM, (hh + 1) * GDN_DIM)
            of_ref[:, sl] = o[hh]
            ob_ref[:, sl] = o[hb + hh]
        s_scr[0, hs] = S_new[:hb]
        s_scr[1, hs] = S_new[hb:]

    fw, bw = (lambda n: n), (lambda n: N - 1 - n)
    qsp = lambda ci: pl.BlockSpec((3, CHUNK, W), lambda n, b: (0, ci(n), b))
    gsp = lambda ci: pl.BlockSpec((CHUNK, LANES), lambda n, b: (ci(n), 0))
    osp = lambda ci: pl.BlockSpec((CHUNK, W), lambda n, b: (ci(n), b))
    ssp = lambda ci: pl.BlockSpec((1, hb, GDN_DIM, GDN_DIM), lambda n, b: (ci(n), b, 0, 0))
    asp = lambda ci: pl.BlockSpec((1, hb, CHUNK, CHUNK), lambda n, b: (ci(n), b, 0, 0))
    o_shape = jax.ShapeDtypeStruct((T, nh * GDN_DIM), f32)
    s_shape = jax.ShapeDtypeStruct((N, nh, GDN_DIM, GDN_DIM), f32)
    a_shape = jax.ShapeDtypeStruct((N, nh, CHUNK, CHUNK), f32)
    return pl.pallas_call(
        body,
        out_shape=(o_shape, o_shape, s_shape, s_shape, a_shape, a_shape),
        grid=(N, nh // hb),
        in_specs=[qsp(fw), qsp(bw), gsp(fw), gsp(bw), gsp(fw), gsp(bw)],
        out_specs=(osp(fw), osp(bw), ssp(fw), ssp(bw), asp(fw), asp(bw)),
        scratch_shapes=[pltpu.VMEM((2, nh, GDN_DIM, GDN_DIM), f32)],
        compiler_params=_cparams(("arbitrary", "arbitrary")),
        name="gdn_scan_fwd",
    )(qkvn, qkvn, gates, gates, gc, gc)


def _gdn_bwd(qkvn, gates, gc, st_f, st_b, a_f, a_b, do):
    T = qkvn.shape[1]
    N = T // CHUNK
    nh, hb = GDN_HEADS, GDN_HB
    nb = nh // hb
    W = hb * GDN_DIM

    def body(qf_ref, qb_ref, gf_ref, gb_ref, cf_ref, cb_ref, sf_ref, sb_ref, af_ref, ab_ref, dof_ref, dob_ref,
             dqf_ref, dqb_ref, dgf_ref, dgb_ref, ds_scr):
        n, b = pl.program_id(0), pl.program_id(1)
        hs = pl.ds(b * hb, hb)

        @pl.when(n == 0)
        def _():
            ds_scr[0, hs] = jnp.zeros((hb, GDN_DIM, GDN_DIM), f32)
            ds_scr[1, hs] = jnp.zeros((hb, GDN_DIM, GDN_DIM), f32)

        @pl.when(b == 0)
        def _():
            dgf_ref[...] = jnp.zeros((CHUNK, LANES), f32)
            dgb_ref[...] = jnp.zeros((CHUNK, LANES), f32)

        q, k, v, gcol, grow, beta = _gdn_chain_inputs(qf_ref, qb_ref, gf_ref, gb_ref, cf_ref, cb_ref, b, hb)
        ck = _GdnBatch(q, k, v, gcol, grow, beta, hb, A=jnp.concatenate([af_ref[0], ab_ref[0]], axis=0))
        S = jnp.concatenate([sf_ref[0], sb_ref[0]], axis=0)
        dS = jnp.concatenate([ds_scr[0, hs], ds_scr[1, hs]], axis=0)
        dov = jnp.stack(_heads(dof_ref, (), hb, GDN_DIM) + _heads(dob_ref, (), hb, GDN_DIM))
        v_new = ck.u - _mx(ck.w, S, "nn")
        d_vnew = _mx(ck.attn, dov, "tn") + _mx(ck.kd, dS, "nn")
        d_attn = jnp.where(ck.incl, _mx(dov, v_new, "nt"), 0.0)
        d_qd = _mx(dov, S, "nt")
        d_kd = _mx(v_new, dS, "nt")
        d_gl = jnp.sum(jnp.sum(dS * S, axis=2, keepdims=True), axis=1, keepdims=True) * ck.egl
        dS_new = dS * ck.egl + _mx(ck.qd, dov, "tn") - _mx(ck.w, d_vnew, "tn")
        ds_scr[0, hs] = dS_new[:hb]
        ds_scr[1, hs] = dS_new[hb:]
        d_w = -_mx(d_vnew, S, "nt")
        d_rhs = _h3s(ck.A2, _split2(jnp.concatenate([d_vnew, d_w], axis=2)), "tn")
        d_vb, d_kbg = d_rhs[:, :, :GDN_DIM], d_rhs[:, :, GDN_DIM:]
        dL = -jnp.where(ck.strict, _h3(d_rhs, ck.sol, "nt"), 0.0)
        P = dL * ck.decay
        Q = d_attn * ck.decay
        d_kb = _mx(P, k, "nn") + d_kbg * ck.eg
        d_qs = _mx(Q, k, "nn") + d_qd * ck.eg
        d_k = _mx(P, ck.kb, "tn") + _mx(Q, ck.qs, "tn") + d_kd * ck.ekd + d_kb * ck.beta
        E = dL * ck.L + d_attn * ck.attn
        kdsum = jnp.sum(d_kd * ck.kd, axis=2, keepdims=True)
        d_gc = (jnp.sum(E, axis=2, keepdims=True) - _colsum_col(E)
                + jnp.sum(d_kbg * ck.kbg, axis=2, keepdims=True)
                + jnp.sum(d_qd * ck.qd, axis=2, keepdims=True) - kdsum)
        d_gc = _add_at_last_row(d_gc, d_gl + jnp.sum(kdsum, axis=1, keepdims=True), hb)
        d_beta = jnp.sum(d_vb * v, axis=2, keepdims=True) + jnp.sum(d_kb * k, axis=2, keepdims=True)
        d_q = d_qs * (GDN_DIM**-0.5)
        d_v = d_vb * ck.beta
        lane = _iota2((CHUNK, LANES), 1)
        for d, (dq_ref, dg_ref) in enumerate(((dqf_ref, dgf_ref), (dqb_ref, dgb_ref))):
            acc = dg_ref[...]
            for hh in range(hb):
                c, h = d * hb + hh, b * hb + hh
                sl = slice(hh * GDN_DIM, (hh + 1) * GDN_DIM)
                dq_ref[0, :, sl] = d_q[c]
                dq_ref[1, :, sl] = d_k[c]
                dq_ref[2, :, sl] = d_v[c]
                acc = jnp.where(lane == h + 8 * d, d_gc[c], acc)
                acc = jnp.where(lane == 16 + h + 8 * d, d_beta[c], acc)
            dg_ref[...] = acc

        @pl.when(b == nb - 1)
        def _():
            for d, dg_ref in enumerate((dgf_ref, dgb_ref)):
                blk = dg_ref[...]
                dg_ref[...] = jnp.where(lane < 16, _exact01(_chunk_tri(CHUNK, d == 0), blk), blk)

    fw, bw = (lambda n: N - 1 - n), (lambda n: n)
    qsp = lambda ci: pl.BlockSpec((3, CHUNK, W), lambda n, b: (0, ci(n), b))
    gsp = lambda ci: pl.BlockSpec((CHUNK, LANES), lambda n, b: (ci(n), 0))
    osp = lambda ci: pl.BlockSpec((CHUNK, W), lambda n, b: (ci(n), b))
    ssp = lambda ci: pl.BlockSpec((1, hb, GDN_DIM, GDN_DIM), lambda n, b: (ci(n), b, 0, 0))
    asp = lambda ci: pl.BlockSpec((1, hb, CHUNK, CHUNK), lambda n, b: (ci(n), b, 0, 0))
    dq_shape = jax.ShapeDtypeStruct((3, T, nh * GDN_DIM), f32)
    dg_shape = jax.ShapeDtypeStruct((T, LANES), f32)
    return pl.pallas_call(
        body,
        out_shape=(dq_shape, dq_shape, dg_shape, dg_shape),
        grid=(N, nb),
        in_specs=[qsp(fw), qsp(bw), gsp(fw), gsp(bw), gsp(fw), gsp(bw), ssp(fw), ssp(bw), asp(fw), asp(bw), osp(fw), osp(bw)],
        out_specs=(qsp(fw), qsp(bw), gsp(fw), gsp(bw)),
        scratch_shapes=[pltpu.VMEM((2, nh, GDN_DIM, GDN_DIM), f32)],
        compiler_params=_cparams(("arbitrary", "arbitrary")),
        name="gdn_scan_bwd",
    )(qkvn, qkvn, gates, gates, gc, gc, st_f, st_b, a_f, a_b, do, do)


class _GlaBatch:
    def __init__(self, q, k, G, nf):
        B, Cn = q.shape[0], q.shape[1]
        self.incl, _, _ = _dir_masks(B, nf, Cn)
        self.eG = jnp.exp(G)
        self.enG = jnp.exp(-G)
        self.qg = q * (GLA_DK**-0.5) * self.eG
        self.kg = k * self.enG
        self.attn = jnp.where(self.incl, _mx(self.qg, self.kg, "nt"), 0.0)
        gl = _last_row(G, nf)
        self.ekd = jnp.exp(gl - G)
        self.kd = k * self.ekd
        self.egl = jnp.exp(gl)


GLA_CPS = 4


def _chunk_rows(s):
    return slice(s * CHUNK, (s + 1) * CHUNK)


def _gla_chain_inputs(refs_f, refs_b, rows_f, rows_b):
    nh = GLA_HEADS
    out = []
    for i, width in enumerate((GLA_DK, GLA_DK, GLA_DV, GLA_DK)):
        out.append(jnp.stack(_heads(refs_f[i], (), nh, width, rows_f) + _heads(refs_b[i], (), nh, width, rows_b)))
    return out


def _gla_specs(rows, ci):
    nh = GLA_HEADS
    return [
        pl.BlockSpec((rows, nh * GLA_DK), lambda n: (ci(n), 4096 // (nh * GLA_DK))),
        pl.BlockSpec((rows, nh * GLA_DK), lambda n: (ci(n), 4608 // (nh * GLA_DK))),
        pl.BlockSpec((rows, nh * GLA_DV), lambda n: (ci(n), 5120 // (nh * GLA_DV))),
        pl.BlockSpec((rows, nh * GLA_DK), lambda n: (ci(n), 0)),
    ]


def _gla_fwd(proj, G_f, G_b):
    T = proj.shape[0]
    N = T // CHUNK
    nh = GLA_HEADS
    cps = min(GLA_CPS, N)
    NB = N // cps

    def body(qf, kf, vf, gf, qb, kb, vb, gb, of_ref, ob_ref, sf_ref, sb_ref, s_scr):
        n = pl.program_id(0)

        @pl.when(n == 0)
        def _():
            s_scr[...] = jnp.zeros((2 * nh, GLA_DV, GLA_DK), f32)

        for s in range(cps):
            sb = cps - 1 - s
            q, k, v, G = _gla_chain_inputs((qf, kf, vf, gf), (qb, kb, vb, gb), _chunk_rows(s), _chunk_rows(sb))
            ck = _GlaBatch(q, k, G, nh)
            St = s_scr[...]
            sf_ref[s], sb_ref[sb] = St[:nh].astype(sf_ref.dtype), St[nh:].astype(sb_ref.dtype)
            o = _mx(ck.qg, St, "nt") + _mx(ck.attn, v, "nn")
            for i in range(nh):
                sl = slice(i * GLA_DV, (i + 1) * GLA_DV)
                of_ref[_chunk_rows(s), sl] = o[i]
                ob_ref[_chunk_rows(sb), sl] = o[nh + i]
            s_scr[...] = St * ck.egl + _mx(v, ck.kd, "tn")

    fw, bw = (lambda n: n), (lambda n: NB - 1 - n)
    osp = lambda ci: pl.BlockSpec((cps * CHUNK, nh * GLA_DV), lambda n: (ci(n), 0))
    ssp = lambda ci: pl.BlockSpec((cps, nh, GLA_DV, GLA_DK), lambda n: (ci(n), 0, 0, 0))
    o_shape = jax.ShapeDtypeStruct((T, nh * GLA_DV), f32)
    s_shape = jax.ShapeDtypeStruct((N, nh, GLA_DV, GLA_DK), MXU_DTYPE)
    return pl.pallas_call(
        body,
        out_shape=(o_shape, o_shape, s_shape, s_shape),
        grid=(NB,),
        in_specs=_gla_specs(cps * CHUNK, fw) + _gla_specs(cps * CHUNK, bw),
        out_specs=(osp(fw), osp(bw), ssp(fw), ssp(bw)),
        scratch_shapes=[pltpu.VMEM((2 * nh, GLA_DV, GLA_DK), f32)],
        compiler_params=_cparams(("arbitrary",)),
        name="gla_scan_fwd",
    )(proj, proj, proj, G_f, proj, proj, proj, G_b)


def _gla_bwd(proj, G_f, G_b, st_f, st_b, do):
    T = proj.shape[0]
    N = T // CHUNK
    nh = GLA_HEADS
    cps = min(GLA_CPS, N)
    NB = N // cps

    def body(qf, kf, vf, gf, sf_ref, dof_ref, qb, kb, vb, gb, sb_ref, dob_ref,
             dqf_ref, dkf_ref, dvf_ref, dgf_ref, dqb_ref, dkb_ref, dvb_ref, dgb_ref, ds_scr):
        n = pl.program_id(0)

        @pl.when(n == 0)
        def _():
            ds_scr[...] = jnp.zeros((2 * nh, GLA_DV, GLA_DK), f32)

        shp = (2 * nh, CHUNK, CHUNK)
        bb, ii, jj = _iota3(shp, 0), _iota3(shp, 1), _iota3(shp, 2)
        tri = jnp.where(jnp.logical_or(jnp.logical_and(bb < nh, jj >= ii), jnp.logical_and(bb >= nh, jj <= ii)), 1.0, 0.0)
        for s in range(cps):
            sf = cps - 1 - s
            rows = (_chunk_rows(sf), _chunk_rows(s))
            q, k, v, G = _gla_chain_inputs((qf, kf, vf, gf), (qb, kb, vb, gb), *rows)
            ck = _GlaBatch(q, k, G, nh)
            St = jnp.concatenate([sf_ref[sf], sb_ref[s]], axis=0).astype(f32)
            dSt = ds_scr[...]
            dov = jnp.stack(_heads(dof_ref, (), nh, GLA_DV, rows[0]) + _heads(dob_ref, (), nh, GLA_DV, rows[1]))
            d_attn = jnp.where(ck.incl, _mx(dov, v, "nt"), 0.0)
            d_qg = _mx(dov, St, "nn") + _mx(d_attn, ck.kg, "nn")
            d_kg = _mx(d_attn, ck.qg, "tn")
            d_v = _mx(ck.attn, dov, "tn") + _mx(ck.kd, dSt, "nt")
            d_kd = _mx(v, dSt, "nn")
            ds_scr[...] = dSt * ck.egl + _mx(dov, ck.qg, "tn")
            kdd = d_kd * ck.kd
            d_gl = jnp.sum(dSt * St, axis=1, keepdims=True) * ck.egl + jnp.sum(kdd, axis=1, keepdims=True)
            d_q = d_qg * ck.eG * (GLA_DK**-0.5)
            d_k = d_kg * ck.enG + d_kd * ck.ekd
            d_gk = _exact01(tri, _add_at_last_row(d_qg * ck.qg - d_kg * ck.kg - kdd, d_gl, nh))
            for d, (dq_ref, dk_ref, dv_ref, dg_ref) in enumerate(((dqf_ref, dkf_ref, dvf_ref, dgf_ref), (dqb_ref, dkb_ref, dvb_ref, dgb_ref))):
                for i in range(nh):
                    c = d * nh + i
                    ks, vs = slice(i * GLA_DK, (i + 1) * GLA_DK), slice(i * GLA_DV, (i + 1) * GLA_DV)
                    dq_ref[rows[d], ks] = d_q[c]
                    dk_ref[rows[d], ks] = d_k[c]
                    dv_ref[rows[d], vs] = d_v[c]
                    dg_ref[rows[d], ks] = d_gk[c]

    R = cps * CHUNK

    def in_specs(ci):
        return _gla_specs(R, ci) + [
            pl.BlockSpec((cps, nh, GLA_DV, GLA_DK), lambda n: (ci(n), 0, 0, 0)),
            pl.BlockSpec((R, nh * GLA_DV), lambda n: (ci(n), 0)),
        ]

    def out_specs(ci):
        ksp = pl.BlockSpec((R, nh * GLA_DK), lambda n: (ci(n), 0))
        return [ksp, ksp, pl.BlockSpec((R, nh * GLA_DV), lambda n: (ci(n), 0)), ksp]

    fw, bw = (lambda n: NB - 1 - n), (lambda n: n)
    k_shape = jax.ShapeDtypeStruct((T, nh * GLA_DK), f32)
    v_shape = jax.ShapeDtypeStruct((T, nh * GLA_DV), f32)
    return pl.pallas_call(
        body,
        out_shape=(k_shape, k_shape, v_shape, k_shape) * 2,
        grid=(NB,),
        in_specs=in_specs(fw) + in_specs(bw),
        out_specs=tuple(out_specs(fw) + out_specs(bw)),
        scratch_shapes=[pltpu.VMEM((2 * nh, GLA_DV, GLA_DK), f32)],
        compiler_params=_cparams(("arbitrary",)),
        name="gla_scan_bwd",
    )(proj, proj, proj, G_f, st_f, do, proj, proj, proj, G_b, st_b, do)


def _gla_dsum(dq_f, dq_b, dk_f, dk_b, dv_f, dv_b, dproj):
    T = dq_f.shape[0]
    tb = min(ROW_TILE, T)
    wk, wv = GLA_HEADS * GLA_DK, GLA_HEADS * GLA_DV

    def body(qf, qb, kf, kb, vf, vb, _, o_ref):
        o_ref[:, :wk] = (qf[...] + qb[...]).astype(o_ref.dtype)
        o_ref[:, wk : 2 * wk] = (kf[...] + kb[...]).astype(o_ref.dtype)
        o_ref[:, 2 * wk :] = (vf[...] + vb[...]).astype(o_ref.dtype)

    ksp = pl.BlockSpec((tb, wk), lambda i: (i, 0))
    vsp = pl.BlockSpec((tb, wv), lambda i: (i, 0))
    return pl.pallas_call(
        body,
        out_shape=jax.ShapeDtypeStruct(dproj.shape, dproj.dtype),
        grid=(T // tb,),
        in_specs=[ksp, ksp, ksp, ksp, vsp, vsp, _ANY_SPEC],
        out_specs=pl.BlockSpec((tb, 2 * wk + wv), lambda i: (i, DP_GLA_QKV_BLK)),
        input_output_aliases={6: 0},
        compiler_params=_cparams(("parallel",)),
        name="gla_dsum",
    )(dq_f, dq_b, dk_f, dk_b, dv_f, dv_b, dproj)


def _local_step(x, target, wcat, conv_w, avec, dvec, gdn_norm_w, w2f, w2b, b2f, b2b, gla_norm_w, proj_weights, w_pre, w_post,
                send_proj_grads=None, send_in_grads=None):
    h, h_t = _rms_pre(x, w_pre)
    proj = _matmul(h, wcat, "nt", f32, "proj_in", 2048, 512, 1024)
    qkvn = _gdn_prep(proj, conv_w)
    gates, gc = _gdn_gates(proj, avec, dvec)
    oa_f, oa_b, sa_f, sa_b, inv_f, inv_b = _gdn_fwd(qkvn, gates, gc)
    og = _out_norm(oa_f, oa_b, proj, Z_BLK, gdn_norm_w, GDN_DIM, "gdn_out")
    G_f, G_b = _gla_prep(proj, w2f, w2b, b2f, b2b)
    ob_f, ob_b, sb_f, sb_b = _gla_fwd(proj, G_f, G_b)
    obg = _out_norm(ob_f, ob_b, proj, G_B_BLK, gla_norm_w, GLA_DV, "gla_out")
    wpg, wpl, wout = proj_weights(obg)
    (loss_row, d_wpost, dy, m, dout, dya, dyb, dproj, dog, dobg) = _mid(og, obg, proj, x, target, wpg, wpl, wout, w_post)

    d_wout = _matmul(m, dout, "tn", MXU_DTYPE, "dw_out", 1024, 512, 2048)
    d_wpg = _matmul(og, dya, "tn", MXU_DTYPE, "dw_proj_gdn", 1024, 512, 2048)
    d_wpl = _matmul(obg, dyb, "tn", MXU_DTYPE, "dw_proj_gla", 1024, 512, 2048)
    if send_proj_grads is not None:
        gla_norm_w = gla_norm_w + send_proj_grads(d_wpg, d_wpl, d_wout)

    do_b, dproj, d_gla_norm = _out_norm_bwd(dobg, ob_f, ob_b, proj, G_B_BLK, gla_norm_w, GLA_DV, "gla_out_bwd", dproj)
    dq_f, dk_f, dv_f, dgk_f, dq_b, dk_b, dv_b, dgk_b = _gla_bwd(proj, G_f, G_b, sb_f, sb_b, do_b)
    dproj = _gla_dsum(dq_f, dq_b, dk_f, dk_b, dv_f, dv_b, dproj)
    dsmall_gla, d_w2f, d_w2b, d_b2f, d_b2b = _gla_prep_bwd(proj, w2f, w2b, b2f, b2b, dgk_f, dgk_b)

    do_a, dproj, d_gdn_norm = _out_norm_bwd(dog, oa_f, oa_b, proj, Z_BLK, gdn_norm_w, GDN_DIM, "gdn_out_bwd", dproj)
    dqkv_f, dqkv_b, dg_f, dg_b = _gdn_bwd(qkvn, gates, gc, sa_f, sa_b, inv_f, inv_b, do_a)
    dproj, d_conv = _gdn_prep_bwd(proj, conv_w, dqkv_f, dqkv_b, dproj)
    dproj, d_alog, d_dtb = _gdn_gates_bwd(proj, avec, dvec, dg_f, dg_b, dsmall_gla, dproj)

    d_wcat = _matmul(h_t, dproj, "nn", MXU_DTYPE, "dw_in", 1024, 512, 2048, out_t=True)
    started = None if send_in_grads is None else send_in_grads(d_wcat)
    dh = _matmul(dproj, wcat, "nn", f32, "dh", 1024, 1024, 2432, after=started)
    grad_x, d_wpre = _rms_pre_bwd(x, w_pre, dh, dy)
    return dict(loss_row=loss_row, grad_x=grad_x, d_wcat=d_wcat, d_conv=d_conv, d_alog=d_alog, d_dtb=d_dtb,
                d_gdn_norm=d_gdn_norm, d_wpg=d_wpg, d_w2f=d_w2f, d_w2b=d_w2b, d_b2f=d_b2f, d_b2b=d_b2b,
                d_gla_norm=d_gla_norm, d_wpl=d_wpl, d_wout=d_wout, d_wpost=d_wpost, d_wpre=d_wpre)


def _to_cat(wf):
    pad = jnp.zeros((N_CAT - N_IN,) + wf.shape[1:], wf.dtype)
    return jnp.concatenate([wf[:4096], wf[4128:6176], wf[7232:9280], wf[6176:7200], wf[4096:4128], wf[7200:7232], pad], axis=0)


def _from_cat(wc):
    return jnp.concatenate([wc[:4096], wc[9216:9248], wc[4096:6144], wc[8192:9216], wc[9248:9280], wc[6144:8192]], axis=0)


def _lane_row(*pieces):
    row = jnp.concatenate(pieces, axis=1)
    return jnp.pad(row, ((0, 0), (0, LANES - row.shape[1])))


def _pad_w2(w2, row0):
    return jnp.pad(w2, ((row0, LANES - row0 - w2.shape[0]), (0, 0)))


SM_ROWS, SM_COLS = 32, 384
REP_ROWS = 32


def _pack_small_shard(conv, w2f, w2b):
    lead = conv.shape[:-2]
    z = lambda r, c: jnp.zeros(lead + (r, c), f32)
    top = jnp.concatenate([conv, z(8 - CONV_K, SM_COLS)], axis=-2)
    mid = jnp.concatenate([w2f, w2b, z(16, SM_COLS - 128)], axis=-1)
    return jnp.concatenate([top, mid, z(SM_ROWS - 24, SM_COLS)], axis=-2)


LOSS_ROW = 28


def _pack_rep(ln_pre, ln_post, b2f, b2b, gla_norm, gdn_norm, alog_f, alog_b, dt_f, dt_b, loss=None):
    last = _lane_row(alog_f, alog_b, dt_f, dt_b)
    extra = jnp.zeros((1, LANES), f32) if loss is None else _lane_row(loss)
    rows = [ln_pre.reshape(8, LANES), ln_post.reshape(8, LANES), b2f.reshape(4, LANES), b2b.reshape(4, LANES),
            gla_norm.reshape(2, LANES), gdn_norm.reshape(1, LANES), last, extra, jnp.zeros((REP_ROWS - LOSS_ROW - 1, LANES), f32)]
    return jnp.concatenate(rows, axis=0)


def _unpack_rep(p):
    return dict(ln_pre_w=p[0:8].reshape(1, 1024), ln_post_w=p[8:16].reshape(1, 1024), gk_b2_fwd=p[16:20].reshape(1, 512),
                gk_b2_bwd=p[20:24].reshape(1, 512), gla_norm_w=p[24:26].reshape(1, 256), gdn_norm_w=p[26:27],
                a_log_fwd=p[27:28, 0:8], a_log_bwd=p[27:28, 8:16], dt_bias_fwd=p[27:28, 16:24], dt_bias_bwd=p[27:28, 24:32])


_MESH = pl.DeviceIdType.MESH


N_BIG_COPIES, N_SMALL_COPIES = 9, 7


def _gather_weights(w_in_s, sm_s):
    shapes = (w_in_s.shape, sm_s.shape)
    dtypes = (MXU_DTYPE, f32)

    def body(win_ref, sm_ref, gin_ref, gsm_ref, send_sems, recv_sems):
        x, y, c = lax.axis_index("x"), lax.axis_index("y"), lax.axis_index("c")
        me, sibling = (x, y, c), (x, y, 1 - c)
        chips = [(1 - x, y), (x, 1 - y), (1 - x, 1 - y)]
        bufs = (gin_ref, gsm_ref)

        def idx(px, py, pc):
            return 4 * px + 2 * py + pc

        for r in range(w_in_s.shape[1] // LANES):
            cols = slice(r * LANES, (r + 1) * LANES)
            gin_ref[idx(*me), :, cols] = win_ref[:, cols].astype(MXU_DTYPE)
        gsm_ref[idx(*me)] = sm_ref[...]

        def copy(sem, blk, to):
            return pltpu.make_async_remote_copy(src_ref=blk, dst_ref=blk, send_sem=send_sems.at[sem], recv_sem=recv_sems.at[sem],
                                                device_id=to, device_id_type=_MESH)

        def small(k, block, to):
            return copy(N_BIG_COPIES + k, gsm_ref.at[idx(*block)], to)

        half = w_in_s.shape[1] // 2
        xn, yn, dg = (1 - x, y), (x, 1 - y), (1 - x, 1 - y)

        def big(sem, block, to, part=None):
            blk = gin_ref.at[idx(*block)]
            return copy(sem, blk if part is None else blk.at[:, pl.ds(part * half, half)], to)

        started = [big(0, me, sibling), big(1, me, (*xn, c)), big(2, me, (*yn, c)), small(0, me, sibling)]
        started += [small(1 + j, me, (*chip, c)) for j, chip in enumerate(chips)]
        for cp in started:
            cp.start()

        def then(arrived, forwards):
            arrived.wait_recv()
            for cp in forwards:
                cp.start()
            started.extend(forwards)

        then(big(1, (*xn, c), me), [big(3, (*xn, c), (*yn, c), 1), big(5, (*xn, c), sibling)])
        then(big(2, (*yn, c), me), [big(4, (*yn, c), (*xn, c), 0), big(6, (*yn, c), sibling)])
        then(big(4, (*dg, c), me, 0), [big(7, (*dg, c), sibling, 0)])
        then(big(3, (*dg, c), me, 1), [big(8, (*dg, c), sibling, 1)])
        for j, chip in enumerate(chips):
            then(small(1 + j, (*chip, c), me), [small(4 + j, (*chip, c), sibling)])
        big(0, sibling, me).wait_recv()
        big(5, (*xn, 1 - c), me).wait_recv()
        big(6, (*yn, 1 - c), me).wait_recv()
        big(7, (*dg, 1 - c), me, 0).wait_recv()
        big(8, (*dg, 1 - c), me, 1).wait_recv()
        small(0, sibling, me).wait_recv()
        for j, chip in enumerate(chips):
            small(4 + j, (*chip, 1 - c), me).wait_recv()
        for cp in started:
            cp.wait_send()

    vm = pl.BlockSpec(memory_space=pltpu.VMEM)
    return pl.pallas_call(
        body,
        out_shape=tuple(jax.ShapeDtypeStruct((N_DEV,) + s, d) for s, d in zip(shapes, dtypes)),
        in_specs=[vm, vm],
        out_specs=(vm, vm),
        scratch_shapes=[pltpu.SemaphoreType.DMA((N_BIG_COPIES + N_SMALL_COPIES,)), pltpu.SemaphoreType.DMA((N_BIG_COPIES + N_SMALL_COPIES,))],
        compiler_params=pltpu.CompilerParams(vmem_limit_bytes=VMEM_LIMIT),
        name="gather_weights",
    )(w_in_s, sm_s)


N_CHIP = 4
_EFFECT = pltpu.SideEffectType.DATAFLOW_SIDE_EFFECTING
_HBM_SPEC = pl.BlockSpec(memory_space=pltpu.HBM)
_SEM_SPEC = pl.BlockSpec(memory_space=pltpu.SEMAPHORE)


def _relation_peer(k, x, y, c):
    px = (1 - x) if (k & 4) else x
    py = (1 - y) if (k & 2) else y
    pc = (1 - c) if (k & 1) else c
    return (px, py, pc), 4 * px + 2 * py + pc


_PLAN_COPIES = {"gather": N_DEV - 1, "blocks": N_DEV - 1, "sibling": 4, "chips": 3}
_PLAN_SLOTS = {"gather": N_DEV, "blocks": N_DEV, "sibling": 4, "chips": 4}


def _plan_copies(plan, src_ref, land_ref, send_sems, recv_sems, arrival):
    x, y, c = lax.axis_index("x"), lax.axis_index("y"), lax.axis_index("c")
    me, my_chip = 4 * x + 2 * y + c, 2 * x + y
    out = []

    def add(i, src, there, here, dev):
        out.append(pltpu.make_async_remote_copy(src_ref=src, dst_ref=land_ref.at[here if arrival else there], send_sem=send_sems.at[i],
                                                recv_sem=recv_sems.at[i], device_id=dev, device_id_type=_MESH))

    if plan in ("gather", "blocks"):
        for k in range(1, N_DEV):
            dev, p = _relation_peer(k, x, y, c)
            add(k - 1, src_ref.at[p] if plan == "blocks" else src_ref, me, p, dev)
    elif plan == "sibling":
        for j in range(4):
            add(j, src_ref.at[2 * j + 1 - c], j, j, (x, y, 1 - c))
    else:
        for i, k in enumerate((2, 4, 6)):
            (px, py, pc), _ = _relation_peer(k, x, y, c)
            add(i, src_ref.at[2 * px + py], my_chip, 2 * px + py, (px, py, pc))
    return out


def _exchange_start(src, plan, name, after=None):
    n = _PLAN_COPIES[plan]
    land = lax.empty((_PLAN_SLOTS[plan],) + (src.shape if plan == "gather" else src.shape[1:]), src.dtype)
    extra = [] if after is None else [after]

    def body(src_ref, land_ref, *rest):
        send_sems, recv_sems, _, _, token = rest[len(extra) :]
        for cp in _plan_copies(plan, src_ref, land_ref, send_sems, recv_sems, False):
            cp.start()
        token[...] = jnp.zeros_like(token)

    return pl.pallas_call(
        body,
        name=name,
        out_shape=(pltpu.SemaphoreType.DMA((n,)), pltpu.SemaphoreType.DMA((n,)), pltpu.HBM(src.shape, src.dtype),
                   pltpu.HBM(land.shape, land.dtype), jax.ShapeDtypeStruct((8, LANES), f32)),
        in_specs=(_HBM_SPEC, _HBM_SPEC) + (_ANY_SPEC,) * len(extra),
        out_specs=(_SEM_SPEC, _SEM_SPEC, _HBM_SPEC, _HBM_SPEC, pl.BlockSpec(memory_space=pltpu.VMEM)),
        input_output_aliases={0: 2, 1: 3},
        compiler_params=pltpu.CompilerParams(has_side_effects=_EFFECT),
    )(pltpu.with_memory_space_constraint(src, pltpu.HBM), pltpu.with_memory_space_constraint(land, pltpu.HBM), *extra)


def _exchange_wait(started, after, plan, name):
    send_sems, recv_sems, src_thru, land_thru, _ = started

    def body(src_ref, land_ref, send_sems, recv_sems, after_ref, src_dead, got_ref):
        for cp in _plan_copies(plan, src_ref, land_ref, send_sems, recv_sems, True):
            cp.wait_send()
            cp.wait_recv()

    return pl.pallas_call(
        body,
        name=name,
        out_shape=(pltpu.HBM(src_thru.shape, src_thru.dtype), pltpu.HBM(land_thru.shape, land_thru.dtype)),
        in_specs=(_HBM_SPEC, _HBM_SPEC, _SEM_SPEC, _SEM_SPEC, pl.BlockSpec(memory_space=pl.ANY)),
        out_specs=(_HBM_SPEC, _HBM_SPEC),
        input_output_aliases={0: 0, 1: 1},
        compiler_params=pltpu.CompilerParams(has_side_effects=_EFFECT),
    )(src_thru, land_thru, send_sems, recv_sems, after)


def _pair_sum(g, r, name, tc=LANES):
    _, R, Cc = g.shape

    def body(g_ref, r_ref, o_ref):
        o_ref[...] = (g_ref[...].astype(f32) + r_ref[...].astype(f32)).astype(o_ref.dtype)

    return pl.pallas_call(
        body,
        out_shape=jax.ShapeDtypeStruct(r.shape, r.dtype),
        grid=(N_CHIP, Cc // tc),
        in_specs=[pl.BlockSpec((1, R, tc), lambda k, i: (2 * k + lax.axis_index("c"), 0, i)), pl.BlockSpec((1, R, tc), lambda k, i: (k, 0, i))],
        out_specs=pl.BlockSpec((1, R, tc), lambda k, i: (k, 0, i)),
        compiler_params=_cparams(("parallel", "parallel")),
        name=name,
    )(g, r)


def _exchange_small(g_sm, g_rep):
    srcs = (g_sm, g_rep)

    def body(in0, in1, out0, out1, send_sems, recv_sems, local_sems):
        x, y, c = lax.axis_index("x"), lax.axis_index("y"), lax.axis_index("c")
        me = 4 * x + 2 * y + c
        ins, outs = (in0, in1), (out0, out1)

        def copy(a, k, arrival):
            dev, p = _relation_peer(k, x, y, c)
            return pltpu.make_async_remote_copy(src_ref=ins[a].at[p] if a == 0 else ins[a], dst_ref=outs[a].at[p if arrival else me],
                                                send_sem=send_sems.at[7 * a + k - 1], recv_sem=recv_sems.at[7 * a + k - 1],
                                                device_id=dev, device_id_type=_MESH)

        pairs = [(a, k) for a in (0, 1) for k in range(1, N_DEV)]
        sends = [copy(a, k, False) for a, k in pairs]
        for cp in sends:
            cp.start()
        local = [pltpu.make_async_copy(ins[0].at[me], outs[0].at[me], local_sems.at[0]),
                 pltpu.make_async_copy(ins[1], outs[1].at[me], local_sems.at[1])]
        for cp in local:
            cp.start()
        for a, k in pairs:
            copy(a, k, True).wait_recv()
        for cp in sends:
            cp.wait_send()
        for cp in local:
            cp.wait()

    return pl.pallas_call(
        body,
        out_shape=(jax.ShapeDtypeStruct(g_sm.shape, g_sm.dtype), jax.ShapeDtypeStruct((N_DEV,) + g_rep.shape, g_rep.dtype)),
        in_specs=[_HBM_SPEC] * 2,
        out_specs=(_HBM_SPEC,) * 2,
        scratch_shapes=[pltpu.SemaphoreType.DMA((14,)), pltpu.SemaphoreType.DMA((14,)), pltpu.SemaphoreType.DMA((2,))],
        name="exchange_small",
    )(*srcs)


def _sum_adam(parts, w, m, v, name, tb, tc=None, own=None, own_slot=None):
    R, Cc = w.shape
    n_parts = parts.shape[0]
    tb = R if tc else min(tb, R)
    tc = tc or Cc
    assert R % tb == 0 and Cc % tc == 0
    bc1 = 1.0 - ADAM_B1**ADAM_STEP
    bc2 = 1.0 - ADAM_B2**ADAM_STEP
    extra = [] if own is None else [own]

    def body(p_ref, w_ref, m_ref, v_ref, *rest):
        g_ref, d_ref, nm_ref, nv_ref = rest[len(extra) :]

        def part(j):
            pj = p_ref[j].astype(f32)
            return pj if own is None else jnp.where(own_slot() == j, rest[0][0].astype(f32), pj)

        g = part(0)
        for j in range(1, n_parts):
            g = g + part(j)
        g_ref[...] = g
        m2 = ADAM_B1 * m_ref[...] + (1.0 - ADAM_B1) * g
        v2 = ADAM_B2 * v_ref[...] + (1.0 - ADAM_B2) * (g * g)
        nm_ref[...] = m2
        nv_ref[...] = v2
        d_ref[...] = -ADAM_LR * ((m2 / bc1) / (jnp.sqrt(v2 / bc2) + ADAM_EPS) + ADAM_WD * w_ref[...])

    blk = pl.BlockSpec((tb, tc), lambda i, j: (i, j))
    o = jax.ShapeDtypeStruct((R, Cc), f32)
    return pl.pallas_call(
        body,
        out_shape=(o, o, o, o),
        grid=(R // tb, Cc // tc),
        in_specs=[pl.BlockSpec((n_parts, tb, tc), lambda i, j: (0, i, j)), blk, blk, blk]
        + [pl.BlockSpec((1, tb, tc), lambda i, j: (own_slot(), i, j))] * len(extra),
        out_specs=(blk, blk, blk, blk),
        compiler_params=_cparams(("parallel", "parallel")),
        name=name,
    )(parts, w, m, v, *extra)


def _my_device():
    return 4 * lax.axis_index("x") + 2 * lax.axis_index("y") + lax.axis_index("c")


def _my_chip():
    return 2 * lax.axis_index("x") + lax.axis_index("y")


_WEIGHTS = ["ln_pre_w", "w_in", "conv_w", "a_log_fwd", "a_log_bwd", "dt_bias_fwd", "dt_bias_bwd", "gdn_norm_w", "w_proj_gdn",
            "gk_w2_fwd", "gk_b2_fwd", "gk_w2_bwd", "gk_b2_bwd", "gla_norm_w", "w_proj_gla", "w_out", "ln_post_w"]
_REP_ORDER = ["ln_pre_w", "ln_post_w", "gk_b2_fwd", "gk_b2_bwd", "gla_norm_w", "gdn_norm_w", "a_log_fwd", "a_log_bwd", "dt_bias_fwd", "dt_bias_bwd"]


def kernel(x, ln_pre_w, w_in, conv_w, a_log_fwd, a_log_bwd, dt_bias_fwd, dt_bias_bwd, gdn_norm_w, w_proj_gdn, gk_w2_fwd, gk_b2_fwd, gk_w2_bwd, gk_b2_bwd, gla_norm_w, w_proj_gla, w_out, ln_post_w, loss_target, m_ln_pre_w, m_w_in, m_conv_w, m_a_log_fwd, m_a_log_bwd, m_dt_bias_fwd, m_dt_bias_bwd, m_gdn_norm_w, m_w_proj_gdn, m_gk_w2_fwd, m_gk_b2_fwd, m_gk_w2_bwd, m_gk_b2_bwd, m_gla_norm_w, m_w_proj_gla, m_w_out, m_ln_post_w, v_ln_pre_w, v_w_in, v_conv_w, v_a_log_fwd, v_a_log_bwd, v_dt_bias_fwd, v_dt_bias_bwd, v_gdn_norm_w, v_w_proj_gdn, v_gk_w2_fwd, v_gk_b2_fwd, v_gk_w2_bwd, v_gk_b2_bwd, v_gla_norm_w, v_w_proj_gla, v_w_out, v_ln_post_w):
    args = locals()
    W = {n: args[n] for n in _WEIGHTS}
    M = {n: args["m_" + n] for n in _WEIGHTS}
    V = {n: args["v_" + n] for n in _WEIGHTS}

    wp_stack = lambda P: jnp.concatenate([P["w_proj_gdn"], P["w_proj_gla"], P["w_out"]], axis=0)
    sm_pack = lambda P: _pack_small_shard(P["conv_w"][0], P["gk_w2_fwd"][0], P["gk_w2_bwd"][0])
    me = 4 * lax.axis_index("x") + 2 * lax.axis_index("y") + lax.axis_index("c")
    w_in_t, m_in_t, v_in_t = (jnp.transpose(a[0]) for a in (w_in, m_w_in, v_w_in))
    g_in, g_sm = _gather_weights(w_in_t, sm_pack(W))
    wp_own = wp_stack(W).astype(MXU_DTYPE)
    wp_started = _exchange_start(wp_own, "gather", "gather_proj_start", after=g_sm)

    def proj_weights(after):
        own, land = _exchange_wait(wp_started, after, "gather", "gather_proj_wait")
        g_p = lax.dynamic_update_slice(land, own[None], (me, 0, 0, 0))
        return tuple(g_p[:, i].reshape(D_MODEL, D_MODEL) for i in range(3))

    rows_p = 3 * D_MODEL // N_DEV
    sent = {}

    def send_proj_grads(d_wpg, d_wpl, d_wout):
        p_p = jnp.stack([d_wpg, d_wpl, d_wout]).reshape(3, N_DEV, D_MODEL // N_DEV, D_MODEL).transpose(1, 0, 2, 3)
        sent["p_p"] = p_p.reshape(N_DEV, rows_p, D_MODEL)
        sent["started"] = _exchange_start(sent["p_p"], "blocks", "exchange_proj_start")
        return sent["started"][4][0:1, 0:1]

    def send_in_grads(d_wcat):
        p_in = _from_cat(d_wcat).reshape(N_DEV, SHARD_IN, D_MODEL)
        sib = _exchange_start(p_in, "sibling", "exchange_in_sibling_start")
        p_p, land_p = _exchange_wait(sent["started"], sib[4], "blocks", "exchange_proj_wait")
        sent["o_p"] = _sum_adam(land_p, wp_stack(W).reshape(rows_p, D_MODEL), wp_stack(M).reshape(rows_p, D_MODEL),
                                wp_stack(V).reshape(rows_p, D_MODEL), "adam_w_proj", 128, own=p_p, own_slot=_my_device)
        p_in, sib_land = _exchange_wait(sib, sent["o_p"][0], "sibling", "exchange_in_sibling_wait")
        sent["chips"] = _exchange_start(_pair_sum(p_in, sib_land, "pair_sum_w_in", tc=512), "chips", "exchange_in_chips_start")
        return sent["chips"][4]

    wcat = _to_cat(g_in.reshape(N_IN, D_MODEL))
    conv_full = g_sm[:, 0:CONV_K, :].transpose(1, 0, 2).reshape(CONV_K, N_DEV * SM_COLS)
    w2f_full = g_sm[:, 8:24, 0:64].transpose(1, 0, 2).reshape(16, 512)
    w2b_full = g_sm[:, 8:24, 64:128].transpose(1, 0, 2).reshape(16, 512)

    avec = _lane_row(a_log_fwd, a_log_bwd)
    dvec = _lane_row(dt_bias_fwd, dt_bias_bwd)
    G = _local_step(x[0], loss_target[0], wcat, conv_full, avec, dvec, gdn_norm_w, _pad_w2(w2f_full, 32), _pad_w2(w2b_full, 48),
                    gk_b2_fwd, gk_b2_bwd, gla_norm_w, proj_weights, ln_pre_w + wp_started[4][0:1, 0:1], ln_post_w, send_proj_grads,
                    send_in_grads)

    p_sm = _pack_small_shard(G["d_conv"].reshape(CONV_K, N_DEV, SM_COLS).transpose(1, 0, 2),
                             G["d_w2f"][32:48].reshape(16, N_DEV, 64).transpose(1, 0, 2),
                             G["d_w2b"][48:64].reshape(16, N_DEV, 64).transpose(1, 0, 2))
    p_rep = _pack_rep(G["d_wpre"], G["d_wpost"], G["d_b2f"], G["d_b2b"], G["d_gla_norm"], G["d_gdn_norm"],
                      G["d_alog"][:, 0:8], G["d_alog"][:, 8:16], G["d_dtb"][:, 0:8], G["d_dtb"][:, 8:16],
                      loss=0.5 * jnp.sum(G["loss_row"], axis=1, keepdims=True) / D_MODEL)
    p_small = jnp.concatenate([p_sm, jnp.broadcast_to(p_rep[None], (N_DEV,) + p_rep.shape)], axis=2)
    small_started = _exchange_start(p_small, "blocks", "exchange_small_start")
    q_in, land_in = _exchange_wait(sent["chips"], small_started[4], "chips", "exchange_in_chips_wait")
    o_in = _sum_adam(land_in, w_in_t, m_in_t, v_in_t, "adam_w_in", SHARD_IN, tc=256, own=q_in, own_slot=_my_chip)
    o_p = sent["o_p"]
    p_small, land_small = _exchange_wait(small_started, o_in[0], "blocks", "exchange_small_wait")
    rep_pack = lambda P: _pack_rep(*[P[n] for n in _REP_ORDER])
    small_pack = lambda P: jnp.concatenate([sm_pack(P), rep_pack(P)], axis=1)
    o_small = _sum_adam(land_small, small_pack(W), small_pack(M), small_pack(V), "adam_small", SM_ROWS, own=p_small, own_slot=_my_device)
    o_sm, o_rep = [o[:, :SM_COLS] for o in o_small], [o[:, SM_COLS:] for o in o_small]

    res = []
    for kind in range(4):
        rep = _unpack_rep(o_rep[kind])
        per = dict(rep)
        per["w_in"] = jnp.transpose(o_in[kind])[None]
        pp = o_p[kind].reshape(3, 1, D_MODEL // N_DEV, D_MODEL)
        per["w_proj_gdn"], per["w_proj_gla"], per["w_out"] = pp[0], pp[1], pp[2]
        per["conv_w"] = o_sm[kind][None, 0:CONV_K, :]
        per["gk_w2_fwd"] = o_sm[kind][None, 8:24, 0:64]
        per["gk_w2_bwd"] = o_sm[kind][None, 8:24, 64:128]
        res.append([per[n] for n in _WEIGHTS])
    loss = o_rep[0][LOSS_ROW, 0]
    return (loss, G["grad_x"][None], *res[0], *res[1], *res[2], *res[3])
```

```python
import jax
import jax.numpy as jnp
from jax import lax
from jax.experimental import pallas as pl
from jax.experimental.pallas import tpu as pltpu

f32 = jnp.float32
MXU_DTYPE = jnp.bfloat16

D_MODEL = 1024
CHUNK = 64
NORM_EPS = 1e-6
GDN_HEADS, GDN_DIM = 8, 128
GLA_HEADS, GLA_DK, GLA_DV = 4, 128, 256
GATE_NORMALIZER = 16.0
CONV_K = 5
N_IN = 9280
N_DEV = 8
SHARD_IN = N_IN // N_DEV

N_CAT = 9728
SMALL_OFF = 9216
LANES = 128

ADAM_LR, ADAM_B1, ADAM_B2, ADAM_EPS, ADAM_WD, ADAM_STEP = 0.001, 0.9, 0.999, 1e-08, 0.01, 10

VMEM_LIMIT = 56 * 1024 * 1024


def _cparams(sem=None):
    return pltpu.CompilerParams(dimension_semantics=sem, vmem_limit_bytes=VMEM_LIMIT)


_DN = {"nn": (((1,), (0,)), ((), ())), "nt": (((1,), (1,)), ((), ())), "tn": (((0,), (0,)), ((), ()))}
_BDN = {"nn": (((2,), (1,)), ((0,), (0,))), "nt": (((2,), (2,)), ((0,), (0,))), "tn": (((1,), (1,)), ((0,), (0,)))}


def _dot16(a, b, kind):
    return lax.dot_general(a, b, (_DN if a.ndim == 2 else _BDN)[kind], preferred_element_type=f32)


def _mx(a, b, kind):
    return _dot16(a.astype(MXU_DTYPE), b.astype(MXU_DTYPE), kind)


def _split2(a):
    hi = a.astype(jnp.bfloat16)
    return hi, (a - hi.astype(f32)).astype(jnp.bfloat16)


def _split3(a):
    p1 = a.astype(jnp.bfloat16)
    r = a - p1.astype(f32)
    p2 = r.astype(jnp.bfloat16)
    return p1, p2, (r - p2.astype(f32)).astype(jnp.bfloat16)


def _h3s(a2, b2, kind):
    (ah, al), (bh, bl) = a2, b2
    return _dot16(ah, bh, kind) + (_dot16(ah, bl, kind) + _dot16(al, bh, kind))


def _h3(a, b, kind):
    return _h3s(_split2(a), _split2(b), kind)


def _exact01(t01, x, kind="nn"):
    t = t01.astype(jnp.bfloat16)
    x1, x2, x3 = _split3(x)
    return _dot16(t, x1, kind) + (_dot16(t, x2, kind) + _dot16(t, x3, kind))


def _colsum_col(e):
    ones = jnp.ones(e.shape[:-1] + (LANES,), jnp.bfloat16)
    e1, e2, e3 = _split3(e)
    return (_dot16(e1, ones, "tn") + (_dot16(e2, ones, "tn") + _dot16(e3, ones, "tn")))[..., 0:1]


def _iota3(shape, axis):
    return lax.broadcasted_iota(jnp.int32, shape, axis)


def _dir_masks(B, nf, Cn):
    shp = (B, Cn, Cn)
    bb, ii, jj = _iota3(shp, 0), _iota3(shp, 1), _iota3(shp, 2)
    fwd = bb < nf
    rev = jnp.logical_not(fwd)
    incl = jnp.logical_or(jnp.logical_and(fwd, ii >= jj), jnp.logical_and(rev, ii <= jj))
    strict = jnp.logical_and(incl, ii != jj)
    return incl, strict, ii == jj


def _last_row(x, nf):
    B, Cn = x.shape[0], x.shape[1]
    fwd = _iota3((B, 1, 1), 0) < nf
    return jnp.where(fwd, x[:, Cn - 1 : Cn, :], x[:, 0:1, :])


def _add_at_last_row(x, val, nf):
    B, Cn = x.shape[0], x.shape[1]
    bb, rr = _iota3((B, Cn, 1), 0), _iota3((B, Cn, 1), 1)
    at = jnp.logical_or(jnp.logical_and(bb < nf, rr == Cn - 1), jnp.logical_and(bb >= nf, rr == 0))
    return x + jnp.where(at, val, 0.0)


def _sigmoid(x):
    return jax.nn.sigmoid(x)


def _silu(x):
    return x * _sigmoid(x)


def _silu_grad(x):
    s = _sigmoid(x)
    return s * (1.0 + x * (1.0 - s))


def _softplus(x):
    u = jnp.exp(-jnp.abs(x))
    l1p = jnp.where(u < 1e-3, u * (1.0 - u * (0.5 - u * (1.0 / 3.0))), jnp.log(1.0 + u))
    return jnp.maximum(x, 0.0) + l1p


def _mxr(x):
    return x.astype(MXU_DTYPE).astype(f32)


def _iota2(shape, axis):
    return lax.broadcasted_iota(jnp.int32, shape, axis)


def _matmul(a, b, kind, out_dtype, name, tm, tn, tk, out_t=False, after=None):
    extra = [] if after is None else [after]
    if kind == "nn":
        (M, K), N = a.shape, b.shape[1]
    elif kind == "nt":
        (M, K), N = a.shape, b.shape[0]
    else:
        (K, M), N = a.shape, b.shape[1]
    tm, tn, tk = min(tm, M), min(tn, N), min(tk, K)
    assert M % tm == 0 and N % tn == 0 and K % tk == 0, (name, M, N, K)
    nk = K // tk
    if kind == "tn":
        a_spec = pl.BlockSpec((tk, tm), lambda i, j, k: (k, i))
    else:
        a_spec = pl.BlockSpec((tm, tk), lambda i, j, k: (i, k))
    if kind == "nt":
        b_spec = pl.BlockSpec((tn, tk), lambda i, j, k: (j, k))
    else:
        b_spec = pl.BlockSpec((tk, tn), lambda i, j, k: (k, j))

    def body(a_ref, b_ref, *rest):
        o_ref, acc = rest[len(extra)], rest[len(extra) + 1 :]
        p = _mx(a_ref[...], b_ref[...], kind)
        if nk == 1:
            o_ref[...] = (p.T if out_t else p).astype(out_dtype)
        else:
            assert not out_t
            acc_ref = acc[0]
            k = pl.program_id(2)

            @pl.when(k == 0)
            def _():
                acc_ref[...] = p

            @pl.when(k > 0)
            def _():
                acc_ref[...] += p

            @pl.when(k == nk - 1)
            def _():
                o_ref[...] = acc_ref[...].astype(out_dtype)

    return pl.pallas_call(
        body,
        out_shape=jax.ShapeDtypeStruct((N, M) if out_t else (M, N), out_dtype),
        grid=(M // tm, N // tn, nk),
        in_specs=[a_spec, b_spec] + [pl.BlockSpec((8, LANES), lambda i, j, k: (0, 0))] * len(extra),
        out_specs=pl.BlockSpec((tn, tm), lambda i, j, k: (j, i)) if out_t else pl.BlockSpec((tm, tn), lambda i, j, k: (i, j)),
        scratch_shapes=[] if nk == 1 else [pltpu.VMEM((tm, tn), f32)],
        compiler_params=_cparams(("parallel", "parallel", "arbitrary")),
        name=name,
    )(a, b, *extra)


ROW_TILE = 512


def _rms_pre(x, w):
    T = x.shape[0]
    tb = min(ROW_TILE, T)

    def body(x_ref, w_ref, h_ref, ht_ref):
        xv = x_ref[...]
        r = lax.rsqrt(jnp.mean(xv * xv, axis=1, keepdims=True) + NORM_EPS)
        h = xv * r * w_ref[...]
        h_ref[...] = h.astype(h_ref.dtype)
        ht_ref[...] = h.T.astype(ht_ref.dtype)

    return pl.pallas_call(
        body,
        out_shape=(jax.ShapeDtypeStruct((T, D_MODEL), MXU_DTYPE), jax.ShapeDtypeStruct((D_MODEL, T), MXU_DTYPE)),
        grid=(T // tb,),
        in_specs=[pl.BlockSpec((tb, D_MODEL), lambda i: (i, 0)), pl.BlockSpec((1, D_MODEL), lambda i: (0, 0))],
        out_specs=(pl.BlockSpec((tb, D_MODEL), lambda i: (i, 0)), pl.BlockSpec((D_MODEL, tb), lambda i: (0, i))),
        compiler_params=_cparams(("parallel",)),
        name="rms_pre",
    )(x, w)


def _rms_pre_bwd(x, w, dh, dy):
    T = x.shape[0]
    tb = min(ROW_TILE, T)

    def body(x_ref, w_ref, dh_ref, dy_ref, dx_ref, dw_ref):
        i = pl.program_id(0)
        xv, dhv = x_ref[...], dh_ref[...]
        r = lax.rsqrt(jnp.mean(xv * xv, axis=1, keepdims=True) + NORM_EPS)
        dhw = dhv * w_ref[...]
        dx_ref[...] = dy_ref[...] + r * dhw - xv * (r * r * r) * jnp.mean(dhw * xv, axis=1, keepdims=True)
        part = jnp.sum(dhv * xv * r, axis=0, keepdims=True)

        @pl.when(i == 0)
        def _():
            dw_ref[...] = part

        @pl.when(i > 0)
        def _():
            dw_ref[...] += part

    blk = pl.BlockSpec((tb, D_MODEL), lambda i: (i, 0))
    row = pl.BlockSpec((1, D_MODEL), lambda i: (0, 0))
    return pl.pallas_call(
        body,
        out_shape=(jax.ShapeDtypeStruct((T, D_MODEL), f32), jax.ShapeDtypeStruct((1, D_MODEL), f32)),
        grid=(T // tb,),
        in_specs=[blk, row, blk, blk],
        out_specs=(blk, row),
        compiler_params=_cparams(("arbitrary",)),
        name="rms_pre_bwd",
    )(x, w, dh, dy)


SHIFTS = (-2, -1, 1, 2)


def _shifted(x):
    T = x.shape[0]
    t = _iota2(x.shape, 0)
    out = {}
    for d in SHIFTS:
        valid = (t < T - d) if d > 0 else (t >= -d)
        out[d] = jnp.where(valid, pltpu.roll(x, (-d) % T, 0), 0.0)
    return out


def _conv5(u, ush, cw):
    acc = u * cw[2:3, :]
    for j in (0, 1, 3, 4):
        acc = acc + ush[j - 2] * cw[j : j + 1, :]
    return acc


def _gdn_prep(proj, conv_w):
    T = proj.shape[0]
    nh = GDN_HEADS

    def body(u_ref, cw_ref, o_ref):
        j = pl.program_id(0)
        u = _mxr(u_ref[...])
        s = _silu(_conv5(u, _shifted(u), _mxr(cw_ref[...])))

        @pl.when(j < 2 * nh)
        def _():
            o_ref[0] = s * lax.rsqrt(jnp.sum(s * s, axis=1, keepdims=True) + NORM_EPS)

        @pl.when(j >= 2 * nh)
        def _():
            o_ref[0] = s

    return pl.pallas_call(
        body,
        out_shape=jax.ShapeDtypeStruct((3, T, GDN_HEADS * GDN_DIM), f32),
        grid=(3 * nh,),
        in_specs=[pl.BlockSpec((T, GDN_DIM), lambda j: (0, j)), pl.BlockSpec((CONV_K, GDN_DIM), lambda j: (0, j))],
        out_specs=pl.BlockSpec((1, T, GDN_DIM), lambda j: (j // nh, 0, j % nh)),
        compiler_params=_cparams(("parallel",)),
        name="gdn_prep",
    )(proj, conv_w)


def _gdn_prep_bwd(proj, conv_w, dqkv_f, dqkv_b, dproj):
    T = proj.shape[0]
    nh = GDN_HEADS


    def body(u_ref, cw_ref, df_ref, db_ref, _, du_ref, dcw_ref):
        j = pl.program_id(0)
        u, cw = _mxr(u_ref[...]), _mxr(cw_ref[...])
        ush = _shifted(u)
        c = _conv5(u, ush, cw)
        s = _silu(c)
        dn = df_ref[0] + db_ref[0]
        rinv = lax.rsqrt(jnp.sum(s * s, axis=1, keepdims=True) + NORM_EPS)
        ds_norm = rinv * dn - s * (rinv * rinv * rinv) * jnp.sum(dn * s, axis=1, keepdims=True)
        ds = jnp.where(j < 2 * nh, ds_norm, dn)
        dc = _mxr(ds * _silu_grad(c))
        dcsh = _shifted(dc)
        du = dc * cw[2:3, :]
        for jj in range(CONV_K):
            d = jj - 2
            dcw_ref[jj : jj + 1, :] = jnp.sum(dc * (u if d == 0 else ush[d]), axis=0, keepdims=True)
            if d != 0:
                du = du + dcsh[-d] * cw[jj : jj + 1, :]
        du_ref[...] = du.astype(du_ref.dtype)

    qspec = pl.BlockSpec((1, T, GDN_DIM), lambda j: (j // nh, 0, j % nh))
    return pl.pallas_call(
        body,
        out_shape=(jax.ShapeDtypeStruct(dproj.shape, dproj.dtype), jax.ShapeDtypeStruct((CONV_K, 3 * nh * GDN_DIM), f32)),
        grid=(3 * nh,),
        in_specs=[pl.BlockSpec((T, GDN_DIM), lambda j: (0, j)), pl.BlockSpec((CONV_K, GDN_DIM), lambda j: (0, j)), qspec, qspec, _ANY_SPEC],
        out_specs=(pl.BlockSpec((T, GDN_DIM), lambda j: (0, j)), pl.BlockSpec((CONV_K, GDN_DIM), lambda j: (0, j))),
        input_output_aliases={4: 0},
        compiler_params=_cparams(("parallel",)),
        name="gdn_prep_bwd",
    )(proj, conv_w, dqkv_f, dqkv_b, dproj)


def _chunk_tri(n, rev):
    i, j = _iota2((n, n), 0), _iota2((n, n), 1)
    same = jnp.right_shift(i, 6) == jnp.right_shift(j, 6)
    order = (j >= i) if rev else (j <= i)
    return jnp.where(jnp.logical_and(same, order), 1.0, 0.0).astype(f32)


def _gdn_gates(proj, avec, dvec, tb=256):
    T = proj.shape[0]

    def body(s_ref, a_ref, d_ref, g_ref, gc_ref):
        small = s_ref[...]
        lane = _iota2(small.shape, 1)
        lg = -jnp.exp(a_ref[...]) * _softplus(small + d_ref[...])
        beta = _sigmoid(small)
        g_ref[...] = jnp.where(lane < 16, lg, jnp.where(lane < 32, beta, 0.0))
        lgm = jnp.where(lane < 16, lg, 0.0)
        gcf = _exact01(_chunk_tri(tb, False), lgm)
        gcr = _exact01(_chunk_tri(tb, True), lgm)
        gc_ref[...] = jnp.where(lane < 8, gcf, gcr)

    blk = pl.BlockSpec((tb, LANES), lambda i: (i, 0))
    row = pl.BlockSpec((1, LANES), lambda i: (0, 0))
    return pl.pallas_call(
        body,
        out_shape=(jax.ShapeDtypeStruct((T, LANES), f32), jax.ShapeDtypeStruct((T, LANES), f32)),
        grid=(T // tb,),
        in_specs=[pl.BlockSpec((tb, LANES), lambda i: (i, SMALL_OFF // LANES)), row, row],
        out_specs=(blk, blk),
        compiler_params=_cparams(("parallel",)),
        name="gdn_gates",
    )(proj, avec, dvec)


def _gdn_gates_bwd(proj, avec, dvec, dg_f, dg_b, dsmall_gla, dproj):
    T = proj.shape[0]
    tb = min(ROW_TILE, T)
    pad = N_CAT - SMALL_OFF

    def body(s_ref, a_ref, d_ref, gf_ref, gb_ref, dl_ref, _, ds_ref, da_ref, dd_ref):
        i = pl.program_id(0)
        small = s_ref[...]
        lane = _iota2(small.shape, 1)
        dgate = gf_ref[...] + gb_ref[...]
        z = small + d_ref[...]
        nega = -jnp.exp(a_ref[...])
        dz = dgate * nega * _sigmoid(z)
        beta = _sigmoid(small)
        dsm = jnp.where(lane < 16, dz, jnp.where(lane < 32, dgate * beta * (1.0 - beta), 0.0))
        ds_ref[:, :LANES] = (dsm + dl_ref[...]).astype(ds_ref.dtype)
        ds_ref[:, LANES:] = jnp.zeros((tb, pad - LANES), ds_ref.dtype)
        lg = nega * _softplus(z)
        pa = jnp.sum(jnp.where(lane < 16, dgate * lg, 0.0), axis=0, keepdims=True)
        pd = jnp.sum(jnp.where(lane < 16, dz, 0.0), axis=0, keepdims=True)

        @pl.when(i == 0)
        def _():
            da_ref[...] = pa
            dd_ref[...] = pd

        @pl.when(i > 0)
        def _():
            da_ref[...] += pa
            dd_ref[...] += pd

    blk = pl.BlockSpec((tb, LANES), lambda i: (i, 0))
    row = pl.BlockSpec((1, LANES), lambda i: (0, 0))
    return pl.pallas_call(
        body,
        out_shape=(jax.ShapeDtypeStruct(dproj.shape, dproj.dtype), jax.ShapeDtypeStruct((1, LANES), f32), jax.ShapeDtypeStruct((1, LANES), f32)),
        grid=(T // tb,),
        in_specs=[pl.BlockSpec((tb, LANES), lambda i: (i, SMALL_OFF // LANES)), row, row, blk, blk, blk, _ANY_SPEC],
        out_specs=(pl.BlockSpec((tb, pad), lambda i: (i, DP_SMALL_BLK)), row, row),
        input_output_aliases={6: 0},
        compiler_params=_cparams(("arbitrary",)),
        name="gdn_gates_bwd",
    )(proj, avec, dvec, dg_f, dg_b, dsmall_gla, dproj)


GDN_HB = 8


def _inv_unit_lower(L, eye):
    A = eye - L
    P2 = _split2(L)
    for _ in range(3):
        P2 = _split2(_h3s(P2, P2, "nn"))
        A = A + _h3s(_split2(A), P2, "nn")
    Ph = P2[0]
    for _ in range(2):
        Ph = _dot16(Ph, Ph, "nn").astype(jnp.bfloat16)
        A = A + _dot16(A.astype(jnp.bfloat16), Ph, "nn")
    return A


def _lane_col(blk, idx):
    lane = _iota2(blk.shape, 1)
    return jnp.sum(jnp.where(lane == idx, blk, 0.0), axis=1, keepdims=True)


def _out_norm(o_f, o_b, proj, gate_blk, w, hd, name):
    T, W = o_f.shape
    tb = min(ROW_TILE, T)
    nh = W // hd

    def body(of_ref, ob_ref, z_ref, w_ref, y_ref):
        wv = w_ref[...]
        for i in range(nh):
            sl = slice(i * hd, (i + 1) * hd)
            o = of_ref[:, sl] + ob_ref[:, sl]
            r = lax.rsqrt(jnp.mean(o * o, axis=1, keepdims=True) + NORM_EPS)
            y_ref[:, sl] = (o * r * wv * _silu(z_ref[:, sl])).astype(y_ref.dtype)

    blk = pl.BlockSpec((tb, W), lambda i: (i, 0))
    return pl.pallas_call(
        body,
        out_shape=jax.ShapeDtypeStruct((T, W), MXU_DTYPE),
        grid=(T // tb,),
        in_specs=[blk, blk, pl.BlockSpec((tb, W), lambda i: (i, gate_blk)), pl.BlockSpec((1, hd), lambda i: (0, 0))],
        out_specs=blk,
        compiler_params=_cparams(("parallel",)),
        name=name,
    )(o_f, o_b, proj, w)


def _out_norm_bwd(dy, o_f, o_b, proj, gate_blk, w, hd, name, dproj):
    T, W = o_f.shape
    tb = min(ROW_TILE, T)
    nh = W // hd

    def body(dy_ref, of_ref, ob_ref, z_ref, w_ref, _, do_ref, dz_ref, dw_ref):
        i = pl.program_id(0)
        wv = w_ref[...]
        dw = jnp.zeros((1, hd), f32)
        for a in range(nh):
            sl = slice(a * hd, (a + 1) * hd)
            o = of_ref[:, sl] + ob_ref[:, sl]
            z = z_ref[:, sl]
            dyv = dy_ref[:, sl]
            r = lax.rsqrt(jnp.mean(o * o, axis=1, keepdims=True) + NORM_EPS)
            orr = o * r
            dn = dyv * _silu(z)
            dz_ref[:, sl] = (dyv * orr * wv * _silu_grad(z)).astype(dz_ref.dtype)
            dw = dw + jnp.sum(dn * orr, axis=0, keepdims=True)
            dnw = dn * wv
            do_ref[:, sl] = r * dnw - o * (r * r * r) * jnp.mean(dnw * o, axis=1, keepdims=True)

        @pl.when(i == 0)
        def _():
            dw_ref[...] = dw

        @pl.when(i > 0)
        def _():
            dw_ref[...] += dw

    blk = pl.BlockSpec((tb, W), lambda i: (i, 0))
    row = pl.BlockSpec((1, hd), lambda i: (0, 0))
    return pl.pallas_call(
        body,
        out_shape=(jax.ShapeDtypeStruct((T, W), f32), jax.ShapeDtypeStruct(dproj.shape, dproj.dtype), jax.ShapeDtypeStruct((1, hd), f32)),
        grid=(T // tb,),
        in_specs=[blk, blk, blk, pl.BlockSpec((tb, W), lambda i: (i, gate_blk)), row, _ANY_SPEC],
        out_specs=(blk, pl.BlockSpec((tb, W), lambda i: (i, gate_blk)), row),
        input_output_aliases={5: 1},
        compiler_params=_cparams(("arbitrary",)),
        name=name,
    )(dy, o_f, o_b, proj, w, dproj)


def _gla_prep(proj, w2f, w2b, b2f, b2b, tb=256):
    T = proj.shape[0]
    W = GLA_HEADS * GLA_DK

    def body(s_ref, wf_ref, wb_ref, bf_ref, bb_ref, gf_ref, gb_ref):
        small = s_ref[...]
        gkf = -_softplus(-(_mx(small, wf_ref[...], "nn") + bf_ref[...])) * (1.0 / GATE_NORMALIZER)
        gkb = -_softplus(-(_mx(small, wb_ref[...], "nn") + bb_ref[...])) * (1.0 / GATE_NORMALIZER)
        gf_ref[...] = _exact01(_chunk_tri(tb, False), gkf)
        gb_ref[...] = _exact01(_chunk_tri(tb, True), gkb)

    blk = pl.BlockSpec((tb, W), lambda i: (i, 0))
    wsp = pl.BlockSpec((LANES, W), lambda i: (0, 0))
    row = pl.BlockSpec((1, W), lambda i: (0, 0))
    return pl.pallas_call(
        body,
        out_shape=(jax.ShapeDtypeStruct((T, W), f32), jax.ShapeDtypeStruct((T, W), f32)),
        grid=(T // tb,),
        in_specs=[pl.BlockSpec((tb, LANES), lambda i: (i, SMALL_OFF // LANES)), wsp, wsp, row, row],
        out_specs=(blk, blk),
        compiler_params=_cparams(("parallel",)),
        name="gla_prep",
    )(proj, w2f, w2b, b2f, b2b)


def _gla_prep_bwd(proj, w2f, w2b, b2f, b2b, dgk_f, dgk_b):
    T = proj.shape[0]
    tb = min(ROW_TILE, T)
    W = GLA_HEADS * GLA_DK

    def body(s_ref, wf_ref, wb_ref, bf_ref, bb_ref, df_ref, db_ref, ds_ref, dwf_ref, dwb_ref, dbf_ref, dbb_ref):
        i = pl.program_id(0)
        small = s_ref[...]
        dsm = jnp.zeros((tb, LANES), f32)
        parts = []
        for w_ref, b_ref, d_ref in ((wf_ref, bf_ref, df_ref), (wb_ref, bb_ref, db_ref)):
            pre = _mx(small, w_ref[...], "nn") + b_ref[...]
            dpre = d_ref[...] * (1.0 / GATE_NORMALIZER) * _sigmoid(-pre)
            dsm = dsm + _mx(dpre, w_ref[...], "nt")
            parts.append((_mx(small, dpre, "tn"), jnp.sum(dpre, axis=0, keepdims=True)))
        ds_ref[...] = dsm

        @pl.when(i == 0)
        def _():
            dwf_ref[...], dbf_ref[...] = parts[0]
            dwb_ref[...], dbb_ref[...] = parts[1]

        @pl.when(i > 0)
        def _():
            dwf_ref[...] += parts[0][0]
            dbf_ref[...] += parts[0][1]
            dwb_ref[...] += parts[1][0]
            dbb_ref[...] += parts[1][1]

    blk = pl.BlockSpec((tb, W), lambda i: (i, 0))
    wsp = pl.BlockSpec((LANES, W), lambda i: (0, 0))
    row = pl.BlockSpec((1, W), lambda i: (0, 0))
    return pl.pallas_call(
        body,
        out_shape=(jax.ShapeDtypeStruct((T, LANES), f32), jax.ShapeDtypeStruct((LANES, W), f32), jax.ShapeDtypeStruct((LANES, W), f32),
                   jax.ShapeDtypeStruct((1, W), f32), jax.ShapeDtypeStruct((1, W), f32)),
        grid=(T // tb,),
        in_specs=[pl.BlockSpec((tb, LANES), lambda i: (i, SMALL_OFF // LANES)), wsp, wsp, row, row, blk, blk],
        out_specs=(pl.BlockSpec((tb, LANES), lambda i: (i, 0)), wsp, wsp, row, row),
        compiler_params=_cparams(("arbitrary",)),
        name="gla_prep_bwd",
    )(proj, w2f, w2b, b2f, b2b, dgk_f, dgk_b)


GATE_A_BLK, GATE_B_BLK, Z_BLK, G_B_BLK = 6144 // 1024, 7168 // 1024, 3072 // 1024, 8192 // 1024
DP_GATES_BLK, DP_GLA_QKV_BLK, DP_SMALL_BLK = 6144 // 2048, 4096 // 2048, SMALL_OFF // 512
_ANY_SPEC = pl.BlockSpec(memory_space=pl.ANY)


def _mid(og, obg, proj, x, target, wpg, wpl, wout, w_post, tb=256):
    T = x.shape[0]
    Dm = D_MODEL

    def body(og_ref, obg_ref, ga_ref, gb_ref, x_ref, t_ref, wpg_ref, wpl_ref, wo_ref, wp_ref,
             loss_ref, dwp_ref, dy_ref, m_ref, dout_ref, dya_ref, dyb_ref, dgates_ref, dog_ref, dobg_ref):
        i = pl.program_id(0)
        ya = _mx(og_ref[...], wpg_ref[...], "nn")
        yb = _mx(obg_ref[...], wpl_ref[...], "nn")
        sa, sb = _sigmoid(ga_ref[...]), _sigmoid(gb_ref[...])
        m = sa * ya + sb * yb
        m_ref[...] = m.astype(m_ref.dtype)
        out = _mx(m, wo_ref[...], "nn")
        r = lax.rsqrt(jnp.mean(out * out, axis=1, keepdims=True) + NORM_EPS)
        wp = wp_ref[...]
        e = x_ref[...] + out * r * wp - t_ref[...]
        dy = e * (1.0 / Dm)
        dy_ref[...] = dy
        lpart = jnp.sum(e * e, axis=0, keepdims=True)
        wpart = jnp.sum(dy * out * r, axis=0, keepdims=True)
        dyn = dy * wp
        dout = r * dyn - out * (r * r * r) * jnp.mean(dyn * out, axis=1, keepdims=True)
        dout_ref[...] = dout.astype(dout_ref.dtype)
        dm = _mx(dout, wo_ref[...], "nt")
        dya, dyb = dm * sa, dm * sb
        dya_ref[...] = dya.astype(dya_ref.dtype)
        dyb_ref[...] = dyb.astype(dyb_ref.dtype)
        dgates_ref[:, :Dm] = (dm * ya * sa * (1.0 - sa)).astype(dgates_ref.dtype)
        dgates_ref[:, Dm:] = (dm * yb * sb * (1.0 - sb)).astype(dgates_ref.dtype)
        dog_ref[...] = _mx(dya, wpg_ref[...], "nt")
        dobg_ref[...] = _mx(dyb, wpl_ref[...], "nt")

        @pl.when(i == 0)
        def _():
            loss_ref[...] = lpart
            dwp_ref[...] = wpart

        @pl.when(i > 0)
        def _():
            loss_ref[...] += lpart
            dwp_ref[...] += wpart

    blk = pl.BlockSpec((tb, Dm), lambda i: (i, 0))
    row = pl.BlockSpec((1, Dm), lambda i: (0, 0))
    wsp = pl.BlockSpec((Dm, Dm), lambda i: (0, 0))
    act = jax.ShapeDtypeStruct((T, Dm), MXU_DTYPE)
    big = jax.ShapeDtypeStruct((T, Dm), f32)
    vec = jax.ShapeDtypeStruct((1, Dm), f32)
    return pl.pallas_call(
        body,
        out_shape=(vec, vec, big, act, act, act, act, jax.ShapeDtypeStruct((T, N_CAT), MXU_DTYPE), big, big),
        grid=(T // tb,),
        in_specs=[blk, blk, pl.BlockSpec((tb, Dm), lambda i: (i, GATE_A_BLK)), pl.BlockSpec((tb, Dm), lambda i: (i, GATE_B_BLK)),
                  blk, blk, wsp, wsp, wsp, row],
        out_specs=(row, row) + (blk,) * 5 + (pl.BlockSpec((tb, 2 * Dm), lambda i: (i, DP_GATES_BLK)), blk, blk),
        compiler_params=_cparams(("arbitrary",)),
        name="mid",
    )(og, obg, proj, proj, x, target, wpg, wpl, wout, w_post)


def _heads(ref, lead, n, width, rows=slice(None)):
    return [ref[lead + (rows, slice(i * width, (i + 1) * width))] for i in range(n)]


class _GdnBatch:
    def __init__(self, q, k, v, gcol, grow, beta, nf, A=None):
        B, Cn = q.shape[0], q.shape[1]
        self.nf = nf
        self.incl, self.strict, eye = _dir_masks(B, nf, Cn)
        self.decay = jnp.where(self.incl, jnp.exp(jnp.where(self.incl, gcol - grow, 0.0)), 0.0)
        self.k, self.v, self.beta = k, v, beta
        self.qs = q * (GDN_DIM**-0.5)
        self.kb = k * beta
        self.L = jnp.where(self.strict, _mx(self.kb, k, "nt") * self.decay, 0.0)
        if A is None:
            A = _inv_unit_lower(self.L, jnp.where(eye, 1.0, 0.0).astype(f32))
        self.A = A
        self.A2 = _split2(A)
        self.eg = jnp.exp(gcol)
        self.kbg = self.kb * self.eg
        self.sol = _h3s(self.A2, _split2(jnp.concatenate([v * beta, self.kbg], axis=2)), "nn")
        self.u = self.sol[:, :, :GDN_DIM]
        self.w = self.sol[:, :, GDN_DIM:]
        self.attn = jnp.where(self.incl, _mx(self.qs, k, "nt") * self.decay, 0.0)
        self.qd = self.qs * self.eg
        gl = _last_row(gcol, nf)
        self.ekd = jnp.exp(gl - gcol)
        self.kd = k * self.ekd
        self.egl = jnp.exp(gl)


def _gdn_chain_inputs(qf_ref, qb_ref, gf_ref, gb_ref, cf_ref, cb_ref, b, hb):
    qkv = [jnp.stack(_heads(qf_ref, (i,), hb, GDN_DIM) + _heads(qb_ref, (i,), hb, GDN_DIM)) for i in range(3)]
    cols, rows, betas = [], [], []
    for d, (g_ref, c_ref) in enumerate(((gf_ref, cf_ref), (gb_ref, cb_ref))):
        gates, gc = g_ref[...], c_ref[...]
        for hh in range(hb):
            h = b * hb + hh
            col = _lane_col(gc, h + 8 * d)
            cols.append(col)
            rows.append(jnp.broadcast_to(col, (CHUNK, LANES)).T[0:1, :])
            betas.append(_lane_col(gates, 16 + h + 8 * d))
    return qkv[0], qkv[1], qkv[2], jnp.stack(cols), jnp.stack(rows), jnp.stack(betas)


GDN_CPS = 2


def _chunk_views(refs, s_f, s_b):
    out = []
    for i, r in enumerate(refs):
        s = s_f if i % 2 == 0 else s_b
        rows = pl.ds(s * CHUNK, CHUNK)
        out.append(r.at[pl.ds(s, 1)] if len(r.shape) == 4 else (r.at[:, rows, :] if len(r.shape) == 3 else r.at[rows, :]))
    return out


def _gdn_specs(cps, hb):
    R, W = cps * CHUNK, hb * GDN_DIM
    qsp = lambda ci: pl.BlockSpec((3, R, W), lambda n, b: (0, ci(n), b))
    gsp = lambda ci: pl.BlockSpec((R, LANES), lambda n, b: (ci(n), 0))
    osp = lambda ci: pl.BlockSpec((R, W), lambda n, b: (ci(n), b))
    ssp = lambda ci: pl.BlockSpec((cps, hb, GDN_DIM, GDN_DIM), lambda n, b: (ci(n), b, 0, 0))
    asp = lambda ci: pl.BlockSpec((cps, hb, CHUNK, CHUNK), lambda n, b: (ci(n), b, 0, 0))
    return qsp, gsp, osp, ssp, asp


def _gdn_fwd(qkvn, gates, gc):
    T = qkvn.shape[1]
    N = T // CHUNK
    nh, hb = GDN_HEADS, GDN_HB
    cps = min(GDN_CPS, N)
    NB = N // cps

    def body(*refs):
        n, b = pl.program_id(0), pl.program_id(1)
        s_scr = refs[-1]

        @pl.when(n == 0)
        def _():
            s_scr[0, pl.ds(b * hb, hb)] = jnp.zeros((hb, GDN_DIM, GDN_DIM), f32)
            s_scr[1, pl.ds(b * hb, hb)] = jnp.zeros((hb, GDN_DIM, GDN_DIM), f32)

        for s in range(cps):
            chunk(*_chunk_views(refs[:-1], s, cps - 1 - s), s_scr, b)

    def chunk(qf_ref, qb_ref, gf_ref, gb_ref, cf_ref, cb_ref, of_ref, ob_ref, sf_ref, sb_ref, af_ref, ab_ref, s_scr, b):
        hs = pl.ds(b * hb, hb)
        q, k, v, gcol, grow, beta = _gdn_chain_inputs(qf_ref, qb_ref, gf_ref, gb_ref, cf_ref, cb_ref, b, hb)
        ck = _GdnBatch(q, k, v, gcol, grow, beta, hb)
        S = jnp.concatenate([s_scr[0, hs], s_scr[1, hs]], axis=0)
        sf_ref[0], sb_ref[0] = S[:hb], S[hb:]
        af_ref[0], ab_ref[0] = ck.A[:hb], ck.A[hb:]
        v_new = ck.u - _mx(ck.w, S, "nn")
        o = _mx(ck.qd, S, "nn") + _mx(ck.attn, v_new, "nn")
        S_new = S * ck.egl + _mx(ck.kd, v_new, "tn")
        for hh in range(hb):
            sl = slice(hh * GDN_DIM, (hh + 1) * GDN_DIM)
            of_ref[:, sl] = o[hh]
            ob_ref[:, sl] = o[hb + hh]
        s_scr[0, hs] = S_new[:hb]
        s_scr[1, hs] = S_new[hb:]

    fw, bw = (lambda n: n), (lambda n: NB - 1 - n)
    qsp, gsp, osp, ssp, asp = _gdn_specs(cps, hb)
    o_shape = jax.ShapeDtypeStruct((T, nh * GDN_DIM), f32)
    s_shape = jax.ShapeDtypeStruct((N, nh, GDN_DIM, GDN_DIM), f32)
    a_shape = jax.ShapeDtypeStruct((N, nh, CHUNK, CHUNK), f32)
    return pl.pallas_call(
        body,
        out_shape=(o_shape, o_shape, s_shape, s_shape, a_shape, a_shape),
        grid=(NB, nh // hb),
        in_specs=[qsp(fw), qsp(bw), gsp(fw), gsp(bw), gsp(fw), gsp(bw)],
        out_specs=(osp(fw), osp(bw), ssp(fw), ssp(bw), asp(fw), asp(bw)),
        scratch_shapes=[pltpu.VMEM((2, nh, GDN_DIM, GDN_DIM), f32)],
        compiler_params=_cparams(("arbitrary", "arbitrary")),
        name="gdn_scan_fwd",
    )(qkvn, qkvn, gates, gates, gc, gc)


def _gdn_bwd(qkvn, gates, gc, st_f, st_b, a_f, a_b, do):
    T = qkvn.shape[1]
    N = T // CHUNK
    nh, hb = GDN_HEADS, GDN_HB
    nb = nh // hb
    cps = min(GDN_CPS, N)
    NB = N // cps

    def body(*refs):
        n, b = pl.program_id(0), pl.program_id(1)
        ds_scr = refs[-1]

        @pl.when(n == 0)
        def _():
            ds_scr[0, pl.ds(b * hb, hb)] = jnp.zeros((hb, GDN_DIM, GDN_DIM), f32)
            ds_scr[1, pl.ds(b * hb, hb)] = jnp.zeros((hb, GDN_DIM, GDN_DIM), f32)

        for s in range(cps):
            chunk(*_chunk_views(refs[:-1], cps - 1 - s, s), ds_scr, b)

    def chunk(qf_ref, qb_ref, gf_ref, gb_ref, cf_ref, cb_ref, sf_ref, sb_ref, af_ref, ab_ref, dof_ref, dob_ref,
              dqf_ref, dqb_ref, dgf_ref, dgb_ref, ds_scr, b):
        hs = pl.ds(b * hb, hb)

        @pl.when(b == 0)
        def _():
            dgf_ref[...] = jnp.zeros((CHUNK, LANES), f32)
            dgb_ref[...] = jnp.zeros((CHUNK, LANES), f32)

        q, k, v, gcol, grow, beta = _gdn_chain_inputs(qf_ref, qb_ref, gf_ref, gb_ref, cf_ref, cb_ref, b, hb)
        ck = _GdnBatch(q, k, v, gcol, grow, beta, hb, A=jnp.concatenate([af_ref[0], ab_ref[0]], axis=0))
        S = jnp.concatenate([sf_ref[0], sb_ref[0]], axis=0)
        dS = jnp.concatenate([ds_scr[0, hs], ds_scr[1, hs]], axis=0)
        dov = jnp.stack(_heads(dof_ref, (), hb, GDN_DIM) + _heads(dob_ref, (), hb, GDN_DIM))
        v_new = ck.u - _mx(ck.w, S, "nn")
        d_vnew = _mx(ck.attn, dov, "tn") + _mx(ck.kd, dS, "nn")
        d_attn = jnp.where(ck.incl, _mx(dov, v_new, "nt"), 0.0)
        d_qd = _mx(dov, S, "nt")
        d_kd = _mx(v_new, dS, "nt")
        d_gl = jnp.sum(jnp.sum(dS * S, axis=2, keepdims=True), axis=1, keepdims=True) * ck.egl
        dS_new = dS * ck.egl + _mx(ck.qd, dov, "tn") - _mx(ck.w, d_vnew, "tn")
        ds_scr[0, hs] = dS_new[:hb]
        ds_scr[1, hs] = dS_new[hb:]
        d_w = -_mx(d_vnew, S, "nt")
        d_rhs = _h3s(ck.A2, _split2(jnp.concatenate([d_vnew, d_w], axis=2)), "tn")
        d_vb, d_kbg = d_rhs[:, :, :GDN_DIM], d_rhs[:, :, GDN_DIM:]
        dL = -jnp.where(ck.strict, _h3(d_rhs, ck.sol, "nt"), 0.0)
        P = dL * ck.decay
        Q = d_attn * ck.decay
        d_kb = _mx(P, k, "nn") + d_kbg * ck.eg
        d_qs = _mx(Q, k, "nn") + d_qd * ck.eg
        d_k = _mx(P, ck.kb, "tn") + _mx(Q, ck.qs, "tn") + d_kd * ck.ekd + d_kb * ck.beta
        E = dL * ck.L + d_attn * ck.attn
        kdsum = jnp.sum(d_kd * ck.kd, axis=2, keepdims=True)
        d_gc = (jnp.sum(E, axis=2, keepdims=True) - _colsum_col(E)
                + jnp.sum(d_kbg * ck.kbg, axis=2, keepdims=True)
                + jnp.sum(d_qd * ck.qd, axis=2, keepdims=True) - kdsum)
        d_gc = _add_at_last_row(d_gc, d_gl + jnp.sum(kdsum, axis=1, keepdims=True), hb)
        d_beta = jnp.sum(d_vb * v, axis=2, keepdims=True) + jnp.sum(d_kb * k, axis=2, keepdims=True)
        d_q = d_qs * (GDN_DIM**-0.5)
        d_v = d_vb * ck.beta
        lane = _iota2((CHUNK, LANES), 1)
        for d, (dq_ref, dg_ref) in enumerate(((dqf_ref, dgf_ref), (dqb_ref, dgb_ref))):
            acc = dg_ref[...]
            for hh in range(hb):
                c, h = d * hb + hh, b * hb + hh
                sl = slice(hh * GDN_DIM, (hh + 1) * GDN_DIM)
                dq_ref[0, :, sl] = d_q[c]
                dq_ref[1, :, sl] = d_k[c]
                dq_ref[2, :, sl] = d_v[c]
                acc = jnp.where(lane == h + 8 * d, d_gc[c], acc)
                acc = jnp.where(lane == 16 + h + 8 * d, d_beta[c], acc)
            dg_ref[...] = acc

        @pl.when(b == nb - 1)
        def _():
            for d, dg_ref in enumerate((dgf_ref, dgb_ref)):
                blk = dg_ref[...]
                dg_ref[...] = jnp.where(lane < 16, _exact01(_chunk_tri(CHUNK, d == 0), blk), blk)

    fw, bw = (lambda n: NB - 1 - n), (lambda n: n)
    qsp, gsp, osp, ssp, asp = _gdn_specs(cps, hb)
    dq_shape = jax.ShapeDtypeStruct((3, T, nh * GDN_DIM), f32)
    dg_shape = jax.ShapeDtypeStruct((T, LANES), f32)
    return pl.pallas_call(
        body,
        out_shape=(dq_shape, dq_shape, dg_shape, dg_shape),
        grid=(NB, nb),
        in_specs=[qsp(fw), qsp(bw), gsp(fw), gsp(bw), gsp(fw), gsp(bw), ssp(fw), ssp(bw), asp(fw), asp(bw), osp(fw), osp(bw)],
        out_specs=(qsp(fw), qsp(bw), gsp(fw), gsp(bw)),
        scratch_shapes=[pltpu.VMEM((2, nh, GDN_DIM, GDN_DIM), f32)],
        compiler_params=_cparams(("arbitrary", "arbitrary")),
        name="gdn_scan_bwd",
    )(qkvn, qkvn, gates, gates, gc, gc, st_f, st_b, a_f, a_b, do, do)


class _GlaBatch:
    def __init__(self, q, k, G, nf):
        B, Cn = q.shape[0], q.shape[1]
        self.incl, _, _ = _dir_masks(B, nf, Cn)
        self.eG = jnp.exp(G)
        self.enG = jnp.exp(-G)
        self.qg = q * (GLA_DK**-0.5) * self.eG
        self.kg = k * self.enG
        self.attn = jnp.where(self.incl, _mx(self.qg, self.kg, "nt"), 0.0)
        gl = _last_row(G, nf)
        self.ekd = jnp.exp(gl - G)
        self.kd = k * self.ekd
        self.egl = jnp.exp(gl)


GLA_CPS = 4


def _chunk_rows(s):
    return slice(s * CHUNK, (s + 1) * CHUNK)


def _gla_chain_inputs(refs_f, refs_b, rows_f, rows_b):
    nh = GLA_HEADS
    out = []
    for i, width in enumerate((GLA_DK, GLA_DK, GLA_DV, GLA_DK)):
        out.append(jnp.stack(_heads(refs_f[i], (), nh, width, rows_f) + _heads(refs_b[i], (), nh, width, rows_b)))
    return out


def _gla_specs(rows, ci):
    nh = GLA_HEADS
    return [
        pl.BlockSpec((rows, nh * GLA_DK), lambda n: (ci(n), 4096 // (nh * GLA_DK))),
        pl.BlockSpec((rows, nh * GLA_DK), lambda n: (ci(n), 4608 // (nh * GLA_DK))),
        pl.BlockSpec((rows, nh * GLA_DV), lambda n: (ci(n), 5120 // (nh * GLA_DV))),
        pl.BlockSpec((rows, nh * GLA_DK), lambda n: (ci(n), 0)),
    ]


def _gla_fwd(proj, G_f, G_b):
    T = proj.shape[0]
    N = T // CHUNK
    nh = GLA_HEADS
    cps = min(GLA_CPS, N)
    NB = N // cps

    def body(qf, kf, vf, gf, qb, kb, vb, gb, of_ref, ob_ref, sf_ref, sb_ref, s_scr):
        n = pl.program_id(0)

        @pl.when(n == 0)
        def _():
            s_scr[...] = jnp.zeros((2 * nh, GLA_DV, GLA_DK), f32)

        for s in range(cps):
            sb = cps - 1 - s
            q, k, v, G = _gla_chain_inputs((qf, kf, vf, gf), (qb, kb, vb, gb), _chunk_rows(s), _chunk_rows(sb))
            ck = _GlaBatch(q, k, G, nh)
            St = s_scr[...]
            sf_ref[s], sb_ref[sb] = St[:nh].astype(sf_ref.dtype), St[nh:].astype(sb_ref.dtype)
            o = _mx(ck.qg, St, "nt") + _mx(ck.attn, v, "nn")
            for i in range(nh):
                sl = slice(i * GLA_DV, (i + 1) * GLA_DV)
                of_ref[_chunk_rows(s), sl] = o[i]
                ob_ref[_chunk_rows(sb), sl] = o[nh + i]
            s_scr[...] = St * ck.egl + _mx(v, ck.kd, "tn")

    fw, bw = (lambda n: n), (lambda n: NB - 1 - n)
    osp = lambda ci: pl.BlockSpec((cps * CHUNK, nh * GLA_DV), lambda n: (ci(n), 0))
    ssp = lambda ci: pl.BlockSpec((cps, nh, GLA_DV, GLA_DK), lambda n: (ci(n), 0, 0, 0))
    o_shape = jax.ShapeDtypeStruct((T, nh * GLA_DV), f32)
    s_shape = jax.ShapeDtypeStruct((N, nh, GLA_DV, GLA_DK), MXU_DTYPE)
    return pl.pallas_call(
        body,
        out_shape=(o_shape, o_shape, s_shape, s_shape),
        grid=(NB,),
        in_specs=_gla_specs(cps * CHUNK, fw) + _gla_specs(cps * CHUNK, bw),
        out_specs=(osp(fw), osp(bw), ssp(fw), ssp(bw)),
        scratch_shapes=[pltpu.VMEM((2 * nh, GLA_DV, GLA_DK), f32)],
        compiler_params=_cparams(("arbitrary",)),
        name="gla_scan_fwd",
    )(proj, proj, proj, G_f, proj, proj, proj, G_b)


def _gla_bwd(proj, G_f, G_b, st_f, st_b, do):
    T = proj.shape[0]
    N = T // CHUNK
    nh = GLA_HEADS
    cps = min(GLA_CPS, N)
    NB = N // cps

    def body(qf, kf, vf, gf, sf_ref, dof_ref, qb, kb, vb, gb, sb_ref, dob_ref,
             dqf_ref, dkf_ref, dvf_ref, dgf_ref, dqb_ref, dkb_ref, dvb_ref, dgb_ref, ds_scr):
        n = pl.program_id(0)

        @pl.when(n == 0)
        def _():
            ds_scr[...] = jnp.zeros((2 * nh, GLA_DV, GLA_DK), f32)

        shp = (2 * nh, CHUNK, CHUNK)
        bb, ii, jj = _iota3(shp, 0), _iota3(shp, 1), _iota3(shp, 2)
        tri = jnp.where(jnp.logical_or(jnp.logical_and(bb < nh, jj >= ii), jnp.logical_and(bb >= nh, jj <= ii)), 1.0, 0.0)
        for s in range(cps):
            sf = cps - 1 - s
            rows = (_chunk_rows(sf), _chunk_rows(s))
            q, k, v, G = _gla_chain_inputs((qf, kf, vf, gf), (qb, kb, vb, gb), *rows)
            ck = _GlaBatch(q, k, G, nh)
            St = jnp.concatenate([sf_ref[sf], sb_ref[s]], axis=0).astype(f32)
            dSt = ds_scr[...]
            dov = jnp.stack(_heads(dof_ref, (), nh, GLA_DV, rows[0]) + _heads(dob_ref, (), nh, GLA_DV, rows[1]))
            d_attn = jnp.where(ck.incl, _mx(dov, v, "nt"), 0.0)
            d_qg = _mx(dov, St, "nn") + _mx(d_attn, ck.kg, "nn")
            d_kg = _mx(d_attn, ck.qg, "tn")
            d_v = _mx(ck.attn, dov, "tn") + _mx(ck.kd, dSt, "nt")
            d_kd = _mx(v, dSt, "nn")
            ds_scr[...] = dSt * ck.egl + _mx(dov, ck.qg, "tn")
            kdd = d_kd * ck.kd
            d_gl = jnp.sum(dSt * St, axis=1, keepdims=True) * ck.egl + jnp.sum(kdd, axis=1, keepdims=True)
            d_q = d_qg * ck.eG * (GLA_DK**-0.5)
            d_k = d_kg * ck.enG + d_kd * ck.ekd
            d_gk = _exact01(tri, _add_at_last_row(d_qg * ck.qg - d_kg * ck.kg - kdd, d_gl, nh))
            for d, (dq_ref, dk_ref, dv_ref, dg_ref) in enumerate(((dqf_ref, dkf_ref, dvf_ref, dgf_ref), (dqb_ref, dkb_ref, dvb_ref, dgb_ref))):
                for i in range(nh):
                    c = d * nh + i
                    ks, vs = slice(i * GLA_DK, (i + 1) * GLA_DK), slice(i * GLA_DV, (i + 1) * GLA_DV)
                    dq_ref[rows[d], ks] = d_q[c]
                    dk_ref[rows[d], ks] = d_k[c]
                    dv_ref[rows[d], vs] = d_v[c]
                    dg_ref[rows[d], ks] = d_gk[c]

    R = cps * CHUNK

    def in_specs(ci):
        return _gla_specs(R, ci) + [
            pl.BlockSpec((cps, nh, GLA_DV, GLA_DK), lambda n: (ci(n), 0, 0, 0)),
            pl.BlockSpec((R, nh * GLA_DV), lambda n: (ci(n), 0)),
        ]

    def out_specs(ci):
        ksp = pl.BlockSpec((R, nh * GLA_DK), lambda n: (ci(n), 0))
        return [ksp, ksp, pl.BlockSpec((R, nh * GLA_DV), lambda n: (ci(n), 0)), ksp]

    fw, bw = (lambda n: NB - 1 - n), (lambda n: n)
    k_shape = jax.ShapeDtypeStruct((T, nh * GLA_DK), f32)
    v_shape = jax.ShapeDtypeStruct((T, nh * GLA_DV), f32)
    return pl.pallas_call(
        body,
        out_shape=(k_shape, k_shape, v_shape, k_shape) * 2,
        grid=(NB,),
        in_specs=in_specs(fw) + in_specs(bw),
        out_specs=tuple(out_specs(fw) + out_specs(bw)),
        scratch_shapes=[pltpu.VMEM((2 * nh, GLA_DV, GLA_DK), f32)],
        compiler_params=_cparams(("arbitrary",)),
        name="gla_scan_bwd",
    )(proj, proj, proj, G_f, st_f, do, proj, proj, proj, G_b, st_b, do)


def _gla_dsum(dq_f, dq_b, dk_f, dk_b, dv_f, dv_b, dproj):
    T = dq_f.shape[0]
    tb = min(ROW_TILE, T)
    wk, wv = GLA_HEADS * GLA_DK, GLA_HEADS * GLA_DV

    def body(qf, qb, kf, kb, vf, vb, _, o_ref):
        o_ref[:, :wk] = (qf[...] + qb[...]).astype(o_ref.dtype)
        o_ref[:, wk : 2 * wk] = (kf[...] + kb[...]).astype(o_ref.dtype)
        o_ref[:, 2 * wk :] = (vf[...] + vb[...]).astype(o_ref.dtype)

    ksp = pl.BlockSpec((tb, wk), lambda i: (i, 0))
    vsp = pl.BlockSpec((tb, wv), lambda i: (i, 0))
    return pl.pallas_call(
        body,
        out_shape=jax.ShapeDtypeStruct(dproj.shape, dproj.dtype),
        grid=(T // tb,),
        in_specs=[ksp, ksp, ksp, ksp, vsp, vsp, _ANY_SPEC],
        out_specs=pl.BlockSpec((tb, 2 * wk + wv), lambda i: (i, DP_GLA_QKV_BLK)),
        input_output_aliases={6: 0},
        compiler_params=_cparams(("parallel",)),
        name="gla_dsum",
    )(dq_f, dq_b, dk_f, dk_b, dv_f, dv_b, dproj)


def _local_step(x, target, wcat, conv_w, avec, dvec, gdn_norm_w, w2f, w2b, b2f, b2b, gla_norm_w, proj_weights, w_pre, w_post,
                send_proj_grads=None, send_in_grads=None):
    h, h_t = _rms_pre(x, w_pre)
    proj = _matmul(h, wcat, "nt", f32, "proj_in", 2048, 512, 1024)
    qkvn = _gdn_prep(proj, conv_w)
    gates, gc = _gdn_gates(proj, avec, dvec)
    oa_f, oa_b, sa_f, sa_b, inv_f, inv_b = _gdn_fwd(qkvn, gates, gc)
    og = _out_norm(oa_f, oa_b, proj, Z_BLK, gdn_norm_w, GDN_DIM, "gdn_out")
    G_f, G_b = _gla_prep(proj, w2f, w2b, b2f, b2b)
    ob_f, ob_b, sb_f, sb_b = _gla_fwd(proj, G_f, G_b)
    obg = _out_norm(ob_f, ob_b, proj, G_B_BLK, gla_norm_w, GLA_DV, "gla_out")
    wpg, wpl, wout = proj_weights(obg)
    (loss_row, d_wpost, dy, m, dout, dya, dyb, dproj, dog, dobg) = _mid(og, obg, proj, x, target, wpg, wpl, wout, w_post)

    d_wout = _matmul(m, dout, "tn", MXU_DTYPE, "dw_out", 1024, 512, 2048)
    d_wpg = _matmul(og, dya, "tn", MXU_DTYPE, "dw_proj_gdn", 1024, 512, 2048)
    d_wpl = _matmul(obg, dyb, "tn", MXU_DTYPE, "dw_proj_gla", 1024, 512, 2048)
    if send_proj_grads is not None:
        gla_norm_w = gla_norm_w + send_proj_grads(d_wpg, d_wpl, d_wout)

    do_b, dproj, d_gla_norm = _out_norm_bwd(dobg, ob_f, ob_b, proj, G_B_BLK, gla_norm_w, GLA_DV, "gla_out_bwd", dproj)
    dq_f, dk_f, dv_f, dgk_f, dq_b, dk_b, dv_b, dgk_b = _gla_bwd(proj, G_f, G_b, sb_f, sb_b, do_b)
    dproj = _gla_dsum(dq_f, dq_b, dk_f, dk_b, dv_f, dv_b, dproj)
    dsmall_gla, d_w2f, d_w2b, d_b2f, d_b2b = _gla_prep_bwd(proj, w2f, w2b, b2f, b2b, dgk_f, dgk_b)

    do_a, dproj, d_gdn_norm = _out_norm_bwd(dog, oa_f, oa_b, proj, Z_BLK, gdn_norm_w, GDN_DIM, "gdn_out_bwd", dproj)
    dqkv_f, dqkv_b, dg_f, dg_b = _gdn_bwd(qkvn, gates, gc, sa_f, sa_b, inv_f, inv_b, do_a)
    dproj, d_conv = _gdn_prep_bwd(proj, conv_w, dqkv_f, dqkv_b, dproj)
    dproj, d_alog, d_dtb = _gdn_gates_bwd(proj, avec, dvec, dg_f, dg_b, dsmall_gla, dproj)

    d_wcat = _matmul(h_t, dproj, "nn", MXU_DTYPE, "dw_in", 1024, 512, 2048, out_t=True)
    started = None if send_in_grads is None else send_in_grads(d_wcat)
    dh = _matmul(dproj, wcat, "nn", f32, "dh", 1024, 1024, 2432, after=started)
    grad_x, d_wpre = _rms_pre_bwd(x, w_pre, dh, dy)
    return dict(loss_row=loss_row, grad_x=grad_x, d_wcat=d_wcat, d_conv=d_conv, d_alog=d_alog, d_dtb=d_dtb,
                d_gdn_norm=d_gdn_norm, d_wpg=d_wpg, d_w2f=d_w2f, d_w2b=d_w2b, d_b2f=d_b2f, d_b2b=d_b2b,
                d_gla_norm=d_gla_norm, d_wpl=d_wpl, d_wout=d_wout, d_wpost=d_wpost, d_wpre=d_wpre)


def _to_cat(wf):
    pad = jnp.zeros((N_CAT - N_IN,) + wf.shape[1:], wf.dtype)
    return jnp.concatenate([wf[:4096], wf[4128:6176], wf[7232:9280], wf[6176:7200], wf[4096:4128], wf[7200:7232], pad], axis=0)


def _from_cat(wc):
    return jnp.concatenate([wc[:4096], wc[9216:9248], wc[4096:6144], wc[8192:9216], wc[9248:9280], wc[6144:8192]], axis=0)


def _lane_row(*pieces):
    row = jnp.concatenate(pieces, axis=1)
    return jnp.pad(row, ((0, 0), (0, LANES - row.shape[1])))


def _pad_w2(w2, row0):
    return jnp.pad(w2, ((row0, LANES - row0 - w2.shape[0]), (0, 0)))


SM_ROWS, SM_COLS = 32, 384
REP_ROWS = 32


def _pack_small_shard(conv, w2f, w2b):
    lead = conv.shape[:-2]
    z = lambda r, c: jnp.zeros(lead + (r, c), f32)
    top = jnp.concatenate([conv, z(8 - CONV_K, SM_COLS)], axis=-2)
    mid = jnp.concatenate([w2f, w2b, z(16, SM_COLS - 128)], axis=-1)
    return jnp.concatenate([top, mid, z(SM_ROWS - 24, SM_COLS)], axis=-2)


LOSS_ROW = 28


def _pack_rep(ln_pre, ln_post, b2f, b2b, gla_norm, gdn_norm, alog_f, alog_b, dt_f, dt_b, loss=None):
    last = _lane_row(alog_f, alog_b, dt_f, dt_b)
    extra = jnp.zeros((1, LANES), f32) if loss is None else _lane_row(loss)
    rows = [ln_pre.reshape(8, LANES), ln_post.reshape(8, LANES), b2f.reshape(4, LANES), b2b.reshape(4, LANES),
            gla_norm.reshape(2, LANES), gdn_norm.reshape(1, LANES), last, extra, jnp.zeros((REP_ROWS - LOSS_ROW - 1, LANES), f32)]
    return jnp.concatenate(rows, axis=0)


def _unpack_rep(p):
    return dict(ln_pre_w=p[0:8].reshape(1, 1024), ln_post_w=p[8:16].reshape(1, 1024), gk_b2_fwd=p[16:20].reshape(1, 512),
                gk_b2_bwd=p[20:24].reshape(1, 512), gla_norm_w=p[24:26].reshape(1, 256), gdn_norm_w=p[26:27],
                a_log_fwd=p[27:28, 0:8], a_log_bwd=p[27:28, 8:16], dt_bias_fwd=p[27:28, 16:24], dt_bias_bwd=p[27:28, 24:32])


_MESH = pl.DeviceIdType.MESH


N_BIG_COPIES, N_SMALL_COPIES = 9, 7


def _gather_weights(w_in_s, sm_s):
    shapes = (w_in_s.shape, sm_s.shape)
    dtypes = (MXU_DTYPE, f32)

    def body(win_ref, sm_ref, gin_ref, gsm_ref, send_sems, recv_sems):
        x, y, c = lax.axis_index("x"), lax.axis_index("y"), lax.axis_index("c")
        me, sibling = (x, y, c), (x, y, 1 - c)
        chips = [(1 - x, y), (x, 1 - y), (1 - x, 1 - y)]

        def idx(px, py, pc):
            return 4 * px + 2 * py + pc

        for r in range(w_in_s.shape[1] // LANES):
            cols = slice(r * LANES, (r + 1) * LANES)
            gin_ref[idx(*me), :, cols] = win_ref[:, cols].astype(MXU_DTYPE)
        gsm_ref[idx(*me)] = sm_ref[...]

        def copy(sem, blk, to):
            return pltpu.make_async_remote_copy(src_ref=blk, dst_ref=blk, send_sem=send_sems.at[sem], recv_sem=recv_sems.at[sem],
                                                device_id=to, device_id_type=_MESH)

        def small(k, block, to):
            return copy(N_BIG_COPIES + k, gsm_ref.at[idx(*block)], to)

        half = w_in_s.shape[1] // 2
        xn, yn, dg = (1 - x, y), (x, 1 - y), (1 - x, 1 - y)

        def big(sem, block, to, part=None):
            blk = gin_ref.at[idx(*block)]
            return copy(sem, blk if part is None else blk.at[:, pl.ds(part * half, half)], to)

        started = [big(0, me, sibling), big(1, me, (*xn, c)), big(2, me, (*yn, c)), small(0, me, sibling)]
        started += [small(1 + j, me, (*chip, c)) for j, chip in enumerate(chips)]
        for cp in started:
            cp.start()

        def then(arrived, forwards):
            arrived.wait_recv()
            for cp in forwards:
                cp.start()
            started.extend(forwards)

        then(big(1, (*xn, c), me), [big(3, (*xn, c), (*yn, c), 1), big(5, (*xn, c), sibling)])
        then(big(2, (*yn, c), me), [big(4, (*yn, c), (*xn, c), 0), big(6, (*yn, c), sibling)])
        then(big(4, (*dg, c), me, 0), [big(7, (*dg, c), sibling, 0)])
        then(big(3, (*dg, c), me, 1), [big(8, (*dg, c), sibling, 1)])
        for j, chip in enumerate(chips):
            then(small(1 + j, (*chip, c), me), [small(4 + j, (*chip, c), sibling)])
        big(0, sibling, me).wait_recv()
        big(5, (*xn, 1 - c), me).wait_recv()
        big(6, (*yn, 1 - c), me).wait_recv()
        big(7, (*dg, 1 - c), me, 0).wait_recv()
        big(8, (*dg, 1 - c), me, 1).wait_recv()
        small(0, sibling, me).wait_recv()
        for j, chip in enumerate(chips):
            small(4 + j, (*chip, 1 - c), me).wait_recv()
        for cp in started:
            cp.wait_send()

    vm = pl.BlockSpec(memory_space=pltpu.VMEM)
    return pl.pallas_call(
        body,
        out_shape=tuple(jax.ShapeDtypeStruct((N_DEV,) + s, d) for s, d in zip(shapes, dtypes)),
        in_specs=[vm, vm],
        out_specs=(vm, vm),
        scratch_shapes=[pltpu.SemaphoreType.DMA((N_BIG_COPIES + N_SMALL_COPIES,)), pltpu.SemaphoreType.DMA((N_BIG_COPIES + N_SMALL_COPIES,))],
        compiler_params=pltpu.CompilerParams(vmem_limit_bytes=VMEM_LIMIT),
        name="gather_weights",
    )(w_in_s, sm_s)


N_CHIP = 4
_EFFECT = pltpu.SideEffectType.DATAFLOW_SIDE_EFFECTING
_HBM_SPEC = pl.BlockSpec(memory_space=pltpu.HBM)
_SEM_SPEC = pl.BlockSpec(memory_space=pltpu.SEMAPHORE)


def _relation_peer(k, x, y, c):
    px = (1 - x) if (k & 4) else x
    py = (1 - y) if (k & 2) else y
    pc = (1 - c) if (k & 1) else c
    return (px, py, pc), 4 * px + 2 * py + pc


_PLAN_COPIES = {"gather": N_DEV - 1, "blocks": N_DEV - 1, "sibling": 4, "chips": 3}
_PLAN_SLOTS = {"gather": N_DEV, "blocks": N_DEV, "sibling": 4, "chips": 4}


def _plan_copies(plan, src_ref, land_ref, send_sems, recv_sems, arrival):
    x, y, c = lax.axis_index("x"), lax.axis_index("y"), lax.axis_index("c")
    me, my_chip = 4 * x + 2 * y + c, 2 * x + y
    out = []

    def add(i, src, there, here, dev):
        out.append(pltpu.make_async_remote_copy(src_ref=src, dst_ref=land_ref.at[here if arrival else there], send_sem=send_sems.at[i],
                                                recv_sem=recv_sems.at[i], device_id=dev, device_id_type=_MESH))

    if plan in ("gather", "blocks"):
        for k in range(1, N_DEV):
            dev, p = _relation_peer(k, x, y, c)
            add(k - 1, src_ref.at[p] if plan == "blocks" else src_ref, me, p, dev)
    elif plan == "sibling":
        for j in range(4):
            add(j, src_ref.at[2 * j + 1 - c], j, j, (x, y, 1 - c))
    else:
        for i, k in enumerate((2, 4, 6)):
            (px, py, pc), _ = _relation_peer(k, x, y, c)
            add(i, src_ref.at[2 * px + py], my_chip, 2 * px + py, (px, py, pc))
    return out


def _exchange_start(src, plan, name, after=None):
    n = _PLAN_COPIES[plan]
    land = lax.empty((_PLAN_SLOTS[plan],) + (src.shape if plan == "gather" else src.shape[1:]), src.dtype)
    extra = [] if after is None else [after]

    def body(src_ref, land_ref, *rest):
        send_sems, recv_sems, _, _, token = rest[len(extra) :]
        for cp in _plan_copies(plan, src_ref, land_ref, send_sems, recv_sems, False):
            cp.start()
        token[...] = jnp.zeros_like(token)

    return pl.pallas_call(
        body,
        name=name,
        out_shape=(pltpu.SemaphoreType.DMA((n,)), pltpu.SemaphoreType.DMA((n,)), pltpu.HBM(src.shape, src.dtype),
                   pltpu.HBM(land.shape, land.dtype), jax.ShapeDtypeStruct((8, LANES), f32)),
        in_specs=(_HBM_SPEC, _HBM_SPEC) + (_ANY_SPEC,) * len(extra),
        out_specs=(_SEM_SPEC, _SEM_SPEC, _HBM_SPEC, _HBM_SPEC, pl.BlockSpec(memory_space=pltpu.VMEM)),
        input_output_aliases={0: 2, 1: 3},
        compiler_params=pltpu.CompilerParams(has_side_effects=_EFFECT),
    )(pltpu.with_memory_space_constraint(src, pltpu.HBM), pltpu.with_memory_space_constraint(land, pltpu.HBM), *extra)


def _exchange_wait(started, after, plan, name):
    send_sems, recv_sems, src_thru, land_thru, _ = started

    def body(src_ref, land_ref, send_sems, recv_sems, after_ref, src_dead, got_ref):
        for cp in _plan_copies(plan, src_ref, land_ref, send_sems, recv_sems, True):
            cp.wait_send()
            cp.wait_recv()

    return pl.pallas_call(
        body,
        name=name,
        out_shape=(pltpu.HBM(src_thru.shape, src_thru.dtype), pltpu.HBM(land_thru.shape, land_thru.dtype)),
        in_specs=(_HBM_SPEC, _HBM_SPEC, _SEM_SPEC, _SEM_SPEC, pl.BlockSpec(memory_space=pl.ANY)),
        out_specs=(_HBM_SPEC, _HBM_SPEC),
        input_output_aliases={0: 0, 1: 1},
        compiler_params=pltpu.CompilerParams(has_side_effects=_EFFECT),
    )(src_thru, land_thru, send_sems, recv_sems, after)


def _pair_sum(g, r, name, tc=LANES):
    _, R, Cc = g.shape

    def body(g_ref, r_ref, o_ref):
        o_ref[...] = (g_ref[...].astype(f32) + r_ref[...].astype(f32)).astype(o_ref.dtype)

    return pl.pallas_call(
        body,
        out_shape=jax.ShapeDtypeStruct(r.shape, r.dtype),
        grid=(N_CHIP, Cc // tc),
        in_specs=[pl.BlockSpec((1, R, tc), lambda k, i: (2 * k + lax.axis_index("c"), 0, i)), pl.BlockSpec((1, R, tc), lambda k, i: (k, 0, i))],
        out_specs=pl.BlockSpec((1, R, tc), lambda k, i: (k, 0, i)),
        compiler_params=_cparams(("parallel", "parallel")),
        name=name,
    )(g, r)


def _sum_adam(parts, w, m, v, name, tb, tc=None, own=None, own_slot=None):
    R, Cc = w.shape
    n_parts = parts.shape[0]
    tb = R if tc else min(tb, R)
    tc = tc or Cc
    assert R % tb == 0 and Cc % tc == 0
    bc1 = 1.0 - ADAM_B1**ADAM_STEP
    bc2 = 1.0 - ADAM_B2**ADAM_STEP
    extra = [] if own is None else [own]

    def body(p_ref, w_ref, m_ref, v_ref, *rest):
        g_ref, d_ref, nm_ref, nv_ref = rest[len(extra) :]

        def part(j):
            pj = p_ref[j].astype(f32)
            return pj if own is None else jnp.where(own_slot() == j, rest[0][0].astype(f32), pj)

        g = part(0)
        for j in range(1, n_parts):
            g = g + part(j)
        g_ref[...] = g
        m2 = ADAM_B1 * m_ref[...] + (1.0 - ADAM_B1) * g
        v2 = ADAM_B2 * v_ref[...] + (1.0 - ADAM_B2) * (g * g)
        nm_ref[...] = m2
        nv_ref[...] = v2
        d_ref[...] = -ADAM_LR * ((m2 / bc1) / (jnp.sqrt(v2 / bc2) + ADAM_EPS) + ADAM_WD * w_ref[...])

    blk = pl.BlockSpec((tb, tc), lambda i, j: (i, j))
    o = jax.ShapeDtypeStruct((R, Cc), f32)
    return pl.pallas_call(
        body,
        out_shape=(o, o, o, o),
        grid=(R // tb, Cc // tc),
        in_specs=[pl.BlockSpec((n_parts, tb, tc), lambda i, j: (0, i, j)), blk, blk, blk]
        + [pl.BlockSpec((1, tb, tc), lambda i, j: (own_slot(), i, j))] * len(extra),
        out_specs=(blk, blk, blk, blk),
        compiler_params=_cparams(("parallel", "parallel")),
        name=name,
    )(parts, w, m, v, *extra)


def _my_device():
    return 4 * lax.axis_index("x") + 2 * lax.axis_index("y") + lax.axis_index("c")


def _my_chip():
    return 2 * lax.axis_index("x") + lax.axis_index("y")


_WEIGHTS = ["ln_pre_w", "w_in", "conv_w", "a_log_fwd", "a_log_bwd", "dt_bias_fwd", "dt_bias_bwd", "gdn_norm_w", "w_proj_gdn",
            "gk_w2_fwd", "gk_b2_fwd", "gk_w2_bwd", "gk_b2_bwd", "gla_norm_w", "w_proj_gla", "w_out", "ln_post_w"]
_REP_ORDER = ["ln_pre_w", "ln_post_w", "gk_b2_fwd", "gk_b2_bwd", "gla_norm_w", "gdn_norm_w", "a_log_fwd", "a_log_bwd", "dt_bias_fwd", "dt_bias_bwd"]


def kernel(x, ln_pre_w, w_in, conv_w, a_log_fwd, a_log_bwd, dt_bias_fwd, dt_bias_bwd, gdn_norm_w, w_proj_gdn, gk_w2_fwd, gk_b2_fwd, gk_w2_bwd, gk_b2_bwd, gla_norm_w, w_proj_gla, w_out, ln_post_w, loss_target, m_ln_pre_w, m_w_in, m_conv_w, m_a_log_fwd, m_a_log_bwd, m_dt_bias_fwd, m_dt_bias_bwd, m_gdn_norm_w, m_w_proj_gdn, m_gk_w2_fwd, m_gk_b2_fwd, m_gk_w2_bwd, m_gk_b2_bwd, m_gla_norm_w, m_w_proj_gla, m_w_out, m_ln_post_w, v_ln_pre_w, v_w_in, v_conv_w, v_a_log_fwd, v_a_log_bwd, v_dt_bias_fwd, v_dt_bias_bwd, v_gdn_norm_w, v_w_proj_gdn, v_gk_w2_fwd, v_gk_b2_fwd, v_gk_w2_bwd, v_gk_b2_bwd, v_gla_norm_w, v_w_proj_gla, v_w_out, v_ln_post_w):
    args = locals()
    W = {n: args[n] for n in _WEIGHTS}
    M = {n: args["m_" + n] for n in _WEIGHTS}
    V = {n: args["v_" + n] for n in _WEIGHTS}

    wp_stack = lambda P: jnp.concatenate([P["w_proj_gdn"], P["w_proj_gla"], P["w_out"]], axis=0)
    sm_pack = lambda P: _pack_small_shard(P["conv_w"][0], P["gk_w2_fwd"][0], P["gk_w2_bwd"][0])
    me = 4 * lax.axis_index("x") + 2 * lax.axis_index("y") + lax.axis_index("c")
    w_in_t, m_in_t, v_in_t = (jnp.transpose(a[0]) for a in (w_in, m_w_in, v_w_in))
    g_in, g_sm = _gather_weights(w_in_t, sm_pack(W))
    wp_own = wp_stack(W).astype(MXU_DTYPE)
    wp_started = _exchange_start(wp_own, "gather", "gather_proj_start", after=g_sm)

    def proj_weights(after):
        own, land = _exchange_wait(wp_started, after, "gather", "gather_proj_wait")
        g_p = lax.dynamic_update_slice(land, own[None], (me, 0, 0, 0))
        return tuple(g_p[:, i].reshape(D_MODEL, D_MODEL) for i in range(3))

    rows_p = 3 * D_MODEL // N_DEV
    sent = {}

    def send_proj_grads(d_wpg, d_wpl, d_wout):
        p_p = jnp.stack([d_wpg, d_wpl, d_wout]).reshape(3, N_DEV, D_MODEL // N_DEV, D_MODEL).transpose(1, 0, 2, 3)
        sent["p_p"] = p_p.reshape(N_DEV, rows_p, D_MODEL)
        sent["started"] = _exchange_start(sent["p_p"], "blocks", "exchange_proj_start")
        return sent["started"][4][0:1, 0:1]

    def send_in_grads(d_wcat):
        p_in = _from_cat(d_wcat).reshape(N_DEV, SHARD_IN, D_MODEL)
        sib = _exchange_start(p_in, "sibling", "exchange_in_sibling_start")
        p_p, land_p = _exchange_wait(sent["started"], sib[4], "blocks", "exchange_proj_wait")
        sent["o_p"] = _sum_adam(land_p, wp_stack(W).reshape(rows_p, D_MODEL), wp_stack(M).reshape(rows_p, D_MODEL),
                                wp_stack(V).reshape(rows_p, D_MODEL), "adam_w_proj", 128, own=p_p, own_slot=_my_device)
        p_in, sib_land = _exchange_wait(sib, sent["o_p"][0], "sibling", "exchange_in_sibling_wait")
        sent["chips"] = _exchange_start(_pair_sum(p_in, sib_land, "pair_sum_w_in", tc=512), "chips", "exchange_in_chips_start")
        return sent["chips"][4]

    wcat = _to_cat(g_in.reshape(N_IN, D_MODEL))
    conv_full = g_sm[:, 0:CONV_K, :].transpose(1, 0, 2).reshape(CONV_K, N_DEV * SM_COLS)
    w2f_full = g_sm[:, 8:24, 0:64].transpose(1, 0, 2).reshape(16, 512)
    w2b_full = g_sm[:, 8:24, 64:128].transpose(1, 0, 2).reshape(16, 512)

    avec = _lane_row(a_log_fwd, a_log_bwd)
    dvec = _lane_row(dt_bias_fwd, dt_bias_bwd)
    G = _local_step(x[0], loss_target[0], wcat, conv_full, avec, dvec, gdn_norm_w, _pad_w2(w2f_full, 32), _pad_w2(w2b_full, 48),
                    gk_b2_fwd, gk_b2_bwd, gla_norm_w, proj_weights, ln_pre_w + wp_started[4][0:1, 0:1], ln_post_w, send_proj_grads,
                    send_in_grads)

    p_sm = _pack_small_shard(G["d_conv"].reshape(CONV_K, N_DEV, SM_COLS).transpose(1, 0, 2),
                             G["d_w2f"][32:48].reshape(16, N_DEV, 64).transpose(1, 0, 2),
                             G["d_w2b"][48:64].reshape(16, N_DEV, 64).transpose(1, 0, 2))
    p_rep = _pack_rep(G["d_wpre"], G["d_wpost"], G["d_b2f"], G["d_b2b"], G["d_gla_norm"], G["d_gdn_norm"],
                      G["d_alog"][:, 0:8], G["d_alog"][:, 8:16], G["d_dtb"][:, 0:8], G["d_dtb"][:, 8:16],
                      loss=0.5 * jnp.sum(G["loss_row"], axis=1, keepdims=True) / D_MODEL)
    p_small = jnp.concatenate([p_sm, jnp.broadcast_to(p_rep[None], (N_DEV,) + p_rep.shape)], axis=2)
    small_started = _exchange_start(p_small, "blocks", "exchange_small_start")
    q_in, land_in = _exchange_wait(sent["chips"], small_started[4], "chips", "exchange_in_chips_wait")
    o_in = _sum_adam(land_in, w_in_t, m_in_t, v_in_t, "adam_w_in", SHARD_IN, tc=256, own=q_in, own_slot=_my_chip)
    o_p = sent["o_p"]
    p_small, land_small = _exchange_wait(small_started, o_in[0], "blocks", "exchange_small_wait")
    rep_pack = lambda P: _pack_rep(*[P[n] for n in _REP_ORDER])
    small_pack = lambda P: jnp.concatenate([sm_pack(P), rep_pack(P)], axis=1)
    o_small = _sum_adam(land_small, small_pack(W), small_pack(M), small_pack(V), "adam_small", SM_ROWS, own=p_small, own_slot=_my_device)
    o_sm, o_rep = [o[:, :SM_COLS] for o in o_small], [o[:, SM_COLS:] for o in o_small]

    res = []
    for kind in range(4):
        rep = _unpack_rep(o_rep[kind])
        per = dict(rep)
        per["w_in"] = jnp.transpose(o_in[kind])[None]
        pp = o_p[kind].reshape(3, 1, D_MODEL // N_DEV, D_MODEL)
        per["w_proj_gdn"], per["w_proj_gla"], per["w_out"] = pp[0], pp[1], pp[2]
        per["conv_w"] = o_sm[kind][None, 0:CONV_K, :]
        per["gk_w2_fwd"] = o_sm[kind][None, 8:24, 0:64]
        per["gk_w2_bwd"] = o_sm[kind][None, 8:24, 64:128]
        res.append([per[n] for n in _WEIGHTS])
    loss = o_rep[0][LOSS_ROW, 0]
    return (loss, G["grad_x"][None], *res[0], *res[1], *res[2], *res[3])
```

```python
import jax
import jax.numpy as jnp
from jax import lax
from jax.experimental import pallas as pl
from jax.experimental.pallas import tpu as pltpu

f32 = jnp.float32
MXU_DTYPE = jnp.bfloat16

D_MODEL = 1024
CHUNK = 64
NORM_EPS = 1e-6
GDN_HEADS, GDN_DIM = 8, 128
GLA_HEADS, GLA_DK, GLA_DV = 4, 128, 256
GATE_NORMALIZER = 16.0
CONV_K = 5
N_IN = 9280
N_DEV = 8
SHARD_IN = N_IN // N_DEV

N_CAT = 9728
SMALL_OFF = 9216
LANES = 128

ADAM_LR, ADAM_B1, ADAM_B2, ADAM_EPS, ADAM_WD, ADAM_STEP = 0.001, 0.9, 0.999, 1e-08, 0.01, 10

VMEM_LIMIT = 56 * 1024 * 1024


def _cparams(sem=None):
    return pltpu.CompilerParams(dimension_semantics=sem, vmem_limit_bytes=VMEM_LIMIT)


_DN = {"nn": (((1,), (0,)), ((), ())), "nt": (((1,), (1,)), ((), ())), "tn": (((0,), (0,)), ((), ()))}
_BDN = {"nn": (((2,), (1,)), ((0,), (0,))), "nt": (((2,), (2,)), ((0,), (0,))), "tn": (((1,), (1,)), ((0,), (0,)))}


def _dot16(a, b, kind):
    return lax.dot_general(a, b, (_DN if a.ndim == 2 else _BDN)[kind], preferred_element_type=f32)


def _mx(a, b, kind):
    return _dot16(a.astype(MXU_DTYPE), b.astype(MXU_DTYPE), kind)


def _split2(a):
    hi = a.astype(jnp.bfloat16)
    return hi, (a - hi.astype(f32)).astype(jnp.bfloat16)


def _split3(a):
    p1 = a.astype(jnp.bfloat16)
    r = a - p1.astype(f32)
    p2 = r.astype(jnp.bfloat16)
    return p1, p2, (r - p2.astype(f32)).astype(jnp.bfloat16)


def _h3s(a2, b2, kind):
    (ah, al), (bh, bl) = a2, b2
    return _dot16(ah, bh, kind) + (_dot16(ah, bl, kind) + _dot16(al, bh, kind))


def _h3(a, b, kind):
    return _h3s(_split2(a), _split2(b), kind)


def _exact01(t01, x, kind="nn"):
    t = t01.astype(jnp.bfloat16)
    x1, x2, x3 = _split3(x)
    return _dot16(t, x1, kind) + (_dot16(t, x2, kind) + _dot16(t, x3, kind))


def _colsum_col(e):
    ones = jnp.ones(e.shape[:-1] + (LANES,), jnp.bfloat16)
    e1, e2, e3 = _split3(e)
    return (_dot16(e1, ones, "tn") + (_dot16(e2, ones, "tn") + _dot16(e3, ones, "tn")))[..., 0:1]


def _iota3(shape, axis):
    return lax.broadcasted_iota(jnp.int32, shape, axis)


def _dir_masks(B, nf, Cn):
    shp = (B, Cn, Cn)
    bb, ii, jj = _iota3(shp, 0), _iota3(shp, 1), _iota3(shp, 2)
    fwd = bb < nf
    rev = jnp.logical_not(fwd)
    incl = jnp.logical_or(jnp.logical_and(fwd, ii >= jj), jnp.logical_and(rev, ii <= jj))
    strict = jnp.logical_and(incl, ii != jj)
    return incl, strict, ii == jj


def _last_row(x, nf):
    B, Cn = x.shape[0], x.shape[1]
    fwd = _iota3((B, 1, 1), 0) < nf
    return jnp.where(fwd, x[:, Cn - 1 : Cn, :], x[:, 0:1, :])


def _add_at_last_row(x, val, nf):
    B, Cn = x.shape[0], x.shape[1]
    bb, rr = _iota3((B, Cn, 1), 0), _iota3((B, Cn, 1), 1)
    at = jnp.logical_or(jnp.logical_and(bb < nf, rr == Cn - 1), jnp.logical_and(bb >= nf, rr == 0))
    return x + jnp.where(at, val, 0.0)


def _sigmoid(x):
    return jax.nn.sigmoid(x)


def _silu(x):
    return x * _sigmoid(x)


def _silu_grad(x):
    s = _sigmoid(x)
    return s * (1.0 + x * (1.0 - s))


def _softplus(x):
    u = jnp.exp(-jnp.abs(x))
    l1p = jnp.where(u < 1e-3, u * (1.0 - u * (0.5 - u * (1.0 / 3.0))), jnp.log(1.0 + u))
    return jnp.maximum(x, 0.0) + l1p


def _mxr(x):
    return x.astype(MXU_DTYPE).astype(f32)


def _iota2(shape, axis):
    return lax.broadcasted_iota(jnp.int32, shape, axis)


def _matmul(a, b, kind, out_dtype, name, tm, tn, tk, out_t=False, after=None):
    extra = [] if after is None else [after]
    if kind == "nn":
        (M, K), N = a.shape, b.shape[1]
    elif kind == "nt":
        (M, K), N = a.shape, b.shape[0]
    else:
        (K, M), N = a.shape, b.shape[1]
    tm, tn, tk = min(tm, M), min(tn, N), min(tk, K)
    assert M % tm == 0 and N % tn == 0 and K % tk == 0, (name, M, N, K)
    nk = K // tk
    if kind == "tn":
        a_spec = pl.BlockSpec((tk, tm), lambda i, j, k: (k, i))
    else:
        a_spec = pl.BlockSpec((tm, tk), lambda i, j, k: (i, k))
    if kind == "nt":
        b_spec = pl.BlockSpec((tn, tk), lambda i, j, k: (j, k))
    else:
        b_spec = pl.BlockSpec((tk, tn), lambda i, j, k: (k, j))

    def body(a_ref, b_ref, *rest):
        o_ref, acc = rest[len(extra)], rest[len(extra) + 1 :]
        p = _mx(a_ref[...], b_ref[...], kind)
        if nk == 1:
            o_ref[...] = (p.T if out_t else p).astype(out_dtype)
        else:
            assert not out_t
            acc_ref = acc[0]
            k = pl.program_id(2)

            @pl.when(k == 0)
            def _():
                acc_ref[...] = p

            @pl.when(k > 0)
            def _():
                acc_ref[...] += p

            @pl.when(k == nk - 1)
            def _():
                o_ref[...] = acc_ref[...].astype(out_dtype)

    return pl.pallas_call(
        body,
        out_shape=jax.ShapeDtypeStruct((N, M) if out_t else (M, N), out_dtype),
        grid=(M // tm, N // tn, nk),
        in_specs=[a_spec, b_spec] + [pl.BlockSpec((8, LANES), lambda i, j, k: (0, 0))] * len(extra),
        out_specs=pl.BlockSpec((tn, tm), lambda i, j, k: (j, i)) if out_t else pl.BlockSpec((tm, tn), lambda i, j, k: (i, j)),
        scratch_shapes=[] if nk == 1 else [pltpu.VMEM((tm, tn), f32)],
        compiler_params=_cparams(("parallel", "parallel", "arbitrary")),
        name=name,
    )(a, b, *extra)


ROW_TILE = 512


def _rms_pre(x, w):
    T = x.shape[0]
    tb = min(ROW_TILE, T)

    def body(x_ref, w_ref, h_ref, ht_ref):
        xv = x_ref[...]
        r = lax.rsqrt(jnp.mean(xv * xv, axis=1, keepdims=True) + NORM_EPS)
        h = xv * r * w_ref[...]
        h_ref[...] = h.astype(h_ref.dtype)
        ht_ref[...] = h.T.astype(ht_ref.dtype)

    return pl.pallas_call(
        body,
        out_shape=(jax.ShapeDtypeStruct((T, D_MODEL), MXU_DTYPE), jax.ShapeDtypeStruct((D_MODEL, T), MXU_DTYPE)),
        grid=(T // tb,),
        in_specs=[pl.BlockSpec((tb, D_MODEL), lambda i: (i, 0)), pl.BlockSpec((1, D_MODEL), lambda i: (0, 0))],
        out_specs=(pl.BlockSpec((tb, D_MODEL), lambda i: (i, 0)), pl.BlockSpec((D_MODEL, tb), lambda i: (0, i))),
        compiler_params=_cparams(("parallel",)),
        name="rms_pre",
    )(x, w)


def _rms_pre_bwd(x, w, dh, dy):
    T = x.shape[0]
    tb = min(ROW_TILE, T)

    def body(x_ref, w_ref, dh_ref, dy_ref, dx_ref, dw_ref):
        i = pl.program_id(0)
        xv, dhv = x_ref[...], dh_ref[...]
        r = lax.rsqrt(jnp.mean(xv * xv, axis=1, keepdims=True) + NORM_EPS)
        dhw = dhv * w_ref[...]
        dx_ref[...] = dy_ref[...] + r * dhw - xv * (r * r * r) * jnp.mean(dhw * xv, axis=1, keepdims=True)
        part = jnp.sum(dhv * xv * r, axis=0, keepdims=True)

        @pl.when(i == 0)
        def _():
            dw_ref[...] = part

        @pl.when(i > 0)
        def _():
            dw_ref[...] += part

    blk = pl.BlockSpec((tb, D_MODEL), lambda i: (i, 0))
    row = pl.BlockSpec((1, D_MODEL), lambda i: (0, 0))
    return pl.pallas_call(
        body,
        out_shape=(jax.ShapeDtypeStruct((T, D_MODEL), f32), jax.ShapeDtypeStruct((1, D_MODEL), f32)),
        grid=(T // tb,),
        in_specs=[blk, row, blk, blk],
        out_specs=(blk, row),
        compiler_params=_cparams(("arbitrary",)),
        name="rms_pre_bwd",
    )(x, w, dh, dy)


SHIFTS = (-2, -1, 1, 2)


def _shifted(x):
    T = x.shape[0]
    t = _iota2(x.shape, 0)
    out = {}
    for d in SHIFTS:
        valid = (t < T - d) if d > 0 else (t >= -d)
        out[d] = jnp.where(valid, pltpu.roll(x, (-d) % T, 0), 0.0)
    return out


def _conv5(u, ush, cw):
    acc = u * cw[2:3, :]
    for j in (0, 1, 3, 4):
        acc = acc + ush[j - 2] * cw[j : j + 1, :]
    return acc


def _gdn_prep(proj, conv_w):
    T = proj.shape[0]
    nh = GDN_HEADS

    def body(u_ref, cw_ref, o_ref):
        j = pl.program_id(0)
        u = _mxr(u_ref[...])
        s = _silu(_conv5(u, _shifted(u), _mxr(cw_ref[...])))

        @pl.when(j < 2 * nh)
        def _():
            o_ref[0] = s * lax.rsqrt(jnp.sum(s * s, axis=1, keepdims=True) + NORM_EPS)

        @pl.when(j >= 2 * nh)
        def _():
            o_ref[0] = s

    return pl.pallas_call(
        body,
        out_shape=jax.ShapeDtypeStruct((3, T, GDN_HEADS * GDN_DIM), f32),
        grid=(3 * nh,),
        in_specs=[pl.BlockSpec((T, GDN_DIM), lambda j: (0, j)), pl.BlockSpec((CONV_K, GDN_DIM), lambda j: (0, j))],
        out_specs=pl.BlockSpec((1, T, GDN_DIM), lambda j: (j // nh, 0, j % nh)),
        compiler_params=_cparams(("parallel",)),
        name="gdn_prep",
    )(proj, conv_w)


def _gdn_prep_bwd(proj, conv_w, dqkv_f, dqkv_b, dproj):
    T = proj.shape[0]
    nh = GDN_HEADS


    def body(u_ref, cw_ref, df_ref, db_ref, _, du_ref, dcw_ref):
        j = pl.program_id(0)
        u, cw = _mxr(u_ref[...]), _mxr(cw_ref[...])
        ush = _shifted(u)
        c = _conv5(u, ush, cw)
        s = _silu(c)
        dn = df_ref[0] + db_ref[0]
        rinv = lax.rsqrt(jnp.sum(s * s, axis=1, keepdims=True) + NORM_EPS)
        ds_norm = rinv * dn - s * (rinv * rinv * rinv) * jnp.sum(dn * s, axis=1, keepdims=True)
        ds = jnp.where(j < 2 * nh, ds_norm, dn)
        dc = _mxr(ds * _silu_grad(c))
        dcsh = _shifted(dc)
        du = dc * cw[2:3, :]
        for jj in range(CONV_K):
            d = jj - 2
            dcw_ref[jj : jj + 1, :] = jnp.sum(dc * (u if d == 0 else ush[d]), axis=0, keepdims=True)
            if d != 0:
                du = du + dcsh[-d] * cw[jj : jj + 1, :]
        du_ref[...] = du.astype(du_ref.dtype)

    qspec = pl.BlockSpec((1, T, GDN_DIM), lambda j: (j // nh, 0, j % nh))
    return pl.pallas_call(
        body,
        out_shape=(jax.ShapeDtypeStruct(dproj.shape, dproj.dtype), jax.ShapeDtypeStruct((CONV_K, 3 * nh * GDN_DIM), f32)),
        grid=(3 * nh,),
        in_specs=[pl.BlockSpec((T, GDN_DIM), lambda j: (0, j)), pl.BlockSpec((CONV_K, GDN_DIM), lambda j: (0, j)), qspec, qspec, _ANY_SPEC],
        out_specs=(pl.BlockSpec((T, GDN_DIM), lambda j: (0, j)), pl.BlockSpec((CONV_K, GDN_DIM), lambda j: (0, j))),
        input_output_aliases={4: 0},
        compiler_params=_cparams(("parallel",)),
        name="gdn_prep_bwd",
    )(proj, conv_w, dqkv_f, dqkv_b, dproj)


def _chunk_tri(n, rev):
    i, j = _iota2((n, n), 0), _iota2((n, n), 1)
    same = jnp.right_shift(i, 6) == jnp.right_shift(j, 6)
    order = (j >= i) if rev else (j <= i)
    return jnp.where(jnp.logical_and(same, order), 1.0, 0.0).astype(f32)


def _gdn_gates(proj, avec, dvec, tb=256):
    T = proj.shape[0]

    def body(s_ref, a_ref, d_ref, g_ref, gc_ref):
        small = s_ref[...]
        lane = _iota2(small.shape, 1)
        lg = -jnp.exp(a_ref[...]) * _softplus(small + d_ref[...])
        beta = _sigmoid(small)
        g_ref[...] = jnp.where(lane < 16, lg, jnp.where(lane < 32, beta, 0.0))
        lgm = jnp.where(lane < 16, lg, 0.0)
        gcf = _exact01(_chunk_tri(tb, False), lgm)
        gcr = _exact01(_chunk_tri(tb, True), lgm)
        gc_ref[...] = jnp.where(lane < 8, gcf, gcr)

    blk = pl.BlockSpec((tb, LANES), lambda i: (i, 0))
    row = pl.BlockSpec((1, LANES), lambda i: (0, 0))
    return pl.pallas_call(
        body,
        out_shape=(jax.ShapeDtypeStruct((T, LANES), f32), jax.ShapeDtypeStruct((T, LANES), f32)),
        grid=(T // tb,),
        in_specs=[pl.BlockSpec((tb, LANES), lambda i: (i, SMALL_OFF // LANES)), row, row],
        out_specs=(blk, blk),
        compiler_params=_cparams(("parallel",)),
        name="gdn_gates",
    )(proj, avec, dvec)


def _gdn_gates_bwd(proj, avec, dvec, dg_f, dg_b, dsmall_gla, dproj):
    T = proj.shape[0]
    tb = min(ROW_TILE, T)
    pad = N_CAT - SMALL_OFF

    def body(s_ref, a_ref, d_ref, gf_ref, gb_ref, dl_ref, _, ds_ref, da_ref, dd_ref):
        i = pl.program_id(0)
        small = s_ref[...]
        lane = _iota2(small.shape, 1)
        dgate = gf_ref[...] + gb_ref[...]
        z = small + d_ref[...]
        nega = -jnp.exp(a_ref[...])
        dz = dgate * nega * _sigmoid(z)
        beta = _sigmoid(small)
        dsm = jnp.where(lane < 16, dz, jnp.where(lane < 32, dgate * beta * (1.0 - beta), 0.0))
        ds_ref[:, :LANES] = (dsm + dl_ref[...]).astype(ds_ref.dtype)
        ds_ref[:, LANES:] = jnp.zeros((tb, pad - LANES), ds_ref.dtype)
        lg = nega * _softplus(z)
        pa = jnp.sum(jnp.where(lane < 16, dgate * lg, 0.0), axis=0, keepdims=True)
        pd = jnp.sum(jnp.where(lane < 16, dz, 0.0), axis=0, keepdims=True)

        @pl.when(i == 0)
        def _():
            da_ref[...] = pa
            dd_ref[...] = pd

        @pl.when(i > 0)
        def _():
            da_ref[...] += pa
            dd_ref[...] += pd

    blk = pl.BlockSpec((tb, LANES), lambda i: (i, 0))
    row = pl.BlockSpec((1, LANES), lambda i: (0, 0))
    return pl.pallas_call(
        body,
        out_shape=(jax.ShapeDtypeStruct(dproj.shape, dproj.dtype), jax.ShapeDtypeStruct((1, LANES), f32), jax.ShapeDtypeStruct((1, LANES), f32)),
        grid=(T // tb,),
        in_specs=[pl.BlockSpec((tb, LANES), lambda i: (i, SMALL_OFF // LANES)), row, row, blk, blk, blk, _ANY_SPEC],
        out_specs=(pl.BlockSpec((tb, pad), lambda i: (i, DP_SMALL_BLK)), row, row),
        input_output_aliases={6: 0},
        compiler_params=_cparams(("arbitrary",)),
        name="gdn_gates_bwd",
    )(proj, avec, dvec, dg_f, dg_b, dsmall_gla, dproj)


GDN_HB = 8


def _inv_unit_lower(L, eye):
    A = eye - L
    P2 = _split2(L)
    for _ in range(3):
        P2 = _split2(_h3s(P2, P2, "nn"))
        A = A + _h3s(_split2(A), P2, "nn")
    Ph = P2[0]
    for _ in range(2):
        Ph = _dot16(Ph, Ph, "nn").astype(jnp.bfloat16)
        A = A + _dot16(A.astype(jnp.bfloat16), Ph, "nn")
    return A


def _lane_col(blk, idx):
    lane = _iota2(blk.shape, 1)
    return jnp.sum(jnp.where(lane == idx, blk, 0.0), axis=1, keepdims=True)


def _out_norm(o_f, o_b, proj, gate_blk, w, hd, name):
    T, W = o_f.shape
    tb = min(ROW_TILE, T)
    nh = W // hd

    def body(of_ref, ob_ref, z_ref, w_ref, y_ref):
        wv = w_ref[...]
        for i in range(nh):
            sl = slice(i * hd, (i + 1) * hd)
            o = of_ref[:, sl] + ob_ref[:, sl]
            r = lax.rsqrt(jnp.mean(o * o, axis=1, keepdims=True) + NORM_EPS)
            y_ref[:, sl] = (o * r * wv * _silu(z_ref[:, sl])).astype(y_ref.dtype)

    blk = pl.BlockSpec((tb, W), lambda i: (i, 0))
    return pl.pallas_call(
        body,
        out_shape=jax.ShapeDtypeStruct((T, W), MXU_DTYPE),
        grid=(T // tb,),
        in_specs=[blk, blk, pl.BlockSpec((tb, W), lambda i: (i, gate_blk)), pl.BlockSpec((1, hd), lambda i: (0, 0))],
        out_specs=blk,
        compiler_params=_cparams(("parallel",)),
        name=name,
    )(o_f, o_b, proj, w)


def _out_norm_bwd(dy, o_f, o_b, proj, gate_blk, w, hd, name, dproj):
    T, W = o_f.shape
    tb = min(ROW_TILE, T)
    nh = W // hd

    def body(dy_ref, of_ref, ob_ref, z_ref, w_ref, _, do_ref, dz_ref, dw_ref):
        i = pl.program_id(0)
        wv = w_ref[...]
        dw = jnp.zeros((1, hd), f32)
        for a in range(nh):
            sl = slice(a * hd, (a + 1) * hd)
            o = of_ref[:, sl] + ob_ref[:, sl]
            z = z_ref[:, sl]
            dyv = dy_ref[:, sl]
            r = lax.rsqrt(jnp.mean(o * o, axis=1, keepdims=True) + NORM_EPS)
            orr = o * r
            dn = dyv * _silu(z)
            dz_ref[:, sl] = (dyv * orr * wv * _silu_grad(z)).astype(dz_ref.dtype)
            dw = dw + jnp.sum(dn * orr, axis=0, keepdims=True)
            dnw = dn * wv
            do_ref[:, sl] = r * dnw - o * (r * r * r) * jnp.mean(dnw * o, axis=1, keepdims=True)

        @pl.when(i == 0)
        def _():
            dw_ref[...] = dw

        @pl.when(i > 0)
        def _():
            dw_ref[...] += dw

    blk = pl.BlockSpec((tb, W), lambda i: (i, 0))
    row = pl.BlockSpec((1, hd), lambda i: (0, 0))
    return pl.pallas_call(
        body,
        out_shape=(jax.ShapeDtypeStruct((T, W), f32), jax.ShapeDtypeStruct(dproj.shape, dproj.dtype), jax.ShapeDtypeStruct((1, hd), f32)),
        grid=(T // tb,),
        in_specs=[blk, blk, blk, pl.BlockSpec((tb, W), lambda i: (i, gate_blk)), row, _ANY_SPEC],
        out_specs=(blk, pl.BlockSpec((tb, W), lambda i: (i, gate_blk)), row),
        input_output_aliases={5: 1},
        compiler_params=_cparams(("arbitrary",)),
        name=name,
    )(dy, o_f, o_b, proj, w, dproj)


def _gla_prep(proj, w2f, w2b, b2f, b2b, tb=256):
    T = proj.shape[0]
    W = GLA_HEADS * GLA_DK

    def body(s_ref, wf_ref, wb_ref, bf_ref, bb_ref, gf_ref, gb_ref):
        small = s_ref[...]
        gkf = -_softplus(-(_mx(small, wf_ref[...], "nn") + bf_ref[...])) * (1.0 / GATE_NORMALIZER)
        gkb = -_softplus(-(_mx(small, wb_ref[...], "nn") + bb_ref[...])) * (1.0 / GATE_NORMALIZER)
        gf_ref[...] = _exact01(_chunk_tri(tb, False), gkf)
        gb_ref[...] = _exact01(_chunk_tri(tb, True), gkb)

    blk = pl.BlockSpec((tb, W), lambda i: (i, 0))
    wsp = pl.BlockSpec((LANES, W), lambda i: (0, 0))
    row = pl.BlockSpec((1, W), lambda i: (0, 0))
    return pl.pallas_call(
        body,
        out_shape=(jax.ShapeDtypeStruct((T, W), f32), jax.ShapeDtypeStruct((T, W), f32)),
        grid=(T // tb,),
        in_specs=[pl.BlockSpec((tb, LANES), lambda i: (i, SMALL_OFF // LANES)), wsp, wsp, row, row],
        out_specs=(blk, blk),
        compiler_params=_cparams(("parallel",)),
        name="gla_prep",
    )(proj, w2f, w2b, b2f, b2b)


def _gla_prep_bwd(proj, w2f, w2b, b2f, b2b, dgk_f, dgk_b):
    T = proj.shape[0]
    tb = min(ROW_TILE, T)
    W = GLA_HEADS * GLA_DK

    def body(s_ref, wf_ref, wb_ref, bf_ref, bb_ref, df_ref, db_ref, ds_ref, dwf_ref, dwb_ref, dbf_ref, dbb_ref):
        i = pl.program_id(0)
        small = s_ref[...]
        dsm = jnp.zeros((tb, LANES), f32)
        parts = []
        for w_ref, b_ref, d_ref in ((wf_ref, bf_ref, df_ref), (wb_ref, bb_ref, db_ref)):
            pre = _mx(small, w_ref[...], "nn") + b_ref[...]
            dpre = d_ref[...] * (1.0 / GATE_NORMALIZER) * _sigmoid(-pre)
            dsm = dsm + _mx(dpre, w_ref[...], "nt")
            parts.append((_mx(small, dpre, "tn"), jnp.sum(dpre, axis=0, keepdims=True)))
        ds_ref[...] = dsm

        @pl.when(i == 0)
        def _():
            dwf_ref[...], dbf_ref[...] = parts[0]
            dwb_ref[...], dbb_ref[...] = parts[1]

        @pl.when(i > 0)
        def _():
            dwf_ref[...] += parts[0][0]
            dbf_ref[...] += parts[0][1]
            dwb_ref[...] += parts[1][0]
            dbb_ref[...] += parts[1][1]

    blk = pl.BlockSpec((tb, W), lambda i: (i, 0))
    wsp = pl.BlockSpec((LANES, W), lambda i: (0, 0))
    row = pl.BlockSpec((1, W), lambda i: (0, 0))
    return pl.pallas_call(
        body,
        out_shape=(jax.ShapeDtypeStruct((T, LANES), f32), jax.ShapeDtypeStruct((LANES, W), f32), jax.ShapeDtypeStruct((LANES, W), f32),
                   jax.ShapeDtypeStruct((1, W), f32), jax.ShapeDtypeStruct((1, W), f32)),
        grid=(T // tb,),
        in_specs=[pl.BlockSpec((tb, LANES), lambda i: (i, SMALL_OFF // LANES)), wsp, wsp, row, row, blk, blk],
        out_specs=(pl.BlockSpec((tb, LANES), lambda i: (i, 0)), wsp, wsp, row, row),
        compiler_params=_cparams(("arbitrary",)),
        name="gla_prep_bwd",
    )(proj, w2f, w2b, b2f, b2b, dgk_f, dgk_b)


GATE_A_BLK, GATE_B_BLK, Z_BLK, G_B_BLK = 6144 // 1024, 7168 // 1024, 3072 // 1024, 8192 // 1024
DP_GATES_BLK, DP_GLA_QKV_BLK, DP_SMALL_BLK = 6144 // 2048, 4096 // 2048, SMALL_OFF // 512
_ANY_SPEC = pl.BlockSpec(memory_space=pl.ANY)


def _mid(og, obg, proj, x, target, wpg, wpl, wout, w_post, tb=256):
    T = x.shape[0]
    Dm = D_MODEL

    def body(og_ref, obg_ref, ga_ref, gb_ref, x_ref, t_ref, wpg_ref, wpl_ref, wo_ref, wp_ref,
             loss_ref, dwp_ref, dy_ref, m_ref, dout_ref, dya_ref, dyb_ref, dgates_ref, dog_ref, dobg_ref):
        i = pl.program_id(0)
        ya = _mx(og_ref[...], wpg_ref[...], "nn")
        yb = _mx(obg_ref[...], wpl_ref[...], "nn")
        sa, sb = _sigmoid(ga_ref[...]), _sigmoid(gb_ref[...])
        m = sa * ya + sb * yb
        m_ref[...] = m.astype(m_ref.dtype)
        out = _mx(m, wo_ref[...], "nn")
        r = lax.rsqrt(jnp.mean(out * out, axis=1, keepdims=True) + NORM_EPS)
        wp = wp_ref[...]
        e = x_ref[...] + out * r * wp - t_ref[...]
        dy = e * (1.0 / Dm)
        dy_ref[...] = dy
        lpart = jnp.sum(e * e, axis=0, keepdims=True)
        wpart = jnp.sum(dy * out * r, axis=0, keepdims=True)
        dyn = dy * wp
        dout = r * dyn - out * (r * r * r) * jnp.mean(dyn * out, axis=1, keepdims=True)
        dout_ref[...] = dout.astype(dout_ref.dtype)
        dm = _mx(dout, wo_ref[...], "nt")
        dya, dyb = dm * sa, dm * sb
        dya_ref[...] = dya.astype(dya_ref.dtype)
        dyb_ref[...] = dyb.astype(dyb_ref.dtype)
        dgates_ref[:, :Dm] = (dm * ya * sa * (1.0 - sa)).astype(dgates_ref.dtype)
        dgates_ref[:, Dm:] = (dm * yb * sb * (1.0 - sb)).astype(dgates_ref.dtype)
        dog_ref[...] = _mx(dya, wpg_ref[...], "nt")
        dobg_ref[...] = _mx(dyb, wpl_ref[...], "nt")

        @pl.when(i == 0)
        def _():
            loss_ref[...] = lpart
            dwp_ref[...] = wpart

        @pl.when(i > 0)
        def _():
            loss_ref[...] += lpart
            dwp_ref[...] += wpart

    blk = pl.BlockSpec((tb, Dm), lambda i: (i, 0))
    row = pl.BlockSpec((1, Dm), lambda i: (0, 0))
    wsp = pl.BlockSpec((Dm, Dm), lambda i: (0, 0))
    act = jax.ShapeDtypeStruct((T, Dm), MXU_DTYPE)
    big = jax.ShapeDtypeStruct((T, Dm), f32)
    vec = jax.ShapeDtypeStruct((1, Dm), f32)
    return pl.pallas_call(
        body,
        out_shape=(vec, vec, big, act, act, act, act, jax.ShapeDtypeStruct((T, N_CAT), MXU_DTYPE), big, big),
        grid=(T // tb,),
        in_specs=[blk, blk, pl.BlockSpec((tb, Dm), lambda i: (i, GATE_A_BLK)), pl.BlockSpec((tb, Dm), lambda i: (i, GATE_B_BLK)),
                  blk, blk, wsp, wsp, wsp, row],
        out_specs=(row, row) + (blk,) * 5 + (pl.BlockSpec((tb, 2 * Dm), lambda i: (i, DP_GATES_BLK)), blk, blk),
        compiler_params=_cparams(("arbitrary",)),
        name="mid",
    )(og, obg, proj, proj, x, target, wpg, wpl, wout, w_post)


def _heads(ref, lead, n, width, rows=slice(None)):
    return [ref[lead + (rows, slice(i * width, (i + 1) * width))] for i in range(n)]


class _GdnBatch:
    def __init__(self, q, k, v, gcol, grow, beta, nf, A=None):
        B, Cn = q.shape[0], q.shape[1]
        self.nf = nf
        self.incl, self.strict, eye = _dir_masks(B, nf, Cn)
        self.decay = jnp.where(self.incl, jnp.exp(jnp.where(self.incl, gcol - grow, 0.0)), 0.0)
        self.k, self.v, self.beta = k, v, beta
        self.qs = q * (GDN_DIM**-0.5)
        self.kb = k * beta
        self.L = jnp.where(self.strict, _mx(self.kb, k, "nt") * self.decay, 0.0)
        if A is None:
            A = _inv_unit_lower(self.L, jnp.where(eye, 1.0, 0.0).astype(f32))
        self.A = A
        self.A2 = _split2(A)
        self.eg = jnp.exp(gcol)
        self.kbg = self.kb * self.eg
        self.sol = _h3s(self.A2, _split2(jnp.concatenate([v * beta, self.kbg], axis=2)), "nn")
        self.u = self.sol[:, :, :GDN_DIM]
        self.w = self.sol[:, :, GDN_DIM:]
        self.attn = jnp.where(self.incl, _mx(self.qs, k, "nt") * self.decay, 0.0)
        self.qd = self.qs * self.eg
        gl = _last_row(gcol, nf)
        self.ekd = jnp.exp(gl - gcol)
        self.kd = k * self.ekd
        self.egl = jnp.exp(gl)


def _gdn_chain_inputs(qf_ref, qb_ref, gf_ref, gb_ref, cf_ref, cb_ref, b, hb):
    qkv = [jnp.stack(_heads(qf_ref, (i,), hb, GDN_DIM) + _heads(qb_ref, (i,), hb, GDN_DIM)) for i in range(3)]
    cols, rows, betas = [], [], []
    for d, (g_ref, c_ref) in enumerate(((gf_ref, cf_ref), (gb_ref, cb_ref))):
        gates, gc = g_ref[...], c_ref[...]
        for hh in range(hb):
            h = b * hb + hh
            col = _lane_col(gc, h + 8 * d)
            cols.append(col)
            rows.append(jnp.broadcast_to(col, (CHUNK, LANES)).T[0:1, :])
            betas.append(_lane_col(gates, 16 + h + 8 * d))
    return qkv[0], qkv[1], qkv[2], jnp.stack(cols), jnp.stack(rows), jnp.stack(betas)


GDN_CPS_FWD, GDN_CPS_BWD = 2, 1


def _chunk_views(refs, s_f, s_b):
    out = []
    for i, r in enumerate(refs):
        s = s_f if i % 2 == 0 else s_b
        rows = pl.ds(s * CHUNK, CHUNK)
        out.append(r.at[pl.ds(s, 1)] if len(r.shape) == 4 else (r.at[:, rows, :] if len(r.shape) == 3 else r.at[rows, :]))
    return out


def _gdn_specs(cps, hb):
    R, W = cps * CHUNK, hb * GDN_DIM
    qsp = lambda ci: pl.BlockSpec((3, R, W), lambda n, b: (0, ci(n), b))
    gsp = lambda ci: pl.BlockSpec((R, LANES), lambda n, b: (ci(n), 0))
    osp = lambda ci: pl.BlockSpec((R, W), lambda n, b: (ci(n), b))
    ssp = lambda ci: pl.BlockSpec((cps, hb, GDN_DIM, GDN_DIM), lambda n, b: (ci(n), b, 0, 0))
    asp = lambda ci: pl.BlockSpec((cps, hb, CHUNK, CHUNK), lambda n, b: (ci(n), b, 0, 0))
    return qsp, gsp, osp, ssp, asp


def _gdn_fwd(qkvn, gates, gc):
    T = qkvn.shape[1]
    N = T // CHUNK
    nh, hb = GDN_HEADS, GDN_HB
    cps = min(GDN_CPS_FWD, N)
    NB = N // cps

    def body(*refs):
        n, b = pl.program_id(0), pl.program_id(1)
        s_scr = refs[-1]

        @pl.when(n == 0)
        def _():
            s_scr[0, pl.ds(b * hb, hb)] = jnp.zeros((hb, GDN_DIM, GDN_DIM), f32)
            s_scr[1, pl.ds(b * hb, hb)] = jnp.zeros((hb, GDN_DIM, GDN_DIM), f32)

        for s in range(cps):
            chunk(*_chunk_views(refs[:-1], s, cps - 1 - s), s_scr, b)

    def chunk(qf_ref, qb_ref, gf_ref, gb_ref, cf_ref, cb_ref, of_ref, ob_ref, sf_ref, sb_ref, af_ref, ab_ref, s_scr, b):
        hs = pl.ds(b * hb, hb)
        q, k, v, gcol, grow, beta = _gdn_chain_inputs(qf_ref, qb_ref, gf_ref, gb_ref, cf_ref, cb_ref, b, hb)
        ck = _GdnBatch(q, k, v, gcol, grow, beta, hb)
        S = jnp.concatenate([s_scr[0, hs], s_scr[1, hs]], axis=0)
        sf_ref[0], sb_ref[0] = S[:hb], S[hb:]
        af_ref[0], ab_ref[0] = ck.A[:hb], ck.A[hb:]
        v_new = ck.u - _mx(ck.w, S, "nn")
        o = _mx(ck.qd, S, "nn") + _mx(ck.attn, v_new, "nn")
        S_new = S * ck.egl + _mx(ck.kd, v_new, "tn")
        for hh in range(hb):
            sl = slice(hh * GDN_DIM, (hh + 1) * GDN_DIM)
            of_ref[:, sl] = o[hh]
            ob_ref[:, sl] = o[hb + hh]
        s_scr[0, hs] = S_new[:hb]
        s_scr[1, hs] = S_new[hb:]

    fw, bw = (lambda n: n), (lambda n: NB - 1 - n)
    qsp, gsp, osp, ssp, asp = _gdn_specs(cps, hb)
    o_shape = jax.ShapeDtypeStruct((T, nh * GDN_DIM), f32)
    s_shape = jax.ShapeDtypeStruct((N, nh, GDN_DIM, GDN_DIM), f32)
    a_shape = jax.ShapeDtypeStruct((N, nh, CHUNK, CHUNK), f32)
    return pl.pallas_call(
        body,
        out_shape=(o_shape, o_shape, s_shape, s_shape, a_shape, a_shape),
        grid=(NB, nh // hb),
        in_specs=[qsp(fw), qsp(bw), gsp(fw), gsp(bw), gsp(fw), gsp(bw)],
        out_specs=(osp(fw), osp(bw), ssp(fw), ssp(bw), asp(fw), asp(bw)),
        scratch_shapes=[pltpu.VMEM((2, nh, GDN_DIM, GDN_DIM), f32)],
        compiler_params=_cparams(("arbitrary", "arbitrary")),
        name="gdn_scan_fwd",
    )(qkvn, qkvn, gates, gates, gc, gc)


def _gdn_bwd(qkvn, gates, gc, st_f, st_b, a_f, a_b, do):
    T = qkvn.shape[1]
    N = T // CHUNK
    nh, hb = GDN_HEADS, GDN_HB
    nb = nh // hb
    cps = min(GDN_CPS_BWD, N)
    NB = N // cps

    def body(*refs):
        n, b = pl.program_id(0), pl.program_id(1)
        ds_scr = refs[-1]

        @pl.when(n == 0)
        def _():
            ds_scr[0, pl.ds(b * hb, hb)] = jnp.zeros((hb, GDN_DIM, GDN_DIM), f32)
            ds_scr[1, pl.ds(b * hb, hb)] = jnp.zeros((hb, GDN_DIM, GDN_DIM), f32)

        for s in range(cps):
            chunk(*_chunk_views(refs[:-1], cps - 1 - s, s), ds_scr, b)

    def chunk(qf_ref, qb_ref, gf_ref, gb_ref, cf_ref, cb_ref, sf_ref, sb_ref, af_ref, ab_ref, dof_ref, dob_ref,
              dqf_ref, dqb_ref, dgf_ref, dgb_ref, ds_scr, b):
        hs = pl.ds(b * hb, hb)

        @pl.when(b == 0)
        def _():
            dgf_ref[...] = jnp.zeros((CHUNK, LANES), f32)
            dgb_ref[...] = jnp.zeros((CHUNK, LANES), f32)

        q, k, v, gcol, grow, beta = _gdn_chain_inputs(qf_ref, qb_ref, gf_ref, gb_ref, cf_ref, cb_ref, b, hb)
        ck = _GdnBatch(q, k, v, gcol, grow, beta, hb, A=jnp.concatenate([af_ref[0], ab_ref[0]], axis=0))
        S = jnp.concatenate([sf_ref[0], sb_ref[0]], axis=0)
        dS = jnp.concatenate([ds_scr[0, hs], ds_scr[1, hs]], axis=0)
        dov = jnp.stack(_heads(dof_ref, (), hb, GDN_DIM) + _heads(dob_ref, (), hb, GDN_DIM))
        v_new = ck.u - _mx(ck.w, S, "nn")
        d_vnew = _mx(ck.attn, dov, "tn") + _mx(ck.kd, dS, "nn")
        d_attn = jnp.where(ck.incl, _mx(dov, v_new, "nt"), 0.0)
        d_qd = _mx(dov, S, "nt")
        d_kd = _mx(v_new, dS, "nt")
        d_gl = jnp.sum(jnp.sum(dS * S, axis=2, keepdims=True), axis=1, keepdims=True) * ck.egl
        dS_new = dS * ck.egl + _mx(ck.qd, dov, "tn") - _mx(ck.w, d_vnew, "tn")
        ds_scr[0, hs] = dS_new[:hb]
        ds_scr[1, hs] = dS_new[hb:]
        d_w = -_mx(d_vnew, S, "nt")
        d_rhs = _h3s(ck.A2, _split2(jnp.concatenate([d_vnew, d_w], axis=2)), "tn")
        d_vb, d_kbg = d_rhs[:, :, :GDN_DIM], d_rhs[:, :, GDN_DIM:]
        dL = -jnp.where(ck.strict, _h3(d_rhs, ck.sol, "nt"), 0.0)
        P = dL * ck.decay
        Q = d_attn * ck.decay
        d_kb = _mx(P, k, "nn") + d_kbg * ck.eg
        d_qs = _mx(Q, k, "nn") + d_qd * ck.eg
        d_k = _mx(P, ck.kb, "tn") + _mx(Q, ck.qs, "tn") + d_kd * ck.ekd + d_kb * ck.beta
        E = dL * ck.L + d_attn * ck.attn
        kdsum = jnp.sum(d_kd * ck.kd, axis=2, keepdims=True)
        d_gc = (jnp.sum(E, axis=2, keepdims=True) - _colsum_col(E)
                + jnp.sum(d_kbg * ck.kbg, axis=2, keepdims=True)
                + jnp.sum(d_qd * ck.qd, axis=2, keepdims=True) - kdsum)
        d_gc = _add_at_last_row(d_gc, d_gl + jnp.sum(kdsum, axis=1, keepdims=True), hb)
        d_beta = jnp.sum(d_vb * v, axis=2, keepdims=True) + jnp.sum(d_kb * k, axis=2, keepdims=True)
        d_q = d_qs * (GDN_DIM**-0.5)
        d_v = d_vb * ck.beta
        lane = _iota2((CHUNK, LANES), 1)
        for d, (dq_ref, dg_ref) in enumerate(((dqf_ref, dgf_ref), (dqb_ref, dgb_ref))):
            acc = dg_ref[...]
            for hh in range(hb):
                c, h = d * hb + hh, b * hb + hh
                sl = slice(hh * GDN_DIM, (hh + 1) * GDN_DIM)
                dq_ref[0, :, sl] = d_q[c]
                dq_ref[1, :, sl] = d_k[c]
                dq_ref[2, :, sl] = d_v[c]
                acc = jnp.where(lane == h + 8 * d, d_gc[c], acc)
                acc = jnp.where(lane == 16 + h + 8 * d, d_beta[c], acc)
            dg_ref[...] = acc

        @pl.when(b == nb - 1)
        def _():
            for d, dg_ref in enumerate((dgf_ref, dgb_ref)):
                blk = dg_ref[...]
                dg_ref[...] = jnp.where(lane < 16, _exact01(_chunk_tri(CHUNK, d == 0), blk), blk)

    fw, bw = (lambda n: NB - 1 - n), (lambda n: n)
    qsp, gsp, osp, ssp, asp = _gdn_specs(cps, hb)
    dq_shape = jax.ShapeDtypeStruct((3, T, nh * GDN_DIM), f32)
    dg_shape = jax.ShapeDtypeStruct((T, LANES), f32)
    return pl.pallas_call(
        body,
        out_shape=(dq_shape, dq_shape, dg_shape, dg_shape),
        grid=(NB, nb),
        in_specs=[qsp(fw), qsp(bw), gsp(fw), gsp(bw), gsp(fw), gsp(bw), ssp(fw), ssp(bw), asp(fw), asp(bw), osp(fw), osp(bw)],
        out_specs=(qsp(fw), qsp(bw), gsp(fw), gsp(bw)),
        scratch_shapes=[pltpu.VMEM((2, nh, GDN_DIM, GDN_DIM), f32)],
        compiler_params=_cparams(("arbitrary", "arbitrary")),
        name="gdn_scan_bwd",
    )(qkvn, qkvn, gates, gates, gc, gc, st_f, st_b, a_f, a_b, do, do)


class _GlaBatch:
    def __init__(self, q, k, G, nf):
        B, Cn = q.shape[0], q.shape[1]
        self.incl, _, _ = _dir_masks(B, nf, Cn)
        self.eG = jnp.exp(G)
        self.enG = jnp.exp(-G)
        self.qg = q * (GLA_DK**-0.5) * self.eG
        self.kg = k * self.enG
        self.attn = jnp.where(self.incl, _mx(self.qg, self.kg, "nt"), 0.0)
        gl = _last_row(G, nf)
        self.ekd = jnp.exp(gl - G)
        self.kd = k * self.ekd
        self.egl = jnp.exp(gl)


GLA_CPS = 4


def _chunk_rows(s):
    return slice(s * CHUNK, (s + 1) * CHUNK)


def _gla_chain_inputs(refs_f, refs_b, rows_f, rows_b):
    nh = GLA_HEADS
    out = []
    for i, width in enumerate((GLA_DK, GLA_DK, GLA_DV, GLA_DK)):
        out.append(jnp.stack(_heads(refs_f[i], (), nh, width, rows_f) + _heads(refs_b[i], (), nh, width, rows_b)))
    return out


def _gla_specs(rows, ci):
    nh = GLA_HEADS
    return [
        pl.BlockSpec((rows, nh * GLA_DK), lambda n: (ci(n), 4096 // (nh * GLA_DK))),
        pl.BlockSpec((rows, nh * GLA_DK), lambda n: (ci(n), 4608 // (nh * GLA_DK))),
        pl.BlockSpec((rows, nh * GLA_DV), lambda n: (ci(n), 5120 // (nh * GLA_DV))),
        pl.BlockSpec((rows, nh * GLA_DK), lambda n: (ci(n), 0)),
    ]


def _gla_fwd(proj, G_f, G_b):
    T = proj.shape[0]
    N = T // CHUNK
    nh = GLA_HEADS
    cps = min(GLA_CPS, N)
    NB = N // cps

    def body(qf, kf, vf, gf, qb, kb, vb, gb, of_ref, ob_ref, sf_ref, sb_ref, s_scr):
        n = pl.program_id(0)

        @pl.when(n == 0)
        def _():
            s_scr[...] = jnp.zeros((2 * nh, GLA_DV, GLA_DK), f32)

        for s in range(cps):
            sb = cps - 1 - s
            q, k, v, G = _gla_chain_inputs((qf, kf, vf, gf), (qb, kb, vb, gb), _chunk_rows(s), _chunk_rows(sb))
            ck = _GlaBatch(q, k, G, nh)
            St = s_scr[...]
            sf_ref[s], sb_ref[sb] = St[:nh].astype(sf_ref.dtype), St[nh:].astype(sb_ref.dtype)
            o = _mx(ck.qg, St, "nt") + _mx(ck.attn, v, "nn")
            for i in range(nh):
                sl = slice(i * GLA_DV, (i + 1) * GLA_DV)
                of_ref[_chunk_rows(s), sl] = o[i]
                ob_ref[_chunk_rows(sb), sl] = o[nh + i]
            s_scr[...] = St * ck.egl + _mx(v, ck.kd, "tn")

    fw, bw = (lambda n: n), (lambda n: NB - 1 - n)
    osp = lambda ci: pl.BlockSpec((cps * CHUNK, nh * GLA_DV), lambda n: (ci(n), 0))
    ssp = lambda ci: pl.BlockSpec((cps, nh, GLA_DV, GLA_DK), lambda n: (ci(n), 0, 0, 0))
    o_shape = jax.ShapeDtypeStruct((T, nh * GLA_DV), f32)
    s_shape = jax.ShapeDtypeStruct((N, nh, GLA_DV, GLA_DK), MXU_DTYPE)
    return pl.pallas_call(
        body,
        out_shape=(o_shape, o_shape, s_shape, s_shape),
        grid=(NB,),
        in_specs=_gla_specs(cps * CHUNK, fw) + _gla_specs(cps * CHUNK, bw),
        out_specs=(osp(fw), osp(bw), ssp(fw), ssp(bw)),
        scratch_shapes=[pltpu.VMEM((2 * nh, GLA_DV, GLA_DK), f32)],
        compiler_params=_cparams(("arbitrary",)),
        name="gla_scan_fwd",
    )(proj, proj, proj, G_f, proj, proj, proj, G_b)


def _gla_bwd(proj, G_f, G_b, st_f, st_b, do):
    T = proj.shape[0]
    N = T // CHUNK
    nh = GLA_HEADS
    cps = min(GLA_CPS, N)
    NB = N // cps

    def body(qf, kf, vf, gf, sf_ref, dof_ref, qb, kb, vb, gb, sb_ref, dob_ref,
             dqf_ref, dkf_ref, dvf_ref, dgf_ref, dqb_ref, dkb_ref, dvb_ref, dgb_ref, ds_scr):
        n = pl.program_id(0)

        @pl.when(n == 0)
        def _():
            ds_scr[...] = jnp.zeros((2 * nh, GLA_DV, GLA_DK), f32)

        shp = (2 * nh, CHUNK, CHUNK)
        bb, ii, jj = _iota3(shp, 0), _iota3(shp, 1), _iota3(shp, 2)
        tri = jnp.where(jnp.logical_or(jnp.logical_and(bb < nh, jj >= ii), jnp.logical_and(bb >= nh, jj <= ii)), 1.0, 0.0)
        for s in range(cps):
            sf = cps - 1 - s
            rows = (_chunk_rows(sf), _chunk_rows(s))
            q, k, v, G = _gla_chain_inputs((qf, kf, vf, gf), (qb, kb, vb, gb), *rows)
            ck = _GlaBatch(q, k, G, nh)
            St = jnp.concatenate([sf_ref[sf], sb_ref[s]], axis=0).astype(f32)
            dSt = ds_scr[...]
            dov = jnp.stack(_heads(dof_ref, (), nh, GLA_DV, rows[0]) + _heads(dob_ref, (), nh, GLA_DV, rows[1]))
            d_attn = jnp.where(ck.incl, _mx(dov, v, "nt"), 0.0)
            d_qg = _mx(dov, St, "nn") + _mx(d_attn, ck.kg, "nn")
            d_kg = _mx(d_attn, ck.qg, "tn")
            d_v = _mx(ck.attn, dov, "tn") + _mx(ck.kd, dSt, "nt")
            d_kd = _mx(v, dSt, "nn")
            ds_scr[...] = dSt * ck.egl + _mx(dov, ck.qg, "tn")
            kdd = d_kd * ck.kd
            d_gl = jnp.sum(dSt * St, axis=1, keepdims=True) * ck.egl + jnp.sum(kdd, axis=1, keepdims=True)
            d_q = d_qg * ck.eG * (GLA_DK**-0.5)
            d_k = d_kg * ck.enG + d_kd * ck.ekd
            d_gk = _exact01(tri, _add_at_last_row(d_qg * ck.qg - d_kg * ck.kg - kdd, d_gl, nh))
            for d, (dq_ref, dk_ref, dv_ref, dg_ref) in enumerate(((dqf_ref, dkf_ref, dvf_ref, dgf_ref), (dqb_ref, dkb_ref, dvb_ref, dgb_ref))):
                for i in range(nh):
                    c = d * nh + i
                    ks, vs = slice(i * GLA_DK, (i + 1) * GLA_DK), slice(i * GLA_DV, (i + 1) * GLA_DV)
                    dq_ref[rows[d], ks] = d_q[c]
                    dk_ref[rows[d], ks] = d_k[c]
                    dv_ref[rows[d], vs] = d_v[c]
                    dg_ref[rows[d], ks] = d_gk[c]

    R = cps * CHUNK

    def in_specs(ci):
        return _gla_specs(R, ci) + [
            pl.BlockSpec((cps, nh, GLA_DV, GLA_DK), lambda n: (ci(n), 0, 0, 0)),
            pl.BlockSpec((R, nh * GLA_DV), lambda n: (ci(n), 0)),
        ]

    def out_specs(ci):
        ksp = pl.BlockSpec((R, nh * GLA_DK), lambda n: (ci(n), 0))
        return [ksp, ksp, pl.BlockSpec((R, nh * GLA_DV), lambda n: (ci(n), 0)), ksp]

    fw, bw = (lambda n: NB - 1 - n), (lambda n: n)
    k_shape = jax.ShapeDtypeStruct((T, nh * GLA_DK), f32)
    v_shape = jax.ShapeDtypeStruct((T, nh * GLA_DV), f32)
    return pl.pallas_call(
        body,
        out_shape=(k_shape, k_shape, v_shape, k_shape) * 2,
        grid=(NB,),
        in_specs=in_specs(fw) + in_specs(bw),
        out_specs=tuple(out_specs(fw) + out_specs(bw)),
        scratch_shapes=[pltpu.VMEM((2 * nh, GLA_DV, GLA_DK), f32)],
        compiler_params=_cparams(("arbitrary",)),
        name="gla_scan_bwd",
    )(proj, proj, proj, G_f, st_f, do, proj, proj, proj, G_b, st_b, do)


def _gla_dsum(dq_f, dq_b, dk_f, dk_b, dv_f, dv_b, dproj):
    T = dq_f.shape[0]
    tb = min(ROW_TILE, T)
    wk, wv = GLA_HEADS * GLA_DK, GLA_HEADS * GLA_DV

    def body(qf, qb, kf, kb, vf, vb, _, o_ref):
        o_ref[:, :wk] = (qf[...] + qb[...]).astype(o_ref.dtype)
        o_ref[:, wk : 2 * wk] = (kf[...] + kb[...]).astype(o_ref.dtype)
        o_ref[:, 2 * wk :] = (vf[...] + vb[...]).astype(o_ref.dtype)

    ksp = pl.BlockSpec((tb, wk), lambda i: (i, 0))
    vsp = pl.BlockSpec((tb, wv), lambda i: (i, 0))
    return pl.pallas_call(
        body,
        out_shape=jax.ShapeDtypeStruct(dproj.shape, dproj.dtype),
        grid=(T // tb,),
        in_specs=[ksp, ksp, ksp, ksp, vsp, vsp, _ANY_SPEC],
        out_specs=pl.BlockSpec((tb, 2 * wk + wv), lambda i: (i, DP_GLA_QKV_BLK)),
        input_output_aliases={6: 0},
        compiler_params=_cparams(("parallel",)),
        name="gla_dsum",
    )(dq_f, dq_b, dk_f, dk_b, dv_f, dv_b, dproj)


def _local_step(x, target, wcat, conv_w, avec, dvec, gdn_norm_w, w2f, w2b, b2f, b2b, gla_norm_w, proj_weights, w_pre, w_post,
                send_proj_grads=None, send_in_grads=None):
    h, h_t = _rms_pre(x, w_pre)
    proj = _matmul(h, wcat, "nt", f32, "proj_in", 2048, 512, 1024)
    qkvn = _gdn_prep(proj, conv_w)
    gates, gc = _gdn_gates(proj, avec, dvec)
    oa_f, oa_b, sa_f, sa_b, inv_f, inv_b = _gdn_fwd(qkvn, gates, gc)
    og = _out_norm(oa_f, oa_b, proj, Z_BLK, gdn_norm_w, GDN_DIM, "gdn_out")
    G_f, G_b = _gla_prep(proj, w2f, w2b, b2f, b2b)
    ob_f, ob_b, sb_f, sb_b = _gla_fwd(proj, G_f, G_b)
    obg = _out_norm(ob_f, ob_b, proj, G_B_BLK, gla_norm_w, GLA_DV, "gla_out")
    wpg, wpl, wout = proj_weights(obg)
    (loss_row, d_wpost, dy, m, dout, dya, dyb, dproj, dog, dobg) = _mid(og, obg, proj, x, target, wpg, wpl, wout, w_post)

    d_wout = _matmul(m, dout, "tn", MXU_DTYPE, "dw_out", 1024, 512, 2048)
    d_wpg = _matmul(og, dya, "tn", MXU_DTYPE, "dw_proj_gdn", 1024, 512, 2048)
    d_wpl = _matmul(obg, dyb, "tn", MXU_DTYPE, "dw_proj_gla", 1024, 512, 2048)
    if send_proj_grads is not None:
        gla_norm_w = gla_norm_w + send_proj_grads(d_wpg, d_wpl, d_wout)

    do_b, dproj, d_gla_norm = _out_norm_bwd(dobg, ob_f, ob_b, proj, G_B_BLK, gla_norm_w, GLA_DV, "gla_out_bwd", dproj)
    dq_f, dk_f, dv_f, dgk_f, dq_b, dk_b, dv_b, dgk_b = _gla_bwd(proj, G_f, G_b, sb_f, sb_b, do_b)
    dproj = _gla_dsum(dq_f, dq_b, dk_f, dk_b, dv_f, dv_b, dproj)
    dsmall_gla, d_w2f, d_w2b, d_b2f, d_b2b = _gla_prep_bwd(proj, w2f, w2b, b2f, b2b, dgk_f, dgk_b)

    do_a, dproj, d_gdn_norm = _out_norm_bwd(dog, oa_f, oa_b, proj, Z_BLK, gdn_norm_w, GDN_DIM, "gdn_out_bwd", dproj)
    dqkv_f, dqkv_b, dg_f, dg_b = _gdn_bwd(qkvn, gates, gc, sa_f, sa_b, inv_f, inv_b, do_a)
    dproj, d_conv = _gdn_prep_bwd(proj, conv_w, dqkv_f, dqkv_b, dproj)
    dproj, d_alog, d_dtb = _gdn_gates_bwd(proj, avec, dvec, dg_f, dg_b, dsmall_gla, dproj)

    d_wcat = _matmul(h_t, dproj, "nn", MXU_DTYPE, "dw_in", 1024, 512, 2048, out_t=True)
    started = None if send_in_grads is None else send_in_grads(d_wcat)
    dh = _matmul(dproj, wcat, "nn", f32, "dh", 1024, 1024, 2432, after=started)
    grad_x, d_wpre = _rms_pre_bwd(x, w_pre, dh, dy)
    return dict(loss_row=loss_row, grad_x=grad_x, d_wcat=d_wcat, d_conv=d_conv, d_alog=d_alog, d_dtb=d_dtb,
                d_gdn_norm=d_gdn_norm, d_wpg=d_wpg, d_w2f=d_w2f, d_w2b=d_w2b, d_b2f=d_b2f, d_b2b=d_b2b,
                d_gla_norm=d_gla_norm, d_wpl=d_wpl, d_wout=d_wout, d_wpost=d_wpost, d_wpre=d_wpre)


def _to_cat(wf):
    pad = jnp.zeros((N_CAT - N_IN,) + wf.shape[1:], wf.dtype)
    return jnp.concatenate([wf[:4096], wf[4128:6176], wf[7232:9280], wf[6176:7200], wf[4096:4128], wf[7200:7232], pad], axis=0)


def _from_cat(wc):
    return jnp.concatenate([wc[:4096], wc[9216:9248], wc[4096:6144], wc[8192:9216], wc[9248:9280], wc[6144:8192]], axis=0)


def _lane_row(*pieces):
    row = jnp.concatenate(pieces, axis=1)
    return jnp.pad(row, ((0, 0), (0, LANES - row.shape[1])))


def _pad_w2(w2, row0):
    return jnp.pad(w2, ((row0, LANES - row0 - w2.shape[0]), (0, 0)))


SM_ROWS, SM_COLS = 32, 384
REP_ROWS = 32


def _pack_small_shard(conv, w2f, w2b):
    lead = conv.shape[:-2]
    z = lambda r, c: jnp.zeros(lead + (r, c), f32)
    top = jnp.concatenate([conv, z(8 - CONV_K, SM_COLS)], axis=-2)
    mid = jnp.concatenate([w2f, w2b, z(16, SM_COLS - 128)], axis=-1)
    return jnp.concatenate([top, mid, z(SM_ROWS - 24, SM_COLS)], axis=-2)


LOSS_ROW = 28


def _pack_rep(ln_pre, ln_post, b2f, b2b, gla_norm, gdn_norm, alog_f, alog_b, dt_f, dt_b, loss=None):
    last = _lane_row(alog_f, alog_b, dt_f, dt_b)
    extra = jnp.zeros((1, LANES), f32) if loss is None else _lane_row(loss)
    rows = [ln_pre.reshape(8, LANES), ln_post.reshape(8, LANES), b2f.reshape(4, LANES), b2b.reshape(4, LANES),
            gla_norm.reshape(2, LANES), gdn_norm.reshape(1, LANES), last, extra, jnp.zeros((REP_ROWS - LOSS_ROW - 1, LANES), f32)]
    return jnp.concatenate(rows, axis=0)


def _unpack_rep(p):
    return dict(ln_pre_w=p[0:8].reshape(1, 1024), ln_post_w=p[8:16].reshape(1, 1024), gk_b2_fwd=p[16:20].reshape(1, 512),
                gk_b2_bwd=p[20:24].reshape(1, 512), gla_norm_w=p[24:26].reshape(1, 256), gdn_norm_w=p[26:27],
                a_log_fwd=p[27:28, 0:8], a_log_bwd=p[27:28, 8:16], dt_bias_fwd=p[27:28, 16:24], dt_bias_bwd=p[27:28, 24:32])


_MESH = pl.DeviceIdType.MESH


N_BIG_COPIES, N_SMALL_COPIES = 9, 7


def _gather_weights(w_in_s, sm_s):
    shapes = (w_in_s.shape, sm_s.shape)
    dtypes = (MXU_DTYPE, f32)

    def body(win_ref, sm_ref, gin_ref, gsm_ref, send_sems, recv_sems):
        x, y, c = lax.axis_index("x"), lax.axis_index("y"), lax.axis_index("c")
        me, sibling = (x, y, c), (x, y, 1 - c)
        chips = [(1 - x, y), (x, 1 - y), (1 - x, 1 - y)]

        def idx(px, py, pc):
            return 4 * px + 2 * py + pc

        for r in range(w_in_s.shape[1] // LANES):
            cols = slice(r * LANES, (r + 1) * LANES)
            gin_ref[idx(*me), :, cols] = win_ref[:, cols].astype(MXU_DTYPE)
        gsm_ref[idx(*me)] = sm_ref[...]

        def copy(sem, blk, to):
            return pltpu.make_async_remote_copy(src_ref=blk, dst_ref=blk, send_sem=send_sems.at[sem], recv_sem=recv_sems.at[sem],
                                                device_id=to, device_id_type=_MESH)

        def small(k, block, to):
            return copy(N_BIG_COPIES + k, gsm_ref.at[idx(*block)], to)

        half = w_in_s.shape[1] // 2
        xn, yn, dg = (1 - x, y), (x, 1 - y), (1 - x, 1 - y)

        def big(sem, block, to, part=None):
            blk = gin_ref.at[idx(*block)]
            return copy(sem, blk if part is None else blk.at[:, pl.ds(part * half, half)], to)

        started = [big(0, me, sibling), big(1, me, (*xn, c)), big(2, me, (*yn, c)), small(0, me, sibling)]
        started += [small(1 + j, me, (*chip, c)) for j, chip in enumerate(chips)]
        for cp in started:
            cp.start()

        def then(arrived, forwards):
            arrived.wait_recv()
            for cp in forwards:
                cp.start()
            started.extend(forwards)

        then(big(1, (*xn, c), me), [big(3, (*xn, c), (*yn, c), 1), big(5, (*xn, c), sibling)])
        then(big(2, (*yn, c), me), [big(4, (*yn, c), (*xn, c), 0), big(6, (*yn, c), sibling)])
        then(big(4, (*dg, c), me, 0), [big(7, (*dg, c), sibling, 0)])
        then(big(3, (*dg, c), me, 1), [big(8, (*dg, c), sibling, 1)])
        for j, chip in enumerate(chips):
            then(small(1 + j, (*chip, c), me), [small(4 + j, (*chip, c), sibling)])
        big(0, sibling, me).wait_recv()
        big(5, (*xn, 1 - c), me).wait_recv()
        big(6, (*yn, 1 - c), me).wait_recv()
        big(7, (*dg, 1 - c), me, 0).wait_recv()
        big(8, (*dg, 1 - c), me, 1).wait_recv()
        small(0, sibling, me).wait_recv()
        for j, chip in enumerate(chips):
            small(4 + j, (*chip, 1 - c), me).wait_recv()
        for cp in started:
            cp.wait_send()

    vm = pl.BlockSpec(memory_space=pltpu.VMEM)
    return pl.pallas_call(
        body,
        out_shape=tuple(jax.ShapeDtypeStruct((N_DEV,) + s, d) for s, d in zip(shapes, dtypes)),
        in_specs=[vm, vm],
        out_specs=(vm, vm),
        scratch_shapes=[pltpu.SemaphoreType.DMA((N_BIG_COPIES + N_SMALL_COPIES,)), pltpu.SemaphoreType.DMA((N_BIG_COPIES + N_SMALL_COPIES,))],
        compiler_params=pltpu.CompilerParams(vmem_limit_bytes=VMEM_LIMIT),
        name="gather_weights",
    )(w_in_s, sm_s)


N_CHIP = 4
_EFFECT = pltpu.SideEffectType.DATAFLOW_SIDE_EFFECTING
_HBM_SPEC = pl.BlockSpec(memory_space=pltpu.HBM)
_SEM_SPEC = pl.BlockSpec(memory_space=pltpu.SEMAPHORE)


def _relation_peer(k, x, y, c):
    px = (1 - x) if (k & 4) else x
    py = (1 - y) if (k & 2) else y
    pc = (1 - c) if (k & 1) else c
    return (px, py, pc), 4 * px + 2 * py + pc


_PLAN_COPIES = {"gather": N_DEV - 1, "blocks": N_DEV - 1, "sibling": 4, "chips": 3}
_PLAN_SLOTS = {"gather": N_DEV, "blocks": N_DEV, "sibling": 4, "chips": 4}


def _plan_copies(plan, src_ref, land_ref, send_sems, recv_sems, arrival):
    x, y, c = lax.axis_index("x"), lax.axis_index("y"), lax.axis_index("c")
    me, my_chip = 4 * x + 2 * y + c, 2 * x + y
    out = []

    def add(i, src, there, here, dev):
        out.append(pltpu.make_async_remote_copy(src_ref=src, dst_ref=land_ref.at[here if arrival else there], send_sem=send_sems.at[i],
                                                recv_sem=recv_sems.at[i], device_id=dev, device_id_type=_MESH))

    if plan in ("gather", "blocks"):
        for k in range(1, N_DEV):
            dev, p = _relation_peer(k, x, y, c)
            add(k - 1, src_ref.at[p] if plan == "blocks" else src_ref, me, p, dev)
    elif plan == "sibling":
        for j in range(4):
            add(j, src_ref.at[2 * j + 1 - c], j, j, (x, y, 1 - c))
    else:
        for i, k in enumerate((2, 4, 6)):
            (px, py, pc), _ = _relation_peer(k, x, y, c)
            add(i, src_ref.at[2 * px + py], my_chip, 2 * px + py, (px, py, pc))
    return out


def _exchange_start(src, plan, name, after=None):
    n = _PLAN_COPIES[plan]
    land = lax.empty((_PLAN_SLOTS[plan],) + (src.shape if plan == "gather" else src.shape[1:]), src.dtype)
    extra = [] if after is None else [after]

    def body(src_ref, land_ref, *rest):
        send_sems, recv_sems, _, _, token = rest[len(extra) :]
        for cp in _plan_copies(plan, src_ref, land_ref, send_sems, recv_sems, False):
            cp.start()
        token[...] = jnp.zeros_like(token)

    return pl.pallas_call(
        body,
        name=name,
        out_shape=(pltpu.SemaphoreType.DMA((n,)), pltpu.SemaphoreType.DMA((n,)), pltpu.HBM(src.shape, src.dtype),
                   pltpu.HBM(land.shape, land.dtype), jax.ShapeDtypeStruct((8, LANES), f32)),
        in_specs=(_HBM_SPEC, _HBM_SPEC) + (_ANY_SPEC,) * len(extra),
        out_specs=(_SEM_SPEC, _SEM_SPEC, _HBM_SPEC, _HBM_SPEC, pl.BlockSpec(memory_space=pltpu.VMEM)),
        input_output_aliases={0: 2, 1: 3},
        compiler_params=pltpu.CompilerParams(has_side_effects=_EFFECT),
    )(pltpu.with_memory_space_constraint(src, pltpu.HBM), pltpu.with_memory_space_constraint(land, pltpu.HBM), *extra)


def _exchange_wait(started, after, plan, name):
    send_sems, recv_sems, src_thru, land_thru, _ = started

    def body(src_ref, land_ref, send_sems, recv_sems, after_ref, src_dead, got_ref):
        for cp in _plan_copies(plan, src_ref, land_ref, send_sems, recv_sems, True):
            cp.wait_send()
            cp.wait_recv()

    return pl.pallas_call(
        body,
        name=name,
        out_shape=(pltpu.HBM(src_thru.shape, src_thru.dtype), pltpu.HBM(land_thru.shape, land_thru.dtype)),
        in_specs=(_HBM_SPEC, _HBM_SPEC, _SEM_SPEC, _SEM_SPEC, pl.BlockSpec(memory_space=pl.ANY)),
        out_specs=(_HBM_SPEC, _HBM_SPEC),
        input_output_aliases={0: 0, 1: 1},
        compiler_params=pltpu.CompilerParams(has_side_effects=_EFFECT),
    )(src_thru, land_thru, send_sems, recv_sems, after)


def _pair_sum(g, r, name, tc=LANES):
    _, R, Cc = g.shape

    def body(g_ref, r_ref, o_ref):
        o_ref[...] = (g_ref[...].astype(f32) + r_ref[...].astype(f32)).astype(o_ref.dtype)

    return pl.pallas_call(
        body,
        out_shape=jax.ShapeDtypeStruct(r.shape, r.dtype),
        grid=(N_CHIP, Cc // tc),
        in_specs=[pl.BlockSpec((1, R, tc), lambda k, i: (2 * k + lax.axis_index("c"), 0, i)), pl.BlockSpec((1, R, tc), lambda k, i: (k, 0, i))],
        out_specs=pl.BlockSpec((1, R, tc), lambda k, i: (k, 0, i)),
        compiler_params=_cparams(("parallel", "parallel")),
        name=name,
    )(g, r)


def _sum_adam(parts, w, m, v, name, tb, tc=None, own=None, own_slot=None):
    R, Cc = w.shape
    n_parts = parts.shape[0]
    tb = R if tc else min(tb, R)
    tc = tc or Cc
    assert R % tb == 0 and Cc % tc == 0
    bc1 = 1.0 - ADAM_B1**ADAM_STEP
    bc2 = 1.0 - ADAM_B2**ADAM_STEP
    extra = [] if own is None else [own]

    def body(p_ref, w_ref, m_ref, v_ref, *rest):
        g_ref, d_ref, nm_ref, nv_ref = rest[len(extra) :]

        def part(j):
            pj = p_ref[j].astype(f32)
            return pj if own is None else jnp.where(own_slot() == j, rest[0][0].astype(f32), pj)

        g = part(0)
        for j in range(1, n_parts):
            g = g + part(j)
        g_ref[...] = g
        m2 = ADAM_B1 * m_ref[...] + (1.0 - ADAM_B1) * g
        v2 = ADAM_B2 * v_ref[...] + (1.0 - ADAM_B2) * (g * g)
        nm_ref[...] = m2
        nv_ref[...] = v2
        d_ref[...] = -ADAM_LR * ((m2 / bc1) / (jnp.sqrt(v2 / bc2) + ADAM_EPS) + ADAM_WD * w_ref[...])

    blk = pl.BlockSpec((tb, tc), lambda i, j: (i, j))
    o = jax.ShapeDtypeStruct((R, Cc), f32)
    return pl.pallas_call(
        body,
        out_shape=(o, o, o, o),
        grid=(R // tb, Cc // tc),
        in_specs=[pl.BlockSpec((n_parts, tb, tc), lambda i, j: (0, i, j)), blk, blk, blk]
        + [pl.BlockSpec((1, tb, tc), lambda i, j: (own_slot(), i, j))] * len(extra),
        out_specs=(blk, blk, blk, blk),
        compiler_params=_cparams(("parallel", "parallel")),
        name=name,
    )(parts, w, m, v, *extra)


def _my_device():
    return 4 * lax.axis_index("x") + 2 * lax.axis_index("y") + lax.axis_index("c")


def _my_chip():
    return 2 * lax.axis_index("x") + lax.axis_index("y")


_WEIGHTS = ["ln_pre_w", "w_in", "conv_w", "a_log_fwd", "a_log_bwd", "dt_bias_fwd", "dt_bias_bwd", "gdn_norm_w", "w_proj_gdn",
            "gk_w2_fwd", "gk_b2_fwd", "gk_w2_bwd", "gk_b2_bwd", "gla_norm_w", "w_proj_gla", "w_out", "ln_post_w"]
_REP_ORDER = ["ln_pre_w", "ln_post_w", "gk_b2_fwd", "gk_b2_bwd", "gla_norm_w", "gdn_norm_w", "a_log_fwd", "a_log_bwd", "dt_bias_fwd", "dt_bias_bwd"]


def kernel(x, ln_pre_w, w_in, conv_w, a_log_fwd, a_log_bwd, dt_bias_fwd, dt_bias_bwd, gdn_norm_w, w_proj_gdn, gk_w2_fwd, gk_b2_fwd, gk_w2_bwd, gk_b2_bwd, gla_norm_w, w_proj_gla, w_out, ln_post_w, loss_target, m_ln_pre_w, m_w_in, m_conv_w, m_a_log_fwd, m_a_log_bwd, m_dt_bias_fwd, m_dt_bias_bwd, m_gdn_norm_w, m_w_proj_gdn, m_gk_w2_fwd, m_gk_b2_fwd, m_gk_w2_bwd, m_gk_b2_bwd, m_gla_norm_w, m_w_proj_gla, m_w_out, m_ln_post_w, v_ln_pre_w, v_w_in, v_conv_w, v_a_log_fwd, v_a_log_bwd, v_dt_bias_fwd, v_dt_bias_bwd, v_gdn_norm_w, v_w_proj_gdn, v_gk_w2_fwd, v_gk_b2_fwd, v_gk_w2_bwd, v_gk_b2_bwd, v_gla_norm_w, v_w_proj_gla, v_w_out, v_ln_post_w):
    args = locals()
    W = {n: args[n] for n in _WEIGHTS}
    M = {n: args["m_" + n] for n in _WEIGHTS}
    V = {n: args["v_" + n] for n in _WEIGHTS}

    wp_stack = lambda P: jnp.concatenate([P["w_proj_gdn"], P["w_proj_gla"], P["w_out"]], axis=0)
    sm_pack = lambda P: _pack_small_shard(P["conv_w"][0], P["gk_w2_fwd"][0], P["gk_w2_bwd"][0])
    me = 4 * lax.axis_index("x") + 2 * lax.axis_index("y") + lax.axis_index("c")
    w_in_t, m_in_t, v_in_t = (jnp.transpose(a[0]) for a in (w_in, m_w_in, v_w_in))
    g_in, g_sm = _gather_weights(w_in_t, sm_pack(W))
    wp_own = wp_stack(W).astype(MXU_DTYPE)
    wp_started = _exchange_start(wp_own, "gather", "gather_proj_start", after=g_sm)

    def proj_weights(after):
        own, land = _exchange_wait(wp_started, after, "gather", "gather_proj_wait")
        g_p = lax.dynamic_update_slice(land, own[None], (me, 0, 0, 0))
        return tuple(g_p[:, i].reshape(D_MODEL, D_MODEL) for i in range(3))

    rows_p = 3 * D_MODEL // N_DEV
    sent = {}

    def send_proj_grads(d_wpg, d_wpl, d_wout):
        p_p = jnp.stack([d_wpg, d_wpl, d_wout]).reshape(3, N_DEV, D_MODEL // N_DEV, D_MODEL).transpose(1, 0, 2, 3)
        sent["p_p"] = p_p.reshape(N_DEV, rows_p, D_MODEL)
        sent["started"] = _exchange_start(sent["p_p"], "blocks", "exchange_proj_start")
        return sent["started"][4][0:1, 0:1]

    def send_in_grads(d_wcat):
        p_in = _from_cat(d_wcat).reshape(N_DEV, SHARD_IN, D_MODEL)
        sib = _exchange_start(p_in, "sibling", "exchange_in_sibling_start")
        p_p, land_p = _exchange_wait(sent["started"], sib[4], "blocks", "exchange_proj_wait")
        sent["o_p"] = _sum_adam(land_p, wp_stack(W).reshape(rows_p, D_MODEL), wp_stack(M).reshape(rows_p, D_MODEL),
                                wp_stack(V).reshape(rows_p, D_MODEL), "adam_w_proj", 128, own=p_p, own_slot=_my_device)
        p_in, sib_land = _exchange_wait(sib, sent["o_p"][0], "sibling", "exchange_in_sibling_wait")
        sent["chips"] = _exchange_start(_pair_sum(p_in, sib_land, "pair_sum_w_in", tc=512), "chips", "exchange_in_chips_start")
        return sent["chips"][4]

    wcat = _to_cat(g_in.reshape(N_IN, D_MODEL))
    conv_full = g_sm[:, 0:CONV_K, :].transpose(1, 0, 2).reshape(CONV_K, N_DEV * SM_COLS)
    w2f_full = g_sm[:, 8:24, 0:64].transpose(1, 0, 2).reshape(16, 512)
    w2b_full = g_sm[:, 8:24, 64:128].transpose(1, 0, 2).reshape(16, 512)

    avec = _lane_row(a_log_fwd, a_log_bwd)
    dvec = _lane_row(dt_bias_fwd, dt_bias_bwd)
    G = _local_step(x[0], loss_target[0], wcat, conv_full, avec, dvec, gdn_norm_w, _pad_w2(w2f_full, 32), _pad_w2(w2b_full, 48),
                    gk_b2_fwd, gk_b2_bwd, gla_norm_w, proj_weights, ln_pre_w + wp_started[4][0:1, 0:1], ln_post_w, send_proj_grads,
                    send_in_grads)

    p_sm = _pack_small_shard(G["d_conv"].reshape(CONV_K, N_DEV, SM_COLS).transpose(1, 0, 2),
                             G["d_w2f"][32:48].reshape(16, N_DEV, 64).transpose(1, 0, 2),
                             G["d_w2b"][48:64].reshape(16, N_DEV, 64).transpose(1, 0, 2))
    p_rep = _pack_rep(G["d_wpre"], G["d_wpost"], G["d_b2f"], G["d_b2b"], G["d_gla_norm"], G["d_gdn_norm"],
                      G["d_alog"][:, 0:8], G["d_alog"][:, 8:16], G["d_dtb"][:, 0:8], G["d_dtb"][:, 8:16],
                      loss=0.5 * jnp.sum(G["loss_row"], axis=1, keepdims=True) / D_MODEL)
    p_small = jnp.concatenate([p_sm, jnp.broadcast_to(p_rep[None], (N_DEV,) + p_rep.shape)], axis=2)
    small_started = _exchange_start(p_small, "blocks", "exchange_small_start")
    q_in, land_in = _exchange_wait(sent["chips"], small_started[4], "chips", "exchange_in_chips_wait")
    o_in = _sum_adam(land_in, w_in_t, m_in_t, v_in_t, "adam_w_in", SHARD_IN, tc=256, own=q_in, own_slot=_my_chip)
    o_p = sent["o_p"]
    p_small, land_small = _exchange_wait(small_started, o_in[0], "blocks", "exchange_small_wait")
    rep_pack = lambda P: _pack_rep(*[P[n] for n in _REP_ORDER])
    small_pack = lambda P: jnp.concatenate([sm_pack(P), rep_pack(P)], axis=1)
    o_small = _sum_adam(land_small, small_pack(W), small_pack(M), small_pack(V), "adam_small", SM_ROWS, own=p_small, own_slot=_my_device)
    o_sm, o_rep = [o[:, :SM_COLS] for o in o_small], [o[:, SM_COLS:] for o in o_small]

    res = []
    for kind in range(4):
        rep = _unpack_rep(o_rep[kind])
        per = dict(rep)
        per["w_in"] = jnp.transpose(o_in[kind])[None]
        pp = o_p[kind].reshape(3, 1, D_MODEL // N_DEV, D_MODEL)
        per["w_proj_gdn"], per["w_proj_gla"], per["w_out"] = pp[0], pp[1], pp[2]
        per["conv_w"] = o_sm[kind][None, 0:CONV_K, :]
        per["gk_w2_fwd"] = o_sm[kind][None, 8:24, 0:64]
        per["gk_w2_bwd"] = o_sm[kind][None, 8:24, 64:128]
        res.append([per[n] for n in _WEIGHTS])
    loss = o_rep[0][LOSS_ROW, 0]
    return (loss, G["grad_x"][None], *res[0], *res[1], *res[2], *res[3])
```

```python
import jax
import jax.numpy as jnp
from jax import lax
from jax.experimental import pallas as pl
from jax.experimental.pallas import tpu as pltpu

f32 = jnp.float32
MXU_DTYPE = jnp.bfloat16

D_MODEL = 1024
CHUNK = 64
NORM_EPS = 1e-6
GDN_HEADS, GDN_DIM = 8, 128
GLA_HEADS, GLA_DK, GLA_DV = 4, 128, 256
GATE_NORMALIZER = 16.0
CONV_K = 5
N_IN = 9280
N_DEV = 8
SHARD_IN = N_IN // N_DEV

N_CAT = 9728
SMALL_OFF = 9216
LANES = 128

ADAM_LR, ADAM_B1, ADAM_B2, ADAM_EPS, ADAM_WD, ADAM_STEP = 0.001, 0.9, 0.999, 1e-08, 0.01, 10

VMEM_LIMIT = 56 * 1024 * 1024


def _cparams(sem=None):
    return pltpu.CompilerParams(dimension_semantics=sem, vmem_limit_bytes=VMEM_LIMIT)


_DN = {"nn": (((1,), (0,)), ((), ())), "nt": (((1,), (1,)), ((), ())), "tn": (((0,), (0,)), ((), ()))}
_BDN = {"nn": (((2,), (1,)), ((0,), (0,))), "nt": (((2,), (2,)), ((0,), (0,))), "tn": (((1,), (1,)), ((0,), (0,)))}


def _dot16(a, b, kind):
    return lax.dot_general(a, b, (_DN if a.ndim == 2 else _BDN)[kind], preferred_element_type=f32)


def _mx(a, b, kind):
    return _dot16(a.astype(MXU_DTYPE), b.astype(MXU_DTYPE), kind)


def _split2(a):
    hi = a.astype(jnp.bfloat16)
    return hi, (a - hi.astype(f32)).astype(jnp.bfloat16)


def _split3(a):
    p1 = a.astype(jnp.bfloat16)
    r = a - p1.astype(f32)
    p2 = r.astype(jnp.bfloat16)
    return p1, p2, (r - p2.astype(f32)).astype(jnp.bfloat16)


def _h3s(a2, b2, kind):
    (ah, al), (bh, bl) = a2, b2
    return _dot16(ah, bh, kind) + (_dot16(ah, bl, kind) + _dot16(al, bh, kind))


def _h3(a, b, kind):
    return _h3s(_split2(a), _split2(b), kind)


def _exact01(t01, x, kind="nn"):
    t = t01.astype(jnp.bfloat16)
    x1, x2, x3 = _split3(x)
    return _dot16(t, x1, kind) + (_dot16(t, x2, kind) + _dot16(t, x3, kind))


def _colsum_col(e):
    ones = jnp.ones(e.shape[:-1] + (LANES,), jnp.bfloat16)
    e1, e2, e3 = _split3(e)
    return (_dot16(e1, ones, "tn") + (_dot16(e2, ones, "tn") + _dot16(e3, ones, "tn")))[..., 0:1]


def _iota3(shape, axis):
    return lax.broadcasted_iota(jnp.int32, shape, axis)


def _dir_masks(B, nf, Cn):
    shp = (B, Cn, Cn)
    bb, ii, jj = _iota3(shp, 0), _iota3(shp, 1), _iota3(shp, 2)
    fwd = bb < nf
    rev = jnp.logical_not(fwd)
    incl = jnp.logical_or(jnp.logical_and(fwd, ii >= jj), jnp.logical_and(rev, ii <= jj))
    strict = jnp.logical_and(incl, ii != jj)
    return incl, strict, ii == jj


def _last_row(x, nf):
    B, Cn = x.shape[0], x.shape[1]
    fwd = _iota3((B, 1, 1), 0) < nf
    return jnp.where(fwd, x[:, Cn - 1 : Cn, :], x[:, 0:1, :])


def _add_at_last_row(x, val, nf):
    B, Cn = x.shape[0], x.shape[1]
    bb, rr = _iota3((B, Cn, 1), 0), _iota3((B, Cn, 1), 1)
    at = jnp.logical_or(jnp.logical_and(bb < nf, rr == Cn - 1), jnp.logical_and(bb >= nf, rr == 0))
    return x + jnp.where(at, val, 0.0)


def _sigmoid(x):
    return jax.nn.sigmoid(x)


def _silu(x):
    return x * _sigmoid(x)


def _silu_grad(x):
    s = _sigmoid(x)
    return s * (1.0 + x * (1.0 - s))


def _softplus(x):
    u = jnp.exp(-jnp.abs(x))
    l1p = jnp.where(u < 1e-3, u * (1.0 - u * (0.5 - u * (1.0 / 3.0))), jnp.log(1.0 + u))
    return jnp.maximum(x, 0.0) + l1p


def _mxr(x):
    return x.astype(MXU_DTYPE).astype(f32)


def _iota2(shape, axis):
    return lax.broadcasted_iota(jnp.int32, shape, axis)


def _matmul(a, b, kind, out_dtype, name, tm, tn, tk, out_t=False, after=None):
    extra = [] if after is None else [after]
    if kind == "nn":
        (M, K), N = a.shape, b.shape[1]
    elif kind == "nt":
        (M, K), N = a.shape, b.shape[0]
    else:
        (K, M), N = a.shape, b.shape[1]
    tm, tn, tk = min(tm, M), min(tn, N), min(tk, K)
    assert M % tm == 0 and N % tn == 0 and K % tk == 0, (name, M, N, K)
    nk = K // tk
    if kind == "tn":
        a_spec = pl.BlockSpec((tk, tm), lambda i, j, k: (k, i))
    else:
        a_spec = pl.BlockSpec((tm, tk), lambda i, j, k: (i, k))
    if kind == "nt":
        b_spec = pl.BlockSpec((tn, tk), lambda i, j, k: (j, k))
    else:
        b_spec = pl.BlockSpec((tk, tn), lambda i, j, k: (k, j))

    def body(a_ref, b_ref, *rest):
        o_ref, acc = rest[len(extra)], rest[len(extra) + 1 :]
        p = _mx(a_ref[...], b_ref[...], kind)
        if nk == 1:
            o_ref[...] = (p.T if out_t else p).astype(out_dtype)
        else:
            assert not out_t
            acc_ref = acc[0]
            k = pl.program_id(2)

            @pl.when(k == 0)
            def _():
                acc_ref[...] = p

            @pl.when(k > 0)
            def _():
                acc_ref[...] += p

            @pl.when(k == nk - 1)
            def _():
                o_ref[...] = acc_ref[...].astype(out_dtype)

    return pl.pallas_call(
        body,
        out_shape=jax.ShapeDtypeStruct((N, M) if out_t else (M, N), out_dtype),
        grid=(M // tm, N // tn, nk),
        in_specs=[a_spec, b_spec] + [pl.BlockSpec((8, LANES), lambda i, j, k: (0, 0))] * len(extra),
        out_specs=pl.BlockSpec((tn, tm), lambda i, j, k: (j, i)) if out_t else pl.BlockSpec((tm, tn), lambda i, j, k: (i, j)),
        scratch_shapes=[] if nk == 1 else [pltpu.VMEM((tm, tn), f32)],
        compiler_params=_cparams(("parallel", "parallel", "arbitrary")),
        name=name,
    )(a, b, *extra)


ROW_TILE = 512


def _rms_pre(x, w):
    T = x.shape[0]
    tb = min(ROW_TILE, T)

    def body(x_ref, w_ref, h_ref, ht_ref):
        xv = x_ref[...]
        r = lax.rsqrt(jnp.mean(xv * xv, axis=1, keepdims=True) + NORM_EPS)
        h = xv * r * w_ref[...]
        h_ref[...] = h.astype(h_ref.dtype)
        ht_ref[...] = h.T.astype(ht_ref.dtype)

    return pl.pallas_call(
        body,
        out_shape=(jax.ShapeDtypeStruct((T, D_MODEL), MXU_DTYPE), jax.ShapeDtypeStruct((D_MODEL, T), MXU_DTYPE)),
        grid=(T // tb,),
        in_specs=[pl.BlockSpec((tb, D_MODEL), lambda i: (i, 0)), pl.BlockSpec((1, D_MODEL), lambda i: (0, 0))],
        out_specs=(pl.BlockSpec((tb, D_MODEL), lambda i: (i, 0)), pl.BlockSpec((D_MODEL, tb), lambda i: (0, i))),
        compiler_params=_cparams(("parallel",)),
        name="rms_pre",
    )(x, w)


def _rms_pre_bwd(x, w, dh, dy):
    T = x.shape[0]
    tb = min(ROW_TILE, T)

    def body(x_ref, w_ref, dh_ref, dy_ref, dx_ref, dw_ref):
        i = pl.program_id(0)
        xv, dhv = x_ref[...], dh_ref[...]
        r = lax.rsqrt(jnp.mean(xv * xv, axis=1, keepdims=True) + NORM_EPS)
        dhw = dhv * w_ref[...]
        dx_ref[...] = dy_ref[...] + r * dhw - xv * (r * r * r) * jnp.mean(dhw * xv, axis=1, keepdims=True)
        part = jnp.sum(dhv * xv * r, axis=0, keepdims=True)

        @pl.when(i == 0)
        def _():
            dw_ref[...] = part

        @pl.when(i > 0)
        def _():
            dw_ref[...] += part

    blk = pl.BlockSpec((tb, D_MODEL), lambda i: (i, 0))
    row = pl.BlockSpec((1, D_MODEL), lambda i: (0, 0))
    return pl.pallas_call(
        body,
        out_shape=(jax.ShapeDtypeStruct((T, D_MODEL), f32), jax.ShapeDtypeStruct((1, D_MODEL), f32)),
        grid=(T // tb,),
        in_specs=[blk, row, blk, blk],
        out_specs=(blk, row),
        compiler_params=_cparams(("arbitrary",)),
        name="rms_pre_bwd",
    )(x, w, dh, dy)


SHIFTS = (-2, -1, 1, 2)


def _shifted(x):
    T = x.shape[0]
    t = _iota2(x.shape, 0)
    out = {}
    for d in SHIFTS:
        valid = (t < T - d) if d > 0 else (t >= -d)
        out[d] = jnp.where(valid, pltpu.roll(x, (-d) % T, 0), 0.0)
    return out


def _conv5(u, ush, cw):
    acc = u * cw[2:3, :]
    for j in (0, 1, 3, 4):
        acc = acc + ush[j - 2] * cw[j : j + 1, :]
    return acc


def _gdn_prep(proj, conv_w):
    T = proj.shape[0]
    nh = GDN_HEADS

    def body(u_ref, cw_ref, o_ref):
        j = pl.program_id(0)
        u = _mxr(u_ref[...])
        s = _silu(_conv5(u, _shifted(u), _mxr(cw_ref[...])))

        @pl.when(j < 2 * nh)
        def _():
            o_ref[0] = s * lax.rsqrt(jnp.sum(s * s, axis=1, keepdims=True) + NORM_EPS)

        @pl.when(j >= 2 * nh)
        def _():
            o_ref[0] = s

    return pl.pallas_call(
        body,
        out_shape=jax.ShapeDtypeStruct((3, T, GDN_HEADS * GDN_DIM), f32),
        grid=(3 * nh,),
        in_specs=[pl.BlockSpec((T, GDN_DIM), lambda j: (0, j)), pl.BlockSpec((CONV_K, GDN_DIM), lambda j: (0, j))],
        out_specs=pl.BlockSpec((1, T, GDN_DIM), lambda j: (j // nh, 0, j % nh)),
        compiler_params=_cparams(("parallel",)),
        name="gdn_prep",
    )(proj, conv_w)


def _gdn_prep_bwd(proj, conv_w, dqkv_f, dqkv_b, dproj):
    T = proj.shape[0]
    nh = GDN_HEADS


    def body(u_ref, cw_ref, df_ref, db_ref, _, du_ref, dcw_ref):
        j = pl.program_id(0)
        u, cw = _mxr(u_ref[...]), _mxr(cw_ref[...])
        ush = _shifted(u)
        c = _conv5(u, ush, cw)
        s = _silu(c)
        dn = df_ref[0] + db_ref[0]
        rinv = lax.rsqrt(jnp.sum(s * s, axis=1, keepdims=True) + NORM_EPS)
        ds_norm = rinv * dn - s * (rinv * rinv * rinv) * jnp.sum(dn * s, axis=1, keepdims=True)
        ds = jnp.where(j < 2 * nh, ds_norm, dn)
        dc = _mxr(ds * _silu_grad(c))
        dcsh = _shifted(dc)
        du = dc * cw[2:3, :]
        for jj in range(CONV_K):
            d = jj - 2
            dcw_ref[jj : jj + 1, :] = jnp.sum(dc * (u if d == 0 else ush[d]), axis=0, keepdims=True)
            if d != 0:
                du = du + dcsh[-d] * cw[jj : jj + 1, :]
        du_ref[...] = du.astype(du_ref.dtype)

    qspec = pl.BlockSpec((1, T, GDN_DIM), lambda j: (j // nh, 0, j % nh))
    return pl.pallas_call(
        body,
        out_shape=(jax.ShapeDtypeStruct(dproj.shape, dproj.dtype), jax.ShapeDtypeStruct((CONV_K, 3 * nh * GDN_DIM), f32)),
        grid=(3 * nh,),
        in_specs=[pl.BlockSpec((T, GDN_DIM), lambda j: (0, j)), pl.BlockSpec((CONV_K, GDN_DIM), lambda j: (0, j)), qspec, qspec, _ANY_SPEC],
        out_specs=(pl.BlockSpec((T, GDN_DIM), lambda j: (0, j)), pl.BlockSpec((CONV_K, GDN_DIM), lambda j: (0, j))),
        input_output_aliases={4: 0},
        compiler_params=_cparams(("parallel",)),
        name="gdn_prep_bwd",
    )(proj, conv_w, dqkv_f, dqkv_b, dproj)


def _chunk_tri(n, rev):
    i, j = _iota2((n, n), 0), _iota2((n, n), 1)
    same = jnp.right_shift(i, 6) == jnp.right_shift(j, 6)
    order = (j >= i) if rev else (j <= i)
    return jnp.where(jnp.logical_and(same, order), 1.0, 0.0).astype(f32)


def _gdn_gates(proj, avec, dvec, tb=256):
    T = proj.shape[0]

    def body(s_ref, a_ref, d_ref, g_ref, gc_ref):
        small = s_ref[...]
        lane = _iota2(small.shape, 1)
        lg = -jnp.exp(a_ref[...]) * _softplus(small + d_ref[...])
        beta = _sigmoid(small)
        g_ref[...] = jnp.where(lane < 16, lg, jnp.where(lane < 32, beta, 0.0))
        lgm = jnp.where(lane < 16, lg, 0.0)
        gcf = _exact01(_chunk_tri(tb, False), lgm)
        gcr = _exact01(_chunk_tri(tb, True), lgm)
        gc_ref[...] = jnp.where(lane < 8, gcf, gcr)

    blk = pl.BlockSpec((tb, LANES), lambda i: (i, 0))
    row = pl.BlockSpec((1, LANES), lambda i: (0, 0))
    return pl.pallas_call(
        body,
        out_shape=(jax.ShapeDtypeStruct((T, LANES), f32), jax.ShapeDtypeStruct((T, LANES), f32)),
        grid=(T // tb,),
        in_specs=[pl.BlockSpec((tb, LANES), lambda i: (i, SMALL_OFF // LANES)), row, row],
        out_specs=(blk, blk),
        compiler_params=_cparams(("parallel",)),
        name="gdn_gates",
    )(proj, avec, dvec)


def _gdn_gates_bwd(proj, avec, dvec, dg_f, dg_b, dsmall_gla, dproj):
    T = proj.shape[0]
    tb = min(ROW_TILE, T)
    pad = N_CAT - SMALL_OFF

    def body(s_ref, a_ref, d_ref, gf_ref, gb_ref, dl_ref, _, ds_ref, da_ref, dd_ref):
        i = pl.program_id(0)
        small = s_ref[...]
        lane = _iota2(small.shape, 1)
        dgate = gf_ref[...] + gb_ref[...]
        z = small + d_ref[...]
        nega = -jnp.exp(a_ref[...])
        dz = dgate * nega * _sigmoid(z)
        beta = _sigmoid(small)
        dsm = jnp.where(lane < 16, dz, jnp.where(lane < 32, dgate * beta * (1.0 - beta), 0.0))
        ds_ref[:, :LANES] = (dsm + dl_ref[...]).astype(ds_ref.dtype)
        ds_ref[:, LANES:] = jnp.zeros((tb, pad - LANES), ds_ref.dtype)
        lg = nega * _softplus(z)
        pa = jnp.sum(jnp.where(lane < 16, dgate * lg, 0.0), axis=0, keepdims=True)
        pd = jnp.sum(jnp.where(lane < 16, dz, 0.0), axis=0, keepdims=True)

        @pl.when(i == 0)
        def _():
            da_ref[...] = pa
            dd_ref[...] = pd

        @pl.when(i > 0)
        def _():
            da_ref[...] += pa
            dd_ref[...] += pd

    blk = pl.BlockSpec((tb, LANES), lambda i: (i, 0))
    row = pl.BlockSpec((1, LANES), lambda i: (0, 0))
    return pl.pallas_call(
        body,
        out_shape=(jax.ShapeDtypeStruct(dproj.shape, dproj.dtype), jax.ShapeDtypeStruct((1, LANES), f32), jax.ShapeDtypeStruct((1, LANES), f32)),
        grid=(T // tb,),
        in_specs=[pl.BlockSpec((tb, LANES), lambda i: (i, SMALL_OFF // LANES)), row, row, blk, blk, blk, _ANY_SPEC],
        out_specs=(pl.BlockSpec((tb, pad), lambda i: (i, DP_SMALL_BLK)), row, row),
        input_output_aliases={6: 0},
        compiler_params=_cparams(("arbitrary",)),
        name="gdn_gates_bwd",
    )(proj, avec, dvec, dg_f, dg_b, dsmall_gla, dproj)


GDN_HB = 8


def _inv_unit_lower(L, eye):
    A = eye - L
    P2 = _split2(L)
    for _ in range(3):
        P2 = _split2(_h3s(P2, P2, "nn"))
        A = A + _h3s(_split2(A), P2, "nn")
    Ph = P2[0]
    for _ in range(2):
        Ph = _dot16(Ph, Ph, "nn").astype(jnp.bfloat16)
        A = A + _dot16(A.astype(jnp.bfloat16), Ph, "nn")
    return A


def _lane_col(blk, idx):
    lane = _iota2(blk.shape, 1)
    return jnp.sum(jnp.where(lane == idx, blk, 0.0), axis=1, keepdims=True)


def _out_norm(o_f, o_b, proj, gate_blk, w, hd, name):
    T, W = o_f.shape
    tb = min(ROW_TILE, T)
    nh = W // hd

    def body(of_ref, ob_ref, z_ref, w_ref, y_ref):
        wv = w_ref[...]
        for i in range(nh):
            sl = slice(i * hd, (i + 1) * hd)
            o = of_ref[:, sl] + ob_ref[:, sl]
            r = lax.rsqrt(jnp.mean(o * o, axis=1, keepdims=True) + NORM_EPS)
            y_ref[:, sl] = (o * r * wv * _silu(z_ref[:, sl])).astype(y_ref.dtype)

    blk = pl.BlockSpec((tb, W), lambda i: (i, 0))
    return pl.pallas_call(
        body,
        out_shape=jax.ShapeDtypeStruct((T, W), MXU_DTYPE),
        grid=(T // tb,),
        in_specs=[blk, blk, pl.BlockSpec((tb, W), lambda i: (i, gate_blk)), pl.BlockSpec((1, hd), lambda i: (0, 0))],
        out_specs=blk,
        compiler_params=_cparams(("parallel",)),
        name=name,
    )(o_f, o_b, proj, w)


def _out_norm_bwd(dy, o_f, o_b, proj, gate_blk, w, hd, name, dproj):
    T, W = o_f.shape
    tb = min(ROW_TILE, T)
    nh = W // hd

    def body(dy_ref, of_ref, ob_ref, z_ref, w_ref, _, do_ref, dz_ref, dw_ref):
        i = pl.program_id(0)
        wv = w_ref[...]
        dw = jnp.zeros((1, hd), f32)
        for a in range(nh):
            sl = slice(a * hd, (a + 1) * hd)
            o = of_ref[:, sl] + ob_ref[:, sl]
            z = z_ref[:, sl]
            dyv = dy_ref[:, sl]
            r = lax.rsqrt(jnp.mean(o * o, axis=1, keepdims=True) + NORM_EPS)
            orr = o * r
            dn = dyv * _silu(z)
            dz_ref[:, sl] = (dyv * orr * wv * _silu_grad(z)).astype(dz_ref.dtype)
            dw = dw + jnp.sum(dn * orr, axis=0, keepdims=True)
            dnw = dn * wv
            do_ref[:, sl] = r * dnw - o * (r * r * r) * jnp.mean(dnw * o, axis=1, keepdims=True)

        @pl.when(i == 0)
        def _():
            dw_ref[...] = dw

        @pl.when(i > 0)
        def _():
            dw_ref[...] += dw

    blk = pl.BlockSpec((tb, W), lambda i: (i, 0))
    row = pl.BlockSpec((1, hd), lambda i: (0, 0))
    return pl.pallas_call(
        body,
        out_shape=(jax.ShapeDtypeStruct((T, W), f32), jax.ShapeDtypeStruct(dproj.shape, dproj.dtype), jax.ShapeDtypeStruct((1, hd), f32)),
        grid=(T // tb,),
        in_specs=[blk, blk, blk, pl.BlockSpec((tb, W), lambda i: (i, gate_blk)), row, _ANY_SPEC],
        out_specs=(blk, pl.BlockSpec((tb, W), lambda i: (i, gate_blk)), row),
        input_output_aliases={5: 1},
        compiler_params=_cparams(("arbitrary",)),
        name=name,
    )(dy, o_f, o_b, proj, w, dproj)


def _gla_prep(proj, w2f, w2b, b2f, b2b, tb=256):
    T = proj.shape[0]
    W = GLA_HEADS * GLA_DK

    def body(s_ref, wf_ref, wb_ref, bf_ref, bb_ref, gf_ref, gb_ref):
        small = s_ref[...]
        gkf = -_softplus(-(_mx(small, wf_ref[...], "nn") + bf_ref[...])) * (1.0 / GATE_NORMALIZER)
        gkb = -_softplus(-(_mx(small, wb_ref[...], "nn") + bb_ref[...])) * (1.0 / GATE_NORMALIZER)
        gf_ref[...] = _exact01(_chunk_tri(tb, False), gkf)
        gb_ref[...] = _exact01(_chunk_tri(tb, True), gkb)

    blk = pl.BlockSpec((tb, W), lambda i: (i, 0))
    wsp = pl.BlockSpec((LANES, W), lambda i: (0, 0))
    row = pl.BlockSpec((1, W), lambda i: (0, 0))
    return pl.pallas_call(
        body,
        out_shape=(jax.ShapeDtypeStruct((T, W), f32), jax.ShapeDtypeStruct((T, W), f32)),
        grid=(T // tb,),
        in_specs=[pl.BlockSpec((tb, LANES), lambda i: (i, SMALL_OFF // LANES)), wsp, wsp, row, row],
        out_specs=(blk, blk),
        compiler_params=_cparams(("parallel",)),
        name="gla_prep",
    )(proj, w2f, w2b, b2f, b2b)


def _gla_prep_bwd(proj, w2f, w2b, b2f, b2b, dgk_f, dgk_b):
    T = proj.shape[0]
    tb = min(ROW_TILE, T)
    W = GLA_HEADS * GLA_DK

    def body(s_ref, wf_ref, wb_ref, bf_ref, bb_ref, df_ref, db_ref, ds_ref, dwf_ref, dwb_ref, dbf_ref, dbb_ref):
        i = pl.program_id(0)
        small = s_ref[...]
        dsm = jnp.zeros((tb, LANES), f32)
        parts = []
        for w_ref, b_ref, d_ref in ((wf_ref, bf_ref, df_ref), (wb_ref, bb_ref, db_ref)):
            pre = _mx(small, w_ref[...], "nn") + b_ref[...]
            dpre = d_ref[...] * (1.0 / GATE_NORMALIZER) * _sigmoid(-pre)
            dsm = dsm + _mx(dpre, w_ref[...], "nt")
            parts.append((_mx(small, dpre, "tn"), jnp.sum(dpre, axis=0, keepdims=True)))
        ds_ref[...] = dsm

        @pl.when(i == 0)
        def _():
            dwf_ref[...], dbf_ref[...] = parts[0]
            dwb_ref[...], dbb_ref[...] = parts[1]

        @pl.when(i > 0)
        def _():
            dwf_ref[...] += parts[0][0]
            dbf_ref[...] += parts[0][1]
            dwb_ref[...] += parts[1][0]
            dbb_ref[...] += parts[1][1]

    blk = pl.BlockSpec((tb, W), lambda i: (i, 0))
    wsp = pl.BlockSpec((LANES, W), lambda i: (0, 0))
    row = pl.BlockSpec((1, W), lambda i: (0, 0))
    return pl.pallas_call(
        body,
        out_shape=(jax.ShapeDtypeStruct((T, LANES), f32), jax.ShapeDtypeStruct((LANES, W), f32), jax.ShapeDtypeStruct((LANES, W), f32),
                   jax.ShapeDtypeStruct((1, W), f32), jax.ShapeDtypeStruct((1, W), f32)),
        grid=(T // tb,),
        in_specs=[pl.BlockSpec((tb, LANES), lambda i: (i, SMALL_OFF // LANES)), wsp, wsp, row, row, blk, blk],
        out_specs=(pl.BlockSpec((tb, LANES), lambda i: (i, 0)), wsp, wsp, row, row),
        compiler_params=_cparams(("arbitrary",)),
        name="gla_prep_bwd",
    )(proj, w2f, w2b, b2f, b2b, dgk_f, dgk_b)


GATE_A_BLK, GATE_B_BLK, Z_BLK, G_B_BLK = 6144 // 1024, 7168 // 1024, 3072 // 1024, 8192 // 1024
DP_GATES_BLK, DP_GLA_QKV_BLK, DP_SMALL_BLK = 6144 // 2048, 4096 // 2048, SMALL_OFF // 512
_ANY_SPEC = pl.BlockSpec(memory_space=pl.ANY)


def _mid(og, obg, proj, x, target, wpg, wpl, wout, w_post, tb=256):
    T = x.shape[0]
    Dm = D_MODEL

    def body(og_ref, obg_ref, ga_ref, gb_ref, x_ref, t_ref, wpg_ref, wpl_ref, wo_ref, wp_ref,
             loss_ref, dwp_ref, dy_ref, m_ref, dout_ref, dya_ref, dyb_ref, dgates_ref, dog_ref, dobg_ref):
        i = pl.program_id(0)
        ya = _mx(og_ref[...], wpg_ref[...], "nn")
        yb = _mx(obg_ref[...], wpl_ref[...], "nn")
        sa, sb = _sigmoid(ga_ref[...]), _sigmoid(gb_ref[...])
        m = sa * ya + sb * yb
        m_ref[...] = m.astype(m_ref.dtype)
        out = _mx(m, wo_ref[...], "nn")
        r = lax.rsqrt(jnp.mean(out * out, axis=1, keepdims=True) + NORM_EPS)
        wp = wp_ref[...]
        e = x_ref[...] + out * r * wp - t_ref[...]
        dy = e * (1.0 / Dm)
        dy_ref[...] = dy
        lpart = jnp.sum(e * e, axis=0, keepdims=True)
        wpart = jnp.sum(dy * out * r, axis=0, keepdims=True)
        dyn = dy * wp
        dout = r * dyn - out * (r * r * r) * jnp.mean(dyn * out, axis=1, keepdims=True)
        dout_ref[...] = dout.astype(dout_ref.dtype)
        dm = _mx(dout, wo_ref[...], "nt")
        dya, dyb = dm * sa, dm * sb
        dya_ref[...] = dya.astype(dya_ref.dtype)
        dyb_ref[...] = dyb.astype(dyb_ref.dtype)
        dgates_ref[:, :Dm] = (dm * ya * sa * (1.0 - sa)).astype(dgates_ref.dtype)
        dgates_ref[:, Dm:] = (dm * yb * sb * (1.0 - sb)).astype(dgates_ref.dtype)
        dog_ref[...] = _mx(dya, wpg_ref[...], "nt")
        dobg_ref[...] = _mx(dyb, wpl_ref[...], "nt")

        @pl.when(i == 0)
        def _():
            loss_ref[...] = lpart
            dwp_ref[...] = wpart

        @pl.when(i > 0)
        def _():
            loss_ref[...] += lpart
            dwp_ref[...] += wpart

    blk = pl.BlockSpec((tb, Dm), lambda i: (i, 0))
    row = pl.BlockSpec((1, Dm), lambda i: (0, 0))
    wsp = pl.BlockSpec((Dm, Dm), lambda i: (0, 0))
    act = jax.ShapeDtypeStruct((T, Dm), MXU_DTYPE)
    big = jax.ShapeDtypeStruct((T, Dm), f32)
    vec = jax.ShapeDtypeStruct((1, Dm), f32)
    return pl.pallas_call(
        body,
        out_shape=(vec, vec, big, act, act, act, act, jax.ShapeDtypeStruct((T, N_CAT), MXU_DTYPE), big, big),
        grid=(T // tb,),
        in_specs=[blk, blk, pl.BlockSpec((tb, Dm), lambda i: (i, GATE_A_BLK)), pl.BlockSpec((tb, Dm), lambda i: (i, GATE_B_BLK)),
                  blk, blk, wsp, wsp, wsp, row],
        out_specs=(row, row) + (blk,) * 5 + (pl.BlockSpec((tb, 2 * Dm), lambda i: (i, DP_GATES_BLK)), blk, blk),
        compiler_params=_cparams(("arbitrary",)),
        name="mid",
    )(og, obg, proj, proj, x, target, wpg, wpl, wout, w_post)


def _heads(ref, lead, n, width, rows=slice(None)):
    return [ref[lead + (rows, slice(i * width, (i + 1) * width))] for i in range(n)]


class _GdnBatch:
    def __init__(self, q, k, v, gcol, grow, beta, nf, A=None):
        B, Cn = q.shape[0], q.shape[1]
        self.nf = nf
        self.incl, self.strict, eye = _dir_masks(B, nf, Cn)
        self.decay = jnp.where(self.incl, jnp.exp(jnp.where(self.incl, gcol - grow, 0.0)), 0.0)
        self.k, self.v, self.beta = k, v, beta
        self.qs = q * (GDN_DIM**-0.5)
        self.kb = k * beta
        self.L = jnp.where(self.strict, _mx(self.kb, k, "nt") * self.decay, 0.0)
        if A is None:
            A = _inv_unit_lower(self.L, jnp.where(eye, 1.0, 0.0).astype(f32))
        self.A = A
        self.A2 = _split2(A)
        self.eg = jnp.exp(gcol)
        self.kbg = self.kb * self.eg
        self.sol = _h3s(self.A2, _split2(jnp.concatenate([v * beta, self.kbg], axis=2)), "nn")
        self.u = self.sol[:, :, :GDN_DIM]
        self.w = self.sol[:, :, GDN_DIM:]
        self.attn = jnp.where(self.incl, _mx(self.qs, k, "nt") * self.decay, 0.0)
        self.qd = self.qs * self.eg
        gl = _last_row(gcol, nf)
        self.ekd = jnp.exp(gl - gcol)
        self.kd = k * self.ekd
        self.egl = jnp.exp(gl)


def _gdn_chain_inputs(qf_ref, qb_ref, gf_ref, gb_ref, cf_ref, cb_ref, b, hb):
    qkv = [jnp.stack(_heads(qf_ref, (i,), hb, GDN_DIM) + _heads(qb_ref, (i,), hb, GDN_DIM)) for i in range(3)]
    cols, rows, betas = [], [], []
    for d, (g_ref, c_ref) in enumerate(((gf_ref, cf_ref), (gb_ref, cb_ref))):
        gates, gc = g_ref[...], c_ref[...]
        for hh in range(hb):
            h = b * hb + hh
            col = _lane_col(gc, h + 8 * d)
            cols.append(col)
            rows.append(jnp.broadcast_to(col, (CHUNK, LANES)).T[0:1, :])
            betas.append(_lane_col(gates, 16 + h + 8 * d))
    return qkv[0], qkv[1], qkv[2], jnp.stack(cols), jnp.stack(rows), jnp.stack(betas)


GDN_CPS_FWD, GDN_CPS_BWD = 2, 1


def _chunk_views(refs, s_f, s_b):
    out = []
    for i, r in enumerate(refs):
        s = s_f if i % 2 == 0 else s_b
        rows = pl.ds(s * CHUNK, CHUNK)
        out.append(r.at[pl.ds(s, 1)] if len(r.shape) == 4 else (r.at[:, rows, :] if len(r.shape) == 3 else r.at[rows, :]))
    return out


def _gdn_specs(cps, hb):
    R, W = cps * CHUNK, hb * GDN_DIM
    qsp = lambda ci: pl.BlockSpec((3, R, W), lambda n, b: (0, ci(n), b))
    gsp = lambda ci: pl.BlockSpec((R, LANES), lambda n, b: (ci(n), 0))
    osp = lambda ci: pl.BlockSpec((R, W), lambda n, b: (ci(n), b))
    ssp = lambda ci: pl.BlockSpec((cps, hb, GDN_DIM, GDN_DIM), lambda n, b: (ci(n), b, 0, 0))
    asp = lambda ci: pl.BlockSpec((cps, hb, CHUNK, CHUNK), lambda n, b: (ci(n), b, 0, 0))
    return qsp, gsp, osp, ssp, asp


def _gdn_fwd(qkvn, gates, gc):
    T = qkvn.shape[1]
    N = T // CHUNK
    nh, hb = GDN_HEADS, GDN_HB
    cps = min(GDN_CPS_FWD, N)
    NB = N // cps

    def body(*refs):
        n, b = pl.program_id(0), pl.program_id(1)
        s_scr = refs[-1]

        @pl.when(n == 0)
        def _():
            s_scr[0, pl.ds(b * hb, hb)] = jnp.zeros((hb, GDN_DIM, GDN_DIM), f32)
            s_scr[1, pl.ds(b * hb, hb)] = jnp.zeros((hb, GDN_DIM, GDN_DIM), f32)

        for s in range(cps):
            chunk(*_chunk_views(refs[:-1], s, cps - 1 - s), s_scr, b)

    def chunk(qf_ref, qb_ref, gf_ref, gb_ref, cf_ref, cb_ref, of_ref, ob_ref, sf_ref, sb_ref, af_ref, ab_ref, s_scr, b):
        hs = pl.ds(b * hb, hb)
        q, k, v, gcol, grow, beta = _gdn_chain_inputs(qf_ref, qb_ref, gf_ref, gb_ref, cf_ref, cb_ref, b, hb)
        ck = _GdnBatch(q, k, v, gcol, grow, beta, hb)
        S = jnp.concatenate([s_scr[0, hs], s_scr[1, hs]], axis=0)
        sf_ref[0], sb_ref[0] = S[:hb], S[hb:]
        af_ref[0], ab_ref[0] = ck.A[:hb], ck.A[hb:]
        v_new = ck.u - _mx(ck.w, S, "nn")
        o = _mx(ck.qd, S, "nn") + _mx(ck.attn, v_new, "nn")
        S_new = S * ck.egl + _mx(ck.kd, v_new, "tn")
        for hh in range(hb):
            sl = slice(hh * GDN_DIM, (hh + 1) * GDN_DIM)
            of_ref[:, sl] = o[hh]
            ob_ref[:, sl] = o[hb + hh]
        s_scr[0, hs] = S_new[:hb]
        s_scr[1, hs] = S_new[hb:]

    fw, bw = (lambda n: n), (lambda n: NB - 1 - n)
    qsp, gsp, osp, ssp, asp = _gdn_specs(cps, hb)
    o_shape = jax.ShapeDtypeStruct((T, nh * GDN_DIM), f32)
    s_shape = jax.ShapeDtypeStruct((N, nh, GDN_DIM, GDN_DIM), f32)
    a_shape = jax.ShapeDtypeStruct((N, nh, CHUNK, CHUNK), f32)
    return pl.pallas_call(
        body,
        out_shape=(o_shape, o_shape, s_shape, s_shape, a_shape, a_shape),
        grid=(NB, nh // hb),
        in_specs=[qsp(fw), qsp(bw), gsp(fw), gsp(bw), gsp(fw), gsp(bw)],
        out_specs=(osp(fw), osp(bw), ssp(fw), ssp(bw), asp(fw), asp(bw)),
        scratch_shapes=[pltpu.VMEM((2, nh, GDN_DIM, GDN_DIM), f32)],
        compiler_params=_cparams(("arbitrary", "arbitrary")),
        name="gdn_scan_fwd",
    )(qkvn, qkvn, gates, gates, gc, gc)


def _gdn_bwd(qkvn, gates, gc, st_f, st_b, a_f, a_b, do):
    T = qkvn.shape[1]
    N = T // CHUNK
    nh, hb = GDN_HEADS, GDN_HB
    nb = nh // hb
    cps = min(GDN_CPS_BWD, N)
    NB = N // cps

    def body(*refs):
        n, b = pl.program_id(0), pl.program_id(1)
        ds_scr = refs[-1]

        @pl.when(n == 0)
        def _():
            ds_scr[0, pl.ds(b * hb, hb)] = jnp.zeros((hb, GDN_DIM, GDN_DIM), f32)
            ds_scr[1, pl.ds(b * hb, hb)] = jnp.zeros((hb, GDN_DIM, GDN_DIM), f32)

        for s in range(cps):
            chunk(*_chunk_views(refs[:-1], cps - 1 - s, s), ds_scr, b)

    def chunk(qf_ref, qb_ref, gf_ref, gb_ref, cf_ref, cb_ref, sf_ref, sb_ref, af_ref, ab_ref, dof_ref, dob_ref,
              dqf_ref, dqb_ref, dgf_ref, dgb_ref, ds_scr, b):
        hs = pl.ds(b * hb, hb)

        @pl.when(b == 0)
        def _():
            dgf_ref[...] = jnp.zeros((CHUNK, LANES), f32)
            dgb_ref[...] = jnp.zeros((CHUNK, LANES), f32)

        q, k, v, gcol, grow, beta = _gdn_chain_inputs(qf_ref, qb_ref, gf_ref, gb_ref, cf_ref, cb_ref, b, hb)
        ck = _GdnBatch(q, k, v, gcol, grow, beta, hb, A=jnp.concatenate([af_ref[0], ab_ref[0]], axis=0))
        S = jnp.concatenate([sf_ref[0], sb_ref[0]], axis=0)
        dS = jnp.concatenate([ds_scr[0, hs], ds_scr[1, hs]], axis=0)
        dov = jnp.stack(_heads(dof_ref, (), hb, GDN_DIM) + _heads(dob_ref, (), hb, GDN_DIM))
        v_new = ck.u - _mx(ck.w, S, "nn")
        d_vnew = _mx(ck.attn, dov, "tn") + _mx(ck.kd, dS, "nn")
        d_attn = jnp.where(ck.incl, _mx(dov, v_new, "nt"), 0.0)
        d_qd = _mx(dov, S, "nt")
        d_kd = _mx(v_new, dS, "nt")
        d_gl = jnp.sum(jnp.sum(dS * S, axis=2, keepdims=True), axis=1, keepdims=True) * ck.egl
        dS_new = dS * ck.egl + _mx(ck.qd, dov, "tn") - _mx(ck.w, d_vnew, "tn")
        ds_scr[0, hs] = dS_new[:hb]
        ds_scr[1, hs] = dS_new[hb:]
        d_w = -_mx(d_vnew, S, "nt")
        d_rhs = _h3s(ck.A2, _split2(jnp.concatenate([d_vnew, d_w], axis=2)), "tn")
        d_vb, d_kbg = d_rhs[:, :, :GDN_DIM], d_rhs[:, :, GDN_DIM:]
        dL = -jnp.where(ck.strict, _h3(d_rhs, ck.sol, "nt"), 0.0)
        P = dL * ck.decay
        Q = d_attn * ck.decay
        d_kb = _mx(P, k, "nn") + d_kbg * ck.eg
        d_qs = _mx(Q, k, "nn") + d_qd * ck.eg
        d_k = _mx(P, ck.kb, "tn") + _mx(Q, ck.qs, "tn") + d_kd * ck.ekd + d_kb * ck.beta
        E = dL * ck.L + d_attn * ck.attn
        kdsum = jnp.sum(d_kd * ck.kd, axis=2, keepdims=True)
        d_gc = (jnp.sum(E, axis=2, keepdims=True) - _colsum_col(E)
                + jnp.sum(d_kbg * ck.kbg, axis=2, keepdims=True)
                + jnp.sum(d_qd * ck.qd, axis=2, keepdims=True) - kdsum)
        d_gc = _add_at_last_row(d_gc, d_gl + jnp.sum(kdsum, axis=1, keepdims=True), hb)
        d_beta = jnp.sum(d_vb * v, axis=2, keepdims=True) + jnp.sum(d_kb * k, axis=2, keepdims=True)
        d_q = d_qs * (GDN_DIM**-0.5)
        d_v = d_vb * ck.beta
        lane = _iota2((CHUNK, LANES), 1)
        for d, (dq_ref, dg_ref) in enumerate(((dqf_ref, dgf_ref), (dqb_ref, dgb_ref))):
            acc = dg_ref[...]
            for hh in range(hb):
                c, h = d * hb + hh, b * hb + hh
                sl = slice(hh * GDN_DIM, (hh + 1) * GDN_DIM)
                dq_ref[0, :, sl] = d_q[c]
                dq_ref[1, :, sl] = d_k[c]
                dq_ref[2, :, sl] = d_v[c]
                acc = jnp.where(lane == h + 8 * d, d_gc[c], acc)
                acc = jnp.where(lane == 16 + h + 8 * d, d_beta[c], acc)
            dg_ref[...] = acc

        @pl.when(b == nb - 1)
        def _():
            for d, dg_ref in enumerate((dgf_ref, dgb_ref)):
                blk = dg_ref[...]
                dg_ref[...] = jnp.where(lane < 16, _exact01(_chunk_tri(CHUNK, d == 0), blk), blk)

    fw, bw = (lambda n: NB - 1 - n), (lambda n: n)
    qsp, gsp, osp, ssp, asp = _gdn_specs(cps, hb)
    dq_shape = jax.ShapeDtypeStruct((3, T, nh * GDN_DIM), f32)
    dg_shape = jax.ShapeDtypeStruct((T, LANES), f32)
    return pl.pallas_call(
        body,
        out_shape=(dq_shape, dq_shape, dg_shape, dg_shape),
        grid=(NB, nb),
        in_specs=[qsp(fw), qsp(bw), gsp(fw), gsp(bw), gsp(fw), gsp(bw), ssp(fw), ssp(bw), asp(fw), asp(bw), osp(fw), osp(bw)],
        out_specs=(qsp(fw), qsp(bw), gsp(fw), gsp(bw)),
        scratch_shapes=[pltpu.VMEM((2, nh, GDN_DIM, GDN_DIM), f32)],
        compiler_params=_cparams(("arbitrary", "arbitrary")),
        name="gdn_scan_bwd",
    )(qkvn, qkvn, gates, gates, gc, gc, st_f, st_b, a_f, a_b, do, do)


class _GlaBatch:
    def __init__(self, q, k, G, nf):
        B, Cn = q.shape[0], q.shape[1]
        self.incl, _, _ = _dir_masks(B, nf, Cn)
        self.eG = jnp.exp(G)
        self.enG = jnp.exp(-G)
        self.qg = q * (GLA_DK**-0.5) * self.eG
        self.kg = k * self.enG
        self.attn = jnp.where(self.incl, _mx(self.qg, self.kg, "nt"), 0.0)
        gl = _last_row(G, nf)
        self.ekd = jnp.exp(gl - G)
        self.kd = k * self.ekd
        self.egl = jnp.exp(gl)


GLA_CPS = 4


def _chunk_rows(s):
    return slice(s * CHUNK, (s + 1) * CHUNK)


def _gla_chain_inputs(refs_f, refs_b, rows_f, rows_b):
    nh = GLA_HEADS
    out = []
    for i, width in enumerate((GLA_DK, GLA_DK, GLA_DV, GLA_DK)):
        out.append(jnp.stack(_heads(refs_f[i], (), nh, width, rows_f) + _heads(refs_b[i], (), nh, width, rows_b)))
    return out


def _gla_specs(rows, ci):
    nh = GLA_HEADS
    return [
        pl.BlockSpec((rows, nh * GLA_DK), lambda n: (ci(n), 4096 // (nh * GLA_DK))),
        pl.BlockSpec((rows, nh * GLA_DK), lambda n: (ci(n), 4608 // (nh * GLA_DK))),
        pl.BlockSpec((rows, nh * GLA_DV), lambda n: (ci(n), 5120 // (nh * GLA_DV))),
        pl.BlockSpec((rows, nh * GLA_DK), lambda n: (ci(n), 0)),
    ]


def _gla_fwd(proj, G_f, G_b):
    T = proj.shape[0]
    N = T // CHUNK
    nh = GLA_HEADS
    cps = min(GLA_CPS, N)
    NB = N // cps

    def body(qf, kf, vf, gf, qb, kb, vb, gb, of_ref, ob_ref, sf_ref, sb_ref, s_scr):
        n = pl.program_id(0)

        @pl.when(n == 0)
        def _():
            s_scr[...] = jnp.zeros((2 * nh, GLA_DV, GLA_DK), f32)

        for s in range(cps):
            sb = cps - 1 - s
            q, k, v, G = _gla_chain_inputs((qf, kf, vf, gf), (qb, kb, vb, gb), _chunk_rows(s), _chunk_rows(sb))
            ck = _GlaBatch(q, k, G, nh)
            St = s_scr[...]
            sf_ref[s], sb_ref[sb] = St[:nh].astype(sf_ref.dtype), St[nh:].astype(sb_ref.dtype)
            o = _mx(ck.qg, St, "nt") + _mx(ck.attn, v, "nn")
            for i in range(nh):
                sl = slice(i * GLA_DV, (i + 1) * GLA_DV)
                of_ref[_chunk_rows(s), sl] = o[i]
                ob_ref[_chunk_rows(sb), sl] = o[nh + i]
            s_scr[...] = St * ck.egl + _mx(v, ck.kd, "tn")

    fw, bw = (lambda n: n), (lambda n: NB - 1 - n)
    osp = lambda ci: pl.BlockSpec((cps * CHUNK, nh * GLA_DV), lambda n: (ci(n), 0))
    ssp = lambda ci: pl.BlockSpec((cps, nh, GLA_DV, GLA_DK), lambda n: (ci(n), 0, 0, 0))
    o_shape = jax.ShapeDtypeStruct((T, nh * GLA_DV), f32)
    s_shape = jax.ShapeDtypeStruct((N, nh, GLA_DV, GLA_DK), MXU_DTYPE)
    return pl.pallas_call(
        body,
        out_shape=(o_shape, o_shape, s_shape, s_shape),
        grid=(NB,),
        in_specs=_gla_specs(cps * CHUNK, fw) + _gla_specs(cps * CHUNK, bw),
        out_specs=(osp(fw), osp(bw), ssp(fw), ssp(bw)),
        scratch_shapes=[pltpu.VMEM((2 * nh, GLA_DV, GLA_DK), f32)],
        compiler_params=_cparams(("arbitrary",)),
        name="gla_scan_fwd",
    )(proj, proj, proj, G_f, proj, proj, proj, G_b)


def _gla_bwd(proj, G_f, G_b, st_f, st_b, do):
    T = proj.shape[0]
    N = T // CHUNK
    nh = GLA_HEADS
    cps = min(GLA_CPS, N)
    NB = N // cps

    def body(qf, kf, vf, gf, sf_ref, dof_ref, qb, kb, vb, gb, sb_ref, dob_ref,
             dqf_ref, dkf_ref, dvf_ref, dgf_ref, dqb_ref, dkb_ref, dvb_ref, dgb_ref, ds_scr):
        n = pl.program_id(0)

        @pl.when(n == 0)
        def _():
            ds_scr[...] = jnp.zeros((2 * nh, GLA_DV, GLA_DK), f32)

        shp = (2 * nh, CHUNK, CHUNK)
        bb, ii, jj = _iota3(shp, 0), _iota3(shp, 1), _iota3(shp, 2)
        tri = jnp.where(jnp.logical_or(jnp.logical_and(bb < nh, jj >= ii), jnp.logical_and(bb >= nh, jj <= ii)), 1.0, 0.0)
        for s in range(cps):
            sf = cps - 1 - s
            rows = (_chunk_rows(sf), _chunk_rows(s))
            q, k, v, G = _gla_chain_inputs((qf, kf, vf, gf), (qb, kb, vb, gb), *rows)
            ck = _GlaBatch(q, k, G, nh)
            St = jnp.concatenate([sf_ref[sf], sb_ref[s]], axis=0).astype(f32)
            dSt = ds_scr[...]
            dov = jnp.stack(_heads(dof_ref, (), nh, GLA_DV, rows[0]) + _heads(dob_ref, (), nh, GLA_DV, rows[1]))
            d_attn = jnp.where(ck.incl, _mx(dov, v, "nt"), 0.0)
            d_qg = _mx(dov, St, "nn") + _mx(d_attn, ck.kg, "nn")
            d_kg = _mx(d_attn, ck.qg, "tn")
            d_v = _mx(ck.attn, dov, "tn") + _mx(ck.kd, dSt, "nt")
            d_kd = _mx(v, dSt, "nn")
            ds_scr[...] = dSt * ck.egl + _mx(dov, ck.qg, "tn")
            kdd = d_kd * ck.kd
            d_gl = jnp.sum(dSt * St, axis=1, keepdims=True) * ck.egl + jnp.sum(kdd, axis=1, keepdims=True)
            d_q = d_qg * ck.eG * (GLA_DK**-0.5)
            d_k = d_kg * ck.enG + d_kd * ck.ekd
            d_gk = _exact01(tri, _add_at_last_row(d_qg * ck.qg - d_kg * ck.kg - kdd, d_gl, nh))
            for d, (dq_ref, dk_ref, dv_ref, dg_ref) in enumerate(((dqf_ref, dkf_ref, dvf_ref, dgf_ref), (dqb_ref, dkb_ref, dvb_ref, dgb_ref))):
                for i in range(nh):
                    c = d * nh + i
                    ks, vs = slice(i * GLA_DK, (i + 1) * GLA_DK), slice(i * GLA_DV, (i + 1) * GLA_DV)
                    dq_ref[rows[d], ks] = d_q[c]
                    dk_ref[rows[d], ks] = d_k[c]
                    dv_ref[rows[d], vs] = d_v[c]
                    dg_ref[rows[d], ks] = d_gk[c]

    R = cps * CHUNK

    def in_specs(ci):
        return _gla_specs(R, ci) + [
            pl.BlockSpec((cps, nh, GLA_DV, GLA_DK), lambda n: (ci(n), 0, 0, 0)),
            pl.BlockSpec((R, nh * GLA_DV), lambda n: (ci(n), 0)),
        ]

    def out_specs(ci):
        ksp = pl.BlockSpec((R, nh * GLA_DK), lambda n: (ci(n), 0))
        return [ksp, ksp, pl.BlockSpec((R, nh * GLA_DV), lambda n: (ci(n), 0)), ksp]

    fw, bw = (lambda n: NB - 1 - n), (lambda n: n)
    k_shape = jax.ShapeDtypeStruct((T, nh * GLA_DK), f32)
    v_shape = jax.ShapeDtypeStruct((T, nh * GLA_DV), f32)
    return pl.pallas_call(
        body,
        out_shape=(k_shape, k_shape, v_shape, k_shape) * 2,
        grid=(NB,),
        in_specs=in_specs(fw) + in_specs(bw),
        out_specs=tuple(out_specs(fw) + out_specs(bw)),
        scratch_shapes=[pltpu.VMEM((2 * nh, GLA_DV, GLA_DK), f32)],
        compiler_params=_cparams(("arbitrary",)),
        name="gla_scan_bwd",
    )(proj, proj, proj, G_f, st_f, do, proj, proj, proj, G_b, st_b, do)


def _gla_dsum(dq_f, dq_b, dk_f, dk_b, dv_f, dv_b, dproj):
    T = dq_f.shape[0]
    tb = min(ROW_TILE, T)
    wk, wv = GLA_HEADS * GLA_DK, GLA_HEADS * GLA_DV

    def body(qf, qb, kf, kb, vf, vb, _, o_ref):
        o_ref[:, :wk] = (qf[...] + qb[...]).astype(o_ref.dtype)
        o_ref[:, wk : 2 * wk] = (kf[...] + kb[...]).astype(o_ref.dtype)
        o_ref[:, 2 * wk :] = (vf[...] + vb[...]).astype(o_ref.dtype)

    ksp = pl.BlockSpec((tb, wk), lambda i: (i, 0))
    vsp = pl.BlockSpec((tb, wv), lambda i: (i, 0))
    return pl.pallas_call(
        body,
        out_shape=jax.ShapeDtypeStruct(dproj.shape, dproj.dtype),
        grid=(T // tb,),
        in_specs=[ksp, ksp, ksp, ksp, vsp, vsp, _ANY_SPEC],
        out_specs=pl.BlockSpec((tb, 2 * wk + wv), lambda i: (i, DP_GLA_QKV_BLK)),
        input_output_aliases={6: 0},
        compiler_params=_cparams(("parallel",)),
        name="gla_dsum",
    )(dq_f, dq_b, dk_f, dk_b, dv_f, dv_b, dproj)


def _local_step(x, target, wcat, conv_w, avec, dvec, gdn_norm_w, w2f, w2b, b2f, b2b, gla_norm_w, proj_weights, w_pre, w_post,
                send_proj_grads=None, send_in_grads=None):
    h, h_t = _rms_pre(x, w_pre)
    proj = _matmul(h, wcat, "nt", f32, "proj_in", 2048, 512, 1024)
    qkvn = _gdn_prep(proj, conv_w)
    gates, gc = _gdn_gates(proj, avec, dvec)
    oa_f, oa_b, sa_f, sa_b, inv_f, inv_b = _gdn_fwd(qkvn, gates, gc)
    og = _out_norm(oa_f, oa_b, proj, Z_BLK, gdn_norm_w, GDN_DIM, "gdn_out")
    G_f, G_b = _gla_prep(proj, w2f, w2b, b2f, b2b)
    ob_f, ob_b, sb_f, sb_b = _gla_fwd(proj, G_f, G_b)
    obg = _out_norm(ob_f, ob_b, proj, G_B_BLK, gla_norm_w, GLA_DV, "gla_out")
    wpg, wpl, wout = proj_weights(obg)
    (loss_row, d_wpost, dy, m, dout, dya, dyb, dproj, dog, dobg) = _mid(og, obg, proj, x, target, wpg, wpl, wout, w_post)

    d_wout = _matmul(m, dout, "tn", MXU_DTYPE, "dw_out", 1024, 512, 2048)
    d_wpg = _matmul(og, dya, "tn", MXU_DTYPE, "dw_proj_gdn", 1024, 512, 2048)
    d_wpl = _matmul(obg, dyb, "tn", MXU_DTYPE, "dw_proj_gla", 1024, 512, 2048)
    if send_proj_grads is not None:
        gla_norm_w = gla_norm_w + send_proj_grads(d_wpg, d_wpl, d_wout)

    do_b, dproj, d_gla_norm = _out_norm_bwd(dobg, ob_f, ob_b, proj, G_B_BLK, gla_norm_w, GLA_DV, "gla_out_bwd", dproj)
    dq_f, dk_f, dv_f, dgk_f, dq_b, dk_b, dv_b, dgk_b = _gla_bwd(proj, G_f, G_b, sb_f, sb_b, do_b)
    dproj = _gla_dsum(dq_f, dq_b, dk_f, dk_b, dv_f, dv_b, dproj)
    dsmall_gla, d_w2f, d_w2b, d_b2f, d_b2b = _gla_prep_bwd(proj, w2f, w2b, b2f, b2b, dgk_f, dgk_b)

    do_a, dproj, d_gdn_norm = _out_norm_bwd(dog, oa_f, oa_b, proj, Z_BLK, gdn_norm_w, GDN_DIM, "gdn_out_bwd", dproj)
    dqkv_f, dqkv_b, dg_f, dg_b = _gdn_bwd(qkvn, gates, gc, sa_f, sa_b, inv_f, inv_b, do_a)
    dproj, d_conv = _gdn_prep_bwd(proj, conv_w, dqkv_f, dqkv_b, dproj)
    dproj, d_alog, d_dtb = _gdn_gates_bwd(proj, avec, dvec, dg_f, dg_b, dsmall_gla, dproj)

    d_wcat = _matmul(h_t, dproj, "nn", MXU_DTYPE, "dw_in", 1024, 512, 2048, out_t=True)
    started = None if send_in_grads is None else send_in_grads(d_wcat)
    dh = _matmul(dproj, wcat, "nn", f32, "dh", 1024, 1024, 2432, after=started)
    grad_x, d_wpre = _rms_pre_bwd(x, w_pre, dh, dy)
    return dict(loss_row=loss_row, grad_x=grad_x, d_wcat=d_wcat, d_conv=d_conv, d_alog=d_alog, d_dtb=d_dtb,
                d_gdn_norm=d_gdn_norm, d_wpg=d_wpg, d_w2f=d_w2f, d_w2b=d_w2b, d_b2f=d_b2f, d_b2b=d_b2b,
                d_gla_norm=d_gla_norm, d_wpl=d_wpl, d_wout=d_wout, d_wpost=d_wpost, d_wpre=d_wpre)


def _to_cat(wf):
    pad = jnp.zeros((N_CAT - N_IN,) + wf.shape[1:], wf.dtype)
    return jnp.concatenate([wf[:4096], wf[4128:6176], wf[7232:9280], wf[6176:7200], wf[4096:4128], wf[7200:7232], pad], axis=0)


def _from_cat(wc):
    return jnp.concatenate([wc[:4096], wc[9216:9248], wc[4096:6144], wc[8192:9216], wc[9248:9280], wc[6144:8192]], axis=0)


def _lane_row(*pieces):
    row = jnp.concatenate(pieces, axis=1)
    return jnp.pad(row, ((0, 0), (0, LANES - row.shape[1])))


def _pad_w2(w2, row0):
    return jnp.pad(w2, ((row0, LANES - row0 - w2.shape[0]), (0, 0)))


SM_ROWS, SM_COLS = 32, 384


def _pack_small_shard(conv, w2f, w2b):
    lead = conv.shape[:-2]
    z = lambda r, c: jnp.zeros(lead + (r, c), f32)
    top = jnp.concatenate([conv, z(8 - CONV_K, SM_COLS)], axis=-2)
    mid = jnp.concatenate([w2f, w2b, z(16, SM_COLS - 128)], axis=-1)
    return jnp.concatenate([top, mid, z(SM_ROWS - 24, SM_COLS)], axis=-2)


SMALL_REGION = {
    "ln_pre_w": (0, 1, 0, 1024), "ln_post_w": (1, 1, 0, 1024), "gk_b2_fwd": (2, 1, 0, 512), "gk_b2_bwd": (2, 1, 512, 512),
    "gla_norm_w": (3, 1, 0, 256), "gdn_norm_w": (3, 1, 256, 128), "a_log_fwd": (4, 1, 0, 8), "a_log_bwd": (5, 1, 0, 8),
    "dt_bias_fwd": (6, 1, 0, 8), "dt_bias_bwd": (7, 1, 0, 8), "conv_w": (8, CONV_K, 0, SM_COLS),
    "gk_w2_fwd": (16, 16, 0, 64), "gk_w2_bwd": (16, 16, 128, 64),
}
SMALL_NAMES = tuple(SMALL_REGION)
SMALL_SHARDED = ("conv_w", "gk_w2_fwd", "gk_w2_bwd")
PACK_ROWS, PACK_COLS = 32, 1024
LOSS_AT = (13, 0)


def _region(name):
    r0, nr, l0, nl = SMALL_REGION[name]
    return slice(r0, r0 + nr), slice(l0, l0 + nl)


def _pack_small_grads(G):
    names = ("d_wpre", "d_wpost", "d_b2f", "d_b2b", "d_gla_norm", "d_gdn_norm", "d_alog", "d_dtb", "loss_row", "d_conv", "d_w2f", "d_w2b")

    def body(wpre, wpost, b2f, b2b, glan, gdnn, alog, dtb, lrow, conv, w2f, w2b, o_ref):
        o_ref[...] = jnp.zeros(o_ref.shape, f32)
        loss = 0.5 * jnp.sum(lrow[...], axis=1, keepdims=True) * (1.0 / D_MODEL)
        whole = {"ln_pre_w": wpre, "ln_post_w": wpost, "gk_b2_fwd": b2f, "gk_b2_bwd": b2b, "gla_norm_w": glan, "gdn_norm_w": gdnn}
        for j in range(N_DEV):
            for name, ref in whole.items():
                o_ref[(j,) + _region(name)] = ref[...]
            o_ref[(j,) + _region("a_log_fwd")] = alog[0:1, 0:8]
            o_ref[(j,) + _region("a_log_bwd")] = alog[0:1, 8:16]
            o_ref[(j,) + _region("dt_bias_fwd")] = dtb[0:1, 0:8]
            o_ref[(j,) + _region("dt_bias_bwd")] = dtb[0:1, 8:16]
            o_ref[(j,) + _region("conv_w")] = conv[:, j * SM_COLS : (j + 1) * SM_COLS]
            o_ref[(j,) + _region("gk_w2_fwd")] = w2f[32:48, j * 64 : (j + 1) * 64]
            o_ref[(j,) + _region("gk_w2_bwd")] = w2b[48:64, j * 64 : (j + 1) * 64]
            o_ref[j, LOSS_AT[0] : LOSS_AT[0] + 1, LOSS_AT[1] : LOSS_AT[1] + 1] = loss

    return pl.pallas_call(
        body, out_shape=jax.ShapeDtypeStruct((N_DEV, PACK_ROWS, PACK_COLS), f32), name="pack_small_grads",
    )(*[G[n] for n in names])


def _small_adam(land, own, W, M, V):
    bc1 = 1.0 - ADAM_B1**ADAM_STEP
    bc2 = 1.0 - ADAM_B2**ADAM_STEP
    n = len(SMALL_NAMES)

    def body(land_ref, own_ref, *refs):
        w_refs, m_refs, v_refs = refs[0:n], refs[n : 2 * n], refs[2 * n : 3 * n]
        outs, loss_ref, g_scr = refs[3 * n : 7 * n], refs[7 * n], refs[7 * n + 1]
        me = _my_device()
        g = jnp.where(me == 0, own_ref[0], land_ref[0])
        for j in range(1, N_DEV):
            g = g + jnp.where(me == j, own_ref[j], land_ref[j])
        g_scr[...] = g
        loss_ref[...] = g_scr[LOSS_AT[0] : LOSS_AT[0] + 1, LOSS_AT[1] : LOSS_AT[1] + 1]
        for i, name in enumerate(SMALL_NAMES):
            gp = g_scr[_region(name)]
            lead = (0,) if name in SMALL_SHARDED else (Ellipsis,)
            w, m, v = w_refs[i][lead], m_refs[i][lead], v_refs[i][lead]
            m2 = ADAM_B1 * m + (1.0 - ADAM_B1) * gp
            v2 = ADAM_B2 * v + (1.0 - ADAM_B2) * (gp * gp)
            delta = -ADAM_LR * ((m2 / bc1) / (jnp.sqrt(v2 / bc2) + ADAM_EPS) + ADAM_WD * w)
            for k, val in enumerate((gp, delta, m2, v2)):
                outs[4 * i + k][lead] = val

    shapes = [jax.ShapeDtypeStruct(W[name].shape, f32) for name in SMALL_NAMES for _ in range(4)]
    res = pl.pallas_call(
        body,
        out_shape=tuple(shapes) + (jax.ShapeDtypeStruct((1, 1), f32),),
        scratch_shapes=[pltpu.VMEM((PACK_ROWS, PACK_COLS), f32)],
        name="adam_small",
    )(land, own, *[P[name] for P in (W, M, V) for name in SMALL_NAMES])
    return {name: res[4 * i : 4 * i + 4] for i, name in enumerate(SMALL_NAMES)}, res[4 * n]


_MESH = pl.DeviceIdType.MESH


N_BIG_COPIES, N_SMALL_COPIES = 9, 7


def _gather_weights(w_in_s, sm_s):
    shapes = (w_in_s.shape, sm_s.shape)
    dtypes = (MXU_DTYPE, f32)

    def body(win_ref, sm_ref, gin_ref, gsm_ref, send_sems, recv_sems):
        x, y, c = lax.axis_index("x"), lax.axis_index("y"), lax.axis_index("c")
        me, sibling = (x, y, c), (x, y, 1 - c)
        chips = [(1 - x, y), (x, 1 - y), (1 - x, 1 - y)]

        def idx(px, py, pc):
            return 4 * px + 2 * py + pc

        for r in range(w_in_s.shape[1] // LANES):
            cols = slice(r * LANES, (r + 1) * LANES)
            gin_ref[idx(*me), :, cols] = win_ref[:, cols].astype(MXU_DTYPE)
        gsm_ref[idx(*me)] = sm_ref[...]

        def copy(sem, blk, to):
            return pltpu.make_async_remote_copy(src_ref=blk, dst_ref=blk, send_sem=send_sems.at[sem], recv_sem=recv_sems.at[sem],
                                                device_id=to, device_id_type=_MESH)

        def small(k, block, to):
            return copy(N_BIG_COPIES + k, gsm_ref.at[idx(*block)], to)

        half = w_in_s.shape[1] // 2
        xn, yn, dg = (1 - x, y), (x, 1 - y), (1 - x, 1 - y)

        def big(sem, block, to, part=None):
            blk = gin_ref.at[idx(*block)]
            return copy(sem, blk if part is None else blk.at[:, pl.ds(part * half, half)], to)

        started = [big(0, me, sibling), big(1, me, (*xn, c)), big(2, me, (*yn, c)), small(0, me, sibling)]
        started += [small(1 + j, me, (*chip, c)) for j, chip in enumerate(chips)]
        for cp in started:
            cp.start()

        def then(arrived, forwards):
            arrived.wait_recv()
            for cp in forwards:
                cp.start()
            started.extend(forwards)

        then(big(1, (*xn, c), me), [big(3, (*xn, c), (*yn, c), 1), big(5, (*xn, c), sibling)])
        then(big(2, (*yn, c), me), [big(4, (*yn, c), (*xn, c), 0), big(6, (*yn, c), sibling)])
        then(big(4, (*dg, c), me, 0), [big(7, (*dg, c), sibling, 0)])
        then(big(3, (*dg, c), me, 1), [big(8, (*dg, c), sibling, 1)])
        for j, chip in enumerate(chips):
            then(small(1 + j, (*chip, c), me), [small(4 + j, (*chip, c), sibling)])
        big(0, sibling, me).wait_recv()
        big(5, (*xn, 1 - c), me).wait_recv()
        big(6, (*yn, 1 - c), me).wait_recv()
        big(7, (*dg, 1 - c), me, 0).wait_recv()
        big(8, (*dg, 1 - c), me, 1).wait_recv()
        small(0, sibling, me).wait_recv()
        for j, chip in enumerate(chips):
            small(4 + j, (*chip, 1 - c), me).wait_recv()
        for cp in started:
            cp.wait_send()

    vm = pl.BlockSpec(memory_space=pltpu.VMEM)
    return pl.pallas_call(
        body,
        out_shape=tuple(jax.ShapeDtypeStruct((N_DEV,) + s, d) for s, d in zip(shapes, dtypes)),
        in_specs=[vm, vm],
        out_specs=(vm, vm),
        scratch_shapes=[pltpu.SemaphoreType.DMA((N_BIG_COPIES + N_SMALL_COPIES,)), pltpu.SemaphoreType.DMA((N_BIG_COPIES + N_SMALL_COPIES,))],
        compiler_params=pltpu.CompilerParams(vmem_limit_bytes=VMEM_LIMIT),
        name="gather_weights",
    )(w_in_s, sm_s)


N_CHIP = 4
_EFFECT = pltpu.SideEffectType.DATAFLOW_SIDE_EFFECTING
_HBM_SPEC = pl.BlockSpec(memory_space=pltpu.HBM)
_SEM_SPEC = pl.BlockSpec(memory_space=pltpu.SEMAPHORE)


def _relation_peer(k, x, y, c):
    px = (1 - x) if (k & 4) else x
    py = (1 - y) if (k & 2) else y
    pc = (1 - c) if (k & 1) else c
    return (px, py, pc), 4 * px + 2 * py + pc


_PLAN_COPIES = {"gather": N_DEV - 1, "blocks": N_DEV - 1, "sibling": 4, "chips": 3}
_PLAN_SLOTS = {"gather": N_DEV, "blocks": N_DEV, "sibling": 4, "chips": 4}


def _plan_copies(plan, src_ref, land_ref, send_sems, recv_sems, arrival):
    x, y, c = lax.axis_index("x"), lax.axis_index("y"), lax.axis_index("c")
    me, my_chip = 4 * x + 2 * y + c, 2 * x + y
    out = []

    def add(i, src, there, here, dev):
        out.append(pltpu.make_async_remote_copy(src_ref=src, dst_ref=land_ref.at[here if arrival else there], send_sem=send_sems.at[i],
                                                recv_sem=recv_sems.at[i], device_id=dev, device_id_type=_MESH))

    if plan in ("gather", "blocks"):
        for k in range(1, N_DEV):
            dev, p = _relation_peer(k, x, y, c)
            add(k - 1, src_ref.at[p] if plan == "blocks" else src_ref, me, p, dev)
    elif plan == "sibling":
        for j in range(4):
            add(j, src_ref.at[2 * j + 1 - c], j, j, (x, y, 1 - c))
    else:
        for i, k in enumerate((2, 4, 6)):
            (px, py, pc), _ = _relation_peer(k, x, y, c)
            add(i, src_ref.at[2 * px + py], my_chip, 2 * px + py, (px, py, pc))
    return out


def _exchange_start(src, plan, name, after=None):
    n = _PLAN_COPIES[plan]
    land = lax.empty((_PLAN_SLOTS[plan],) + (src.shape if plan == "gather" else src.shape[1:]), src.dtype)
    extra = [] if after is None else [after]

    def body(src_ref, land_ref, *rest):
        send_sems, recv_sems, _, _, token = rest[len(extra) :]
        for cp in _plan_copies(plan, src_ref, land_ref, send_sems, recv_sems, False):
            cp.start()
        token[...] = jnp.zeros_like(token)

    return pl.pallas_call(
        body,
        name=name,
        out_shape=(pltpu.SemaphoreType.DMA((n,)), pltpu.SemaphoreType.DMA((n,)), pltpu.HBM(src.shape, src.dtype),
                   pltpu.HBM(land.shape, land.dtype), jax.ShapeDtypeStruct((8, LANES), f32)),
        in_specs=(_HBM_SPEC, _HBM_SPEC) + (_ANY_SPEC,) * len(extra),
        out_specs=(_SEM_SPEC, _SEM_SPEC, _HBM_SPEC, _HBM_SPEC, pl.BlockSpec(memory_space=pltpu.VMEM)),
        input_output_aliases={0: 2, 1: 3},
        compiler_params=pltpu.CompilerParams(has_side_effects=_EFFECT),
    )(pltpu.with_memory_space_constraint(src, pltpu.HBM), pltpu.with_memory_space_constraint(land, pltpu.HBM), *extra)


def _exchange_wait(started, after, plan, name):
    send_sems, recv_sems, src_thru, land_thru, _ = started

    def body(src_ref, land_ref, send_sems, recv_sems, after_ref, src_dead, got_ref):
        for cp in _plan_copies(plan, src_ref, land_ref, send_sems, recv_sems, True):
            cp.wait_send()
            cp.wait_recv()

    return pl.pallas_call(
        body,
        name=name,
        out_shape=(pltpu.HBM(src_thru.shape, src_thru.dtype), pltpu.HBM(land_thru.shape, land_thru.dtype)),
        in_specs=(_HBM_SPEC, _HBM_SPEC, _SEM_SPEC, _SEM_SPEC, pl.BlockSpec(memory_space=pl.ANY)),
        out_specs=(_HBM_SPEC, _HBM_SPEC),
        input_output_aliases={0: 0, 1: 1},
        compiler_params=pltpu.CompilerParams(has_side_effects=_EFFECT),
    )(src_thru, land_thru, send_sems, recv_sems, after)


def _pair_sum(g, r, name, tc=LANES):
    _, R, Cc = g.shape

    def body(g_ref, r_ref, o_ref):
        o_ref[...] = (g_ref[...].astype(f32) + r_ref[...].astype(f32)).astype(o_ref.dtype)

    return pl.pallas_call(
        body,
        out_shape=jax.ShapeDtypeStruct(r.shape, r.dtype),
        grid=(N_CHIP, Cc // tc),
        in_specs=[pl.BlockSpec((1, R, tc), lambda k, i: (2 * k + lax.axis_index("c"), 0, i)), pl.BlockSpec((1, R, tc), lambda k, i: (k, 0, i))],
        out_specs=pl.BlockSpec((1, R, tc), lambda k, i: (k, 0, i)),
        compiler_params=_cparams(("parallel", "parallel")),
        name=name,
    )(g, r)


def _sum_adam(parts, w, m, v, name, tb, tc=None, own=None, own_slot=None):
    R, Cc = w.shape
    n_parts = parts.shape[0]
    tb = R if tc else min(tb, R)
    tc = tc or Cc
    assert R % tb == 0 and Cc % tc == 0
    bc1 = 1.0 - ADAM_B1**ADAM_STEP
    bc2 = 1.0 - ADAM_B2**ADAM_STEP
    extra = [] if own is None else [own]

    def body(p_ref, w_ref, m_ref, v_ref, *rest):
        g_ref, d_ref, nm_ref, nv_ref = rest[len(extra) :]

        def part(j):
            pj = p_ref[j].astype(f32)
            return pj if own is None else jnp.where(own_slot() == j, rest[0][0].astype(f32), pj)

        g = part(0)
        for j in range(1, n_parts):
            g = g + part(j)
        g_ref[...] = g
        m2 = ADAM_B1 * m_ref[...] + (1.0 - ADAM_B1) * g
        v2 = ADAM_B2 * v_ref[...] + (1.0 - ADAM_B2) * (g * g)
        nm_ref[...] = m2
        nv_ref[...] = v2
        d_ref[...] = -ADAM_LR * ((m2 / bc1) / (jnp.sqrt(v2 / bc2) + ADAM_EPS) + ADAM_WD * w_ref[...])

    blk = pl.BlockSpec((tb, tc), lambda i, j: (i, j))
    o = jax.ShapeDtypeStruct((R, Cc), f32)
    return pl.pallas_call(
        body,
        out_shape=(o, o, o, o),
        grid=(R // tb, Cc // tc),
        in_specs=[pl.BlockSpec((n_parts, tb, tc), lambda i, j: (0, i, j)), blk, blk, blk]
        + [pl.BlockSpec((1, tb, tc), lambda i, j: (own_slot(), i, j))] * len(extra),
        out_specs=(blk, blk, blk, blk),
        compiler_params=_cparams(("parallel", "parallel")),
        name=name,
    )(parts, w, m, v, *extra)


def _my_device():
    return 4 * lax.axis_index("x") + 2 * lax.axis_index("y") + lax.axis_index("c")


def _my_chip():
    return 2 * lax.axis_index("x") + lax.axis_index("y")


_WEIGHTS = ["ln_pre_w", "w_in", "conv_w", "a_log_fwd", "a_log_bwd", "dt_bias_fwd", "dt_bias_bwd", "gdn_norm_w", "w_proj_gdn",
            "gk_w2_fwd", "gk_b2_fwd", "gk_w2_bwd", "gk_b2_bwd", "gla_norm_w", "w_proj_gla", "w_out", "ln_post_w"]


def kernel(x, ln_pre_w, w_in, conv_w, a_log_fwd, a_log_bwd, dt_bias_fwd, dt_bias_bwd, gdn_norm_w, w_proj_gdn, gk_w2_fwd, gk_b2_fwd, gk_w2_bwd, gk_b2_bwd, gla_norm_w, w_proj_gla, w_out, ln_post_w, loss_target, m_ln_pre_w, m_w_in, m_conv_w, m_a_log_fwd, m_a_log_bwd, m_dt_bias_fwd, m_dt_bias_bwd, m_gdn_norm_w, m_w_proj_gdn, m_gk_w2_fwd, m_gk_b2_fwd, m_gk_w2_bwd, m_gk_b2_bwd, m_gla_norm_w, m_w_proj_gla, m_w_out, m_ln_post_w, v_ln_pre_w, v_w_in, v_conv_w, v_a_log_fwd, v_a_log_bwd, v_dt_bias_fwd, v_dt_bias_bwd, v_gdn_norm_w, v_w_proj_gdn, v_gk_w2_fwd, v_gk_b2_fwd, v_gk_w2_bwd, v_gk_b2_bwd, v_gla_norm_w, v_w_proj_gla, v_w_out, v_ln_post_w):
    args = locals()
    W = {n: args[n] for n in _WEIGHTS}
    M = {n: args["m_" + n] for n in _WEIGHTS}
    V = {n: args["v_" + n] for n in _WEIGHTS}

    wp_stack = lambda P: jnp.concatenate([P["w_proj_gdn"], P["w_proj_gla"], P["w_out"]], axis=0)
    sm_pack = lambda P: _pack_small_shard(P["conv_w"][0], P["gk_w2_fwd"][0], P["gk_w2_bwd"][0])
    me = 4 * lax.axis_index("x") + 2 * lax.axis_index("y") + lax.axis_index("c")
    w_in_t, m_in_t, v_in_t = (jnp.transpose(a[0]) for a in (w_in, m_w_in, v_w_in))
    g_in, g_sm = _gather_weights(w_in_t, sm_pack(W))
    wp_own = wp_stack(W).astype(MXU_DTYPE)
    wp_started = _exchange_start(wp_own, "gather", "gather_proj_start", after=g_sm)

    def proj_weights(after):
        own, land = _exchange_wait(wp_started, after, "gather", "gather_proj_wait")
        g_p = lax.dynamic_update_slice(land, own[None], (me, 0, 0, 0))
        return tuple(g_p[:, i].reshape(D_MODEL, D_MODEL) for i in range(3))

    rows_p = 3 * D_MODEL // N_DEV
    sent = {}

    def send_proj_grads(d_wpg, d_wpl, d_wout):
        p_p = jnp.stack([d_wpg, d_wpl, d_wout]).reshape(3, N_DEV, D_MODEL // N_DEV, D_MODEL).transpose(1, 0, 2, 3)
        sent["p_p"] = p_p.reshape(N_DEV, rows_p, D_MODEL)
        sent["started"] = _exchange_start(sent["p_p"], "blocks", "exchange_proj_start")
        return sent["started"][4][0:1, 0:1]

    def send_in_grads(d_wcat):
        p_in = _from_cat(d_wcat).reshape(N_DEV, SHARD_IN, D_MODEL)
        sib = _exchange_start(p_in, "sibling", "exchange_in_sibling_start")
        p_p, land_p = _exchange_wait(sent["started"], sib[4], "blocks", "exchange_proj_wait")
        sent["o_p"] = _sum_adam(land_p, wp_stack(W).reshape(rows_p, D_MODEL), wp_stack(M).reshape(rows_p, D_MODEL),
                                wp_stack(V).reshape(rows_p, D_MODEL), "adam_w_proj", 128, own=p_p, own_slot=_my_device)
        p_in, sib_land = _exchange_wait(sib, sent["o_p"][0], "sibling", "exchange_in_sibling_wait")
        sent["chips"] = _exchange_start(_pair_sum(p_in, sib_land, "pair_sum_w_in", tc=512), "chips", "exchange_in_chips_start")
        return sent["chips"][4]

    wcat = _to_cat(g_in.reshape(N_IN, D_MODEL))
    conv_full = g_sm[:, 0:CONV_K, :].transpose(1, 0, 2).reshape(CONV_K, N_DEV * SM_COLS)
    w2f_full = g_sm[:, 8:24, 0:64].transpose(1, 0, 2).reshape(16, 512)
    w2b_full = g_sm[:, 8:24, 64:128].transpose(1, 0, 2).reshape(16, 512)

    avec = _lane_row(a_log_fwd, a_log_bwd)
    dvec = _lane_row(dt_bias_fwd, dt_bias_bwd)
    G = _local_step(x[0], loss_target[0], wcat, conv_full, avec, dvec, gdn_norm_w, _pad_w2(w2f_full, 32), _pad_w2(w2b_full, 48),
                    gk_b2_fwd, gk_b2_bwd, gla_norm_w, proj_weights, ln_pre_w + wp_started[4][0:1, 0:1], ln_post_w, send_proj_grads,
                    send_in_grads)

    small_started = _exchange_start(_pack_small_grads(G), "blocks", "exchange_small_start")
    q_in, land_in = _exchange_wait(sent["chips"], small_started[4], "chips", "exchange_in_chips_wait")
    o_in = _sum_adam(land_in, w_in_t, m_in_t, v_in_t, "adam_w_in", SHARD_IN, tc=256, own=q_in, own_slot=_my_chip)
    o_p = sent["o_p"]
    p_small, land_small = _exchange_wait(small_started, o_in[0], "blocks", "exchange_small_wait")
    o_small, loss = _small_adam(land_small, p_small, W, M, V)

    res = []
    for kind in range(4):
        per = {name: o_small[name][kind] for name in SMALL_NAMES}
        per["w_in"] = jnp.transpose(o_in[kind])[None]
        pp = o_p[kind].reshape(3, 1, D_MODEL // N_DEV, D_MODEL)
        per["w_proj_gdn"], per["w_proj_gla"], per["w_out"] = pp[0], pp[1], pp[2]
        res.append([per[n] for n in _WEIGHTS])
    return (loss[0, 0], G["grad_x"][None], *res[0], *res[1], *res[2], *res[3])
```

```python
import jax
import jax.numpy as jnp
from jax import lax
from jax.experimental import pallas as pl
from jax.experimental.pallas import tpu as pltpu

f32 = jnp.float32
MXU_DTYPE = jnp.bfloat16

D_MODEL = 1024
CHUNK = 64
NORM_EPS = 1e-6
GDN_HEADS, GDN_DIM = 8, 128
GLA_HEADS, GLA_DK, GLA_DV = 4, 128, 256
GATE_NORMALIZER = 16.0
CONV_K = 5
N_IN = 9280
N_DEV = 8
SHARD_IN = N_IN // N_DEV

N_CAT = 9728
SMALL_OFF = 9216
LANES = 128

ADAM_LR, ADAM_B1, ADAM_B2, ADAM_EPS, ADAM_WD, ADAM_STEP = 0.001, 0.9, 0.999, 1e-08, 0.01, 10

VMEM_LIMIT = 56 * 1024 * 1024


def _cparams(sem=None):
    return pltpu.CompilerParams(dimension_semantics=sem, vmem_limit_bytes=VMEM_LIMIT)


_DN = {"nn": (((1,), (0,)), ((), ())), "nt": (((1,), (1,)), ((), ())), "tn": (((0,), (0,)), ((), ()))}
_BDN = {"nn": (((2,), (1,)), ((0,), (0,))), "nt": (((2,), (2,)), ((0,), (0,))), "tn": (((1,), (1,)), ((0,), (0,)))}


def _dot16(a, b, kind):
    return lax.dot_general(a, b, (_DN if a.ndim == 2 else _BDN)[kind], preferred_element_type=f32)


def _mx(a, b, kind):
    return _dot16(a.astype(MXU_DTYPE), b.astype(MXU_DTYPE), kind)


def _split2(a):
    hi = a.astype(jnp.bfloat16)
    return hi, (a - hi.astype(f32)).astype(jnp.bfloat16)


def _split3(a):
    p1 = a.astype(jnp.bfloat16)
    r = a - p1.astype(f32)
    p2 = r.astype(jnp.bfloat16)
    return p1, p2, (r - p2.astype(f32)).astype(jnp.bfloat16)


def _h3s(a2, b2, kind):
    (ah, al), (bh, bl) = a2, b2
    return _dot16(ah, bh, kind) + (_dot16(ah, bl, kind) + _dot16(al, bh, kind))


def _h3(a, b, kind):
    return _h3s(_split2(a), _split2(b), kind)


def _exact01(t01, x, kind="nn"):
    t = t01.astype(jnp.bfloat16)
    x1, x2, x3 = _split3(x)
    return _dot16(t, x1, kind) + (_dot16(t, x2, kind) + _dot16(t, x3, kind))


def _colsum_col(e):
    ones = jnp.ones(e.shape[:-1] + (LANES,), jnp.bfloat16)
    e1, e2, e3 = _split3(e)
    return (_dot16(e1, ones, "tn") + (_dot16(e2, ones, "tn") + _dot16(e3, ones, "tn")))[..., 0:1]


def _iota3(shape, axis):
    return lax.broadcasted_iota(jnp.int32, shape, axis)


def _dir_masks(B, nf, Cn):
    shp = (B, Cn, Cn)
    bb, ii, jj = _iota3(shp, 0), _iota3(shp, 1), _iota3(shp, 2)
    fwd = bb < nf
    rev = jnp.logical_not(fwd)
    incl = jnp.logical_or(jnp.logical_and(fwd, ii >= jj), jnp.logical_and(rev, ii <= jj))
    strict = jnp.logical_and(incl, ii != jj)
    return incl, strict, ii == jj


def _last_row(x, nf):
    B, Cn = x.shape[0], x.shape[1]
    fwd = _iota3((B, 1, 1), 0) < nf
    return jnp.where(fwd, x[:, Cn - 1 : Cn, :], x[:, 0:1, :])


def _add_at_last_row(x, val, nf):
    B, Cn = x.shape[0], x.shape[1]
    bb, rr = _iota3((B, Cn, 1), 0), _iota3((B, Cn, 1), 1)
    at = jnp.logical_or(jnp.logical_and(bb < nf, rr == Cn - 1), jnp.logical_and(bb >= nf, rr == 0))
    return x + jnp.where(at, val, 0.0)


def _sigmoid(x):
    return jax.nn.sigmoid(x)


def _silu(x):
    return x * _sigmoid(x)


def _silu_grad(x):
    s = _sigmoid(x)
    return s * (1.0 + x * (1.0 - s))


def _softplus(x):
    u = jnp.exp(-jnp.abs(x))
    l1p = jnp.where(u < 1e-3, u * (1.0 - u * (0.5 - u * (1.0 / 3.0))), jnp.log(1.0 + u))
    return jnp.maximum(x, 0.0) + l1p


def _mxr(x):
    return x.astype(MXU_DTYPE).astype(f32)


def _iota2(shape, axis):
    return lax.broadcasted_iota(jnp.int32, shape, axis)


def _matmul(a, b, kind, out_dtype, name, tm, tn, tk, out_t=False, after=None):
    extra = [] if after is None else [after]
    if kind == "nn":
        (M, K), N = a.shape, b.shape[1]
    elif kind == "nt":
        (M, K), N = a.shape, b.shape[0]
    else:
        (K, M), N = a.shape, b.shape[1]
    tm, tn, tk = min(tm, M), min(tn, N), min(tk, K)
    assert M % tm == 0 and N % tn == 0 and K % tk == 0, (name, M, N, K)
    nk = K // tk
    if kind == "tn":
        a_spec = pl.BlockSpec((tk, tm), lambda i, j, k: (k, i))
    else:
        a_spec = pl.BlockSpec((tm, tk), lambda i, j, k: (i, k))
    b_mode = dict(pipeline_mode=pl.Buffered(1)) if (tn == N and tk == K) else {}
    if kind == "nt":
        b_spec = pl.BlockSpec((tn, tk), lambda i, j, k: (j, k), **b_mode)
    else:
        b_spec = pl.BlockSpec((tk, tn), lambda i, j, k: (k, j), **b_mode)

    def body(a_ref, b_ref, *rest):
        o_ref, acc = rest[len(extra)], rest[len(extra) + 1 :]
        p = _mx(a_ref[...], b_ref[...], kind)
        if nk == 1:
            o_ref[...] = (p.T if out_t else p).astype(out_dtype)
        else:
            assert not out_t
            acc_ref = acc[0]
            k = pl.program_id(2)

            @pl.when(k == 0)
            def _():
                acc_ref[...] = p

            @pl.when(k > 0)
            def _():
                acc_ref[...] += p

            @pl.when(k == nk - 1)
            def _():
                o_ref[...] = acc_ref[...].astype(out_dtype)

    return pl.pallas_call(
        body,
        out_shape=jax.ShapeDtypeStruct((N, M) if out_t else (M, N), out_dtype),
        grid=(M // tm, N // tn, nk),
        in_specs=[a_spec, b_spec] + [pl.BlockSpec((8, LANES), lambda i, j, k: (0, 0))] * len(extra),
        out_specs=pl.BlockSpec((tn, tm), lambda i, j, k: (j, i)) if out_t else pl.BlockSpec((tm, tn), lambda i, j, k: (i, j)),
        scratch_shapes=[] if nk == 1 else [pltpu.VMEM((tm, tn), f32)],
        compiler_params=_cparams(("parallel", "parallel", "arbitrary")),
        name=name,
    )(a, b, *extra)


ROW_TILE = 512


def _rms_pre(x, w):
    T = x.shape[0]
    tb = min(ROW_TILE, T)

    def body(x_ref, w_ref, h_ref, ht_ref):
        xv = x_ref[...]
        r = lax.rsqrt(jnp.mean(xv * xv, axis=1, keepdims=True) + NORM_EPS)
        h = xv * r * w_ref[...]
        h_ref[...] = h.astype(h_ref.dtype)
        ht_ref[...] = h.T.astype(ht_ref.dtype)

    return pl.pallas_call(
        body,
        out_shape=(jax.ShapeDtypeStruct((T, D_MODEL), MXU_DTYPE), jax.ShapeDtypeStruct((D_MODEL, T), MXU_DTYPE)),
        grid=(T // tb,),
        in_specs=[pl.BlockSpec((tb, D_MODEL), lambda i: (i, 0)), pl.BlockSpec((1, D_MODEL), lambda i: (0, 0))],
        out_specs=(pl.BlockSpec((tb, D_MODEL), lambda i: (i, 0)), pl.BlockSpec((D_MODEL, tb), lambda i: (0, i))),
        compiler_params=_cparams(("parallel",)),
        name="rms_pre",
    )(x, w)


def _rms_pre_bwd(x, w, dh, dy):
    T = x.shape[0]
    tb = min(ROW_TILE, T)

    def body(x_ref, w_ref, dh_ref, dy_ref, dx_ref, dw_ref):
        i = pl.program_id(0)
        xv, dhv = x_ref[...], dh_ref[...]
        r = lax.rsqrt(jnp.mean(xv * xv, axis=1, keepdims=True) + NORM_EPS)
        dhw = dhv * w_ref[...]
        dx_ref[...] = dy_ref[...] + r * dhw - xv * (r * r * r) * jnp.mean(dhw * xv, axis=1, keepdims=True)
        part = jnp.sum(dhv * xv * r, axis=0, keepdims=True)

        @pl.when(i == 0)
        def _():
            dw_ref[...] = part

        @pl.when(i > 0)
        def _():
            dw_ref[...] += part

    blk = pl.BlockSpec((tb, D_MODEL), lambda i: (i, 0))
    row = pl.BlockSpec((1, D_MODEL), lambda i: (0, 0))
    return pl.pallas_call(
        body,
        out_shape=(jax.ShapeDtypeStruct((T, D_MODEL), f32), jax.ShapeDtypeStruct((1, D_MODEL), f32)),
        grid=(T // tb,),
        in_specs=[blk, row, blk, blk],
        out_specs=(blk, row),
        compiler_params=_cparams(("arbitrary",)),
        name="rms_pre_bwd",
    )(x, w, dh, dy)


SHIFTS = (-2, -1, 1, 2)


def _shifted(x):
    T = x.shape[0]
    t = _iota2(x.shape, 0)
    out = {}
    for d in SHIFTS:
        valid = (t < T - d) if d > 0 else (t >= -d)
        out[d] = jnp.where(valid, pltpu.roll(x, (-d) % T, 0), 0.0)
    return out


def _conv5(u, ush, cw):
    acc = u * cw[2:3, :]
    for j in (0, 1, 3, 4):
        acc = acc + ush[j - 2] * cw[j : j + 1, :]
    return acc


def _gdn_prep(proj, conv_w):
    T = proj.shape[0]
    nh = GDN_HEADS

    def body(u_ref, cw_ref, o_ref):
        j = pl.program_id(0)
        u = _mxr(u_ref[...])
        s = _silu(_conv5(u, _shifted(u), _mxr(cw_ref[...])))

        @pl.when(j < 2 * nh)
        def _():
            o_ref[0] = s * lax.rsqrt(jnp.sum(s * s, axis=1, keepdims=True) + NORM_EPS)

        @pl.when(j >= 2 * nh)
        def _():
            o_ref[0] = s

    return pl.pallas_call(
        body,
        out_shape=jax.ShapeDtypeStruct((3, T, GDN_HEADS * GDN_DIM), f32),
        grid=(3 * nh,),
        in_specs=[pl.BlockSpec((T, GDN_DIM), lambda j: (0, j)), pl.BlockSpec((CONV_K, GDN_DIM), lambda j: (0, j))],
        out_specs=pl.BlockSpec((1, T, GDN_DIM), lambda j: (j // nh, 0, j % nh)),
        compiler_params=_cparams(("parallel",)),
        name="gdn_prep",
    )(proj, conv_w)


def _gdn_prep_bwd(proj, conv_w, dqkv_f, dqkv_b, dproj):
    T = proj.shape[0]
    nh = GDN_HEADS


    def body(u_ref, cw_ref, df_ref, db_ref, _, du_ref, dcw_ref):
        j = pl.program_id(0)
        u, cw = _mxr(u_ref[...]), _mxr(cw_ref[...])
        ush = _shifted(u)
        c = _conv5(u, ush, cw)
        s = _silu(c)
        dn = df_ref[0] + db_ref[0]
        rinv = lax.rsqrt(jnp.sum(s * s, axis=1, keepdims=True) + NORM_EPS)
        ds_norm = rinv * dn - s * (rinv * rinv * rinv) * jnp.sum(dn * s, axis=1, keepdims=True)
        ds = jnp.where(j < 2 * nh, ds_norm, dn)
        dc = _mxr(ds * _silu_grad(c))
        dcsh = _shifted(dc)
        du = dc * cw[2:3, :]
        for jj in range(CONV_K):
            d = jj - 2
            dcw_ref[jj : jj + 1, :] = jnp.sum(dc * (u if d == 0 else ush[d]), axis=0, keepdims=True)
            if d != 0:
                du = du + dcsh[-d] * cw[jj : jj + 1, :]
        du_ref[...] = du.astype(du_ref.dtype)

    qspec = pl.BlockSpec((1, T, GDN_DIM), lambda j: (j // nh, 0, j % nh))
    return pl.pallas_call(
        body,
        out_shape=(jax.ShapeDtypeStruct(dproj.shape, dproj.dtype), jax.ShapeDtypeStruct((CONV_K, 3 * nh * GDN_DIM), f32)),
        grid=(3 * nh,),
        in_specs=[pl.BlockSpec((T, GDN_DIM), lambda j: (0, j)), pl.BlockSpec((CONV_K, GDN_DIM), lambda j: (0, j)), qspec, qspec, _ANY_SPEC],
        out_specs=(pl.BlockSpec((T, GDN_DIM), lambda j: (0, j)), pl.BlockSpec((CONV_K, GDN_DIM), lambda j: (0, j))),
        input_output_aliases={4: 0},
        compiler_params=_cparams(("parallel",)),
        name="gdn_prep_bwd",
    )(proj, conv_w, dqkv_f, dqkv_b, dproj)


def _chunk_tri(n, rev):
    i, j = _iota2((n, n), 0), _iota2((n, n), 1)
    same = jnp.right_shift(i, 6) == jnp.right_shift(j, 6)
    order = (j >= i) if rev else (j <= i)
    return jnp.where(jnp.logical_and(same, order), 1.0, 0.0).astype(f32)


def _gdn_gates(proj, avec, dvec, tb=256):
    T = proj.shape[0]

    def body(s_ref, a_ref, d_ref, g_ref, gc_ref):
        small = s_ref[...]
        lane = _iota2(small.shape, 1)
        lg = -jnp.exp(a_ref[...]) * _softplus(small + d_ref[...])
        beta = _sigmoid(small)
        g_ref[...] = jnp.where(lane < 16, lg, jnp.where(lane < 32, beta, 0.0))
        lgm = jnp.where(lane < 16, lg, 0.0)
        gcf = _exact01(_chunk_tri(tb, False), lgm)
        gcr = _exact01(_chunk_tri(tb, True), lgm)
        gc_ref[...] = jnp.where(lane < 8, gcf, gcr)

    blk = pl.BlockSpec((tb, LANES), lambda i: (i, 0))
    row = pl.BlockSpec((1, LANES), lambda i: (0, 0))
    return pl.pallas_call(
        body,
        out_shape=(jax.ShapeDtypeStruct((T, LANES), f32), jax.ShapeDtypeStruct((T, LANES), f32)),
        grid=(T // tb,),
        in_specs=[pl.BlockSpec((tb, LANES), lambda i: (i, SMALL_OFF // LANES)), row, row],
        out_specs=(blk, blk),
        compiler_params=_cparams(("parallel",)),
        name="gdn_gates",
    )(proj, avec, dvec)


def _gdn_gates_bwd(proj, avec, dvec, dg_f, dg_b, dsmall_gla, dproj):
    T = proj.shape[0]
    tb = min(ROW_TILE, T)
    pad = N_CAT - SMALL_OFF

    def body(s_ref, a_ref, d_ref, gf_ref, gb_ref, dl_ref, _, ds_ref, da_ref, dd_ref):
        i = pl.program_id(0)
        small = s_ref[...]
        lane = _iota2(small.shape, 1)
        dgate = gf_ref[...] + gb_ref[...]
        z = small + d_ref[...]
        nega = -jnp.exp(a_ref[...])
        dz = dgate * nega * _sigmoid(z)
        beta = _sigmoid(small)
        dsm = jnp.where(lane < 16, dz, jnp.where(lane < 32, dgate * beta * (1.0 - beta), 0.0))
        ds_ref[:, :LANES] = (dsm + dl_ref[...]).astype(ds_ref.dtype)
        ds_ref[:, LANES:] = jnp.zeros((tb, pad - LANES), ds_ref.dtype)
        lg = nega * _softplus(z)
        pa = jnp.sum(jnp.where(lane < 16, dgate * lg, 0.0), axis=0, keepdims=True)
        pd = jnp.sum(jnp.where(lane < 16, dz, 0.0), axis=0, keepdims=True)

        @pl.when(i == 0)
        def _():
            da_ref[...] = pa
            dd_ref[...] = pd

        @pl.when(i > 0)
        def _():
            da_ref[...] += pa
            dd_ref[...] += pd

    blk = pl.BlockSpec((tb, LANES), lambda i: (i, 0))
    row = pl.BlockSpec((1, LANES), lambda i: (0, 0))
    return pl.pallas_call(
        body,
        out_shape=(jax.ShapeDtypeStruct(dproj.shape, dproj.dtype), jax.ShapeDtypeStruct((1, LANES), f32), jax.ShapeDtypeStruct((1, LANES), f32)),
        grid=(T // tb,),
        in_specs=[pl.BlockSpec((tb, LANES), lambda i: (i, SMALL_OFF // LANES)), row, row, blk, blk, blk, _ANY_SPEC],
        out_specs=(pl.BlockSpec((tb, pad), lambda i: (i, DP_SMALL_BLK)), row, row),
        input_output_aliases={6: 0},
        compiler_params=_cparams(("arbitrary",)),
        name="gdn_gates_bwd",
    )(proj, avec, dvec, dg_f, dg_b, dsmall_gla, dproj)


GDN_HB = 8


def _inv_unit_lower(L, eye):
    A = eye - L
    P2 = _split2(L)
    for _ in range(3):
        P2 = _split2(_h3s(P2, P2, "nn"))
        A = A + _h3s(_split2(A), P2, "nn")
    Ph = P2[0]
    for _ in range(2):
        Ph = _dot16(Ph, Ph, "nn").astype(jnp.bfloat16)
        A = A + _dot16(A.astype(jnp.bfloat16), Ph, "nn")
    return A


def _lane_col(blk, idx):
    lane = _iota2(blk.shape, 1)
    return jnp.sum(jnp.where(lane == idx, blk, 0.0), axis=1, keepdims=True)


def _out_norm(o_f, o_b, proj, gate_blk, w, hd, name):
    T, W = o_f.shape
    tb = min(ROW_TILE, T)
    nh = W // hd

    def body(of_ref, ob_ref, z_ref, w_ref, y_ref):
        wv = w_ref[...]
        for i in range(nh):
            sl = slice(i * hd, (i + 1) * hd)
            o = of_ref[:, sl] + ob_ref[:, sl]
            r = lax.rsqrt(jnp.mean(o * o, axis=1, keepdims=True) + NORM_EPS)
            y_ref[:, sl] = (o * r * wv * _silu(z_ref[:, sl])).astype(y_ref.dtype)

    blk = pl.BlockSpec((tb, W), lambda i: (i, 0))
    return pl.pallas_call(
        body,
        out_shape=jax.ShapeDtypeStruct((T, W), MXU_DTYPE),
        grid=(T // tb,),
        in_specs=[blk, blk, pl.BlockSpec((tb, W), lambda i: (i, gate_blk)), pl.BlockSpec((1, hd), lambda i: (0, 0))],
        out_specs=blk,
        compiler_params=_cparams(("parallel",)),
        name=name,
    )(o_f, o_b, proj, w)


def _out_norm_bwd(dy, o_f, o_b, proj, gate_blk, w, hd, name, dproj):
    T, W = o_f.shape
    tb = min(ROW_TILE, T)
    nh = W // hd

    def body(dy_ref, of_ref, ob_ref, z_ref, w_ref, _, do_ref, dz_ref, dw_ref):
        i = pl.program_id(0)
        wv = w_ref[...]
        dw = jnp.zeros((1, hd), f32)
        for a in range(nh):
            sl = slice(a * hd, (a + 1) * hd)
            o = of_ref[:, sl] + ob_ref[:, sl]
            z = z_ref[:, sl]
            dyv = dy_ref[:, sl]
            r = lax.rsqrt(jnp.mean(o * o, axis=1, keepdims=True) + NORM_EPS)
            orr = o * r
            dn = dyv * _silu(z)
            dz_ref[:, sl] = (dyv * orr * wv * _silu_grad(z)).astype(dz_ref.dtype)
            dw = dw + jnp.sum(dn * orr, axis=0, keepdims=True)
            dnw = dn * wv
            do_ref[:, sl] = r * dnw - o * (r * r * r) * jnp.mean(dnw * o, axis=1, keepdims=True)

        @pl.when(i == 0)
        def _():
            dw_ref[...] = dw

        @pl.when(i > 0)
        def _():
            dw_ref[...] += dw

    blk = pl.BlockSpec((tb, W), lambda i: (i, 0))
    row = pl.BlockSpec((1, hd), lambda i: (0, 0))
    return pl.pallas_call(
        body,
        out_shape=(jax.ShapeDtypeStruct((T, W), f32), jax.ShapeDtypeStruct(dproj.shape, dproj.dtype), jax.ShapeDtypeStruct((1, hd), f32)),
        grid=(T // tb,),
        in_specs=[blk, blk, blk, pl.BlockSpec((tb, W), lambda i: (i, gate_blk)), row, _ANY_SPEC],
        out_specs=(blk, pl.BlockSpec((tb, W), lambda i: (i, gate_blk)), row),
        input_output_aliases={5: 1},
        compiler_params=_cparams(("arbitrary",)),
        name=name,
    )(dy, o_f, o_b, proj, w, dproj)


def _gla_prep(proj, w2f, w2b, b2f, b2b, tb=256):
    T = proj.shape[0]
    W = GLA_HEADS * GLA_DK

    def body(s_ref, wf_ref, wb_ref, bf_ref, bb_ref, gf_ref, gb_ref):
        small = s_ref[...]
        gkf = -_softplus(-(_mx(small, wf_ref[...], "nn") + bf_ref[...])) * (1.0 / GATE_NORMALIZER)
        gkb = -_softplus(-(_mx(small, wb_ref[...], "nn") + bb_ref[...])) * (1.0 / GATE_NORMALIZER)
        gf_ref[...] = _exact01(_chunk_tri(tb, False), gkf)
        gb_ref[...] = _exact01(_chunk_tri(tb, True), gkb)

    blk = pl.BlockSpec((tb, W), lambda i: (i, 0))
    wsp = pl.BlockSpec((LANES, W), lambda i: (0, 0))
    row = pl.BlockSpec((1, W), lambda i: (0, 0))
    return pl.pallas_call(
        body,
        out_shape=(jax.ShapeDtypeStruct((T, W), f32), jax.ShapeDtypeStruct((T, W), f32)),
        grid=(T // tb,),
        in_specs=[pl.BlockSpec((tb, LANES), lambda i: (i, SMALL_OFF // LANES)), wsp, wsp, row, row],
        out_specs=(blk, blk),
        compiler_params=_cparams(("parallel",)),
        name="gla_prep",
    )(proj, w2f, w2b, b2f, b2b)


def _gla_prep_bwd(proj, w2f, w2b, b2f, b2b, dgk_f, dgk_b):
    T = proj.shape[0]
    tb = min(ROW_TILE, T)
    W = GLA_HEADS * GLA_DK

    def body(s_ref, wf_ref, wb_ref, bf_ref, bb_ref, df_ref, db_ref, ds_ref, dwf_ref, dwb_ref, dbf_ref, dbb_ref):
        i = pl.program_id(0)
        small = s_ref[...]
        dsm = jnp.zeros((tb, LANES), f32)
        parts = []
        for w_ref, b_ref, d_ref in ((wf_ref, bf_ref, df_ref), (wb_ref, bb_ref, db_ref)):
            pre = _mx(small, w_ref[...], "nn") + b_ref[...]
            dpre = d_ref[...] * (1.0 / GATE_NORMALIZER) * _sigmoid(-pre)
            dsm = dsm + _mx(dpre, w_ref[...], "nt")
            parts.append((_mx(small, dpre, "tn"), jnp.sum(dpre, axis=0, keepdims=True)))
        ds_ref[...] = dsm

        @pl.when(i == 0)
        def _():
            dwf_ref[...], dbf_ref[...] = parts[0]
            dwb_ref[...], dbb_ref[...] = parts[1]

        @pl.when(i > 0)
        def _():
            dwf_ref[...] += parts[0][0]
            dbf_ref[...] += parts[0][1]
            dwb_ref[...] += parts[1][0]
            dbb_ref[...] += parts[1][1]

    blk = pl.BlockSpec((tb, W), lambda i: (i, 0))
    wsp = pl.BlockSpec((LANES, W), lambda i: (0, 0))
    row = pl.BlockSpec((1, W), lambda i: (0, 0))
    return pl.pallas_call(
        body,
        out_shape=(jax.ShapeDtypeStruct((T, LANES), f32), jax.ShapeDtypeStruct((LANES, W), f32), jax.ShapeDtypeStruct((LANES, W), f32),
                   jax.ShapeDtypeStruct((1, W), f32), jax.ShapeDtypeStruct((1, W), f32)),
        grid=(T // tb,),
        in_specs=[pl.BlockSpec((tb, LANES), lambda i: (i, SMALL_OFF // LANES)), wsp, wsp, row, row, blk, blk],
        out_specs=(pl.BlockSpec((tb, LANES), lambda i: (i, 0)), wsp, wsp, row, row),
        compiler_params=_cparams(("arbitrary",)),
        name="gla_prep_bwd",
    )(proj, w2f, w2b, b2f, b2b, dgk_f, dgk_b)


GATE_A_BLK, GATE_B_BLK, Z_BLK, G_B_BLK = 6144 // 1024, 7168 // 1024, 3072 // 1024, 8192 // 1024
DP_GATES_BLK, DP_GLA_QKV_BLK, DP_SMALL_BLK = 6144 // 2048, 4096 // 2048, SMALL_OFF // 512
_ANY_SPEC = pl.BlockSpec(memory_space=pl.ANY)


def _mid(og, obg, proj, x, target, wpg, wpl, wout, w_post, tb=256):
    T = x.shape[0]
    Dm = D_MODEL

    def body(og_ref, obg_ref, ga_ref, gb_ref, x_ref, t_ref, wpg_ref, wpl_ref, wo_ref, wp_ref,
             loss_ref, dwp_ref, dy_ref, m_ref, dout_ref, dya_ref, dyb_ref, dgates_ref, dog_ref, dobg_ref):
        i = pl.program_id(0)
        ya = _mx(og_ref[...], wpg_ref[...], "nn")
        yb = _mx(obg_ref[...], wpl_ref[...], "nn")
        sa, sb = _sigmoid(ga_ref[...]), _sigmoid(gb_ref[...])
        m = sa * ya + sb * yb
        m_ref[...] = m.astype(m_ref.dtype)
        out = _mx(m, wo_ref[...], "nn")
        r = lax.rsqrt(jnp.mean(out * out, axis=1, keepdims=True) + NORM_EPS)
        wp = wp_ref[...]
        e = x_ref[...] + out * r * wp - t_ref[...]
        dy = e * (1.0 / Dm)
        dy_ref[...] = dy
        lpart = jnp.sum(e * e, axis=0, keepdims=True)
        wpart = jnp.sum(dy * out * r, axis=0, keepdims=True)
        dyn = dy * wp
        dout = r * dyn - out * (r * r * r) * jnp.mean(dyn * out, axis=1, keepdims=True)
        dout_ref[...] = dout.astype(dout_ref.dtype)
        dm = _mx(dout, wo_ref[...], "nt")
        dya, dyb = dm * sa, dm * sb
        dya_ref[...] = dya.astype(dya_ref.dtype)
        dyb_ref[...] = dyb.astype(dyb_ref.dtype)
        dgates_ref[:, :Dm] = (dm * ya * sa * (1.0 - sa)).astype(dgates_ref.dtype)
        dgates_ref[:, Dm:] = (dm * yb * sb * (1.0 - sb)).astype(dgates_ref.dtype)
        dog_ref[...] = _mx(dya, wpg_ref[...], "nt")
        dobg_ref[...] = _mx(dyb, wpl_ref[...], "nt")

        @pl.when(i == 0)
        def _():
            loss_ref[...] = lpart
            dwp_ref[...] = wpart

        @pl.when(i > 0)
        def _():
            loss_ref[...] += lpart
            dwp_ref[...] += wpart

    blk = pl.BlockSpec((tb, Dm), lambda i: (i, 0))
    row = pl.BlockSpec((1, Dm), lambda i: (0, 0))
    wsp = pl.BlockSpec((Dm, Dm), lambda i: (0, 0))
    act = jax.ShapeDtypeStruct((T, Dm), MXU_DTYPE)
    big = jax.ShapeDtypeStruct((T, Dm), f32)
    vec = jax.ShapeDtypeStruct((1, Dm), f32)
    return pl.pallas_call(
        body,
        out_shape=(vec, vec, big, act, act, act, act, jax.ShapeDtypeStruct((T, N_CAT), MXU_DTYPE), big, big),
        grid=(T // tb,),
        in_specs=[blk, blk, pl.BlockSpec((tb, Dm), lambda i: (i, GATE_A_BLK)), pl.BlockSpec((tb, Dm), lambda i: (i, GATE_B_BLK)),
                  blk, blk, wsp, wsp, wsp, row],
        out_specs=(row, row) + (blk,) * 5 + (pl.BlockSpec((tb, 2 * Dm), lambda i: (i, DP_GATES_BLK)), blk, blk),
        compiler_params=_cparams(("arbitrary",)),
        name="mid",
    )(og, obg, proj, proj, x, target, wpg, wpl, wout, w_post)


def _heads(ref, lead, n, width, rows=slice(None)):
    return [ref[lead + (rows, slice(i * width, (i + 1) * width))] for i in range(n)]


class _GdnBatch:
    def __init__(self, q, k, v, gcol, grow, beta, nf, A=None):
        B, Cn = q.shape[0], q.shape[1]
        self.nf = nf
        self.incl, self.strict, eye = _dir_masks(B, nf, Cn)
        self.decay = jnp.where(self.incl, jnp.exp(jnp.where(self.incl, gcol - grow, 0.0)), 0.0)
        self.k, self.v, self.beta = k, v, beta
        self.qs = q * (GDN_DIM**-0.5)
        self.kb = k * beta
        self.L = jnp.where(self.strict, _mx(self.kb, k, "nt") * self.decay, 0.0)
        if A is None:
            A = _inv_unit_lower(self.L, jnp.where(eye, 1.0, 0.0).astype(f32))
        self.A = A
        self.A2 = _split2(A)
        self.eg = jnp.exp(gcol)
        self.kbg = self.kb * self.eg
        self.sol = _h3s(self.A2, _split2(jnp.concatenate([v * beta, self.kbg], axis=2)), "nn")
        self.u = self.sol[:, :, :GDN_DIM]
        self.w = self.sol[:, :, GDN_DIM:]
        self.attn = jnp.where(self.incl, _mx(self.qs, k, "nt") * self.decay, 0.0)
        self.qd = self.qs * self.eg
        gl = _last_row(gcol, nf)
        self.ekd = jnp.exp(gl - gcol)
        self.kd = k * self.ekd
        self.egl = jnp.exp(gl)


def _gdn_chain_inputs(qf_ref, qb_ref, gf_ref, gb_ref, cf_ref, cb_ref, b, hb):
    qkv = [jnp.stack(_heads(qf_ref, (i,), hb, GDN_DIM) + _heads(qb_ref, (i,), hb, GDN_DIM)) for i in range(3)]
    cols, rows, betas = [], [], []
    for d, (g_ref, c_ref) in enumerate(((gf_ref, cf_ref), (gb_ref, cb_ref))):
        gates, gc = g_ref[...], c_ref[...]
        for hh in range(hb):
            h = b * hb + hh
            col = _lane_col(gc, h + 8 * d)
            cols.append(col)
            rows.append(jnp.broadcast_to(col, (CHUNK, LANES)).T[0:1, :])
            betas.append(_lane_col(gates, 16 + h + 8 * d))
    return qkv[0], qkv[1], qkv[2], jnp.stack(cols), jnp.stack(rows), jnp.stack(betas)


GDN_CPS_FWD, GDN_CPS_BWD = 2, 1


def _chunk_views(refs, s_f, s_b):
    out = []
    for i, r in enumerate(refs):
        s = s_f if i % 2 == 0 else s_b
        rows = pl.ds(s * CHUNK, CHUNK)
        out.append(r.at[pl.ds(s, 1)] if len(r.shape) == 4 else (r.at[:, rows, :] if len(r.shape) == 3 else r.at[rows, :]))
    return out


def _gdn_specs(cps, hb):
    R, W = cps * CHUNK, hb * GDN_DIM
    qsp = lambda ci: pl.BlockSpec((3, R, W), lambda n, b: (0, ci(n), b))
    gsp = lambda ci: pl.BlockSpec((R, LANES), lambda n, b: (ci(n), 0))
    osp = lambda ci: pl.BlockSpec((R, W), lambda n, b: (ci(n), b))
    ssp = lambda ci: pl.BlockSpec((cps, hb, GDN_DIM, GDN_DIM), lambda n, b: (ci(n), b, 0, 0))
    asp = lambda ci: pl.BlockSpec((cps, hb, CHUNK, CHUNK), lambda n, b: (ci(n), b, 0, 0))
    return qsp, gsp, osp, ssp, asp


def _gdn_fwd(qkvn, gates, gc):
    T = qkvn.shape[1]
    N = T // CHUNK
    nh, hb = GDN_HEADS, GDN_HB
    cps = min(GDN_CPS_FWD, N)
    NB = N // cps

    def body(*refs):
        n, b = pl.program_id(0), pl.program_id(1)
        s_scr = refs[-1]

        @pl.when(n == 0)
        def _():
            s_scr[0, pl.ds(b * hb, hb)] = jnp.zeros((hb, GDN_DIM, GDN_DIM), f32)
            s_scr[1, pl.ds(b * hb, hb)] = jnp.zeros((hb, GDN_DIM, GDN_DIM), f32)

        for s in range(cps):
            chunk(*_chunk_views(refs[:-1], s, cps - 1 - s), s_scr, b)

    def chunk(qf_ref, qb_ref, gf_ref, gb_ref, cf_ref, cb_ref, of_ref, ob_ref, sf_ref, sb_ref, af_ref, ab_ref, s_scr, b):
        hs = pl.ds(b * hb, hb)
        q, k, v, gcol, grow, beta = _gdn_chain_inputs(qf_ref, qb_ref, gf_ref, gb_ref, cf_ref, cb_ref, b, hb)
        ck = _GdnBatch(q, k, v, gcol, grow, beta, hb)
        S = jnp.concatenate([s_scr[0, hs], s_scr[1, hs]], axis=0)
        sf_ref[0], sb_ref[0] = S[:hb], S[hb:]
        af_ref[0], ab_ref[0] = ck.A[:hb], ck.A[hb:]
        v_new = ck.u - _mx(ck.w, S, "nn")
        o = _mx(ck.qd, S, "nn") + _mx(ck.attn, v_new, "nn")
        S_new = S * ck.egl + _mx(ck.kd, v_new, "tn")
        for hh in range(hb):
            sl = slice(hh * GDN_DIM, (hh + 1) * GDN_DIM)
            of_ref[:, sl] = o[hh]
            ob_ref[:, sl] = o[hb + hh]
        s_scr[0, hs] = S_new[:hb]
        s_scr[1, hs] = S_new[hb:]

    fw, bw = (lambda n: n), (lambda n: NB - 1 - n)
    qsp, gsp, osp, ssp, asp = _gdn_specs(cps, hb)
    o_shape = jax.ShapeDtypeStruct((T, nh * GDN_DIM), f32)
    s_shape = jax.ShapeDtypeStruct((N, nh, GDN_DIM, GDN_DIM), f32)
    a_shape = jax.ShapeDtypeStruct((N, nh, CHUNK, CHUNK), f32)
    return pl.pallas_call(
        body,
        out_shape=(o_shape, o_shape, s_shape, s_shape, a_shape, a_shape),
        grid=(NB, nh // hb),
        in_specs=[qsp(fw), qsp(bw), gsp(fw), gsp(bw), gsp(fw), gsp(bw)],
        out_specs=(osp(fw), osp(bw), ssp(fw), ssp(bw), asp(fw), asp(bw)),
        scratch_shapes=[pltpu.VMEM((2, nh, GDN_DIM, GDN_DIM), f32)],
        compiler_params=_cparams(("arbitrary", "arbitrary")),
        name="gdn_scan_fwd",
    )(qkvn, qkvn, gates, gates, gc, gc)


def _gdn_bwd(qkvn, gates, gc, st_f, st_b, a_f, a_b, do):
    T = qkvn.shape[1]
    N = T // CHUNK
    nh, hb = GDN_HEADS, GDN_HB
    nb = nh // hb
    cps = min(GDN_CPS_BWD, N)
    NB = N // cps

    def body(*refs):
        n, b = pl.program_id(0), pl.program_id(1)
        ds_scr = refs[-1]

        @pl.when(n == 0)
        def _():
            ds_scr[0, pl.ds(b * hb, hb)] = jnp.zeros((hb, GDN_DIM, GDN_DIM), f32)
            ds_scr[1, pl.ds(b * hb, hb)] = jnp.zeros((hb, GDN_DIM, GDN_DIM), f32)

        for s in range(cps):
            chunk(*_chunk_views(refs[:-1], cps - 1 - s, s), ds_scr, b)

    def chunk(qf_ref, qb_ref, gf_ref, gb_ref, cf_ref, cb_ref, sf_ref, sb_ref, af_ref, ab_ref, dof_ref, dob_ref,
              dqf_ref, dqb_ref, dgf_ref, dgb_ref, ds_scr, b):
        hs = pl.ds(b * hb, hb)

        @pl.when(b == 0)
        def _():
            dgf_ref[...] = jnp.zeros((CHUNK, LANES), f32)
            dgb_ref[...] = jnp.zeros((CHUNK, LANES), f32)

        q, k, v, gcol, grow, beta = _gdn_chain_inputs(qf_ref, qb_ref, gf_ref, gb_ref, cf_ref, cb_ref, b, hb)
        ck = _GdnBatch(q, k, v, gcol, grow, beta, hb, A=jnp.concatenate([af_ref[0], ab_ref[0]], axis=0))
        S = jnp.concatenate([sf_ref[0], sb_ref[0]], axis=0)
        dS = jnp.concatenate([ds_scr[0, hs], ds_scr[1, hs]], axis=0)
        dov = jnp.stack(_heads(dof_ref, (), hb, GDN_DIM) + _heads(dob_ref, (), hb, GDN_DIM))
        v_new = ck.u - _mx(ck.w, S, "nn")
        d_vnew = _mx(ck.attn, dov, "tn") + _mx(ck.kd, dS, "nn")
        d_attn = jnp.where(ck.incl, _mx(dov, v_new, "nt"), 0.0)
        d_qd = _mx(dov, S, "nt")
        d_kd = _mx(v_new, dS, "nt")
        d_gl = jnp.sum(jnp.sum(dS * S, axis=2, keepdims=True), axis=1, keepdims=True) * ck.egl
        dS_new = dS * ck.egl + _mx(ck.qd, dov, "tn") - _mx(ck.w, d_vnew, "tn")
        ds_scr[0, hs] = dS_new[:hb]
        ds_scr[1, hs] = dS_new[hb:]
        d_w = -_mx(d_vnew, S, "nt")
        d_rhs = _h3s(ck.A2, _split2(jnp.concatenate([d_vnew, d_w], axis=2)), "tn")
        d_vb, d_kbg = d_rhs[:, :, :GDN_DIM], d_rhs[:, :, GDN_DIM:]
        dL = -jnp.where(ck.strict, _h3(d_rhs, ck.sol, "nt"), 0.0)
        P = dL * ck.decay
        Q = d_attn * ck.decay
        d_kb = _mx(P, k, "nn") + d_kbg * ck.eg
        d_qs = _mx(Q, k, "nn") + d_qd * ck.eg
        d_k = _mx(P, ck.kb, "tn") + _mx(Q, ck.qs, "tn") + d_kd * ck.ekd + d_kb * ck.beta
        E = dL * ck.L + d_attn * ck.attn
        kdsum = jnp.sum(d_kd * ck.kd, axis=2, keepdims=True)
        d_gc = (jnp.sum(E, axis=2, keepdims=True) - _colsum_col(E)
                + jnp.sum(d_kbg * ck.kbg, axis=2, keepdims=True)
                + jnp.sum(d_qd * ck.qd, axis=2, keepdims=True) - kdsum)
        d_gc = _add_at_last_row(d_gc, d_gl + jnp.sum(kdsum, axis=1, keepdims=True), hb)
        d_beta = jnp.sum(d_vb * v, axis=2, keepdims=True) + jnp.sum(d_kb * k, axis=2, keepdims=True)
        d_q = d_qs * (GDN_DIM**-0.5)
        d_v = d_vb * ck.beta
        lane = _iota2((CHUNK, LANES), 1)
        for d, (dq_ref, dg_ref) in enumerate(((dqf_ref, dgf_ref), (dqb_ref, dgb_ref))):
            acc = dg_ref[...]
            for hh in range(hb):
                c, h = d * hb + hh, b * hb + hh
                sl = slice(hh * GDN_DIM, (hh + 1) * GDN_DIM)
                dq_ref[0, :, sl] = d_q[c]
                dq_ref[1, :, sl] = d_k[c]
                dq_ref[2, :, sl] = d_v[c]
                acc = jnp.where(lane == h + 8 * d, d_gc[c], acc)
                acc = jnp.where(lane == 16 + h + 8 * d, d_beta[c], acc)
            dg_ref[...] = acc

        @pl.when(b == nb - 1)
        def _():
            for d, dg_ref in enumerate((dgf_ref, dgb_ref)):
                blk = dg_ref[...]
                dg_ref[...] = jnp.where(lane < 16, _exact01(_chunk_tri(CHUNK, d == 0), blk), blk)

    fw, bw = (lambda n: NB - 1 - n), (lambda n: n)
    qsp, gsp, osp, ssp, asp = _gdn_specs(cps, hb)
    dq_shape = jax.ShapeDtypeStruct((3, T, nh * GDN_DIM), f32)
    dg_shape = jax.ShapeDtypeStruct((T, LANES), f32)
    return pl.pallas_call(
        body,
        out_shape=(dq_shape, dq_shape, dg_shape, dg_shape),
        grid=(NB, nb),
        in_specs=[qsp(fw), qsp(bw), gsp(fw), gsp(bw), gsp(fw), gsp(bw), ssp(fw), ssp(bw), asp(fw), asp(bw), osp(fw), osp(bw)],
        out_specs=(qsp(fw), qsp(bw), gsp(fw), gsp(bw)),
        scratch_shapes=[pltpu.VMEM((2, nh, GDN_DIM, GDN_DIM), f32)],
        compiler_params=_cparams(("arbitrary", "arbitrary")),
        name="gdn_scan_bwd",
    )(qkvn, qkvn, gates, gates, gc, gc, st_f, st_b, a_f, a_b, do, do)


class _GlaBatch:
    def __init__(self, q, k, G, nf):
        B, Cn = q.shape[0], q.shape[1]
        self.incl, _, _ = _dir_masks(B, nf, Cn)
        self.eG = jnp.exp(G)
        self.enG = jnp.exp(-G)
        self.qg = q * (GLA_DK**-0.5) * self.eG
        self.kg = k * self.enG
        self.attn = jnp.where(self.incl, _mx(self.qg, self.kg, "nt"), 0.0)
        gl = _last_row(G, nf)
        self.ekd = jnp.exp(gl - G)
        self.kd = k * self.ekd
        self.egl = jnp.exp(gl)


GLA_CPS = 4


def _chunk_rows(s):
    return slice(s * CHUNK, (s + 1) * CHUNK)


def _gla_chain_inputs(refs_f, refs_b, rows_f, rows_b):
    nh = GLA_HEADS
    out = []
    for i, width in enumerate((GLA_DK, GLA_DK, GLA_DV, GLA_DK)):
        out.append(jnp.stack(_heads(refs_f[i], (), nh, width, rows_f) + _heads(refs_b[i], (), nh, width, rows_b)))
    return out


def _gla_specs(rows, ci):
    nh = GLA_HEADS
    return [
        pl.BlockSpec((rows, nh * GLA_DK), lambda n: (ci(n), 4096 // (nh * GLA_DK))),
        pl.BlockSpec((rows, nh * GLA_DK), lambda n: (ci(n), 4608 // (nh * GLA_DK))),
        pl.BlockSpec((rows, nh * GLA_DV), lambda n: (ci(n), 5120 // (nh * GLA_DV))),
        pl.BlockSpec((rows, nh * GLA_DK), lambda n: (ci(n), 0)),
    ]


def _gla_fwd(proj, G_f, G_b):
    T = proj.shape[0]
    N = T // CHUNK
    nh = GLA_HEADS
    cps = min(GLA_CPS, N)
    NB = N // cps

    def body(qf, kf, vf, gf, qb, kb, vb, gb, of_ref, ob_ref, sf_ref, sb_ref, s_scr):
        n = pl.program_id(0)

        @pl.when(n == 0)
        def _():
            s_scr[...] = jnp.zeros((2 * nh, GLA_DV, GLA_DK), f32)

        for s in range(cps):
            sb = cps - 1 - s
            q, k, v, G = _gla_chain_inputs((qf, kf, vf, gf), (qb, kb, vb, gb), _chunk_rows(s), _chunk_rows(sb))
            ck = _GlaBatch(q, k, G, nh)
            St = s_scr[...]
            sf_ref[s], sb_ref[sb] = St[:nh].astype(sf_ref.dtype), St[nh:].astype(sb_ref.dtype)
            o = _mx(ck.qg, St, "nt") + _mx(ck.attn, v, "nn")
            for i in range(nh):
                sl = slice(i * GLA_DV, (i + 1) * GLA_DV)
                of_ref[_chunk_rows(s), sl] = o[i]
                ob_ref[_chunk_rows(sb), sl] = o[nh + i]
            s_scr[...] = St * ck.egl + _mx(v, ck.kd, "tn")

    fw, bw = (lambda n: n), (lambda n: NB - 1 - n)
    osp = lambda ci: pl.BlockSpec((cps * CHUNK, nh * GLA_DV), lambda n: (ci(n), 0))
    ssp = lambda ci: pl.BlockSpec((cps, nh, GLA_DV, GLA_DK), lambda n: (ci(n), 0, 0, 0))
    o_shape = jax.ShapeDtypeStruct((T, nh * GLA_DV), f32)
    s_shape = jax.ShapeDtypeStruct((N, nh, GLA_DV, GLA_DK), MXU_DTYPE)
    return pl.pallas_call(
        body,
        out_shape=(o_shape, o_shape, s_shape, s_shape),
        grid=(NB,),
        in_specs=_gla_specs(cps * CHUNK, fw) + _gla_specs(cps * CHUNK, bw),
        out_specs=(osp(fw), osp(bw), ssp(fw), ssp(bw)),
        scratch_shapes=[pltpu.VMEM((2 * nh, GLA_DV, GLA_DK), f32)],
        compiler_params=_cparams(("arbitrary",)),
        name="gla_scan_fwd",
    )(proj, proj, proj, G_f, proj, proj, proj, G_b)


def _gla_bwd(proj, G_f, G_b, st_f, st_b, do):
    T = proj.shape[0]
    N = T // CHUNK
    nh = GLA_HEADS
    cps = min(GLA_CPS, N)
    NB = N // cps

    def body(qf, kf, vf, gf, sf_ref, dof_ref, qb, kb, vb, gb, sb_ref, dob_ref,
             dqf_ref, dkf_ref, dvf_ref, dgf_ref, dqb_ref, dkb_ref, dvb_ref, dgb_ref, ds_scr):
        n = pl.program_id(0)

        @pl.when(n == 0)
        def _():
            ds_scr[...] = jnp.zeros((2 * nh, GLA_DV, GLA_DK), f32)

        shp = (2 * nh, CHUNK, CHUNK)
        bb, ii, jj = _iota3(shp, 0), _iota3(shp, 1), _iota3(shp, 2)
        tri = jnp.where(jnp.logical_or(jnp.logical_and(bb < nh, jj >= ii), jnp.logical_and(bb >= nh, jj <= ii)), 1.0, 0.0)
        for s in range(cps):
            sf = cps - 1 - s
            rows = (_chunk_rows(sf), _chunk_rows(s))
            q, k, v, G = _gla_chain_inputs((qf, kf, vf, gf), (qb, kb, vb, gb), *rows)
            ck = _GlaBatch(q, k, G, nh)
            St = jnp.concatenate([sf_ref[sf], sb_ref[s]], axis=0).astype(f32)
            dSt = ds_scr[...]
            dov = jnp.stack(_heads(dof_ref, (), nh, GLA_DV, rows[0]) + _heads(dob_ref, (), nh, GLA_DV, rows[1]))
            d_attn = jnp.where(ck.incl, _mx(dov, v, "nt"), 0.0)
            d_qg = _mx(dov, St, "nn") + _mx(d_attn, ck.kg, "nn")
            d_kg = _mx(d_attn, ck.qg, "tn")
            d_v = _mx(ck.attn, dov, "tn") + _mx(ck.kd, dSt, "nt")
            d_kd = _mx(v, dSt, "nn")
            ds_scr[...] = dSt * ck.egl + _mx(dov, ck.qg, "tn")
            kdd = d_kd * ck.kd
            d_gl = jnp.sum(dSt * St, axis=1, keepdims=True) * ck.egl + jnp.sum(kdd, axis=1, keepdims=True)
            d_q = d_qg * ck.eG * (GLA_DK**-0.5)
            d_k = d_kg * ck.enG + d_kd * ck.ekd
            d_gk = _exact01(tri, _add_at_last_row(d_qg * ck.qg - d_kg * ck.kg - kdd, d_gl, nh))
            for d, (dq_ref, dk_ref, dv_ref, dg_ref) in enumerate(((dqf_ref, dkf_ref, dvf_ref, dgf_ref), (dqb_ref, dkb_ref, dvb_ref, dgb_ref))):
                for i in range(nh):
                    c = d * nh + i
                    ks, vs = slice(i * GLA_DK, (i + 1) * GLA_DK), slice(i * GLA_DV, (i + 1) * GLA_DV)
                    dq_ref[rows[d], ks] = d_q[c]
                    dk_ref[rows[d], ks] = d_k[c]
                    dv_ref[rows[d], vs] = d_v[c]
                    dg_ref[rows[d], ks] = d_gk[c]

    R = cps * CHUNK

    def in_specs(ci):
        return _gla_specs(R, ci) + [
            pl.BlockSpec((cps, nh, GLA_DV, GLA_DK), lambda n: (ci(n), 0, 0, 0)),
            pl.BlockSpec((R, nh * GLA_DV), lambda n: (ci(n), 0)),
        ]

    def out_specs(ci):
        ksp = pl.BlockSpec((R, nh * GLA_DK), lambda n: (ci(n), 0))
        return [ksp, ksp, pl.BlockSpec((R, nh * GLA_DV), lambda n: (ci(n), 0)), ksp]

    fw, bw = (lambda n: NB - 1 - n), (lambda n: n)
    k_shape = jax.ShapeDtypeStruct((T, nh * GLA_DK), f32)
    v_shape = jax.ShapeDtypeStruct((T, nh * GLA_DV), f32)
    return pl.pallas_call(
        body,
        out_shape=(k_shape, k_shape, v_shape, k_shape) * 2,
        grid=(NB,),
        in_specs=in_specs(fw) + in_specs(bw),
        out_specs=tuple(out_specs(fw) + out_specs(bw)),
        scratch_shapes=[pltpu.VMEM((2 * nh, GLA_DV, GLA_DK), f32)],
        compiler_params=_cparams(("arbitrary",)),
        name="gla_scan_bwd",
    )(proj, proj, proj, G_f, st_f, do, proj, proj, proj, G_b, st_b, do)


def _gla_dsum(dq_f, dq_b, dk_f, dk_b, dv_f, dv_b, dproj):
    T = dq_f.shape[0]
    tb = min(ROW_TILE, T)
    wk, wv = GLA_HEADS * GLA_DK, GLA_HEADS * GLA_DV

    def body(qf, qb, kf, kb, vf, vb, _, o_ref):
        o_ref[:, :wk] = (qf[...] + qb[...]).astype(o_ref.dtype)
        o_ref[:, wk : 2 * wk] = (kf[...] + kb[...]).astype(o_ref.dtype)
        o_ref[:, 2 * wk :] = (vf[...] + vb[...]).astype(o_ref.dtype)

    ksp = pl.BlockSpec((tb, wk), lambda i: (i, 0))
    vsp = pl.BlockSpec((tb, wv), lambda i: (i, 0))
    return pl.pallas_call(
        body,
        out_shape=jax.ShapeDtypeStruct(dproj.shape, dproj.dtype),
        grid=(T // tb,),
        in_specs=[ksp, ksp, ksp, ksp, vsp, vsp, _ANY_SPEC],
        out_specs=pl.BlockSpec((tb, 2 * wk + wv), lambda i: (i, DP_GLA_QKV_BLK)),
        input_output_aliases={6: 0},
        compiler_params=_cparams(("parallel",)),
        name="gla_dsum",
    )(dq_f, dq_b, dk_f, dk_b, dv_f, dv_b, dproj)


def _local_step(x, target, wcat, conv_w, avec, dvec, gdn_norm_w, w2f, w2b, b2f, b2b, gla_norm_w, proj_weights, w_pre, w_post,
                send_proj_grads=None, send_in_grads=None):
    h, h_t = _rms_pre(x, w_pre)
    proj = _matmul(h, wcat, "nt", f32, "proj_in", 2048, 512, 1024)
    qkvn = _gdn_prep(proj, conv_w)
    gates, gc = _gdn_gates(proj, avec, dvec)
    oa_f, oa_b, sa_f, sa_b, inv_f, inv_b = _gdn_fwd(qkvn, gates, gc)
    og = _out_norm(oa_f, oa_b, proj, Z_BLK, gdn_norm_w, GDN_DIM, "gdn_out")
    G_f, G_b = _gla_prep(proj, w2f, w2b, b2f, b2b)
    ob_f, ob_b, sb_f, sb_b = _gla_fwd(proj, G_f, G_b)
    obg = _out_norm(ob_f, ob_b, proj, G_B_BLK, gla_norm_w, GLA_DV, "gla_out")
    wpg, wpl, wout = proj_weights(obg)
    (loss_row, d_wpost, dy, m, dout, dya, dyb, dproj, dog, dobg) = _mid(og, obg, proj, x, target, wpg, wpl, wout, w_post)

    d_wout = _matmul(m, dout, "tn", MXU_DTYPE, "dw_out", 1024, 512, 2048)
    d_wpg = _matmul(og, dya, "tn", MXU_DTYPE, "dw_proj_gdn", 1024, 512, 2048)
    d_wpl = _matmul(obg, dyb, "tn", MXU_DTYPE, "dw_proj_gla", 1024, 512, 2048)
    if send_proj_grads is not None:
        gla_norm_w = gla_norm_w + send_proj_grads(d_wpg, d_wpl, d_wout)

    do_b, dproj, d_gla_norm = _out_norm_bwd(dobg, ob_f, ob_b, proj, G_B_BLK, gla_norm_w, GLA_DV, "gla_out_bwd", dproj)
    dq_f, dk_f, dv_f, dgk_f, dq_b, dk_b, dv_b, dgk_b = _gla_bwd(proj, G_f, G_b, sb_f, sb_b, do_b)
    dproj = _gla_dsum(dq_f, dq_b, dk_f, dk_b, dv_f, dv_b, dproj)
    dsmall_gla, d_w2f, d_w2b, d_b2f, d_b2b = _gla_prep_bwd(proj, w2f, w2b, b2f, b2b, dgk_f, dgk_b)

    do_a, dproj, d_gdn_norm = _out_norm_bwd(dog, oa_f, oa_b, proj, Z_BLK, gdn_norm_w, GDN_DIM, "gdn_out_bwd", dproj)
    dqkv_f, dqkv_b, dg_f, dg_b = _gdn_bwd(qkvn, gates, gc, sa_f, sa_b, inv_f, inv_b, do_a)
    dproj, d_conv = _gdn_prep_bwd(proj, conv_w, dqkv_f, dqkv_b, dproj)
    dproj, d_alog, d_dtb = _gdn_gates_bwd(proj, avec, dvec, dg_f, dg_b, dsmall_gla, dproj)

    d_wcat = _matmul(h_t, dproj, "nn", MXU_DTYPE, "dw_in", 1024, 512, 2048, out_t=True)
    started = None if send_in_grads is None else send_in_grads(d_wcat)
    dh = _matmul(dproj, wcat, "nn", f32, "dh", 512, 1024, N_CAT, after=started)
    grad_x, d_wpre = _rms_pre_bwd(x, w_pre, dh, dy)
    return dict(loss_row=loss_row, grad_x=grad_x, d_wcat=d_wcat, d_conv=d_conv, d_alog=d_alog, d_dtb=d_dtb,
                d_gdn_norm=d_gdn_norm, d_wpg=d_wpg, d_w2f=d_w2f, d_w2b=d_w2b, d_b2f=d_b2f, d_b2b=d_b2b,
                d_gla_norm=d_gla_norm, d_wpl=d_wpl, d_wout=d_wout, d_wpost=d_wpost, d_wpre=d_wpre)


def _to_cat(wf):
    pad = jnp.zeros((N_CAT - N_IN,) + wf.shape[1:], wf.dtype)
    return jnp.concatenate([wf[:4096], wf[4128:6176], wf[7232:9280], wf[6176:7200], wf[4096:4128], wf[7200:7232], pad], axis=0)


def _from_cat(wc):
    return jnp.concatenate([wc[:4096], wc[9216:9248], wc[4096:6144], wc[8192:9216], wc[9248:9280], wc[6144:8192]], axis=0)


def _lane_row(*pieces):
    row = jnp.concatenate(pieces, axis=1)
    return jnp.pad(row, ((0, 0), (0, LANES - row.shape[1])))


def _pad_w2(w2, row0):
    return jnp.pad(w2, ((row0, LANES - row0 - w2.shape[0]), (0, 0)))


SM_ROWS, SM_COLS = 32, 384


def _pack_small_shard(conv, w2f, w2b):
    lead = conv.shape[:-2]
    z = lambda r, c: jnp.zeros(lead + (r, c), f32)
    top = jnp.concatenate([conv, z(8 - CONV_K, SM_COLS)], axis=-2)
    mid = jnp.concatenate([w2f, w2b, z(16, SM_COLS - 128)], axis=-1)
    return jnp.concatenate([top, mid, z(SM_ROWS - 24, SM_COLS)], axis=-2)


SMALL_REGION = {
    "ln_pre_w": (0, 1, 0, 1024), "ln_post_w": (1, 1, 0, 1024), "gk_b2_fwd": (2, 1, 0, 512), "gk_b2_bwd": (2, 1, 512, 512),
    "gla_norm_w": (3, 1, 0, 256), "gdn_norm_w": (3, 1, 256, 128), "a_log_fwd": (4, 1, 0, 8), "a_log_bwd": (5, 1, 0, 8),
    "dt_bias_fwd": (6, 1, 0, 8), "dt_bias_bwd": (7, 1, 0, 8), "conv_w": (8, CONV_K, 0, SM_COLS),
    "gk_w2_fwd": (16, 16, 0, 64), "gk_w2_bwd": (16, 16, 128, 64),
}
SMALL_NAMES = tuple(SMALL_REGION)
SMALL_SHARDED = ("conv_w", "gk_w2_fwd", "gk_w2_bwd")
PACK_ROWS, PACK_COLS = 32, 1024
LOSS_AT = (13, 0)


def _region(name):
    r0, nr, l0, nl = SMALL_REGION[name]
    return slice(r0, r0 + nr), slice(l0, l0 + nl)


def _pack_small_grads(G):
    names = ("d_wpre", "d_wpost", "d_b2f", "d_b2b", "d_gla_norm", "d_gdn_norm", "d_alog", "d_dtb", "loss_row", "d_conv", "d_w2f", "d_w2b")

    def body(wpre, wpost, b2f, b2b, glan, gdnn, alog, dtb, lrow, conv, w2f, w2b, o_ref):
        o_ref[...] = jnp.zeros(o_ref.shape, f32)
        loss = 0.5 * jnp.sum(lrow[...], axis=1, keepdims=True) * (1.0 / D_MODEL)
        whole = {"ln_pre_w": wpre, "ln_post_w": wpost, "gk_b2_fwd": b2f, "gk_b2_bwd": b2b, "gla_norm_w": glan, "gdn_norm_w": gdnn}
        for j in range(N_DEV):
            for name, ref in whole.items():
                o_ref[(j,) + _region(name)] = ref[...]
            o_ref[(j,) + _region("a_log_fwd")] = alog[0:1, 0:8]
            o_ref[(j,) + _region("a_log_bwd")] = alog[0:1, 8:16]
            o_ref[(j,) + _region("dt_bias_fwd")] = dtb[0:1, 0:8]
            o_ref[(j,) + _region("dt_bias_bwd")] = dtb[0:1, 8:16]
            o_ref[(j,) + _region("conv_w")] = conv[:, j * SM_COLS : (j + 1) * SM_COLS]
            o_ref[(j,) + _region("gk_w2_fwd")] = w2f[32:48, j * 64 : (j + 1) * 64]
            o_ref[(j,) + _region("gk_w2_bwd")] = w2b[48:64, j * 64 : (j + 1) * 64]
            o_ref[j, LOSS_AT[0] : LOSS_AT[0] + 1, LOSS_AT[1] : LOSS_AT[1] + 1] = loss

    return pl.pallas_call(
        body, out_shape=jax.ShapeDtypeStruct((N_DEV, PACK_ROWS, PACK_COLS), f32), name="pack_small_grads",
    )(*[G[n] for n in names])


def _small_adam(land, own, W, M, V):
    bc1 = 1.0 - ADAM_B1**ADAM_STEP
    bc2 = 1.0 - ADAM_B2**ADAM_STEP
    n = len(SMALL_NAMES)

    def body(land_ref, own_ref, *refs):
        w_refs, m_refs, v_refs = refs[0:n], refs[n : 2 * n], refs[2 * n : 3 * n]
        outs, loss_ref, g_scr = refs[3 * n : 7 * n], refs[7 * n], refs[7 * n + 1]
        me = _my_device()
        g = jnp.where(me == 0, own_ref[0], land_ref[0])
        for j in range(1, N_DEV):
            g = g + jnp.where(me == j, own_ref[j], land_ref[j])
        g_scr[...] = g
        loss_ref[...] = g_scr[LOSS_AT[0] : LOSS_AT[0] + 1, LOSS_AT[1] : LOSS_AT[1] + 1]
        for i, name in enumerate(SMALL_NAMES):
            gp = g_scr[_region(name)]
            lead = (0,) if name in SMALL_SHARDED else (Ellipsis,)
            w, m, v = w_refs[i][lead], m_refs[i][lead], v_refs[i][lead]
            m2 = ADAM_B1 * m + (1.0 - ADAM_B1) * gp
            v2 = ADAM_B2 * v + (1.0 - ADAM_B2) * (gp * gp)
            delta = -ADAM_LR * ((m2 / bc1) / (jnp.sqrt(v2 / bc2) + ADAM_EPS) + ADAM_WD * w)
            for k, val in enumerate((gp, delta, m2, v2)):
                outs[4 * i + k][lead] = val

    shapes = [jax.ShapeDtypeStruct(W[name].shape, f32) for name in SMALL_NAMES for _ in range(4)]
    res = pl.pallas_call(
        body,
        out_shape=tuple(shapes) + (jax.ShapeDtypeStruct((1, 1), f32),),
        scratch_shapes=[pltpu.VMEM((PACK_ROWS, PACK_COLS), f32)],
        name="adam_small",
    )(land, own, *[P[name] for P in (W, M, V) for name in SMALL_NAMES])
    return {name: res[4 * i : 4 * i + 4] for i, name in enumerate(SMALL_NAMES)}, res[4 * n]


_MESH = pl.DeviceIdType.MESH


N_BIG_COPIES, N_SMALL_COPIES = 9, 7


def _gather_weights(w_in_s, sm_s):
    shapes = (w_in_s.shape, sm_s.shape)
    dtypes = (MXU_DTYPE, f32)

    def body(win_ref, sm_ref, gin_ref, gsm_ref, send_sems, recv_sems):
        x, y, c = lax.axis_index("x"), lax.axis_index("y"), lax.axis_index("c")
        me, sibling = (x, y, c), (x, y, 1 - c)
        chips = [(1 - x, y), (x, 1 - y), (1 - x, 1 - y)]

        def idx(px, py, pc):
            return 4 * px + 2 * py + pc

        for r in range(w_in_s.shape[1] // LANES):
            cols = slice(r * LANES, (r + 1) * LANES)
            gin_ref[idx(*me), :, cols] = win_ref[:, cols].astype(MXU_DTYPE)
        gsm_ref[idx(*me)] = sm_ref[...]

        def copy(sem, blk, to):
            return pltpu.make_async_remote_copy(src_ref=blk, dst_ref=blk, send_sem=send_sems.at[sem], recv_sem=recv_sems.at[sem],
                                                device_id=to, device_id_type=_MESH)

        def small(k, block, to):
            return copy(N_BIG_COPIES + k, gsm_ref.at[idx(*block)], to)

        half = w_in_s.shape[1] // 2
        xn, yn, dg = (1 - x, y), (x, 1 - y), (1 - x, 1 - y)

        def big(sem, block, to, part=None):
            blk = gin_ref.at[idx(*block)]
            return copy(sem, blk if part is None else blk.at[:, pl.ds(part * half, half)], to)

        started = [big(0, me, sibling), big(1, me, (*xn, c)), big(2, me, (*yn, c)), small(0, me, sibling)]
        started += [small(1 + j, me, (*chip, c)) for j, chip in enumerate(chips)]
        for cp in started:
            cp.start()

        def then(arrived, forwards):
            arrived.wait_recv()
            for cp in forwards:
                cp.start()
            started.extend(forwards)

        then(big(1, (*xn, c), me), [big(3, (*xn, c), (*yn, c), 1), big(5, (*xn, c), sibling)])
        then(big(2, (*yn, c), me), [big(4, (*yn, c), (*xn, c), 0), big(6, (*yn, c), sibling)])
        then(big(4, (*dg, c), me, 0), [big(7, (*dg, c), sibling, 0)])
        then(big(3, (*dg, c), me, 1), [big(8, (*dg, c), sibling, 1)])
        for j, chip in enumerate(chips):
            then(small(1 + j, (*chip, c), me), [small(4 + j, (*chip, c), sibling)])
        big(0, sibling, me).wait_recv()
        big(5, (*xn, 1 - c), me).wait_recv()
        big(6, (*yn, 1 - c), me).wait_recv()
        big(7, (*dg, 1 - c), me, 0).wait_recv()
        big(8, (*dg, 1 - c), me, 1).wait_recv()
        small(0, sibling, me).wait_recv()
        for j, chip in enumerate(chips):
            small(4 + j, (*chip, 1 - c), me).wait_recv()
        for cp in started:
            cp.wait_send()

    vm = pl.BlockSpec(memory_space=pltpu.VMEM)
    return pl.pallas_call(
        body,
        out_shape=tuple(jax.ShapeDtypeStruct((N_DEV,) + s, d) for s, d in zip(shapes, dtypes)),
        in_specs=[vm, vm],
        out_specs=(vm, vm),
        scratch_shapes=[pltpu.SemaphoreType.DMA((N_BIG_COPIES + N_SMALL_COPIES,)), pltpu.SemaphoreType.DMA((N_BIG_COPIES + N_SMALL_COPIES,))],
        compiler_params=pltpu.CompilerParams(vmem_limit_bytes=VMEM_LIMIT),
        name="gather_weights",
    )(w_in_s, sm_s)


N_CHIP = 4
_EFFECT = pltpu.SideEffectType.DATAFLOW_SIDE_EFFECTING
_HBM_SPEC = pl.BlockSpec(memory_space=pltpu.HBM)
_SEM_SPEC = pl.BlockSpec(memory_space=pltpu.SEMAPHORE)


def _relation_peer(k, x, y, c):
    px = (1 - x) if (k & 4) else x
    py = (1 - y) if (k & 2) else y
    pc = (1 - c) if (k & 1) else c
    return (px, py, pc), 4 * px + 2 * py + pc


_PLAN_COPIES = {"gather": N_DEV - 1, "blocks": N_DEV - 1, "sibling": 4, "chips": 3}
_PLAN_SLOTS = {"gather": N_DEV, "blocks": N_DEV, "sibling": 4, "chips": 4}


def _plan_copies(plan, src_ref, land_ref, send_sems, recv_sems, arrival):
    x, y, c = lax.axis_index("x"), lax.axis_index("y"), lax.axis_index("c")
    me, my_chip = 4 * x + 2 * y + c, 2 * x + y
    out = []

    def add(i, src, there, here, dev):
        out.append(pltpu.make_async_remote_copy(src_ref=src, dst_ref=land_ref.at[here if arrival else there], send_sem=send_sems.at[i],
                                                recv_sem=recv_sems.at[i], device_id=dev, device_id_type=_MESH))

    if plan in ("gather", "blocks"):
        for k in range(1, N_DEV):
            dev, p = _relation_peer(k, x, y, c)
            add(k - 1, src_ref.at[p] if plan == "blocks" else src_ref, me, p, dev)
    elif plan == "sibling":
        for j in range(4):
            add(j, src_ref.at[2 * j + 1 - c], j, j, (x, y, 1 - c))
    else:
        for i, k in enumerate((2, 4, 6)):
            (px, py, pc), _ = _relation_peer(k, x, y, c)
            add(i, src_ref.at[2 * px + py], my_chip, 2 * px + py, (px, py, pc))
    return out


def _exchange_start(src, plan, name, after=None):
    n = _PLAN_COPIES[plan]
    land = lax.empty((_PLAN_SLOTS[plan],) + (src.shape if plan == "gather" else src.shape[1:]), src.dtype)
    extra = [] if after is None else [after]

    def body(src_ref, land_ref, *rest):
        send_sems, recv_sems, _, _, token = rest[len(extra) :]
        for cp in _plan_copies(plan, src_ref, land_ref, send_sems, recv_sems, False):
            cp.start()
        token[...] = jnp.zeros_like(token)

    return pl.pallas_call(
        body,
        name=name,
        out_shape=(pltpu.SemaphoreType.DMA((n,)), pltpu.SemaphoreType.DMA((n,)), pltpu.HBM(src.shape, src.dtype),
                   pltpu.HBM(land.shape, land.dtype), jax.ShapeDtypeStruct((8, LANES), f32)),
        in_specs=(_HBM_SPEC, _HBM_SPEC) + (_ANY_SPEC,) * len(extra),
        out_specs=(_SEM_SPEC, _SEM_SPEC, _HBM_SPEC, _HBM_SPEC, pl.BlockSpec(memory_space=pltpu.VMEM)),
        input_output_aliases={0: 2, 1: 3},
        compiler_params=pltpu.CompilerParams(has_side_effects=_EFFECT),
    )(pltpu.with_memory_space_constraint(src, pltpu.HBM), pltpu.with_memory_space_constraint(land, pltpu.HBM), *extra)


def _exchange_wait(started, after, plan, name):
    send_sems, recv_sems, src_thru, land_thru, _ = started

    def body(src_ref, land_ref, send_sems, recv_sems, after_ref, src_dead, got_ref):
        for cp in _plan_copies(plan, src_ref, land_ref, send_sems, recv_sems, True):
            cp.wait_send()
            cp.wait_recv()

    return pl.pallas_call(
        body,
        name=name,
        out_shape=(pltpu.HBM(src_thru.shape, src_thru.dtype), pltpu.HBM(land_thru.shape, land_thru.dtype)),
        in_specs=(_HBM_SPEC, _HBM_SPEC, _SEM_SPEC, _SEM_SPEC, pl.BlockSpec(memory_space=pl.ANY)),
        out_specs=(_HBM_SPEC, _HBM_SPEC),
        input_output_aliases={0: 0, 1: 1},
        compiler_params=pltpu.CompilerParams(has_side_effects=_EFFECT),
    )(src_thru, land_thru, send_sems, recv_sems, after)


def _pair_sum(g, r, name, tc=LANES):
    _, R, Cc = g.shape

    def body(g_ref, r_ref, o_ref):
        o_ref[...] = (g_ref[...].astype(f32) + r_ref[...].astype(f32)).astype(o_ref.dtype)

    return pl.pallas_call(
        body,
        out_shape=jax.ShapeDtypeStruct(r.shape, r.dtype),
        grid=(N_CHIP, Cc // tc),
        in_specs=[pl.BlockSpec((1, R, tc), lambda k, i: (2 * k + lax.axis_index("c"), 0, i)), pl.BlockSpec((1, R, tc), lambda k, i: (k, 0, i))],
        out_specs=pl.BlockSpec((1, R, tc), lambda k, i: (k, 0, i)),
        compiler_params=_cparams(("parallel", "parallel")),
        name=name,
    )(g, r)


def _sum_adam(parts, w, m, v, name, tb, tc=None, own=None, own_slot=None):
    R, Cc = w.shape
    n_parts = parts.shape[0]
    tb = R if tc else min(tb, R)
    tc = tc or Cc
    assert R % tb == 0 and Cc % tc == 0
    bc1 = 1.0 - ADAM_B1**ADAM_STEP
    bc2 = 1.0 - ADAM_B2**ADAM_STEP
    extra = [] if own is None else [own]

    def body(p_ref, w_ref, m_ref, v_ref, *rest):
        g_ref, d_ref, nm_ref, nv_ref = rest[len(extra) :]

        def part(j):
            pj = p_ref[j].astype(f32)
            return pj if own is None else jnp.where(own_slot() == j, rest[0][0].astype(f32), pj)

        g = part(0)
        for j in range(1, n_parts):
            g = g + part(j)
        g_ref[...] = g
        m2 = ADAM_B1 * m_ref[...] + (1.0 - ADAM_B1) * g
        v2 = ADAM_B2 * v_ref[...] + (1.0 - ADAM_B2) * (g * g)
        nm_ref[...] = m2
        nv_ref[...] = v2
        d_ref[...] = -ADAM_LR * ((m2 / bc1) / (jnp.sqrt(v2 / bc2) + ADAM_EPS) + ADAM_WD * w_ref[...])

    blk = pl.BlockSpec((tb, tc), lambda i, j: (i, j))
    o = jax.ShapeDtypeStruct((R, Cc), f32)
    return pl.pallas_call(
        body,
        out_shape=(o, o, o, o),
        grid=(R // tb, Cc // tc),
        in_specs=[pl.BlockSpec((n_parts, tb, tc), lambda i, j: (0, i, j)), blk, blk, blk]
        + [pl.BlockSpec((1, tb, tc), lambda i, j: (own_slot(), i, j))] * len(extra),
        out_specs=(blk, blk, blk, blk),
        compiler_params=_cparams(("parallel", "parallel")),
        name=name,
    )(parts, w, m, v, *extra)


def _my_device():
    return 4 * lax.axis_index("x") + 2 * lax.axis_index("y") + lax.axis_index("c")


def _my_chip():
    return 2 * lax.axis_index("x") + lax.axis_index("y")


_WEIGHTS = ["ln_pre_w", "w_in", "conv_w", "a_log_fwd", "a_log_bwd", "dt_bias_fwd", "dt_bias_bwd", "gdn_norm_w", "w_proj_gdn",
            "gk_w2_fwd", "gk_b2_fwd", "gk_w2_bwd", "gk_b2_bwd", "gla_norm_w", "w_proj_gla", "w_out", "ln_post_w"]


def kernel(x, ln_pre_w, w_in, conv_w, a_log_fwd, a_log_bwd, dt_bias_fwd, dt_bias_bwd, gdn_norm_w, w_proj_gdn, gk_w2_fwd, gk_b2_fwd, gk_w2_bwd, gk_b2_bwd, gla_norm_w, w_proj_gla, w_out, ln_post_w, loss_target, m_ln_pre_w, m_w_in, m_conv_w, m_a_log_fwd, m_a_log_bwd, m_dt_bias_fwd, m_dt_bias_bwd, m_gdn_norm_w, m_w_proj_gdn, m_gk_w2_fwd, m_gk_b2_fwd, m_gk_w2_bwd, m_gk_b2_bwd, m_gla_norm_w, m_w_proj_gla, m_w_out, m_ln_post_w, v_ln_pre_w, v_w_in, v_conv_w, v_a_log_fwd, v_a_log_bwd, v_dt_bias_fwd, v_dt_bias_bwd, v_gdn_norm_w, v_w_proj_gdn, v_gk_w2_fwd, v_gk_b2_fwd, v_gk_w2_bwd, v_gk_b2_bwd, v_gla_norm_w, v_w_proj_gla, v_w_out, v_ln_post_w):
    args = locals()
    W = {n: args[n] for n in _WEIGHTS}
    M = {n: args["m_" + n] for n in _WEIGHTS}
    V = {n: args["v_" + n] for n in _WEIGHTS}

    wp_stack = lambda P: jnp.concatenate([P["w_proj_gdn"], P["w_proj_gla"], P["w_out"]], axis=0)
    sm_pack = lambda P: _pack_small_shard(P["conv_w"][0], P["gk_w2_fwd"][0], P["gk_w2_bwd"][0])
    me = 4 * lax.axis_index("x") + 2 * lax.axis_index("y") + lax.axis_index("c")
    w_in_t, m_in_t, v_in_t = (jnp.transpose(a[0]) for a in (w_in, m_w_in, v_w_in))
    g_in, g_sm = _gather_weights(w_in_t, sm_pack(W))
    wp_own = wp_stack(W).astype(MXU_DTYPE)
    wp_started = _exchange_start(wp_own, "gather", "gather_proj_start", after=g_sm)

    def proj_weights(after):
        own, land = _exchange_wait(wp_started, after, "gather", "gather_proj_wait")
        g_p = lax.dynamic_update_slice(land, own[None], (me, 0, 0, 0))
        return tuple(g_p[:, i].reshape(D_MODEL, D_MODEL) for i in range(3))

    rows_p = 3 * D_MODEL // N_DEV
    sent = {}

    def send_proj_grads(d_wpg, d_wpl, d_wout):
        p_p = jnp.stack([d_wpg, d_wpl, d_wout]).reshape(3, N_DEV, D_MODEL // N_DEV, D_MODEL).transpose(1, 0, 2, 3)
        sent["p_p"] = p_p.reshape(N_DEV, rows_p, D_MODEL)
        sent["started"] = _exchange_start(sent["p_p"], "blocks", "exchange_proj_start")
        return sent["started"][4][0:1, 0:1]

    def send_in_grads(d_wcat):
        p_in = _from_cat(d_wcat).reshape(N_DEV, SHARD_IN, D_MODEL)
        sib = _exchange_start(p_in, "sibling", "exchange_in_sibling_start")
        p_p, land_p = _exchange_wait(sent["started"], sib[4], "blocks", "exchange_proj_wait")
        sent["o_p"] = _sum_adam(land_p, wp_stack(W).reshape(rows_p, D_MODEL), wp_stack(M).reshape(rows_p, D_MODEL),
                                wp_stack(V).reshape(rows_p, D_MODEL), "adam_w_proj", 128, own=p_p, own_slot=_my_device)
        p_in, sib_land = _exchange_wait(sib, sent["o_p"][0], "sibling", "exchange_in_sibling_wait")
        sent["chips"] = _exchange_start(_pair_sum(p_in, sib_land, "pair_sum_w_in", tc=512), "chips", "exchange_in_chips_start")
        return sent["chips"][4]

    wcat = _to_cat(g_in.reshape(N_IN, D_MODEL))
    conv_full = g_sm[:, 0:CONV_K, :].transpose(1, 0, 2).reshape(CONV_K, N_DEV * SM_COLS)
    w2f_full = g_sm[:, 8:24, 0:64].transpose(1, 0, 2).reshape(16, 512)
    w2b_full = g_sm[:, 8:24, 64:128].transpose(1, 0, 2).reshape(16, 512)

    avec = _lane_row(a_log_fwd, a_log_bwd)
    dvec = _lane_row(dt_bias_fwd, dt_bias_bwd)
    G = _local_step(x[0], loss_target[0], wcat, conv_full, avec, dvec, gdn_norm_w, _pad_w2(w2f_full, 32), _pad_w2(w2b_full, 48),
                    gk_b2_fwd, gk_b2_bwd, gla_norm_w, proj_weights, ln_pre_w + wp_started[4][0:1, 0:1], ln_post_w, send_proj_grads,
                    send_in_grads)

    small_started = _exchange_start(_pack_small_grads(G), "blocks", "exchange_small_start")
    q_in, land_in = _exchange_wait(sent["chips"], small_started[4], "chips", "exchange_in_chips_wait")
    o_in = _sum_adam(land_in, w_in_t, m_in_t, v_in_t, "adam_w_in", SHARD_IN, tc=256, own=q_in, own_slot=_my_chip)
    o_p = sent["o_p"]
    p_small, land_small = _exchange_wait(small_started, o_in[0], "blocks", "exchange_small_wait")
    o_small, loss = _small_adam(land_small, p_small, W, M, V)

    res = []
    for kind in range(4):
        per = {name: o_small[name][kind] for name in SMALL_NAMES}
        per["w_in"] = jnp.transpose(o_in[kind])[None]
        pp = o_p[kind].reshape(3, 1, D_MODEL // N_DEV, D_MODEL)
        per["w_proj_gdn"], per["w_proj_gla"], per["w_out"] = pp[0], pp[1], pp[2]
        res.append([per[n] for n in _WEIGHTS])
    return (loss[0, 0], G["grad_x"][None], *res[0], *res[1], *res[2], *res[3])
```

```python
import jax
import jax.numpy as jnp
from jax import lax
from jax.experimental import pallas as pl
from jax.experimental.pallas import tpu as pltpu

f32 = jnp.float32
MXU_DTYPE = jnp.bfloat16

D_MODEL = 1024
CHUNK = 64
NORM_EPS = 1e-6
GDN_HEADS, GDN_DIM = 8, 128
GLA_HEADS, GLA_DK, GLA_DV = 4, 128, 256
GATE_NORMALIZER = 16.0
CONV_K = 5
N_IN = 9280
N_DEV = 8
SHARD_IN = N_IN // N_DEV

N_CAT = 9728
SMALL_OFF = 9216
LANES = 128

ADAM_LR, ADAM_B1, ADAM_B2, ADAM_EPS, ADAM_WD, ADAM_STEP = 0.001, 0.9, 0.999, 1e-08, 0.01, 10

VMEM_LIMIT = 56 * 1024 * 1024


def _cparams(sem=None):
    return pltpu.CompilerParams(dimension_semantics=sem, vmem_limit_bytes=VMEM_LIMIT)


_DN = {"nn": (((1,), (0,)), ((), ())), "nt": (((1,), (1,)), ((), ())), "tn": (((0,), (0,)), ((), ()))}
_BDN = {"nn": (((2,), (1,)), ((0,), (0,))), "nt": (((2,), (2,)), ((0,), (0,))), "tn": (((1,), (1,)), ((0,), (0,)))}


def _dot16(a, b, kind):
    return lax.dot_general(a, b, (_DN if a.ndim == 2 else _BDN)[kind], preferred_element_type=f32)


def _mx(a, b, kind):
    return _dot16(a.astype(MXU_DTYPE), b.astype(MXU_DTYPE), kind)


def _split2(a):
    hi = a.astype(jnp.bfloat16)
    return hi, (a - hi.astype(f32)).astype(jnp.bfloat16)


def _split3(a):
    p1 = a.astype(jnp.bfloat16)
    r = a - p1.astype(f32)
    p2 = r.astype(jnp.bfloat16)
    return p1, p2, (r - p2.astype(f32)).astype(jnp.bfloat16)


def _h3s(a2, b2, kind):
    (ah, al), (bh, bl) = a2, b2
    return _dot16(ah, bh, kind) + (_dot16(ah, bl, kind) + _dot16(al, bh, kind))


def _h3(a, b, kind):
    return _h3s(_split2(a), _split2(b), kind)


def _exact01(t01, x, kind="nn"):
    t = t01.astype(jnp.bfloat16)
    x1, x2, x3 = _split3(x)
    return _dot16(t, x1, kind) + (_dot16(t, x2, kind) + _dot16(t, x3, kind))


def _colsum_col(e):
    ones = jnp.ones(e.shape[:-1] + (LANES,), jnp.bfloat16)
    e1, e2, e3 = _split3(e)
    return (_dot16(e1, ones, "tn") + (_dot16(e2, ones, "tn") + _dot16(e3, ones, "tn")))[..., 0:1]


def _iota3(shape, axis):
    return lax.broadcasted_iota(jnp.int32, shape, axis)


def _dir_masks(B, nf, Cn):
    shp = (B, Cn, Cn)
    bb, ii, jj = _iota3(shp, 0), _iota3(shp, 1), _iota3(shp, 2)
    fwd = bb < nf
    rev = jnp.logical_not(fwd)
    incl = jnp.logical_or(jnp.logical_and(fwd, ii >= jj), jnp.logical_and(rev, ii <= jj))
    strict = jnp.logical_and(incl, ii != jj)
    return incl, strict, ii == jj


def _last_row(x, nf):
    B, Cn = x.shape[0], x.shape[1]
    fwd = _iota3((B, 1, 1), 0) < nf
    return jnp.where(fwd, x[:, Cn - 1 : Cn, :], x[:, 0:1, :])


def _add_at_last_row(x, val, nf):
    B, Cn = x.shape[0], x.shape[1]
    bb, rr = _iota3((B, Cn, 1), 0), _iota3((B, Cn, 1), 1)
    at = jnp.logical_or(jnp.logical_and(bb < nf, rr == Cn - 1), jnp.logical_and(bb >= nf, rr == 0))
    return x + jnp.where(at, val, 0.0)


def _sigmoid(x):
    return jax.nn.sigmoid(x)


def _silu(x):
    return x * _sigmoid(x)


def _silu_grad(x):
    s = _sigmoid(x)
    return s * (1.0 + x * (1.0 - s))


def _softplus(x):
    u = jnp.exp(-jnp.abs(x))
    l1p = jnp.where(u < 1e-3, u * (1.0 - u * (0.5 - u * (1.0 / 3.0))), jnp.log(1.0 + u))
    return jnp.maximum(x, 0.0) + l1p


def _mxr(x):
    return x.astype(MXU_DTYPE).astype(f32)


def _iota2(shape, axis):
    return lax.broadcasted_iota(jnp.int32, shape, axis)


def _matmul(a, b, kind, out_dtype, name, tm, tn, tk, out_t=False, after=None):
    extra = [] if after is None else [after]
    if kind == "nn":
        (M, K), N = a.shape, b.shape[1]
    elif kind == "nt":
        (M, K), N = a.shape, b.shape[0]
    else:
        (K, M), N = a.shape, b.shape[1]
    tm, tn, tk = min(tm, M), min(tn, N), min(tk, K)
    assert M % tm == 0 and N % tn == 0 and K % tk == 0, (name, M, N, K)
    nk = K // tk
    if kind == "tn":
        a_spec = pl.BlockSpec((tk, tm), lambda i, j, k: (k, i))
    else:
        a_spec = pl.BlockSpec((tm, tk), lambda i, j, k: (i, k))
    b_mode = dict(pipeline_mode=pl.Buffered(1)) if (tn == N and tk == K) else {}
    if kind == "nt":
        b_spec = pl.BlockSpec((tn, tk), lambda i, j, k: (j, k), **b_mode)
    else:
        b_spec = pl.BlockSpec((tk, tn), lambda i, j, k: (k, j), **b_mode)

    def body(a_ref, b_ref, *rest):
        o_ref, acc = rest[len(extra)], rest[len(extra) + 1 :]
        p = _mx(a_ref[...], b_ref[...], kind)
        if nk == 1:
            o_ref[...] = (p.T if out_t else p).astype(out_dtype)
        else:
            assert not out_t
            acc_ref = acc[0]
            k = pl.program_id(2)

            @pl.when(k == 0)
            def _():
                acc_ref[...] = p

            @pl.when(k > 0)
            def _():
                acc_ref[...] += p

            @pl.when(k == nk - 1)
            def _():
                o_ref[...] = acc_ref[...].astype(out_dtype)

    return pl.pallas_call(
        body,
        out_shape=jax.ShapeDtypeStruct((N, M) if out_t else (M, N), out_dtype),
        grid=(M // tm, N // tn, nk),
        in_specs=[a_spec, b_spec] + [pl.BlockSpec((8, LANES), lambda i, j, k: (0, 0))] * len(extra),
        out_specs=pl.BlockSpec((tn, tm), lambda i, j, k: (j, i)) if out_t else pl.BlockSpec((tm, tn), lambda i, j, k: (i, j)),
        scratch_shapes=[] if nk == 1 else [pltpu.VMEM((tm, tn), f32)],
        compiler_params=_cparams(("parallel", "parallel", "arbitrary")),
        name=name,
    )(a, b, *extra)


ROW_TILE = 512


def _rms_pre(x, w):
    T = x.shape[0]
    tb = min(ROW_TILE, T)

    def body(x_ref, w_ref, h_ref, ht_ref):
        xv = x_ref[...]
        r = lax.rsqrt(jnp.mean(xv * xv, axis=1, keepdims=True) + NORM_EPS)
        h = xv * r * w_ref[...]
        h_ref[...] = h.astype(h_ref.dtype)
        ht_ref[...] = h.T.astype(ht_ref.dtype)

    return pl.pallas_call(
        body,
        out_shape=(jax.ShapeDtypeStruct((T, D_MODEL), MXU_DTYPE), jax.ShapeDtypeStruct((D_MODEL, T), MXU_DTYPE)),
        grid=(T // tb,),
        in_specs=[pl.BlockSpec((tb, D_MODEL), lambda i: (i, 0)), pl.BlockSpec((1, D_MODEL), lambda i: (0, 0))],
        out_specs=(pl.BlockSpec((tb, D_MODEL), lambda i: (i, 0)), pl.BlockSpec((D_MODEL, tb), lambda i: (0, i))),
        compiler_params=_cparams(("parallel",)),
        name="rms_pre",
    )(x, w)


def _rms_pre_bwd(x, w, dh, dy):
    T = x.shape[0]
    tb = min(ROW_TILE, T)

    def body(x_ref, w_ref, dh_ref, dy_ref, dx_ref, dw_ref):
        i = pl.program_id(0)
        xv, dhv = x_ref[...], dh_ref[...]
        r = lax.rsqrt(jnp.mean(xv * xv, axis=1, keepdims=True) + NORM_EPS)
        dhw = dhv * w_ref[...]
        dx_ref[...] = dy_ref[...] + r * dhw - xv * (r * r * r) * jnp.mean(dhw * xv, axis=1, keepdims=True)
        part = jnp.sum(dhv * xv * r, axis=0, keepdims=True)

        @pl.when(i == 0)
        def _():
            dw_ref[...] = part

        @pl.when(i > 0)
        def _():
            dw_ref[...] += part

    blk = pl.BlockSpec((tb, D_MODEL), lambda i: (i, 0))
    row = pl.BlockSpec((1, D_MODEL), lambda i: (0, 0))
    return pl.pallas_call(
        body,
        out_shape=(jax.ShapeDtypeStruct((T, D_MODEL), f32), jax.ShapeDtypeStruct((1, D_MODEL), f32)),
        grid=(T // tb,),
        in_specs=[blk, row, blk, blk],
        out_specs=(blk, row),
        compiler_params=_cparams(("arbitrary",)),
        name="rms_pre_bwd",
    )(x, w, dh, dy)


SHIFTS = (-2, -1, 1, 2)


def _shifted(x):
    T = x.shape[0]
    t = _iota2(x.shape, 0)
    out = {}
    for d in SHIFTS:
        valid = (t < T - d) if d > 0 else (t >= -d)
        out[d] = jnp.where(valid, pltpu.roll(x, (-d) % T, 0), 0.0)
    return out


def _conv5(u, ush, cw):
    acc = u * cw[2:3, :]
    for j in (0, 1, 3, 4):
        acc = acc + ush[j - 2] * cw[j : j + 1, :]
    return acc


def _gdn_prep(proj, conv_w):
    T = proj.shape[0]
    nh = GDN_HEADS

    def body(u_ref, cw_ref, o_ref):
        j = pl.program_id(0)
        u = _mxr(u_ref[...])
        s = _silu(_conv5(u, _shifted(u), _mxr(cw_ref[...])))

        @pl.when(j < 2 * nh)
        def _():
            o_ref[0] = s * lax.rsqrt(jnp.sum(s * s, axis=1, keepdims=True) + NORM_EPS)

        @pl.when(j >= 2 * nh)
        def _():
            o_ref[0] = s

    return pl.pallas_call(
        body,
        out_shape=jax.ShapeDtypeStruct((3, T, GDN_HEADS * GDN_DIM), f32),
        grid=(3 * nh,),
        in_specs=[pl.BlockSpec((T, GDN_DIM), lambda j: (0, j)), pl.BlockSpec((CONV_K, GDN_DIM), lambda j: (0, j))],
        out_specs=pl.BlockSpec((1, T, GDN_DIM), lambda j: (j // nh, 0, j % nh)),
        compiler_params=_cparams(("parallel",)),
        name="gdn_prep",
    )(proj, conv_w)


def _gdn_prep_bwd(proj, conv_w, dqkv_f, dqkv_b, dproj):
    T = proj.shape[0]
    nh = GDN_HEADS


    def body(u_ref, cw_ref, df_ref, db_ref, _, du_ref, dcw_ref):
        j = pl.program_id(0)
        u, cw = _mxr(u_ref[...]), _mxr(cw_ref[...])
        ush = _shifted(u)
        c = _conv5(u, ush, cw)
        s = _silu(c)
        dn = df_ref[0] + db_ref[0]
        rinv = lax.rsqrt(jnp.sum(s * s, axis=1, keepdims=True) + NORM_EPS)
        ds_norm = rinv * dn - s * (rinv * rinv * rinv) * jnp.sum(dn * s, axis=1, keepdims=True)
        ds = jnp.where(j < 2 * nh, ds_norm, dn)
        dc = _mxr(ds * _silu_grad(c))
        dcsh = _shifted(dc)
        du = dc * cw[2:3, :]
        for jj in range(CONV_K):
            d = jj - 2
            dcw_ref[jj : jj + 1, :] = jnp.sum(dc * (u if d == 0 else ush[d]), axis=0, keepdims=True)
            if d != 0:
                du = du + dcsh[-d] * cw[jj : jj + 1, :]
        du_ref[...] = du.astype(du_ref.dtype)

    qspec = pl.BlockSpec((1, T, GDN_DIM), lambda j: (j // nh, 0, j % nh))
    return pl.pallas_call(
        body,
        out_shape=(jax.ShapeDtypeStruct(dproj.shape, dproj.dtype), jax.ShapeDtypeStruct((CONV_K, 3 * nh * GDN_DIM), f32)),
        grid=(3 * nh,),
        in_specs=[pl.BlockSpec((T, GDN_DIM), lambda j: (0, j)), pl.BlockSpec((CONV_K, GDN_DIM), lambda j: (0, j)), qspec, qspec, _ANY_SPEC],
        out_specs=(pl.BlockSpec((T, GDN_DIM), lambda j: (0, j)), pl.BlockSpec((CONV_K, GDN_DIM), lambda j: (0, j))),
        input_output_aliases={4: 0},
        compiler_params=_cparams(("parallel",)),
        name="gdn_prep_bwd",
    )(proj, conv_w, dqkv_f, dqkv_b, dproj)


def _chunk_tri(n, rev):
    i, j = _iota2((n, n), 0), _iota2((n, n), 1)
    same = jnp.right_shift(i, 6) == jnp.right_shift(j, 6)
    order = (j >= i) if rev else (j <= i)
    return jnp.where(jnp.logical_and(same, order), 1.0, 0.0).astype(f32)


def _gdn_gates(proj, avec, dvec, tb=256):
    T = proj.shape[0]

    def body(s_ref, a_ref, d_ref, g_ref, gc_ref):
        small = s_ref[...]
        lane = _iota2(small.shape, 1)
        lg = -jnp.exp(a_ref[...]) * _softplus(small + d_ref[...])
        beta = _sigmoid(small)
        g_ref[...] = jnp.where(lane < 16, lg, jnp.where(lane < 32, beta, 0.0))
        lgm = jnp.where(lane < 16, lg, 0.0)
        gcf = _exact01(_chunk_tri(tb, False), lgm)
        gcr = _exact01(_chunk_tri(tb, True), lgm)
        gc_ref[...] = jnp.where(lane < 8, gcf, gcr)

    blk = pl.BlockSpec((tb, LANES), lambda i: (i, 0))
    row = pl.BlockSpec((1, LANES), lambda i: (0, 0))
    return pl.pallas_call(
        body,
        out_shape=(jax.ShapeDtypeStruct((T, LANES), f32), jax.ShapeDtypeStruct((T, LANES), f32)),
        grid=(T // tb,),
        in_specs=[pl.BlockSpec((tb, LANES), lambda i: (i, SMALL_OFF // LANES)), row, row],
        out_specs=(blk, blk),
        compiler_params=_cparams(("parallel",)),
        name="gdn_gates",
    )(proj, avec, dvec)


def _gdn_gates_bwd(proj, avec, dvec, dg_f, dg_b, dsmall_gla, dproj):
    T = proj.shape[0]
    tb = min(ROW_TILE, T)
    pad = N_CAT - SMALL_OFF

    def body(s_ref, a_ref, d_ref, gf_ref, gb_ref, dl_ref, _, ds_ref, da_ref, dd_ref):
        i = pl.program_id(0)
        small = s_ref[...]
        lane = _iota2(small.shape, 1)
        dgate = gf_ref[...] + gb_ref[...]
        z = small + d_ref[...]
        nega = -jnp.exp(a_ref[...])
        dz = dgate * nega * _sigmoid(z)
        beta = _sigmoid(small)
        dsm = jnp.where(lane < 16, dz, jnp.where(lane < 32, dgate * beta * (1.0 - beta), 0.0))
        ds_ref[:, :LANES] = (dsm + dl_ref[...]).astype(ds_ref.dtype)
        ds_ref[:, LANES:] = jnp.zeros((tb, pad - LANES), ds_ref.dtype)
        lg = nega * _softplus(z)
        pa = jnp.sum(jnp.where(lane < 16, dgate * lg, 0.0), axis=0, keepdims=True)
        pd = jnp.sum(jnp.where(lane < 16, dz, 0.0), axis=0, keepdims=True)

        @pl.when(i == 0)
        def _():
            da_ref[...] = pa
            dd_ref[...] = pd

        @pl.when(i > 0)
        def _():
            da_ref[...] += pa
            dd_ref[...] += pd

    blk = pl.BlockSpec((tb, LANES), lambda i: (i, 0))
    row = pl.BlockSpec((1, LANES), lambda i: (0, 0))
    return pl.pallas_call(
        body,
        out_shape=(jax.ShapeDtypeStruct(dproj.shape, dproj.dtype), jax.ShapeDtypeStruct((1, LANES), f32), jax.ShapeDtypeStruct((1, LANES), f32)),
        grid=(T // tb,),
        in_specs=[pl.BlockSpec((tb, LANES), lambda i: (i, SMALL_OFF // LANES)), row, row, blk, blk, blk, _ANY_SPEC],
        out_specs=(pl.BlockSpec((tb, pad), lambda i: (i, DP_SMALL_BLK)), row, row),
        input_output_aliases={6: 0},
        compiler_params=_cparams(("arbitrary",)),
        name="gdn_gates_bwd",
    )(proj, avec, dvec, dg_f, dg_b, dsmall_gla, dproj)


GDN_HB = 8


def _inv_unit_lower(L, eye):
    A = eye - L
    P2 = _split2(L)
    for _ in range(3):
        P2 = _split2(_h3s(P2, P2, "nn"))
        A = A + _h3s(_split2(A), P2, "nn")
    Ph = P2[0]
    for _ in range(2):
        Ph = _dot16(Ph, Ph, "nn").astype(jnp.bfloat16)
        A = A + _dot16(A.astype(jnp.bfloat16), Ph, "nn")
    return A


def _lane_col(blk, idx):
    lane = _iota2(blk.shape, 1)
    return jnp.sum(jnp.where(lane == idx, blk, 0.0), axis=1, keepdims=True)


def _out_norm(o_f, o_b, proj, gate_blk, w, hd, name):
    T, W = o_f.shape
    tb = min(ROW_TILE, T)
    nh = W // hd

    def body(of_ref, ob_ref, z_ref, w_ref, y_ref):
        wv = w_ref[...]
        for i in range(nh):
            sl = slice(i * hd, (i + 1) * hd)
            o = of_ref[:, sl] + ob_ref[:, sl]
            r = lax.rsqrt(jnp.mean(o * o, axis=1, keepdims=True) + NORM_EPS)
            y_ref[:, sl] = (o * r * wv * _silu(z_ref[:, sl])).astype(y_ref.dtype)

    blk = pl.BlockSpec((tb, W), lambda i: (i, 0))
    return pl.pallas_call(
        body,
        out_shape=jax.ShapeDtypeStruct((T, W), MXU_DTYPE),
        grid=(T // tb,),
        in_specs=[blk, blk, pl.BlockSpec((tb, W), lambda i: (i, gate_blk)), pl.BlockSpec((1, hd), lambda i: (0, 0))],
        out_specs=blk,
        compiler_params=_cparams(("parallel",)),
        name=name,
    )(o_f, o_b, proj, w)


def _out_norm_bwd(dy, o_f, o_b, proj, gate_blk, w, hd, name, dproj):
    T, W = o_f.shape
    tb = min(ROW_TILE, T)
    nh = W // hd

    def body(dy_ref, of_ref, ob_ref, z_ref, w_ref, _, do_ref, dz_ref, dw_ref):
        i = pl.program_id(0)
        wv = w_ref[...]
        dw = jnp.zeros((1, hd), f32)
        for a in range(nh):
            sl = slice(a * hd, (a + 1) * hd)
            o = of_ref[:, sl] + ob_ref[:, sl]
            z = z_ref[:, sl]
            dyv = dy_ref[:, sl]
            r = lax.rsqrt(jnp.mean(o * o, axis=1, keepdims=True) + NORM_EPS)
            orr = o * r
            dn = dyv * _silu(z)
            dz_ref[:, sl] = (dyv * orr * wv * _silu_grad(z)).astype(dz_ref.dtype)
            dw = dw + jnp.sum(dn * orr, axis=0, keepdims=True)
            dnw = dn * wv
            do_ref[:, sl] = r * dnw - o * (r * r * r) * jnp.mean(dnw * o, axis=1, keepdims=True)

        @pl.when(i == 0)
        def _():
            dw_ref[...] = dw

        @pl.when(i > 0)
        def _():
            dw_ref[...] += dw

    blk = pl.BlockSpec((tb, W), lambda i: (i, 0))
    row = pl.BlockSpec((1, hd), lambda i: (0, 0))
    return pl.pallas_call(
        body,
        out_shape=(jax.ShapeDtypeStruct((T, W), f32), jax.ShapeDtypeStruct(dproj.shape, dproj.dtype), jax.ShapeDtypeStruct((1, hd), f32)),
        grid=(T // tb,),
        in_specs=[blk, blk, blk, pl.BlockSpec((tb, W), lambda i: (i, gate_blk)), row, _ANY_SPEC],
        out_specs=(blk, pl.BlockSpec((tb, W), lambda i: (i, gate_blk)), row),
        input_output_aliases={5: 1},
        compiler_params=_cparams(("arbitrary",)),
        name=name,
    )(dy, o_f, o_b, proj, w, dproj)


def _gla_prep(proj, w2f, w2b, b2f, b2b, tb=256):
    T = proj.shape[0]
    W = GLA_HEADS * GLA_DK

    def body(s_ref, wf_ref, wb_ref, bf_ref, bb_ref, gf_ref, gb_ref):
        small = s_ref[...]
        gkf = -_softplus(-(_mx(small, wf_ref[...], "nn") + bf_ref[...])) * (1.0 / GATE_NORMALIZER)
        gkb = -_softplus(-(_mx(small, wb_ref[...], "nn") + bb_ref[...])) * (1.0 / GATE_NORMALIZER)
        gf_ref[...] = _exact01(_chunk_tri(tb, False), gkf)
        gb_ref[...] = _exact01(_chunk_tri(tb, True), gkb)

    blk = pl.BlockSpec((tb, W), lambda i: (i, 0))
    wsp = pl.BlockSpec((LANES, W), lambda i: (0, 0))
    row = pl.BlockSpec((1, W), lambda i: (0, 0))
    return pl.pallas_call(
        body,
        out_shape=(jax.ShapeDtypeStruct((T, W), f32), jax.ShapeDtypeStruct((T, W), f32)),
        grid=(T // tb,),
        in_specs=[pl.BlockSpec((tb, LANES), lambda i: (i, SMALL_OFF // LANES)), wsp, wsp, row, row],
        out_specs=(blk, blk),
        compiler_params=_cparams(("parallel",)),
        name="gla_prep",
    )(proj, w2f, w2b, b2f, b2b)


def _gla_prep_bwd(proj, w2f, w2b, b2f, b2b, dgk_f, dgk_b):
    T = proj.shape[0]
    tb = min(ROW_TILE, T)
    W = GLA_HEADS * GLA_DK

    def body(s_ref, wf_ref, wb_ref, bf_ref, bb_ref, df_ref, db_ref, ds_ref, dwf_ref, dwb_ref, dbf_ref, dbb_ref):
        i = pl.program_id(0)
        small = s_ref[...]
        dsm = jnp.zeros((tb, LANES), f32)
        parts = []
        for w_ref, b_ref, d_ref in ((wf_ref, bf_ref, df_ref), (wb_ref, bb_ref, db_ref)):
            pre = _mx(small, w_ref[...], "nn") + b_ref[...]
            dpre = d_ref[...] * (1.0 / GATE_NORMALIZER) * _sigmoid(-pre)
            dsm = dsm + _mx(dpre, w_ref[...], "nt")
            parts.append((_mx(small, dpre, "tn"), jnp.sum(dpre, axis=0, keepdims=True)))
        ds_ref[...] = dsm

        @pl.when(i == 0)
        def _():
            dwf_ref[...], dbf_ref[...] = parts[0]
            dwb_ref[...], dbb_ref[...] = parts[1]

        @pl.when(i > 0)
        def _():
            dwf_ref[...] += parts[0][0]
            dbf_ref[...] += parts[0][1]
            dwb_ref[...] += parts[1][0]
            dbb_ref[...] += parts[1][1]

    blk = pl.BlockSpec((tb, W), lambda i: (i, 0))
    wsp = pl.BlockSpec((LANES, W), lambda i: (0, 0))
    row = pl.BlockSpec((1, W), lambda i: (0, 0))
    return pl.pallas_call(
        body,
        out_shape=(jax.ShapeDtypeStruct((T, LANES), f32), jax.ShapeDtypeStruct((LANES, W), f32), jax.ShapeDtypeStruct((LANES, W), f32),
                   jax.ShapeDtypeStruct((1, W), f32), jax.ShapeDtypeStruct((1, W), f32)),
        grid=(T // tb,),
        in_specs=[pl.BlockSpec((tb, LANES), lambda i: (i, SMALL_OFF // LANES)), wsp, wsp, row, row, blk, blk],
        out_specs=(pl.BlockSpec((tb, LANES), lambda i: (i, 0)), wsp, wsp, row, row),
        compiler_params=_cparams(("arbitrary",)),
        name="gla_prep_bwd",
    )(proj, w2f, w2b, b2f, b2b, dgk_f, dgk_b)


GATE_A_BLK, GATE_B_BLK, Z_BLK, G_B_BLK = 6144 // 1024, 7168 // 1024, 3072 // 1024, 8192 // 1024
DP_GATES_BLK, DP_GLA_QKV_BLK, DP_SMALL_BLK = 6144 // 2048, 4096 // 2048, SMALL_OFF // 512
_ANY_SPEC = pl.BlockSpec(memory_space=pl.ANY)


def _mid(og, obg, proj, x, target, wpg, wpl, wout, w_post, tb=256):
    T = x.shape[0]
    Dm = D_MODEL

    def body(og_ref, obg_ref, ga_ref, gb_ref, x_ref, t_ref, wpg_ref, wpl_ref, wo_ref, wp_ref,
             loss_ref, dwp_ref, dy_ref, m_ref, dout_ref, dya_ref, dyb_ref, dgates_ref, dog_ref, dobg_ref):
        i = pl.program_id(0)
        ya = _mx(og_ref[...], wpg_ref[...], "nn")
        yb = _mx(obg_ref[...], wpl_ref[...], "nn")
        sa, sb = _sigmoid(ga_ref[...]), _sigmoid(gb_ref[...])
        m = sa * ya + sb * yb
        m_ref[...] = m.astype(m_ref.dtype)
        out = _mx(m, wo_ref[...], "nn")
        r = lax.rsqrt(jnp.mean(out * out, axis=1, keepdims=True) + NORM_EPS)
        wp = wp_ref[...]
        e = x_ref[...] + out * r * wp - t_ref[...]
        dy = e * (1.0 / Dm)
        dy_ref[...] = dy
        lpart = jnp.sum(e * e, axis=0, keepdims=True)
        wpart = jnp.sum(dy * out * r, axis=0, keepdims=True)
        dyn = dy * wp
        dout = r * dyn - out * (r * r * r) * jnp.mean(dyn * out, axis=1, keepdims=True)
        dout_ref[...] = dout.astype(dout_ref.dtype)
        dm = _mx(dout, wo_ref[...], "nt")
        dya, dyb = dm * sa, dm * sb
        dya_ref[...] = dya.astype(dya_ref.dtype)
        dyb_ref[...] = dyb.astype(dyb_ref.dtype)
        dgates_ref[:, :Dm] = (dm * ya * sa * (1.0 - sa)).astype(dgates_ref.dtype)
        dgates_ref[:, Dm:] = (dm * yb * sb * (1.0 - sb)).astype(dgates_ref.dtype)
        dog_ref[...] = _mx(dya, wpg_ref[...], "nt")
        dobg_ref[...] = _mx(dyb, wpl_ref[...], "nt")

        @pl.when(i == 0)
        def _():
            loss_ref[...] = lpart
            dwp_ref[...] = wpart

        @pl.when(i > 0)
        def _():
            loss_ref[...] += lpart
            dwp_ref[...] += wpart

    blk = pl.BlockSpec((tb, Dm), lambda i: (i, 0))
    row = pl.BlockSpec((1, Dm), lambda i: (0, 0))
    wsp = pl.BlockSpec((Dm, Dm), lambda i: (0, 0))
    act = jax.ShapeDtypeStruct((T, Dm), MXU_DTYPE)
    big = jax.ShapeDtypeStruct((T, Dm), f32)
    vec = jax.ShapeDtypeStruct((1, Dm), f32)
    return pl.pallas_call(
        body,
        out_shape=(vec, vec, big, act, act, act, act, jax.ShapeDtypeStruct((T, N_CAT), MXU_DTYPE), big, big),
        grid=(T // tb,),
        in_specs=[blk, blk, pl.BlockSpec((tb, Dm), lambda i: (i, GATE_A_BLK)), pl.BlockSpec((tb, Dm), lambda i: (i, GATE_B_BLK)),
                  blk, blk, wsp, wsp, wsp, row],
        out_specs=(row, row) + (blk,) * 5 + (pl.BlockSpec((tb, 2 * Dm), lambda i: (i, DP_GATES_BLK)), blk, blk),
        compiler_params=_cparams(("arbitrary",)),
        name="mid",
    )(og, obg, proj, proj, x, target, wpg, wpl, wout, w_post)


def _heads(ref, lead, n, width, rows=slice(None)):
    return [ref[lead + (rows, slice(i * width, (i + 1) * width))] for i in range(n)]


class _GdnBatch:
    def __init__(self, q, k, v, gcol, grow, beta, nf, A=None):
        B, Cn = q.shape[0], q.shape[1]
        self.nf = nf
        self.incl, self.strict, eye = _dir_masks(B, nf, Cn)
        self.decay = jnp.where(self.incl, jnp.exp(jnp.where(self.incl, gcol - grow, 0.0)), 0.0)
        self.k, self.v = k, v
        gl = _last_row(gcol, nf)
        self.beta = jnp.broadcast_to(beta, q.shape)
        self.eg = jnp.broadcast_to(jnp.exp(gcol), q.shape)
        self.ekd = jnp.broadcast_to(jnp.exp(gl - gcol), q.shape)
        self.qs = q * (GDN_DIM**-0.5)
        self.kb = k * self.beta
        self.L = jnp.where(self.strict, _mx(self.kb, k, "nt") * self.decay, 0.0)
        if A is None:
            A = _inv_unit_lower(self.L, jnp.where(eye, 1.0, 0.0).astype(f32))
        self.A = A
        self.A2 = _split2(A)
        self.kbg = self.kb * self.eg
        self.sol = _h3s(self.A2, _split2(jnp.concatenate([v * self.beta, self.kbg], axis=2)), "nn")
        self.u = self.sol[:, :, :GDN_DIM]
        self.w = self.sol[:, :, GDN_DIM:]
        self.attn = jnp.where(self.incl, _mx(self.qs, k, "nt") * self.decay, 0.0)
        self.qd = self.qs * self.eg
        self.kd = k * self.ekd
        self.egl = jnp.exp(gl)


def _gdn_chain_inputs(qf_ref, qb_ref, gf_ref, gb_ref, cf_ref, cb_ref, b, hb):
    qkv = [jnp.stack(_heads(qf_ref, (i,), hb, GDN_DIM) + _heads(qb_ref, (i,), hb, GDN_DIM)) for i in range(3)]
    cols, rows, betas = [], [], []
    for d, (g_ref, c_ref) in enumerate(((gf_ref, cf_ref), (gb_ref, cb_ref))):
        gates, gc = g_ref[...], c_ref[...]
        for hh in range(hb):
            h = b * hb + hh
            col = _lane_col(gc, h + 8 * d)
            cols.append(col)
            rows.append(jnp.broadcast_to(col, (CHUNK, LANES)).T[0:1, :])
            betas.append(_lane_col(gates, 16 + h + 8 * d))
    return qkv[0], qkv[1], qkv[2], jnp.stack(cols), jnp.stack(rows), jnp.stack(betas)


GDN_CPS_FWD, GDN_CPS_BWD = 2, 1


def _chunk_views(refs, s_f, s_b):
    out = []
    for i, r in enumerate(refs):
        s = s_f if i % 2 == 0 else s_b
        rows = pl.ds(s * CHUNK, CHUNK)
        out.append(r.at[pl.ds(s, 1)] if len(r.shape) == 4 else (r.at[:, rows, :] if len(r.shape) == 3 else r.at[rows, :]))
    return out


def _gdn_specs(cps, hb):
    R, W = cps * CHUNK, hb * GDN_DIM
    qsp = lambda ci: pl.BlockSpec((3, R, W), lambda n, b: (0, ci(n), b))
    gsp = lambda ci: pl.BlockSpec((R, LANES), lambda n, b: (ci(n), 0))
    osp = lambda ci: pl.BlockSpec((R, W), lambda n, b: (ci(n), b))
    ssp = lambda ci: pl.BlockSpec((cps, hb, GDN_DIM, GDN_DIM), lambda n, b: (ci(n), b, 0, 0))
    asp = lambda ci: pl.BlockSpec((cps, hb, CHUNK, CHUNK), lambda n, b: (ci(n), b, 0, 0))
    return qsp, gsp, osp, ssp, asp


def _gdn_fwd(qkvn, gates, gc):
    T = qkvn.shape[1]
    N = T // CHUNK
    nh, hb = GDN_HEADS, GDN_HB
    cps = min(GDN_CPS_FWD, N)
    NB = N // cps

    def body(*refs):
        n, b = pl.program_id(0), pl.program_id(1)
        s_scr = refs[-1]

        @pl.when(n == 0)
        def _():
            s_scr[0, pl.ds(b * hb, hb)] = jnp.zeros((hb, GDN_DIM, GDN_DIM), f32)
            s_scr[1, pl.ds(b * hb, hb)] = jnp.zeros((hb, GDN_DIM, GDN_DIM), f32)

        for s in range(cps):
            chunk(*_chunk_views(refs[:-1], s, cps - 1 - s), s_scr, b)

    def chunk(qf_ref, qb_ref, gf_ref, gb_ref, cf_ref, cb_ref, of_ref, ob_ref, sf_ref, sb_ref, af_ref, ab_ref, s_scr, b):
        hs = pl.ds(b * hb, hb)
        q, k, v, gcol, grow, beta = _gdn_chain_inputs(qf_ref, qb_ref, gf_ref, gb_ref, cf_ref, cb_ref, b, hb)
        ck = _GdnBatch(q, k, v, gcol, grow, beta, hb)
        S = jnp.concatenate([s_scr[0, hs], s_scr[1, hs]], axis=0)
        sf_ref[0], sb_ref[0] = S[:hb], S[hb:]
        af_ref[0], ab_ref[0] = ck.A[:hb], ck.A[hb:]
        v_new = ck.u - _mx(ck.w, S, "nn")
        o = _mx(ck.qd, S, "nn") + _mx(ck.attn, v_new, "nn")
        S_new = S * ck.egl + _mx(ck.kd, v_new, "tn")
        for hh in range(hb):
            sl = slice(hh * GDN_DIM, (hh + 1) * GDN_DIM)
            of_ref[:, sl] = o[hh]
            ob_ref[:, sl] = o[hb + hh]
        s_scr[0, hs] = S_new[:hb]
        s_scr[1, hs] = S_new[hb:]

    fw, bw = (lambda n: n), (lambda n: NB - 1 - n)
    qsp, gsp, osp, ssp, asp = _gdn_specs(cps, hb)
    o_shape = jax.ShapeDtypeStruct((T, nh * GDN_DIM), f32)
    s_shape = jax.ShapeDtypeStruct((N, nh, GDN_DIM, GDN_DIM), f32)
    a_shape = jax.ShapeDtypeStruct((N, nh, CHUNK, CHUNK), f32)
    return pl.pallas_call(
        body,
        out_shape=(o_shape, o_shape, s_shape, s_shape, a_shape, a_shape),
        grid=(NB, nh // hb),
        in_specs=[qsp(fw), qsp(bw), gsp(fw), gsp(bw), gsp(fw), gsp(bw)],
        out_specs=(osp(fw), osp(bw), ssp(fw), ssp(bw), asp(fw), asp(bw)),
        scratch_shapes=[pltpu.VMEM((2, nh, GDN_DIM, GDN_DIM), f32)],
        compiler_params=_cparams(("arbitrary", "arbitrary")),
        name="gdn_scan_fwd",
    )(qkvn, qkvn, gates, gates, gc, gc)


def _gdn_bwd(qkvn, gates, gc, st_f, st_b, a_f, a_b, do):
    T = qkvn.shape[1]
    N = T // CHUNK
    nh, hb = GDN_HEADS, GDN_HB
    nb = nh // hb
    cps = min(GDN_CPS_BWD, N)
    NB = N // cps

    def body(*refs):
        n, b = pl.program_id(0), pl.program_id(1)
        ds_scr = refs[-1]

        @pl.when(n == 0)
        def _():
            ds_scr[0, pl.ds(b * hb, hb)] = jnp.zeros((hb, GDN_DIM, GDN_DIM), f32)
            ds_scr[1, pl.ds(b * hb, hb)] = jnp.zeros((hb, GDN_DIM, GDN_DIM), f32)

        for s in range(cps):
            chunk(*_chunk_views(refs[:-1], cps - 1 - s, s), ds_scr, b)

    def chunk(qf_ref, qb_ref, gf_ref, gb_ref, cf_ref, cb_ref, sf_ref, sb_ref, af_ref, ab_ref, dof_ref, dob_ref,
              dqf_ref, dqb_ref, dgf_ref, dgb_ref, ds_scr, b):
        hs = pl.ds(b * hb, hb)

        @pl.when(b == 0)
        def _():
            dgf_ref[...] = jnp.zeros((CHUNK, LANES), f32)
            dgb_ref[...] = jnp.zeros((CHUNK, LANES), f32)

        q, k, v, gcol, grow, beta = _gdn_chain_inputs(qf_ref, qb_ref, gf_ref, gb_ref, cf_ref, cb_ref, b, hb)
        ck = _GdnBatch(q, k, v, gcol, grow, beta, hb, A=jnp.concatenate([af_ref[0], ab_ref[0]], axis=0))
        S = jnp.concatenate([sf_ref[0], sb_ref[0]], axis=0)
        dS = jnp.concatenate([ds_scr[0, hs], ds_scr[1, hs]], axis=0)
        dov = jnp.stack(_heads(dof_ref, (), hb, GDN_DIM) + _heads(dob_ref, (), hb, GDN_DIM))
        v_new = ck.u - _mx(ck.w, S, "nn")
        d_vnew = _mx(ck.attn, dov, "tn") + _mx(ck.kd, dS, "nn")
        d_attn = jnp.where(ck.incl, _mx(dov, v_new, "nt"), 0.0)
        d_qd = _mx(dov, S, "nt")
        d_kd = _mx(v_new, dS, "nt")
        d_gl = jnp.sum(jnp.sum(dS * S, axis=2, keepdims=True), axis=1, keepdims=True) * ck.egl
        dS_new = dS * ck.egl + _mx(ck.qd, dov, "tn") - _mx(ck.w, d_vnew, "tn")
        ds_scr[0, hs] = dS_new[:hb]
        ds_scr[1, hs] = dS_new[hb:]
        d_w = -_mx(d_vnew, S, "nt")
        d_rhs = _h3s(ck.A2, _split2(jnp.concatenate([d_vnew, d_w], axis=2)), "tn")
        d_vb, d_kbg = d_rhs[:, :, :GDN_DIM], d_rhs[:, :, GDN_DIM:]
        dL = -jnp.where(ck.strict, _h3(d_rhs, ck.sol, "nt"), 0.0)
        P = dL * ck.decay
        Q = d_attn * ck.decay
        d_kb = _mx(P, k, "nn") + d_kbg * ck.eg
        d_qs = _mx(Q, k, "nn") + d_qd * ck.eg
        d_k = _mx(P, ck.kb, "tn") + _mx(Q, ck.qs, "tn") + d_kd * ck.ekd + d_kb * ck.beta
        E = dL * ck.L + d_attn * ck.attn
        kdsum = jnp.sum(d_kd * ck.kd, axis=2, keepdims=True)
        d_gc = (jnp.sum(E, axis=2, keepdims=True) - _colsum_col(E)
                + jnp.sum(d_kbg * ck.kbg, axis=2, keepdims=True)
                + jnp.sum(d_qd * ck.qd, axis=2, keepdims=True) - kdsum)
        d_gc = _add_at_last_row(d_gc, d_gl + jnp.sum(kdsum, axis=1, keepdims=True), hb)
        d_beta = jnp.sum(d_vb * v, axis=2, keepdims=True) + jnp.sum(d_kb * k, axis=2, keepdims=True)
        d_q = d_qs * (GDN_DIM**-0.5)
        d_v = d_vb * ck.beta
        lane = _iota2((CHUNK, LANES), 1)
        for d, (dq_ref, dg_ref) in enumerate(((dqf_ref, dgf_ref), (dqb_ref, dgb_ref))):
            acc = dg_ref[...]
            for hh in range(hb):
                c, h = d * hb + hh, b * hb + hh
                sl = slice(hh * GDN_DIM, (hh + 1) * GDN_DIM)
                dq_ref[0, :, sl] = d_q[c]
                dq_ref[1, :, sl] = d_k[c]
                dq_ref[2, :, sl] = d_v[c]
                acc = jnp.where(lane == h + 8 * d, d_gc[c], acc)
                acc = jnp.where(lane == 16 + h + 8 * d, d_beta[c], acc)
            dg_ref[...] = acc

        @pl.when(b == nb - 1)
        def _():
            for d, dg_ref in enumerate((dgf_ref, dgb_ref)):
                blk = dg_ref[...]
                dg_ref[...] = jnp.where(lane < 16, _exact01(_chunk_tri(CHUNK, d == 0), blk), blk)

    fw, bw = (lambda n: NB - 1 - n), (lambda n: n)
    qsp, gsp, osp, ssp, asp = _gdn_specs(cps, hb)
    dq_shape = jax.ShapeDtypeStruct((3, T, nh * GDN_DIM), f32)
    dg_shape = jax.ShapeDtypeStruct((T, LANES), f32)
    return pl.pallas_call(
        body,
        out_shape=(dq_shape, dq_shape, dg_shape, dg_shape),
        grid=(NB, nb),
        in_specs=[qsp(fw), qsp(bw), gsp(fw), gsp(bw), gsp(fw), gsp(bw), ssp(fw), ssp(bw), asp(fw), asp(bw), osp(fw), osp(bw)],
        out_specs=(qsp(fw), qsp(bw), gsp(fw), gsp(bw)),
        scratch_shapes=[pltpu.VMEM((2, nh, GDN_DIM, GDN_DIM), f32)],
        compiler_params=_cparams(("arbitrary", "arbitrary")),
        name="gdn_scan_bwd",
    )(qkvn, qkvn, gates, gates, gc, gc, st_f, st_b, a_f, a_b, do, do)


class _GlaBatch:
    def __init__(self, q, k, G, nf):
        B, Cn = q.shape[0], q.shape[1]
        self.incl, _, _ = _dir_masks(B, nf, Cn)
        self.eG = jnp.exp(G)
        self.enG = jnp.exp(-G)
        self.qg = q * (GLA_DK**-0.5) * self.eG
        self.kg = k * self.enG
        self.attn = jnp.where(self.incl, _mx(self.qg, self.kg, "nt"), 0.0)
        gl = _last_row(G, nf)
        self.ekd = jnp.exp(gl - G)
        self.kd = k * self.ekd
        self.egl = jnp.exp(gl)


GLA_CPS = 4


def _chunk_rows(s):
    return slice(s * CHUNK, (s + 1) * CHUNK)


def _gla_chain_inputs(refs_f, refs_b, rows_f, rows_b):
    nh = GLA_HEADS
    out = []
    for i, width in enumerate((GLA_DK, GLA_DK, GLA_DV, GLA_DK)):
        out.append(jnp.stack(_heads(refs_f[i], (), nh, width, rows_f) + _heads(refs_b[i], (), nh, width, rows_b)))
    return out


def _gla_specs(rows, ci):
    nh = GLA_HEADS
    return [
        pl.BlockSpec((rows, nh * GLA_DK), lambda n: (ci(n), 4096 // (nh * GLA_DK))),
        pl.BlockSpec((rows, nh * GLA_DK), lambda n: (ci(n), 4608 // (nh * GLA_DK))),
        pl.BlockSpec((rows, nh * GLA_DV), lambda n: (ci(n), 5120 // (nh * GLA_DV))),
        pl.BlockSpec((rows, nh * GLA_DK), lambda n: (ci(n), 0)),
    ]


def _gla_fwd(proj, G_f, G_b):
    T = proj.shape[0]
    N = T // CHUNK
    nh = GLA_HEADS
    cps = min(GLA_CPS, N)
    NB = N // cps

    def body(qf, kf, vf, gf, qb, kb, vb, gb, of_ref, ob_ref, sf_ref, sb_ref, s_scr):
        n = pl.program_id(0)

        @pl.when(n == 0)
        def _():
            s_scr[...] = jnp.zeros((2 * nh, GLA_DV, GLA_DK), f32)

        for s in range(cps):
            sb = cps - 1 - s
            q, k, v, G = _gla_chain_inputs((qf, kf, vf, gf), (qb, kb, vb, gb), _chunk_rows(s), _chunk_rows(sb))
            ck = _GlaBatch(q, k, G, nh)
            St = s_scr[...]
            sf_ref[s], sb_ref[sb] = St[:nh].astype(sf_ref.dtype), St[nh:].astype(sb_ref.dtype)
            o = _mx(ck.qg, St, "nt") + _mx(ck.attn, v, "nn")
            for i in range(nh):
                sl = slice(i * GLA_DV, (i + 1) * GLA_DV)
                of_ref[_chunk_rows(s), sl] = o[i]
                ob_ref[_chunk_rows(sb), sl] = o[nh + i]
            s_scr[...] = St * ck.egl + _mx(v, ck.kd, "tn")

    fw, bw = (lambda n: n), (lambda n: NB - 1 - n)
    osp = lambda ci: pl.BlockSpec((cps * CHUNK, nh * GLA_DV), lambda n: (ci(n), 0))
    ssp = lambda ci: pl.BlockSpec((cps, nh, GLA_DV, GLA_DK), lambda n: (ci(n), 0, 0, 0))
    o_shape = jax.ShapeDtypeStruct((T, nh * GLA_DV), f32)
    s_shape = jax.ShapeDtypeStruct((N, nh, GLA_DV, GLA_DK), MXU_DTYPE)
    return pl.pallas_call(
        body,
        out_shape=(o_shape, o_shape, s_shape, s_shape),
        grid=(NB,),
        in_specs=_gla_specs(cps * CHUNK, fw) + _gla_specs(cps * CHUNK, bw),
        out_specs=(osp(fw), osp(bw), ssp(fw), ssp(bw)),
        scratch_shapes=[pltpu.VMEM((2 * nh, GLA_DV, GLA_DK), f32)],
        compiler_params=_cparams(("arbitrary",)),
        name="gla_scan_fwd",
    )(proj, proj, proj, G_f, proj, proj, proj, G_b)


def _gla_bwd(proj, G_f, G_b, st_f, st_b, do):
    T = proj.shape[0]
    N = T // CHUNK
    nh = GLA_HEADS
    cps = min(GLA_CPS, N)
    NB = N // cps

    def body(qf, kf, vf, gf, sf_ref, dof_ref, qb, kb, vb, gb, sb_ref, dob_ref,
             dqf_ref, dkf_ref, dvf_ref, dgf_ref, dqb_ref, dkb_ref, dvb_ref, dgb_ref, ds_scr):
        n = pl.program_id(0)

        @pl.when(n == 0)
        def _():
            ds_scr[...] = jnp.zeros((2 * nh, GLA_DV, GLA_DK), f32)

        shp = (2 * nh, CHUNK, CHUNK)
        bb, ii, jj = _iota3(shp, 0), _iota3(shp, 1), _iota3(shp, 2)
        tri = jnp.where(jnp.logical_or(jnp.logical_and(bb < nh, jj >= ii), jnp.logical_and(bb >= nh, jj <= ii)), 1.0, 0.0)
        for s in range(cps):
            sf = cps - 1 - s
            rows = (_chunk_rows(sf), _chunk_rows(s))
            q, k, v, G = _gla_chain_inputs((qf, kf, vf, gf), (qb, kb, vb, gb), *rows)
            ck = _GlaBatch(q, k, G, nh)
            St = jnp.concatenate([sf_ref[sf], sb_ref[s]], axis=0).astype(f32)
            dSt = ds_scr[...]
            dov = jnp.stack(_heads(dof_ref, (), nh, GLA_DV, rows[0]) + _heads(dob_ref, (), nh, GLA_DV, rows[1]))
            d_attn = jnp.where(ck.incl, _mx(dov, v, "nt"), 0.0)
            d_qg = _mx(dov, St, "nn") + _mx(d_attn, ck.kg, "nn")
            d_kg = _mx(d_attn, ck.qg, "tn")
            d_v = _mx(ck.attn, dov, "tn") + _mx(ck.kd, dSt, "nt")
            d_kd = _mx(v, dSt, "nn")
            ds_scr[...] = dSt * ck.egl + _mx(dov, ck.qg, "tn")
            kdd = d_kd * ck.kd
            d_gl = jnp.sum(dSt * St, axis=1, keepdims=True) * ck.egl + jnp.sum(kdd, axis=1, keepdims=True)
            d_q = d_qg * ck.eG * (GLA_DK**-0.5)
            d_k = d_kg * ck.enG + d_kd * ck.ekd
            d_gk = _exact01(tri, _add_at_last_row(d_qg * ck.qg - d_kg * ck.kg - kdd, d_gl, nh))
            for d, (dq_ref, dk_ref, dv_ref, dg_ref) in enumerate(((dqf_ref, dkf_ref, dvf_ref, dgf_ref), (dqb_ref, dkb_ref, dvb_ref, dgb_ref))):
                for i in range(nh):
                    c = d * nh + i
                    ks, vs = slice(i * GLA_DK, (i + 1) * GLA_DK), slice(i * GLA_DV, (i + 1) * GLA_DV)
                    dq_ref[rows[d], ks] = d_q[c]
                    dk_ref[rows[d], ks] = d_k[c]
                    dv_ref[rows[d], vs] = d_v[c]
                    dg_ref[rows[d], ks] = d_gk[c]

    R = cps * CHUNK

    def in_specs(ci):
        return _gla_specs(R, ci) + [
            pl.BlockSpec((cps, nh, GLA_DV, GLA_DK), lambda n: (ci(n), 0, 0, 0)),
            pl.BlockSpec((R, nh * GLA_DV), lambda n: (ci(n), 0)),
        ]

    def out_specs(ci):
        ksp = pl.BlockSpec((R, nh * GLA_DK), lambda n: (ci(n), 0))
        return [ksp, ksp, pl.BlockSpec((R, nh * GLA_DV), lambda n: (ci(n), 0)), ksp]

    fw, bw = (lambda n: NB - 1 - n), (lambda n: n)
    k_shape = jax.ShapeDtypeStruct((T, nh * GLA_DK), f32)
    v_shape = jax.ShapeDtypeStruct((T, nh * GLA_DV), f32)
    return pl.pallas_call(
        body,
        out_shape=(k_shape, k_shape, v_shape, k_shape) * 2,
        grid=(NB,),
        in_specs=in_specs(fw) + in_specs(bw),
        out_specs=tuple(out_specs(fw) + out_specs(bw)),
        scratch_shapes=[pltpu.VMEM((2 * nh, GLA_DV, GLA_DK), f32)],
        compiler_params=_cparams(("arbitrary",)),
        name="gla_scan_bwd",
    )(proj, proj, proj, G_f, st_f, do, proj, proj, proj, G_b, st_b, do)


def _gla_dsum(dq_f, dq_b, dk_f, dk_b, dv_f, dv_b, dproj):
    T = dq_f.shape[0]
    tb = min(ROW_TILE, T)
    wk, wv = GLA_HEADS * GLA_DK, GLA_HEADS * GLA_DV

    def body(qf, qb, kf, kb, vf, vb, _, o_ref):
        o_ref[:, :wk] = (qf[...] + qb[...]).astype(o_ref.dtype)
        o_ref[:, wk : 2 * wk] = (kf[...] + kb[...]).astype(o_ref.dtype)
        o_ref[:, 2 * wk :] = (vf[...] + vb[...]).astype(o_ref.dtype)

    ksp = pl.BlockSpec((tb, wk), lambda i: (i, 0))
    vsp = pl.BlockSpec((tb, wv), lambda i: (i, 0))
    return pl.pallas_call(
        body,
        out_shape=jax.ShapeDtypeStruct(dproj.shape, dproj.dtype),
        grid=(T // tb,),
        in_specs=[ksp, ksp, ksp, ksp, vsp, vsp, _ANY_SPEC],
        out_specs=pl.BlockSpec((tb, 2 * wk + wv), lambda i: (i, DP_GLA_QKV_BLK)),
        input_output_aliases={6: 0},
        compiler_params=_cparams(("parallel",)),
        name="gla_dsum",
    )(dq_f, dq_b, dk_f, dk_b, dv_f, dv_b, dproj)


def _local_step(x, target, wcat, conv_w, avec, dvec, gdn_norm_w, w2f, w2b, b2f, b2b, gla_norm_w, proj_weights, w_pre, w_post,
                send_proj_grads=None, send_in_grads=None):
    h, h_t = _rms_pre(x, w_pre)
    proj = _matmul(h, wcat, "nt", f32, "proj_in", 2048, 512, 1024)
    qkvn = _gdn_prep(proj, conv_w)
    gates, gc = _gdn_gates(proj, avec, dvec)
    oa_f, oa_b, sa_f, sa_b, inv_f, inv_b = _gdn_fwd(qkvn, gates, gc)
    og = _out_norm(oa_f, oa_b, proj, Z_BLK, gdn_norm_w, GDN_DIM, "gdn_out")
    G_f, G_b = _gla_prep(proj, w2f, w2b, b2f, b2b)
    ob_f, ob_b, sb_f, sb_b = _gla_fwd(proj, G_f, G_b)
    obg = _out_norm(ob_f, ob_b, proj, G_B_BLK, gla_norm_w, GLA_DV, "gla_out")
    wpg, wpl, wout = proj_weights(obg)
    (loss_row, d_wpost, dy, m, dout, dya, dyb, dproj, dog, dobg) = _mid(og, obg, proj, x, target, wpg, wpl, wout, w_post)

    d_wout = _matmul(m, dout, "tn", MXU_DTYPE, "dw_out", 1024, 512, 2048)
    d_wpg = _matmul(og, dya, "tn", MXU_DTYPE, "dw_proj_gdn", 1024, 512, 2048)
    d_wpl = _matmul(obg, dyb, "tn", MXU_DTYPE, "dw_proj_gla", 1024, 512, 2048)
    if send_proj_grads is not None:
        gla_norm_w = gla_norm_w + send_proj_grads(d_wpg, d_wpl, d_wout)

    do_b, dproj, d_gla_norm = _out_norm_bwd(dobg, ob_f, ob_b, proj, G_B_BLK, gla_norm_w, GLA_DV, "gla_out_bwd", dproj)
    dq_f, dk_f, dv_f, dgk_f, dq_b, dk_b, dv_b, dgk_b = _gla_bwd(proj, G_f, G_b, sb_f, sb_b, do_b)
    dproj = _gla_dsum(dq_f, dq_b, dk_f, dk_b, dv_f, dv_b, dproj)
    dsmall_gla, d_w2f, d_w2b, d_b2f, d_b2b = _gla_prep_bwd(proj, w2f, w2b, b2f, b2b, dgk_f, dgk_b)

    do_a, dproj, d_gdn_norm = _out_norm_bwd(dog, oa_f, oa_b, proj, Z_BLK, gdn_norm_w, GDN_DIM, "gdn_out_bwd", dproj)
    dqkv_f, dqkv_b, dg_f, dg_b = _gdn_bwd(qkvn, gates, gc, sa_f, sa_b, inv_f, inv_b, do_a)
    dproj, d_conv = _gdn_prep_bwd(proj, conv_w, dqkv_f, dqkv_b, dproj)
    dproj, d_alog, d_dtb = _gdn_gates_bwd(proj, avec, dvec, dg_f, dg_b, dsmall_gla, dproj)

    d_wcat = _matmul(h_t, dproj, "nn", MXU_DTYPE, "dw_in", 1024, 512, 2048, out_t=True)
    started = None if send_in_grads is None else send_in_grads(d_wcat)
    dh = _matmul(dproj, wcat, "nn", f32, "dh", 512, 1024, N_CAT, after=started)
    grad_x, d_wpre = _rms_pre_bwd(x, w_pre, dh, dy)
    return dict(loss_row=loss_row, grad_x=grad_x, d_wcat=d_wcat, d_conv=d_conv, d_alog=d_alog, d_dtb=d_dtb,
                d_gdn_norm=d_gdn_norm, d_wpg=d_wpg, d_w2f=d_w2f, d_w2b=d_w2b, d_b2f=d_b2f, d_b2b=d_b2b,
                d_gla_norm=d_gla_norm, d_wpl=d_wpl, d_wout=d_wout, d_wpost=d_wpost, d_wpre=d_wpre)


def _to_cat(wf):
    pad = jnp.zeros((N_CAT - N_IN,) + wf.shape[1:], wf.dtype)
    return jnp.concatenate([wf[:4096], wf[4128:6176], wf[7232:9280], wf[6176:7200], wf[4096:4128], wf[7200:7232], pad], axis=0)


def _from_cat(wc):
    return jnp.concatenate([wc[:4096], wc[9216:9248], wc[4096:6144], wc[8192:9216], wc[9248:9280], wc[6144:8192]], axis=0)


def _lane_row(*pieces):
    row = jnp.concatenate(pieces, axis=1)
    return jnp.pad(row, ((0, 0), (0, LANES - row.shape[1])))


def _pad_w2(w2, row0):
    return jnp.pad(w2, ((row0, LANES - row0 - w2.shape[0]), (0, 0)))


SM_ROWS, SM_COLS = 32, 384


def _pack_small_shard(conv, w2f, w2b):
    lead = conv.shape[:-2]
    z = lambda r, c: jnp.zeros(lead + (r, c), f32)
    top = jnp.concatenate([conv, z(8 - CONV_K, SM_COLS)], axis=-2)
    mid = jnp.concatenate([w2f, w2b, z(16, SM_COLS - 128)], axis=-1)
    return jnp.concatenate([top, mid, z(SM_ROWS - 24, SM_COLS)], axis=-2)


SMALL_REGION = {
    "ln_pre_w": (0, 1, 0, 1024), "ln_post_w": (1, 1, 0, 1024), "gk_b2_fwd": (2, 1, 0, 512), "gk_b2_bwd": (2, 1, 512, 512),
    "gla_norm_w": (3, 1, 0, 256), "gdn_norm_w": (3, 1, 256, 128), "a_log_fwd": (4, 1, 0, 8), "a_log_bwd": (5, 1, 0, 8),
    "dt_bias_fwd": (6, 1, 0, 8), "dt_bias_bwd": (7, 1, 0, 8), "conv_w": (8, CONV_K, 0, SM_COLS),
    "gk_w2_fwd": (16, 16, 0, 64), "gk_w2_bwd": (16, 16, 128, 64),
}
SMALL_NAMES = tuple(SMALL_REGION)
SMALL_SHARDED = ("conv_w", "gk_w2_fwd", "gk_w2_bwd")
PACK_ROWS, PACK_COLS = 32, 1024
LOSS_AT = (13, 0)


def _region(name):
    r0, nr, l0, nl = SMALL_REGION[name]
    return slice(r0, r0 + nr), slice(l0, l0 + nl)


def _pack_small_grads(G):
    names = ("d_wpre", "d_wpost", "d_b2f", "d_b2b", "d_gla_norm", "d_gdn_norm", "d_alog", "d_dtb", "loss_row", "d_conv", "d_w2f", "d_w2b")

    def body(wpre, wpost, b2f, b2b, glan, gdnn, alog, dtb, lrow, conv, w2f, w2b, o_ref):
        o_ref[...] = jnp.zeros(o_ref.shape, f32)
        loss = 0.5 * jnp.sum(lrow[...], axis=1, keepdims=True) * (1.0 / D_MODEL)
        whole = {"ln_pre_w": wpre, "ln_post_w": wpost, "gk_b2_fwd": b2f, "gk_b2_bwd": b2b, "gla_norm_w": glan, "gdn_norm_w": gdnn}
        for j in range(N_DEV):
            for name, ref in whole.items():
                o_ref[(j,) + _region(name)] = ref[...]
            o_ref[(j,) + _region("a_log_fwd")] = alog[0:1, 0:8]
            o_ref[(j,) + _region("a_log_bwd")] = alog[0:1, 8:16]
            o_ref[(j,) + _region("dt_bias_fwd")] = dtb[0:1, 0:8]
            o_ref[(j,) + _region("dt_bias_bwd")] = dtb[0:1, 8:16]
            o_ref[(j,) + _region("conv_w")] = conv[:, j * SM_COLS : (j + 1) * SM_COLS]
            o_ref[(j,) + _region("gk_w2_fwd")] = w2f[32:48, j * 64 : (j + 1) * 64]
            o_ref[(j,) + _region("gk_w2_bwd")] = w2b[48:64, j * 64 : (j + 1) * 64]
            o_ref[j, LOSS_AT[0] : LOSS_AT[0] + 1, LOSS_AT[1] : LOSS_AT[1] + 1] = loss

    return pl.pallas_call(
        body, out_shape=jax.ShapeDtypeStruct((N_DEV, PACK_ROWS, PACK_COLS), f32), name="pack_small_grads",
    )(*[G[n] for n in names])


def _small_adam(land, own, W, M, V):
    bc1 = 1.0 - ADAM_B1**ADAM_STEP
    bc2 = 1.0 - ADAM_B2**ADAM_STEP
    n = len(SMALL_NAMES)

    def body(land_ref, own_ref, *refs):
        w_refs, m_refs, v_refs = refs[0:n], refs[n : 2 * n], refs[2 * n : 3 * n]
        outs, loss_ref, g_scr = refs[3 * n : 7 * n], refs[7 * n], refs[7 * n + 1]
        me = _my_device()
        g = jnp.where(me == 0, own_ref[0], land_ref[0])
        for j in range(1, N_DEV):
            g = g + jnp.where(me == j, own_ref[j], land_ref[j])
        g_scr[...] = g
        loss_ref[...] = g_scr[LOSS_AT[0] : LOSS_AT[0] + 1, LOSS_AT[1] : LOSS_AT[1] + 1]
        for i, name in enumerate(SMALL_NAMES):
            gp = g_scr[_region(name)]
            lead = (0,) if name in SMALL_SHARDED else (Ellipsis,)
            w, m, v = w_refs[i][lead], m_refs[i][lead], v_refs[i][lead]
            m2 = ADAM_B1 * m + (1.0 - ADAM_B1) * gp
            v2 = ADAM_B2 * v + (1.0 - ADAM_B2) * (gp * gp)
            delta = -ADAM_LR * ((m2 / bc1) / (jnp.sqrt(v2 / bc2) + ADAM_EPS) + ADAM_WD * w)
            for k, val in enumerate((gp, delta, m2, v2)):
                outs[4 * i + k][lead] = val

    shapes = [jax.ShapeDtypeStruct(W[name].shape, f32) for name in SMALL_NAMES for _ in range(4)]
    res = pl.pallas_call(
        body,
        out_shape=tuple(shapes) + (jax.ShapeDtypeStruct((1, 1), f32),),
        scratch_shapes=[pltpu.VMEM((PACK_ROWS, PACK_COLS), f32)],
        name="adam_small",
    )(land, own, *[P[name] for P in (W, M, V) for name in SMALL_NAMES])
    return {name: res[4 * i : 4 * i + 4] for i, name in enumerate(SMALL_NAMES)}, res[4 * n]


_MESH = pl.DeviceIdType.MESH


N_BIG_COPIES, N_SMALL_COPIES = 9, 7


def _gather_weights(w_in_s, sm_s):
    shapes = (w_in_s.shape, sm_s.shape)
    dtypes = (MXU_DTYPE, f32)

    def body(win_ref, sm_ref, gin_ref, gsm_ref, send_sems, recv_sems):
        x, y, c = lax.axis_index("x"), lax.axis_index("y"), lax.axis_index("c")
        me, sibling = (x, y, c), (x, y, 1 - c)
        chips = [(1 - x, y), (x, 1 - y), (1 - x, 1 - y)]

        def idx(px, py, pc):
            return 4 * px + 2 * py + pc

        for r in range(w_in_s.shape[1] // LANES):
            cols = slice(r * LANES, (r + 1) * LANES)
            gin_ref[idx(*me), :, cols] = win_ref[:, cols].astype(MXU_DTYPE)
        gsm_ref[idx(*me)] = sm_ref[...]

        def copy(sem, blk, to):
            return pltpu.make_async_remote_copy(src_ref=blk, dst_ref=blk, send_sem=send_sems.at[sem], recv_sem=recv_sems.at[sem],
                                                device_id=to, device_id_type=_MESH)

        def small(k, block, to):
            return copy(N_BIG_COPIES + k, gsm_ref.at[idx(*block)], to)

        half = w_in_s.shape[1] // 2
        xn, yn, dg = (1 - x, y), (x, 1 - y), (1 - x, 1 - y)

        def big(sem, block, to, part=None):
            blk = gin_ref.at[idx(*block)]
            return copy(sem, blk if part is None else blk.at[:, pl.ds(part * half, half)], to)

        started = [big(0, me, sibling), big(1, me, (*xn, c)), big(2, me, (*yn, c)), small(0, me, sibling)]
        started += [small(1 + j, me, (*chip, c)) for j, chip in enumerate(chips)]
        for cp in started:
            cp.start()

        def then(arrived, forwards):
            arrived.wait_recv()
            for cp in forwards:
                cp.start()
            started.extend(forwards)

        then(big(1, (*xn, c), me), [big(3, (*xn, c), (*yn, c), 1), big(5, (*xn, c), sibling)])
        then(big(2, (*yn, c), me), [big(4, (*yn, c), (*xn, c), 0), big(6, (*yn, c), sibling)])
        then(big(4, (*dg, c), me, 0), [big(7, (*dg, c), sibling, 0)])
        then(big(3, (*dg, c), me, 1), [big(8, (*dg, c), sibling, 1)])
        for j, chip in enumerate(chips):
            then(small(1 + j, (*chip, c), me), [small(4 + j, (*chip, c), sibling)])
        big(0, sibling, me).wait_recv()
        big(5, (*xn, 1 - c), me).wait_recv()
        big(6, (*yn, 1 - c), me).wait_recv()
        big(7, (*dg, 1 - c), me, 0).wait_recv()
        big(8, (*dg, 1 - c), me, 1).wait_recv()
        small(0, sibling, me).wait_recv()
        for j, chip in enumerate(chips):
            small(4 + j, (*chip, 1 - c), me).wait_recv()
        for cp in started:
            cp.wait_send()

    vm = pl.BlockSpec(memory_space=pltpu.VMEM)
    return pl.pallas_call(
        body,
        out_shape=tuple(jax.ShapeDtypeStruct((N_DEV,) + s, d) for s, d in zip(shapes, dtypes)),
        in_specs=[vm, vm],
        out_specs=(vm, vm),
        scratch_shapes=[pltpu.SemaphoreType.DMA((N_BIG_COPIES + N_SMALL_COPIES,)), pltpu.SemaphoreType.DMA((N_BIG_COPIES + N_SMALL_COPIES,))],
        compiler_params=pltpu.CompilerParams(vmem_limit_bytes=VMEM_LIMIT),
        name="gather_weights",
    )(w_in_s, sm_s)


N_CHIP = 4
_EFFECT = pltpu.SideEffectType.DATAFLOW_SIDE_EFFECTING
_HBM_SPEC = pl.BlockSpec(memory_space=pltpu.HBM)
_SEM_SPEC = pl.BlockSpec(memory_space=pltpu.SEMAPHORE)


def _relation_peer(k, x, y, c):
    px = (1 - x) if (k & 4) else x
    py = (1 - y) if (k & 2) else y
    pc = (1 - c) if (k & 1) else c
    return (px, py, pc), 4 * px + 2 * py + pc


_PLAN_COPIES = {"gather": N_DEV - 1, "blocks": N_DEV - 1, "sibling": 4, "chips": 3}
_PLAN_SLOTS = {"gather": N_DEV, "blocks": N_DEV, "sibling": 4, "chips": 4}


def _plan_copies(plan, src_ref, land_ref, send_sems, recv_sems, arrival):
    x, y, c = lax.axis_index("x"), lax.axis_index("y"), lax.axis_index("c")
    me, my_chip = 4 * x + 2 * y + c, 2 * x + y
    out = []

    def add(i, src, there, here, dev):
        out.append(pltpu.make_async_remote_copy(src_ref=src, dst_ref=land_ref.at[here if arrival else there], send_sem=send_sems.at[i],
                                                recv_sem=recv_sems.at[i], device_id=dev, device_id_type=_MESH))

    if plan in ("gather", "blocks"):
        for k in range(1, N_DEV):
            dev, p = _relation_peer(k, x, y, c)
            add(k - 1, src_ref.at[p] if plan == "blocks" else src_ref, me, p, dev)
    elif plan == "sibling":
        for j in range(4):
            add(j, src_ref.at[2 * j + 1 - c], j, j, (x, y, 1 - c))
    else:
        for i, k in enumerate((2, 4, 6)):
            (px, py, pc), _ = _relation_peer(k, x, y, c)
            add(i, src_ref.at[2 * px + py], my_chip, 2 * px + py, (px, py, pc))
    return out


def _exchange_start(src, plan, name, after=None):
    n = _PLAN_COPIES[plan]
    land = lax.empty((_PLAN_SLOTS[plan],) + (src.shape if plan == "gather" else src.shape[1:]), src.dtype)
    extra = [] if after is None else [after]

    def body(src_ref, land_ref, *rest):
        send_sems, recv_sems, _, _, token = rest[len(extra) :]
        for cp in _plan_copies(plan, src_ref, land_ref, send_sems, recv_sems, False):
            cp.start()
        token[...] = jnp.zeros_like(token)

    return pl.pallas_call(
        body,
        name=name,
        out_shape=(pltpu.SemaphoreType.DMA((n,)), pltpu.SemaphoreType.DMA((n,)), pltpu.HBM(src.shape, src.dtype),
                   pltpu.HBM(land.shape, land.dtype), jax.ShapeDtypeStruct((8, LANES), f32)),
        in_specs=(_HBM_SPEC, _HBM_SPEC) + (_ANY_SPEC,) * len(extra),
        out_specs=(_SEM_SPEC, _SEM_SPEC, _HBM_SPEC, _HBM_SPEC, pl.BlockSpec(memory_space=pltpu.VMEM)),
        input_output_aliases={0: 2, 1: 3},
        compiler_params=pltpu.CompilerParams(has_side_effects=_EFFECT),
    )(pltpu.with_memory_space_constraint(src, pltpu.HBM), pltpu.with_memory_space_constraint(land, pltpu.HBM), *extra)


def _exchange_wait(started, after, plan, name):
    send_sems, recv_sems, src_thru, land_thru, _ = started

    def body(src_ref, land_ref, send_sems, recv_sems, after_ref, src_dead, got_ref):
        for cp in _plan_copies(plan, src_ref, land_ref, send_sems, recv_sems, True):
            cp.wait_send()
            cp.wait_recv()

    return pl.pallas_call(
        body,
        name=name,
        out_shape=(pltpu.HBM(src_thru.shape, src_thru.dtype), pltpu.HBM(land_thru.shape, land_thru.dtype)),
        in_specs=(_HBM_SPEC, _HBM_SPEC, _SEM_SPEC, _SEM_SPEC, pl.BlockSpec(memory_space=pl.ANY)),
        out_specs=(_HBM_SPEC, _HBM_SPEC),
        input_output_aliases={0: 0, 1: 1},
        compiler_params=pltpu.CompilerParams(has_side_effects=_EFFECT),
    )(src_thru, land_thru, send_sems, recv_sems, after)


def _pair_sum(g, r, name, tc=LANES):
    _, R, Cc = g.shape

    def body(g_ref, r_ref, o_ref):
        o_ref[...] = (g_ref[...].astype(f32) + r_ref[...].astype(f32)).astype(o_ref.dtype)

    return pl.pallas_call(
        body,
        out_shape=jax.ShapeDtypeStruct(r.shape, r.dtype),
        grid=(N_CHIP, Cc // tc),
        in_specs=[pl.BlockSpec((1, R, tc), lambda k, i: (2 * k + lax.axis_index("c"), 0, i)), pl.BlockSpec((1, R, tc), lambda k, i: (k, 0, i))],
        out_specs=pl.BlockSpec((1, R, tc), lambda k, i: (k, 0, i)),
        compiler_params=_cparams(("parallel", "parallel")),
        name=name,
    )(g, r)


def _sum_adam(parts, w, m, v, name, tb, tc=None, own=None, own_slot=None):
    R, Cc = w.shape
    n_parts = parts.shape[0]
    tb = R if tc else min(tb, R)
    tc = tc or Cc
    assert R % tb == 0 and Cc % tc == 0
    bc1 = 1.0 - ADAM_B1**ADAM_STEP
    bc2 = 1.0 - ADAM_B2**ADAM_STEP
    extra = [] if own is None else [own]

    def body(p_ref, w_ref, m_ref, v_ref, *rest):
        g_ref, d_ref, nm_ref, nv_ref = rest[len(extra) :]

        def part(j):
            pj = p_ref[j].astype(f32)
            return pj if own is None else jnp.where(own_slot() == j, rest[0][0].astype(f32), pj)

        g = part(0)
        for j in range(1, n_parts):
            g = g + part(j)
        g_ref[...] = g
        m2 = ADAM_B1 * m_ref[...] + (1.0 - ADAM_B1) * g
        v2 = ADAM_B2 * v_ref[...] + (1.0 - ADAM_B2) * (g * g)
        nm_ref[...] = m2
        nv_ref[...] = v2
        d_ref[...] = -ADAM_LR * ((m2 / bc1) / (jnp.sqrt(v2 / bc2) + ADAM_EPS) + ADAM_WD * w_ref[...])

    blk = pl.BlockSpec((tb, tc), lambda i, j: (i, j))
    o = jax.ShapeDtypeStruct((R, Cc), f32)
    return pl.pallas_call(
        body,
        out_shape=(o, o, o, o),
        grid=(R // tb, Cc // tc),
        in_specs=[pl.BlockSpec((n_parts, tb, tc), lambda i, j: (0, i, j)), blk, blk, blk]
        + [pl.BlockSpec((1, tb, tc), lambda i, j: (own_slot(), i, j))] * len(extra),
        out_specs=(blk, blk, blk, blk),
        compiler_params=_cparams(("parallel", "parallel")),
        name=name,
    )(parts, w, m, v, *extra)


def _my_device():
    return 4 * lax.axis_index("x") + 2 * lax.axis_index("y") + lax.axis_index("c")


def _my_chip():
    return 2 * lax.axis_index("x") + lax.axis_index("y")


_WEIGHTS = ["ln_pre_w", "w_in", "conv_w", "a_log_fwd", "a_log_bwd", "dt_bias_fwd", "dt_bias_bwd", "gdn_norm_w", "w_proj_gdn",
            "gk_w2_fwd", "gk_b2_fwd", "gk_w2_bwd", "gk_b2_bwd", "gla_norm_w", "w_proj_gla", "w_out", "ln_post_w"]


def kernel(x, ln_pre_w, w_in, conv_w, a_log_fwd, a_log_bwd, dt_bias_fwd, dt_bias_bwd, gdn_norm_w, w_proj_gdn, gk_w2_fwd, gk_b2_fwd, gk_w2_bwd, gk_b2_bwd, gla_norm_w, w_proj_gla, w_out, ln_post_w, loss_target, m_ln_pre_w, m_w_in, m_conv_w, m_a_log_fwd, m_a_log_bwd, m_dt_bias_fwd, m_dt_bias_bwd, m_gdn_norm_w, m_w_proj_gdn, m_gk_w2_fwd, m_gk_b2_fwd, m_gk_w2_bwd, m_gk_b2_bwd, m_gla_norm_w, m_w_proj_gla, m_w_out, m_ln_post_w, v_ln_pre_w, v_w_in, v_conv_w, v_a_log_fwd, v_a_log_bwd, v_dt_bias_fwd, v_dt_bias_bwd, v_gdn_norm_w, v_w_proj_gdn, v_gk_w2_fwd, v_gk_b2_fwd, v_gk_w2_bwd, v_gk_b2_bwd, v_gla_norm_w, v_w_proj_gla, v_w_out, v_ln_post_w):
    args = locals()
    W = {n: args[n] for n in _WEIGHTS}
    M = {n: args["m_" + n] for n in _WEIGHTS}
    V = {n: args["v_" + n] for n in _WEIGHTS}

    wp_stack = lambda P: jnp.concatenate([P["w_proj_gdn"], P["w_proj_gla"], P["w_out"]], axis=0)
    sm_pack = lambda P: _pack_small_shard(P["conv_w"][0], P["gk_w2_fwd"][0], P["gk_w2_bwd"][0])
    me = 4 * lax.axis_index("x") + 2 * lax.axis_index("y") + lax.axis_index("c")
    w_in_t, m_in_t, v_in_t = (jnp.transpose(a[0]) for a in (w_in, m_w_in, v_w_in))
    g_in, g_sm = _gather_weights(w_in_t, sm_pack(W))
    wp_own = wp_stack(W).astype(MXU_DTYPE)
    wp_started = _exchange_start(wp_own, "gather", "gather_proj_start", after=g_sm)

    def proj_weights(after):
        own, land = _exchange_wait(wp_started, after, "gather", "gather_proj_wait")
        g_p = lax.dynamic_update_slice(land, own[None], (me, 0, 0, 0))
        return tuple(g_p[:, i].reshape(D_MODEL, D_MODEL) for i in range(3))

    rows_p = 3 * D_MODEL // N_DEV
    sent = {}

    def send_proj_grads(d_wpg, d_wpl, d_wout):
        p_p = jnp.stack([d_wpg, d_wpl, d_wout]).reshape(3, N_DEV, D_MODEL // N_DEV, D_MODEL).transpose(1, 0, 2, 3)
        sent["p_p"] = p_p.reshape(N_DEV, rows_p, D_MODEL)
        sent["started"] = _exchange_start(sent["p_p"], "blocks", "exchange_proj_start")
        return sent["started"][4][0:1, 0:1]

    def send_in_grads(d_wcat):
        p_in = _from_cat(d_wcat).reshape(N_DEV, SHARD_IN, D_MODEL)
        sib = _exchange_start(p_in, "sibling", "exchange_in_sibling_start")
        p_p, land_p = _exchange_wait(sent["started"], sib[4], "blocks", "exchange_proj_wait")
        sent["o_p"] = _sum_adam(land_p, wp_stack(W).reshape(rows_p, D_MODEL), wp_stack(M).reshape(rows_p, D_MODEL),
                                wp_stack(V).reshape(rows_p, D_MODEL), "adam_w_proj", 128, own=p_p, own_slot=_my_device)
        p_in, sib_land = _exchange_wait(sib, sent["o_p"][0], "sibling", "exchange_in_sibling_wait")
        sent["chips"] = _exchange_start(_pair_sum(p_in, sib_land, "pair_sum_w_in", tc=512), "chips", "exchange_in_chips_start")
        return sent["chips"][4]

    wcat = _to_cat(g_in.reshape(N_IN, D_MODEL))
    conv_full = g_sm[:, 0:CONV_K, :].transpose(1, 0, 2).reshape(CONV_K, N_DEV * SM_COLS)
    w2f_full = g_sm[:, 8:24, 0:64].transpose(1, 0, 2).reshape(16, 512)
    w2b_full = g_sm[:, 8:24, 64:128].transpose(1, 0, 2).reshape(16, 512)

    avec = _lane_row(a_log_fwd, a_log_bwd)
    dvec = _lane_row(dt_bias_fwd, dt_bias_bwd)
    G = _local_step(x[0], loss_target[0], wcat, conv_full, avec, dvec, gdn_norm_w, _pad_w2(w2f_full, 32), _pad_w2(w2b_full, 48),
                    gk_b2_fwd, gk_b2_bwd, gla_norm_w, proj_weights, ln_pre_w + wp_started[4][0:1, 0:1], ln_post_w, send_proj_grads,
                    send_in_grads)

    small_started = _exchange_start(_pack_small_grads(G), "blocks", "exchange_small_start")
    q_in, land_in = _exchange_wait(sent["chips"], small_started[4], "chips", "exchange_in_chips_wait")
    o_in = _sum_adam(land_in, w_in_t, m_in_t, v_in_t, "adam_w_in", SHARD_IN, tc=256, own=q_in, own_slot=_my_chip)
    o_p = sent["o_p"]
    p_small, land_small = _exchange_wait(small_started, o_in[0], "blocks", "exchange_small_wait")
    o_small, loss = _small_adam(land_small, p_small, W, M, V)

    res = []
    for kind in range(4):
        per = {name: o_small[name][kind] for name in SMALL_NAMES}
        per["w_in"] = jnp.transpose(o_in[kind])[None]
        pp = o_p[kind].reshape(3, 1, D_MODEL // N_DEV, D_MODEL)
        per["w_proj_gdn"], per["w_proj_gla"], per["w_out"] = pp[0], pp[1], pp[2]
        res.append([per[n] for n in _WEIGHTS])
    return (loss[0, 0], G["grad_x"][None], *res[0], *res[1], *res[2], *res[3])
```

```python
import jax
import jax.numpy as jnp
from jax import lax
from jax.experimental import pallas as pl
from jax.experimental.pallas import tpu as pltpu

f32 = jnp.float32
MXU_DTYPE = jnp.bfloat16

D_MODEL = 1024
CHUNK = 64
NORM_EPS = 1e-6
GDN_HEADS, GDN_DIM = 8, 128
GLA_HEADS, GLA_DK, GLA_DV = 4, 128, 256
GATE_NORMALIZER = 16.0
CONV_K = 5
N_IN = 9280
N_DEV = 8
SHARD_IN = N_IN // N_DEV

N_CAT = 9728
SMALL_OFF = 9216
LANES = 128

ADAM_LR, ADAM_B1, ADAM_B2, ADAM_EPS, ADAM_WD, ADAM_STEP = 0.001, 0.9, 0.999, 1e-08, 0.01, 10

VMEM_LIMIT = 56 * 1024 * 1024


def _cparams(sem=None):
    return pltpu.CompilerParams(dimension_semantics=sem, vmem_limit_bytes=VMEM_LIMIT)


_DN = {"nn": (((1,), (0,)), ((), ())), "nt": (((1,), (1,)), ((), ())), "tn": (((0,), (0,)), ((), ()))}
_BDN = {"nn": (((2,), (1,)), ((0,), (0,))), "nt": (((2,), (2,)), ((0,), (0,))), "tn": (((1,), (1,)), ((0,), (0,)))}


def _dot16(a, b, kind):
    return lax.dot_general(a, b, (_DN if a.ndim == 2 else _BDN)[kind], preferred_element_type=f32)


def _mx(a, b, kind):
    return _dot16(a.astype(MXU_DTYPE), b.astype(MXU_DTYPE), kind)


def _split2(a):
    hi = a.astype(jnp.bfloat16)
    return hi, (a - hi.astype(f32)).astype(jnp.bfloat16)


def _split3(a):
    p1 = a.astype(jnp.bfloat16)
    r = a - p1.astype(f32)
    p2 = r.astype(jnp.bfloat16)
    return p1, p2, (r - p2.astype(f32)).astype(jnp.bfloat16)


def _h3s(a2, b2, kind):
    (ah, al), (bh, bl) = a2, b2
    return _dot16(ah, bh, kind) + (_dot16(ah, bl, kind) + _dot16(al, bh, kind))


def _h3(a, b, kind):
    return _h3s(_split2(a), _split2(b), kind)


def _exact01(t01, x, kind="nn"):
    t = t01.astype(jnp.bfloat16)
    x1, x2, x3 = _split3(x)
    return _dot16(t, x1, kind) + (_dot16(t, x2, kind) + _dot16(t, x3, kind))


def _colsum_col(e):
    ones = jnp.ones(e.shape[:-1] + (LANES,), jnp.bfloat16)
    e1, e2, e3 = _split3(e)
    return (_dot16(e1, ones, "tn") + (_dot16(e2, ones, "tn") + _dot16(e3, ones, "tn")))[..., 0:1]


def _iota3(shape, axis):
    return lax.broadcasted_iota(jnp.int32, shape, axis)


def _dir_masks(B, nf, Cn):
    shp = (B, Cn, Cn)
    bb, ii, jj = _iota3(shp, 0), _iota3(shp, 1), _iota3(shp, 2)
    fwd = bb < nf
    rev = jnp.logical_not(fwd)
    incl = jnp.logical_or(jnp.logical_and(fwd, ii >= jj), jnp.logical_and(rev, ii <= jj))
    strict = jnp.logical_and(incl, ii != jj)
    return incl, strict, ii == jj


def _last_row(x, nf):
    B, Cn = x.shape[0], x.shape[1]
    fwd = _iota3((B, 1, 1), 0) < nf
    return jnp.where(fwd, x[:, Cn - 1 : Cn, :], x[:, 0:1, :])


def _add_at_last_row(x, val, nf):
    B, Cn = x.shape[0], x.shape[1]
    bb, rr = _iota3((B, Cn, 1), 0), _iota3((B, Cn, 1), 1)
    at = jnp.logical_or(jnp.logical_and(bb < nf, rr == Cn - 1), jnp.logical_and(bb >= nf, rr == 0))
    return x + jnp.where(at, val, 0.0)


def _sigmoid(x):
    return jax.nn.sigmoid(x)


def _silu(x):
    return x * _sigmoid(x)


def _silu_grad(x):
    s = _sigmoid(x)
    return s * (1.0 + x * (1.0 - s))


def _softplus(x):
    u = jnp.exp(-jnp.abs(x))
    l1p = jnp.where(u < 1e-3, u * (1.0 - u * (0.5 - u * (1.0 / 3.0))), jnp.log(1.0 + u))
    return jnp.maximum(x, 0.0) + l1p


def _mxr(x):
    return x.astype(MXU_DTYPE).astype(f32)


def _iota2(shape, axis):
    return lax.broadcasted_iota(jnp.int32, shape, axis)


def _matmul(a, b, kind, out_dtype, name, tm, tn, tk, out_t=False, after=None):
    extra = [] if after is None else [after]
    if kind == "nn":
        (M, K), N = a.shape, b.shape[1]
    elif kind == "nt":
        (M, K), N = a.shape, b.shape[0]
    else:
        (K, M), N = a.shape, b.shape[1]
    tm, tn, tk = min(tm, M), min(tn, N), min(tk, K)
    assert M % tm == 0 and N % tn == 0 and K % tk == 0, (name, M, N, K)
    nk = K // tk
    if kind == "tn":
        a_spec = pl.BlockSpec((tk, tm), lambda i, j, k: (k, i))
    else:
        a_spec = pl.BlockSpec((tm, tk), lambda i, j, k: (i, k))
    b_mode = dict(pipeline_mode=pl.Buffered(1)) if (tn == N and tk == K) else {}
    if kind == "nt":
        b_spec = pl.BlockSpec((tn, tk), lambda i, j, k: (j, k), **b_mode)
    else:
        b_spec = pl.BlockSpec((tk, tn), lambda i, j, k: (k, j), **b_mode)

    def body(a_ref, b_ref, *rest):
        o_ref, acc = rest[len(extra)], rest[len(extra) + 1 :]
        p = _mx(a_ref[...], b_ref[...], kind)
        if nk == 1:
            o_ref[...] = (p.T if out_t else p).astype(out_dtype)
        else:
            assert not out_t
            acc_ref = acc[0]
            k = pl.program_id(2)

            @pl.when(k == 0)
            def _():
                acc_ref[...] = p

            @pl.when(k > 0)
            def _():
                acc_ref[...] += p

            @pl.when(k == nk - 1)
            def _():
                o_ref[...] = acc_ref[...].astype(out_dtype)

    return pl.pallas_call(
        body,
        out_shape=jax.ShapeDtypeStruct((N, M) if out_t else (M, N), out_dtype),
        grid=(M // tm, N // tn, nk),
        in_specs=[a_spec, b_spec] + [pl.BlockSpec((8, LANES), lambda i, j, k: (0, 0))] * len(extra),
        out_specs=pl.BlockSpec((tn, tm), lambda i, j, k: (j, i)) if out_t else pl.BlockSpec((tm, tn), lambda i, j, k: (i, j)),
        scratch_shapes=[] if nk == 1 else [pltpu.VMEM((tm, tn), f32)],
        compiler_params=_cparams(("parallel", "parallel", "arbitrary")),
        name=name,
    )(a, b, *extra)


ROW_TILE = 512


def _rms_pre(x, w):
    T = x.shape[0]
    tb = min(ROW_TILE, T)

    def body(x_ref, w_ref, h_ref, ht_ref):
        xv = x_ref[...]
        r = lax.rsqrt(jnp.mean(xv * xv, axis=1, keepdims=True) + NORM_EPS)
        h = xv * r * w_ref[...]
        h_ref[...] = h.astype(h_ref.dtype)
        ht_ref[...] = h.T.astype(ht_ref.dtype)

    return pl.pallas_call(
        body,
        out_shape=(jax.ShapeDtypeStruct((T, D_MODEL), MXU_DTYPE), jax.ShapeDtypeStruct((D_MODEL, T), MXU_DTYPE)),
        grid=(T // tb,),
        in_specs=[pl.BlockSpec((tb, D_MODEL), lambda i: (i, 0)), pl.BlockSpec((1, D_MODEL), lambda i: (0, 0))],
        out_specs=(pl.BlockSpec((tb, D_MODEL), lambda i: (i, 0)), pl.BlockSpec((D_MODEL, tb), lambda i: (0, i))),
        compiler_params=_cparams(("parallel",)),
        name="rms_pre",
    )(x, w)


def _rms_pre_bwd(x, w, dh, dy):
    T = x.shape[0]
    tb = min(ROW_TILE, T)

    def body(x_ref, w_ref, dh_ref, dy_ref, dx_ref, dw_ref):
        i = pl.program_id(0)
        xv, dhv = x_ref[...], dh_ref[...]
        r = lax.rsqrt(jnp.mean(xv * xv, axis=1, keepdims=True) + NORM_EPS)
        dhw = dhv * w_ref[...]
        dx_ref[...] = dy_ref[...] + r * dhw - xv * (r * r * r) * jnp.mean(dhw * xv, axis=1, keepdims=True)
        part = jnp.sum(dhv * xv * r, axis=0, keepdims=True)

        @pl.when(i == 0)
        def _():
            dw_ref[...] = part

        @pl.when(i > 0)
        def _():
            dw_ref[...] += part

    blk = pl.BlockSpec((tb, D_MODEL), lambda i: (i, 0))
    row = pl.BlockSpec((1, D_MODEL), lambda i: (0, 0))
    return pl.pallas_call(
        body,
        out_shape=(jax.ShapeDtypeStruct((T, D_MODEL), f32), jax.ShapeDtypeStruct((1, D_MODEL), f32)),
        grid=(T // tb,),
        in_specs=[blk, row, blk, blk],
        out_specs=(blk, row),
        compiler_params=_cparams(("arbitrary",)),
        name="rms_pre_bwd",
    )(x, w, dh, dy)


SHIFTS = (-2, -1, 1, 2)


def _shifted(x):
    T = x.shape[0]
    t = _iota2(x.shape, 0)
    out = {}
    for d in SHIFTS:
        valid = (t < T - d) if d > 0 else (t >= -d)
        out[d] = jnp.where(valid, pltpu.roll(x, (-d) % T, 0), 0.0)
    return out


def _conv5(u, ush, cw):
    acc = u * cw[2:3, :]
    for j in (0, 1, 3, 4):
        acc = acc + ush[j - 2] * cw[j : j + 1, :]
    return acc


def _gdn_prep(proj, conv_w):
    T = proj.shape[0]
    nh = GDN_HEADS

    def body(u_ref, cw_ref, o_ref):
        j = pl.program_id(0)
        u = _mxr(u_ref[...])
        s = _silu(_conv5(u, _shifted(u), _mxr(cw_ref[...])))

        @pl.when(j < 2 * nh)
        def _():
            o_ref[0] = s * lax.rsqrt(jnp.sum(s * s, axis=1, keepdims=True) + NORM_EPS)

        @pl.when(j >= 2 * nh)
        def _():
            o_ref[0] = s

    return pl.pallas_call(
        body,
        out_shape=jax.ShapeDtypeStruct((3, T, GDN_HEADS * GDN_DIM), f32),
        grid=(3 * nh,),
        in_specs=[pl.BlockSpec((T, GDN_DIM), lambda j: (0, j)), pl.BlockSpec((CONV_K, GDN_DIM), lambda j: (0, j))],
        out_specs=pl.BlockSpec((1, T, GDN_DIM), lambda j: (j // nh, 0, j % nh)),
        compiler_params=_cparams(("parallel",)),
        name="gdn_prep",
    )(proj, conv_w)


def _gdn_prep_bwd(proj, conv_w, dqkv_f, dqkv_b, dproj):
    T = proj.shape[0]
    nh = GDN_HEADS


    def body(u_ref, cw_ref, df_ref, db_ref, _, du_ref, dcw_ref):
        j = pl.program_id(0)
        u, cw = _mxr(u_ref[...]), _mxr(cw_ref[...])
        ush = _shifted(u)
        c = _conv5(u, ush, cw)
        s = _silu(c)
        dn = df_ref[0] + db_ref[0]
        rinv = lax.rsqrt(jnp.sum(s * s, axis=1, keepdims=True) + NORM_EPS)
        ds_norm = rinv * dn - s * (rinv * rinv * rinv) * jnp.sum(dn * s, axis=1, keepdims=True)
        ds = jnp.where(j < 2 * nh, ds_norm, dn)
        dc = _mxr(ds * _silu_grad(c))
        dcsh = _shifted(dc)
        du = dc * cw[2:3, :]
        for jj in range(CONV_K):
            d = jj - 2
            dcw_ref[jj : jj + 1, :] = jnp.sum(dc * (u if d == 0 else ush[d]), axis=0, keepdims=True)
            if d != 0:
                du = du + dcsh[-d] * cw[jj : jj + 1, :]
        du_ref[...] = du.astype(du_ref.dtype)

    qspec = pl.BlockSpec((1, T, GDN_DIM), lambda j: (j // nh, 0, j % nh))
    return pl.pallas_call(
        body,
        out_shape=(jax.ShapeDtypeStruct(dproj.shape, dproj.dtype), jax.ShapeDtypeStruct((CONV_K, 3 * nh * GDN_DIM), f32)),
        grid=(3 * nh,),
        in_specs=[pl.BlockSpec((T, GDN_DIM), lambda j: (0, j)), pl.BlockSpec((CONV_K, GDN_DIM), lambda j: (0, j)), qspec, qspec, _ANY_SPEC],
        out_specs=(pl.BlockSpec((T, GDN_DIM), lambda j: (0, j)), pl.BlockSpec((CONV_K, GDN_DIM), lambda j: (0, j))),
        input_output_aliases={4: 0},
        compiler_params=_cparams(("parallel",)),
        name="gdn_prep_bwd",
    )(proj, conv_w, dqkv_f, dqkv_b, dproj)


def _chunk_tri(n, rev):
    i, j = _iota2((n, n), 0), _iota2((n, n), 1)
    same = jnp.right_shift(i, 6) == jnp.right_shift(j, 6)
    order = (j >= i) if rev else (j <= i)
    return jnp.where(jnp.logical_and(same, order), 1.0, 0.0).astype(f32)


def _gate_prep(proj, avec, dvec, w2f, w2b, b2f, b2b, tb=256):
    T = proj.shape[0]
    W = GLA_HEADS * GLA_DK

    def body(s_ref, a_ref, d_ref, wf_ref, wb_ref, bf_ref, bb_ref, g_ref, gc_ref, gf_ref, gb_ref):
        small = s_ref[...]
        tri_f, tri_r = _chunk_tri(tb, False), _chunk_tri(tb, True)
        lane = _iota2(small.shape, 1)
        lg = -jnp.exp(a_ref[...]) * _softplus(small + d_ref[...])
        beta = _sigmoid(small)
        g_ref[...] = jnp.where(lane < 16, lg, jnp.where(lane < 32, beta, 0.0))
        lgm = jnp.where(lane < 16, lg, 0.0)
        gc_ref[...] = jnp.where(lane < 8, _exact01(tri_f, lgm), _exact01(tri_r, lgm))
        gkf = -_softplus(-(_mx(small, wf_ref[...], "nn") + bf_ref[...])) * (1.0 / GATE_NORMALIZER)
        gkb = -_softplus(-(_mx(small, wb_ref[...], "nn") + bb_ref[...])) * (1.0 / GATE_NORMALIZER)
        gf_ref[...] = _exact01(tri_f, gkf)
        gb_ref[...] = _exact01(tri_r, gkb)

    blk = pl.BlockSpec((tb, LANES), lambda i: (i, 0))
    row = pl.BlockSpec((1, LANES), lambda i: (0, 0))
    wide = pl.BlockSpec((tb, W), lambda i: (i, 0))
    wsp = pl.BlockSpec((LANES, W), lambda i: (0, 0))
    wrow = pl.BlockSpec((1, W), lambda i: (0, 0))
    return pl.pallas_call(
        body,
        out_shape=(jax.ShapeDtypeStruct((T, LANES), f32), jax.ShapeDtypeStruct((T, LANES), f32),
                   jax.ShapeDtypeStruct((T, W), f32), jax.ShapeDtypeStruct((T, W), f32)),
        grid=(T // tb,),
        in_specs=[pl.BlockSpec((tb, LANES), lambda i: (i, SMALL_OFF // LANES)), row, row, wsp, wsp, wrow, wrow],
        out_specs=(blk, blk, wide, wide),
        compiler_params=_cparams(("parallel",)),
        name="gate_prep",
    )(proj, avec, dvec, w2f, w2b, b2f, b2b)


def _gdn_gates_bwd(proj, avec, dvec, dg_f, dg_b, dsmall_gla, dproj):
    T = proj.shape[0]
    tb = min(ROW_TILE, T)
    pad = N_CAT - SMALL_OFF

    def body(s_ref, a_ref, d_ref, gf_ref, gb_ref, dl_ref, _, ds_ref, da_ref, dd_ref):
        i = pl.program_id(0)
        small = s_ref[...]
        lane = _iota2(small.shape, 1)
        dgate = gf_ref[...] + gb_ref[...]
        z = small + d_ref[...]
        nega = -jnp.exp(a_ref[...])
        dz = dgate * nega * _sigmoid(z)
        beta = _sigmoid(small)
        dsm = jnp.where(lane < 16, dz, jnp.where(lane < 32, dgate * beta * (1.0 - beta), 0.0))
        ds_ref[:, :LANES] = (dsm + dl_ref[...]).astype(ds_ref.dtype)
        ds_ref[:, LANES:] = jnp.zeros((tb, pad - LANES), ds_ref.dtype)
        lg = nega * _softplus(z)
        pa = jnp.sum(jnp.where(lane < 16, dgate * lg, 0.0), axis=0, keepdims=True)
        pd = jnp.sum(jnp.where(lane < 16, dz, 0.0), axis=0, keepdims=True)

        @pl.when(i == 0)
        def _():
            da_ref[...] = pa
            dd_ref[...] = pd

        @pl.when(i > 0)
        def _():
            da_ref[...] += pa
            dd_ref[...] += pd

    blk = pl.BlockSpec((tb, LANES), lambda i: (i, 0))
    row = pl.BlockSpec((1, LANES), lambda i: (0, 0))
    return pl.pallas_call(
        body,
        out_shape=(jax.ShapeDtypeStruct(dproj.shape, dproj.dtype), jax.ShapeDtypeStruct((1, LANES), f32), jax.ShapeDtypeStruct((1, LANES), f32)),
        grid=(T // tb,),
        in_specs=[pl.BlockSpec((tb, LANES), lambda i: (i, SMALL_OFF // LANES)), row, row, blk, blk, blk, _ANY_SPEC],
        out_specs=(pl.BlockSpec((tb, pad), lambda i: (i, DP_SMALL_BLK)), row, row),
        input_output_aliases={6: 0},
        compiler_params=_cparams(("arbitrary",)),
        name="gdn_gates_bwd",
    )(proj, avec, dvec, dg_f, dg_b, dsmall_gla, dproj)


GDN_HB = 8


def _inv_unit_lower(L, eye):
    A = eye - L
    P2 = _split2(L)
    for _ in range(3):
        P2 = _split2(_h3s(P2, P2, "nn"))
        A = A + _h3s(_split2(A), P2, "nn")
    Ph = P2[0]
    for _ in range(2):
        Ph = _dot16(Ph, Ph, "nn").astype(jnp.bfloat16)
        A = A + _dot16(A.astype(jnp.bfloat16), Ph, "nn")
    return A


def _lane_col(blk, idx):
    lane = _iota2(blk.shape, 1)
    return jnp.sum(jnp.where(lane == idx, blk, 0.0), axis=1, keepdims=True)


def _out_norm(o_f, o_b, proj, gate_blk, w, hd, name):
    T, W = o_f.shape
    tb = min(ROW_TILE, T)
    nh = W // hd

    def body(of_ref, ob_ref, z_ref, w_ref, y_ref):
        wv = w_ref[...]
        for i in range(nh):
            sl = slice(i * hd, (i + 1) * hd)
            o = of_ref[:, sl] + ob_ref[:, sl]
            r = lax.rsqrt(jnp.mean(o * o, axis=1, keepdims=True) + NORM_EPS)
            y_ref[:, sl] = (o * r * wv * _silu(z_ref[:, sl])).astype(y_ref.dtype)

    blk = pl.BlockSpec((tb, W), lambda i: (i, 0))
    return pl.pallas_call(
        body,
        out_shape=jax.ShapeDtypeStruct((T, W), MXU_DTYPE),
        grid=(T // tb,),
        in_specs=[blk, blk, pl.BlockSpec((tb, W), lambda i: (i, gate_blk)), pl.BlockSpec((1, hd), lambda i: (0, 0))],
        out_specs=blk,
        compiler_params=_cparams(("parallel",)),
        name=name,
    )(o_f, o_b, proj, w)


def _out_norm_bwd(dy, o_f, o_b, proj, gate_blk, w, hd, name, dproj):
    T, W = o_f.shape
    tb = min(ROW_TILE, T)
    nh = W // hd

    def body(dy_ref, of_ref, ob_ref, z_ref, w_ref, _, do_ref, dz_ref, dw_ref):
        i = pl.program_id(0)
        wv = w_ref[...]
        dw = jnp.zeros((1, hd), f32)
        for a in range(nh):
            sl = slice(a * hd, (a + 1) * hd)
            o = of_ref[:, sl] + ob_ref[:, sl]
            z = z_ref[:, sl]
            dyv = dy_ref[:, sl]
            r = lax.rsqrt(jnp.mean(o * o, axis=1, keepdims=True) + NORM_EPS)
            orr = o * r
            dn = dyv * _silu(z)
            dz_ref[:, sl] = (dyv * orr * wv * _silu_grad(z)).astype(dz_ref.dtype)
            dw = dw + jnp.sum(dn * orr, axis=0, keepdims=True)
            dnw = dn * wv
            do_ref[:, sl] = r * dnw - o * (r * r * r) * jnp.mean(dnw * o, axis=1, keepdims=True)

        @pl.when(i == 0)
        def _():
            dw_ref[...] = dw

        @pl.when(i > 0)
        def _():
            dw_ref[...] += dw

    blk = pl.BlockSpec((tb, W), lambda i: (i, 0))
    row = pl.BlockSpec((1, hd), lambda i: (0, 0))
    return pl.pallas_call(
        body,
        out_shape=(jax.ShapeDtypeStruct((T, W), f32), jax.ShapeDtypeStruct(dproj.shape, dproj.dtype), jax.ShapeDtypeStruct((1, hd), f32)),
        grid=(T // tb,),
        in_specs=[blk, blk, blk, pl.BlockSpec((tb, W), lambda i: (i, gate_blk)), row, _ANY_SPEC],
        out_specs=(blk, pl.BlockSpec((tb, W), lambda i: (i, gate_blk)), row),
        input_output_aliases={5: 1},
        compiler_params=_cparams(("arbitrary",)),
        name=name,
    )(dy, o_f, o_b, proj, w, dproj)


def _gla_prep_bwd(proj, w2f, w2b, b2f, b2b, dgk_f, dgk_b, dqkv, dproj):
    T = proj.shape[0]
    tb = min(ROW_TILE, T)
    W = GLA_HEADS * GLA_DK
    WV = GLA_HEADS * GLA_DV

    def body(s_ref, wf_ref, wb_ref, bf_ref, bb_ref, df_ref, db_ref, qf, qb, kf, kb, vf, vb, _,
             dp_ref, ds_ref, dwf_ref, dwb_ref, dbf_ref, dbb_ref):
        i = pl.program_id(0)
        dp_ref[:, :W] = (qf[...] + qb[...]).astype(dp_ref.dtype)
        dp_ref[:, W : 2 * W] = (kf[...] + kb[...]).astype(dp_ref.dtype)
        dp_ref[:, 2 * W :] = (vf[...] + vb[...]).astype(dp_ref.dtype)
        small = s_ref[...]
        dsm = jnp.zeros((tb, LANES), f32)
        parts = []
        for w_ref, b_ref, d_ref in ((wf_ref, bf_ref, df_ref), (wb_ref, bb_ref, db_ref)):
            pre = _mx(small, w_ref[...], "nn") + b_ref[...]
            dpre = d_ref[...] * (1.0 / GATE_NORMALIZER) * _sigmoid(-pre)
            dsm = dsm + _mx(dpre, w_ref[...], "nt")
            parts.append((_mx(small, dpre, "tn"), jnp.sum(dpre, axis=0, keepdims=True)))
        ds_ref[...] = dsm

        @pl.when(i == 0)
        def _():
            dwf_ref[...], dbf_ref[...] = parts[0]
            dwb_ref[...], dbb_ref[...] = parts[1]

        @pl.when(i > 0)
        def _():
            dwf_ref[...] += parts[0][0]
            dbf_ref[...] += parts[0][1]
            dwb_ref[...] += parts[1][0]
            dbb_ref[...] += parts[1][1]

    blk = pl.BlockSpec((tb, W), lambda i: (i, 0))
    vblk = pl.BlockSpec((tb, WV), lambda i: (i, 0))
    wsp = pl.BlockSpec((LANES, W), lambda i: (0, 0))
    row = pl.BlockSpec((1, W), lambda i: (0, 0))
    return pl.pallas_call(
        body,
        out_shape=(jax.ShapeDtypeStruct(dproj.shape, dproj.dtype), jax.ShapeDtypeStruct((T, LANES), f32), jax.ShapeDtypeStruct((LANES, W), f32),
                   jax.ShapeDtypeStruct((LANES, W), f32), jax.ShapeDtypeStruct((1, W), f32), jax.ShapeDtypeStruct((1, W), f32)),
        grid=(T // tb,),
        in_specs=[pl.BlockSpec((tb, LANES), lambda i: (i, SMALL_OFF // LANES)), wsp, wsp, row, row, blk, blk,
                  blk, blk, blk, blk, vblk, vblk, _ANY_SPEC],
        out_specs=(pl.BlockSpec((tb, 2 * W + WV), lambda i: (i, DP_GLA_QKV_BLK)), pl.BlockSpec((tb, LANES), lambda i: (i, 0)), wsp, wsp, row, row),
        input_output_aliases={13: 0},
        compiler_params=_cparams(("arbitrary",)),
        name="gla_prep_bwd",
    )(proj, w2f, w2b, b2f, b2b, dgk_f, dgk_b, *dqkv, dproj)


GATE_A_BLK, GATE_B_BLK, Z_BLK, G_B_BLK = 6144 // 1024, 7168 // 1024, 3072 // 1024, 8192 // 1024
DP_GATES_BLK, DP_GLA_QKV_BLK, DP_SMALL_BLK = 6144 // 2048, 4096 // 2048, SMALL_OFF // 512
_ANY_SPEC = pl.BlockSpec(memory_space=pl.ANY)


def _mid(og, obg, proj, x, target, wpg, wpl, wout, w_post, tb=256):
    T = x.shape[0]
    Dm = D_MODEL

    def body(og_ref, obg_ref, ga_ref, gb_ref, x_ref, t_ref, wpg_ref, wpl_ref, wo_ref, wp_ref,
             loss_ref, dwp_ref, dy_ref, m_ref, dout_ref, dya_ref, dyb_ref, dgates_ref, dog_ref, dobg_ref):
        i = pl.program_id(0)
        ya = _mx(og_ref[...], wpg_ref[...], "nn")
        yb = _mx(obg_ref[...], wpl_ref[...], "nn")
        sa, sb = _sigmoid(ga_ref[...]), _sigmoid(gb_ref[...])
        m = sa * ya + sb * yb
        m_ref[...] = m.astype(m_ref.dtype)
        out = _mx(m, wo_ref[...], "nn")
        r = lax.rsqrt(jnp.mean(out * out, axis=1, keepdims=True) + NORM_EPS)
        wp = wp_ref[...]
        e = x_ref[...] + out * r * wp - t_ref[...]
        dy = e * (1.0 / Dm)
        dy_ref[...] = dy
        lpart = jnp.sum(e * e, axis=0, keepdims=True)
        wpart = jnp.sum(dy * out * r, axis=0, keepdims=True)
        dyn = dy * wp
        dout = r * dyn - out * (r * r * r) * jnp.mean(dyn * out, axis=1, keepdims=True)
        dout_ref[...] = dout.astype(dout_ref.dtype)
        dm = _mx(dout, wo_ref[...], "nt")
        dya, dyb = dm * sa, dm * sb
        dya_ref[...] = dya.astype(dya_ref.dtype)
        dyb_ref[...] = dyb.astype(dyb_ref.dtype)
        dgates_ref[:, :Dm] = (dm * ya * sa * (1.0 - sa)).astype(dgates_ref.dtype)
        dgates_ref[:, Dm:] = (dm * yb * sb * (1.0 - sb)).astype(dgates_ref.dtype)
        dog_ref[...] = _mx(dya, wpg_ref[...], "nt")
        dobg_ref[...] = _mx(dyb, wpl_ref[...], "nt")

        @pl.when(i == 0)
        def _():
            loss_ref[...] = lpart
            dwp_ref[...] = wpart

        @pl.when(i > 0)
        def _():
            loss_ref[...] += lpart
            dwp_ref[...] += wpart

    blk = pl.BlockSpec((tb, Dm), lambda i: (i, 0))
    row = pl.BlockSpec((1, Dm), lambda i: (0, 0))
    wsp = pl.BlockSpec((Dm, Dm), lambda i: (0, 0))
    act = jax.ShapeDtypeStruct((T, Dm), MXU_DTYPE)
    big = jax.ShapeDtypeStruct((T, Dm), f32)
    vec = jax.ShapeDtypeStruct((1, Dm), f32)
    return pl.pallas_call(
        body,
        out_shape=(vec, vec, big, act, act, act, act, jax.ShapeDtypeStruct((T, N_CAT), MXU_DTYPE), big, big),
        grid=(T // tb,),
        in_specs=[blk, blk, pl.BlockSpec((tb, Dm), lambda i: (i, GATE_A_BLK)), pl.BlockSpec((tb, Dm), lambda i: (i, GATE_B_BLK)),
                  blk, blk, wsp, wsp, wsp, row],
        out_specs=(row, row) + (blk,) * 5 + (pl.BlockSpec((tb, 2 * Dm), lambda i: (i, DP_GATES_BLK)), blk, blk),
        compiler_params=_cparams(("arbitrary",)),
        name="mid",
    )(og, obg, proj, proj, x, target, wpg, wpl, wout, w_post)


def _heads(ref, lead, n, width, rows=slice(None)):
    return [ref[lead + (rows, slice(i * width, (i + 1) * width))] for i in range(n)]


class _GdnBatch:
    def __init__(self, q, k, v, gcol, grow, beta, nf, A=None):
        B, Cn = q.shape[0], q.shape[1]
        self.incl, self.strict, eye = _dir_masks(B, nf, Cn)
        self.decay = jnp.where(self.incl, jnp.exp(jnp.where(self.incl, gcol - grow, 0.0)), 0.0)
        self.k, self.v = k, v
        gl = _last_row(gcol, nf)
        self.beta = jnp.broadcast_to(beta, q.shape)
        self.eg = jnp.broadcast_to(jnp.exp(gcol), q.shape)
        self.ekd = jnp.broadcast_to(jnp.exp(gl - gcol), q.shape)
        self.qs = q * (GDN_DIM**-0.5)
        self.kb = k * self.beta
        self.L = jnp.where(self.strict, _mx(self.kb, k, "nt") * self.decay, 0.0)
        if A is None:
            A = _inv_unit_lower(self.L, jnp.where(eye, 1.0, 0.0).astype(f32))
        self.A = A
        self.A2 = _split2(A)
        self.kbg = self.kb * self.eg
        self.sol = _h3s(self.A2, _split2(jnp.concatenate([v * self.beta, self.kbg], axis=2)), "nn")
        self.u = self.sol[:, :, :GDN_DIM]
        self.w = self.sol[:, :, GDN_DIM:]
        self.attn = jnp.where(self.incl, _mx(self.qs, k, "nt") * self.decay, 0.0)
        self.qd = self.qs * self.eg
        self.kd = k * self.ekd
        self.egl = jnp.exp(gl)


def _gdn_chain_inputs(qf_ref, qb_ref, gf_ref, gb_ref, cf_ref, cb_ref, b, hb):
    qkv = [jnp.stack(_heads(qf_ref, (i,), hb, GDN_DIM) + _heads(qb_ref, (i,), hb, GDN_DIM)) for i in range(3)]
    cols, rows, betas = [], [], []
    for d, (g_ref, c_ref) in enumerate(((gf_ref, cf_ref), (gb_ref, cb_ref))):
        gates, gc = g_ref[...], c_ref[...]
        for hh in range(hb):
            h = b * hb + hh
            col = _lane_col(gc, h + 8 * d)
            cols.append(col)
            rows.append(jnp.broadcast_to(col, (CHUNK, LANES)).T[0:1, :])
            betas.append(_lane_col(gates, 16 + h + 8 * d))
    return qkv[0], qkv[1], qkv[2], jnp.stack(cols), jnp.stack(rows), jnp.stack(betas)


GDN_CPS_FWD, GDN_CPS_BWD = 2, 1


def _chunk_views(refs, s_f, s_b):
    out = []
    for i, r in enumerate(refs):
        s = s_f if i % 2 == 0 else s_b
        rows = pl.ds(s * CHUNK, CHUNK)
        out.append(r.at[pl.ds(s, 1)] if len(r.shape) == 4 else (r.at[:, rows, :] if len(r.shape) == 3 else r.at[rows, :]))
    return out


def _gdn_specs(cps, hb):
    R, W = cps * CHUNK, hb * GDN_DIM
    qsp = lambda ci: pl.BlockSpec((3, R, W), lambda n, b: (0, ci(n), b))
    gsp = lambda ci: pl.BlockSpec((R, LANES), lambda n, b: (ci(n), 0))
    osp = lambda ci: pl.BlockSpec((R, W), lambda n, b: (ci(n), b))
    ssp = lambda ci: pl.BlockSpec((cps, hb, GDN_DIM, GDN_DIM), lambda n, b: (ci(n), b, 0, 0))
    asp = lambda ci: pl.BlockSpec((cps, hb, CHUNK, CHUNK), lambda n, b: (ci(n), b, 0, 0))
    return qsp, gsp, osp, ssp, asp


def _gdn_fwd(qkvn, gates, gc):
    T = qkvn.shape[1]
    N = T // CHUNK
    nh, hb = GDN_HEADS, GDN_HB
    cps = min(GDN_CPS_FWD, N)
    NB = N // cps

    def body(*refs):
        n, b = pl.program_id(0), pl.program_id(1)
        s_scr = refs[-1]

        @pl.when(n == 0)
        def _():
            s_scr[0, pl.ds(b * hb, hb)] = jnp.zeros((hb, GDN_DIM, GDN_DIM), f32)
            s_scr[1, pl.ds(b * hb, hb)] = jnp.zeros((hb, GDN_DIM, GDN_DIM), f32)

        for s in range(cps):
            chunk(*_chunk_views(refs[:-1], s, cps - 1 - s), s_scr, b)

    def chunk(qf_ref, qb_ref, gf_ref, gb_ref, cf_ref, cb_ref, of_ref, ob_ref, sf_ref, sb_ref, af_ref, ab_ref, s_scr, b):
        hs = pl.ds(b * hb, hb)
        q, k, v, gcol, grow, beta = _gdn_chain_inputs(qf_ref, qb_ref, gf_ref, gb_ref, cf_ref, cb_ref, b, hb)
        ck = _GdnBatch(q, k, v, gcol, grow, beta, hb)
        S = jnp.concatenate([s_scr[0, hs], s_scr[1, hs]], axis=0)
        sf_ref[0], sb_ref[0] = S[:hb], S[hb:]
        af_ref[0], ab_ref[0] = ck.A[:hb], ck.A[hb:]
        v_new = ck.u - _mx(ck.w, S, "nn")
        o = _mx(ck.qd, S, "nn") + _mx(ck.attn, v_new, "nn")
        S_new = S * ck.egl + _mx(ck.kd, v_new, "tn")
        for hh in range(hb):
            sl = slice(hh * GDN_DIM, (hh + 1) * GDN_DIM)
            of_ref[:, sl] = o[hh]
            ob_ref[:, sl] = o[hb + hh]
        s_scr[0, hs] = S_new[:hb]
        s_scr[1, hs] = S_new[hb:]

    fw, bw = (lambda n: n), (lambda n: NB - 1 - n)
    qsp, gsp, osp, ssp, asp = _gdn_specs(cps, hb)
    o_shape = jax.ShapeDtypeStruct((T, nh * GDN_DIM), f32)
    s_shape = jax.ShapeDtypeStruct((N, nh, GDN_DIM, GDN_DIM), f32)
    a_shape = jax.ShapeDtypeStruct((N, nh, CHUNK, CHUNK), f32)
    return pl.pallas_call(
        body,
        out_shape=(o_shape, o_shape, s_shape, s_shape, a_shape, a_shape),
        grid=(NB, nh // hb),
        in_specs=[qsp(fw), qsp(bw), gsp(fw), gsp(bw), gsp(fw), gsp(bw)],
        out_specs=(osp(fw), osp(bw), ssp(fw), ssp(bw), asp(fw), asp(bw)),
        scratch_shapes=[pltpu.VMEM((2, nh, GDN_DIM, GDN_DIM), f32)],
        compiler_params=_cparams(("arbitrary", "arbitrary")),
        name="gdn_scan_fwd",
    )(qkvn, qkvn, gates, gates, gc, gc)


def _gdn_bwd(qkvn, gates, gc, st_f, st_b, a_f, a_b, do):
    T = qkvn.shape[1]
    N = T // CHUNK
    nh, hb = GDN_HEADS, GDN_HB
    nb = nh // hb
    cps = min(GDN_CPS_BWD, N)
    NB = N // cps

    def body(*refs):
        n, b = pl.program_id(0), pl.program_id(1)
        ds_scr = refs[-1]

        @pl.when(n == 0)
        def _():
            ds_scr[0, pl.ds(b * hb, hb)] = jnp.zeros((hb, GDN_DIM, GDN_DIM), f32)
            ds_scr[1, pl.ds(b * hb, hb)] = jnp.zeros((hb, GDN_DIM, GDN_DIM), f32)

        for s in range(cps):
            chunk(*_chunk_views(refs[:-1], cps - 1 - s, s), ds_scr, b)

    def chunk(qf_ref, qb_ref, gf_ref, gb_ref, cf_ref, cb_ref, sf_ref, sb_ref, af_ref, ab_ref, dof_ref, dob_ref,
              dqf_ref, dqb_ref, dgf_ref, dgb_ref, ds_scr, b):
        hs = pl.ds(b * hb, hb)

        @pl.when(b == 0)
        def _():
            dgf_ref[...] = jnp.zeros((CHUNK, LANES), f32)
            dgb_ref[...] = jnp.zeros((CHUNK, LANES), f32)

        q, k, v, gcol, grow, beta = _gdn_chain_inputs(qf_ref, qb_ref, gf_ref, gb_ref, cf_ref, cb_ref, b, hb)
        ck = _GdnBatch(q, k, v, gcol, grow, beta, hb, A=jnp.concatenate([af_ref[0], ab_ref[0]], axis=0))
        S = jnp.concatenate([sf_ref[0], sb_ref[0]], axis=0)
        dS = jnp.concatenate([ds_scr[0, hs], ds_scr[1, hs]], axis=0)
        dov = jnp.stack(_heads(dof_ref, (), hb, GDN_DIM) + _heads(dob_ref, (), hb, GDN_DIM))
        v_new = ck.u - _mx(ck.w, S, "nn")
        d_vnew = _mx(ck.attn, dov, "tn") + _mx(ck.kd, dS, "nn")
        d_attn = jnp.where(ck.incl, _mx(dov, v_new, "nt"), 0.0)
        d_qd = _mx(dov, S, "nt")
        d_kd = _mx(v_new, dS, "nt")
        d_gl = jnp.sum(jnp.sum(dS * S, axis=2, keepdims=True), axis=1, keepdims=True) * ck.egl
        dS_new = dS * ck.egl + _mx(ck.qd, dov, "tn") - _mx(ck.w, d_vnew, "tn")
        ds_scr[0, hs] = dS_new[:hb]
        ds_scr[1, hs] = dS_new[hb:]
        d_w = -_mx(d_vnew, S, "nt")
        d_rhs = _h3s(ck.A2, _split2(jnp.concatenate([d_vnew, d_w], axis=2)), "tn")
        d_vb, d_kbg = d_rhs[:, :, :GDN_DIM], d_rhs[:, :, GDN_DIM:]
        dL = -jnp.where(ck.strict, _h3(d_rhs, ck.sol, "nt"), 0.0)
        P = dL * ck.decay
        Q = d_attn * ck.decay
        d_kb = _mx(P, k, "nn") + d_kbg * ck.eg
        d_qs = _mx(Q, k, "nn") + d_qd * ck.eg
        d_k = _mx(P, ck.kb, "tn") + _mx(Q, ck.qs, "tn") + d_kd * ck.ekd + d_kb * ck.beta
        E = dL * ck.L + d_attn * ck.attn
        kdsum = jnp.sum(d_kd * ck.kd, axis=2, keepdims=True)
        d_gc = (jnp.sum(E, axis=2, keepdims=True) - _colsum_col(E)
                + jnp.sum(d_kbg * ck.kbg, axis=2, keepdims=True)
                + jnp.sum(d_qd * ck.qd, axis=2, keepdims=True) - kdsum)
        d_gc = _add_at_last_row(d_gc, d_gl + jnp.sum(kdsum, axis=1, keepdims=True), hb)
        d_beta = jnp.sum(d_vb * v, axis=2, keepdims=True) + jnp.sum(d_kb * k, axis=2, keepdims=True)
        d_q = d_qs * (GDN_DIM**-0.5)
        d_v = d_vb * ck.beta
        lane = _iota2((CHUNK, LANES), 1)
        for d, (dq_ref, dg_ref) in enumerate(((dqf_ref, dgf_ref), (dqb_ref, dgb_ref))):
            acc = dg_ref[...]
            for hh in range(hb):
                c, h = d * hb + hh, b * hb + hh
                sl = slice(hh * GDN_DIM, (hh + 1) * GDN_DIM)
                dq_ref[0, :, sl] = d_q[c]
                dq_ref[1, :, sl] = d_k[c]
                dq_ref[2, :, sl] = d_v[c]
                acc = jnp.where(lane == h + 8 * d, d_gc[c], acc)
                acc = jnp.where(lane == 16 + h + 8 * d, d_beta[c], acc)
            dg_ref[...] = acc

        @pl.when(b == nb - 1)
        def _():
            for d, dg_ref in enumerate((dgf_ref, dgb_ref)):
                blk = dg_ref[...]
                dg_ref[...] = jnp.where(lane < 16, _exact01(_chunk_tri(CHUNK, d == 0), blk), blk)

    fw, bw = (lambda n: NB - 1 - n), (lambda n: n)
    qsp, gsp, osp, ssp, asp = _gdn_specs(cps, hb)
    dq_shape = jax.ShapeDtypeStruct((3, T, nh * GDN_DIM), f32)
    dg_shape = jax.ShapeDtypeStruct((T, LANES), f32)
    return pl.pallas_call(
        body,
        out_shape=(dq_shape, dq_shape, dg_shape, dg_shape),
        grid=(NB, nb),
        in_specs=[qsp(fw), qsp(bw), gsp(fw), gsp(bw), gsp(fw), gsp(bw), ssp(fw), ssp(bw), asp(fw), asp(bw), osp(fw), osp(bw)],
        out_specs=(qsp(fw), qsp(bw), gsp(fw), gsp(bw)),
        scratch_shapes=[pltpu.VMEM((2, nh, GDN_DIM, GDN_DIM), f32)],
        compiler_params=_cparams(("arbitrary", "arbitrary")),
        name="gdn_scan_bwd",
    )(qkvn, qkvn, gates, gates, gc, gc, st_f, st_b, a_f, a_b, do, do)


class _GlaBatch:
    def __init__(self, q, k, G, nf):
        B, Cn = q.shape[0], q.shape[1]
        self.incl, _, _ = _dir_masks(B, nf, Cn)
        self.eG = jnp.exp(G)
        self.enG = jnp.exp(-G)
        self.qg = q * (GLA_DK**-0.5) * self.eG
        self.kg = k * self.enG
        self.attn = jnp.where(self.incl, _mx(self.qg, self.kg, "nt"), 0.0)
        gl = _last_row(G, nf)
        self.ekd = jnp.exp(gl - G)
        self.kd = k * self.ekd
        self.egl = jnp.exp(gl)


GLA_CPS = 4


def _chunk_rows(s):
    return slice(s * CHUNK, (s + 1) * CHUNK)


def _gla_chain_inputs(refs_f, refs_b, rows_f, rows_b):
    nh = GLA_HEADS
    out = []
    for i, width in enumerate((GLA_DK, GLA_DK, GLA_DV, GLA_DK)):
        out.append(jnp.stack(_heads(refs_f[i], (), nh, width, rows_f) + _heads(refs_b[i], (), nh, width, rows_b)))
    return out


def _gla_specs(rows, ci):
    nh = GLA_HEADS
    return [
        pl.BlockSpec((rows, nh * GLA_DK), lambda n: (ci(n), 4096 // (nh * GLA_DK))),
        pl.BlockSpec((rows, nh * GLA_DK), lambda n: (ci(n), 4608 // (nh * GLA_DK))),
        pl.BlockSpec((rows, nh * GLA_DV), lambda n: (ci(n), 5120 // (nh * GLA_DV))),
        pl.BlockSpec((rows, nh * GLA_DK), lambda n: (ci(n), 0)),
    ]


def _gla_fwd(proj, G_f, G_b):
    T = proj.shape[0]
    N = T // CHUNK
    nh = GLA_HEADS
    cps = min(GLA_CPS, N)
    NB = N // cps

    def body(qf, kf, vf, gf, qb, kb, vb, gb, of_ref, ob_ref, sf_ref, sb_ref, s_scr):
        n = pl.program_id(0)

        @pl.when(n == 0)
        def _():
            s_scr[...] = jnp.zeros((2 * nh, GLA_DV, GLA_DK), f32)

        for s in range(cps):
            sb = cps - 1 - s
            q, k, v, G = _gla_chain_inputs((qf, kf, vf, gf), (qb, kb, vb, gb), _chunk_rows(s), _chunk_rows(sb))
            ck = _GlaBatch(q, k, G, nh)
            St = s_scr[...]
            sf_ref[s], sb_ref[sb] = St[:nh].astype(sf_ref.dtype), St[nh:].astype(sb_ref.dtype)
            o = _mx(ck.qg, St, "nt") + _mx(ck.attn, v, "nn")
            for i in range(nh):
                sl = slice(i * GLA_DV, (i + 1) * GLA_DV)
                of_ref[_chunk_rows(s), sl] = o[i]
                ob_ref[_chunk_rows(sb), sl] = o[nh + i]
            s_scr[...] = St * ck.egl + _mx(v, ck.kd, "tn")

    fw, bw = (lambda n: n), (lambda n: NB - 1 - n)
    osp = lambda ci: pl.BlockSpec((cps * CHUNK, nh * GLA_DV), lambda n: (ci(n), 0))
    ssp = lambda ci: pl.BlockSpec((cps, nh, GLA_DV, GLA_DK), lambda n: (ci(n), 0, 0, 0))
    o_shape = jax.ShapeDtypeStruct((T, nh * GLA_DV), f32)
    s_shape = jax.ShapeDtypeStruct((N, nh, GLA_DV, GLA_DK), MXU_DTYPE)
    return pl.pallas_call(
        body,
        out_shape=(o_shape, o_shape, s_shape, s_shape),
        grid=(NB,),
        in_specs=_gla_specs(cps * CHUNK, fw) + _gla_specs(cps * CHUNK, bw),
        out_specs=(osp(fw), osp(bw), ssp(fw), ssp(bw)),
        scratch_shapes=[pltpu.VMEM((2 * nh, GLA_DV, GLA_DK), f32)],
        compiler_params=_cparams(("arbitrary",)),
        name="gla_scan_fwd",
    )(proj, proj, proj, G_f, proj, proj, proj, G_b)


def _gla_bwd(proj, G_f, G_b, st_f, st_b, do):
    T = proj.shape[0]
    N = T // CHUNK
    nh = GLA_HEADS
    cps = min(GLA_CPS, N)
    NB = N // cps

    def body(qf, kf, vf, gf, sf_ref, dof_ref, qb, kb, vb, gb, sb_ref, dob_ref,
             dqf_ref, dkf_ref, dvf_ref, dgf_ref, dqb_ref, dkb_ref, dvb_ref, dgb_ref, ds_scr):
        n = pl.program_id(0)

        @pl.when(n == 0)
        def _():
            ds_scr[...] = jnp.zeros((2 * nh, GLA_DV, GLA_DK), f32)

        shp = (2 * nh, CHUNK, CHUNK)
        bb, ii, jj = _iota3(shp, 0), _iota3(shp, 1), _iota3(shp, 2)
        tri = jnp.where(jnp.logical_or(jnp.logical_and(bb < nh, jj >= ii), jnp.logical_and(bb >= nh, jj <= ii)), 1.0, 0.0)
        for s in range(cps):
            sf = cps - 1 - s
            rows = (_chunk_rows(sf), _chunk_rows(s))
            q, k, v, G = _gla_chain_inputs((qf, kf, vf, gf), (qb, kb, vb, gb), *rows)
            ck = _GlaBatch(q, k, G, nh)
            St = jnp.concatenate([sf_ref[sf], sb_ref[s]], axis=0).astype(f32)
            dSt = ds_scr[...]
            dov = jnp.stack(_heads(dof_ref, (), nh, GLA_DV, rows[0]) + _heads(dob_ref, (), nh, GLA_DV, rows[1]))
            d_attn = jnp.where(ck.incl, _mx(dov, v, "nt"), 0.0)
            d_qg = _mx(dov, St, "nn") + _mx(d_attn, ck.kg, "nn")
            d_kg = _mx(d_attn, ck.qg, "tn")
            d_v = _mx(ck.attn, dov, "tn") + _mx(ck.kd, dSt, "nt")
            d_kd = _mx(v, dSt, "nn")
            ds_scr[...] = dSt * ck.egl + _mx(dov, ck.qg, "tn")
            kdd = d_kd * ck.kd
            d_gl = jnp.sum(dSt * St, axis=1, keepdims=True) * ck.egl + jnp.sum(kdd, axis=1, keepdims=True)
            d_q = d_qg * ck.eG * (GLA_DK**-0.5)
            d_k = d_kg * ck.enG + d_kd * ck.ekd
            d_gk = _exact01(tri, _add_at_last_row(d_qg * ck.qg - d_kg * ck.kg - kdd, d_gl, nh))
            for d, (dq_ref, dk_ref, dv_ref, dg_ref) in enumerate(((dqf_ref, dkf_ref, dvf_ref, dgf_ref), (dqb_ref, dkb_ref, dvb_ref, dgb_ref))):
                for i in range(nh):
                    c = d * nh + i
                    ks, vs = slice(i * GLA_DK, (i + 1) * GLA_DK), slice(i * GLA_DV, (i + 1) * GLA_DV)
                    dq_ref[rows[d], ks] = d_q[c]
                    dk_ref[rows[d], ks] = d_k[c]
                    dv_ref[rows[d], vs] = d_v[c]
                    dg_ref[rows[d], ks] = d_gk[c]

    R = cps * CHUNK

    def in_specs(ci):
        return _gla_specs(R, ci) + [
            pl.BlockSpec((cps, nh, GLA_DV, GLA_DK), lambda n: (ci(n), 0, 0, 0)),
            pl.BlockSpec((R, nh * GLA_DV), lambda n: (ci(n), 0)),
        ]

    def out_specs(ci):
        ksp = pl.BlockSpec((R, nh * GLA_DK), lambda n: (ci(n), 0))
        return [ksp, ksp, pl.BlockSpec((R, nh * GLA_DV), lambda n: (ci(n), 0)), ksp]

    fw, bw = (lambda n: NB - 1 - n), (lambda n: n)
    k_shape = jax.ShapeDtypeStruct((T, nh * GLA_DK), f32)
    v_shape = jax.ShapeDtypeStruct((T, nh * GLA_DV), f32)
    return pl.pallas_call(
        body,
        out_shape=(k_shape, k_shape, v_shape, k_shape) * 2,
        grid=(NB,),
        in_specs=in_specs(fw) + in_specs(bw),
        out_specs=tuple(out_specs(fw) + out_specs(bw)),
        scratch_shapes=[pltpu.VMEM((2 * nh, GLA_DV, GLA_DK), f32)],
        compiler_params=_cparams(("arbitrary",)),
        name="gla_scan_bwd",
    )(proj, proj, proj, G_f, st_f, do, proj, proj, proj, G_b, st_b, do)


def _local_step(x, target, wcat, conv_w, avec, dvec, gdn_norm_w, w2f, w2b, b2f, b2b, gla_norm_w, proj_weights, w_pre, w_post,
                send_proj_grads=None, send_in_grads=None):
    h, h_t = _rms_pre(x, w_pre)
    proj = _matmul(h, wcat, "nt", f32, "proj_in", 2048, 512, 1024)
    qkvn = _gdn_prep(proj, conv_w)
    gates, gc, G_f, G_b = _gate_prep(proj, avec, dvec, w2f, w2b, b2f, b2b)
    oa_f, oa_b, sa_f, sa_b, inv_f, inv_b = _gdn_fwd(qkvn, gates, gc)
    og = _out_norm(oa_f, oa_b, proj, Z_BLK, gdn_norm_w, GDN_DIM, "gdn_out")
    ob_f, ob_b, sb_f, sb_b = _gla_fwd(proj, G_f, G_b)
    obg = _out_norm(ob_f, ob_b, proj, G_B_BLK, gla_norm_w, GLA_DV, "gla_out")
    wpg, wpl, wout = proj_weights(obg)
    (loss_row, d_wpost, dy, m, dout, dya, dyb, dproj, dog, dobg) = _mid(og, obg, proj, x, target, wpg, wpl, wout, w_post)

    d_wout = _matmul(m, dout, "tn", MXU_DTYPE, "dw_out", 1024, 512, 2048)
    d_wpg = _matmul(og, dya, "tn", MXU_DTYPE, "dw_proj_gdn", 1024, 512, 2048)
    d_wpl = _matmul(obg, dyb, "tn", MXU_DTYPE, "dw_proj_gla", 1024, 512, 2048)
    if send_proj_grads is not None:
        gla_norm_w = gla_norm_w + send_proj_grads(d_wpg, d_wpl, d_wout)

    do_b, dproj, d_gla_norm = _out_norm_bwd(dobg, ob_f, ob_b, proj, G_B_BLK, gla_norm_w, GLA_DV, "gla_out_bwd", dproj)
    dq_f, dk_f, dv_f, dgk_f, dq_b, dk_b, dv_b, dgk_b = _gla_bwd(proj, G_f, G_b, sb_f, sb_b, do_b)
    dproj, dsmall_gla, d_w2f, d_w2b, d_b2f, d_b2b = _gla_prep_bwd(proj, w2f, w2b, b2f, b2b, dgk_f, dgk_b,
                                                                 (dq_f, dq_b, dk_f, dk_b, dv_f, dv_b), dproj)

    do_a, dproj, d_gdn_norm = _out_norm_bwd(dog, oa_f, oa_b, proj, Z_BLK, gdn_norm_w, GDN_DIM, "gdn_out_bwd", dproj)
    dqkv_f, dqkv_b, dg_f, dg_b = _gdn_bwd(qkvn, gates, gc, sa_f, sa_b, inv_f, inv_b, do_a)
    dproj, d_conv = _gdn_prep_bwd(proj, conv_w, dqkv_f, dqkv_b, dproj)
    dproj, d_alog, d_dtb = _gdn_gates_bwd(proj, avec, dvec, dg_f, dg_b, dsmall_gla, dproj)

    d_wcat = _matmul(h_t, dproj, "nn", MXU_DTYPE, "dw_in", 1024, 512, 2048, out_t=True)
    started = None if send_in_grads is None else send_in_grads(d_wcat)
    dh = _matmul(dproj, wcat, "nn", f32, "dh", 512, 1024, N_CAT, after=started)
    grad_x, d_wpre = _rms_pre_bwd(x, w_pre, dh, dy)
    return dict(loss_row=loss_row, grad_x=grad_x, d_wcat=d_wcat, d_conv=d_conv, d_alog=d_alog, d_dtb=d_dtb,
                d_gdn_norm=d_gdn_norm, d_wpg=d_wpg, d_w2f=d_w2f, d_w2b=d_w2b, d_b2f=d_b2f, d_b2b=d_b2b,
                d_gla_norm=d_gla_norm, d_wpl=d_wpl, d_wout=d_wout, d_wpost=d_wpost, d_wpre=d_wpre)


def _to_cat(wf):
    pad = jnp.zeros((N_CAT - N_IN,) + wf.shape[1:], wf.dtype)
    return jnp.concatenate([wf[:4096], wf[4128:6176], wf[7232:9280], wf[6176:7200], wf[4096:4128], wf[7200:7232], pad], axis=0)


def _from_cat(wc):
    return jnp.concatenate([wc[:4096], wc[9216:9248], wc[4096:6144], wc[8192:9216], wc[9248:9280], wc[6144:8192]], axis=0)


def _lane_row(*pieces):
    row = jnp.concatenate(pieces, axis=1)
    return jnp.pad(row, ((0, 0), (0, LANES - row.shape[1])))


def _pad_w2(w2, row0):
    return jnp.pad(w2, ((row0, LANES - row0 - w2.shape[0]), (0, 0)))


SM_ROWS, SM_COLS = 32, 384


def _pack_small_shard(conv, w2f, w2b):
    lead = conv.shape[:-2]
    z = lambda r, c: jnp.zeros(lead + (r, c), f32)
    top = jnp.concatenate([conv, z(8 - CONV_K, SM_COLS)], axis=-2)
    mid = jnp.concatenate([w2f, w2b, z(16, SM_COLS - 128)], axis=-1)
    return jnp.concatenate([top, mid, z(SM_ROWS - 24, SM_COLS)], axis=-2)


SMALL_REGION = {
    "ln_pre_w": (0, 1, 0, 1024), "ln_post_w": (1, 1, 0, 1024), "gk_b2_fwd": (2, 1, 0, 512), "gk_b2_bwd": (2, 1, 512, 512),
    "gla_norm_w": (3, 1, 0, 256), "gdn_norm_w": (3, 1, 256, 128), "a_log_fwd": (4, 1, 0, 8), "a_log_bwd": (5, 1, 0, 8),
    "dt_bias_fwd": (6, 1, 0, 8), "dt_bias_bwd": (7, 1, 0, 8), "conv_w": (8, CONV_K, 0, SM_COLS),
    "gk_w2_fwd": (16, 16, 0, 64), "gk_w2_bwd": (16, 16, 128, 64),
}
SMALL_NAMES = tuple(SMALL_REGION)
SMALL_SHARDED = ("conv_w", "gk_w2_fwd", "gk_w2_bwd")
PACK_ROWS, PACK_COLS = 32, 1024
LOSS_AT = (13, 0)


def _region(name):
    r0, nr, l0, nl = SMALL_REGION[name]
    return slice(r0, r0 + nr), slice(l0, l0 + nl)


def _pack_small_grads(G):
    names = ("d_wpre", "d_wpost", "d_b2f", "d_b2b", "d_gla_norm", "d_gdn_norm", "d_alog", "d_dtb", "loss_row", "d_conv", "d_w2f", "d_w2b")

    def body(wpre, wpost, b2f, b2b, glan, gdnn, alog, dtb, lrow, conv, w2f, w2b, o_ref):
        o_ref[...] = jnp.zeros(o_ref.shape, f32)
        loss = 0.5 * jnp.sum(lrow[...], axis=1, keepdims=True) * (1.0 / D_MODEL)
        whole = {"ln_pre_w": wpre, "ln_post_w": wpost, "gk_b2_fwd": b2f, "gk_b2_bwd": b2b, "gla_norm_w": glan, "gdn_norm_w": gdnn}
        for j in range(N_DEV):
            for name, ref in whole.items():
                o_ref[(j,) + _region(name)] = ref[...]
            o_ref[(j,) + _region("a_log_fwd")] = alog[0:1, 0:8]
            o_ref[(j,) + _region("a_log_bwd")] = alog[0:1, 8:16]
            o_ref[(j,) + _region("dt_bias_fwd")] = dtb[0:1, 0:8]
            o_ref[(j,) + _region("dt_bias_bwd")] = dtb[0:1, 8:16]
            o_ref[(j,) + _region("conv_w")] = conv[:, j * SM_COLS : (j + 1) * SM_COLS]
            o_ref[(j,) + _region("gk_w2_fwd")] = w2f[32:48, j * 64 : (j + 1) * 64]
            o_ref[(j,) + _region("gk_w2_bwd")] = w2b[48:64, j * 64 : (j + 1) * 64]
            o_ref[j, LOSS_AT[0] : LOSS_AT[0] + 1, LOSS_AT[1] : LOSS_AT[1] + 1] = loss

    return pl.pallas_call(
        body, out_shape=jax.ShapeDtypeStruct((N_DEV, PACK_ROWS, PACK_COLS), f32), name="pack_small_grads",
    )(*[G[n] for n in names])


def _small_adam(land, own, W, M, V):
    bc1 = 1.0 - ADAM_B1**ADAM_STEP
    bc2 = 1.0 - ADAM_B2**ADAM_STEP
    n = len(SMALL_NAMES)

    def body(land_ref, own_ref, *refs):
        w_refs, m_refs, v_refs = refs[0:n], refs[n : 2 * n], refs[2 * n : 3 * n]
        outs, loss_ref, g_scr = refs[3 * n : 7 * n], refs[7 * n], refs[7 * n + 1]
        me = _my_device()
        g = jnp.where(me == 0, own_ref[0], land_ref[0])
        for j in range(1, N_DEV):
            g = g + jnp.where(me == j, own_ref[j], land_ref[j])
        g_scr[...] = g
        loss_ref[...] = g_scr[LOSS_AT[0] : LOSS_AT[0] + 1, LOSS_AT[1] : LOSS_AT[1] + 1]
        for i, name in enumerate(SMALL_NAMES):
            gp = g_scr[_region(name)]
            lead = (0,) if name in SMALL_SHARDED else (Ellipsis,)
            w, m, v = w_refs[i][lead], m_refs[i][lead], v_refs[i][lead]
            m2 = ADAM_B1 * m + (1.0 - ADAM_B1) * gp
            v2 = ADAM_B2 * v + (1.0 - ADAM_B2) * (gp * gp)
            delta = -ADAM_LR * ((m2 / bc1) / (jnp.sqrt(v2 / bc2) + ADAM_EPS) + ADAM_WD * w)
            for k, val in enumerate((gp, delta, m2, v2)):
                outs[4 * i + k][lead] = val

    shapes = [jax.ShapeDtypeStruct(W[name].shape, f32) for name in SMALL_NAMES for _ in range(4)]
    res = pl.pallas_call(
        body,
        out_shape=tuple(shapes) + (jax.ShapeDtypeStruct((1, 1), f32),),
        scratch_shapes=[pltpu.VMEM((PACK_ROWS, PACK_COLS), f32)],
        name="adam_small",
    )(land, own, *[P[name] for P in (W, M, V) for name in SMALL_NAMES])
    return {name: res[4 * i : 4 * i + 4] for i, name in enumerate(SMALL_NAMES)}, res[4 * n]


_MESH = pl.DeviceIdType.MESH


N_BIG_COPIES, N_SMALL_COPIES = 9, 7


def _gather_weights(w_in_s, sm_s):
    shapes = (w_in_s.shape, sm_s.shape)
    dtypes = (MXU_DTYPE, f32)

    def body(win_ref, sm_ref, gin_ref, gsm_ref, send_sems, recv_sems):
        x, y, c = lax.axis_index("x"), lax.axis_index("y"), lax.axis_index("c")
        me, sibling = (x, y, c), (x, y, 1 - c)
        chips = [(1 - x, y), (x, 1 - y), (1 - x, 1 - y)]

        def idx(px, py, pc):
            return 4 * px + 2 * py + pc

        for r in range(w_in_s.shape[1] // LANES):
            cols = slice(r * LANES, (r + 1) * LANES)
            gin_ref[idx(*me), :, cols] = win_ref[:, cols].astype(MXU_DTYPE)
        gsm_ref[idx(*me)] = sm_ref[...]

        def copy(sem, blk, to):
            return pltpu.make_async_remote_copy(src_ref=blk, dst_ref=blk, send_sem=send_sems.at[sem], recv_sem=recv_sems.at[sem],
                                                device_id=to, device_id_type=_MESH)

        def small(k, block, to):
            return copy(N_BIG_COPIES + k, gsm_ref.at[idx(*block)], to)

        half = w_in_s.shape[1] // 2
        xn, yn, dg = (1 - x, y), (x, 1 - y), (1 - x, 1 - y)

        def big(sem, block, to, part=None):
            blk = gin_ref.at[idx(*block)]
            return copy(sem, blk if part is None else blk.at[:, pl.ds(part * half, half)], to)

        started = [big(0, me, sibling), big(1, me, (*xn, c)), big(2, me, (*yn, c)), small(0, me, sibling)]
        started += [small(1 + j, me, (*chip, c)) for j, chip in enumerate(chips)]
        for cp in started:
            cp.start()

        def then(arrived, forwards):
            arrived.wait_recv()
            for cp in forwards:
                cp.start()
            started.extend(forwards)

        then(big(1, (*xn, c), me), [big(3, (*xn, c), (*yn, c), 1), big(5, (*xn, c), sibling)])
        then(big(2, (*yn, c), me), [big(4, (*yn, c), (*xn, c), 0), big(6, (*yn, c), sibling)])
        then(big(4, (*dg, c), me, 0), [big(7, (*dg, c), sibling, 0)])
        then(big(3, (*dg, c), me, 1), [big(8, (*dg, c), sibling, 1)])
        for j, chip in enumerate(chips):
            then(small(1 + j, (*chip, c), me), [small(4 + j, (*chip, c), sibling)])
        big(0, sibling, me).wait_recv()
        big(5, (*xn, 1 - c), me).wait_recv()
        big(6, (*yn, 1 - c), me).wait_recv()
        big(7, (*dg, 1 - c), me, 0).wait_recv()
        big(8, (*dg, 1 - c), me, 1).wait_recv()
        small(0, sibling, me).wait_recv()
        for j, chip in enumerate(chips):
            small(4 + j, (*chip, 1 - c), me).wait_recv()
        for cp in started:
            cp.wait_send()

    vm = pl.BlockSpec(memory_space=pltpu.VMEM)
    return pl.pallas_call(
        body,
        out_shape=tuple(jax.ShapeDtypeStruct((N_DEV,) + s, d) for s, d in zip(shapes, dtypes)),
        in_specs=[vm, vm],
        out_specs=(vm, vm),
        scratch_shapes=[pltpu.SemaphoreType.DMA((N_BIG_COPIES + N_SMALL_COPIES,)), pltpu.SemaphoreType.DMA((N_BIG_COPIES + N_SMALL_COPIES,))],
        compiler_params=pltpu.CompilerParams(vmem_limit_bytes=VMEM_LIMIT),
        name="gather_weights",
    )(w_in_s, sm_s)


N_CHIP = 4
_EFFECT = pltpu.SideEffectType.DATAFLOW_SIDE_EFFECTING
_HBM_SPEC = pl.BlockSpec(memory_space=pltpu.HBM)
_SEM_SPEC = pl.BlockSpec(memory_space=pltpu.SEMAPHORE)


def _relation_peer(k, x, y, c):
    px = (1 - x) if (k & 4) else x
    py = (1 - y) if (k & 2) else y
    pc = (1 - c) if (k & 1) else c
    return (px, py, pc), 4 * px + 2 * py + pc


_PLAN_COPIES = {"gather": N_DEV - 1, "blocks": N_DEV - 1, "sibling": 4, "chips": 3}
_PLAN_SLOTS = {"gather": N_DEV, "blocks": N_DEV, "sibling": 4, "chips": 4}


def _plan_copies(plan, src_ref, land_ref, send_sems, recv_sems, arrival):
    x, y, c = lax.axis_index("x"), lax.axis_index("y"), lax.axis_index("c")
    me, my_chip = 4 * x + 2 * y + c, 2 * x + y
    out = []

    def add(i, src, there, here, dev):
        out.append(pltpu.make_async_remote_copy(src_ref=src, dst_ref=land_ref.at[here if arrival else there], send_sem=send_sems.at[i],
                                                recv_sem=recv_sems.at[i], device_id=dev, device_id_type=_MESH))

    if plan in ("gather", "blocks"):
        for k in range(1, N_DEV):
            dev, p = _relation_peer(k, x, y, c)
            add(k - 1, src_ref.at[p] if plan == "blocks" else src_ref, me, p, dev)
    elif plan == "sibling":
        for j in range(4):
            add(j, src_ref.at[2 * j + 1 - c], j, j, (x, y, 1 - c))
    else:
        for i, k in enumerate((2, 4, 6)):
            (px, py, pc), _ = _relation_peer(k, x, y, c)
            add(i, src_ref.at[2 * px + py], my_chip, 2 * px + py, (px, py, pc))
    return out


def _exchange_start(src, plan, name, after=None):
    n = _PLAN_COPIES[plan]
    land = lax.empty((_PLAN_SLOTS[plan],) + (src.shape if plan == "gather" else src.shape[1:]), src.dtype)
    extra = [] if after is None else [after]

    def body(src_ref, land_ref, *rest):
        send_sems, recv_sems, _, _, token = rest[len(extra) :]
        for cp in _plan_copies(plan, src_ref, land_ref, send_sems, recv_sems, False):
            cp.start()
        token[...] = jnp.zeros_like(token)

    return pl.pallas_call(
        body,
        name=name,
        out_shape=(pltpu.SemaphoreType.DMA((n,)), pltpu.SemaphoreType.DMA((n,)), pltpu.HBM(src.shape, src.dtype),
                   pltpu.HBM(land.shape, land.dtype), jax.ShapeDtypeStruct((8, LANES), f32)),
        in_specs=(_HBM_SPEC, _HBM_SPEC) + (_ANY_SPEC,) * len(extra),
        out_specs=(_SEM_SPEC, _SEM_SPEC, _HBM_SPEC, _HBM_SPEC, pl.BlockSpec(memory_space=pltpu.VMEM)),
        input_output_aliases={0: 2, 1: 3},
        compiler_params=pltpu.CompilerParams(has_side_effects=_EFFECT),
    )(pltpu.with_memory_space_constraint(src, pltpu.HBM), pltpu.with_memory_space_constraint(land, pltpu.HBM), *extra)


def _exchange_wait(started, after, plan, name):
    send_sems, recv_sems, src_thru, land_thru, _ = started

    def body(src_ref, land_ref, send_sems, recv_sems, after_ref, src_dead, got_ref):
        for cp in _plan_copies(plan, src_ref, land_ref, send_sems, recv_sems, True):
            cp.wait_send()
            cp.wait_recv()

    return pl.pallas_call(
        body,
        name=name,
        out_shape=(pltpu.HBM(src_thru.shape, src_thru.dtype), pltpu.HBM(land_thru.shape, land_thru.dtype)),
        in_specs=(_HBM_SPEC, _HBM_SPEC, _SEM_SPEC, _SEM_SPEC, pl.BlockSpec(memory_space=pl.ANY)),
        out_specs=(_HBM_SPEC, _HBM_SPEC),
        input_output_aliases={0: 0, 1: 1},
        compiler_params=pltpu.CompilerParams(has_side_effects=_EFFECT),
    )(src_thru, land_thru, send_sems, recv_sems, after)


def _pair_sum(g, r, name, tc=LANES):
    _, R, Cc = g.shape

    def body(g_ref, r_ref, o_ref):
        o_ref[...] = (g_ref[...].astype(f32) + r_ref[...].astype(f32)).astype(o_ref.dtype)

    return pl.pallas_call(
        body,
        out_shape=jax.ShapeDtypeStruct(r.shape, r.dtype),
        grid=(N_CHIP, Cc // tc),
        in_specs=[pl.BlockSpec((1, R, tc), lambda k, i: (2 * k + lax.axis_index("c"), 0, i)), pl.BlockSpec((1, R, tc), lambda k, i: (k, 0, i))],
        out_specs=pl.BlockSpec((1, R, tc), lambda k, i: (k, 0, i)),
        compiler_params=_cparams(("parallel", "parallel")),
        name=name,
    )(g, r)


def _sum_adam(parts, w, m, v, name, tb, tc=None, own=None, own_slot=None):
    R, Cc = w.shape
    n_parts = parts.shape[0]
    tb = R if tc else min(tb, R)
    tc = tc or Cc
    assert R % tb == 0 and Cc % tc == 0
    bc1 = 1.0 - ADAM_B1**ADAM_STEP
    bc2 = 1.0 - ADAM_B2**ADAM_STEP
    extra = [] if own is None else [own]

    def body(p_ref, w_ref, m_ref, v_ref, *rest):
        g_ref, d_ref, nm_ref, nv_ref = rest[len(extra) :]

        def part(j):
            pj = p_ref[j].astype(f32)
            return pj if own is None else jnp.where(own_slot() == j, rest[0][0].astype(f32), pj)

        g = part(0)
        for j in range(1, n_parts):
            g = g + part(j)
        g_ref[...] = g
        m2 = ADAM_B1 * m_ref[...] + (1.0 - ADAM_B1) * g
        v2 = ADAM_B2 * v_ref[...] + (1.0 - ADAM_B2) * (g * g)
        nm_ref[...] = m2
        nv_ref[...] = v2
        d_ref[...] = -ADAM_LR * ((m2 / bc1) / (jnp.sqrt(v2 / bc2) + ADAM_EPS) + ADAM_WD * w_ref[...])

    blk = pl.BlockSpec((tb, tc), lambda i, j: (i, j))
    o = jax.ShapeDtypeStruct((R, Cc), f32)
    return pl.pallas_call(
        body,
        out_shape=(o, o, o, o),
        grid=(R // tb, Cc // tc),
        in_specs=[pl.BlockSpec((n_parts, tb, tc), lambda i, j: (0, i, j)), blk, blk, blk]
        + [pl.BlockSpec((1, tb, tc), lambda i, j: (own_slot(), i, j))] * len(extra),
        out_specs=(blk, blk, blk, blk),
        compiler_params=_cparams(("parallel", "parallel")),
        name=name,
    )(parts, w, m, v, *extra)


def _my_device():
    return 4 * lax.axis_index("x") + 2 * lax.axis_index("y") + lax.axis_index("c")


def _my_chip():
    return 2 * lax.axis_index("x") + lax.axis_index("y")


_WEIGHTS = ["ln_pre_w", "w_in", "conv_w", "a_log_fwd", "a_log_bwd", "dt_bias_fwd", "dt_bias_bwd", "gdn_norm_w", "w_proj_gdn",
            "gk_w2_fwd", "gk_b2_fwd", "gk_w2_bwd", "gk_b2_bwd", "gla_norm_w", "w_proj_gla", "w_out", "ln_post_w"]


def kernel(x, ln_pre_w, w_in, conv_w, a_log_fwd, a_log_bwd, dt_bias_fwd, dt_bias_bwd, gdn_norm_w, w_proj_gdn, gk_w2_fwd, gk_b2_fwd, gk_w2_bwd, gk_b2_bwd, gla_norm_w, w_proj_gla, w_out, ln_post_w, loss_target, m_ln_pre_w, m_w_in, m_conv_w, m_a_log_fwd, m_a_log_bwd, m_dt_bias_fwd, m_dt_bias_bwd, m_gdn_norm_w, m_w_proj_gdn, m_gk_w2_fwd, m_gk_b2_fwd, m_gk_w2_bwd, m_gk_b2_bwd, m_gla_norm_w, m_w_proj_gla, m_w_out, m_ln_post_w, v_ln_pre_w, v_w_in, v_conv_w, v_a_log_fwd, v_a_log_bwd, v_dt_bias_fwd, v_dt_bias_bwd, v_gdn_norm_w, v_w_proj_gdn, v_gk_w2_fwd, v_gk_b2_fwd, v_gk_w2_bwd, v_gk_b2_bwd, v_gla_norm_w, v_w_proj_gla, v_w_out, v_ln_post_w):
    args = locals()
    W = {n: args[n] for n in _WEIGHTS}
    M = {n: args["m_" + n] for n in _WEIGHTS}
    V = {n: args["v_" + n] for n in _WEIGHTS}

    wp_stack = lambda P: jnp.concatenate([P["w_proj_gdn"], P["w_proj_gla"], P["w_out"]], axis=0)
    sm_pack = lambda P: _pack_small_shard(P["conv_w"][0], P["gk_w2_fwd"][0], P["gk_w2_bwd"][0])
    me = 4 * lax.axis_index("x") + 2 * lax.axis_index("y") + lax.axis_index("c")
    w_in_t, m_in_t, v_in_t = (jnp.transpose(a[0]) for a in (w_in, m_w_in, v_w_in))
    g_in, g_sm = _gather_weights(w_in_t, sm_pack(W))
    wp_own = wp_stack(W).astype(MXU_DTYPE)
    wp_started = _exchange_start(wp_own, "gather", "gather_proj_start", after=g_sm)

    def proj_weights(after):
        own, land = _exchange_wait(wp_started, after, "gather", "gather_proj_wait")
        g_p = lax.dynamic_update_slice(land, own[None], (me, 0, 0, 0))
        return tuple(g_p[:, i].reshape(D_MODEL, D_MODEL) for i in range(3))

    rows_p = 3 * D_MODEL // N_DEV
    sent = {}

    def send_proj_grads(d_wpg, d_wpl, d_wout):
        p_p = jnp.stack([d_wpg, d_wpl, d_wout]).reshape(3, N_DEV, D_MODEL // N_DEV, D_MODEL).transpose(1, 0, 2, 3)
        sent["p_p"] = p_p.reshape(N_DEV, rows_p, D_MODEL)
        sent["started"] = _exchange_start(sent["p_p"], "blocks", "exchange_proj_start")
        return sent["started"][4][0:1, 0:1]

    def send_in_grads(d_wcat):
        p_in = _from_cat(d_wcat).reshape(N_DEV, SHARD_IN, D_MODEL)
        sib = _exchange_start(p_in, "sibling", "exchange_in_sibling_start")
        p_p, land_p = _exchange_wait(sent["started"], sib[4], "blocks", "exchange_proj_wait")
        sent["o_p"] = _sum_adam(land_p, wp_stack(W).reshape(rows_p, D_MODEL), wp_stack(M).reshape(rows_p, D_MODEL),
                                wp_stack(V).reshape(rows_p, D_MODEL), "adam_w_proj", 128, own=p_p, own_slot=_my_device)
        p_in, sib_land = _exchange_wait(sib, sent["o_p"][0], "sibling", "exchange_in_sibling_wait")
        sent["chips"] = _exchange_start(_pair_sum(p_in, sib_land, "pair_sum_w_in", tc=512), "chips", "exchange_in_chips_start")
        return sent["chips"][4]

    wcat = _to_cat(g_in.reshape(N_IN, D_MODEL))
    conv_full = g_sm[:, 0:CONV_K, :].transpose(1, 0, 2).reshape(CONV_K, N_DEV * SM_COLS)
    w2f_full = g_sm[:, 8:24, 0:64].transpose(1, 0, 2).reshape(16, 512)
    w2b_full = g_sm[:, 8:24, 64:128].transpose(1, 0, 2).reshape(16, 512)

    avec = _lane_row(a_log_fwd, a_log_bwd)
    dvec = _lane_row(dt_bias_fwd, dt_bias_bwd)
    G = _local_step(x[0], loss_target[0], wcat, conv_full, avec, dvec, gdn_norm_w, _pad_w2(w2f_full, 32), _pad_w2(w2b_full, 48),
                    gk_b2_fwd, gk_b2_bwd, gla_norm_w, proj_weights, ln_pre_w + wp_started[4][0:1, 0:1], ln_post_w, send_proj_grads,
                    send_in_grads)

    small_started = _exchange_start(_pack_small_grads(G), "blocks", "exchange_small_start")
    q_in, land_in = _exchange_wait(sent["chips"], small_started[4], "chips", "exchange_in_chips_wait")
    o_in = _sum_adam(land_in, w_in_t, m_in_t, v_in_t, "adam_w_in", SHARD_IN, tc=256, own=q_in, own_slot=_my_chip)
    o_p = sent["o_p"]
    p_small, land_small = _exchange_wait(small_started, o_in[0], "blocks", "exchange_small_wait")
    o_small, loss = _small_adam(land_small, p_small, W, M, V)

    res = []
    for kind in range(4):
        per = {name: o_small[name][kind] for name in SMALL_NAMES}
        per["w_in"] = jnp.transpose(o_in[kind])[None]
        pp = o_p[kind].reshape(3, 1, D_MODEL // N_DEV, D_MODEL)
        per["w_proj_gdn"], per["w_proj_gla"], per["w_out"] = pp[0], pp[1], pp[2]
        res.append([per[n] for n in _WEIGHTS])
    return (loss[0, 0], G["grad_x"][None], *res[0], *res[1], *res[2], *res[3])
```

```python
import jax
import jax.numpy as jnp
from jax import lax
from jax.experimental import pallas as pl
from jax.experimental.pallas import tpu as pltpu

f32 = jnp.float32
MXU_DTYPE = jnp.bfloat16

D_MODEL = 1024
CHUNK = 64
NORM_EPS = 1e-6
GDN_HEADS, GDN_DIM = 8, 128
GLA_HEADS, GLA_DK, GLA_DV = 4, 128, 256
GATE_NORMALIZER = 16.0
CONV_K = 5
N_IN = 9280
N_DEV = 8
SHARD_IN = N_IN // N_DEV

N_CAT = 9728
SMALL_OFF = 9216
LANES = 128

ADAM_LR, ADAM_B1, ADAM_B2, ADAM_EPS, ADAM_WD, ADAM_STEP = 0.001, 0.9, 0.999, 1e-08, 0.01, 10

VMEM_LIMIT = 56 * 1024 * 1024


def _cparams(sem=None):
    return pltpu.CompilerParams(dimension_semantics=sem, vmem_limit_bytes=VMEM_LIMIT)


_DN = {"nn": (((1,), (0,)), ((), ())), "nt": (((1,), (1,)), ((), ())), "tn": (((0,), (0,)), ((), ()))}
_BDN = {"nn": (((2,), (1,)), ((0,), (0,))), "nt": (((2,), (2,)), ((0,), (0,))), "tn": (((1,), (1,)), ((0,), (0,)))}


def _dot16(a, b, kind):
    return lax.dot_general(a, b, (_DN if a.ndim == 2 else _BDN)[kind], preferred_element_type=f32)


def _mx(a, b, kind):
    return _dot16(a.astype(MXU_DTYPE), b.astype(MXU_DTYPE), kind)


def _split2(a):
    hi = a.astype(jnp.bfloat16)
    return hi, (a - hi.astype(f32)).astype(jnp.bfloat16)


def _split3(a):
    p1 = a.astype(jnp.bfloat16)
    r = a - p1.astype(f32)
    p2 = r.astype(jnp.bfloat16)
    return p1, p2, (r - p2.astype(f32)).astype(jnp.bfloat16)


def _h3s(a2, b2, kind):
    (ah, al), (bh, bl) = a2, b2
    return _dot16(ah, bh, kind) + (_dot16(ah, bl, kind) + _dot16(al, bh, kind))


def _h3(a, b, kind):
    return _h3s(_split2(a), _split2(b), kind)


def _exact01(t01, x, kind="nn"):
    t = t01.astype(jnp.bfloat16)
    x1, x2, x3 = _split3(x)
    return _dot16(t, x1, kind) + (_dot16(t, x2, kind) + _dot16(t, x3, kind))


def _colsum_col(e):
    ones = jnp.ones(e.shape[:-1] + (LANES,), jnp.bfloat16)
    e1, e2, e3 = _split3(e)
    return (_dot16(e1, ones, "tn") + (_dot16(e2, ones, "tn") + _dot16(e3, ones, "tn")))[..., 0:1]


def _iota3(shape, axis):
    return lax.broadcasted_iota(jnp.int32, shape, axis)


def _dir_masks(B, nf, Cn):
    shp = (B, Cn, Cn)
    bb, ii, jj = _iota3(shp, 0), _iota3(shp, 1), _iota3(shp, 2)
    fwd = bb < nf
    rev = jnp.logical_not(fwd)
    incl = jnp.logical_or(jnp.logical_and(fwd, ii >= jj), jnp.logical_and(rev, ii <= jj))
    strict = jnp.logical_and(incl, ii != jj)
    return incl, strict, ii == jj


def _last_row(x, nf):
    B, Cn = x.shape[0], x.shape[1]
    fwd = _iota3((B, 1, 1), 0) < nf
    return jnp.where(fwd, x[:, Cn - 1 : Cn, :], x[:, 0:1, :])


def _add_at_last_row(x, val, nf):
    B, Cn = x.shape[0], x.shape[1]
    bb, rr = _iota3((B, Cn, 1), 0), _iota3((B, Cn, 1), 1)
    at = jnp.logical_or(jnp.logical_and(bb < nf, rr == Cn - 1), jnp.logical_and(bb >= nf, rr == 0))
    return x + jnp.where(at, val, 0.0)


def _sigmoid(x):
    return jax.nn.sigmoid(x)


def _silu(x):
    return x * _sigmoid(x)


def _silu_grad(x):
    s = _sigmoid(x)
    return s * (1.0 + x * (1.0 - s))


def _softplus(x):
    u = jnp.exp(-jnp.abs(x))
    l1p = jnp.where(u < 1e-3, u * (1.0 - u * (0.5 - u * (1.0 / 3.0))), jnp.log(1.0 + u))
    return jnp.maximum(x, 0.0) + l1p


def _mxr(x):
    return x.astype(MXU_DTYPE).astype(f32)


def _iota2(shape, axis):
    return lax.broadcasted_iota(jnp.int32, shape, axis)


def _matmul(a, b, kind, out_dtype, name, tm, tn, tk, out_t=False, after=None):
    extra = [] if after is None else [after]
    if kind == "nn":
        (M, K), N = a.shape, b.shape[1]
    elif kind == "nt":
        (M, K), N = a.shape, b.shape[0]
    else:
        (K, M), N = a.shape, b.shape[1]
    tm, tn, tk = min(tm, M), min(tn, N), min(tk, K)
    assert M % tm == 0 and N % tn == 0 and K % tk == 0, (name, M, N, K)
    nk = K // tk
    if kind == "tn":
        a_spec = pl.BlockSpec((tk, tm), lambda i, j, k: (k, i))
    else:
        a_spec = pl.BlockSpec((tm, tk), lambda i, j, k: (i, k))
    b_mode = dict(pipeline_mode=pl.Buffered(1)) if (tn == N and tk == K) else {}
    if kind == "nt":
        b_spec = pl.BlockSpec((tn, tk), lambda i, j, k: (j, k), **b_mode)
    else:
        b_spec = pl.BlockSpec((tk, tn), lambda i, j, k: (k, j), **b_mode)

    def body(a_ref, b_ref, *rest):
        o_ref, acc = rest[len(extra)], rest[len(extra) + 1 :]
        p = _mx(a_ref[...], b_ref[...], kind)
        if nk == 1:
            o_ref[...] = (p.T if out_t else p).astype(out_dtype)
        else:
            assert not out_t
            acc_ref = acc[0]
            k = pl.program_id(2)

            @pl.when(k == 0)
            def _():
                acc_ref[...] = p

            @pl.when(k > 0)
            def _():
                acc_ref[...] += p

            @pl.when(k == nk - 1)
            def _():
                o_ref[...] = acc_ref[...].astype(out_dtype)

    return pl.pallas_call(
        body,
        out_shape=jax.ShapeDtypeStruct((N, M) if out_t else (M, N), out_dtype),
        grid=(M // tm, N // tn, nk),
        in_specs=[a_spec, b_spec] + [pl.BlockSpec((8, LANES), lambda i, j, k: (0, 0))] * len(extra),
        out_specs=pl.BlockSpec((tn, tm), lambda i, j, k: (j, i)) if out_t else pl.BlockSpec((tm, tn), lambda i, j, k: (i, j)),
        scratch_shapes=[] if nk == 1 else [pltpu.VMEM((tm, tn), f32)],
        compiler_params=_cparams(("parallel", "parallel", "arbitrary")),
        name=name,
    )(a, b, *extra)


ROW_TILE = 512


def _rms_pre(x, w):
    T = x.shape[0]
    tb = min(ROW_TILE, T)

    def body(x_ref, w_ref, h_ref, ht_ref):
        xv = x_ref[...]
        r = lax.rsqrt(jnp.mean(xv * xv, axis=1, keepdims=True) + NORM_EPS)
        h = xv * r * w_ref[...]
        h_ref[...] = h.astype(h_ref.dtype)
        ht_ref[...] = h.T.astype(ht_ref.dtype)

    return pl.pallas_call(
        body,
        out_shape=(jax.ShapeDtypeStruct((T, D_MODEL), MXU_DTYPE), jax.ShapeDtypeStruct((D_MODEL, T), MXU_DTYPE)),
        grid=(T // tb,),
        in_specs=[pl.BlockSpec((tb, D_MODEL), lambda i: (i, 0)), pl.BlockSpec((1, D_MODEL), lambda i: (0, 0))],
        out_specs=(pl.BlockSpec((tb, D_MODEL), lambda i: (i, 0)), pl.BlockSpec((D_MODEL, tb), lambda i: (0, i))),
        compiler_params=_cparams(("parallel",)),
        name="rms_pre",
    )(x, w)


def _rms_pre_bwd(x, w, dh, dy):
    T = x.shape[0]
    tb = min(ROW_TILE, T)

    def body(x_ref, w_ref, dh_ref, dy_ref, dx_ref, dw_ref):
        i = pl.program_id(0)
        xv, dhv = x_ref[...], dh_ref[...]
        r = lax.rsqrt(jnp.mean(xv * xv, axis=1, keepdims=True) + NORM_EPS)
        dhw = dhv * w_ref[...]
        dx_ref[...] = dy_ref[...] + r * dhw - xv * (r * r * r) * jnp.mean(dhw * xv, axis=1, keepdims=True)
        part = jnp.sum(dhv * xv * r, axis=0, keepdims=True)

        @pl.when(i == 0)
        def _():
            dw_ref[...] = part

        @pl.when(i > 0)
        def _():
            dw_ref[...] += part

    blk = pl.BlockSpec((tb, D_MODEL), lambda i: (i, 0))
    row = pl.BlockSpec((1, D_MODEL), lambda i: (0, 0))
    return pl.pallas_call(
        body,
        out_shape=(jax.ShapeDtypeStruct((T, D_MODEL), f32), jax.ShapeDtypeStruct((1, D_MODEL), f32)),
        grid=(T // tb,),
        in_specs=[blk, row, blk, blk],
        out_specs=(blk, row),
        compiler_params=_cparams(("arbitrary",)),
        name="rms_pre_bwd",
    )(x, w, dh, dy)


SHIFTS = (-2, -1, 1, 2)


SHIFT_PAD = 8


def _shifted(x, pad_ref):
    T, cb = x.shape
    pad_ref[0:SHIFT_PAD] = jnp.zeros((SHIFT_PAD, cb), f32)
    pad_ref[SHIFT_PAD + T :] = jnp.zeros((SHIFT_PAD, cb), f32)
    pad_ref[SHIFT_PAD : SHIFT_PAD + T] = x
    return {d: pad_ref[SHIFT_PAD + d : SHIFT_PAD + d + T] for d in SHIFTS}


def _conv5(u, ush, cw):
    acc = u * cw[2:3, :]
    for j in (0, 1, 3, 4):
        acc = acc + ush[j - 2] * cw[j : j + 1, :]
    return acc


def _gdn_prep(proj, conv_w):
    T = proj.shape[0]
    nh = GDN_HEADS

    def body(u_ref, cw_ref, o_ref, pad_ref):
        j = pl.program_id(0)
        u = _mxr(u_ref[...])
        s = _silu(_conv5(u, _shifted(u, pad_ref), _mxr(cw_ref[...])))

        @pl.when(j < 2 * nh)
        def _():
            o_ref[0] = s * lax.rsqrt(jnp.sum(s * s, axis=1, keepdims=True) + NORM_EPS)

        @pl.when(j >= 2 * nh)
        def _():
            o_ref[0] = s

    return pl.pallas_call(
        body,
        out_shape=jax.ShapeDtypeStruct((3, T, GDN_HEADS * GDN_DIM), f32),
        grid=(3 * nh,),
        in_specs=[pl.BlockSpec((T, GDN_DIM), lambda j: (0, j)), pl.BlockSpec((CONV_K, GDN_DIM), lambda j: (0, j))],
        out_specs=pl.BlockSpec((1, T, GDN_DIM), lambda j: (j // nh, 0, j % nh)),
        scratch_shapes=[pltpu.VMEM((T + 2 * SHIFT_PAD, GDN_DIM), f32)],
        compiler_params=_cparams(("parallel",)),
        name="gdn_prep",
    )(proj, conv_w)


def _gdn_prep_bwd(proj, conv_w, dqkv_f, dqkv_b, dproj):
    T = proj.shape[0]
    nh = GDN_HEADS


    def body(u_ref, cw_ref, df_ref, db_ref, _, du_ref, dcw_ref, upad_ref, dcpad_ref):
        j = pl.program_id(0)
        u, cw = _mxr(u_ref[...]), _mxr(cw_ref[...])
        ush = _shifted(u, upad_ref)
        c = _conv5(u, ush, cw)
        s = _silu(c)
        dn = df_ref[0] + db_ref[0]
        rinv = lax.rsqrt(jnp.sum(s * s, axis=1, keepdims=True) + NORM_EPS)
        ds_norm = rinv * dn - s * (rinv * rinv * rinv) * jnp.sum(dn * s, axis=1, keepdims=True)
        ds = jnp.where(j < 2 * nh, ds_norm, dn)
        dc = _mxr(ds * _silu_grad(c))
        dcsh = _shifted(dc, dcpad_ref)
        du = dc * cw[2:3, :]
        for jj in range(CONV_K):
            d = jj - 2
            dcw_ref[jj : jj + 1, :] = jnp.sum(dc * (u if d == 0 else ush[d]), axis=0, keepdims=True)
            if d != 0:
                du = du + dcsh[-d] * cw[jj : jj + 1, :]
        du_ref[...] = du.astype(du_ref.dtype)

    qspec = pl.BlockSpec((1, T, GDN_DIM), lambda j: (j // nh, 0, j % nh))
    return pl.pallas_call(
        body,
        out_shape=(jax.ShapeDtypeStruct(dproj.shape, dproj.dtype), jax.ShapeDtypeStruct((CONV_K, 3 * nh * GDN_DIM), f32)),
        grid=(3 * nh,),
        in_specs=[pl.BlockSpec((T, GDN_DIM), lambda j: (0, j)), pl.BlockSpec((CONV_K, GDN_DIM), lambda j: (0, j)), qspec, qspec, _ANY_SPEC],
        out_specs=(pl.BlockSpec((T, GDN_DIM), lambda j: (0, j)), pl.BlockSpec((CONV_K, GDN_DIM), lambda j: (0, j))),
        input_output_aliases={4: 0},
        scratch_shapes=[pltpu.VMEM((T + 2 * SHIFT_PAD, GDN_DIM), f32)] * 2,
        compiler_params=_cparams(("parallel",)),
        name="gdn_prep_bwd",
    )(proj, conv_w, dqkv_f, dqkv_b, dproj)


def _chunk_tri(n, rev):
    i, j = _iota2((n, n), 0), _iota2((n, n), 1)
    same = jnp.right_shift(i, 6) == jnp.right_shift(j, 6)
    order = (j >= i) if rev else (j <= i)
    return jnp.where(jnp.logical_and(same, order), 1.0, 0.0).astype(f32)


def _gate_prep(proj, avec, dvec, w2f, w2b, b2f, b2b, tb=256):
    T = proj.shape[0]
    W = GLA_HEADS * GLA_DK

    def body(s_ref, a_ref, d_ref, wf_ref, wb_ref, bf_ref, bb_ref, g_ref, gc_ref, gf_ref, gb_ref):
        small = s_ref[...]
        tri_f, tri_r = _chunk_tri(tb, False), _chunk_tri(tb, True)
        lane = _iota2(small.shape, 1)
        lg = -jnp.exp(a_ref[...]) * _softplus(small + d_ref[...])
        beta = _sigmoid(small)
        g_ref[...] = jnp.where(lane < 16, lg, jnp.where(lane < 32, beta, 0.0))
        lgm = jnp.where(lane < 16, lg, 0.0)
        gc_ref[...] = jnp.where(lane < 8, _exact01(tri_f, lgm), _exact01(tri_r, lgm))
        gkf = -_softplus(-(_mx(small, wf_ref[...], "nn") + bf_ref[...])) * (1.0 / GATE_NORMALIZER)
        gkb = -_softplus(-(_mx(small, wb_ref[...], "nn") + bb_ref[...])) * (1.0 / GATE_NORMALIZER)
        gf_ref[...] = _exact01(tri_f, gkf)
        gb_ref[...] = _exact01(tri_r, gkb)

    blk = pl.BlockSpec((tb, LANES), lambda i: (i, 0))
    row = pl.BlockSpec((1, LANES), lambda i: (0, 0))
    wide = pl.BlockSpec((tb, W), lambda i: (i, 0))
    wsp = pl.BlockSpec((LANES, W), lambda i: (0, 0))
    wrow = pl.BlockSpec((1, W), lambda i: (0, 0))
    return pl.pallas_call(
        body,
        out_shape=(jax.ShapeDtypeStruct((T, LANES), f32), jax.ShapeDtypeStruct((T, LANES), f32),
                   jax.ShapeDtypeStruct((T, W), f32), jax.ShapeDtypeStruct((T, W), f32)),
        grid=(T // tb,),
        in_specs=[pl.BlockSpec((tb, LANES), lambda i: (i, SMALL_OFF // LANES)), row, row, wsp, wsp, wrow, wrow],
        out_specs=(blk, blk, wide, wide),
        compiler_params=_cparams(("parallel",)),
        name="gate_prep",
    )(proj, avec, dvec, w2f, w2b, b2f, b2b)


def _gdn_gates_bwd(proj, avec, dvec, dg_f, dg_b, dsmall_gla, dproj):
    T = proj.shape[0]
    tb = min(ROW_TILE, T)
    pad = N_CAT - SMALL_OFF

    def body(s_ref, a_ref, d_ref, gf_ref, gb_ref, dl_ref, _, ds_ref, da_ref, dd_ref):
        i = pl.program_id(0)
        small = s_ref[...]
        lane = _iota2(small.shape, 1)
        dgate = gf_ref[...] + gb_ref[...]
        z = small + d_ref[...]
        nega = -jnp.exp(a_ref[...])
        dz = dgate * nega * _sigmoid(z)
        beta = _sigmoid(small)
        dsm = jnp.where(lane < 16, dz, jnp.where(lane < 32, dgate * beta * (1.0 - beta), 0.0))
        ds_ref[:, :LANES] = (dsm + dl_ref[...]).astype(ds_ref.dtype)
        ds_ref[:, LANES:] = jnp.zeros((tb, pad - LANES), ds_ref.dtype)
        lg = nega * _softplus(z)
        pa = jnp.sum(jnp.where(lane < 16, dgate * lg, 0.0), axis=0, keepdims=True)
        pd = jnp.sum(jnp.where(lane < 16, dz, 0.0), axis=0, keepdims=True)

        @pl.when(i == 0)
        def _():
            da_ref[...] = pa
            dd_ref[...] = pd

        @pl.when(i > 0)
        def _():
            da_ref[...] += pa
            dd_ref[...] += pd

    blk = pl.BlockSpec((tb, LANES), lambda i: (i, 0))
    row = pl.BlockSpec((1, LANES), lambda i: (0, 0))
    return pl.pallas_call(
        body,
        out_shape=(jax.ShapeDtypeStruct(dproj.shape, dproj.dtype), jax.ShapeDtypeStruct((1, LANES), f32), jax.ShapeDtypeStruct((1, LANES), f32)),
        grid=(T // tb,),
        in_specs=[pl.BlockSpec((tb, LANES), lambda i: (i, SMALL_OFF // LANES)), row, row, blk, blk, blk, _ANY_SPEC],
        out_specs=(pl.BlockSpec((tb, pad), lambda i: (i, DP_SMALL_BLK)), row, row),
        input_output_aliases={6: 0},
        compiler_params=_cparams(("arbitrary",)),
        name="gdn_gates_bwd",
    )(proj, avec, dvec, dg_f, dg_b, dsmall_gla, dproj)


GDN_HB = 8


def _inv_unit_lower(L, eye):
    A = eye - L
    P2 = _split2(L)
    for _ in range(3):
        P2 = _split2(_h3s(P2, P2, "nn"))
        A = A + _h3s(_split2(A), P2, "nn")
    Ph = P2[0]
    for _ in range(2):
        Ph = _dot16(Ph, Ph, "nn").astype(jnp.bfloat16)
        A = A + _dot16(A.astype(jnp.bfloat16), Ph, "nn")
    return A


def _lane_col(blk, idx):
    lane = _iota2(blk.shape, 1)
    return jnp.sum(jnp.where(lane == idx, blk, 0.0), axis=1, keepdims=True)


def _out_norm(o_f, o_b, proj, gate_blk, w, hd, name):
    T, W = o_f.shape
    tb = min(ROW_TILE, T)
    nh = W // hd

    def body(of_ref, ob_ref, z_ref, w_ref, y_ref):
        wv = w_ref[...]
        for i in range(nh):
            sl = slice(i * hd, (i + 1) * hd)
            o = of_ref[:, sl] + ob_ref[:, sl]
            r = lax.rsqrt(jnp.mean(o * o, axis=1, keepdims=True) + NORM_EPS)
            y_ref[:, sl] = (o * r * wv * _silu(z_ref[:, sl])).astype(y_ref.dtype)

    blk = pl.BlockSpec((tb, W), lambda i: (i, 0))
    return pl.pallas_call(
        body,
        out_shape=jax.ShapeDtypeStruct((T, W), MXU_DTYPE),
        grid=(T // tb,),
        in_specs=[blk, blk, pl.BlockSpec((tb, W), lambda i: (i, gate_blk)), pl.BlockSpec((1, hd), lambda i: (0, 0))],
        out_specs=blk,
        compiler_params=_cparams(("parallel",)),
        name=name,
    )(o_f, o_b, proj, w)


def _out_norm_bwd(dy, o_f, o_b, proj, gate_blk, w, hd, name, dproj):
    T, W = o_f.shape
    tb = min(ROW_TILE, T)
    nh = W // hd

    def body(dy_ref, of_ref, ob_ref, z_ref, w_ref, _, do_ref, dz_ref, dw_ref):
        i = pl.program_id(0)
        wv = w_ref[...]
        dw = jnp.zeros((1, hd), f32)
        for a in range(nh):
            sl = slice(a * hd, (a + 1) * hd)
            o = of_ref[:, sl] + ob_ref[:, sl]
            z = z_ref[:, sl]
            dyv = dy_ref[:, sl]
            r = lax.rsqrt(jnp.mean(o * o, axis=1, keepdims=True) + NORM_EPS)
            orr = o * r
            dn = dyv * _silu(z)
            dz_ref[:, sl] = (dyv * orr * wv * _silu_grad(z)).astype(dz_ref.dtype)
            dw = dw + jnp.sum(dn * orr, axis=0, keepdims=True)
            dnw = dn * wv
            do_ref[:, sl] = r * dnw - o * (r * r * r) * jnp.mean(dnw * o, axis=1, keepdims=True)

        @pl.when(i == 0)
        def _():
            dw_ref[...] = dw

        @pl.when(i > 0)
        def _():
            dw_ref[...] += dw

    blk = pl.BlockSpec((tb, W), lambda i: (i, 0))
    row = pl.BlockSpec((1, hd), lambda i: (0, 0))
    return pl.pallas_call(
        body,
        out_shape=(jax.ShapeDtypeStruct((T, W), f32), jax.ShapeDtypeStruct(dproj.shape, dproj.dtype), jax.ShapeDtypeStruct((1, hd), f32)),
        grid=(T // tb,),
        in_specs=[blk, blk, blk, pl.BlockSpec((tb, W), lambda i: (i, gate_blk)), row, _ANY_SPEC],
        out_specs=(blk, pl.BlockSpec((tb, W), lambda i: (i, gate_blk)), row),
        input_output_aliases={5: 1},
        compiler_params=_cparams(("arbitrary",)),
        name=name,
    )(dy, o_f, o_b, proj, w, dproj)


def _gla_prep_bwd(proj, w2f, w2b, b2f, b2b, dgk_f, dgk_b, dqkv, dproj):
    T = proj.shape[0]
    tb = min(ROW_TILE, T)
    W = GLA_HEADS * GLA_DK
    WV = GLA_HEADS * GLA_DV

    def body(s_ref, wf_ref, wb_ref, bf_ref, bb_ref, df_ref, db_ref, qf, qb, kf, kb, vf, vb, _,
             dp_ref, ds_ref, dwf_ref, dwb_ref, dbf_ref, dbb_ref):
        i = pl.program_id(0)
        dp_ref[:, :W] = (qf[...] + qb[...]).astype(dp_ref.dtype)
        dp_ref[:, W : 2 * W] = (kf[...] + kb[...]).astype(dp_ref.dtype)
        dp_ref[:, 2 * W :] = (vf[...] + vb[...]).astype(dp_ref.dtype)
        small = s_ref[...]
        dsm = jnp.zeros((tb, LANES), f32)
        parts = []
        for w_ref, b_ref, d_ref in ((wf_ref, bf_ref, df_ref), (wb_ref, bb_ref, db_ref)):
            pre = _mx(small, w_ref[...], "nn") + b_ref[...]
            dpre = d_ref[...] * (1.0 / GATE_NORMALIZER) * _sigmoid(-pre)
            dsm = dsm + _mx(dpre, w_ref[...], "nt")
            parts.append((_mx(small, dpre, "tn"), jnp.sum(dpre, axis=0, keepdims=True)))
        ds_ref[...] = dsm

        @pl.when(i == 0)
        def _():
            dwf_ref[...], dbf_ref[...] = parts[0]
            dwb_ref[...], dbb_ref[...] = parts[1]

        @pl.when(i > 0)
        def _():
            dwf_ref[...] += parts[0][0]
            dbf_ref[...] += parts[0][1]
            dwb_ref[...] += parts[1][0]
            dbb_ref[...] += parts[1][1]

    blk = pl.BlockSpec((tb, W), lambda i: (i, 0))
    vblk = pl.BlockSpec((tb, WV), lambda i: (i, 0))
    wsp = pl.BlockSpec((LANES, W), lambda i: (0, 0))
    row = pl.BlockSpec((1, W), lambda i: (0, 0))
    return pl.pallas_call(
        body,
        out_shape=(jax.ShapeDtypeStruct(dproj.shape, dproj.dtype), jax.ShapeDtypeStruct((T, LANES), f32), jax.ShapeDtypeStruct((LANES, W), f32),
                   jax.ShapeDtypeStruct((LANES, W), f32), jax.ShapeDtypeStruct((1, W), f32), jax.ShapeDtypeStruct((1, W), f32)),
        grid=(T // tb,),
        in_specs=[pl.BlockSpec((tb, LANES), lambda i: (i, SMALL_OFF // LANES)), wsp, wsp, row, row, blk, blk,
                  blk, blk, blk, blk, vblk, vblk, _ANY_SPEC],
        out_specs=(pl.BlockSpec((tb, 2 * W + WV), lambda i: (i, DP_GLA_QKV_BLK)), pl.BlockSpec((tb, LANES), lambda i: (i, 0)), wsp, wsp, row, row),
        input_output_aliases={13: 0},
        compiler_params=_cparams(("arbitrary",)),
        name="gla_prep_bwd",
    )(proj, w2f, w2b, b2f, b2b, dgk_f, dgk_b, *dqkv, dproj)


GATE_A_BLK, GATE_B_BLK, Z_BLK, G_B_BLK = 6144 // 1024, 7168 // 1024, 3072 // 1024, 8192 // 1024
DP_GATES_BLK, DP_GLA_QKV_BLK, DP_SMALL_BLK = 6144 // 2048, 4096 // 2048, SMALL_OFF // 512
_ANY_SPEC = pl.BlockSpec(memory_space=pl.ANY)


def _mid(og, obg, proj, x, target, wpg, wpl, wout, w_post, tb=256):
    T = x.shape[0]
    Dm = D_MODEL

    def body(og_ref, obg_ref, ga_ref, gb_ref, x_ref, t_ref, wpg_ref, wpl_ref, wo_ref, wp_ref,
             loss_ref, dwp_ref, dy_ref, m_ref, dout_ref, dya_ref, dyb_ref, dgates_ref, dog_ref, dobg_ref):
        i = pl.program_id(0)
        ya = _mx(og_ref[...], wpg_ref[...], "nn")
        yb = _mx(obg_ref[...], wpl_ref[...], "nn")
        sa, sb = _sigmoid(ga_ref[...]), _sigmoid(gb_ref[...])
        m = sa * ya + sb * yb
        m_ref[...] = m.astype(m_ref.dtype)
        out = _mx(m, wo_ref[...], "nn")
        r = lax.rsqrt(jnp.mean(out * out, axis=1, keepdims=True) + NORM_EPS)
        wp = wp_ref[...]
        e = x_ref[...] + out * r * wp - t_ref[...]
        dy = e * (1.0 / Dm)
        dy_ref[...] = dy
        lpart = jnp.sum(e * e, axis=0, keepdims=True)
        wpart = jnp.sum(dy * out * r, axis=0, keepdims=True)
        dyn = dy * wp
        dout = r * dyn - out * (r * r * r) * jnp.mean(dyn * out, axis=1, keepdims=True)
        dout_ref[...] = dout.astype(dout_ref.dtype)
        dm = _mx(dout, wo_ref[...], "nt")
        dya, dyb = dm * sa, dm * sb
        dya_ref[...] = dya.astype(dya_ref.dtype)
        dyb_ref[...] = dyb.astype(dyb_ref.dtype)
        dgates_ref[:, :Dm] = (dm * ya * sa * (1.0 - sa)).astype(dgates_ref.dtype)
        dgates_ref[:, Dm:] = (dm * yb * sb * (1.0 - sb)).astype(dgates_ref.dtype)
        dog_ref[...] = _mx(dya, wpg_ref[...], "nt")
        dobg_ref[...] = _mx(dyb, wpl_ref[...], "nt")

        @pl.when(i == 0)
        def _():
            loss_ref[...] = lpart
            dwp_ref[...] = wpart

        @pl.when(i > 0)
        def _():
            loss_ref[...] += lpart
            dwp_ref[...] += wpart

    blk = pl.BlockSpec((tb, Dm), lambda i: (i, 0))
    row = pl.BlockSpec((1, Dm), lambda i: (0, 0))
    wsp = pl.BlockSpec((Dm, Dm), lambda i: (0, 0))
    act = jax.ShapeDtypeStruct((T, Dm), MXU_DTYPE)
    big = jax.ShapeDtypeStruct((T, Dm), f32)
    vec = jax.ShapeDtypeStruct((1, Dm), f32)
    return pl.pallas_call(
        body,
        out_shape=(vec, vec, big, act, act, act, act, jax.ShapeDtypeStruct((T, N_CAT), MXU_DTYPE), big, big),
        grid=(T // tb,),
        in_specs=[blk, blk, pl.BlockSpec((tb, Dm), lambda i: (i, GATE_A_BLK)), pl.BlockSpec((tb, Dm), lambda i: (i, GATE_B_BLK)),
                  blk, blk, wsp, wsp, wsp, row],
        out_specs=(row, row) + (blk,) * 5 + (pl.BlockSpec((tb, 2 * Dm), lambda i: (i, DP_GATES_BLK)), blk, blk),
        compiler_params=_cparams(("arbitrary",)),
        name="mid",
    )(og, obg, proj, proj, x, target, wpg, wpl, wout, w_post)


def _heads(ref, lead, n, width, rows=slice(None)):
    return [ref[lead + (rows, slice(i * width, (i + 1) * width))] for i in range(n)]


class _GdnBatch:
    def __init__(self, q, k, v, gcol, grow, beta, nf, A=None):
        B, Cn = q.shape[0], q.shape[1]
        self.incl, self.strict, eye = _dir_masks(B, nf, Cn)
        self.decay = jnp.where(self.incl, jnp.exp(jnp.where(self.incl, gcol - grow, 0.0)), 0.0)
        self.k, self.v = k, v
        gl = _last_row(gcol, nf)
        self.beta = jnp.broadcast_to(beta, q.shape)
        self.eg = jnp.broadcast_to(jnp.exp(gcol), q.shape)
        self.ekd = jnp.broadcast_to(jnp.exp(gl - gcol), q.shape)
        self.qs = q * (GDN_DIM**-0.5)
        self.kb = k * self.beta
        self.L = jnp.where(self.strict, _mx(self.kb, k, "nt") * self.decay, 0.0)
        if A is None:
            A = _inv_unit_lower(self.L, jnp.where(eye, 1.0, 0.0).astype(f32))
        self.A = A
        self.A2 = _split2(A)
        self.kbg = self.kb * self.eg
        self.sol = _h3s(self.A2, _split2(jnp.concatenate([v * self.beta, self.kbg], axis=2)), "nn")
        self.u = self.sol[:, :, :GDN_DIM]
        self.w = self.sol[:, :, GDN_DIM:]
        self.attn = jnp.where(self.incl, _mx(self.qs, k, "nt") * self.decay, 0.0)
        self.qd = self.qs * self.eg
        self.kd = k * self.ekd
        self.egl = jnp.exp(gl)


def _gdn_chain_inputs(qf_ref, qb_ref, gf_ref, gb_ref, cf_ref, cb_ref, b, hb):
    qkv = [jnp.stack(_heads(qf_ref, (i,), hb, GDN_DIM) + _heads(qb_ref, (i,), hb, GDN_DIM)) for i in range(3)]
    cols, rows, betas = [], [], []
    for d, (g_ref, c_ref) in enumerate(((gf_ref, cf_ref), (gb_ref, cb_ref))):
        gates, gc = g_ref[...], c_ref[...]
        for hh in range(hb):
            h = b * hb + hh
            col = _lane_col(gc, h + 8 * d)
            cols.append(col)
            rows.append(jnp.broadcast_to(col, (CHUNK, LANES)).T[0:1, :])
            betas.append(_lane_col(gates, 16 + h + 8 * d))
    return qkv[0], qkv[1], qkv[2], jnp.stack(cols), jnp.stack(rows), jnp.stack(betas)


GDN_CPS_FWD, GDN_CPS_BWD = 2, 1


def _chunk_views(refs, s_f, s_b):
    out = []
    for i, r in enumerate(refs):
        s = s_f if i % 2 == 0 else s_b
        rows = pl.ds(s * CHUNK, CHUNK)
        out.append(r.at[pl.ds(s, 1)] if len(r.shape) == 4 else (r.at[:, rows, :] if len(r.shape) == 3 else r.at[rows, :]))
    return out


def _gdn_specs(cps, hb):
    R, W = cps * CHUNK, hb * GDN_DIM
    qsp = lambda ci: pl.BlockSpec((3, R, W), lambda n, b: (0, ci(n), b))
    gsp = lambda ci: pl.BlockSpec((R, LANES), lambda n, b: (ci(n), 0))
    osp = lambda ci: pl.BlockSpec((R, W), lambda n, b: (ci(n), b))
    ssp = lambda ci: pl.BlockSpec((cps, hb, GDN_DIM, GDN_DIM), lambda n, b: (ci(n), b, 0, 0))
    asp = lambda ci: pl.BlockSpec((cps, hb, CHUNK, CHUNK), lambda n, b: (ci(n), b, 0, 0))
    return qsp, gsp, osp, ssp, asp


def _gdn_fwd(qkvn, gates, gc):
    T = qkvn.shape[1]
    N = T // CHUNK
    nh, hb = GDN_HEADS, GDN_HB
    cps = min(GDN_CPS_FWD, N)
    NB = N // cps

    def body(*refs):
        n, b = pl.program_id(0), pl.program_id(1)
        s_scr = refs[-1]

        @pl.when(n == 0)
        def _():
            s_scr[0, pl.ds(b * hb, hb)] = jnp.zeros((hb, GDN_DIM, GDN_DIM), f32)
            s_scr[1, pl.ds(b * hb, hb)] = jnp.zeros((hb, GDN_DIM, GDN_DIM), f32)

        for s in range(cps):
            chunk(*_chunk_views(refs[:-1], s, cps - 1 - s), s_scr, b)

    def chunk(qf_ref, qb_ref, gf_ref, gb_ref, cf_ref, cb_ref, of_ref, ob_ref, sf_ref, sb_ref, af_ref, ab_ref, s_scr, b):
        hs = pl.ds(b * hb, hb)
        q, k, v, gcol, grow, beta = _gdn_chain_inputs(qf_ref, qb_ref, gf_ref, gb_ref, cf_ref, cb_ref, b, hb)
        ck = _GdnBatch(q, k, v, gcol, grow, beta, hb)
        S = jnp.concatenate([s_scr[0, hs], s_scr[1, hs]], axis=0)
        sf_ref[0], sb_ref[0] = S[:hb], S[hb:]
        af_ref[0], ab_ref[0] = ck.A[:hb], ck.A[hb:]
        v_new = ck.u - _mx(ck.w, S, "nn")
        o = _mx(ck.qd, S, "nn") + _mx(ck.attn, v_new, "nn")
        S_new = S * ck.egl + _mx(ck.kd, v_new, "tn")
        for hh in range(hb):
            sl = slice(hh * GDN_DIM, (hh + 1) * GDN_DIM)
            of_ref[:, sl] = o[hh]
            ob_ref[:, sl] = o[hb + hh]
        s_scr[0, hs] = S_new[:hb]
        s_scr[1, hs] = S_new[hb:]

    fw, bw = (lambda n: n), (lambda n: NB - 1 - n)
    qsp, gsp, osp, ssp, asp = _gdn_specs(cps, hb)
    o_shape = jax.ShapeDtypeStruct((T, nh * GDN_DIM), f32)
    s_shape = jax.ShapeDtypeStruct((N, nh, GDN_DIM, GDN_DIM), f32)
    a_shape = jax.ShapeDtypeStruct((N, nh, CHUNK, CHUNK), f32)
    return pl.pallas_call(
        body,
        out_shape=(o_shape, o_shape, s_shape, s_shape, a_shape, a_shape),
        grid=(NB, nh // hb),
        in_specs=[qsp(fw), qsp(bw), gsp(fw), gsp(bw), gsp(fw), gsp(bw)],
        out_specs=(osp(fw), osp(bw), ssp(fw), ssp(bw), asp(fw), asp(bw)),
        scratch_shapes=[pltpu.VMEM((2, nh, GDN_DIM, GDN_DIM), f32)],
        compiler_params=_cparams(("arbitrary", "arbitrary")),
        name="gdn_scan_fwd",
    )(qkvn, qkvn, gates, gates, gc, gc)


def _gdn_bwd(qkvn, gates, gc, st_f, st_b, a_f, a_b, do):
    T = qkvn.shape[1]
    N = T // CHUNK
    nh, hb = GDN_HEADS, GDN_HB
    nb = nh // hb
    cps = min(GDN_CPS_BWD, N)
    NB = N // cps

    def body(*refs):
        n, b = pl.program_id(0), pl.program_id(1)
        ds_scr = refs[-1]

        @pl.when(n == 0)
        def _():
            ds_scr[0, pl.ds(b * hb, hb)] = jnp.zeros((hb, GDN_DIM, GDN_DIM), f32)
            ds_scr[1, pl.ds(b * hb, hb)] = jnp.zeros((hb, GDN_DIM, GDN_DIM), f32)

        for s in range(cps):
            chunk(*_chunk_views(refs[:-1], cps - 1 - s, s), ds_scr, b)

    def chunk(qf_ref, qb_ref, gf_ref, gb_ref, cf_ref, cb_ref, sf_ref, sb_ref, af_ref, ab_ref, dof_ref, dob_ref,
              dqf_ref, dqb_ref, dgf_ref, dgb_ref, ds_scr, b):
        hs = pl.ds(b * hb, hb)

        @pl.when(b == 0)
        def _():
            dgf_ref[...] = jnp.zeros((CHUNK, LANES), f32)
            dgb_ref[...] = jnp.zeros((CHUNK, LANES), f32)

        q, k, v, gcol, grow, beta = _gdn_chain_inputs(qf_ref, qb_ref, gf_ref, gb_ref, cf_ref, cb_ref, b, hb)
        ck = _GdnBatch(q, k, v, gcol, grow, beta, hb, A=jnp.concatenate([af_ref[0], ab_ref[0]], axis=0))
        S = jnp.concatenate([sf_ref[0], sb_ref[0]], axis=0)
        dS = jnp.concatenate([ds_scr[0, hs], ds_scr[1, hs]], axis=0)
        dov = jnp.stack(_heads(dof_ref, (), hb, GDN_DIM) + _heads(dob_ref, (), hb, GDN_DIM))
        v_new = ck.u - _mx(ck.w, S, "nn")
        d_vnew = _mx(ck.attn, dov, "tn") + _mx(ck.kd, dS, "nn")
        d_attn = jnp.where(ck.incl, _mx(dov, v_new, "nt"), 0.0)
        d_qd = _mx(dov, S, "nt")
        d_kd = _mx(v_new, dS, "nt")
        d_gl = jnp.sum(jnp.sum(dS * S, axis=2, keepdims=True), axis=1, keepdims=True) * ck.egl
        dS_new = dS * ck.egl + _mx(ck.qd, dov, "tn") - _mx(ck.w, d_vnew, "tn")
        ds_scr[0, hs] = dS_new[:hb]
        ds_scr[1, hs] = dS_new[hb:]
        d_w = -_mx(d_vnew, S, "nt")
        d_rhs = _h3s(ck.A2, _split2(jnp.concatenate([d_vnew, d_w], axis=2)), "tn")
        d_vb, d_kbg = d_rhs[:, :, :GDN_DIM], d_rhs[:, :, GDN_DIM:]
        dL = -jnp.where(ck.strict, _h3(d_rhs, ck.sol, "nt"), 0.0)
        P = dL * ck.decay
        Q = d_attn * ck.decay
        d_kb = _mx(P, k, "nn") + d_kbg * ck.eg
        d_qs = _mx(Q, k, "nn") + d_qd * ck.eg
        d_k = _mx(P, ck.kb, "tn") + _mx(Q, ck.qs, "tn") + d_kd * ck.ekd + d_kb * ck.beta
        E = dL * ck.L + d_attn * ck.attn
        kdsum = jnp.sum(d_kd * ck.kd, axis=2, keepdims=True)
        d_gc = (jnp.sum(E, axis=2, keepdims=True) - _colsum_col(E)
                + jnp.sum(d_kbg * ck.kbg, axis=2, keepdims=True)
                + jnp.sum(d_qd * ck.qd, axis=2, keepdims=True) - kdsum)
        d_gc = _add_at_last_row(d_gc, d_gl + jnp.sum(kdsum, axis=1, keepdims=True), hb)
        d_beta = jnp.sum(d_vb * v, axis=2, keepdims=True) + jnp.sum(d_kb * k, axis=2, keepdims=True)
        d_q = d_qs * (GDN_DIM**-0.5)
        d_v = d_vb * ck.beta
        lane = _iota2((CHUNK, LANES), 1)
        for d, (dq_ref, dg_ref) in enumerate(((dqf_ref, dgf_ref), (dqb_ref, dgb_ref))):
            acc = dg_ref[...]
            for hh in range(hb):
                c, h = d * hb + hh, b * hb + hh
                sl = slice(hh * GDN_DIM, (hh + 1) * GDN_DIM)
                dq_ref[0, :, sl] = d_q[c]
                dq_ref[1, :, sl] = d_k[c]
                dq_ref[2, :, sl] = d_v[c]
                acc = jnp.where(lane == h + 8 * d, d_gc[c], acc)
                acc = jnp.where(lane == 16 + h + 8 * d, d_beta[c], acc)
            dg_ref[...] = acc

        @pl.when(b == nb - 1)
        def _():
            for d, dg_ref in enumerate((dgf_ref, dgb_ref)):
                blk = dg_ref[...]
                dg_ref[...] = jnp.where(lane < 16, _exact01(_chunk_tri(CHUNK, d == 0), blk), blk)

    fw, bw = (lambda n: NB - 1 - n), (lambda n: n)
    qsp, gsp, osp, ssp, asp = _gdn_specs(cps, hb)
    dq_shape = jax.ShapeDtypeStruct((3, T, nh * GDN_DIM), f32)
    dg_shape = jax.ShapeDtypeStruct((T, LANES), f32)
    return pl.pallas_call(
        body,
        out_shape=(dq_shape, dq_shape, dg_shape, dg_shape),
        grid=(NB, nb),
        in_specs=[qsp(fw), qsp(bw), gsp(fw), gsp(bw), gsp(fw), gsp(bw), ssp(fw), ssp(bw), asp(fw), asp(bw), osp(fw), osp(bw)],
        out_specs=(qsp(fw), qsp(bw), gsp(fw), gsp(bw)),
        scratch_shapes=[pltpu.VMEM((2, nh, GDN_DIM, GDN_DIM), f32)],
        compiler_params=_cparams(("arbitrary", "arbitrary")),
        name="gdn_scan_bwd",
    )(qkvn, qkvn, gates, gates, gc, gc, st_f, st_b, a_f, a_b, do, do)


class _GlaBatch:
    def __init__(self, q, k, G, nf):
        B, Cn = q.shape[0], q.shape[1]
        self.incl, _, _ = _dir_masks(B, nf, Cn)
        self.eG = jnp.exp(G)
        self.enG = jnp.exp(-G)
        self.qg = q * (GLA_DK**-0.5) * self.eG
        self.kg = k * self.enG
        self.attn = jnp.where(self.incl, _mx(self.qg, self.kg, "nt"), 0.0)
        gl = _last_row(G, nf)
        self.ekd = jnp.exp(gl - G)
        self.kd = k * self.ekd
        self.egl = jnp.exp(gl)


GLA_CPS = 4


def _chunk_rows(s):
    return slice(s * CHUNK, (s + 1) * CHUNK)


def _gla_chain_inputs(refs_f, refs_b, rows_f, rows_b):
    nh = GLA_HEADS
    out = []
    for i, width in enumerate((GLA_DK, GLA_DK, GLA_DV, GLA_DK)):
        out.append(jnp.stack(_heads(refs_f[i], (), nh, width, rows_f) + _heads(refs_b[i], (), nh, width, rows_b)))
    return out


def _gla_specs(rows, ci):
    nh = GLA_HEADS
    return [
        pl.BlockSpec((rows, nh * GLA_DK), lambda n: (ci(n), 4096 // (nh * GLA_DK))),
        pl.BlockSpec((rows, nh * GLA_DK), lambda n: (ci(n), 4608 // (nh * GLA_DK))),
        pl.BlockSpec((rows, nh * GLA_DV), lambda n: (ci(n), 5120 // (nh * GLA_DV))),
        pl.BlockSpec((rows, nh * GLA_DK), lambda n: (ci(n), 0)),
    ]


def _gla_fwd(proj, G_f, G_b):
    T = proj.shape[0]
    N = T // CHUNK
    nh = GLA_HEADS
    cps = min(GLA_CPS, N)
    NB = N // cps

    def body(qf, kf, vf, gf, qb, kb, vb, gb, of_ref, ob_ref, sf_ref, sb_ref, s_scr):
        n = pl.program_id(0)

        @pl.when(n == 0)
        def _():
            s_scr[...] = jnp.zeros((2 * nh, GLA_DV, GLA_DK), f32)

        for s in range(cps):
            sb = cps - 1 - s
            q, k, v, G = _gla_chain_inputs((qf, kf, vf, gf), (qb, kb, vb, gb), _chunk_rows(s), _chunk_rows(sb))
            ck = _GlaBatch(q, k, G, nh)
            St = s_scr[...]
            sf_ref[s], sb_ref[sb] = St[:nh].astype(sf_ref.dtype), St[nh:].astype(sb_ref.dtype)
            o = _mx(ck.qg, St, "nt") + _mx(ck.attn, v, "nn")
            for i in range(nh):
                sl = slice(i * GLA_DV, (i + 1) * GLA_DV)
                of_ref[_chunk_rows(s), sl] = o[i]
                ob_ref[_chunk_rows(sb), sl] = o[nh + i]
            s_scr[...] = St * ck.egl + _mx(v, ck.kd, "tn")

    fw, bw = (lambda n: n), (lambda n: NB - 1 - n)
    osp = lambda ci: pl.BlockSpec((cps * CHUNK, nh * GLA_DV), lambda n: (ci(n), 0))
    ssp = lambda ci: pl.BlockSpec((cps, nh, GLA_DV, GLA_DK), lambda n: (ci(n), 0, 0, 0))
    o_shape = jax.ShapeDtypeStruct((T, nh * GLA_DV), f32)
    s_shape = jax.ShapeDtypeStruct((N, nh, GLA_DV, GLA_DK), MXU_DTYPE)
    return pl.pallas_call(
        body,
        out_shape=(o_shape, o_shape, s_shape, s_shape),
        grid=(NB,),
        in_specs=_gla_specs(cps * CHUNK, fw) + _gla_specs(cps * CHUNK, bw),
        out_specs=(osp(fw), osp(bw), ssp(fw), ssp(bw)),
        scratch_shapes=[pltpu.VMEM((2 * nh, GLA_DV, GLA_DK), f32)],
        compiler_params=_cparams(("arbitrary",)),
        name="gla_scan_fwd",
    )(proj, proj, proj, G_f, proj, proj, proj, G_b)


def _gla_bwd(proj, G_f, G_b, st_f, st_b, do):
    T = proj.shape[0]
    N = T // CHUNK
    nh = GLA_HEADS
    cps = min(GLA_CPS, N)
    NB = N // cps

    def body(qf, kf, vf, gf, sf_ref, dof_ref, qb, kb, vb, gb, sb_ref, dob_ref,
             dqf_ref, dkf_ref, dvf_ref, dgf_ref, dqb_ref, dkb_ref, dvb_ref, dgb_ref, ds_scr):
        n = pl.program_id(0)

        @pl.when(n == 0)
        def _():
            ds_scr[...] = jnp.zeros((2 * nh, GLA_DV, GLA_DK), f32)

        shp = (2 * nh, CHUNK, CHUNK)
        bb, ii, jj = _iota3(shp, 0), _iota3(shp, 1), _iota3(shp, 2)
        tri = jnp.where(jnp.logical_or(jnp.logical_and(bb < nh, jj >= ii), jnp.logical_and(bb >= nh, jj <= ii)), 1.0, 0.0)
        for s in range(cps):
            sf = cps - 1 - s
            rows = (_chunk_rows(sf), _chunk_rows(s))
            q, k, v, G = _gla_chain_inputs((qf, kf, vf, gf), (qb, kb, vb, gb), *rows)
            ck = _GlaBatch(q, k, G, nh)
            St = jnp.concatenate([sf_ref[sf], sb_ref[s]], axis=0).astype(f32)
            dSt = ds_scr[...]
            dov = jnp.stack(_heads(dof_ref, (), nh, GLA_DV, rows[0]) + _heads(dob_ref, (), nh, GLA_DV, rows[1]))
            d_attn = jnp.where(ck.incl, _mx(dov, v, "nt"), 0.0)
            d_qg = _mx(dov, St, "nn") + _mx(d_attn, ck.kg, "nn")
            d_kg = _mx(d_attn, ck.qg, "tn")
            d_v = _mx(ck.attn, dov, "tn") + _mx(ck.kd, dSt, "nt")
            d_kd = _mx(v, dSt, "nn")
            ds_scr[...] = dSt * ck.egl + _mx(dov, ck.qg, "tn")
            kdd = d_kd * ck.kd
            d_gl = jnp.sum(dSt * St, axis=1, keepdims=True) * ck.egl + jnp.sum(kdd, axis=1, keepdims=True)
            d_q = d_qg * ck.eG * (GLA_DK**-0.5)
            d_k = d_kg * ck.enG + d_kd * ck.ekd
            d_gk = _exact01(tri, _add_at_last_row(d_qg * ck.qg - d_kg * ck.kg - kdd, d_gl, nh))
            for d, (dq_ref, dk_ref, dv_ref, dg_ref) in enumerate(((dqf_ref, dkf_ref, dvf_ref, dgf_ref), (dqb_ref, dkb_ref, dvb_ref, dgb_ref))):
                for i in range(nh):
                    c = d * nh + i
                    ks, vs = slice(i * GLA_DK, (i + 1) * GLA_DK), slice(i * GLA_DV, (i + 1) * GLA_DV)
                    dq_ref[rows[d], ks] = d_q[c]
                    dk_ref[rows[d], ks] = d_k[c]
                    dv_ref[rows[d], vs] = d_v[c]
                    dg_ref[rows[d], ks] = d_gk[c]

    R = cps * CHUNK

    def in_specs(ci):
        return _gla_specs(R, ci) + [
            pl.BlockSpec((cps, nh, GLA_DV, GLA_DK), lambda n: (ci(n), 0, 0, 0)),
            pl.BlockSpec((R, nh * GLA_DV), lambda n: (ci(n), 0)),
        ]

    def out_specs(ci):
        ksp = pl.BlockSpec((R, nh * GLA_DK), lambda n: (ci(n), 0))
        return [ksp, ksp, pl.BlockSpec((R, nh * GLA_DV), lambda n: (ci(n), 0)), ksp]

    fw, bw = (lambda n: NB - 1 - n), (lambda n: n)
    k_shape = jax.ShapeDtypeStruct((T, nh * GLA_DK), f32)
    v_shape = jax.ShapeDtypeStruct((T, nh * GLA_DV), f32)
    return pl.pallas_call(
        body,
        out_shape=(k_shape, k_shape, v_shape, k_shape) * 2,
        grid=(NB,),
        in_specs=in_specs(fw) + in_specs(bw),
        out_specs=tuple(out_specs(fw) + out_specs(bw)),
        scratch_shapes=[pltpu.VMEM((2 * nh, GLA_DV, GLA_DK), f32)],
        compiler_params=_cparams(("arbitrary",)),
        name="gla_scan_bwd",
    )(proj, proj, proj, G_f, st_f, do, proj, proj, proj, G_b, st_b, do)


def _local_step(x, target, wcat, conv_w, avec, dvec, gdn_norm_w, w2f, w2b, b2f, b2b, gla_norm_w, proj_weights, w_pre, w_post,
                send_proj_grads=None, send_in_grads=None):
    h, h_t = _rms_pre(x, w_pre)
    proj = _matmul(h, wcat, "nt", f32, "proj_in", 2048, 512, 1024)
    qkvn = _gdn_prep(proj, conv_w)
    gates, gc, G_f, G_b = _gate_prep(proj, avec, dvec, w2f, w2b, b2f, b2b)
    oa_f, oa_b, sa_f, sa_b, inv_f, inv_b = _gdn_fwd(qkvn, gates, gc)
    og = _out_norm(oa_f, oa_b, proj, Z_BLK, gdn_norm_w, GDN_DIM, "gdn_out")
    ob_f, ob_b, sb_f, sb_b = _gla_fwd(proj, G_f, G_b)
    obg = _out_norm(ob_f, ob_b, proj, G_B_BLK, gla_norm_w, GLA_DV, "gla_out")
    wpg, wpl, wout = proj_weights(obg)
    (loss_row, d_wpost, dy, m, dout, dya, dyb, dproj, dog, dobg) = _mid(og, obg, proj, x, target, wpg, wpl, wout, w_post)

    d_wout = _matmul(m, dout, "tn", MXU_DTYPE, "dw_out", 1024, 512, 2048)
    d_wpg = _matmul(og, dya, "tn", MXU_DTYPE, "dw_proj_gdn", 1024, 512, 2048)
    d_wpl = _matmul(obg, dyb, "tn", MXU_DTYPE, "dw_proj_gla", 1024, 512, 2048)
    if send_proj_grads is not None:
        gla_norm_w = gla_norm_w + send_proj_grads(d_wpg, d_wpl, d_wout)

    do_b, dproj, d_gla_norm = _out_norm_bwd(dobg, ob_f, ob_b, proj, G_B_BLK, gla_norm_w, GLA_DV, "gla_out_bwd", dproj)
    dq_f, dk_f, dv_f, dgk_f, dq_b, dk_b, dv_b, dgk_b = _gla_bwd(proj, G_f, G_b, sb_f, sb_b, do_b)
    dproj, dsmall_gla, d_w2f, d_w2b, d_b2f, d_b2b = _gla_prep_bwd(proj, w2f, w2b, b2f, b2b, dgk_f, dgk_b,
                                                                 (dq_f, dq_b, dk_f, dk_b, dv_f, dv_b), dproj)

    do_a, dproj, d_gdn_norm = _out_norm_bwd(dog, oa_f, oa_b, proj, Z_BLK, gdn_norm_w, GDN_DIM, "gdn_out_bwd", dproj)
    dqkv_f, dqkv_b, dg_f, dg_b = _gdn_bwd(qkvn, gates, gc, sa_f, sa_b, inv_f, inv_b, do_a)
    dproj, d_conv = _gdn_prep_bwd(proj, conv_w, dqkv_f, dqkv_b, dproj)
    dproj, d_alog, d_dtb = _gdn_gates_bwd(proj, avec, dvec, dg_f, dg_b, dsmall_gla, dproj)

    d_wcat = _matmul(h_t, dproj, "nn", MXU_DTYPE, "dw_in", 1024, 512, 2048, out_t=True)
    started = None if send_in_grads is None else send_in_grads(d_wcat)
    dh = _matmul(dproj, wcat, "nn", f32, "dh", 512, 1024, N_CAT, after=started)
    grad_x, d_wpre = _rms_pre_bwd(x, w_pre, dh, dy)
    return dict(loss_row=loss_row, grad_x=grad_x, d_wcat=d_wcat, d_conv=d_conv, d_alog=d_alog, d_dtb=d_dtb,
                d_gdn_norm=d_gdn_norm, d_wpg=d_wpg, d_w2f=d_w2f, d_w2b=d_w2b, d_b2f=d_b2f, d_b2b=d_b2b,
                d_gla_norm=d_gla_norm, d_wpl=d_wpl, d_wout=d_wout, d_wpost=d_wpost, d_wpre=d_wpre)


def _to_cat(wf):
    pad = jnp.zeros((N_CAT - N_IN,) + wf.shape[1:], wf.dtype)
    return jnp.concatenate([wf[:4096], wf[4128:6176], wf[7232:9280], wf[6176:7200], wf[4096:4128], wf[7200:7232], pad], axis=0)


def _from_cat(wc):
    return jnp.concatenate([wc[:4096], wc[9216:9248], wc[4096:6144], wc[8192:9216], wc[9248:9280], wc[6144:8192]], axis=0)


def _lane_row(*pieces):
    row = jnp.concatenate(pieces, axis=1)
    return jnp.pad(row, ((0, 0), (0, LANES - row.shape[1])))


def _pad_w2(w2, row0):
    return jnp.pad(w2, ((row0, LANES - row0 - w2.shape[0]), (0, 0)))


SM_ROWS, SM_COLS = 32, 384


def _pack_small_shard(conv, w2f, w2b):
    lead = conv.shape[:-2]
    z = lambda r, c: jnp.zeros(lead + (r, c), f32)
    top = jnp.concatenate([conv, z(8 - CONV_K, SM_COLS)], axis=-2)
    mid = jnp.concatenate([w2f, w2b, z(16, SM_COLS - 128)], axis=-1)
    return jnp.concatenate([top, mid, z(SM_ROWS - 24, SM_COLS)], axis=-2)


SMALL_REGION = {
    "ln_pre_w": (0, 1, 0, 1024), "ln_post_w": (1, 1, 0, 1024), "gk_b2_fwd": (2, 1, 0, 512), "gk_b2_bwd": (2, 1, 512, 512),
    "gla_norm_w": (3, 1, 0, 256), "gdn_norm_w": (3, 1, 256, 128), "a_log_fwd": (4, 1, 0, 8), "a_log_bwd": (5, 1, 0, 8),
    "dt_bias_fwd": (6, 1, 0, 8), "dt_bias_bwd": (7, 1, 0, 8), "conv_w": (8, CONV_K, 0, SM_COLS),
    "gk_w2_fwd": (16, 16, 0, 64), "gk_w2_bwd": (16, 16, 128, 64),
}
SMALL_NAMES = tuple(SMALL_REGION)
SMALL_SHARDED = ("conv_w", "gk_w2_fwd", "gk_w2_bwd")
PACK_ROWS, PACK_COLS = 32, 1024
LOSS_AT = (13, 0)


def _region(name):
    r0, nr, l0, nl = SMALL_REGION[name]
    return slice(r0, r0 + nr), slice(l0, l0 + nl)


def _pack_small_grads(G):
    names = ("d_wpre", "d_wpost", "d_b2f", "d_b2b", "d_gla_norm", "d_gdn_norm", "d_alog", "d_dtb", "loss_row", "d_conv", "d_w2f", "d_w2b")

    def body(wpre, wpost, b2f, b2b, glan, gdnn, alog, dtb, lrow, conv, w2f, w2b, o_ref):
        o_ref[...] = jnp.zeros(o_ref.shape, f32)
        loss = 0.5 * jnp.sum(lrow[...], axis=1, keepdims=True) * (1.0 / D_MODEL)
        whole = {"ln_pre_w": wpre, "ln_post_w": wpost, "gk_b2_fwd": b2f, "gk_b2_bwd": b2b, "gla_norm_w": glan, "gdn_norm_w": gdnn}
        for j in range(N_DEV):
            for name, ref in whole.items():
                o_ref[(j,) + _region(name)] = ref[...]
            o_ref[(j,) + _region("a_log_fwd")] = alog[0:1, 0:8]
            o_ref[(j,) + _region("a_log_bwd")] = alog[0:1, 8:16]
            o_ref[(j,) + _region("dt_bias_fwd")] = dtb[0:1, 0:8]
            o_ref[(j,) + _region("dt_bias_bwd")] = dtb[0:1, 8:16]
            o_ref[(j,) + _region("conv_w")] = conv[:, j * SM_COLS : (j + 1) * SM_COLS]
            o_ref[(j,) + _region("gk_w2_fwd")] = w2f[32:48, j * 64 : (j + 1) * 64]
            o_ref[(j,) + _region("gk_w2_bwd")] = w2b[48:64, j * 64 : (j + 1) * 64]
            o_ref[j, LOSS_AT[0] : LOSS_AT[0] + 1, LOSS_AT[1] : LOSS_AT[1] + 1] = loss

    return pl.pallas_call(
        body, out_shape=jax.ShapeDtypeStruct((N_DEV, PACK_ROWS, PACK_COLS), f32), name="pack_small_grads",
    )(*[G[n] for n in names])


def _small_adam(land, own, W, M, V):
    bc1 = 1.0 - ADAM_B1**ADAM_STEP
    bc2 = 1.0 - ADAM_B2**ADAM_STEP
    n = len(SMALL_NAMES)

    def body(land_ref, own_ref, *refs):
        w_refs, m_refs, v_refs = refs[0:n], refs[n : 2 * n], refs[2 * n : 3 * n]
        outs, loss_ref, g_scr = refs[3 * n : 7 * n], refs[7 * n], refs[7 * n + 1]
        me = _my_device()
        g = jnp.where(me == 0, own_ref[0], land_ref[0])
        for j in range(1, N_DEV):
            g = g + jnp.where(me == j, own_ref[j], land_ref[j])
        g_scr[...] = g
        loss_ref[...] = g_scr[LOSS_AT[0] : LOSS_AT[0] + 1, LOSS_AT[1] : LOSS_AT[1] + 1]
        for i, name in enumerate(SMALL_NAMES):
            gp = g_scr[_region(name)]
            lead = (0,) if name in SMALL_SHARDED else (Ellipsis,)
            w, m, v = w_refs[i][lead], m_refs[i][lead], v_refs[i][lead]
            m2 = ADAM_B1 * m + (1.0 - ADAM_B1) * gp
            v2 = ADAM_B2 * v + (1.0 - ADAM_B2) * (gp * gp)
            delta = -ADAM_LR * ((m2 / bc1) / (jnp.sqrt(v2 / bc2) + ADAM_EPS) + ADAM_WD * w)
            for k, val in enumerate((gp, delta, m2, v2)):
                outs[4 * i + k][lead] = val

    shapes = [jax.ShapeDtypeStruct(W[name].shape, f32) for name in SMALL_NAMES for _ in range(4)]
    res = pl.pallas_call(
        body,
        out_shape=tuple(shapes) + (jax.ShapeDtypeStruct((1, 1), f32),),
        scratch_shapes=[pltpu.VMEM((PACK_ROWS, PACK_COLS), f32)],
        name="adam_small",
    )(land, own, *[P[name] for P in (W, M, V) for name in SMALL_NAMES])
    return {name: res[4 * i : 4 * i + 4] for i, name in enumerate(SMALL_NAMES)}, res[4 * n]


_MESH = pl.DeviceIdType.MESH


N_BIG_COPIES, N_SMALL_COPIES = 9, 7


def _gather_weights(w_in_s, sm_s):
    shapes = (w_in_s.shape, sm_s.shape)
    dtypes = (MXU_DTYPE, f32)

    def body(win_ref, sm_ref, gin_ref, gsm_ref, send_sems, recv_sems):
        x, y, c = lax.axis_index("x"), lax.axis_index("y"), lax.axis_index("c")
        me, sibling = (x, y, c), (x, y, 1 - c)
        chips = [(1 - x, y), (x, 1 - y), (1 - x, 1 - y)]

        def idx(px, py, pc):
            return 4 * px + 2 * py + pc

        for r in range(w_in_s.shape[1] // LANES):
            cols = slice(r * LANES, (r + 1) * LANES)
            gin_ref[idx(*me), :, cols] = win_ref[:, cols].astype(MXU_DTYPE)
        gsm_ref[idx(*me)] = sm_ref[...]

        def copy(sem, blk, to):
            return pltpu.make_async_remote_copy(src_ref=blk, dst_ref=blk, send_sem=send_sems.at[sem], recv_sem=recv_sems.at[sem],
                                                device_id=to, device_id_type=_MESH)

        def small(k, block, to):
            return copy(N_BIG_COPIES + k, gsm_ref.at[idx(*block)], to)

        half = w_in_s.shape[1] // 2
        xn, yn, dg = (1 - x, y), (x, 1 - y), (1 - x, 1 - y)

        def big(sem, block, to, part=None):
            blk = gin_ref.at[idx(*block)]
            return copy(sem, blk if part is None else blk.at[:, pl.ds(part * half, half)], to)

        started = [big(0, me, sibling), big(1, me, (*xn, c)), big(2, me, (*yn, c)), small(0, me, sibling)]
        started += [small(1 + j, me, (*chip, c)) for j, chip in enumerate(chips)]
        for cp in started:
            cp.start()

        def then(arrived, forwards):
            arrived.wait_recv()
            for cp in forwards:
                cp.start()
            started.extend(forwards)

        then(big(1, (*xn, c), me), [big(3, (*xn, c), (*yn, c), 1), big(5, (*xn, c), sibling)])
        then(big(2, (*yn, c), me), [big(4, (*yn, c), (*xn, c), 0), big(6, (*yn, c), sibling)])
        then(big(4, (*dg, c), me, 0), [big(7, (*dg, c), sibling, 0)])
        then(big(3, (*dg, c), me, 1), [big(8, (*dg, c), sibling, 1)])
        for j, chip in enumerate(chips):
            then(small(1 + j, (*chip, c), me), [small(4 + j, (*chip, c), sibling)])
        big(0, sibling, me).wait_recv()
        big(5, (*xn, 1 - c), me).wait_recv()
        big(6, (*yn, 1 - c), me).wait_recv()
        big(7, (*dg, 1 - c), me, 0).wait_recv()
        big(8, (*dg, 1 - c), me, 1).wait_recv()
        small(0, sibling, me).wait_recv()
        for j, chip in enumerate(chips):
            small(4 + j, (*chip, 1 - c), me).wait_recv()
        for cp in started:
            cp.wait_send()

    vm = pl.BlockSpec(memory_space=pltpu.VMEM)
    return pl.pallas_call(
        body,
        out_shape=tuple(jax.ShapeDtypeStruct((N_DEV,) + s, d) for s, d in zip(shapes, dtypes)),
        in_specs=[vm, vm],
        out_specs=(vm, vm),
        scratch_shapes=[pltpu.SemaphoreType.DMA((N_BIG_COPIES + N_SMALL_COPIES,)), pltpu.SemaphoreType.DMA((N_BIG_COPIES + N_SMALL_COPIES,))],
        compiler_params=pltpu.CompilerParams(vmem_limit_bytes=VMEM_LIMIT),
        name="gather_weights",
    )(w_in_s, sm_s)


N_CHIP = 4
_EFFECT = pltpu.SideEffectType.DATAFLOW_SIDE_EFFECTING
_HBM_SPEC = pl.BlockSpec(memory_space=pltpu.HBM)
_SEM_SPEC = pl.BlockSpec(memory_space=pltpu.SEMAPHORE)


def _relation_peer(k, x, y, c):
    px = (1 - x) if (k & 4) else x
    py = (1 - y) if (k & 2) else y
    pc = (1 - c) if (k & 1) else c
    return (px, py, pc), 4 * px + 2 * py + pc


_PLAN_COPIES = {"gather": N_DEV - 1, "blocks": N_DEV - 1, "sibling": 4, "chips": 3}
_PLAN_SLOTS = {"gather": N_DEV, "blocks": N_DEV, "sibling": 4, "chips": 4}


def _plan_copies(plan, src_ref, land_ref, send_sems, recv_sems, arrival):
    x, y, c = lax.axis_index("x"), lax.axis_index("y"), lax.axis_index("c")
    me, my_chip = 4 * x + 2 * y + c, 2 * x + y
    out = []

    def add(i, src, there, here, dev):
        out.append(pltpu.make_async_remote_copy(src_ref=src, dst_ref=land_ref.at[here if arrival else there], send_sem=send_sems.at[i],
                                                recv_sem=recv_sems.at[i], device_id=dev, device_id_type=_MESH))

    if plan in ("gather", "blocks"):
        for k in range(1, N_DEV):
            dev, p = _relation_peer(k, x, y, c)
            add(k - 1, src_ref.at[p] if plan == "blocks" else src_ref, me, p, dev)
    elif plan == "sibling":
        for j in range(4):
            add(j, src_ref.at[2 * j + 1 - c], j, j, (x, y, 1 - c))
    else:
        for i, k in enumerate((2, 4, 6)):
            (px, py, pc), _ = _relation_peer(k, x, y, c)
            add(i, src_ref.at[2 * px + py], my_chip, 2 * px + py, (px, py, pc))
    return out


def _exchange_start(src, plan, name, after=None):
    n = _PLAN_COPIES[plan]
    land = lax.empty((_PLAN_SLOTS[plan],) + (src.shape if plan == "gather" else src.shape[1:]), src.dtype)
    extra = [] if after is None else [after]

    def body(src_ref, land_ref, *rest):
        send_sems, recv_sems, _, _, token = rest[len(extra) :]
        for cp in _plan_copies(plan, src_ref, land_ref, send_sems, recv_sems, False):
            cp.start()
        token[...] = jnp.zeros_like(token)

    return pl.pallas_call(
        body,
        name=name,
        out_shape=(pltpu.SemaphoreType.DMA((n,)), pltpu.SemaphoreType.DMA((n,)), pltpu.HBM(src.shape, src.dtype),
                   pltpu.HBM(land.shape, land.dtype), jax.ShapeDtypeStruct((8, LANES), f32)),
        in_specs=(_HBM_SPEC, _HBM_SPEC) + (_ANY_SPEC,) * len(extra),
        out_specs=(_SEM_SPEC, _SEM_SPEC, _HBM_SPEC, _HBM_SPEC, pl.BlockSpec(memory_space=pltpu.VMEM)),
        input_output_aliases={0: 2, 1: 3},
        compiler_params=pltpu.CompilerParams(has_side_effects=_EFFECT),
    )(pltpu.with_memory_space_constraint(src, pltpu.HBM), pltpu.with_memory_space_constraint(land, pltpu.HBM), *extra)


def _exchange_wait(started, after, plan, name):
    send_sems, recv_sems, src_thru, land_thru, _ = started

    def body(src_ref, land_ref, send_sems, recv_sems, after_ref, src_dead, got_ref):
        for cp in _plan_copies(plan, src_ref, land_ref, send_sems, recv_sems, True):
            cp.wait_send()
            cp.wait_recv()

    return pl.pallas_call(
        body,
        name=name,
        out_shape=(pltpu.HBM(src_thru.shape, src_thru.dtype), pltpu.HBM(land_thru.shape, land_thru.dtype)),
        in_specs=(_HBM_SPEC, _HBM_SPEC, _SEM_SPEC, _SEM_SPEC, pl.BlockSpec(memory_space=pl.ANY)),
        out_specs=(_HBM_SPEC, _HBM_SPEC),
        input_output_aliases={0: 0, 1: 1},
        compiler_params=pltpu.CompilerParams(has_side_effects=_EFFECT),
    )(src_thru, land_thru, send_sems, recv_sems, after)


def _pair_sum(g, r, name, tc=LANES):
    _, R, Cc = g.shape

    def body(g_ref, r_ref, o_ref):
        o_ref[...] = (g_ref[...].astype(f32) + r_ref[...].astype(f32)).astype(o_ref.dtype)

    return pl.pallas_call(
        body,
        out_shape=jax.ShapeDtypeStruct(r.shape, r.dtype),
        grid=(N_CHIP, Cc // tc),
        in_specs=[pl.BlockSpec((1, R, tc), lambda k, i: (2 * k + lax.axis_index("c"), 0, i)), pl.BlockSpec((1, R, tc), lambda k, i: (k, 0, i))],
        out_specs=pl.BlockSpec((1, R, tc), lambda k, i: (k, 0, i)),
        compiler_params=_cparams(("parallel", "parallel")),
        name=name,
    )(g, r)


def _sum_adam(parts, w, m, v, name, tb, tc=None, own=None, own_slot=None):
    R, Cc = w.shape
    n_parts = parts.shape[0]
    tb = R if tc else min(tb, R)
    tc = tc or Cc
    assert R % tb == 0 and Cc % tc == 0
    bc1 = 1.0 - ADAM_B1**ADAM_STEP
    bc2 = 1.0 - ADAM_B2**ADAM_STEP
    extra = [] if own is None else [own]

    def body(p_ref, w_ref, m_ref, v_ref, *rest):
        g_ref, d_ref, nm_ref, nv_ref = rest[len(extra) :]

        def part(j):
            pj = p_ref[j].astype(f32)
            return pj if own is None else jnp.where(own_slot() == j, rest[0][0].astype(f32), pj)

        g = part(0)
        for j in range(1, n_parts):
            g = g + part(j)
        g_ref[...] = g
        m2 = ADAM_B1 * m_ref[...] + (1.0 - ADAM_B1) * g
        v2 = ADAM_B2 * v_ref[...] + (1.0 - ADAM_B2) * (g * g)
        nm_ref[...] = m2
        nv_ref[...] = v2
        d_ref[...] = -ADAM_LR * ((m2 / bc1) / (jnp.sqrt(v2 / bc2) + ADAM_EPS) + ADAM_WD * w_ref[...])

    blk = pl.BlockSpec((tb, tc), lambda i, j: (i, j))
    o = jax.ShapeDtypeStruct((R, Cc), f32)
    return pl.pallas_call(
        body,
        out_shape=(o, o, o, o),
        grid=(R // tb, Cc // tc),
        in_specs=[pl.BlockSpec((n_parts, tb, tc), lambda i, j: (0, i, j)), blk, blk, blk]
        + [pl.BlockSpec((1, tb, tc), lambda i, j: (own_slot(), i, j))] * len(extra),
        out_specs=(blk, blk, blk, blk),
        compiler_params=_cparams(("parallel", "parallel")),
        name=name,
    )(parts, w, m, v, *extra)


def _my_device():
    return 4 * lax.axis_index("x") + 2 * lax.axis_index("y") + lax.axis_index("c")


def _my_chip():
    return 2 * lax.axis_index("x") + lax.axis_index("y")


_WEIGHTS = ["ln_pre_w", "w_in", "conv_w", "a_log_fwd", "a_log_bwd", "dt_bias_fwd", "dt_bias_bwd", "gdn_norm_w", "w_proj_gdn",
            "gk_w2_fwd", "gk_b2_fwd", "gk_w2_bwd", "gk_b2_bwd", "gla_norm_w", "w_proj_gla", "w_out", "ln_post_w"]


def kernel(x, ln_pre_w, w_in, conv_w, a_log_fwd, a_log_bwd, dt_bias_fwd, dt_bias_bwd, gdn_norm_w, w_proj_gdn, gk_w2_fwd, gk_b2_fwd, gk_w2_bwd, gk_b2_bwd, gla_norm_w, w_proj_gla, w_out, ln_post_w, loss_target, m_ln_pre_w, m_w_in, m_conv_w, m_a_log_fwd, m_a_log_bwd, m_dt_bias_fwd, m_dt_bias_bwd, m_gdn_norm_w, m_w_proj_gdn, m_gk_w2_fwd, m_gk_b2_fwd, m_gk_w2_bwd, m_gk_b2_bwd, m_gla_norm_w, m_w_proj_gla, m_w_out, m_ln_post_w, v_ln_pre_w, v_w_in, v_conv_w, v_a_log_fwd, v_a_log_bwd, v_dt_bias_fwd, v_dt_bias_bwd, v_gdn_norm_w, v_w_proj_gdn, v_gk_w2_fwd, v_gk_b2_fwd, v_gk_w2_bwd, v_gk_b2_bwd, v_gla_norm_w, v_w_proj_gla, v_w_out, v_ln_post_w):
    args = locals()
    W = {n: args[n] for n in _WEIGHTS}
    M = {n: args["m_" + n] for n in _WEIGHTS}
    V = {n: args["v_" + n] for n in _WEIGHTS}

    wp_stack = lambda P: jnp.concatenate([P["w_proj_gdn"], P["w_proj_gla"], P["w_out"]], axis=0)
    sm_pack = lambda P: _pack_small_shard(P["conv_w"][0], P["gk_w2_fwd"][0], P["gk_w2_bwd"][0])
    me = 4 * lax.axis_index("x") + 2 * lax.axis_index("y") + lax.axis_index("c")
    w_in_t, m_in_t, v_in_t = (jnp.transpose(a[0]) for a in (w_in, m_w_in, v_w_in))
    g_in, g_sm = _gather_weights(w_in_t, sm_pack(W))
    wp_own = wp_stack(W).astype(MXU_DTYPE)
    wp_started = _exchange_start(wp_own, "gather", "gather_proj_start", after=g_sm)

    def proj_weights(after):
        own, land = _exchange_wait(wp_started, after, "gather", "gather_proj_wait")
        g_p = lax.dynamic_update_slice(land, own[None], (me, 0, 0, 0))
        return tuple(g_p[:, i].reshape(D_MODEL, D_MODEL) for i in range(3))

    rows_p = 3 * D_MODEL // N_DEV
    sent = {}

    def send_proj_grads(d_wpg, d_wpl, d_wout):
        p_p = jnp.stack([d_wpg, d_wpl, d_wout]).reshape(3, N_DEV, D_MODEL // N_DEV, D_MODEL).transpose(1, 0, 2, 3)
        sent["p_p"] = p_p.reshape(N_DEV, rows_p, D_MODEL)
        sent["started"] = _exchange_start(sent["p_p"], "blocks", "exchange_proj_start")
        return sent["started"][4][0:1, 0:1]

    def send_in_grads(d_wcat):
        p_in = _from_cat(d_wcat).reshape(N_DEV, SHARD_IN, D_MODEL)
        sib = _exchange_start(p_in, "sibling", "exchange_in_sibling_start")
        p_p, land_p = _exchange_wait(sent["started"], sib[4], "blocks", "exchange_proj_wait")
        sent["o_p"] = _sum_adam(land_p, wp_stack(W).reshape(rows_p, D_MODEL), wp_stack(M).reshape(rows_p, D_MODEL),
                                wp_stack(V).reshape(rows_p, D_MODEL), "adam_w_proj", 128, own=p_p, own_slot=_my_device)
        p_in, sib_land = _exchange_wait(sib, sent["o_p"][0], "sibling", "exchange_in_sibling_wait")
        sent["chips"] = _exchange_start(_pair_sum(p_in, sib_land, "pair_sum_w_in", tc=512), "chips", "exchange_in_chips_start")
        return sent["chips"][4]

    wcat = _to_cat(g_in.reshape(N_IN, D_MODEL))
    conv_full = g_sm[:, 0:CONV_K, :].transpose(1, 0, 2).reshape(CONV_K, N_DEV * SM_COLS)
    w2f_full = g_sm[:, 8:24, 0:64].transpose(1, 0, 2).reshape(16, 512)
    w2b_full = g_sm[:, 8:24, 64:128].transpose(1, 0, 2).reshape(16, 512)

    avec = _lane_row(a_log_fwd, a_log_bwd)
    dvec = _lane_row(dt_bias_fwd, dt_bias_bwd)
    G = _local_step(x[0], loss_target[0], wcat, conv_full, avec, dvec, gdn_norm_w, _pad_w2(w2f_full, 32), _pad_w2(w2b_full, 48),
                    gk_b2_fwd, gk_b2_bwd, gla_norm_w, proj_weights, ln_pre_w + wp_started[4][0:1, 0:1], ln_post_w, send_proj_grads,
                    send_in_grads)

    small_started = _exchange_start(_pack_small_grads(G), "blocks", "exchange_small_start")
    q_in, land_in = _exchange_wait(sent["chips"], small_started[4], "chips", "exchange_in_chips_wait")
    o_in = _sum_adam(land_in, w_in_t, m_in_t, v_in_t, "adam_w_in", SHARD_IN, tc=256, own=q_in, own_slot=_my_chip)
    o_p = sent["o_p"]
    p_small, land_small = _exchange_wait(small_started, o_in[0], "blocks", "exchange_small_wait")
    o_small, loss = _small_adam(land_small, p_small, W, M, V)

    res = []
    for kind in range(4):
        per = {name: o_small[name][kind] for name in SMALL_NAMES}
        per["w_in"] = jnp.transpose(o_in[kind])[None]
        pp = o_p[kind].reshape(3, 1, D_MODEL // N_DEV, D_MODEL)
        per["w_proj_gdn"], per["w_proj_gla"], per["w_out"] = pp[0], pp[1], pp[2]
        res.append([per[n] for n in _WEIGHTS])
    return (loss[0, 0], G["grad_x"][None], *res[0], *res[1], *res[2], *res[3])
```

```python
import jax
import jax.numpy as jnp
from jax import lax
from jax.experimental import pallas as pl
from jax.experimental.pallas import tpu as pltpu

f32 = jnp.float32
MXU_DTYPE = jnp.bfloat16

D_MODEL = 1024
CHUNK = 64
NORM_EPS = 1e-6
GDN_HEADS, GDN_DIM = 8, 128
GLA_HEADS, GLA_DK, GLA_DV = 4, 128, 256
GATE_NORMALIZER = 16.0
CONV_K = 5
N_IN = 9280
N_DEV = 8
SHARD_IN = N_IN // N_DEV

N_CAT = 9728
SMALL_OFF = 9216
LANES = 128

ADAM_LR, ADAM_B1, ADAM_B2, ADAM_EPS, ADAM_WD, ADAM_STEP = 0.001, 0.9, 0.999, 1e-08, 0.01, 10

VMEM_LIMIT = 56 * 1024 * 1024


def _cparams(sem=None):
    return pltpu.CompilerParams(dimension_semantics=sem, vmem_limit_bytes=VMEM_LIMIT)


_DN = {"nn": (((1,), (0,)), ((), ())), "nt": (((1,), (1,)), ((), ())), "tn": (((0,), (0,)), ((), ()))}
_BDN = {"nn": (((2,), (1,)), ((0,), (0,))), "nt": (((2,), (2,)), ((0,), (0,))), "tn": (((1,), (1,)), ((0,), (0,)))}


def _dot16(a, b, kind):
    return lax.dot_general(a, b, (_DN if a.ndim == 2 else _BDN)[kind], preferred_element_type=f32)


def _mx(a, b, kind):
    return _dot16(a.astype(MXU_DTYPE), b.astype(MXU_DTYPE), kind)


def _split2(a):
    hi = a.astype(jnp.bfloat16)
    return hi, (a - hi.astype(f32)).astype(jnp.bfloat16)


def _split3(a):
    p1 = a.astype(jnp.bfloat16)
    r = a - p1.astype(f32)
    p2 = r.astype(jnp.bfloat16)
    return p1, p2, (r - p2.astype(f32)).astype(jnp.bfloat16)


def _h3s(a2, b2, kind):
    (ah, al), (bh, bl) = a2, b2
    return _dot16(ah, bh, kind) + (_dot16(ah, bl, kind) + _dot16(al, bh, kind))


def _h3(a, b, kind):
    return _h3s(_split2(a), _split2(b), kind)


def _exact01(t01, x, kind="nn"):
    t = t01.astype(jnp.bfloat16)
    x1, x2, x3 = _split3(x)
    return _dot16(t, x1, kind) + (_dot16(t, x2, kind) + _dot16(t, x3, kind))


def _colsum_col(e):
    ones = jnp.ones(e.shape[:-1] + (LANES,), jnp.bfloat16)
    e1, e2, e3 = _split3(e)
    return (_dot16(e1, ones, "tn") + (_dot16(e2, ones, "tn") + _dot16(e3, ones, "tn")))[..., 0:1]


def _iota3(shape, axis):
    return lax.broadcasted_iota(jnp.int32, shape, axis)


def _dir_masks(B, nf, Cn):
    shp = (B, Cn, Cn)
    bb, ii, jj = _iota3(shp, 0), _iota3(shp, 1), _iota3(shp, 2)
    fwd = bb < nf
    rev = jnp.logical_not(fwd)
    incl = jnp.logical_or(jnp.logical_and(fwd, ii >= jj), jnp.logical_and(rev, ii <= jj))
    strict = jnp.logical_and(incl, ii != jj)
    return incl, strict, ii == jj


def _last_row(x, nf):
    B, Cn = x.shape[0], x.shape[1]
    fwd = _iota3((B, 1, 1), 0) < nf
    return jnp.where(fwd, x[:, Cn - 1 : Cn, :], x[:, 0:1, :])


def _add_at_last_row(x, val, nf):
    B, Cn = x.shape[0], x.shape[1]
    bb, rr = _iota3((B, Cn, 1), 0), _iota3((B, Cn, 1), 1)
    at = jnp.logical_or(jnp.logical_and(bb < nf, rr == Cn - 1), jnp.logical_and(bb >= nf, rr == 0))
    return x + jnp.where(at, val, 0.0)


def _sigmoid(x):
    return jax.nn.sigmoid(x)


def _silu(x):
    return x * _sigmoid(x)


def _silu_grad(x):
    s = _sigmoid(x)
    return s * (1.0 + x * (1.0 - s))


def _softplus(x):
    u = jnp.exp(-jnp.abs(x))
    l1p = jnp.where(u < 1e-3, u * (1.0 - u * (0.5 - u * (1.0 / 3.0))), jnp.log(1.0 + u))
    return jnp.maximum(x, 0.0) + l1p


def _mxr(x):
    return x.astype(MXU_DTYPE).astype(f32)


def _iota2(shape, axis):
    return lax.broadcasted_iota(jnp.int32, shape, axis)


def _matmul(a, b, kind, out_dtype, name, tm, tn, tk, out_t=False, after=None):
    extra = [] if after is None else [after]
    if kind == "nn":
        (M, K), N = a.shape, b.shape[1]
    elif kind == "nt":
        (M, K), N = a.shape, b.shape[0]
    else:
        (K, M), N = a.shape, b.shape[1]
    tm, tn, tk = min(tm, M), min(tn, N), min(tk, K)
    assert M % tm == 0 and N % tn == 0 and K % tk == 0, (name, M, N, K)
    nk = K // tk
    if kind == "tn":
        a_spec = pl.BlockSpec((tk, tm), lambda i, j, k: (k, i))
    else:
        a_spec = pl.BlockSpec((tm, tk), lambda i, j, k: (i, k))
    b_mode = dict(pipeline_mode=pl.Buffered(1)) if (tn == N and tk == K) else {}
    if kind == "nt":
        b_spec = pl.BlockSpec((tn, tk), lambda i, j, k: (j, k), **b_mode)
    else:
        b_spec = pl.BlockSpec((tk, tn), lambda i, j, k: (k, j), **b_mode)

    def body(a_ref, b_ref, *rest):
        o_ref, acc = rest[len(extra)], rest[len(extra) + 1 :]
        p = _mx(a_ref[...], b_ref[...], kind)
        if nk == 1:
            o_ref[...] = (p.T if out_t else p).astype(out_dtype)
        else:
            assert not out_t
            acc_ref = acc[0]
            k = pl.program_id(2)

            @pl.when(k == 0)
            def _():
                acc_ref[...] = p

            @pl.when(k > 0)
            def _():
                acc_ref[...] += p

            @pl.when(k == nk - 1)
            def _():
                o_ref[...] = acc_ref[...].astype(out_dtype)

    return pl.pallas_call(
        body,
        out_shape=jax.ShapeDtypeStruct((N, M) if out_t else (M, N), out_dtype),
        grid=(M // tm, N // tn, nk),
        in_specs=[a_spec, b_spec] + [pl.BlockSpec((8, LANES), lambda i, j, k: (0, 0))] * len(extra),
        out_specs=pl.BlockSpec((tn, tm), lambda i, j, k: (j, i)) if out_t else pl.BlockSpec((tm, tn), lambda i, j, k: (i, j)),
        scratch_shapes=[] if nk == 1 else [pltpu.VMEM((tm, tn), f32)],
        compiler_params=_cparams(("parallel", "parallel", "arbitrary")),
        name=name,
    )(a, b, *extra)


ROW_TILE = 512


def _rms_pre(x, w):
    T = x.shape[0]
    tb = min(ROW_TILE, T)

    def body(x_ref, w_ref, h_ref, ht_ref):
        xv = x_ref[...]
        r = lax.rsqrt(jnp.mean(xv * xv, axis=1, keepdims=True) + NORM_EPS)
        h = xv * r * w_ref[...]
        h_ref[...] = h.astype(h_ref.dtype)
        ht_ref[...] = h.T.astype(ht_ref.dtype)

    return pl.pallas_call(
        body,
        out_shape=(jax.ShapeDtypeStruct((T, D_MODEL), MXU_DTYPE), jax.ShapeDtypeStruct((D_MODEL, T), MXU_DTYPE)),
        grid=(T // tb,),
        in_specs=[pl.BlockSpec((tb, D_MODEL), lambda i: (i, 0)), pl.BlockSpec((1, D_MODEL), lambda i: (0, 0))],
        out_specs=(pl.BlockSpec((tb, D_MODEL), lambda i: (i, 0)), pl.BlockSpec((D_MODEL, tb), lambda i: (0, i))),
        compiler_params=_cparams(("parallel",)),
        name="rms_pre",
    )(x, w)


def _rms_pre_bwd(x, w, dh, dy):
    T = x.shape[0]
    tb = min(ROW_TILE, T)

    def body(x_ref, w_ref, dh_ref, dy_ref, dx_ref, dw_ref):
        i = pl.program_id(0)
        xv, dhv = x_ref[...], dh_ref[...]
        r = lax.rsqrt(jnp.mean(xv * xv, axis=1, keepdims=True) + NORM_EPS)
        dhw = dhv * w_ref[...]
        dx_ref[...] = dy_ref[...] + r * dhw - xv * (r * r * r) * jnp.mean(dhw * xv, axis=1, keepdims=True)
        part = jnp.sum(dhv * xv * r, axis=0, keepdims=True)

        @pl.when(i == 0)
        def _():
            dw_ref[...] = part

        @pl.when(i > 0)
        def _():
            dw_ref[...] += part

    blk = pl.BlockSpec((tb, D_MODEL), lambda i: (i, 0))
    row = pl.BlockSpec((1, D_MODEL), lambda i: (0, 0))
    return pl.pallas_call(
        body,
        out_shape=(jax.ShapeDtypeStruct((T, D_MODEL), f32), jax.ShapeDtypeStruct((1, D_MODEL), f32)),
        grid=(T // tb,),
        in_specs=[blk, row, blk, blk],
        out_specs=(blk, row),
        compiler_params=_cparams(("arbitrary",)),
        name="rms_pre_bwd",
    )(x, w, dh, dy)


SHIFTS = (-2, -1, 1, 2)


SHIFT_PAD = 8


def _shifted(x, pad_ref):
    T, cb = x.shape
    pad_ref[0:SHIFT_PAD] = jnp.zeros((SHIFT_PAD, cb), f32)
    pad_ref[SHIFT_PAD + T :] = jnp.zeros((SHIFT_PAD, cb), f32)
    pad_ref[SHIFT_PAD : SHIFT_PAD + T] = x
    return {d: pad_ref[SHIFT_PAD + d : SHIFT_PAD + d + T] for d in SHIFTS}


def _conv5(u, ush, cw):
    acc = u * cw[2:3, :]
    for j in (0, 1, 3, 4):
        acc = acc + ush[j - 2] * cw[j : j + 1, :]
    return acc


def _gdn_prep(proj, conv_w):
    T = proj.shape[0]
    nh = GDN_HEADS

    def body(u_ref, cw_ref, o_ref, pad_ref):
        j = pl.program_id(0)
        u = _mxr(u_ref[...])
        s = _silu(_conv5(u, _shifted(u, pad_ref), _mxr(cw_ref[...])))

        @pl.when(j < 2 * nh)
        def _():
            o_ref[0] = s * lax.rsqrt(jnp.sum(s * s, axis=1, keepdims=True) + NORM_EPS)

        @pl.when(j >= 2 * nh)
        def _():
            o_ref[0] = s

    return pl.pallas_call(
        body,
        out_shape=jax.ShapeDtypeStruct((3, T, GDN_HEADS * GDN_DIM), f32),
        grid=(3 * nh,),
        in_specs=[pl.BlockSpec((T, GDN_DIM), lambda j: (0, j)), pl.BlockSpec((CONV_K, GDN_DIM), lambda j: (0, j))],
        out_specs=pl.BlockSpec((1, T, GDN_DIM), lambda j: (j // nh, 0, j % nh)),
        scratch_shapes=[pltpu.VMEM((T + 2 * SHIFT_PAD, GDN_DIM), f32)],
        compiler_params=_cparams(("parallel",)),
        name="gdn_prep",
    )(proj, conv_w)


def _gdn_prep_bwd(proj, conv_w, dqkv_f, dqkv_b, dproj):
    T = proj.shape[0]
    nh = GDN_HEADS


    def body(u_ref, cw_ref, df_ref, db_ref, _, du_ref, dcw_ref, upad_ref, dcpad_ref):
        j = pl.program_id(0)
        u, cw = _mxr(u_ref[...]), _mxr(cw_ref[...])
        ush = _shifted(u, upad_ref)
        c = _conv5(u, ush, cw)
        s = _silu(c)
        dn = df_ref[0] + db_ref[0]
        rinv = lax.rsqrt(jnp.sum(s * s, axis=1, keepdims=True) + NORM_EPS)
        ds_norm = rinv * dn - s * (rinv * rinv * rinv) * jnp.sum(dn * s, axis=1, keepdims=True)
        ds = jnp.where(j < 2 * nh, ds_norm, dn)
        dc = _mxr(ds * _silu_grad(c))
        dcsh = _shifted(dc, dcpad_ref)
        du = dc * cw[2:3, :]
        for jj in range(CONV_K):
            d = jj - 2
            dcw_ref[jj : jj + 1, :] = jnp.sum(dc * (u if d == 0 else ush[d]), axis=0, keepdims=True)
            if d != 0:
                du = du + dcsh[-d] * cw[jj : jj + 1, :]
        du_ref[...] = du.astype(du_ref.dtype)

    qspec = pl.BlockSpec((1, T, GDN_DIM), lambda j: (j // nh, 0, j % nh))
    return pl.pallas_call(
        body,
        out_shape=(jax.ShapeDtypeStruct(dproj.shape, dproj.dtype), jax.ShapeDtypeStruct((CONV_K, 3 * nh * GDN_DIM), f32)),
        grid=(3 * nh,),
        in_specs=[pl.BlockSpec((T, GDN_DIM), lambda j: (0, j)), pl.BlockSpec((CONV_K, GDN_DIM), lambda j: (0, j)), qspec, qspec, _ANY_SPEC],
        out_specs=(pl.BlockSpec((T, GDN_DIM), lambda j: (0, j)), pl.BlockSpec((CONV_K, GDN_DIM), lambda j: (0, j))),
        input_output_aliases={4: 0},
        scratch_shapes=[pltpu.VMEM((T + 2 * SHIFT_PAD, GDN_DIM), f32)] * 2,
        compiler_params=_cparams(("parallel",)),
        name="gdn_prep_bwd",
    )(proj, conv_w, dqkv_f, dqkv_b, dproj)


def _chunk_tri(n, rev):
    i, j = _iota2((n, n), 0), _iota2((n, n), 1)
    same = jnp.right_shift(i, 6) == jnp.right_shift(j, 6)
    order = (j >= i) if rev else (j <= i)
    return jnp.where(jnp.logical_and(same, order), 1.0, 0.0).astype(f32)


def _gate_prep(proj, avec, dvec, w2f, w2b, b2f, b2b, tb=256):
    T = proj.shape[0]
    W = GLA_HEADS * GLA_DK

    def body(s_ref, a_ref, d_ref, wf_ref, wb_ref, bf_ref, bb_ref, g_ref, gc_ref, gf_ref, gb_ref):
        small = s_ref[...]
        tri_f, tri_r = _chunk_tri(tb, False), _chunk_tri(tb, True)
        lane = _iota2(small.shape, 1)
        lg = -jnp.exp(a_ref[...]) * _softplus(small + d_ref[...])
        beta = _sigmoid(small)
        g_ref[...] = jnp.where(lane < 16, lg, jnp.where(lane < 32, beta, 0.0))
        lgm = jnp.where(lane < 16, lg, 0.0)
        gc_ref[...] = jnp.where(lane < 8, _exact01(tri_f, lgm), _exact01(tri_r, lgm))
        gkf = -_softplus(-(_mx(small, wf_ref[...], "nn") + bf_ref[...])) * (1.0 / GATE_NORMALIZER)
        gkb = -_softplus(-(_mx(small, wb_ref[...], "nn") + bb_ref[...])) * (1.0 / GATE_NORMALIZER)
        gf_ref[...] = _exact01(tri_f, gkf)
        gb_ref[...] = _exact01(tri_r, gkb)

    blk = pl.BlockSpec((tb, LANES), lambda i: (i, 0))
    row = pl.BlockSpec((1, LANES), lambda i: (0, 0))
    wide = pl.BlockSpec((tb, W), lambda i: (i, 0))
    wsp = pl.BlockSpec((LANES, W), lambda i: (0, 0))
    wrow = pl.BlockSpec((1, W), lambda i: (0, 0))
    return pl.pallas_call(
        body,
        out_shape=(jax.ShapeDtypeStruct((T, LANES), f32), jax.ShapeDtypeStruct((T, LANES), f32),
                   jax.ShapeDtypeStruct((T, W), f32), jax.ShapeDtypeStruct((T, W), f32)),
        grid=(T // tb,),
        in_specs=[pl.BlockSpec((tb, LANES), lambda i: (i, SMALL_OFF // LANES)), row, row, wsp, wsp, wrow, wrow],
        out_specs=(blk, blk, wide, wide),
        compiler_params=_cparams(("parallel",)),
        name="gate_prep",
    )(proj, avec, dvec, w2f, w2b, b2f, b2b)


def _gdn_gates_bwd(proj, avec, dvec, dg_f, dg_b, dsmall_gla, dproj):
    T = proj.shape[0]
    tb = min(ROW_TILE, T)
    pad = N_CAT - SMALL_OFF

    def body(s_ref, a_ref, d_ref, gf_ref, gb_ref, dl_ref, _, ds_ref, da_ref, dd_ref):
        i = pl.program_id(0)
        small = s_ref[...]
        lane = _iota2(small.shape, 1)
        dgate = gf_ref[...] + gb_ref[...]
        z = small + d_ref[...]
        nega = -jnp.exp(a_ref[...])
        dz = dgate * nega * _sigmoid(z)
        beta = _sigmoid(small)
        dsm = jnp.where(lane < 16, dz, jnp.where(lane < 32, dgate * beta * (1.0 - beta), 0.0))
        ds_ref[:, :LANES] = (dsm + dl_ref[...]).astype(ds_ref.dtype)
        ds_ref[:, LANES:] = jnp.zeros((tb, pad - LANES), ds_ref.dtype)
        lg = nega * _softplus(z)
        pa = jnp.sum(jnp.where(lane < 16, dgate * lg, 0.0), axis=0, keepdims=True)
        pd = jnp.sum(jnp.where(lane < 16, dz, 0.0), axis=0, keepdims=True)

        @pl.when(i == 0)
        def _():
            da_ref[...] = pa
            dd_ref[...] = pd

        @pl.when(i > 0)
        def _():
            da_ref[...] += pa
            dd_ref[...] += pd

    blk = pl.BlockSpec((tb, LANES), lambda i: (i, 0))
    row = pl.BlockSpec((1, LANES), lambda i: (0, 0))
    return pl.pallas_call(
        body,
        out_shape=(jax.ShapeDtypeStruct(dproj.shape, dproj.dtype), jax.ShapeDtypeStruct((1, LANES), f32), jax.ShapeDtypeStruct((1, LANES), f32)),
        grid=(T // tb,),
        in_specs=[pl.BlockSpec((tb, LANES), lambda i: (i, SMALL_OFF // LANES)), row, row, blk, blk, blk, _ANY_SPEC],
        out_specs=(pl.BlockSpec((tb, pad), lambda i: (i, DP_SMALL_BLK)), row, row),
        input_output_aliases={6: 0},
        compiler_params=_cparams(("arbitrary",)),
        name="gdn_gates_bwd",
    )(proj, avec, dvec, dg_f, dg_b, dsmall_gla, dproj)


GDN_HB = 8


def _inv_unit_lower(L, eye):
    A = eye - L
    P2 = _split2(L)
    for _ in range(3):
        P2 = _split2(_h3s(P2, P2, "nn"))
        A = A + _h3s(_split2(A), P2, "nn")
    Ph = P2[0]
    for _ in range(2):
        Ph = _dot16(Ph, Ph, "nn").astype(jnp.bfloat16)
        A = A + _dot16(A.astype(jnp.bfloat16), Ph, "nn")
    return A


def _lane_col(blk, idx):
    lane = _iota2(blk.shape, 1)
    return jnp.sum(jnp.where(lane == idx, blk, 0.0), axis=1, keepdims=True)


def _out_norm(o_f, o_b, proj, gate_blk, w, hd, name):
    T, W = o_f.shape
    tb = min(ROW_TILE, T)
    nh = W // hd

    def body(of_ref, ob_ref, z_ref, w_ref, y_ref):
        wv = w_ref[...]
        for i in range(nh):
            sl = slice(i * hd, (i + 1) * hd)
            o = of_ref[:, sl] + ob_ref[:, sl]
            r = lax.rsqrt(jnp.mean(o * o, axis=1, keepdims=True) + NORM_EPS)
            y_ref[:, sl] = (o * r * wv * _silu(z_ref[:, sl])).astype(y_ref.dtype)

    blk = pl.BlockSpec((tb, W), lambda i: (i, 0))
    return pl.pallas_call(
        body,
        out_shape=jax.ShapeDtypeStruct((T, W), MXU_DTYPE),
        grid=(T // tb,),
        in_specs=[blk, blk, pl.BlockSpec((tb, W), lambda i: (i, gate_blk)), pl.BlockSpec((1, hd), lambda i: (0, 0))],
        out_specs=blk,
        compiler_params=_cparams(("parallel",)),
        name=name,
    )(o_f, o_b, proj, w)


def _out_norm_bwd(dy, o_f, o_b, proj, gate_blk, w, hd, name, dproj):
    T, W = o_f.shape
    tb = min(ROW_TILE, T)
    nh = W // hd

    def body(dy_ref, of_ref, ob_ref, z_ref, w_ref, _, do_ref, dz_ref, dw_ref):
        i = pl.program_id(0)
        wv = w_ref[...]
        dw = jnp.zeros((1, hd), f32)
        for a in range(nh):
            sl = slice(a * hd, (a + 1) * hd)
            o = of_ref[:, sl] + ob_ref[:, sl]
            z = z_ref[:, sl]
            dyv = dy_ref[:, sl]
            r = lax.rsqrt(jnp.mean(o * o, axis=1, keepdims=True) + NORM_EPS)
            orr = o * r
            dn = dyv * _silu(z)
            dz_ref[:, sl] = (dyv * orr * wv * _silu_grad(z)).astype(dz_ref.dtype)
            dw = dw + jnp.sum(dn * orr, axis=0, keepdims=True)
            dnw = dn * wv
            do_ref[:, sl] = r * dnw - o * (r * r * r) * jnp.mean(dnw * o, axis=1, keepdims=True)

        @pl.when(i == 0)
        def _():
            dw_ref[...] = dw

        @pl.when(i > 0)
        def _():
            dw_ref[...] += dw

    blk = pl.BlockSpec((tb, W), lambda i: (i, 0))
    row = pl.BlockSpec((1, hd), lambda i: (0, 0))
    return pl.pallas_call(
        body,
        out_shape=(jax.ShapeDtypeStruct((T, W), f32), jax.ShapeDtypeStruct(dproj.shape, dproj.dtype), jax.ShapeDtypeStruct((1, hd), f32)),
        grid=(T // tb,),
        in_specs=[blk, blk, blk, pl.BlockSpec((tb, W), lambda i: (i, gate_blk)), row, _ANY_SPEC],
        out_specs=(blk, pl.BlockSpec((tb, W), lambda i: (i, gate_blk)), row),
        input_output_aliases={5: 1},
        compiler_params=_cparams(("arbitrary",)),
        name=name,
    )(dy, o_f, o_b, proj, w, dproj)


def _gla_prep_bwd(proj, w2f, w2b, b2f, b2b, dgk_f, dgk_b, dqkv, dproj):
    T = proj.shape[0]
    tb = min(ROW_TILE, T)
    W = GLA_HEADS * GLA_DK
    WV = GLA_HEADS * GLA_DV

    def body(s_ref, wf_ref, wb_ref, bf_ref, bb_ref, df_ref, db_ref, qf, qb, kf, kb, vf, vb, _,
             dp_ref, ds_ref, dwf_ref, dwb_ref, dbf_ref, dbb_ref):
        i = pl.program_id(0)
        dp_ref[:, :W] = (qf[...] + qb[...]).astype(dp_ref.dtype)
        dp_ref[:, W : 2 * W] = (kf[...] + kb[...]).astype(dp_ref.dtype)
        dp_ref[:, 2 * W :] = (vf[...] + vb[...]).astype(dp_ref.dtype)
        small = s_ref[...]
        dsm = jnp.zeros((tb, LANES), f32)
        parts = []
        for w_ref, b_ref, d_ref in ((wf_ref, bf_ref, df_ref), (wb_ref, bb_ref, db_ref)):
            pre = _mx(small, w_ref[...], "nn") + b_ref[...]
            dpre = d_ref[...] * (1.0 / GATE_NORMALIZER) * _sigmoid(-pre)
            dsm = dsm + _mx(dpre, w_ref[...], "nt")
            parts.append((_mx(small, dpre, "tn"), jnp.sum(dpre, axis=0, keepdims=True)))
        ds_ref[...] = dsm

        @pl.when(i == 0)
        def _():
            dwf_ref[...], dbf_ref[...] = parts[0]
            dwb_ref[...], dbb_ref[...] = parts[1]

        @pl.when(i > 0)
        def _():
            dwf_ref[...] += parts[0][0]
            dbf_ref[...] += parts[0][1]
            dwb_ref[...] += parts[1][0]
            dbb_ref[...] += parts[1][1]

    blk = pl.BlockSpec((tb, W), lambda i: (i, 0))
    vblk = pl.BlockSpec((tb, WV), lambda i: (i, 0))
    wsp = pl.BlockSpec((LANES, W), lambda i: (0, 0))
    row = pl.BlockSpec((1, W), lambda i: (0, 0))
    return pl.pallas_call(
        body,
        out_shape=(jax.ShapeDtypeStruct(dproj.shape, dproj.dtype), jax.ShapeDtypeStruct((T, LANES), f32), jax.ShapeDtypeStruct((LANES, W), f32),
                   jax.ShapeDtypeStruct((LANES, W), f32), jax.ShapeDtypeStruct((1, W), f32), jax.ShapeDtypeStruct((1, W), f32)),
        grid=(T // tb,),
        in_specs=[pl.BlockSpec((tb, LANES), lambda i: (i, SMALL_OFF // LANES)), wsp, wsp, row, row, blk, blk,
                  blk, blk, blk, blk, vblk, vblk, _ANY_SPEC],
        out_specs=(pl.BlockSpec((tb, 2 * W + WV), lambda i: (i, DP_GLA_QKV_BLK)), pl.BlockSpec((tb, LANES), lambda i: (i, 0)), wsp, wsp, row, row),
        input_output_aliases={13: 0},
        compiler_params=_cparams(("arbitrary",)),
        name="gla_prep_bwd",
    )(proj, w2f, w2b, b2f, b2b, dgk_f, dgk_b, *dqkv, dproj)


GATE_A_BLK, GATE_B_BLK, Z_BLK, G_B_BLK = 6144 // 1024, 7168 // 1024, 3072 // 1024, 8192 // 1024
DP_GATES_BLK, DP_GLA_QKV_BLK, DP_SMALL_BLK = 6144 // 2048, 4096 // 2048, SMALL_OFF // 512
_ANY_SPEC = pl.BlockSpec(memory_space=pl.ANY)


def _mid(og, obg, proj, x, target, wpg, wpl, wout, w_post, tb=256):
    T = x.shape[0]
    Dm = D_MODEL

    def body(og_ref, obg_ref, ga_ref, gb_ref, x_ref, t_ref, wpg_ref, wpl_ref, wo_ref, wp_ref,
             loss_ref, dwp_ref, dy_ref, m_ref, dout_ref, dya_ref, dyb_ref, dgates_ref, dog_ref, dobg_ref):
        i = pl.program_id(0)
        ya = _mx(og_ref[...], wpg_ref[...], "nn")
        yb = _mx(obg_ref[...], wpl_ref[...], "nn")
        sa, sb = _sigmoid(ga_ref[...]), _sigmoid(gb_ref[...])
        m = sa * ya + sb * yb
        m_ref[...] = m.astype(m_ref.dtype)
        out = _mx(m, wo_ref[...], "nn")
        r = lax.rsqrt(jnp.mean(out * out, axis=1, keepdims=True) + NORM_EPS)
        wp = wp_ref[...]
        e = x_ref[...] + out * r * wp - t_ref[...]
        dy = e * (1.0 / Dm)
        dy_ref[...] = dy
        lpart = jnp.sum(e * e, axis=0, keepdims=True)
        wpart = jnp.sum(dy * out * r, axis=0, keepdims=True)
        dyn = dy * wp
        dout = r * dyn - out * (r * r * r) * jnp.mean(dyn * out, axis=1, keepdims=True)
        dout_ref[...] = dout.astype(dout_ref.dtype)
        dm = _mx(dout, wo_ref[...], "nt")
        dya, dyb = dm * sa, dm * sb
        dya_ref[...] = dya.astype(dya_ref.dtype)
        dyb_ref[...] = dyb.astype(dyb_ref.dtype)
        dgates_ref[:, :Dm] = (dm * ya * sa * (1.0 - sa)).astype(dgates_ref.dtype)
        dgates_ref[:, Dm:] = (dm * yb * sb * (1.0 - sb)).astype(dgates_ref.dtype)
        dog_ref[...] = _mx(dya, wpg_ref[...], "nt")
        dobg_ref[...] = _mx(dyb, wpl_ref[...], "nt")

        @pl.when(i == 0)
        def _():
            loss_ref[...] = lpart
            dwp_ref[...] = wpart

        @pl.when(i > 0)
        def _():
            loss_ref[...] += lpart
            dwp_ref[...] += wpart

    blk = pl.BlockSpec((tb, Dm), lambda i: (i, 0))
    row = pl.BlockSpec((1, Dm), lambda i: (0, 0))
    wsp = pl.BlockSpec((Dm, Dm), lambda i: (0, 0))
    act = jax.ShapeDtypeStruct((T, Dm), MXU_DTYPE)
    big = jax.ShapeDtypeStruct((T, Dm), f32)
    vec = jax.ShapeDtypeStruct((1, Dm), f32)
    return pl.pallas_call(
        body,
        out_shape=(vec, vec, big, act, act, act, act, jax.ShapeDtypeStruct((T, N_CAT), MXU_DTYPE), big, big),
        grid=(T // tb,),
        in_specs=[blk, blk, pl.BlockSpec((tb, Dm), lambda i: (i, GATE_A_BLK)), pl.BlockSpec((tb, Dm), lambda i: (i, GATE_B_BLK)),
                  blk, blk, wsp, wsp, wsp, row],
        out_specs=(row, row) + (blk,) * 5 + (pl.BlockSpec((tb, 2 * Dm), lambda i: (i, DP_GATES_BLK)), blk, blk),
        compiler_params=_cparams(("arbitrary",)),
        name="mid",
    )(og, obg, proj, proj, x, target, wpg, wpl, wout, w_post)


def _heads(ref, lead, n, width, rows=slice(None)):
    return [ref[lead + (rows, slice(i * width, (i + 1) * width))] for i in range(n)]


class _GdnBatch:
    def __init__(self, q, k, v, gcol, grow, beta, nf, A=None):
        B, Cn = q.shape[0], q.shape[1]
        self.incl, self.strict, eye = _dir_masks(B, nf, Cn)
        self.decay = jnp.where(self.incl, jnp.exp(jnp.where(self.incl, gcol - grow, 0.0)), 0.0)
        self.k, self.v = k, v
        gl = _last_row(gcol, nf)
        self.beta = jnp.broadcast_to(beta, q.shape)
        self.eg = jnp.broadcast_to(jnp.exp(gcol), q.shape)
        self.ekd = jnp.broadcast_to(jnp.exp(gl - gcol), q.shape)
        self.qs = q * (GDN_DIM**-0.5)
        self.kb = k * self.beta
        self.L = jnp.where(self.strict, _mx(self.kb, k, "nt") * self.decay, 0.0)
        if A is None:
            A = _inv_unit_lower(self.L, jnp.where(eye, 1.0, 0.0).astype(f32))
        self.A = A
        self.A2 = _split2(A)
        self.kbg = self.kb * self.eg
        self.sol = _h3s(self.A2, _split2(jnp.concatenate([v * self.beta, self.kbg], axis=2)), "nn")
        self.u = self.sol[:, :, :GDN_DIM]
        self.w = self.sol[:, :, GDN_DIM:]
        self.attn = jnp.where(self.incl, _mx(self.qs, k, "nt") * self.decay, 0.0)
        self.qd = self.qs * self.eg
        self.kd = k * self.ekd
        self.egl = jnp.exp(gl)


def _gdn_chain_inputs(qf_ref, qb_ref, gf_ref, gb_ref, cf_ref, cb_ref, b, hb):
    qkv = [jnp.stack(_heads(qf_ref, (i,), hb, GDN_DIM) + _heads(qb_ref, (i,), hb, GDN_DIM)) for i in range(3)]
    cols, rows, betas = [], [], []
    for d, (g_ref, c_ref) in enumerate(((gf_ref, cf_ref), (gb_ref, cb_ref))):
        gates, gc = g_ref[...], c_ref[...]
        for hh in range(hb):
            h = b * hb + hh
            col = _lane_col(gc, h + 8 * d)
            cols.append(col)
            rows.append(jnp.broadcast_to(col, (CHUNK, LANES)).T[0:1, :])
            betas.append(_lane_col(gates, 16 + h + 8 * d))
    return qkv[0], qkv[1], qkv[2], jnp.stack(cols), jnp.stack(rows), jnp.stack(betas)


GDN_CPS_FWD, GDN_CPS_BWD = 4, 1


def _chunk_views(refs, s_f, s_b):
    out = []
    for i, r in enumerate(refs):
        s = s_f if i % 2 == 0 else s_b
        rows = pl.ds(s * CHUNK, CHUNK)
        out.append(r.at[pl.ds(s, 1)] if len(r.shape) == 4 else (r.at[:, rows, :] if len(r.shape) == 3 else r.at[rows, :]))
    return out


def _gdn_specs(cps, hb):
    R, W = cps * CHUNK, hb * GDN_DIM
    qsp = lambda ci: pl.BlockSpec((3, R, W), lambda n, b: (0, ci(n), b))
    gsp = lambda ci: pl.BlockSpec((R, LANES), lambda n, b: (ci(n), 0))
    osp = lambda ci: pl.BlockSpec((R, W), lambda n, b: (ci(n), b))
    ssp = lambda ci: pl.BlockSpec((cps, hb, GDN_DIM, GDN_DIM), lambda n, b: (ci(n), b, 0, 0))
    asp = lambda ci: pl.BlockSpec((cps, hb, CHUNK, CHUNK), lambda n, b: (ci(n), b, 0, 0))
    return qsp, gsp, osp, ssp, asp


def _gdn_fwd(qkvn, gates, gc):
    T = qkvn.shape[1]
    N = T // CHUNK
    nh, hb = GDN_HEADS, GDN_HB
    cps = min(GDN_CPS_FWD, N)
    NB = N // cps

    def body(*refs):
        n, b = pl.program_id(0), pl.program_id(1)
        s_scr = refs[-1]

        @pl.when(n == 0)
        def _():
            s_scr[0, pl.ds(b * hb, hb)] = jnp.zeros((hb, GDN_DIM, GDN_DIM), f32)
            s_scr[1, pl.ds(b * hb, hb)] = jnp.zeros((hb, GDN_DIM, GDN_DIM), f32)

        for s in range(cps):
            chunk(*_chunk_views(refs[:-1], s, cps - 1 - s), s_scr, b)

    def chunk(qf_ref, qb_ref, gf_ref, gb_ref, cf_ref, cb_ref, of_ref, ob_ref, sf_ref, sb_ref, af_ref, ab_ref, s_scr, b):
        hs = pl.ds(b * hb, hb)
        q, k, v, gcol, grow, beta = _gdn_chain_inputs(qf_ref, qb_ref, gf_ref, gb_ref, cf_ref, cb_ref, b, hb)
        ck = _GdnBatch(q, k, v, gcol, grow, beta, hb)
        S = jnp.concatenate([s_scr[0, hs], s_scr[1, hs]], axis=0)
        sf_ref[0], sb_ref[0] = S[:hb], S[hb:]
        af_ref[0], ab_ref[0] = ck.A[:hb], ck.A[hb:]
        v_new = ck.u - _mx(ck.w, S, "nn")
        o = _mx(ck.qd, S, "nn") + _mx(ck.attn, v_new, "nn")
        S_new = S * ck.egl + _mx(ck.kd, v_new, "tn")
        for hh in range(hb):
            sl = slice(hh * GDN_DIM, (hh + 1) * GDN_DIM)
            of_ref[:, sl] = o[hh]
            ob_ref[:, sl] = o[hb + hh]
        s_scr[0, hs] = S_new[:hb]
        s_scr[1, hs] = S_new[hb:]

    fw, bw = (lambda n: n), (lambda n: NB - 1 - n)
    qsp, gsp, osp, ssp, asp = _gdn_specs(cps, hb)
    o_shape = jax.ShapeDtypeStruct((T, nh * GDN_DIM), f32)
    s_shape = jax.ShapeDtypeStruct((N, nh, GDN_DIM, GDN_DIM), f32)
    a_shape = jax.ShapeDtypeStruct((N, nh, CHUNK, CHUNK), f32)
    return pl.pallas_call(
        body,
        out_shape=(o_shape, o_shape, s_shape, s_shape, a_shape, a_shape),
        grid=(NB, nh // hb),
        in_specs=[qsp(fw), qsp(bw), gsp(fw), gsp(bw), gsp(fw), gsp(bw)],
        out_specs=(osp(fw), osp(bw), ssp(fw), ssp(bw), asp(fw), asp(bw)),
        scratch_shapes=[pltpu.VMEM((2, nh, GDN_DIM, GDN_DIM), f32)],
        compiler_params=_cparams(("arbitrary", "arbitrary")),
        name="gdn_scan_fwd",
    )(qkvn, qkvn, gates, gates, gc, gc)


def _gdn_bwd(qkvn, gates, gc, st_f, st_b, a_f, a_b, do):
    T = qkvn.shape[1]
    N = T // CHUNK
    nh, hb = GDN_HEADS, GDN_HB
    nb = nh // hb
    cps = min(GDN_CPS_BWD, N)
    NB = N // cps

    def body(*refs):
        n, b = pl.program_id(0), pl.program_id(1)
        ds_scr = refs[-1]

        @pl.when(n == 0)
        def _():
            ds_scr[0, pl.ds(b * hb, hb)] = jnp.zeros((hb, GDN_DIM, GDN_DIM), f32)
            ds_scr[1, pl.ds(b * hb, hb)] = jnp.zeros((hb, GDN_DIM, GDN_DIM), f32)

        for s in range(cps):
            chunk(*_chunk_views(refs[:-1], cps - 1 - s, s), ds_scr, b)

    def chunk(qf_ref, qb_ref, gf_ref, gb_ref, cf_ref, cb_ref, sf_ref, sb_ref, af_ref, ab_ref, dof_ref, dob_ref,
              dqf_ref, dqb_ref, dgf_ref, dgb_ref, ds_scr, b):
        hs = pl.ds(b * hb, hb)

        @pl.when(b == 0)
        def _():
            dgf_ref[...] = jnp.zeros((CHUNK, LANES), f32)
            dgb_ref[...] = jnp.zeros((CHUNK, LANES), f32)

        q, k, v, gcol, grow, beta = _gdn_chain_inputs(qf_ref, qb_ref, gf_ref, gb_ref, cf_ref, cb_ref, b, hb)
        ck = _GdnBatch(q, k, v, gcol, grow, beta, hb, A=jnp.concatenate([af_ref[0], ab_ref[0]], axis=0))
        S = jnp.concatenate([sf_ref[0], sb_ref[0]], axis=0)
        dS = jnp.concatenate([ds_scr[0, hs], ds_scr[1, hs]], axis=0)
        dov = jnp.stack(_heads(dof_ref, (), hb, GDN_DIM) + _heads(dob_ref, (), hb, GDN_DIM))
        v_new = ck.u - _mx(ck.w, S, "nn")
        d_vnew = _mx(ck.attn, dov, "tn") + _mx(ck.kd, dS, "nn")
        d_attn = jnp.where(ck.incl, _mx(dov, v_new, "nt"), 0.0)
        d_qd = _mx(dov, S, "nt")
        d_kd = _mx(v_new, dS, "nt")
        d_gl = jnp.sum(jnp.sum(dS * S, axis=2, keepdims=True), axis=1, keepdims=True) * ck.egl
        dS_new = dS * ck.egl + _mx(ck.qd, dov, "tn") - _mx(ck.w, d_vnew, "tn")
        ds_scr[0, hs] = dS_new[:hb]
        ds_scr[1, hs] = dS_new[hb:]
        d_w = -_mx(d_vnew, S, "nt")
        d_rhs = _h3s(ck.A2, _split2(jnp.concatenate([d_vnew, d_w], axis=2)), "tn")
        d_vb, d_kbg = d_rhs[:, :, :GDN_DIM], d_rhs[:, :, GDN_DIM:]
        dL = -jnp.where(ck.strict, _h3(d_rhs, ck.sol, "nt"), 0.0)
        P = dL * ck.decay
        Q = d_attn * ck.decay
        d_kb = _mx(P, k, "nn") + d_kbg * ck.eg
        d_qs = _mx(Q, k, "nn") + d_qd * ck.eg
        d_k = _mx(P, ck.kb, "tn") + _mx(Q, ck.qs, "tn") + d_kd * ck.ekd + d_kb * ck.beta
        E = dL * ck.L + d_attn * ck.attn
        kdsum = jnp.sum(d_kd * ck.kd, axis=2, keepdims=True)
        d_gc = (jnp.sum(E, axis=2, keepdims=True) - _colsum_col(E)
                + jnp.sum(d_kbg * ck.kbg, axis=2, keepdims=True)
                + jnp.sum(d_qd * ck.qd, axis=2, keepdims=True) - kdsum)
        d_gc = _add_at_last_row(d_gc, d_gl + jnp.sum(kdsum, axis=1, keepdims=True), hb)
        d_beta = jnp.sum(d_vb * v, axis=2, keepdims=True) + jnp.sum(d_kb * k, axis=2, keepdims=True)
        d_q = d_qs * (GDN_DIM**-0.5)
        d_v = d_vb * ck.beta
        lane = _iota2((CHUNK, LANES), 1)
        for d, (dq_ref, dg_ref) in enumerate(((dqf_ref, dgf_ref), (dqb_ref, dgb_ref))):
            acc = dg_ref[...]
            for hh in range(hb):
                c, h = d * hb + hh, b * hb + hh
                sl = slice(hh * GDN_DIM, (hh + 1) * GDN_DIM)
                dq_ref[0, :, sl] = d_q[c]
                dq_ref[1, :, sl] = d_k[c]
                dq_ref[2, :, sl] = d_v[c]
                acc = jnp.where(lane == h + 8 * d, d_gc[c], acc)
                acc = jnp.where(lane == 16 + h + 8 * d, d_beta[c], acc)
            dg_ref[...] = acc

        @pl.when(b == nb - 1)
        def _():
            for d, dg_ref in enumerate((dgf_ref, dgb_ref)):
                blk = dg_ref[...]
                dg_ref[...] = jnp.where(lane < 16, _exact01(_chunk_tri(CHUNK, d == 0), blk), blk)

    fw, bw = (lambda n: NB - 1 - n), (lambda n: n)
    qsp, gsp, osp, ssp, asp = _gdn_specs(cps, hb)
    dq_shape = jax.ShapeDtypeStruct((3, T, nh * GDN_DIM), f32)
    dg_shape = jax.ShapeDtypeStruct((T, LANES), f32)
    return pl.pallas_call(
        body,
        out_shape=(dq_shape, dq_shape, dg_shape, dg_shape),
        grid=(NB, nb),
        in_specs=[qsp(fw), qsp(bw), gsp(fw), gsp(bw), gsp(fw), gsp(bw), ssp(fw), ssp(bw), asp(fw), asp(bw), osp(fw), osp(bw)],
        out_specs=(qsp(fw), qsp(bw), gsp(fw), gsp(bw)),
        scratch_shapes=[pltpu.VMEM((2, nh, GDN_DIM, GDN_DIM), f32)],
        compiler_params=_cparams(("arbitrary", "arbitrary")),
        name="gdn_scan_bwd",
    )(qkvn, qkvn, gates, gates, gc, gc, st_f, st_b, a_f, a_b, do, do)


class _GlaBatch:
    def __init__(self, q, k, G, nf):
        B, Cn = q.shape[0], q.shape[1]
        self.incl, _, _ = _dir_masks(B, nf, Cn)
        self.eG = jnp.exp(G)
        self.enG = jnp.exp(-G)
        self.qg = q * (GLA_DK**-0.5) * self.eG
        self.kg = k * self.enG
        self.attn = jnp.where(self.incl, _mx(self.qg, self.kg, "nt"), 0.0)
        gl = _last_row(G, nf)
        self.ekd = jnp.exp(gl - G)
        self.kd = k * self.ekd
        self.egl = jnp.exp(gl)


GLA_CPS = 4


def _chunk_rows(s):
    return slice(s * CHUNK, (s + 1) * CHUNK)


def _gla_chain_inputs(refs_f, refs_b, rows_f, rows_b):
    nh = GLA_HEADS
    out = []
    for i, width in enumerate((GLA_DK, GLA_DK, GLA_DV, GLA_DK)):
        out.append(jnp.stack(_heads(refs_f[i], (), nh, width, rows_f) + _heads(refs_b[i], (), nh, width, rows_b)))
    return out


def _gla_specs(rows, ci):
    nh = GLA_HEADS
    return [
        pl.BlockSpec((rows, nh * GLA_DK), lambda n: (ci(n), 4096 // (nh * GLA_DK))),
        pl.BlockSpec((rows, nh * GLA_DK), lambda n: (ci(n), 4608 // (nh * GLA_DK))),
        pl.BlockSpec((rows, nh * GLA_DV), lambda n: (ci(n), 5120 // (nh * GLA_DV))),
        pl.BlockSpec((rows, nh * GLA_DK), lambda n: (ci(n), 0)),
    ]


def _gla_fwd(proj, G_f, G_b):
    T = proj.shape[0]
    N = T // CHUNK
    nh = GLA_HEADS
    cps = min(GLA_CPS, N)
    NB = N // cps

    def body(qf, kf, vf, gf, qb, kb, vb, gb, of_ref, ob_ref, sf_ref, sb_ref, s_scr):
        n = pl.program_id(0)

        @pl.when(n == 0)
        def _():
            s_scr[...] = jnp.zeros((2 * nh, GLA_DV, GLA_DK), f32)

        for s in range(cps):
            sb = cps - 1 - s
            q, k, v, G = _gla_chain_inputs((qf, kf, vf, gf), (qb, kb, vb, gb), _chunk_rows(s), _chunk_rows(sb))
            ck = _GlaBatch(q, k, G, nh)
            St = s_scr[...]
            sf_ref[s], sb_ref[sb] = St[:nh].astype(sf_ref.dtype), St[nh:].astype(sb_ref.dtype)
            o = _mx(ck.qg, St, "nt") + _mx(ck.attn, v, "nn")
            for i in range(nh):
                sl = slice(i * GLA_DV, (i + 1) * GLA_DV)
                of_ref[_chunk_rows(s), sl] = o[i]
                ob_ref[_chunk_rows(sb), sl] = o[nh + i]
            s_scr[...] = St * ck.egl + _mx(v, ck.kd, "tn")

    fw, bw = (lambda n: n), (lambda n: NB - 1 - n)
    osp = lambda ci: pl.BlockSpec((cps * CHUNK, nh * GLA_DV), lambda n: (ci(n), 0))
    ssp = lambda ci: pl.BlockSpec((cps, nh, GLA_DV, GLA_DK), lambda n: (ci(n), 0, 0, 0))
    o_shape = jax.ShapeDtypeStruct((T, nh * GLA_DV), f32)
    s_shape = jax.ShapeDtypeStruct((N, nh, GLA_DV, GLA_DK), MXU_DTYPE)
    return pl.pallas_call(
        body,
        out_shape=(o_shape, o_shape, s_shape, s_shape),
        grid=(NB,),
        in_specs=_gla_specs(cps * CHUNK, fw) + _gla_specs(cps * CHUNK, bw),
        out_specs=(osp(fw), osp(bw), ssp(fw), ssp(bw)),
        scratch_shapes=[pltpu.VMEM((2 * nh, GLA_DV, GLA_DK), f32)],
        compiler_params=_cparams(("arbitrary",)),
        name="gla_scan_fwd",
    )(proj, proj, proj, G_f, proj, proj, proj, G_b)


def _gla_bwd(proj, G_f, G_b, st_f, st_b, do):
    T = proj.shape[0]
    N = T // CHUNK
    nh = GLA_HEADS
    cps = min(GLA_CPS, N)
    NB = N // cps

    def body(qf, kf, vf, gf, sf_ref, dof_ref, qb, kb, vb, gb, sb_ref, dob_ref,
             dqf_ref, dkf_ref, dvf_ref, dgf_ref, dqb_ref, dkb_ref, dvb_ref, dgb_ref, ds_scr):
        n = pl.program_id(0)

        @pl.when(n == 0)
        def _():
            ds_scr[...] = jnp.zeros((2 * nh, GLA_DV, GLA_DK), f32)

        shp = (2 * nh, CHUNK, CHUNK)
        bb, ii, jj = _iota3(shp, 0), _iota3(shp, 1), _iota3(shp, 2)
        tri = jnp.where(jnp.logical_or(jnp.logical_and(bb < nh, jj >= ii), jnp.logical_and(bb >= nh, jj <= ii)), 1.0, 0.0)
        for s in range(cps):
            sf = cps - 1 - s
            rows = (_chunk_rows(sf), _chunk_rows(s))
            q, k, v, G = _gla_chain_inputs((qf, kf, vf, gf), (qb, kb, vb, gb), *rows)
            ck = _GlaBatch(q, k, G, nh)
            St = jnp.concatenate([sf_ref[sf], sb_ref[s]], axis=0).astype(f32)
            dSt = ds_scr[...]
            dov = jnp.stack(_heads(dof_ref, (), nh, GLA_DV, rows[0]) + _heads(dob_ref, (), nh, GLA_DV, rows[1]))
            d_attn = jnp.where(ck.incl, _mx(dov, v, "nt"), 0.0)
            d_qg = _mx(dov, St, "nn") + _mx(d_attn, ck.kg, "nn")
            d_kg = _mx(d_attn, ck.qg, "tn")
            d_v = _mx(ck.attn, dov, "tn") + _mx(ck.kd, dSt, "nt")
            d_kd = _mx(v, dSt, "nn")
            ds_scr[...] = dSt * ck.egl + _mx(dov, ck.qg, "tn")
            kdd = d_kd * ck.kd
            d_gl = jnp.sum(dSt * St, axis=1, keepdims=True) * ck.egl + jnp.sum(kdd, axis=1, keepdims=True)
            d_q = d_qg * ck.eG * (GLA_DK**-0.5)
            d_k = d_kg * ck.enG + d_kd * ck.ekd
            d_gk = _exact01(tri, _add_at_last_row(d_qg * ck.qg - d_kg * ck.kg - kdd, d_gl, nh))
            for d, (dq_ref, dk_ref, dv_ref, dg_ref) in enumerate(((dqf_ref, dkf_ref, dvf_ref, dgf_ref), (dqb_ref, dkb_ref, dvb_ref, dgb_ref))):
                for i in range(nh):
                    c = d * nh + i
                    ks, vs = slice(i * GLA_DK, (i + 1) * GLA_DK), slice(i * GLA_DV, (i + 1) * GLA_DV)
                    dq_ref[rows[d], ks] = d_q[c]
                    dk_ref[rows[d], ks] = d_k[c]
                    dv_ref[rows[d], vs] = d_v[c]
                    dg_ref[rows[d], ks] = d_gk[c]

    R = cps * CHUNK

    def in_specs(ci):
        return _gla_specs(R, ci) + [
            pl.BlockSpec((cps, nh, GLA_DV, GLA_DK), lambda n: (ci(n), 0, 0, 0)),
            pl.BlockSpec((R, nh * GLA_DV), lambda n: (ci(n), 0)),
        ]

    def out_specs(ci):
        ksp = pl.BlockSpec((R, nh * GLA_DK), lambda n: (ci(n), 0))
        return [ksp, ksp, pl.BlockSpec((R, nh * GLA_DV), lambda n: (ci(n), 0)), ksp]

    fw, bw = (lambda n: NB - 1 - n), (lambda n: n)
    k_shape = jax.ShapeDtypeStruct((T, nh * GLA_DK), f32)
    v_shape = jax.ShapeDtypeStruct((T, nh * GLA_DV), f32)
    return pl.pallas_call(
        body,
        out_shape=(k_shape, k_shape, v_shape, k_shape) * 2,
        grid=(NB,),
        in_specs=in_specs(fw) + in_specs(bw),
        out_specs=tuple(out_specs(fw) + out_specs(bw)),
        scratch_shapes=[pltpu.VMEM((2 * nh, GLA_DV, GLA_DK), f32)],
        compiler_params=_cparams(("arbitrary",)),
        name="gla_scan_bwd",
    )(proj, proj, proj, G_f, st_f, do, proj, proj, proj, G_b, st_b, do)


def _local_step(x, target, wcat, conv_w, avec, dvec, gdn_norm_w, w2f, w2b, b2f, b2b, gla_norm_w, proj_weights, w_pre, w_post,
                send_proj_grads=None, send_in_grads=None):
    h, h_t = _rms_pre(x, w_pre)
    proj = _matmul(h, wcat, "nt", f32, "proj_in", 2048, 512, 1024)
    qkvn = _gdn_prep(proj, conv_w)
    gates, gc, G_f, G_b = _gate_prep(proj, avec, dvec, w2f, w2b, b2f, b2b)
    oa_f, oa_b, sa_f, sa_b, inv_f, inv_b = _gdn_fwd(qkvn, gates, gc)
    og = _out_norm(oa_f, oa_b, proj, Z_BLK, gdn_norm_w, GDN_DIM, "gdn_out")
    ob_f, ob_b, sb_f, sb_b = _gla_fwd(proj, G_f, G_b)
    obg = _out_norm(ob_f, ob_b, proj, G_B_BLK, gla_norm_w, GLA_DV, "gla_out")
    wpg, wpl, wout = proj_weights(obg)
    (loss_row, d_wpost, dy, m, dout, dya, dyb, dproj, dog, dobg) = _mid(og, obg, proj, x, target, wpg, wpl, wout, w_post)

    d_wout = _matmul(m, dout, "tn", MXU_DTYPE, "dw_out", 1024, 512, 2048)
    d_wpg = _matmul(og, dya, "tn", MXU_DTYPE, "dw_proj_gdn", 1024, 512, 2048)
    d_wpl = _matmul(obg, dyb, "tn", MXU_DTYPE, "dw_proj_gla", 1024, 512, 2048)
    if send_proj_grads is not None:
        gla_norm_w = gla_norm_w + send_proj_grads(d_wpg, d_wpl, d_wout)

    do_b, dproj, d_gla_norm = _out_norm_bwd(dobg, ob_f, ob_b, proj, G_B_BLK, gla_norm_w, GLA_DV, "gla_out_bwd", dproj)
    dq_f, dk_f, dv_f, dgk_f, dq_b, dk_b, dv_b, dgk_b = _gla_bwd(proj, G_f, G_b, sb_f, sb_b, do_b)
    dproj, dsmall_gla, d_w2f, d_w2b, d_b2f, d_b2b = _gla_prep_bwd(proj, w2f, w2b, b2f, b2b, dgk_f, dgk_b,
                                                                 (dq_f, dq_b, dk_f, dk_b, dv_f, dv_b), dproj)

    do_a, dproj, d_gdn_norm = _out_norm_bwd(dog, oa_f, oa_b, proj, Z_BLK, gdn_norm_w, GDN_DIM, "gdn_out_bwd", dproj)
    dqkv_f, dqkv_b, dg_f, dg_b = _gdn_bwd(qkvn, gates, gc, sa_f, sa_b, inv_f, inv_b, do_a)
    dproj, d_conv = _gdn_prep_bwd(proj, conv_w, dqkv_f, dqkv_b, dproj)
    dproj, d_alog, d_dtb = _gdn_gates_bwd(proj, avec, dvec, dg_f, dg_b, dsmall_gla, dproj)

    d_wcat = _matmul(h_t, dproj, "nn", MXU_DTYPE, "dw_in", 1024, 512, 2048, out_t=True)
    started = None if send_in_grads is None else send_in_grads(d_wcat)
    dh = _matmul(dproj, wcat, "nn", f32, "dh", 512, 1024, N_CAT, after=started)
    grad_x, d_wpre = _rms_pre_bwd(x, w_pre, dh, dy)
    return dict(loss_row=loss_row, grad_x=grad_x, d_wcat=d_wcat, d_conv=d_conv, d_alog=d_alog, d_dtb=d_dtb,
                d_gdn_norm=d_gdn_norm, d_wpg=d_wpg, d_w2f=d_w2f, d_w2b=d_w2b, d_b2f=d_b2f, d_b2b=d_b2b,
                d_gla_norm=d_gla_norm, d_wpl=d_wpl, d_wout=d_wout, d_wpost=d_wpost, d_wpre=d_wpre)


def _to_cat(wf):
    pad = jnp.zeros((N_CAT - N_IN,) + wf.shape[1:], wf.dtype)
    return jnp.concatenate([wf[:4096], wf[4128:6176], wf[7232:9280], wf[6176:7200], wf[4096:4128], wf[7200:7232], pad], axis=0)


def _from_cat(wc):
    return jnp.concatenate([wc[:4096], wc[9216:9248], wc[4096:6144], wc[8192:9216], wc[9248:9280], wc[6144:8192]], axis=0)


def _lane_row(*pieces):
    row = jnp.concatenate(pieces, axis=1)
    return jnp.pad(row, ((0, 0), (0, LANES - row.shape[1])))


def _pad_w2(w2, row0):
    return jnp.pad(w2, ((row0, LANES - row0 - w2.shape[0]), (0, 0)))


SM_ROWS, SM_COLS = 32, 384


def _pack_small_shard(conv, w2f, w2b):
    lead = conv.shape[:-2]
    z = lambda r, c: jnp.zeros(lead + (r, c), f32)
    top = jnp.concatenate([conv, z(8 - CONV_K, SM_COLS)], axis=-2)
    mid = jnp.concatenate([w2f, w2b, z(16, SM_COLS - 128)], axis=-1)
    return jnp.concatenate([top, mid, z(SM_ROWS - 24, SM_COLS)], axis=-2)


SMALL_REGION = {
    "ln_pre_w": (0, 1, 0, 1024), "ln_post_w": (1, 1, 0, 1024), "gk_b2_fwd": (2, 1, 0, 512), "gk_b2_bwd": (2, 1, 512, 512),
    "gla_norm_w": (3, 1, 0, 256), "gdn_norm_w": (3, 1, 256, 128), "a_log_fwd": (4, 1, 0, 8), "a_log_bwd": (5, 1, 0, 8),
    "dt_bias_fwd": (6, 1, 0, 8), "dt_bias_bwd": (7, 1, 0, 8), "conv_w": (8, CONV_K, 0, SM_COLS),
    "gk_w2_fwd": (16, 16, 0, 64), "gk_w2_bwd": (16, 16, 128, 64),
}
SMALL_NAMES = tuple(SMALL_REGION)
SMALL_SHARDED = ("conv_w", "gk_w2_fwd", "gk_w2_bwd")
PACK_ROWS, PACK_COLS = 32, 1024
LOSS_AT = (13, 0)


def _region(name):
    r0, nr, l0, nl = SMALL_REGION[name]
    return slice(r0, r0 + nr), slice(l0, l0 + nl)


def _pack_small_grads(G):
    names = ("d_wpre", "d_wpost", "d_b2f", "d_b2b", "d_gla_norm", "d_gdn_norm", "d_alog", "d_dtb", "loss_row", "d_conv", "d_w2f", "d_w2b")

    def body(wpre, wpost, b2f, b2b, glan, gdnn, alog, dtb, lrow, conv, w2f, w2b, o_ref):
        o_ref[...] = jnp.zeros(o_ref.shape, f32)
        loss = 0.5 * jnp.sum(lrow[...], axis=1, keepdims=True) * (1.0 / D_MODEL)
        whole = {"ln_pre_w": wpre, "ln_post_w": wpost, "gk_b2_fwd": b2f, "gk_b2_bwd": b2b, "gla_norm_w": glan, "gdn_norm_w": gdnn}
        for j in range(N_DEV):
            for name, ref in whole.items():
                o_ref[(j,) + _region(name)] = ref[...]
            o_ref[(j,) + _region("a_log_fwd")] = alog[0:1, 0:8]
            o_ref[(j,) + _region("a_log_bwd")] = alog[0:1, 8:16]
            o_ref[(j,) + _region("dt_bias_fwd")] = dtb[0:1, 0:8]
            o_ref[(j,) + _region("dt_bias_bwd")] = dtb[0:1, 8:16]
            o_ref[(j,) + _region("conv_w")] = conv[:, j * SM_COLS : (j + 1) * SM_COLS]
            o_ref[(j,) + _region("gk_w2_fwd")] = w2f[32:48, j * 64 : (j + 1) * 64]
            o_ref[(j,) + _region("gk_w2_bwd")] = w2b[48:64, j * 64 : (j + 1) * 64]
            o_ref[j, LOSS_AT[0] : LOSS_AT[0] + 1, LOSS_AT[1] : LOSS_AT[1] + 1] = loss

    return pl.pallas_call(
        body, out_shape=jax.ShapeDtypeStruct((N_DEV, PACK_ROWS, PACK_COLS), f32), name="pack_small_grads",
    )(*[G[n] for n in names])


def _small_adam(land, own, W, M, V):
    bc1 = 1.0 - ADAM_B1**ADAM_STEP
    bc2 = 1.0 - ADAM_B2**ADAM_STEP
    n = len(SMALL_NAMES)

    def body(land_ref, own_ref, *refs):
        w_refs, m_refs, v_refs = refs[0:n], refs[n : 2 * n], refs[2 * n : 3 * n]
        outs, loss_ref, g_scr = refs[3 * n : 7 * n], refs[7 * n], refs[7 * n + 1]
        me = _my_device()
        g = jnp.where(me == 0, own_ref[0], land_ref[0])
        for j in range(1, N_DEV):
            g = g + jnp.where(me == j, own_ref[j], land_ref[j])
        g_scr[...] = g
        loss_ref[...] = g_scr[LOSS_AT[0] : LOSS_AT[0] + 1, LOSS_AT[1] : LOSS_AT[1] + 1]
        for i, name in enumerate(SMALL_NAMES):
            gp = g_scr[_region(name)]
            lead = (0,) if name in SMALL_SHARDED else (Ellipsis,)
            w, m, v = w_refs[i][lead], m_refs[i][lead], v_refs[i][lead]
            m2 = ADAM_B1 * m + (1.0 - ADAM_B1) * gp
            v2 = ADAM_B2 * v + (1.0 - ADAM_B2) * (gp * gp)
            delta = -ADAM_LR * ((m2 / bc1) / (jnp.sqrt(v2 / bc2) + ADAM_EPS) + ADAM_WD * w)
            for k, val in enumerate((gp, delta, m2, v2)):
                outs[4 * i + k][lead] = val

    shapes = [jax.ShapeDtypeStruct(W[name].shape, f32) for name in SMALL_NAMES for _ in range(4)]
    res = pl.pallas_call(
        body,
        out_shape=tuple(shapes) + (jax.ShapeDtypeStruct((1, 1), f32),),
        scratch_shapes=[pltpu.VMEM((PACK_ROWS, PACK_COLS), f32)],
        name="adam_small",
    )(land, own, *[P[name] for P in (W, M, V) for name in SMALL_NAMES])
    return {name: res[4 * i : 4 * i + 4] for i, name in enumerate(SMALL_NAMES)}, res[4 * n]


_MESH = pl.DeviceIdType.MESH


N_BIG_COPIES, N_SMALL_COPIES = 9, 7


def _gather_weights(w_in_s, sm_s):
    shapes = (w_in_s.shape, sm_s.shape)
    dtypes = (MXU_DTYPE, f32)

    def body(win_ref, sm_ref, gin_ref, gsm_ref, send_sems, recv_sems):
        x, y, c = lax.axis_index("x"), lax.axis_index("y"), lax.axis_index("c")
        me, sibling = (x, y, c), (x, y, 1 - c)
        chips = [(1 - x, y), (x, 1 - y), (1 - x, 1 - y)]

        def idx(px, py, pc):
            return 4 * px + 2 * py + pc

        for r in range(w_in_s.shape[1] // LANES):
            cols = slice(r * LANES, (r + 1) * LANES)
            gin_ref[idx(*me), :, cols] = win_ref[:, cols].astype(MXU_DTYPE)
        gsm_ref[idx(*me)] = sm_ref[...]

        def copy(sem, blk, to):
            return pltpu.make_async_remote_copy(src_ref=blk, dst_ref=blk, send_sem=send_sems.at[sem], recv_sem=recv_sems.at[sem],
                                                device_id=to, device_id_type=_MESH)

        def small(k, block, to):
            return copy(N_BIG_COPIES + k, gsm_ref.at[idx(*block)], to)

        half = w_in_s.shape[1] // 2
        xn, yn, dg = (1 - x, y), (x, 1 - y), (1 - x, 1 - y)

        def big(sem, block, to, part=None):
            blk = gin_ref.at[idx(*block)]
            return copy(sem, blk if part is None else blk.at[:, pl.ds(part * half, half)], to)

        started = [big(0, me, sibling), big(1, me, (*xn, c)), big(2, me, (*yn, c)), small(0, me, sibling)]
        started += [small(1 + j, me, (*chip, c)) for j, chip in enumerate(chips)]
        for cp in started:
            cp.start()

        def then(arrived, forwards):
            arrived.wait_recv()
            for cp in forwards:
                cp.start()
            started.extend(forwards)

        then(big(1, (*xn, c), me), [big(3, (*xn, c), (*yn, c), 1), big(5, (*xn, c), sibling)])
        then(big(2, (*yn, c), me), [big(4, (*yn, c), (*xn, c), 0), big(6, (*yn, c), sibling)])
        then(big(4, (*dg, c), me, 0), [big(7, (*dg, c), sibling, 0)])
        then(big(3, (*dg, c), me, 1), [big(8, (*dg, c), sibling, 1)])
        for j, chip in enumerate(chips):
            then(small(1 + j, (*chip, c), me), [small(4 + j, (*chip, c), sibling)])
        big(0, sibling, me).wait_recv()
        big(5, (*xn, 1 - c), me).wait_recv()
        big(6, (*yn, 1 - c), me).wait_recv()
        big(7, (*dg, 1 - c), me, 0).wait_recv()
        big(8, (*dg, 1 - c), me, 1).wait_recv()
        small(0, sibling, me).wait_recv()
        for j, chip in enumerate(chips):
            small(4 + j, (*chip, 1 - c), me).wait_recv()
        for cp in started:
            cp.wait_send()

    vm = pl.BlockSpec(memory_space=pltpu.VMEM)
    return pl.pallas_call(
        body,
        out_shape=tuple(jax.ShapeDtypeStruct((N_DEV,) + s, d) for s, d in zip(shapes, dtypes)),
        in_specs=[vm, vm],
        out_specs=(vm, vm),
        scratch_shapes=[pltpu.SemaphoreType.DMA((N_BIG_COPIES + N_SMALL_COPIES,)), pltpu.SemaphoreType.DMA((N_BIG_COPIES + N_SMALL_COPIES,))],
        compiler_params=pltpu.CompilerParams(vmem_limit_bytes=VMEM_LIMIT),
        name="gather_weights",
    )(w_in_s, sm_s)


N_CHIP = 4
_EFFECT = pltpu.SideEffectType.DATAFLOW_SIDE_EFFECTING
_HBM_SPEC = pl.BlockSpec(memory_space=pltpu.HBM)
_SEM_SPEC = pl.BlockSpec(memory_space=pltpu.SEMAPHORE)


def _relation_peer(k, x, y, c):
    px = (1 - x) if (k & 4) else x
    py = (1 - y) if (k & 2) else y
    pc = (1 - c) if (k & 1) else c
    return (px, py, pc), 4 * px + 2 * py + pc


_PLAN_COPIES = {"gather": N_DEV - 1, "blocks": N_DEV - 1, "sibling": 4, "chips": 3}
_PLAN_SLOTS = {"gather": N_DEV, "blocks": N_DEV, "sibling": 4, "chips": 4}


def _plan_copies(plan, src_ref, land_ref, send_sems, recv_sems, arrival):
    x, y, c = lax.axis_index("x"), lax.axis_index("y"), lax.axis_index("c")
    me, my_chip = 4 * x + 2 * y + c, 2 * x + y
    out = []

    def add(i, src, there, here, dev):
        out.append(pltpu.make_async_remote_copy(src_ref=src, dst_ref=land_ref.at[here if arrival else there], send_sem=send_sems.at[i],
                                                recv_sem=recv_sems.at[i], device_id=dev, device_id_type=_MESH))

    if plan in ("gather", "blocks"):
        for k in range(1, N_DEV):
            dev, p = _relation_peer(k, x, y, c)
            add(k - 1, src_ref.at[p] if plan == "blocks" else src_ref, me, p, dev)
    elif plan == "sibling":
        for j in range(4):
            add(j, src_ref.at[2 * j + 1 - c], j, j, (x, y, 1 - c))
    else:
        for i, k in enumerate((2, 4, 6)):
            (px, py, pc), _ = _relation_peer(k, x, y, c)
            add(i, src_ref.at[2 * px + py], my_chip, 2 * px + py, (px, py, pc))
    return out


def _exchange_start(src, plan, name, after=None):
    n = _PLAN_COPIES[plan]
    land = lax.empty((_PLAN_SLOTS[plan],) + (src.shape if plan == "gather" else src.shape[1:]), src.dtype)
    extra = [] if after is None else [after]

    def body(src_ref, land_ref, *rest):
        send_sems, recv_sems, _, _, token = rest[len(extra) :]
        for cp in _plan_copies(plan, src_ref, land_ref, send_sems, recv_sems, False):
            cp.start()
        token[...] = jnp.zeros_like(token)

    return pl.pallas_call(
        body,
        name=name,
        out_shape=(pltpu.SemaphoreType.DMA((n,)), pltpu.SemaphoreType.DMA((n,)), pltpu.HBM(src.shape, src.dtype),
                   pltpu.HBM(land.shape, land.dtype), jax.ShapeDtypeStruct((8, LANES), f32)),
        in_specs=(_HBM_SPEC, _HBM_SPEC) + (_ANY_SPEC,) * len(extra),
        out_specs=(_SEM_SPEC, _SEM_SPEC, _HBM_SPEC, _HBM_SPEC, pl.BlockSpec(memory_space=pltpu.VMEM)),
        input_output_aliases={0: 2, 1: 3},
        compiler_params=pltpu.CompilerParams(has_side_effects=_EFFECT),
    )(pltpu.with_memory_space_constraint(src, pltpu.HBM), pltpu.with_memory_space_constraint(land, pltpu.HBM), *extra)


def _exchange_wait(started, after, plan, name):
    send_sems, recv_sems, src_thru, land_thru, _ = started

    def body(src_ref, land_ref, send_sems, recv_sems, after_ref, src_dead, got_ref):
        for cp in _plan_copies(plan, src_ref, land_ref, send_sems, recv_sems, True):
            cp.wait_send()
            cp.wait_recv()

    return pl.pallas_call(
        body,
        name=name,
        out_shape=(pltpu.HBM(src_thru.shape, src_thru.dtype), pltpu.HBM(land_thru.shape, land_thru.dtype)),
        in_specs=(_HBM_SPEC, _HBM_SPEC, _SEM_SPEC, _SEM_SPEC, pl.BlockSpec(memory_space=pl.ANY)),
        out_specs=(_HBM_SPEC, _HBM_SPEC),
        input_output_aliases={0: 0, 1: 1},
        compiler_params=pltpu.CompilerParams(has_side_effects=_EFFECT),
    )(src_thru, land_thru, send_sems, recv_sems, after)


def _pair_sum(g, r, name, tc=LANES):
    _, R, Cc = g.shape

    def body(g_ref, r_ref, o_ref):
        o_ref[...] = (g_ref[...].astype(f32) + r_ref[...].astype(f32)).astype(o_ref.dtype)

    return pl.pallas_call(
        body,
        out_shape=jax.ShapeDtypeStruct(r.shape, r.dtype),
        grid=(N_CHIP, Cc // tc),
        in_specs=[pl.BlockSpec((1, R, tc), lambda k, i: (2 * k + lax.axis_index("c"), 0, i)), pl.BlockSpec((1, R, tc), lambda k, i: (k, 0, i))],
        out_specs=pl.BlockSpec((1, R, tc), lambda k, i: (k, 0, i)),
        compiler_params=_cparams(("parallel", "parallel")),
        name=name,
    )(g, r)


def _sum_adam(parts, w, m, v, name, tb, tc=None, own=None, own_slot=None):
    R, Cc = w.shape
    n_parts = parts.shape[0]
    tb = R if tc else min(tb, R)
    tc = tc or Cc
    assert R % tb == 0 and Cc % tc == 0
    bc1 = 1.0 - ADAM_B1**ADAM_STEP
    bc2 = 1.0 - ADAM_B2**ADAM_STEP
    extra = [] if own is None else [own]

    def body(p_ref, w_ref, m_ref, v_ref, *rest):
        g_ref, d_ref, nm_ref, nv_ref = rest[len(extra) :]

        def part(j):
            pj = p_ref[j].astype(f32)
            return pj if own is None else jnp.where(own_slot() == j, rest[0][0].astype(f32), pj)

        g = part(0)
        for j in range(1, n_parts):
            g = g + part(j)
        g_ref[...] = g
        m2 = ADAM_B1 * m_ref[...] + (1.0 - ADAM_B1) * g
        v2 = ADAM_B2 * v_ref[...] + (1.0 - ADAM_B2) * (g * g)
        nm_ref[...] = m2
        nv_ref[...] = v2
        d_ref[...] = -ADAM_LR * ((m2 / bc1) / (jnp.sqrt(v2 / bc2) + ADAM_EPS) + ADAM_WD * w_ref[...])

    blk = pl.BlockSpec((tb, tc), lambda i, j: (i, j))
    o = jax.ShapeDtypeStruct((R, Cc), f32)
    return pl.pallas_call(
        body,
        out_shape=(o, o, o, o),
        grid=(R // tb, Cc // tc),
        in_specs=[pl.BlockSpec((n_parts, tb, tc), lambda i, j: (0, i, j)), blk, blk, blk]
        + [pl.BlockSpec((1, tb, tc), lambda i, j: (own_slot(), i, j))] * len(extra),
        out_specs=(blk, blk, blk, blk),
        compiler_params=_cparams(("parallel", "parallel")),
        name=name,
    )(parts, w, m, v, *extra)


def _my_device():
    return 4 * lax.axis_index("x") + 2 * lax.axis_index("y") + lax.axis_index("c")


def _my_chip():
    return 2 * lax.axis_index("x") + lax.axis_index("y")


_WEIGHTS = ["ln_pre_w", "w_in", "conv_w", "a_log_fwd", "a_log_bwd", "dt_bias_fwd", "dt_bias_bwd", "gdn_norm_w", "w_proj_gdn",
            "gk_w2_fwd", "gk_b2_fwd", "gk_w2_bwd", "gk_b2_bwd", "gla_norm_w", "w_proj_gla", "w_out", "ln_post_w"]


def kernel(x, ln_pre_w, w_in, conv_w, a_log_fwd, a_log_bwd, dt_bias_fwd, dt_bias_bwd, gdn_norm_w, w_proj_gdn, gk_w2_fwd, gk_b2_fwd, gk_w2_bwd, gk_b2_bwd, gla_norm_w, w_proj_gla, w_out, ln_post_w, loss_target, m_ln_pre_w, m_w_in, m_conv_w, m_a_log_fwd, m_a_log_bwd, m_dt_bias_fwd, m_dt_bias_bwd, m_gdn_norm_w, m_w_proj_gdn, m_gk_w2_fwd, m_gk_b2_fwd, m_gk_w2_bwd, m_gk_b2_bwd, m_gla_norm_w, m_w_proj_gla, m_w_out, m_ln_post_w, v_ln_pre_w, v_w_in, v_conv_w, v_a_log_fwd, v_a_log_bwd, v_dt_bias_fwd, v_dt_bias_bwd, v_gdn_norm_w, v_w_proj_gdn, v_gk_w2_fwd, v_gk_b2_fwd, v_gk_w2_bwd, v_gk_b2_bwd, v_gla_norm_w, v_w_proj_gla, v_w_out, v_ln_post_w):
    args = locals()
    W = {n: args[n] for n in _WEIGHTS}
    M = {n: args["m_" + n] for n in _WEIGHTS}
    V = {n: args["v_" + n] for n in _WEIGHTS}

    wp_stack = lambda P: jnp.concatenate([P["w_proj_gdn"], P["w_proj_gla"], P["w_out"]], axis=0)
    sm_pack = lambda P: _pack_small_shard(P["conv_w"][0], P["gk_w2_fwd"][0], P["gk_w2_bwd"][0])
    me = 4 * lax.axis_index("x") + 2 * lax.axis_index("y") + lax.axis_index("c")
    w_in_t, m_in_t, v_in_t = (jnp.transpose(a[0]) for a in (w_in, m_w_in, v_w_in))
    g_in, g_sm = _gather_weights(w_in_t, sm_pack(W))
    wp_own = wp_stack(W).astype(MXU_DTYPE)
    wp_started = _exchange_start(wp_own, "gather", "gather_proj_start", after=g_sm)

    def proj_weights(after):
        own, land = _exchange_wait(wp_started, after, "gather", "gather_proj_wait")
        g_p = lax.dynamic_update_slice(land, own[None], (me, 0, 0, 0))
        return tuple(g_p[:, i].reshape(D_MODEL, D_MODEL) for i in range(3))

    rows_p = 3 * D_MODEL // N_DEV
    sent = {}

    def send_proj_grads(d_wpg, d_wpl, d_wout):
        p_p = jnp.stack([d_wpg, d_wpl, d_wout]).reshape(3, N_DEV, D_MODEL // N_DEV, D_MODEL).transpose(1, 0, 2, 3)
        sent["p_p"] = p_p.reshape(N_DEV, rows_p, D_MODEL)
        sent["started"] = _exchange_start(sent["p_p"], "blocks", "exchange_proj_start")
        return sent["started"][4][0:1, 0:1]

    def send_in_grads(d_wcat):
        p_in = _from_cat(d_wcat).reshape(N_DEV, SHARD_IN, D_MODEL)
        sib = _exchange_start(p_in, "sibling", "exchange_in_sibling_start")
        p_p, land_p = _exchange_wait(sent["started"], sib[4], "blocks", "exchange_proj_wait")
        sent["o_p"] = _sum_adam(land_p, wp_stack(W).reshape(rows_p, D_MODEL), wp_stack(M).reshape(rows_p, D_MODEL),
                                wp_stack(V).reshape(rows_p, D_MODEL), "adam_w_proj", 128, own=p_p, own_slot=_my_device)
        p_in, sib_land = _exchange_wait(sib, sent["o_p"][0], "sibling", "exchange_in_sibling_wait")
        sent["chips"] = _exchange_start(_pair_sum(p_in, sib_land, "pair_sum_w_in", tc=512), "chips", "exchange_in_chips_start")
        return sent["chips"][4]

    wcat = _to_cat(g_in.reshape(N_IN, D_MODEL))
    conv_full = g_sm[:, 0:CONV_K, :].transpose(1, 0, 2).reshape(CONV_K, N_DEV * SM_COLS)
    w2f_full = g_sm[:, 8:24, 0:64].transpose(1, 0, 2).reshape(16, 512)
    w2b_full = g_sm[:, 8:24, 64:128].transpose(1, 0, 2).reshape(16, 512)

    avec = _lane_row(a_log_fwd, a_log_bwd)
    dvec = _lane_row(dt_bias_fwd, dt_bias_bwd)
    G = _local_step(x[0], loss_target[0], wcat, conv_full, avec, dvec, gdn_norm_w, _pad_w2(w2f_full, 32), _pad_w2(w2b_full, 48),
                    gk_b2_fwd, gk_b2_bwd, gla_norm_w, proj_weights, ln_pre_w + wp_started[4][0:1, 0:1], ln_post_w, send_proj_grads,
                    send_in_grads)

    small_started = _exchange_start(_pack_small_grads(G), "blocks", "exchange_small_start")
    q_in, land_in = _exchange_wait(sent["chips"], small_started[4], "chips", "exchange_in_chips_wait")
    o_in = _sum_adam(land_in, w_in_t, m_in_t, v_in_t, "adam_w_in", SHARD_IN, tc=256, own=q_in, own_slot=_my_chip)
    o_p = sent["o_p"]
    p_small, land_small = _exchange_wait(small_started, o_in[0], "blocks", "exchange_small_wait")
    o_small, loss = _small_adam(land_small, p_small, W, M, V)

    res = []
    for kind in range(4):
        per = {name: o_small[name][kind] for name in SMALL_NAMES}
        per["w_in"] = jnp.transpose(o_in[kind])[None]
        pp = o_p[kind].reshape(3, 1, D_MODEL // N_DEV, D_MODEL)
        per["w_proj_gdn"], per["w_proj_gla"], per["w_out"] = pp[0], pp[1], pp[2]
        res.append([per[n] for n in _WEIGHTS])
    return (loss[0, 0], G["grad_x"][None], *res[0], *res[1], *res[2], *res[3])
```

```python
import jax
import jax.numpy as jnp
from jax import lax
from jax.experimental import pallas as pl
from jax.experimental.pallas import tpu as pltpu

f32 = jnp.float32
MXU_DTYPE = jnp.bfloat16

D_MODEL = 1024
CHUNK = 64
NORM_EPS = 1e-6
GDN_HEADS, GDN_DIM = 8, 128
GLA_HEADS, GLA_DK, GLA_DV = 4, 128, 256
GATE_NORMALIZER = 16.0
CONV_K = 5
N_IN = 9280
N_DEV = 8
SHARD_IN = N_IN // N_DEV

N_CAT = 9728
SMALL_OFF = 9216
LANES = 128

ADAM_LR, ADAM_B1, ADAM_B2, ADAM_EPS, ADAM_WD, ADAM_STEP = 0.001, 0.9, 0.999, 1e-08, 0.01, 10

VMEM_LIMIT = 56 * 1024 * 1024


def _cparams(sem=None):
    return pltpu.CompilerParams(dimension_semantics=sem, vmem_limit_bytes=VMEM_LIMIT)


_DN = {"nn": (((1,), (0,)), ((), ())), "nt": (((1,), (1,)), ((), ())), "tn": (((0,), (0,)), ((), ()))}
_BDN = {"nn": (((2,), (1,)), ((0,), (0,))), "nt": (((2,), (2,)), ((0,), (0,))), "tn": (((1,), (1,)), ((0,), (0,)))}


def _dot16(a, b, kind):
    return lax.dot_general(a, b, (_DN if a.ndim == 2 else _BDN)[kind], preferred_element_type=f32)


def _mx(a, b, kind):
    return _dot16(a.astype(MXU_DTYPE), b.astype(MXU_DTYPE), kind)


def _split2(a):
    hi = a.astype(jnp.bfloat16)
    return hi, (a - hi.astype(f32)).astype(jnp.bfloat16)


def _split3(a):
    p1 = a.astype(jnp.bfloat16)
    r = a - p1.astype(f32)
    p2 = r.astype(jnp.bfloat16)
    return p1, p2, (r - p2.astype(f32)).astype(jnp.bfloat16)


def _h3s(a2, b2, kind):
    (ah, al), (bh, bl) = a2, b2
    return _dot16(ah, bh, kind) + (_dot16(ah, bl, kind) + _dot16(al, bh, kind))


def _h3(a, b, kind):
    return _h3s(_split2(a), _split2(b), kind)


def _exact01(t01, x, kind="nn"):
    t = t01.astype(jnp.bfloat16)
    x1, x2, x3 = _split3(x)
    return _dot16(t, x1, kind) + (_dot16(t, x2, kind) + _dot16(t, x3, kind))


def _colsum_col(e):
    ones = jnp.ones(e.shape[:-1] + (LANES,), jnp.bfloat16)
    e1, e2, e3 = _split3(e)
    return (_dot16(e1, ones, "tn") + (_dot16(e2, ones, "tn") + _dot16(e3, ones, "tn")))[..., 0:1]


def _iota3(shape, axis):
    return lax.broadcasted_iota(jnp.int32, shape, axis)


def _dir_masks(B, nf, Cn):
    shp = (B, Cn, Cn)
    bb, ii, jj = _iota3(shp, 0), _iota3(shp, 1), _iota3(shp, 2)
    fwd = bb < nf
    rev = jnp.logical_not(fwd)
    incl = jnp.logical_or(jnp.logical_and(fwd, ii >= jj), jnp.logical_and(rev, ii <= jj))
    strict = jnp.logical_and(incl, ii != jj)
    return incl, strict, ii == jj


def _last_row(x, nf):
    B, Cn = x.shape[0], x.shape[1]
    fwd = _iota3((B, 1, 1), 0) < nf
    return jnp.where(fwd, x[:, Cn - 1 : Cn, :], x[:, 0:1, :])


def _add_at_last_row(x, val, nf):
    B, Cn = x.shape[0], x.shape[1]
    bb, rr = _iota3((B, Cn, 1), 0), _iota3((B, Cn, 1), 1)
    at = jnp.logical_or(jnp.logical_and(bb < nf, rr == Cn - 1), jnp.logical_and(bb >= nf, rr == 0))
    return x + jnp.where(at, val, 0.0)


def _sigmoid(x):
    return jax.nn.sigmoid(x)


def _silu(x):
    return x * _sigmoid(x)


def _silu_grad(x):
    s = _sigmoid(x)
    return s * (1.0 + x * (1.0 - s))


def _softplus(x):
    u = jnp.exp(-jnp.abs(x))
    l1p = jnp.where(u < 1e-3, u * (1.0 - u * (0.5 - u * (1.0 / 3.0))), jnp.log(1.0 + u))
    return jnp.maximum(x, 0.0) + l1p


def _mxr(x):
    return x.astype(MXU_DTYPE).astype(f32)


def _iota2(shape, axis):
    return lax.broadcasted_iota(jnp.int32, shape, axis)


def _matmul(a, b, kind, out_dtype, name, tm, tn, tk, out_t=False, after=None):
    extra = [] if after is None else [after]
    if kind == "nn":
        (M, K), N = a.shape, b.shape[1]
    elif kind == "nt":
        (M, K), N = a.shape, b.shape[0]
    else:
        (K, M), N = a.shape, b.shape[1]
    tm, tn, tk = min(tm, M), min(tn, N), min(tk, K)
    assert M % tm == 0 and N % tn == 0 and K % tk == 0, (name, M, N, K)
    nk = K // tk
    if kind == "tn":
        a_spec = pl.BlockSpec((tk, tm), lambda i, j, k: (k, i))
    else:
        a_spec = pl.BlockSpec((tm, tk), lambda i, j, k: (i, k))
    b_mode = dict(pipeline_mode=pl.Buffered(1)) if (tn == N and tk == K) else {}
    if kind == "nt":
        b_spec = pl.BlockSpec((tn, tk), lambda i, j, k: (j, k), **b_mode)
    else:
        b_spec = pl.BlockSpec((tk, tn), lambda i, j, k: (k, j), **b_mode)

    def body(a_ref, b_ref, *rest):
        o_ref, acc = rest[len(extra)], rest[len(extra) + 1 :]
        p = _mx(a_ref[...], b_ref[...], kind)
        if nk == 1:
            o_ref[...] = (p.T if out_t else p).astype(out_dtype)
        else:
            assert not out_t
            acc_ref = acc[0]
            k = pl.program_id(2)

            @pl.when(k == 0)
            def _():
                acc_ref[...] = p

            @pl.when(k > 0)
            def _():
                acc_ref[...] += p

            @pl.when(k == nk - 1)
            def _():
                o_ref[...] = acc_ref[...].astype(out_dtype)

    return pl.pallas_call(
        body,
        out_shape=jax.ShapeDtypeStruct((N, M) if out_t else (M, N), out_dtype),
        grid=(M // tm, N // tn, nk),
        in_specs=[a_spec, b_spec] + [pl.BlockSpec((8, LANES), lambda i, j, k: (0, 0))] * len(extra),
        out_specs=pl.BlockSpec((tn, tm), lambda i, j, k: (j, i)) if out_t else pl.BlockSpec((tm, tn), lambda i, j, k: (i, j)),
        scratch_shapes=[] if nk == 1 else [pltpu.VMEM((tm, tn), f32)],
        compiler_params=_cparams(("parallel", "parallel", "arbitrary")),
        name=name,
    )(a, b, *extra)


ROW_TILE = 512


def _rms_pre(x, w):
    T = x.shape[0]
    tb = min(ROW_TILE, T)

    def body(x_ref, w_ref, h_ref, ht_ref):
        xv = x_ref[...]
        r = lax.rsqrt(jnp.mean(xv * xv, axis=1, keepdims=True) + NORM_EPS)
        h = xv * r * w_ref[...]
        h_ref[...] = h.astype(h_ref.dtype)
        ht_ref[...] = h.T.astype(ht_ref.dtype)

    return pl.pallas_call(
        body,
        out_shape=(jax.ShapeDtypeStruct((T, D_MODEL), MXU_DTYPE), jax.ShapeDtypeStruct((D_MODEL, T), MXU_DTYPE)),
        grid=(T // tb,),
        in_specs=[pl.BlockSpec((tb, D_MODEL), lambda i: (i, 0)), pl.BlockSpec((1, D_MODEL), lambda i: (0, 0))],
        out_specs=(pl.BlockSpec((tb, D_MODEL), lambda i: (i, 0)), pl.BlockSpec((D_MODEL, tb), lambda i: (0, i))),
        compiler_params=_cparams(("parallel",)),
        name="rms_pre",
    )(x, w)


def _rms_pre_bwd(x, w, dh, dy):
    T = x.shape[0]
    tb = min(ROW_TILE, T)

    def body(x_ref, w_ref, dh_ref, dy_ref, dx_ref, dw_ref):
        i = pl.program_id(0)
        xv, dhv = x_ref[...], dh_ref[...]
        r = lax.rsqrt(jnp.mean(xv * xv, axis=1, keepdims=True) + NORM_EPS)
        dhw = dhv * w_ref[...]
        dx_ref[...] = dy_ref[...] + r * dhw - xv * (r * r * r) * jnp.mean(dhw * xv, axis=1, keepdims=True)
        part = jnp.sum(dhv * xv * r, axis=0, keepdims=True)

        @pl.when(i == 0)
        def _():
            dw_ref[...] = part

        @pl.when(i > 0)
        def _():
            dw_ref[...] += part

    blk = pl.BlockSpec((tb, D_MODEL), lambda i: (i, 0))
    row = pl.BlockSpec((1, D_MODEL), lambda i: (0, 0))
    return pl.pallas_call(
        body,
        out_shape=(jax.ShapeDtypeStruct((T, D_MODEL), f32), jax.ShapeDtypeStruct((1, D_MODEL), f32)),
        grid=(T // tb,),
        in_specs=[blk, row, blk, blk],
        out_specs=(blk, row),
        compiler_params=_cparams(("arbitrary",)),
        name="rms_pre_bwd",
    )(x, w, dh, dy)


SHIFTS = (-2, -1, 1, 2)


SHIFT_PAD = 8
PREP_HEADS = 4


def _shifted(x, pad_ref):
    T, cb = x.shape
    pad_ref[0:SHIFT_PAD] = jnp.zeros((SHIFT_PAD, cb), f32)
    pad_ref[SHIFT_PAD + T :] = jnp.zeros((SHIFT_PAD, cb), f32)
    pad_ref[SHIFT_PAD : SHIFT_PAD + T] = x
    return {d: pad_ref[SHIFT_PAD + d : SHIFT_PAD + d + T] for d in SHIFTS}


def _conv5(u, ush, cw):
    acc = u * cw[2:3, :]
    for j in (0, 1, 3, 4):
        acc = acc + ush[j - 2] * cw[j : j + 1, :]
    return acc


def _gdn_prep(proj, conv_w):
    T = proj.shape[0]
    hps, W = PREP_HEADS, PREP_HEADS * GDN_DIM
    per = GDN_HEADS // hps

    def body(u_ref, cw_ref, o_ref, pad_ref):
        j = pl.program_id(0)
        for i in range(hps):
            sl = slice(i * GDN_DIM, (i + 1) * GDN_DIM)
            u = _mxr(u_ref[:, sl])
            s = _silu(_conv5(u, _shifted(u, pad_ref), _mxr(cw_ref[:, sl])))
            o_ref[0, :, sl] = jnp.where(j < 2 * per, s * lax.rsqrt(jnp.sum(s * s, axis=1, keepdims=True) + NORM_EPS), s)

    return pl.pallas_call(
        body,
        out_shape=jax.ShapeDtypeStruct((3, T, GDN_HEADS * GDN_DIM), f32),
        grid=(3 * per,),
        in_specs=[pl.BlockSpec((T, W), lambda j: (0, j)), pl.BlockSpec((CONV_K, W), lambda j: (0, j))],
        out_specs=pl.BlockSpec((1, T, W), lambda j: (j // per, 0, j % per)),
        scratch_shapes=[pltpu.VMEM((T + 2 * SHIFT_PAD, GDN_DIM), f32)],
        compiler_params=_cparams(("parallel",)),
        name="gdn_prep",
    )(proj, conv_w)


def _gdn_prep_bwd(proj, conv_w, dqkv_f, dqkv_b, dproj):
    T = proj.shape[0]
    hps, W = PREP_HEADS, PREP_HEADS * GDN_DIM
    per = GDN_HEADS // hps

    def body(u_ref, cw_ref, df_ref, db_ref, _, du_ref, dcw_ref, upad_ref, dcpad_ref):
        j = pl.program_id(0)
        for i in range(hps):
            sl = slice(i * GDN_DIM, (i + 1) * GDN_DIM)
            u, cw = _mxr(u_ref[:, sl]), _mxr(cw_ref[:, sl])
            ush = _shifted(u, upad_ref)
            c = _conv5(u, ush, cw)
            s = _silu(c)
            dn = df_ref[0, :, sl] + db_ref[0, :, sl]
            rinv = lax.rsqrt(jnp.sum(s * s, axis=1, keepdims=True) + NORM_EPS)
            ds_norm = rinv * dn - s * (rinv * rinv * rinv) * jnp.sum(dn * s, axis=1, keepdims=True)
            ds = jnp.where(j < 2 * per, ds_norm, dn)
            dc = _mxr(ds * _silu_grad(c))
            dcsh = _shifted(dc, dcpad_ref)
            du = dc * cw[2:3, :]
            for jj in range(CONV_K):
                d = jj - 2
                dcw_ref[jj : jj + 1, sl] = jnp.sum(dc * (u if d == 0 else ush[d]), axis=0, keepdims=True)
                if d != 0:
                    du = du + dcsh[-d] * cw[jj : jj + 1, :]
            du_ref[:, sl] = du.astype(du_ref.dtype)

    qspec = pl.BlockSpec((1, T, W), lambda j: (j // per, 0, j % per))
    return pl.pallas_call(
        body,
        out_shape=(jax.ShapeDtypeStruct(dproj.shape, dproj.dtype), jax.ShapeDtypeStruct((CONV_K, 3 * GDN_HEADS * GDN_DIM), f32)),
        grid=(3 * per,),
        in_specs=[pl.BlockSpec((T, W), lambda j: (0, j)), pl.BlockSpec((CONV_K, W), lambda j: (0, j)), qspec, qspec, _ANY_SPEC],
        out_specs=(pl.BlockSpec((T, W), lambda j: (0, j)), pl.BlockSpec((CONV_K, W), lambda j: (0, j))),
        input_output_aliases={4: 0},
        scratch_shapes=[pltpu.VMEM((T + 2 * SHIFT_PAD, GDN_DIM), f32)] * 2,
        compiler_params=_cparams(("parallel",)),
        name="gdn_prep_bwd",
    )(proj, conv_w, dqkv_f, dqkv_b, dproj)


def _chunk_tri(n, rev):
    i, j = _iota2((n, n), 0), _iota2((n, n), 1)
    same = jnp.right_shift(i, 6) == jnp.right_shift(j, 6)
    order = (j >= i) if rev else (j <= i)
    return jnp.where(jnp.logical_and(same, order), 1.0, 0.0).astype(f32)


def _gate_prep(proj, avec, dvec, w2f, w2b, b2f, b2b, tb=256):
    T = proj.shape[0]
    W = GLA_HEADS * GLA_DK

    def body(s_ref, a_ref, d_ref, wf_ref, wb_ref, bf_ref, bb_ref, g_ref, gc_ref, gf_ref, gb_ref):
        small = s_ref[...]
        tri_f, tri_r = _chunk_tri(tb, False), _chunk_tri(tb, True)
        lane = _iota2(small.shape, 1)
        lg = -jnp.exp(a_ref[...]) * _softplus(small + d_ref[...])
        beta = _sigmoid(small)
        g_ref[...] = jnp.where(lane < 16, lg, jnp.where(lane < 32, beta, 0.0))
        lgm = jnp.where(lane < 16, lg, 0.0)
        gc_ref[...] = jnp.where(lane < 8, _exact01(tri_f, lgm), _exact01(tri_r, lgm))
        gkf = -_softplus(-(_mx(small, wf_ref[...], "nn") + bf_ref[...])) * (1.0 / GATE_NORMALIZER)
        gkb = -_softplus(-(_mx(small, wb_ref[...], "nn") + bb_ref[...])) * (1.0 / GATE_NORMALIZER)
        gf_ref[...] = _exact01(tri_f, gkf)
        gb_ref[...] = _exact01(tri_r, gkb)

    blk = pl.BlockSpec((tb, LANES), lambda i: (i, 0))
    row = pl.BlockSpec((1, LANES), lambda i: (0, 0))
    wide = pl.BlockSpec((tb, W), lambda i: (i, 0))
    wsp = pl.BlockSpec((LANES, W), lambda i: (0, 0))
    wrow = pl.BlockSpec((1, W), lambda i: (0, 0))
    return pl.pallas_call(
        body,
        out_shape=(jax.ShapeDtypeStruct((T, LANES), f32), jax.ShapeDtypeStruct((T, LANES), f32),
                   jax.ShapeDtypeStruct((T, W), f32), jax.ShapeDtypeStruct((T, W), f32)),
        grid=(T // tb,),
        in_specs=[pl.BlockSpec((tb, LANES), lambda i: (i, SMALL_OFF // LANES)), row, row, wsp, wsp, wrow, wrow],
        out_specs=(blk, blk, wide, wide),
        compiler_params=_cparams(("parallel",)),
        name="gate_prep",
    )(proj, avec, dvec, w2f, w2b, b2f, b2b)


def _gdn_gates_bwd(proj, avec, dvec, dg_f, dg_b, dsmall_gla, dproj):
    T = proj.shape[0]
    tb = min(ROW_TILE, T)
    pad = N_CAT - SMALL_OFF

    def body(s_ref, a_ref, d_ref, gf_ref, gb_ref, dl_ref, _, ds_ref, da_ref, dd_ref):
        i = pl.program_id(0)
        small = s_ref[...]
        lane = _iota2(small.shape, 1)
        dgate = gf_ref[...] + gb_ref[...]
        z = small + d_ref[...]
        nega = -jnp.exp(a_ref[...])
        dz = dgate * nega * _sigmoid(z)
        beta = _sigmoid(small)
        dsm = jnp.where(lane < 16, dz, jnp.where(lane < 32, dgate * beta * (1.0 - beta), 0.0))
        ds_ref[:, :LANES] = (dsm + dl_ref[...]).astype(ds_ref.dtype)
        ds_ref[:, LANES:] = jnp.zeros((tb, pad - LANES), ds_ref.dtype)
        lg = nega * _softplus(z)
        pa = jnp.sum(jnp.where(lane < 16, dgate * lg, 0.0), axis=0, keepdims=True)
        pd = jnp.sum(jnp.where(lane < 16, dz, 0.0), axis=0, keepdims=True)

        @pl.when(i == 0)
        def _():
            da_ref[...] = pa
            dd_ref[...] = pd

        @pl.when(i > 0)
        def _():
            da_ref[...] += pa
            dd_ref[...] += pd

    blk = pl.BlockSpec((tb, LANES), lambda i: (i, 0))
    row = pl.BlockSpec((1, LANES), lambda i: (0, 0))
    return pl.pallas_call(
        body,
        out_shape=(jax.ShapeDtypeStruct(dproj.shape, dproj.dtype), jax.ShapeDtypeStruct((1, LANES), f32), jax.ShapeDtypeStruct((1, LANES), f32)),
        grid=(T // tb,),
        in_specs=[pl.BlockSpec((tb, LANES), lambda i: (i, SMALL_OFF // LANES)), row, row, blk, blk, blk, _ANY_SPEC],
        out_specs=(pl.BlockSpec((tb, pad), lambda i: (i, DP_SMALL_BLK)), row, row),
        input_output_aliases={6: 0},
        compiler_params=_cparams(("arbitrary",)),
        name="gdn_gates_bwd",
    )(proj, avec, dvec, dg_f, dg_b, dsmall_gla, dproj)


GDN_HB = 8


def _inv_unit_lower(L, eye):
    A = eye - L
    P2 = _split2(L)
    for _ in range(3):
        P2 = _split2(_h3s(P2, P2, "nn"))
        A = A + _h3s(_split2(A), P2, "nn")
    Ph = P2[0]
    for _ in range(2):
        Ph = _dot16(Ph, Ph, "nn").astype(jnp.bfloat16)
        A = A + _dot16(A.astype(jnp.bfloat16), Ph, "nn")
    return A


def _lane_col(blk, idx):
    lane = _iota2(blk.shape, 1)
    return jnp.sum(jnp.where(lane == idx, blk, 0.0), axis=1, keepdims=True)


def _out_norm(o_f, o_b, proj, gate_blk, w, hd, name):
    T, W = o_f.shape
    tb = min(ROW_TILE, T)
    nh = W // hd

    def body(of_ref, ob_ref, z_ref, w_ref, y_ref):
        wv = w_ref[...]
        for i in range(nh):
            sl = slice(i * hd, (i + 1) * hd)
            o = of_ref[:, sl] + ob_ref[:, sl]
            r = lax.rsqrt(jnp.mean(o * o, axis=1, keepdims=True) + NORM_EPS)
            y_ref[:, sl] = (o * r * wv * _silu(z_ref[:, sl])).astype(y_ref.dtype)

    blk = pl.BlockSpec((tb, W), lambda i: (i, 0))
    return pl.pallas_call(
        body,
        out_shape=jax.ShapeDtypeStruct((T, W), MXU_DTYPE),
        grid=(T // tb,),
        in_specs=[blk, blk, pl.BlockSpec((tb, W), lambda i: (i, gate_blk)), pl.BlockSpec((1, hd), lambda i: (0, 0))],
        out_specs=blk,
        compiler_params=_cparams(("parallel",)),
        name=name,
    )(o_f, o_b, proj, w)


def _out_norm_bwd(dy, o_f, o_b, proj, gate_blk, w, hd, name, dproj):
    T, W = o_f.shape
    tb = min(ROW_TILE, T)
    nh = W // hd

    def body(dy_ref, of_ref, ob_ref, z_ref, w_ref, _, do_ref, dz_ref, dw_ref):
        i = pl.program_id(0)
        wv = w_ref[...]
        dw = jnp.zeros((1, hd), f32)
        for a in range(nh):
            sl = slice(a * hd, (a + 1) * hd)
            o = of_ref[:, sl] + ob_ref[:, sl]
            z = z_ref[:, sl]
            dyv = dy_ref[:, sl]
            r = lax.rsqrt(jnp.mean(o * o, axis=1, keepdims=True) + NORM_EPS)
            orr = o * r
            dn = dyv * _silu(z)
            dz_ref[:, sl] = (dyv * orr * wv * _silu_grad(z)).astype(dz_ref.dtype)
            dw = dw + jnp.sum(dn * orr, axis=0, keepdims=True)
            dnw = dn * wv
            do_ref[:, sl] = r * dnw - o * (r * r * r) * jnp.mean(dnw * o, axis=1, keepdims=True)

        @pl.when(i == 0)
        def _():
            dw_ref[...] = dw

        @pl.when(i > 0)
        def _():
            dw_ref[...] += dw

    blk = pl.BlockSpec((tb, W), lambda i: (i, 0))
    row = pl.BlockSpec((1, hd), lambda i: (0, 0))
    return pl.pallas_call(
        body,
        out_shape=(jax.ShapeDtypeStruct((T, W), f32), jax.ShapeDtypeStruct(dproj.shape, dproj.dtype), jax.ShapeDtypeStruct((1, hd), f32)),
        grid=(T // tb,),
        in_specs=[blk, blk, blk, pl.BlockSpec((tb, W), lambda i: (i, gate_blk)), row, _ANY_SPEC],
        out_specs=(blk, pl.BlockSpec((tb, W), lambda i: (i, gate_blk)), row),
        input_output_aliases={5: 1},
        compiler_params=_cparams(("arbitrary",)),
        name=name,
    )(dy, o_f, o_b, proj, w, dproj)


def _gla_prep_bwd(proj, w2f, w2b, b2f, b2b, dgk_f, dgk_b, dqkv, dproj):
    T = proj.shape[0]
    tb = min(ROW_TILE, T)
    W = GLA_HEADS * GLA_DK
    WV = GLA_HEADS * GLA_DV

    def body(s_ref, wf_ref, wb_ref, bf_ref, bb_ref, df_ref, db_ref, qf, qb, kf, kb, vf, vb, _,
             dp_ref, ds_ref, dwf_ref, dwb_ref, dbf_ref, dbb_ref):
        i = pl.program_id(0)
        dp_ref[:, :W] = (qf[...] + qb[...]).astype(dp_ref.dtype)
        dp_ref[:, W : 2 * W] = (kf[...] + kb[...]).astype(dp_ref.dtype)
        dp_ref[:, 2 * W :] = (vf[...] + vb[...]).astype(dp_ref.dtype)
        small = s_ref[...]
        dsm = jnp.zeros((tb, LANES), f32)
        parts = []
        for w_ref, b_ref, d_ref in ((wf_ref, bf_ref, df_ref), (wb_ref, bb_ref, db_ref)):
            pre = _mx(small, w_ref[...], "nn") + b_ref[...]
            dpre = d_ref[...] * (1.0 / GATE_NORMALIZER) * _sigmoid(-pre)
            dsm = dsm + _mx(dpre, w_ref[...], "nt")
            parts.append((_mx(small, dpre, "tn"), jnp.sum(dpre, axis=0, keepdims=True)))
        ds_ref[...] = dsm

        @pl.when(i == 0)
        def _():
            dwf_ref[...], dbf_ref[...] = parts[0]
            dwb_ref[...], dbb_ref[...] = parts[1]

        @pl.when(i > 0)
        def _():
            dwf_ref[...] += parts[0][0]
            dbf_ref[...] += parts[0][1]
            dwb_ref[...] += parts[1][0]
            dbb_ref[...] += parts[1][1]

    blk = pl.BlockSpec((tb, W), lambda i: (i, 0))
    vblk = pl.BlockSpec((tb, WV), lambda i: (i, 0))
    wsp = pl.BlockSpec((LANES, W), lambda i: (0, 0))
    row = pl.BlockSpec((1, W), lambda i: (0, 0))
    return pl.pallas_call(
        body,
        out_shape=(jax.ShapeDtypeStruct(dproj.shape, dproj.dtype), jax.ShapeDtypeStruct((T, LANES), f32), jax.ShapeDtypeStruct((LANES, W), f32),
                   jax.ShapeDtypeStruct((LANES, W), f32), jax.ShapeDtypeStruct((1, W), f32), jax.ShapeDtypeStruct((1, W), f32)),
        grid=(T // tb,),
        in_specs=[pl.BlockSpec((tb, LANES), lambda i: (i, SMALL_OFF // LANES)), wsp, wsp, row, row, blk, blk,
                  blk, blk, blk, blk, vblk, vblk, _ANY_SPEC],
        out_specs=(pl.BlockSpec((tb, 2 * W + WV), lambda i: (i, DP_GLA_QKV_BLK)), pl.BlockSpec((tb, LANES), lambda i: (i, 0)), wsp, wsp, row, row),
        input_output_aliases={13: 0},
        compiler_params=_cparams(("arbitrary",)),
        name="gla_prep_bwd",
    )(proj, w2f, w2b, b2f, b2b, dgk_f, dgk_b, *dqkv, dproj)


GATE_A_BLK, GATE_B_BLK, Z_BLK, G_B_BLK = 6144 // 1024, 7168 // 1024, 3072 // 1024, 8192 // 1024
DP_GATES_BLK, DP_GLA_QKV_BLK, DP_SMALL_BLK = 6144 // 2048, 4096 // 2048, SMALL_OFF // 512
_ANY_SPEC = pl.BlockSpec(memory_space=pl.ANY)


def _mid(og, obg, proj, x, target, wpg, wpl, wout, w_post, tb=256):
    T = x.shape[0]
    Dm = D_MODEL

    def body(og_ref, obg_ref, ga_ref, gb_ref, x_ref, t_ref, wpg_ref, wpl_ref, wo_ref, wp_ref,
             loss_ref, dwp_ref, dy_ref, m_ref, dout_ref, dya_ref, dyb_ref, dgates_ref, dog_ref, dobg_ref):
        i = pl.program_id(0)
        ya = _mx(og_ref[...], wpg_ref[...], "nn")
        yb = _mx(obg_ref[...], wpl_ref[...], "nn")
        sa, sb = _sigmoid(ga_ref[...]), _sigmoid(gb_ref[...])
        m = sa * ya + sb * yb
        m_ref[...] = m.astype(m_ref.dtype)
        out = _mx(m, wo_ref[...], "nn")
        r = lax.rsqrt(jnp.mean(out * out, axis=1, keepdims=True) + NORM_EPS)
        wp = wp_ref[...]
        e = x_ref[...] + out * r * wp - t_ref[...]
        dy = e * (1.0 / Dm)
        dy_ref[...] = dy
        lpart = jnp.sum(e * e, axis=0, keepdims=True)
        wpart = jnp.sum(dy * out * r, axis=0, keepdims=True)
        dyn = dy * wp
        dout = r * dyn - out * (r * r * r) * jnp.mean(dyn * out, axis=1, keepdims=True)
        dout_ref[...] = dout.astype(dout_ref.dtype)
        dm = _mx(dout, wo_ref[...], "nt")
        dya, dyb = dm * sa, dm * sb
        dya_ref[...] = dya.astype(dya_ref.dtype)
        dyb_ref[...] = dyb.astype(dyb_ref.dtype)
        dgates_ref[:, :Dm] = (dm * ya * sa * (1.0 - sa)).astype(dgates_ref.dtype)
        dgates_ref[:, Dm:] = (dm * yb * sb * (1.0 - sb)).astype(dgates_ref.dtype)
        dog_ref[...] = _mx(dya, wpg_ref[...], "nt")
        dobg_ref[...] = _mx(dyb, wpl_ref[...], "nt")

        @pl.when(i == 0)
        def _():
            loss_ref[...] = lpart
            dwp_ref[...] = wpart

        @pl.when(i > 0)
        def _():
            loss_ref[...] += lpart
            dwp_ref[...] += wpart

    blk = pl.BlockSpec((tb, Dm), lambda i: (i, 0))
    row = pl.BlockSpec((1, Dm), lambda i: (0, 0))
    wsp = pl.BlockSpec((Dm, Dm), lambda i: (0, 0))
    act = jax.ShapeDtypeStruct((T, Dm), MXU_DTYPE)
    big = jax.ShapeDtypeStruct((T, Dm), f32)
    vec = jax.ShapeDtypeStruct((1, Dm), f32)
    return pl.pallas_call(
        body,
        out_shape=(vec, vec, big, act, act, act, act, jax.ShapeDtypeStruct((T, N_CAT), MXU_DTYPE), big, big),
        grid=(T // tb,),
        in_specs=[blk, blk, pl.BlockSpec((tb, Dm), lambda i: (i, GATE_A_BLK)), pl.BlockSpec((tb, Dm), lambda i: (i, GATE_B_BLK)),
                  blk, blk, wsp, wsp, wsp, row],
        out_specs=(row, row) + (blk,) * 5 + (pl.BlockSpec((tb, 2 * Dm), lambda i: (i, DP_GATES_BLK)), blk, blk),
        compiler_params=_cparams(("arbitrary",)),
        name="mid",
    )(og, obg, proj, proj, x, target, wpg, wpl, wout, w_post)


def _heads(ref, lead, n, width, rows=slice(None)):
    return [ref[lead + (rows, slice(i * width, (i + 1) * width))] for i in range(n)]


class _GdnBatch:
    def __init__(self, q, k, v, gcol, grow, beta, nf, A=None):
        B, Cn = q.shape[0], q.shape[1]
        self.incl, self.strict, eye = _dir_masks(B, nf, Cn)
        self.decay = jnp.where(self.incl, jnp.exp(jnp.where(self.incl, gcol - grow, 0.0)), 0.0)
        self.k, self.v = k, v
        gl = _last_row(gcol, nf)
        self.beta = jnp.broadcast_to(beta, q.shape)
        self.eg = jnp.broadcast_to(jnp.exp(gcol), q.shape)
        self.ekd = jnp.broadcast_to(jnp.exp(gl - gcol), q.shape)
        self.qs = q * (GDN_DIM**-0.5)
        self.kb = k * self.beta
        self.L = jnp.where(self.strict, _mx(self.kb, k, "nt") * self.decay, 0.0)
        if A is None:
            A = _inv_unit_lower(self.L, jnp.where(eye, 1.0, 0.0).astype(f32))
        self.A = A
        self.A2 = _split2(A)
        self.kbg = self.kb * self.eg
        self.sol = _h3s(self.A2, _split2(jnp.concatenate([v * self.beta, self.kbg], axis=2)), "nn")
        self.u = self.sol[:, :, :GDN_DIM]
        self.w = self.sol[:, :, GDN_DIM:]
        self.attn = jnp.where(self.incl, _mx(self.qs, k, "nt") * self.decay, 0.0)
        self.qd = self.qs * self.eg
        self.kd = k * self.ekd
        self.egl = jnp.exp(gl)


def _gdn_chain_inputs(qf_ref, qb_ref, gf_ref, gb_ref, cf_ref, cb_ref, b, hb):
    qkv = [jnp.stack(_heads(qf_ref, (i,), hb, GDN_DIM) + _heads(qb_ref, (i,), hb, GDN_DIM)) for i in range(3)]
    cols, rows, betas = [], [], []
    for d, (g_ref, c_ref) in enumerate(((gf_ref, cf_ref), (gb_ref, cb_ref))):
        gates, gc = g_ref[...], c_ref[...]
        for hh in range(hb):
            h = b * hb + hh
            col = _lane_col(gc, h + 8 * d)
            cols.append(col)
            rows.append(jnp.broadcast_to(col, (CHUNK, LANES)).T[0:1, :])
            betas.append(_lane_col(gates, 16 + h + 8 * d))
    return qkv[0], qkv[1], qkv[2], jnp.stack(cols), jnp.stack(rows), jnp.stack(betas)


GDN_CPS_FWD, GDN_CPS_BWD = 2, 1


def _chunk_views(refs, s_f, s_b):
    out = []
    for i, r in enumerate(refs):
        s = s_f if i % 2 == 0 else s_b
        rows = pl.ds(s * CHUNK, CHUNK)
        out.append(r.at[pl.ds(s, 1)] if len(r.shape) == 4 else (r.at[:, rows, :] if len(r.shape) == 3 else r.at[rows, :]))
    return out


def _gdn_specs(cps, hb):
    R, W = cps * CHUNK, hb * GDN_DIM
    qsp = lambda ci: pl.BlockSpec((3, R, W), lambda n, b: (0, ci(n), b))
    gsp = lambda ci: pl.BlockSpec((R, LANES), lambda n, b: (ci(n), 0))
    osp = lambda ci: pl.BlockSpec((R, W), lambda n, b: (ci(n), b))
    ssp = lambda ci: pl.BlockSpec((cps, hb, GDN_DIM, GDN_DIM), lambda n, b: (ci(n), b, 0, 0))
    asp = lambda ci: pl.BlockSpec((cps, hb, CHUNK, CHUNK), lambda n, b: (ci(n), b, 0, 0))
    return qsp, gsp, osp, ssp, asp


def _gdn_fwd(qkvn, gates, gc):
    T = qkvn.shape[1]
    N = T // CHUNK
    nh, hb = GDN_HEADS, GDN_HB
    cps = min(GDN_CPS_FWD, N)
    NB = N // cps

    def body(*refs):
        n, b = pl.program_id(0), pl.program_id(1)
        s_scr = refs[-1]

        @pl.when(n == 0)
        def _():
            s_scr[0, pl.ds(b * hb, hb)] = jnp.zeros((hb, GDN_DIM, GDN_DIM), f32)
            s_scr[1, pl.ds(b * hb, hb)] = jnp.zeros((hb, GDN_DIM, GDN_DIM), f32)

        for s in range(cps):
            chunk(*_chunk_views(refs[:-1], s, cps - 1 - s), s_scr, b)

    def chunk(qf_ref, qb_ref, gf_ref, gb_ref, cf_ref, cb_ref, of_ref, ob_ref, sf_ref, sb_ref, af_ref, ab_ref, s_scr, b):
        hs = pl.ds(b * hb, hb)
        q, k, v, gcol, grow, beta = _gdn_chain_inputs(qf_ref, qb_ref, gf_ref, gb_ref, cf_ref, cb_ref, b, hb)
        ck = _GdnBatch(q, k, v, gcol, grow, beta, hb)
        S = jnp.concatenate([s_scr[0, hs], s_scr[1, hs]], axis=0)
        sf_ref[0], sb_ref[0] = S[:hb], S[hb:]
        af_ref[0], ab_ref[0] = ck.A[:hb], ck.A[hb:]
        v_new = ck.u - _mx(ck.w, S, "nn")
        o = _mx(ck.qd, S, "nn") + _mx(ck.attn, v_new, "nn")
        S_new = S * ck.egl + _mx(ck.kd, v_new, "tn")
        for hh in range(hb):
            sl = slice(hh * GDN_DIM, (hh + 1) * GDN_DIM)
            of_ref[:, sl] = o[hh]
            ob_ref[:, sl] = o[hb + hh]
        s_scr[0, hs] = S_new[:hb]
        s_scr[1, hs] = S_new[hb:]

    fw, bw = (lambda n: n), (lambda n: NB - 1 - n)
    qsp, gsp, osp, ssp, asp = _gdn_specs(cps, hb)
    o_shape = jax.ShapeDtypeStruct((T, nh * GDN_DIM), f32)
    s_shape = jax.ShapeDtypeStruct((N, nh, GDN_DIM, GDN_DIM), f32)
    a_shape = jax.ShapeDtypeStruct((N, nh, CHUNK, CHUNK), f32)
    return pl.pallas_call(
        body,
        out_shape=(o_shape, o_shape, s_shape, s_shape, a_shape, a_shape),
        grid=(NB, nh // hb),
        in_specs=[qsp(fw), qsp(bw), gsp(fw), gsp(bw), gsp(fw), gsp(bw)],
        out_specs=(osp(fw), osp(bw), ssp(fw), ssp(bw), asp(fw), asp(bw)),
        scratch_shapes=[pltpu.VMEM((2, nh, GDN_DIM, GDN_DIM), f32)],
        compiler_params=_cparams(("arbitrary", "arbitrary")),
        name="gdn_scan_fwd",
    )(qkvn, qkvn, gates, gates, gc, gc)


def _gdn_bwd(qkvn, gates, gc, st_f, st_b, a_f, a_b, do):
    T = qkvn.shape[1]
    N = T // CHUNK
    nh, hb = GDN_HEADS, GDN_HB
    nb = nh // hb
    cps = min(GDN_CPS_BWD, N)
    NB = N // cps

    def body(*refs):
        n, b = pl.program_id(0), pl.program_id(1)
        ds_scr = refs[-1]

        @pl.when(n == 0)
        def _():
            ds_scr[0, pl.ds(b * hb, hb)] = jnp.zeros((hb, GDN_DIM, GDN_DIM), f32)
            ds_scr[1, pl.ds(b * hb, hb)] = jnp.zeros((hb, GDN_DIM, GDN_DIM), f32)

        for s in range(cps):
            chunk(*_chunk_views(refs[:-1], cps - 1 - s, s), ds_scr, b)

    def chunk(qf_ref, qb_ref, gf_ref, gb_ref, cf_ref, cb_ref, sf_ref, sb_ref, af_ref, ab_ref, dof_ref, dob_ref,
              dqf_ref, dqb_ref, dgf_ref, dgb_ref, ds_scr, b):
        hs = pl.ds(b * hb, hb)

        @pl.when(b == 0)
        def _():
            dgf_ref[...] = jnp.zeros((CHUNK, LANES), f32)
            dgb_ref[...] = jnp.zeros((CHUNK, LANES), f32)

        q, k, v, gcol, grow, beta = _gdn_chain_inputs(qf_ref, qb_ref, gf_ref, gb_ref, cf_ref, cb_ref, b, hb)
        ck = _GdnBatch(q, k, v, gcol, grow, beta, hb, A=jnp.concatenate([af_ref[0], ab_ref[0]], axis=0))
        S = jnp.concatenate([sf_ref[0], sb_ref[0]], axis=0)
        dS = jnp.concatenate([ds_scr[0, hs], ds_scr[1, hs]], axis=0)
        dov = jnp.stack(_heads(dof_ref, (), hb, GDN_DIM) + _heads(dob_ref, (), hb, GDN_DIM))
        v_new = ck.u - _mx(ck.w, S, "nn")
        d_vnew = _mx(ck.attn, dov, "tn") + _mx(ck.kd, dS, "nn")
        d_attn = jnp.where(ck.incl, _mx(dov, v_new, "nt"), 0.0)
        d_qd = _mx(dov, S, "nt")
        d_kd = _mx(v_new, dS, "nt")
        d_gl = jnp.sum(jnp.sum(dS * S, axis=2, keepdims=True), axis=1, keepdims=True) * ck.egl
        dS_new = dS * ck.egl + _mx(ck.qd, dov, "tn") - _mx(ck.w, d_vnew, "tn")
        ds_scr[0, hs] = dS_new[:hb]
        ds_scr[1, hs] = dS_new[hb:]
        d_w = -_mx(d_vnew, S, "nt")
        d_rhs = _h3s(ck.A2, _split2(jnp.concatenate([d_vnew, d_w], axis=2)), "tn")
        d_vb, d_kbg = d_rhs[:, :, :GDN_DIM], d_rhs[:, :, GDN_DIM:]
        dL = -jnp.where(ck.strict, _h3(d_rhs, ck.sol, "nt"), 0.0)
        P = dL * ck.decay
        Q = d_attn * ck.decay
        d_kb = _mx(P, k, "nn") + d_kbg * ck.eg
        d_qs = _mx(Q, k, "nn") + d_qd * ck.eg
        d_k = _mx(P, ck.kb, "tn") + _mx(Q, ck.qs, "tn") + d_kd * ck.ekd + d_kb * ck.beta
        E = dL * ck.L + d_attn * ck.attn
        kdsum = jnp.sum(d_kd * ck.kd, axis=2, keepdims=True)
        d_gc = (jnp.sum(E, axis=2, keepdims=True) - _colsum_col(E)
                + jnp.sum(d_kbg * ck.kbg, axis=2, keepdims=True)
                + jnp.sum(d_qd * ck.qd, axis=2, keepdims=True) - kdsum)
        d_gc = _add_at_last_row(d_gc, d_gl + jnp.sum(kdsum, axis=1, keepdims=True), hb)
        d_beta = jnp.sum(d_vb * v, axis=2, keepdims=True) + jnp.sum(d_kb * k, axis=2, keepdims=True)
        d_q = d_qs * (GDN_DIM**-0.5)
        d_v = d_vb * ck.beta
        lane = _iota2((CHUNK, LANES), 1)
        for d, (dq_ref, dg_ref) in enumerate(((dqf_ref, dgf_ref), (dqb_ref, dgb_ref))):
            acc = dg_ref[...]
            for hh in range(hb):
                c, h = d * hb + hh, b * hb + hh
                sl = slice(hh * GDN_DIM, (hh + 1) * GDN_DIM)
                dq_ref[0, :, sl] = d_q[c]
                dq_ref[1, :, sl] = d_k[c]
                dq_ref[2, :, sl] = d_v[c]
                acc = jnp.where(lane == h + 8 * d, d_gc[c], acc)
                acc = jnp.where(lane == 16 + h + 8 * d, d_beta[c], acc)
            dg_ref[...] = acc

        @pl.when(b == nb - 1)
        def _():
            for d, dg_ref in enumerate((dgf_ref, dgb_ref)):
                blk = dg_ref[...]
                dg_ref[...] = jnp.where(lane < 16, _exact01(_chunk_tri(CHUNK, d == 0), blk), blk)

    fw, bw = (lambda n: NB - 1 - n), (lambda n: n)
    qsp, gsp, osp, ssp, asp = _gdn_specs(cps, hb)
    dq_shape = jax.ShapeDtypeStruct((3, T, nh * GDN_DIM), f32)
    dg_shape = jax.ShapeDtypeStruct((T, LANES), f32)
    return pl.pallas_call(
        body,
        out_shape=(dq_shape, dq_shape, dg_shape, dg_shape),
        grid=(NB, nb),
        in_specs=[qsp(fw), qsp(bw), gsp(fw), gsp(bw), gsp(fw), gsp(bw), ssp(fw), ssp(bw), asp(fw), asp(bw), osp(fw), osp(bw)],
        out_specs=(qsp(fw), qsp(bw), gsp(fw), gsp(bw)),
        scratch_shapes=[pltpu.VMEM((2, nh, GDN_DIM, GDN_DIM), f32)],
        compiler_params=_cparams(("arbitrary", "arbitrary")),
        name="gdn_scan_bwd",
    )(qkvn, qkvn, gates, gates, gc, gc, st_f, st_b, a_f, a_b, do, do)


class _GlaBatch:
    def __init__(self, q, k, G, nf):
        B, Cn = q.shape[0], q.shape[1]
        self.incl, _, _ = _dir_masks(B, nf, Cn)
        self.eG = jnp.exp(G)
        self.enG = jnp.exp(-G)
        self.qg = q * (GLA_DK**-0.5) * self.eG
        self.kg = k * self.enG
        self.attn = jnp.where(self.incl, _mx(self.qg, self.kg, "nt"), 0.0)
        gl = _last_row(G, nf)
        self.ekd = jnp.exp(gl - G)
        self.kd = k * self.ekd
        self.egl = jnp.exp(gl)


GLA_CPS = 4


def _chunk_rows(s):
    return slice(s * CHUNK, (s + 1) * CHUNK)


def _gla_chain_inputs(refs_f, refs_b, rows_f, rows_b):
    nh = GLA_HEADS
    out = []
    for i, width in enumerate((GLA_DK, GLA_DK, GLA_DV, GLA_DK)):
        out.append(jnp.stack(_heads(refs_f[i], (), nh, width, rows_f) + _heads(refs_b[i], (), nh, width, rows_b)))
    return out


def _gla_specs(rows, ci):
    nh = GLA_HEADS
    return [
        pl.BlockSpec((rows, nh * GLA_DK), lambda n: (ci(n), 4096 // (nh * GLA_DK))),
        pl.BlockSpec((rows, nh * GLA_DK), lambda n: (ci(n), 4608 // (nh * GLA_DK))),
        pl.BlockSpec((rows, nh * GLA_DV), lambda n: (ci(n), 5120 // (nh * GLA_DV))),
        pl.BlockSpec((rows, nh * GLA_DK), lambda n: (ci(n), 0)),
    ]


def _gla_fwd(proj, G_f, G_b):
    T = proj.shape[0]
    N = T // CHUNK
    nh = GLA_HEADS
    cps = min(GLA_CPS, N)
    NB = N // cps

    def body(qf, kf, vf, gf, qb, kb, vb, gb, of_ref, ob_ref, sf_ref, sb_ref, s_scr):
        n = pl.program_id(0)

        @pl.when(n == 0)
        def _():
            s_scr[...] = jnp.zeros((2 * nh, GLA_DV, GLA_DK), f32)

        for s in range(cps):
            sb = cps - 1 - s
            q, k, v, G = _gla_chain_inputs((qf, kf, vf, gf), (qb, kb, vb, gb), _chunk_rows(s), _chunk_rows(sb))
            ck = _GlaBatch(q, k, G, nh)
            St = s_scr[...]
            sf_ref[s], sb_ref[sb] = St[:nh].astype(sf_ref.dtype), St[nh:].astype(sb_ref.dtype)
            o = _mx(ck.qg, St, "nt") + _mx(ck.attn, v, "nn")
            for i in range(nh):
                sl = slice(i * GLA_DV, (i + 1) * GLA_DV)
                of_ref[_chunk_rows(s), sl] = o[i]
                ob_ref[_chunk_rows(sb), sl] = o[nh + i]
            s_scr[...] = St * ck.egl + _mx(v, ck.kd, "tn")

    fw, bw = (lambda n: n), (lambda n: NB - 1 - n)
    osp = lambda ci: pl.BlockSpec((cps * CHUNK, nh * GLA_DV), lambda n: (ci(n), 0))
    ssp = lambda ci: pl.BlockSpec((cps, nh, GLA_DV, GLA_DK), lambda n: (ci(n), 0, 0, 0))
    o_shape = jax.ShapeDtypeStruct((T, nh * GLA_DV), f32)
    s_shape = jax.ShapeDtypeStruct((N, nh, GLA_DV, GLA_DK), MXU_DTYPE)
    return pl.pallas_call(
        body,
        out_shape=(o_shape, o_shape, s_shape, s_shape),
        grid=(NB,),
        in_specs=_gla_specs(cps * CHUNK, fw) + _gla_specs(cps * CHUNK, bw),
        out_specs=(osp(fw), osp(bw), ssp(fw), ssp(bw)),
        scratch_shapes=[pltpu.VMEM((2 * nh, GLA_DV, GLA_DK), f32)],
        compiler_params=_cparams(("arbitrary",)),
        name="gla_scan_fwd",
    )(proj, proj, proj, G_f, proj, proj, proj, G_b)


def _gla_bwd(proj, G_f, G_b, st_f, st_b, do):
    T = proj.shape[0]
    N = T // CHUNK
    nh = GLA_HEADS
    cps = min(GLA_CPS, N)
    NB = N // cps

    def body(qf, kf, vf, gf, sf_ref, dof_ref, qb, kb, vb, gb, sb_ref, dob_ref,
             dqf_ref, dkf_ref, dvf_ref, dgf_ref, dqb_ref, dkb_ref, dvb_ref, dgb_ref, ds_scr):
        n = pl.program_id(0)

        @pl.when(n == 0)
        def _():
            ds_scr[...] = jnp.zeros((2 * nh, GLA_DV, GLA_DK), f32)

        shp = (2 * nh, CHUNK, CHUNK)
        bb, ii, jj = _iota3(shp, 0), _iota3(shp, 1), _iota3(shp, 2)
        tri = jnp.where(jnp.logical_or(jnp.logical_and(bb < nh, jj >= ii), jnp.logical_and(bb >= nh, jj <= ii)), 1.0, 0.0)
        for s in range(cps):
            sf = cps - 1 - s
            rows = (_chunk_rows(sf), _chunk_rows(s))
            q, k, v, G = _gla_chain_inputs((qf, kf, vf, gf), (qb, kb, vb, gb), *rows)
            ck = _GlaBatch(q, k, G, nh)
            St = jnp.concatenate([sf_ref[sf], sb_ref[s]], axis=0).astype(f32)
            dSt = ds_scr[...]
            dov = jnp.stack(_heads(dof_ref, (), nh, GLA_DV, rows[0]) + _heads(dob_ref, (), nh, GLA_DV, rows[1]))
            d_attn = jnp.where(ck.incl, _mx(dov, v, "nt"), 0.0)
            d_qg = _mx(dov, St, "nn") + _mx(d_attn, ck.kg, "nn")
            d_kg = _mx(d_attn, ck.qg, "tn")
            d_v = _mx(ck.attn, dov, "tn") + _mx(ck.kd, dSt, "nt")
            d_kd = _mx(v, dSt, "nn")
            ds_scr[...] = dSt * ck.egl + _mx(dov, ck.qg, "tn")
            kdd = d_kd * ck.kd
            d_gl = jnp.sum(dSt * St, axis=1, keepdims=True) * ck.egl + jnp.sum(kdd, axis=1, keepdims=True)
            d_q = d_qg * ck.eG * (GLA_DK**-0.5)
            d_k = d_kg * ck.enG + d_kd * ck.ekd
            d_gk = _exact01(tri, _add_at_last_row(d_qg * ck.qg - d_kg * ck.kg - kdd, d_gl, nh))
            for d, (dq_ref, dk_ref, dv_ref, dg_ref) in enumerate(((dqf_ref, dkf_ref, dvf_ref, dgf_ref), (dqb_ref, dkb_ref, dvb_ref, dgb_ref))):
                for i in range(nh):
                    c = d * nh + i
                    ks, vs = slice(i * GLA_DK, (i + 1) * GLA_DK), slice(i * GLA_DV, (i + 1) * GLA_DV)
                    dq_ref[rows[d], ks] = d_q[c]
                    dk_ref[rows[d], ks] = d_k[c]
                    dv_ref[rows[d], vs] = d_v[c]
                    dg_ref[rows[d], ks] = d_gk[c]

    R = cps * CHUNK

    def in_specs(ci):
        return _gla_specs(R, ci) + [
            pl.BlockSpec((cps, nh, GLA_DV, GLA_DK), lambda n: (ci(n), 0, 0, 0)),
            pl.BlockSpec((R, nh * GLA_DV), lambda n: (ci(n), 0)),
        ]

    def out_specs(ci):
        ksp = pl.BlockSpec((R, nh * GLA_DK), lambda n: (ci(n), 0))
        return [ksp, ksp, pl.BlockSpec((R, nh * GLA_DV), lambda n: (ci(n), 0)), ksp]

    fw, bw = (lambda n: NB - 1 - n), (lambda n: n)
    k_shape = jax.ShapeDtypeStruct((T, nh * GLA_DK), f32)
    v_shape = jax.ShapeDtypeStruct((T, nh * GLA_DV), f32)
    return pl.pallas_call(
        body,
        out_shape=(k_shape, k_shape, v_shape, k_shape) * 2,
        grid=(NB,),
        in_specs=in_specs(fw) + in_specs(bw),
        out_specs=tuple(out_specs(fw) + out_specs(bw)),
        scratch_shapes=[pltpu.VMEM((2 * nh, GLA_DV, GLA_DK), f32)],
        compiler_params=_cparams(("arbitrary",)),
        name="gla_scan_bwd",
    )(proj, proj, proj, G_f, st_f, do, proj, proj, proj, G_b, st_b, do)


def _local_step(x, target, wcat, conv_w, avec, dvec, gdn_norm_w, w2f, w2b, b2f, b2b, gla_norm_w, proj_weights, w_pre, w_post,
                send_proj_grads=None, send_in_grads=None):
    h, h_t = _rms_pre(x, w_pre)
    proj = _matmul(h, wcat, "nt", f32, "proj_in", 2048, 512, 1024)
    qkvn = _gdn_prep(proj, conv_w)
    gates, gc, G_f, G_b = _gate_prep(proj, avec, dvec, w2f, w2b, b2f, b2b)
    oa_f, oa_b, sa_f, sa_b, inv_f, inv_b = _gdn_fwd(qkvn, gates, gc)
    og = _out_norm(oa_f, oa_b, proj, Z_BLK, gdn_norm_w, GDN_DIM, "gdn_out")
    ob_f, ob_b, sb_f, sb_b = _gla_fwd(proj, G_f, G_b)
    obg = _out_norm(ob_f, ob_b, proj, G_B_BLK, gla_norm_w, GLA_DV, "gla_out")
    wpg, wpl, wout = proj_weights(obg)
    (loss_row, d_wpost, dy, m, dout, dya, dyb, dproj, dog, dobg) = _mid(og, obg, proj, x, target, wpg, wpl, wout, w_post)

    d_wout = _matmul(m, dout, "tn", MXU_DTYPE, "dw_out", 1024, 512, 2048)
    d_wpg = _matmul(og, dya, "tn", MXU_DTYPE, "dw_proj_gdn", 1024, 512, 2048)
    d_wpl = _matmul(obg, dyb, "tn", MXU_DTYPE, "dw_proj_gla", 1024, 512, 2048)
    if send_proj_grads is not None:
        gla_norm_w = gla_norm_w + send_proj_grads(d_wpg, d_wpl, d_wout)

    do_b, dproj, d_gla_norm = _out_norm_bwd(dobg, ob_f, ob_b, proj, G_B_BLK, gla_norm_w, GLA_DV, "gla_out_bwd", dproj)
    dq_f, dk_f, dv_f, dgk_f, dq_b, dk_b, dv_b, dgk_b = _gla_bwd(proj, G_f, G_b, sb_f, sb_b, do_b)
    dproj, dsmall_gla, d_w2f, d_w2b, d_b2f, d_b2b = _gla_prep_bwd(proj, w2f, w2b, b2f, b2b, dgk_f, dgk_b,
                                                                 (dq_f, dq_b, dk_f, dk_b, dv_f, dv_b), dproj)

    do_a, dproj, d_gdn_norm = _out_norm_bwd(dog, oa_f, oa_b, proj, Z_BLK, gdn_norm_w, GDN_DIM, "gdn_out_bwd", dproj)
    dqkv_f, dqkv_b, dg_f, dg_b = _gdn_bwd(qkvn, gates, gc, sa_f, sa_b, inv_f, inv_b, do_a)
    dproj, d_conv = _gdn_prep_bwd(proj, conv_w, dqkv_f, dqkv_b, dproj)
    dproj, d_alog, d_dtb = _gdn_gates_bwd(proj, avec, dvec, dg_f, dg_b, dsmall_gla, dproj)

    d_wcat = _matmul(h_t, dproj, "nn", MXU_DTYPE, "dw_in", 1024, 512, 2048, out_t=True)
    started = None if send_in_grads is None else send_in_grads(d_wcat)
    dh = _matmul(dproj, wcat, "nn", f32, "dh", 512, 1024, N_CAT, after=started)
    grad_x, d_wpre = _rms_pre_bwd(x, w_pre, dh, dy)
    return dict(loss_row=loss_row, grad_x=grad_x, d_wcat=d_wcat, d_conv=d_conv, d_alog=d_alog, d_dtb=d_dtb,
                d_gdn_norm=d_gdn_norm, d_wpg=d_wpg, d_w2f=d_w2f, d_w2b=d_w2b, d_b2f=d_b2f, d_b2b=d_b2b,
                d_gla_norm=d_gla_norm, d_wpl=d_wpl, d_wout=d_wout, d_wpost=d_wpost, d_wpre=d_wpre)


def _to_cat(wf):
    pad = jnp.zeros((N_CAT - N_IN,) + wf.shape[1:], wf.dtype)
    return jnp.concatenate([wf[:4096], wf[4128:6176], wf[7232:9280], wf[6176:7200], wf[4096:4128], wf[7200:7232], pad], axis=0)


def _from_cat(wc):
    return jnp.concatenate([wc[:4096], wc[9216:9248], wc[4096:6144], wc[8192:9216], wc[9248:9280], wc[6144:8192]], axis=0)


def _lane_row(*pieces):
    row = jnp.concatenate(pieces, axis=1)
    return jnp.pad(row, ((0, 0), (0, LANES - row.shape[1])))


def _pad_w2(w2, row0):
    return jnp.pad(w2, ((row0, LANES - row0 - w2.shape[0]), (0, 0)))


SM_ROWS, SM_COLS = 32, 384


def _pack_small_shard(conv, w2f, w2b):
    lead = conv.shape[:-2]
    z = lambda r, c: jnp.zeros(lead + (r, c), f32)
    top = jnp.concatenate([conv, z(8 - CONV_K, SM_COLS)], axis=-2)
    mid = jnp.concatenate([w2f, w2b, z(16, SM_COLS - 128)], axis=-1)
    return jnp.concatenate([top, mid, z(SM_ROWS - 24, SM_COLS)], axis=-2)


SMALL_REGION = {
    "ln_pre_w": (0, 1, 0, 1024), "ln_post_w": (1, 1, 0, 1024), "gk_b2_fwd": (2, 1, 0, 512), "gk_b2_bwd": (2, 1, 512, 512),
    "gla_norm_w": (3, 1, 0, 256), "gdn_norm_w": (3, 1, 256, 128), "a_log_fwd": (4, 1, 0, 8), "a_log_bwd": (5, 1, 0, 8),
    "dt_bias_fwd": (6, 1, 0, 8), "dt_bias_bwd": (7, 1, 0, 8), "conv_w": (8, CONV_K, 0, SM_COLS),
    "gk_w2_fwd": (16, 16, 0, 64), "gk_w2_bwd": (16, 16, 128, 64),
}
SMALL_NAMES = tuple(SMALL_REGION)
SMALL_SHARDED = ("conv_w", "gk_w2_fwd", "gk_w2_bwd")
PACK_ROWS, PACK_COLS = 32, 1024
LOSS_AT = (13, 0)


def _region(name):
    r0, nr, l0, nl = SMALL_REGION[name]
    return slice(r0, r0 + nr), slice(l0, l0 + nl)


def _pack_small_grads(G):
    names = ("d_wpre", "d_wpost", "d_b2f", "d_b2b", "d_gla_norm", "d_gdn_norm", "d_alog", "d_dtb", "loss_row", "d_conv", "d_w2f", "d_w2b")

    def body(wpre, wpost, b2f, b2b, glan, gdnn, alog, dtb, lrow, conv, w2f, w2b, o_ref):
        o_ref[...] = jnp.zeros(o_ref.shape, f32)
        loss = 0.5 * jnp.sum(lrow[...], axis=1, keepdims=True) * (1.0 / D_MODEL)
        whole = {"ln_pre_w": wpre, "ln_post_w": wpost, "gk_b2_fwd": b2f, "gk_b2_bwd": b2b, "gla_norm_w": glan, "gdn_norm_w": gdnn}
        for j in range(N_DEV):
            for name, ref in whole.items():
                o_ref[(j,) + _region(name)] = ref[...]
            o_ref[(j,) + _region("a_log_fwd")] = alog[0:1, 0:8]
            o_ref[(j,) + _region("a_log_bwd")] = alog[0:1, 8:16]
            o_ref[(j,) + _region("dt_bias_fwd")] = dtb[0:1, 0:8]
            o_ref[(j,) + _region("dt_bias_bwd")] = dtb[0:1, 8:16]
            o_ref[(j,) + _region("conv_w")] = conv[:, j * SM_COLS : (j + 1) * SM_COLS]
            o_ref[(j,) + _region("gk_w2_fwd")] = w2f[32:48, j * 64 : (j + 1) * 64]
            o_ref[(j,) + _region("gk_w2_bwd")] = w2b[48:64, j * 64 : (j + 1) * 64]
            o_ref[j, LOSS_AT[0] : LOSS_AT[0] + 1, LOSS_AT[1] : LOSS_AT[1] + 1] = loss

    return pl.pallas_call(
        body, out_shape=jax.ShapeDtypeStruct((N_DEV, PACK_ROWS, PACK_COLS), f32), name="pack_small_grads",
    )(*[G[n] for n in names])


def _small_adam(land, own, W, M, V):
    bc1 = 1.0 - ADAM_B1**ADAM_STEP
    bc2 = 1.0 - ADAM_B2**ADAM_STEP
    n = len(SMALL_NAMES)

    def body(land_ref, own_ref, *refs):
        w_refs, m_refs, v_refs = refs[0:n], refs[n : 2 * n], refs[2 * n : 3 * n]
        outs, loss_ref, g_scr = refs[3 * n : 7 * n], refs[7 * n], refs[7 * n + 1]
        me = _my_device()
        g = jnp.where(me == 0, own_ref[0], land_ref[0])
        for j in range(1, N_DEV):
            g = g + jnp.where(me == j, own_ref[j], land_ref[j])
        g_scr[...] = g
        loss_ref[...] = g_scr[LOSS_AT[0] : LOSS_AT[0] + 1, LOSS_AT[1] : LOSS_AT[1] + 1]
        for i, name in enumerate(SMALL_NAMES):
            gp = g_scr[_region(name)]
            lead = (0,) if name in SMALL_SHARDED else (Ellipsis,)
            w, m, v = w_refs[i][lead], m_refs[i][lead], v_refs[i][lead]
            m2 = ADAM_B1 * m + (1.0 - ADAM_B1) * gp
            v2 = ADAM_B2 * v + (1.0 - ADAM_B2) * (gp * gp)
            delta = -ADAM_LR * ((m2 / bc1) / (jnp.sqrt(v2 / bc2) + ADAM_EPS) + ADAM_WD * w)
            for k, val in enumerate((gp, delta, m2, v2)):
                outs[4 * i + k][lead] = val

    shapes = [jax.ShapeDtypeStruct(W[name].shape, f32) for name in SMALL_NAMES for _ in range(4)]
    res = pl.pallas_call(
        body,
        out_shape=tuple(shapes) + (jax.ShapeDtypeStruct((1, 1), f32),),
        scratch_shapes=[pltpu.VMEM((PACK_ROWS, PACK_COLS), f32)],
        name="adam_small",
    )(land, own, *[P[name] for P in (W, M, V) for name in SMALL_NAMES])
    return {name: res[4 * i : 4 * i + 4] for i, name in enumerate(SMALL_NAMES)}, res[4 * n]


_MESH = pl.DeviceIdType.MESH


N_BIG_COPIES, N_SMALL_COPIES = 9, 7


def _gather_weights(w_in_s, sm_s):
    shapes = (w_in_s.shape, sm_s.shape)
    dtypes = (MXU_DTYPE, f32)

    def body(win_ref, sm_ref, gin_ref, gsm_ref, send_sems, recv_sems):
        x, y, c = lax.axis_index("x"), lax.axis_index("y"), lax.axis_index("c")
        me, sibling = (x, y, c), (x, y, 1 - c)
        chips = [(1 - x, y), (x, 1 - y), (1 - x, 1 - y)]

        def idx(px, py, pc):
            return 4 * px + 2 * py + pc

        for r in range(w_in_s.shape[1] // LANES):
            cols = slice(r * LANES, (r + 1) * LANES)
            gin_ref[idx(*me), :, cols] = win_ref[:, cols].astype(MXU_DTYPE)
        gsm_ref[idx(*me)] = sm_ref[...]

        def copy(sem, blk, to):
            return pltpu.make_async_remote_copy(src_ref=blk, dst_ref=blk, send_sem=send_sems.at[sem], recv_sem=recv_sems.at[sem],
                                                device_id=to, device_id_type=_MESH)

        def small(k, block, to):
            return copy(N_BIG_COPIES + k, gsm_ref.at[idx(*block)], to)

        half = w_in_s.shape[1] // 2
        xn, yn, dg = (1 - x, y), (x, 1 - y), (1 - x, 1 - y)

        def big(sem, block, to, part=None):
            blk = gin_ref.at[idx(*block)]
            return copy(sem, blk if part is None else blk.at[:, pl.ds(part * half, half)], to)

        started = [big(0, me, sibling), big(1, me, (*xn, c)), big(2, me, (*yn, c)), small(0, me, sibling)]
        started += [small(1 + j, me, (*chip, c)) for j, chip in enumerate(chips)]
        for cp in started:
            cp.start()

        def then(arrived, forwards):
            arrived.wait_recv()
            for cp in forwards:
                cp.start()
            started.extend(forwards)

        then(big(1, (*xn, c), me), [big(3, (*xn, c), (*yn, c), 1), big(5, (*xn, c), sibling)])
        then(big(2, (*yn, c), me), [big(4, (*yn, c), (*xn, c), 0), big(6, (*yn, c), sibling)])
        then(big(4, (*dg, c), me, 0), [big(7, (*dg, c), sibling, 0)])
        then(big(3, (*dg, c), me, 1), [big(8, (*dg, c), sibling, 1)])
        for j, chip in enumerate(chips):
            then(small(1 + j, (*chip, c), me), [small(4 + j, (*chip, c), sibling)])
        big(0, sibling, me).wait_recv()
        big(5, (*xn, 1 - c), me).wait_recv()
        big(6, (*yn, 1 - c), me).wait_recv()
        big(7, (*dg, 1 - c), me, 0).wait_recv()
        big(8, (*dg, 1 - c), me, 1).wait_recv()
        small(0, sibling, me).wait_recv()
        for j, chip in enumerate(chips):
            small(4 + j, (*chip, 1 - c), me).wait_recv()
        for cp in started:
            cp.wait_send()

    vm = pl.BlockSpec(memory_space=pltpu.VMEM)
    return pl.pallas_call(
        body,
        out_shape=tuple(jax.ShapeDtypeStruct((N_DEV,) + s, d) for s, d in zip(shapes, dtypes)),
        in_specs=[vm, vm],
        out_specs=(vm, vm),
        scratch_shapes=[pltpu.SemaphoreType.DMA((N_BIG_COPIES + N_SMALL_COPIES,)), pltpu.SemaphoreType.DMA((N_BIG_COPIES + N_SMALL_COPIES,))],
        compiler_params=pltpu.CompilerParams(vmem_limit_bytes=VMEM_LIMIT),
        name="gather_weights",
    )(w_in_s, sm_s)


N_CHIP = 4
_EFFECT = pltpu.SideEffectType.DATAFLOW_SIDE_EFFECTING
_HBM_SPEC = pl.BlockSpec(memory_space=pltpu.HBM)
_SEM_SPEC = pl.BlockSpec(memory_space=pltpu.SEMAPHORE)


def _relation_peer(k, x, y, c):
    px = (1 - x) if (k & 4) else x
    py = (1 - y) if (k & 2) else y
    pc = (1 - c) if (k & 1) else c
    return (px, py, pc), 4 * px + 2 * py + pc


_PLAN_COPIES = {"gather": N_DEV - 1, "blocks": N_DEV - 1, "sibling": 4, "chips": 3}
_PLAN_SLOTS = {"gather": N_DEV, "blocks": N_DEV, "sibling": 4, "chips": 4}


def _plan_copies(plan, src_ref, land_ref, send_sems, recv_sems, arrival):
    x, y, c = lax.axis_index("x"), lax.axis_index("y"), lax.axis_index("c")
    me, my_chip = 4 * x + 2 * y + c, 2 * x + y
    out = []

    def add(i, src, there, here, dev):
        out.append(pltpu.make_async_remote_copy(src_ref=src, dst_ref=land_ref.at[here if arrival else there], send_sem=send_sems.at[i],
                                                recv_sem=recv_sems.at[i], device_id=dev, device_id_type=_MESH))

    if plan in ("gather", "blocks"):
        for k in range(1, N_DEV):
            dev, p = _relation_peer(k, x, y, c)
            add(k - 1, src_ref.at[p] if plan == "blocks" else src_ref, me, p, dev)
    elif plan == "sibling":
        for j in range(4):
            add(j, src_ref.at[2 * j + 1 - c], j, j, (x, y, 1 - c))
    else:
        for i, k in enumerate((2, 4, 6)):
            (px, py, pc), _ = _relation_peer(k, x, y, c)
            add(i, src_ref.at[2 * px + py], my_chip, 2 * px + py, (px, py, pc))
    return out


def _exchange_start(src, plan, name, after=None):
    n = _PLAN_COPIES[plan]
    land = lax.empty((_PLAN_SLOTS[plan],) + (src.shape if plan == "gather" else src.shape[1:]), src.dtype)
    extra = [] if after is None else [after]

    def body(src_ref, land_ref, *rest):
        send_sems, recv_sems, _, _, token = rest[len(extra) :]
        for cp in _plan_copies(plan, src_ref, land_ref, send_sems, recv_sems, False):
            cp.start()
        token[...] = jnp.zeros_like(token)

    return pl.pallas_call(
        body,
        name=name,
        out_shape=(pltpu.SemaphoreType.DMA((n,)), pltpu.SemaphoreType.DMA((n,)), pltpu.HBM(src.shape, src.dtype),
                   pltpu.HBM(land.shape, land.dtype), jax.ShapeDtypeStruct((8, LANES), f32)),
        in_specs=(_HBM_SPEC, _HBM_SPEC) + (_ANY_SPEC,) * len(extra),
        out_specs=(_SEM_SPEC, _SEM_SPEC, _HBM_SPEC, _HBM_SPEC, pl.BlockSpec(memory_space=pltpu.VMEM)),
        input_output_aliases={0: 2, 1: 3},
        compiler_params=pltpu.CompilerParams(has_side_effects=_EFFECT),
    )(pltpu.with_memory_space_constraint(src, pltpu.HBM), pltpu.with_memory_space_constraint(land, pltpu.HBM), *extra)


def _exchange_wait(started, after, plan, name):
    send_sems, recv_sems, src_thru, land_thru, _ = started

    def body(src_ref, land_ref, send_sems, recv_sems, after_ref, src_dead, got_ref):
        for cp in _plan_copies(plan, src_ref, land_ref, send_sems, recv_sems, True):
            cp.wait_send()
            cp.wait_recv()

    return pl.pallas_call(
        body,
        name=name,
        out_shape=(pltpu.HBM(src_thru.shape, src_thru.dtype), pltpu.HBM(land_thru.shape, land_thru.dtype)),
        in_specs=(_HBM_SPEC, _HBM_SPEC, _SEM_SPEC, _SEM_SPEC, pl.BlockSpec(memory_space=pl.ANY)),
        out_specs=(_HBM_SPEC, _HBM_SPEC),
        input_output_aliases={0: 0, 1: 1},
        compiler_params=pltpu.CompilerParams(has_side_effects=_EFFECT),
    )(src_thru, land_thru, send_sems, recv_sems, after)


def _pair_sum(g, r, name, tc=LANES):
    _, R, Cc = g.shape

    def body(g_ref, r_ref, o_ref):
        o_ref[...] = (g_ref[...].astype(f32) + r_ref[...].astype(f32)).astype(o_ref.dtype)

    return pl.pallas_call(
        body,
        out_shape=jax.ShapeDtypeStruct(r.shape, r.dtype),
        grid=(N_CHIP, Cc // tc),
        in_specs=[pl.BlockSpec((1, R, tc), lambda k, i: (2 * k + lax.axis_index("c"), 0, i)), pl.BlockSpec((1, R, tc), lambda k, i: (k, 0, i))],
        out_specs=pl.BlockSpec((1, R, tc), lambda k, i: (k, 0, i)),
        compiler_params=_cparams(("parallel", "parallel")),
        name=name,
    )(g, r)


def _sum_adam(parts, w, m, v, name, tb, tc=None, own=None, own_slot=None):
    R, Cc = w.shape
    n_parts = parts.shape[0]
    tb = R if tc else min(tb, R)
    tc = tc or Cc
    assert R % tb == 0 and Cc % tc == 0
    bc1 = 1.0 - ADAM_B1**ADAM_STEP
    bc2 = 1.0 - ADAM_B2**ADAM_STEP
    extra = [] if own is None else [own]

    def body(p_ref, w_ref, m_ref, v_ref, *rest):
        g_ref, d_ref, nm_ref, nv_ref = rest[len(extra) :]

        def part(j):
            pj = p_ref[j].astype(f32)
            return pj if own is None else jnp.where(own_slot() == j, rest[0][0].astype(f32), pj)

        g = part(0)
        for j in range(1, n_parts):
            g = g + part(j)
        g_ref[...] = g
        m2 = ADAM_B1 * m_ref[...] + (1.0 - ADAM_B1) * g
        v2 = ADAM_B2 * v_ref[...] + (1.0 - ADAM_B2) * (g * g)
        nm_ref[...] = m2
        nv_ref[...] = v2
        d_ref[...] = -ADAM_LR * ((m2 / bc1) / (jnp.sqrt(v2 / bc2) + ADAM_EPS) + ADAM_WD * w_ref[...])

    blk = pl.BlockSpec((tb, tc), lambda i, j: (i, j))
    o = jax.ShapeDtypeStruct((R, Cc), f32)
    return pl.pallas_call(
        body,
        out_shape=(o, o, o, o),
        grid=(R // tb, Cc // tc),
        in_specs=[pl.BlockSpec((n_parts, tb, tc), lambda i, j: (0, i, j)), blk, blk, blk]
        + [pl.BlockSpec((1, tb, tc), lambda i, j: (own_slot(), i, j))] * len(extra),
        out_specs=(blk, blk, blk, blk),
        compiler_params=_cparams(("parallel", "parallel")),
        name=name,
    )(parts, w, m, v, *extra)


def _my_device():
    return 4 * lax.axis_index("x") + 2 * lax.axis_index("y") + lax.axis_index("c")


def _my_chip():
    return 2 * lax.axis_index("x") + lax.axis_index("y")


_WEIGHTS = ["ln_pre_w", "w_in", "conv_w", "a_log_fwd", "a_log_bwd", "dt_bias_fwd", "dt_bias_bwd", "gdn_norm_w", "w_proj_gdn",
            "gk_w2_fwd", "gk_b2_fwd", "gk_w2_bwd", "gk_b2_bwd", "gla_norm_w", "w_proj_gla", "w_out", "ln_post_w"]


def kernel(x, ln_pre_w, w_in, conv_w, a_log_fwd, a_log_bwd, dt_bias_fwd, dt_bias_bwd, gdn_norm_w, w_proj_gdn, gk_w2_fwd, gk_b2_fwd, gk_w2_bwd, gk_b2_bwd, gla_norm_w, w_proj_gla, w_out, ln_post_w, loss_target, m_ln_pre_w, m_w_in, m_conv_w, m_a_log_fwd, m_a_log_bwd, m_dt_bias_fwd, m_dt_bias_bwd, m_gdn_norm_w, m_w_proj_gdn, m_gk_w2_fwd, m_gk_b2_fwd, m_gk_w2_bwd, m_gk_b2_bwd, m_gla_norm_w, m_w_proj_gla, m_w_out, m_ln_post_w, v_ln_pre_w, v_w_in, v_conv_w, v_a_log_fwd, v_a_log_bwd, v_dt_bias_fwd, v_dt_bias_bwd, v_gdn_norm_w, v_w_proj_gdn, v_gk_w2_fwd, v_gk_b2_fwd, v_gk_w2_bwd, v_gk_b2_bwd, v_gla_norm_w, v_w_proj_gla, v_w_out, v_ln_post_w):
    args = locals()
    W = {n: args[n] for n in _WEIGHTS}
    M = {n: args["m_" + n] for n in _WEIGHTS}
    V = {n: args["v_" + n] for n in _WEIGHTS}

    wp_stack = lambda P: jnp.concatenate([P["w_proj_gdn"], P["w_proj_gla"], P["w_out"]], axis=0)
    sm_pack = lambda P: _pack_small_shard(P["conv_w"][0], P["gk_w2_fwd"][0], P["gk_w2_bwd"][0])
    me = 4 * lax.axis_index("x") + 2 * lax.axis_index("y") + lax.axis_index("c")
    w_in_t, m_in_t, v_in_t = (jnp.transpose(a[0]) for a in (w_in, m_w_in, v_w_in))
    g_in, g_sm = _gather_weights(w_in_t, sm_pack(W))
    wp_own = wp_stack(W).astype(MXU_DTYPE)
    wp_started = _exchange_start(wp_own, "gather", "gather_proj_start", after=g_sm)

    def proj_weights(after):
        own, land = _exchange_wait(wp_started, after, "gather", "gather_proj_wait")
        g_p = lax.dynamic_update_slice(land, own[None], (me, 0, 0, 0))
        return tuple(g_p[:, i].reshape(D_MODEL, D_MODEL) for i in range(3))

    rows_p = 3 * D_MODEL // N_DEV
    sent = {}

    def send_proj_grads(d_wpg, d_wpl, d_wout):
        p_p = jnp.stack([d_wpg, d_wpl, d_wout]).reshape(3, N_DEV, D_MODEL // N_DEV, D_MODEL).transpose(1, 0, 2, 3)
        sent["p_p"] = p_p.reshape(N_DEV, rows_p, D_MODEL)
        sent["started"] = _exchange_start(sent["p_p"], "blocks", "exchange_proj_start")
        return sent["started"][4][0:1, 0:1]

    def send_in_grads(d_wcat):
        p_in = _from_cat(d_wcat).reshape(N_DEV, SHARD_IN, D_MODEL)
        sib = _exchange_start(p_in, "sibling", "exchange_in_sibling_start")
        p_p, land_p = _exchange_wait(sent["started"], sib[4], "blocks", "exchange_proj_wait")
        sent["o_p"] = _sum_adam(land_p, wp_stack(W).reshape(rows_p, D_MODEL), wp_stack(M).reshape(rows_p, D_MODEL),
                                wp_stack(V).reshape(rows_p, D_MODEL), "adam_w_proj", 128, own=p_p, own_slot=_my_device)
        p_in, sib_land = _exchange_wait(sib, sent["o_p"][0], "sibling", "exchange_in_sibling_wait")
        sent["chips"] = _exchange_start(_pair_sum(p_in, sib_land, "pair_sum_w_in", tc=512), "chips", "exchange_in_chips_start")
        return sent["chips"][4]

    wcat = _to_cat(g_in.reshape(N_IN, D_MODEL))
    conv_full = g_sm[:, 0:CONV_K, :].transpose(1, 0, 2).reshape(CONV_K, N_DEV * SM_COLS)
    w2f_full = g_sm[:, 8:24, 0:64].transpose(1, 0, 2).reshape(16, 512)
    w2b_full = g_sm[:, 8:24, 64:128].transpose(1, 0, 2).reshape(16, 512)

    avec = _lane_row(a_log_fwd, a_log_bwd)
    dvec = _lane_row(dt_bias_fwd, dt_bias_bwd)
    G = _local_step(x[0], loss_target[0], wcat, conv_full, avec, dvec, gdn_norm_w, _pad_w2(w2f_full, 32), _pad_w2(w2b_full, 48),
                    gk_b2_fwd, gk_b2_bwd, gla_norm_w, proj_weights, ln_pre_w + wp_started[4][0:1, 0:1], ln_post_w, send_proj_grads,
                    send_in_grads)

    small_started = _exchange_start(_pack_small_grads(G), "blocks", "exchange_small_start")
    q_in, land_in = _exchange_wait(sent["chips"], small_started[4], "chips", "exchange_in_chips_wait")
    o_in = _sum_adam(land_in, w_in_t, m_in_t, v_in_t, "adam_w_in", SHARD_IN, tc=256, own=q_in, own_slot=_my_chip)
    o_p = sent["o_p"]
    p_small, land_small = _exchange_wait(small_started, o_in[0], "blocks", "exchange_small_wait")
    o_small, loss = _small_adam(land_small, p_small, W, M, V)

    res = []
    for kind in range(4):
        per = {name: o_small[name][kind] for name in SMALL_NAMES}
        per["w_in"] = jnp.transpose(o_in[kind])[None]
        pp = o_p[kind].reshape(3, 1, D_MODEL // N_DEV, D_MODEL)
        per["w_proj_gdn"], per["w_proj_gla"], per["w_out"] = pp[0], pp[1], pp[2]
        res.append([per[n] for n in _WEIGHTS])
    return (loss[0, 0], G["grad_x"][None], *res[0], *res[1], *res[2], *res[3])
```

```python
import jax
import jax.numpy as jnp
from jax import lax
from jax.experimental import pallas as pl
from jax.experimental.pallas import tpu as pltpu

f32 = jnp.float32
MXU_DTYPE = jnp.bfloat16

D_MODEL = 1024
CHUNK = 64
NORM_EPS = 1e-6
GDN_HEADS, GDN_DIM = 8, 128
GLA_HEADS, GLA_DK, GLA_DV = 4, 128, 256
GATE_NORMALIZER = 16.0
CONV_K = 5
N_IN = 9280
N_DEV = 8
SHARD_IN = N_IN // N_DEV

N_CAT = 9728
SMALL_OFF = 9216
LANES = 128

ADAM_LR, ADAM_B1, ADAM_B2, ADAM_EPS, ADAM_WD, ADAM_STEP = 0.001, 0.9, 0.999, 1e-08, 0.01, 10

VMEM_LIMIT = 56 * 1024 * 1024


def _cparams(sem=None):
    return pltpu.CompilerParams(dimension_semantics=sem, vmem_limit_bytes=VMEM_LIMIT)


_DN = {"nn": (((1,), (0,)), ((), ())), "nt": (((1,), (1,)), ((), ())), "tn": (((0,), (0,)), ((), ()))}
_BDN = {"nn": (((2,), (1,)), ((0,), (0,))), "nt": (((2,), (2,)), ((0,), (0,))), "tn": (((1,), (1,)), ((0,), (0,)))}


def _dot16(a, b, kind):
    return lax.dot_general(a, b, (_DN if a.ndim == 2 else _BDN)[kind], preferred_element_type=f32)


def _mx(a, b, kind):
    return _dot16(a.astype(MXU_DTYPE), b.astype(MXU_DTYPE), kind)


def _split2(a):
    hi = a.astype(jnp.bfloat16)
    return hi, (a - hi.astype(f32)).astype(jnp.bfloat16)


def _split3(a):
    p1 = a.astype(jnp.bfloat16)
    r = a - p1.astype(f32)
    p2 = r.astype(jnp.bfloat16)
    return p1, p2, (r - p2.astype(f32)).astype(jnp.bfloat16)


def _h3s(a2, b2, kind):
    (ah, al), (bh, bl) = a2, b2
    return _dot16(ah, bh, kind) + (_dot16(ah, bl, kind) + _dot16(al, bh, kind))


def _h3(a, b, kind):
    return _h3s(_split2(a), _split2(b), kind)


def _exact01(t01, x, kind="nn"):
    t = t01.astype(jnp.bfloat16)
    x1, x2, x3 = _split3(x)
    return _dot16(t, x1, kind) + (_dot16(t, x2, kind) + _dot16(t, x3, kind))


def _colsum_col(e):
    ones = jnp.ones(e.shape[:-1] + (LANES,), jnp.bfloat16)
    e1, e2, e3 = _split3(e)
    return (_dot16(e1, ones, "tn") + (_dot16(e2, ones, "tn") + _dot16(e3, ones, "tn")))[..., 0:1]


def _iota3(shape, axis):
    return lax.broadcasted_iota(jnp.int32, shape, axis)


def _dir_masks(B, nf, Cn):
    shp = (B, Cn, Cn)
    bb, ii, jj = _iota3(shp, 0), _iota3(shp, 1), _iota3(shp, 2)
    fwd = bb < nf
    rev = jnp.logical_not(fwd)
    incl = jnp.logical_or(jnp.logical_and(fwd, ii >= jj), jnp.logical_and(rev, ii <= jj))
    strict = jnp.logical_and(incl, ii != jj)
    return incl, strict, ii == jj


def _last_row(x, nf):
    B, Cn = x.shape[0], x.shape[1]
    fwd = _iota3((B, 1, 1), 0) < nf
    return jnp.where(fwd, x[:, Cn - 1 : Cn, :], x[:, 0:1, :])


def _add_at_last_row(x, val, nf):
    B, Cn = x.shape[0], x.shape[1]
    bb, rr = _iota3((B, Cn, 1), 0), _iota3((B, Cn, 1), 1)
    at = jnp.logical_or(jnp.logical_and(bb < nf, rr == Cn - 1), jnp.logical_and(bb >= nf, rr == 0))
    return x + jnp.where(at, val, 0.0)


def _sigmoid(x):
    return jax.nn.sigmoid(x)


def _silu(x):
    return x * _sigmoid(x)


def _silu_grad(x):
    s = _sigmoid(x)
    return s * (1.0 + x * (1.0 - s))


def _softplus(x):
    u = jnp.exp(-jnp.abs(x))
    l1p = jnp.where(u < 1e-3, u * (1.0 - u * (0.5 - u * (1.0 / 3.0))), jnp.log(1.0 + u))
    return jnp.maximum(x, 0.0) + l1p


def _mxr(x):
    return x.astype(MXU_DTYPE).astype(f32)


def _iota2(shape, axis):
    return lax.broadcasted_iota(jnp.int32, shape, axis)


def _matmul(a, b, kind, out_dtype, name, tm, tn, tk, out_t=False, after=None):
    extra = [] if after is None else [after]
    if kind == "nn":
        (M, K), N = a.shape, b.shape[1]
    elif kind == "nt":
        (M, K), N = a.shape, b.shape[0]
    else:
        (K, M), N = a.shape, b.shape[1]
    tm, tn, tk = min(tm, M), min(tn, N), min(tk, K)
    assert M % tm == 0 and N % tn == 0 and K % tk == 0, (name, M, N, K)
    nk = K // tk
    if kind == "tn":
        a_spec = pl.BlockSpec((tk, tm), lambda i, j, k: (k, i))
    else:
        a_spec = pl.BlockSpec((tm, tk), lambda i, j, k: (i, k))
    b_mode = dict(pipeline_mode=pl.Buffered(1)) if (tn == N and tk == K) else {}
    if kind == "nt":
        b_spec = pl.BlockSpec((tn, tk), lambda i, j, k: (j, k), **b_mode)
    else:
        b_spec = pl.BlockSpec((tk, tn), lambda i, j, k: (k, j), **b_mode)

    def body(a_ref, b_ref, *rest):
        o_ref, acc = rest[len(extra)], rest[len(extra) + 1 :]
        p = _mx(a_ref[...], b_ref[...], kind)
        if nk == 1:
            o_ref[...] = (p.T if out_t else p).astype(out_dtype)
        else:
            assert not out_t
            acc_ref = acc[0]
            k = pl.program_id(2)

            @pl.when(k == 0)
            def _():
                acc_ref[...] = p

            @pl.when(k > 0)
            def _():
                acc_ref[...] += p

            @pl.when(k == nk - 1)
            def _():
                o_ref[...] = acc_ref[...].astype(out_dtype)

    return pl.pallas_call(
        body,
        out_shape=jax.ShapeDtypeStruct((N, M) if out_t else (M, N), out_dtype),
        grid=(M // tm, N // tn, nk),
        in_specs=[a_spec, b_spec] + [pl.BlockSpec((8, LANES), lambda i, j, k: (0, 0))] * len(extra),
        out_specs=pl.BlockSpec((tn, tm), lambda i, j, k: (j, i)) if out_t else pl.BlockSpec((tm, tn), lambda i, j, k: (i, j)),
        scratch_shapes=[] if nk == 1 else [pltpu.VMEM((tm, tn), f32)],
        compiler_params=_cparams(("parallel", "parallel", "arbitrary")),
        name=name,
    )(a, b, *extra)


ROW_TILE = 512


def _rms_pre(x, w):
    T = x.shape[0]
    tb = min(ROW_TILE, T)

    def body(x_ref, w_ref, h_ref, ht_ref):
        xv = x_ref[...]
        r = lax.rsqrt(jnp.mean(xv * xv, axis=1, keepdims=True) + NORM_EPS)
        h = xv * r * w_ref[...]
        h_ref[...] = h.astype(h_ref.dtype)
        ht_ref[...] = h.T.astype(ht_ref.dtype)

    return pl.pallas_call(
        body,
        out_shape=(jax.ShapeDtypeStruct((T, D_MODEL), MXU_DTYPE), jax.ShapeDtypeStruct((D_MODEL, T), MXU_DTYPE)),
        grid=(T // tb,),
        in_specs=[pl.BlockSpec((tb, D_MODEL), lambda i: (i, 0)), pl.BlockSpec((1, D_MODEL), lambda i: (0, 0))],
        out_specs=(pl.BlockSpec((tb, D_MODEL), lambda i: (i, 0)), pl.BlockSpec((D_MODEL, tb), lambda i: (0, i))),
        compiler_params=_cparams(("parallel",)),
        name="rms_pre",
    )(x, w)


def _rms_pre_bwd(x, w, dh, dy):
    T = x.shape[0]
    tb = min(ROW_TILE, T)

    def body(x_ref, w_ref, dh_ref, dy_ref, dx_ref, dw_ref):
        i = pl.program_id(0)
        xv, dhv = x_ref[...], dh_ref[...]
        r = lax.rsqrt(jnp.mean(xv * xv, axis=1, keepdims=True) + NORM_EPS)
        dhw = dhv * w_ref[...]
        dx_ref[...] = dy_ref[...] + r * dhw - xv * (r * r * r) * jnp.mean(dhw * xv, axis=1, keepdims=True)
        part = jnp.sum(dhv * xv * r, axis=0, keepdims=True)

        @pl.when(i == 0)
        def _():
            dw_ref[...] = part

        @pl.when(i > 0)
        def _():
            dw_ref[...] += part

    blk = pl.BlockSpec((tb, D_MODEL), lambda i: (i, 0))
    row = pl.BlockSpec((1, D_MODEL), lambda i: (0, 0))
    return pl.pallas_call(
        body,
        out_shape=(jax.ShapeDtypeStruct((T, D_MODEL), f32), jax.ShapeDtypeStruct((1, D_MODEL), f32)),
        grid=(T // tb,),
        in_specs=[blk, row, blk, blk],
        out_specs=(blk, row),
        compiler_params=_cparams(("arbitrary",)),
        name="rms_pre_bwd",
    )(x, w, dh, dy)


SHIFTS = (-2, -1, 1, 2)


SHIFT_PAD = 8
PREP_HEADS, PREP_HEADS_BWD = 4, 2


def _shifted(x, pad_ref):
    T, cb = x.shape
    pad_ref[0:SHIFT_PAD] = jnp.zeros((SHIFT_PAD, cb), f32)
    pad_ref[SHIFT_PAD + T :] = jnp.zeros((SHIFT_PAD, cb), f32)
    pad_ref[SHIFT_PAD : SHIFT_PAD + T] = x
    return {d: pad_ref[SHIFT_PAD + d : SHIFT_PAD + d + T] for d in SHIFTS}


def _conv5(u, ush, cw):
    acc = u * cw[2:3, :]
    for j in (0, 1, 3, 4):
        acc = acc + ush[j - 2] * cw[j : j + 1, :]
    return acc


def _gdn_prep(proj, conv_w):
    T = proj.shape[0]
    hps, W = PREP_HEADS, PREP_HEADS * GDN_DIM
    per = GDN_HEADS // hps

    def body(u_ref, cw_ref, o_ref, pad_ref):
        j = pl.program_id(0)
        for i in range(hps):
            sl = slice(i * GDN_DIM, (i + 1) * GDN_DIM)
            u = _mxr(u_ref[:, sl])
            s = _silu(_conv5(u, _shifted(u, pad_ref), _mxr(cw_ref[:, sl])))
            o_ref[0, :, sl] = jnp.where(j < 2 * per, s * lax.rsqrt(jnp.sum(s * s, axis=1, keepdims=True) + NORM_EPS), s)

    return pl.pallas_call(
        body,
        out_shape=jax.ShapeDtypeStruct((3, T, GDN_HEADS * GDN_DIM), f32),
        grid=(3 * per,),
        in_specs=[pl.BlockSpec((T, W), lambda j: (0, j)), pl.BlockSpec((CONV_K, W), lambda j: (0, j))],
        out_specs=pl.BlockSpec((1, T, W), lambda j: (j // per, 0, j % per)),
        scratch_shapes=[pltpu.VMEM((T + 2 * SHIFT_PAD, GDN_DIM), f32)],
        compiler_params=_cparams(("parallel",)),
        name="gdn_prep",
    )(proj, conv_w)


def _gdn_prep_bwd(proj, conv_w, dqkv_f, dqkv_b, dproj):
    T = proj.shape[0]
    hps, W = PREP_HEADS_BWD, PREP_HEADS_BWD * GDN_DIM
    per = GDN_HEADS // hps

    def body(u_ref, cw_ref, df_ref, db_ref, _, du_ref, dcw_ref, upad_ref, dcpad_ref):
        j = pl.program_id(0)
        for i in range(hps):
            sl = slice(i * GDN_DIM, (i + 1) * GDN_DIM)
            u, cw = _mxr(u_ref[:, sl]), _mxr(cw_ref[:, sl])
            ush = _shifted(u, upad_ref)
            c = _conv5(u, ush, cw)
            s = _silu(c)
            dn = df_ref[0, :, sl] + db_ref[0, :, sl]
            rinv = lax.rsqrt(jnp.sum(s * s, axis=1, keepdims=True) + NORM_EPS)
            ds_norm = rinv * dn - s * (rinv * rinv * rinv) * jnp.sum(dn * s, axis=1, keepdims=True)
            ds = jnp.where(j < 2 * per, ds_norm, dn)
            dc = _mxr(ds * _silu_grad(c))
            dcsh = _shifted(dc, dcpad_ref)
            du = dc * cw[2:3, :]
            for jj in range(CONV_K):
                d = jj - 2
                dcw_ref[jj : jj + 1, sl] = jnp.sum(dc * (u if d == 0 else ush[d]), axis=0, keepdims=True)
                if d != 0:
                    du = du + dcsh[-d] * cw[jj : jj + 1, :]
            du_ref[:, sl] = du.astype(du_ref.dtype)

    qspec = pl.BlockSpec((1, T, W), lambda j: (j // per, 0, j % per))
    return pl.pallas_call(
        body,
        out_shape=(jax.ShapeDtypeStruct(dproj.shape, dproj.dtype), jax.ShapeDtypeStruct((CONV_K, 3 * GDN_HEADS * GDN_DIM), f32)),
        grid=(3 * per,),
        in_specs=[pl.BlockSpec((T, W), lambda j: (0, j)), pl.BlockSpec((CONV_K, W), lambda j: (0, j)), qspec, qspec, _ANY_SPEC],
        out_specs=(pl.BlockSpec((T, W), lambda j: (0, j)), pl.BlockSpec((CONV_K, W), lambda j: (0, j))),
        input_output_aliases={4: 0},
        scratch_shapes=[pltpu.VMEM((T + 2 * SHIFT_PAD, GDN_DIM), f32)] * 2,
        compiler_params=_cparams(("parallel",)),
        name="gdn_prep_bwd",
    )(proj, conv_w, dqkv_f, dqkv_b, dproj)


def _chunk_tri(n, rev):
    i, j = _iota2((n, n), 0), _iota2((n, n), 1)
    same = jnp.right_shift(i, 6) == jnp.right_shift(j, 6)
    order = (j >= i) if rev else (j <= i)
    return jnp.where(jnp.logical_and(same, order), 1.0, 0.0).astype(f32)


def _gate_prep(proj, avec, dvec, w2f, w2b, b2f, b2b, tb=256):
    T = proj.shape[0]
    W = GLA_HEADS * GLA_DK

    def body(s_ref, a_ref, d_ref, wf_ref, wb_ref, bf_ref, bb_ref, g_ref, gc_ref, gf_ref, gb_ref):
        small = s_ref[...]
        tri_f, tri_r = _chunk_tri(tb, False), _chunk_tri(tb, True)
        lane = _iota2(small.shape, 1)
        lg = -jnp.exp(a_ref[...]) * _softplus(small + d_ref[...])
        beta = _sigmoid(small)
        g_ref[...] = jnp.where(lane < 16, lg, jnp.where(lane < 32, beta, 0.0))
        lgm = jnp.where(lane < 16, lg, 0.0)
        gc_ref[...] = jnp.where(lane < 8, _exact01(tri_f, lgm), _exact01(tri_r, lgm))
        gkf = -_softplus(-(_mx(small, wf_ref[...], "nn") + bf_ref[...])) * (1.0 / GATE_NORMALIZER)
        gkb = -_softplus(-(_mx(small, wb_ref[...], "nn") + bb_ref[...])) * (1.0 / GATE_NORMALIZER)
        gf_ref[...] = _exact01(tri_f, gkf)
        gb_ref[...] = _exact01(tri_r, gkb)

    blk = pl.BlockSpec((tb, LANES), lambda i: (i, 0))
    row = pl.BlockSpec((1, LANES), lambda i: (0, 0))
    wide = pl.BlockSpec((tb, W), lambda i: (i, 0))
    wsp = pl.BlockSpec((LANES, W), lambda i: (0, 0))
    wrow = pl.BlockSpec((1, W), lambda i: (0, 0))
    return pl.pallas_call(
        body,
        out_shape=(jax.ShapeDtypeStruct((T, LANES), f32), jax.ShapeDtypeStruct((T, LANES), f32),
                   jax.ShapeDtypeStruct((T, W), f32), jax.ShapeDtypeStruct((T, W), f32)),
        grid=(T // tb,),
        in_specs=[pl.BlockSpec((tb, LANES), lambda i: (i, SMALL_OFF // LANES)), row, row, wsp, wsp, wrow, wrow],
        out_specs=(blk, blk, wide, wide),
        compiler_params=_cparams(("parallel",)),
        name="gate_prep",
    )(proj, avec, dvec, w2f, w2b, b2f, b2b)


def _gdn_gates_bwd(proj, avec, dvec, dg_f, dg_b, dsmall_gla, dproj):
    T = proj.shape[0]
    tb = min(ROW_TILE, T)
    pad = N_CAT - SMALL_OFF

    def body(s_ref, a_ref, d_ref, gf_ref, gb_ref, dl_ref, _, ds_ref, da_ref, dd_ref):
        i = pl.program_id(0)
        small = s_ref[...]
        lane = _iota2(small.shape, 1)
        dgate = gf_ref[...] + gb_ref[...]
        z = small + d_ref[...]
        nega = -jnp.exp(a_ref[...])
        dz = dgate * nega * _sigmoid(z)
        beta = _sigmoid(small)
        dsm = jnp.where(lane < 16, dz, jnp.where(lane < 32, dgate * beta * (1.0 - beta), 0.0))
        ds_ref[:, :LANES] = (dsm + dl_ref[...]).astype(ds_ref.dtype)
        ds_ref[:, LANES:] = jnp.zeros((tb, pad - LANES), ds_ref.dtype)
        lg = nega * _softplus(z)
        pa = jnp.sum(jnp.where(lane < 16, dgate * lg, 0.0), axis=0, keepdims=True)
        pd = jnp.sum(jnp.where(lane < 16, dz, 0.0), axis=0, keepdims=True)

        @pl.when(i == 0)
        def _():
            da_ref[...] = pa
            dd_ref[...] = pd

        @pl.when(i > 0)
        def _():
            da_ref[...] += pa
            dd_ref[...] += pd

    blk = pl.BlockSpec((tb, LANES), lambda i: (i, 0))
    row = pl.BlockSpec((1, LANES), lambda i: (0, 0))
    return pl.pallas_call(
        body,
        out_shape=(jax.ShapeDtypeStruct(dproj.shape, dproj.dtype), jax.ShapeDtypeStruct((1, LANES), f32), jax.ShapeDtypeStruct((1, LANES), f32)),
        grid=(T // tb,),
        in_specs=[pl.BlockSpec((tb, LANES), lambda i: (i, SMALL_OFF // LANES)), row, row, blk, blk, blk, _ANY_SPEC],
        out_specs=(pl.BlockSpec((tb, pad), lambda i: (i, DP_SMALL_BLK)), row, row),
        input_output_aliases={6: 0},
        compiler_params=_cparams(("arbitrary",)),
        name="gdn_gates_bwd",
    )(proj, avec, dvec, dg_f, dg_b, dsmall_gla, dproj)


GDN_HB = 8


def _inv_unit_lower(L, eye):
    A = eye - L
    P2 = _split2(L)
    for _ in range(3):
        P2 = _split2(_h3s(P2, P2, "nn"))
        A = A + _h3s(_split2(A), P2, "nn")
    Ph = P2[0]
    for _ in range(2):
        Ph = _dot16(Ph, Ph, "nn").astype(jnp.bfloat16)
        A = A + _dot16(A.astype(jnp.bfloat16), Ph, "nn")
    return A


def _lane_col(blk, idx):
    lane = _iota2(blk.shape, 1)
    return jnp.sum(jnp.where(lane == idx, blk, 0.0), axis=1, keepdims=True)


def _out_norm(o_f, o_b, proj, gate_blk, w, hd, name):
    T, W = o_f.shape
    tb = min(ROW_TILE, T)
    nh = W // hd

    def body(of_ref, ob_ref, z_ref, w_ref, y_ref):
        wv = w_ref[...]
        for i in range(nh):
            sl = slice(i * hd, (i + 1) * hd)
            o = of_ref[:, sl] + ob_ref[:, sl]
            r = lax.rsqrt(jnp.mean(o * o, axis=1, keepdims=True) + NORM_EPS)
            y_ref[:, sl] = (o * r * wv * _silu(z_ref[:, sl])).astype(y_ref.dtype)

    blk = pl.BlockSpec((tb, W), lambda i: (i, 0))
    return pl.pallas_call(
        body,
        out_shape=jax.ShapeDtypeStruct((T, W), MXU_DTYPE),
        grid=(T // tb,),
        in_specs=[blk, blk, pl.BlockSpec((tb, W), lambda i: (i, gate_blk)), pl.BlockSpec((1, hd), lambda i: (0, 0))],
        out_specs=blk,
        compiler_params=_cparams(("parallel",)),
        name=name,
    )(o_f, o_b, proj, w)


def _out_norm_bwd(dy, o_f, o_b, proj, gate_blk, w, hd, name, dproj):
    T, W = o_f.shape
    tb = min(ROW_TILE, T)
    nh = W // hd

    def body(dy_ref, of_ref, ob_ref, z_ref, w_ref, _, do_ref, dz_ref, dw_ref):
        i = pl.program_id(0)
        wv = w_ref[...]
        dw = jnp.zeros((1, hd), f32)
        for a in range(nh):
            sl = slice(a * hd, (a + 1) * hd)
            o = of_ref[:, sl] + ob_ref[:, sl]
            z = z_ref[:, sl]
            dyv = dy_ref[:, sl]
            r = lax.rsqrt(jnp.mean(o * o, axis=1, keepdims=True) + NORM_EPS)
            orr = o * r
            dn = dyv * _silu(z)
            dz_ref[:, sl] = (dyv * orr * wv * _silu_grad(z)).astype(dz_ref.dtype)
            dw = dw + jnp.sum(dn * orr, axis=0, keepdims=True)
            dnw = dn * wv
            do_ref[:, sl] = r * dnw - o * (r * r * r) * jnp.mean(dnw * o, axis=1, keepdims=True)

        @pl.when(i == 0)
        def _():
            dw_ref[...] = dw

        @pl.when(i > 0)
        def _():
            dw_ref[...] += dw

    blk = pl.BlockSpec((tb, W), lambda i: (i, 0))
    row = pl.BlockSpec((1, hd), lambda i: (0, 0))
    return pl.pallas_call(
        body,
        out_shape=(jax.ShapeDtypeStruct((T, W), f32), jax.ShapeDtypeStruct(dproj.shape, dproj.dtype), jax.ShapeDtypeStruct((1, hd), f32)),
        grid=(T // tb,),
        in_specs=[blk, blk, blk, pl.BlockSpec((tb, W), lambda i: (i, gate_blk)), row, _ANY_SPEC],
        out_specs=(blk, pl.BlockSpec((tb, W), lambda i: (i, gate_blk)), row),
        input_output_aliases={5: 1},
        compiler_params=_cparams(("arbitrary",)),
        name=name,
    )(dy, o_f, o_b, proj, w, dproj)


def _gla_prep_bwd(proj, w2f, w2b, b2f, b2b, dgk_f, dgk_b, dqkv, dproj):
    T = proj.shape[0]
    tb = min(ROW_TILE, T)
    W = GLA_HEADS * GLA_DK
    WV = GLA_HEADS * GLA_DV

    def body(s_ref, wf_ref, wb_ref, bf_ref, bb_ref, df_ref, db_ref, qf, qb, kf, kb, vf, vb, _,
             dp_ref, ds_ref, dwf_ref, dwb_ref, dbf_ref, dbb_ref):
        i = pl.program_id(0)
        dp_ref[:, :W] = (qf[...] + qb[...]).astype(dp_ref.dtype)
        dp_ref[:, W : 2 * W] = (kf[...] + kb[...]).astype(dp_ref.dtype)
        dp_ref[:, 2 * W :] = (vf[...] + vb[...]).astype(dp_ref.dtype)
        small = s_ref[...]
        dsm = jnp.zeros((tb, LANES), f32)
        parts = []
        for w_ref, b_ref, d_ref in ((wf_ref, bf_ref, df_ref), (wb_ref, bb_ref, db_ref)):
            pre = _mx(small, w_ref[...], "nn") + b_ref[...]
            dpre = d_ref[...] * (1.0 / GATE_NORMALIZER) * _sigmoid(-pre)
            dsm = dsm + _mx(dpre, w_ref[...], "nt")
            parts.append((_mx(small, dpre, "tn"), jnp.sum(dpre, axis=0, keepdims=True)))
        ds_ref[...] = dsm

        @pl.when(i == 0)
        def _():
            dwf_ref[...], dbf_ref[...] = parts[0]
            dwb_ref[...], dbb_ref[...] = parts[1]

        @pl.when(i > 0)
        def _():
            dwf_ref[...] += parts[0][0]
            dbf_ref[...] += parts[0][1]
            dwb_ref[...] += parts[1][0]
            dbb_ref[...] += parts[1][1]

    blk = pl.BlockSpec((tb, W), lambda i: (i, 0))
    vblk = pl.BlockSpec((tb, WV), lambda i: (i, 0))
    wsp = pl.BlockSpec((LANES, W), lambda i: (0, 0))
    row = pl.BlockSpec((1, W), lambda i: (0, 0))
    return pl.pallas_call(
        body,
        out_shape=(jax.ShapeDtypeStruct(dproj.shape, dproj.dtype), jax.ShapeDtypeStruct((T, LANES), f32), jax.ShapeDtypeStruct((LANES, W), f32),
                   jax.ShapeDtypeStruct((LANES, W), f32), jax.ShapeDtypeStruct((1, W), f32), jax.ShapeDtypeStruct((1, W), f32)),
        grid=(T // tb,),
        in_specs=[pl.BlockSpec((tb, LANES), lambda i: (i, SMALL_OFF // LANES)), wsp, wsp, row, row, blk, blk,
                  blk, blk, blk, blk, vblk, vblk, _ANY_SPEC],
        out_specs=(pl.BlockSpec((tb, 2 * W + WV), lambda i: (i, DP_GLA_QKV_BLK)), pl.BlockSpec((tb, LANES), lambda i: (i, 0)), wsp, wsp, row, row),
        input_output_aliases={13: 0},
        compiler_params=_cparams(("arbitrary",)),
        name="gla_prep_bwd",
    )(proj, w2f, w2b, b2f, b2b, dgk_f, dgk_b, *dqkv, dproj)


GATE_A_BLK, GATE_B_BLK, Z_BLK, G_B_BLK = 6144 // 1024, 7168 // 1024, 3072 // 1024, 8192 // 1024
DP_GATES_BLK, DP_GLA_QKV_BLK, DP_SMALL_BLK = 6144 // 2048, 4096 // 2048, SMALL_OFF // 512
_ANY_SPEC = pl.BlockSpec(memory_space=pl.ANY)


def _mid(og, obg, proj, x, target, wpg, wpl, wout, w_post, tb=256):
    T = x.shape[0]
    Dm = D_MODEL

    def body(og_ref, obg_ref, ga_ref, gb_ref, x_ref, t_ref, wpg_ref, wpl_ref, wo_ref, wp_ref,
             loss_ref, dwp_ref, dy_ref, m_ref, dout_ref, dya_ref, dyb_ref, dgates_ref, dog_ref, dobg_ref):
        i = pl.program_id(0)
        ya = _mx(og_ref[...], wpg_ref[...], "nn")
        yb = _mx(obg_ref[...], wpl_ref[...], "nn")
        sa, sb = _sigmoid(ga_ref[...]), _sigmoid(gb_ref[...])
        m = sa * ya + sb * yb
        m_ref[...] = m.astype(m_ref.dtype)
        out = _mx(m, wo_ref[...], "nn")
        r = lax.rsqrt(jnp.mean(out * out, axis=1, keepdims=True) + NORM_EPS)
        wp = wp_ref[...]
        e = x_ref[...] + out * r * wp - t_ref[...]
        dy = e * (1.0 / Dm)
        dy_ref[...] = dy
        lpart = jnp.sum(e * e, axis=0, keepdims=True)
        wpart = jnp.sum(dy * out * r, axis=0, keepdims=True)
        dyn = dy * wp
        dout = r * dyn - out * (r * r * r) * jnp.mean(dyn * out, axis=1, keepdims=True)
        dout_ref[...] = dout.astype(dout_ref.dtype)
        dm = _mx(dout, wo_ref[...], "nt")
        dya, dyb = dm * sa, dm * sb
        dya_ref[...] = dya.astype(dya_ref.dtype)
        dyb_ref[...] = dyb.astype(dyb_ref.dtype)
        dgates_ref[:, :Dm] = (dm * ya * sa * (1.0 - sa)).astype(dgates_ref.dtype)
        dgates_ref[:, Dm:] = (dm * yb * sb * (1.0 - sb)).astype(dgates_ref.dtype)
        dog_ref[...] = _mx(dya, wpg_ref[...], "nt")
        dobg_ref[...] = _mx(dyb, wpl_ref[...], "nt")

        @pl.when(i == 0)
        def _():
            loss_ref[...] = lpart
            dwp_ref[...] = wpart

        @pl.when(i > 0)
        def _():
            loss_ref[...] += lpart
            dwp_ref[...] += wpart

    blk = pl.BlockSpec((tb, Dm), lambda i: (i, 0))
    row = pl.BlockSpec((1, Dm), lambda i: (0, 0))
    wsp = pl.BlockSpec((Dm, Dm), lambda i: (0, 0))
    act = jax.ShapeDtypeStruct((T, Dm), MXU_DTYPE)
    big = jax.ShapeDtypeStruct((T, Dm), f32)
    vec = jax.ShapeDtypeStruct((1, Dm), f32)
    return pl.pallas_call(
        body,
        out_shape=(vec, vec, big, act, act, act, act, jax.ShapeDtypeStruct((T, N_CAT), MXU_DTYPE), big, big),
        grid=(T // tb,),
        in_specs=[blk, blk, pl.BlockSpec((tb, Dm), lambda i: (i, GATE_A_BLK)), pl.BlockSpec((tb, Dm), lambda i: (i, GATE_B_BLK)),
                  blk, blk, wsp, wsp, wsp, row],
        out_specs=(row, row) + (blk,) * 5 + (pl.BlockSpec((tb, 2 * Dm), lambda i: (i, DP_GATES_BLK)), blk, blk),
        compiler_params=_cparams(("arbitrary",)),
        name="mid",
    )(og, obg, proj, proj, x, target, wpg, wpl, wout, w_post)


def _heads(ref, lead, n, width, rows=slice(None)):
    return [ref[lead + (rows, slice(i * width, (i + 1) * width))] for i in range(n)]


class _GdnBatch:
    def __init__(self, q, k, v, gcol, grow, beta, nf, A=None):
        B, Cn = q.shape[0], q.shape[1]
        self.incl, self.strict, eye = _dir_masks(B, nf, Cn)
        self.decay = jnp.where(self.incl, jnp.exp(jnp.where(self.incl, gcol - grow, 0.0)), 0.0)
        self.k, self.v = k, v
        gl = _last_row(gcol, nf)
        self.beta = jnp.broadcast_to(beta, q.shape)
        self.eg = jnp.broadcast_to(jnp.exp(gcol), q.shape)
        self.ekd = jnp.broadcast_to(jnp.exp(gl - gcol), q.shape)
        self.qs = q * (GDN_DIM**-0.5)
        self.kb = k * self.beta
        self.L = jnp.where(self.strict, _mx(self.kb, k, "nt") * self.decay, 0.0)
        if A is None:
            A = _inv_unit_lower(self.L, jnp.where(eye, 1.0, 0.0).astype(f32))
        self.A = A
        self.A2 = _split2(A)
        self.kbg = self.kb * self.eg
        self.sol = _h3s(self.A2, _split2(jnp.concatenate([v * self.beta, self.kbg], axis=2)), "nn")
        self.u = self.sol[:, :, :GDN_DIM]
        self.w = self.sol[:, :, GDN_DIM:]
        self.attn = jnp.where(self.incl, _mx(self.qs, k, "nt") * self.decay, 0.0)
        self.qd = self.qs * self.eg
        self.kd = k * self.ekd
        self.egl = jnp.exp(gl)


def _gdn_chain_inputs(qf_ref, qb_ref, gf_ref, gb_ref, cf_ref, cb_ref, b, hb):
    qkv = [jnp.stack(_heads(qf_ref, (i,), hb, GDN_DIM) + _heads(qb_ref, (i,), hb, GDN_DIM)) for i in range(3)]
    cols, rows, betas = [], [], []
    for d, (g_ref, c_ref) in enumerate(((gf_ref, cf_ref), (gb_ref, cb_ref))):
        gates, gc = g_ref[...], c_ref[...]
        for hh in range(hb):
            h = b * hb + hh
            col = _lane_col(gc, h + 8 * d)
            cols.append(col)
            rows.append(jnp.broadcast_to(col, (CHUNK, LANES)).T[0:1, :])
            betas.append(_lane_col(gates, 16 + h + 8 * d))
    return qkv[0], qkv[1], qkv[2], jnp.stack(cols), jnp.stack(rows), jnp.stack(betas)


GDN_CPS_FWD, GDN_CPS_BWD = 2, 1


def _chunk_views(refs, s_f, s_b):
    out = []
    for i, r in enumerate(refs):
        s = s_f if i % 2 == 0 else s_b
        rows = pl.ds(s * CHUNK, CHUNK)
        out.append(r.at[pl.ds(s, 1)] if len(r.shape) == 4 else (r.at[:, rows, :] if len(r.shape) == 3 else r.at[rows, :]))
    return out


def _gdn_specs(cps, hb):
    R, W = cps * CHUNK, hb * GDN_DIM
    qsp = lambda ci: pl.BlockSpec((3, R, W), lambda n, b: (0, ci(n), b))
    gsp = lambda ci: pl.BlockSpec((R, LANES), lambda n, b: (ci(n), 0))
    osp = lambda ci: pl.BlockSpec((R, W), lambda n, b: (ci(n), b))
    ssp = lambda ci: pl.BlockSpec((cps, hb, GDN_DIM, GDN_DIM), lambda n, b: (ci(n), b, 0, 0))
    asp = lambda ci: pl.BlockSpec((cps, hb, CHUNK, CHUNK), lambda n, b: (ci(n), b, 0, 0))
    return qsp, gsp, osp, ssp, asp


def _gdn_fwd(qkvn, gates, gc):
    T = qkvn.shape[1]
    N = T // CHUNK
    nh, hb = GDN_HEADS, GDN_HB
    cps = min(GDN_CPS_FWD, N)
    NB = N // cps

    def body(*refs):
        n, b = pl.program_id(0), pl.program_id(1)
        s_scr = refs[-1]

        @pl.when(n == 0)
        def _():
            s_scr[0, pl.ds(b * hb, hb)] = jnp.zeros((hb, GDN_DIM, GDN_DIM), f32)
            s_scr[1, pl.ds(b * hb, hb)] = jnp.zeros((hb, GDN_DIM, GDN_DIM), f32)

        for s in range(cps):
            chunk(*_chunk_views(refs[:-1], s, cps - 1 - s), s_scr, b)

    def chunk(qf_ref, qb_ref, gf_ref, gb_ref, cf_ref, cb_ref, of_ref, ob_ref, sf_ref, sb_ref, af_ref, ab_ref, s_scr, b):
        hs = pl.ds(b * hb, hb)
        q, k, v, gcol, grow, beta = _gdn_chain_inputs(qf_ref, qb_ref, gf_ref, gb_ref, cf_ref, cb_ref, b, hb)
        ck = _GdnBatch(q, k, v, gcol, grow, beta, hb)
        S = jnp.concatenate([s_scr[0, hs], s_scr[1, hs]], axis=0)
        sf_ref[0], sb_ref[0] = S[:hb], S[hb:]
        af_ref[0], ab_ref[0] = ck.A[:hb], ck.A[hb:]
        v_new = ck.u - _mx(ck.w, S, "nn")
        o = _mx(ck.qd, S, "nn") + _mx(ck.attn, v_new, "nn")
        S_new = S * ck.egl + _mx(ck.kd, v_new, "tn")
        for hh in range(hb):
            sl = slice(hh * GDN_DIM, (hh + 1) * GDN_DIM)
            of_ref[:, sl] = o[hh]
            ob_ref[:, sl] = o[hb + hh]
        s_scr[0, hs] = S_new[:hb]
        s_scr[1, hs] = S_new[hb:]

    fw, bw = (lambda n: n), (lambda n: NB - 1 - n)
    qsp, gsp, osp, ssp, asp = _gdn_specs(cps, hb)
    o_shape = jax.ShapeDtypeStruct((T, nh * GDN_DIM), f32)
    s_shape = jax.ShapeDtypeStruct((N, nh, GDN_DIM, GDN_DIM), f32)
    a_shape = jax.ShapeDtypeStruct((N, nh, CHUNK, CHUNK), f32)
    return pl.pallas_call(
        body,
        out_shape=(o_shape, o_shape, s_shape, s_shape, a_shape, a_shape),
        grid=(NB, nh // hb),
        in_specs=[qsp(fw), qsp(bw), gsp(fw), gsp(bw), gsp(fw), gsp(bw)],
        out_specs=(osp(fw), osp(bw), ssp(fw), ssp(bw), asp(fw), asp(bw)),
        scratch_shapes=[pltpu.VMEM((2, nh, GDN_DIM, GDN_DIM), f32)],
        compiler_params=_cparams(("arbitrary", "arbitrary")),
        name="gdn_scan_fwd",
    )(qkvn, qkvn, gates, gates, gc, gc)


def _gdn_bwd(qkvn, gates, gc, st_f, st_b, a_f, a_b, do):
    T = qkvn.shape[1]
    N = T // CHUNK
    nh, hb = GDN_HEADS, GDN_HB
    nb = nh // hb
    cps = min(GDN_CPS_BWD, N)
    NB = N // cps

    def body(*refs):
        n, b = pl.program_id(0), pl.program_id(1)
        ds_scr = refs[-1]

        @pl.when(n == 0)
        def _():
            ds_scr[0, pl.ds(b * hb, hb)] = jnp.zeros((hb, GDN_DIM, GDN_DIM), f32)
            ds_scr[1, pl.ds(b * hb, hb)] = jnp.zeros((hb, GDN_DIM, GDN_DIM), f32)

        for s in range(cps):
            chunk(*_chunk_views(refs[:-1], cps - 1 - s, s), ds_scr, b)

    def chunk(qf_ref, qb_ref, gf_ref, gb_ref, cf_ref, cb_ref, sf_ref, sb_ref, af_ref, ab_ref, dof_ref, dob_ref,
              dqf_ref, dqb_ref, dgf_ref, dgb_ref, ds_scr, b):
        hs = pl.ds(b * hb, hb)

        @pl.when(b == 0)
        def _():
            dgf_ref[...] = jnp.zeros((CHUNK, LANES), f32)
            dgb_ref[...] = jnp.zeros((CHUNK, LANES), f32)

        q, k, v, gcol, grow, beta = _gdn_chain_inputs(qf_ref, qb_ref, gf_ref, gb_ref, cf_ref, cb_ref, b, hb)
        ck = _GdnBatch(q, k, v, gcol, grow, beta, hb, A=jnp.concatenate([af_ref[0], ab_ref[0]], axis=0))
        S = jnp.concatenate([sf_ref[0], sb_ref[0]], axis=0)
        dS = jnp.concatenate([ds_scr[0, hs], ds_scr[1, hs]], axis=0)
        dov = jnp.stack(_heads(dof_ref, (), hb, GDN_DIM) + _heads(dob_ref, (), hb, GDN_DIM))
        v_new = ck.u - _mx(ck.w, S, "nn")
        d_vnew = _mx(ck.attn, dov, "tn") + _mx(ck.kd, dS, "nn")
        d_attn = jnp.where(ck.incl, _mx(dov, v_new, "nt"), 0.0)
        d_qd = _mx(dov, S, "nt")
        d_kd = _mx(v_new, dS, "nt")
        d_gl = jnp.sum(jnp.sum(dS * S, axis=2, keepdims=True), axis=1, keepdims=True) * ck.egl
        dS_new = dS * ck.egl + _mx(ck.qd, dov, "tn") - _mx(ck.w, d_vnew, "tn")
        ds_scr[0, hs] = dS_new[:hb]
        ds_scr[1, hs] = dS_new[hb:]
        d_w = -_mx(d_vnew, S, "nt")
        d_rhs = _h3s(ck.A2, _split2(jnp.concatenate([d_vnew, d_w], axis=2)), "tn")
        d_vb, d_kbg = d_rhs[:, :, :GDN_DIM], d_rhs[:, :, GDN_DIM:]
        dL = -jnp.where(ck.strict, _h3(d_rhs, ck.sol, "nt"), 0.0)
        P = dL * ck.decay
        Q = d_attn * ck.decay
        d_kb = _mx(P, k, "nn") + d_kbg * ck.eg
        d_qs = _mx(Q, k, "nn") + d_qd * ck.eg
        d_k = _mx(P, ck.kb, "tn") + _mx(Q, ck.qs, "tn") + d_kd * ck.ekd + d_kb * ck.beta
        E = dL * ck.L + d_attn * ck.attn
        kdsum = jnp.sum(d_kd * ck.kd, axis=2, keepdims=True)
        d_gc = (jnp.sum(E, axis=2, keepdims=True) - _colsum_col(E)
                + jnp.sum(d_kbg * ck.kbg, axis=2, keepdims=True)
                + jnp.sum(d_qd * ck.qd, axis=2, keepdims=True) - kdsum)
        d_gc = _add_at_last_row(d_gc, d_gl + jnp.sum(kdsum, axis=1, keepdims=True), hb)
        d_beta = jnp.sum(d_vb * v, axis=2, keepdims=True) + jnp.sum(d_kb * k, axis=2, keepdims=True)
        d_q = d_qs * (GDN_DIM**-0.5)
        d_v = d_vb * ck.beta
        lane = _iota2((CHUNK, LANES), 1)
        for d, (dq_ref, dg_ref) in enumerate(((dqf_ref, dgf_ref), (dqb_ref, dgb_ref))):
            acc = dg_ref[...]
            for hh in range(hb):
                c, h = d * hb + hh, b * hb + hh
                sl = slice(hh * GDN_DIM, (hh + 1) * GDN_DIM)
                dq_ref[0, :, sl] = d_q[c]
                dq_ref[1, :, sl] = d_k[c]
                dq_ref[2, :, sl] = d_v[c]
                acc = jnp.where(lane == h + 8 * d, d_gc[c], acc)
                acc = jnp.where(lane == 16 + h + 8 * d, d_beta[c], acc)
            dg_ref[...] = acc

        @pl.when(b == nb - 1)
        def _():
            for d, dg_ref in enumerate((dgf_ref, dgb_ref)):
                blk = dg_ref[...]
                dg_ref[...] = jnp.where(lane < 16, _exact01(_chunk_tri(CHUNK, d == 0), blk), blk)

    fw, bw = (lambda n: NB - 1 - n), (lambda n: n)
    qsp, gsp, osp, ssp, asp = _gdn_specs(cps, hb)
    dq_shape = jax.ShapeDtypeStruct((3, T, nh * GDN_DIM), f32)
    dg_shape = jax.ShapeDtypeStruct((T, LANES), f32)
    return pl.pallas_call(
        body,
        out_shape=(dq_shape, dq_shape, dg_shape, dg_shape),
        grid=(NB, nb),
        in_specs=[qsp(fw), qsp(bw), gsp(fw), gsp(bw), gsp(fw), gsp(bw), ssp(fw), ssp(bw), asp(fw), asp(bw), osp(fw), osp(bw)],
        out_specs=(qsp(fw), qsp(bw), gsp(fw), gsp(bw)),
        scratch_shapes=[pltpu.VMEM((2, nh, GDN_DIM, GDN_DIM), f32)],
        compiler_params=_cparams(("arbitrary", "arbitrary")),
        name="gdn_scan_bwd",
    )(qkvn, qkvn, gates, gates, gc, gc, st_f, st_b, a_f, a_b, do, do)


class _GlaBatch:
    def __init__(self, q, k, G, nf):
        B, Cn = q.shape[0], q.shape[1]
        self.incl, _, _ = _dir_masks(B, nf, Cn)
        self.eG = jnp.exp(G)
        self.enG = jnp.exp(-G)
        self.qg = q * (GLA_DK**-0.5) * self.eG
        self.kg = k * self.enG
        self.attn = jnp.where(self.incl, _mx(self.qg, self.kg, "nt"), 0.0)
        gl = _last_row(G, nf)
        self.ekd = jnp.exp(gl - G)
        self.kd = k * self.ekd
        self.egl = jnp.exp(gl)


GLA_CPS = 4


def _chunk_rows(s):
    return slice(s * CHUNK, (s + 1) * CHUNK)


def _gla_chain_inputs(refs_f, refs_b, rows_f, rows_b):
    nh = GLA_HEADS
    out = []
    for i, width in enumerate((GLA_DK, GLA_DK, GLA_DV, GLA_DK)):
        out.append(jnp.stack(_heads(refs_f[i], (), nh, width, rows_f) + _heads(refs_b[i], (), nh, width, rows_b)))
    return out


def _gla_specs(rows, ci):
    nh = GLA_HEADS
    return [
        pl.BlockSpec((rows, nh * GLA_DK), lambda n: (ci(n), 4096 // (nh * GLA_DK))),
        pl.BlockSpec((rows, nh * GLA_DK), lambda n: (ci(n), 4608 // (nh * GLA_DK))),
        pl.BlockSpec((rows, nh * GLA_DV), lambda n: (ci(n), 5120 // (nh * GLA_DV))),
        pl.BlockSpec((rows, nh * GLA_DK), lambda n: (ci(n), 0)),
    ]


def _gla_fwd(proj, G_f, G_b):
    T = proj.shape[0]
    N = T // CHUNK
    nh = GLA_HEADS
    cps = min(GLA_CPS, N)
    NB = N // cps

    def body(qf, kf, vf, gf, qb, kb, vb, gb, of_ref, ob_ref, sf_ref, sb_ref, s_scr):
        n = pl.program_id(0)

        @pl.when(n == 0)
        def _():
            s_scr[...] = jnp.zeros((2 * nh, GLA_DV, GLA_DK), f32)

        for s in range(cps):
            sb = cps - 1 - s
            q, k, v, G = _gla_chain_inputs((qf, kf, vf, gf), (qb, kb, vb, gb), _chunk_rows(s), _chunk_rows(sb))
            ck = _GlaBatch(q, k, G, nh)
            St = s_scr[...]
            sf_ref[s], sb_ref[sb] = St[:nh].astype(sf_ref.dtype), St[nh:].astype(sb_ref.dtype)
            o = _mx(ck.qg, St, "nt") + _mx(ck.attn, v, "nn")
            for i in range(nh):
                sl = slice(i * GLA_DV, (i + 1) * GLA_DV)
                of_ref[_chunk_rows(s), sl] = o[i]
                ob_ref[_chunk_rows(sb), sl] = o[nh + i]
            s_scr[...] = St * ck.egl + _mx(v, ck.kd, "tn")

    fw, bw = (lambda n: n), (lambda n: NB - 1 - n)
    osp = lambda ci: pl.BlockSpec((cps * CHUNK, nh * GLA_DV), lambda n: (ci(n), 0))
    ssp = lambda ci: pl.BlockSpec((cps, nh, GLA_DV, GLA_DK), lambda n: (ci(n), 0, 0, 0))
    o_shape = jax.ShapeDtypeStruct((T, nh * GLA_DV), f32)
    s_shape = jax.ShapeDtypeStruct((N, nh, GLA_DV, GLA_DK), MXU_DTYPE)
    return pl.pallas_call(
        body,
        out_shape=(o_shape, o_shape, s_shape, s_shape),
        grid=(NB,),
        in_specs=_gla_specs(cps * CHUNK, fw) + _gla_specs(cps * CHUNK, bw),
        out_specs=(osp(fw), osp(bw), ssp(fw), ssp(bw)),
        scratch_shapes=[pltpu.VMEM((2 * nh, GLA_DV, GLA_DK), f32)],
        compiler_params=_cparams(("arbitrary",)),
        name="gla_scan_fwd",
    )(proj, proj, proj, G_f, proj, proj, proj, G_b)


def _gla_bwd(proj, G_f, G_b, st_f, st_b, do):
    T = proj.shape[0]
    N = T // CHUNK
    nh = GLA_HEADS
    cps = min(GLA_CPS, N)
    NB = N // cps

    def body(qf, kf, vf, gf, sf_ref, dof_ref, qb, kb, vb, gb, sb_ref, dob_ref,
             dqf_ref, dkf_ref, dvf_ref, dgf_ref, dqb_ref, dkb_ref, dvb_ref, dgb_ref, ds_scr):
        n = pl.program_id(0)

        @pl.when(n == 0)
        def _():
            ds_scr[...] = jnp.zeros((2 * nh, GLA_DV, GLA_DK), f32)

        shp = (2 * nh, CHUNK, CHUNK)
        bb, ii, jj = _iota3(shp, 0), _iota3(shp, 1), _iota3(shp, 2)
        tri = jnp.where(jnp.logical_or(jnp.logical_and(bb < nh, jj >= ii), jnp.logical_and(bb >= nh, jj <= ii)), 1.0, 0.0)
        for s in range(cps):
            sf = cps - 1 - s
            rows = (_chunk_rows(sf), _chunk_rows(s))
            q, k, v, G = _gla_chain_inputs((qf, kf, vf, gf), (qb, kb, vb, gb), *rows)
            ck = _GlaBatch(q, k, G, nh)
            St = jnp.concatenate([sf_ref[sf], sb_ref[s]], axis=0).astype(f32)
            dSt = ds_scr[...]
            dov = jnp.stack(_heads(dof_ref, (), nh, GLA_DV, rows[0]) + _heads(dob_ref, (), nh, GLA_DV, rows[1]))
            d_attn = jnp.where(ck.incl, _mx(dov, v, "nt"), 0.0)
            d_qg = _mx(dov, St, "nn") + _mx(d_attn, ck.kg, "nn")
            d_kg = _mx(d_attn, ck.qg, "tn")
            d_v = _mx(ck.attn, dov, "tn") + _mx(ck.kd, dSt, "nt")
            d_kd = _mx(v, dSt, "nn")
            ds_scr[...] = dSt * ck.egl + _mx(dov, ck.qg, "tn")
            kdd = d_kd * ck.kd
            d_gl = jnp.sum(dSt * St, axis=1, keepdims=True) * ck.egl + jnp.sum(kdd, axis=1, keepdims=True)
            d_q = d_qg * ck.eG * (GLA_DK**-0.5)
            d_k = d_kg * ck.enG + d_kd * ck.ekd
            d_gk = _exact01(tri, _add_at_last_row(d_qg * ck.qg - d_kg * ck.kg - kdd, d_gl, nh))
            for d, (dq_ref, dk_ref, dv_ref, dg_ref) in enumerate(((dqf_ref, dkf_ref, dvf_ref, dgf_ref), (dqb_ref, dkb_ref, dvb_ref, dgb_ref))):
                for i in range(nh):
                    c = d * nh + i
                    ks, vs = slice(i * GLA_DK, (i + 1) * GLA_DK), slice(i * GLA_DV, (i + 1) * GLA_DV)
                    dq_ref[rows[d], ks] = d_q[c]
                    dk_ref[rows[d], ks] = d_k[c]
                    dv_ref[rows[d], vs] = d_v[c]
                    dg_ref[rows[d], ks] = d_gk[c]

    R = cps * CHUNK

    def in_specs(ci):
        return _gla_specs(R, ci) + [
            pl.BlockSpec((cps, nh, GLA_DV, GLA_DK), lambda n: (ci(n), 0, 0, 0)),
            pl.BlockSpec((R, nh * GLA_DV), lambda n: (ci(n), 0)),
        ]

    def out_specs(ci):
        ksp = pl.BlockSpec((R, nh * GLA_DK), lambda n: (ci(n), 0))
        return [ksp, ksp, pl.BlockSpec((R, nh * GLA_DV), lambda n: (ci(n), 0)), ksp]

    fw, bw = (lambda n: NB - 1 - n), (lambda n: n)
    k_shape = jax.ShapeDtypeStruct((T, nh * GLA_DK), f32)
    v_shape = jax.ShapeDtypeStruct((T, nh * GLA_DV), f32)
    return pl.pallas_call(
        body,
        out_shape=(k_shape, k_shape, v_shape, k_shape) * 2,
        grid=(NB,),
        in_specs=in_specs(fw) + in_specs(bw),
        out_specs=tuple(out_specs(fw) + out_specs(bw)),
        scratch_shapes=[pltpu.VMEM((2 * nh, GLA_DV, GLA_DK), f32)],
        compiler_params=_cparams(("arbitrary",)),
        name="gla_scan_bwd",
    )(proj, proj, proj, G_f, st_f, do, proj, proj, proj, G_b, st_b, do)


def _local_step(x, target, wcat, conv_w, avec, dvec, gdn_norm_w, w2f, w2b, b2f, b2b, gla_norm_w, proj_weights, w_pre, w_post,
                send_proj_grads=None, send_in_grads=None):
    h, h_t = _rms_pre(x, w_pre)
    proj = _matmul(h, wcat, "nt", f32, "proj_in", 2048, 512, 1024)
    qkvn = _gdn_prep(proj, conv_w)
    gates, gc, G_f, G_b = _gate_prep(proj, avec, dvec, w2f, w2b, b2f, b2b)
    oa_f, oa_b, sa_f, sa_b, inv_f, inv_b = _gdn_fwd(qkvn, gates, gc)
    og = _out_norm(oa_f, oa_b, proj, Z_BLK, gdn_norm_w, GDN_DIM, "gdn_out")
    ob_f, ob_b, sb_f, sb_b = _gla_fwd(proj, G_f, G_b)
    obg = _out_norm(ob_f, ob_b, proj, G_B_BLK, gla_norm_w, GLA_DV, "gla_out")
    wpg, wpl, wout = proj_weights(obg)
    (loss_row, d_wpost, dy, m, dout, dya, dyb, dproj, dog, dobg) = _mid(og, obg, proj, x, target, wpg, wpl, wout, w_post)

    d_wout = _matmul(m, dout, "tn", MXU_DTYPE, "dw_out", 1024, 512, 2048)
    d_wpg = _matmul(og, dya, "tn", MXU_DTYPE, "dw_proj_gdn", 1024, 512, 2048)
    d_wpl = _matmul(obg, dyb, "tn", MXU_DTYPE, "dw_proj_gla", 1024, 512, 2048)
    if send_proj_grads is not None:
        gla_norm_w = gla_norm_w + send_proj_grads(d_wpg, d_wpl, d_wout)

    do_b, dproj, d_gla_norm = _out_norm_bwd(dobg, ob_f, ob_b, proj, G_B_BLK, gla_norm_w, GLA_DV, "gla_out_bwd", dproj)
    dq_f, dk_f, dv_f, dgk_f, dq_b, dk_b, dv_b, dgk_b = _gla_bwd(proj, G_f, G_b, sb_f, sb_b, do_b)
    dproj, dsmall_gla, d_w2f, d_w2b, d_b2f, d_b2b = _gla_prep_bwd(proj, w2f, w2b, b2f, b2b, dgk_f, dgk_b,
                                                                 (dq_f, dq_b, dk_f, dk_b, dv_f, dv_b), dproj)

    do_a, dproj, d_gdn_norm = _out_norm_bwd(dog, oa_f, oa_b, proj, Z_BLK, gdn_norm_w, GDN_DIM, "gdn_out_bwd", dproj)
    dqkv_f, dqkv_b, dg_f, dg_b = _gdn_bwd(qkvn, gates, gc, sa_f, sa_b, inv_f, inv_b, do_a)
    dproj, d_conv = _gdn_prep_bwd(proj, conv_w, dqkv_f, dqkv_b, dproj)
    dproj, d_alog, d_dtb = _gdn_gates_bwd(proj, avec, dvec, dg_f, dg_b, dsmall_gla, dproj)

    d_wcat = _matmul(h_t, dproj, "nn", MXU_DTYPE, "dw_in", 1024, 512, 2048, out_t=True)
    started = None if send_in_grads is None else send_in_grads(d_wcat)
    dh = _matmul(dproj, wcat, "nn", f32, "dh", 512, 1024, N_CAT, after=started)
    grad_x, d_wpre = _rms_pre_bwd(x, w_pre, dh, dy)
    return dict(loss_row=loss_row, grad_x=grad_x, d_wcat=d_wcat, d_conv=d_conv, d_alog=d_alog, d_dtb=d_dtb,
                d_gdn_norm=d_gdn_norm, d_wpg=d_wpg, d_w2f=d_w2f, d_w2b=d_w2b, d_b2f=d_b2f, d_b2b=d_b2b,
                d_gla_norm=d_gla_norm, d_wpl=d_wpl, d_wout=d_wout, d_wpost=d_wpost, d_wpre=d_wpre)


def _to_cat(wf):
    pad = jnp.zeros((N_CAT - N_IN,) + wf.shape[1:], wf.dtype)
    return jnp.concatenate([wf[:4096], wf[4128:6176], wf[7232:9280], wf[6176:7200], wf[4096:4128], wf[7200:7232], pad], axis=0)


def _from_cat(wc):
    return jnp.concatenate([wc[:4096], wc[9216:9248], wc[4096:6144], wc[8192:9216], wc[9248:9280], wc[6144:8192]], axis=0)


def _lane_row(*pieces):
    row = jnp.concatenate(pieces, axis=1)
    return jnp.pad(row, ((0, 0), (0, LANES - row.shape[1])))


def _pad_w2(w2, row0):
    return jnp.pad(w2, ((row0, LANES - row0 - w2.shape[0]), (0, 0)))


SM_ROWS, SM_COLS = 32, 384


def _pack_small_shard(conv, w2f, w2b):
    lead = conv.shape[:-2]
    z = lambda r, c: jnp.zeros(lead + (r, c), f32)
    top = jnp.concatenate([conv, z(8 - CONV_K, SM_COLS)], axis=-2)
    mid = jnp.concatenate([w2f, w2b, z(16, SM_COLS - 128)], axis=-1)
    return jnp.concatenate([top, mid, z(SM_ROWS - 24, SM_COLS)], axis=-2)


SMALL_REGION = {
    "ln_pre_w": (0, 1, 0, 1024), "ln_post_w": (1, 1, 0, 1024), "gk_b2_fwd": (2, 1, 0, 512), "gk_b2_bwd": (2, 1, 512, 512),
    "gla_norm_w": (3, 1, 0, 256), "gdn_norm_w": (3, 1, 256, 128), "a_log_fwd": (4, 1, 0, 8), "a_log_bwd": (5, 1, 0, 8),
    "dt_bias_fwd": (6, 1, 0, 8), "dt_bias_bwd": (7, 1, 0, 8), "conv_w": (8, CONV_K, 0, SM_COLS),
    "gk_w2_fwd": (16, 16, 0, 64), "gk_w2_bwd": (16, 16, 128, 64),
}
SMALL_NAMES = tuple(SMALL_REGION)
SMALL_SHARDED = ("conv_w", "gk_w2_fwd", "gk_w2_bwd")
PACK_ROWS, PACK_COLS = 32, 1024
LOSS_AT = (13, 0)


def _region(name):
    r0, nr, l0, nl = SMALL_REGION[name]
    return slice(r0, r0 + nr), slice(l0, l0 + nl)


def _pack_small_grads(G):
    names = ("d_wpre", "d_wpost", "d_b2f", "d_b2b", "d_gla_norm", "d_gdn_norm", "d_alog", "d_dtb", "loss_row", "d_conv", "d_w2f", "d_w2b")

    def body(wpre, wpost, b2f, b2b, glan, gdnn, alog, dtb, lrow, conv, w2f, w2b, o_ref):
        o_ref[...] = jnp.zeros(o_ref.shape, f32)
        loss = 0.5 * jnp.sum(lrow[...], axis=1, keepdims=True) * (1.0 / D_MODEL)
        whole = {"ln_pre_w": wpre, "ln_post_w": wpost, "gk_b2_fwd": b2f, "gk_b2_bwd": b2b, "gla_norm_w": glan, "gdn_norm_w": gdnn}
        for j in range(N_DEV):
            for name, ref in whole.items():
                o_ref[(j,) + _region(name)] = ref[...]
            o_ref[(j,) + _region("a_log_fwd")] = alog[0:1, 0:8]
            o_ref[(j,) + _region("a_log_bwd")] = alog[0:1, 8:16]
            o_ref[(j,) + _region("dt_bias_fwd")] = dtb[0:1, 0:8]
            o_ref[(j,) + _region("dt_bias_bwd")] = dtb[0:1, 8:16]
            o_ref[(j,) + _region("conv_w")] = conv[:, j * SM_COLS : (j + 1) * SM_COLS]
            o_ref[(j,) + _region("gk_w2_fwd")] = w2f[32:48, j * 64 : (j + 1) * 64]
            o_ref[(j,) + _region("gk_w2_bwd")] = w2b[48:64, j * 64 : (j + 1) * 64]
            o_ref[j, LOSS_AT[0] : LOSS_AT[0] + 1, LOSS_AT[1] : LOSS_AT[1] + 1] = loss

    return pl.pallas_call(
        body, out_shape=jax.ShapeDtypeStruct((N_DEV, PACK_ROWS, PACK_COLS), f32), name="pack_small_grads",
    )(*[G[n] for n in names])


def _small_adam(land, own, W, M, V):
    bc1 = 1.0 - ADAM_B1**ADAM_STEP
    bc2 = 1.0 - ADAM_B2**ADAM_STEP
    n = len(SMALL_NAMES)

    def body(land_ref, own_ref, *refs):
        w_refs, m_refs, v_refs = refs[0:n], refs[n : 2 * n], refs[2 * n : 3 * n]
        outs, loss_ref, g_scr = refs[3 * n : 7 * n], refs[7 * n], refs[7 * n + 1]
        me = _my_device()
        g = jnp.where(me == 0, own_ref[0], land_ref[0])
        for j in range(1, N_DEV):
            g = g + jnp.where(me == j, own_ref[j], land_ref[j])
        g_scr[...] = g
        loss_ref[...] = g_scr[LOSS_AT[0] : LOSS_AT[0] + 1, LOSS_AT[1] : LOSS_AT[1] + 1]
        for i, name in enumerate(SMALL_NAMES):
            gp = g_scr[_region(name)]
            lead = (0,) if name in SMALL_SHARDED else (Ellipsis,)
            w, m, v = w_refs[i][lead], m_refs[i][lead], v_refs[i][lead]
            m2 = ADAM_B1 * m + (1.0 - ADAM_B1) * gp
            v2 = ADAM_B2 * v + (1.0 - ADAM_B2) * (gp * gp)
            delta = -ADAM_LR * ((m2 / bc1) / (jnp.sqrt(v2 / bc2) + ADAM_EPS) + ADAM_WD * w)
            for k, val in enumerate((gp, delta, m2, v2)):
                outs[4 * i + k][lead] = val

    shapes = [jax.ShapeDtypeStruct(W[name].shape, f32) for name in SMALL_NAMES for _ in range(4)]
    res = pl.pallas_call(
        body,
        out_shape=tuple(shapes) + (jax.ShapeDtypeStruct((1, 1), f32),),
        scratch_shapes=[pltpu.VMEM((PACK_ROWS, PACK_COLS), f32)],
        name="adam_small",
    )(land, own, *[P[name] for P in (W, M, V) for name in SMALL_NAMES])
    return {name: res[4 * i : 4 * i + 4] for i, name in enumerate(SMALL_NAMES)}, res[4 * n]


_MESH = pl.DeviceIdType.MESH


N_BIG_COPIES, N_SMALL_COPIES = 9, 7


def _gather_weights(w_in_s, sm_s):
    shapes = (w_in_s.shape, sm_s.shape)
    dtypes = (MXU_DTYPE, f32)

    def body(win_ref, sm_ref, gin_ref, gsm_ref, send_sems, recv_sems):
        x, y, c = lax.axis_index("x"), lax.axis_index("y"), lax.axis_index("c")
        me, sibling = (x, y, c), (x, y, 1 - c)
        chips = [(1 - x, y), (x, 1 - y), (1 - x, 1 - y)]

        def idx(px, py, pc):
            return 4 * px + 2 * py + pc

        for r in range(w_in_s.shape[1] // LANES):
            cols = slice(r * LANES, (r + 1) * LANES)
            gin_ref[idx(*me), :, cols] = win_ref[:, cols].astype(MXU_DTYPE)
        gsm_ref[idx(*me)] = sm_ref[...]

        def copy(sem, blk, to):
            return pltpu.make_async_remote_copy(src_ref=blk, dst_ref=blk, send_sem=send_sems.at[sem], recv_sem=recv_sems.at[sem],
                                                device_id=to, device_id_type=_MESH)

        def small(k, block, to):
            return copy(N_BIG_COPIES + k, gsm_ref.at[idx(*block)], to)

        half = w_in_s.shape[1] // 2
        xn, yn, dg = (1 - x, y), (x, 1 - y), (1 - x, 1 - y)

        def big(sem, block, to, part=None):
            blk = gin_ref.at[idx(*block)]
            return copy(sem, blk if part is None else blk.at[:, pl.ds(part * half, half)], to)

        started = [big(0, me, sibling), big(1, me, (*xn, c)), big(2, me, (*yn, c)), small(0, me, sibling)]
        started += [small(1 + j, me, (*chip, c)) for j, chip in enumerate(chips)]
        for cp in started:
            cp.start()

        def then(arrived, forwards):
            arrived.wait_recv()
            for cp in forwards:
                cp.start()
            started.extend(forwards)

        then(big(1, (*xn, c), me), [big(3, (*xn, c), (*yn, c), 1), big(5, (*xn, c), sibling)])
        then(big(2, (*yn, c), me), [big(4, (*yn, c), (*xn, c), 0), big(6, (*yn, c), sibling)])
        then(big(4, (*dg, c), me, 0), [big(7, (*dg, c), sibling, 0)])
        then(big(3, (*dg, c), me, 1), [big(8, (*dg, c), sibling, 1)])
        for j, chip in enumerate(chips):
            then(small(1 + j, (*chip, c), me), [small(4 + j, (*chip, c), sibling)])
        big(0, sibling, me).wait_recv()
        big(5, (*xn, 1 - c), me).wait_recv()
        big(6, (*yn, 1 - c), me).wait_recv()
        big(7, (*dg, 1 - c), me, 0).wait_recv()
        big(8, (*dg, 1 - c), me, 1).wait_recv()
        small(0, sibling, me).wait_recv()
        for j, chip in enumerate(chips):
            small(4 + j, (*chip, 1 - c), me).wait_recv()
        for cp in started:
            cp.wait_send()

    vm = pl.BlockSpec(memory_space=pltpu.VMEM)
    return pl.pallas_call(
        body,
        out_shape=tuple(jax.ShapeDtypeStruct((N_DEV,) + s, d) for s, d in zip(shapes, dtypes)),
        in_specs=[vm, vm],
        out_specs=(vm, vm),
        scratch_shapes=[pltpu.SemaphoreType.DMA((N_BIG_COPIES + N_SMALL_COPIES,)), pltpu.SemaphoreType.DMA((N_BIG_COPIES + N_SMALL_COPIES,))],
        compiler_params=pltpu.CompilerParams(vmem_limit_bytes=VMEM_LIMIT),
        name="gather_weights",
    )(w_in_s, sm_s)


N_CHIP = 4
_EFFECT = pltpu.SideEffectType.DATAFLOW_SIDE_EFFECTING
_HBM_SPEC = pl.BlockSpec(memory_space=pltpu.HBM)
_SEM_SPEC = pl.BlockSpec(memory_space=pltpu.SEMAPHORE)


def _relation_peer(k, x, y, c):
    px = (1 - x) if (k & 4) else x
    py = (1 - y) if (k & 2) else y
    pc = (1 - c) if (k & 1) else c
    return (px, py, pc), 4 * px + 2 * py + pc


_PLAN_COPIES = {"gather": N_DEV - 1, "blocks": N_DEV - 1, "sibling": 4, "chips": 3}
_PLAN_SLOTS = {"gather": N_DEV, "blocks": N_DEV, "sibling": 4, "chips": 4}


def _plan_copies(plan, src_ref, land_ref, send_sems, recv_sems, arrival):
    x, y, c = lax.axis_index("x"), lax.axis_index("y"), lax.axis_index("c")
    me, my_chip = 4 * x + 2 * y + c, 2 * x + y
    out = []

    def add(i, src, there, here, dev):
        out.append(pltpu.make_async_remote_copy(src_ref=src, dst_ref=land_ref.at[here if arrival else there], send_sem=send_sems.at[i],
                                                recv_sem=recv_sems.at[i], device_id=dev, device_id_type=_MESH))

    if plan in ("gather", "blocks"):
        for k in range(1, N_DEV):
            dev, p = _relation_peer(k, x, y, c)
            add(k - 1, src_ref.at[p] if plan == "blocks" else src_ref, me, p, dev)
    elif plan == "sibling":
        for j in range(4):
            add(j, src_ref.at[2 * j + 1 - c], j, j, (x, y, 1 - c))
    else:
        for i, k in enumerate((2, 4, 6)):
            (px, py, pc), _ = _relation_peer(k, x, y, c)
            add(i, src_ref.at[2 * px + py], my_chip, 2 * px + py, (px, py, pc))
    return out


def _exchange_start(src, plan, name, after=None):
    n = _PLAN_COPIES[plan]
    land = lax.empty((_PLAN_SLOTS[plan],) + (src.shape if plan == "gather" else src.shape[1:]), src.dtype)
    extra = [] if after is None else [after]

    def body(src_ref, land_ref, *rest):
        send_sems, recv_sems, _, _, token = rest[len(extra) :]
        for cp in _plan_copies(plan, src_ref, land_ref, send_sems, recv_sems, False):
            cp.start()
        token[...] = jnp.zeros_like(token)

    return pl.pallas_call(
        body,
        name=name,
        out_shape=(pltpu.SemaphoreType.DMA((n,)), pltpu.SemaphoreType.DMA((n,)), pltpu.HBM(src.shape, src.dtype),
                   pltpu.HBM(land.shape, land.dtype), jax.ShapeDtypeStruct((8, LANES), f32)),
        in_specs=(_HBM_SPEC, _HBM_SPEC) + (_ANY_SPEC,) * len(extra),
        out_specs=(_SEM_SPEC, _SEM_SPEC, _HBM_SPEC, _HBM_SPEC, pl.BlockSpec(memory_space=pltpu.VMEM)),
        input_output_aliases={0: 2, 1: 3},
        compiler_params=pltpu.CompilerParams(has_side_effects=_EFFECT),
    )(pltpu.with_memory_space_constraint(src, pltpu.HBM), pltpu.with_memory_space_constraint(land, pltpu.HBM), *extra)


def _exchange_wait(started, after, plan, name):
    send_sems, recv_sems, src_thru, land_thru, _ = started

    def body(src_ref, land_ref, send_sems, recv_sems, after_ref, src_dead, got_ref):
        for cp in _plan_copies(plan, src_ref, land_ref, send_sems, recv_sems, True):
            cp.wait_send()
            cp.wait_recv()

    return pl.pallas_call(
        body,
        name=name,
        out_shape=(pltpu.HBM(src_thru.shape, src_thru.dtype), pltpu.HBM(land_thru.shape, land_thru.dtype)),
        in_specs=(_HBM_SPEC, _HBM_SPEC, _SEM_SPEC, _SEM_SPEC, pl.BlockSpec(memory_space=pl.ANY)),
        out_specs=(_HBM_SPEC, _HBM_SPEC),
        input_output_aliases={0: 0, 1: 1},
        compiler_params=pltpu.CompilerParams(has_side_effects=_EFFECT),
    )(src_thru, land_thru, send_sems, recv_sems, after)


def _pair_sum(g, r, name, tc=LANES):
    _, R, Cc = g.shape

    def body(g_ref, r_ref, o_ref):
        o_ref[...] = (g_ref[...].astype(f32) + r_ref[...].astype(f32)).astype(o_ref.dtype)

    return pl.pallas_call(
        body,
        out_shape=jax.ShapeDtypeStruct(r.shape, r.dtype),
        grid=(N_CHIP, Cc // tc),
        in_specs=[pl.BlockSpec((1, R, tc), lambda k, i: (2 * k + lax.axis_index("c"), 0, i)), pl.BlockSpec((1, R, tc), lambda k, i: (k, 0, i))],
        out_specs=pl.BlockSpec((1, R, tc), lambda k, i: (k, 0, i)),
        compiler_params=_cparams(("parallel", "parallel")),
        name=name,
    )(g, r)


def _sum_adam(parts, w, m, v, name, tb, tc=None, own=None, own_slot=None):
    R, Cc = w.shape
    n_parts = parts.shape[0]
    tb = R if tc else min(tb, R)
    tc = tc or Cc
    assert R % tb == 0 and Cc % tc == 0
    bc1 = 1.0 - ADAM_B1**ADAM_STEP
    bc2 = 1.0 - ADAM_B2**ADAM_STEP
    extra = [] if own is None else [own]

    def body(p_ref, w_ref, m_ref, v_ref, *rest):
        g_ref, d_ref, nm_ref, nv_ref = rest[len(extra) :]

        def part(j):
            pj = p_ref[j].astype(f32)
            return pj if own is None else jnp.where(own_slot() == j, rest[0][0].astype(f32), pj)

        g = part(0)
        for j in range(1, n_parts):
            g = g + part(j)
        g_ref[...] = g
        m2 = ADAM_B1 * m_ref[...] + (1.0 - ADAM_B1) * g
        v2 = ADAM_B2 * v_ref[...] + (1.0 - ADAM_B2) * (g * g)
        nm_ref[...] = m2
        nv_ref[...] = v2
        d_ref[...] = -ADAM_LR * ((m2 / bc1) / (jnp.sqrt(v2 / bc2) + ADAM_EPS) + ADAM_WD * w_ref[...])

    blk = pl.BlockSpec((tb, tc), lambda i, j: (i, j))
    o = jax.ShapeDtypeStruct((R, Cc), f32)
    return pl.pallas_call(
        body,
        out_shape=(o, o, o, o),
        grid=(R // tb, Cc // tc),
        in_specs=[pl.BlockSpec((n_parts, tb, tc), lambda i, j: (0, i, j)), blk, blk, blk]
        + [pl.BlockSpec((1, tb, tc), lambda i, j: (own_slot(), i, j))] * len(extra),
        out_specs=(blk, blk, blk, blk),
        compiler_params=_cparams(("parallel", "parallel")),
        name=name,
    )(parts, w, m, v, *extra)


def _my_device():
    return 4 * lax.axis_index("x") + 2 * lax.axis_index("y") + lax.axis_index("c")


def _my_chip():
    return 2 * lax.axis_index("x") + lax.axis_index("y")


_WEIGHTS = ["ln_pre_w", "w_in", "conv_w", "a_log_fwd", "a_log_bwd", "dt_bias_fwd", "dt_bias_bwd", "gdn_norm_w", "w_proj_gdn",
            "gk_w2_fwd", "gk_b2_fwd", "gk_w2_bwd", "gk_b2_bwd", "gla_norm_w", "w_proj_gla", "w_out", "ln_post_w"]


def kernel(x, ln_pre_w, w_in, conv_w, a_log_fwd, a_log_bwd, dt_bias_fwd, dt_bias_bwd, gdn_norm_w, w_proj_gdn, gk_w2_fwd, gk_b2_fwd, gk_w2_bwd, gk_b2_bwd, gla_norm_w, w_proj_gla, w_out, ln_post_w, loss_target, m_ln_pre_w, m_w_in, m_conv_w, m_a_log_fwd, m_a_log_bwd, m_dt_bias_fwd, m_dt_bias_bwd, m_gdn_norm_w, m_w_proj_gdn, m_gk_w2_fwd, m_gk_b2_fwd, m_gk_w2_bwd, m_gk_b2_bwd, m_gla_norm_w, m_w_proj_gla, m_w_out, m_ln_post_w, v_ln_pre_w, v_w_in, v_conv_w, v_a_log_fwd, v_a_log_bwd, v_dt_bias_fwd, v_dt_bias_bwd, v_gdn_norm_w, v_w_proj_gdn, v_gk_w2_fwd, v_gk_b2_fwd, v_gk_w2_bwd, v_gk_b2_bwd, v_gla_norm_w, v_w_proj_gla, v_w_out, v_ln_post_w):
    args = locals()
    W = {n: args[n] for n in _WEIGHTS}
    M = {n: args["m_" + n] for n in _WEIGHTS}
    V = {n: args["v_" + n] for n in _WEIGHTS}

    wp_stack = lambda P: jnp.concatenate([P["w_proj_gdn"], P["w_proj_gla"], P["w_out"]], axis=0)
    sm_pack = lambda P: _pack_small_shard(P["conv_w"][0], P["gk_w2_fwd"][0], P["gk_w2_bwd"][0])
    me = 4 * lax.axis_index("x") + 2 * lax.axis_index("y") + lax.axis_index("c")
    w_in_t, m_in_t, v_in_t = (jnp.transpose(a[0]) for a in (w_in, m_w_in, v_w_in))
    g_in, g_sm = _gather_weights(w_in_t, sm_pack(W))
    wp_own = wp_stack(W).astype(MXU_DTYPE)
    wp_started = _exchange_start(wp_own, "gather", "gather_proj_start", after=g_sm)

    def proj_weights(after):
        own, land = _exchange_wait(wp_started, after, "gather", "gather_proj_wait")
        g_p = lax.dynamic_update_slice(land, own[None], (me, 0, 0, 0))
        return tuple(g_p[:, i].reshape(D_MODEL, D_MODEL) for i in range(3))

    rows_p = 3 * D_MODEL // N_DEV
    sent = {}

    def send_proj_grads(d_wpg, d_wpl, d_wout):
        p_p = jnp.stack([d_wpg, d_wpl, d_wout]).reshape(3, N_DEV, D_MODEL // N_DEV, D_MODEL).transpose(1, 0, 2, 3)
        sent["p_p"] = p_p.reshape(N_DEV, rows_p, D_MODEL)
        sent["started"] = _exchange_start(sent["p_p"], "blocks", "exchange_proj_start")
        return sent["started"][4][0:1, 0:1]

    def send_in_grads(d_wcat):
        p_in = _from_cat(d_wcat).reshape(N_DEV, SHARD_IN, D_MODEL)
        sib = _exchange_start(p_in, "sibling", "exchange_in_sibling_start")
        p_p, land_p = _exchange_wait(sent["started"], sib[4], "blocks", "exchange_proj_wait")
        sent["o_p"] = _sum_adam(land_p, wp_stack(W).reshape(rows_p, D_MODEL), wp_stack(M).reshape(rows_p, D_MODEL),
                                wp_stack(V).reshape(rows_p, D_MODEL), "adam_w_proj", 128, own=p_p, own_slot=_my_device)
        p_in, sib_land = _exchange_wait(sib, sent["o_p"][0], "sibling", "exchange_in_sibling_wait")
        sent["chips"] = _exchange_start(_pair_sum(p_in, sib_land, "pair_sum_w_in", tc=512), "chips", "exchange_in_chips_start")
        return sent["chips"][4]

    wcat = _to_cat(g_in.reshape(N_IN, D_MODEL))
    conv_full = g_sm[:, 0:CONV_K, :].transpose(1, 0, 2).reshape(CONV_K, N_DEV * SM_COLS)
    w2f_full = g_sm[:, 8:24, 0:64].transpose(1, 0, 2).reshape(16, 512)
    w2b_full = g_sm[:, 8:24, 64:128].transpose(1, 0, 2).reshape(16, 512)

    avec = _lane_row(a_log_fwd, a_log_bwd)
    dvec = _lane_row(dt_bias_fwd, dt_bias_bwd)
    G = _local_step(x[0], loss_target[0], wcat, conv_full, avec, dvec, gdn_norm_w, _pad_w2(w2f_full, 32), _pad_w2(w2b_full, 48),
                    gk_b2_fwd, gk_b2_bwd, gla_norm_w, proj_weights, ln_pre_w + wp_started[4][0:1, 0:1], ln_post_w, send_proj_grads,
                    send_in_grads)

    small_started = _exchange_start(_pack_small_grads(G), "blocks", "exchange_small_start")
    q_in, land_in = _exchange_wait(sent["chips"], small_started[4], "chips", "exchange_in_chips_wait")
    o_in = _sum_adam(land_in, w_in_t, m_in_t, v_in_t, "adam_w_in", SHARD_IN, tc=256, own=q_in, own_slot=_my_chip)
    o_p = sent["o_p"]
    p_small, land_small = _exchange_wait(small_started, o_in[0], "blocks", "exchange_small_wait")
    o_small, loss = _small_adam(land_small, p_small, W, M, V)

    res = []
    for kind in range(4):
        per = {name: o_small[name][kind] for name in SMALL_NAMES}
        per["w_in"] = jnp.transpose(o_in[kind])[None]
        pp = o_p[kind].reshape(3, 1, D_MODEL // N_DEV, D_MODEL)
        per["w_proj_gdn"], per["w_proj_gla"], per["w_out"] = pp[0], pp[1], pp[2]
        res.append([per[n] for n in _WEIGHTS])
    return (loss[0, 0], G["grad_x"][None], *res[0], *res[1], *res[2], *res[3])
```

```python
import jax
import jax.numpy as jnp
from jax import lax
from jax.experimental import pallas as pl
from jax.experimental.pallas import tpu as pltpu

f32 = jnp.float32
MXU_DTYPE = jnp.bfloat16

D_MODEL = 1024
CHUNK = 64
NORM_EPS = 1e-6
GDN_HEADS, GDN_DIM = 8, 128
GLA_HEADS, GLA_DK, GLA_DV = 4, 128, 256
GATE_NORMALIZER = 16.0
CONV_K = 5
N_IN = 9280
N_DEV = 8
SHARD_IN = N_IN // N_DEV

N_CAT = 9728
SMALL_OFF = 9216
LANES = 128

ADAM_LR, ADAM_B1, ADAM_B2, ADAM_EPS, ADAM_WD, ADAM_STEP = 0.001, 0.9, 0.999, 1e-08, 0.01, 10

VMEM_LIMIT = 56 * 1024 * 1024


def _cparams(sem=None):
    return pltpu.CompilerParams(dimension_semantics=sem, vmem_limit_bytes=VMEM_LIMIT)


_DN = {"nn": (((1,), (0,)), ((), ())), "nt": (((1,), (1,)), ((), ())), "tn": (((0,), (0,)), ((), ()))}
_BDN = {"nn": (((2,), (1,)), ((0,), (0,))), "nt": (((2,), (2,)), ((0,), (0,))), "tn": (((1,), (1,)), ((0,), (0,)))}


def _dot16(a, b, kind):
    return lax.dot_general(a, b, (_DN if a.ndim == 2 else _BDN)[kind], preferred_element_type=f32)


def _mx(a, b, kind):
    return _dot16(a.astype(MXU_DTYPE), b.astype(MXU_DTYPE), kind)


def _split2(a):
    hi = a.astype(jnp.bfloat16)
    return hi, (a - hi.astype(f32)).astype(jnp.bfloat16)


def _split3(a):
    p1 = a.astype(jnp.bfloat16)
    r = a - p1.astype(f32)
    p2 = r.astype(jnp.bfloat16)
    return p1, p2, (r - p2.astype(f32)).astype(jnp.bfloat16)


def _h3s(a2, b2, kind):
    (ah, al), (bh, bl) = a2, b2
    return _dot16(ah, bh, kind) + (_dot16(ah, bl, kind) + _dot16(al, bh, kind))


def _h3(a, b, kind):
    return _h3s(_split2(a), _split2(b), kind)


def _exact01(t01, x, kind="nn"):
    t = t01.astype(jnp.bfloat16)
    x1, x2, x3 = _split3(x)
    return _dot16(t, x1, kind) + (_dot16(t, x2, kind) + _dot16(t, x3, kind))


def _colsum_col(e):
    ones = jnp.ones(e.shape[:-1] + (LANES,), jnp.bfloat16)
    e1, e2, e3 = _split3(e)
    return (_dot16(e1, ones, "tn") + (_dot16(e2, ones, "tn") + _dot16(e3, ones, "tn")))[..., 0:1]


def _iota3(shape, axis):
    return lax.broadcasted_iota(jnp.int32, shape, axis)


def _dir_masks(B, nf, Cn):
    shp = (B, Cn, Cn)
    bb, ii, jj = _iota3(shp, 0), _iota3(shp, 1), _iota3(shp, 2)
    fwd = bb < nf
    rev = jnp.logical_not(fwd)
    incl = jnp.logical_or(jnp.logical_and(fwd, ii >= jj), jnp.logical_and(rev, ii <= jj))
    strict = jnp.logical_and(incl, ii != jj)
    return incl, strict, ii == jj


def _last_row(x, nf):
    B, Cn = x.shape[0], x.shape[1]
    fwd = _iota3((B, 1, 1), 0) < nf
    return jnp.where(fwd, x[:, Cn - 1 : Cn, :], x[:, 0:1, :])


def _add_at_last_row(x, val, nf):
    B, Cn = x.shape[0], x.shape[1]
    bb, rr = _iota3((B, Cn, 1), 0), _iota3((B, Cn, 1), 1)
    at = jnp.logical_or(jnp.logical_and(bb < nf, rr == Cn - 1), jnp.logical_and(bb >= nf, rr == 0))
    return x + jnp.where(at, val, 0.0)


def _sigmoid(x):
    return jax.nn.sigmoid(x)


def _silu(x):
    return x * _sigmoid(x)


def _silu_grad(x):
    s = _sigmoid(x)
    return s * (1.0 + x * (1.0 - s))


def _softplus(x):
    u = jnp.exp(-jnp.abs(x))
    l1p = jnp.where(u < 1e-3, u * (1.0 - u * (0.5 - u * (1.0 / 3.0))), jnp.log(1.0 + u))
    return jnp.maximum(x, 0.0) + l1p


def _mxr(x):
    return x.astype(MXU_DTYPE).astype(f32)


def _iota2(shape, axis):
    return lax.broadcasted_iota(jnp.int32, shape, axis)


def _matmul(a, b, kind, out_dtype, name, tm, tn, tk, out_t=False, after=None):
    extra = [] if after is None else [after]
    if kind == "nn":
        (M, K), N = a.shape, b.shape[1]
    elif kind == "nt":
        (M, K), N = a.shape, b.shape[0]
    else:
        (K, M), N = a.shape, b.shape[1]
    tm, tn, tk = min(tm, M), min(tn, N), min(tk, K)
    assert M % tm == 0 and N % tn == 0 and K % tk == 0, (name, M, N, K)
    nk = K // tk
    if kind == "tn":
        a_spec = pl.BlockSpec((tk, tm), lambda i, j, k: (k, i))
    else:
        a_spec = pl.BlockSpec((tm, tk), lambda i, j, k: (i, k))
    b_mode = dict(pipeline_mode=pl.Buffered(1)) if (tn == N and tk == K) else {}
    if kind == "nt":
        b_spec = pl.BlockSpec((tn, tk), lambda i, j, k: (j, k), **b_mode)
    else:
        b_spec = pl.BlockSpec((tk, tn), lambda i, j, k: (k, j), **b_mode)

    def body(a_ref, b_ref, *rest):
        o_ref, acc = rest[len(extra)], rest[len(extra) + 1 :]
        p = _mx(a_ref[...], b_ref[...], kind)
        if nk == 1:
            o_ref[...] = (p.T if out_t else p).astype(out_dtype)
        else:
            assert not out_t
            acc_ref = acc[0]
            k = pl.program_id(2)

            @pl.when(k == 0)
            def _():
                acc_ref[...] = p

            @pl.when(k > 0)
            def _():
                acc_ref[...] += p

            @pl.when(k == nk - 1)
            def _():
                o_ref[...] = acc_ref[...].astype(out_dtype)

    return pl.pallas_call(
        body,
        out_shape=jax.ShapeDtypeStruct((N, M) if out_t else (M, N), out_dtype),
        grid=(M // tm, N // tn, nk),
        in_specs=[a_spec, b_spec] + [pl.BlockSpec((8, LANES), lambda i, j, k: (0, 0))] * len(extra),
        out_specs=pl.BlockSpec((tn, tm), lambda i, j, k: (j, i)) if out_t else pl.BlockSpec((tm, tn), lambda i, j, k: (i, j)),
        scratch_shapes=[] if nk == 1 else [pltpu.VMEM((tm, tn), f32)],
        compiler_params=_cparams(("parallel", "parallel", "arbitrary")),
        name=name,
    )(a, b, *extra)


ROW_TILE = 512


def _rms_pre(x, w):
    T = x.shape[0]
    tb = min(ROW_TILE, T)

    def body(x_ref, w_ref, h_ref, ht_ref):
        xv = x_ref[...]
        r = lax.rsqrt(jnp.mean(xv * xv, axis=1, keepdims=True) + NORM_EPS)
        h = xv * r * w_ref[...]
        h_ref[...] = h.astype(h_ref.dtype)
        ht_ref[...] = h.T.astype(ht_ref.dtype)

    return pl.pallas_call(
        body,
        out_shape=(jax.ShapeDtypeStruct((T, D_MODEL), MXU_DTYPE), jax.ShapeDtypeStruct((D_MODEL, T), MXU_DTYPE)),
        grid=(T // tb,),
        in_specs=[pl.BlockSpec((tb, D_MODEL), lambda i: (i, 0)), pl.BlockSpec((1, D_MODEL), lambda i: (0, 0))],
        out_specs=(pl.BlockSpec((tb, D_MODEL), lambda i: (i, 0)), pl.BlockSpec((D_MODEL, tb), lambda i: (0, i))),
        compiler_params=_cparams(("parallel",)),
        name="rms_pre",
    )(x, w)


def _rms_pre_bwd(x, w, dh, dy):
    T = x.shape[0]
    tb = min(ROW_TILE, T)

    def body(x_ref, w_ref, dh_ref, dy_ref, dx_ref, dw_ref):
        i = pl.program_id(0)
        xv, dhv = x_ref[...], dh_ref[...]
        r = lax.rsqrt(jnp.mean(xv * xv, axis=1, keepdims=True) + NORM_EPS)
        dhw = dhv * w_ref[...]
        dx_ref[...] = dy_ref[...] + r * dhw - xv * (r * r * r) * jnp.mean(dhw * xv, axis=1, keepdims=True)
        part = jnp.sum(dhv * xv * r, axis=0, keepdims=True)

        @pl.when(i == 0)
        def _():
            dw_ref[...] = part

        @pl.when(i > 0)
        def _():
            dw_ref[...] += part

    blk = pl.BlockSpec((tb, D_MODEL), lambda i: (i, 0))
    row = pl.BlockSpec((1, D_MODEL), lambda i: (0, 0))
    return pl.pallas_call(
        body,
        out_shape=(jax.ShapeDtypeStruct((T, D_MODEL), f32), jax.ShapeDtypeStruct((1, D_MODEL), f32)),
        grid=(T // tb,),
        in_specs=[blk, row, blk, blk],
        out_specs=(blk, row),
        compiler_params=_cparams(("arbitrary",)),
        name="rms_pre_bwd",
    )(x, w, dh, dy)


SHIFTS = (-2, -1, 1, 2)


SHIFT_PAD = 8
PREP_HEADS, PREP_HEADS_BWD = 4, 1


def _shifted(x, pad_ref):
    T, cb = x.shape
    pad_ref[0:SHIFT_PAD] = jnp.zeros((SHIFT_PAD, cb), f32)
    pad_ref[SHIFT_PAD + T :] = jnp.zeros((SHIFT_PAD, cb), f32)
    pad_ref[SHIFT_PAD : SHIFT_PAD + T] = x
    return {d: pad_ref[SHIFT_PAD + d : SHIFT_PAD + d + T] for d in SHIFTS}


def _conv5(u, ush, cw):
    acc = u * cw[2:3, :]
    for j in (0, 1, 3, 4):
        acc = acc + ush[j - 2] * cw[j : j + 1, :]
    return acc


def _gdn_prep(proj, conv_w):
    T = proj.shape[0]
    hps, W = PREP_HEADS, PREP_HEADS * GDN_DIM
    per = GDN_HEADS // hps

    def body(u_ref, cw_ref, o_ref, pad_ref):
        j = pl.program_id(0)
        for i in range(hps):
            sl = slice(i * GDN_DIM, (i + 1) * GDN_DIM)
            u = _mxr(u_ref[:, sl])
            s = _silu(_conv5(u, _shifted(u, pad_ref), _mxr(cw_ref[:, sl])))
            o_ref[0, :, sl] = jnp.where(j < 2 * per, s * lax.rsqrt(jnp.sum(s * s, axis=1, keepdims=True) + NORM_EPS), s)

    return pl.pallas_call(
        body,
        out_shape=jax.ShapeDtypeStruct((3, T, GDN_HEADS * GDN_DIM), f32),
        grid=(3 * per,),
        in_specs=[pl.BlockSpec((T, W), lambda j: (0, j)), pl.BlockSpec((CONV_K, W), lambda j: (0, j))],
        out_specs=pl.BlockSpec((1, T, W), lambda j: (j // per, 0, j % per)),
        scratch_shapes=[pltpu.VMEM((T + 2 * SHIFT_PAD, GDN_DIM), f32)],
        compiler_params=_cparams(("parallel",)),
        name="gdn_prep",
    )(proj, conv_w)


def _gdn_prep_bwd(proj, conv_w, dqkv_f, dqkv_b, dproj):
    T = proj.shape[0]
    hps, W = PREP_HEADS_BWD, PREP_HEADS_BWD * GDN_DIM
    per = GDN_HEADS // hps

    def body(u_ref, cw_ref, df_ref, db_ref, _, du_ref, dcw_ref, upad_ref, dcpad_ref):
        j = pl.program_id(0)
        for i in range(hps):
            sl = slice(i * GDN_DIM, (i + 1) * GDN_DIM)
            u, cw = _mxr(u_ref[:, sl]), _mxr(cw_ref[:, sl])
            ush = _shifted(u, upad_ref)
            c = _conv5(u, ush, cw)
            s = _silu(c)
            dn = df_ref[0, :, sl] + db_ref[0, :, sl]
            rinv = lax.rsqrt(jnp.sum(s * s, axis=1, keepdims=True) + NORM_EPS)
            ds_norm = rinv * dn - s * (rinv * rinv * rinv) * jnp.sum(dn * s, axis=1, keepdims=True)
            ds = jnp.where(j < 2 * per, ds_norm, dn)
            dc = _mxr(ds * _silu_grad(c))
            dcsh = _shifted(dc, dcpad_ref)
            du = dc * cw[2:3, :]
            for jj in range(CONV_K):
                d = jj - 2
                dcw_ref[jj : jj + 1, sl] = jnp.sum(dc * (u if d == 0 else ush[d]), axis=0, keepdims=True)
                if d != 0:
                    du = du + dcsh[-d] * cw[jj : jj + 1, :]
            du_ref[:, sl] = du.astype(du_ref.dtype)

    qspec = pl.BlockSpec((1, T, W), lambda j: (j // per, 0, j % per))
    return pl.pallas_call(
        body,
        out_shape=(jax.ShapeDtypeStruct(dproj.shape, dproj.dtype), jax.ShapeDtypeStruct((CONV_K, 3 * GDN_HEADS * GDN_DIM), f32)),
        grid=(3 * per,),
        in_specs=[pl.BlockSpec((T, W), lambda j: (0, j)), pl.BlockSpec((CONV_K, W), lambda j: (0, j)), qspec, qspec, _ANY_SPEC],
        out_specs=(pl.BlockSpec((T, W), lambda j: (0, j)), pl.BlockSpec((CONV_K, W), lambda j: (0, j))),
        input_output_aliases={4: 0},
        scratch_shapes=[pltpu.VMEM((T + 2 * SHIFT_PAD, GDN_DIM), f32)] * 2,
        compiler_params=_cparams(("parallel",)),
        name="gdn_prep_bwd",
    )(proj, conv_w, dqkv_f, dqkv_b, dproj)


def _chunk_tri(n, rev):
    i, j = _iota2((n, n), 0), _iota2((n, n), 1)
    same = jnp.right_shift(i, 6) == jnp.right_shift(j, 6)
    order = (j >= i) if rev else (j <= i)
    return jnp.where(jnp.logical_and(same, order), 1.0, 0.0).astype(f32)


def _gate_prep(proj, avec, dvec, w2f, w2b, b2f, b2b, tb=256):
    T = proj.shape[0]
    W = GLA_HEADS * GLA_DK

    def body(s_ref, a_ref, d_ref, wf_ref, wb_ref, bf_ref, bb_ref, g_ref, gc_ref, gf_ref, gb_ref):
        small = s_ref[...]
        tri_f, tri_r = _chunk_tri(tb, False), _chunk_tri(tb, True)
        lane = _iota2(small.shape, 1)
        lg = -jnp.exp(a_ref[...]) * _softplus(small + d_ref[...])
        beta = _sigmoid(small)
        g_ref[...] = jnp.where(lane < 16, lg, jnp.where(lane < 32, beta, 0.0))
        lgm = jnp.where(lane < 16, lg, 0.0)
        gc_ref[...] = jnp.where(lane < 8, _exact01(tri_f, lgm), _exact01(tri_r, lgm))
        gkf = -_softplus(-(_mx(small, wf_ref[...], "nn") + bf_ref[...])) * (1.0 / GATE_NORMALIZER)
        gkb = -_softplus(-(_mx(small, wb_ref[...], "nn") + bb_ref[...])) * (1.0 / GATE_NORMALIZER)
        gf_ref[...] = _exact01(tri_f, gkf)
        gb_ref[...] = _exact01(tri_r, gkb)

    blk = pl.BlockSpec((tb, LANES), lambda i: (i, 0))
    row = pl.BlockSpec((1, LANES), lambda i: (0, 0))
    wide = pl.BlockSpec((tb, W), lambda i: (i, 0))
    wsp = pl.BlockSpec((LANES, W), lambda i: (0, 0))
    wrow = pl.BlockSpec((1, W), lambda i: (0, 0))
    return pl.pallas_call(
        body,
        out_shape=(jax.ShapeDtypeStruct((T, LANES), f32), jax.ShapeDtypeStruct((T, LANES), f32),
                   jax.ShapeDtypeStruct((T, W), f32), jax.ShapeDtypeStruct((T, W), f32)),
        grid=(T // tb,),
        in_specs=[pl.BlockSpec((tb, LANES), lambda i: (i, SMALL_OFF // LANES)), row, row, wsp, wsp, wrow, wrow],
        out_specs=(blk, blk, wide, wide),
        compiler_params=_cparams(("parallel",)),
        name="gate_prep",
    )(proj, avec, dvec, w2f, w2b, b2f, b2b)


def _gdn_gates_bwd(proj, avec, dvec, dg_f, dg_b, dsmall_gla, dproj):
    T = proj.shape[0]
    tb = min(ROW_TILE, T)
    pad = N_CAT - SMALL_OFF

    def body(s_ref, a_ref, d_ref, gf_ref, gb_ref, dl_ref, _, ds_ref, da_ref, dd_ref):
        i = pl.program_id(0)
        small = s_ref[...]
        lane = _iota2(small.shape, 1)
        dgate = gf_ref[...] + gb_ref[...]
        z = small + d_ref[...]
        nega = -jnp.exp(a_ref[...])
        dz = dgate * nega * _sigmoid(z)
        beta = _sigmoid(small)
        dsm = jnp.where(lane < 16, dz, jnp.where(lane < 32, dgate * beta * (1.0 - beta), 0.0))
        ds_ref[:, :LANES] = (dsm + dl_ref[...]).astype(ds_ref.dtype)
        ds_ref[:, LANES:] = jnp.zeros((tb, pad - LANES), ds_ref.dtype)
        lg = nega * _softplus(z)
        pa = jnp.sum(jnp.where(lane < 16, dgate * lg, 0.0), axis=0, keepdims=True)
        pd = jnp.sum(jnp.where(lane < 16, dz, 0.0), axis=0, keepdims=True)

        @pl.when(i == 0)
        def _():
            da_ref[...] = pa
            dd_ref[...] = pd

        @pl.when(i > 0)
        def _():
            da_ref[...] += pa
            dd_ref[...] += pd

    blk = pl.BlockSpec((tb, LANES), lambda i: (i, 0))
    row = pl.BlockSpec((1, LANES), lambda i: (0, 0))
    return pl.pallas_call(
        body,
        out_shape=(jax.ShapeDtypeStruct(dproj.shape, dproj.dtype), jax.ShapeDtypeStruct((1, LANES), f32), jax.ShapeDtypeStruct((1, LANES), f32)),
        grid=(T // tb,),
        in_specs=[pl.BlockSpec((tb, LANES), lambda i: (i, SMALL_OFF // LANES)), row, row, blk, blk, blk, _ANY_SPEC],
        out_specs=(pl.BlockSpec((tb, pad), lambda i: (i, DP_SMALL_BLK)), row, row),
        input_output_aliases={6: 0},
        compiler_params=_cparams(("arbitrary",)),
        name="gdn_gates_bwd",
    )(proj, avec, dvec, dg_f, dg_b, dsmall_gla, dproj)


GDN_HB = 8


def _inv_unit_lower(L, eye):
    A = eye - L
    P2 = _split2(L)
    for _ in range(3):
        P2 = _split2(_h3s(P2, P2, "nn"))
        A = A + _h3s(_split2(A), P2, "nn")
    Ph = P2[0]
    for _ in range(2):
        Ph = _dot16(Ph, Ph, "nn").astype(jnp.bfloat16)
        A = A + _dot16(A.astype(jnp.bfloat16), Ph, "nn")
    return A


def _lane_col(blk, idx):
    lane = _iota2(blk.shape, 1)
    return jnp.sum(jnp.where(lane == idx, blk, 0.0), axis=1, keepdims=True)


def _out_norm(o_f, o_b, proj, gate_blk, w, hd, name):
    T, W = o_f.shape
    tb = min(ROW_TILE, T)
    nh = W // hd

    def body(of_ref, ob_ref, z_ref, w_ref, y_ref):
        wv = w_ref[...]
        for i in range(nh):
            sl = slice(i * hd, (i + 1) * hd)
            o = of_ref[:, sl] + ob_ref[:, sl]
            r = lax.rsqrt(jnp.mean(o * o, axis=1, keepdims=True) + NORM_EPS)
            y_ref[:, sl] = (o * r * wv * _silu(z_ref[:, sl])).astype(y_ref.dtype)

    blk = pl.BlockSpec((tb, W), lambda i: (i, 0))
    return pl.pallas_call(
        body,
        out_shape=jax.ShapeDtypeStruct((T, W), MXU_DTYPE),
        grid=(T // tb,),
        in_specs=[blk, blk, pl.BlockSpec((tb, W), lambda i: (i, gate_blk)), pl.BlockSpec((1, hd), lambda i: (0, 0))],
        out_specs=blk,
        compiler_params=_cparams(("parallel",)),
        name=name,
    )(o_f, o_b, proj, w)


def _out_norm_bwd(dy, o_f, o_b, proj, gate_blk, w, hd, name, dproj):
    T, W = o_f.shape
    tb = min(ROW_TILE, T)
    nh = W // hd

    def body(dy_ref, of_ref, ob_ref, z_ref, w_ref, _, do_ref, dz_ref, dw_ref):
        i = pl.program_id(0)
        wv = w_ref[...]
        dw = jnp.zeros((1, hd), f32)
        for a in range(nh):
            sl = slice(a * hd, (a + 1) * hd)
            o = of_ref[:, sl] + ob_ref[:, sl]
            z = z_ref[:, sl]
            dyv = dy_ref[:, sl]
            r = lax.rsqrt(jnp.mean(o * o, axis=1, keepdims=True) + NORM_EPS)
            orr = o * r
            dn = dyv * _silu(z)
            dz_ref[:, sl] = (dyv * orr * wv * _silu_grad(z)).astype(dz_ref.dtype)
            dw = dw + jnp.sum(dn * orr, axis=0, keepdims=True)
            dnw = dn * wv
            do_ref[:, sl] = r * dnw - o * (r * r * r) * jnp.mean(dnw * o, axis=1, keepdims=True)

        @pl.when(i == 0)
        def _():
            dw_ref[...] = dw

        @pl.when(i > 0)
        def _():
            dw_ref[...] += dw

    blk = pl.BlockSpec((tb, W), lambda i: (i, 0))
    row = pl.BlockSpec((1, hd), lambda i: (0, 0))
    return pl.pallas_call(
        body,
        out_shape=(jax.ShapeDtypeStruct((T, W), f32), jax.ShapeDtypeStruct(dproj.shape, dproj.dtype), jax.ShapeDtypeStruct((1, hd), f32)),
        grid=(T // tb,),
        in_specs=[blk, blk, blk, pl.BlockSpec((tb, W), lambda i: (i, gate_blk)), row, _ANY_SPEC],
        out_specs=(blk, pl.BlockSpec((tb, W), lambda i: (i, gate_blk)), row),
        input_output_aliases={5: 1},
        compiler_params=_cparams(("arbitrary",)),
        name=name,
    )(dy, o_f, o_b, proj, w, dproj)


def _gla_prep_bwd(proj, w2f, w2b, b2f, b2b, dgk_f, dgk_b, dqkv, dproj):
    T = proj.shape[0]
    tb = min(ROW_TILE, T)
    W = GLA_HEADS * GLA_DK
    WV = GLA_HEADS * GLA_DV

    def body(s_ref, wf_ref, wb_ref, bf_ref, bb_ref, df_ref, db_ref, qf, qb, kf, kb, vf, vb, _,
             dp_ref, ds_ref, dwf_ref, dwb_ref, dbf_ref, dbb_ref):
        i = pl.program_id(0)
        dp_ref[:, :W] = (qf[...] + qb[...]).astype(dp_ref.dtype)
        dp_ref[:, W : 2 * W] = (kf[...] + kb[...]).astype(dp_ref.dtype)
        dp_ref[:, 2 * W :] = (vf[...] + vb[...]).astype(dp_ref.dtype)
        small = s_ref[...]
        dsm = jnp.zeros((tb, LANES), f32)
        parts = []
        for w_ref, b_ref, d_ref in ((wf_ref, bf_ref, df_ref), (wb_ref, bb_ref, db_ref)):
            pre = _mx(small, w_ref[...], "nn") + b_ref[...]
            dpre = d_ref[...] * (1.0 / GATE_NORMALIZER) * _sigmoid(-pre)
            dsm = dsm + _mx(dpre, w_ref[...], "nt")
            parts.append((_mx(small, dpre, "tn"), jnp.sum(dpre, axis=0, keepdims=True)))
        ds_ref[...] = dsm

        @pl.when(i == 0)
        def _():
            dwf_ref[...], dbf_ref[...] = parts[0]
            dwb_ref[...], dbb_ref[...] = parts[1]

        @pl.when(i > 0)
        def _():
            dwf_ref[...] += parts[0][0]
            dbf_ref[...] += parts[0][1]
            dwb_ref[...] += parts[1][0]
            dbb_ref[...] += parts[1][1]

    blk = pl.BlockSpec((tb, W), lambda i: (i, 0))
    vblk = pl.BlockSpec((tb, WV), lambda i: (i, 0))
    wsp = pl.BlockSpec((LANES, W), lambda i: (0, 0))
    row = pl.BlockSpec((1, W), lambda i: (0, 0))
    return pl.pallas_call(
        body,
        out_shape=(jax.ShapeDtypeStruct(dproj.shape, dproj.dtype), jax.ShapeDtypeStruct((T, LANES), f32), jax.ShapeDtypeStruct((LANES, W), f32),
                   jax.ShapeDtypeStruct((LANES, W), f32), jax.ShapeDtypeStruct((1, W), f32), jax.ShapeDtypeStruct((1, W), f32)),
        grid=(T // tb,),
        in_specs=[pl.BlockSpec((tb, LANES), lambda i: (i, SMALL_OFF // LANES)), wsp, wsp, row, row, blk, blk,
                  blk, blk, blk, blk, vblk, vblk, _ANY_SPEC],
        out_specs=(pl.BlockSpec((tb, 2 * W + WV), lambda i: (i, DP_GLA_QKV_BLK)), pl.BlockSpec((tb, LANES), lambda i: (i, 0)), wsp, wsp, row, row),
        input_output_aliases={13: 0},
        compiler_params=_cparams(("arbitrary",)),
        name="gla_prep_bwd",
    )(proj, w2f, w2b, b2f, b2b, dgk_f, dgk_b, *dqkv, dproj)


GATE_A_BLK, GATE_B_BLK, Z_BLK, G_B_BLK = 6144 // 1024, 7168 // 1024, 3072 // 1024, 8192 // 1024
DP_GATES_BLK, DP_GLA_QKV_BLK, DP_SMALL_BLK = 6144 // 2048, 4096 // 2048, SMALL_OFF // 512
_ANY_SPEC = pl.BlockSpec(memory_space=pl.ANY)


def _mid(og, obg, proj, x, target, wpg, wpl, wout, w_post, tb=256):
    T = x.shape[0]
    Dm = D_MODEL

    def body(og_ref, obg_ref, ga_ref, gb_ref, x_ref, t_ref, wpg_ref, wpl_ref, wo_ref, wp_ref,
             loss_ref, dwp_ref, dy_ref, m_ref, dout_ref, dya_ref, dyb_ref, dgates_ref, dog_ref, dobg_ref):
        i = pl.program_id(0)
        ya = _mx(og_ref[...], wpg_ref[...], "nn")
        yb = _mx(obg_ref[...], wpl_ref[...], "nn")
        sa, sb = _sigmoid(ga_ref[...]), _sigmoid(gb_ref[...])
        m = sa * ya + sb * yb
        m_ref[...] = m.astype(m_ref.dtype)
        out = _mx(m, wo_ref[...], "nn")
        r = lax.rsqrt(jnp.mean(out * out, axis=1, keepdims=True) + NORM_EPS)
        wp = wp_ref[...]
        e = x_ref[...] + out * r * wp - t_ref[...]
        dy = e * (1.0 / Dm)
        dy_ref[...] = dy
        lpart = jnp.sum(e * e, axis=0, keepdims=True)
        wpart = jnp.sum(dy * out * r, axis=0, keepdims=True)
        dyn = dy * wp
        dout = r * dyn - out * (r * r * r) * jnp.mean(dyn * out, axis=1, keepdims=True)
        dout_ref[...] = dout.astype(dout_ref.dtype)
        dm = _mx(dout, wo_ref[...], "nt")
        dya, dyb = dm * sa, dm * sb
        dya_ref[...] = dya.astype(dya_ref.dtype)
        dyb_ref[...] = dyb.astype(dyb_ref.dtype)
        dgates_ref[:, :Dm] = (dm * ya * sa * (1.0 - sa)).astype(dgates_ref.dtype)
        dgates_ref[:, Dm:] = (dm * yb * sb * (1.0 - sb)).astype(dgates_ref.dtype)
        dog_ref[...] = _mx(dya, wpg_ref[...], "nt")
        dobg_ref[...] = _mx(dyb, wpl_ref[...], "nt")

        @pl.when(i == 0)
        def _():
            loss_ref[...] = lpart
            dwp_ref[...] = wpart

        @pl.when(i > 0)
        def _():
            loss_ref[...] += lpart
            dwp_ref[...] += wpart

    blk = pl.BlockSpec((tb, Dm), lambda i: (i, 0))
    row = pl.BlockSpec((1, Dm), lambda i: (0, 0))
    wsp = pl.BlockSpec((Dm, Dm), lambda i: (0, 0))
    act = jax.ShapeDtypeStruct((T, Dm), MXU_DTYPE)
    big = jax.ShapeDtypeStruct((T, Dm), f32)
    vec = jax.ShapeDtypeStruct((1, Dm), f32)
    return pl.pallas_call(
        body,
        out_shape=(vec, vec, big, act, act, act, act, jax.ShapeDtypeStruct((T, N_CAT), MXU_DTYPE), big, big),
        grid=(T // tb,),
        in_specs=[blk, blk, pl.BlockSpec((tb, Dm), lambda i: (i, GATE_A_BLK)), pl.BlockSpec((tb, Dm), lambda i: (i, GATE_B_BLK)),
                  blk, blk, wsp, wsp, wsp, row],
        out_specs=(row, row) + (blk,) * 5 + (pl.BlockSpec((tb, 2 * Dm), lambda i: (i, DP_GATES_BLK)), blk, blk),
        compiler_params=_cparams(("arbitrary",)),
        name="mid",
    )(og, obg, proj, proj, x, target, wpg, wpl, wout, w_post)


def _heads(ref, lead, n, width, rows=slice(None)):
    return [ref[lead + (rows, slice(i * width, (i + 1) * width))] for i in range(n)]


class _GdnBatch:
    def __init__(self, q, k, v, gcol, grow, beta, nf, A=None):
        B, Cn = q.shape[0], q.shape[1]
        self.incl, self.strict, eye = _dir_masks(B, nf, Cn)
        self.decay = jnp.where(self.incl, jnp.exp(jnp.where(self.incl, gcol - grow, 0.0)), 0.0)
        self.k, self.v = k, v
        gl = _last_row(gcol, nf)
        self.beta = jnp.broadcast_to(beta, q.shape)
        self.eg = jnp.broadcast_to(jnp.exp(gcol), q.shape)
        self.ekd = jnp.broadcast_to(jnp.exp(gl - gcol), q.shape)
        self.qs = q * (GDN_DIM**-0.5)
        self.kb = k * self.beta
        self.L = jnp.where(self.strict, _mx(self.kb, k, "nt") * self.decay, 0.0)
        if A is None:
            A = _inv_unit_lower(self.L, jnp.where(eye, 1.0, 0.0).astype(f32))
        self.A = A
        self.A2 = _split2(A)
        self.kbg = self.kb * self.eg
        self.sol = _h3s(self.A2, _split2(jnp.concatenate([v * self.beta, self.kbg], axis=2)), "nn")
        self.u = self.sol[:, :, :GDN_DIM]
        self.w = self.sol[:, :, GDN_DIM:]
        self.attn = jnp.where(self.incl, _mx(self.qs, k, "nt") * self.decay, 0.0)
        self.qd = self.qs * self.eg
        self.kd = k * self.ekd
        self.egl = jnp.exp(gl)


def _gdn_chain_inputs(qf_ref, qb_ref, gf_ref, gb_ref, cf_ref, cb_ref, b, hb):
    qkv = [jnp.stack(_heads(qf_ref, (i,), hb, GDN_DIM) + _heads(qb_ref, (i,), hb, GDN_DIM)) for i in range(3)]
    cols, rows, betas = [], [], []
    for d, (g_ref, c_ref) in enumerate(((gf_ref, cf_ref), (gb_ref, cb_ref))):
        gates, gc = g_ref[...], c_ref[...]
        for hh in range(hb):
            h = b * hb + hh
            col = _lane_col(gc, h + 8 * d)
            cols.append(col)
            rows.append(jnp.broadcast_to(col, (CHUNK, LANES)).T[0:1, :])
            betas.append(_lane_col(gates, 16 + h + 8 * d))
    return qkv[0], qkv[1], qkv[2], jnp.stack(cols), jnp.stack(rows), jnp.stack(betas)


GDN_CPS_FWD, GDN_CPS_BWD = 2, 1


def _chunk_views(refs, s_f, s_b):
    out = []
    for i, r in enumerate(refs):
        s = s_f if i % 2 == 0 else s_b
        rows = pl.ds(s * CHUNK, CHUNK)
        out.append(r.at[pl.ds(s, 1)] if len(r.shape) == 4 else (r.at[:, rows, :] if len(r.shape) == 3 else r.at[rows, :]))
    return out


def _gdn_specs(cps, hb):
    R, W = cps * CHUNK, hb * GDN_DIM
    qsp = lambda ci: pl.BlockSpec((3, R, W), lambda n, b: (0, ci(n), b))
    gsp = lambda ci: pl.BlockSpec((R, LANES), lambda n, b: (ci(n), 0))
    osp = lambda ci: pl.BlockSpec((R, W), lambda n, b: (ci(n), b))
    ssp = lambda ci: pl.BlockSpec((cps, hb, GDN_DIM, GDN_DIM), lambda n, b: (ci(n), b, 0, 0))
    asp = lambda ci: pl.BlockSpec((cps, hb, CHUNK, CHUNK), lambda n, b: (ci(n), b, 0, 0))
    return qsp, gsp, osp, ssp, asp


def _gdn_fwd(qkvn, gates, gc):
    T = qkvn.shape[1]
    N = T // CHUNK
    nh, hb = GDN_HEADS, GDN_HB
    cps = min(GDN_CPS_FWD, N)
    NB = N // cps

    def body(*refs):
        n, b = pl.program_id(0), pl.program_id(1)
        s_scr = refs[-1]

        @pl.when(n == 0)
        def _():
            s_scr[0, pl.ds(b * hb, hb)] = jnp.zeros((hb, GDN_DIM, GDN_DIM), f32)
            s_scr[1, pl.ds(b * hb, hb)] = jnp.zeros((hb, GDN_DIM, GDN_DIM), f32)

        for s in range(cps):
            chunk(*_chunk_views(refs[:-1], s, cps - 1 - s), s_scr, b)

    def chunk(qf_ref, qb_ref, gf_ref, gb_ref, cf_ref, cb_ref, of_ref, ob_ref, sf_ref, sb_ref, af_ref, ab_ref, s_scr, b):
        hs = pl.ds(b * hb, hb)
        q, k, v, gcol, grow, beta = _gdn_chain_inputs(qf_ref, qb_ref, gf_ref, gb_ref, cf_ref, cb_ref, b, hb)
        ck = _GdnBatch(q, k, v, gcol, grow, beta, hb)
        S = jnp.concatenate([s_scr[0, hs], s_scr[1, hs]], axis=0)
        sf_ref[0], sb_ref[0] = S[:hb], S[hb:]
        af_ref[0], ab_ref[0] = ck.A[:hb], ck.A[hb:]
        v_new = ck.u - _mx(ck.w, S, "nn")
        o = _mx(ck.qd, S, "nn") + _mx(ck.attn, v_new, "nn")
        S_new = S * ck.egl + _mx(ck.kd, v_new, "tn")
        for hh in range(hb):
            sl = slice(hh * GDN_DIM, (hh + 1) * GDN_DIM)
            of_ref[:, sl] = o[hh]
            ob_ref[:, sl] = o[hb + hh]
        s_scr[0, hs] = S_new[:hb]
        s_scr[1, hs] = S_new[hb:]

    fw, bw = (lambda n: n), (lambda n: NB - 1 - n)
    qsp, gsp, osp, ssp, asp = _gdn_specs(cps, hb)
    o_shape = jax.ShapeDtypeStruct((T, nh * GDN_DIM), f32)
    s_shape = jax.ShapeDtypeStruct((N, nh, GDN_DIM, GDN_DIM), f32)
    a_shape = jax.ShapeDtypeStruct((N, nh, CHUNK, CHUNK), f32)
    return pl.pallas_call(
        body,
        out_shape=(o_shape, o_shape, s_shape, s_shape, a_shape, a_shape),
        grid=(NB, nh // hb),
        in_specs=[qsp(fw), qsp(bw), gsp(fw), gsp(bw), gsp(fw), gsp(bw)],
        out_specs=(osp(fw), osp(bw), ssp(fw), ssp(bw), asp(fw), asp(bw)),
        scratch_shapes=[pltpu.VMEM((2, nh, GDN_DIM, GDN_DIM), f32)],
        compiler_params=_cparams(("arbitrary", "arbitrary")),
        name="gdn_scan_fwd",
    )(qkvn, qkvn, gates, gates, gc, gc)


def _gdn_bwd(qkvn, gates, gc, st_f, st_b, a_f, a_b, do):
    T = qkvn.shape[1]
    N = T // CHUNK
    nh, hb = GDN_HEADS, GDN_HB
    nb = nh // hb
    cps = min(GDN_CPS_BWD, N)
    NB = N // cps

    def body(*refs):
        n, b = pl.program_id(0), pl.program_id(1)
        ds_scr = refs[-1]

        @pl.when(n == 0)
        def _():
            ds_scr[0, pl.ds(b * hb, hb)] = jnp.zeros((hb, GDN_DIM, GDN_DIM), f32)
            ds_scr[1, pl.ds(b * hb, hb)] = jnp.zeros((hb, GDN_DIM, GDN_DIM), f32)

        for s in range(cps):
            chunk(*_chunk_views(refs[:-1], cps - 1 - s, s), ds_scr, b)

    def chunk(qf_ref, qb_ref, gf_ref, gb_ref, cf_ref, cb_ref, sf_ref, sb_ref, af_ref, ab_ref, dof_ref, dob_ref,
              dqf_ref, dqb_ref, dgf_ref, dgb_ref, ds_scr, b):
        hs = pl.ds(b * hb, hb)

        @pl.when(b == 0)
        def _():
            dgf_ref[...] = jnp.zeros((CHUNK, LANES), f32)
            dgb_ref[...] = jnp.zeros((CHUNK, LANES), f32)

        q, k, v, gcol, grow, beta = _gdn_chain_inputs(qf_ref, qb_ref, gf_ref, gb_ref, cf_ref, cb_ref, b, hb)
        ck = _GdnBatch(q, k, v, gcol, grow, beta, hb, A=jnp.concatenate([af_ref[0], ab_ref[0]], axis=0))
        S = jnp.concatenate([sf_ref[0], sb_ref[0]], axis=0)
        dS = jnp.concatenate([ds_scr[0, hs], ds_scr[1, hs]], axis=0)
        dov = jnp.stack(_heads(dof_ref, (), hb, GDN_DIM) + _heads(dob_ref, (), hb, GDN_DIM))
        v_new = ck.u - _mx(ck.w, S, "nn")
        d_vnew = _mx(ck.attn, dov, "tn") + _mx(ck.kd, dS, "nn")
        d_attn = jnp.where(ck.incl, _mx(dov, v_new, "nt"), 0.0)
        d_qd = _mx(dov, S, "nt")
        d_kd = _mx(v_new, dS, "nt")
        d_gl = jnp.sum(jnp.sum(dS * S, axis=2, keepdims=True), axis=1, keepdims=True) * ck.egl
        dS_new = dS * ck.egl + _mx(ck.qd, dov, "tn") - _mx(ck.w, d_vnew, "tn")
        ds_scr[0, hs] = dS_new[:hb]
        ds_scr[1, hs] = dS_new[hb:]
        d_w = -_mx(d_vnew, S, "nt")
        d_rhs = _h3s(ck.A2, _split2(jnp.concatenate([d_vnew, d_w], axis=2)), "tn")
        d_vb, d_kbg = d_rhs[:, :, :GDN_DIM], d_rhs[:, :, GDN_DIM:]
        dL = -jnp.where(ck.strict, _h3(d_rhs, ck.sol, "nt"), 0.0)
        P = dL * ck.decay
        Q = d_attn * ck.decay
        d_kb = _mx(P, k, "nn") + d_kbg * ck.eg
        d_qs = _mx(Q, k, "nn") + d_qd * ck.eg
        d_k = _mx(P, ck.kb, "tn") + _mx(Q, ck.qs, "tn") + d_kd * ck.ekd + d_kb * ck.beta
        E = dL * ck.L + d_attn * ck.attn
        kdsum = jnp.sum(d_kd * ck.kd, axis=2, keepdims=True)
        d_gc = (jnp.sum(E, axis=2, keepdims=True) - _colsum_col(E)
                + jnp.sum(d_kbg * ck.kbg, axis=2, keepdims=True)
                + jnp.sum(d_qd * ck.qd, axis=2, keepdims=True) - kdsum)
        d_gc = _add_at_last_row(d_gc, d_gl + jnp.sum(kdsum, axis=1, keepdims=True), hb)
        d_beta = jnp.sum(d_vb * v, axis=2, keepdims=True) + jnp.sum(d_kb * k, axis=2, keepdims=True)
        d_q = d_qs * (GDN_DIM**-0.5)
        d_v = d_vb * ck.beta
        lane = _iota2((CHUNK, LANES), 1)
        for d, (dq_ref, dg_ref) in enumerate(((dqf_ref, dgf_ref), (dqb_ref, dgb_ref))):
            acc = dg_ref[...]
            for hh in range(hb):
                c, h = d * hb + hh, b * hb + hh
                sl = slice(hh * GDN_DIM, (hh + 1) * GDN_DIM)
                dq_ref[0, :, sl] = d_q[c]
                dq_ref[1, :, sl] = d_k[c]
                dq_ref[2, :, sl] = d_v[c]
                acc = jnp.where(lane == h + 8 * d, d_gc[c], acc)
                acc = jnp.where(lane == 16 + h + 8 * d, d_beta[c], acc)
            dg_ref[...] = acc

        @pl.when(b == nb - 1)
        def _():
            for d, dg_ref in enumerate((dgf_ref, dgb_ref)):
                blk = dg_ref[...]
                dg_ref[...] = jnp.where(lane < 16, _exact01(_chunk_tri(CHUNK, d == 0), blk), blk)

    fw, bw = (lambda n: NB - 1 - n), (lambda n: n)
    qsp, gsp, osp, ssp, asp = _gdn_specs(cps, hb)
    dq_shape = jax.ShapeDtypeStruct((3, T, nh * GDN_DIM), f32)
    dg_shape = jax.ShapeDtypeStruct((T, LANES), f32)
    return pl.pallas_call(
        body,
        out_shape=(dq_shape, dq_shape, dg_shape, dg_shape),
        grid=(NB, nb),
        in_specs=[qsp(fw), qsp(bw), gsp(fw), gsp(bw), gsp(fw), gsp(bw), ssp(fw), ssp(bw), asp(fw), asp(bw), osp(fw), osp(bw)],
        out_specs=(qsp(fw), qsp(bw), gsp(fw), gsp(bw)),
        scratch_shapes=[pltpu.VMEM((2, nh, GDN_DIM, GDN_DIM), f32)],
        compiler_params=_cparams(("arbitrary", "arbitrary")),
        name="gdn_scan_bwd",
    )(qkvn, qkvn, gates, gates, gc, gc, st_f, st_b, a_f, a_b, do, do)


class _GlaBatch:
    def __init__(self, q, k, G, nf):
        B, Cn = q.shape[0], q.shape[1]
        self.incl, _, _ = _dir_masks(B, nf, Cn)
        self.eG = jnp.exp(G)
        self.enG = jnp.exp(-G)
        self.qg = q * (GLA_DK**-0.5) * self.eG
        self.kg = k * self.enG
        self.attn = jnp.where(self.incl, _mx(self.qg, self.kg, "nt"), 0.0)
        gl = _last_row(G, nf)
        self.ekd = jnp.exp(gl - G)
        self.kd = k * self.ekd
        self.egl = jnp.exp(gl)


GLA_CPS = 4


def _chunk_rows(s):
    return slice(s * CHUNK, (s + 1) * CHUNK)


def _gla_chain_inputs(refs_f, refs_b, rows_f, rows_b):
    nh = GLA_HEADS
    out = []
    for i, width in enumerate((GLA_DK, GLA_DK, GLA_DV, GLA_DK)):
        out.append(jnp.stack(_heads(refs_f[i], (), nh, width, rows_f) + _heads(refs_b[i], (), nh, width, rows_b)))
    return out


def _gla_specs(rows, ci):
    nh = GLA_HEADS
    return [
        pl.BlockSpec((rows, nh * GLA_DK), lambda n: (ci(n), 4096 // (nh * GLA_DK))),
        pl.BlockSpec((rows, nh * GLA_DK), lambda n: (ci(n), 4608 // (nh * GLA_DK))),
        pl.BlockSpec((rows, nh * GLA_DV), lambda n: (ci(n), 5120 // (nh * GLA_DV))),
        pl.BlockSpec((rows, nh * GLA_DK), lambda n: (ci(n), 0)),
    ]


def _gla_fwd(proj, G_f, G_b):
    T = proj.shape[0]
    N = T // CHUNK
    nh = GLA_HEADS
    cps = min(GLA_CPS, N)
    NB = N // cps

    def body(qf, kf, vf, gf, qb, kb, vb, gb, of_ref, ob_ref, sf_ref, sb_ref, s_scr):
        n = pl.program_id(0)

        @pl.when(n == 0)
        def _():
            s_scr[...] = jnp.zeros((2 * nh, GLA_DV, GLA_DK), f32)

        for s in range(cps):
            sb = cps - 1 - s
            q, k, v, G = _gla_chain_inputs((qf, kf, vf, gf), (qb, kb, vb, gb), _chunk_rows(s), _chunk_rows(sb))
            ck = _GlaBatch(q, k, G, nh)
            St = s_scr[...]
            sf_ref[s], sb_ref[sb] = St[:nh].astype(sf_ref.dtype), St[nh:].astype(sb_ref.dtype)
            o = _mx(ck.qg, St, "nt") + _mx(ck.attn, v, "nn")
            for i in range(nh):
                sl = slice(i * GLA_DV, (i + 1) * GLA_DV)
                of_ref[_chunk_rows(s), sl] = o[i]
                ob_ref[_chunk_rows(sb), sl] = o[nh + i]
            s_scr[...] = St * ck.egl + _mx(v, ck.kd, "tn")

    fw, bw = (lambda n: n), (lambda n: NB - 1 - n)
    osp = lambda ci: pl.BlockSpec((cps * CHUNK, nh * GLA_DV), lambda n: (ci(n), 0))
    ssp = lambda ci: pl.BlockSpec((cps, nh, GLA_DV, GLA_DK), lambda n: (ci(n), 0, 0, 0))
    o_shape = jax.ShapeDtypeStruct((T, nh * GLA_DV), f32)
    s_shape = jax.ShapeDtypeStruct((N, nh, GLA_DV, GLA_DK), MXU_DTYPE)
    return pl.pallas_call(
        body,
        out_shape=(o_shape, o_shape, s_shape, s_shape),
        grid=(NB,),
        in_specs=_gla_specs(cps * CHUNK, fw) + _gla_specs(cps * CHUNK, bw),
        out_specs=(osp(fw), osp(bw), ssp(fw), ssp(bw)),
        scratch_shapes=[pltpu.VMEM((2 * nh, GLA_DV, GLA_DK), f32)],
        compiler_params=_cparams(("arbitrary",)),
        name="gla_scan_fwd",
    )(proj, proj, proj, G_f, proj, proj, proj, G_b)


def _gla_bwd(proj, G_f, G_b, st_f, st_b, do):
    T = proj.shape[0]
    N = T // CHUNK
    nh = GLA_HEADS
    cps = min(GLA_CPS, N)
    NB = N // cps

    def body(qf, kf, vf, gf, sf_ref, dof_ref, qb, kb, vb, gb, sb_ref, dob_ref,
             dqf_ref, dkf_ref, dvf_ref, dgf_ref, dqb_ref, dkb_ref, dvb_ref, dgb_ref, ds_scr):
        n = pl.program_id(0)

        @pl.when(n == 0)
        def _():
            ds_scr[...] = jnp.zeros((2 * nh, GLA_DV, GLA_DK), f32)

        shp = (2 * nh, CHUNK, CHUNK)
        bb, ii, jj = _iota3(shp, 0), _iota3(shp, 1), _iota3(shp, 2)
        tri = jnp.where(jnp.logical_or(jnp.logical_and(bb < nh, jj >= ii), jnp.logical_and(bb >= nh, jj <= ii)), 1.0, 0.0)
        for s in range(cps):
            sf = cps - 1 - s
            rows = (_chunk_rows(sf), _chunk_rows(s))
            q, k, v, G = _gla_chain_inputs((qf, kf, vf, gf), (qb, kb, vb, gb), *rows)
            ck = _GlaBatch(q, k, G, nh)
            St = jnp.concatenate([sf_ref[sf], sb_ref[s]], axis=0).astype(f32)
            dSt = ds_scr[...]
            dov = jnp.stack(_heads(dof_ref, (), nh, GLA_DV, rows[0]) + _heads(dob_ref, (), nh, GLA_DV, rows[1]))
            d_attn = jnp.where(ck.incl, _mx(dov, v, "nt"), 0.0)
            d_qg = _mx(dov, St, "nn") + _mx(d_attn, ck.kg, "nn")
            d_kg = _mx(d_attn, ck.qg, "tn")
            d_v = _mx(ck.attn, dov, "tn") + _mx(ck.kd, dSt, "nt")
            d_kd = _mx(v, dSt, "nn")
            ds_scr[...] = dSt * ck.egl + _mx(dov, ck.qg, "tn")
            kdd = d_kd * ck.kd
            d_gl = jnp.sum(dSt * St, axis=1, keepdims=True) * ck.egl + jnp.sum(kdd, axis=1, keepdims=True)
            d_q = d_qg * ck.eG * (GLA_DK**-0.5)
            d_k = d_kg * ck.enG + d_kd * ck.ekd
            d_gk = _exact01(tri, _add_at_last_row(d_qg * ck.qg - d_kg * ck.kg - kdd, d_gl, nh))
            for d, (dq_ref, dk_ref, dv_ref, dg_ref) in enumerate(((dqf_ref, dkf_ref, dvf_ref, dgf_ref), (dqb_ref, dkb_ref, dvb_ref, dgb_ref))):
                for i in range(nh):
                    c = d * nh + i
                    ks, vs = slice(i * GLA_DK, (i + 1) * GLA_DK), slice(i * GLA_DV, (i + 1) * GLA_DV)
                    dq_ref[rows[d], ks] = d_q[c]
                    dk_ref[rows[d], ks] = d_k[c]
                    dv_ref[rows[d], vs] = d_v[c]
                    dg_ref[rows[d], ks] = d_gk[c]

    R = cps * CHUNK

    def in_specs(ci):
        return _gla_specs(R, ci) + [
            pl.BlockSpec((cps, nh, GLA_DV, GLA_DK), lambda n: (ci(n), 0, 0, 0)),
            pl.BlockSpec((R, nh * GLA_DV), lambda n: (ci(n), 0)),
        ]

    def out_specs(ci):
        ksp = pl.BlockSpec((R, nh * GLA_DK), lambda n: (ci(n), 0))
        return [ksp, ksp, pl.BlockSpec((R, nh * GLA_DV), lambda n: (ci(n), 0)), ksp]

    fw, bw = (lambda n: NB - 1 - n), (lambda n: n)
    k_shape = jax.ShapeDtypeStruct((T, nh * GLA_DK), f32)
    v_shape = jax.ShapeDtypeStruct((T, nh * GLA_DV), f32)
    return pl.pallas_call(
        body,
        out_shape=(k_shape, k_shape, v_shape, k_shape) * 2,
        grid=(NB,),
        in_specs=in_specs(fw) + in_specs(bw),
        out_specs=tuple(out_specs(fw) + out_specs(bw)),
        scratch_shapes=[pltpu.VMEM((2 * nh, GLA_DV, GLA_DK), f32)],
        compiler_params=_cparams(("arbitrary",)),
        name="gla_scan_bwd",
    )(proj, proj, proj, G_f, st_f, do, proj, proj, proj, G_b, st_b, do)


def _local_step(x, target, wcat, conv_w, avec, dvec, gdn_norm_w, w2f, w2b, b2f, b2b, gla_norm_w, proj_weights, w_pre, w_post,
                send_proj_grads=None, send_in_grads=None):
    h, h_t = _rms_pre(x, w_pre)
    proj = _matmul(h, wcat, "nt", f32, "proj_in", 2048, 512, 1024)
    qkvn = _gdn_prep(proj, conv_w)
    gates, gc, G_f, G_b = _gate_prep(proj, avec, dvec, w2f, w2b, b2f, b2b)
    oa_f, oa_b, sa_f, sa_b, inv_f, inv_b = _gdn_fwd(qkvn, gates, gc)
    og = _out_norm(oa_f, oa_b, proj, Z_BLK, gdn_norm_w, GDN_DIM, "gdn_out")
    ob_f, ob_b, sb_f, sb_b = _gla_fwd(proj, G_f, G_b)
    obg = _out_norm(ob_f, ob_b, proj, G_B_BLK, gla_norm_w, GLA_DV, "gla_out")
    wpg, wpl, wout = proj_weights(obg)
    (loss_row, d_wpost, dy, m, dout, dya, dyb, dproj, dog, dobg) = _mid(og, obg, proj, x, target, wpg, wpl, wout, w_post)

    d_wout = _matmul(m, dout, "tn", MXU_DTYPE, "dw_out", 1024, 512, 2048)
    d_wpg = _matmul(og, dya, "tn", MXU_DTYPE, "dw_proj_gdn", 1024, 512, 2048)
    d_wpl = _matmul(obg, dyb, "tn", MXU_DTYPE, "dw_proj_gla", 1024, 512, 2048)
    if send_proj_grads is not None:
        gla_norm_w = gla_norm_w + send_proj_grads(d_wpg, d_wpl, d_wout)

    do_b, dproj, d_gla_norm = _out_norm_bwd(dobg, ob_f, ob_b, proj, G_B_BLK, gla_norm_w, GLA_DV, "gla_out_bwd", dproj)
    dq_f, dk_f, dv_f, dgk_f, dq_b, dk_b, dv_b, dgk_b = _gla_bwd(proj, G_f, G_b, sb_f, sb_b, do_b)
    dproj, dsmall_gla, d_w2f, d_w2b, d_b2f, d_b2b = _gla_prep_bwd(proj, w2f, w2b, b2f, b2b, dgk_f, dgk_b,
                                                                 (dq_f, dq_b, dk_f, dk_b, dv_f, dv_b), dproj)

    do_a, dproj, d_gdn_norm = _out_norm_bwd(dog, oa_f, oa_b, proj, Z_BLK, gdn_norm_w, GDN_DIM, "gdn_out_bwd", dproj)
    dqkv_f, dqkv_b, dg_f, dg_b = _gdn_bwd(qkvn, gates, gc, sa_f, sa_b, inv_f, inv_b, do_a)
    dproj, d_conv = _gdn_prep_bwd(proj, conv_w, dqkv_f, dqkv_b, dproj)
    dproj, d_alog, d_dtb = _gdn_gates_bwd(proj, avec, dvec, dg_f, dg_b, dsmall_gla, dproj)

    d_wcat = _matmul(h_t, dproj, "nn", MXU_DTYPE, "dw_in", 1024, 512, 2048, out_t=True)
    started = None if send_in_grads is None else send_in_grads(d_wcat)
    dh = _matmul(dproj, wcat, "nn", f32, "dh", 512, 1024, N_CAT, after=started)
    grad_x, d_wpre = _rms_pre_bwd(x, w_pre, dh, dy)
    return dict(loss_row=loss_row, grad_x=grad_x, d_wcat=d_wcat, d_conv=d_conv, d_alog=d_alog, d_dtb=d_dtb,
                d_gdn_norm=d_gdn_norm, d_wpg=d_wpg, d_w2f=d_w2f, d_w2b=d_w2b, d_b2f=d_b2f, d_b2b=d_b2b,
                d_gla_norm=d_gla_norm, d_wpl=d_wpl, d_wout=d_wout, d_wpost=d_wpost, d_wpre=d_wpre)


def _to_cat(wf):
    pad = jnp.zeros((N_CAT - N_IN,) + wf.shape[1:], wf.dtype)
    return jnp.concatenate([wf[:4096], wf[4128:6176], wf[7232:9280], wf[6176:7200], wf[4096:4128], wf[7200:7232], pad], axis=0)


def _from_cat(wc):
    return jnp.concatenate([wc[:4096], wc[9216:9248], wc[4096:6144], wc[8192:9216], wc[9248:9280], wc[6144:8192]], axis=0)


def _lane_row(*pieces):
    row = jnp.concatenate(pieces, axis=1)
    return jnp.pad(row, ((0, 0), (0, LANES - row.shape[1])))


def _pad_w2(w2, row0):
    return jnp.pad(w2, ((row0, LANES - row0 - w2.shape[0]), (0, 0)))


SM_ROWS, SM_COLS = 32, 384


def _pack_small_shard(conv, w2f, w2b):
    lead = conv.shape[:-2]
    z = lambda r, c: jnp.zeros(lead + (r, c), f32)
    top = jnp.concatenate([conv, z(8 - CONV_K, SM_COLS)], axis=-2)
    mid = jnp.concatenate([w2f, w2b, z(16, SM_COLS - 128)], axis=-1)
    return jnp.concatenate([top, mid, z(SM_ROWS - 24, SM_COLS)], axis=-2)


SMALL_REGION = {
    "ln_pre_w": (0, 1, 0, 1024), "ln_post_w": (1, 1, 0, 1024), "gk_b2_fwd": (2, 1, 0, 512), "gk_b2_bwd": (2, 1, 512, 512),
    "gla_norm_w": (3, 1, 0, 256), "gdn_norm_w": (3, 1, 256, 128), "a_log_fwd": (4, 1, 0, 8), "a_log_bwd": (5, 1, 0, 8),
    "dt_bias_fwd": (6, 1, 0, 8), "dt_bias_bwd": (7, 1, 0, 8), "conv_w": (8, CONV_K, 0, SM_COLS),
    "gk_w2_fwd": (16, 16, 0, 64), "gk_w2_bwd": (16, 16, 128, 64),
}
SMALL_NAMES = tuple(SMALL_REGION)
SMALL_SHARDED = ("conv_w", "gk_w2_fwd", "gk_w2_bwd")
PACK_ROWS, PACK_COLS = 32, 1024
LOSS_AT = (13, 0)


def _region(name):
    r0, nr, l0, nl = SMALL_REGION[name]
    return slice(r0, r0 + nr), slice(l0, l0 + nl)


def _pack_small_grads(G):
    names = ("d_wpre", "d_wpost", "d_b2f", "d_b2b", "d_gla_norm", "d_gdn_norm", "d_alog", "d_dtb", "loss_row", "d_conv", "d_w2f", "d_w2b")

    def body(wpre, wpost, b2f, b2b, glan, gdnn, alog, dtb, lrow, conv, w2f, w2b, o_ref):
        o_ref[...] = jnp.zeros(o_ref.shape, f32)
        loss = 0.5 * jnp.sum(lrow[...], axis=1, keepdims=True) * (1.0 / D_MODEL)
        whole = {"ln_pre_w": wpre, "ln_post_w": wpost, "gk_b2_fwd": b2f, "gk_b2_bwd": b2b, "gla_norm_w": glan, "gdn_norm_w": gdnn}
        for j in range(N_DEV):
            for name, ref in whole.items():
                o_ref[(j,) + _region(name)] = ref[...]
            o_ref[(j,) + _region("a_log_fwd")] = alog[0:1, 0:8]
            o_ref[(j,) + _region("a_log_bwd")] = alog[0:1, 8:16]
            o_ref[(j,) + _region("dt_bias_fwd")] = dtb[0:1, 0:8]
            o_ref[(j,) + _region("dt_bias_bwd")] = dtb[0:1, 8:16]
            o_ref[(j,) + _region("conv_w")] = conv[:, j * SM_COLS : (j + 1) * SM_COLS]
            o_ref[(j,) + _region("gk_w2_fwd")] = w2f[32:48, j * 64 : (j + 1) * 64]
            o_ref[(j,) + _region("gk_w2_bwd")] = w2b[48:64, j * 64 : (j + 1) * 64]
            o_ref[j, LOSS_AT[0] : LOSS_AT[0] + 1, LOSS_AT[1] : LOSS_AT[1] + 1] = loss

    return pl.pallas_call(
        body, out_shape=jax.ShapeDtypeStruct((N_DEV, PACK_ROWS, PACK_COLS), f32), name="pack_small_grads",
    )(*[G[n] for n in names])


def _small_adam(land, own, W, M, V):
    bc1 = 1.0 - ADAM_B1**ADAM_STEP
    bc2 = 1.0 - ADAM_B2**ADAM_STEP
    n = len(SMALL_NAMES)

    def body(land_ref, own_ref, *refs):
        w_refs, m_refs, v_refs = refs[0:n], refs[n : 2 * n], refs[2 * n : 3 * n]
        outs, loss_ref, g_scr = refs[3 * n : 7 * n], refs[7 * n], refs[7 * n + 1]
        me = _my_device()
        g = jnp.where(me == 0, own_ref[0], land_ref[0])
        for j in range(1, N_DEV):
            g = g + jnp.where(me == j, own_ref[j], land_ref[j])
        g_scr[...] = g
        loss_ref[...] = g_scr[LOSS_AT[0] : LOSS_AT[0] + 1, LOSS_AT[1] : LOSS_AT[1] + 1]
        for i, name in enumerate(SMALL_NAMES):
            gp = g_scr[_region(name)]
            lead = (0,) if name in SMALL_SHARDED else (Ellipsis,)
            w, m, v = w_refs[i][lead], m_refs[i][lead], v_refs[i][lead]
            m2 = ADAM_B1 * m + (1.0 - ADAM_B1) * gp
            v2 = ADAM_B2 * v + (1.0 - ADAM_B2) * (gp * gp)
            delta = -ADAM_LR * ((m2 / bc1) / (jnp.sqrt(v2 / bc2) + ADAM_EPS) + ADAM_WD * w)
            for k, val in enumerate((gp, delta, m2, v2)):
                outs[4 * i + k][lead] = val

    shapes = [jax.ShapeDtypeStruct(W[name].shape, f32) for name in SMALL_NAMES for _ in range(4)]
    res = pl.pallas_call(
        body,
        out_shape=tuple(shapes) + (jax.ShapeDtypeStruct((1, 1), f32),),
        scratch_shapes=[pltpu.VMEM((PACK_ROWS, PACK_COLS), f32)],
        name="adam_small",
    )(land, own, *[P[name] for P in (W, M, V) for name in SMALL_NAMES])
    return {name: res[4 * i : 4 * i + 4] for i, name in enumerate(SMALL_NAMES)}, res[4 * n]


_MESH = pl.DeviceIdType.MESH


N_BIG_COPIES, N_SMALL_COPIES = 9, 7


def _gather_weights(w_in_s, sm_s):
    shapes = (w_in_s.shape, sm_s.shape)
    dtypes = (MXU_DTYPE, f32)

    def body(win_ref, sm_ref, gin_ref, gsm_ref, send_sems, recv_sems):
        x, y, c = lax.axis_index("x"), lax.axis_index("y"), lax.axis_index("c")
        me, sibling = (x, y, c), (x, y, 1 - c)
        chips = [(1 - x, y), (x, 1 - y), (1 - x, 1 - y)]

        def idx(px, py, pc):
            return 4 * px + 2 * py + pc

        for r in range(w_in_s.shape[1] // LANES):
            cols = slice(r * LANES, (r + 1) * LANES)
            gin_ref[idx(*me), :, cols] = win_ref[:, cols].astype(MXU_DTYPE)
        gsm_ref[idx(*me)] = sm_ref[...]

        def copy(sem, blk, to):
            return pltpu.make_async_remote_copy(src_ref=blk, dst_ref=blk, send_sem=send_sems.at[sem], recv_sem=recv_sems.at[sem],
                                                device_id=to, device_id_type=_MESH)

        def small(k, block, to):
            return copy(N_BIG_COPIES + k, gsm_ref.at[idx(*block)], to)

        half = w_in_s.shape[1] // 2
        xn, yn, dg = (1 - x, y), (x, 1 - y), (1 - x, 1 - y)

        def big(sem, block, to, part=None):
            blk = gin_ref.at[idx(*block)]
            return copy(sem, blk if part is None else blk.at[:, pl.ds(part * half, half)], to)

        started = [big(0, me, sibling), big(1, me, (*xn, c)), big(2, me, (*yn, c)), small(0, me, sibling)]
        started += [small(1 + j, me, (*chip, c)) for j, chip in enumerate(chips)]
        for cp in started:
            cp.start()

        def then(arrived, forwards):
            arrived.wait_recv()
            for cp in forwards:
                cp.start()
            started.extend(forwards)

        then(big(1, (*xn, c), me), [big(3, (*xn, c), (*yn, c), 1), big(5, (*xn, c), sibling)])
        then(big(2, (*yn, c), me), [big(4, (*yn, c), (*xn, c), 0), big(6, (*yn, c), sibling)])
        then(big(4, (*dg, c), me, 0), [big(7, (*dg, c), sibling, 0)])
        then(big(3, (*dg, c), me, 1), [big(8, (*dg, c), sibling, 1)])
        for j, chip in enumerate(chips):
            then(small(1 + j, (*chip, c), me), [small(4 + j, (*chip, c), sibling)])
        big(0, sibling, me).wait_recv()
        big(5, (*xn, 1 - c), me).wait_recv()
        big(6, (*yn, 1 - c), me).wait_recv()
        big(7, (*dg, 1 - c), me, 0).wait_recv()
        big(8, (*dg, 1 - c), me, 1).wait_recv()
        small(0, sibling, me).wait_recv()
        for j, chip in enumerate(chips):
            small(4 + j, (*chip, 1 - c), me).wait_recv()
        for cp in started:
            cp.wait_send()

    vm = pl.BlockSpec(memory_space=pltpu.VMEM)
    return pl.pallas_call(
        body,
        out_shape=tuple(jax.ShapeDtypeStruct((N_DEV,) + s, d) for s, d in zip(shapes, dtypes)),
        in_specs=[vm, vm],
        out_specs=(vm, vm),
        scratch_shapes=[pltpu.SemaphoreType.DMA((N_BIG_COPIES + N_SMALL_COPIES,)), pltpu.SemaphoreType.DMA((N_BIG_COPIES + N_SMALL_COPIES,))],
        compiler_params=pltpu.CompilerParams(vmem_limit_bytes=VMEM_LIMIT),
        name="gather_weights",
    )(w_in_s, sm_s)


N_CHIP = 4
_EFFECT = pltpu.SideEffectType.DATAFLOW_SIDE_EFFECTING
_HBM_SPEC = pl.BlockSpec(memory_space=pltpu.HBM)
_SEM_SPEC = pl.BlockSpec(memory_space=pltpu.SEMAPHORE)


def _relation_peer(k, x, y, c):
    px = (1 - x) if (k & 4) else x
    py = (1 - y) if (k & 2) else y
    pc = (1 - c) if (k & 1) else c
    return (px, py, pc), 4 * px + 2 * py + pc


_PLAN_COPIES = {"gather": N_DEV - 1, "blocks": N_DEV - 1, "sibling": 4, "chips": 3}
_PLAN_SLOTS = {"gather": N_DEV, "blocks": N_DEV, "sibling": 4, "chips": 4}


def _plan_copies(plan, src_ref, land_ref, send_sems, recv_sems, arrival):
    x, y, c = lax.axis_index("x"), lax.axis_index("y"), lax.axis_index("c")
    me, my_chip = 4 * x + 2 * y + c, 2 * x + y
    out = []

    def add(i, src, there, here, dev):
        out.append(pltpu.make_async_remote_copy(src_ref=src, dst_ref=land_ref.at[here if arrival else there], send_sem=send_sems.at[i],
                                                recv_sem=recv_sems.at[i], device_id=dev, device_id_type=_MESH))

    if plan in ("gather", "blocks"):
        for k in range(1, N_DEV):
            dev, p = _relation_peer(k, x, y, c)
            add(k - 1, src_ref.at[p] if plan == "blocks" else src_ref, me, p, dev)
    elif plan == "sibling":
        for j in range(4):
            add(j, src_ref.at[2 * j + 1 - c], j, j, (x, y, 1 - c))
    else:
        for i, k in enumerate((2, 4, 6)):
            (px, py, pc), _ = _relation_peer(k, x, y, c)
            add(i, src_ref.at[2 * px + py], my_chip, 2 * px + py, (px, py, pc))
    return out


def _exchange_start(src, plan, name, after=None):
    n = _PLAN_COPIES[plan]
    land = lax.empty((_PLAN_SLOTS[plan],) + (src.shape if plan == "gather" else src.shape[1:]), src.dtype)
    extra = [] if after is None else [after]

    def body(src_ref, land_ref, *rest):
        send_sems, recv_sems, _, _, token = rest[len(extra) :]
        for cp in _plan_copies(plan, src_ref, land_ref, send_sems, recv_sems, False):
            cp.start()
        token[...] = jnp.zeros_like(token)

    return pl.pallas_call(
        body,
        name=name,
        out_shape=(pltpu.SemaphoreType.DMA((n,)), pltpu.SemaphoreType.DMA((n,)), pltpu.HBM(src.shape, src.dtype),
                   pltpu.HBM(land.shape, land.dtype), jax.ShapeDtypeStruct((8, LANES), f32)),
        in_specs=(_HBM_SPEC, _HBM_SPEC) + (_ANY_SPEC,) * len(extra),
        out_specs=(_SEM_SPEC, _SEM_SPEC, _HBM_SPEC, _HBM_SPEC, pl.BlockSpec(memory_space=pltpu.VMEM)),
        input_output_aliases={0: 2, 1: 3},
        compiler_params=pltpu.CompilerParams(has_side_effects=_EFFECT),
    )(pltpu.with_memory_space_constraint(src, pltpu.HBM), pltpu.with_memory_space_constraint(land, pltpu.HBM), *extra)


def _exchange_wait(started, after, plan, name):
    send_sems, recv_sems, src_thru, land_thru, _ = started

    def body(src_ref, land_ref, send_sems, recv_sems, after_ref, src_dead, got_ref):
        for cp in _plan_copies(plan, src_ref, land_ref, send_sems, recv_sems, True):
            cp.wait_send()
            cp.wait_recv()

    return pl.pallas_call(
        body,
        name=name,
        out_shape=(pltpu.HBM(src_thru.shape, src_thru.dtype), pltpu.HBM(land_thru.shape, land_thru.dtype)),
        in_specs=(_HBM_SPEC, _HBM_SPEC, _SEM_SPEC, _SEM_SPEC, pl.BlockSpec(memory_space=pl.ANY)),
        out_specs=(_HBM_SPEC, _HBM_SPEC),
        input_output_aliases={0: 0, 1: 1},
        compiler_params=pltpu.CompilerParams(has_side_effects=_EFFECT),
    )(src_thru, land_thru, send_sems, recv_sems, after)


def _pair_sum(g, r, name, tc=LANES):
    _, R, Cc = g.shape

    def body(g_ref, r_ref, o_ref):
        o_ref[...] = (g_ref[...].astype(f32) + r_ref[...].astype(f32)).astype(o_ref.dtype)

    return pl.pallas_call(
        body,
        out_shape=jax.ShapeDtypeStruct(r.shape, r.dtype),
        grid=(N_CHIP, Cc // tc),
        in_specs=[pl.BlockSpec((1, R, tc), lambda k, i: (2 * k + lax.axis_index("c"), 0, i)), pl.BlockSpec((1, R, tc), lambda k, i: (k, 0, i))],
        out_specs=pl.BlockSpec((1, R, tc), lambda k, i: (k, 0, i)),
        compiler_params=_cparams(("parallel", "parallel")),
        name=name,
    )(g, r)


def _sum_adam(parts, w, m, v, name, tb, tc=None, own=None, own_slot=None):
    R, Cc = w.shape
    n_parts = parts.shape[0]
    tb = R if tc else min(tb, R)
    tc = tc or Cc
    assert R % tb == 0 and Cc % tc == 0
    bc1 = 1.0 - ADAM_B1**ADAM_STEP
    bc2 = 1.0 - ADAM_B2**ADAM_STEP
    extra = [] if own is None else [own]

    def body(p_ref, w_ref, m_ref, v_ref, *rest):
        g_ref, d_ref, nm_ref, nv_ref = rest[len(extra) :]

        def part(j):
            pj = p_ref[j].astype(f32)
            return pj if own is None else jnp.where(own_slot() == j, rest[0][0].astype(f32), pj)

        g = part(0)
        for j in range(1, n_parts):
            g = g + part(j)
        g_ref[...] = g
        m2 = ADAM_B1 * m_ref[...] + (1.0 - ADAM_B1) * g
        v2 = ADAM_B2 * v_ref[...] + (1.0 - ADAM_B2) * (g * g)
        nm_ref[...] = m2
        nv_ref[...] = v2
        d_ref[...] = -ADAM_LR * ((m2 / bc1) / (jnp.sqrt(v2 / bc2) + ADAM_EPS) + ADAM_WD * w_ref[...])

    blk = pl.BlockSpec((tb, tc), lambda i, j: (i, j))
    o = jax.ShapeDtypeStruct((R, Cc), f32)
    return pl.pallas_call(
        body,
        out_shape=(o, o, o, o),
        grid=(R // tb, Cc // tc),
        in_specs=[pl.BlockSpec((n_parts, tb, tc), lambda i, j: (0, i, j)), blk, blk, blk]
        + [pl.BlockSpec((1, tb, tc), lambda i, j: (own_slot(), i, j))] * len(extra),
        out_specs=(blk, blk, blk, blk),
        compiler_params=_cparams(("parallel", "parallel")),
        name=name,
    )(parts, w, m, v, *extra)


def _my_device():
    return 4 * lax.axis_index("x") + 2 * lax.axis_index("y") + lax.axis_index("c")


def _my_chip():
    return 2 * lax.axis_index("x") + lax.axis_index("y")


_WEIGHTS = ["ln_pre_w", "w_in", "conv_w", "a_log_fwd", "a_log_bwd", "dt_bias_fwd", "dt_bias_bwd", "gdn_norm_w", "w_proj_gdn",
            "gk_w2_fwd", "gk_b2_fwd", "gk_w2_bwd", "gk_b2_bwd", "gla_norm_w", "w_proj_gla", "w_out", "ln_post_w"]


def kernel(x, ln_pre_w, w_in, conv_w, a_log_fwd, a_log_bwd, dt_bias_fwd, dt_bias_bwd, gdn_norm_w, w_proj_gdn, gk_w2_fwd, gk_b2_fwd, gk_w2_bwd, gk_b2_bwd, gla_norm_w, w_proj_gla, w_out, ln_post_w, loss_target, m_ln_pre_w, m_w_in, m_conv_w, m_a_log_fwd, m_a_log_bwd, m_dt_bias_fwd, m_dt_bias_bwd, m_gdn_norm_w, m_w_proj_gdn, m_gk_w2_fwd, m_gk_b2_fwd, m_gk_w2_bwd, m_gk_b2_bwd, m_gla_norm_w, m_w_proj_gla, m_w_out, m_ln_post_w, v_ln_pre_w, v_w_in, v_conv_w, v_a_log_fwd, v_a_log_bwd, v_dt_bias_fwd, v_dt_bias_bwd, v_gdn_norm_w, v_w_proj_gdn, v_gk_w2_fwd, v_gk_b2_fwd, v_gk_w2_bwd, v_gk_b2_bwd, v_gla_norm_w, v_w_proj_gla, v_w_out, v_ln_post_w):
    args = locals()
    W = {n: args[n] for n in _WEIGHTS}
    M = {n: args["m_" + n] for n in _WEIGHTS}
    V = {n: args["v_" + n] for n in _WEIGHTS}

    wp_stack = lambda P: jnp.concatenate([P["w_proj_gdn"], P["w_proj_gla"], P["w_out"]], axis=0)
    sm_pack = lambda P: _pack_small_shard(P["conv_w"][0], P["gk_w2_fwd"][0], P["gk_w2_bwd"][0])
    me = 4 * lax.axis_index("x") + 2 * lax.axis_index("y") + lax.axis_index("c")
    w_in_t, m_in_t, v_in_t = (jnp.transpose(a[0]) for a in (w_in, m_w_in, v_w_in))
    g_in, g_sm = _gather_weights(w_in_t, sm_pack(W))
    wp_own = wp_stack(W).astype(MXU_DTYPE)
    wp_started = _exchange_start(wp_own, "gather", "gather_proj_start", after=g_sm)

    def proj_weights(after):
        own, land = _exchange_wait(wp_started, after, "gather", "gather_proj_wait")
        g_p = lax.dynamic_update_slice(land, own[None], (me, 0, 0, 0))
        return tuple(g_p[:, i].reshape(D_MODEL, D_MODEL) for i in range(3))

    rows_p = 3 * D_MODEL // N_DEV
    sent = {}

    def send_proj_grads(d_wpg, d_wpl, d_wout):
        p_p = jnp.stack([d_wpg, d_wpl, d_wout]).reshape(3, N_DEV, D_MODEL // N_DEV, D_MODEL).transpose(1, 0, 2, 3)
        sent["p_p"] = p_p.reshape(N_DEV, rows_p, D_MODEL)
        sent["started"] = _exchange_start(sent["p_p"], "blocks", "exchange_proj_start")
        return sent["started"][4][0:1, 0:1]

    def send_in_grads(d_wcat):
        p_in = _from_cat(d_wcat).reshape(N_DEV, SHARD_IN, D_MODEL)
        sib = _exchange_start(p_in, "sibling", "exchange_in_sibling_start")
        p_p, land_p = _exchange_wait(sent["started"], sib[4], "blocks", "exchange_proj_wait")
        sent["o_p"] = _sum_adam(land_p, wp_stack(W).reshape(rows_p, D_MODEL), wp_stack(M).reshape(rows_p, D_MODEL),
                                wp_stack(V).reshape(rows_p, D_MODEL), "adam_w_proj", 128, own=p_p, own_slot=_my_device)
        p_in, sib_land = _exchange_wait(sib, sent["o_p"][0], "sibling", "exchange_in_sibling_wait")
        sent["chips"] = _exchange_start(_pair_sum(p_in, sib_land, "pair_sum_w_in", tc=512), "chips", "exchange_in_chips_start")
        return sent["chips"][4]

    wcat = _to_cat(g_in.reshape(N_IN, D_MODEL))
    conv_full = g_sm[:, 0:CONV_K, :].transpose(1, 0, 2).reshape(CONV_K, N_DEV * SM_COLS)
    w2f_full = g_sm[:, 8:24, 0:64].transpose(1, 0, 2).reshape(16, 512)
    w2b_full = g_sm[:, 8:24, 64:128].transpose(1, 0, 2).reshape(16, 512)

    avec = _lane_row(a_log_fwd, a_log_bwd)
    dvec = _lane_row(dt_bias_fwd, dt_bias_bwd)
    G = _local_step(x[0], loss_target[0], wcat, conv_full, avec, dvec, gdn_norm_w, _pad_w2(w2f_full, 32), _pad_w2(w2b_full, 48),
                    gk_b2_fwd, gk_b2_bwd, gla_norm_w, proj_weights, ln_pre_w + wp_started[4][0:1, 0:1], ln_post_w, send_proj_grads,
                    send_in_grads)

    small_started = _exchange_start(_pack_small_grads(G), "blocks", "exchange_small_start")
    q_in, land_in = _exchange_wait(sent["chips"], small_started[4], "chips", "exchange_in_chips_wait")
    o_in = _sum_adam(land_in, w_in_t, m_in_t, v_in_t, "adam_w_in", SHARD_IN, tc=256, own=q_in, own_slot=_my_chip)
    o_p = sent["o_p"]
    p_small, land_small = _exchange_wait(small_started, o_in[0], "blocks", "exchange_small_wait")
    o_small, loss = _small_adam(land_small, p_small, W, M, V)

    res = []
    for kind in range(4):
        per = {name: o_small[name][kind] for name in SMALL_NAMES}
        per["w_in"] = jnp.transpose(o_in[kind])[None]
        pp = o_p[kind].reshape(3, 1, D_MODEL // N_DEV, D_MODEL)
        per["w_proj_gdn"], per["w_proj_gla"], per["w_out"] = pp[0], pp[1], pp[2]
        res.append([per[n] for n in _WEIGHTS])
    return (loss[0, 0], G["grad_x"][None], *res[0], *res[1], *res[2], *res[3])
```

```python
import jax
import jax.numpy as jnp
from jax import lax
from jax.experimental import pallas as pl
from jax.experimental.pallas import tpu as pltpu

f32 = jnp.float32
MXU_DTYPE = jnp.bfloat16

D_MODEL = 1024
CHUNK = 64
NORM_EPS = 1e-6
GDN_HEADS, GDN_DIM = 8, 128
GLA_HEADS, GLA_DK, GLA_DV = 4, 128, 256
GATE_NORMALIZER = 16.0
CONV_K = 5
N_IN = 9280
N_DEV = 8
SHARD_IN = N_IN // N_DEV

N_CAT = 9728
SMALL_OFF = 9216
LANES = 128

ADAM_LR, ADAM_B1, ADAM_B2, ADAM_EPS, ADAM_WD, ADAM_STEP = 0.001, 0.9, 0.999, 1e-08, 0.01, 10

VMEM_LIMIT = 56 * 1024 * 1024


def _cparams(sem=None):
    return pltpu.CompilerParams(dimension_semantics=sem, vmem_limit_bytes=VMEM_LIMIT)


_DN = {"nn": (((1,), (0,)), ((), ())), "nt": (((1,), (1,)), ((), ())), "tn": (((0,), (0,)), ((), ()))}
_BDN = {"nn": (((2,), (1,)), ((0,), (0,))), "nt": (((2,), (2,)), ((0,), (0,))), "tn": (((1,), (1,)), ((0,), (0,)))}


def _dot16(a, b, kind):
    return lax.dot_general(a, b, (_DN if a.ndim == 2 else _BDN)[kind], preferred_element_type=f32)


def _mx(a, b, kind):
    return _dot16(a.astype(MXU_DTYPE), b.astype(MXU_DTYPE), kind)


def _split2(a):
    hi = a.astype(jnp.bfloat16)
    return hi, (a - hi.astype(f32)).astype(jnp.bfloat16)


def _split3(a):
    p1 = a.astype(jnp.bfloat16)
    r = a - p1.astype(f32)
    p2 = r.astype(jnp.bfloat16)
    return p1, p2, (r - p2.astype(f32)).astype(jnp.bfloat16)


def _h3s(a2, b2, kind):
    (ah, al), (bh, bl) = a2, b2
    return _dot16(ah, bh, kind) + (_dot16(ah, bl, kind) + _dot16(al, bh, kind))


def _h3(a, b, kind):
    return _h3s(_split2(a), _split2(b), kind)


def _exact01(t01, x, kind="nn"):
    t = t01.astype(jnp.bfloat16)
    x1, x2, x3 = _split3(x)
    return _dot16(t, x1, kind) + (_dot16(t, x2, kind) + _dot16(t, x3, kind))


def _colsum_col(e):
    ones = jnp.ones(e.shape[:-1] + (LANES,), jnp.bfloat16)
    e1, e2, e3 = _split3(e)
    return (_dot16(e1, ones, "tn") + (_dot16(e2, ones, "tn") + _dot16(e3, ones, "tn")))[..., 0:1]


def _iota3(shape, axis):
    return lax.broadcasted_iota(jnp.int32, shape, axis)


def _dir_masks(B, nf, Cn):
    shp = (B, Cn, Cn)
    bb, ii, jj = _iota3(shp, 0), _iota3(shp, 1), _iota3(shp, 2)
    fwd = bb < nf
    rev = jnp.logical_not(fwd)
    incl = jnp.logical_or(jnp.logical_and(fwd, ii >= jj), jnp.logical_and(rev, ii <= jj))
    strict = jnp.logical_and(incl, ii != jj)
    return incl, strict, ii == jj


def _last_row(x, nf):
    B, Cn = x.shape[0], x.shape[1]
    fwd = _iota3((B, 1, 1), 0) < nf
    return jnp.where(fwd, x[:, Cn - 1 : Cn, :], x[:, 0:1, :])


def _add_at_last_row(x, val, nf):
    B, Cn = x.shape[0], x.shape[1]
    bb, rr = _iota3((B, Cn, 1), 0), _iota3((B, Cn, 1), 1)
    at = jnp.logical_or(jnp.logical_and(bb < nf, rr == Cn - 1), jnp.logical_and(bb >= nf, rr == 0))
    return x + jnp.where(at, val, 0.0)


def _sigmoid(x):
    return jax.nn.sigmoid(x)


def _silu(x):
    return x * _sigmoid(x)


def _silu_grad(x):
    s = _sigmoid(x)
    return s * (1.0 + x * (1.0 - s))


def _softplus(x):
    u = jnp.exp(-jnp.abs(x))
    l1p = jnp.where(u < 1e-3, u * (1.0 - u * (0.5 - u * (1.0 / 3.0))), jnp.log(1.0 + u))
    return jnp.maximum(x, 0.0) + l1p


def _mxr(x):
    return x.astype(MXU_DTYPE).astype(f32)


def _iota2(shape, axis):
    return lax.broadcasted_iota(jnp.int32, shape, axis)


def _matmul(a, b, kind, out_dtype, name, tm, tn, tk, out_t=False, after=None):
    extra = [] if after is None else [after]
    if kind == "nn":
        (M, K), N = a.shape, b.shape[1]
    elif kind == "nt":
        (M, K), N = a.shape, b.shape[0]
    else:
        (K, M), N = a.shape, b.shape[1]
    tm, tn, tk = min(tm, M), min(tn, N), min(tk, K)
    assert M % tm == 0 and N % tn == 0 and K % tk == 0, (name, M, N, K)
    nk = K // tk
    if kind == "tn":
        a_spec = pl.BlockSpec((tk, tm), lambda i, j, k: (k, i))
    else:
        a_spec = pl.BlockSpec((tm, tk), lambda i, j, k: (i, k))
    b_mode = dict(pipeline_mode=pl.Buffered(1)) if (tn == N and tk == K) else {}
    if kind == "nt":
        b_spec = pl.BlockSpec((tn, tk), lambda i, j, k: (j, k), **b_mode)
    else:
        b_spec = pl.BlockSpec((tk, tn), lambda i, j, k: (k, j), **b_mode)

    def body(a_ref, b_ref, *rest):
        o_ref, acc = rest[len(extra)], rest[len(extra) + 1 :]
        p = _mx(a_ref[...], b_ref[...], kind)
        if nk == 1:
            o_ref[...] = (p.T if out_t else p).astype(out_dtype)
        else:
            assert not out_t
            acc_ref = acc[0]
            k = pl.program_id(2)

            @pl.when(k == 0)
            def _():
                acc_ref[...] = p

            @pl.when(k > 0)
            def _():
                acc_ref[...] += p

            @pl.when(k == nk - 1)
            def _():
                o_ref[...] = acc_ref[...].astype(out_dtype)

    return pl.pallas_call(
        body,
        out_shape=jax.ShapeDtypeStruct((N, M) if out_t else (M, N), out_dtype),
        grid=(M // tm, N // tn, nk),
        in_specs=[a_spec, b_spec] + [pl.BlockSpec((8, LANES), lambda i, j, k: (0, 0))] * len(extra),
        out_specs=pl.BlockSpec((tn, tm), lambda i, j, k: (j, i)) if out_t else pl.BlockSpec((tm, tn), lambda i, j, k: (i, j)),
        scratch_shapes=[] if nk == 1 else [pltpu.VMEM((tm, tn), f32)],
        compiler_params=_cparams(("parallel", "parallel", "arbitrary")),
        name=name,
    )(a, b, *extra)


ROW_TILE = 512


def _rms_pre(x, w):
    T = x.shape[0]
    tb = min(ROW_TILE, T)

    def body(x_ref, w_ref, h_ref, ht_ref):
        xv = x_ref[...]
        r = lax.rsqrt(jnp.mean(xv * xv, axis=1, keepdims=True) + NORM_EPS)
        h = xv * r * w_ref[...]
        h_ref[...] = h.astype(h_ref.dtype)
        ht_ref[...] = h.T.astype(ht_ref.dtype)

    return pl.pallas_call(
        body,
        out_shape=(jax.ShapeDtypeStruct((T, D_MODEL), MXU_DTYPE), jax.ShapeDtypeStruct((D_MODEL, T), MXU_DTYPE)),
        grid=(T // tb,),
        in_specs=[pl.BlockSpec((tb, D_MODEL), lambda i: (i, 0)), pl.BlockSpec((1, D_MODEL), lambda i: (0, 0))],
        out_specs=(pl.BlockSpec((tb, D_MODEL), lambda i: (i, 0)), pl.BlockSpec((D_MODEL, tb), lambda i: (0, i))),
        compiler_params=_cparams(("parallel",)),
        name="rms_pre",
    )(x, w)


def _rms_pre_bwd(x, w, dh, dy):
    T = x.shape[0]
    tb = min(ROW_TILE, T)

    def body(x_ref, w_ref, dh_ref, dy_ref, dx_ref, dw_ref):
        i = pl.program_id(0)
        xv, dhv = x_ref[...], dh_ref[...]
        r = lax.rsqrt(jnp.mean(xv * xv, axis=1, keepdims=True) + NORM_EPS)
        dhw = dhv * w_ref[...]
        dx_ref[...] = dy_ref[...] + r * dhw - xv * (r * r * r) * jnp.mean(dhw * xv, axis=1, keepdims=True)
        part = jnp.sum(dhv * xv * r, axis=0, keepdims=True)

        @pl.when(i == 0)
        def _():
            dw_ref[...] = part

        @pl.when(i > 0)
        def _():
            dw_ref[...] += part

    blk = pl.BlockSpec((tb, D_MODEL), lambda i: (i, 0))
    row = pl.BlockSpec((1, D_MODEL), lambda i: (0, 0))
    return pl.pallas_call(
        body,
        out_shape=(jax.ShapeDtypeStruct((T, D_MODEL), f32), jax.ShapeDtypeStruct((1, D_MODEL), f32)),
        grid=(T // tb,),
        in_specs=[blk, row, blk, blk],
        out_specs=(blk, row),
        compiler_params=_cparams(("arbitrary",)),
        name="rms_pre_bwd",
    )(x, w, dh, dy)


SHIFTS = (-2, -1, 1, 2)


SHIFT_PAD = 8
PREP_HEADS, PREP_HEADS_BWD = 4, 1


def _shifted(x, pad_ref):
    T, cb = x.shape
    pad_ref[0:SHIFT_PAD] = jnp.zeros((SHIFT_PAD, cb), f32)
    pad_ref[SHIFT_PAD + T :] = jnp.zeros((SHIFT_PAD, cb), f32)
    pad_ref[SHIFT_PAD : SHIFT_PAD + T] = x
    return {d: pad_ref[SHIFT_PAD + d : SHIFT_PAD + d + T] for d in SHIFTS}


def _conv5(u, ush, cw):
    acc = u * cw[2:3, :]
    for j in (0, 1, 3, 4):
        acc = acc + ush[j - 2] * cw[j : j + 1, :]
    return acc


def _gdn_prep(proj, conv_w):
    T = proj.shape[0]
    hps, W = PREP_HEADS, PREP_HEADS * GDN_DIM
    per = GDN_HEADS // hps

    def body(u_ref, cw_ref, o_ref, pad_ref):
        j = pl.program_id(0)
        for i in range(hps):
            sl = slice(i * GDN_DIM, (i + 1) * GDN_DIM)
            u = _mxr(u_ref[:, sl])
            s = _silu(_conv5(u, _shifted(u, pad_ref), _mxr(cw_ref[:, sl])))
            o_ref[0, :, sl] = jnp.where(j < 2 * per, s * lax.rsqrt(jnp.sum(s * s, axis=1, keepdims=True) + NORM_EPS), s)

    return pl.pallas_call(
        body,
        out_shape=jax.ShapeDtypeStruct((3, T, GDN_HEADS * GDN_DIM), f32),
        grid=(3 * per,),
        in_specs=[pl.BlockSpec((T, W), lambda j: (0, j)), pl.BlockSpec((CONV_K, W), lambda j: (0, j))],
        out_specs=pl.BlockSpec((1, T, W), lambda j: (j // per, 0, j % per)),
        scratch_shapes=[pltpu.VMEM((T + 2 * SHIFT_PAD, GDN_DIM), f32)],
        compiler_params=_cparams(("parallel",)),
        name="gdn_prep",
    )(proj, conv_w)


def _gdn_prep_bwd(proj, conv_w, dqkv_f, dqkv_b, dproj):
    T = proj.shape[0]
    hps, W = PREP_HEADS_BWD, PREP_HEADS_BWD * GDN_DIM
    per = GDN_HEADS // hps

    def body(u_ref, cw_ref, df_ref, db_ref, _, du_ref, dcw_ref, upad_ref, dcpad_ref):
        j = pl.program_id(0)
        for i in range(hps):
            sl = slice(i * GDN_DIM, (i + 1) * GDN_DIM)
            u, cw = _mxr(u_ref[:, sl]), _mxr(cw_ref[:, sl])
            ush = _shifted(u, upad_ref)
            c = _conv5(u, ush, cw)
            s = _silu(c)
            dn = df_ref[0, :, sl] + db_ref[0, :, sl]
            rinv = lax.rsqrt(jnp.sum(s * s, axis=1, keepdims=True) + NORM_EPS)
            ds_norm = rinv * dn - s * (rinv * rinv * rinv) * jnp.sum(dn * s, axis=1, keepdims=True)
            ds = jnp.where(j < 2 * per, ds_norm, dn)
            dc = _mxr(ds * _silu_grad(c))
            dcsh = _shifted(dc, dcpad_ref)
            du = dc * cw[2:3, :]
            for jj in range(CONV_K):
                d = jj - 2
                dcw_ref[jj : jj + 1, sl] = jnp.sum(dc * (u if d == 0 else ush[d]), axis=0, keepdims=True)
                if d != 0:
                    du = du + dcsh[-d] * cw[jj : jj + 1, :]
            du_ref[:, sl] = du.astype(du_ref.dtype)

    qspec = pl.BlockSpec((1, T, W), lambda j: (j // per, 0, j % per))
    return pl.pallas_call(
        body,
        out_shape=(jax.ShapeDtypeStruct(dproj.shape, dproj.dtype), jax.ShapeDtypeStruct((CONV_K, 3 * GDN_HEADS * GDN_DIM), f32)),
        grid=(3 * per,),
        in_specs=[pl.BlockSpec((T, W), lambda j: (0, j)), pl.BlockSpec((CONV_K, W), lambda j: (0, j)), qspec, qspec, _ANY_SPEC],
        out_specs=(pl.BlockSpec((T, W), lambda j: (0, j)), pl.BlockSpec((CONV_K, W), lambda j: (0, j))),
        input_output_aliases={4: 0},
        scratch_shapes=[pltpu.VMEM((T + 2 * SHIFT_PAD, GDN_DIM), f32)] * 2,
        compiler_params=_cparams(("parallel",)),
        name="gdn_prep_bwd",
    )(proj, conv_w, dqkv_f, dqkv_b, dproj)


def _chunk_tri(n, rev):
    i, j = _iota2((n, n), 0), _iota2((n, n), 1)
    same = jnp.right_shift(i, 6) == jnp.right_shift(j, 6)
    order = (j >= i) if rev else (j <= i)
    return jnp.where(jnp.logical_and(same, order), 1.0, 0.0).astype(f32)


def _gate_prep(proj, avec, dvec, w2f, w2b, b2f, b2b, tb=256):
    T = proj.shape[0]
    W = GLA_HEADS * GLA_DK

    def body(s_ref, a_ref, d_ref, wf_ref, wb_ref, bf_ref, bb_ref, g_ref, gc_ref, gf_ref, gb_ref):
        small = s_ref[...]
        tri_f, tri_r = _chunk_tri(tb, False), _chunk_tri(tb, True)
        lane = _iota2(small.shape, 1)
        lg = -jnp.exp(a_ref[...]) * _softplus(small + d_ref[...])
        beta = _sigmoid(small)
        g_ref[...] = jnp.where(lane < 16, lg, jnp.where(lane < 32, beta, 0.0))
        lgm = jnp.where(lane < 16, lg, 0.0)
        gc_ref[...] = jnp.where(lane < 8, _exact01(tri_f, lgm), _exact01(tri_r, lgm))
        gkf = -_softplus(-(_mx(small, wf_ref[...], "nn") + bf_ref[...])) * (1.0 / GATE_NORMALIZER)
        gkb = -_softplus(-(_mx(small, wb_ref[...], "nn") + bb_ref[...])) * (1.0 / GATE_NORMALIZER)
        gf_ref[...] = _exact01(tri_f, gkf)
        gb_ref[...] = _exact01(tri_r, gkb)

    blk = pl.BlockSpec((tb, LANES), lambda i: (i, 0))
    row = pl.BlockSpec((1, LANES), lambda i: (0, 0))
    wide = pl.BlockSpec((tb, W), lambda i: (i, 0))
    wsp = pl.BlockSpec((LANES, W), lambda i: (0, 0))
    wrow = pl.BlockSpec((1, W), lambda i: (0, 0))
    return pl.pallas_call(
        body,
        out_shape=(jax.ShapeDtypeStruct((T, LANES), f32), jax.ShapeDtypeStruct((T, LANES), f32),
                   jax.ShapeDtypeStruct((T, W), f32), jax.ShapeDtypeStruct((T, W), f32)),
        grid=(T // tb,),
        in_specs=[pl.BlockSpec((tb, LANES), lambda i: (i, SMALL_OFF // LANES)), row, row, wsp, wsp, wrow, wrow],
        out_specs=(blk, blk, wide, wide),
        compiler_params=_cparams(("parallel",)),
        name="gate_prep",
    )(proj, avec, dvec, w2f, w2b, b2f, b2b)


def _gdn_gates_bwd(proj, avec, dvec, dg_f, dg_b, dsmall_gla, dproj):
    T = proj.shape[0]
    tb = min(ROW_TILE, T)
    pad = N_CAT - SMALL_OFF

    def body(s_ref, a_ref, d_ref, gf_ref, gb_ref, dl_ref, _, ds_ref, da_ref, dd_ref):
        i = pl.program_id(0)
        small = s_ref[...]
        lane = _iota2(small.shape, 1)
        dgate = gf_ref[...] + gb_ref[...]
        z = small + d_ref[...]
        nega = -jnp.exp(a_ref[...])
        dz = dgate * nega * _sigmoid(z)
        beta = _sigmoid(small)
        dsm = jnp.where(lane < 16, dz, jnp.where(lane < 32, dgate * beta * (1.0 - beta), 0.0))
        ds_ref[:, :LANES] = (dsm + dl_ref[...]).astype(ds_ref.dtype)
        ds_ref[:, LANES:] = jnp.zeros((tb, pad - LANES), ds_ref.dtype)
        lg = nega * _softplus(z)
        pa = jnp.sum(jnp.where(lane < 16, dgate * lg, 0.0), axis=0, keepdims=True)
        pd = jnp.sum(jnp.where(lane < 16, dz, 0.0), axis=0, keepdims=True)

        @pl.when(i == 0)
        def _():
            da_ref[...] = pa
            dd_ref[...] = pd

        @pl.when(i > 0)
        def _():
            da_ref[...] += pa
            dd_ref[...] += pd

    blk = pl.BlockSpec((tb, LANES), lambda i: (i, 0))
    row = pl.BlockSpec((1, LANES), lambda i: (0, 0))
    return pl.pallas_call(
        body,
        out_shape=(jax.ShapeDtypeStruct(dproj.shape, dproj.dtype), jax.ShapeDtypeStruct((1, LANES), f32), jax.ShapeDtypeStruct((1, LANES), f32)),
        grid=(T // tb,),
        in_specs=[pl.BlockSpec((tb, LANES), lambda i: (i, SMALL_OFF // LANES)), row, row, blk, blk, blk, _ANY_SPEC],
        out_specs=(pl.BlockSpec((tb, pad), lambda i: (i, DP_SMALL_BLK)), row, row),
        input_output_aliases={6: 0},
        compiler_params=_cparams(("arbitrary",)),
        name="gdn_gates_bwd",
    )(proj, avec, dvec, dg_f, dg_b, dsmall_gla, dproj)


GDN_HB = 8


def _inv_unit_lower(L, eye):
    A = eye - L
    P2 = _split2(L)
    for _ in range(3):
        P2 = _split2(_h3s(P2, P2, "nn"))
        A = A + _h3s(_split2(A), P2, "nn")
    Ph = P2[0]
    for _ in range(2):
        Ph = _dot16(Ph, Ph, "nn").astype(jnp.bfloat16)
        A = A + _dot16(A.astype(jnp.bfloat16), Ph, "nn")
    return A


def _lane_col(blk, idx):
    lane = _iota2(blk.shape, 1)
    return jnp.sum(jnp.where(lane == idx, blk, 0.0), axis=1, keepdims=True)


def _out_norm(o_f, o_b, proj, gate_blk, w, hd, name):
    T, W = o_f.shape
    tb = min(ROW_TILE, T)
    nh = W // hd

    def body(of_ref, ob_ref, z_ref, w_ref, y_ref):
        wv = w_ref[...]
        for i in range(nh):
            sl = slice(i * hd, (i + 1) * hd)
            o = of_ref[:, sl] + ob_ref[:, sl]
            r = lax.rsqrt(jnp.mean(o * o, axis=1, keepdims=True) + NORM_EPS)
            y_ref[:, sl] = (o * r * wv * _silu(z_ref[:, sl])).astype(y_ref.dtype)

    blk = pl.BlockSpec((tb, W), lambda i: (i, 0))
    return pl.pallas_call(
        body,
        out_shape=jax.ShapeDtypeStruct((T, W), MXU_DTYPE),
        grid=(T // tb,),
        in_specs=[blk, blk, pl.BlockSpec((tb, W), lambda i: (i, gate_blk)), pl.BlockSpec((1, hd), lambda i: (0, 0))],
        out_specs=blk,
        compiler_params=_cparams(("parallel",)),
        name=name,
    )(o_f, o_b, proj, w)


def _out_norm_bwd(dy, o_f, o_b, proj, gate_blk, w, hd, name, dproj):
    T, W = o_f.shape
    tb = min(ROW_TILE, T)
    nh = W // hd

    def body(dy_ref, of_ref, ob_ref, z_ref, w_ref, _, do_ref, dz_ref, dw_ref):
        i = pl.program_id(0)
        wv = w_ref[...]
        dw = jnp.zeros((1, hd), f32)
        for a in range(nh):
            sl = slice(a * hd, (a + 1) * hd)
            o = of_ref[:, sl] + ob_ref[:, sl]
            z = z_ref[:, sl]
            dyv = dy_ref[:, sl]
            r = lax.rsqrt(jnp.mean(o * o, axis=1, keepdims=True) + NORM_EPS)
            orr = o * r
            dn = dyv * _silu(z)
            dz_ref[:, sl] = (dyv * orr * wv * _silu_grad(z)).astype(dz_ref.dtype)
            dw = dw + jnp.sum(dn * orr, axis=0, keepdims=True)
            dnw = dn * wv
            do_ref[:, sl] = r * dnw - o * (r * r * r) * jnp.mean(dnw * o, axis=1, keepdims=True)

        @pl.when(i == 0)
        def _():
            dw_ref[...] = dw

        @pl.when(i > 0)
        def _():
            dw_ref[...] += dw

    blk = pl.BlockSpec((tb, W), lambda i: (i, 0))
    row = pl.BlockSpec((1, hd), lambda i: (0, 0))
    return pl.pallas_call(
        body,
        out_shape=(jax.ShapeDtypeStruct((T, W), f32), jax.ShapeDtypeStruct(dproj.shape, dproj.dtype), jax.ShapeDtypeStruct((1, hd), f32)),
        grid=(T // tb,),
        in_specs=[blk, blk, blk, pl.BlockSpec((tb, W), lambda i: (i, gate_blk)), row, _ANY_SPEC],
        out_specs=(blk, pl.BlockSpec((tb, W), lambda i: (i, gate_blk)), row),
        input_output_aliases={5: 1},
        compiler_params=_cparams(("arbitrary",)),
        name=name,
    )(dy, o_f, o_b, proj, w, dproj)


def _gla_prep_bwd(proj, w2f, w2b, b2f, b2b, dgk_f, dgk_b, dqkv, dproj):
    T = proj.shape[0]
    tb = min(ROW_TILE, T)
    W = GLA_HEADS * GLA_DK
    WV = GLA_HEADS * GLA_DV

    def body(s_ref, wf_ref, wb_ref, bf_ref, bb_ref, df_ref, db_ref, qf, qb, kf, kb, vf, vb, _,
             dp_ref, ds_ref, dwf_ref, dwb_ref, dbf_ref, dbb_ref):
        i = pl.program_id(0)
        dp_ref[:, :W] = (qf[...] + qb[...]).astype(dp_ref.dtype)
        dp_ref[:, W : 2 * W] = (kf[...] + kb[...]).astype(dp_ref.dtype)
        dp_ref[:, 2 * W :] = (vf[...] + vb[...]).astype(dp_ref.dtype)
        small = s_ref[...]
        dsm = jnp.zeros((tb, LANES), f32)
        parts = []
        for w_ref, b_ref, d_ref in ((wf_ref, bf_ref, df_ref), (wb_ref, bb_ref, db_ref)):
            pre = _mx(small, w_ref[...], "nn") + b_ref[...]
            dpre = d_ref[...] * (1.0 / GATE_NORMALIZER) * _sigmoid(-pre)
            dsm = dsm + _mx(dpre, w_ref[...], "nt")
            parts.append((_mx(small, dpre, "tn"), jnp.sum(dpre, axis=0, keepdims=True)))
        ds_ref[...] = dsm

        @pl.when(i == 0)
        def _():
            dwf_ref[...], dbf_ref[...] = parts[0]
            dwb_ref[...], dbb_ref[...] = parts[1]

        @pl.when(i > 0)
        def _():
            dwf_ref[...] += parts[0][0]
            dbf_ref[...] += parts[0][1]
            dwb_ref[...] += parts[1][0]
            dbb_ref[...] += parts[1][1]

    blk = pl.BlockSpec((tb, W), lambda i: (i, 0))
    vblk = pl.BlockSpec((tb, WV), lambda i: (i, 0))
    wsp = pl.BlockSpec((LANES, W), lambda i: (0, 0))
    row = pl.BlockSpec((1, W), lambda i: (0, 0))
    return pl.pallas_call(
        body,
        out_shape=(jax.ShapeDtypeStruct(dproj.shape, dproj.dtype), jax.ShapeDtypeStruct((T, LANES), f32), jax.ShapeDtypeStruct((LANES, W), f32),
                   jax.ShapeDtypeStruct((LANES, W), f32), jax.ShapeDtypeStruct((1, W), f32), jax.ShapeDtypeStruct((1, W), f32)),
        grid=(T // tb,),
        in_specs=[pl.BlockSpec((tb, LANES), lambda i: (i, SMALL_OFF // LANES)), wsp, wsp, row, row, blk, blk,
                  blk, blk, blk, blk, vblk, vblk, _ANY_SPEC],
        out_specs=(pl.BlockSpec((tb, 2 * W + WV), lambda i: (i, DP_GLA_QKV_BLK)), pl.BlockSpec((tb, LANES), lambda i: (i, 0)), wsp, wsp, row, row),
        input_output_aliases={13: 0},
        compiler_params=_cparams(("arbitrary",)),
        name="gla_prep_bwd",
    )(proj, w2f, w2b, b2f, b2b, dgk_f, dgk_b, *dqkv, dproj)


GATE_A_BLK, GATE_B_BLK, Z_BLK, G_B_BLK = 6144 // 1024, 7168 // 1024, 3072 // 1024, 8192 // 1024
DP_GATES_BLK, DP_GLA_QKV_BLK, DP_SMALL_BLK = 6144 // 2048, 4096 // 2048, SMALL_OFF // 512
_ANY_SPEC = pl.BlockSpec(memory_space=pl.ANY)


def _mid(og, obg, proj, x, target, wpg, wpl, wout, w_post, tb=256):
    T = x.shape[0]
    Dm = D_MODEL

    def body(og_ref, obg_ref, ga_ref, gb_ref, x_ref, t_ref, wpg_ref, wpl_ref, wo_ref, wp_ref,
             loss_ref, dwp_ref, dy_ref, m_ref, dout_ref, dya_ref, dyb_ref, dgates_ref, dog_ref, dobg_ref):
        i = pl.program_id(0)
        ya = _mx(og_ref[...], wpg_ref[...], "nn")
        yb = _mx(obg_ref[...], wpl_ref[...], "nn")
        sa, sb = _sigmoid(ga_ref[...]), _sigmoid(gb_ref[...])
        m = sa * ya + sb * yb
        m_ref[...] = m.astype(m_ref.dtype)
        out = _mx(m, wo_ref[...], "nn")
        r = lax.rsqrt(jnp.mean(out * out, axis=1, keepdims=True) + NORM_EPS)
        wp = wp_ref[...]
        e = x_ref[...] + out * r * wp - t_ref[...]
        dy = e * (1.0 / Dm)
        dy_ref[...] = dy
        lpart = jnp.sum(e * e, axis=0, keepdims=True)
        wpart = jnp.sum(dy * out * r, axis=0, keepdims=True)
        dyn = dy * wp
        dout = r * dyn - out * (r * r * r) * jnp.mean(dyn * out, axis=1, keepdims=True)
        dout_ref[...] = dout.astype(dout_ref.dtype)
        dm = _mx(dout, wo_ref[...], "nt")
        dya, dyb = dm * sa, dm * sb
        dya_ref[...] = dya.astype(dya_ref.dtype)
        dyb_ref[...] = dyb.astype(dyb_ref.dtype)
        dgates_ref[:, :Dm] = (dm * ya * sa * (1.0 - sa)).astype(dgates_ref.dtype)
        dgates_ref[:, Dm:] = (dm * yb * sb * (1.0 - sb)).astype(dgates_ref.dtype)
        dog_ref[...] = _mx(dya, wpg_ref[...], "nt")
        dobg_ref[...] = _mx(dyb, wpl_ref[...], "nt")

        @pl.when(i == 0)
        def _():
            loss_ref[...] = lpart
            dwp_ref[...] = wpart

        @pl.when(i > 0)
        def _():
            loss_ref[...] += lpart
            dwp_ref[...] += wpart

    blk = pl.BlockSpec((tb, Dm), lambda i: (i, 0))
    row = pl.BlockSpec((1, Dm), lambda i: (0, 0))
    wsp = pl.BlockSpec((Dm, Dm), lambda i: (0, 0))
    act = jax.ShapeDtypeStruct((T, Dm), MXU_DTYPE)
    big = jax.ShapeDtypeStruct((T, Dm), f32)
    vec = jax.ShapeDtypeStruct((1, Dm), f32)
    return pl.pallas_call(
        body,
        out_shape=(vec, vec, big, act, act, act, act, jax.ShapeDtypeStruct((T, N_CAT), MXU_DTYPE), big, big),
        grid=(T // tb,),
        in_specs=[blk, blk, pl.BlockSpec((tb, Dm), lambda i: (i, GATE_A_BLK)), pl.BlockSpec((tb, Dm), lambda i: (i, GATE_B_BLK)),
                  blk, blk, wsp, wsp, wsp, row],
        out_specs=(row, row) + (blk,) * 5 + (pl.BlockSpec((tb, 2 * Dm), lambda i: (i, DP_GATES_BLK)), blk, blk),
        compiler_params=_cparams(("arbitrary",)),
        name="mid",
    )(og, obg, proj, proj, x, target, wpg, wpl, wout, w_post)


def _heads(ref, lead, n, width, rows=slice(None)):
    return [ref[lead + (rows, slice(i * width, (i + 1) * width))] for i in range(n)]


class _GdnBatch:
    def __init__(self, q, k, v, gcol, grow, beta, nf, A=None, sol=None):
        B, Cn = q.shape[0], q.shape[1]
        self.incl, self.strict, eye = _dir_masks(B, nf, Cn)
        self.decay = jnp.where(self.incl, jnp.exp(jnp.where(self.incl, gcol - grow, 0.0)), 0.0)
        self.k, self.v = k, v
        gl = _last_row(gcol, nf)
        self.beta = jnp.broadcast_to(beta, q.shape)
        self.eg = jnp.broadcast_to(jnp.exp(gcol), q.shape)
        self.ekd = jnp.broadcast_to(jnp.exp(gl - gcol), q.shape)
        self.qs = q * (GDN_DIM**-0.5)
        self.kb = k * self.beta
        self.L = jnp.where(self.strict, _mx(self.kb, k, "nt") * self.decay, 0.0)
        if A is None:
            A = _inv_unit_lower(self.L, jnp.where(eye, 1.0, 0.0).astype(f32))
        self.A = A
        self.A2 = _split2(A)
        self.kbg = self.kb * self.eg
        if sol is None:
            sol = _h3s(self.A2, _split2(jnp.concatenate([v * self.beta, self.kbg], axis=2)), "nn")
        self.sol = sol
        self.u = self.sol[:, :, :GDN_DIM]
        self.w = self.sol[:, :, GDN_DIM:]
        self.attn = jnp.where(self.incl, _mx(self.qs, k, "nt") * self.decay, 0.0)
        self.qd = self.qs * self.eg
        self.kd = k * self.ekd
        self.egl = jnp.exp(gl)


def _gdn_chain_inputs(qf_ref, qb_ref, gf_ref, gb_ref, cf_ref, cb_ref, b, hb):
    qkv = [jnp.stack(_heads(qf_ref, (i,), hb, GDN_DIM) + _heads(qb_ref, (i,), hb, GDN_DIM)) for i in range(3)]
    cols, rows, betas = [], [], []
    for d, (g_ref, c_ref) in enumerate(((gf_ref, cf_ref), (gb_ref, cb_ref))):
        gates, gc = g_ref[...], c_ref[...]
        for hh in range(hb):
            h = b * hb + hh
            col = _lane_col(gc, h + 8 * d)
            cols.append(col)
            rows.append(jnp.broadcast_to(col, (CHUNK, LANES)).T[0:1, :])
            betas.append(_lane_col(gates, 16 + h + 8 * d))
    return qkv[0], qkv[1], qkv[2], jnp.stack(cols), jnp.stack(rows), jnp.stack(betas)


GDN_CPS_FWD, GDN_CPS_BWD = 2, 1


def _chunk_views(refs, s_f, s_b):
    out = []
    for i, r in enumerate(refs):
        s = s_f if i % 2 == 0 else s_b
        rows = pl.ds(s * CHUNK, CHUNK)
        out.append(r.at[pl.ds(s, 1)] if len(r.shape) == 4 else (r.at[:, rows, :] if len(r.shape) == 3 else r.at[rows, :]))
    return out


def _gdn_specs(cps, hb):
    R, W = cps * CHUNK, hb * GDN_DIM
    qsp = lambda ci: pl.BlockSpec((3, R, W), lambda n, b: (0, ci(n), b))
    gsp = lambda ci: pl.BlockSpec((R, LANES), lambda n, b: (ci(n), 0))
    osp = lambda ci: pl.BlockSpec((R, W), lambda n, b: (ci(n), b))
    ssp = lambda ci: pl.BlockSpec((cps, hb, GDN_DIM, GDN_DIM), lambda n, b: (ci(n), b, 0, 0))
    asp = lambda ci: pl.BlockSpec((cps, hb, CHUNK, CHUNK), lambda n, b: (ci(n), b, 0, 0))
    usp = lambda ci: pl.BlockSpec((cps, hb, CHUNK, 2 * GDN_DIM), lambda n, b: (ci(n), b, 0, 0))
    return qsp, gsp, osp, ssp, asp, usp


def _gdn_fwd(qkvn, gates, gc):
    T = qkvn.shape[1]
    N = T // CHUNK
    nh, hb = GDN_HEADS, GDN_HB
    cps = min(GDN_CPS_FWD, N)
    NB = N // cps

    def body(*refs):
        n, b = pl.program_id(0), pl.program_id(1)
        s_scr = refs[-1]

        @pl.when(n == 0)
        def _():
            s_scr[0, pl.ds(b * hb, hb)] = jnp.zeros((hb, GDN_DIM, GDN_DIM), f32)
            s_scr[1, pl.ds(b * hb, hb)] = jnp.zeros((hb, GDN_DIM, GDN_DIM), f32)

        for s in range(cps):
            chunk(*_chunk_views(refs[:-1], s, cps - 1 - s), s_scr, b)

    def chunk(qf_ref, qb_ref, gf_ref, gb_ref, cf_ref, cb_ref, of_ref, ob_ref, sf_ref, sb_ref, af_ref, ab_ref, uf_ref, ub_ref, s_scr, b):
        hs = pl.ds(b * hb, hb)
        q, k, v, gcol, grow, beta = _gdn_chain_inputs(qf_ref, qb_ref, gf_ref, gb_ref, cf_ref, cb_ref, b, hb)
        ck = _GdnBatch(q, k, v, gcol, grow, beta, hb)
        S = jnp.concatenate([s_scr[0, hs], s_scr[1, hs]], axis=0)
        sf_ref[0], sb_ref[0] = S[:hb], S[hb:]
        af_ref[0], ab_ref[0] = ck.A[:hb], ck.A[hb:]
        uf_ref[0], ub_ref[0] = ck.sol[:hb], ck.sol[hb:]
        v_new = ck.u - _mx(ck.w, S, "nn")
        o = _mx(ck.qd, S, "nn") + _mx(ck.attn, v_new, "nn")
        S_new = S * ck.egl + _mx(ck.kd, v_new, "tn")
        for hh in range(hb):
            sl = slice(hh * GDN_DIM, (hh + 1) * GDN_DIM)
            of_ref[:, sl] = o[hh]
            ob_ref[:, sl] = o[hb + hh]
        s_scr[0, hs] = S_new[:hb]
        s_scr[1, hs] = S_new[hb:]

    fw, bw = (lambda n: n), (lambda n: NB - 1 - n)
    qsp, gsp, osp, ssp, asp, usp = _gdn_specs(cps, hb)
    o_shape = jax.ShapeDtypeStruct((T, nh * GDN_DIM), f32)
    s_shape = jax.ShapeDtypeStruct((N, nh, GDN_DIM, GDN_DIM), f32)
    a_shape = jax.ShapeDtypeStruct((N, nh, CHUNK, CHUNK), f32)
    u_shape = jax.ShapeDtypeStruct((N, nh, CHUNK, 2 * GDN_DIM), f32)
    return pl.pallas_call(
        body,
        out_shape=(o_shape, o_shape, s_shape, s_shape, a_shape, a_shape, u_shape, u_shape),
        grid=(NB, nh // hb),
        in_specs=[qsp(fw), qsp(bw), gsp(fw), gsp(bw), gsp(fw), gsp(bw)],
        out_specs=(osp(fw), osp(bw), ssp(fw), ssp(bw), asp(fw), asp(bw), usp(fw), usp(bw)),
        scratch_shapes=[pltpu.VMEM((2, nh, GDN_DIM, GDN_DIM), f32)],
        compiler_params=_cparams(("arbitrary", "arbitrary")),
        name="gdn_scan_fwd",
    )(qkvn, qkvn, gates, gates, gc, gc)


def _gdn_bwd(qkvn, gates, gc, st_f, st_b, a_f, a_b, u_f, u_b, do):
    T = qkvn.shape[1]
    N = T // CHUNK
    nh, hb = GDN_HEADS, GDN_HB
    nb = nh // hb
    cps = min(GDN_CPS_BWD, N)
    NB = N // cps

    def body(*refs):
        n, b = pl.program_id(0), pl.program_id(1)
        ds_scr = refs[-1]

        @pl.when(n == 0)
        def _():
            ds_scr[0, pl.ds(b * hb, hb)] = jnp.zeros((hb, GDN_DIM, GDN_DIM), f32)
            ds_scr[1, pl.ds(b * hb, hb)] = jnp.zeros((hb, GDN_DIM, GDN_DIM), f32)

        for s in range(cps):
            chunk(*_chunk_views(refs[:-1], cps - 1 - s, s), ds_scr, b)

    def chunk(qf_ref, qb_ref, gf_ref, gb_ref, cf_ref, cb_ref, sf_ref, sb_ref, af_ref, ab_ref, uf_ref, ub_ref, dof_ref, dob_ref,
              dqf_ref, dqb_ref, dgf_ref, dgb_ref, ds_scr, b):
        hs = pl.ds(b * hb, hb)

        @pl.when(b == 0)
        def _():
            dgf_ref[...] = jnp.zeros((CHUNK, LANES), f32)
            dgb_ref[...] = jnp.zeros((CHUNK, LANES), f32)

        q, k, v, gcol, grow, beta = _gdn_chain_inputs(qf_ref, qb_ref, gf_ref, gb_ref, cf_ref, cb_ref, b, hb)
        ck = _GdnBatch(q, k, v, gcol, grow, beta, hb, A=jnp.concatenate([af_ref[0], ab_ref[0]], axis=0),
                       sol=jnp.concatenate([uf_ref[0], ub_ref[0]], axis=0))
        S = jnp.concatenate([sf_ref[0], sb_ref[0]], axis=0)
        dS = jnp.concatenate([ds_scr[0, hs], ds_scr[1, hs]], axis=0)
        dov = jnp.stack(_heads(dof_ref, (), hb, GDN_DIM) + _heads(dob_ref, (), hb, GDN_DIM))
        v_new = ck.u - _mx(ck.w, S, "nn")
        d_vnew = _mx(ck.attn, dov, "tn") + _mx(ck.kd, dS, "nn")
        d_attn = jnp.where(ck.incl, _mx(dov, v_new, "nt"), 0.0)
        d_qd = _mx(dov, S, "nt")
        d_kd = _mx(v_new, dS, "nt")
        d_gl = jnp.sum(jnp.sum(dS * S, axis=2, keepdims=True), axis=1, keepdims=True) * ck.egl
        dS_new = dS * ck.egl + _mx(ck.qd, dov, "tn") - _mx(ck.w, d_vnew, "tn")
        ds_scr[0, hs] = dS_new[:hb]
        ds_scr[1, hs] = dS_new[hb:]
        d_w = -_mx(d_vnew, S, "nt")
        d_rhs = _h3s(ck.A2, _split2(jnp.concatenate([d_vnew, d_w], axis=2)), "tn")
        d_vb, d_kbg = d_rhs[:, :, :GDN_DIM], d_rhs[:, :, GDN_DIM:]
        dL = -jnp.where(ck.strict, _h3(d_rhs, ck.sol, "nt"), 0.0)
        P = dL * ck.decay
        Q = d_attn * ck.decay
        d_kb = _mx(P, k, "nn") + d_kbg * ck.eg
        d_qs = _mx(Q, k, "nn") + d_qd * ck.eg
        d_k = _mx(P, ck.kb, "tn") + _mx(Q, ck.qs, "tn") + d_kd * ck.ekd + d_kb * ck.beta
        E = dL * ck.L + d_attn * ck.attn
        kdsum = jnp.sum(d_kd * ck.kd, axis=2, keepdims=True)
        d_gc = (jnp.sum(E, axis=2, keepdims=True) - _colsum_col(E)
                + jnp.sum(d_kbg * ck.kbg, axis=2, keepdims=True)
                + jnp.sum(d_qd * ck.qd, axis=2, keepdims=True) - kdsum)
        d_gc = _add_at_last_row(d_gc, d_gl + jnp.sum(kdsum, axis=1, keepdims=True), hb)
        d_beta = jnp.sum(d_vb * v, axis=2, keepdims=True) + jnp.sum(d_kb * k, axis=2, keepdims=True)
        d_q = d_qs * (GDN_DIM**-0.5)
        d_v = d_vb * ck.beta
        lane = _iota2((CHUNK, LANES), 1)
        for d, (dq_ref, dg_ref) in enumerate(((dqf_ref, dgf_ref), (dqb_ref, dgb_ref))):
            acc = dg_ref[...]
            for hh in range(hb):
                c, h = d * hb + hh, b * hb + hh
                sl = slice(hh * GDN_DIM, (hh + 1) * GDN_DIM)
                dq_ref[0, :, sl] = d_q[c]
                dq_ref[1, :, sl] = d_k[c]
                dq_ref[2, :, sl] = d_v[c]
                acc = jnp.where(lane == h + 8 * d, d_gc[c], acc)
                acc = jnp.where(lane == 16 + h + 8 * d, d_beta[c], acc)
            dg_ref[...] = acc

        @pl.when(b == nb - 1)
        def _():
            for d, dg_ref in enumerate((dgf_ref, dgb_ref)):
                blk = dg_ref[...]
                dg_ref[...] = jnp.where(lane < 16, _exact01(_chunk_tri(CHUNK, d == 0), blk), blk)

    fw, bw = (lambda n: NB - 1 - n), (lambda n: n)
    qsp, gsp, osp, ssp, asp, usp = _gdn_specs(cps, hb)
    dq_shape = jax.ShapeDtypeStruct((3, T, nh * GDN_DIM), f32)
    dg_shape = jax.ShapeDtypeStruct((T, LANES), f32)
    return pl.pallas_call(
        body,
        out_shape=(dq_shape, dq_shape, dg_shape, dg_shape),
        grid=(NB, nb),
        in_specs=[qsp(fw), qsp(bw), gsp(fw), gsp(bw), gsp(fw), gsp(bw), ssp(fw), ssp(bw), asp(fw), asp(bw), usp(fw), usp(bw),
                  osp(fw), osp(bw)],
        out_specs=(qsp(fw), qsp(bw), gsp(fw), gsp(bw)),
        scratch_shapes=[pltpu.VMEM((2, nh, GDN_DIM, GDN_DIM), f32)],
        compiler_params=_cparams(("arbitrary", "arbitrary")),
        name="gdn_scan_bwd",
    )(qkvn, qkvn, gates, gates, gc, gc, st_f, st_b, a_f, a_b, u_f, u_b, do, do)


class _GlaBatch:
    def __init__(self, q, k, G, nf):
        B, Cn = q.shape[0], q.shape[1]
        self.incl, _, _ = _dir_masks(B, nf, Cn)
        self.eG = jnp.exp(G)
        self.enG = jnp.exp(-G)
        self.qg = q * (GLA_DK**-0.5) * self.eG
        self.kg = k * self.enG
        self.attn = jnp.where(self.incl, _mx(self.qg, self.kg, "nt"), 0.0)
        gl = _last_row(G, nf)
        self.ekd = jnp.exp(gl - G)
        self.kd = k * self.ekd
        self.egl = jnp.exp(gl)


GLA_CPS = 4


def _chunk_rows(s):
    return slice(s * CHUNK, (s + 1) * CHUNK)


def _gla_chain_inputs(refs_f, refs_b, rows_f, rows_b):
    nh = GLA_HEADS
    out = []
    for i, width in enumerate((GLA_DK, GLA_DK, GLA_DV, GLA_DK)):
        out.append(jnp.stack(_heads(refs_f[i], (), nh, width, rows_f) + _heads(refs_b[i], (), nh, width, rows_b)))
    return out


def _gla_specs(rows, ci):
    nh = GLA_HEADS
    return [
        pl.BlockSpec((rows, nh * GLA_DK), lambda n: (ci(n), 4096 // (nh * GLA_DK))),
        pl.BlockSpec((rows, nh * GLA_DK), lambda n: (ci(n), 4608 // (nh * GLA_DK))),
        pl.BlockSpec((rows, nh * GLA_DV), lambda n: (ci(n), 5120 // (nh * GLA_DV))),
        pl.BlockSpec((rows, nh * GLA_DK), lambda n: (ci(n), 0)),
    ]


def _gla_fwd(proj, G_f, G_b):
    T = proj.shape[0]
    N = T // CHUNK
    nh = GLA_HEADS
    cps = min(GLA_CPS, N)
    NB = N // cps

    def body(qf, kf, vf, gf, qb, kb, vb, gb, of_ref, ob_ref, sf_ref, sb_ref, s_scr):
        n = pl.program_id(0)

        @pl.when(n == 0)
        def _():
            s_scr[...] = jnp.zeros((2 * nh, GLA_DV, GLA_DK), f32)

        for s in range(cps):
            sb = cps - 1 - s
            q, k, v, G = _gla_chain_inputs((qf, kf, vf, gf), (qb, kb, vb, gb), _chunk_rows(s), _chunk_rows(sb))
            ck = _GlaBatch(q, k, G, nh)
            St = s_scr[...]
            sf_ref[s], sb_ref[sb] = St[:nh].astype(sf_ref.dtype), St[nh:].astype(sb_ref.dtype)
            o = _mx(ck.qg, St, "nt") + _mx(ck.attn, v, "nn")
            for i in range(nh):
                sl = slice(i * GLA_DV, (i + 1) * GLA_DV)
                of_ref[_chunk_rows(s), sl] = o[i]
                ob_ref[_chunk_rows(sb), sl] = o[nh + i]
            s_scr[...] = St * ck.egl + _mx(v, ck.kd, "tn")

    fw, bw = (lambda n: n), (lambda n: NB - 1 - n)
    osp = lambda ci: pl.BlockSpec((cps * CHUNK, nh * GLA_DV), lambda n: (ci(n), 0))
    ssp = lambda ci: pl.BlockSpec((cps, nh, GLA_DV, GLA_DK), lambda n: (ci(n), 0, 0, 0))
    o_shape = jax.ShapeDtypeStruct((T, nh * GLA_DV), f32)
    s_shape = jax.ShapeDtypeStruct((N, nh, GLA_DV, GLA_DK), MXU_DTYPE)
    return pl.pallas_call(
        body,
        out_shape=(o_shape, o_shape, s_shape, s_shape),
        grid=(NB,),
        in_specs=_gla_specs(cps * CHUNK, fw) + _gla_specs(cps * CHUNK, bw),
        out_specs=(osp(fw), osp(bw), ssp(fw), ssp(bw)),
        scratch_shapes=[pltpu.VMEM((2 * nh, GLA_DV, GLA_DK), f32)],
        compiler_params=_cparams(("arbitrary",)),
        name="gla_scan_fwd",
    )(proj, proj, proj, G_f, proj, proj, proj, G_b)


def _gla_bwd(proj, G_f, G_b, st_f, st_b, do):
    T = proj.shape[0]
    N = T // CHUNK
    nh = GLA_HEADS
    cps = min(GLA_CPS, N)
    NB = N // cps

    def body(qf, kf, vf, gf, sf_ref, dof_ref, qb, kb, vb, gb, sb_ref, dob_ref,
             dqf_ref, dkf_ref, dvf_ref, dgf_ref, dqb_ref, dkb_ref, dvb_ref, dgb_ref, ds_scr):
        n = pl.program_id(0)

        @pl.when(n == 0)
        def _():
            ds_scr[...] = jnp.zeros((2 * nh, GLA_DV, GLA_DK), f32)

        shp = (2 * nh, CHUNK, CHUNK)
        bb, ii, jj = _iota3(shp, 0), _iota3(shp, 1), _iota3(shp, 2)
        tri = jnp.where(jnp.logical_or(jnp.logical_and(bb < nh, jj >= ii), jnp.logical_and(bb >= nh, jj <= ii)), 1.0, 0.0)
        for s in range(cps):
            sf = cps - 1 - s
            rows = (_chunk_rows(sf), _chunk_rows(s))
            q, k, v, G = _gla_chain_inputs((qf, kf, vf, gf), (qb, kb, vb, gb), *rows)
            ck = _GlaBatch(q, k, G, nh)
            St = jnp.concatenate([sf_ref[sf], sb_ref[s]], axis=0).astype(f32)
            dSt = ds_scr[...]
            dov = jnp.stack(_heads(dof_ref, (), nh, GLA_DV, rows[0]) + _heads(dob_ref, (), nh, GLA_DV, rows[1]))
            d_attn = jnp.where(ck.incl, _mx(dov, v, "nt"), 0.0)
            d_qg = _mx(dov, St, "nn") + _mx(d_attn, ck.kg, "nn")
            d_kg = _mx(d_attn, ck.qg, "tn")
            d_v = _mx(ck.attn, dov, "tn") + _mx(ck.kd, dSt, "nt")
            d_kd = _mx(v, dSt, "nn")
            ds_scr[...] = dSt * ck.egl + _mx(dov, ck.qg, "tn")
            kdd = d_kd * ck.kd
            d_gl = jnp.sum(dSt * St, axis=1, keepdims=True) * ck.egl + jnp.sum(kdd, axis=1, keepdims=True)
            d_q = d_qg * ck.eG * (GLA_DK**-0.5)
            d_k = d_kg * ck.enG + d_kd * ck.ekd
            d_gk = _exact01(tri, _add_at_last_row(d_qg * ck.qg - d_kg * ck.kg - kdd, d_gl, nh))
            for d, (dq_ref, dk_ref, dv_ref, dg_ref) in enumerate(((dqf_ref, dkf_ref, dvf_ref, dgf_ref), (dqb_ref, dkb_ref, dvb_ref, dgb_ref))):
                for i in range(nh):
                    c = d * nh + i
                    ks, vs = slice(i * GLA_DK, (i + 1) * GLA_DK), slice(i * GLA_DV, (i + 1) * GLA_DV)
                    dq_ref[rows[d], ks] = d_q[c]
                    dk_ref[rows[d], ks] = d_k[c]
                    dv_ref[rows[d], vs] = d_v[c]
                    dg_ref[rows[d], ks] = d_gk[c]

    R = cps * CHUNK

    def in_specs(ci):
        return _gla_specs(R, ci) + [
            pl.BlockSpec((cps, nh, GLA_DV, GLA_DK), lambda n: (ci(n), 0, 0, 0)),
            pl.BlockSpec((R, nh * GLA_DV), lambda n: (ci(n), 0)),
        ]

    def out_specs(ci):
        ksp = pl.BlockSpec((R, nh * GLA_DK), lambda n: (ci(n), 0))
        return [ksp, ksp, pl.BlockSpec((R, nh * GLA_DV), lambda n: (ci(n), 0)), ksp]

    fw, bw = (lambda n: NB - 1 - n), (lambda n: n)
    k_shape = jax.ShapeDtypeStruct((T, nh * GLA_DK), f32)
    v_shape = jax.ShapeDtypeStruct((T, nh * GLA_DV), f32)
    return pl.pallas_call(
        body,
        out_shape=(k_shape, k_shape, v_shape, k_shape) * 2,
        grid=(NB,),
        in_specs=in_specs(fw) + in_specs(bw),
        out_specs=tuple(out_specs(fw) + out_specs(bw)),
        scratch_shapes=[pltpu.VMEM((2 * nh, GLA_DV, GLA_DK), f32)],
        compiler_params=_cparams(("arbitrary",)),
        name="gla_scan_bwd",
    )(proj, proj, proj, G_f, st_f, do, proj, proj, proj, G_b, st_b, do)


def _local_step(x, target, wcat, conv_w, avec, dvec, gdn_norm_w, w2f, w2b, b2f, b2b, gla_norm_w, proj_weights, w_pre, w_post,
                send_proj_grads=None, send_in_grads=None):
    h, h_t = _rms_pre(x, w_pre)
    proj = _matmul(h, wcat, "nt", f32, "proj_in", 2048, 512, 1024)
    qkvn = _gdn_prep(proj, conv_w)
    gates, gc, G_f, G_b = _gate_prep(proj, avec, dvec, w2f, w2b, b2f, b2b)
    oa_f, oa_b, sa_f, sa_b, inv_f, inv_b, sol_f, sol_b = _gdn_fwd(qkvn, gates, gc)
    og = _out_norm(oa_f, oa_b, proj, Z_BLK, gdn_norm_w, GDN_DIM, "gdn_out")
    ob_f, ob_b, sb_f, sb_b = _gla_fwd(proj, G_f, G_b)
    obg = _out_norm(ob_f, ob_b, proj, G_B_BLK, gla_norm_w, GLA_DV, "gla_out")
    wpg, wpl, wout = proj_weights(obg)
    (loss_row, d_wpost, dy, m, dout, dya, dyb, dproj, dog, dobg) = _mid(og, obg, proj, x, target, wpg, wpl, wout, w_post)

    d_wout = _matmul(m, dout, "tn", MXU_DTYPE, "dw_out", 1024, 512, 2048)
    d_wpg = _matmul(og, dya, "tn", MXU_DTYPE, "dw_proj_gdn", 1024, 512, 2048)
    d_wpl = _matmul(obg, dyb, "tn", MXU_DTYPE, "dw_proj_gla", 1024, 512, 2048)
    if send_proj_grads is not None:
        gla_norm_w = gla_norm_w + send_proj_grads(d_wpg, d_wpl, d_wout)

    do_b, dproj, d_gla_norm = _out_norm_bwd(dobg, ob_f, ob_b, proj, G_B_BLK, gla_norm_w, GLA_DV, "gla_out_bwd", dproj)
    dq_f, dk_f, dv_f, dgk_f, dq_b, dk_b, dv_b, dgk_b = _gla_bwd(proj, G_f, G_b, sb_f, sb_b, do_b)
    dproj, dsmall_gla, d_w2f, d_w2b, d_b2f, d_b2b = _gla_prep_bwd(proj, w2f, w2b, b2f, b2b, dgk_f, dgk_b,
                                                                 (dq_f, dq_b, dk_f, dk_b, dv_f, dv_b), dproj)

    do_a, dproj, d_gdn_norm = _out_norm_bwd(dog, oa_f, oa_b, proj, Z_BLK, gdn_norm_w, GDN_DIM, "gdn_out_bwd", dproj)
    dqkv_f, dqkv_b, dg_f, dg_b = _gdn_bwd(qkvn, gates, gc, sa_f, sa_b, inv_f, inv_b, sol_f, sol_b, do_a)
    dproj, d_conv = _gdn_prep_bwd(proj, conv_w, dqkv_f, dqkv_b, dproj)
    dproj, d_alog, d_dtb = _gdn_gates_bwd(proj, avec, dvec, dg_f, dg_b, dsmall_gla, dproj)

    d_wcat = _matmul(h_t, dproj, "nn", MXU_DTYPE, "dw_in", 1024, 512, 2048, out_t=True)
    started = None if send_in_grads is None else send_in_grads(d_wcat)
    dh = _matmul(dproj, wcat, "nn", f32, "dh", 512, 1024, N_CAT, after=started)
    grad_x, d_wpre = _rms_pre_bwd(x, w_pre, dh, dy)
    return dict(loss_row=loss_row, grad_x=grad_x, d_wcat=d_wcat, d_conv=d_conv, d_alog=d_alog, d_dtb=d_dtb,
                d_gdn_norm=d_gdn_norm, d_wpg=d_wpg, d_w2f=d_w2f, d_w2b=d_w2b, d_b2f=d_b2f, d_b2b=d_b2b,
                d_gla_norm=d_gla_norm, d_wpl=d_wpl, d_wout=d_wout, d_wpost=d_wpost, d_wpre=d_wpre)


def _to_cat(wf):
    pad = jnp.zeros((N_CAT - N_IN,) + wf.shape[1:], wf.dtype)
    return jnp.concatenate([wf[:4096], wf[4128:6176], wf[7232:9280], wf[6176:7200], wf[4096:4128], wf[7200:7232], pad], axis=0)


def _from_cat(wc):
    return jnp.concatenate([wc[:4096], wc[9216:9248], wc[4096:6144], wc[8192:9216], wc[9248:9280], wc[6144:8192]], axis=0)


def _lane_row(*pieces):
    row = jnp.concatenate(pieces, axis=1)
    return jnp.pad(row, ((0, 0), (0, LANES - row.shape[1])))


def _pad_w2(w2, row0):
    return jnp.pad(w2, ((row0, LANES - row0 - w2.shape[0]), (0, 0)))


SM_ROWS, SM_COLS = 32, 384


def _pack_small_shard(conv, w2f, w2b):
    lead = conv.shape[:-2]
    z = lambda r, c: jnp.zeros(lead + (r, c), f32)
    top = jnp.concatenate([conv, z(8 - CONV_K, SM_COLS)], axis=-2)
    mid = jnp.concatenate([w2f, w2b, z(16, SM_COLS - 128)], axis=-1)
    return jnp.concatenate([top, mid, z(SM_ROWS - 24, SM_COLS)], axis=-2)


SMALL_REGION = {
    "ln_pre_w": (0, 1, 0, 1024), "ln_post_w": (1, 1, 0, 1024), "gk_b2_fwd": (2, 1, 0, 512), "gk_b2_bwd": (2, 1, 512, 512),
    "gla_norm_w": (3, 1, 0, 256), "gdn_norm_w": (3, 1, 256, 128), "a_log_fwd": (4, 1, 0, 8), "a_log_bwd": (5, 1, 0, 8),
    "dt_bias_fwd": (6, 1, 0, 8), "dt_bias_bwd": (7, 1, 0, 8), "conv_w": (8, CONV_K, 0, SM_COLS),
    "gk_w2_fwd": (16, 16, 0, 64), "gk_w2_bwd": (16, 16, 128, 64),
}
SMALL_NAMES = tuple(SMALL_REGION)
SMALL_SHARDED = ("conv_w", "gk_w2_fwd", "gk_w2_bwd")
PACK_ROWS, PACK_COLS = 32, 1024
LOSS_AT = (13, 0)


def _region(name):
    r0, nr, l0, nl = SMALL_REGION[name]
    return slice(r0, r0 + nr), slice(l0, l0 + nl)


def _pack_small_grads(G):
    names = ("d_wpre", "d_wpost", "d_b2f", "d_b2b", "d_gla_norm", "d_gdn_norm", "d_alog", "d_dtb", "loss_row", "d_conv", "d_w2f", "d_w2b")

    def body(wpre, wpost, b2f, b2b, glan, gdnn, alog, dtb, lrow, conv, w2f, w2b, o_ref):
        o_ref[...] = jnp.zeros(o_ref.shape, f32)
        loss = 0.5 * jnp.sum(lrow[...], axis=1, keepdims=True) * (1.0 / D_MODEL)
        whole = {"ln_pre_w": wpre, "ln_post_w": wpost, "gk_b2_fwd": b2f, "gk_b2_bwd": b2b, "gla_norm_w": glan, "gdn_norm_w": gdnn}
        for j in range(N_DEV):
            for name, ref in whole.items():
                o_ref[(j,) + _region(name)] = ref[...]
            o_ref[(j,) + _region("a_log_fwd")] = alog[0:1, 0:8]
            o_ref[(j,) + _region("a_log_bwd")] = alog[0:1, 8:16]
            o_ref[(j,) + _region("dt_bias_fwd")] = dtb[0:1, 0:8]
            o_ref[(j,) + _region("dt_bias_bwd")] = dtb[0:1, 8:16]
            o_ref[(j,) + _region("conv_w")] = conv[:, j * SM_COLS : (j + 1) * SM_COLS]
            o_ref[(j,) + _region("gk_w2_fwd")] = w2f[32:48, j * 64 : (j + 1) * 64]
            o_ref[(j,) + _region("gk_w2_bwd")] = w2b[48:64, j * 64 : (j + 1) * 64]
            o_ref[j, LOSS_AT[0] : LOSS_AT[0] + 1, LOSS_AT[1] : LOSS_AT[1] + 1] = loss

    return pl.pallas_call(
        body, out_shape=jax.ShapeDtypeStruct((N_DEV, PACK_ROWS, PACK_COLS), f32), name="pack_small_grads",
    )(*[G[n] for n in names])


def _small_adam(land, own, W, M, V):
    bc1 = 1.0 - ADAM_B1**ADAM_STEP
    bc2 = 1.0 - ADAM_B2**ADAM_STEP
    n = len(SMALL_NAMES)

    def body(land_ref, own_ref, *refs):
        w_refs, m_refs, v_refs = refs[0:n], refs[n : 2 * n], refs[2 * n : 3 * n]
        outs, loss_ref, g_scr = refs[3 * n : 7 * n], refs[7 * n], refs[7 * n + 1]
        me = _my_device()
        g = jnp.where(me == 0, own_ref[0], land_ref[0])
        for j in range(1, N_DEV):
            g = g + jnp.where(me == j, own_ref[j], land_ref[j])
        g_scr[...] = g
        loss_ref[...] = g_scr[LOSS_AT[0] : LOSS_AT[0] + 1, LOSS_AT[1] : LOSS_AT[1] + 1]
        for i, name in enumerate(SMALL_NAMES):
            gp = g_scr[_region(name)]
            lead = (0,) if name in SMALL_SHARDED else (Ellipsis,)
            w, m, v = w_refs[i][lead], m_refs[i][lead], v_refs[i][lead]
            m2 = ADAM_B1 * m + (1.0 - ADAM_B1) * gp
            v2 = ADAM_B2 * v + (1.0 - ADAM_B2) * (gp * gp)
            delta = -ADAM_LR * ((m2 / bc1) / (jnp.sqrt(v2 / bc2) + ADAM_EPS) + ADAM_WD * w)
            for k, val in enumerate((gp, delta, m2, v2)):
                outs[4 * i + k][lead] = val

    shapes = [jax.ShapeDtypeStruct(W[name].shape, f32) for name in SMALL_NAMES for _ in range(4)]
    res = pl.pallas_call(
        body,
        out_shape=tuple(shapes) + (jax.ShapeDtypeStruct((1, 1), f32),),
        scratch_shapes=[pltpu.VMEM((PACK_ROWS, PACK_COLS), f32)],
        name="adam_small",
    )(land, own, *[P[name] for P in (W, M, V) for name in SMALL_NAMES])
    return {name: res[4 * i : 4 * i + 4] for i, name in enumerate(SMALL_NAMES)}, res[4 * n]


_MESH = pl.DeviceIdType.MESH


N_BIG_COPIES, N_SMALL_COPIES = 9, 7


def _gather_weights(w_in_s, sm_s):
    shapes = (w_in_s.shape, sm_s.shape)
    dtypes = (MXU_DTYPE, f32)

    def body(win_ref, sm_ref, gin_ref, gsm_ref, send_sems, recv_sems):
        x, y, c = lax.axis_index("x"), lax.axis_index("y"), lax.axis_index("c")
        me, sibling = (x, y, c), (x, y, 1 - c)
        chips = [(1 - x, y), (x, 1 - y), (1 - x, 1 - y)]

        def idx(px, py, pc):
            return 4 * px + 2 * py + pc

        for r in range(w_in_s.shape[1] // LANES):
            cols = slice(r * LANES, (r + 1) * LANES)
            gin_ref[idx(*me), :, cols] = win_ref[:, cols].astype(MXU_DTYPE)
        gsm_ref[idx(*me)] = sm_ref[...]

        def copy(sem, blk, to):
            return pltpu.make_async_remote_copy(src_ref=blk, dst_ref=blk, send_sem=send_sems.at[sem], recv_sem=recv_sems.at[sem],
                                                device_id=to, device_id_type=_MESH)

        def small(k, block, to):
            return copy(N_BIG_COPIES + k, gsm_ref.at[idx(*block)], to)

        half = w_in_s.shape[1] // 2
        xn, yn, dg = (1 - x, y), (x, 1 - y), (1 - x, 1 - y)

        def big(sem, block, to, part=None):
            blk = gin_ref.at[idx(*block)]
            return copy(sem, blk if part is None else blk.at[:, pl.ds(part * half, half)], to)

        started = [big(0, me, sibling), big(1, me, (*xn, c)), big(2, me, (*yn, c)), small(0, me, sibling)]
        started += [small(1 + j, me, (*chip, c)) for j, chip in enumerate(chips)]
        for cp in started:
            cp.start()

        def then(arrived, forwards):
            arrived.wait_recv()
            for cp in forwards:
                cp.start()
            started.extend(forwards)

        then(big(1, (*xn, c), me), [big(3, (*xn, c), (*yn, c), 1), big(5, (*xn, c), sibling)])
        then(big(2, (*yn, c), me), [big(4, (*yn, c), (*xn, c), 0), big(6, (*yn, c), sibling)])
        then(big(4, (*dg, c), me, 0), [big(7, (*dg, c), sibling, 0)])
        then(big(3, (*dg, c), me, 1), [big(8, (*dg, c), sibling, 1)])
        for j, chip in enumerate(chips):
            then(small(1 + j, (*chip, c), me), [small(4 + j, (*chip, c), sibling)])
        big(0, sibling, me).wait_recv()
        big(5, (*xn, 1 - c), me).wait_recv()
        big(6, (*yn, 1 - c), me).wait_recv()
        big(7, (*dg, 1 - c), me, 0).wait_recv()
        big(8, (*dg, 1 - c), me, 1).wait_recv()
        small(0, sibling, me).wait_recv()
        for j, chip in enumerate(chips):
            small(4 + j, (*chip, 1 - c), me).wait_recv()
        for cp in started:
            cp.wait_send()

    vm = pl.BlockSpec(memory_space=pltpu.VMEM)
    return pl.pallas_call(
        body,
        out_shape=tuple(jax.ShapeDtypeStruct((N_DEV,) + s, d) for s, d in zip(shapes, dtypes)),
        in_specs=[vm, vm],
        out_specs=(vm, vm),
        scratch_shapes=[pltpu.SemaphoreType.DMA((N_BIG_COPIES + N_SMALL_COPIES,)), pltpu.SemaphoreType.DMA((N_BIG_COPIES + N_SMALL_COPIES,))],
        compiler_params=pltpu.CompilerParams(vmem_limit_bytes=VMEM_LIMIT),
        name="gather_weights",
    )(w_in_s, sm_s)


N_CHIP = 4
_EFFECT = pltpu.SideEffectType.DATAFLOW_SIDE_EFFECTING
_HBM_SPEC = pl.BlockSpec(memory_space=pltpu.HBM)
_SEM_SPEC = pl.BlockSpec(memory_space=pltpu.SEMAPHORE)


def _relation_peer(k, x, y, c):
    px = (1 - x) if (k & 4) else x
    py = (1 - y) if (k & 2) else y
    pc = (1 - c) if (k & 1) else c
    return (px, py, pc), 4 * px + 2 * py + pc


_PLAN_COPIES = {"gather": N_DEV - 1, "blocks": N_DEV - 1, "sibling": 4, "chips": 3}
_PLAN_SLOTS = {"gather": N_DEV, "blocks": N_DEV, "sibling": 4, "chips": 4}


def _plan_copies(plan, src_ref, land_ref, send_sems, recv_sems, arrival):
    x, y, c = lax.axis_index("x"), lax.axis_index("y"), lax.axis_index("c")
    me, my_chip = 4 * x + 2 * y + c, 2 * x + y
    out = []

    def add(i, src, there, here, dev):
        out.append(pltpu.make_async_remote_copy(src_ref=src, dst_ref=land_ref.at[here if arrival else there], send_sem=send_sems.at[i],
                                                recv_sem=recv_sems.at[i], device_id=dev, device_id_type=_MESH))

    if plan in ("gather", "blocks"):
        for k in range(1, N_DEV):
            dev, p = _relation_peer(k, x, y, c)
            add(k - 1, src_ref.at[p] if plan == "blocks" else src_ref, me, p, dev)
    elif plan == "sibling":
        for j in range(4):
            add(j, src_ref.at[2 * j + 1 - c], j, j, (x, y, 1 - c))
    else:
        for i, k in enumerate((2, 4, 6)):
            (px, py, pc), _ = _relation_peer(k, x, y, c)
            add(i, src_ref.at[2 * px + py], my_chip, 2 * px + py, (px, py, pc))
    return out


def _exchange_start(src, plan, name, after=None):
    n = _PLAN_COPIES[plan]
    land = lax.empty((_PLAN_SLOTS[plan],) + (src.shape if plan == "gather" else src.shape[1:]), src.dtype)
    extra = [] if after is None else [after]

    def body(src_ref, land_ref, *rest):
        send_sems, recv_sems, _, _, token = rest[len(extra) :]
        for cp in _plan_copies(plan, src_ref, land_ref, send_sems, recv_sems, False):
            cp.start()
        token[...] = jnp.zeros_like(token)

    return pl.pallas_call(
        body,
        name=name,
        out_shape=(pltpu.SemaphoreType.DMA((n,)), pltpu.SemaphoreType.DMA((n,)), pltpu.HBM(src.shape, src.dtype),
                   pltpu.HBM(land.shape, land.dtype), jax.ShapeDtypeStruct((8, LANES), f32)),
        in_specs=(_HBM_SPEC, _HBM_SPEC) + (_ANY_SPEC,) * len(extra),
        out_specs=(_SEM_SPEC, _SEM_SPEC, _HBM_SPEC, _HBM_SPEC, pl.BlockSpec(memory_space=pltpu.VMEM)),
        input_output_aliases={0: 2, 1: 3},
        compiler_params=pltpu.CompilerParams(has_side_effects=_EFFECT),
    )(pltpu.with_memory_space_constraint(src, pltpu.HBM), pltpu.with_memory_space_constraint(land, pltpu.HBM), *extra)


def _exchange_wait(started, after, plan, name):
    send_sems, recv_sems, src_thru, land_thru, _ = started

    def body(src_ref, land_ref, send_sems, recv_sems, after_ref, src_dead, got_ref):
        for cp in _plan_copies(plan, src_ref, land_ref, send_sems, recv_sems, True):
            cp.wait_send()
            cp.wait_recv()

    return pl.pallas_call(
        body,
        name=name,
        out_shape=(pltpu.HBM(src_thru.shape, src_thru.dtype), pltpu.HBM(land_thru.shape, land_thru.dtype)),
        in_specs=(_HBM_SPEC, _HBM_SPEC, _SEM_SPEC, _SEM_SPEC, pl.BlockSpec(memory_space=pl.ANY)),
        out_specs=(_HBM_SPEC, _HBM_SPEC),
        input_output_aliases={0: 0, 1: 1},
        compiler_params=pltpu.CompilerParams(has_side_effects=_EFFECT),
    )(src_thru, land_thru, send_sems, recv_sems, after)


def _pair_sum(g, r, name, tc=LANES):
    _, R, Cc = g.shape

    def body(g_ref, r_ref, o_ref):
        o_ref[...] = (g_ref[...].astype(f32) + r_ref[...].astype(f32)).astype(o_ref.dtype)

    return pl.pallas_call(
        body,
        out_shape=jax.ShapeDtypeStruct(r.shape, r.dtype),
        grid=(N_CHIP, Cc // tc),
        in_specs=[pl.BlockSpec((1, R, tc), lambda k, i: (2 * k + lax.axis_index("c"), 0, i)), pl.BlockSpec((1, R, tc), lambda k, i: (k, 0, i))],
        out_specs=pl.BlockSpec((1, R, tc), lambda k, i: (k, 0, i)),
        compiler_params=_cparams(("parallel", "parallel")),
        name=name,
    )(g, r)


def _sum_adam(parts, w, m, v, name, tb, tc=None, own=None, own_slot=None):
    R, Cc = w.shape
    n_parts = parts.shape[0]
    tb = R if tc else min(tb, R)
    tc = tc or Cc
    assert R % tb == 0 and Cc % tc == 0
    bc1 = 1.0 - ADAM_B1**ADAM_STEP
    bc2 = 1.0 - ADAM_B2**ADAM_STEP
    extra = [] if own is None else [own]

    def body(p_ref, w_ref, m_ref, v_ref, *rest):
        g_ref, d_ref, nm_ref, nv_ref = rest[len(extra) :]

        def part(j):
            pj = p_ref[j].astype(f32)
            return pj if own is None else jnp.where(own_slot() == j, rest[0][0].astype(f32), pj)

        g = part(0)
        for j in range(1, n_parts):
            g = g + part(j)
        g_ref[...] = g
        m2 = ADAM_B1 * m_ref[...] + (1.0 - ADAM_B1) * g
        v2 = ADAM_B2 * v_ref[...] + (1.0 - ADAM_B2) * (g * g)
        nm_ref[...] = m2
        nv_ref[...] = v2
        d_ref[...] = -ADAM_LR * ((m2 / bc1) / (jnp.sqrt(v2 / bc2) + ADAM_EPS) + ADAM_WD * w_ref[...])

    blk = pl.BlockSpec((tb, tc), lambda i, j: (i, j))
    o = jax.ShapeDtypeStruct((R, Cc), f32)
    return pl.pallas_call(
        body,
        out_shape=(o, o, o, o),
        grid=(R // tb, Cc // tc),
        in_specs=[pl.BlockSpec((n_parts, tb, tc), lambda i, j: (0, i, j)), blk, blk, blk]
        + [pl.BlockSpec((1, tb, tc), lambda i, j: (own_slot(), i, j))] * len(extra),
        out_specs=(blk, blk, blk, blk),
        compiler_params=_cparams(("parallel", "parallel")),
        name=name,
    )(parts, w, m, v, *extra)


def _my_device():
    return 4 * lax.axis_index("x") + 2 * lax.axis_index("y") + lax.axis_index("c")


def _my_chip():
    return 2 * lax.axis_index("x") + lax.axis_index("y")


_WEIGHTS = ["ln_pre_w", "w_in", "conv_w", "a_log_fwd", "a_log_bwd", "dt_bias_fwd", "dt_bias_bwd", "gdn_norm_w", "w_proj_gdn",
            "gk_w2_fwd", "gk_b2_fwd", "gk_w2_bwd", "gk_b2_bwd", "gla_norm_w", "w_proj_gla", "w_out", "ln_post_w"]


def kernel(x, ln_pre_w, w_in, conv_w, a_log_fwd, a_log_bwd, dt_bias_fwd, dt_bias_bwd, gdn_norm_w, w_proj_gdn, gk_w2_fwd, gk_b2_fwd, gk_w2_bwd, gk_b2_bwd, gla_norm_w, w_proj_gla, w_out, ln_post_w, loss_target, m_ln_pre_w, m_w_in, m_conv_w, m_a_log_fwd, m_a_log_bwd, m_dt_bias_fwd, m_dt_bias_bwd, m_gdn_norm_w, m_w_proj_gdn, m_gk_w2_fwd, m_gk_b2_fwd, m_gk_w2_bwd, m_gk_b2_bwd, m_gla_norm_w, m_w_proj_gla, m_w_out, m_ln_post_w, v_ln_pre_w, v_w_in, v_conv_w, v_a_log_fwd, v_a_log_bwd, v_dt_bias_fwd, v_dt_bias_bwd, v_gdn_norm_w, v_w_proj_gdn, v_gk_w2_fwd, v_gk_b2_fwd, v_gk_w2_bwd, v_gk_b2_bwd, v_gla_norm_w, v_w_proj_gla, v_w_out, v_ln_post_w):
    args = locals()
    W = {n: args[n] for n in _WEIGHTS}
    M = {n: args["m_" + n] for n in _WEIGHTS}
    V = {n: args["v_" + n] for n in _WEIGHTS}

    wp_stack = lambda P: jnp.concatenate([P["w_proj_gdn"], P["w_proj_gla"], P["w_out"]], axis=0)
    sm_pack = lambda P: _pack_small_shard(P["conv_w"][0], P["gk_w2_fwd"][0], P["gk_w2_bwd"][0])
    me = 4 * lax.axis_index("x") + 2 * lax.axis_index("y") + lax.axis_index("c")
    w_in_t, m_in_t, v_in_t = (jnp.transpose(a[0]) for a in (w_in, m_w_in, v_w_in))
    g_in, g_sm = _gather_weights(w_in_t, sm_pack(W))
    wp_own = wp_stack(W).astype(MXU_DTYPE)
    wp_started = _exchange_start(wp_own, "gather", "gather_proj_start", after=g_sm)

    def proj_weights(after):
        own, land = _exchange_wait(wp_started, after, "gather", "gather_proj_wait")
        g_p = lax.dynamic_update_slice(land, own[None], (me, 0, 0, 0))
        return tuple(g_p[:, i].reshape(D_MODEL, D_MODEL) for i in range(3))

    rows_p = 3 * D_MODEL // N_DEV
    sent = {}

    def send_proj_grads(d_wpg, d_wpl, d_wout):
        p_p = jnp.stack([d_wpg, d_wpl, d_wout]).reshape(3, N_DEV, D_MODEL // N_DEV, D_MODEL).transpose(1, 0, 2, 3)
        sent["p_p"] = p_p.reshape(N_DEV, rows_p, D_MODEL)
        sent["started"] = _exchange_start(sent["p_p"], "blocks", "exchange_proj_start")
        return sent["started"][4][0:1, 0:1]

    def send_in_grads(d_wcat):
        p_in = _from_cat(d_wcat).reshape(N_DEV, SHARD_IN, D_MODEL)
        sib = _exchange_start(p_in, "sibling", "exchange_in_sibling_start")
        p_p, land_p = _exchange_wait(sent["started"], sib[4], "blocks", "exchange_proj_wait")
        sent["o_p"] = _sum_adam(land_p, wp_stack(W).reshape(rows_p, D_MODEL), wp_stack(M).reshape(rows_p, D_MODEL),
                                wp_stack(V).reshape(rows_p, D_MODEL), "adam_w_proj", 128, own=p_p, own_slot=_my_device)
        p_in, sib_land = _exchange_wait(sib, sent["o_p"][0], "sibling", "exchange_in_sibling_wait")
        sent["chips"] = _exchange_start(_pair_sum(p_in, sib_land, "pair_sum_w_in", tc=512), "chips", "exchange_in_chips_start")
        return sent["chips"][4]

    wcat = _to_cat(g_in.reshape(N_IN, D_MODEL))
    conv_full = g_sm[:, 0:CONV_K, :].transpose(1, 0, 2).reshape(CONV_K, N_DEV * SM_COLS)
    w2f_full = g_sm[:, 8:24, 0:64].transpose(1, 0, 2).reshape(16, 512)
    w2b_full = g_sm[:, 8:24, 64:128].transpose(1, 0, 2).reshape(16, 512)

    avec = _lane_row(a_log_fwd, a_log_bwd)
    dvec = _lane_row(dt_bias_fwd, dt_bias_bwd)
    G = _local_step(x[0], loss_target[0], wcat, conv_full, avec, dvec, gdn_norm_w, _pad_w2(w2f_full, 32), _pad_w2(w2b_full, 48),
                    gk_b2_fwd, gk_b2_bwd, gla_norm_w, proj_weights, ln_pre_w + wp_started[4][0:1, 0:1], ln_post_w, send_proj_grads,
                    send_in_grads)

    small_started = _exchange_start(_pack_small_grads(G), "blocks", "exchange_small_start")
    q_in, land_in = _exchange_wait(sent["chips"], small_started[4], "chips", "exchange_in_chips_wait")
    o_in = _sum_adam(land_in, w_in_t, m_in_t, v_in_t, "adam_w_in", SHARD_IN, tc=256, own=q_in, own_slot=_my_chip)
    o_p = sent["o_p"]
    p_small, land_small = _exchange_wait(small_started, o_in[0], "blocks", "exchange_small_wait")
    o_small, loss = _small_adam(land_small, p_small, W, M, V)

    res = []
    for kind in range(4):
        per = {name: o_small[name][kind] for name in SMALL_NAMES}
        per["w_in"] = jnp.transpose(o_in[kind])[None]
        pp = o_p[kind].reshape(3, 1, D_MODEL // N_DEV, D_MODEL)
        per["w_proj_gdn"], per["w_proj_gla"], per["w_out"] = pp[0], pp[1], pp[2]
        res.append([per[n] for n in _WEIGHTS])
    return (loss[0, 0], G["grad_x"][None], *res[0], *res[1], *res[2], *res[3])
```

```python
import jax
import jax.numpy as jnp
from jax import lax
from jax.experimental import pallas as pl
from jax.experimental.pallas import tpu as pltpu

f32 = jnp.float32
MXU_DTYPE = jnp.bfloat16

D_MODEL = 1024
CHUNK = 64
NORM_EPS = 1e-6
GDN_HEADS, GDN_DIM = 8, 128
GLA_HEADS, GLA_DK, GLA_DV = 4, 128, 256
GATE_NORMALIZER = 16.0
CONV_K = 5
N_IN = 9280
N_DEV = 8
SHARD_IN = N_IN // N_DEV

N_CAT = 9728
SMALL_OFF = 9216
LANES = 128

ADAM_LR, ADAM_B1, ADAM_B2, ADAM_EPS, ADAM_WD, ADAM_STEP = 0.001, 0.9, 0.999, 1e-08, 0.01, 10

VMEM_LIMIT = 56 * 1024 * 1024


def _cparams(sem=None):
    return pltpu.CompilerParams(dimension_semantics=sem, vmem_limit_bytes=VMEM_LIMIT)


_DN = {"nn": (((1,), (0,)), ((), ())), "nt": (((1,), (1,)), ((), ())), "tn": (((0,), (0,)), ((), ()))}
_BDN = {"nn": (((2,), (1,)), ((0,), (0,))), "nt": (((2,), (2,)), ((0,), (0,))), "tn": (((1,), (1,)), ((0,), (0,)))}


def _dot16(a, b, kind):
    return lax.dot_general(a, b, (_DN if a.ndim == 2 else _BDN)[kind], preferred_element_type=f32)


def _mx(a, b, kind):
    return _dot16(a.astype(MXU_DTYPE), b.astype(MXU_DTYPE), kind)


def _split2(a):
    hi = a.astype(jnp.bfloat16)
    return hi, (a - hi.astype(f32)).astype(jnp.bfloat16)


def _split3(a):
    p1 = a.astype(jnp.bfloat16)
    r = a - p1.astype(f32)
    p2 = r.astype(jnp.bfloat16)
    return p1, p2, (r - p2.astype(f32)).astype(jnp.bfloat16)


def _h3s(a2, b2, kind):
    (ah, al), (bh, bl) = a2, b2
    return _dot16(ah, bh, kind) + (_dot16(ah, bl, kind) + _dot16(al, bh, kind))


def _h3(a, b, kind):
    return _h3s(_split2(a), _split2(b), kind)


def _exact01(t01, x, kind="nn"):
    t = t01.astype(jnp.bfloat16)
    x1, x2, x3 = _split3(x)
    return _dot16(t, x1, kind) + (_dot16(t, x2, kind) + _dot16(t, x3, kind))


def _colsum_col(e):
    ones = jnp.ones(e.shape[:-1] + (LANES,), jnp.bfloat16)
    e1, e2, e3 = _split3(e)
    return (_dot16(e1, ones, "tn") + (_dot16(e2, ones, "tn") + _dot16(e3, ones, "tn")))[..., 0:1]


def _iota3(shape, axis):
    return lax.broadcasted_iota(jnp.int32, shape, axis)


def _dir_masks(B, nf, Cn):
    shp = (B, Cn, Cn)
    bb, ii, jj = _iota3(shp, 0), _iota3(shp, 1), _iota3(shp, 2)
    fwd = bb < nf
    rev = jnp.logical_not(fwd)
    incl = jnp.logical_or(jnp.logical_and(fwd, ii >= jj), jnp.logical_and(rev, ii <= jj))
    strict = jnp.logical_and(incl, ii != jj)
    return incl, strict, ii == jj


def _last_row(x, nf):
    B, Cn = x.shape[0], x.shape[1]
    fwd = _iota3((B, 1, 1), 0) < nf
    return jnp.where(fwd, x[:, Cn - 1 : Cn, :], x[:, 0:1, :])


def _add_at_last_row(x, val, nf):
    B, Cn = x.shape[0], x.shape[1]
    bb, rr = _iota3((B, Cn, 1), 0), _iota3((B, Cn, 1), 1)
    at = jnp.logical_or(jnp.logical_and(bb < nf, rr == Cn - 1), jnp.logical_and(bb >= nf, rr == 0))
    return x + jnp.where(at, val, 0.0)


def _sigmoid(x):
    return jax.nn.sigmoid(x)


def _silu(x):
    return x * _sigmoid(x)


def _silu_grad(x):
    s = _sigmoid(x)
    return s * (1.0 + x * (1.0 - s))


def _softplus(x):
    u = jnp.exp(-jnp.abs(x))
    l1p = jnp.where(u < 1e-3, u * (1.0 - u * (0.5 - u * (1.0 / 3.0))), jnp.log(1.0 + u))
    return jnp.maximum(x, 0.0) + l1p


def _mxr(x):
    return x.astype(MXU_DTYPE).astype(f32)


def _iota2(shape, axis):
    return lax.broadcasted_iota(jnp.int32, shape, axis)


def _matmul(a, b, kind, out_dtype, name, tm, tn, tk, out_t=False, after=None):
    extra = [] if after is None else [after]
    if kind == "nn":
        (M, K), N = a.shape, b.shape[1]
    elif kind == "nt":
        (M, K), N = a.shape, b.shape[0]
    else:
        (K, M), N = a.shape, b.shape[1]
    tm, tn, tk = min(tm, M), min(tn, N), min(tk, K)
    assert M % tm == 0 and N % tn == 0 and K % tk == 0, (name, M, N, K)
    nk = K // tk
    if kind == "tn":
        a_spec = pl.BlockSpec((tk, tm), lambda i, j, k: (k, i))
    else:
        a_spec = pl.BlockSpec((tm, tk), lambda i, j, k: (i, k))
    b_mode = dict(pipeline_mode=pl.Buffered(1)) if (tn == N and tk == K) else {}
    if kind == "nt":
        b_spec = pl.BlockSpec((tn, tk), lambda i, j, k: (j, k), **b_mode)
    else:
        b_spec = pl.BlockSpec((tk, tn), lambda i, j, k: (k, j), **b_mode)

    def body(a_ref, b_ref, *rest):
        o_ref, acc = rest[len(extra)], rest[len(extra) + 1 :]
        p = _mx(a_ref[...], b_ref[...], kind)
        if nk == 1:
            o_ref[...] = (p.T if out_t else p).astype(out_dtype)
        else:
            assert not out_t
            acc_ref = acc[0]
            k = pl.program_id(2)

            @pl.when(k == 0)
            def _():
                acc_ref[...] = p

            @pl.when(k > 0)
            def _():
                acc_ref[...] += p

            @pl.when(k == nk - 1)
            def _():
                o_ref[...] = acc_ref[...].astype(out_dtype)

    return pl.pallas_call(
        body,
        out_shape=jax.ShapeDtypeStruct((N, M) if out_t else (M, N), out_dtype),
        grid=(M // tm, N // tn, nk),
        in_specs=[a_spec, b_spec] + [pl.BlockSpec((8, LANES), lambda i, j, k: (0, 0))] * len(extra),
        out_specs=pl.BlockSpec((tn, tm), lambda i, j, k: (j, i)) if out_t else pl.BlockSpec((tm, tn), lambda i, j, k: (i, j)),
        scratch_shapes=[] if nk == 1 else [pltpu.VMEM((tm, tn), f32)],
        compiler_params=_cparams(("parallel", "parallel", "arbitrary")),
        name=name,
    )(a, b, *extra)


ROW_TILE = 512


def _rms_pre(x, w):
    T = x.shape[0]
    tb = min(ROW_TILE, T)

    def body(x_ref, w_ref, h_ref, ht_ref):
        xv = x_ref[...]
        r = lax.rsqrt(jnp.mean(xv * xv, axis=1, keepdims=True) + NORM_EPS)
        h = xv * r * w_ref[...]
        h_ref[...] = h.astype(h_ref.dtype)
        ht_ref[...] = h.T.astype(ht_ref.dtype)

    return pl.pallas_call(
        body,
        out_shape=(jax.ShapeDtypeStruct((T, D_MODEL), MXU_DTYPE), jax.ShapeDtypeStruct((D_MODEL, T), MXU_DTYPE)),
        grid=(T // tb,),
        in_specs=[pl.BlockSpec((tb, D_MODEL), lambda i: (i, 0)), pl.BlockSpec((1, D_MODEL), lambda i: (0, 0))],
        out_specs=(pl.BlockSpec((tb, D_MODEL), lambda i: (i, 0)), pl.BlockSpec((D_MODEL, tb), lambda i: (0, i))),
        compiler_params=_cparams(("parallel",)),
        name="rms_pre",
    )(x, w)


def _rms_pre_bwd(x, w, dh, dy):
    T = x.shape[0]
    tb = min(ROW_TILE, T)

    def body(x_ref, w_ref, dh_ref, dy_ref, dx_ref, dw_ref):
        i = pl.program_id(0)
        xv, dhv = x_ref[...], dh_ref[...]
        r = lax.rsqrt(jnp.mean(xv * xv, axis=1, keepdims=True) + NORM_EPS)
        dhw = dhv * w_ref[...]
        dx_ref[...] = dy_ref[...] + r * dhw - xv * (r * r * r) * jnp.mean(dhw * xv, axis=1, keepdims=True)
        part = jnp.sum(dhv * xv * r, axis=0, keepdims=True)

        @pl.when(i == 0)
        def _():
            dw_ref[...] = part

        @pl.when(i > 0)
        def _():
            dw_ref[...] += part

    blk = pl.BlockSpec((tb, D_MODEL), lambda i: (i, 0))
    row = pl.BlockSpec((1, D_MODEL), lambda i: (0, 0))
    return pl.pallas_call(
        body,
        out_shape=(jax.ShapeDtypeStruct((T, D_MODEL), f32), jax.ShapeDtypeStruct((1, D_MODEL), f32)),
        grid=(T // tb,),
        in_specs=[blk, row, blk, blk],
        out_specs=(blk, row),
        compiler_params=_cparams(("arbitrary",)),
        name="rms_pre_bwd",
    )(x, w, dh, dy)


SHIFTS = (-2, -1, 1, 2)


SHIFT_PAD = 8
PREP_HEADS, PREP_HEADS_BWD = 4, 1


def _shifted(x, pad_ref):
    T, cb = x.shape
    pad_ref[0:SHIFT_PAD] = jnp.zeros((SHIFT_PAD, cb), f32)
    pad_ref[SHIFT_PAD + T :] = jnp.zeros((SHIFT_PAD, cb), f32)
    pad_ref[SHIFT_PAD : SHIFT_PAD + T] = x
    return {d: pad_ref[SHIFT_PAD + d : SHIFT_PAD + d + T] for d in SHIFTS}


def _conv5(u, ush, cw):
    acc = u * cw[2:3, :]
    for j in (0, 1, 3, 4):
        acc = acc + ush[j - 2] * cw[j : j + 1, :]
    return acc


def _gdn_prep(proj, conv_w):
    T = proj.shape[0]
    hps, W = PREP_HEADS, PREP_HEADS * GDN_DIM
    per = GDN_HEADS // hps

    def body(u_ref, cw_ref, o_ref, pad_ref):
        j = pl.program_id(0)
        for i in range(hps):
            sl = slice(i * GDN_DIM, (i + 1) * GDN_DIM)
            u = _mxr(u_ref[:, sl])
            s = _silu(_conv5(u, _shifted(u, pad_ref), _mxr(cw_ref[:, sl])))
            o_ref[0, :, sl] = jnp.where(j < 2 * per, s * lax.rsqrt(jnp.sum(s * s, axis=1, keepdims=True) + NORM_EPS), s)

    return pl.pallas_call(
        body,
        out_shape=jax.ShapeDtypeStruct((3, T, GDN_HEADS * GDN_DIM), f32),
        grid=(3 * per,),
        in_specs=[pl.BlockSpec((T, W), lambda j: (0, j)), pl.BlockSpec((CONV_K, W), lambda j: (0, j))],
        out_specs=pl.BlockSpec((1, T, W), lambda j: (j // per, 0, j % per)),
        scratch_shapes=[pltpu.VMEM((T + 2 * SHIFT_PAD, GDN_DIM), f32)],
        compiler_params=_cparams(("parallel",)),
        name="gdn_prep",
    )(proj, conv_w)


def _gdn_prep_bwd(proj, conv_w, dqkv_f, dqkv_b, dproj):
    T = proj.shape[0]
    hps, W = PREP_HEADS_BWD, PREP_HEADS_BWD * GDN_DIM
    per = GDN_HEADS // hps

    def body(u_ref, cw_ref, df_ref, db_ref, _, du_ref, dcw_ref, upad_ref, dcpad_ref):
        j = pl.program_id(0)
        for i in range(hps):
            sl = slice(i * GDN_DIM, (i + 1) * GDN_DIM)
            u, cw = _mxr(u_ref[:, sl]), _mxr(cw_ref[:, sl])
            ush = _shifted(u, upad_ref)
            c = _conv5(u, ush, cw)
            s = _silu(c)
            dn = df_ref[0, :, sl] + db_ref[0, :, sl]
            rinv = lax.rsqrt(jnp.sum(s * s, axis=1, keepdims=True) + NORM_EPS)
            ds_norm = rinv * dn - s * (rinv * rinv * rinv) * jnp.sum(dn * s, axis=1, keepdims=True)
            ds = jnp.where(j < 2 * per, ds_norm, dn)
            dc = _mxr(ds * _silu_grad(c))
            dcsh = _shifted(dc, dcpad_ref)
            du = dc * cw[2:3, :]
            for jj in range(CONV_K):
                d = jj - 2
                dcw_ref[jj : jj + 1, sl] = jnp.sum(dc * (u if d == 0 else ush[d]), axis=0, keepdims=True)
                if d != 0:
                    du = du + dcsh[-d] * cw[jj : jj + 1, :]
            du_ref[:, sl] = du.astype(du_ref.dtype)

    qspec = pl.BlockSpec((1, T, W), lambda j: (j // per, 0, j % per))
    return pl.pallas_call(
        body,
        out_shape=(jax.ShapeDtypeStruct(dproj.shape, dproj.dtype), jax.ShapeDtypeStruct((CONV_K, 3 * GDN_HEADS * GDN_DIM), f32)),
        grid=(3 * per,),
        in_specs=[pl.BlockSpec((T, W), lambda j: (0, j)), pl.BlockSpec((CONV_K, W), lambda j: (0, j)), qspec, qspec, _ANY_SPEC],
        out_specs=(pl.BlockSpec((T, W), lambda j: (0, j)), pl.BlockSpec((CONV_K, W), lambda j: (0, j))),
        input_output_aliases={4: 0},
        scratch_shapes=[pltpu.VMEM((T + 2 * SHIFT_PAD, GDN_DIM), f32)] * 2,
        compiler_params=_cparams(("parallel",)),
        name="gdn_prep_bwd",
    )(proj, conv_w, dqkv_f, dqkv_b, dproj)


def _chunk_tri(n, rev):
    i, j = _iota2((n, n), 0), _iota2((n, n), 1)
    same = jnp.right_shift(i, 6) == jnp.right_shift(j, 6)
    order = (j >= i) if rev else (j <= i)
    return jnp.where(jnp.logical_and(same, order), 1.0, 0.0).astype(f32)


def _gate_prep(proj, avec, dvec, w2f, w2b, b2f, b2b, tb=256):
    T = proj.shape[0]
    W = GLA_HEADS * GLA_DK

    def body(s_ref, a_ref, d_ref, wf_ref, wb_ref, bf_ref, bb_ref, g_ref, gc_ref, gf_ref, gb_ref):
        small = s_ref[...]
        tri_f, tri_r = _chunk_tri(tb, False), _chunk_tri(tb, True)
        lane = _iota2(small.shape, 1)
        lg = -jnp.exp(a_ref[...]) * _softplus(small + d_ref[...])
        beta = _sigmoid(small)
        g_ref[...] = jnp.where(lane < 16, lg, jnp.where(lane < 32, beta, 0.0))
        lgm = jnp.where(lane < 16, lg, 0.0)
        gc_ref[...] = jnp.where(lane < 8, _exact01(tri_f, lgm), _exact01(tri_r, lgm))
        gkf = -_softplus(-(_mx(small, wf_ref[...], "nn") + bf_ref[...])) * (1.0 / GATE_NORMALIZER)
        gkb = -_softplus(-(_mx(small, wb_ref[...], "nn") + bb_ref[...])) * (1.0 / GATE_NORMALIZER)
        gf_ref[...] = _exact01(tri_f, gkf)
        gb_ref[...] = _exact01(tri_r, gkb)

    blk = pl.BlockSpec((tb, LANES), lambda i: (i, 0))
    row = pl.BlockSpec((1, LANES), lambda i: (0, 0))
    wide = pl.BlockSpec((tb, W), lambda i: (i, 0))
    wsp = pl.BlockSpec((LANES, W), lambda i: (0, 0))
    wrow = pl.BlockSpec((1, W), lambda i: (0, 0))
    return pl.pallas_call(
        body,
        out_shape=(jax.ShapeDtypeStruct((T, LANES), f32), jax.ShapeDtypeStruct((T, LANES), f32),
                   jax.ShapeDtypeStruct((T, W), f32), jax.ShapeDtypeStruct((T, W), f32)),
        grid=(T // tb,),
        in_specs=[pl.BlockSpec((tb, LANES), lambda i: (i, SMALL_OFF // LANES)), row, row, wsp, wsp, wrow, wrow],
        out_specs=(blk, blk, wide, wide),
        compiler_params=_cparams(("parallel",)),
        name="gate_prep",
    )(proj, avec, dvec, w2f, w2b, b2f, b2b)


def _gdn_gates_bwd(proj, avec, dvec, dg_f, dg_b, dsmall_gla, dproj):
    T = proj.shape[0]
    tb = min(ROW_TILE, T)
    pad = N_CAT - SMALL_OFF

    def body(s_ref, a_ref, d_ref, gf_ref, gb_ref, dl_ref, _, ds_ref, da_ref, dd_ref):
        i = pl.program_id(0)
        small = s_ref[...]
        lane = _iota2(small.shape, 1)
        dgate = gf_ref[...] + gb_ref[...]
        z = small + d_ref[...]
        nega = -jnp.exp(a_ref[...])
        dz = dgate * nega * _sigmoid(z)
        beta = _sigmoid(small)
        dsm = jnp.where(lane < 16, dz, jnp.where(lane < 32, dgate * beta * (1.0 - beta), 0.0))
        ds_ref[:, :LANES] = (dsm + dl_ref[...]).astype(ds_ref.dtype)
        ds_ref[:, LANES:] = jnp.zeros((tb, pad - LANES), ds_ref.dtype)
        lg = nega * _softplus(z)
        pa = jnp.sum(jnp.where(lane < 16, dgate * lg, 0.0), axis=0, keepdims=True)
        pd = jnp.sum(jnp.where(lane < 16, dz, 0.0), axis=0, keepdims=True)

        @pl.when(i == 0)
        def _():
            da_ref[...] = pa
            dd_ref[...] = pd

        @pl.when(i > 0)
        def _():
            da_ref[...] += pa
            dd_ref[...] += pd

    blk = pl.BlockSpec((tb, LANES), lambda i: (i, 0))
    row = pl.BlockSpec((1, LANES), lambda i: (0, 0))
    return pl.pallas_call(
        body,
        out_shape=(jax.ShapeDtypeStruct(dproj.shape, dproj.dtype), jax.ShapeDtypeStruct((1, LANES), f32), jax.ShapeDtypeStruct((1, LANES), f32)),
        grid=(T // tb,),
        in_specs=[pl.BlockSpec((tb, LANES), lambda i: (i, SMALL_OFF // LANES)), row, row, blk, blk, blk, _ANY_SPEC],
        out_specs=(pl.BlockSpec((tb, pad), lambda i: (i, DP_SMALL_BLK)), row, row),
        input_output_aliases={6: 0},
        compiler_params=_cparams(("arbitrary",)),
        name="gdn_gates_bwd",
    )(proj, avec, dvec, dg_f, dg_b, dsmall_gla, dproj)


GDN_HB = 8


def _inv_unit_lower(L, eye):
    A = eye - L
    P2 = _split2(L)
    for _ in range(3):
        P2 = _split2(_h3s(P2, P2, "nn"))
        A = A + _h3s(_split2(A), P2, "nn")
    Ph = P2[0]
    for _ in range(2):
        Ph = _dot16(Ph, Ph, "nn").astype(jnp.bfloat16)
        A = A + _dot16(A.astype(jnp.bfloat16), Ph, "nn")
    return A


def _lane_col(blk, idx):
    lane = _iota2(blk.shape, 1)
    return jnp.sum(jnp.where(lane == idx, blk, 0.0), axis=1, keepdims=True)


def _out_norm(o_f, o_b, proj, gate_blk, w, hd, name):
    T, W = o_f.shape
    tb = min(ROW_TILE, T)
    nh = W // hd

    def body(of_ref, ob_ref, z_ref, w_ref, y_ref):
        wv = w_ref[...]
        for i in range(nh):
            sl = slice(i * hd, (i + 1) * hd)
            o = of_ref[:, sl] + ob_ref[:, sl]
            r = lax.rsqrt(jnp.mean(o * o, axis=1, keepdims=True) + NORM_EPS)
            y_ref[:, sl] = (o * r * wv * _silu(z_ref[:, sl])).astype(y_ref.dtype)

    blk = pl.BlockSpec((tb, W), lambda i: (i, 0))
    return pl.pallas_call(
        body,
        out_shape=jax.ShapeDtypeStruct((T, W), MXU_DTYPE),
        grid=(T // tb,),
        in_specs=[blk, blk, pl.BlockSpec((tb, W), lambda i: (i, gate_blk)), pl.BlockSpec((1, hd), lambda i: (0, 0))],
        out_specs=blk,
        compiler_params=_cparams(("parallel",)),
        name=name,
    )(o_f, o_b, proj, w)


def _out_norm_bwd(dy, o_f, o_b, proj, gate_blk, w, hd, name, dproj):
    T, W = o_f.shape
    tb = min(ROW_TILE, T)
    nh = W // hd

    def body(dy_ref, of_ref, ob_ref, z_ref, w_ref, _, do_ref, dz_ref, dw_ref):
        i = pl.program_id(0)
        wv = w_ref[...]
        dw = jnp.zeros((1, hd), f32)
        for a in range(nh):
            sl = slice(a * hd, (a + 1) * hd)
            o = of_ref[:, sl] + ob_ref[:, sl]
            z = z_ref[:, sl]
            dyv = dy_ref[:, sl]
            r = lax.rsqrt(jnp.mean(o * o, axis=1, keepdims=True) + NORM_EPS)
            orr = o * r
            dn = dyv * _silu(z)
            dz_ref[:, sl] = (dyv * orr * wv * _silu_grad(z)).astype(dz_ref.dtype)
            dw = dw + jnp.sum(dn * orr, axis=0, keepdims=True)
            dnw = dn * wv
            do_ref[:, sl] = r * dnw - o * (r * r * r) * jnp.mean(dnw * o, axis=1, keepdims=True)

        @pl.when(i == 0)
        def _():
            dw_ref[...] = dw

        @pl.when(i > 0)
        def _():
            dw_ref[...] += dw

    blk = pl.BlockSpec((tb, W), lambda i: (i, 0))
    row = pl.BlockSpec((1, hd), lambda i: (0, 0))
    return pl.pallas_call(
        body,
        out_shape=(jax.ShapeDtypeStruct((T, W), f32), jax.ShapeDtypeStruct(dproj.shape, dproj.dtype), jax.ShapeDtypeStruct((1, hd), f32)),
        grid=(T // tb,),
        in_specs=[blk, blk, blk, pl.BlockSpec((tb, W), lambda i: (i, gate_blk)), row, _ANY_SPEC],
        out_specs=(blk, pl.BlockSpec((tb, W), lambda i: (i, gate_blk)), row),
        input_output_aliases={5: 1},
        compiler_params=_cparams(("arbitrary",)),
        name=name,
    )(dy, o_f, o_b, proj, w, dproj)


def _gla_prep_bwd(proj, w2f, w2b, b2f, b2b, dgk_f, dgk_b, dqkv, dproj):
    T = proj.shape[0]
    tb = min(ROW_TILE, T)
    W = GLA_HEADS * GLA_DK
    WV = GLA_HEADS * GLA_DV

    def body(s_ref, wf_ref, wb_ref, bf_ref, bb_ref, df_ref, db_ref, qf, qb, kf, kb, vf, vb, _,
             dp_ref, ds_ref, dwf_ref, dwb_ref, dbf_ref, dbb_ref):
        i = pl.program_id(0)
        dp_ref[:, :W] = (qf[...] + qb[...]).astype(dp_ref.dtype)
        dp_ref[:, W : 2 * W] = (kf[...] + kb[...]).astype(dp_ref.dtype)
        dp_ref[:, 2 * W :] = (vf[...] + vb[...]).astype(dp_ref.dtype)
        small = s_ref[...]
        dsm = jnp.zeros((tb, LANES), f32)
        parts = []
        for w_ref, b_ref, d_ref in ((wf_ref, bf_ref, df_ref), (wb_ref, bb_ref, db_ref)):
            pre = _mx(small, w_ref[...], "nn") + b_ref[...]
            dpre = d_ref[...] * (1.0 / GATE_NORMALIZER) * _sigmoid(-pre)
            dsm = dsm + _mx(dpre, w_ref[...], "nt")
            parts.append((_mx(small, dpre, "tn"), jnp.sum(dpre, axis=0, keepdims=True)))
        ds_ref[...] = dsm

        @pl.when(i == 0)
        def _():
            dwf_ref[...], dbf_ref[...] = parts[0]
            dwb_ref[...], dbb_ref[...] = parts[1]

        @pl.when(i > 0)
        def _():
            dwf_ref[...] += parts[0][0]
            dbf_ref[...] += parts[0][1]
            dwb_ref[...] += parts[1][0]
            dbb_ref[...] += parts[1][1]

    blk = pl.BlockSpec((tb, W), lambda i: (i, 0))
    vblk = pl.BlockSpec((tb, WV), lambda i: (i, 0))
    wsp = pl.BlockSpec((LANES, W), lambda i: (0, 0))
    row = pl.BlockSpec((1, W), lambda i: (0, 0))
    return pl.pallas_call(
        body,
        out_shape=(jax.ShapeDtypeStruct(dproj.shape, dproj.dtype), jax.ShapeDtypeStruct((T, LANES), f32), jax.ShapeDtypeStruct((LANES, W), f32),
                   jax.ShapeDtypeStruct((LANES, W), f32), jax.ShapeDtypeStruct((1, W), f32), jax.ShapeDtypeStruct((1, W), f32)),
        grid=(T // tb,),
        in_specs=[pl.BlockSpec((tb, LANES), lambda i: (i, SMALL_OFF // LANES)), wsp, wsp, row, row, blk, blk,
                  blk, blk, blk, blk, vblk, vblk, _ANY_SPEC],
        out_specs=(pl.BlockSpec((tb, 2 * W + WV), lambda i: (i, DP_GLA_QKV_BLK)), pl.BlockSpec((tb, LANES), lambda i: (i, 0)), wsp, wsp, row, row),
        input_output_aliases={13: 0},
        compiler_params=_cparams(("arbitrary",)),
        name="gla_prep_bwd",
    )(proj, w2f, w2b, b2f, b2b, dgk_f, dgk_b, *dqkv, dproj)


GATE_A_BLK, GATE_B_BLK, Z_BLK, G_B_BLK = 6144 // 1024, 7168 // 1024, 3072 // 1024, 8192 // 1024
DP_GATES_BLK, DP_GLA_QKV_BLK, DP_SMALL_BLK = 6144 // 2048, 4096 // 2048, SMALL_OFF // 512
_ANY_SPEC = pl.BlockSpec(memory_space=pl.ANY)


def _mid(og, obg, proj, x, target, wpg, wpl, wout, w_post, tb=256):
    T = x.shape[0]
    Dm = D_MODEL

    def body(og_ref, obg_ref, ga_ref, gb_ref, x_ref, t_ref, wpg_ref, wpl_ref, wo_ref, wp_ref,
             loss_ref, dwp_ref, dy_ref, m_ref, dout_ref, dya_ref, dyb_ref, dgates_ref, dog_ref, dobg_ref):
        i = pl.program_id(0)
        ya = _mx(og_ref[...], wpg_ref[...], "nn")
        yb = _mx(obg_ref[...], wpl_ref[...], "nn")
        sa, sb = _sigmoid(ga_ref[...]), _sigmoid(gb_ref[...])
        m = sa * ya + sb * yb
        m_ref[...] = m.astype(m_ref.dtype)
        out = _mx(m, wo_ref[...], "nn")
        r = lax.rsqrt(jnp.mean(out * out, axis=1, keepdims=True) + NORM_EPS)
        wp = wp_ref[...]
        e = x_ref[...] + out * r * wp - t_ref[...]
        dy = e * (1.0 / Dm)
        dy_ref[...] = dy
        lpart = jnp.sum(e * e, axis=0, keepdims=True)
        wpart = jnp.sum(dy * out * r, axis=0, keepdims=True)
        dyn = dy * wp
        dout = r * dyn - out * (r * r * r) * jnp.mean(dyn * out, axis=1, keepdims=True)
        dout_ref[...] = dout.astype(dout_ref.dtype)
        dm = _mx(dout, wo_ref[...], "nt")
        dya, dyb = dm * sa, dm * sb
        dya_ref[...] = dya.astype(dya_ref.dtype)
        dyb_ref[...] = dyb.astype(dyb_ref.dtype)
        dgates_ref[:, :Dm] = (dm * ya * sa * (1.0 - sa)).astype(dgates_ref.dtype)
        dgates_ref[:, Dm:] = (dm * yb * sb * (1.0 - sb)).astype(dgates_ref.dtype)
        dog_ref[...] = _mx(dya, wpg_ref[...], "nt")
        dobg_ref[...] = _mx(dyb, wpl_ref[...], "nt")

        @pl.when(i == 0)
        def _():
            loss_ref[...] = lpart
            dwp_ref[...] = wpart

        @pl.when(i > 0)
        def _():
            loss_ref[...] += lpart
            dwp_ref[...] += wpart

    blk = pl.BlockSpec((tb, Dm), lambda i: (i, 0))
    row = pl.BlockSpec((1, Dm), lambda i: (0, 0))
    wsp = pl.BlockSpec((Dm, Dm), lambda i: (0, 0))
    act = jax.ShapeDtypeStruct((T, Dm), MXU_DTYPE)
    big = jax.ShapeDtypeStruct((T, Dm), f32)
    vec = jax.ShapeDtypeStruct((1, Dm), f32)
    return pl.pallas_call(
        body,
        out_shape=(vec, vec, big, act, act, act, act, jax.ShapeDtypeStruct((T, N_CAT), MXU_DTYPE), big, big),
        grid=(T // tb,),
        in_specs=[blk, blk, pl.BlockSpec((tb, Dm), lambda i: (i, GATE_A_BLK)), pl.BlockSpec((tb, Dm), lambda i: (i, GATE_B_BLK)),
                  blk, blk, wsp, wsp, wsp, row],
        out_specs=(row, row) + (blk,) * 5 + (pl.BlockSpec((tb, 2 * Dm), lambda i: (i, DP_GATES_BLK)), blk, blk),
        compiler_params=_cparams(("arbitrary",)),
        name="mid",
    )(og, obg, proj, proj, x, target, wpg, wpl, wout, w_post)


def _heads(ref, lead, n, width, rows=slice(None)):
    return [ref[lead + (rows, slice(i * width, (i + 1) * width))] for i in range(n)]


class _GdnBatch:
    def __init__(self, q, k, v, gcol, grow, beta, nf, saved=None, sol=None):
        B, Cn = q.shape[0], q.shape[1]
        self.incl, self.strict, eye = _dir_masks(B, nf, Cn)
        A = None
        if saved is None:
            self.decay = jnp.where(self.incl, jnp.exp(jnp.where(self.incl, gcol - grow, 0.0)), 0.0)
        else:
            A, self.L, self.attn, self.decay = (saved[:, :, i * Cn : (i + 1) * Cn] for i in range(4))
        self.k, self.v = k, v
        gl = _last_row(gcol, nf)
        self.beta = jnp.broadcast_to(beta, q.shape)
        self.eg = jnp.broadcast_to(jnp.exp(gcol), q.shape)
        self.ekd = jnp.broadcast_to(jnp.exp(gl - gcol), q.shape)
        self.qs = q * (GDN_DIM**-0.5)
        self.kb = k * self.beta
        if saved is None:
            self.L = jnp.where(self.strict, _mx(self.kb, k, "nt") * self.decay, 0.0)
            self.attn = jnp.where(self.incl, _mx(self.qs, k, "nt") * self.decay, 0.0)
            A = _inv_unit_lower(self.L, jnp.where(eye, 1.0, 0.0).astype(f32))
        self.A2 = _split2(A)
        self.saved = jnp.concatenate([A, self.L, self.attn, self.decay], axis=2) if saved is None else saved
        self.kbg = self.kb * self.eg
        if sol is None:
            sol = _h3s(self.A2, _split2(jnp.concatenate([v * self.beta, self.kbg], axis=2)), "nn")
        self.sol = sol
        self.u = self.sol[:, :, :GDN_DIM]
        self.w = self.sol[:, :, GDN_DIM:]
        self.qd = self.qs * self.eg
        self.kd = k * self.ekd
        self.egl = jnp.exp(gl)


def _gdn_chain_inputs(qf_ref, qb_ref, gf_ref, gb_ref, cf_ref, cb_ref, b, hb, need_rows=True):
    qkv = [jnp.stack(_heads(qf_ref, (i,), hb, GDN_DIM) + _heads(qb_ref, (i,), hb, GDN_DIM)) for i in range(3)]
    cols, rows, betas = [], [], []
    for d, (g_ref, c_ref) in enumerate(((gf_ref, cf_ref), (gb_ref, cb_ref))):
        gates, gc = g_ref[...], c_ref[...]
        for hh in range(hb):
            h = b * hb + hh
            col = _lane_col(gc, h + 8 * d)
            cols.append(col)
            if need_rows:
                rows.append(jnp.broadcast_to(col, (CHUNK, LANES)).T[0:1, :])
            betas.append(_lane_col(gates, 16 + h + 8 * d))
    return qkv[0], qkv[1], qkv[2], jnp.stack(cols), (jnp.stack(rows) if need_rows else None), jnp.stack(betas)


GDN_CPS_FWD, GDN_CPS_BWD = 2, 1


def _chunk_views(refs, s_f, s_b):
    out = []
    for i, r in enumerate(refs):
        s = s_f if i % 2 == 0 else s_b
        rows = pl.ds(s * CHUNK, CHUNK)
        out.append(r.at[pl.ds(s, 1)] if len(r.shape) == 4 else (r.at[:, rows, :] if len(r.shape) == 3 else r.at[rows, :]))
    return out


def _gdn_specs(cps, hb):
    R, W = cps * CHUNK, hb * GDN_DIM
    qsp = lambda ci: pl.BlockSpec((3, R, W), lambda n, b: (0, ci(n), b))
    gsp = lambda ci: pl.BlockSpec((R, LANES), lambda n, b: (ci(n), 0))
    osp = lambda ci: pl.BlockSpec((R, W), lambda n, b: (ci(n), b))
    ssp = lambda ci: pl.BlockSpec((cps, hb, GDN_DIM, GDN_DIM), lambda n, b: (ci(n), b, 0, 0))
    asp = lambda ci: pl.BlockSpec((cps, hb, CHUNK, 4 * CHUNK), lambda n, b: (ci(n), b, 0, 0))
    usp = lambda ci: pl.BlockSpec((cps, hb, CHUNK, 2 * GDN_DIM), lambda n, b: (ci(n), b, 0, 0))
    return qsp, gsp, osp, ssp, asp, usp


def _gdn_fwd(qkvn, gates, gc):
    T = qkvn.shape[1]
    N = T // CHUNK
    nh, hb = GDN_HEADS, GDN_HB
    cps = min(GDN_CPS_FWD, N)
    NB = N // cps

    def body(*refs):
        n, b = pl.program_id(0), pl.program_id(1)
        s_scr = refs[-1]

        @pl.when(n == 0)
        def _():
            s_scr[0, pl.ds(b * hb, hb)] = jnp.zeros((hb, GDN_DIM, GDN_DIM), f32)
            s_scr[1, pl.ds(b * hb, hb)] = jnp.zeros((hb, GDN_DIM, GDN_DIM), f32)

        for s in range(cps):
            chunk(*_chunk_views(refs[:-1], s, cps - 1 - s), s_scr, b)

    def chunk(qf_ref, qb_ref, gf_ref, gb_ref, cf_ref, cb_ref, of_ref, ob_ref, sf_ref, sb_ref, af_ref, ab_ref, uf_ref, ub_ref, s_scr, b):
        hs = pl.ds(b * hb, hb)
        q, k, v, gcol, grow, beta = _gdn_chain_inputs(qf_ref, qb_ref, gf_ref, gb_ref, cf_ref, cb_ref, b, hb)
        ck = _GdnBatch(q, k, v, gcol, grow, beta, hb)
        S = jnp.concatenate([s_scr[0, hs], s_scr[1, hs]], axis=0)
        sf_ref[0], sb_ref[0] = S[:hb], S[hb:]
        af_ref[0], ab_ref[0] = ck.saved[:hb], ck.saved[hb:]
        uf_ref[0], ub_ref[0] = ck.sol[:hb], ck.sol[hb:]
        v_new = ck.u - _mx(ck.w, S, "nn")
        o = _mx(ck.qd, S, "nn") + _mx(ck.attn, v_new, "nn")
        S_new = S * ck.egl + _mx(ck.kd, v_new, "tn")
        for hh in range(hb):
            sl = slice(hh * GDN_DIM, (hh + 1) * GDN_DIM)
            of_ref[:, sl] = o[hh]
            ob_ref[:, sl] = o[hb + hh]
        s_scr[0, hs] = S_new[:hb]
        s_scr[1, hs] = S_new[hb:]

    fw, bw = (lambda n: n), (lambda n: NB - 1 - n)
    qsp, gsp, osp, ssp, asp, usp = _gdn_specs(cps, hb)
    o_shape = jax.ShapeDtypeStruct((T, nh * GDN_DIM), f32)
    s_shape = jax.ShapeDtypeStruct((N, nh, GDN_DIM, GDN_DIM), f32)
    a_shape = jax.ShapeDtypeStruct((N, nh, CHUNK, 4 * CHUNK), f32)
    u_shape = jax.ShapeDtypeStruct((N, nh, CHUNK, 2 * GDN_DIM), f32)
    return pl.pallas_call(
        body,
        out_shape=(o_shape, o_shape, s_shape, s_shape, a_shape, a_shape, u_shape, u_shape),
        grid=(NB, nh // hb),
        in_specs=[qsp(fw), qsp(bw), gsp(fw), gsp(bw), gsp(fw), gsp(bw)],
        out_specs=(osp(fw), osp(bw), ssp(fw), ssp(bw), asp(fw), asp(bw), usp(fw), usp(bw)),
        scratch_shapes=[pltpu.VMEM((2, nh, GDN_DIM, GDN_DIM), f32)],
        compiler_params=_cparams(("arbitrary", "arbitrary")),
        name="gdn_scan_fwd",
    )(qkvn, qkvn, gates, gates, gc, gc)


def _gdn_bwd(qkvn, gates, gc, st_f, st_b, a_f, a_b, u_f, u_b, do):
    T = qkvn.shape[1]
    N = T // CHUNK
    nh, hb = GDN_HEADS, GDN_HB
    nb = nh // hb
    cps = min(GDN_CPS_BWD, N)
    NB = N // cps

    def body(*refs):
        n, b = pl.program_id(0), pl.program_id(1)
        ds_scr = refs[-1]

        @pl.when(n == 0)
        def _():
            ds_scr[0, pl.ds(b * hb, hb)] = jnp.zeros((hb, GDN_DIM, GDN_DIM), f32)
            ds_scr[1, pl.ds(b * hb, hb)] = jnp.zeros((hb, GDN_DIM, GDN_DIM), f32)

        for s in range(cps):
            chunk(*_chunk_views(refs[:-1], cps - 1 - s, s), ds_scr, b)

    def chunk(qf_ref, qb_ref, gf_ref, gb_ref, cf_ref, cb_ref, sf_ref, sb_ref, af_ref, ab_ref, uf_ref, ub_ref, dof_ref, dob_ref,
              dqf_ref, dqb_ref, dgf_ref, dgb_ref, ds_scr, b):
        hs = pl.ds(b * hb, hb)

        @pl.when(b == 0)
        def _():
            dgf_ref[...] = jnp.zeros((CHUNK, LANES), f32)
            dgb_ref[...] = jnp.zeros((CHUNK, LANES), f32)

        q, k, v, gcol, grow, beta = _gdn_chain_inputs(qf_ref, qb_ref, gf_ref, gb_ref, cf_ref, cb_ref, b, hb, need_rows=False)
        ck = _GdnBatch(q, k, v, gcol, grow, beta, hb, saved=jnp.concatenate([af_ref[0], ab_ref[0]], axis=0),
                       sol=jnp.concatenate([uf_ref[0], ub_ref[0]], axis=0))
        S = jnp.concatenate([sf_ref[0], sb_ref[0]], axis=0)
        dS = jnp.concatenate([ds_scr[0, hs], ds_scr[1, hs]], axis=0)
        dov = jnp.stack(_heads(dof_ref, (), hb, GDN_DIM) + _heads(dob_ref, (), hb, GDN_DIM))
        v_new = ck.u - _mx(ck.w, S, "nn")
        d_vnew = _mx(ck.attn, dov, "tn") + _mx(ck.kd, dS, "nn")
        d_attn = jnp.where(ck.incl, _mx(dov, v_new, "nt"), 0.0)
        d_qd = _mx(dov, S, "nt")
        d_kd = _mx(v_new, dS, "nt")
        d_gl = jnp.sum(jnp.sum(dS * S, axis=2, keepdims=True), axis=1, keepdims=True) * ck.egl
        dS_new = dS * ck.egl + _mx(ck.qd, dov, "tn") - _mx(ck.w, d_vnew, "tn")
        ds_scr[0, hs] = dS_new[:hb]
        ds_scr[1, hs] = dS_new[hb:]
        d_w = -_mx(d_vnew, S, "nt")
        d_rhs = _h3s(ck.A2, _split2(jnp.concatenate([d_vnew, d_w], axis=2)), "tn")
        d_vb, d_kbg = d_rhs[:, :, :GDN_DIM], d_rhs[:, :, GDN_DIM:]
        dL = -jnp.where(ck.strict, _h3(d_rhs, ck.sol, "nt"), 0.0)
        P = dL * ck.decay
        Q = d_attn * ck.decay
        d_kb = _mx(P, k, "nn") + d_kbg * ck.eg
        d_qs = _mx(Q, k, "nn") + d_qd * ck.eg
        d_k = _mx(P, ck.kb, "tn") + _mx(Q, ck.qs, "tn") + d_kd * ck.ekd + d_kb * ck.beta
        E = dL * ck.L + d_attn * ck.attn
        kdsum = jnp.sum(d_kd * ck.kd, axis=2, keepdims=True)
        d_gc = (jnp.sum(E, axis=2, keepdims=True) - _colsum_col(E)
                + jnp.sum(d_kbg * ck.kbg, axis=2, keepdims=True)
                + jnp.sum(d_qd * ck.qd, axis=2, keepdims=True) - kdsum)
        d_gc = _add_at_last_row(d_gc, d_gl + jnp.sum(kdsum, axis=1, keepdims=True), hb)
        d_beta = jnp.sum(d_vb * v, axis=2, keepdims=True) + jnp.sum(d_kb * k, axis=2, keepdims=True)
        d_q = d_qs * (GDN_DIM**-0.5)
        d_v = d_vb * ck.beta
        lane = _iota2((CHUNK, LANES), 1)
        for d, (dq_ref, dg_ref) in enumerate(((dqf_ref, dgf_ref), (dqb_ref, dgb_ref))):
            acc = dg_ref[...]
            for hh in range(hb):
                c, h = d * hb + hh, b * hb + hh
                sl = slice(hh * GDN_DIM, (hh + 1) * GDN_DIM)
                dq_ref[0, :, sl] = d_q[c]
                dq_ref[1, :, sl] = d_k[c]
                dq_ref[2, :, sl] = d_v[c]
                acc = jnp.where(lane == h + 8 * d, d_gc[c], acc)
                acc = jnp.where(lane == 16 + h + 8 * d, d_beta[c], acc)
            dg_ref[...] = acc

        @pl.when(b == nb - 1)
        def _():
            for d, dg_ref in enumerate((dgf_ref, dgb_ref)):
                blk = dg_ref[...]
                dg_ref[...] = jnp.where(lane < 16, _exact01(_chunk_tri(CHUNK, d == 0), blk), blk)

    fw, bw = (lambda n: NB - 1 - n), (lambda n: n)
    qsp, gsp, osp, ssp, asp, usp = _gdn_specs(cps, hb)
    dq_shape = jax.ShapeDtypeStruct((3, T, nh * GDN_DIM), f32)
    dg_shape = jax.ShapeDtypeStruct((T, LANES), f32)
    return pl.pallas_call(
        body,
        out_shape=(dq_shape, dq_shape, dg_shape, dg_shape),
        grid=(NB, nb),
        in_specs=[qsp(fw), qsp(bw), gsp(fw), gsp(bw), gsp(fw), gsp(bw), ssp(fw), ssp(bw), asp(fw), asp(bw), usp(fw), usp(bw),
                  osp(fw), osp(bw)],
        out_specs=(qsp(fw), qsp(bw), gsp(fw), gsp(bw)),
        scratch_shapes=[pltpu.VMEM((2, nh, GDN_DIM, GDN_DIM), f32)],
        compiler_params=_cparams(("arbitrary", "arbitrary")),
        name="gdn_scan_bwd",
    )(qkvn, qkvn, gates, gates, gc, gc, st_f, st_b, a_f, a_b, u_f, u_b, do, do)


class _GlaBatch:
    def __init__(self, q, k, G, nf):
        B, Cn = q.shape[0], q.shape[1]
        self.incl, _, _ = _dir_masks(B, nf, Cn)
        self.eG = jnp.exp(G)
        self.enG = jnp.exp(-G)
        self.qg = q * (GLA_DK**-0.5) * self.eG
        self.kg = k * self.enG
        self.attn = jnp.where(self.incl, _mx(self.qg, self.kg, "nt"), 0.0)
        gl = _last_row(G, nf)
        self.ekd = jnp.exp(gl - G)
        self.kd = k * self.ekd
        self.egl = jnp.exp(gl)


GLA_CPS = 4


def _chunk_rows(s):
    return slice(s * CHUNK, (s + 1) * CHUNK)


def _gla_chain_inputs(refs_f, refs_b, rows_f, rows_b):
    nh = GLA_HEADS
    out = []
    for i, width in enumerate((GLA_DK, GLA_DK, GLA_DV, GLA_DK)):
        out.append(jnp.stack(_heads(refs_f[i], (), nh, width, rows_f) + _heads(refs_b[i], (), nh, width, rows_b)))
    return out


def _gla_specs(rows, ci):
    nh = GLA_HEADS
    return [
        pl.BlockSpec((rows, nh * GLA_DK), lambda n: (ci(n), 4096 // (nh * GLA_DK))),
        pl.BlockSpec((rows, nh * GLA_DK), lambda n: (ci(n), 4608 // (nh * GLA_DK))),
        pl.BlockSpec((rows, nh * GLA_DV), lambda n: (ci(n), 5120 // (nh * GLA_DV))),
        pl.BlockSpec((rows, nh * GLA_DK), lambda n: (ci(n), 0)),
    ]


def _gla_fwd(proj, G_f, G_b):
    T = proj.shape[0]
    N = T // CHUNK
    nh = GLA_HEADS
    cps = min(GLA_CPS, N)
    NB = N // cps

    def body(qf, kf, vf, gf, qb, kb, vb, gb, of_ref, ob_ref, sf_ref, sb_ref, s_scr):
        n = pl.program_id(0)

        @pl.when(n == 0)
        def _():
            s_scr[...] = jnp.zeros((2 * nh, GLA_DV, GLA_DK), f32)

        for s in range(cps):
            sb = cps - 1 - s
            q, k, v, G = _gla_chain_inputs((qf, kf, vf, gf), (qb, kb, vb, gb), _chunk_rows(s), _chunk_rows(sb))
            ck = _GlaBatch(q, k, G, nh)
            St = s_scr[...]
            sf_ref[s], sb_ref[sb] = St[:nh].astype(sf_ref.dtype), St[nh:].astype(sb_ref.dtype)
            o = _mx(ck.qg, St, "nt") + _mx(ck.attn, v, "nn")
            for i in range(nh):
                sl = slice(i * GLA_DV, (i + 1) * GLA_DV)
                of_ref[_chunk_rows(s), sl] = o[i]
                ob_ref[_chunk_rows(sb), sl] = o[nh + i]
            s_scr[...] = St * ck.egl + _mx(v, ck.kd, "tn")

    fw, bw = (lambda n: n), (lambda n: NB - 1 - n)
    osp = lambda ci: pl.BlockSpec((cps * CHUNK, nh * GLA_DV), lambda n: (ci(n), 0))
    ssp = lambda ci: pl.BlockSpec((cps, nh, GLA_DV, GLA_DK), lambda n: (ci(n), 0, 0, 0))
    o_shape = jax.ShapeDtypeStruct((T, nh * GLA_DV), f32)
    s_shape = jax.ShapeDtypeStruct((N, nh, GLA_DV, GLA_DK), MXU_DTYPE)
    return pl.pallas_call(
        body,
        out_shape=(o_shape, o_shape, s_shape, s_shape),
        grid=(NB,),
        in_specs=_gla_specs(cps * CHUNK, fw) + _gla_specs(cps * CHUNK, bw),
        out_specs=(osp(fw), osp(bw), ssp(fw), ssp(bw)),
        scratch_shapes=[pltpu.VMEM((2 * nh, GLA_DV, GLA_DK), f32)],
        compiler_params=_cparams(("arbitrary",)),
        name="gla_scan_fwd",
    )(proj, proj, proj, G_f, proj, proj, proj, G_b)


def _gla_bwd(proj, G_f, G_b, st_f, st_b, do):
    T = proj.shape[0]
    N = T // CHUNK
    nh = GLA_HEADS
    cps = min(GLA_CPS, N)
    NB = N // cps

    def body(qf, kf, vf, gf, sf_ref, dof_ref, qb, kb, vb, gb, sb_ref, dob_ref,
             dqf_ref, dkf_ref, dvf_ref, dgf_ref, dqb_ref, dkb_ref, dvb_ref, dgb_ref, ds_scr):
        n = pl.program_id(0)

        @pl.when(n == 0)
        def _():
            ds_scr[...] = jnp.zeros((2 * nh, GLA_DV, GLA_DK), f32)

        shp = (2 * nh, CHUNK, CHUNK)
        bb, ii, jj = _iota3(shp, 0), _iota3(shp, 1), _iota3(shp, 2)
        tri = jnp.where(jnp.logical_or(jnp.logical_and(bb < nh, jj >= ii), jnp.logical_and(bb >= nh, jj <= ii)), 1.0, 0.0)
        for s in range(cps):
            sf = cps - 1 - s
            rows = (_chunk_rows(sf), _chunk_rows(s))
            q, k, v, G = _gla_chain_inputs((qf, kf, vf, gf), (qb, kb, vb, gb), *rows)
            ck = _GlaBatch(q, k, G, nh)
            St = jnp.concatenate([sf_ref[sf], sb_ref[s]], axis=0).astype(f32)
            dSt = ds_scr[...]
            dov = jnp.stack(_heads(dof_ref, (), nh, GLA_DV, rows[0]) + _heads(dob_ref, (), nh, GLA_DV, rows[1]))
            d_attn = jnp.where(ck.incl, _mx(dov, v, "nt"), 0.0)
            d_qg = _mx(dov, St, "nn") + _mx(d_attn, ck.kg, "nn")
            d_kg = _mx(d_attn, ck.qg, "tn")
            d_v = _mx(ck.attn, dov, "tn") + _mx(ck.kd, dSt, "nt")
            d_kd = _mx(v, dSt, "nn")
            ds_scr[...] = dSt * ck.egl + _mx(dov, ck.qg, "tn")
            kdd = d_kd * ck.kd
            d_gl = jnp.sum(dSt * St, axis=1, keepdims=True) * ck.egl + jnp.sum(kdd, axis=1, keepdims=True)
            d_q = d_qg * ck.eG * (GLA_DK**-0.5)
            d_k = d_kg * ck.enG + d_kd * ck.ekd
            d_gk = _exact01(tri, _add_at_last_row(d_qg * ck.qg - d_kg * ck.kg - kdd, d_gl, nh))
            for d, (dq_ref, dk_ref, dv_ref, dg_ref) in enumerate(((dqf_ref, dkf_ref, dvf_ref, dgf_ref), (dqb_ref, dkb_ref, dvb_ref, dgb_ref))):
                for i in range(nh):
                    c = d * nh + i
                    ks, vs = slice(i * GLA_DK, (i + 1) * GLA_DK), slice(i * GLA_DV, (i + 1) * GLA_DV)
                    dq_ref[rows[d], ks] = d_q[c]
                    dk_ref[rows[d], ks] = d_k[c]
                    dv_ref[rows[d], vs] = d_v[c]
                    dg_ref[rows[d], ks] = d_gk[c]

    R = cps * CHUNK

    def in_specs(ci):
        return _gla_specs(R, ci) + [
            pl.BlockSpec((cps, nh, GLA_DV, GLA_DK), lambda n: (ci(n), 0, 0, 0)),
            pl.BlockSpec((R, nh * GLA_DV), lambda n: (ci(n), 0)),
        ]

    def out_specs(ci):
        ksp = pl.BlockSpec((R, nh * GLA_DK), lambda n: (ci(n), 0))
        return [ksp, ksp, pl.BlockSpec((R, nh * GLA_DV), lambda n: (ci(n), 0)), ksp]

    fw, bw = (lambda n: NB - 1 - n), (lambda n: n)
    k_shape = jax.ShapeDtypeStruct((T, nh * GLA_DK), f32)
    v_shape = jax.ShapeDtypeStruct((T, nh * GLA_DV), f32)
    return pl.pallas_call(
        body,
        out_shape=(k_shape, k_shape, v_shape, k_shape) * 2,
        grid=(NB,),
        in_specs=in_specs(fw) + in_specs(bw),
        out_specs=tuple(out_specs(fw) + out_specs(bw)),
        scratch_shapes=[pltpu.VMEM((2 * nh, GLA_DV, GLA_DK), f32)],
        compiler_params=_cparams(("arbitrary",)),
        name="gla_scan_bwd",
    )(proj, proj, proj, G_f, st_f, do, proj, proj, proj, G_b, st_b, do)


def _local_step(x, target, wcat, conv_w, avec, dvec, gdn_norm_w, w2f, w2b, b2f, b2b, gla_norm_w, proj_weights, w_pre, w_post,
                send_proj_grads=None, send_in_grads=None):
    h, h_t = _rms_pre(x, w_pre)
    proj = _matmul(h, wcat, "nt", f32, "proj_in", 2048, 512, 1024)
    qkvn = _gdn_prep(proj, conv_w)
    gates, gc, G_f, G_b = _gate_prep(proj, avec, dvec, w2f, w2b, b2f, b2b)
    oa_f, oa_b, sa_f, sa_b, inv_f, inv_b, sol_f, sol_b = _gdn_fwd(qkvn, gates, gc)
    og = _out_norm(oa_f, oa_b, proj, Z_BLK, gdn_norm_w, GDN_DIM, "gdn_out")
    ob_f, ob_b, sb_f, sb_b = _gla_fwd(proj, G_f, G_b)
    obg = _out_norm(ob_f, ob_b, proj, G_B_BLK, gla_norm_w, GLA_DV, "gla_out")
    wpg, wpl, wout = proj_weights(obg)
    (loss_row, d_wpost, dy, m, dout, dya, dyb, dproj, dog, dobg) = _mid(og, obg, proj, x, target, wpg, wpl, wout, w_post)

    d_wout = _matmul(m, dout, "tn", MXU_DTYPE, "dw_out", 1024, 512, 2048)
    d_wpg = _matmul(og, dya, "tn", MXU_DTYPE, "dw_proj_gdn", 1024, 512, 2048)
    d_wpl = _matmul(obg, dyb, "tn", MXU_DTYPE, "dw_proj_gla", 1024, 512, 2048)
    if send_proj_grads is not None:
        gla_norm_w = gla_norm_w + send_proj_grads(d_wpg, d_wpl, d_wout)

    do_b, dproj, d_gla_norm = _out_norm_bwd(dobg, ob_f, ob_b, proj, G_B_BLK, gla_norm_w, GLA_DV, "gla_out_bwd", dproj)
    dq_f, dk_f, dv_f, dgk_f, dq_b, dk_b, dv_b, dgk_b = _gla_bwd(proj, G_f, G_b, sb_f, sb_b, do_b)
    dproj, dsmall_gla, d_w2f, d_w2b, d_b2f, d_b2b = _gla_prep_bwd(proj, w2f, w2b, b2f, b2b, dgk_f, dgk_b,
                                                                 (dq_f, dq_b, dk_f, dk_b, dv_f, dv_b), dproj)

    do_a, dproj, d_gdn_norm = _out_norm_bwd(dog, oa_f, oa_b, proj, Z_BLK, gdn_norm_w, GDN_DIM, "gdn_out_bwd", dproj)
    dqkv_f, dqkv_b, dg_f, dg_b = _gdn_bwd(qkvn, gates, gc, sa_f, sa_b, inv_f, inv_b, sol_f, sol_b, do_a)
    dproj, d_conv = _gdn_prep_bwd(proj, conv_w, dqkv_f, dqkv_b, dproj)
    dproj, d_alog, d_dtb = _gdn_gates_bwd(proj, avec, dvec, dg_f, dg_b, dsmall_gla, dproj)

    d_wcat = _matmul(h_t, dproj, "nn", MXU_DTYPE, "dw_in", 1024, 512, 2048, out_t=True)
    started = None if send_in_grads is None else send_in_grads(d_wcat)
    dh = _matmul(dproj, wcat, "nn", f32, "dh", 512, 1024, N_CAT, after=started)
    grad_x, d_wpre = _rms_pre_bwd(x, w_pre, dh, dy)
    return dict(loss_row=loss_row, grad_x=grad_x, d_wcat=d_wcat, d_conv=d_conv, d_alog=d_alog, d_dtb=d_dtb,
                d_gdn_norm=d_gdn_norm, d_wpg=d_wpg, d_w2f=d_w2f, d_w2b=d_w2b, d_b2f=d_b2f, d_b2b=d_b2b,
                d_gla_norm=d_gla_norm, d_wpl=d_wpl, d_wout=d_wout, d_wpost=d_wpost, d_wpre=d_wpre)


def _to_cat(wf):
    pad = jnp.zeros((N_CAT - N_IN,) + wf.shape[1:], wf.dtype)
    return jnp.concatenate([wf[:4096], wf[4128:6176], wf[7232:9280], wf[6176:7200], wf[4096:4128], wf[7200:7232], pad], axis=0)


def _from_cat(wc):
    return jnp.concatenate([wc[:4096], wc[9216:9248], wc[4096:6144], wc[8192:9216], wc[9248:9280], wc[6144:8192]], axis=0)


def _lane_row(*pieces):
    row = jnp.concatenate(pieces, axis=1)
    return jnp.pad(row, ((0, 0), (0, LANES - row.shape[1])))


def _pad_w2(w2, row0):
    return jnp.pad(w2, ((row0, LANES - row0 - w2.shape[0]), (0, 0)))


SM_ROWS, SM_COLS = 32, 384


def _pack_small_shard(conv, w2f, w2b):
    lead = conv.shape[:-2]
    z = lambda r, c: jnp.zeros(lead + (r, c), f32)
    top = jnp.concatenate([conv, z(8 - CONV_K, SM_COLS)], axis=-2)
    mid = jnp.concatenate([w2f, w2b, z(16, SM_COLS - 128)], axis=-1)
    return jnp.concatenate([top, mid, z(SM_ROWS - 24, SM_COLS)], axis=-2)


SMALL_REGION = {
    "ln_pre_w": (0, 1, 0, 1024), "ln_post_w": (1, 1, 0, 1024), "gk_b2_fwd": (2, 1, 0, 512), "gk_b2_bwd": (2, 1, 512, 512),
    "gla_norm_w": (3, 1, 0, 256), "gdn_norm_w": (3, 1, 256, 128), "a_log_fwd": (4, 1, 0, 8), "a_log_bwd": (5, 1, 0, 8),
    "dt_bias_fwd": (6, 1, 0, 8), "dt_bias_bwd": (7, 1, 0, 8), "conv_w": (8, CONV_K, 0, SM_COLS),
    "gk_w2_fwd": (16, 16, 0, 64), "gk_w2_bwd": (16, 16, 128, 64),
}
SMALL_NAMES = tuple(SMALL_REGION)
SMALL_SHARDED = ("conv_w", "gk_w2_fwd", "gk_w2_bwd")
PACK_ROWS, PACK_COLS = 32, 1024
LOSS_AT = (13, 0)


def _region(name):
    r0, nr, l0, nl = SMALL_REGION[name]
    return slice(r0, r0 + nr), slice(l0, l0 + nl)


def _pack_small_grads(G):
    names = ("d_wpre", "d_wpost", "d_b2f", "d_b2b", "d_gla_norm", "d_gdn_norm", "d_alog", "d_dtb", "loss_row", "d_conv", "d_w2f", "d_w2b")

    def body(wpre, wpost, b2f, b2b, glan, gdnn, alog, dtb, lrow, conv, w2f, w2b, o_ref):
        o_ref[...] = jnp.zeros(o_ref.shape, f32)
        loss = 0.5 * jnp.sum(lrow[...], axis=1, keepdims=True) * (1.0 / D_MODEL)
        whole = {"ln_pre_w": wpre, "ln_post_w": wpost, "gk_b2_fwd": b2f, "gk_b2_bwd": b2b, "gla_norm_w": glan, "gdn_norm_w": gdnn}
        for j in range(N_DEV):
            for name, ref in whole.items():
                o_ref[(j,) + _region(name)] = ref[...]
            o_ref[(j,) + _region("a_log_fwd")] = alog[0:1, 0:8]
            o_ref[(j,) + _region("a_log_bwd")] = alog[0:1, 8:16]
            o_ref[(j,) + _region("dt_bias_fwd")] = dtb[0:1, 0:8]
            o_ref[(j,) + _region("dt_bias_bwd")] = dtb[0:1, 8:16]
            o_ref[(j,) + _region("conv_w")] = conv[:, j * SM_COLS : (j + 1) * SM_COLS]
            o_ref[(j,) + _region("gk_w2_fwd")] = w2f[32:48, j * 64 : (j + 1) * 64]
            o_ref[(j,) + _region("gk_w2_bwd")] = w2b[48:64, j * 64 : (j + 1) * 64]
            o_ref[j, LOSS_AT[0] : LOSS_AT[0] + 1, LOSS_AT[1] : LOSS_AT[1] + 1] = loss

    return pl.pallas_call(
        body, out_shape=jax.ShapeDtypeStruct((N_DEV, PACK_ROWS, PACK_COLS), f32), name="pack_small_grads",
    )(*[G[n] for n in names])


def _small_adam(land, own, W, M, V):
    bc1 = 1.0 - ADAM_B1**ADAM_STEP
    bc2 = 1.0 - ADAM_B2**ADAM_STEP
    n = len(SMALL_NAMES)

    def body(land_ref, own_ref, *refs):
        w_refs, m_refs, v_refs = refs[0:n], refs[n : 2 * n], refs[2 * n : 3 * n]
        outs, loss_ref, g_scr = refs[3 * n : 7 * n], refs[7 * n], refs[7 * n + 1]
        me = _my_device()
        g = jnp.where(me == 0, own_ref[0], land_ref[0])
        for j in range(1, N_DEV):
            g = g + jnp.where(me == j, own_ref[j], land_ref[j])
        g_scr[...] = g
        loss_ref[...] = g_scr[LOSS_AT[0] : LOSS_AT[0] + 1, LOSS_AT[1] : LOSS_AT[1] + 1]
        for i, name in enumerate(SMALL_NAMES):
            gp = g_scr[_region(name)]
            lead = (0,) if name in SMALL_SHARDED else (Ellipsis,)
            w, m, v = w_refs[i][lead], m_refs[i][lead], v_refs[i][lead]
            m2 = ADAM_B1 * m + (1.0 - ADAM_B1) * gp
            v2 = ADAM_B2 * v + (1.0 - ADAM_B2) * (gp * gp)
            delta = -ADAM_LR * ((m2 / bc1) / (jnp.sqrt(v2 / bc2) + ADAM_EPS) + ADAM_WD * w)
            for k, val in enumerate((gp, delta, m2, v2)):
                outs[4 * i + k][lead] = val

    shapes = [jax.ShapeDtypeStruct(W[name].shape, f32) for name in SMALL_NAMES for _ in range(4)]
    res = pl.pallas_call(
        body,
        out_shape=tuple(shapes) + (jax.ShapeDtypeStruct((1, 1), f32),),
        scratch_shapes=[pltpu.VMEM((PACK_ROWS, PACK_COLS), f32)],
        name="adam_small",
    )(land, own, *[P[name] for P in (W, M, V) for name in SMALL_NAMES])
    return {name: res[4 * i : 4 * i + 4] for i, name in enumerate(SMALL_NAMES)}, res[4 * n]


_MESH = pl.DeviceIdType.MESH


N_BIG_COPIES, N_SMALL_COPIES = 9, 7


def _gather_weights(w_in_s, sm_s):
    shapes = (w_in_s.shape, sm_s.shape)
    dtypes = (MXU_DTYPE, f32)

    def body(win_ref, sm_ref, gin_ref, gsm_ref, send_sems, recv_sems):
        x, y, c = lax.axis_index("x"), lax.axis_index("y"), lax.axis_index("c")
        me, sibling = (x, y, c), (x, y, 1 - c)
        chips = [(1 - x, y), (x, 1 - y), (1 - x, 1 - y)]

        def idx(px, py, pc):
            return 4 * px + 2 * py + pc

        for r in range(w_in_s.shape[1] // LANES):
            cols = slice(r * LANES, (r + 1) * LANES)
            gin_ref[idx(*me), :, cols] = win_ref[:, cols].astype(MXU_DTYPE)
        gsm_ref[idx(*me)] = sm_ref[...]

        def copy(sem, blk, to):
            return pltpu.make_async_remote_copy(src_ref=blk, dst_ref=blk, send_sem=send_sems.at[sem], recv_sem=recv_sems.at[sem],
                                                device_id=to, device_id_type=_MESH)

        def small(k, block, to):
            return copy(N_BIG_COPIES + k, gsm_ref.at[idx(*block)], to)

        half = w_in_s.shape[1] // 2
        xn, yn, dg = (1 - x, y), (x, 1 - y), (1 - x, 1 - y)

        def big(sem, block, to, part=None):
            blk = gin_ref.at[idx(*block)]
            return copy(sem, blk if part is None else blk.at[:, pl.ds(part * half, half)], to)

        started = [big(0, me, sibling), big(1, me, (*xn, c)), big(2, me, (*yn, c)), small(0, me, sibling)]
        started += [small(1 + j, me, (*chip, c)) for j, chip in enumerate(chips)]
        for cp in started:
            cp.start()

        def then(arrived, forwards):
            arrived.wait_recv()
            for cp in forwards:
                cp.start()
            started.extend(forwards)

        then(big(1, (*xn, c), me), [big(3, (*xn, c), (*yn, c), 1), big(5, (*xn, c), sibling)])
        then(big(2, (*yn, c), me), [big(4, (*yn, c), (*xn, c), 0), big(6, (*yn, c), sibling)])
        then(big(4, (*dg, c), me, 0), [big(7, (*dg, c), sibling, 0)])
        then(big(3, (*dg, c), me, 1), [big(8, (*dg, c), sibling, 1)])
        for j, chip in enumerate(chips):
            then(small(1 + j, (*chip, c), me), [small(4 + j, (*chip, c), sibling)])
        big(0, sibling, me).wait_recv()
        big(5, (*xn, 1 - c), me).wait_recv()
        big(6, (*yn, 1 - c), me).wait_recv()
        big(7, (*dg, 1 - c), me, 0).wait_recv()
        big(8, (*dg, 1 - c), me, 1).wait_recv()
        small(0, sibling, me).wait_recv()
        for j, chip in enumerate(chips):
            small(4 + j, (*chip, 1 - c), me).wait_recv()
        for cp in started:
            cp.wait_send()

    vm = pl.BlockSpec(memory_space=pltpu.VMEM)
    return pl.pallas_call(
        body,
        out_shape=tuple(jax.ShapeDtypeStruct((N_DEV,) + s, d) for s, d in zip(shapes, dtypes)),
        in_specs=[vm, vm],
        out_specs=(vm, vm),
        scratch_shapes=[pltpu.SemaphoreType.DMA((N_BIG_COPIES + N_SMALL_COPIES,)), pltpu.SemaphoreType.DMA((N_BIG_COPIES + N_SMALL_COPIES,))],
        compiler_params=pltpu.CompilerParams(vmem_limit_bytes=VMEM_LIMIT),
        name="gather_weights",
    )(w_in_s, sm_s)


N_CHIP = 4
_EFFECT = pltpu.SideEffectType.DATAFLOW_SIDE_EFFECTING
_HBM_SPEC = pl.BlockSpec(memory_space=pltpu.HBM)
_SEM_SPEC = pl.BlockSpec(memory_space=pltpu.SEMAPHORE)


def _relation_peer(k, x, y, c):
    px = (1 - x) if (k & 4) else x
    py = (1 - y) if (k & 2) else y
    pc = (1 - c) if (k & 1) else c
    return (px, py, pc), 4 * px + 2 * py + pc


_PLAN_COPIES = {"gather": N_DEV - 1, "blocks": N_DEV - 1, "sibling": 4, "chips": 3}
_PLAN_SLOTS = {"gather": N_DEV, "blocks": N_DEV, "sibling": 4, "chips": 4}


def _plan_copies(plan, src_ref, land_ref, send_sems, recv_sems, arrival):
    x, y, c = lax.axis_index("x"), lax.axis_index("y"), lax.axis_index("c")
    me, my_chip = 4 * x + 2 * y + c, 2 * x + y
    out = []

    def add(i, src, there, here, dev):
        out.append(pltpu.make_async_remote_copy(src_ref=src, dst_ref=land_ref.at[here if arrival else there], send_sem=send_sems.at[i],
                                                recv_sem=recv_sems.at[i], device_id=dev, device_id_type=_MESH))

    if plan in ("gather", "blocks"):
        for k in range(1, N_DEV):
            dev, p = _relation_peer(k, x, y, c)
            add(k - 1, src_ref.at[p] if plan == "blocks" else src_ref, me, p, dev)
    elif plan == "sibling":
        for j in range(4):
            add(j, src_ref.at[2 * j + 1 - c], j, j, (x, y, 1 - c))
    else:
        for i, k in enumerate((2, 4, 6)):
            (px, py, pc), _ = _relation_peer(k, x, y, c)
            add(i, src_ref.at[2 * px + py], my_chip, 2 * px + py, (px, py, pc))
    return out


def _exchange_start(src, plan, name, after=None):
    n = _PLAN_COPIES[plan]
    land = lax.empty((_PLAN_SLOTS[plan],) + (src.shape if plan == "gather" else src.shape[1:]), src.dtype)
    extra = [] if after is None else [after]

    def body(src_ref, land_ref, *rest):
        send_sems, recv_sems, _, _, token = rest[len(extra) :]
        for cp in _plan_copies(plan, src_ref, land_ref, send_sems, recv_sems, False):
            cp.start()
        token[...] = jnp.zeros_like(token)

    return pl.pallas_call(
        body,
        name=name,
        out_shape=(pltpu.SemaphoreType.DMA((n,)), pltpu.SemaphoreType.DMA((n,)), pltpu.HBM(src.shape, src.dtype),
                   pltpu.HBM(land.shape, land.dtype), jax.ShapeDtypeStruct((8, LANES), f32)),
        in_specs=(_HBM_SPEC, _HBM_SPEC) + (_ANY_SPEC,) * len(extra),
        out_specs=(_SEM_SPEC, _SEM_SPEC, _HBM_SPEC, _HBM_SPEC, pl.BlockSpec(memory_space=pltpu.VMEM)),
        input_output_aliases={0: 2, 1: 3},
        compiler_params=pltpu.CompilerParams(has_side_effects=_EFFECT),
    )(pltpu.with_memory_space_constraint(src, pltpu.HBM), pltpu.with_memory_space_constraint(land, pltpu.HBM), *extra)


def _exchange_wait(started, after, plan, name):
    send_sems, recv_sems, src_thru, land_thru, _ = started

    def body(src_ref, land_ref, send_sems, recv_sems, after_ref, src_dead, got_ref):
        for cp in _plan_copies(plan, src_ref, land_ref, send_sems, recv_sems, True):
            cp.wait_send()
            cp.wait_recv()

    return pl.pallas_call(
        body,
        name=name,
        out_shape=(pltpu.HBM(src_thru.shape, src_thru.dtype), pltpu.HBM(land_thru.shape, land_thru.dtype)),
        in_specs=(_HBM_SPEC, _HBM_SPEC, _SEM_SPEC, _SEM_SPEC, pl.BlockSpec(memory_space=pl.ANY)),
        out_specs=(_HBM_SPEC, _HBM_SPEC),
        input_output_aliases={0: 0, 1: 1},
        compiler_params=pltpu.CompilerParams(has_side_effects=_EFFECT),
    )(src_thru, land_thru, send_sems, recv_sems, after)


def _pair_sum(g, r, name, tc=LANES):
    _, R, Cc = g.shape

    def body(g_ref, r_ref, o_ref):
        o_ref[...] = (g_ref[...].astype(f32) + r_ref[...].astype(f32)).astype(o_ref.dtype)

    return pl.pallas_call(
        body,
        out_shape=jax.ShapeDtypeStruct(r.shape, r.dtype),
        grid=(N_CHIP, Cc // tc),
        in_specs=[pl.BlockSpec((1, R, tc), lambda k, i: (2 * k + lax.axis_index("c"), 0, i)), pl.BlockSpec((1, R, tc), lambda k, i: (k, 0, i))],
        out_specs=pl.BlockSpec((1, R, tc), lambda k, i: (k, 0, i)),
        compiler_params=_cparams(("parallel", "parallel")),
        name=name,
    )(g, r)


def _sum_adam(parts, w, m, v, name, tb, tc=None, own=None, own_slot=None):
    R, Cc = w.shape
    n_parts = parts.shape[0]
    tb = R if tc else min(tb, R)
    tc = tc or Cc
    assert R % tb == 0 and Cc % tc == 0
    bc1 = 1.0 - ADAM_B1**ADAM_STEP
    bc2 = 1.0 - ADAM_B2**ADAM_STEP
    extra = [] if own is None else [own]

    def body(p_ref, w_ref, m_ref, v_ref, *rest):
        g_ref, d_ref, nm_ref, nv_ref = rest[len(extra) :]

        def part(j):
            pj = p_ref[j].astype(f32)
            return pj if own is None else jnp.where(own_slot() == j, rest[0][0].astype(f32), pj)

        g = part(0)
        for j in range(1, n_parts):
            g = g + part(j)
        g_ref[...] = g
        m2 = ADAM_B1 * m_ref[...] + (1.0 - ADAM_B1) * g
        v2 = ADAM_B2 * v_ref[...] + (1.0 - ADAM_B2) * (g * g)
        nm_ref[...] = m2
        nv_ref[...] = v2
        d_ref[...] = -ADAM_LR * ((m2 / bc1) / (jnp.sqrt(v2 / bc2) + ADAM_EPS) + ADAM_WD * w_ref[...])

    blk = pl.BlockSpec((tb, tc), lambda i, j: (i, j))
    o = jax.ShapeDtypeStruct((R, Cc), f32)
    return pl.pallas_call(
        body,
        out_shape=(o, o, o, o),
        grid=(R // tb, Cc // tc),
        in_specs=[pl.BlockSpec((n_parts, tb, tc), lambda i, j: (0, i, j)), blk, blk, blk]
        + [pl.BlockSpec((1, tb, tc), lambda i, j: (own_slot(), i, j))] * len(extra),
        out_specs=(blk, blk, blk, blk),
        compiler_params=_cparams(("parallel", "parallel")),
        name=name,
    )(parts, w, m, v, *extra)


def _my_device():
    return 4 * lax.axis_index("x") + 2 * lax.axis_index("y") + lax.axis_index("c")


def _my_chip():
    return 2 * lax.axis_index("x") + lax.axis_index("y")


_WEIGHTS = ["ln_pre_w", "w_in", "conv_w", "a_log_fwd", "a_log_bwd", "dt_bias_fwd", "dt_bias_bwd", "gdn_norm_w", "w_proj_gdn",
            "gk_w2_fwd", "gk_b2_fwd", "gk_w2_bwd", "gk_b2_bwd", "gla_norm_w", "w_proj_gla", "w_out", "ln_post_w"]


def kernel(x, ln_pre_w, w_in, conv_w, a_log_fwd, a_log_bwd, dt_bias_fwd, dt_bias_bwd, gdn_norm_w, w_proj_gdn, gk_w2_fwd, gk_b2_fwd, gk_w2_bwd, gk_b2_bwd, gla_norm_w, w_proj_gla, w_out, ln_post_w, loss_target, m_ln_pre_w, m_w_in, m_conv_w, m_a_log_fwd, m_a_log_bwd, m_dt_bias_fwd, m_dt_bias_bwd, m_gdn_norm_w, m_w_proj_gdn, m_gk_w2_fwd, m_gk_b2_fwd, m_gk_w2_bwd, m_gk_b2_bwd, m_gla_norm_w, m_w_proj_gla, m_w_out, m_ln_post_w, v_ln_pre_w, v_w_in, v_conv_w, v_a_log_fwd, v_a_log_bwd, v_dt_bias_fwd, v_dt_bias_bwd, v_gdn_norm_w, v_w_proj_gdn, v_gk_w2_fwd, v_gk_b2_fwd, v_gk_w2_bwd, v_gk_b2_bwd, v_gla_norm_w, v_w_proj_gla, v_w_out, v_ln_post_w):
    args = locals()
    W = {n: args[n] for n in _WEIGHTS}
    M = {n: args["m_" + n] for n in _WEIGHTS}
    V = {n: args["v_" + n] for n in _WEIGHTS}

    wp_stack = lambda P: jnp.concatenate([P["w_proj_gdn"], P["w_proj_gla"], P["w_out"]], axis=0)
    sm_pack = lambda P: _pack_small_shard(P["conv_w"][0], P["gk_w2_fwd"][0], P["gk_w2_bwd"][0])
    me = 4 * lax.axis_index("x") + 2 * lax.axis_index("y") + lax.axis_index("c")
    w_in_t, m_in_t, v_in_t = (jnp.transpose(a[0]) for a in (w_in, m_w_in, v_w_in))
    g_in, g_sm = _gather_weights(w_in_t, sm_pack(W))
    wp_own = wp_stack(W).astype(MXU_DTYPE)
    wp_started = _exchange_start(wp_own, "gather", "gather_proj_start", after=g_sm)

    def proj_weights(after):
        own, land = _exchange_wait(wp_started, after, "gather", "gather_proj_wait")
        g_p = lax.dynamic_update_slice(land, own[None], (me, 0, 0, 0))
        return tuple(g_p[:, i].reshape(D_MODEL, D_MODEL) for i in range(3))

    rows_p = 3 * D_MODEL // N_DEV
    sent = {}

    def send_proj_grads(d_wpg, d_wpl, d_wout):
        p_p = jnp.stack([d_wpg, d_wpl, d_wout]).reshape(3, N_DEV, D_MODEL // N_DEV, D_MODEL).transpose(1, 0, 2, 3)
        sent["p_p"] = p_p.reshape(N_DEV, rows_p, D_MODEL)
        sent["started"] = _exchange_start(sent["p_p"], "blocks", "exchange_proj_start")
        return sent["started"][4][0:1, 0:1]

    def send_in_grads(d_wcat):
        p_in = _from_cat(d_wcat).reshape(N_DEV, SHARD_IN, D_MODEL)
        sib = _exchange_start(p_in, "sibling", "exchange_in_sibling_start")
        p_p, land_p = _exchange_wait(sent["started"], sib[4], "blocks", "exchange_proj_wait")
        sent["o_p"] = _sum_adam(land_p, wp_stack(W).reshape(rows_p, D_MODEL), wp_stack(M).reshape(rows_p, D_MODEL),
                                wp_stack(V).reshape(rows_p, D_MODEL), "adam_w_proj", 128, own=p_p, own_slot=_my_device)
        p_in, sib_land = _exchange_wait(sib, sent["o_p"][0], "sibling", "exchange_in_sibling_wait")
        sent["chips"] = _exchange_start(_pair_sum(p_in, sib_land, "pair_sum_w_in", tc=512), "chips", "exchange_in_chips_start")
        return sent["chips"][4]

    wcat = _to_cat(g_in.reshape(N_IN, D_MODEL))
    conv_full = g_sm[:, 0:CONV_K, :].transpose(1, 0, 2).reshape(CONV_K, N_DEV * SM_COLS)
    w2f_full = g_sm[:, 8:24, 0:64].transpose(1, 0, 2).reshape(16, 512)
    w2b_full = g_sm[:, 8:24, 64:128].transpose(1, 0, 2).reshape(16, 512)

    avec = _lane_row(a_log_fwd, a_log_bwd)
    dvec = _lane_row(dt_bias_fwd, dt_bias_bwd)
    G = _local_step(x[0], loss_target[0], wcat, conv_full, avec, dvec, gdn_norm_w, _pad_w2(w2f_full, 32), _pad_w2(w2b_full, 48),
                    gk_b2_fwd, gk_b2_bwd, gla_norm_w, proj_weights, ln_pre_w + wp_started[4][0:1, 0:1], ln_post_w, send_proj_grads,
                    send_in_grads)

    small_started = _exchange_start(_pack_small_grads(G), "blocks", "exchange_small_start")
    q_in, land_in = _exchange_wait(sent["chips"], small_started[4], "chips", "exchange_in_chips_wait")
    o_in = _sum_adam(land_in, w_in_t, m_in_t, v_in_t, "adam_w_in", SHARD_IN, tc=256, own=q_in, own_slot=_my_chip)
    o_p = sent["o_p"]
    p_small, land_small = _exchange_wait(small_started, o_in[0], "blocks", "exchange_small_wait")
    o_small, loss = _small_adam(land_small, p_small, W, M, V)

    res = []
    for kind in range(4):
        per = {name: o_small[name][kind] for name in SMALL_NAMES}
        per["w_in"] = jnp.transpose(o_in[kind])[None]
        pp = o_p[kind].reshape(3, 1, D_MODEL // N_DEV, D_MODEL)
        per["w_proj_gdn"], per["w_proj_gla"], per["w_out"] = pp[0], pp[1], pp[2]
        res.append([per[n] for n in _WEIGHTS])
    return (loss[0, 0], G["grad_x"][None], *res[0], *res[1], *res[2], *res[3])
```

```python
import jax
import jax.numpy as jnp
from jax import lax
from jax.experimental import pallas as pl
from jax.experimental.pallas import tpu as pltpu

f32 = jnp.float32
MXU_DTYPE = jnp.bfloat16

D_MODEL = 1024
CHUNK = 64
NORM_EPS = 1e-6
GDN_HEADS, GDN_DIM = 8, 128
GLA_HEADS, GLA_DK, GLA_DV = 4, 128, 256
GATE_NORMALIZER = 16.0
CONV_K = 5
N_IN = 9280
N_DEV = 8
SHARD_IN = N_IN // N_DEV

N_CAT = 9728
SMALL_OFF = 9216
LANES = 128

ADAM_LR, ADAM_B1, ADAM_B2, ADAM_EPS, ADAM_WD, ADAM_STEP = 0.001, 0.9, 0.999, 1e-08, 0.01, 10

VMEM_LIMIT = 56 * 1024 * 1024


def _cparams(sem=None):
    return pltpu.CompilerParams(dimension_semantics=sem, vmem_limit_bytes=VMEM_LIMIT)


_DN = {"nn": (((1,), (0,)), ((), ())), "nt": (((1,), (1,)), ((), ())), "tn": (((0,), (0,)), ((), ()))}
_BDN = {"nn": (((2,), (1,)), ((0,), (0,))), "nt": (((2,), (2,)), ((0,), (0,))), "tn": (((1,), (1,)), ((0,), (0,)))}


def _dot16(a, b, kind):
    return lax.dot_general(a, b, (_DN if a.ndim == 2 else _BDN)[kind], preferred_element_type=f32)


def _mx(a, b, kind):
    return _dot16(a.astype(MXU_DTYPE), b.astype(MXU_DTYPE), kind)


def _split2(a):
    hi = a.astype(jnp.bfloat16)
    return hi, (a - hi.astype(f32)).astype(jnp.bfloat16)


def _split3(a):
    p1 = a.astype(jnp.bfloat16)
    r = a - p1.astype(f32)
    p2 = r.astype(jnp.bfloat16)
    return p1, p2, (r - p2.astype(f32)).astype(jnp.bfloat16)


def _h3s(a2, b2, kind):
    (ah, al), (bh, bl) = a2, b2
    return _dot16(ah, bh, kind) + (_dot16(ah, bl, kind) + _dot16(al, bh, kind))


def _h3(a, b, kind):
    return _h3s(_split2(a), _split2(b), kind)


def _exact01(t01, x, kind="nn"):
    t = t01.astype(jnp.bfloat16)
    x1, x2, x3 = _split3(x)
    return _dot16(t, x1, kind) + (_dot16(t, x2, kind) + _dot16(t, x3, kind))


def _colsum_col(e):
    ones = jnp.ones(e.shape[:-1] + (LANES,), jnp.bfloat16)
    e1, e2, e3 = _split3(e)
    return (_dot16(e1, ones, "tn") + (_dot16(e2, ones, "tn") + _dot16(e3, ones, "tn")))[..., 0:1]


def _iota3(shape, axis):
    return lax.broadcasted_iota(jnp.int32, shape, axis)


def _dir_masks(B, nf, Cn):
    shp = (B, Cn, Cn)
    bb, ii, jj = _iota3(shp, 0), _iota3(shp, 1), _iota3(shp, 2)
    fwd = bb < nf
    rev = jnp.logical_not(fwd)
    incl = jnp.logical_or(jnp.logical_and(fwd, ii >= jj), jnp.logical_and(rev, ii <= jj))
    strict = jnp.logical_and(incl, ii != jj)
    return incl, strict, ii == jj


def _last_row(x, nf):
    B, Cn = x.shape[0], x.shape[1]
    fwd = _iota3((B, 1, 1), 0) < nf
    return jnp.where(fwd, x[:, Cn - 1 : Cn, :], x[:, 0:1, :])


def _add_at_last_row(x, val, nf):
    B, Cn = x.shape[0], x.shape[1]
    bb, rr = _iota3((B, Cn, 1), 0), _iota3((B, Cn, 1), 1)
    at = jnp.logical_or(jnp.logical_and(bb < nf, rr == Cn - 1), jnp.logical_and(bb >= nf, rr == 0))
    return x + jnp.where(at, val, 0.0)


def _sigmoid(x):
    return jax.nn.sigmoid(x)


def _silu(x):
    return x * _sigmoid(x)


def _silu_grad(x):
    s = _sigmoid(x)
    return s * (1.0 + x * (1.0 - s))


def _softplus(x):
    u = jnp.exp(-jnp.abs(x))
    l1p = jnp.where(u < 1e-3, u * (1.0 - u * (0.5 - u * (1.0 / 3.0))), jnp.log(1.0 + u))
    return jnp.maximum(x, 0.0) + l1p


def _mxr(x):
    return x.astype(MXU_DTYPE).astype(f32)


def _iota2(shape, axis):
    return lax.broadcasted_iota(jnp.int32, shape, axis)


def _matmul(a, b, kind, out_dtype, name, tm, tn, tk, out_t=False, after=None):
    extra = [] if after is None else [after]
    if kind == "nn":
        (M, K), N = a.shape, b.shape[1]
    elif kind == "nt":
        (M, K), N = a.shape, b.shape[0]
    else:
        (K, M), N = a.shape, b.shape[1]
    tm, tn, tk = min(tm, M), min(tn, N), min(tk, K)
    assert M % tm == 0 and N % tn == 0 and K % tk == 0, (name, M, N, K)
    nk = K // tk
    if kind == "tn":
        a_spec = pl.BlockSpec((tk, tm), lambda i, j, k: (k, i))
    else:
        a_spec = pl.BlockSpec((tm, tk), lambda i, j, k: (i, k))
    b_mode = dict(pipeline_mode=pl.Buffered(1)) if (tn == N and tk == K) else {}
    if kind == "nt":
        b_spec = pl.BlockSpec((tn, tk), lambda i, j, k: (j, k), **b_mode)
    else:
        b_spec = pl.BlockSpec((tk, tn), lambda i, j, k: (k, j), **b_mode)

    def body(a_ref, b_ref, *rest):
        o_ref, acc = rest[len(extra)], rest[len(extra) + 1 :]
        p = _mx(a_ref[...], b_ref[...], kind)
        if nk == 1:
            o_ref[...] = (p.T if out_t else p).astype(out_dtype)
        else:
            assert not out_t
            acc_ref = acc[0]
            k = pl.program_id(2)

            @pl.when(k == 0)
            def _():
                acc_ref[...] = p

            @pl.when(k > 0)
            def _():
                acc_ref[...] += p

            @pl.when(k == nk - 1)
            def _():
                o_ref[...] = acc_ref[...].astype(out_dtype)

    return pl.pallas_call(
        body,
        out_shape=jax.ShapeDtypeStruct((N, M) if out_t else (M, N), out_dtype),
        grid=(M // tm, N // tn, nk),
        in_specs=[a_spec, b_spec] + [pl.BlockSpec((8, LANES), lambda i, j, k: (0, 0))] * len(extra),
        out_specs=pl.BlockSpec((tn, tm), lambda i, j, k: (j, i)) if out_t else pl.BlockSpec((tm, tn), lambda i, j, k: (i, j)),
        scratch_shapes=[] if nk == 1 else [pltpu.VMEM((tm, tn), f32)],
        compiler_params=_cparams(("parallel", "parallel", "arbitrary")),
        name=name,
    )(a, b, *extra)


ROW_TILE = 512


def _rms_pre(x, w):
    T = x.shape[0]
    tb = min(ROW_TILE, T)

    def body(x_ref, w_ref, h_ref, ht_ref):
        xv = x_ref[...]
        r = lax.rsqrt(jnp.mean(xv * xv, axis=1, keepdims=True) + NORM_EPS)
        h = xv * r * w_ref[...]
        h_ref[...] = h.astype(h_ref.dtype)
        ht_ref[...] = h.T.astype(ht_ref.dtype)

    return pl.pallas_call(
        body,
        out_shape=(jax.ShapeDtypeStruct((T, D_MODEL), MXU_DTYPE), jax.ShapeDtypeStruct((D_MODEL, T), MXU_DTYPE)),
        grid=(T // tb,),
        in_specs=[pl.BlockSpec((tb, D_MODEL), lambda i: (i, 0)), pl.BlockSpec((1, D_MODEL), lambda i: (0, 0))],
        out_specs=(pl.BlockSpec((tb, D_MODEL), lambda i: (i, 0)), pl.BlockSpec((D_MODEL, tb), lambda i: (0, i))),
        compiler_params=_cparams(("parallel",)),
        name="rms_pre",
    )(x, w)


def _rms_pre_bwd(x, w, dh, dy):
    T = x.shape[0]
    tb = min(ROW_TILE, T)

    def body(x_ref, w_ref, dh_ref, dy_ref, dx_ref, dw_ref):
        i = pl.program_id(0)
        xv, dhv = x_ref[...], dh_ref[...]
        r = lax.rsqrt(jnp.mean(xv * xv, axis=1, keepdims=True) + NORM_EPS)
        dhw = dhv * w_ref[...]
        dx_ref[...] = dy_ref[...] + r * dhw - xv * (r * r * r) * jnp.mean(dhw * xv, axis=1, keepdims=True)
        part = jnp.sum(dhv * xv * r, axis=0, keepdims=True)

        @pl.when(i == 0)
        def _():
            dw_ref[...] = part

        @pl.when(i > 0)
        def _():
            dw_ref[...] += part

    blk = pl.BlockSpec((tb, D_MODEL), lambda i: (i, 0))
    row = pl.BlockSpec((1, D_MODEL), lambda i: (0, 0))
    return pl.pallas_call(
        body,
        out_shape=(jax.ShapeDtypeStruct((T, D_MODEL), f32), jax.ShapeDtypeStruct((1, D_MODEL), f32)),
        grid=(T // tb,),
        in_specs=[blk, row, blk, blk],
        out_specs=(blk, row),
        compiler_params=_cparams(("arbitrary",)),
        name="rms_pre_bwd",
    )(x, w, dh, dy)


SHIFTS = (-2, -1, 1, 2)


SHIFT_PAD = 8
PREP_HEADS, PREP_HEADS_BWD = 4, 1


def _shifted(x, pad_ref):
    T, cb = x.shape
    pad_ref[0:SHIFT_PAD] = jnp.zeros((SHIFT_PAD, cb), f32)
    pad_ref[SHIFT_PAD + T :] = jnp.zeros((SHIFT_PAD, cb), f32)
    pad_ref[SHIFT_PAD : SHIFT_PAD + T] = x
    return {d: pad_ref[SHIFT_PAD + d : SHIFT_PAD + d + T] for d in SHIFTS}


def _conv5(u, ush, cw):
    acc = u * cw[2:3, :]
    for j in (0, 1, 3, 4):
        acc = acc + ush[j - 2] * cw[j : j + 1, :]
    return acc


def _gdn_prep(proj, conv_w):
    T = proj.shape[0]
    hps, W = PREP_HEADS, PREP_HEADS * GDN_DIM
    per = GDN_HEADS // hps

    def body(u_ref, cw_ref, o_ref, pad_ref):
        j = pl.program_id(0)
        for i in range(hps):
            sl = slice(i * GDN_DIM, (i + 1) * GDN_DIM)
            u = _mxr(u_ref[:, sl])
            s = _silu(_conv5(u, _shifted(u, pad_ref), _mxr(cw_ref[:, sl])))
            o_ref[0, :, sl] = jnp.where(j < 2 * per, s * lax.rsqrt(jnp.sum(s * s, axis=1, keepdims=True) + NORM_EPS), s)

    return pl.pallas_call(
        body,
        out_shape=jax.ShapeDtypeStruct((3, T, GDN_HEADS * GDN_DIM), f32),
        grid=(3 * per,),
        in_specs=[pl.BlockSpec((T, W), lambda j: (0, j)), pl.BlockSpec((CONV_K, W), lambda j: (0, j))],
        out_specs=pl.BlockSpec((1, T, W), lambda j: (j // per, 0, j % per)),
        scratch_shapes=[pltpu.VMEM((T + 2 * SHIFT_PAD, GDN_DIM), f32)],
        compiler_params=_cparams(("parallel",)),
        name="gdn_prep",
    )(proj, conv_w)


def _gdn_prep_bwd(proj, conv_w, dqkv_f, dqkv_b, dproj):
    T = proj.shape[0]
    hps, W = PREP_HEADS_BWD, PREP_HEADS_BWD * GDN_DIM
    per = GDN_HEADS // hps

    def body(u_ref, cw_ref, df_ref, db_ref, _, du_ref, dcw_ref, upad_ref, dcpad_ref):
        j = pl.program_id(0)
        for i in range(hps):
            sl = slice(i * GDN_DIM, (i + 1) * GDN_DIM)
            u, cw = _mxr(u_ref[:, sl]), _mxr(cw_ref[:, sl])
            ush = _shifted(u, upad_ref)
            c = _conv5(u, ush, cw)
            s = _silu(c)
            dn = df_ref[0, :, sl] + db_ref[0, :, sl]
            rinv = lax.rsqrt(jnp.sum(s * s, axis=1, keepdims=True) + NORM_EPS)
            ds_norm = rinv * dn - s * (rinv * rinv * rinv) * jnp.sum(dn * s, axis=1, keepdims=True)
            ds = jnp.where(j < 2 * per, ds_norm, dn)
            dc = _mxr(ds * _silu_grad(c))
            dcsh = _shifted(dc, dcpad_ref)
            du = dc * cw[2:3, :]
            for jj in range(CONV_K):
                d = jj - 2
                dcw_ref[jj : jj + 1, sl] = jnp.sum(dc * (u if d == 0 else ush[d]), axis=0, keepdims=True)
                if d != 0:
                    du = du + dcsh[-d] * cw[jj : jj + 1, :]
            du_ref[:, sl] = du.astype(du_ref.dtype)

    qspec = pl.BlockSpec((1, T, W), lambda j: (j // per, 0, j % per))
    return pl.pallas_call(
        body,
        out_shape=(jax.ShapeDtypeStruct(dproj.shape, dproj.dtype), jax.ShapeDtypeStruct((CONV_K, 3 * GDN_HEADS * GDN_DIM), f32)),
        grid=(3 * per,),
        in_specs=[pl.BlockSpec((T, W), lambda j: (0, j)), pl.BlockSpec((CONV_K, W), lambda j: (0, j)), qspec, qspec, _ANY_SPEC],
        out_specs=(pl.BlockSpec((T, W), lambda j: (0, j)), pl.BlockSpec((CONV_K, W), lambda j: (0, j))),
        input_output_aliases={4: 0},
        scratch_shapes=[pltpu.VMEM((T + 2 * SHIFT_PAD, GDN_DIM), f32)] * 2,
        compiler_params=_cparams(("parallel",)),
        name="gdn_prep_bwd",
    )(proj, conv_w, dqkv_f, dqkv_b, dproj)


def _chunk_tri(n, rev):
    i, j = _iota2((n, n), 0), _iota2((n, n), 1)
    same = jnp.right_shift(i, 6) == jnp.right_shift(j, 6)
    order = (j >= i) if rev else (j <= i)
    return jnp.where(jnp.logical_and(same, order), 1.0, 0.0).astype(f32)


def _gate_prep(proj, avec, dvec, w2f, w2b, b2f, b2b, tb=256):
    T = proj.shape[0]
    W = GLA_HEADS * GLA_DK

    def body(s_ref, a_ref, d_ref, wf_ref, wb_ref, bf_ref, bb_ref, g_ref, gc_ref, gf_ref, gb_ref):
        small = s_ref[...]
        tri_f, tri_r = _chunk_tri(tb, False), _chunk_tri(tb, True)
        lane = _iota2(small.shape, 1)
        lg = -jnp.exp(a_ref[...]) * _softplus(small + d_ref[...])
        beta = _sigmoid(small)
        g_ref[...] = jnp.where(lane < 16, lg, jnp.where(lane < 32, beta, 0.0))
        lgm = jnp.where(lane < 16, lg, 0.0)
        gc_ref[...] = jnp.where(lane < 8, _exact01(tri_f, lgm), _exact01(tri_r, lgm))
        gkf = -_softplus(-(_mx(small, wf_ref[...], "nn") + bf_ref[...])) * (1.0 / GATE_NORMALIZER)
        gkb = -_softplus(-(_mx(small, wb_ref[...], "nn") + bb_ref[...])) * (1.0 / GATE_NORMALIZER)
        gf_ref[...] = _exact01(tri_f, gkf)
        gb_ref[...] = _exact01(tri_r, gkb)

    blk = pl.BlockSpec((tb, LANES), lambda i: (i, 0))
    row = pl.BlockSpec((1, LANES), lambda i: (0, 0))
    wide = pl.BlockSpec((tb, W), lambda i: (i, 0))
    wsp = pl.BlockSpec((LANES, W), lambda i: (0, 0))
    wrow = pl.BlockSpec((1, W), lambda i: (0, 0))
    return pl.pallas_call(
        body,
        out_shape=(jax.ShapeDtypeStruct((T, LANES), f32), jax.ShapeDtypeStruct((T, LANES), f32),
                   jax.ShapeDtypeStruct((T, W), f32), jax.ShapeDtypeStruct((T, W), f32)),
        grid=(T // tb,),
        in_specs=[pl.BlockSpec((tb, LANES), lambda i: (i, SMALL_OFF // LANES)), row, row, wsp, wsp, wrow, wrow],
        out_specs=(blk, blk, wide, wide),
        compiler_params=_cparams(("parallel",)),
        name="gate_prep",
    )(proj, avec, dvec, w2f, w2b, b2f, b2b)


def _gdn_gates_bwd(proj, avec, dvec, dg_f, dg_b, dsmall_gla, dproj):
    T = proj.shape[0]
    tb = min(ROW_TILE, T)
    pad = N_CAT - SMALL_OFF

    def body(s_ref, a_ref, d_ref, gf_ref, gb_ref, dl_ref, _, ds_ref, da_ref, dd_ref):
        i = pl.program_id(0)
        small = s_ref[...]
        lane = _iota2(small.shape, 1)
        dgate = gf_ref[...] + gb_ref[...]
        z = small + d_ref[...]
        nega = -jnp.exp(a_ref[...])
        dz = dgate * nega * _sigmoid(z)
        beta = _sigmoid(small)
        dsm = jnp.where(lane < 16, dz, jnp.where(lane < 32, dgate * beta * (1.0 - beta), 0.0))
        ds_ref[:, :LANES] = (dsm + dl_ref[...]).astype(ds_ref.dtype)
        ds_ref[:, LANES:] = jnp.zeros((tb, pad - LANES), ds_ref.dtype)
        lg = nega * _softplus(z)
        pa = jnp.sum(jnp.where(lane < 16, dgate * lg, 0.0), axis=0, keepdims=True)
        pd = jnp.sum(jnp.where(lane < 16, dz, 0.0), axis=0, keepdims=True)

        @pl.when(i == 0)
        def _():
            da_ref[...] = pa
            dd_ref[...] = pd

        @pl.when(i > 0)
        def _():
            da_ref[...] += pa
            dd_ref[...] += pd

    blk = pl.BlockSpec((tb, LANES), lambda i: (i, 0))
    row = pl.BlockSpec((1, LANES), lambda i: (0, 0))
    return pl.pallas_call(
        body,
        out_shape=(jax.ShapeDtypeStruct(dproj.shape, dproj.dtype), jax.ShapeDtypeStruct((1, LANES), f32), jax.ShapeDtypeStruct((1, LANES), f32)),
        grid=(T // tb,),
        in_specs=[pl.BlockSpec((tb, LANES), lambda i: (i, SMALL_OFF // LANES)), row, row, blk, blk, blk, _ANY_SPEC],
        out_specs=(pl.BlockSpec((tb, pad), lambda i: (i, DP_SMALL_BLK)), row, row),
        input_output_aliases={6: 0},
        compiler_params=_cparams(("arbitrary",)),
        name="gdn_gates_bwd",
    )(proj, avec, dvec, dg_f, dg_b, dsmall_gla, dproj)


GDN_HB = 8


def _inv_unit_lower(L, eye):
    A = eye - L
    P2 = _split2(L)
    for _ in range(3):
        P2 = _split2(_h3s(P2, P2, "nn"))
        A = A + _h3s(_split2(A), P2, "nn")
    Ph = P2[0]
    for _ in range(2):
        Ph = _dot16(Ph, Ph, "nn").astype(jnp.bfloat16)
        A = A + _dot16(A.astype(jnp.bfloat16), Ph, "nn")
    return A


def _lane_col(blk, idx):
    lane = _iota2(blk.shape, 1)
    return jnp.sum(jnp.where(lane == idx, blk, 0.0), axis=1, keepdims=True)


def _out_norm(o_f, o_b, proj, gate_blk, w, hd, name):
    T, W = o_f.shape
    tb = min(ROW_TILE, T)
    nh = W // hd

    def body(of_ref, ob_ref, z_ref, w_ref, y_ref):
        wv = w_ref[...]
        for i in range(nh):
            sl = slice(i * hd, (i + 1) * hd)
            o = of_ref[:, sl] + ob_ref[:, sl]
            r = lax.rsqrt(jnp.mean(o * o, axis=1, keepdims=True) + NORM_EPS)
            y_ref[:, sl] = (o * r * wv * _silu(z_ref[:, sl])).astype(y_ref.dtype)

    blk = pl.BlockSpec((tb, W), lambda i: (i, 0))
    return pl.pallas_call(
        body,
        out_shape=jax.ShapeDtypeStruct((T, W), MXU_DTYPE),
        grid=(T // tb,),
        in_specs=[blk, blk, pl.BlockSpec((tb, W), lambda i: (i, gate_blk)), pl.BlockSpec((1, hd), lambda i: (0, 0))],
        out_specs=blk,
        compiler_params=_cparams(("parallel",)),
        name=name,
    )(o_f, o_b, proj, w)


def _out_norm_bwd(dy, o_f, o_b, proj, gate_blk, w, hd, name, dproj):
    T, W = o_f.shape
    tb = min(ROW_TILE, T)
    nh = W // hd

    def body(dy_ref, of_ref, ob_ref, z_ref, w_ref, _, do_ref, dz_ref, dw_ref):
        i = pl.program_id(0)
        wv = w_ref[...]
        dw = jnp.zeros((1, hd), f32)
        for a in range(nh):
            sl = slice(a * hd, (a + 1) * hd)
            o = of_ref[:, sl] + ob_ref[:, sl]
            z = z_ref[:, sl]
            dyv = dy_ref[:, sl]
            r = lax.rsqrt(jnp.mean(o * o, axis=1, keepdims=True) + NORM_EPS)
            orr = o * r
            dn = dyv * _silu(z)
            dz_ref[:, sl] = (dyv * orr * wv * _silu_grad(z)).astype(dz_ref.dtype)
            dw = dw + jnp.sum(dn * orr, axis=0, keepdims=True)
            dnw = dn * wv
            do_ref[:, sl] = r * dnw - o * (r * r * r) * jnp.mean(dnw * o, axis=1, keepdims=True)

        @pl.when(i == 0)
        def _():
            dw_ref[...] = dw

        @pl.when(i > 0)
        def _():
            dw_ref[...] += dw

    blk = pl.BlockSpec((tb, W), lambda i: (i, 0))
    row = pl.BlockSpec((1, hd), lambda i: (0, 0))
    return pl.pallas_call(
        body,
        out_shape=(jax.ShapeDtypeStruct((T, W), f32), jax.ShapeDtypeStruct(dproj.shape, dproj.dtype), jax.ShapeDtypeStruct((1, hd), f32)),
        grid=(T // tb,),
        in_specs=[blk, blk, blk, pl.BlockSpec((tb, W), lambda i: (i, gate_blk)), row, _ANY_SPEC],
        out_specs=(blk, pl.BlockSpec((tb, W), lambda i: (i, gate_blk)), row),
        input_output_aliases={5: 1},
        compiler_params=_cparams(("arbitrary",)),
        name=name,
    )(dy, o_f, o_b, proj, w, dproj)


def _gla_prep_bwd(proj, w2f, w2b, b2f, b2b, dgk_f, dgk_b, dqkv, dproj):
    T = proj.shape[0]
    tb = min(ROW_TILE, T)
    W = GLA_HEADS * GLA_DK
    WV = GLA_HEADS * GLA_DV

    def body(s_ref, wf_ref, wb_ref, bf_ref, bb_ref, df_ref, db_ref, qf, qb, kf, kb, vf, vb, _,
             dp_ref, ds_ref, dwf_ref, dwb_ref, dbf_ref, dbb_ref):
        i = pl.program_id(0)
        dp_ref[:, :W] = (qf[...] + qb[...]).astype(dp_ref.dtype)
        dp_ref[:, W : 2 * W] = (kf[...] + kb[...]).astype(dp_ref.dtype)
        dp_ref[:, 2 * W :] = (vf[...] + vb[...]).astype(dp_ref.dtype)
        small = s_ref[...]
        dsm = jnp.zeros((tb, LANES), f32)
        parts = []
        for w_ref, b_ref, d_ref in ((wf_ref, bf_ref, df_ref), (wb_ref, bb_ref, db_ref)):
            pre = _mx(small, w_ref[...], "nn") + b_ref[...]
            dpre = d_ref[...] * (1.0 / GATE_NORMALIZER) * _sigmoid(-pre)
            dsm = dsm + _mx(dpre, w_ref[...], "nt")
            parts.append((_mx(small, dpre, "tn"), jnp.sum(dpre, axis=0, keepdims=True)))
        ds_ref[...] = dsm

        @pl.when(i == 0)
        def _():
            dwf_ref[...], dbf_ref[...] = parts[0]
            dwb_ref[...], dbb_ref[...] = parts[1]

        @pl.when(i > 0)
        def _():
            dwf_ref[...] += parts[0][0]
            dbf_ref[...] += parts[0][1]
            dwb_ref[...] += parts[1][0]
            dbb_ref[...] += parts[1][1]

    blk = pl.BlockSpec((tb, W), lambda i: (i, 0))
    vblk = pl.BlockSpec((tb, WV), lambda i: (i, 0))
    wsp = pl.BlockSpec((LANES, W), lambda i: (0, 0))
    row = pl.BlockSpec((1, W), lambda i: (0, 0))
    return pl.pallas_call(
        body,
        out_shape=(jax.ShapeDtypeStruct(dproj.shape, dproj.dtype), jax.ShapeDtypeStruct((T, LANES), f32), jax.ShapeDtypeStruct((LANES, W), f32),
                   jax.ShapeDtypeStruct((LANES, W), f32), jax.ShapeDtypeStruct((1, W), f32), jax.ShapeDtypeStruct((1, W), f32)),
        grid=(T // tb,),
        in_specs=[pl.BlockSpec((tb, LANES), lambda i: (i, SMALL_OFF // LANES)), wsp, wsp, row, row, blk, blk,
                  blk, blk, blk, blk, vblk, vblk, _ANY_SPEC],
        out_specs=(pl.BlockSpec((tb, 2 * W + WV), lambda i: (i, DP_GLA_QKV_BLK)), pl.BlockSpec((tb, LANES), lambda i: (i, 0)), wsp, wsp, row, row),
        input_output_aliases={13: 0},
        compiler_params=_cparams(("arbitrary",)),
        name="gla_prep_bwd",
    )(proj, w2f, w2b, b2f, b2b, dgk_f, dgk_b, *dqkv, dproj)


GATE_A_BLK, GATE_B_BLK, Z_BLK, G_B_BLK = 6144 // 1024, 7168 // 1024, 3072 // 1024, 8192 // 1024
DP_GATES_BLK, DP_GLA_QKV_BLK, DP_SMALL_BLK = 6144 // 2048, 4096 // 2048, SMALL_OFF // 512
_ANY_SPEC = pl.BlockSpec(memory_space=pl.ANY)


def _mid(og, obg, proj, x, target, wpg, wpl, wout, w_post, tb=256):
    T = x.shape[0]
    Dm = D_MODEL

    def body(og_ref, obg_ref, ga_ref, gb_ref, x_ref, t_ref, wpg_ref, wpl_ref, wo_ref, wp_ref,
             loss_ref, dwp_ref, dy_ref, m_ref, dout_ref, dya_ref, dyb_ref, dgates_ref, dog_ref, dobg_ref):
        i = pl.program_id(0)
        ya = _mx(og_ref[...], wpg_ref[...], "nn")
        yb = _mx(obg_ref[...], wpl_ref[...], "nn")
        sa, sb = _sigmoid(ga_ref[...]), _sigmoid(gb_ref[...])
        m = sa * ya + sb * yb
        m_ref[...] = m.astype(m_ref.dtype)
        out = _mx(m, wo_ref[...], "nn")
        r = lax.rsqrt(jnp.mean(out * out, axis=1, keepdims=True) + NORM_EPS)
        wp = wp_ref[...]
        e = x_ref[...] + out * r * wp - t_ref[...]
        dy = e * (1.0 / Dm)
        dy_ref[...] = dy
        lpart = jnp.sum(e * e, axis=0, keepdims=True)
        wpart = jnp.sum(dy * out * r, axis=0, keepdims=True)
        dyn = dy * wp
        dout = r * dyn - out * (r * r * r) * jnp.mean(dyn * out, axis=1, keepdims=True)
        dout_ref[...] = dout.astype(dout_ref.dtype)
        dm = _mx(dout, wo_ref[...], "nt")
        dya, dyb = dm * sa, dm * sb
        dya_ref[...] = dya.astype(dya_ref.dtype)
        dyb_ref[...] = dyb.astype(dyb_ref.dtype)
        dgates_ref[:, :Dm] = (dm * ya * sa * (1.0 - sa)).astype(dgates_ref.dtype)
        dgates_ref[:, Dm:] = (dm * yb * sb * (1.0 - sb)).astype(dgates_ref.dtype)
        dog_ref[...] = _mx(dya, wpg_ref[...], "nt")
        dobg_ref[...] = _mx(dyb, wpl_ref[...], "nt")

        @pl.when(i == 0)
        def _():
            loss_ref[...] = lpart
            dwp_ref[...] = wpart

        @pl.when(i > 0)
        def _():
            loss_ref[...] += lpart
            dwp_ref[...] += wpart

    blk = pl.BlockSpec((tb, Dm), lambda i: (i, 0))
    row = pl.BlockSpec((1, Dm), lambda i: (0, 0))
    wsp = pl.BlockSpec((Dm, Dm), lambda i: (0, 0))
    act = jax.ShapeDtypeStruct((T, Dm), MXU_DTYPE)
    big = jax.ShapeDtypeStruct((T, Dm), f32)
    vec = jax.ShapeDtypeStruct((1, Dm), f32)
    return pl.pallas_call(
        body,
        out_shape=(vec, vec, big, act, act, act, act, jax.ShapeDtypeStruct((T, N_CAT), MXU_DTYPE), big, big),
        grid=(T // tb,),
        in_specs=[blk, blk, pl.BlockSpec((tb, Dm), lambda i: (i, GATE_A_BLK)), pl.BlockSpec((tb, Dm), lambda i: (i, GATE_B_BLK)),
                  blk, blk, wsp, wsp, wsp, row],
        out_specs=(row, row) + (blk,) * 5 + (pl.BlockSpec((tb, 2 * Dm), lambda i: (i, DP_GATES_BLK)), blk, blk),
        compiler_params=_cparams(("arbitrary",)),
        name="mid",
    )(og, obg, proj, proj, x, target, wpg, wpl, wout, w_post)


def _heads(ref, lead, n, width, rows=slice(None)):
    return [ref[lead + (rows, slice(i * width, (i + 1) * width))] for i in range(n)]


class _GdnBatch:
    def __init__(self, q, k, v, gcol, grow, beta, nf, saved=None, sol=None):
        B, Cn = q.shape[0], q.shape[1]
        self.incl, self.strict, eye = _dir_masks(B, nf, Cn)
        A = None
        if saved is None:
            self.decay = jnp.where(self.incl, jnp.exp(jnp.where(self.incl, gcol - grow, 0.0)), 0.0)
        else:
            A, self.L, self.attn, self.decay = (saved[:, i * Cn : (i + 1) * Cn] for i in range(4))
        self.k, self.v = k, v
        gl = _last_row(gcol, nf)
        self.beta = jnp.broadcast_to(beta, q.shape)
        self.eg = jnp.broadcast_to(jnp.exp(gcol), q.shape)
        self.ekd = jnp.broadcast_to(jnp.exp(gl - gcol), q.shape)
        self.qs = q * (GDN_DIM**-0.5)
        self.kb = k * self.beta
        if saved is None:
            self.L = jnp.where(self.strict, _mx(self.kb, k, "nt") * self.decay, 0.0)
            self.attn = jnp.where(self.incl, _mx(self.qs, k, "nt") * self.decay, 0.0)
            A = _inv_unit_lower(self.L, jnp.where(eye, 1.0, 0.0).astype(f32))
        self.A2 = _split2(A)
        self.saved = jnp.concatenate([A, self.L, self.attn, self.decay], axis=1) if saved is None else saved
        self.kbg = self.kb * self.eg
        if sol is None:
            sol = _h3s(self.A2, _split2(jnp.concatenate([v * self.beta, self.kbg], axis=2)), "nn")
        self.sol = sol
        self.u = self.sol[:, :, :GDN_DIM]
        self.w = self.sol[:, :, GDN_DIM:]
        self.qd = self.qs * self.eg
        self.kd = k * self.ekd
        self.egl = jnp.exp(gl)


def _gdn_chain_inputs(qf_ref, qb_ref, gf_ref, gb_ref, cf_ref, cb_ref, b, hb, need_rows=True):
    qkv = [jnp.stack(_heads(qf_ref, (i,), hb, GDN_DIM) + _heads(qb_ref, (i,), hb, GDN_DIM)) for i in range(3)]
    cols, rows, betas = [], [], []
    for d, (g_ref, c_ref) in enumerate(((gf_ref, cf_ref), (gb_ref, cb_ref))):
        gates, gc = g_ref[...], c_ref[...]
        for hh in range(hb):
            h = b * hb + hh
            col = _lane_col(gc, h + 8 * d)
            cols.append(col)
            if need_rows:
                rows.append(jnp.broadcast_to(col, (CHUNK, LANES)).T[0:1, :])
            betas.append(_lane_col(gates, 16 + h + 8 * d))
    return qkv[0], qkv[1], qkv[2], jnp.stack(cols), (jnp.stack(rows) if need_rows else None), jnp.stack(betas)


GDN_CPS_FWD, GDN_CPS_BWD = 2, 1


def _chunk_views(refs, s_f, s_b):
    out = []
    for i, r in enumerate(refs):
        s = s_f if i % 2 == 0 else s_b
        rows = pl.ds(s * CHUNK, CHUNK)
        out.append(r.at[pl.ds(s, 1)] if len(r.shape) == 4 else (r.at[:, rows, :] if len(r.shape) == 3 else r.at[rows, :]))
    return out


def _gdn_specs(cps, hb):
    R, W = cps * CHUNK, hb * GDN_DIM
    qsp = lambda ci: pl.BlockSpec((3, R, W), lambda n, b: (0, ci(n), b))
    gsp = lambda ci: pl.BlockSpec((R, LANES), lambda n, b: (ci(n), 0))
    osp = lambda ci: pl.BlockSpec((R, W), lambda n, b: (ci(n), b))
    ssp = lambda ci: pl.BlockSpec((cps, hb, GDN_DIM, GDN_DIM), lambda n, b: (ci(n), b, 0, 0))
    asp = lambda ci: pl.BlockSpec((cps, hb, 4 * CHUNK, CHUNK), lambda n, b: (ci(n), b, 0, 0))
    usp = lambda ci: pl.BlockSpec((cps, hb, CHUNK, 2 * GDN_DIM), lambda n, b: (ci(n), b, 0, 0))
    return qsp, gsp, osp, ssp, asp, usp


def _gdn_fwd(qkvn, gates, gc):
    T = qkvn.shape[1]
    N = T // CHUNK
    nh, hb = GDN_HEADS, GDN_HB
    cps = min(GDN_CPS_FWD, N)
    NB = N // cps

    def body(*refs):
        n, b = pl.program_id(0), pl.program_id(1)
        s_scr = refs[-1]

        @pl.when(n == 0)
        def _():
            s_scr[0, pl.ds(b * hb, hb)] = jnp.zeros((hb, GDN_DIM, GDN_DIM), f32)
            s_scr[1, pl.ds(b * hb, hb)] = jnp.zeros((hb, GDN_DIM, GDN_DIM), f32)

        for s in range(cps):
            chunk(*_chunk_views(refs[:-1], s, cps - 1 - s), s_scr, b)

    def chunk(qf_ref, qb_ref, gf_ref, gb_ref, cf_ref, cb_ref, of_ref, ob_ref, sf_ref, sb_ref, af_ref, ab_ref, uf_ref, ub_ref, s_scr, b):
        hs = pl.ds(b * hb, hb)
        q, k, v, gcol, grow, beta = _gdn_chain_inputs(qf_ref, qb_ref, gf_ref, gb_ref, cf_ref, cb_ref, b, hb)
        ck = _GdnBatch(q, k, v, gcol, grow, beta, hb)
        S = jnp.concatenate([s_scr[0, hs], s_scr[1, hs]], axis=0)
        sf_ref[0], sb_ref[0] = S[:hb], S[hb:]
        af_ref[0], ab_ref[0] = ck.saved[:hb], ck.saved[hb:]
        uf_ref[0], ub_ref[0] = ck.sol[:hb], ck.sol[hb:]
        v_new = ck.u - _mx(ck.w, S, "nn")
        o = _mx(ck.qd, S, "nn") + _mx(ck.attn, v_new, "nn")
        S_new = S * ck.egl + _mx(ck.kd, v_new, "tn")
        for hh in range(hb):
            sl = slice(hh * GDN_DIM, (hh + 1) * GDN_DIM)
            of_ref[:, sl] = o[hh]
            ob_ref[:, sl] = o[hb + hh]
        s_scr[0, hs] = S_new[:hb]
        s_scr[1, hs] = S_new[hb:]

    fw, bw = (lambda n: n), (lambda n: NB - 1 - n)
    qsp, gsp, osp, ssp, asp, usp = _gdn_specs(cps, hb)
    o_shape = jax.ShapeDtypeStruct((T, nh * GDN_DIM), f32)
    s_shape = jax.ShapeDtypeStruct((N, nh, GDN_DIM, GDN_DIM), f32)
    a_shape = jax.ShapeDtypeStruct((N, nh, 4 * CHUNK, CHUNK), f32)
    u_shape = jax.ShapeDtypeStruct((N, nh, CHUNK, 2 * GDN_DIM), f32)
    return pl.pallas_call(
        body,
        out_shape=(o_shape, o_shape, s_shape, s_shape, a_shape, a_shape, u_shape, u_shape),
        grid=(NB, nh // hb),
        in_specs=[qsp(fw), qsp(bw), gsp(fw), gsp(bw), gsp(fw), gsp(bw)],
        out_specs=(osp(fw), osp(bw), ssp(fw), ssp(bw), asp(fw), asp(bw), usp(fw), usp(bw)),
        scratch_shapes=[pltpu.VMEM((2, nh, GDN_DIM, GDN_DIM), f32)],
        compiler_params=_cparams(("arbitrary", "arbitrary")),
        name="gdn_scan_fwd",
    )(qkvn, qkvn, gates, gates, gc, gc)


def _gdn_bwd(qkvn, gates, gc, st_f, st_b, a_f, a_b, u_f, u_b, do):
    T = qkvn.shape[1]
    N = T // CHUNK
    nh, hb = GDN_HEADS, GDN_HB
    nb = nh // hb
    cps = min(GDN_CPS_BWD, N)
    NB = N // cps

    def body(*refs):
        n, b = pl.program_id(0), pl.program_id(1)
        ds_scr = refs[-1]

        @pl.when(n == 0)
        def _():
            ds_scr[0, pl.ds(b * hb, hb)] = jnp.zeros((hb, GDN_DIM, GDN_DIM), f32)
            ds_scr[1, pl.ds(b * hb, hb)] = jnp.zeros((hb, GDN_DIM, GDN_DIM), f32)

        for s in range(cps):
            chunk(*_chunk_views(refs[:-1], cps - 1 - s, s), ds_scr, b)

    def chunk(qf_ref, qb_ref, gf_ref, gb_ref, cf_ref, cb_ref, sf_ref, sb_ref, af_ref, ab_ref, uf_ref, ub_ref, dof_ref, dob_ref,
              dqf_ref, dqb_ref, dgf_ref, dgb_ref, ds_scr, b):
        hs = pl.ds(b * hb, hb)

        @pl.when(b == 0)
        def _():
            dgf_ref[...] = jnp.zeros((CHUNK, LANES), f32)
            dgb_ref[...] = jnp.zeros((CHUNK, LANES), f32)

        q, k, v, gcol, grow, beta = _gdn_chain_inputs(qf_ref, qb_ref, gf_ref, gb_ref, cf_ref, cb_ref, b, hb, need_rows=False)
        ck = _GdnBatch(q, k, v, gcol, grow, beta, hb, saved=jnp.concatenate([af_ref[0], ab_ref[0]], axis=0),
                       sol=jnp.concatenate([uf_ref[0], ub_ref[0]], axis=0))
        S = jnp.concatenate([sf_ref[0], sb_ref[0]], axis=0)
        dS = jnp.concatenate([ds_scr[0, hs], ds_scr[1, hs]], axis=0)
        dov = jnp.stack(_heads(dof_ref, (), hb, GDN_DIM) + _heads(dob_ref, (), hb, GDN_DIM))
        v_new = ck.u - _mx(ck.w, S, "nn")
        d_vnew = _mx(ck.attn, dov, "tn") + _mx(ck.kd, dS, "nn")
        d_attn = jnp.where(ck.incl, _mx(dov, v_new, "nt"), 0.0)
        d_qd = _mx(dov, S, "nt")
        d_kd = _mx(v_new, dS, "nt")
        d_gl = jnp.sum(jnp.sum(dS * S, axis=2, keepdims=True), axis=1, keepdims=True) * ck.egl
        dS_new = dS * ck.egl + _mx(ck.qd, dov, "tn") - _mx(ck.w, d_vnew, "tn")
        ds_scr[0, hs] = dS_new[:hb]
        ds_scr[1, hs] = dS_new[hb:]
        d_w = -_mx(d_vnew, S, "nt")
        d_rhs = _h3s(ck.A2, _split2(jnp.concatenate([d_vnew, d_w], axis=2)), "tn")
        d_vb, d_kbg = d_rhs[:, :, :GDN_DIM], d_rhs[:, :, GDN_DIM:]
        dL = -jnp.where(ck.strict, _h3(d_rhs, ck.sol, "nt"), 0.0)
        P = dL * ck.decay
        Q = d_attn * ck.decay
        d_kb = _mx(P, k, "nn") + d_kbg * ck.eg
        d_qs = _mx(Q, k, "nn") + d_qd * ck.eg
        d_k = _mx(P, ck.kb, "tn") + _mx(Q, ck.qs, "tn") + d_kd * ck.ekd + d_kb * ck.beta
        E = dL * ck.L + d_attn * ck.attn
        kdsum = jnp.sum(d_kd * ck.kd, axis=2, keepdims=True)
        d_gc = (jnp.sum(E, axis=2, keepdims=True) - _colsum_col(E)
                + jnp.sum(d_kbg * ck.kbg, axis=2, keepdims=True)
                + jnp.sum(d_qd * ck.qd, axis=2, keepdims=True) - kdsum)
        d_gc = _add_at_last_row(d_gc, d_gl + jnp.sum(kdsum, axis=1, keepdims=True), hb)
        d_beta = jnp.sum(d_vb * v, axis=2, keepdims=True) + jnp.sum(d_kb * k, axis=2, keepdims=True)
        d_q = d_qs * (GDN_DIM**-0.5)
        d_v = d_vb * ck.beta
        lane = _iota2((CHUNK, LANES), 1)
        for d, (dq_ref, dg_ref) in enumerate(((dqf_ref, dgf_ref), (dqb_ref, dgb_ref))):
            acc = dg_ref[...]
            for hh in range(hb):
                c, h = d * hb + hh, b * hb + hh
                sl = slice(hh * GDN_DIM, (hh + 1) * GDN_DIM)
                dq_ref[0, :, sl] = d_q[c]
                dq_ref[1, :, sl] = d_k[c]
                dq_ref[2, :, sl] = d_v[c]
                acc = jnp.where(lane == h + 8 * d, d_gc[c], acc)
                acc = jnp.where(lane == 16 + h + 8 * d, d_beta[c], acc)
            dg_ref[...] = acc

        @pl.when(b == nb - 1)
        def _():
            for d, dg_ref in enumerate((dgf_ref, dgb_ref)):
                blk = dg_ref[...]
                dg_ref[...] = jnp.where(lane < 16, _exact01(_chunk_tri(CHUNK, d == 0), blk), blk)

    fw, bw = (lambda n: NB - 1 - n), (lambda n: n)
    qsp, gsp, osp, ssp, asp, usp = _gdn_specs(cps, hb)
    dq_shape = jax.ShapeDtypeStruct((3, T, nh * GDN_DIM), f32)
    dg_shape = jax.ShapeDtypeStruct((T, LANES), f32)
    return pl.pallas_call(
        body,
        out_shape=(dq_shape, dq_shape, dg_shape, dg_shape),
        grid=(NB, nb),
        in_specs=[qsp(fw), qsp(bw), gsp(fw), gsp(bw), gsp(fw), gsp(bw), ssp(fw), ssp(bw), asp(fw), asp(bw), usp(fw), usp(bw),
                  osp(fw), osp(bw)],
        out_specs=(qsp(fw), qsp(bw), gsp(fw), gsp(bw)),
        scratch_shapes=[pltpu.VMEM((2, nh, GDN_DIM, GDN_DIM), f32)],
        compiler_params=_cparams(("arbitrary", "arbitrary")),
        name="gdn_scan_bwd",
    )(qkvn, qkvn, gates, gates, gc, gc, st_f, st_b, a_f, a_b, u_f, u_b, do, do)


class _GlaBatch:
    def __init__(self, q, k, G, nf):
        B, Cn = q.shape[0], q.shape[1]
        self.incl, _, _ = _dir_masks(B, nf, Cn)
        self.eG = jnp.exp(G)
        self.enG = jnp.exp(-G)
        self.qg = q * (GLA_DK**-0.5) * self.eG
        self.kg = k * self.enG
        self.attn = jnp.where(self.incl, _mx(self.qg, self.kg, "nt"), 0.0)
        gl = _last_row(G, nf)
        self.ekd = jnp.exp(gl - G)
        self.kd = k * self.ekd
        self.egl = jnp.exp(gl)


GLA_CPS = 4


def _chunk_rows(s):
    return slice(s * CHUNK, (s + 1) * CHUNK)


def _gla_chain_inputs(refs_f, refs_b, rows_f, rows_b):
    nh = GLA_HEADS
    out = []
    for i, width in enumerate((GLA_DK, GLA_DK, GLA_DV, GLA_DK)):
        out.append(jnp.stack(_heads(refs_f[i], (), nh, width, rows_f) + _heads(refs_b[i], (), nh, width, rows_b)))
    return out


def _gla_specs(rows, ci):
    nh = GLA_HEADS
    return [
        pl.BlockSpec((rows, nh * GLA_DK), lambda n: (ci(n), 4096 // (nh * GLA_DK))),
        pl.BlockSpec((rows, nh * GLA_DK), lambda n: (ci(n), 4608 // (nh * GLA_DK))),
        pl.BlockSpec((rows, nh * GLA_DV), lambda n: (ci(n), 5120 // (nh * GLA_DV))),
        pl.BlockSpec((rows, nh * GLA_DK), lambda n: (ci(n), 0)),
    ]


def _gla_fwd(proj, G_f, G_b):
    T = proj.shape[0]
    N = T // CHUNK
    nh = GLA_HEADS
    cps = min(GLA_CPS, N)
    NB = N // cps

    def body(qf, kf, vf, gf, qb, kb, vb, gb, of_ref, ob_ref, sf_ref, sb_ref, s_scr):
        n = pl.program_id(0)

        @pl.when(n == 0)
        def _():
            s_scr[...] = jnp.zeros((2 * nh, GLA_DV, GLA_DK), f32)

        for s in range(cps):
            sb = cps - 1 - s
            q, k, v, G = _gla_chain_inputs((qf, kf, vf, gf), (qb, kb, vb, gb), _chunk_rows(s), _chunk_rows(sb))
            ck = _GlaBatch(q, k, G, nh)
            St = s_scr[...]
            sf_ref[s], sb_ref[sb] = St[:nh].astype(sf_ref.dtype), St[nh:].astype(sb_ref.dtype)
            o = _mx(ck.qg, St, "nt") + _mx(ck.attn, v, "nn")
            for i in range(nh):
                sl = slice(i * GLA_DV, (i + 1) * GLA_DV)
                of_ref[_chunk_rows(s), sl] = o[i]
                ob_ref[_chunk_rows(sb), sl] = o[nh + i]
            s_scr[...] = St * ck.egl + _mx(v, ck.kd, "tn")

    fw, bw = (lambda n: n), (lambda n: NB - 1 - n)
    osp = lambda ci: pl.BlockSpec((cps * CHUNK, nh * GLA_DV), lambda n: (ci(n), 0))
    ssp = lambda ci: pl.BlockSpec((cps, nh, GLA_DV, GLA_DK), lambda n: (ci(n), 0, 0, 0))
    o_shape = jax.ShapeDtypeStruct((T, nh * GLA_DV), f32)
    s_shape = jax.ShapeDtypeStruct((N, nh, GLA_DV, GLA_DK), MXU_DTYPE)
    return pl.pallas_call(
        body,
        out_shape=(o_shape, o_shape, s_shape, s_shape),
        grid=(NB,),
        in_specs=_gla_specs(cps * CHUNK, fw) + _gla_specs(cps * CHUNK, bw),
        out_specs=(osp(fw), osp(bw), ssp(fw), ssp(bw)),
        scratch_shapes=[pltpu.VMEM((2 * nh, GLA_DV, GLA_DK), f32)],
        compiler_params=_cparams(("arbitrary",)),
        name="gla_scan_fwd",
    )(proj, proj, proj, G_f, proj, proj, proj, G_b)


def _gla_bwd(proj, G_f, G_b, st_f, st_b, do):
    T = proj.shape[0]
    N = T // CHUNK
    nh = GLA_HEADS
    cps = min(GLA_CPS, N)
    NB = N // cps

    def body(qf, kf, vf, gf, sf_ref, dof_ref, qb, kb, vb, gb, sb_ref, dob_ref,
             dqf_ref, dkf_ref, dvf_ref, dgf_ref, dqb_ref, dkb_ref, dvb_ref, dgb_ref, ds_scr):
        n = pl.program_id(0)

        @pl.when(n == 0)
        def _():
            ds_scr[...] = jnp.zeros((2 * nh, GLA_DV, GLA_DK), f32)

        shp = (2 * nh, CHUNK, CHUNK)
        bb, ii, jj = _iota3(shp, 0), _iota3(shp, 1), _iota3(shp, 2)
        tri = jnp.where(jnp.logical_or(jnp.logical_and(bb < nh, jj >= ii), jnp.logical_and(bb >= nh, jj <= ii)), 1.0, 0.0)
        for s in range(cps):
            sf = cps - 1 - s
            rows = (_chunk_rows(sf), _chunk_rows(s))
            q, k, v, G = _gla_chain_inputs((qf, kf, vf, gf), (qb, kb, vb, gb), *rows)
            ck = _GlaBatch(q, k, G, nh)
            St = jnp.concatenate([sf_ref[sf], sb_ref[s]], axis=0).astype(f32)
            dSt = ds_scr[...]
            dov = jnp.stack(_heads(dof_ref, (), nh, GLA_DV, rows[0]) + _heads(dob_ref, (), nh, GLA_DV, rows[1]))
            d_attn = jnp.where(ck.incl, _mx(dov, v, "nt"), 0.0)
            d_qg = _mx(dov, St, "nn") + _mx(d_attn, ck.kg, "nn")
            d_kg = _mx(d_attn, ck.qg, "tn")
            d_v = _mx(ck.attn, dov, "tn") + _mx(ck.kd, dSt, "nt")
            d_kd = _mx(v, dSt, "nn")
            ds_scr[...] = dSt * ck.egl + _mx(dov, ck.qg, "tn")
            kdd = d_kd * ck.kd
            d_gl = jnp.sum(dSt * St, axis=1, keepdims=True) * ck.egl + jnp.sum(kdd, axis=1, keepdims=True)
            d_q = d_qg * ck.eG * (GLA_DK**-0.5)
            d_k = d_kg * ck.enG + d_kd * ck.ekd
            d_gk = _exact01(tri, _add_at_last_row(d_qg * ck.qg - d_kg * ck.kg - kdd, d_gl, nh))
            for d, (dq_ref, dk_ref, dv_ref, dg_ref) in enumerate(((dqf_ref, dkf_ref, dvf_ref, dgf_ref), (dqb_ref, dkb_ref, dvb_ref, dgb_ref))):
                for i in range(nh):
                    c = d * nh + i
                    ks, vs = slice(i * GLA_DK, (i + 1) * GLA_DK), slice(i * GLA_DV, (i + 1) * GLA_DV)
                    dq_ref[rows[d], ks] = d_q[c]
                    dk_ref[rows[d], ks] = d_k[c]
                    dv_ref[rows[d], vs] = d_v[c]
                    dg_ref[rows[d], ks] = d_gk[c]

    R = cps * CHUNK

    def in_specs(ci):
        return _gla_specs(R, ci) + [
            pl.BlockSpec((cps, nh, GLA_DV, GLA_DK), lambda n: (ci(n), 0, 0, 0)),
            pl.BlockSpec((R, nh * GLA_DV), lambda n: (ci(n), 0)),
        ]

    def out_specs(ci):
        ksp = pl.BlockSpec((R, nh * GLA_DK), lambda n: (ci(n), 0))
        return [ksp, ksp, pl.BlockSpec((R, nh * GLA_DV), lambda n: (ci(n), 0)), ksp]

    fw, bw = (lambda n: NB - 1 - n), (lambda n: n)
    k_shape = jax.ShapeDtypeStruct((T, nh * GLA_DK), f32)
    v_shape = jax.ShapeDtypeStruct((T, nh * GLA_DV), f32)
    return pl.pallas_call(
        body,
        out_shape=(k_shape, k_shape, v_shape, k_shape) * 2,
        grid=(NB,),
        in_specs=in_specs(fw) + in_specs(bw),
        out_specs=tuple(out_specs(fw) + out_specs(bw)),
        scratch_shapes=[pltpu.VMEM((2 * nh, GLA_DV, GLA_DK), f32)],
        compiler_params=_cparams(("arbitrary",)),
        name="gla_scan_bwd",
    )(proj, proj, proj, G_f, st_f, do, proj, proj, proj, G_b, st_b, do)


def _local_step(x, target, wcat, conv_w, avec, dvec, gdn_norm_w, w2f, w2b, b2f, b2b, gla_norm_w, proj_weights, w_pre, w_post,
                send_proj_grads=None, send_in_grads=None):
    h, h_t = _rms_pre(x, w_pre)
    proj = _matmul(h, wcat, "nt", f32, "proj_in", 2048, 512, 1024)
    qkvn = _gdn_prep(proj, conv_w)
    gates, gc, G_f, G_b = _gate_prep(proj, avec, dvec, w2f, w2b, b2f, b2b)
    oa_f, oa_b, sa_f, sa_b, inv_f, inv_b, sol_f, sol_b = _gdn_fwd(qkvn, gates, gc)
    og = _out_norm(oa_f, oa_b, proj, Z_BLK, gdn_norm_w, GDN_DIM, "gdn_out")
    ob_f, ob_b, sb_f, sb_b = _gla_fwd(proj, G_f, G_b)
    obg = _out_norm(ob_f, ob_b, proj, G_B_BLK, gla_norm_w, GLA_DV, "gla_out")
    wpg, wpl, wout = proj_weights(obg)
    (loss_row, d_wpost, dy, m, dout, dya, dyb, dproj, dog, dobg) = _mid(og, obg, proj, x, target, wpg, wpl, wout, w_post)

    d_wout = _matmul(m, dout, "tn", MXU_DTYPE, "dw_out", 1024, 512, 2048)
    d_wpg = _matmul(og, dya, "tn", MXU_DTYPE, "dw_proj_gdn", 1024, 512, 2048)
    d_wpl = _matmul(obg, dyb, "tn", MXU_DTYPE, "dw_proj_gla", 1024, 512, 2048)
    if send_proj_grads is not None:
        gla_norm_w = gla_norm_w + send_proj_grads(d_wpg, d_wpl, d_wout)

    do_b, dproj, d_gla_norm = _out_norm_bwd(dobg, ob_f, ob_b, proj, G_B_BLK, gla_norm_w, GLA_DV, "gla_out_bwd", dproj)
    dq_f, dk_f, dv_f, dgk_f, dq_b, dk_b, dv_b, dgk_b = _gla_bwd(proj, G_f, G_b, sb_f, sb_b, do_b)
    dproj, dsmall_gla, d_w2f, d_w2b, d_b2f, d_b2b = _gla_prep_bwd(proj, w2f, w2b, b2f, b2b, dgk_f, dgk_b,
                                                                 (dq_f, dq_b, dk_f, dk_b, dv_f, dv_b), dproj)

    do_a, dproj, d_gdn_norm = _out_norm_bwd(dog, oa_f, oa_b, proj, Z_BLK, gdn_norm_w, GDN_DIM, "gdn_out_bwd", dproj)
    dqkv_f, dqkv_b, dg_f, dg_b = _gdn_bwd(qkvn, gates, gc, sa_f, sa_b, inv_f, inv_b, sol_f, sol_b, do_a)
    dproj, d_conv = _gdn_prep_bwd(proj, conv_w, dqkv_f, dqkv_b, dproj)
    dproj, d_alog, d_dtb = _gdn_gates_bwd(proj, avec, dvec, dg_f, dg_b, dsmall_gla, dproj)

    d_wcat = _matmul(h_t, dproj, "nn", MXU_DTYPE, "dw_in", 1024, 512, 2048, out_t=True)
    started = None if send_in_grads is None else send_in_grads(d_wcat)
    dh = _matmul(dproj, wcat, "nn", f32, "dh", 512, 1024, N_CAT, after=started)
    grad_x, d_wpre = _rms_pre_bwd(x, w_pre, dh, dy)
    return dict(loss_row=loss_row, grad_x=grad_x, d_wcat=d_wcat, d_conv=d_conv, d_alog=d_alog, d_dtb=d_dtb,
                d_gdn_norm=d_gdn_norm, d_wpg=d_wpg, d_w2f=d_w2f, d_w2b=d_w2b, d_b2f=d_b2f, d_b2b=d_b2b,
                d_gla_norm=d_gla_norm, d_wpl=d_wpl, d_wout=d_wout, d_wpost=d_wpost, d_wpre=d_wpre)


def _to_cat(wf):
    pad = jnp.zeros((N_CAT - N_IN,) + wf.shape[1:], wf.dtype)
    return jnp.concatenate([wf[:4096], wf[4128:6176], wf[7232:9280], wf[6176:7200], wf[4096:4128], wf[7200:7232], pad], axis=0)


def _from_cat(wc):
    return jnp.concatenate([wc[:4096], wc[9216:9248], wc[4096:6144], wc[8192:9216], wc[9248:9280], wc[6144:8192]], axis=0)


def _lane_row(*pieces):
    row = jnp.concatenate(pieces, axis=1)
    return jnp.pad(row, ((0, 0), (0, LANES - row.shape[1])))


def _pad_w2(w2, row0):
    return jnp.pad(w2, ((row0, LANES - row0 - w2.shape[0]), (0, 0)))


SM_ROWS, SM_COLS = 32, 384


def _pack_small_shard(conv, w2f, w2b):
    lead = conv.shape[:-2]
    z = lambda r, c: jnp.zeros(lead + (r, c), f32)
    top = jnp.concatenate([conv, z(8 - CONV_K, SM_COLS)], axis=-2)
    mid = jnp.concatenate([w2f, w2b, z(16, SM_COLS - 128)], axis=-1)
    return jnp.concatenate([top, mid, z(SM_ROWS - 24, SM_COLS)], axis=-2)


SMALL_REGION = {
    "ln_pre_w": (0, 1, 0, 1024), "ln_post_w": (1, 1, 0, 1024), "gk_b2_fwd": (2, 1, 0, 512), "gk_b2_bwd": (2, 1, 512, 512),
    "gla_norm_w": (3, 1, 0, 256), "gdn_norm_w": (3, 1, 256, 128), "a_log_fwd": (4, 1, 0, 8), "a_log_bwd": (5, 1, 0, 8),
    "dt_bias_fwd": (6, 1, 0, 8), "dt_bias_bwd": (7, 1, 0, 8), "conv_w": (8, CONV_K, 0, SM_COLS),
    "gk_w2_fwd": (16, 16, 0, 64), "gk_w2_bwd": (16, 16, 128, 64),
}
SMALL_NAMES = tuple(SMALL_REGION)
SMALL_SHARDED = ("conv_w", "gk_w2_fwd", "gk_w2_bwd")
PACK_ROWS, PACK_COLS = 32, 1024
LOSS_AT = (13, 0)


def _region(name):
    r0, nr, l0, nl = SMALL_REGION[name]
    return slice(r0, r0 + nr), slice(l0, l0 + nl)


def _pack_small_grads(G):
    names = ("d_wpre", "d_wpost", "d_b2f", "d_b2b", "d_gla_norm", "d_gdn_norm", "d_alog", "d_dtb", "loss_row", "d_conv", "d_w2f", "d_w2b")

    def body(wpre, wpost, b2f, b2b, glan, gdnn, alog, dtb, lrow, conv, w2f, w2b, o_ref):
        o_ref[...] = jnp.zeros(o_ref.shape, f32)
        loss = 0.5 * jnp.sum(lrow[...], axis=1, keepdims=True) * (1.0 / D_MODEL)
        whole = {"ln_pre_w": wpre, "ln_post_w": wpost, "gk_b2_fwd": b2f, "gk_b2_bwd": b2b, "gla_norm_w": glan, "gdn_norm_w": gdnn}
        for j in range(N_DEV):
            for name, ref in whole.items():
                o_ref[(j,) + _region(name)] = ref[...]
            o_ref[(j,) + _region("a_log_fwd")] = alog[0:1, 0:8]
            o_ref[(j,) + _region("a_log_bwd")] = alog[0:1, 8:16]
            o_ref[(j,) + _region("dt_bias_fwd")] = dtb[0:1, 0:8]
            o_ref[(j,) + _region("dt_bias_bwd")] = dtb[0:1, 8:16]
            o_ref[(j,) + _region("conv_w")] = conv[:, j * SM_COLS : (j + 1) * SM_COLS]
            o_ref[(j,) + _region("gk_w2_fwd")] = w2f[32:48, j * 64 : (j + 1) * 64]
            o_ref[(j,) + _region("gk_w2_bwd")] = w2b[48:64, j * 64 : (j + 1) * 64]
            o_ref[j, LOSS_AT[0] : LOSS_AT[0] + 1, LOSS_AT[1] : LOSS_AT[1] + 1] = loss

    return pl.pallas_call(
        body, out_shape=jax.ShapeDtypeStruct((N_DEV, PACK_ROWS, PACK_COLS), f32), name="pack_small_grads",
    )(*[G[n] for n in names])


def _small_adam(land, own, W, M, V):
    bc1 = 1.0 - ADAM_B1**ADAM_STEP
    bc2 = 1.0 - ADAM_B2**ADAM_STEP
    n = len(SMALL_NAMES)

    def body(land_ref, own_ref, *refs):
        w_refs, m_refs, v_refs = refs[0:n], refs[n : 2 * n], refs[2 * n : 3 * n]
        outs, loss_ref, g_scr = refs[3 * n : 7 * n], refs[7 * n], refs[7 * n + 1]
        me = _my_device()
        g = jnp.where(me == 0, own_ref[0], land_ref[0])
        for j in range(1, N_DEV):
            g = g + jnp.where(me == j, own_ref[j], land_ref[j])
        g_scr[...] = g
        loss_ref[...] = g_scr[LOSS_AT[0] : LOSS_AT[0] + 1, LOSS_AT[1] : LOSS_AT[1] + 1]
        for i, name in enumerate(SMALL_NAMES):
            gp = g_scr[_region(name)]
            lead = (0,) if name in SMALL_SHARDED else (Ellipsis,)
            w, m, v = w_refs[i][lead], m_refs[i][lead], v_refs[i][lead]
            m2 = ADAM_B1 * m + (1.0 - ADAM_B1) * gp
            v2 = ADAM_B2 * v + (1.0 - ADAM_B2) * (gp * gp)
            delta = -ADAM_LR * ((m2 / bc1) / (jnp.sqrt(v2 / bc2) + ADAM_EPS) + ADAM_WD * w)
            for k, val in enumerate((gp, delta, m2, v2)):
                outs[4 * i + k][lead] = val

    shapes = [jax.ShapeDtypeStruct(W[name].shape, f32) for name in SMALL_NAMES for _ in range(4)]
    res = pl.pallas_call(
        body,
        out_shape=tuple(shapes) + (jax.ShapeDtypeStruct((1, 1), f32),),
        scratch_shapes=[pltpu.VMEM((PACK_ROWS, PACK_COLS), f32)],
        name="adam_small",
    )(land, own, *[P[name] for P in (W, M, V) for name in SMALL_NAMES])
    return {name: res[4 * i : 4 * i + 4] for i, name in enumerate(SMALL_NAMES)}, res[4 * n]


_MESH = pl.DeviceIdType.MESH


N_BIG_COPIES, N_SMALL_COPIES = 9, 7


def _gather_weights(w_in_s, sm_s):
    shapes = (w_in_s.shape, sm_s.shape)
    dtypes = (MXU_DTYPE, f32)

    def body(win_ref, sm_ref, gin_ref, gsm_ref, send_sems, recv_sems):
        x, y, c = lax.axis_index("x"), lax.axis_index("y"), lax.axis_index("c")
        me, sibling = (x, y, c), (x, y, 1 - c)
        chips = [(1 - x, y), (x, 1 - y), (1 - x, 1 - y)]

        def idx(px, py, pc):
            return 4 * px + 2 * py + pc

        for r in range(w_in_s.shape[1] // LANES):
            cols = slice(r * LANES, (r + 1) * LANES)
            gin_ref[idx(*me), :, cols] = win_ref[:, cols].astype(MXU_DTYPE)
        gsm_ref[idx(*me)] = sm_ref[...]

        def copy(sem, blk, to):
            return pltpu.make_async_remote_copy(src_ref=blk, dst_ref=blk, send_sem=send_sems.at[sem], recv_sem=recv_sems.at[sem],
                                                device_id=to, device_id_type=_MESH)

        def small(k, block, to):
            return copy(N_BIG_COPIES + k, gsm_ref.at[idx(*block)], to)

        half = w_in_s.shape[1] // 2
        xn, yn, dg = (1 - x, y), (x, 1 - y), (1 - x, 1 - y)

        def big(sem, block, to, part=None):
            blk = gin_ref.at[idx(*block)]
            return copy(sem, blk if part is None else blk.at[:, pl.ds(part * half, half)], to)

        started = [big(0, me, sibling), big(1, me, (*xn, c)), big(2, me, (*yn, c)), small(0, me, sibling)]
        started += [small(1 + j, me, (*chip, c)) for j, chip in enumerate(chips)]
        for cp in started:
            cp.start()

        def then(arrived, forwards):
            arrived.wait_recv()
            for cp in forwards:
                cp.start()
            started.extend(forwards)

        then(big(1, (*xn, c), me), [big(3, (*xn, c), (*yn, c), 1), big(5, (*xn, c), sibling)])
        then(big(2, (*yn, c), me), [big(4, (*yn, c), (*xn, c), 0), big(6, (*yn, c), sibling)])
        then(big(4, (*dg, c), me, 0), [big(7, (*dg, c), sibling, 0)])
        then(big(3, (*dg, c), me, 1), [big(8, (*dg, c), sibling, 1)])
        for j, chip in enumerate(chips):
            then(small(1 + j, (*chip, c), me), [small(4 + j, (*chip, c), sibling)])
        big(0, sibling, me).wait_recv()
        big(5, (*xn, 1 - c), me).wait_recv()
        big(6, (*yn, 1 - c), me).wait_recv()
        big(7, (*dg, 1 - c), me, 0).wait_recv()
        big(8, (*dg, 1 - c), me, 1).wait_recv()
        small(0, sibling, me).wait_recv()
        for j, chip in enumerate(chips):
            small(4 + j, (*chip, 1 - c), me).wait_recv()
        for cp in started:
            cp.wait_send()

    vm = pl.BlockSpec(memory_space=pltpu.VMEM)
    return pl.pallas_call(
        body,
        out_shape=tuple(jax.ShapeDtypeStruct((N_DEV,) + s, d) for s, d in zip(shapes, dtypes)),
        in_specs=[vm, vm],
        out_specs=(vm, vm),
        scratch_shapes=[pltpu.SemaphoreType.DMA((N_BIG_COPIES + N_SMALL_COPIES,)), pltpu.SemaphoreType.DMA((N_BIG_COPIES + N_SMALL_COPIES,))],
        compiler_params=pltpu.CompilerParams(vmem_limit_bytes=VMEM_LIMIT),
        name="gather_weights",
    )(w_in_s, sm_s)


N_CHIP = 4
_EFFECT = pltpu.SideEffectType.DATAFLOW_SIDE_EFFECTING
_HBM_SPEC = pl.BlockSpec(memory_space=pltpu.HBM)
_SEM_SPEC = pl.BlockSpec(memory_space=pltpu.SEMAPHORE)


def _relation_peer(k, x, y, c):
    px = (1 - x) if (k & 4) else x
    py = (1 - y) if (k & 2) else y
    pc = (1 - c) if (k & 1) else c
    return (px, py, pc), 4 * px + 2 * py + pc


_PLAN_COPIES = {"gather": N_DEV - 1, "blocks": N_DEV - 1, "sibling": 4, "chips": 3}
_PLAN_SLOTS = {"gather": N_DEV, "blocks": N_DEV, "sibling": 4, "chips": 4}


def _plan_copies(plan, src_ref, land_ref, send_sems, recv_sems, arrival):
    x, y, c = lax.axis_index("x"), lax.axis_index("y"), lax.axis_index("c")
    me, my_chip = 4 * x + 2 * y + c, 2 * x + y
    out = []

    def add(i, src, there, here, dev):
        out.append(pltpu.make_async_remote_copy(src_ref=src, dst_ref=land_ref.at[here if arrival else there], send_sem=send_sems.at[i],
                                                recv_sem=recv_sems.at[i], device_id=dev, device_id_type=_MESH))

    if plan in ("gather", "blocks"):
        for k in range(1, N_DEV):
            dev, p = _relation_peer(k, x, y, c)
            add(k - 1, src_ref.at[p] if plan == "blocks" else src_ref, me, p, dev)
    elif plan == "sibling":
        for j in range(4):
            add(j, src_ref.at[2 * j + 1 - c], j, j, (x, y, 1 - c))
    else:
        for i, k in enumerate((2, 4, 6)):
            (px, py, pc), _ = _relation_peer(k, x, y, c)
            add(i, src_ref.at[2 * px + py], my_chip, 2 * px + py, (px, py, pc))
    return out


def _exchange_start(src, plan, name, after=None):
    n = _PLAN_COPIES[plan]
    land = lax.empty((_PLAN_SLOTS[plan],) + (src.shape if plan == "gather" else src.shape[1:]), src.dtype)
    extra = [] if after is None else [after]

    def body(src_ref, land_ref, *rest):
        send_sems, recv_sems, _, _, token = rest[len(extra) :]
        for cp in _plan_copies(plan, src_ref, land_ref, send_sems, recv_sems, False):
            cp.start()
        token[...] = jnp.zeros_like(token)

    return pl.pallas_call(
        body,
        name=name,
        out_shape=(pltpu.SemaphoreType.DMA((n,)), pltpu.SemaphoreType.DMA((n,)), pltpu.HBM(src.shape, src.dtype),
                   pltpu.HBM(land.shape, land.dtype), jax.ShapeDtypeStruct((8, LANES), f32)),
        in_specs=(_HBM_SPEC, _HBM_SPEC) + (_ANY_SPEC,) * len(extra),
        out_specs=(_SEM_SPEC, _SEM_SPEC, _HBM_SPEC, _HBM_SPEC, pl.BlockSpec(memory_space=pltpu.VMEM)),
        input_output_aliases={0: 2, 1: 3},
        compiler_params=pltpu.CompilerParams(has_side_effects=_EFFECT),
    )(pltpu.with_memory_space_constraint(src, pltpu.HBM), pltpu.with_memory_space_constraint(land, pltpu.HBM), *extra)


def _exchange_wait(started, after, plan, name):
    send_sems, recv_sems, src_thru, land_thru, _ = started

    def body(src_ref, land_ref, send_sems, recv_sems, after_ref, src_dead, got_ref):
        for cp in _plan_copies(plan, src_ref, land_ref, send_sems, recv_sems, True):
            cp.wait_send()
            cp.wait_recv()

    return pl.pallas_call(
        body,
        name=name,
        out_shape=(pltpu.HBM(src_thru.shape, src_thru.dtype), pltpu.HBM(land_thru.shape, land_thru.dtype)),
        in_specs=(_HBM_SPEC, _HBM_SPEC, _SEM_SPEC, _SEM_SPEC, pl.BlockSpec(memory_space=pl.ANY)),
        out_specs=(_HBM_SPEC, _HBM_SPEC),
        input_output_aliases={0: 0, 1: 1},
        compiler_params=pltpu.CompilerParams(has_side_effects=_EFFECT),
    )(src_thru, land_thru, send_sems, recv_sems, after)


def _pair_sum(g, r, name, tc=LANES):
    _, R, Cc = g.shape

    def body(g_ref, r_ref, o_ref):
        o_ref[...] = (g_ref[...].astype(f32) + r_ref[...].astype(f32)).astype(o_ref.dtype)

    return pl.pallas_call(
        body,
        out_shape=jax.ShapeDtypeStruct(r.shape, r.dtype),
        grid=(N_CHIP, Cc // tc),
        in_specs=[pl.BlockSpec((1, R, tc), lambda k, i: (2 * k + lax.axis_index("c"), 0, i)), pl.BlockSpec((1, R, tc), lambda k, i: (k, 0, i))],
        out_specs=pl.BlockSpec((1, R, tc), lambda k, i: (k, 0, i)),
        compiler_params=_cparams(("parallel", "parallel")),
        name=name,
    )(g, r)


def _sum_adam(parts, w, m, v, name, tb, tc=None, own=None, own_slot=None):
    R, Cc = w.shape
    n_parts = parts.shape[0]
    tb = R if tc else min(tb, R)
    tc = tc or Cc
    assert R % tb == 0 and Cc % tc == 0
    bc1 = 1.0 - ADAM_B1**ADAM_STEP
    bc2 = 1.0 - ADAM_B2**ADAM_STEP
    extra = [] if own is None else [own]

    def body(p_ref, w_ref, m_ref, v_ref, *rest):
        g_ref, d_ref, nm_ref, nv_ref = rest[len(extra) :]

        def part(j):
            pj = p_ref[j].astype(f32)
            return pj if own is None else jnp.where(own_slot() == j, rest[0][0].astype(f32), pj)

        g = part(0)
        for j in range(1, n_parts):
            g = g + part(j)
        g_ref[...] = g
        m2 = ADAM_B1 * m_ref[...] + (1.0 - ADAM_B1) * g
        v2 = ADAM_B2 * v_ref[...] + (1.0 - ADAM_B2) * (g * g)
        nm_ref[...] = m2
        nv_ref[...] = v2
        d_ref[...] = -ADAM_LR * ((m2 / bc1) / (jnp.sqrt(v2 / bc2) + ADAM_EPS) + ADAM_WD * w_ref[...])

    blk = pl.BlockSpec((tb, tc), lambda i, j: (i, j))
    o = jax.ShapeDtypeStruct((R, Cc), f32)
    return pl.pallas_call(
        body,
        out_shape=(o, o, o, o),
        grid=(R // tb, Cc // tc),
        in_specs=[pl.BlockSpec((n_parts, tb, tc), lambda i, j: (0, i, j)), blk, blk, blk]
        + [pl.BlockSpec((1, tb, tc), lambda i, j: (own_slot(), i, j))] * len(extra),
        out_specs=(blk, blk, blk, blk),
        compiler_params=_cparams(("parallel", "parallel")),
        name=name,
    )(parts, w, m, v, *extra)


def _my_device():
    return 4 * lax.axis_index("x") + 2 * lax.axis_index("y") + lax.axis_index("c")


def _my_chip():
    return 2 * lax.axis_index("x") + lax.axis_index("y")


_WEIGHTS = ["ln_pre_w", "w_in", "conv_w", "a_log_fwd", "a_log_bwd", "dt_bias_fwd", "dt_bias_bwd", "gdn_norm_w", "w_proj_gdn",
            "gk_w2_fwd", "gk_b2_fwd", "gk_w2_bwd", "gk_b2_bwd", "gla_norm_w", "w_proj_gla", "w_out", "ln_post_w"]


def kernel(x, ln_pre_w, w_in, conv_w, a_log_fwd, a_log_bwd, dt_bias_fwd, dt_bias_bwd, gdn_norm_w, w_proj_gdn, gk_w2_fwd, gk_b2_fwd, gk_w2_bwd, gk_b2_bwd, gla_norm_w, w_proj_gla, w_out, ln_post_w, loss_target, m_ln_pre_w, m_w_in, m_conv_w, m_a_log_fwd, m_a_log_bwd, m_dt_bias_fwd, m_dt_bias_bwd, m_gdn_norm_w, m_w_proj_gdn, m_gk_w2_fwd, m_gk_b2_fwd, m_gk_w2_bwd, m_gk_b2_bwd, m_gla_norm_w, m_w_proj_gla, m_w_out, m_ln_post_w, v_ln_pre_w, v_w_in, v_conv_w, v_a_log_fwd, v_a_log_bwd, v_dt_bias_fwd, v_dt_bias_bwd, v_gdn_norm_w, v_w_proj_gdn, v_gk_w2_fwd, v_gk_b2_fwd, v_gk_w2_bwd, v_gk_b2_bwd, v_gla_norm_w, v_w_proj_gla, v_w_out, v_ln_post_w):
    args = locals()
    W = {n: args[n] for n in _WEIGHTS}
    M = {n: args["m_" + n] for n in _WEIGHTS}
    V = {n: args["v_" + n] for n in _WEIGHTS}

    wp_stack = lambda P: jnp.concatenate([P["w_proj_gdn"], P["w_proj_gla"], P["w_out"]], axis=0)
    sm_pack = lambda P: _pack_small_shard(P["conv_w"][0], P["gk_w2_fwd"][0], P["gk_w2_bwd"][0])
    me = 4 * lax.axis_index("x") + 2 * lax.axis_index("y") + lax.axis_index("c")
    w_in_t, m_in_t, v_in_t = (jnp.transpose(a[0]) for a in (w_in, m_w_in, v_w_in))
    g_in, g_sm = _gather_weights(w_in_t, sm_pack(W))
    wp_own = wp_stack(W).astype(MXU_DTYPE)
    wp_started = _exchange_start(wp_own, "gather", "gather_proj_start", after=g_sm)

    def proj_weights(after):
        own, land = _exchange_wait(wp_started, after, "gather", "gather_proj_wait")
        g_p = lax.dynamic_update_slice(land, own[None], (me, 0, 0, 0))
        return tuple(g_p[:, i].reshape(D_MODEL, D_MODEL) for i in range(3))

    rows_p = 3 * D_MODEL // N_DEV
    sent = {}

    def send_proj_grads(d_wpg, d_wpl, d_wout):
        p_p = jnp.stack([d_wpg, d_wpl, d_wout]).reshape(3, N_DEV, D_MODEL // N_DEV, D_MODEL).transpose(1, 0, 2, 3)
        sent["p_p"] = p_p.reshape(N_DEV, rows_p, D_MODEL)
        sent["started"] = _exchange_start(sent["p_p"], "blocks", "exchange_proj_start")
        return sent["started"][4][0:1, 0:1]

    def send_in_grads(d_wcat):
        p_in = _from_cat(d_wcat).reshape(N_DEV, SHARD_IN, D_MODEL)
        sib = _exchange_start(p_in, "sibling", "exchange_in_sibling_start")
        p_p, land_p = _exchange_wait(sent["started"], sib[4], "blocks", "exchange_proj_wait")
        sent["o_p"] = _sum_adam(land_p, wp_stack(W).reshape(rows_p, D_MODEL), wp_stack(M).reshape(rows_p, D_MODEL),
                                wp_stack(V).reshape(rows_p, D_MODEL), "adam_w_proj", 128, own=p_p, own_slot=_my_device)
        p_in, sib_land = _exchange_wait(sib, sent["o_p"][0], "sibling", "exchange_in_sibling_wait")
        sent["chips"] = _exchange_start(_pair_sum(p_in, sib_land, "pair_sum_w_in", tc=512), "chips", "exchange_in_chips_start")
        return sent["chips"][4]

    wcat = _to_cat(g_in.reshape(N_IN, D_MODEL))
    conv_full = g_sm[:, 0:CONV_K, :].transpose(1, 0, 2).reshape(CONV_K, N_DEV * SM_COLS)
    w2f_full = g_sm[:, 8:24, 0:64].transpose(1, 0, 2).reshape(16, 512)
    w2b_full = g_sm[:, 8:24, 64:128].transpose(1, 0, 2).reshape(16, 512)

    avec = _lane_row(a_log_fwd, a_log_bwd)
    dvec = _lane_row(dt_bias_fwd, dt_bias_bwd)
    G = _local_step(x[0], loss_target[0], wcat, conv_full, avec, dvec, gdn_norm_w, _pad_w2(w2f_full, 32), _pad_w2(w2b_full, 48),
                    gk_b2_fwd, gk_b2_bwd, gla_norm_w, proj_weights, ln_pre_w + wp_started[4][0:1, 0:1], ln_post_w, send_proj_grads,
                    send_in_grads)

    small_started = _exchange_start(_pack_small_grads(G), "blocks", "exchange_small_start")
    q_in, land_in = _exchange_wait(sent["chips"], small_started[4], "chips", "exchange_in_chips_wait")
    o_in = _sum_adam(land_in, w_in_t, m_in_t, v_in_t, "adam_w_in", SHARD_IN, tc=256, own=q_in, own_slot=_my_chip)
    o_p = sent["o_p"]
    p_small, land_small = _exchange_wait(small_started, o_in[0], "blocks", "exchange_small_wait")
    o_small, loss = _small_adam(land_small, p_small, W, M, V)

    res = []
    for kind in range(4):
        per = {name: o_small[name][kind] for name in SMALL_NAMES}
        per["w_in"] = jnp.transpose(o_in[kind])[None]
        pp = o_p[kind].reshape(3, 1, D_MODEL // N_DEV, D_MODEL)
        per["w_proj_gdn"], per["w_proj_gla"], per["w_out"] = pp[0], pp[1], pp[2]
        res.append([per[n] for n in _WEIGHTS])
    return (loss[0, 0], G["grad_x"][None], *res[0], *res[1], *res[2], *res[3])
```

```python
import jax
import jax.numpy as jnp
from jax import lax
from jax.experimental import pallas as pl
from jax.experimental.pallas import tpu as pltpu

f32 = jnp.float32
MXU_DTYPE = jnp.bfloat16

D_MODEL = 1024
CHUNK = 64
NORM_EPS = 1e-6
GDN_HEADS, GDN_DIM = 8, 128
GLA_HEADS, GLA_DK, GLA_DV = 4, 128, 256
GATE_NORMALIZER = 16.0
CONV_K = 5
N_IN = 9280
N_DEV = 8
SHARD_IN = N_IN // N_DEV

N_CAT = 9728
SMALL_OFF = 9216
LANES = 128

ADAM_LR, ADAM_B1, ADAM_B2, ADAM_EPS, ADAM_WD, ADAM_STEP = 0.001, 0.9, 0.999, 1e-08, 0.01, 10

VMEM_LIMIT = 56 * 1024 * 1024


def _cparams(sem=None):
    return pltpu.CompilerParams(dimension_semantics=sem, vmem_limit_bytes=VMEM_LIMIT)


_DN = {"nn": (((1,), (0,)), ((), ())), "nt": (((1,), (1,)), ((), ())), "tn": (((0,), (0,)), ((), ()))}
_BDN = {"nn": (((2,), (1,)), ((0,), (0,))), "nt": (((2,), (2,)), ((0,), (0,))), "tn": (((1,), (1,)), ((0,), (0,)))}


def _dot16(a, b, kind):
    return lax.dot_general(a, b, (_DN if a.ndim == 2 else _BDN)[kind], preferred_element_type=f32)


def _mx(a, b, kind):
    return _dot16(a.astype(MXU_DTYPE), b.astype(MXU_DTYPE), kind)


def _split2(a):
    hi = a.astype(jnp.bfloat16)
    return hi, (a - hi.astype(f32)).astype(jnp.bfloat16)


def _split3(a):
    p1 = a.astype(jnp.bfloat16)
    r = a - p1.astype(f32)
    p2 = r.astype(jnp.bfloat16)
    return p1, p2, (r - p2.astype(f32)).astype(jnp.bfloat16)


def _h3s(a2, b2, kind):
    (ah, al), (bh, bl) = a2, b2
    return _dot16(ah, bh, kind) + (_dot16(ah, bl, kind) + _dot16(al, bh, kind))


def _h3(a, b, kind):
    return _h3s(_split2(a), _split2(b), kind)


def _exact01(t01, x, kind="nn"):
    t = t01.astype(jnp.bfloat16)
    x1, x2, x3 = _split3(x)
    return _dot16(t, x1, kind) + (_dot16(t, x2, kind) + _dot16(t, x3, kind))


def _colsum_col(e):
    ones = jnp.ones(e.shape[:-1] + (LANES,), jnp.bfloat16)
    e1, e2, e3 = _split3(e)
    return (_dot16(e1, ones, "tn") + (_dot16(e2, ones, "tn") + _dot16(e3, ones, "tn")))[..., 0:1]


def _iota3(shape, axis):
    return lax.broadcasted_iota(jnp.int32, shape, axis)


def _dir_masks(B, nf, Cn):
    shp = (B, Cn, Cn)
    bb, ii, jj = _iota3(shp, 0), _iota3(shp, 1), _iota3(shp, 2)
    fwd = bb < nf
    rev = jnp.logical_not(fwd)
    incl = jnp.logical_or(jnp.logical_and(fwd, ii >= jj), jnp.logical_and(rev, ii <= jj))
    strict = jnp.logical_and(incl, ii != jj)
    return incl, strict, ii == jj


def _last_row(x, nf):
    B, Cn = x.shape[0], x.shape[1]
    fwd = _iota3((B, 1, 1), 0) < nf
    return jnp.where(fwd, x[:, Cn - 1 : Cn, :], x[:, 0:1, :])


def _add_at_last_row(x, val, nf):
    B, Cn = x.shape[0], x.shape[1]
    bb, rr = _iota3((B, Cn, 1), 0), _iota3((B, Cn, 1), 1)
    at = jnp.logical_or(jnp.logical_and(bb < nf, rr == Cn - 1), jnp.logical_and(bb >= nf, rr == 0))
    return x + jnp.where(at, val, 0.0)


def _sigmoid(x):
    return jax.nn.sigmoid(x)


def _silu(x):
    return x * _sigmoid(x)


def _silu_grad(x):
    s = _sigmoid(x)
    return s * (1.0 + x * (1.0 - s))


def _softplus(x):
    u = jnp.exp(-jnp.abs(x))
    l1p = jnp.where(u < 1e-3, u * (1.0 - u * (0.5 - u * (1.0 / 3.0))), jnp.log(1.0 + u))
    return jnp.maximum(x, 0.0) + l1p


def _mxr(x):
    return x.astype(MXU_DTYPE).astype(f32)


def _iota2(shape, axis):
    return lax.broadcasted_iota(jnp.int32, shape, axis)


def _matmul(a, b, kind, out_dtype, name, tm, tn, tk, out_t=False, after=None):
    extra = [] if after is None else [after]
    if kind == "nn":
        (M, K), N = a.shape, b.shape[1]
    elif kind == "nt":
        (M, K), N = a.shape, b.shape[0]
    else:
        (K, M), N = a.shape, b.shape[1]
    tm, tn, tk = min(tm, M), min(tn, N), min(tk, K)
    assert M % tm == 0 and N % tn == 0 and K % tk == 0, (name, M, N, K)
    nk = K // tk
    if kind == "tn":
        a_spec = pl.BlockSpec((tk, tm), lambda i, j, k: (k, i))
    else:
        a_spec = pl.BlockSpec((tm, tk), lambda i, j, k: (i, k))
    b_mode = dict(pipeline_mode=pl.Buffered(1)) if (tn == N and tk == K) else {}
    if kind == "nt":
        b_spec = pl.BlockSpec((tn, tk), lambda i, j, k: (j, k), **b_mode)
    else:
        b_spec = pl.BlockSpec((tk, tn), lambda i, j, k: (k, j), **b_mode)

    def body(a_ref, b_ref, *rest):
        o_ref, acc = rest[len(extra)], rest[len(extra) + 1 :]
        p = _mx(a_ref[...], b_ref[...], kind)
        if nk == 1:
            o_ref[...] = (p.T if out_t else p).astype(out_dtype)
        else:
            assert not out_t
            acc_ref = acc[0]
            k = pl.program_id(2)

            @pl.when(k == 0)
            def _():
                acc_ref[...] = p

            @pl.when(k > 0)
            def _():
                acc_ref[...] += p

            @pl.when(k == nk - 1)
            def _():
                o_ref[...] = acc_ref[...].astype(out_dtype)

    return pl.pallas_call(
        body,
        out_shape=jax.ShapeDtypeStruct((N, M) if out_t else (M, N), out_dtype),
        grid=(M // tm, N // tn, nk),
        in_specs=[a_spec, b_spec] + [pl.BlockSpec((8, LANES), lambda i, j, k: (0, 0))] * len(extra),
        out_specs=pl.BlockSpec((tn, tm), lambda i, j, k: (j, i)) if out_t else pl.BlockSpec((tm, tn), lambda i, j, k: (i, j)),
        scratch_shapes=[] if nk == 1 else [pltpu.VMEM((tm, tn), f32)],
        compiler_params=_cparams(("parallel", "parallel", "arbitrary")),
        name=name,
    )(a, b, *extra)


ROW_TILE = 512


def _rms_pre(x, w):
    T = x.shape[0]
    tb = min(ROW_TILE, T)

    def body(x_ref, w_ref, h_ref, ht_ref):
        xv = x_ref[...]
        r = lax.rsqrt(jnp.mean(xv * xv, axis=1, keepdims=True) + NORM_EPS)
        h = xv * r * w_ref[...]
        h_ref[...] = h.astype(h_ref.dtype)
        ht_ref[...] = h.T.astype(ht_ref.dtype)

    return pl.pallas_call(
        body,
        out_shape=(jax.ShapeDtypeStruct((T, D_MODEL), MXU_DTYPE), jax.ShapeDtypeStruct((D_MODEL, T), MXU_DTYPE)),
        grid=(T // tb,),
        in_specs=[pl.BlockSpec((tb, D_MODEL), lambda i: (i, 0)), pl.BlockSpec((1, D_MODEL), lambda i: (0, 0))],
        out_specs=(pl.BlockSpec((tb, D_MODEL), lambda i: (i, 0)), pl.BlockSpec((D_MODEL, tb), lambda i: (0, i))),
        compiler_params=_cparams(("parallel",)),
        name="rms_pre",
    )(x, w)


def _rms_pre_bwd(x, w, dh, dy):
    T = x.shape[0]
    tb = min(ROW_TILE, T)

    def body(x_ref, w_ref, dh_ref, dy_ref, dx_ref, dw_ref):
        i = pl.program_id(0)
        xv, dhv = x_ref[...], dh_ref[...]
        r = lax.rsqrt(jnp.mean(xv * xv, axis=1, keepdims=True) + NORM_EPS)
        dhw = dhv * w_ref[...]
        dx_ref[...] = dy_ref[...] + r * dhw - xv * (r * r * r) * jnp.mean(dhw * xv, axis=1, keepdims=True)
        part = jnp.sum(dhv * xv * r, axis=0, keepdims=True)

        @pl.when(i == 0)
        def _():
            dw_ref[...] = part

        @pl.when(i > 0)
        def _():
            dw_ref[...] += part

    blk = pl.BlockSpec((tb, D_MODEL), lambda i: (i, 0))
    row = pl.BlockSpec((1, D_MODEL), lambda i: (0, 0))
    return pl.pallas_call(
        body,
        out_shape=(jax.ShapeDtypeStruct((T, D_MODEL), f32), jax.ShapeDtypeStruct((1, D_MODEL), f32)),
        grid=(T // tb,),
        in_specs=[blk, row, blk, blk],
        out_specs=(blk, row),
        compiler_params=_cparams(("arbitrary",)),
        name="rms_pre_bwd",
    )(x, w, dh, dy)


SHIFTS = (-2, -1, 1, 2)


SHIFT_PAD = 8
PREP_HEADS, PREP_HEADS_BWD = 4, 1


def _shifted(x, pad_ref):
    T, cb = x.shape
    pad_ref[0:SHIFT_PAD] = jnp.zeros((SHIFT_PAD, cb), f32)
    pad_ref[SHIFT_PAD + T :] = jnp.zeros((SHIFT_PAD, cb), f32)
    pad_ref[SHIFT_PAD : SHIFT_PAD + T] = x
    return {d: pad_ref[SHIFT_PAD + d : SHIFT_PAD + d + T] for d in SHIFTS}


def _conv5(u, ush, cw):
    acc = u * cw[2:3, :]
    for j in (0, 1, 3, 4):
        acc = acc + ush[j - 2] * cw[j : j + 1, :]
    return acc


def _gdn_prep(proj, conv_w):
    T = proj.shape[0]
    hps, W = PREP_HEADS, PREP_HEADS * GDN_DIM
    per = GDN_HEADS // hps

    def body(u_ref, cw_ref, o_ref, pad_ref):
        j = pl.program_id(0)
        for i in range(hps):
            sl = slice(i * GDN_DIM, (i + 1) * GDN_DIM)
            u = _mxr(u_ref[:, sl])
            s = _silu(_conv5(u, _shifted(u, pad_ref), _mxr(cw_ref[:, sl])))
            o_ref[0, :, sl] = jnp.where(j < 2 * per, s * lax.rsqrt(jnp.sum(s * s, axis=1, keepdims=True) + NORM_EPS), s)

    return pl.pallas_call(
        body,
        out_shape=jax.ShapeDtypeStruct((3, T, GDN_HEADS * GDN_DIM), f32),
        grid=(3 * per,),
        in_specs=[pl.BlockSpec((T, W), lambda j: (0, j)), pl.BlockSpec((CONV_K, W), lambda j: (0, j))],
        out_specs=pl.BlockSpec((1, T, W), lambda j: (j // per, 0, j % per)),
        scratch_shapes=[pltpu.VMEM((T + 2 * SHIFT_PAD, GDN_DIM), f32)],
        compiler_params=_cparams(("parallel",)),
        name="gdn_prep",
    )(proj, conv_w)


def _gdn_prep_bwd(proj, conv_w, dqkv_f, dqkv_b, dproj):
    T = proj.shape[0]
    hps, W = PREP_HEADS_BWD, PREP_HEADS_BWD * GDN_DIM
    per = GDN_HEADS // hps

    def body(u_ref, cw_ref, df_ref, db_ref, _, du_ref, dcw_ref, upad_ref, dcpad_ref):
        j = pl.program_id(0)
        for i in range(hps):
            sl = slice(i * GDN_DIM, (i + 1) * GDN_DIM)
            u, cw = _mxr(u_ref[:, sl]), _mxr(cw_ref[:, sl])
            ush = _shifted(u, upad_ref)
            c = _conv5(u, ush, cw)
            s = _silu(c)
            dn = df_ref[0, :, sl] + db_ref[0, :, sl]
            rinv = lax.rsqrt(jnp.sum(s * s, axis=1, keepdims=True) + NORM_EPS)
            ds_norm = rinv * dn - s * (rinv * rinv * rinv) * jnp.sum(dn * s, axis=1, keepdims=True)
            ds = jnp.where(j < 2 * per, ds_norm, dn)
            dc = _mxr(ds * _silu_grad(c))
            dcsh = _shifted(dc, dcpad_ref)
            du = dc * cw[2:3, :]
            for jj in range(CONV_K):
                d = jj - 2
                dcw_ref[jj : jj + 1, sl] = jnp.sum(dc * (u if d == 0 else ush[d]), axis=0, keepdims=True)
                if d != 0:
                    du = du + dcsh[-d] * cw[jj : jj + 1, :]
            du_ref[:, sl] = du.astype(du_ref.dtype)

    qspec = pl.BlockSpec((1, T, W), lambda j: (j // per, 0, j % per))
    return pl.pallas_call(
        body,
        out_shape=(jax.ShapeDtypeStruct(dproj.shape, dproj.dtype), jax.ShapeDtypeStruct((CONV_K, 3 * GDN_HEADS * GDN_DIM), f32)),
        grid=(3 * per,),
        in_specs=[pl.BlockSpec((T, W), lambda j: (0, j)), pl.BlockSpec((CONV_K, W), lambda j: (0, j)), qspec, qspec, _ANY_SPEC],
        out_specs=(pl.BlockSpec((T, W), lambda j: (0, j)), pl.BlockSpec((CONV_K, W), lambda j: (0, j))),
        input_output_aliases={4: 0},
        scratch_shapes=[pltpu.VMEM((T + 2 * SHIFT_PAD, GDN_DIM), f32)] * 2,
        compiler_params=_cparams(("parallel",)),
        name="gdn_prep_bwd",
    )(proj, conv_w, dqkv_f, dqkv_b, dproj)


def _chunk_tri(n, rev):
    i, j = _iota2((n, n), 0), _iota2((n, n), 1)
    same = jnp.right_shift(i, 6) == jnp.right_shift(j, 6)
    order = (j >= i) if rev else (j <= i)
    return jnp.where(jnp.logical_and(same, order), 1.0, 0.0).astype(f32)


def _gate_prep(proj, avec, dvec, w2f, w2b, b2f, b2b, tb=256):
    T = proj.shape[0]
    W = GLA_HEADS * GLA_DK

    def body(s_ref, a_ref, d_ref, wf_ref, wb_ref, bf_ref, bb_ref, g_ref, gc_ref, gf_ref, gb_ref):
        small = s_ref[...]
        tri_f, tri_r = _chunk_tri(tb, False), _chunk_tri(tb, True)
        lane = _iota2(small.shape, 1)
        lg = -jnp.exp(a_ref[...]) * _softplus(small + d_ref[...])
        beta = _sigmoid(small)
        g_ref[...] = jnp.where(lane < 16, lg, jnp.where(lane < 32, beta, 0.0))
        lgm = jnp.where(lane < 16, lg, 0.0)
        gc_ref[...] = jnp.where(lane < 8, _exact01(tri_f, lgm), _exact01(tri_r, lgm))
        gkf = -_softplus(-(_mx(small, wf_ref[...], "nn") + bf_ref[...])) * (1.0 / GATE_NORMALIZER)
        gkb = -_softplus(-(_mx(small, wb_ref[...], "nn") + bb_ref[...])) * (1.0 / GATE_NORMALIZER)
        gf_ref[...] = _exact01(tri_f, gkf)
        gb_ref[...] = _exact01(tri_r, gkb)

    blk = pl.BlockSpec((tb, LANES), lambda i: (i, 0))
    row = pl.BlockSpec((1, LANES), lambda i: (0, 0))
    wide = pl.BlockSpec((tb, W), lambda i: (i, 0))
    wsp = pl.BlockSpec((LANES, W), lambda i: (0, 0))
    wrow = pl.BlockSpec((1, W), lambda i: (0, 0))
    return pl.pallas_call(
        body,
        out_shape=(jax.ShapeDtypeStruct((T, LANES), f32), jax.ShapeDtypeStruct((T, LANES), f32),
                   jax.ShapeDtypeStruct((T, W), f32), jax.ShapeDtypeStruct((T, W), f32)),
        grid=(T // tb,),
        in_specs=[pl.BlockSpec((tb, LANES), lambda i: (i, SMALL_OFF // LANES)), row, row, wsp, wsp, wrow, wrow],
        out_specs=(blk, blk, wide, wide),
        compiler_params=_cparams(("parallel",)),
        name="gate_prep",
    )(proj, avec, dvec, w2f, w2b, b2f, b2b)


def _gdn_gates_bwd(proj, avec, dvec, dg_f, dg_b, dsmall_gla, dproj):
    T = proj.shape[0]
    tb = min(ROW_TILE, T)
    pad = N_CAT - SMALL_OFF

    def body(s_ref, a_ref, d_ref, gf_ref, gb_ref, dl_ref, _, ds_ref, da_ref, dd_ref):
        i = pl.program_id(0)
        small = s_ref[...]
        lane = _iota2(small.shape, 1)
        dgate = gf_ref[...] + gb_ref[...]
        z = small + d_ref[...]
        nega = -jnp.exp(a_ref[...])
        dz = dgate * nega * _sigmoid(z)
        beta = _sigmoid(small)
        dsm = jnp.where(lane < 16, dz, jnp.where(lane < 32, dgate * beta * (1.0 - beta), 0.0))
        ds_ref[:, :LANES] = (dsm + dl_ref[...]).astype(ds_ref.dtype)
        ds_ref[:, LANES:] = jnp.zeros((tb, pad - LANES), ds_ref.dtype)
        lg = nega * _softplus(z)
        pa = jnp.sum(jnp.where(lane < 16, dgate * lg, 0.0), axis=0, keepdims=True)
        pd = jnp.sum(jnp.where(lane < 16, dz, 0.0), axis=0, keepdims=True)

        @pl.when(i == 0)
        def _():
            da_ref[...] = pa
            dd_ref[...] = pd

        @pl.when(i > 0)
        def _():
            da_ref[...] += pa
            dd_ref[...] += pd

    blk = pl.BlockSpec((tb, LANES), lambda i: (i, 0))
    row = pl.BlockSpec((1, LANES), lambda i: (0, 0))
    return pl.pallas_call(
        body,
        out_shape=(jax.ShapeDtypeStruct(dproj.shape, dproj.dtype), jax.ShapeDtypeStruct((1, LANES), f32), jax.ShapeDtypeStruct((1, LANES), f32)),
        grid=(T // tb,),
        in_specs=[pl.BlockSpec((tb, LANES), lambda i: (i, SMALL_OFF // LANES)), row, row, blk, blk, blk, _ANY_SPEC],
        out_specs=(pl.BlockSpec((tb, pad), lambda i: (i, DP_SMALL_BLK)), row, row),
        input_output_aliases={6: 0},
        compiler_params=_cparams(("arbitrary",)),
        name="gdn_gates_bwd",
    )(proj, avec, dvec, dg_f, dg_b, dsmall_gla, dproj)


GDN_HB = 8


def _inv_unit_lower(L, eye):
    A = eye - L
    P2 = _split2(L)
    for _ in range(3):
        P2 = _split2(_h3s(P2, P2, "nn"))
        A = A + _h3s(_split2(A), P2, "nn")
    Ph = P2[0]
    for _ in range(2):
        Ph = _dot16(Ph, Ph, "nn").astype(jnp.bfloat16)
        A = A + _dot16(A.astype(jnp.bfloat16), Ph, "nn")
    return A


def _lane_col(blk, idx):
    lane = _iota2(blk.shape, 1)
    return jnp.sum(jnp.where(lane == idx, blk, 0.0), axis=1, keepdims=True)


def _out_norm(o_f, o_b, proj, gate_blk, w, hd, name):
    T, W = o_f.shape
    tb = min(ROW_TILE, T)
    nh = W // hd

    def body(of_ref, ob_ref, z_ref, w_ref, y_ref):
        wv = w_ref[...]
        for i in range(nh):
            sl = slice(i * hd, (i + 1) * hd)
            o = of_ref[:, sl] + ob_ref[:, sl]
            r = lax.rsqrt(jnp.mean(o * o, axis=1, keepdims=True) + NORM_EPS)
            y_ref[:, sl] = (o * r * wv * _silu(z_ref[:, sl])).astype(y_ref.dtype)

    blk = pl.BlockSpec((tb, W), lambda i: (i, 0))
    return pl.pallas_call(
        body,
        out_shape=jax.ShapeDtypeStruct((T, W), MXU_DTYPE),
        grid=(T // tb,),
        in_specs=[blk, blk, pl.BlockSpec((tb, W), lambda i: (i, gate_blk)), pl.BlockSpec((1, hd), lambda i: (0, 0))],
        out_specs=blk,
        compiler_params=_cparams(("parallel",)),
        name=name,
    )(o_f, o_b, proj, w)


def _out_norm_bwd(dy, o_f, o_b, proj, gate_blk, w, hd, name, dproj):
    T, W = o_f.shape
    tb = min(ROW_TILE, T)
    nh = W // hd

    def body(dy_ref, of_ref, ob_ref, z_ref, w_ref, _, do_ref, dz_ref, dw_ref):
        i = pl.program_id(0)
        wv = w_ref[...]
        dw = jnp.zeros((1, hd), f32)
        for a in range(nh):
            sl = slice(a * hd, (a + 1) * hd)
            o = of_ref[:, sl] + ob_ref[:, sl]
            z = z_ref[:, sl]
            dyv = dy_ref[:, sl]
            r = lax.rsqrt(jnp.mean(o * o, axis=1, keepdims=True) + NORM_EPS)
            orr = o * r
            dn = dyv * _silu(z)
            dz_ref[:, sl] = (dyv * orr * wv * _silu_grad(z)).astype(dz_ref.dtype)
            dw = dw + jnp.sum(dn * orr, axis=0, keepdims=True)
            dnw = dn * wv
            do_ref[:, sl] = r * dnw - o * (r * r * r) * jnp.mean(dnw * o, axis=1, keepdims=True)

        @pl.when(i == 0)
        def _():
            dw_ref[...] = dw

        @pl.when(i > 0)
        def _():
            dw_ref[...] += dw

    blk = pl.BlockSpec((tb, W), lambda i: (i, 0))
    row = pl.BlockSpec((1, hd), lambda i: (0, 0))
    return pl.pallas_call(
        body,
        out_shape=(jax.ShapeDtypeStruct((T, W), f32), jax.ShapeDtypeStruct(dproj.shape, dproj.dtype), jax.ShapeDtypeStruct((1, hd), f32)),
        grid=(T // tb,),
        in_specs=[blk, blk, blk, pl.BlockSpec((tb, W), lambda i: (i, gate_blk)), row, _ANY_SPEC],
        out_specs=(blk, pl.BlockSpec((tb, W), lambda i: (i, gate_blk)), row),
        input_output_aliases={5: 1},
        compiler_params=_cparams(("arbitrary",)),
        name=name,
    )(dy, o_f, o_b, proj, w, dproj)


def _gla_prep_bwd(proj, w2f, w2b, b2f, b2b, dgk_f, dgk_b, dqkv, dproj):
    T = proj.shape[0]
    tb = min(ROW_TILE, T)
    W = GLA_HEADS * GLA_DK
    WV = GLA_HEADS * GLA_DV

    def body(s_ref, wf_ref, wb_ref, bf_ref, bb_ref, df_ref, db_ref, qf, qb, kf, kb, vf, vb, _,
             dp_ref, ds_ref, dwf_ref, dwb_ref, dbf_ref, dbb_ref):
        i = pl.program_id(0)
        dp_ref[:, :W] = (qf[...] + qb[...]).astype(dp_ref.dtype)
        dp_ref[:, W : 2 * W] = (kf[...] + kb[...]).astype(dp_ref.dtype)
        dp_ref[:, 2 * W :] = (vf[...] + vb[...]).astype(dp_ref.dtype)
        small = s_ref[...]
        dsm = jnp.zeros((tb, LANES), f32)
        parts = []
        for w_ref, b_ref, d_ref in ((wf_ref, bf_ref, df_ref), (wb_ref, bb_ref, db_ref)):
            pre = _mx(small, w_ref[...], "nn") + b_ref[...]
            dpre = d_ref[...] * (1.0 / GATE_NORMALIZER) * _sigmoid(-pre)
            dsm = dsm + _mx(dpre, w_ref[...], "nt")
            parts.append((_mx(small, dpre, "tn"), jnp.sum(dpre, axis=0, keepdims=True)))
        ds_ref[...] = dsm

        @pl.when(i == 0)
        def _():
            dwf_ref[...], dbf_ref[...] = parts[0]
            dwb_ref[...], dbb_ref[...] = parts[1]

        @pl.when(i > 0)
        def _():
            dwf_ref[...] += parts[0][0]
            dbf_ref[...] += parts[0][1]
            dwb_ref[...] += parts[1][0]
            dbb_ref[...] += parts[1][1]

    blk = pl.BlockSpec((tb, W), lambda i: (i, 0))
    vblk = pl.BlockSpec((tb, WV), lambda i: (i, 0))
    wsp = pl.BlockSpec((LANES, W), lambda i: (0, 0))
    row = pl.BlockSpec((1, W), lambda i: (0, 0))
    return pl.pallas_call(
        body,
        out_shape=(jax.ShapeDtypeStruct(dproj.shape, dproj.dtype), jax.ShapeDtypeStruct((T, LANES), f32), jax.ShapeDtypeStruct((LANES, W), f32),
                   jax.ShapeDtypeStruct((LANES, W), f32), jax.ShapeDtypeStruct((1, W), f32), jax.ShapeDtypeStruct((1, W), f32)),
        grid=(T // tb,),
        in_specs=[pl.BlockSpec((tb, LANES), lambda i: (i, SMALL_OFF // LANES)), wsp, wsp, row, row, blk, blk,
                  blk, blk, blk, blk, vblk, vblk, _ANY_SPEC],
        out_specs=(pl.BlockSpec((tb, 2 * W + WV), lambda i: (i, DP_GLA_QKV_BLK)), pl.BlockSpec((tb, LANES), lambda i: (i, 0)), wsp, wsp, row, row),
        input_output_aliases={13: 0},
        compiler_params=_cparams(("arbitrary",)),
        name="gla_prep_bwd",
    )(proj, w2f, w2b, b2f, b2b, dgk_f, dgk_b, *dqkv, dproj)


GATE_A_BLK, GATE_B_BLK, Z_BLK, G_B_BLK = 6144 // 1024, 7168 // 1024, 3072 // 1024, 8192 // 1024
DP_GATES_BLK, DP_GLA_QKV_BLK, DP_SMALL_BLK = 6144 // 2048, 4096 // 2048, SMALL_OFF // 512
_ANY_SPEC = pl.BlockSpec(memory_space=pl.ANY)


def _mid(og, obg, proj, x, target, wpg, wpl, wout, w_post, tb=256):
    T = x.shape[0]
    Dm = D_MODEL

    def body(og_ref, obg_ref, ga_ref, gb_ref, x_ref, t_ref, wpg_ref, wpl_ref, wo_ref, wp_ref,
             loss_ref, dwp_ref, dy_ref, m_ref, dout_ref, dya_ref, dyb_ref, dgates_ref, dog_ref, dobg_ref):
        i = pl.program_id(0)
        ya = _mx(og_ref[...], wpg_ref[...], "nn")
        yb = _mx(obg_ref[...], wpl_ref[...], "nn")
        sa, sb = _sigmoid(ga_ref[...]), _sigmoid(gb_ref[...])
        m = sa * ya + sb * yb
        m_ref[...] = m.astype(m_ref.dtype)
        out = _mx(m, wo_ref[...], "nn")
        r = lax.rsqrt(jnp.mean(out * out, axis=1, keepdims=True) + NORM_EPS)
        wp = wp_ref[...]
        e = x_ref[...] + out * r * wp - t_ref[...]
        dy = e * (1.0 / Dm)
        dy_ref[...] = dy
        lpart = jnp.sum(e * e, axis=0, keepdims=True)
        wpart = jnp.sum(dy * out * r, axis=0, keepdims=True)
        dyn = dy * wp
        dout = r * dyn - out * (r * r * r) * jnp.mean(dyn * out, axis=1, keepdims=True)
        dout_ref[...] = dout.astype(dout_ref.dtype)
        dm = _mx(dout, wo_ref[...], "nt")
        dya, dyb = dm * sa, dm * sb
        dya_ref[...] = dya.astype(dya_ref.dtype)
        dyb_ref[...] = dyb.astype(dyb_ref.dtype)
        dgates_ref[:, :Dm] = (dm * ya * sa * (1.0 - sa)).astype(dgates_ref.dtype)
        dgates_ref[:, Dm:] = (dm * yb * sb * (1.0 - sb)).astype(dgates_ref.dtype)
        dog_ref[...] = _mx(dya, wpg_ref[...], "nt")
        dobg_ref[...] = _mx(dyb, wpl_ref[...], "nt")

        @pl.when(i == 0)
        def _():
            loss_ref[...] = lpart
            dwp_ref[...] = wpart

        @pl.when(i > 0)
        def _():
            loss_ref[...] += lpart
            dwp_ref[...] += wpart

    blk = pl.BlockSpec((tb, Dm), lambda i: (i, 0))
    row = pl.BlockSpec((1, Dm), lambda i: (0, 0))
    wsp = pl.BlockSpec((Dm, Dm), lambda i: (0, 0))
    act = jax.ShapeDtypeStruct((T, Dm), MXU_DTYPE)
    big = jax.ShapeDtypeStruct((T, Dm), f32)
    vec = jax.ShapeDtypeStruct((1, Dm), f32)
    return pl.pallas_call(
        body,
        out_shape=(vec, vec, big, act, act, act, act, jax.ShapeDtypeStruct((T, N_CAT), MXU_DTYPE), big, big),
        grid=(T // tb,),
        in_specs=[blk, blk, pl.BlockSpec((tb, Dm), lambda i: (i, GATE_A_BLK)), pl.BlockSpec((tb, Dm), lambda i: (i, GATE_B_BLK)),
                  blk, blk, wsp, wsp, wsp, row],
        out_specs=(row, row) + (blk,) * 5 + (pl.BlockSpec((tb, 2 * Dm), lambda i: (i, DP_GATES_BLK)), blk, blk),
        compiler_params=_cparams(("arbitrary",)),
        name="mid",
    )(og, obg, proj, proj, x, target, wpg, wpl, wout, w_post)


def _heads(ref, lead, n, width, rows=slice(None)):
    return [ref[lead + (rows, slice(i * width, (i + 1) * width))] for i in range(n)]


class _GdnBatch:
    def __init__(self, q, k, v, gcol, grow, beta, nf, saved=None, sol=None):
        B, Cn = q.shape[0], q.shape[1]
        self.incl, self.strict, eye = _dir_masks(B, nf, Cn)
        A = None
        if saved is None:
            self.decay = jnp.where(self.incl, jnp.exp(jnp.where(self.incl, gcol - grow, 0.0)), 0.0)
        else:
            A, self.L, self.attn, self.decay = (saved[:, i * Cn : (i + 1) * Cn] for i in range(4))
        self.k, self.v = k, v
        gl = _last_row(gcol, nf)
        self.beta = jnp.broadcast_to(beta, q.shape)
        self.eg = jnp.broadcast_to(jnp.exp(gcol), q.shape)
        self.ekd = jnp.broadcast_to(jnp.exp(gl - gcol), q.shape)
        self.qs = q * (GDN_DIM**-0.5)
        self.kb = k * self.beta
        if saved is None:
            self.L = jnp.where(self.strict, _mx(self.kb, k, "nt") * self.decay, 0.0)
            self.attn = jnp.where(self.incl, _mx(self.qs, k, "nt") * self.decay, 0.0)
            A = _inv_unit_lower(self.L, jnp.where(eye, 1.0, 0.0).astype(f32))
        self.A2 = _split2(A)
        self.saved = jnp.concatenate([A, self.L, self.attn, self.decay], axis=1) if saved is None else saved
        self.kbg = self.kb * self.eg
        if sol is None:
            sol = _h3s(self.A2, _split2(jnp.concatenate([v * self.beta, self.kbg], axis=2)), "nn")
        self.sol = sol
        self.u = self.sol[:, :, :GDN_DIM]
        self.w = self.sol[:, :, GDN_DIM:]
        self.qd = self.qs * self.eg
        self.kd = k * self.ekd
        self.egl = jnp.exp(gl)


def _gdn_chain_inputs(qf_ref, qb_ref, gf_ref, gb_ref, cf_ref, cb_ref, b, hb, need_rows=True):
    qkv = [jnp.stack(_heads(qf_ref, (i,), hb, GDN_DIM) + _heads(qb_ref, (i,), hb, GDN_DIM)) for i in range(3)]
    cols, rows, betas = [], [], []
    for d, (g_ref, c_ref) in enumerate(((gf_ref, cf_ref), (gb_ref, cb_ref))):
        gates, gc = g_ref[...], c_ref[...]
        for hh in range(hb):
            h = b * hb + hh
            col = _lane_col(gc, h + 8 * d)
            cols.append(col)
            if need_rows:
                rows.append(jnp.broadcast_to(col, (CHUNK, LANES)).T[0:1, :])
            betas.append(_lane_col(gates, 16 + h + 8 * d))
    return qkv[0], qkv[1], qkv[2], jnp.stack(cols), (jnp.stack(rows) if need_rows else None), jnp.stack(betas)


GDN_CPS_FWD, GDN_CPS_BWD = 2, 1


def _chunk_views(refs, s_f, s_b):
    out = []
    for i, r in enumerate(refs):
        s = s_f if i % 2 == 0 else s_b
        rows = pl.ds(s * CHUNK, CHUNK)
        out.append(r.at[pl.ds(s, 1)] if len(r.shape) == 4 else (r.at[:, rows, :] if len(r.shape) == 3 else r.at[rows, :]))
    return out


def _gdn_specs(cps, hb):
    R, W = cps * CHUNK, hb * GDN_DIM
    qsp = lambda ci: pl.BlockSpec((3, R, W), lambda n, b: (0, ci(n), b))
    gsp = lambda ci: pl.BlockSpec((R, LANES), lambda n, b: (ci(n), 0))
    osp = lambda ci: pl.BlockSpec((R, W), lambda n, b: (ci(n), b))
    ssp = lambda ci: pl.BlockSpec((cps, hb, GDN_DIM, GDN_DIM), lambda n, b: (ci(n), b, 0, 0))
    asp = lambda ci: pl.BlockSpec((cps, hb, 4 * CHUNK, CHUNK), lambda n, b: (ci(n), b, 0, 0))
    usp = lambda ci: pl.BlockSpec((cps, hb, CHUNK, 2 * GDN_DIM), lambda n, b: (ci(n), b, 0, 0))
    return qsp, gsp, osp, ssp, asp, usp


def _gdn_fwd(qkvn, gates, gc):
    T = qkvn.shape[1]
    N = T // CHUNK
    nh, hb = GDN_HEADS, GDN_HB
    cps = min(GDN_CPS_FWD, N)
    NB = N // cps

    def body(*refs):
        n, b = pl.program_id(0), pl.program_id(1)
        s_scr = refs[-1]

        @pl.when(n == 0)
        def _():
            s_scr[0, pl.ds(b * hb, hb)] = jnp.zeros((hb, GDN_DIM, GDN_DIM), f32)
            s_scr[1, pl.ds(b * hb, hb)] = jnp.zeros((hb, GDN_DIM, GDN_DIM), f32)

        for s in range(cps):
            chunk(*_chunk_views(refs[:-1], s, cps - 1 - s), s_scr, b)

    def chunk(qf_ref, qb_ref, gf_ref, gb_ref, cf_ref, cb_ref, of_ref, ob_ref, sf_ref, sb_ref, af_ref, ab_ref, uf_ref, ub_ref, s_scr, b):
        hs = pl.ds(b * hb, hb)
        q, k, v, gcol, grow, beta = _gdn_chain_inputs(qf_ref, qb_ref, gf_ref, gb_ref, cf_ref, cb_ref, b, hb)
        ck = _GdnBatch(q, k, v, gcol, grow, beta, hb)
        S = jnp.concatenate([s_scr[0, hs], s_scr[1, hs]], axis=0)
        sf_ref[0], sb_ref[0] = S[:hb], S[hb:]
        af_ref[0], ab_ref[0] = ck.saved[:hb], ck.saved[hb:]
        uf_ref[0], ub_ref[0] = ck.sol[:hb], ck.sol[hb:]
        v_new = ck.u - _mx(ck.w, S, "nn")
        o = _mx(ck.qd, S, "nn") + _mx(ck.attn, v_new, "nn")
        S_new = S * ck.egl + _mx(ck.kd, v_new, "tn")
        for hh in range(hb):
            sl = slice(hh * GDN_DIM, (hh + 1) * GDN_DIM)
            of_ref[:, sl] = o[hh]
            ob_ref[:, sl] = o[hb + hh]
        s_scr[0, hs] = S_new[:hb]
        s_scr[1, hs] = S_new[hb:]

    fw, bw = (lambda n: n), (lambda n: NB - 1 - n)
    qsp, gsp, osp, ssp, asp, usp = _gdn_specs(cps, hb)
    o_shape = jax.ShapeDtypeStruct((T, nh * GDN_DIM), f32)
    s_shape = jax.ShapeDtypeStruct((N, nh, GDN_DIM, GDN_DIM), f32)
    a_shape = jax.ShapeDtypeStruct((N, nh, 4 * CHUNK, CHUNK), f32)
    u_shape = jax.ShapeDtypeStruct((N, nh, CHUNK, 2 * GDN_DIM), f32)
    return pl.pallas_call(
        body,
        out_shape=(o_shape, o_shape, s_shape, s_shape, a_shape, a_shape, u_shape, u_shape),
        grid=(NB, nh // hb),
        in_specs=[qsp(fw), qsp(bw), gsp(fw), gsp(bw), gsp(fw), gsp(bw)],
        out_specs=(osp(fw), osp(bw), ssp(fw), ssp(bw), asp(fw), asp(bw), usp(fw), usp(bw)),
        scratch_shapes=[pltpu.VMEM((2, nh, GDN_DIM, GDN_DIM), f32)],
        compiler_params=_cparams(("arbitrary", "arbitrary")),
        name="gdn_scan_fwd",
    )(qkvn, qkvn, gates, gates, gc, gc)


def _gdn_bwd(qkvn, gates, gc, st_f, st_b, a_f, a_b, u_f, u_b, do):
    T = qkvn.shape[1]
    N = T // CHUNK
    nh, hb = GDN_HEADS, GDN_HB
    nb = nh // hb
    cps = min(GDN_CPS_BWD, N)
    NB = N // cps

    def body(*refs):
        n, b = pl.program_id(0), pl.program_id(1)
        ds_scr = refs[-1]

        @pl.when(n == 0)
        def _():
            ds_scr[0, pl.ds(b * hb, hb)] = jnp.zeros((hb, GDN_DIM, GDN_DIM), f32)
            ds_scr[1, pl.ds(b * hb, hb)] = jnp.zeros((hb, GDN_DIM, GDN_DIM), f32)

        for s in range(cps):
            chunk(*_chunk_views(refs[:-1], cps - 1 - s, s), ds_scr, b)

    def chunk(qf_ref, qb_ref, gf_ref, gb_ref, cf_ref, cb_ref, sf_ref, sb_ref, af_ref, ab_ref, uf_ref, ub_ref, dof_ref, dob_ref,
              dqf_ref, dqb_ref, dgf_ref, dgb_ref, ds_scr, b):
        hs = pl.ds(b * hb, hb)

        @pl.when(b == 0)
        def _():
            dgf_ref[...] = jnp.zeros((CHUNK, LANES), f32)
            dgb_ref[...] = jnp.zeros((CHUNK, LANES), f32)

        q, k, v, gcol, grow, beta = _gdn_chain_inputs(qf_ref, qb_ref, gf_ref, gb_ref, cf_ref, cb_ref, b, hb, need_rows=False)
        ck = _GdnBatch(q, k, v, gcol, grow, beta, hb, saved=jnp.concatenate([af_ref[0], ab_ref[0]], axis=0),
                       sol=jnp.concatenate([uf_ref[0], ub_ref[0]], axis=0))
        S = jnp.concatenate([sf_ref[0], sb_ref[0]], axis=0)
        dS = jnp.concatenate([ds_scr[0, hs], ds_scr[1, hs]], axis=0)
        dov = jnp.stack(_heads(dof_ref, (), hb, GDN_DIM) + _heads(dob_ref, (), hb, GDN_DIM))
        v_new = ck.u - _mx(ck.w, S, "nn")
        d_vnew = _mx(ck.attn, dov, "tn") + _mx(ck.kd, dS, "nn")
        d_attn = jnp.where(ck.incl, _mx(dov, v_new, "nt"), 0.0)
        d_qd = _mx(dov, S, "nt")
        d_kd = _mx(v_new, dS, "nt")
        d_gl = jnp.sum(jnp.sum(dS * S, axis=2, keepdims=True), axis=1, keepdims=True) * ck.egl
        dS_new = dS * ck.egl + _mx(ck.qd, dov, "tn") - _mx(ck.w, d_vnew, "tn")
        ds_scr[0, hs] = dS_new[:hb]
        ds_scr[1, hs] = dS_new[hb:]
        d_w = -_mx(d_vnew, S, "nt")
        d_rhs = _h3s(ck.A2, _split2(jnp.concatenate([d_vnew, d_w], axis=2)), "tn")
        d_vb, d_kbg = d_rhs[:, :, :GDN_DIM], d_rhs[:, :, GDN_DIM:]
        dL = -jnp.where(ck.strict, _h3(d_rhs, ck.sol, "nt"), 0.0)
        P = dL * ck.decay
        Q = d_attn * ck.decay
        d_kb = _mx(P, k, "nn") + d_kbg * ck.eg
        d_qs = _mx(Q, k, "nn") + d_qd * ck.eg
        d_k = _mx(P, ck.kb, "tn") + _mx(Q, ck.qs, "tn") + d_kd * ck.ekd + d_kb * ck.beta
        E = dL * ck.L + d_attn * ck.attn
        kdsum = jnp.sum(d_kd * ck.kd, axis=2, keepdims=True)
        d_gc = (jnp.sum(E, axis=2, keepdims=True) - _colsum_col(E)
                + jnp.sum(d_kbg * ck.kbg, axis=2, keepdims=True)
                + jnp.sum(d_qd * ck.qd, axis=2, keepdims=True) - kdsum)
        d_gc = _add_at_last_row(d_gc, d_gl + jnp.sum(kdsum, axis=1, keepdims=True), hb)
        d_beta = jnp.sum(d_vb * v, axis=2, keepdims=True) + jnp.sum(d_kb * k, axis=2, keepdims=True)
        d_q = d_qs * (GDN_DIM**-0.5)
        d_v = d_vb * ck.beta
        lane = _iota2((CHUNK, LANES), 1)
        for d, (dq_ref, dg_ref) in enumerate(((dqf_ref, dgf_ref), (dqb_ref, dgb_ref))):
            acc = dg_ref[...]
            for hh in range(hb):
                c, h = d * hb + hh, b * hb + hh
                sl = slice(hh * GDN_DIM, (hh + 1) * GDN_DIM)
                dq_ref[0, :, sl] = d_q[c]
                dq_ref[1, :, sl] = d_k[c]
                dq_ref[2, :, sl] = d_v[c]
                acc = jnp.where(lane == h + 8 * d, d_gc[c], acc)
                acc = jnp.where(lane == 16 + h + 8 * d, d_beta[c], acc)
            dg_ref[...] = acc

        @pl.when(b == nb - 1)
        def _():
            for d, dg_ref in enumerate((dgf_ref, dgb_ref)):
                blk = dg_ref[...]
                dg_ref[...] = jnp.where(lane < 16, _exact01(_chunk_tri(CHUNK, d == 0), blk), blk)

    fw, bw = (lambda n: NB - 1 - n), (lambda n: n)
    qsp, gsp, osp, ssp, asp, usp = _gdn_specs(cps, hb)
    dq_shape = jax.ShapeDtypeStruct((3, T, nh * GDN_DIM), f32)
    dg_shape = jax.ShapeDtypeStruct((T, LANES), f32)
    return pl.pallas_call(
        body,
        out_shape=(dq_shape, dq_shape, dg_shape, dg_shape),
        grid=(NB, nb),
        in_specs=[qsp(fw), qsp(bw), gsp(fw), gsp(bw), gsp(fw), gsp(bw), ssp(fw), ssp(bw), asp(fw), asp(bw), usp(fw), usp(bw),
                  osp(fw), osp(bw)],
        out_specs=(qsp(fw), qsp(bw), gsp(fw), gsp(bw)),
        scratch_shapes=[pltpu.VMEM((2, nh, GDN_DIM, GDN_DIM), f32)],
        compiler_params=_cparams(("arbitrary", "arbitrary")),
        name="gdn_scan_bwd",
    )(qkvn, qkvn, gates, gates, gc, gc, st_f, st_b, a_f, a_b, u_f, u_b, do, do)


class _GlaBatch:
    def __init__(self, q, k, G, nf):
        B, Cn = q.shape[0], q.shape[1]
        self.incl, _, _ = _dir_masks(B, nf, Cn)
        self.eG = jnp.exp(G)
        self.enG = jnp.exp(-G)
        self.qg = q * (GLA_DK**-0.5) * self.eG
        self.kg = k * self.enG
        self.attn = jnp.where(self.incl, _mx(self.qg, self.kg, "nt"), 0.0)
        gl = _last_row(G, nf)
        self.ekd = jnp.exp(gl - G)
        self.kd = k * self.ekd
        self.egl = jnp.exp(gl)


GLA_CPS = 4


def _chunk_rows(s):
    return slice(s * CHUNK, (s + 1) * CHUNK)


def _gla_chain_inputs(refs_f, refs_b, rows_f, rows_b):
    nh = GLA_HEADS
    out = []
    for i, width in enumerate((GLA_DK, GLA_DK, GLA_DV, GLA_DK)):
        out.append(jnp.stack(_heads(refs_f[i], (), nh, width, rows_f) + _heads(refs_b[i], (), nh, width, rows_b)))
    return out


def _gla_specs(rows, ci):
    nh = GLA_HEADS
    return [
        pl.BlockSpec((rows, nh * GLA_DK), lambda n: (ci(n), 4096 // (nh * GLA_DK))),
        pl.BlockSpec((rows, nh * GLA_DK), lambda n: (ci(n), 4608 // (nh * GLA_DK))),
        pl.BlockSpec((rows, nh * GLA_DV), lambda n: (ci(n), 5120 // (nh * GLA_DV))),
        pl.BlockSpec((rows, nh * GLA_DK), lambda n: (ci(n), 0)),
    ]


def _gla_fwd(proj, G_f, G_b):
    T = proj.shape[0]
    N = T // CHUNK
    nh = GLA_HEADS
    cps = min(GLA_CPS, N)
    NB = N // cps

    def body(qf, kf, vf, gf, qb, kb, vb, gb, of_ref, ob_ref, sf_ref, sb_ref, s_scr):
        n = pl.program_id(0)

        @pl.when(n == 0)
        def _():
            s_scr[...] = jnp.zeros((2 * nh, GLA_DV, GLA_DK), f32)

        for s in range(cps):
            sb = cps - 1 - s
            q, k, v, G = _gla_chain_inputs((qf, kf, vf, gf), (qb, kb, vb, gb), _chunk_rows(s), _chunk_rows(sb))
            ck = _GlaBatch(q, k, G, nh)
            St = s_scr[...]
            sf_ref[s], sb_ref[sb] = St[:nh].astype(sf_ref.dtype), St[nh:].astype(sb_ref.dtype)
            o = _mx(ck.qg, St, "nt") + _mx(ck.attn, v, "nn")
            for i in range(nh):
                sl = slice(i * GLA_DV, (i + 1) * GLA_DV)
                of_ref[_chunk_rows(s), sl] = o[i]
                ob_ref[_chunk_rows(sb), sl] = o[nh + i]
            s_scr[...] = St * ck.egl + _mx(v, ck.kd, "tn")

    fw, bw = (lambda n: n), (lambda n: NB - 1 - n)
    osp = lambda ci: pl.BlockSpec((cps * CHUNK, nh * GLA_DV), lambda n: (ci(n), 0))
    ssp = lambda ci: pl.BlockSpec((cps, nh, GLA_DV, GLA_DK), lambda n: (ci(n), 0, 0, 0))
    o_shape = jax.ShapeDtypeStruct((T, nh * GLA_DV), f32)
    s_shape = jax.ShapeDtypeStruct((N, nh, GLA_DV, GLA_DK), MXU_DTYPE)
    return pl.pallas_call(
        body,
        out_shape=(o_shape, o_shape, s_shape, s_shape),
        grid=(NB,),
        in_specs=_gla_specs(cps * CHUNK, fw) + _gla_specs(cps * CHUNK, bw),
        out_specs=(osp(fw), osp(bw), ssp(fw), ssp(bw)),
        scratch_shapes=[pltpu.VMEM((2 * nh, GLA_DV, GLA_DK), f32)],
        compiler_params=_cparams(("arbitrary",)),
        name="gla_scan_fwd",
    )(proj, proj, proj, G_f, proj, proj, proj, G_b)


def _gla_bwd(proj, G_f, G_b, st_f, st_b, do):
    T = proj.shape[0]
    N = T // CHUNK
    nh = GLA_HEADS
    cps = min(GLA_CPS, N)
    NB = N // cps

    def body(qf, kf, vf, gf, sf_ref, dof_ref, qb, kb, vb, gb, sb_ref, dob_ref,
             dqf_ref, dkf_ref, dvf_ref, dgf_ref, dqb_ref, dkb_ref, dvb_ref, dgb_ref, ds_scr):
        n = pl.program_id(0)

        @pl.when(n == 0)
        def _():
            ds_scr[...] = jnp.zeros((2 * nh, GLA_DV, GLA_DK), f32)

        shp = (2 * nh, CHUNK, CHUNK)
        bb, ii, jj = _iota3(shp, 0), _iota3(shp, 1), _iota3(shp, 2)
        tri = jnp.where(jnp.logical_or(jnp.logical_and(bb < nh, jj >= ii), jnp.logical_and(bb >= nh, jj <= ii)), 1.0, 0.0)
        for s in range(cps):
            sf = cps - 1 - s
            rows = (_chunk_rows(sf), _chunk_rows(s))
            q, k, v, G = _gla_chain_inputs((qf, kf, vf, gf), (qb, kb, vb, gb), *rows)
            ck = _GlaBatch(q, k, G, nh)
            St = jnp.concatenate([sf_ref[sf], sb_ref[s]], axis=0).astype(f32)
            dSt = ds_scr[...]
            dov = jnp.stack(_heads(dof_ref, (), nh, GLA_DV, rows[0]) + _heads(dob_ref, (), nh, GLA_DV, rows[1]))
            d_attn = jnp.where(ck.incl, _mx(dov, v, "nt"), 0.0)
            d_qg = _mx(dov, St, "nn") + _mx(d_attn, ck.kg, "nn")
            d_kg = _mx(d_attn, ck.qg, "tn")
            d_v = _mx(ck.attn, dov, "tn") + _mx(ck.kd, dSt, "nt")
            d_kd = _mx(v, dSt, "nn")
            ds_scr[...] = dSt * ck.egl + _mx(dov, ck.qg, "tn")
            kdd = d_kd * ck.kd
            d_gl = jnp.sum(dSt * St, axis=1, keepdims=True) * ck.egl + jnp.sum(kdd, axis=1, keepdims=True)
            d_q = d_qg * ck.eG * (GLA_DK**-0.5)
            d_k = d_kg * ck.enG + d_kd * ck.ekd
            d_gk = _exact01(tri, _add_at_last_row(d_qg * ck.qg - d_kg * ck.kg - kdd, d_gl, nh))
            for d, (dq_ref, dk_ref, dv_ref, dg_ref) in enumerate(((dqf_ref, dkf_ref, dvf_ref, dgf_ref), (dqb_ref, dkb_ref, dvb_ref, dgb_ref))):
                for i in range(nh):
                    c = d * nh + i
                    ks, vs = slice(i * GLA_DK, (i + 1) * GLA_DK), slice(i * GLA_DV, (i + 1) * GLA_DV)
                    dq_ref[rows[d], ks] = d_q[c]
                    dk_ref[rows[d], ks] = d_k[c]
                    dv_ref[rows[d], vs] = d_v[c]
                    dg_ref[rows[d], ks] = d_gk[c]

    R = cps * CHUNK

    def in_specs(ci):
        return _gla_specs(R, ci) + [
            pl.BlockSpec((cps, nh, GLA_DV, GLA_DK), lambda n: (ci(n), 0, 0, 0)),
            pl.BlockSpec((R, nh * GLA_DV), lambda n: (ci(n), 0)),
        ]

    def out_specs(ci):
        ksp = pl.BlockSpec((R, nh * GLA_DK), lambda n: (ci(n), 0))
        return [ksp, ksp, pl.BlockSpec((R, nh * GLA_DV), lambda n: (ci(n), 0)), ksp]

    fw, bw = (lambda n: NB - 1 - n), (lambda n: n)
    k_shape = jax.ShapeDtypeStruct((T, nh * GLA_DK), f32)
    v_shape = jax.ShapeDtypeStruct((T, nh * GLA_DV), f32)
    return pl.pallas_call(
        body,
        out_shape=(k_shape, k_shape, v_shape, k_shape) * 2,
        grid=(NB,),
        in_specs=in_specs(fw) + in_specs(bw),
        out_specs=tuple(out_specs(fw) + out_specs(bw)),
        scratch_shapes=[pltpu.VMEM((2 * nh, GLA_DV, GLA_DK), f32)],
        compiler_params=_cparams(("arbitrary",)),
        name="gla_scan_bwd",
    )(proj, proj, proj, G_f, st_f, do, proj, proj, proj, G_b, st_b, do)


def _local_step(x, target, wcat, conv_w, avec, dvec, gdn_norm_w, w2f, w2b, b2f, b2b, gla_norm_w, proj_weights, w_pre, w_post,
                send_proj_grads=None, send_in_grads=None):
    h, h_t = _rms_pre(x, w_pre)
    proj = _matmul(h, wcat, "nt", f32, "proj_in", 2048, 512, 1024)
    qkvn = _gdn_prep(proj, conv_w)
    gates, gc, G_f, G_b = _gate_prep(proj, avec, dvec, w2f, w2b, b2f, b2b)
    oa_f, oa_b, sa_f, sa_b, inv_f, inv_b, sol_f, sol_b = _gdn_fwd(qkvn, gates, gc)
    og = _out_norm(oa_f, oa_b, proj, Z_BLK, gdn_norm_w, GDN_DIM, "gdn_out")
    ob_f, ob_b, sb_f, sb_b = _gla_fwd(proj, G_f, G_b)
    obg = _out_norm(ob_f, ob_b, proj, G_B_BLK, gla_norm_w, GLA_DV, "gla_out")
    wpg, wpl, wout = proj_weights(obg)
    (loss_row, d_wpost, dy, m, dout, dya, dyb, dproj, dog, dobg) = _mid(og, obg, proj, x, target, wpg, wpl, wout, w_post)

    d_wout = _matmul(m, dout, "tn", MXU_DTYPE, "dw_out", 1024, 512, 2048)
    d_wpg = _matmul(og, dya, "tn", MXU_DTYPE, "dw_proj_gdn", 1024, 512, 2048)
    d_wpl = _matmul(obg, dyb, "tn", MXU_DTYPE, "dw_proj_gla", 1024, 512, 2048)
    if send_proj_grads is not None:
        gla_norm_w = gla_norm_w + send_proj_grads(d_wpg, d_wpl, d_wout)

    do_b, dproj, d_gla_norm = _out_norm_bwd(dobg, ob_f, ob_b, proj, G_B_BLK, gla_norm_w, GLA_DV, "gla_out_bwd", dproj)
    dq_f, dk_f, dv_f, dgk_f, dq_b, dk_b, dv_b, dgk_b = _gla_bwd(proj, G_f, G_b, sb_f, sb_b, do_b)
    dproj, dsmall_gla, d_w2f, d_w2b, d_b2f, d_b2b = _gla_prep_bwd(proj, w2f, w2b, b2f, b2b, dgk_f, dgk_b,
                                                                 (dq_f, dq_b, dk_f, dk_b, dv_f, dv_b), dproj)

    do_a, dproj, d_gdn_norm = _out_norm_bwd(dog, oa_f, oa_b, proj, Z_BLK, gdn_norm_w, GDN_DIM, "gdn_out_bwd", dproj)
    dqkv_f, dqkv_b, dg_f, dg_b = _gdn_bwd(qkvn, gates, gc, sa_f, sa_b, inv_f, inv_b, sol_f, sol_b, do_a)
    dproj, d_conv = _gdn_prep_bwd(proj, conv_w, dqkv_f, dqkv_b, dproj)
    dproj, d_alog, d_dtb = _gdn_gates_bwd(proj, avec, dvec, dg_f, dg_b, dsmall_gla, dproj)

    d_wcat = _matmul(h_t, dproj, "nn", MXU_DTYPE, "dw_in", 1024, 512, 2048, out_t=True)
    started = None if send_in_grads is None else send_in_grads(d_wcat)
    dh = _matmul(dproj, wcat, "nn", f32, "dh", 512, 1024, N_CAT, after=started)
    grad_x, d_wpre = _rms_pre_bwd(x, w_pre, dh, dy)
    return dict(loss_row=loss_row, grad_x=grad_x, d_wcat=d_wcat, d_conv=d_conv, d_alog=d_alog, d_dtb=d_dtb,
                d_gdn_norm=d_gdn_norm, d_wpg=d_wpg, d_w2f=d_w2f, d_w2b=d_w2b, d_b2f=d_b2f, d_b2b=d_b2b,
                d_gla_norm=d_gla_norm, d_wpl=d_wpl, d_wout=d_wout, d_wpost=d_wpost, d_wpre=d_wpre)


def _to_cat(wf):
    pad = jnp.zeros((N_CAT - N_IN,) + wf.shape[1:], wf.dtype)
    return jnp.concatenate([wf[:4096], wf[4128:6176], wf[7232:9280], wf[6176:7200], wf[4096:4128], wf[7200:7232], pad], axis=0)


def _from_cat(wc):
    return jnp.concatenate([wc[:4096], wc[9216:9248], wc[4096:6144], wc[8192:9216], wc[9248:9280], wc[6144:8192]], axis=0)


def _lane_row(*pieces):
    row = jnp.concatenate(pieces, axis=1)
    return jnp.pad(row, ((0, 0), (0, LANES - row.shape[1])))


def _pad_w2(w2, row0):
    return jnp.pad(w2, ((row0, LANES - row0 - w2.shape[0]), (0, 0)))


SM_ROWS, SM_COLS = 32, 384


def _pack_small_shard(conv, w2f, w2b):
    lead = conv.shape[:-2]
    z = lambda r, c: jnp.zeros(lead + (r, c), f32)
    top = jnp.concatenate([conv, z(8 - CONV_K, SM_COLS)], axis=-2)
    mid = jnp.concatenate([w2f, w2b, z(16, SM_COLS - 128)], axis=-1)
    return jnp.concatenate([top, mid, z(SM_ROWS - 24, SM_COLS)], axis=-2)


SMALL_REGION = {
    "ln_pre_w": (0, 1, 0, 1024), "ln_post_w": (1, 1, 0, 1024), "gk_b2_fwd": (2, 1, 0, 512), "gk_b2_bwd": (2, 1, 512, 512),
    "gla_norm_w": (3, 1, 0, 256), "gdn_norm_w": (3, 1, 256, 128), "a_log_fwd": (4, 1, 0, 8), "a_log_bwd": (5, 1, 0, 8),
    "dt_bias_fwd": (6, 1, 0, 8), "dt_bias_bwd": (7, 1, 0, 8), "conv_w": (8, CONV_K, 0, SM_COLS),
    "gk_w2_fwd": (16, 16, 0, 64), "gk_w2_bwd": (16, 16, 128, 64),
}
SMALL_NAMES = tuple(SMALL_REGION)
SMALL_SHARDED = ("conv_w", "gk_w2_fwd", "gk_w2_bwd")
PACK_ROWS, PACK_COLS = 32, 1024
LOSS_AT = (13, 0)


def _region(name):
    r0, nr, l0, nl = SMALL_REGION[name]
    return slice(r0, r0 + nr), slice(l0, l0 + nl)


def _pack_small_grads(G):
    names = ("d_wpre", "d_wpost", "d_b2f", "d_b2b", "d_gla_norm", "d_gdn_norm", "d_alog", "d_dtb", "loss_row", "d_conv", "d_w2f", "d_w2b")

    def body(wpre, wpost, b2f, b2b, glan, gdnn, alog, dtb, lrow, conv, w2f, w2b, o_ref):
        o_ref[...] = jnp.zeros(o_ref.shape, f32)
        loss = 0.5 * jnp.sum(lrow[...], axis=1, keepdims=True) * (1.0 / D_MODEL)
        whole = {"ln_pre_w": wpre, "ln_post_w": wpost, "gk_b2_fwd": b2f, "gk_b2_bwd": b2b, "gla_norm_w": glan, "gdn_norm_w": gdnn}
        for j in range(N_DEV):
            for name, ref in whole.items():
                o_ref[(j,) + _region(name)] = ref[...]
            o_ref[(j,) + _region("a_log_fwd")] = alog[0:1, 0:8]
            o_ref[(j,) + _region("a_log_bwd")] = alog[0:1, 8:16]
            o_ref[(j,) + _region("dt_bias_fwd")] = dtb[0:1, 0:8]
            o_ref[(j,) + _region("dt_bias_bwd")] = dtb[0:1, 8:16]
            o_ref[(j,) + _region("conv_w")] = conv[:, j * SM_COLS : (j + 1) * SM_COLS]
            o_ref[(j,) + _region("gk_w2_fwd")] = w2f[32:48, j * 64 : (j + 1) * 64]
            o_ref[(j,) + _region("gk_w2_bwd")] = w2b[48:64, j * 64 : (j + 1) * 64]
            o_ref[j, LOSS_AT[0] : LOSS_AT[0] + 1, LOSS_AT[1] : LOSS_AT[1] + 1] = loss

    return pl.pallas_call(
        body, out_shape=jax.ShapeDtypeStruct((N_DEV, PACK_ROWS, PACK_COLS), f32), name="pack_small_grads",
    )(*[G[n] for n in names])


def _small_adam(land, own, W, M, V):
    bc1 = 1.0 - ADAM_B1**ADAM_STEP
    bc2 = 1.0 - ADAM_B2**ADAM_STEP
    n = len(SMALL_NAMES)

    def body(land_ref, own_ref, *refs):
        w_refs, m_refs, v_refs = refs[0:n], refs[n : 2 * n], refs[2 * n : 3 * n]
        outs, loss_ref, g_scr = refs[3 * n : 7 * n], refs[7 * n], refs[7 * n + 1]
        me = _my_device()
        g = jnp.where(me == 0, own_ref[0], land_ref[0])
        for j in range(1, N_DEV):
            g = g + jnp.where(me == j, own_ref[j], land_ref[j])
        g_scr[...] = g
        loss_ref[...] = g_scr[LOSS_AT[0] : LOSS_AT[0] + 1, LOSS_AT[1] : LOSS_AT[1] + 1]
        for i, name in enumerate(SMALL_NAMES):
            gp = g_scr[_region(name)]
            lead = (0,) if name in SMALL_SHARDED else (Ellipsis,)
            w, m, v = w_refs[i][lead], m_refs[i][lead], v_refs[i][lead]
            m2 = ADAM_B1 * m + (1.0 - ADAM_B1) * gp
            v2 = ADAM_B2 * v + (1.0 - ADAM_B2) * (gp * gp)
            delta = -ADAM_LR * ((m2 / bc1) / (jnp.sqrt(v2 / bc2) + ADAM_EPS) + ADAM_WD * w)
            for k, val in enumerate((gp, delta, m2, v2)):
                outs[4 * i + k][lead] = val

    shapes = [jax.ShapeDtypeStruct(W[name].shape, f32) for name in SMALL_NAMES for _ in range(4)]
    res = pl.pallas_call(
        body,
        out_shape=tuple(shapes) + (jax.ShapeDtypeStruct((1, 1), f32),),
        scratch_shapes=[pltpu.VMEM((PACK_ROWS, PACK_COLS), f32)],
        name="adam_small",
    )(land, own, *[P[name] for P in (W, M, V) for name in SMALL_NAMES])
    return {name: res[4 * i : 4 * i + 4] for i, name in enumerate(SMALL_NAMES)}, res[4 * n]


_MESH = pl.DeviceIdType.MESH


N_BIG_COPIES, N_SMALL_COPIES = 9, 7


def _gather_weights(w_in_s, sm_s):
    shapes = (w_in_s.shape, sm_s.shape)
    dtypes = (MXU_DTYPE, f32)

    def body(win_ref, sm_ref, gin_ref, gsm_ref, own_ref, send_sems, recv_sems, own_sem):
        x, y, c = lax.axis_index("x"), lax.axis_index("y"), lax.axis_index("c")
        me, sibling = (x, y, c), (x, y, 1 - c)
        chips = [(1 - x, y), (x, 1 - y), (1 - x, 1 - y)]

        def idx(px, py, pc):
            return 4 * px + 2 * py + pc

        for r in range(w_in_s.shape[1] // LANES):
            cols = slice(r * LANES, (r + 1) * LANES)
            own_ref[:, cols] = win_ref[:, cols].astype(MXU_DTYPE)
        keep_own = pltpu.make_async_copy(own_ref, gin_ref.at[idx(*me)], own_sem.at[0])
        keep_own.start()
        gsm_ref[idx(*me)] = sm_ref[...]

        def copy(sem, blk, to, src=None):
            return pltpu.make_async_remote_copy(src_ref=blk if src is None else src, dst_ref=blk, send_sem=send_sems.at[sem],
                                                recv_sem=recv_sems.at[sem], device_id=to, device_id_type=_MESH)

        def small(k, block, to):
            return copy(N_BIG_COPIES + k, gsm_ref.at[idx(*block)], to)

        half = w_in_s.shape[1] // 2
        xn, yn, dg = (1 - x, y), (x, 1 - y), (1 - x, 1 - y)

        def big(sem, block, to, part=None, src=None):
            blk = gin_ref.at[idx(*block)]
            return copy(sem, blk if part is None else blk.at[:, pl.ds(part * half, half)], to, src)

        started = [big(0, me, sibling, src=own_ref), big(1, me, (*xn, c), src=own_ref), big(2, me, (*yn, c), src=own_ref),
                   small(0, me, sibling)]
        started += [small(1 + j, me, (*chip, c)) for j, chip in enumerate(chips)]
        for cp in started:
            cp.start()

        def then(arrived, forwards):
            arrived.wait_recv()
            for cp in forwards:
                cp.start()
            started.extend(forwards)

        then(big(1, (*xn, c), me), [big(3, (*xn, c), (*yn, c), 1), big(5, (*xn, c), sibling)])
        then(big(2, (*yn, c), me), [big(4, (*yn, c), (*xn, c), 0), big(6, (*yn, c), sibling)])
        then(big(4, (*dg, c), me, 0), [big(7, (*dg, c), sibling, 0)])
        then(big(3, (*dg, c), me, 1), [big(8, (*dg, c), sibling, 1)])
        for j, chip in enumerate(chips):
            then(small(1 + j, (*chip, c), me), [small(4 + j, (*chip, c), sibling)])
        big(0, sibling, me).wait_recv()
        big(5, (*xn, 1 - c), me).wait_recv()
        big(6, (*yn, 1 - c), me).wait_recv()
        big(7, (*dg, 1 - c), me, 0).wait_recv()
        big(8, (*dg, 1 - c), me, 1).wait_recv()
        small(0, sibling, me).wait_recv()
        for j, chip in enumerate(chips):
            small(4 + j, (*chip, 1 - c), me).wait_recv()
        for cp in started:
            cp.wait_send()
        keep_own.wait()

    vm = pl.BlockSpec(memory_space=pltpu.VMEM)
    n_copies = N_BIG_COPIES + N_SMALL_COPIES
    return pl.pallas_call(
        body,
        out_shape=tuple(jax.ShapeDtypeStruct((N_DEV,) + s, d) for s, d in zip(shapes, dtypes)),
        in_specs=[vm, vm],
        out_specs=(_HBM_SPEC, vm),
        scratch_shapes=[pltpu.VMEM(w_in_s.shape, MXU_DTYPE), pltpu.SemaphoreType.DMA((n_copies,)), pltpu.SemaphoreType.DMA((n_copies,)),
                        pltpu.SemaphoreType.DMA((1,))],
        compiler_params=pltpu.CompilerParams(vmem_limit_bytes=VMEM_LIMIT),
        name="gather_weights",
    )(w_in_s, sm_s)


N_CHIP = 4
_EFFECT = pltpu.SideEffectType.DATAFLOW_SIDE_EFFECTING
_HBM_SPEC = pl.BlockSpec(memory_space=pltpu.HBM)
_SEM_SPEC = pl.BlockSpec(memory_space=pltpu.SEMAPHORE)


def _relation_peer(k, x, y, c):
    px = (1 - x) if (k & 4) else x
    py = (1 - y) if (k & 2) else y
    pc = (1 - c) if (k & 1) else c
    return (px, py, pc), 4 * px + 2 * py + pc


_PLAN_COPIES = {"gather": N_DEV - 1, "blocks": N_DEV - 1, "sibling": 4, "chips": 3}
_PLAN_SLOTS = {"gather": N_DEV, "blocks": N_DEV, "sibling": 4, "chips": 4}


def _plan_copies(plan, src_ref, land_ref, send_sems, recv_sems, arrival):
    x, y, c = lax.axis_index("x"), lax.axis_index("y"), lax.axis_index("c")
    me, my_chip = 4 * x + 2 * y + c, 2 * x + y
    out = []

    def add(i, src, there, here, dev):
        out.append(pltpu.make_async_remote_copy(src_ref=src, dst_ref=land_ref.at[here if arrival else there], send_sem=send_sems.at[i],
                                                recv_sem=recv_sems.at[i], device_id=dev, device_id_type=_MESH))

    if plan in ("gather", "blocks"):
        for k in range(1, N_DEV):
            dev, p = _relation_peer(k, x, y, c)
            add(k - 1, src_ref.at[p] if plan == "blocks" else src_ref, me, p, dev)
    elif plan == "sibling":
        for j in range(4):
            add(j, src_ref.at[2 * j + 1 - c], j, j, (x, y, 1 - c))
    else:
        for i, k in enumerate((2, 4, 6)):
            (px, py, pc), _ = _relation_peer(k, x, y, c)
            add(i, src_ref.at[2 * px + py], my_chip, 2 * px + py, (px, py, pc))
    return out


def _exchange_start(src, plan, name, after=None):
    n = _PLAN_COPIES[plan]
    land = lax.empty((_PLAN_SLOTS[plan],) + (src.shape if plan == "gather" else src.shape[1:]), src.dtype)
    extra = [] if after is None else [after]

    def body(src_ref, land_ref, *rest):
        send_sems, recv_sems, _, _, token = rest[len(extra) :]
        for cp in _plan_copies(plan, src_ref, land_ref, send_sems, recv_sems, False):
            cp.start()
        token[...] = jnp.zeros_like(token)

    return pl.pallas_call(
        body,
        name=name,
        out_shape=(pltpu.SemaphoreType.DMA((n,)), pltpu.SemaphoreType.DMA((n,)), pltpu.HBM(src.shape, src.dtype),
                   pltpu.HBM(land.shape, land.dtype), jax.ShapeDtypeStruct((8, LANES), f32)),
        in_specs=(_HBM_SPEC, _HBM_SPEC) + (_ANY_SPEC,) * len(extra),
        out_specs=(_SEM_SPEC, _SEM_SPEC, _HBM_SPEC, _HBM_SPEC, pl.BlockSpec(memory_space=pltpu.VMEM)),
        input_output_aliases={0: 2, 1: 3},
        compiler_params=pltpu.CompilerParams(has_side_effects=_EFFECT),
    )(pltpu.with_memory_space_constraint(src, pltpu.HBM), pltpu.with_memory_space_constraint(land, pltpu.HBM), *extra)


def _exchange_wait(started, after, plan, name):
    send_sems, recv_sems, src_thru, land_thru, _ = started

    def body(src_ref, land_ref, send_sems, recv_sems, after_ref, src_dead, got_ref):
        for cp in _plan_copies(plan, src_ref, land_ref, send_sems, recv_sems, True):
            cp.wait_send()
            cp.wait_recv()

    return pl.pallas_call(
        body,
        name=name,
        out_shape=(pltpu.HBM(src_thru.shape, src_thru.dtype), pltpu.HBM(land_thru.shape, land_thru.dtype)),
        in_specs=(_HBM_SPEC, _HBM_SPEC, _SEM_SPEC, _SEM_SPEC, pl.BlockSpec(memory_space=pl.ANY)),
        out_specs=(_HBM_SPEC, _HBM_SPEC),
        input_output_aliases={0: 0, 1: 1},
        compiler_params=pltpu.CompilerParams(has_side_effects=_EFFECT),
    )(src_thru, land_thru, send_sems, recv_sems, after)


def _pair_sum(g, r, name, tc=LANES):
    _, R, Cc = g.shape

    def body(g_ref, r_ref, o_ref):
        o_ref[...] = (g_ref[...].astype(f32) + r_ref[...].astype(f32)).astype(o_ref.dtype)

    return pl.pallas_call(
        body,
        out_shape=jax.ShapeDtypeStruct(r.shape, r.dtype),
        grid=(N_CHIP, Cc // tc),
        in_specs=[pl.BlockSpec((1, R, tc), lambda k, i: (2 * k + lax.axis_index("c"), 0, i)), pl.BlockSpec((1, R, tc), lambda k, i: (k, 0, i))],
        out_specs=pl.BlockSpec((1, R, tc), lambda k, i: (k, 0, i)),
        compiler_params=_cparams(("parallel", "parallel")),
        name=name,
    )(g, r)


def _sum_adam(parts, w, m, v, name, tb, tc=None, own=None, own_slot=None):
    R, Cc = w.shape
    n_parts = parts.shape[0]
    tb = R if tc else min(tb, R)
    tc = tc or Cc
    assert R % tb == 0 and Cc % tc == 0
    bc1 = 1.0 - ADAM_B1**ADAM_STEP
    bc2 = 1.0 - ADAM_B2**ADAM_STEP
    extra = [] if own is None else [own]

    def body(p_ref, w_ref, m_ref, v_ref, *rest):
        g_ref, d_ref, nm_ref, nv_ref = rest[len(extra) :]

        def part(j):
            pj = p_ref[j].astype(f32)
            return pj if own is None else jnp.where(own_slot() == j, rest[0][0].astype(f32), pj)

        g = part(0)
        for j in range(1, n_parts):
            g = g + part(j)
        g_ref[...] = g
        m2 = ADAM_B1 * m_ref[...] + (1.0 - ADAM_B1) * g
        v2 = ADAM_B2 * v_ref[...] + (1.0 - ADAM_B2) * (g * g)
        nm_ref[...] = m2
        nv_ref[...] = v2
        d_ref[...] = -ADAM_LR * ((m2 / bc1) / (jnp.sqrt(v2 / bc2) + ADAM_EPS) + ADAM_WD * w_ref[...])

    blk = pl.BlockSpec((tb, tc), lambda i, j: (i, j))
    o = jax.ShapeDtypeStruct((R, Cc), f32)
    return pl.pallas_call(
        body,
        out_shape=(o, o, o, o),
        grid=(R // tb, Cc // tc),
        in_specs=[pl.BlockSpec((n_parts, tb, tc), lambda i, j: (0, i, j)), blk, blk, blk]
        + [pl.BlockSpec((1, tb, tc), lambda i, j: (own_slot(), i, j))] * len(extra),
        out_specs=(blk, blk, blk, blk),
        compiler_params=_cparams(("parallel", "parallel")),
        name=name,
    )(parts, w, m, v, *extra)


def _my_device():
    return 4 * lax.axis_index("x") + 2 * lax.axis_index("y") + lax.axis_index("c")


def _my_chip():
    return 2 * lax.axis_index("x") + lax.axis_index("y")


_WEIGHTS = ["ln_pre_w", "w_in", "conv_w", "a_log_fwd", "a_log_bwd", "dt_bias_fwd", "dt_bias_bwd", "gdn_norm_w", "w_proj_gdn",
            "gk_w2_fwd", "gk_b2_fwd", "gk_w2_bwd", "gk_b2_bwd", "gla_norm_w", "w_proj_gla", "w_out", "ln_post_w"]


def kernel(x, ln_pre_w, w_in, conv_w, a_log_fwd, a_log_bwd, dt_bias_fwd, dt_bias_bwd, gdn_norm_w, w_proj_gdn, gk_w2_fwd, gk_b2_fwd, gk_w2_bwd, gk_b2_bwd, gla_norm_w, w_proj_gla, w_out, ln_post_w, loss_target, m_ln_pre_w, m_w_in, m_conv_w, m_a_log_fwd, m_a_log_bwd, m_dt_bias_fwd, m_dt_bias_bwd, m_gdn_norm_w, m_w_proj_gdn, m_gk_w2_fwd, m_gk_b2_fwd, m_gk_w2_bwd, m_gk_b2_bwd, m_gla_norm_w, m_w_proj_gla, m_w_out, m_ln_post_w, v_ln_pre_w, v_w_in, v_conv_w, v_a_log_fwd, v_a_log_bwd, v_dt_bias_fwd, v_dt_bias_bwd, v_gdn_norm_w, v_w_proj_gdn, v_gk_w2_fwd, v_gk_b2_fwd, v_gk_w2_bwd, v_gk_b2_bwd, v_gla_norm_w, v_w_proj_gla, v_w_out, v_ln_post_w):
    args = locals()
    W = {n: args[n] for n in _WEIGHTS}
    M = {n: args["m_" + n] for n in _WEIGHTS}
    V = {n: args["v_" + n] for n in _WEIGHTS}

    wp_stack = lambda P: jnp.concatenate([P["w_proj_gdn"], P["w_proj_gla"], P["w_out"]], axis=0)
    sm_pack = lambda P: _pack_small_shard(P["conv_w"][0], P["gk_w2_fwd"][0], P["gk_w2_bwd"][0])
    me = 4 * lax.axis_index("x") + 2 * lax.axis_index("y") + lax.axis_index("c")
    w_in_t, m_in_t, v_in_t = (jnp.transpose(a[0]) for a in (w_in, m_w_in, v_w_in))
    g_in, g_sm = _gather_weights(w_in_t, sm_pack(W))
    wp_own = wp_stack(W).astype(MXU_DTYPE)
    wp_started = _exchange_start(wp_own, "gather", "gather_proj_start", after=g_sm)

    def proj_weights(after):
        own, land = _exchange_wait(wp_started, after, "gather", "gather_proj_wait")
        g_p = lax.dynamic_update_slice(land, own[None], (me, 0, 0, 0))
        return tuple(g_p[:, i].reshape(D_MODEL, D_MODEL) for i in range(3))

    rows_p = 3 * D_MODEL // N_DEV
    sent = {}

    def send_proj_grads(d_wpg, d_wpl, d_wout):
        p_p = jnp.stack([d_wpg, d_wpl, d_wout]).reshape(3, N_DEV, D_MODEL // N_DEV, D_MODEL).transpose(1, 0, 2, 3)
        sent["p_p"] = p_p.reshape(N_DEV, rows_p, D_MODEL)
        sent["started"] = _exchange_start(sent["p_p"], "blocks", "exchange_proj_start")
        return sent["started"][4][0:1, 0:1]

    def send_in_grads(d_wcat):
        p_in = _from_cat(d_wcat).reshape(N_DEV, SHARD_IN, D_MODEL)
        sib = _exchange_start(p_in, "sibling", "exchange_in_sibling_start")
        p_p, land_p = _exchange_wait(sent["started"], sib[4], "blocks", "exchange_proj_wait")
        sent["o_p"] = _sum_adam(land_p, wp_stack(W).reshape(rows_p, D_MODEL), wp_stack(M).reshape(rows_p, D_MODEL),
                                wp_stack(V).reshape(rows_p, D_MODEL), "adam_w_proj", 128, own=p_p, own_slot=_my_device)
        p_in, sib_land = _exchange_wait(sib, sent["o_p"][0], "sibling", "exchange_in_sibling_wait")
        sent["chips"] = _exchange_start(_pair_sum(p_in, sib_land, "pair_sum_w_in", tc=512), "chips", "exchange_in_chips_start")
        return sent["chips"][4]

    wcat = _to_cat(g_in.reshape(N_IN, D_MODEL))
    conv_full = g_sm[:, 0:CONV_K, :].transpose(1, 0, 2).reshape(CONV_K, N_DEV * SM_COLS)
    w2f_full = g_sm[:, 8:24, 0:64].transpose(1, 0, 2).reshape(16, 512)
    w2b_full = g_sm[:, 8:24, 64:128].transpose(1, 0, 2).reshape(16, 512)

    avec = _lane_row(a_log_fwd, a_log_bwd)
    dvec = _lane_row(dt_bias_fwd, dt_bias_bwd)
    G = _local_step(x[0], loss_target[0], wcat, conv_full, avec, dvec, gdn_norm_w, _pad_w2(w2f_full, 32), _pad_w2(w2b_full, 48),
                    gk_b2_fwd, gk_b2_bwd, gla_norm_w, proj_weights, ln_pre_w + wp_started[4][0:1, 0:1], ln_post_w, send_proj_grads,
                    send_in_grads)

    small_started = _exchange_start(_pack_small_grads(G), "blocks", "exchange_small_start")
    q_in, land_in = _exchange_wait(sent["chips"], small_started[4], "chips", "exchange_in_chips_wait")
    o_in = _sum_adam(land_in, w_in_t, m_in_t, v_in_t, "adam_w_in", SHARD_IN, tc=256, own=q_in, own_slot=_my_chip)
    o_p = sent["o_p"]
    p_small, land_small = _exchange_wait(small_started, o_in[0], "blocks", "exchange_small_wait")
    o_small, loss = _small_adam(land_small, p_small, W, M, V)

    res = []
    for kind in range(4):
        per = {name: o_small[name][kind] for name in SMALL_NAMES}
        per["w_in"] = jnp.transpose(o_in[kind])[None]
        pp = o_p[kind].reshape(3, 1, D_MODEL // N_DEV, D_MODEL)
        per["w_proj_gdn"], per["w_proj_gla"], per["w_out"] = pp[0], pp[1], pp[2]
        res.append([per[n] for n in _WEIGHTS])
    return (loss[0, 0], G["grad_x"][None], *res[0], *res[1], *res[2], *res[3])
```

```python
from functools import partial

import jax
import jax.numpy as jnp
from jax import lax
from jax.experimental import pallas as pl
from jax.experimental.pallas import tpu as pltpu

f32 = jnp.float32
MXU_DTYPE = jnp.bfloat16

D_MODEL = 1024
CHUNK = 64
NORM_EPS = 1e-6
GDN_HEADS, GDN_DIM = 8, 128
GLA_HEADS, GLA_DK, GLA_DV = 4, 128, 256
GATE_NORMALIZER = 16.0
CONV_K = 5
N_IN = 9280
N_DEV = 8
SHARD_IN = N_IN // N_DEV

N_CAT = 9728
SMALL_OFF = 9216
LANES = 128

ADAM_LR, ADAM_B1, ADAM_B2, ADAM_EPS, ADAM_WD, ADAM_STEP = 0.001, 0.9, 0.999, 1e-08, 0.01, 10

VMEM_LIMIT = 56 * 1024 * 1024


def _cparams(sem=None):
    return pltpu.CompilerParams(dimension_semantics=sem, vmem_limit_bytes=VMEM_LIMIT)


_DN = {"nn": (((1,), (0,)), ((), ())), "nt": (((1,), (1,)), ((), ())), "tn": (((0,), (0,)), ((), ()))}
_BDN = {"nn": (((2,), (1,)), ((0,), (0,))), "nt": (((2,), (2,)), ((0,), (0,))), "tn": (((1,), (1,)), ((0,), (0,)))}


def _dot16(a, b, kind):
    return lax.dot_general(a, b, (_DN if a.ndim == 2 else _BDN)[kind], preferred_element_type=f32)


def _mx(a, b, kind):
    return _dot16(a.astype(MXU_DTYPE), b.astype(MXU_DTYPE), kind)


def _split2(a):
    hi = a.astype(jnp.bfloat16)
    return hi, (a - hi.astype(f32)).astype(jnp.bfloat16)


def _split3(a):
    p1 = a.astype(jnp.bfloat16)
    r = a - p1.astype(f32)
    p2 = r.astype(jnp.bfloat16)
    return p1, p2, (r - p2.astype(f32)).astype(jnp.bfloat16)


def _h3s(a2, b2, kind):
    (ah, al), (bh, bl) = a2, b2
    return _dot16(ah, bh, kind) + (_dot16(ah, bl, kind) + _dot16(al, bh, kind))


def _h3(a, b, kind):
    return _h3s(_split2(a), _split2(b), kind)


def _exact01(t01, x, kind="nn"):
    t = t01.astype(jnp.bfloat16)
    x1, x2, x3 = _split3(x)
    return _dot16(t, x1, kind) + (_dot16(t, x2, kind) + _dot16(t, x3, kind))


def _colsum_col(e):
    ones = jnp.ones(e.shape[:-1] + (LANES,), jnp.bfloat16)
    e1, e2, e3 = _split3(e)
    return (_dot16(e1, ones, "tn") + (_dot16(e2, ones, "tn") + _dot16(e3, ones, "tn")))[..., 0:1]


def _iota3(shape, axis):
    return lax.broadcasted_iota(jnp.int32, shape, axis)


def _dir_masks(B, nf, Cn):
    shp = (B, Cn, Cn)
    bb, ii, jj = _iota3(shp, 0), _iota3(shp, 1), _iota3(shp, 2)
    fwd = bb < nf
    rev = jnp.logical_not(fwd)
    incl = jnp.logical_or(jnp.logical_and(fwd, ii >= jj), jnp.logical_and(rev, ii <= jj))
    strict = jnp.logical_and(incl, ii != jj)
    return incl, strict, ii == jj


def _last_row(x, nf):
    B, Cn = x.shape[0], x.shape[1]
    fwd = _iota3((B, 1, 1), 0) < nf
    return jnp.where(fwd, x[:, Cn - 1 : Cn, :], x[:, 0:1, :])


def _add_at_last_row(x, val, nf):
    B, Cn = x.shape[0], x.shape[1]
    bb, rr = _iota3((B, Cn, 1), 0), _iota3((B, Cn, 1), 1)
    at = jnp.logical_or(jnp.logical_and(bb < nf, rr == Cn - 1), jnp.logical_and(bb >= nf, rr == 0))
    return x + jnp.where(at, val, 0.0)


def _sigmoid(x):
    return jax.nn.sigmoid(x)


def _silu(x):
    return x * _sigmoid(x)


def _silu_grad(x):
    s = _sigmoid(x)
    return s * (1.0 + x * (1.0 - s))


def _softplus(x):
    u = jnp.exp(-jnp.abs(x))
    l1p = jnp.where(u < 1e-3, u * (1.0 - u * (0.5 - u * (1.0 / 3.0))), jnp.log(1.0 + u))
    return jnp.maximum(x, 0.0) + l1p


def _mxr(x):
    return x.astype(MXU_DTYPE).astype(f32)


def _iota2(shape, axis):
    return lax.broadcasted_iota(jnp.int32, shape, axis)


def _matmul(a, b, kind, out_dtype, name, tm, tn, tk, out_t=False, after=None, b_cols=None):
    extra = [] if after is None else [after]
    first = 0
    if b_cols is not None:
        assert kind == "nn" and b.shape[1] % tn == 0
        first, (M, K), N = b_cols[0], a.shape, b_cols[1] * tn
    elif kind == "nn":
        (M, K), N = a.shape, b.shape[1]
    elif kind == "nt":
        (M, K), N = a.shape, b.shape[0]
    else:
        (K, M), N = a.shape, b.shape[1]
    tm, tn, tk = min(tm, M), min(tn, N), min(tk, K)
    assert M % tm == 0 and N % tn == 0 and K % tk == 0, (name, M, N, K)
    nk = K // tk
    if kind == "tn":
        a_spec = pl.BlockSpec((tk, tm), lambda i, j, k: (k, i))
    else:
        a_spec = pl.BlockSpec((tm, tk), lambda i, j, k: (i, k))
    b_mode = dict(pipeline_mode=pl.Buffered(1)) if (tn == N and tk == K) else {}
    if kind == "nt":
        b_spec = pl.BlockSpec((tn, tk), lambda i, j, k: (j, k), **b_mode)
    else:
        b_spec = pl.BlockSpec((tk, tn), lambda i, j, k: (k, j + first), **b_mode)

    def body(a_ref, b_ref, *rest):
        o_ref, acc = rest[len(extra)], rest[len(extra) + 1 :]
        p = _mx(a_ref[...], b_ref[...], kind)
        if nk == 1:
            o_ref[...] = (p.T if out_t else p).astype(out_dtype)
        else:
            assert not out_t
            acc_ref = acc[0]
            k = pl.program_id(2)

            @pl.when(k == 0)
            def _():
                acc_ref[...] = p

            @pl.when(k > 0)
            def _():
                acc_ref[...] += p

            @pl.when(k == nk - 1)
            def _():
                o_ref[...] = acc_ref[...].astype(out_dtype)

    return pl.pallas_call(
        body,
        out_shape=jax.ShapeDtypeStruct((N, M) if out_t else (M, N), out_dtype),
        grid=(M // tm, N // tn, nk),
        in_specs=[a_spec, b_spec] + [pl.BlockSpec((8, LANES), lambda i, j, k: (0, 0))] * len(extra),
        out_specs=pl.BlockSpec((tn, tm), lambda i, j, k: (j, i)) if out_t else pl.BlockSpec((tm, tn), lambda i, j, k: (i, j)),
        scratch_shapes=[] if nk == 1 else [pltpu.VMEM((tm, tn), f32)],
        compiler_params=_cparams(("parallel", "parallel", "arbitrary")),
        name=name,
    )(a, b, *extra)


ROW_TILE = 512


def _rms_pre(x, w):
    T = x.shape[0]
    tb = min(ROW_TILE, T)

    def body(x_ref, w_ref, h_ref, ht_ref):
        xv = x_ref[...]
        r = lax.rsqrt(jnp.mean(xv * xv, axis=1, keepdims=True) + NORM_EPS)
        h = xv * r * w_ref[...]
        h_ref[...] = h.astype(h_ref.dtype)
        ht_ref[...] = h.T.astype(ht_ref.dtype)

    return pl.pallas_call(
        body,
        out_shape=(jax.ShapeDtypeStruct((T, D_MODEL), MXU_DTYPE), jax.ShapeDtypeStruct((D_MODEL, T), MXU_DTYPE)),
        grid=(T // tb,),
        in_specs=[pl.BlockSpec((tb, D_MODEL), lambda i: (i, 0)), pl.BlockSpec((1, D_MODEL), lambda i: (0, 0))],
        out_specs=(pl.BlockSpec((tb, D_MODEL), lambda i: (i, 0)), pl.BlockSpec((D_MODEL, tb), lambda i: (0, i))),
        compiler_params=_cparams(("parallel",)),
        name="rms_pre",
    )(x, w)


def _rms_pre_bwd(x, w, dh, dy):
    T = x.shape[0]
    tb = min(ROW_TILE, T)

    def body(x_ref, w_ref, dh_ref, dy_ref, dx_ref, dw_ref):
        i = pl.program_id(0)
        xv, dhv = x_ref[...], dh_ref[...]
        r = lax.rsqrt(jnp.mean(xv * xv, axis=1, keepdims=True) + NORM_EPS)
        dhw = dhv * w_ref[...]
        dx_ref[...] = dy_ref[...] + r * dhw - xv * (r * r * r) * jnp.mean(dhw * xv, axis=1, keepdims=True)
        part = jnp.sum(dhv * xv * r, axis=0, keepdims=True)

        @pl.when(i == 0)
        def _():
            dw_ref[...] = part

        @pl.when(i > 0)
        def _():
            dw_ref[...] += part

    blk = pl.BlockSpec((tb, D_MODEL), lambda i: (i, 0))
    row = pl.BlockSpec((1, D_MODEL), lambda i: (0, 0))
    return pl.pallas_call(
        body,
        out_shape=(jax.ShapeDtypeStruct((T, D_MODEL), f32), jax.ShapeDtypeStruct((1, D_MODEL), f32)),
        grid=(T // tb,),
        in_specs=[blk, row, blk, blk],
        out_specs=(blk, row),
        compiler_params=_cparams(("arbitrary",)),
        name="rms_pre_bwd",
    )(x, w, dh, dy)


SHIFTS = (-2, -1, 1, 2)


SHIFT_PAD = 8
PREP_HEADS, PREP_HEADS_BWD = 4, 1


def _shifted(x, pad_ref):
    T, cb = x.shape
    pad_ref[0:SHIFT_PAD] = jnp.zeros((SHIFT_PAD, cb), f32)
    pad_ref[SHIFT_PAD + T :] = jnp.zeros((SHIFT_PAD, cb), f32)
    pad_ref[SHIFT_PAD : SHIFT_PAD + T] = x
    return {d: pad_ref[SHIFT_PAD + d : SHIFT_PAD + d + T] for d in SHIFTS}


def _conv5(u, ush, cw):
    acc = u * cw[2:3, :]
    for j in (0, 1, 3, 4):
        acc = acc + ush[j - 2] * cw[j : j + 1, :]
    return acc


def _gdn_prep(proj, conv_w):
    T = proj.shape[0]
    hps, W = PREP_HEADS, PREP_HEADS * GDN_DIM
    per = GDN_HEADS // hps

    def body(u_ref, cw_ref, o_ref, pad_ref):
        j = pl.program_id(0)
        for i in range(hps):
            sl = slice(i * GDN_DIM, (i + 1) * GDN_DIM)
            u = _mxr(u_ref[:, sl])
            s = _silu(_conv5(u, _shifted(u, pad_ref), _mxr(cw_ref[:, sl])))
            o_ref[0, :, sl] = jnp.where(j < 2 * per, s * lax.rsqrt(jnp.sum(s * s, axis=1, keepdims=True) + NORM_EPS), s)

    return pl.pallas_call(
        body,
        out_shape=jax.ShapeDtypeStruct((3, T, GDN_HEADS * GDN_DIM), f32),
        grid=(3 * per,),
        in_specs=[pl.BlockSpec((T, W), lambda j: (0, j)), pl.BlockSpec((CONV_K, W), lambda j: (0, j))],
        out_specs=pl.BlockSpec((1, T, W), lambda j: (j // per, 0, j % per)),
        scratch_shapes=[pltpu.VMEM((T + 2 * SHIFT_PAD, GDN_DIM), f32)],
        compiler_params=_cparams(("parallel",)),
        name="gdn_prep",
    )(proj, conv_w)


def _gdn_prep_bwd(proj, conv_w, dqkv_f, dqkv_b, dproj, after):
    T = proj.shape[0]
    hps, W = PREP_HEADS_BWD, PREP_HEADS_BWD * GDN_DIM
    per = GDN_HEADS // hps

    def body(u_ref, cw_ref, df_ref, db_ref, _, after_ref, du_ref, dcw_ref, upad_ref, dcpad_ref):
        j = pl.program_id(0)
        for i in range(hps):
            sl = slice(i * GDN_DIM, (i + 1) * GDN_DIM)
            u, cw = _mxr(u_ref[:, sl]), _mxr(cw_ref[:, sl])
            ush = _shifted(u, upad_ref)
            c = _conv5(u, ush, cw)
            s = _silu(c)
            dn = df_ref[0, :, sl] + db_ref[0, :, sl]
            rinv = lax.rsqrt(jnp.sum(s * s, axis=1, keepdims=True) + NORM_EPS)
            ds_norm = rinv * dn - s * (rinv * rinv * rinv) * jnp.sum(dn * s, axis=1, keepdims=True)
            ds = jnp.where(j < 2 * per, ds_norm, dn)
            dc = _mxr(ds * _silu_grad(c))
            dcsh = _shifted(dc, dcpad_ref)
            du = dc * cw[2:3, :]
            for jj in range(CONV_K):
                d = jj - 2
                dcw_ref[jj : jj + 1, sl] = jnp.sum(dc * (u if d == 0 else ush[d]), axis=0, keepdims=True)
                if d != 0:
                    du = du + dcsh[-d] * cw[jj : jj + 1, :]
            du_ref[:, sl] = du.astype(du_ref.dtype)

    qspec = pl.BlockSpec((1, T, W), lambda j: (j // per, 0, j % per))
    return pl.pallas_call(
        body,
        out_shape=(jax.ShapeDtypeStruct(dproj.shape, dproj.dtype), jax.ShapeDtypeStruct((CONV_K, 3 * GDN_HEADS * GDN_DIM), f32)),
        grid=(3 * per,),
        in_specs=[pl.BlockSpec((T, W), lambda j: (0, j)), pl.BlockSpec((CONV_K, W), lambda j: (0, j)), qspec, qspec, _ANY_SPEC,
                  _ANY_SPEC],
        out_specs=(pl.BlockSpec((T, W), lambda j: (0, j)), pl.BlockSpec((CONV_K, W), lambda j: (0, j))),
        input_output_aliases={4: 0},
        scratch_shapes=[pltpu.VMEM((T + 2 * SHIFT_PAD, GDN_DIM), f32)] * 2,
        compiler_params=_cparams(("parallel",)),
        name="gdn_prep_bwd",
    )(proj, conv_w, dqkv_f, dqkv_b, dproj, after)


def _chunk_tri(n, rev):
    i, j = _iota2((n, n), 0), _iota2((n, n), 1)
    same = jnp.right_shift(i, 6) == jnp.right_shift(j, 6)
    order = (j >= i) if rev else (j <= i)
    return jnp.where(jnp.logical_and(same, order), 1.0, 0.0).astype(f32)


def _gate_prep(proj, avec, dvec, w2f, w2b, b2f, b2b, tb=256):
    T = proj.shape[0]
    W = GLA_HEADS * GLA_DK

    def body(s_ref, a_ref, d_ref, wf_ref, wb_ref, bf_ref, bb_ref, g_ref, gc_ref, gf_ref, gb_ref):
        small = s_ref[...]
        tri_f, tri_r = _chunk_tri(tb, False), _chunk_tri(tb, True)
        lane = _iota2(small.shape, 1)
        lg = -jnp.exp(a_ref[...]) * _softplus(small + d_ref[...])
        beta = _sigmoid(small)
        g_ref[...] = jnp.where(lane < 16, lg, jnp.where(lane < 32, beta, 0.0))
        lgm = jnp.where(lane < 16, lg, 0.0)
        gc_ref[...] = jnp.where(lane < 8, _exact01(tri_f, lgm), _exact01(tri_r, lgm))
        gkf = -_softplus(-(_mx(small, wf_ref[...], "nn") + bf_ref[...])) * (1.0 / GATE_NORMALIZER)
        gkb = -_softplus(-(_mx(small, wb_ref[...], "nn") + bb_ref[...])) * (1.0 / GATE_NORMALIZER)
        gf_ref[...] = _exact01(tri_f, gkf)
        gb_ref[...] = _exact01(tri_r, gkb)

    blk = pl.BlockSpec((tb, LANES), lambda i: (i, 0))
    row = pl.BlockSpec((1, LANES), lambda i: (0, 0))
    wide = pl.BlockSpec((tb, W), lambda i: (i, 0))
    wsp = pl.BlockSpec((LANES, W), lambda i: (0, 0))
    wrow = pl.BlockSpec((1, W), lambda i: (0, 0))
    return pl.pallas_call(
        body,
        out_shape=(jax.ShapeDtypeStruct((T, LANES), f32), jax.ShapeDtypeStruct((T, LANES), f32),
                   jax.ShapeDtypeStruct((T, W), f32), jax.ShapeDtypeStruct((T, W), f32)),
        grid=(T // tb,),
        in_specs=[pl.BlockSpec((tb, LANES), lambda i: (i, SMALL_OFF // LANES)), row, row, wsp, wsp, wrow, wrow],
        out_specs=(blk, blk, wide, wide),
        compiler_params=_cparams(("parallel",)),
        name="gate_prep",
    )(proj, avec, dvec, w2f, w2b, b2f, b2b)


def _gdn_gates_bwd(proj, avec, dvec, dg_f, dg_b, dsmall_gla, dproj):
    T = proj.shape[0]
    tb = min(ROW_TILE, T)
    pad = N_CAT - SMALL_OFF

    def body(s_ref, a_ref, d_ref, gf_ref, gb_ref, dl_ref, _, ds_ref, da_ref, dd_ref):
        i = pl.program_id(0)
        small = s_ref[...]
        lane = _iota2(small.shape, 1)
        dgate = gf_ref[...] + gb_ref[...]
        z = small + d_ref[...]
        nega = -jnp.exp(a_ref[...])
        dz = dgate * nega * _sigmoid(z)
        beta = _sigmoid(small)
        dsm = jnp.where(lane < 16, dz, jnp.where(lane < 32, dgate * beta * (1.0 - beta), 0.0))
        ds_ref[:, :LANES] = (dsm + dl_ref[...]).astype(ds_ref.dtype)
        ds_ref[:, LANES:] = jnp.zeros((tb, pad - LANES), ds_ref.dtype)
        lg = nega * _softplus(z)
        pa = jnp.sum(jnp.where(lane < 16, dgate * lg, 0.0), axis=0, keepdims=True)
        pd = jnp.sum(jnp.where(lane < 16, dz, 0.0), axis=0, keepdims=True)

        @pl.when(i == 0)
        def _():
            da_ref[...] = pa
            dd_ref[...] = pd

        @pl.when(i > 0)
        def _():
            da_ref[...] += pa
            dd_ref[...] += pd

    blk = pl.BlockSpec((tb, LANES), lambda i: (i, 0))
    row = pl.BlockSpec((1, LANES), lambda i: (0, 0))
    return pl.pallas_call(
        body,
        out_shape=(jax.ShapeDtypeStruct(dproj.shape, dproj.dtype), jax.ShapeDtypeStruct((1, LANES), f32), jax.ShapeDtypeStruct((1, LANES), f32)),
        grid=(T // tb,),
        in_specs=[pl.BlockSpec((tb, LANES), lambda i: (i, SMALL_OFF // LANES)), row, row, blk, blk, blk, _ANY_SPEC],
        out_specs=(pl.BlockSpec((tb, pad), lambda i: (i, DP_SMALL_BLK)), row, row),
        input_output_aliases={6: 0},
        compiler_params=_cparams(("arbitrary",)),
        name="gdn_gates_bwd",
    )(proj, avec, dvec, dg_f, dg_b, dsmall_gla, dproj)


GDN_HB = 8


def _inv_unit_lower(L, eye):
    A = eye - L
    P2 = _split2(L)
    for _ in range(3):
        P2 = _split2(_h3s(P2, P2, "nn"))
        A = A + _h3s(_split2(A), P2, "nn")
    Ph = P2[0]
    for _ in range(2):
        Ph = _dot16(Ph, Ph, "nn").astype(jnp.bfloat16)
        A = A + _dot16(A.astype(jnp.bfloat16), Ph, "nn")
    return A


def _lane_col(blk, idx):
    lane = _iota2(blk.shape, 1)
    return jnp.sum(jnp.where(lane == idx, blk, 0.0), axis=1, keepdims=True)


def _out_norm(o_f, o_b, proj, gate_blk, w, hd, name):
    T, W = o_f.shape
    tb = min(ROW_TILE, T)
    nh = W // hd

    def body(of_ref, ob_ref, z_ref, w_ref, y_ref):
        wv = w_ref[...]
        for i in range(nh):
            sl = slice(i * hd, (i + 1) * hd)
            o = of_ref[:, sl] + ob_ref[:, sl]
            r = lax.rsqrt(jnp.mean(o * o, axis=1, keepdims=True) + NORM_EPS)
            y_ref[:, sl] = (o * r * wv * _silu(z_ref[:, sl])).astype(y_ref.dtype)

    blk = pl.BlockSpec((tb, W), lambda i: (i, 0))
    return pl.pallas_call(
        body,
        out_shape=jax.ShapeDtypeStruct((T, W), MXU_DTYPE),
        grid=(T // tb,),
        in_specs=[blk, blk, pl.BlockSpec((tb, W), lambda i: (i, gate_blk)), pl.BlockSpec((1, hd), lambda i: (0, 0))],
        out_specs=blk,
        compiler_params=_cparams(("parallel",)),
        name=name,
    )(o_f, o_b, proj, w)


def _out_norm_bwd(dy, o_f, o_b, proj, gate_blk, w, hd, name, dproj):
    T, W = o_f.shape
    tb = min(ROW_TILE, T)
    nh = W // hd

    def body(dy_ref, of_ref, ob_ref, z_ref, w_ref, _, do_ref, dz_ref, dw_ref):
        i = pl.program_id(0)
        wv = w_ref[...]
        dw = jnp.zeros((1, hd), f32)
        for a in range(nh):
            sl = slice(a * hd, (a + 1) * hd)
            o = of_ref[:, sl] + ob_ref[:, sl]
            z = z_ref[:, sl]
            dyv = dy_ref[:, sl]
            r = lax.rsqrt(jnp.mean(o * o, axis=1, keepdims=True) + NORM_EPS)
            orr = o * r
            dn = dyv * _silu(z)
            dz_ref[:, sl] = (dyv * orr * wv * _silu_grad(z)).astype(dz_ref.dtype)
            dw = dw + jnp.sum(dn * orr, axis=0, keepdims=True)
            dnw = dn * wv
            do_ref[:, sl] = r * dnw - o * (r * r * r) * jnp.mean(dnw * o, axis=1, keepdims=True)

        @pl.when(i == 0)
        def _():
            dw_ref[...] = dw

        @pl.when(i > 0)
        def _():
            dw_ref[...] += dw

    blk = pl.BlockSpec((tb, W), lambda i: (i, 0))
    row = pl.BlockSpec((1, hd), lambda i: (0, 0))
    return pl.pallas_call(
        body,
        out_shape=(jax.ShapeDtypeStruct((T, W), f32), jax.ShapeDtypeStruct(dproj.shape, dproj.dtype), jax.ShapeDtypeStruct((1, hd), f32)),
        grid=(T // tb,),
        in_specs=[blk, blk, blk, pl.BlockSpec((tb, W), lambda i: (i, gate_blk)), row, _ANY_SPEC],
        out_specs=(blk, pl.BlockSpec((tb, W), lambda i: (i, gate_blk)), row),
        input_output_aliases={5: 1},
        compiler_params=_cparams(("arbitrary",)),
        name=name,
    )(dy, o_f, o_b, proj, w, dproj)


def _gla_prep_bwd(proj, w2f, w2b, b2f, b2b, dgk_f, dgk_b, dqkv, dproj):
    T = proj.shape[0]
    tb = min(ROW_TILE, T)
    W = GLA_HEADS * GLA_DK
    WV = GLA_HEADS * GLA_DV

    def body(s_ref, wf_ref, wb_ref, bf_ref, bb_ref, df_ref, db_ref, qf, qb, kf, kb, vf, vb, _,
             dp_ref, ds_ref, dwf_ref, dwb_ref, dbf_ref, dbb_ref):
        i = pl.program_id(0)
        dp_ref[:, :W] = (qf[...] + qb[...]).astype(dp_ref.dtype)
        dp_ref[:, W : 2 * W] = (kf[...] + kb[...]).astype(dp_ref.dtype)
        dp_ref[:, 2 * W :] = (vf[...] + vb[...]).astype(dp_ref.dtype)
        small = s_ref[...]
        dsm = jnp.zeros((tb, LANES), f32)
        parts = []
        for w_ref, b_ref, d_ref in ((wf_ref, bf_ref, df_ref), (wb_ref, bb_ref, db_ref)):
            pre = _mx(small, w_ref[...], "nn") + b_ref[...]
            dpre = d_ref[...] * (1.0 / GATE_NORMALIZER) * _sigmoid(-pre)
            dsm = dsm + _mx(dpre, w_ref[...], "nt")
            parts.append((_mx(small, dpre, "tn"), jnp.sum(dpre, axis=0, keepdims=True)))
        ds_ref[...] = dsm

        @pl.when(i == 0)
        def _():
            dwf_ref[...], dbf_ref[...] = parts[0]
            dwb_ref[...], dbb_ref[...] = parts[1]

        @pl.when(i > 0)
        def _():
            dwf_ref[...] += parts[0][0]
            dbf_ref[...] += parts[0][1]
            dwb_ref[...] += parts[1][0]
            dbb_ref[...] += parts[1][1]

    blk = pl.BlockSpec((tb, W), lambda i: (i, 0))
    vblk = pl.BlockSpec((tb, WV), lambda i: (i, 0))
    wsp = pl.BlockSpec((LANES, W), lambda i: (0, 0))
    row = pl.BlockSpec((1, W), lambda i: (0, 0))
    return pl.pallas_call(
        body,
        out_shape=(jax.ShapeDtypeStruct(dproj.shape, dproj.dtype), jax.ShapeDtypeStruct((T, LANES), f32), jax.ShapeDtypeStruct((LANES, W), f32),
                   jax.ShapeDtypeStruct((LANES, W), f32), jax.ShapeDtypeStruct((1, W), f32), jax.ShapeDtypeStruct((1, W), f32)),
        grid=(T // tb,),
        in_specs=[pl.BlockSpec((tb, LANES), lambda i: (i, SMALL_OFF // LANES)), wsp, wsp, row, row, blk, blk,
                  blk, blk, blk, blk, vblk, vblk, _ANY_SPEC],
        out_specs=(pl.BlockSpec((tb, 2 * W + WV), lambda i: (i, DP_GLA_QKV_BLK)), pl.BlockSpec((tb, LANES), lambda i: (i, 0)), wsp, wsp, row, row),
        input_output_aliases={13: 0},
        compiler_params=_cparams(("arbitrary",)),
        name="gla_prep_bwd",
    )(proj, w2f, w2b, b2f, b2b, dgk_f, dgk_b, *dqkv, dproj)


GATE_A_BLK, GATE_B_BLK, Z_BLK, G_B_BLK = 6144 // 1024, 7168 // 1024, 3072 // 1024, 8192 // 1024
DP_GATES_BLK, DP_GLA_QKV_BLK, DP_SMALL_BLK = 6144 // 2048, 4096 // 2048, SMALL_OFF // 512
_ANY_SPEC = pl.BlockSpec(memory_space=pl.ANY)


def _mid(og, obg, proj, x, target, wpg, wpl, wout, w_post, tb=256):
    T = x.shape[0]
    Dm = D_MODEL

    def body(og_ref, obg_ref, ga_ref, gb_ref, x_ref, t_ref, wpg_ref, wpl_ref, wo_ref, wp_ref,
             loss_ref, dwp_ref, dy_ref, m_ref, dout_ref, dya_ref, dyb_ref, dgates_ref, dog_ref, dobg_ref):
        i = pl.program_id(0)
        ya = _mx(og_ref[...], wpg_ref[...], "nn")
        yb = _mx(obg_ref[...], wpl_ref[...], "nn")
        sa, sb = _sigmoid(ga_ref[...]), _sigmoid(gb_ref[...])
        m = sa * ya + sb * yb
        m_ref[...] = m.astype(m_ref.dtype)
        out = _mx(m, wo_ref[...], "nn")
        r = lax.rsqrt(jnp.mean(out * out, axis=1, keepdims=True) + NORM_EPS)
        wp = wp_ref[...]
        e = x_ref[...] + out * r * wp - t_ref[...]
        dy = e * (1.0 / Dm)
        dy_ref[...] = dy
        lpart = jnp.sum(e * e, axis=0, keepdims=True)
        wpart = jnp.sum(dy * out * r, axis=0, keepdims=True)
        dyn = dy * wp
        dout = r * dyn - out * (r * r * r) * jnp.mean(dyn * out, axis=1, keepdims=True)
        dout_ref[...] = dout.astype(dout_ref.dtype)
        dm = _mx(dout, wo_ref[...], "nt")
        dya, dyb = dm * sa, dm * sb
        dya_ref[...] = dya.astype(dya_ref.dtype)
        dyb_ref[...] = dyb.astype(dyb_ref.dtype)
        dgates_ref[:, :Dm] = (dm * ya * sa * (1.0 - sa)).astype(dgates_ref.dtype)
        dgates_ref[:, Dm:] = (dm * yb * sb * (1.0 - sb)).astype(dgates_ref.dtype)
        dog_ref[...] = _mx(dya, wpg_ref[...], "nt")
        dobg_ref[...] = _mx(dyb, wpl_ref[...], "nt")

        @pl.when(i == 0)
        def _():
            loss_ref[...] = lpart
            dwp_ref[...] = wpart

        @pl.when(i > 0)
        def _():
            loss_ref[...] += lpart
            dwp_ref[...] += wpart

    blk = pl.BlockSpec((tb, Dm), lambda i: (i, 0))
    row = pl.BlockSpec((1, Dm), lambda i: (0, 0))
    wsp = pl.BlockSpec((Dm, Dm), lambda i: (0, 0))
    act = jax.ShapeDtypeStruct((T, Dm), MXU_DTYPE)
    big = jax.ShapeDtypeStruct((T, Dm), f32)
    vec = jax.ShapeDtypeStruct((1, Dm), f32)
    return pl.pallas_call(
        body,
        out_shape=(vec, vec, big, act, act, act, act, jax.ShapeDtypeStruct((T, N_CAT), MXU_DTYPE), big, big),
        grid=(T // tb,),
        in_specs=[blk, blk, pl.BlockSpec((tb, Dm), lambda i: (i, GATE_A_BLK)), pl.BlockSpec((tb, Dm), lambda i: (i, GATE_B_BLK)),
                  blk, blk, wsp, wsp, wsp, row],
        out_specs=(row, row) + (blk,) * 5 + (pl.BlockSpec((tb, 2 * Dm), lambda i: (i, DP_GATES_BLK)), blk, blk),
        compiler_params=_cparams(("arbitrary",)),
        name="mid",
    )(og, obg, proj, proj, x, target, wpg, wpl, wout, w_post)


def _heads(ref, lead, n, width, rows=slice(None)):
    return [ref[lead + (rows, slice(i * width, (i + 1) * width))] for i in range(n)]


class _GdnBatch:
    def __init__(self, q, k, v, gcol, grow, beta, nf, saved=None, sol=None):
        B, Cn = q.shape[0], q.shape[1]
        self.incl, self.strict, eye = _dir_masks(B, nf, Cn)
        A = None
        if saved is None:
            self.decay = jnp.where(self.incl, jnp.exp(jnp.where(self.incl, gcol - grow, 0.0)), 0.0)
        else:
            A, self.L, self.attn, self.decay = (saved[:, i * Cn : (i + 1) * Cn] for i in range(4))
        self.k, self.v = k, v
        gl = _last_row(gcol, nf)
        self.beta = jnp.broadcast_to(beta, q.shape)
        self.eg = jnp.broadcast_to(jnp.exp(gcol), q.shape)
        self.ekd = jnp.broadcast_to(jnp.exp(gl - gcol), q.shape)
        self.qs = q * (GDN_DIM**-0.5)
        self.kb = k * self.beta
        if saved is None:
            self.L = jnp.where(self.strict, _mx(self.kb, k, "nt") * self.decay, 0.0)
            self.attn = jnp.where(self.incl, _mx(self.qs, k, "nt") * self.decay, 0.0)
            A = _inv_unit_lower(self.L, jnp.where(eye, 1.0, 0.0).astype(f32))
        self.A2 = _split2(A)
        self.saved = jnp.concatenate([A, self.L, self.attn, self.decay], axis=1) if saved is None else saved
        self.kbg = self.kb * self.eg
        if sol is None:
            sol = _h3s(self.A2, _split2(jnp.concatenate([v * self.beta, self.kbg], axis=2)), "nn")
        self.sol = sol
        self.u = self.sol[:, :, :GDN_DIM]
        self.w = self.sol[:, :, GDN_DIM:]
        self.qd = self.qs * self.eg
        self.kd = k * self.ekd
        self.egl = jnp.exp(gl)


def _gdn_chain_inputs(qf_ref, qb_ref, gf_ref, gb_ref, cf_ref, cb_ref, b, hb, need_rows=True):
    qkv = [jnp.stack(_heads(qf_ref, (i,), hb, GDN_DIM) + _heads(qb_ref, (i,), hb, GDN_DIM)) for i in range(3)]
    cols, rows, betas = [], [], []
    for d, (g_ref, c_ref) in enumerate(((gf_ref, cf_ref), (gb_ref, cb_ref))):
        gates, gc = g_ref[...], c_ref[...]
        for hh in range(hb):
            h = b * hb + hh
            col = _lane_col(gc, h + 8 * d)
            cols.append(col)
            if need_rows:
                rows.append(jnp.broadcast_to(col, (CHUNK, LANES)).T[0:1, :])
            betas.append(_lane_col(gates, 16 + h + 8 * d))
    return qkv[0], qkv[1], qkv[2], jnp.stack(cols), (jnp.stack(rows) if need_rows else None), jnp.stack(betas)


GDN_CPS_FWD, GDN_CPS_BWD = 2, 1


def _chunk_views(refs, s_f, s_b):
    out = []
    for i, r in enumerate(refs):
        s = s_f if i % 2 == 0 else s_b
        rows = pl.ds(s * CHUNK, CHUNK)
        out.append(r.at[pl.ds(s, 1)] if len(r.shape) == 4 else (r.at[:, rows, :] if len(r.shape) == 3 else r.at[rows, :]))
    return out


def _gdn_specs(cps, hb):
    R, W = cps * CHUNK, hb * GDN_DIM
    qsp = lambda ci: pl.BlockSpec((3, R, W), lambda n, b: (0, ci(n), b))
    gsp = lambda ci: pl.BlockSpec((R, LANES), lambda n, b: (ci(n), 0))
    osp = lambda ci: pl.BlockSpec((R, W), lambda n, b: (ci(n), b))
    ssp = lambda ci: pl.BlockSpec((cps, hb, GDN_DIM, GDN_DIM), lambda n, b: (ci(n), b, 0, 0))
    asp = lambda ci: pl.BlockSpec((cps, hb, 4 * CHUNK, CHUNK), lambda n, b: (ci(n), b, 0, 0))
    usp = lambda ci: pl.BlockSpec((cps, hb, CHUNK, 2 * GDN_DIM), lambda n, b: (ci(n), b, 0, 0))
    return qsp, gsp, osp, ssp, asp, usp


def _gdn_fwd(qkvn, gates, gc):
    T = qkvn.shape[1]
    N = T // CHUNK
    nh, hb = GDN_HEADS, GDN_HB
    cps = min(GDN_CPS_FWD, N)
    NB = N // cps

    def body(*refs):
        n, b = pl.program_id(0), pl.program_id(1)
        s_scr = refs[-1]

        @pl.when(n == 0)
        def _():
            s_scr[0, pl.ds(b * hb, hb)] = jnp.zeros((hb, GDN_DIM, GDN_DIM), f32)
            s_scr[1, pl.ds(b * hb, hb)] = jnp.zeros((hb, GDN_DIM, GDN_DIM), f32)

        for s in range(cps):
            chunk(*_chunk_views(refs[:-1], s, cps - 1 - s), s_scr, b)

    def chunk(qf_ref, qb_ref, gf_ref, gb_ref, cf_ref, cb_ref, of_ref, ob_ref, sf_ref, sb_ref, af_ref, ab_ref, uf_ref, ub_ref, s_scr, b):
        hs = pl.ds(b * hb, hb)
        q, k, v, gcol, grow, beta = _gdn_chain_inputs(qf_ref, qb_ref, gf_ref, gb_ref, cf_ref, cb_ref, b, hb)
        ck = _GdnBatch(q, k, v, gcol, grow, beta, hb)
        S = jnp.concatenate([s_scr[0, hs], s_scr[1, hs]], axis=0)
        sf_ref[0], sb_ref[0] = S[:hb], S[hb:]
        af_ref[0], ab_ref[0] = ck.saved[:hb], ck.saved[hb:]
        uf_ref[0], ub_ref[0] = ck.sol[:hb], ck.sol[hb:]
        v_new = ck.u - _mx(ck.w, S, "nn")
        o = _mx(ck.qd, S, "nn") + _mx(ck.attn, v_new, "nn")
        S_new = S * ck.egl + _mx(ck.kd, v_new, "tn")
        for hh in range(hb):
            sl = slice(hh * GDN_DIM, (hh + 1) * GDN_DIM)
            of_ref[:, sl] = o[hh]
            ob_ref[:, sl] = o[hb + hh]
        s_scr[0, hs] = S_new[:hb]
        s_scr[1, hs] = S_new[hb:]

    fw, bw = (lambda n: n), (lambda n: NB - 1 - n)
    qsp, gsp, osp, ssp, asp, usp = _gdn_specs(cps, hb)
    o_shape = jax.ShapeDtypeStruct((T, nh * GDN_DIM), f32)
    s_shape = jax.ShapeDtypeStruct((N, nh, GDN_DIM, GDN_DIM), f32)
    a_shape = jax.ShapeDtypeStruct((N, nh, 4 * CHUNK, CHUNK), f32)
    u_shape = jax.ShapeDtypeStruct((N, nh, CHUNK, 2 * GDN_DIM), f32)
    return pl.pallas_call(
        body,
        out_shape=(o_shape, o_shape, s_shape, s_shape, a_shape, a_shape, u_shape, u_shape),
        grid=(NB, nh // hb),
        in_specs=[qsp(fw), qsp(bw), gsp(fw), gsp(bw), gsp(fw), gsp(bw)],
        out_specs=(osp(fw), osp(bw), ssp(fw), ssp(bw), asp(fw), asp(bw), usp(fw), usp(bw)),
        scratch_shapes=[pltpu.VMEM((2, nh, GDN_DIM, GDN_DIM), f32)],
        compiler_params=_cparams(("arbitrary", "arbitrary")),
        name="gdn_scan_fwd",
    )(qkvn, qkvn, gates, gates, gc, gc)


def _gdn_bwd(qkvn, gates, gc, st_f, st_b, a_f, a_b, u_f, u_b, do):
    T = qkvn.shape[1]
    N = T // CHUNK
    nh, hb = GDN_HEADS, GDN_HB
    nb = nh // hb
    cps = min(GDN_CPS_BWD, N)
    NB = N // cps

    def body(*refs):
        n, b = pl.program_id(0), pl.program_id(1)
        ds_scr = refs[-1]

        @pl.when(n == 0)
        def _():
            ds_scr[0, pl.ds(b * hb, hb)] = jnp.zeros((hb, GDN_DIM, GDN_DIM), f32)
            ds_scr[1, pl.ds(b * hb, hb)] = jnp.zeros((hb, GDN_DIM, GDN_DIM), f32)

        for s in range(cps):
            chunk(*_chunk_views(refs[:-1], cps - 1 - s, s), ds_scr, b)

    def chunk(qf_ref, qb_ref, gf_ref, gb_ref, cf_ref, cb_ref, sf_ref, sb_ref, af_ref, ab_ref, uf_ref, ub_ref, dof_ref, dob_ref,
              dqf_ref, dqb_ref, dgf_ref, dgb_ref, ds_scr, b):
        hs = pl.ds(b * hb, hb)

        @pl.when(b == 0)
        def _():
            dgf_ref[...] = jnp.zeros((CHUNK, LANES), f32)
            dgb_ref[...] = jnp.zeros((CHUNK, LANES), f32)

        q, k, v, gcol, grow, beta = _gdn_chain_inputs(qf_ref, qb_ref, gf_ref, gb_ref, cf_ref, cb_ref, b, hb, need_rows=False)
        ck = _GdnBatch(q, k, v, gcol, grow, beta, hb, saved=jnp.concatenate([af_ref[0], ab_ref[0]], axis=0),
                       sol=jnp.concatenate([uf_ref[0], ub_ref[0]], axis=0))
        S = jnp.concatenate([sf_ref[0], sb_ref[0]], axis=0)
        dS = jnp.concatenate([ds_scr[0, hs], ds_scr[1, hs]], axis=0)
        dov = jnp.stack(_heads(dof_ref, (), hb, GDN_DIM) + _heads(dob_ref, (), hb, GDN_DIM))
        v_new = ck.u - _mx(ck.w, S, "nn")
        d_vnew = _mx(ck.attn, dov, "tn") + _mx(ck.kd, dS, "nn")
        d_attn = jnp.where(ck.incl, _mx(dov, v_new, "nt"), 0.0)
        d_qd = _mx(dov, S, "nt")
        d_kd = _mx(v_new, dS, "nt")
        d_gl = jnp.sum(jnp.sum(dS * S, axis=2, keepdims=True), axis=1, keepdims=True) * ck.egl
        dS_new = dS * ck.egl + _mx(ck.qd, dov, "tn") - _mx(ck.w, d_vnew, "tn")
        ds_scr[0, hs] = dS_new[:hb]
        ds_scr[1, hs] = dS_new[hb:]
        d_w = -_mx(d_vnew, S, "nt")
        d_rhs = _h3s(ck.A2, _split2(jnp.concatenate([d_vnew, d_w], axis=2)), "tn")
        d_vb, d_kbg = d_rhs[:, :, :GDN_DIM], d_rhs[:, :, GDN_DIM:]
        dL = -jnp.where(ck.strict, _h3(d_rhs, ck.sol, "nt"), 0.0)
        P = dL * ck.decay
        Q = d_attn * ck.decay
        d_kb = _mx(P, k, "nn") + d_kbg * ck.eg
        d_qs = _mx(Q, k, "nn") + d_qd * ck.eg
        d_k = _mx(P, ck.kb, "tn") + _mx(Q, ck.qs, "tn") + d_kd * ck.ekd + d_kb * ck.beta
        E = dL * ck.L + d_attn * ck.attn
        kdsum = jnp.sum(d_kd * ck.kd, axis=2, keepdims=True)
        d_gc = (jnp.sum(E, axis=2, keepdims=True) - _colsum_col(E)
                + jnp.sum(d_kbg * ck.kbg, axis=2, keepdims=True)
                + jnp.sum(d_qd * ck.qd, axis=2, keepdims=True) - kdsum)
        d_gc = _add_at_last_row(d_gc, d_gl + jnp.sum(kdsum, axis=1, keepdims=True), hb)
        d_beta = jnp.sum(d_vb * v, axis=2, keepdims=True) + jnp.sum(d_kb * k, axis=2, keepdims=True)
        d_q = d_qs * (GDN_DIM**-0.5)
        d_v = d_vb * ck.beta
        lane = _iota2((CHUNK, LANES), 1)
        for d, (dq_ref, dg_ref) in enumerate(((dqf_ref, dgf_ref), (dqb_ref, dgb_ref))):
            acc = dg_ref[...]
            for hh in range(hb):
                c, h = d * hb + hh, b * hb + hh
                sl = slice(hh * GDN_DIM, (hh + 1) * GDN_DIM)
                dq_ref[0, :, sl] = d_q[c]
                dq_ref[1, :, sl] = d_k[c]
                dq_ref[2, :, sl] = d_v[c]
                acc = jnp.where(lane == h + 8 * d, d_gc[c], acc)
                acc = jnp.where(lane == 16 + h + 8 * d, d_beta[c], acc)
            dg_ref[...] = acc

        @pl.when(b == nb - 1)
        def _():
            for d, dg_ref in enumerate((dgf_ref, dgb_ref)):
                blk = dg_ref[...]
                dg_ref[...] = jnp.where(lane < 16, _exact01(_chunk_tri(CHUNK, d == 0), blk), blk)

    fw, bw = (lambda n: NB - 1 - n), (lambda n: n)
    qsp, gsp, osp, ssp, asp, usp = _gdn_specs(cps, hb)
    dq_shape = jax.ShapeDtypeStruct((3, T, nh * GDN_DIM), f32)
    dg_shape = jax.ShapeDtypeStruct((T, LANES), f32)
    return pl.pallas_call(
        body,
        out_shape=(dq_shape, dq_shape, dg_shape, dg_shape),
        grid=(NB, nb),
        in_specs=[qsp(fw), qsp(bw), gsp(fw), gsp(bw), gsp(fw), gsp(bw), ssp(fw), ssp(bw), asp(fw), asp(bw), usp(fw), usp(bw),
                  osp(fw), osp(bw)],
        out_specs=(qsp(fw), qsp(bw), gsp(fw), gsp(bw)),
        scratch_shapes=[pltpu.VMEM((2, nh, GDN_DIM, GDN_DIM), f32)],
        compiler_params=_cparams(("arbitrary", "arbitrary")),
        name="gdn_scan_bwd",
    )(qkvn, qkvn, gates, gates, gc, gc, st_f, st_b, a_f, a_b, u_f, u_b, do, do)


class _GlaBatch:
    def __init__(self, q, k, G, nf):
        B, Cn = q.shape[0], q.shape[1]
        self.incl, _, _ = _dir_masks(B, nf, Cn)
        self.eG = jnp.exp(G)
        self.enG = jnp.exp(-G)
        self.qg = q * (GLA_DK**-0.5) * self.eG
        self.kg = k * self.enG
        self.attn = jnp.where(self.incl, _mx(self.qg, self.kg, "nt"), 0.0)
        gl = _last_row(G, nf)
        self.ekd = jnp.exp(gl - G)
        self.kd = k * self.ekd
        self.egl = jnp.exp(gl)


GLA_CPS = 4


def _chunk_rows(s):
    return slice(s * CHUNK, (s + 1) * CHUNK)


def _gla_chain_inputs(refs_f, refs_b, rows_f, rows_b):
    nh = GLA_HEADS
    out = []
    for i, width in enumerate((GLA_DK, GLA_DK, GLA_DV, GLA_DK)):
        out.append(jnp.stack(_heads(refs_f[i], (), nh, width, rows_f) + _heads(refs_b[i], (), nh, width, rows_b)))
    return out


def _gla_specs(rows, ci):
    nh = GLA_HEADS
    return [
        pl.BlockSpec((rows, nh * GLA_DK), lambda n: (ci(n), 4096 // (nh * GLA_DK))),
        pl.BlockSpec((rows, nh * GLA_DK), lambda n: (ci(n), 4608 // (nh * GLA_DK))),
        pl.BlockSpec((rows, nh * GLA_DV), lambda n: (ci(n), 5120 // (nh * GLA_DV))),
        pl.BlockSpec((rows, nh * GLA_DK), lambda n: (ci(n), 0)),
    ]


def _gla_fwd(proj, G_f, G_b):
    T = proj.shape[0]
    N = T // CHUNK
    nh = GLA_HEADS
    cps = min(GLA_CPS, N)
    NB = N // cps

    def body(qf, kf, vf, gf, qb, kb, vb, gb, of_ref, ob_ref, sf_ref, sb_ref, s_scr):
        n = pl.program_id(0)

        @pl.when(n == 0)
        def _():
            s_scr[...] = jnp.zeros((2 * nh, GLA_DV, GLA_DK), f32)

        for s in range(cps):
            sb = cps - 1 - s
            q, k, v, G = _gla_chain_inputs((qf, kf, vf, gf), (qb, kb, vb, gb), _chunk_rows(s), _chunk_rows(sb))
            ck = _GlaBatch(q, k, G, nh)
            St = s_scr[...]
            sf_ref[s], sb_ref[sb] = St[:nh].astype(sf_ref.dtype), St[nh:].astype(sb_ref.dtype)
            o = _mx(ck.qg, St, "nt") + _mx(ck.attn, v, "nn")
            for i in range(nh):
                sl = slice(i * GLA_DV, (i + 1) * GLA_DV)
                of_ref[_chunk_rows(s), sl] = o[i]
                ob_ref[_chunk_rows(sb), sl] = o[nh + i]
            s_scr[...] = St * ck.egl + _mx(v, ck.kd, "tn")

    fw, bw = (lambda n: n), (lambda n: NB - 1 - n)
    osp = lambda ci: pl.BlockSpec((cps * CHUNK, nh * GLA_DV), lambda n: (ci(n), 0))
    ssp = lambda ci: pl.BlockSpec((cps, nh, GLA_DV, GLA_DK), lambda n: (ci(n), 0, 0, 0))
    o_shape = jax.ShapeDtypeStruct((T, nh * GLA_DV), f32)
    s_shape = jax.ShapeDtypeStruct((N, nh, GLA_DV, GLA_DK), MXU_DTYPE)
    return pl.pallas_call(
        body,
        out_shape=(o_shape, o_shape, s_shape, s_shape),
        grid=(NB,),
        in_specs=_gla_specs(cps * CHUNK, fw) + _gla_specs(cps * CHUNK, bw),
        out_specs=(osp(fw), osp(bw), ssp(fw), ssp(bw)),
        scratch_shapes=[pltpu.VMEM((2 * nh, GLA_DV, GLA_DK), f32)],
        compiler_params=_cparams(("arbitrary",)),
        name="gla_scan_fwd",
    )(proj, proj, proj, G_f, proj, proj, proj, G_b)


def _gla_bwd(proj, G_f, G_b, st_f, st_b, do):
    T = proj.shape[0]
    N = T // CHUNK
    nh = GLA_HEADS
    cps = min(GLA_CPS, N)
    NB = N // cps

    def body(qf, kf, vf, gf, sf_ref, dof_ref, qb, kb, vb, gb, sb_ref, dob_ref,
             dqf_ref, dkf_ref, dvf_ref, dgf_ref, dqb_ref, dkb_ref, dvb_ref, dgb_ref, ds_scr):
        n = pl.program_id(0)

        @pl.when(n == 0)
        def _():
            ds_scr[...] = jnp.zeros((2 * nh, GLA_DV, GLA_DK), f32)

        shp = (2 * nh, CHUNK, CHUNK)
        bb, ii, jj = _iota3(shp, 0), _iota3(shp, 1), _iota3(shp, 2)
        tri = jnp.where(jnp.logical_or(jnp.logical_and(bb < nh, jj >= ii), jnp.logical_and(bb >= nh, jj <= ii)), 1.0, 0.0)
        for s in range(cps):
            sf = cps - 1 - s
            rows = (_chunk_rows(sf), _chunk_rows(s))
            q, k, v, G = _gla_chain_inputs((qf, kf, vf, gf), (qb, kb, vb, gb), *rows)
            ck = _GlaBatch(q, k, G, nh)
            St = jnp.concatenate([sf_ref[sf], sb_ref[s]], axis=0).astype(f32)
            dSt = ds_scr[...]
            dov = jnp.stack(_heads(dof_ref, (), nh, GLA_DV, rows[0]) + _heads(dob_ref, (), nh, GLA_DV, rows[1]))
            d_attn = jnp.where(ck.incl, _mx(dov, v, "nt"), 0.0)
            d_qg = _mx(dov, St, "nn") + _mx(d_attn, ck.kg, "nn")
            d_kg = _mx(d_attn, ck.qg, "tn")
            d_v = _mx(ck.attn, dov, "tn") + _mx(ck.kd, dSt, "nt")
            d_kd = _mx(v, dSt, "nn")
            ds_scr[...] = dSt * ck.egl + _mx(dov, ck.qg, "tn")
            kdd = d_kd * ck.kd
            d_gl = jnp.sum(dSt * St, axis=1, keepdims=True) * ck.egl + jnp.sum(kdd, axis=1, keepdims=True)
            d_q = d_qg * ck.eG * (GLA_DK**-0.5)
            d_k = d_kg * ck.enG + d_kd * ck.ekd
            d_gk = _exact01(tri, _add_at_last_row(d_qg * ck.qg - d_kg * ck.kg - kdd, d_gl, nh))
            for d, (dq_ref, dk_ref, dv_ref, dg_ref) in enumerate(((dqf_ref, dkf_ref, dvf_ref, dgf_ref), (dqb_ref, dkb_ref, dvb_ref, dgb_ref))):
                for i in range(nh):
                    c = d * nh + i
                    ks, vs = slice(i * GLA_DK, (i + 1) * GLA_DK), slice(i * GLA_DV, (i + 1) * GLA_DV)
                    dq_ref[rows[d], ks] = d_q[c]
                    dk_ref[rows[d], ks] = d_k[c]
                    dv_ref[rows[d], vs] = d_v[c]
                    dg_ref[rows[d], ks] = d_gk[c]

    R = cps * CHUNK

    def in_specs(ci):
        return _gla_specs(R, ci) + [
            pl.BlockSpec((cps, nh, GLA_DV, GLA_DK), lambda n: (ci(n), 0, 0, 0)),
            pl.BlockSpec((R, nh * GLA_DV), lambda n: (ci(n), 0)),
        ]

    def out_specs(ci):
        ksp = pl.BlockSpec((R, nh * GLA_DK), lambda n: (ci(n), 0))
        return [ksp, ksp, pl.BlockSpec((R, nh * GLA_DV), lambda n: (ci(n), 0)), ksp]

    fw, bw = (lambda n: NB - 1 - n), (lambda n: n)
    k_shape = jax.ShapeDtypeStruct((T, nh * GLA_DK), f32)
    v_shape = jax.ShapeDtypeStruct((T, nh * GLA_DV), f32)
    return pl.pallas_call(
        body,
        out_shape=(k_shape, k_shape, v_shape, k_shape) * 2,
        grid=(NB,),
        in_specs=in_specs(fw) + in_specs(bw),
        out_specs=tuple(out_specs(fw) + out_specs(bw)),
        scratch_shapes=[pltpu.VMEM((2 * nh, GLA_DV, GLA_DK), f32)],
        compiler_params=_cparams(("arbitrary",)),
        name="gla_scan_bwd",
    )(proj, proj, proj, G_f, st_f, do, proj, proj, proj, G_b, st_b, do)


def _local_step(x, target, wcat, conv_w, avec, dvec, gdn_norm_w, w2f, w2b, b2f, b2b, gla_norm_w, proj_weights, w_pre, w_post,
                send_proj_grads=None, send_in_grads=None):
    h, h_t = _rms_pre(x, w_pre)
    proj = _matmul(h, wcat, "nt", f32, "proj_in", 2048, 512, 1024)
    qkvn = _gdn_prep(proj, conv_w)
    gates, gc, G_f, G_b = _gate_prep(proj, avec, dvec, w2f, w2b, b2f, b2b)
    oa_f, oa_b, sa_f, sa_b, inv_f, inv_b, sol_f, sol_b = _gdn_fwd(qkvn, gates, gc)
    og = _out_norm(oa_f, oa_b, proj, Z_BLK, gdn_norm_w, GDN_DIM, "gdn_out")
    ob_f, ob_b, sb_f, sb_b = _gla_fwd(proj, G_f, G_b)
    obg = _out_norm(ob_f, ob_b, proj, G_B_BLK, gla_norm_w, GLA_DV, "gla_out")
    wpg, wpl, wout = proj_weights(obg)
    (loss_row, d_wpost, dy, m, dout, dya, dyb, dproj, dog, dobg) = _mid(og, obg, proj, x, target, wpg, wpl, wout, w_post)

    d_wout = _matmul(m, dout, "tn", MXU_DTYPE, "dw_out", 1024, 512, 2048)
    d_wpg = _matmul(og, dya, "tn", MXU_DTYPE, "dw_proj_gdn", 1024, 512, 2048)
    d_wpl = _matmul(obg, dyb, "tn", MXU_DTYPE, "dw_proj_gla", 1024, 512, 2048)
    if send_proj_grads is not None:
        gla_norm_w = gla_norm_w + send_proj_grads(d_wpg, d_wpl, d_wout)

    do_b, dproj, d_gla_norm = _out_norm_bwd(dobg, ob_f, ob_b, proj, G_B_BLK, gla_norm_w, GLA_DV, "gla_out_bwd", dproj)
    dq_f, dk_f, dv_f, dgk_f, dq_b, dk_b, dv_b, dgk_b = _gla_bwd(proj, G_f, G_b, sb_f, sb_b, do_b)
    dproj, dsmall_gla, d_w2f, d_w2b, d_b2f, d_b2b = _gla_prep_bwd(proj, w2f, w2b, b2f, b2b, dgk_f, dgk_b,
                                                                 (dq_f, dq_b, dk_f, dk_b, dv_f, dv_b), dproj)

    do_a, dproj, d_gdn_norm = _out_norm_bwd(dog, oa_f, oa_b, proj, Z_BLK, gdn_norm_w, GDN_DIM, "gdn_out_bwd", dproj)
    cat_tile = 512
    qkv_tiles, small_tile = 3 * GDN_HEADS * GDN_DIM // cat_tile, SMALL_OFF // cat_tile
    dw_in = partial(_matmul, h_t, kind="nn", out_dtype=MXU_DTYPE, tm=1024, tn=cat_tile, tk=2048, out_t=True)
    d_wcat_mid = dw_in(dproj, name="dw_in_z_gla", b_cols=(qkv_tiles, small_tile - qkv_tiles))
    dqkv_f, dqkv_b, dg_f, dg_b = _gdn_bwd(qkvn, gates, gc, sa_f, sa_b, inv_f, inv_b, sol_f, sol_b, do_a)
    dproj, d_conv = _gdn_prep_bwd(proj, conv_w, dqkv_f, dqkv_b, dproj, d_wcat_mid)
    dproj, d_alog, d_dtb = _gdn_gates_bwd(proj, avec, dvec, dg_f, dg_b, dsmall_gla, dproj)

    d_wcat = (dw_in(dproj, name="dw_in_qkv", b_cols=(0, qkv_tiles)), d_wcat_mid,
              dw_in(dproj, name="dw_in_small", b_cols=(small_tile, N_CAT // cat_tile - small_tile)))
    started = None if send_in_grads is None else send_in_grads(d_wcat)
    dh = _matmul(dproj, wcat, "nn", f32, "dh", 512, 1024, N_CAT, after=started)
    grad_x, d_wpre = _rms_pre_bwd(x, w_pre, dh, dy)
    return dict(loss_row=loss_row, grad_x=grad_x, d_wcat=d_wcat, d_conv=d_conv, d_alog=d_alog, d_dtb=d_dtb,
                d_gdn_norm=d_gdn_norm, d_wpg=d_wpg, d_w2f=d_w2f, d_w2b=d_w2b, d_b2f=d_b2f, d_b2b=d_b2b,
                d_gla_norm=d_gla_norm, d_wpl=d_wpl, d_wout=d_wout, d_wpost=d_wpost, d_wpre=d_wpre)


def _to_cat(wf):
    pad = jnp.zeros((N_CAT - N_IN,) + wf.shape[1:], wf.dtype)
    return jnp.concatenate([wf[:4096], wf[4128:6176], wf[7232:9280], wf[6176:7200], wf[4096:4128], wf[7200:7232], pad], axis=0)


def _from_cat(pieces):
    qkv, mid, small = pieces
    return jnp.concatenate([qkv, mid[:1024], small[0:32], mid[1024:3072], mid[5120:6144], small[32:64], mid[3072:5120]], axis=0)


def _lane_row(*pieces):
    row = jnp.concatenate(pieces, axis=1)
    return jnp.pad(row, ((0, 0), (0, LANES - row.shape[1])))


def _pad_w2(w2, row0):
    return jnp.pad(w2, ((row0, LANES - row0 - w2.shape[0]), (0, 0)))


SM_ROWS, SM_COLS = 32, 384


def _pack_small_shard(conv, w2f, w2b):
    lead = conv.shape[:-2]
    z = lambda r, c: jnp.zeros(lead + (r, c), f32)
    top = jnp.concatenate([conv, z(8 - CONV_K, SM_COLS)], axis=-2)
    mid = jnp.concatenate([w2f, w2b, z(16, SM_COLS - 128)], axis=-1)
    return jnp.concatenate([top, mid, z(SM_ROWS - 24, SM_COLS)], axis=-2)


SMALL_REGION = {
    "ln_pre_w": (0, 1, 0, 1024), "ln_post_w": (1, 1, 0, 1024), "gk_b2_fwd": (2, 1, 0, 512), "gk_b2_bwd": (2, 1, 512, 512),
    "gla_norm_w": (3, 1, 0, 256), "gdn_norm_w": (3, 1, 256, 128), "a_log_fwd": (4, 1, 0, 8), "a_log_bwd": (5, 1, 0, 8),
    "dt_bias_fwd": (6, 1, 0, 8), "dt_bias_bwd": (7, 1, 0, 8), "conv_w": (8, CONV_K, 0, SM_COLS),
    "gk_w2_fwd": (16, 16, 0, 64), "gk_w2_bwd": (16, 16, 128, 64),
}
SMALL_NAMES = tuple(SMALL_REGION)
SMALL_SHARDED = ("conv_w", "gk_w2_fwd", "gk_w2_bwd")
PACK_ROWS, PACK_COLS = 32, 1024
LOSS_AT = (13, 0)


def _region(name):
    r0, nr, l0, nl = SMALL_REGION[name]
    return slice(r0, r0 + nr), slice(l0, l0 + nl)


def _pack_small_grads(G):
    names = ("d_wpre", "d_wpost", "d_b2f", "d_b2b", "d_gla_norm", "d_gdn_norm", "d_alog", "d_dtb", "loss_row", "d_conv", "d_w2f", "d_w2b")

    def body(wpre, wpost, b2f, b2b, glan, gdnn, alog, dtb, lrow, conv, w2f, w2b, o_ref):
        o_ref[...] = jnp.zeros(o_ref.shape, f32)
        loss = 0.5 * jnp.sum(lrow[...], axis=1, keepdims=True) * (1.0 / D_MODEL)
        whole = {"ln_pre_w": wpre, "ln_post_w": wpost, "gk_b2_fwd": b2f, "gk_b2_bwd": b2b, "gla_norm_w": glan, "gdn_norm_w": gdnn}
        for j in range(N_DEV):
            for name, ref in whole.items():
                o_ref[(j,) + _region(name)] = ref[...]
            o_ref[(j,) + _region("a_log_fwd")] = alog[0:1, 0:8]
            o_ref[(j,) + _region("a_log_bwd")] = alog[0:1, 8:16]
            o_ref[(j,) + _region("dt_bias_fwd")] = dtb[0:1, 0:8]
            o_ref[(j,) + _region("dt_bias_bwd")] = dtb[0:1, 8:16]
            o_ref[(j,) + _region("conv_w")] = conv[:, j * SM_COLS : (j + 1) * SM_COLS]
            o_ref[(j,) + _region("gk_w2_fwd")] = w2f[32:48, j * 64 : (j + 1) * 64]
            o_ref[(j,) + _region("gk_w2_bwd")] = w2b[48:64, j * 64 : (j + 1) * 64]
            o_ref[j, LOSS_AT[0] : LOSS_AT[0] + 1, LOSS_AT[1] : LOSS_AT[1] + 1] = loss

    return pl.pallas_call(
        body, out_shape=jax.ShapeDtypeStruct((N_DEV, PACK_ROWS, PACK_COLS), f32), name="pack_small_grads",
    )(*[G[n] for n in names])


def _small_adam(land, own, W, M, V):
    bc1 = 1.0 - ADAM_B1**ADAM_STEP
    bc2 = 1.0 - ADAM_B2**ADAM_STEP
    n = len(SMALL_NAMES)

    def body(land_ref, own_ref, *refs):
        w_refs, m_refs, v_refs = refs[0:n], refs[n : 2 * n], refs[2 * n : 3 * n]
        outs, loss_ref, g_scr = refs[3 * n : 7 * n], refs[7 * n], refs[7 * n + 1]
        me = _my_device()
        g = jnp.where(me == 0, own_ref[0], land_ref[0])
        for j in range(1, N_DEV):
            g = g + jnp.where(me == j, own_ref[j], land_ref[j])
        g_scr[...] = g
        loss_ref[...] = g_scr[LOSS_AT[0] : LOSS_AT[0] + 1, LOSS_AT[1] : LOSS_AT[1] + 1]
        for i, name in enumerate(SMALL_NAMES):
            gp = g_scr[_region(name)]
            lead = (0,) if name in SMALL_SHARDED else (Ellipsis,)
            w, m, v = w_refs[i][lead], m_refs[i][lead], v_refs[i][lead]
            m2 = ADAM_B1 * m + (1.0 - ADAM_B1) * gp
            v2 = ADAM_B2 * v + (1.0 - ADAM_B2) * (gp * gp)
            delta = -ADAM_LR * ((m2 / bc1) / (jnp.sqrt(v2 / bc2) + ADAM_EPS) + ADAM_WD * w)
            for k, val in enumerate((gp, delta, m2, v2)):
                outs[4 * i + k][lead] = val

    shapes = [jax.ShapeDtypeStruct(W[name].shape, f32) for name in SMALL_NAMES for _ in range(4)]
    res = pl.pallas_call(
        body,
        out_shape=tuple(shapes) + (jax.ShapeDtypeStruct((1, 1), f32),),
        scratch_shapes=[pltpu.VMEM((PACK_ROWS, PACK_COLS), f32)],
        name="adam_small",
    )(land, own, *[P[name] for P in (W, M, V) for name in SMALL_NAMES])
    return {name: res[4 * i : 4 * i + 4] for i, name in enumerate(SMALL_NAMES)}, res[4 * n]


_MESH = pl.DeviceIdType.MESH


N_BIG_COPIES, N_SMALL_COPIES = 9, 7


def _gather_weights(w_in_s, sm_s):
    shapes = (w_in_s.shape, sm_s.shape)
    dtypes = (MXU_DTYPE, f32)

    def body(win_ref, sm_ref, gin_ref, gsm_ref, own_ref, send_sems, recv_sems, own_sem):
        x, y, c = lax.axis_index("x"), lax.axis_index("y"), lax.axis_index("c")
        me, sibling = (x, y, c), (x, y, 1 - c)
        chips = [(1 - x, y), (x, 1 - y), (1 - x, 1 - y)]

        def idx(px, py, pc):
            return 4 * px + 2 * py + pc

        for r in range(w_in_s.shape[1] // LANES):
            cols = slice(r * LANES, (r + 1) * LANES)
            own_ref[:, cols] = win_ref[:, cols].astype(MXU_DTYPE)
        keep_own = pltpu.make_async_copy(own_ref, gin_ref.at[idx(*me)], own_sem.at[0])
        keep_own.start()
        gsm_ref[idx(*me)] = sm_ref[...]

        def copy(sem, blk, to, src=None):
            return pltpu.make_async_remote_copy(src_ref=blk if src is None else src, dst_ref=blk, send_sem=send_sems.at[sem],
                                                recv_sem=recv_sems.at[sem], device_id=to, device_id_type=_MESH)

        def small(k, block, to):
            return copy(N_BIG_COPIES + k, gsm_ref.at[idx(*block)], to)

        half = w_in_s.shape[1] // 2
        xn, yn, dg = (1 - x, y), (x, 1 - y), (1 - x, 1 - y)

        def big(sem, block, to, part=None, src=None):
            blk = gin_ref.at[idx(*block)]
            return copy(sem, blk if part is None else blk.at[:, pl.ds(part * half, half)], to, src)

        started = [big(0, me, sibling, src=own_ref), big(1, me, (*xn, c), src=own_ref), big(2, me, (*yn, c), src=own_ref),
                   small(0, me, sibling)]
        started += [small(1 + j, me, (*chip, c)) for j, chip in enumerate(chips)]
        for cp in started:
            cp.start()

        def then(arrived, forwards):
            arrived.wait_recv()
            for cp in forwards:
                cp.start()
            started.extend(forwards)

        then(big(1, (*xn, c), me), [big(3, (*xn, c), (*yn, c), 1), big(5, (*xn, c), sibling)])
        then(big(2, (*yn, c), me), [big(4, (*yn, c), (*xn, c), 0), big(6, (*yn, c), sibling)])
        then(big(4, (*dg, c), me, 0), [big(7, (*dg, c), sibling, 0)])
        then(big(3, (*dg, c), me, 1), [big(8, (*dg, c), sibling, 1)])
        for j, chip in enumerate(chips):
            then(small(1 + j, (*chip, c), me), [small(4 + j, (*chip, c), sibling)])
        big(0, sibling, me).wait_recv()
        big(5, (*xn, 1 - c), me).wait_recv()
        big(6, (*yn, 1 - c), me).wait_recv()
        big(7, (*dg, 1 - c), me, 0).wait_recv()
        big(8, (*dg, 1 - c), me, 1).wait_recv()
        small(0, sibling, me).wait_recv()
        for j, chip in enumerate(chips):
            small(4 + j, (*chip, 1 - c), me).wait_recv()
        for cp in started:
            cp.wait_send()
        keep_own.wait()

    vm = pl.BlockSpec(memory_space=pltpu.VMEM)
    n_copies = N_BIG_COPIES + N_SMALL_COPIES
    return pl.pallas_call(
        body,
        out_shape=tuple(jax.ShapeDtypeStruct((N_DEV,) + s, d) for s, d in zip(shapes, dtypes)),
        in_specs=[vm, vm],
        out_specs=(_HBM_SPEC, vm),
        scratch_shapes=[pltpu.VMEM(w_in_s.shape, MXU_DTYPE), pltpu.SemaphoreType.DMA((n_copies,)), pltpu.SemaphoreType.DMA((n_copies,)),
                        pltpu.SemaphoreType.DMA((1,))],
        compiler_params=pltpu.CompilerParams(vmem_limit_bytes=VMEM_LIMIT),
        name="gather_weights",
    )(w_in_s, sm_s)


N_CHIP = 4
_EFFECT = pltpu.SideEffectType.DATAFLOW_SIDE_EFFECTING
_HBM_SPEC = pl.BlockSpec(memory_space=pltpu.HBM)
_SEM_SPEC = pl.BlockSpec(memory_space=pltpu.SEMAPHORE)


def _relation_peer(k, x, y, c):
    px = (1 - x) if (k & 4) else x
    py = (1 - y) if (k & 2) else y
    pc = (1 - c) if (k & 1) else c
    return (px, py, pc), 4 * px + 2 * py + pc


_PLAN_COPIES = {"gather": N_DEV - 1, "blocks": N_DEV - 1, "sibling": 4, "chips": 3}
_PLAN_SLOTS = {"gather": N_DEV, "blocks": N_DEV, "sibling": 4, "chips": 4}


def _plan_copies(plan, src_ref, land_ref, send_sems, recv_sems, arrival):
    x, y, c = lax.axis_index("x"), lax.axis_index("y"), lax.axis_index("c")
    me, my_chip = 4 * x + 2 * y + c, 2 * x + y
    out = []

    def add(i, src, there, here, dev):
        out.append(pltpu.make_async_remote_copy(src_ref=src, dst_ref=land_ref.at[here if arrival else there], send_sem=send_sems.at[i],
                                                recv_sem=recv_sems.at[i], device_id=dev, device_id_type=_MESH))

    if plan in ("gather", "blocks"):
        for k in range(1, N_DEV):
            dev, p = _relation_peer(k, x, y, c)
            add(k - 1, src_ref.at[p] if plan == "blocks" else src_ref, me, p, dev)
    elif plan == "sibling":
        for j in range(4):
            add(j, src_ref.at[2 * j + 1 - c], j, j, (x, y, 1 - c))
    else:
        for i, k in enumerate((2, 4, 6)):
            (px, py, pc), _ = _relation_peer(k, x, y, c)
            add(i, src_ref.at[2 * px + py], my_chip, 2 * px + py, (px, py, pc))
    return out


def _exchange_start(src, plan, name, after=None):
    n = _PLAN_COPIES[plan]
    land = lax.empty((_PLAN_SLOTS[plan],) + (src.shape if plan == "gather" else src.shape[1:]), src.dtype)
    extra = [] if after is None else [after]

    def body(src_ref, land_ref, *rest):
        send_sems, recv_sems, _, _, token = rest[len(extra) :]
        for cp in _plan_copies(plan, src_ref, land_ref, send_sems, recv_sems, False):
            cp.start()
        token[...] = jnp.zeros_like(token)

    return pl.pallas_call(
        body,
        name=name,
        out_shape=(pltpu.SemaphoreType.DMA((n,)), pltpu.SemaphoreType.DMA((n,)), pltpu.HBM(src.shape, src.dtype),
                   pltpu.HBM(land.shape, land.dtype), jax.ShapeDtypeStruct((8, LANES), f32)),
        in_specs=(_HBM_SPEC, _HBM_SPEC) + (_ANY_SPEC,) * len(extra),
        out_specs=(_SEM_SPEC, _SEM_SPEC, _HBM_SPEC, _HBM_SPEC, pl.BlockSpec(memory_space=pltpu.VMEM)),
        input_output_aliases={0: 2, 1: 3},
        compiler_params=pltpu.CompilerParams(has_side_effects=_EFFECT),
    )(pltpu.with_memory_space_constraint(src, pltpu.HBM), pltpu.with_memory_space_constraint(land, pltpu.HBM), *extra)


def _exchange_wait(started, after, plan, name):
    send_sems, recv_sems, src_thru, land_thru, _ = started

    def body(src_ref, land_ref, send_sems, recv_sems, after_ref, src_dead, got_ref):
        for cp in _plan_copies(plan, src_ref, land_ref, send_sems, recv_sems, True):
            cp.wait_send()
            cp.wait_recv()

    return pl.pallas_call(
        body,
        name=name,
        out_shape=(pltpu.HBM(src_thru.shape, src_thru.dtype), pltpu.HBM(land_thru.shape, land_thru.dtype)),
        in_specs=(_HBM_SPEC, _HBM_SPEC, _SEM_SPEC, _SEM_SPEC, pl.BlockSpec(memory_space=pl.ANY)),
        out_specs=(_HBM_SPEC, _HBM_SPEC),
        input_output_aliases={0: 0, 1: 1},
        compiler_params=pltpu.CompilerParams(has_side_effects=_EFFECT),
    )(src_thru, land_thru, send_sems, recv_sems, after)


def _pair_sum(g, r, name, tc=LANES):
    _, R, Cc = g.shape

    def body(g_ref, r_ref, o_ref):
        o_ref[...] = (g_ref[...].astype(f32) + r_ref[...].astype(f32)).astype(o_ref.dtype)

    return pl.pallas_call(
        body,
        out_shape=jax.ShapeDtypeStruct(r.shape, r.dtype),
        grid=(N_CHIP, Cc // tc),
        in_specs=[pl.BlockSpec((1, R, tc), lambda k, i: (2 * k + lax.axis_index("c"), 0, i)), pl.BlockSpec((1, R, tc), lambda k, i: (k, 0, i))],
        out_specs=pl.BlockSpec((1, R, tc), lambda k, i: (k, 0, i)),
        compiler_params=_cparams(("parallel", "parallel")),
        name=name,
    )(g, r)


def _sum_adam(parts, w, m, v, name, tb, tc=None, own=None, own_slot=None):
    R, Cc = w.shape
    n_parts = parts.shape[0]
    tb = R if tc else min(tb, R)
    tc = tc or Cc
    assert R % tb == 0 and Cc % tc == 0
    bc1 = 1.0 - ADAM_B1**ADAM_STEP
    bc2 = 1.0 - ADAM_B2**ADAM_STEP
    extra = [] if own is None else [own]

    def body(p_ref, w_ref, m_ref, v_ref, *rest):
        g_ref, d_ref, nm_ref, nv_ref = rest[len(extra) :]

        def part(j):
            pj = p_ref[j].astype(f32)
            return pj if own is None else jnp.where(own_slot() == j, rest[0][0].astype(f32), pj)

        g = part(0)
        for j in range(1, n_parts):
            g = g + part(j)
        g_ref[...] = g
        m2 = ADAM_B1 * m_ref[...] + (1.0 - ADAM_B1) * g
        v2 = ADAM_B2 * v_ref[...] + (1.0 - ADAM_B2) * (g * g)
        nm_ref[...] = m2
        nv_ref[...] = v2
        d_ref[...] = -ADAM_LR * ((m2 / bc1) / (jnp.sqrt(v2 / bc2) + ADAM_EPS) + ADAM_WD * w_ref[...])

    blk = pl.BlockSpec((tb, tc), lambda i, j: (i, j))
    o = jax.ShapeDtypeStruct((R, Cc), f32)
    return pl.pallas_call(
        body,
        out_shape=(o, o, o, o),
        grid=(R // tb, Cc // tc),
        in_specs=[pl.BlockSpec((n_parts, tb, tc), lambda i, j: (0, i, j)), blk, blk, blk]
        + [pl.BlockSpec((1, tb, tc), lambda i, j: (own_slot(), i, j))] * len(extra),
        out_specs=(blk, blk, blk, blk),
        compiler_params=_cparams(("parallel", "parallel")),
        name=name,
    )(parts, w, m, v, *extra)


def _my_device():
    return 4 * lax.axis_index("x") + 2 * lax.axis_index("y") + lax.axis_index("c")


def _my_chip():
    return 2 * lax.axis_index("x") + lax.axis_index("y")


_WEIGHTS = ["ln_pre_w", "w_in", "conv_w", "a_log_fwd", "a_log_bwd", "dt_bias_fwd", "dt_bias_bwd", "gdn_norm_w", "w_proj_gdn",
            "gk_w2_fwd", "gk_b2_fwd", "gk_w2_bwd", "gk_b2_bwd", "gla_norm_w", "w_proj_gla", "w_out", "ln_post_w"]


def kernel(x, ln_pre_w, w_in, conv_w, a_log_fwd, a_log_bwd, dt_bias_fwd, dt_bias_bwd, gdn_norm_w, w_proj_gdn, gk_w2_fwd, gk_b2_fwd, gk_w2_bwd, gk_b2_bwd, gla_norm_w, w_proj_gla, w_out, ln_post_w, loss_target, m_ln_pre_w, m_w_in, m_conv_w, m_a_log_fwd, m_a_log_bwd, m_dt_bias_fwd, m_dt_bias_bwd, m_gdn_norm_w, m_w_proj_gdn, m_gk_w2_fwd, m_gk_b2_fwd, m_gk_w2_bwd, m_gk_b2_bwd, m_gla_norm_w, m_w_proj_gla, m_w_out, m_ln_post_w, v_ln_pre_w, v_w_in, v_conv_w, v_a_log_fwd, v_a_log_bwd, v_dt_bias_fwd, v_dt_bias_bwd, v_gdn_norm_w, v_w_proj_gdn, v_gk_w2_fwd, v_gk_b2_fwd, v_gk_w2_bwd, v_gk_b2_bwd, v_gla_norm_w, v_w_proj_gla, v_w_out, v_ln_post_w):
    args = locals()
    W = {n: args[n] for n in _WEIGHTS}
    M = {n: args["m_" + n] for n in _WEIGHTS}
    V = {n: args["v_" + n] for n in _WEIGHTS}

    wp_stack = lambda P: jnp.concatenate([P["w_proj_gdn"], P["w_proj_gla"], P["w_out"]], axis=0)
    sm_pack = lambda P: _pack_small_shard(P["conv_w"][0], P["gk_w2_fwd"][0], P["gk_w2_bwd"][0])
    me = 4 * lax.axis_index("x") + 2 * lax.axis_index("y") + lax.axis_index("c")
    w_in_t, m_in_t, v_in_t = (jnp.transpose(a[0]) for a in (w_in, m_w_in, v_w_in))
    g_in, g_sm = _gather_weights(w_in_t, sm_pack(W))
    wp_own = wp_stack(W).astype(MXU_DTYPE)
    wp_started = _exchange_start(wp_own, "gather", "gather_proj_start", after=g_sm)

    def proj_weights(after):
        own, land = _exchange_wait(wp_started, after, "gather", "gather_proj_wait")
        g_p = lax.dynamic_update_slice(land, own[None], (me, 0, 0, 0))
        return tuple(g_p[:, i].reshape(D_MODEL, D_MODEL) for i in range(3))

    rows_p = 3 * D_MODEL // N_DEV
    sent = {}

    def send_proj_grads(d_wpg, d_wpl, d_wout):
        p_p = jnp.stack([d_wpg, d_wpl, d_wout]).reshape(3, N_DEV, D_MODEL // N_DEV, D_MODEL).transpose(1, 0, 2, 3)
        sent["p_p"] = p_p.reshape(N_DEV, rows_p, D_MODEL)
        sent["started"] = _exchange_start(sent["p_p"], "blocks", "exchange_proj_start")
        return sent["started"][4][0:1, 0:1]

    def send_in_grads(d_wcat):
        p_in = _from_cat(d_wcat).reshape(N_DEV, SHARD_IN, D_MODEL)
        sib = _exchange_start(p_in, "sibling", "exchange_in_sibling_start")
        p_p, land_p = _exchange_wait(sent["started"], sib[4], "blocks", "exchange_proj_wait")
        sent["o_p"] = _sum_adam(land_p, wp_stack(W).reshape(rows_p, D_MODEL), wp_stack(M).reshape(rows_p, D_MODEL),
                                wp_stack(V).reshape(rows_p, D_MODEL), "adam_w_proj", 128, own=p_p, own_slot=_my_device)
        p_in, sib_land = _exchange_wait(sib, sent["o_p"][0], "sibling", "exchange_in_sibling_wait")
        sent["chips"] = _exchange_start(_pair_sum(p_in, sib_land, "pair_sum_w_in", tc=512), "chips", "exchange_in_chips_start")
        return sent["chips"][4]

    wcat = _to_cat(g_in.reshape(N_IN, D_MODEL))
    conv_full = g_sm[:, 0:CONV_K, :].transpose(1, 0, 2).reshape(CONV_K, N_DEV * SM_COLS)
    w2f_full = g_sm[:, 8:24, 0:64].transpose(1, 0, 2).reshape(16, 512)
    w2b_full = g_sm[:, 8:24, 64:128].transpose(1, 0, 2).reshape(16, 512)

    avec = _lane_row(a_log_fwd, a_log_bwd)
    dvec = _lane_row(dt_bias_fwd, dt_bias_bwd)
    G = _local_step(x[0], loss_target[0], wcat, conv_full, avec, dvec, gdn_norm_w, _pad_w2(w2f_full, 32), _pad_w2(w2b_full, 48),
                    gk_b2_fwd, gk_b2_bwd, gla_norm_w, proj_weights, ln_pre_w + wp_started[4][0:1, 0:1], ln_post_w, send_proj_grads,
                    send_in_grads)

    small_started = _exchange_start(_pack_small_grads(G), "blocks", "exchange_small_start")
    q_in, land_in = _exchange_wait(sent["chips"], small_started[4], "chips", "exchange_in_chips_wait")
    o_in = _sum_adam(land_in, w_in_t, m_in_t, v_in_t, "adam_w_in", SHARD_IN, tc=256, own=q_in, own_slot=_my_chip)
    o_p = sent["o_p"]
    p_small, land_small = _exchange_wait(small_started, o_in[0], "blocks", "exchange_small_wait")
    o_small, loss = _small_adam(land_small, p_small, W, M, V)

    res = []
    for kind in range(4):
        per = {name: o_small[name][kind] for name in SMALL_NAMES}
        per["w_in"] = jnp.transpose(o_in[kind])[None]
        pp = o_p[kind].reshape(3, 1, D_MODEL // N_DEV, D_MODEL)
        per["w_proj_gdn"], per["w_proj_gla"], per["w_out"] = pp[0], pp[1], pp[2]
        res.append([per[n] for n in _WEIGHTS])
    return (loss[0, 0], G["grad_x"][None], *res[0], *res[1], *res[2], *res[3])
```

```python
import jax
import jax.numpy as jnp
from jax import lax
from jax.experimental import pallas as pl
from jax.experimental.pallas import tpu as pltpu

f32 = jnp.float32
MXU_DTYPE = jnp.bfloat16

D_MODEL = 1024
CHUNK = 64
NORM_EPS = 1e-6
GDN_HEADS, GDN_DIM = 8, 128
GLA_HEADS, GLA_DK, GLA_DV = 4, 128, 256
GATE_NORMALIZER = 16.0
CONV_K = 5
N_IN = 9280
N_DEV = 8
SHARD_IN = N_IN // N_DEV

N_CAT = 9728
SMALL_OFF = 9216
LANES = 128

ADAM_LR, ADAM_B1, ADAM_B2, ADAM_EPS, ADAM_WD, ADAM_STEP = 0.001, 0.9, 0.999, 1e-08, 0.01, 10

VMEM_LIMIT = 56 * 1024 * 1024


def _cparams(sem=None):
    return pltpu.CompilerParams(dimension_semantics=sem, vmem_limit_bytes=VMEM_LIMIT)


_DN = {"nn": (((1,), (0,)), ((), ())), "nt": (((1,), (1,)), ((), ())), "tn": (((0,), (0,)), ((), ()))}
_BDN = {"nn": (((2,), (1,)), ((0,), (0,))), "nt": (((2,), (2,)), ((0,), (0,))), "tn": (((1,), (1,)), ((0,), (0,)))}


def _dot16(a, b, kind):
    return lax.dot_general(a, b, (_DN if a.ndim == 2 else _BDN)[kind], preferred_element_type=f32)


def _mx(a, b, kind):
    return _dot16(a.astype(MXU_DTYPE), b.astype(MXU_DTYPE), kind)


def _split2(a):
    hi = a.astype(jnp.bfloat16)
    return hi, (a - hi.astype(f32)).astype(jnp.bfloat16)


def _split3(a):
    p1 = a.astype(jnp.bfloat16)
    r = a - p1.astype(f32)
    p2 = r.astype(jnp.bfloat16)
    return p1, p2, (r - p2.astype(f32)).astype(jnp.bfloat16)


def _h3s(a2, b2, kind):
    (ah, al), (bh, bl) = a2, b2
    return _dot16(ah, bh, kind) + (_dot16(ah, bl, kind) + _dot16(al, bh, kind))


def _h3(a, b, kind):
    return _h3s(_split2(a), _split2(b), kind)


def _exact01(t01, x, kind="nn"):
    t = t01.astype(jnp.bfloat16)
    x1, x2, x3 = _split3(x)
    return _dot16(t, x1, kind) + (_dot16(t, x2, kind) + _dot16(t, x3, kind))


def _colsum_col(e):
    ones = jnp.ones(e.shape[:-1] + (LANES,), jnp.bfloat16)
    e1, e2, e3 = _split3(e)
    return (_dot16(e1, ones, "tn") + (_dot16(e2, ones, "tn") + _dot16(e3, ones, "tn")))[..., 0:1]


def _iota3(shape, axis):
    return lax.broadcasted_iota(jnp.int32, shape, axis)


def _dir_masks(B, nf, Cn):
    shp = (B, Cn, Cn)
    bb, ii, jj = _iota3(shp, 0), _iota3(shp, 1), _iota3(shp, 2)
    fwd = bb < nf
    rev = jnp.logical_not(fwd)
    incl = jnp.logical_or(jnp.logical_and(fwd, ii >= jj), jnp.logical_and(rev, ii <= jj))
    strict = jnp.logical_and(incl, ii != jj)
    return incl, strict, ii == jj


def _last_row(x, nf):
    B, Cn = x.shape[0], x.shape[1]
    fwd = _iota3((B, 1, 1), 0) < nf
    return jnp.where(fwd, x[:, Cn - 1 : Cn, :], x[:, 0:1, :])


def _add_at_last_row(x, val, nf):
    B, Cn = x.shape[0], x.shape[1]
    bb, rr = _iota3((B, Cn, 1), 0), _iota3((B, Cn, 1), 1)
    at = jnp.logical_or(jnp.logical_and(bb < nf, rr == Cn - 1), jnp.logical_and(bb >= nf, rr == 0))
    return x + jnp.where(at, val, 0.0)


def _sigmoid(x):
    return jax.nn.sigmoid(x)


def _silu(x):
    return x * _sigmoid(x)


def _silu_grad(x):
    s = _sigmoid(x)
    return s * (1.0 + x * (1.0 - s))


def _softplus(x):
    u = jnp.exp(-jnp.abs(x))
    l1p = jnp.where(u < 1e-3, u * (1.0 - u * (0.5 - u * (1.0 / 3.0))), jnp.log(1.0 + u))
    return jnp.maximum(x, 0.0) + l1p


def _mxr(x):
    return x.astype(MXU_DTYPE).astype(f32)


def _iota2(shape, axis):
    return lax.broadcasted_iota(jnp.int32, shape, axis)


def _matmul(a, b, kind, out_dtype, name, tm, tn, tk, out_t=False, after=None):
    extra = [] if after is None else [after]
    if kind == "nn":
        (M, K), N = a.shape, b.shape[1]
    elif kind == "nt":
        (M, K), N = a.shape, b.shape[0]
    else:
        (K, M), N = a.shape, b.shape[1]
    tm, tn, tk = min(tm, M), min(tn, N), min(tk, K)
    assert M % tm == 0 and N % tn == 0 and K % tk == 0, (name, M, N, K)
    nk = K // tk
    if kind == "tn":
        a_spec = pl.BlockSpec((tk, tm), lambda i, j, k: (k, i))
    else:
        a_spec = pl.BlockSpec((tm, tk), lambda i, j, k: (i, k))
    b_mode = dict(pipeline_mode=pl.Buffered(1)) if (tn == N and tk == K) else {}
    if kind == "nt":
        b_spec = pl.BlockSpec((tn, tk), lambda i, j, k: (j, k), **b_mode)
    else:
        b_spec = pl.BlockSpec((tk, tn), lambda i, j, k: (k, j), **b_mode)

    def body(a_ref, b_ref, *rest):
        o_ref, acc = rest[len(extra)], rest[len(extra) + 1 :]
        p = _mx(a_ref[...], b_ref[...], kind)
        if nk == 1:
            o_ref[...] = (p.T if out_t else p).astype(out_dtype)
        else:
            assert not out_t
            acc_ref = acc[0]
            k = pl.program_id(2)

            @pl.when(k == 0)
            def _():
                acc_ref[...] = p

            @pl.when(k > 0)
            def _():
                acc_ref[...] += p

            @pl.when(k == nk - 1)
            def _():
                o_ref[...] = acc_ref[...].astype(out_dtype)

    return pl.pallas_call(
        body,
        out_shape=jax.ShapeDtypeStruct((N, M) if out_t else (M, N), out_dtype),
        grid=(M // tm, N // tn, nk),
        in_specs=[a_spec, b_spec] + [pl.BlockSpec((8, LANES), lambda i, j, k: (0, 0))] * len(extra),
        out_specs=pl.BlockSpec((tn, tm), lambda i, j, k: (j, i)) if out_t else pl.BlockSpec((tm, tn), lambda i, j, k: (i, j)),
        scratch_shapes=[] if nk == 1 else [pltpu.VMEM((tm, tn), f32)],
        compiler_params=_cparams(("parallel", "parallel", "arbitrary")),
        name=name,
    )(a, b, *extra)


ROW_TILE = 512


def _rms_pre(x, w):
    T = x.shape[0]
    tb = min(ROW_TILE, T)

    def body(x_ref, w_ref, h_ref, ht_ref):
        xv = x_ref[...]
        r = lax.rsqrt(jnp.mean(xv * xv, axis=1, keepdims=True) + NORM_EPS)
        h = xv * r * w_ref[...]
        h_ref[...] = h.astype(h_ref.dtype)
        ht_ref[...] = h.T.astype(ht_ref.dtype)

    return pl.pallas_call(
        body,
        out_shape=(jax.ShapeDtypeStruct((T, D_MODEL), MXU_DTYPE), jax.ShapeDtypeStruct((D_MODEL, T), MXU_DTYPE)),
        grid=(T // tb,),
        in_specs=[pl.BlockSpec((tb, D_MODEL), lambda i: (i, 0)), pl.BlockSpec((1, D_MODEL), lambda i: (0, 0))],
        out_specs=(pl.BlockSpec((tb, D_MODEL), lambda i: (i, 0)), pl.BlockSpec((D_MODEL, tb), lambda i: (0, i))),
        compiler_params=_cparams(("parallel",)),
        name="rms_pre",
    )(x, w)


def _rms_pre_bwd(x, w, dh, dy):
    T = x.shape[0]
    tb = min(ROW_TILE, T)

    def body(x_ref, w_ref, dh_ref, dy_ref, dx_ref, dw_ref):
        i = pl.program_id(0)
        xv, dhv = x_ref[...], dh_ref[...]
        r = lax.rsqrt(jnp.mean(xv * xv, axis=1, keepdims=True) + NORM_EPS)
        dhw = dhv * w_ref[...]
        dx_ref[...] = dy_ref[...] + r * dhw - xv * (r * r * r) * jnp.mean(dhw * xv, axis=1, keepdims=True)
        part = jnp.sum(dhv * xv * r, axis=0, keepdims=True)

        @pl.when(i == 0)
        def _():
            dw_ref[...] = part

        @pl.when(i > 0)
        def _():
            dw_ref[...] += part

    blk = pl.BlockSpec((tb, D_MODEL), lambda i: (i, 0))
    row = pl.BlockSpec((1, D_MODEL), lambda i: (0, 0))
    return pl.pallas_call(
        body,
        out_shape=(jax.ShapeDtypeStruct((T, D_MODEL), f32), jax.ShapeDtypeStruct((1, D_MODEL), f32)),
        grid=(T // tb,),
        in_specs=[blk, row, blk, blk],
        out_specs=(blk, row),
        compiler_params=_cparams(("arbitrary",)),
        name="rms_pre_bwd",
    )(x, w, dh, dy)


SHIFTS = (-2, -1, 1, 2)


SHIFT_PAD = 8
PREP_HEADS, PREP_HEADS_BWD = 4, 1


def _shifted(x, pad_ref):
    T, cb = x.shape
    pad_ref[0:SHIFT_PAD] = jnp.zeros((SHIFT_PAD, cb), f32)
    pad_ref[SHIFT_PAD + T :] = jnp.zeros((SHIFT_PAD, cb), f32)
    pad_ref[SHIFT_PAD : SHIFT_PAD + T] = x
    return {d: pad_ref[SHIFT_PAD + d : SHIFT_PAD + d + T] for d in SHIFTS}


def _conv5(u, ush, cw):
    acc = u * cw[2:3, :]
    for j in (0, 1, 3, 4):
        acc = acc + ush[j - 2] * cw[j : j + 1, :]
    return acc


def _gdn_prep(proj, conv_w):
    T = proj.shape[0]
    hps, W = PREP_HEADS, PREP_HEADS * GDN_DIM
    per = GDN_HEADS // hps

    def body(u_ref, cw_ref, o_ref, pad_ref):
        j = pl.program_id(0)
        for i in range(hps):
            sl = slice(i * GDN_DIM, (i + 1) * GDN_DIM)
            u = _mxr(u_ref[:, sl])
            s = _silu(_conv5(u, _shifted(u, pad_ref), _mxr(cw_ref[:, sl])))
            o_ref[0, :, sl] = jnp.where(j < 2 * per, s * lax.rsqrt(jnp.sum(s * s, axis=1, keepdims=True) + NORM_EPS), s)

    return pl.pallas_call(
        body,
        out_shape=jax.ShapeDtypeStruct((3, T, GDN_HEADS * GDN_DIM), f32),
        grid=(3 * per,),
        in_specs=[pl.BlockSpec((T, W), lambda j: (0, j)), pl.BlockSpec((CONV_K, W), lambda j: (0, j))],
        out_specs=pl.BlockSpec((1, T, W), lambda j: (j // per, 0, j % per)),
        scratch_shapes=[pltpu.VMEM((T + 2 * SHIFT_PAD, GDN_DIM), f32)],
        compiler_params=_cparams(("parallel",)),
        name="gdn_prep",
    )(proj, conv_w)


def _gdn_prep_bwd(proj, conv_w, dqkv_f, dqkv_b, dproj):
    T = proj.shape[0]
    hps, W = PREP_HEADS_BWD, PREP_HEADS_BWD * GDN_DIM
    per = GDN_HEADS // hps

    def body(u_ref, cw_ref, df_ref, db_ref, _, du_ref, dcw_ref, upad_ref, dcpad_ref):
        j = pl.program_id(0)
        for i in range(hps):
            sl = slice(i * GDN_DIM, (i + 1) * GDN_DIM)
            u, cw = _mxr(u_ref[:, sl]), _mxr(cw_ref[:, sl])
            ush = _shifted(u, upad_ref)
            c = _conv5(u, ush, cw)
            s = _silu(c)
            dn = df_ref[0, :, sl] + db_ref[0, :, sl]
            rinv = lax.rsqrt(jnp.sum(s * s, axis=1, keepdims=True) + NORM_EPS)
            ds_norm = rinv * dn - s * (rinv * rinv * rinv) * jnp.sum(dn * s, axis=1, keepdims=True)
            ds = jnp.where(j < 2 * per, ds_norm, dn)
            dc = _mxr(ds * _silu_grad(c))
            dcsh = _shifted(dc, dcpad_ref)
            du = dc * cw[2:3, :]
            for jj in range(CONV_K):
                d = jj - 2
                dcw_ref[jj : jj + 1, sl] = jnp.sum(dc * (u if d == 0 else ush[d]), axis=0, keepdims=True)
                if d != 0:
                    du = du + dcsh[-d] * cw[jj : jj + 1, :]
            du_ref[:, sl] = du.astype(du_ref.dtype)

    qspec = pl.BlockSpec((1, T, W), lambda j: (j // per, 0, j % per))
    return pl.pallas_call(
        body,
        out_shape=(jax.ShapeDtypeStruct(dproj.shape, dproj.dtype), jax.ShapeDtypeStruct((CONV_K, 3 * GDN_HEADS * GDN_DIM), f32)),
        grid=(3 * per,),
        in_specs=[pl.BlockSpec((T, W), lambda j: (0, j)), pl.BlockSpec((CONV_K, W), lambda j: (0, j)), qspec, qspec, _ANY_SPEC],
        out_specs=(pl.BlockSpec((T, W), lambda j: (0, j)), pl.BlockSpec((CONV_K, W), lambda j: (0, j))),
        input_output_aliases={4: 0},
        scratch_shapes=[pltpu.VMEM((T + 2 * SHIFT_PAD, GDN_DIM), f32)] * 2,
        compiler_params=_cparams(("parallel",)),
        name="gdn_prep_bwd",
    )(proj, conv_w, dqkv_f, dqkv_b, dproj)


def _chunk_tri(n, rev):
    i, j = _iota2((n, n), 0), _iota2((n, n), 1)
    same = jnp.right_shift(i, 6) == jnp.right_shift(j, 6)
    order = (j >= i) if rev else (j <= i)
    return jnp.where(jnp.logical_and(same, order), 1.0, 0.0).astype(f32)


def _gate_prep(proj, avec, dvec, w2f, w2b, b2f, b2b, tb=256):
    T = proj.shape[0]
    W = GLA_HEADS * GLA_DK

    def body(s_ref, a_ref, d_ref, wf_ref, wb_ref, bf_ref, bb_ref, g_ref, gc_ref, gf_ref, gb_ref):
        small = s_ref[...]
        tri_f, tri_r = _chunk_tri(tb, False), _chunk_tri(tb, True)
        lane = _iota2(small.shape, 1)
        lg = -jnp.exp(a_ref[...]) * _softplus(small + d_ref[...])
        beta = _sigmoid(small)
        g_ref[...] = jnp.where(lane < 16, lg, jnp.where(lane < 32, beta, 0.0))
        lgm = jnp.where(lane < 16, lg, 0.0)
        gc_ref[...] = jnp.where(lane < 8, _exact01(tri_f, lgm), _exact01(tri_r, lgm))
        gkf = -_softplus(-(_mx(small, wf_ref[...], "nn") + bf_ref[...])) * (1.0 / GATE_NORMALIZER)
        gkb = -_softplus(-(_mx(small, wb_ref[...], "nn") + bb_ref[...])) * (1.0 / GATE_NORMALIZER)
        gf_ref[...] = _exact01(tri_f, gkf)
        gb_ref[...] = _exact01(tri_r, gkb)

    blk = pl.BlockSpec((tb, LANES), lambda i: (i, 0))
    row = pl.BlockSpec((1, LANES), lambda i: (0, 0))
    wide = pl.BlockSpec((tb, W), lambda i: (i, 0))
    wsp = pl.BlockSpec((LANES, W), lambda i: (0, 0))
    wrow = pl.BlockSpec((1, W), lambda i: (0, 0))
    return pl.pallas_call(
        body,
        out_shape=(jax.ShapeDtypeStruct((T, LANES), f32), jax.ShapeDtypeStruct((T, LANES), f32),
                   jax.ShapeDtypeStruct((T, W), f32), jax.ShapeDtypeStruct((T, W), f32)),
        grid=(T // tb,),
        in_specs=[pl.BlockSpec((tb, LANES), lambda i: (i, SMALL_OFF // LANES)), row, row, wsp, wsp, wrow, wrow],
        out_specs=(blk, blk, wide, wide),
        compiler_params=_cparams(("parallel",)),
        name="gate_prep",
    )(proj, avec, dvec, w2f, w2b, b2f, b2b)


def _gdn_gates_bwd(proj, avec, dvec, dg_f, dg_b, dsmall_gla, dproj):
    T = proj.shape[0]
    tb = min(ROW_TILE, T)
    pad = N_CAT - SMALL_OFF

    def body(s_ref, a_ref, d_ref, gf_ref, gb_ref, dl_ref, _, ds_ref, da_ref, dd_ref):
        i = pl.program_id(0)
        small = s_ref[...]
        lane = _iota2(small.shape, 1)
        dgate = gf_ref[...] + gb_ref[...]
        z = small + d_ref[...]
        nega = -jnp.exp(a_ref[...])
        dz = dgate * nega * _sigmoid(z)
        beta = _sigmoid(small)
        dsm = jnp.where(lane < 16, dz, jnp.where(lane < 32, dgate * beta * (1.0 - beta), 0.0))
        ds_ref[:, :LANES] = (dsm + dl_ref[...]).astype(ds_ref.dtype)
        ds_ref[:, LANES:] = jnp.zeros((tb, pad - LANES), ds_ref.dtype)
        lg = nega * _softplus(z)
        pa = jnp.sum(jnp.where(lane < 16, dgate * lg, 0.0), axis=0, keepdims=True)
        pd = jnp.sum(jnp.where(lane < 16, dz, 0.0), axis=0, keepdims=True)

        @pl.when(i == 0)
        def _():
            da_ref[...] = pa
            dd_ref[...] = pd

        @pl.when(i > 0)
        def _():
            da_ref[...] += pa
            dd_ref[...] += pd

    blk = pl.BlockSpec((tb, LANES), lambda i: (i, 0))
    row = pl.BlockSpec((1, LANES), lambda i: (0, 0))
    return pl.pallas_call(
        body,
        out_shape=(jax.ShapeDtypeStruct(dproj.shape, dproj.dtype), jax.ShapeDtypeStruct((1, LANES), f32), jax.ShapeDtypeStruct((1, LANES), f32)),
        grid=(T // tb,),
        in_specs=[pl.BlockSpec((tb, LANES), lambda i: (i, SMALL_OFF // LANES)), row, row, blk, blk, blk, _ANY_SPEC],
        out_specs=(pl.BlockSpec((tb, pad), lambda i: (i, DP_SMALL_BLK)), row, row),
        input_output_aliases={6: 0},
        compiler_params=_cparams(("arbitrary",)),
        name="gdn_gates_bwd",
    )(proj, avec, dvec, dg_f, dg_b, dsmall_gla, dproj)


GDN_HB = 8


def _inv_unit_lower(L, eye):
    A = eye - L
    P2 = _split2(L)
    for _ in range(3):
        P2 = _split2(_h3s(P2, P2, "nn"))
        A = A + _h3s(_split2(A), P2, "nn")
    Ph = P2[0]
    for _ in range(2):
        Ph = _dot16(Ph, Ph, "nn").astype(jnp.bfloat16)
        A = A + _dot16(A.astype(jnp.bfloat16), Ph, "nn")
    return A


def _lane_col(blk, idx):
    lane = _iota2(blk.shape, 1)
    return jnp.sum(jnp.where(lane == idx, blk, 0.0), axis=1, keepdims=True)


def _out_norm(o_f, o_b, proj, gate_blk, w, hd, name):
    T, W = o_f.shape
    tb = min(ROW_TILE, T)
    nh = W // hd

    def body(of_ref, ob_ref, z_ref, w_ref, y_ref):
        wv = w_ref[...]
        for i in range(nh):
            sl = slice(i * hd, (i + 1) * hd)
            o = of_ref[:, sl] + ob_ref[:, sl]
            r = lax.rsqrt(jnp.mean(o * o, axis=1, keepdims=True) + NORM_EPS)
            y_ref[:, sl] = (o * r * wv * _silu(z_ref[:, sl])).astype(y_ref.dtype)

    blk = pl.BlockSpec((tb, W), lambda i: (i, 0))
    return pl.pallas_call(
        body,
        out_shape=jax.ShapeDtypeStruct((T, W), MXU_DTYPE),
        grid=(T // tb,),
        in_specs=[blk, blk, pl.BlockSpec((tb, W), lambda i: (i, gate_blk)), pl.BlockSpec((1, hd), lambda i: (0, 0))],
        out_specs=blk,
        compiler_params=_cparams(("parallel",)),
        name=name,
    )(o_f, o_b, proj, w)


def _out_norm_bwd(dy, o_f, o_b, proj, gate_blk, w, hd, name, dproj):
    T, W = o_f.shape
    tb = min(ROW_TILE, T)
    nh = W // hd

    def body(dy_ref, of_ref, ob_ref, z_ref, w_ref, _, do_ref, dz_ref, dw_ref):
        i = pl.program_id(0)
        wv = w_ref[...]
        dw = jnp.zeros((1, hd), f32)
        for a in range(nh):
            sl = slice(a * hd, (a + 1) * hd)
            o = of_ref[:, sl] + ob_ref[:, sl]
            z = z_ref[:, sl]
            dyv = dy_ref[:, sl]
            r = lax.rsqrt(jnp.mean(o * o, axis=1, keepdims=True) + NORM_EPS)
            orr = o * r
            dn = dyv * _silu(z)
            dz_ref[:, sl] = (dyv * orr * wv * _silu_grad(z)).astype(dz_ref.dtype)
            dw = dw + jnp.sum(dn * orr, axis=0, keepdims=True)
            dnw = dn * wv
            do_ref[:, sl] = r * dnw - o * (r * r * r) * jnp.mean(dnw * o, axis=1, keepdims=True)

        @pl.when(i == 0)
        def _():
            dw_ref[...] = dw

        @pl.when(i > 0)
        def _():
            dw_ref[...] += dw

    blk = pl.BlockSpec((tb, W), lambda i: (i, 0))
    row = pl.BlockSpec((1, hd), lambda i: (0, 0))
    return pl.pallas_call(
        body,
        out_shape=(jax.ShapeDtypeStruct((T, W), f32), jax.ShapeDtypeStruct(dproj.shape, dproj.dtype), jax.ShapeDtypeStruct((1, hd), f32)),
        grid=(T // tb,),
        in_specs=[blk, blk, blk, pl.BlockSpec((tb, W), lambda i: (i, gate_blk)), row, _ANY_SPEC],
        out_specs=(blk, pl.BlockSpec((tb, W), lambda i: (i, gate_blk)), row),
        input_output_aliases={5: 1},
        compiler_params=_cparams(("arbitrary",)),
        name=name,
    )(dy, o_f, o_b, proj, w, dproj)


def _gla_prep_bwd(proj, w2f, w2b, b2f, b2b, dgk_f, dgk_b, dqkv, dproj):
    T = proj.shape[0]
    tb = min(ROW_TILE, T)
    W = GLA_HEADS * GLA_DK
    WV = GLA_HEADS * GLA_DV

    def body(s_ref, wf_ref, wb_ref, bf_ref, bb_ref, df_ref, db_ref, qf, qb, kf, kb, vf, vb, _,
             dp_ref, ds_ref, dwf_ref, dwb_ref, dbf_ref, dbb_ref):
        i = pl.program_id(0)
        dp_ref[:, :W] = (qf[...] + qb[...]).astype(dp_ref.dtype)
        dp_ref[:, W : 2 * W] = (kf[...] + kb[...]).astype(dp_ref.dtype)
        dp_ref[:, 2 * W :] = (vf[...] + vb[...]).astype(dp_ref.dtype)
        small = s_ref[...]
        dsm = jnp.zeros((tb, LANES), f32)
        parts = []
        for w_ref, b_ref, d_ref in ((wf_ref, bf_ref, df_ref), (wb_ref, bb_ref, db_ref)):
            pre = _mx(small, w_ref[...], "nn") + b_ref[...]
            dpre = d_ref[...] * (1.0 / GATE_NORMALIZER) * _sigmoid(-pre)
            dsm = dsm + _mx(dpre, w_ref[...], "nt")
            parts.append((_mx(small, dpre, "tn"), jnp.sum(dpre, axis=0, keepdims=True)))
        ds_ref[...] = dsm

        @pl.when(i == 0)
        def _():
            dwf_ref[...], dbf_ref[...] = parts[0]
            dwb_ref[...], dbb_ref[...] = parts[1]

        @pl.when(i > 0)
        def _():
            dwf_ref[...] += parts[0][0]
            dbf_ref[...] += parts[0][1]
            dwb_ref[...] += parts[1][0]
            dbb_ref[...] += parts[1][1]

    blk = pl.BlockSpec((tb, W), lambda i: (i, 0))
    vblk = pl.BlockSpec((tb, WV), lambda i: (i, 0))
    wsp = pl.BlockSpec((LANES, W), lambda i: (0, 0))
    row = pl.BlockSpec((1, W), lambda i: (0, 0))
    return pl.pallas_call(
        body,
        out_shape=(jax.ShapeDtypeStruct(dproj.shape, dproj.dtype), jax.ShapeDtypeStruct((T, LANES), f32), jax.ShapeDtypeStruct((LANES, W), f32),
                   jax.ShapeDtypeStruct((LANES, W), f32), jax.ShapeDtypeStruct((1, W), f32), jax.ShapeDtypeStruct((1, W), f32)),
        grid=(T // tb,),
        in_specs=[pl.BlockSpec((tb, LANES), lambda i: (i, SMALL_OFF // LANES)), wsp, wsp, row, row, blk, blk,
                  blk, blk, blk, blk, vblk, vblk, _ANY_SPEC],
        out_specs=(pl.BlockSpec((tb, 2 * W + WV), lambda i: (i, DP_GLA_QKV_BLK)), pl.BlockSpec((tb, LANES), lambda i: (i, 0)), wsp, wsp, row, row),
        input_output_aliases={13: 0},
        compiler_params=_cparams(("arbitrary",)),
        name="gla_prep_bwd",
    )(proj, w2f, w2b, b2f, b2b, dgk_f, dgk_b, *dqkv, dproj)


GATE_A_BLK, GATE_B_BLK, Z_BLK, G_B_BLK = 6144 // 1024, 7168 // 1024, 3072 // 1024, 8192 // 1024
DP_GATES_BLK, DP_GLA_QKV_BLK, DP_SMALL_BLK = 6144 // 2048, 4096 // 2048, SMALL_OFF // 512
_ANY_SPEC = pl.BlockSpec(memory_space=pl.ANY)


def _mid(og, obg, proj, x, target, wpg, wpl, wout, w_post, tb=256):
    T = x.shape[0]
    Dm = D_MODEL

    def body(og_ref, obg_ref, ga_ref, gb_ref, x_ref, t_ref, wpg_ref, wpl_ref, wo_ref, wp_ref,
             loss_ref, dwp_ref, dy_ref, m_ref, dout_ref, dya_ref, dyb_ref, dgates_ref, dog_ref, dobg_ref):
        i = pl.program_id(0)
        ya = _mx(og_ref[...], wpg_ref[...], "nn")
        yb = _mx(obg_ref[...], wpl_ref[...], "nn")
        sa, sb = _sigmoid(ga_ref[...]), _sigmoid(gb_ref[...])
        m = sa * ya + sb * yb
        m_ref[...] = m.astype(m_ref.dtype)
        out = _mx(m, wo_ref[...], "nn")
        r = lax.rsqrt(jnp.mean(out * out, axis=1, keepdims=True) + NORM_EPS)
        wp = wp_ref[...]
        e = x_ref[...] + out * r * wp - t_ref[...]
        dy = e * (1.0 / Dm)
        dy_ref[...] = dy
        lpart = jnp.sum(e * e, axis=0, keepdims=True)
        wpart = jnp.sum(dy * out * r, axis=0, keepdims=True)
        dyn = dy * wp
        dout = r * dyn - out * (r * r * r) * jnp.mean(dyn * out, axis=1, keepdims=True)
        dout_ref[...] = dout.astype(dout_ref.dtype)
        dm = _mx(dout, wo_ref[...], "nt")
        dya, dyb = dm * sa, dm * sb
        dya_ref[...] = dya.astype(dya_ref.dtype)
        dyb_ref[...] = dyb.astype(dyb_ref.dtype)
        dgates_ref[:, :Dm] = (dm * ya * sa * (1.0 - sa)).astype(dgates_ref.dtype)
        dgates_ref[:, Dm:] = (dm * yb * sb * (1.0 - sb)).astype(dgates_ref.dtype)
        dog_ref[...] = _mx(dya, wpg_ref[...], "nt")
        dobg_ref[...] = _mx(dyb, wpl_ref[...], "nt")

        @pl.when(i == 0)
        def _():
            loss_ref[...] = lpart
            dwp_ref[...] = wpart

        @pl.when(i > 0)
        def _():
            loss_ref[...] += lpart
            dwp_ref[...] += wpart

    blk = pl.BlockSpec((tb, Dm), lambda i: (i, 0))
    row = pl.BlockSpec((1, Dm), lambda i: (0, 0))
    wsp = pl.BlockSpec((Dm, Dm), lambda i: (0, 0))
    act = jax.ShapeDtypeStruct((T, Dm), MXU_DTYPE)
    big = jax.ShapeDtypeStruct((T, Dm), f32)
    vec = jax.ShapeDtypeStruct((1, Dm), f32)
    return pl.pallas_call(
        body,
        out_shape=(vec, vec, big, act, act, act, act, jax.ShapeDtypeStruct((T, N_CAT), MXU_DTYPE), big, big),
        grid=(T // tb,),
        in_specs=[blk, blk, pl.BlockSpec((tb, Dm), lambda i: (i, GATE_A_BLK)), pl.BlockSpec((tb, Dm), lambda i: (i, GATE_B_BLK)),
                  blk, blk, wsp, wsp, wsp, row],
        out_specs=(row, row) + (blk,) * 5 + (pl.BlockSpec((tb, 2 * Dm), lambda i: (i, DP_GATES_BLK)), blk, blk),
        compiler_params=_cparams(("arbitrary",)),
        name="mid",
    )(og, obg, proj, proj, x, target, wpg, wpl, wout, w_post)


def _heads(ref, lead, n, width, rows=slice(None)):
    return [ref[lead + (rows, slice(i * width, (i + 1) * width))] for i in range(n)]


class _GdnBatch:
    def __init__(self, q, k, v, gcol, grow, beta, nf, saved=None, sol=None):
        B, Cn = q.shape[0], q.shape[1]
        self.incl, self.strict, eye = _dir_masks(B, nf, Cn)
        A = None
        if saved is None:
            self.decay = jnp.where(self.incl, jnp.exp(jnp.where(self.incl, gcol - grow, 0.0)), 0.0)
        else:
            A, self.L, self.attn, self.decay = (saved[:, i * Cn : (i + 1) * Cn] for i in range(4))
        self.k, self.v = k, v
        gl = _last_row(gcol, nf)
        self.beta = jnp.broadcast_to(beta, q.shape)
        self.eg = jnp.broadcast_to(jnp.exp(gcol), q.shape)
        self.ekd = jnp.broadcast_to(jnp.exp(gl - gcol), q.shape)
        self.qs = q * (GDN_DIM**-0.5)
        self.kb = k * self.beta
        if saved is None:
            self.L = jnp.where(self.strict, _mx(self.kb, k, "nt") * self.decay, 0.0)
            self.attn = jnp.where(self.incl, _mx(self.qs, k, "nt") * self.decay, 0.0)
            A = _inv_unit_lower(self.L, jnp.where(eye, 1.0, 0.0).astype(f32))
        self.A2 = _split2(A)
        self.saved = jnp.concatenate([A, self.L, self.attn, self.decay], axis=1) if saved is None else saved
        self.kbg = self.kb * self.eg
        if sol is None:
            sol = _h3s(self.A2, _split2(jnp.concatenate([v * self.beta, self.kbg], axis=2)), "nn")
        self.sol = sol
        self.u = self.sol[:, :, :GDN_DIM]
        self.w = self.sol[:, :, GDN_DIM:]
        self.qd = self.qs * self.eg
        self.kd = k * self.ekd
        self.egl = jnp.exp(gl)


def _gdn_chain_inputs(qf_ref, qb_ref, gf_ref, gb_ref, cf_ref, cb_ref, b, hb, need_rows=True):
    qkv = [jnp.stack(_heads(qf_ref, (i,), hb, GDN_DIM) + _heads(qb_ref, (i,), hb, GDN_DIM)) for i in range(3)]
    cols, rows, betas = [], [], []
    for d, (g_ref, c_ref) in enumerate(((gf_ref, cf_ref), (gb_ref, cb_ref))):
        gates, gc = g_ref[...], c_ref[...]
        for hh in range(hb):
            h = b * hb + hh
            col = _lane_col(gc, h + 8 * d)
            cols.append(col)
            if need_rows:
                rows.append(jnp.broadcast_to(col, (CHUNK, LANES)).T[0:1, :])
            betas.append(_lane_col(gates, 16 + h + 8 * d))
    return qkv[0], qkv[1], qkv[2], jnp.stack(cols), (jnp.stack(rows) if need_rows else None), jnp.stack(betas)


GDN_CPS_FWD, GDN_CPS_BWD = 2, 1


def _chunk_views(refs, s_f, s_b):
    out = []
    for i, r in enumerate(refs):
        s = s_f if i % 2 == 0 else s_b
        rows = pl.ds(s * CHUNK, CHUNK)
        out.append(r.at[pl.ds(s, 1)] if len(r.shape) == 4 else (r.at[:, rows, :] if len(r.shape) == 3 else r.at[rows, :]))
    return out


def _gdn_specs(cps, hb):
    R, W = cps * CHUNK, hb * GDN_DIM
    qsp = lambda ci: pl.BlockSpec((3, R, W), lambda n, b: (0, ci(n), b))
    gsp = lambda ci: pl.BlockSpec((R, LANES), lambda n, b: (ci(n), 0))
    osp = lambda ci: pl.BlockSpec((R, W), lambda n, b: (ci(n), b))
    ssp = lambda ci: pl.BlockSpec((cps, hb, GDN_DIM, GDN_DIM), lambda n, b: (ci(n), b, 0, 0))
    asp = lambda ci: pl.BlockSpec((cps, hb, 4 * CHUNK, CHUNK), lambda n, b: (ci(n), b, 0, 0))
    usp = lambda ci: pl.BlockSpec((cps, hb, CHUNK, 2 * GDN_DIM), lambda n, b: (ci(n), b, 0, 0))
    return qsp, gsp, osp, ssp, asp, usp


def _gdn_fwd(qkvn, gates, gc):
    T = qkvn.shape[1]
    N = T // CHUNK
    nh, hb = GDN_HEADS, GDN_HB
    cps = min(GDN_CPS_FWD, N)
    NB = N // cps

    def body(*refs):
        n, b = pl.program_id(0), pl.program_id(1)
        s_scr = refs[-1]

        @pl.when(n == 0)
        def _():
            s_scr[0, pl.ds(b * hb, hb)] = jnp.zeros((hb, GDN_DIM, GDN_DIM), f32)
            s_scr[1, pl.ds(b * hb, hb)] = jnp.zeros((hb, GDN_DIM, GDN_DIM), f32)

        for s in range(cps):
            chunk(*_chunk_views(refs[:-1], s, cps - 1 - s), s_scr, b)

    def chunk(qf_ref, qb_ref, gf_ref, gb_ref, cf_ref, cb_ref, of_ref, ob_ref, sf_ref, sb_ref, af_ref, ab_ref, uf_ref, ub_ref, s_scr, b):
        hs = pl.ds(b * hb, hb)
        q, k, v, gcol, grow, beta = _gdn_chain_inputs(qf_ref, qb_ref, gf_ref, gb_ref, cf_ref, cb_ref, b, hb)
        ck = _GdnBatch(q, k, v, gcol, grow, beta, hb)
        S = jnp.concatenate([s_scr[0, hs], s_scr[1, hs]], axis=0)
        sf_ref[0], sb_ref[0] = S[:hb], S[hb:]
        af_ref[0], ab_ref[0] = ck.saved[:hb], ck.saved[hb:]
        uf_ref[0], ub_ref[0] = ck.sol[:hb], ck.sol[hb:]
        v_new = ck.u - _mx(ck.w, S, "nn")
        o = _mx(ck.qd, S, "nn") + _mx(ck.attn, v_new, "nn")
        S_new = S * ck.egl + _mx(ck.kd, v_new, "tn")
        for hh in range(hb):
            sl = slice(hh * GDN_DIM, (hh + 1) * GDN_DIM)
            of_ref[:, sl] = o[hh]
            ob_ref[:, sl] = o[hb + hh]
        s_scr[0, hs] = S_new[:hb]
        s_scr[1, hs] = S_new[hb:]

    fw, bw = (lambda n: n), (lambda n: NB - 1 - n)
    qsp, gsp, osp, ssp, asp, usp = _gdn_specs(cps, hb)
    o_shape = jax.ShapeDtypeStruct((T, nh * GDN_DIM), f32)
    s_shape = jax.ShapeDtypeStruct((N, nh, GDN_DIM, GDN_DIM), f32)
    a_shape = jax.ShapeDtypeStruct((N, nh, 4 * CHUNK, CHUNK), f32)
    u_shape = jax.ShapeDtypeStruct((N, nh, CHUNK, 2 * GDN_DIM), f32)
    return pl.pallas_call(
        body,
        out_shape=(o_shape, o_shape, s_shape, s_shape, a_shape, a_shape, u_shape, u_shape),
        grid=(NB, nh // hb),
        in_specs=[qsp(fw), qsp(bw), gsp(fw), gsp(bw), gsp(fw), gsp(bw)],
        out_specs=(osp(fw), osp(bw), ssp(fw), ssp(bw), asp(fw), asp(bw), usp(fw), usp(bw)),
        scratch_shapes=[pltpu.VMEM((2, nh, GDN_DIM, GDN_DIM), f32)],
        compiler_params=_cparams(("arbitrary", "arbitrary")),
        name="gdn_scan_fwd",
    )(qkvn, qkvn, gates, gates, gc, gc)


def _gdn_bwd(qkvn, gates, gc, st_f, st_b, a_f, a_b, u_f, u_b, do):
    T = qkvn.shape[1]
    N = T // CHUNK
    nh, hb = GDN_HEADS, GDN_HB
    nb = nh // hb
    cps = min(GDN_CPS_BWD, N)
    NB = N // cps

    def body(*refs):
        n, b = pl.program_id(0), pl.program_id(1)
        ds_scr = refs[-1]

        @pl.when(n == 0)
        def _():
            ds_scr[0, pl.ds(b * hb, hb)] = jnp.zeros((hb, GDN_DIM, GDN_DIM), f32)
            ds_scr[1, pl.ds(b * hb, hb)] = jnp.zeros((hb, GDN_DIM, GDN_DIM), f32)

        for s in range(cps):
            chunk(*_chunk_views(refs[:-1], cps - 1 - s, s), ds_scr, b)

    def chunk(qf_ref, qb_ref, gf_ref, gb_ref, cf_ref, cb_ref, sf_ref, sb_ref, af_ref, ab_ref, uf_ref, ub_ref, dof_ref, dob_ref,
              dqf_ref, dqb_ref, dgf_ref, dgb_ref, ds_scr, b):
        hs = pl.ds(b * hb, hb)

        @pl.when(b == 0)
        def _():
            dgf_ref[...] = jnp.zeros((CHUNK, LANES), f32)
            dgb_ref[...] = jnp.zeros((CHUNK, LANES), f32)

        q, k, v, gcol, grow, beta = _gdn_chain_inputs(qf_ref, qb_ref, gf_ref, gb_ref, cf_ref, cb_ref, b, hb, need_rows=False)
        ck = _GdnBatch(q, k, v, gcol, grow, beta, hb, saved=jnp.concatenate([af_ref[0], ab_ref[0]], axis=0),
                       sol=jnp.concatenate([uf_ref[0], ub_ref[0]], axis=0))
        S = jnp.concatenate([sf_ref[0], sb_ref[0]], axis=0)
        dS = jnp.concatenate([ds_scr[0, hs], ds_scr[1, hs]], axis=0)
        dov = jnp.stack(_heads(dof_ref, (), hb, GDN_DIM) + _heads(dob_ref, (), hb, GDN_DIM))
        v_new = ck.u - _mx(ck.w, S, "nn")
        d_vnew = _mx(ck.attn, dov, "tn") + _mx(ck.kd, dS, "nn")
        d_attn = jnp.where(ck.incl, _mx(dov, v_new, "nt"), 0.0)
        d_qd = _mx(dov, S, "nt")
        d_kd = _mx(v_new, dS, "nt")
        d_gl = jnp.sum(jnp.sum(dS * S, axis=2, keepdims=True), axis=1, keepdims=True) * ck.egl
        dS_new = dS * ck.egl + _mx(ck.qd, dov, "tn") - _mx(ck.w, d_vnew, "tn")
        ds_scr[0, hs] = dS_new[:hb]
        ds_scr[1, hs] = dS_new[hb:]
        d_w = -_mx(d_vnew, S, "nt")
        d_rhs = _h3s(ck.A2, _split2(jnp.concatenate([d_vnew, d_w], axis=2)), "tn")
        d_vb, d_kbg = d_rhs[:, :, :GDN_DIM], d_rhs[:, :, GDN_DIM:]
        dL = -jnp.where(ck.strict, _h3(d_rhs, ck.sol, "nt"), 0.0)
        P = dL * ck.decay
        Q = d_attn * ck.decay
        d_kb = _mx(P, k, "nn") + d_kbg * ck.eg
        d_qs = _mx(Q, k, "nn") + d_qd * ck.eg
        d_k = _mx(P, ck.kb, "tn") + _mx(Q, ck.qs, "tn") + d_kd * ck.ekd + d_kb * ck.beta
        E = dL * ck.L + d_attn * ck.attn
        kdsum = jnp.sum(d_kd * ck.kd, axis=2, keepdims=True)
        d_gc = (jnp.sum(E, axis=2, keepdims=True) - _colsum_col(E)
                + jnp.sum(d_kbg * ck.kbg, axis=2, keepdims=True)
                + jnp.sum(d_qd * ck.qd, axis=2, keepdims=True) - kdsum)
        d_gc = _add_at_last_row(d_gc, d_gl + jnp.sum(kdsum, axis=1, keepdims=True), hb)
        d_beta = jnp.sum(d_vb * v, axis=2, keepdims=True) + jnp.sum(d_kb * k, axis=2, keepdims=True)
        d_q = d_qs * (GDN_DIM**-0.5)
        d_v = d_vb * ck.beta
        lane = _iota2((CHUNK, LANES), 1)
        for d, (dq_ref, dg_ref) in enumerate(((dqf_ref, dgf_ref), (dqb_ref, dgb_ref))):
            acc = dg_ref[...]
            for hh in range(hb):
                c, h = d * hb + hh, b * hb + hh
                sl = slice(hh * GDN_DIM, (hh + 1) * GDN_DIM)
                dq_ref[0, :, sl] = d_q[c]
                dq_ref[1, :, sl] = d_k[c]
                dq_ref[2, :, sl] = d_v[c]
                acc = jnp.where(lane == h + 8 * d, d_gc[c], acc)
                acc = jnp.where(lane == 16 + h + 8 * d, d_beta[c], acc)
            dg_ref[...] = acc

        @pl.when(b == nb - 1)
        def _():
            for d, dg_ref in enumerate((dgf_ref, dgb_ref)):
                blk = dg_ref[...]
                dg_ref[...] = jnp.where(lane < 16, _exact01(_chunk_tri(CHUNK, d == 0), blk), blk)

    fw, bw = (lambda n: NB - 1 - n), (lambda n: n)
    qsp, gsp, osp, ssp, asp, usp = _gdn_specs(cps, hb)
    dq_shape = jax.ShapeDtypeStruct((3, T, nh * GDN_DIM), f32)
    dg_shape = jax.ShapeDtypeStruct((T, LANES), f32)
    return pl.pallas_call(
        body,
        out_shape=(dq_shape, dq_shape, dg_shape, dg_shape),
        grid=(NB, nb),
        in_specs=[qsp(fw), qsp(bw), gsp(fw), gsp(bw), gsp(fw), gsp(bw), ssp(fw), ssp(bw), asp(fw), asp(bw), usp(fw), usp(bw),
                  osp(fw), osp(bw)],
        out_specs=(qsp(fw), qsp(bw), gsp(fw), gsp(bw)),
        scratch_shapes=[pltpu.VMEM((2, nh, GDN_DIM, GDN_DIM), f32)],
        compiler_params=_cparams(("arbitrary", "arbitrary")),
        name="gdn_scan_bwd",
    )(qkvn, qkvn, gates, gates, gc, gc, st_f, st_b, a_f, a_b, u_f, u_b, do, do)


class _GlaBatch:
    def __init__(self, q, k, G, nf):
        B, Cn = q.shape[0], q.shape[1]
        self.incl, _, _ = _dir_masks(B, nf, Cn)
        self.eG = jnp.exp(G)
        self.enG = jnp.exp(-G)
        self.qg = q * (GLA_DK**-0.5) * self.eG
        self.kg = k * self.enG
        self.attn = jnp.where(self.incl, _mx(self.qg, self.kg, "nt"), 0.0)
        gl = _last_row(G, nf)
        self.ekd = jnp.exp(gl - G)
        self.kd = k * self.ekd
        self.egl = jnp.exp(gl)


GLA_CPS = 4


def _chunk_rows(s):
    return slice(s * CHUNK, (s + 1) * CHUNK)


def _gla_chain_inputs(refs_f, refs_b, rows_f, rows_b):
    nh = GLA_HEADS
    out = []
    for i, width in enumerate((GLA_DK, GLA_DK, GLA_DV, GLA_DK)):
        out.append(jnp.stack(_heads(refs_f[i], (), nh, width, rows_f) + _heads(refs_b[i], (), nh, width, rows_b)))
    return out


def _gla_specs(rows, ci):
    nh = GLA_HEADS
    return [
        pl.BlockSpec((rows, nh * GLA_DK), lambda n: (ci(n), 4096 // (nh * GLA_DK))),
        pl.BlockSpec((rows, nh * GLA_DK), lambda n: (ci(n), 4608 // (nh * GLA_DK))),
        pl.BlockSpec((rows, nh * GLA_DV), lambda n: (ci(n), 5120 // (nh * GLA_DV))),
        pl.BlockSpec((rows, nh * GLA_DK), lambda n: (ci(n), 0)),
    ]


def _gla_fwd(proj, G_f, G_b):
    T = proj.shape[0]
    N = T // CHUNK
    nh = GLA_HEADS
    cps = min(GLA_CPS, N)
    NB = N // cps

    def body(qf, kf, vf, gf, qb, kb, vb, gb, of_ref, ob_ref, sf_ref, sb_ref, s_scr):
        n = pl.program_id(0)

        @pl.when(n == 0)
        def _():
            s_scr[...] = jnp.zeros((2 * nh, GLA_DV, GLA_DK), f32)

        for s in range(cps):
            sb = cps - 1 - s
            q, k, v, G = _gla_chain_inputs((qf, kf, vf, gf), (qb, kb, vb, gb), _chunk_rows(s), _chunk_rows(sb))
            ck = _GlaBatch(q, k, G, nh)
            St = s_scr[...]
            sf_ref[s], sb_ref[sb] = St[:nh].astype(sf_ref.dtype), St[nh:].astype(sb_ref.dtype)
            o = _mx(ck.qg, St, "nt") + _mx(ck.attn, v, "nn")
            for i in range(nh):
                sl = slice(i * GLA_DV, (i + 1) * GLA_DV)
                of_ref[_chunk_rows(s), sl] = o[i]
                ob_ref[_chunk_rows(sb), sl] = o[nh + i]
            s_scr[...] = St * ck.egl + _mx(v, ck.kd, "tn")

    fw, bw = (lambda n: n), (lambda n: NB - 1 - n)
    osp = lambda ci: pl.BlockSpec((cps * CHUNK, nh * GLA_DV), lambda n: (ci(n), 0))
    ssp = lambda ci: pl.BlockSpec((cps, nh, GLA_DV, GLA_DK), lambda n: (ci(n), 0, 0, 0))
    o_shape = jax.ShapeDtypeStruct((T, nh * GLA_DV), f32)
    s_shape = jax.ShapeDtypeStruct((N, nh, GLA_DV, GLA_DK), MXU_DTYPE)
    return pl.pallas_call(
        body,
        out_shape=(o_shape, o_shape, s_shape, s_shape),
        grid=(NB,),
        in_specs=_gla_specs(cps * CHUNK, fw) + _gla_specs(cps * CHUNK, bw),
        out_specs=(osp(fw), osp(bw), ssp(fw), ssp(bw)),
        scratch_shapes=[pltpu.VMEM((2 * nh, GLA_DV, GLA_DK), f32)],
        compiler_params=_cparams(("arbitrary",)),
        name="gla_scan_fwd",
    )(proj, proj, proj, G_f, proj, proj, proj, G_b)


def _gla_bwd(proj, G_f, G_b, st_f, st_b, do):
    T = proj.shape[0]
    N = T // CHUNK
    nh = GLA_HEADS
    cps = min(GLA_CPS, N)
    NB = N // cps

    def body(qf, kf, vf, gf, sf_ref, dof_ref, qb, kb, vb, gb, sb_ref, dob_ref,
             dqf_ref, dkf_ref, dvf_ref, dgf_ref, dqb_ref, dkb_ref, dvb_ref, dgb_ref, ds_scr):
        n = pl.program_id(0)

        @pl.when(n == 0)
        def _():
            ds_scr[...] = jnp.zeros((2 * nh, GLA_DV, GLA_DK), f32)

        shp = (2 * nh, CHUNK, CHUNK)
        bb, ii, jj = _iota3(shp, 0), _iota3(shp, 1), _iota3(shp, 2)
        tri = jnp.where(jnp.logical_or(jnp.logical_and(bb < nh, jj >= ii), jnp.logical_and(bb >= nh, jj <= ii)), 1.0, 0.0)
        for s in range(cps):
            sf = cps - 1 - s
            rows = (_chunk_rows(sf), _chunk_rows(s))
            q, k, v, G = _gla_chain_inputs((qf, kf, vf, gf), (qb, kb, vb, gb), *rows)
            ck = _GlaBatch(q, k, G, nh)
            St = jnp.concatenate([sf_ref[sf], sb_ref[s]], axis=0).astype(f32)
            dSt = ds_scr[...]
            dov = jnp.stack(_heads(dof_ref, (), nh, GLA_DV, rows[0]) + _heads(dob_ref, (), nh, GLA_DV, rows[1]))
            d_attn = jnp.where(ck.incl, _mx(dov, v, "nt"), 0.0)
            d_qg = _mx(dov, St, "nn") + _mx(d_attn, ck.kg, "nn")
            d_kg = _mx(d_attn, ck.qg, "tn")
            d_v = _mx(ck.attn, dov, "tn") + _mx(ck.kd, dSt, "nt")
            d_kd = _mx(v, dSt, "nn")
            ds_scr[...] = dSt * ck.egl + _mx(dov, ck.qg, "tn")
            kdd = d_kd * ck.kd
            d_gl = jnp.sum(dSt * St, axis=1, keepdims=True) * ck.egl + jnp.sum(kdd, axis=1, keepdims=True)
            d_q = d_qg * ck.eG * (GLA_DK**-0.5)
            d_k = d_kg * ck.enG + d_kd * ck.ekd
            d_gk = _exact01(tri, _add_at_last_row(d_qg * ck.qg - d_kg * ck.kg - kdd, d_gl, nh))
            for d, (dq_ref, dk_ref, dv_ref, dg_ref) in enumerate(((dqf_ref, dkf_ref, dvf_ref, dgf_ref), (dqb_ref, dkb_ref, dvb_ref, dgb_ref))):
                for i in range(nh):
                    c = d * nh + i
                    ks, vs = slice(i * GLA_DK, (i + 1) * GLA_DK), slice(i * GLA_DV, (i + 1) * GLA_DV)
                    dq_ref[rows[d], ks] = d_q[c]
                    dk_ref[rows[d], ks] = d_k[c]
                    dv_ref[rows[d], vs] = d_v[c]
                    dg_ref[rows[d], ks] = d_gk[c]

    R = cps * CHUNK

    def in_specs(ci):
        return _gla_specs(R, ci) + [
            pl.BlockSpec((cps, nh, GLA_DV, GLA_DK), lambda n: (ci(n), 0, 0, 0)),
            pl.BlockSpec((R, nh * GLA_DV), lambda n: (ci(n), 0)),
        ]

    def out_specs(ci):
        ksp = pl.BlockSpec((R, nh * GLA_DK), lambda n: (ci(n), 0))
        return [ksp, ksp, pl.BlockSpec((R, nh * GLA_DV), lambda n: (ci(n), 0)), ksp]

    fw, bw = (lambda n: NB - 1 - n), (lambda n: n)
    k_shape = jax.ShapeDtypeStruct((T, nh * GLA_DK), f32)
    v_shape = jax.ShapeDtypeStruct((T, nh * GLA_DV), f32)
    return pl.pallas_call(
        body,
        out_shape=(k_shape, k_shape, v_shape, k_shape) * 2,
        grid=(NB,),
        in_specs=in_specs(fw) + in_specs(bw),
        out_specs=tuple(out_specs(fw) + out_specs(bw)),
        scratch_shapes=[pltpu.VMEM((2 * nh, GLA_DV, GLA_DK), f32)],
        compiler_params=_cparams(("arbitrary",)),
        name="gla_scan_bwd",
    )(proj, proj, proj, G_f, st_f, do, proj, proj, proj, G_b, st_b, do)


def _local_step(x, target, wcat, conv_w, avec, dvec, gdn_norm_w, w2f, w2b, b2f, b2b, gla_norm_w, proj_weights, w_pre, w_post,
                send_proj_grads=None, send_in_grads=None):
    h, h_t = _rms_pre(x, w_pre)
    proj = _matmul(h, wcat, "nt", f32, "proj_in", 2048, 512, 1024)
    qkvn = _gdn_prep(proj, conv_w)
    gates, gc, G_f, G_b = _gate_prep(proj, avec, dvec, w2f, w2b, b2f, b2b)
    oa_f, oa_b, sa_f, sa_b, inv_f, inv_b, sol_f, sol_b = _gdn_fwd(qkvn, gates, gc)
    og = _out_norm(oa_f, oa_b, proj, Z_BLK, gdn_norm_w, GDN_DIM, "gdn_out")
    ob_f, ob_b, sb_f, sb_b = _gla_fwd(proj, G_f, G_b)
    obg = _out_norm(ob_f, ob_b, proj, G_B_BLK, gla_norm_w, GLA_DV, "gla_out")
    wpg, wpl, wout = proj_weights(obg)
    (loss_row, d_wpost, dy, m, dout, dya, dyb, dproj, dog, dobg) = _mid(og, obg, proj, x, target, wpg, wpl, wout, w_post)

    d_wout = _matmul(m, dout, "tn", MXU_DTYPE, "dw_out", 1024, 512, 2048)
    d_wpg = _matmul(og, dya, "tn", MXU_DTYPE, "dw_proj_gdn", 1024, 512, 2048)
    d_wpl = _matmul(obg, dyb, "tn", MXU_DTYPE, "dw_proj_gla", 1024, 512, 2048)
    if send_proj_grads is not None:
        gla_norm_w = gla_norm_w + send_proj_grads(d_wpg, d_wpl, d_wout)

    do_b, dproj, d_gla_norm = _out_norm_bwd(dobg, ob_f, ob_b, proj, G_B_BLK, gla_norm_w, GLA_DV, "gla_out_bwd", dproj)
    dq_f, dk_f, dv_f, dgk_f, dq_b, dk_b, dv_b, dgk_b = _gla_bwd(proj, G_f, G_b, sb_f, sb_b, do_b)
    dproj, dsmall_gla, d_w2f, d_w2b, d_b2f, d_b2b = _gla_prep_bwd(proj, w2f, w2b, b2f, b2b, dgk_f, dgk_b,
                                                                 (dq_f, dq_b, dk_f, dk_b, dv_f, dv_b), dproj)

    do_a, dproj, d_gdn_norm = _out_norm_bwd(dog, oa_f, oa_b, proj, Z_BLK, gdn_norm_w, GDN_DIM, "gdn_out_bwd", dproj)
    dqkv_f, dqkv_b, dg_f, dg_b = _gdn_bwd(qkvn, gates, gc, sa_f, sa_b, inv_f, inv_b, sol_f, sol_b, do_a)
    dproj, d_conv = _gdn_prep_bwd(proj, conv_w, dqkv_f, dqkv_b, dproj)
    dproj, d_alog, d_dtb = _gdn_gates_bwd(proj, avec, dvec, dg_f, dg_b, dsmall_gla, dproj)

    d_wcat = _matmul(h_t, dproj, "nn", MXU_DTYPE, "dw_in", 1024, 512, 2048, out_t=True)
    started = None if send_in_grads is None else send_in_grads(d_wcat)
    dh = _matmul(dproj, wcat, "nn", f32, "dh", 2048, 1024, 512, after=started)
    grad_x, d_wpre = _rms_pre_bwd(x, w_pre, dh, dy)
    return dict(loss_row=loss_row, grad_x=grad_x, d_wcat=d_wcat, d_conv=d_conv, d_alog=d_alog, d_dtb=d_dtb,
                d_gdn_norm=d_gdn_norm, d_wpg=d_wpg, d_w2f=d_w2f, d_w2b=d_w2b, d_b2f=d_b2f, d_b2b=d_b2b,
                d_gla_norm=d_gla_norm, d_wpl=d_wpl, d_wout=d_wout, d_wpost=d_wpost, d_wpre=d_wpre)


def _to_cat(wf):
    pad = jnp.zeros((N_CAT - N_IN,) + wf.shape[1:], wf.dtype)
    return jnp.concatenate([wf[:4096], wf[4128:6176], wf[7232:9280], wf[6176:7200], wf[4096:4128], wf[7200:7232], pad], axis=0)


def _from_cat(wc):
    return jnp.concatenate([wc[:4096], wc[9216:9248], wc[4096:6144], wc[8192:9216], wc[9248:9280], wc[6144:8192]], axis=0)


def _lane_row(*pieces):
    row = jnp.concatenate(pieces, axis=1)
    return jnp.pad(row, ((0, 0), (0, LANES - row.shape[1])))


def _pad_w2(w2, row0):
    return jnp.pad(w2, ((row0, LANES - row0 - w2.shape[0]), (0, 0)))


SM_ROWS, SM_COLS = 32, 384


def _pack_small_shard(conv, w2f, w2b):
    lead = conv.shape[:-2]
    z = lambda r, c: jnp.zeros(lead + (r, c), f32)
    top = jnp.concatenate([conv, z(8 - CONV_K, SM_COLS)], axis=-2)
    mid = jnp.concatenate([w2f, w2b, z(16, SM_COLS - 128)], axis=-1)
    return jnp.concatenate([top, mid, z(SM_ROWS - 24, SM_COLS)], axis=-2)


SMALL_REGION = {
    "ln_pre_w": (0, 1, 0, 1024), "ln_post_w": (1, 1, 0, 1024), "gk_b2_fwd": (2, 1, 0, 512), "gk_b2_bwd": (2, 1, 512, 512),
    "gla_norm_w": (3, 1, 0, 256), "gdn_norm_w": (3, 1, 256, 128), "a_log_fwd": (4, 1, 0, 8), "a_log_bwd": (5, 1, 0, 8),
    "dt_bias_fwd": (6, 1, 0, 8), "dt_bias_bwd": (7, 1, 0, 8), "conv_w": (8, CONV_K, 0, SM_COLS),
    "gk_w2_fwd": (16, 16, 0, 64), "gk_w2_bwd": (16, 16, 128, 64),
}
SMALL_NAMES = tuple(SMALL_REGION)
SMALL_SHARDED = ("conv_w", "gk_w2_fwd", "gk_w2_bwd")
PACK_ROWS, PACK_COLS = 32, 1024
LOSS_AT = (13, 0)


def _region(name):
    r0, nr, l0, nl = SMALL_REGION[name]
    return slice(r0, r0 + nr), slice(l0, l0 + nl)


def _pack_small_grads(G):
    names = ("d_wpre", "d_wpost", "d_b2f", "d_b2b", "d_gla_norm", "d_gdn_norm", "d_alog", "d_dtb", "loss_row", "d_conv", "d_w2f", "d_w2b")

    def body(wpre, wpost, b2f, b2b, glan, gdnn, alog, dtb, lrow, conv, w2f, w2b, o_ref):
        o_ref[...] = jnp.zeros(o_ref.shape, f32)
        loss = 0.5 * jnp.sum(lrow[...], axis=1, keepdims=True) * (1.0 / D_MODEL)
        whole = {"ln_pre_w": wpre, "ln_post_w": wpost, "gk_b2_fwd": b2f, "gk_b2_bwd": b2b, "gla_norm_w": glan, "gdn_norm_w": gdnn}
        for j in range(N_DEV):
            for name, ref in whole.items():
                o_ref[(j,) + _region(name)] = ref[...]
            o_ref[(j,) + _region("a_log_fwd")] = alog[0:1, 0:8]
            o_ref[(j,) + _region("a_log_bwd")] = alog[0:1, 8:16]
            o_ref[(j,) + _region("dt_bias_fwd")] = dtb[0:1, 0:8]
            o_ref[(j,) + _region("dt_bias_bwd")] = dtb[0:1, 8:16]
            o_ref[(j,) + _region("conv_w")] = conv[:, j * SM_COLS : (j + 1) * SM_COLS]
            o_ref[(j,) + _region("gk_w2_fwd")] = w2f[32:48, j * 64 : (j + 1) * 64]
            o_ref[(j,) + _region("gk_w2_bwd")] = w2b[48:64, j * 64 : (j + 1) * 64]
            o_ref[j, LOSS_AT[0] : LOSS_AT[0] + 1, LOSS_AT[1] : LOSS_AT[1] + 1] = loss

    return pl.pallas_call(
        body, out_shape=jax.ShapeDtypeStruct((N_DEV, PACK_ROWS, PACK_COLS), f32), name="pack_small_grads",
    )(*[G[n] for n in names])


def _small_adam(land, own, W, M, V):
    bc1 = 1.0 - ADAM_B1**ADAM_STEP
    bc2 = 1.0 - ADAM_B2**ADAM_STEP
    n = len(SMALL_NAMES)

    def body(land_ref, own_ref, *refs):
        w_refs, m_refs, v_refs = refs[0:n], refs[n : 2 * n], refs[2 * n : 3 * n]
        outs, loss_ref, g_scr = refs[3 * n : 7 * n], refs[7 * n], refs[7 * n + 1]
        me = _my_device()
        g = jnp.where(me == 0, own_ref[0], land_ref[0])
        for j in range(1, N_DEV):
            g = g + jnp.where(me == j, own_ref[j], land_ref[j])
        g_scr[...] = g
        loss_ref[...] = g_scr[LOSS_AT[0] : LOSS_AT[0] + 1, LOSS_AT[1] : LOSS_AT[1] + 1]
        for i, name in enumerate(SMALL_NAMES):
            gp = g_scr[_region(name)]
            lead = (0,) if name in SMALL_SHARDED else (Ellipsis,)
            w, m, v = w_refs[i][lead], m_refs[i][lead], v_refs[i][lead]
            m2 = ADAM_B1 * m + (1.0 - ADAM_B1) * gp
            v2 = ADAM_B2 * v + (1.0 - ADAM_B2) * (gp * gp)
            delta = -ADAM_LR * ((m2 / bc1) / (jnp.sqrt(v2 / bc2) + ADAM_EPS) + ADAM_WD * w)
            for k, val in enumerate((gp, delta, m2, v2)):
                outs[4 * i + k][lead] = val

    shapes = [jax.ShapeDtypeStruct(W[name].shape, f32) for name in SMALL_NAMES for _ in range(4)]
    res = pl.pallas_call(
        body,
        out_shape=tuple(shapes) + (jax.ShapeDtypeStruct((1, 1), f32),),
        scratch_shapes=[pltpu.VMEM((PACK_ROWS, PACK_COLS), f32)],
        name="adam_small",
    )(land, own, *[P[name] for P in (W, M, V) for name in SMALL_NAMES])
    return {name: res[4 * i : 4 * i + 4] for i, name in enumerate(SMALL_NAMES)}, res[4 * n]


_MESH = pl.DeviceIdType.MESH


N_BIG_COPIES, N_SMALL_COPIES = 9, 7


def _gather_weights(w_in_s, sm_s):
    shapes = (w_in_s.shape, sm_s.shape)
    dtypes = (MXU_DTYPE, f32)

    def body(win_ref, sm_ref, gin_ref, gsm_ref, own_ref, send_sems, recv_sems, own_sem):
        x, y, c = lax.axis_index("x"), lax.axis_index("y"), lax.axis_index("c")
        me, sibling = (x, y, c), (x, y, 1 - c)
        chips = [(1 - x, y), (x, 1 - y), (1 - x, 1 - y)]

        def idx(px, py, pc):
            return 4 * px + 2 * py + pc

        for r in range(w_in_s.shape[1] // LANES):
            cols = slice(r * LANES, (r + 1) * LANES)
            own_ref[:, cols] = win_ref[:, cols].astype(MXU_DTYPE)
        keep_own = pltpu.make_async_copy(own_ref, gin_ref.at[idx(*me)], own_sem.at[0])
        keep_own.start()
        gsm_ref[idx(*me)] = sm_ref[...]

        def copy(sem, blk, to, src=None):
            return pltpu.make_async_remote_copy(src_ref=blk if src is None else src, dst_ref=blk, send_sem=send_sems.at[sem],
                                                recv_sem=recv_sems.at[sem], device_id=to, device_id_type=_MESH)

        def small(k, block, to):
            return copy(N_BIG_COPIES + k, gsm_ref.at[idx(*block)], to)

        half = w_in_s.shape[1] // 2
        xn, yn, dg = (1 - x, y), (x, 1 - y), (1 - x, 1 - y)

        def big(sem, block, to, part=None, src=None):
            blk = gin_ref.at[idx(*block)]
            return copy(sem, blk if part is None else blk.at[:, pl.ds(part * half, half)], to, src)

        started = [big(0, me, sibling, src=own_ref), big(1, me, (*xn, c), src=own_ref), big(2, me, (*yn, c), src=own_ref),
                   small(0, me, sibling)]
        started += [small(1 + j, me, (*chip, c)) for j, chip in enumerate(chips)]
        for cp in started:
            cp.start()

        def then(arrived, forwards):
            arrived.wait_recv()
            for cp in forwards:
                cp.start()
            started.extend(forwards)

        then(big(1, (*xn, c), me), [big(3, (*xn, c), (*yn, c), 1), big(5, (*xn, c), sibling)])
        then(big(2, (*yn, c), me), [big(4, (*yn, c), (*xn, c), 0), big(6, (*yn, c), sibling)])
        then(big(4, (*dg, c), me, 0), [big(7, (*dg, c), sibling, 0)])
        then(big(3, (*dg, c), me, 1), [big(8, (*dg, c), sibling, 1)])
        for j, chip in enumerate(chips):
            then(small(1 + j, (*chip, c), me), [small(4 + j, (*chip, c), sibling)])
        big(0, sibling, me).wait_recv()
        big(5, (*xn, 1 - c), me).wait_recv()
        big(6, (*yn, 1 - c), me).wait_recv()
        big(7, (*dg, 1 - c), me, 0).wait_recv()
        big(8, (*dg, 1 - c), me, 1).wait_recv()
        small(0, sibling, me).wait_recv()
        for j, chip in enumerate(chips):
            small(4 + j, (*chip, 1 - c), me).wait_recv()
        for cp in started:
            cp.wait_send()
        keep_own.wait()

    vm = pl.BlockSpec(memory_space=pltpu.VMEM)
    n_copies = N_BIG_COPIES + N_SMALL_COPIES
    return pl.pallas_call(
        body,
        out_shape=tuple(jax.ShapeDtypeStruct((N_DEV,) + s, d) for s, d in zip(shapes, dtypes)),
        in_specs=[vm, vm],
        out_specs=(_HBM_SPEC, vm),
        scratch_shapes=[pltpu.VMEM(w_in_s.shape, MXU_DTYPE), pltpu.SemaphoreType.DMA((n_copies,)), pltpu.SemaphoreType.DMA((n_copies,)),
                        pltpu.SemaphoreType.DMA((1,))],
        compiler_params=pltpu.CompilerParams(vmem_limit_bytes=VMEM_LIMIT),
        name="gather_weights",
    )(w_in_s, sm_s)


N_CHIP = 4
_EFFECT = pltpu.SideEffectType.DATAFLOW_SIDE_EFFECTING
_HBM_SPEC = pl.BlockSpec(memory_space=pltpu.HBM)
_SEM_SPEC = pl.BlockSpec(memory_space=pltpu.SEMAPHORE)


def _relation_peer(k, x, y, c):
    px = (1 - x) if (k & 4) else x
    py = (1 - y) if (k & 2) else y
    pc = (1 - c) if (k & 1) else c
    return (px, py, pc), 4 * px + 2 * py + pc


_PLAN_COPIES = {"gather": N_DEV - 1, "blocks": N_DEV - 1, "sibling": 4, "chips": 3}
_PLAN_SLOTS = {"gather": N_DEV, "blocks": N_DEV, "sibling": 4, "chips": 4}


def _plan_copies(plan, src_ref, land_ref, send_sems, recv_sems, arrival):
    x, y, c = lax.axis_index("x"), lax.axis_index("y"), lax.axis_index("c")
    me, my_chip = 4 * x + 2 * y + c, 2 * x + y
    out = []

    def add(i, src, there, here, dev):
        out.append(pltpu.make_async_remote_copy(src_ref=src, dst_ref=land_ref.at[here if arrival else there], send_sem=send_sems.at[i],
                                                recv_sem=recv_sems.at[i], device_id=dev, device_id_type=_MESH))

    if plan in ("gather", "blocks"):
        for k in range(1, N_DEV):
            dev, p = _relation_peer(k, x, y, c)
            add(k - 1, src_ref.at[p] if plan == "blocks" else src_ref, me, p, dev)
    elif plan == "sibling":
        for j in range(4):
            add(j, src_ref.at[2 * j + 1 - c], j, j, (x, y, 1 - c))
    else:
        for i, k in enumerate((2, 4, 6)):
            (px, py, pc), _ = _relation_peer(k, x, y, c)
            add(i, src_ref.at[2 * px + py], my_chip, 2 * px + py, (px, py, pc))
    return out


def _exchange_start(src, plan, name, after=None):
    n = _PLAN_COPIES[plan]
    land = lax.empty((_PLAN_SLOTS[plan],) + (src.shape if plan == "gather" else src.shape[1:]), src.dtype)
    extra = [] if after is None else [after]

    def body(src_ref, land_ref, *rest):
        send_sems, recv_sems, _, _, token = rest[len(extra) :]
        for cp in _plan_copies(plan, src_ref, land_ref, send_sems, recv_sems, False):
            cp.start()
        token[...] = jnp.zeros_like(token)

    return pl.pallas_call(
        body,
        name=name,
        out_shape=(pltpu.SemaphoreType.DMA((n,)), pltpu.SemaphoreType.DMA((n,)), pltpu.HBM(src.shape, src.dtype),
                   pltpu.HBM(land.shape, land.dtype), jax.ShapeDtypeStruct((8, LANES), f32)),
        in_specs=(_HBM_SPEC, _HBM_SPEC) + (_ANY_SPEC,) * len(extra),
        out_specs=(_SEM_SPEC, _SEM_SPEC, _HBM_SPEC, _HBM_SPEC, pl.BlockSpec(memory_space=pltpu.VMEM)),
        input_output_aliases={0: 2, 1: 3},
        compiler_params=pltpu.CompilerParams(has_side_effects=_EFFECT),
    )(pltpu.with_memory_space_constraint(src, pltpu.HBM), pltpu.with_memory_space_constraint(land, pltpu.HBM), *extra)


def _exchange_wait(started, after, plan, name):
    send_sems, recv_sems, src_thru, land_thru, _ = started

    def body(src_ref, land_ref, send_sems, recv_sems, after_ref, src_dead, got_ref):
        for cp in _plan_copies(plan, src_ref, land_ref, send_sems, recv_sems, True):
            cp.wait_send()
            cp.wait_recv()

    return pl.pallas_call(
        body,
        name=name,
        out_shape=(pltpu.HBM(src_thru.shape, src_thru.dtype), pltpu.HBM(land_thru.shape, land_thru.dtype)),
        in_specs=(_HBM_SPEC, _HBM_SPEC, _SEM_SPEC, _SEM_SPEC, pl.BlockSpec(memory_space=pl.ANY)),
        out_specs=(_HBM_SPEC, _HBM_SPEC),
        input_output_aliases={0: 0, 1: 1},
        compiler_params=pltpu.CompilerParams(has_side_effects=_EFFECT),
    )(src_thru, land_thru, send_sems, recv_sems, after)


def _pair_sum(g, r, name, tc=LANES):
    _, R, Cc = g.shape

    def body(g_ref, r_ref, o_ref):
        o_ref[...] = (g_ref[...].astype(f32) + r_ref[...].astype(f32)).astype(o_ref.dtype)

    return pl.pallas_call(
        body,
        out_shape=jax.ShapeDtypeStruct(r.shape, r.dtype),
        grid=(N_CHIP, Cc // tc),
        in_specs=[pl.BlockSpec((1, R, tc), lambda k, i: (2 * k + lax.axis_index("c"), 0, i)), pl.BlockSpec((1, R, tc), lambda k, i: (k, 0, i))],
        out_specs=pl.BlockSpec((1, R, tc), lambda k, i: (k, 0, i)),
        compiler_params=_cparams(("parallel", "parallel")),
        name=name,
    )(g, r)


def _sum_adam(parts, w, m, v, name, tb, tc=None, own=None, own_slot=None):
    R, Cc = w.shape
    n_parts = parts.shape[0]
    tb = R if tc else min(tb, R)
    tc = tc or Cc
    assert R % tb == 0 and Cc % tc == 0
    bc1 = 1.0 - ADAM_B1**ADAM_STEP
    bc2 = 1.0 - ADAM_B2**ADAM_STEP
    extra = [] if own is None else [own]

    def body(p_ref, w_ref, m_ref, v_ref, *rest):
        g_ref, d_ref, nm_ref, nv_ref = rest[len(extra) :]

        def part(j):
            pj = p_ref[j].astype(f32)
            return pj if own is None else jnp.where(own_slot() == j, rest[0][0].astype(f32), pj)

        g = part(0)
        for j in range(1, n_parts):
            g = g + part(j)
        g_ref[...] = g
        m2 = ADAM_B1 * m_ref[...] + (1.0 - ADAM_B1) * g
        v2 = ADAM_B2 * v_ref[...] + (1.0 - ADAM_B2) * (g * g)
        nm_ref[...] = m2
        nv_ref[...] = v2
        d_ref[...] = -ADAM_LR * ((m2 / bc1) / (jnp.sqrt(v2 / bc2) + ADAM_EPS) + ADAM_WD * w_ref[...])

    blk = pl.BlockSpec((tb, tc), lambda i, j: (i, j))
    o = jax.ShapeDtypeStruct((R, Cc), f32)
    return pl.pallas_call(
        body,
        out_shape=(o, o, o, o),
        grid=(R // tb, Cc // tc),
        in_specs=[pl.BlockSpec((n_parts, tb, tc), lambda i, j: (0, i, j)), blk, blk, blk]
        + [pl.BlockSpec((1, tb, tc), lambda i, j: (own_slot(), i, j))] * len(extra),
        out_specs=(blk, blk, blk, blk),
        compiler_params=_cparams(("parallel", "parallel")),
        name=name,
    )(parts, w, m, v, *extra)


def _my_device():
    return 4 * lax.axis_index("x") + 2 * lax.axis_index("y") + lax.axis_index("c")


def _my_chip():
    return 2 * lax.axis_index("x") + lax.axis_index("y")


_WEIGHTS = ["ln_pre_w", "w_in", "conv_w", "a_log_fwd", "a_log_bwd", "dt_bias_fwd", "dt_bias_bwd", "gdn_norm_w", "w_proj_gdn",
            "gk_w2_fwd", "gk_b2_fwd", "gk_w2_bwd", "gk_b2_bwd", "gla_norm_w", "w_proj_gla", "w_out", "ln_post_w"]


def kernel(x, ln_pre_w, w_in, conv_w, a_log_fwd, a_log_bwd, dt_bias_fwd, dt_bias_bwd, gdn_norm_w, w_proj_gdn, gk_w2_fwd, gk_b2_fwd, gk_w2_bwd, gk_b2_bwd, gla_norm_w, w_proj_gla, w_out, ln_post_w, loss_target, m_ln_pre_w, m_w_in, m_conv_w, m_a_log_fwd, m_a_log_bwd, m_dt_bias_fwd, m_dt_bias_bwd, m_gdn_norm_w, m_w_proj_gdn, m_gk_w2_fwd, m_gk_b2_fwd, m_gk_w2_bwd, m_gk_b2_bwd, m_gla_norm_w, m_w_proj_gla, m_w_out, m_ln_post_w, v_ln_pre_w, v_w_in, v_conv_w, v_a_log_fwd, v_a_log_bwd, v_dt_bias_fwd, v_dt_bias_bwd, v_gdn_norm_w, v_w_proj_gdn, v_gk_w2_fwd, v_gk_b2_fwd, v_gk_w2_bwd, v_gk_b2_bwd, v_gla_norm_w, v_w_proj_gla, v_w_out, v_ln_post_w):
    args = locals()
    W = {n: args[n] for n in _WEIGHTS}
    M = {n: args["m_" + n] for n in _WEIGHTS}
    V = {n: args["v_" + n] for n in _WEIGHTS}

    wp_stack = lambda P: jnp.concatenate([P["w_proj_gdn"], P["w_proj_gla"], P["w_out"]], axis=0)
    sm_pack = lambda P: _pack_small_shard(P["conv_w"][0], P["gk_w2_fwd"][0], P["gk_w2_bwd"][0])
    me = 4 * lax.axis_index("x") + 2 * lax.axis_index("y") + lax.axis_index("c")
    w_in_t, m_in_t, v_in_t = (jnp.transpose(a[0]) for a in (w_in, m_w_in, v_w_in))
    g_in, g_sm = _gather_weights(w_in_t, sm_pack(W))
    wp_own = wp_stack(W).astype(MXU_DTYPE)
    wp_started = _exchange_start(wp_own, "gather", "gather_proj_start", after=g_sm)

    def proj_weights(after):
        own, land = _exchange_wait(wp_started, after, "gather", "gather_proj_wait")
        g_p = lax.dynamic_update_slice(land, own[None], (me, 0, 0, 0))
        return tuple(g_p[:, i].reshape(D_MODEL, D_MODEL) for i in range(3))

    rows_p = 3 * D_MODEL // N_DEV
    sent = {}

    def send_proj_grads(d_wpg, d_wpl, d_wout):
        p_p = jnp.stack([d_wpg, d_wpl, d_wout]).reshape(3, N_DEV, D_MODEL // N_DEV, D_MODEL).transpose(1, 0, 2, 3)
        sent["p_p"] = p_p.reshape(N_DEV, rows_p, D_MODEL)
        sent["started"] = _exchange_start(sent["p_p"], "blocks", "exchange_proj_start")
        return sent["started"][4][0:1, 0:1]

    def send_in_grads(d_wcat):
        p_in = _from_cat(d_wcat).reshape(N_DEV, SHARD_IN, D_MODEL)
        sib = _exchange_start(p_in, "sibling", "exchange_in_sibling_start")
        p_p, land_p = _exchange_wait(sent["started"], sib[4], "blocks", "exchange_proj_wait")
        sent["o_p"] = _sum_adam(land_p, wp_stack(W).reshape(rows_p, D_MODEL), wp_stack(M).reshape(rows_p, D_MODEL),
                                wp_stack(V).reshape(rows_p, D_MODEL), "adam_w_proj", 128, own=p_p, own_slot=_my_device)
        p_in, sib_land = _exchange_wait(sib, sent["o_p"][0], "sibling", "exchange_in_sibling_wait")
        sent["chips"] = _exchange_start(_pair_sum(p_in, sib_land, "pair_sum_w_in", tc=512), "chips", "exchange_in_chips_start")
        return sent["chips"][4]

    wcat = _to_cat(g_in.reshape(N_IN, D_MODEL))
    conv_full = g_sm[:, 0:CONV_K, :].transpose(1, 0, 2).reshape(CONV_K, N_DEV * SM_COLS)
    w2f_full = g_sm[:, 8:24, 0:64].transpose(1, 0, 2).reshape(16, 512)
    w2b_full = g_sm[:, 8:24, 64:128].transpose(1, 0, 2).reshape(16, 512)

    avec = _lane_row(a_log_fwd, a_log_bwd)
    dvec = _lane_row(dt_bias_fwd, dt_bias_bwd)
    G = _local_step(x[0], loss_target[0], wcat, conv_full, avec, dvec, gdn_norm_w, _pad_w2(w2f_full, 32), _pad_w2(w2b_full, 48),
                    gk_b2_fwd, gk_b2_bwd, gla_norm_w, proj_weights, ln_pre_w + wp_started[4][0:1, 0:1], ln_post_w, send_proj_grads,
                    send_in_grads)

    small_started = _exchange_start(_pack_small_grads(G), "blocks", "exchange_small_start")
    q_in, land_in = _exchange_wait(sent["chips"], small_started[4], "chips", "exchange_in_chips_wait")
    o_in = _sum_adam(land_in, w_in_t, m_in_t, v_in_t, "adam_w_in", SHARD_IN, tc=256, own=q_in, own_slot=_my_chip)
    o_p = sent["o_p"]
    p_small, land_small = _exchange_wait(small_started, o_in[0], "blocks", "exchange_small_wait")
    o_small, loss = _small_adam(land_small, p_small, W, M, V)

    res = []
    for kind in range(4):
        per = {name: o_small[name][kind] for name in SMALL_NAMES}
        per["w_in"] = jnp.transpose(o_in[kind])[None]
        pp = o_p[kind].reshape(3, 1, D_MODEL // N_DEV, D_MODEL)
        per["w_proj_gdn"], per["w_proj_gla"], per["w_out"] = pp[0], pp[1], pp[2]
        res.append([per[n] for n in _WEIGHTS])
    return (loss[0, 0], G["grad_x"][None], *res[0], *res[1], *res[2], *res[3])
```

```python
import jax
import jax.numpy as jnp
from jax import lax
from jax.experimental import pallas as pl
from jax.experimental.pallas import tpu as pltpu

f32 = jnp.float32
MXU_DTYPE = jnp.bfloat16

D_MODEL = 1024
CHUNK = 64
NORM_EPS = 1e-6
GDN_HEADS, GDN_DIM = 8, 128
GLA_HEADS, GLA_DK, GLA_DV = 4, 128, 256
GATE_NORMALIZER = 16.0
CONV_K = 5
N_IN = 9280
N_DEV = 8
SHARD_IN = N_IN // N_DEV

N_CAT = 9728
SMALL_OFF = 9216
LANES = 128

ADAM_LR, ADAM_B1, ADAM_B2, ADAM_EPS, ADAM_WD, ADAM_STEP = 0.001, 0.9, 0.999, 1e-08, 0.01, 10

VMEM_LIMIT = 56 * 1024 * 1024


def _cparams(sem=None):
    return pltpu.CompilerParams(dimension_semantics=sem, vmem_limit_bytes=VMEM_LIMIT)


_DN = {"nn": (((1,), (0,)), ((), ())), "nt": (((1,), (1,)), ((), ())), "tn": (((0,), (0,)), ((), ()))}
_BDN = {"nn": (((2,), (1,)), ((0,), (0,))), "nt": (((2,), (2,)), ((0,), (0,))), "tn": (((1,), (1,)), ((0,), (0,)))}


def _dot16(a, b, kind):
    return lax.dot_general(a, b, (_DN if a.ndim == 2 else _BDN)[kind], preferred_element_type=f32)


def _mx(a, b, kind):
    return _dot16(a.astype(MXU_DTYPE), b.astype(MXU_DTYPE), kind)


def _split2(a):
    hi = a.astype(jnp.bfloat16)
    return hi, (a - hi.astype(f32)).astype(jnp.bfloat16)


def _split3(a):
    p1 = a.astype(jnp.bfloat16)
    r = a - p1.astype(f32)
    p2 = r.astype(jnp.bfloat16)
    return p1, p2, (r - p2.astype(f32)).astype(jnp.bfloat16)


def _h3s(a2, b2, kind):
    (ah, al), (bh, bl) = a2, b2
    return _dot16(ah, bh, kind) + (_dot16(ah, bl, kind) + _dot16(al, bh, kind))


def _h3(a, b, kind):
    return _h3s(_split2(a), _split2(b), kind)


def _exact01(t01, x, kind="nn"):
    t = t01.astype(jnp.bfloat16)
    x1, x2, x3 = _split3(x)
    return _dot16(t, x1, kind) + (_dot16(t, x2, kind) + _dot16(t, x3, kind))


def _colsum_col(e):
    ones = jnp.ones(e.shape[:-1] + (LANES,), jnp.bfloat16)
    e1, e2, e3 = _split3(e)
    return (_dot16(e1, ones, "tn") + (_dot16(e2, ones, "tn") + _dot16(e3, ones, "tn")))[..., 0:1]


def _iota3(shape, axis):
    return lax.broadcasted_iota(jnp.int32, shape, axis)


def _dir_masks(B, nf, Cn):
    shp = (B, Cn, Cn)
    bb, ii, jj = _iota3(shp, 0), _iota3(shp, 1), _iota3(shp, 2)
    fwd = bb < nf
    rev = jnp.logical_not(fwd)
    incl = jnp.logical_or(jnp.logical_and(fwd, ii >= jj), jnp.logical_and(rev, ii <= jj))
    strict = jnp.logical_and(incl, ii != jj)
    return incl, strict, ii == jj


def _last_row(x, nf):
    B, Cn = x.shape[0], x.shape[1]
    fwd = _iota3((B, 1, 1), 0) < nf
    return jnp.where(fwd, x[:, Cn - 1 : Cn, :], x[:, 0:1, :])


def _add_at_last_row(x, val, nf):
    B, Cn = x.shape[0], x.shape[1]
    bb, rr = _iota3((B, Cn, 1), 0), _iota3((B, Cn, 1), 1)
    at = jnp.logical_or(jnp.logical_and(bb < nf, rr == Cn - 1), jnp.logical_and(bb >= nf, rr == 0))
    return x + jnp.where(at, val, 0.0)


def _sigmoid(x):
    return jax.nn.sigmoid(x)


def _silu(x):
    return x * _sigmoid(x)


def _silu_grad(x):
    s = _sigmoid(x)
    return s * (1.0 + x * (1.0 - s))


def _softplus(x):
    u = jnp.exp(-jnp.abs(x))
    l1p = jnp.where(u < 1e-3, u * (1.0 - u * (0.5 - u * (1.0 / 3.0))), jnp.log(1.0 + u))
    return jnp.maximum(x, 0.0) + l1p


def _mxr(x):
    return x.astype(MXU_DTYPE).astype(f32)


def _iota2(shape, axis):
    return lax.broadcasted_iota(jnp.int32, shape, axis)


def _matmul(a, b, kind, out_dtype, name, tm, tn, tk, out_t=False, after=None):
    extra = [] if after is None else [after]
    if kind == "nn":
        (M, K), N = a.shape, b.shape[1]
    elif kind == "nt":
        (M, K), N = a.shape, b.shape[0]
    else:
        (K, M), N = a.shape, b.shape[1]
    tm, tn, tk = min(tm, M), min(tn, N), min(tk, K)
    assert M % tm == 0 and N % tn == 0 and K % tk == 0, (name, M, N, K)
    nk = K // tk
    if kind == "tn":
        a_spec = pl.BlockSpec((tk, tm), lambda i, j, k: (k, i))
    else:
        a_spec = pl.BlockSpec((tm, tk), lambda i, j, k: (i, k))
    b_mode = dict(pipeline_mode=pl.Buffered(1)) if (tn == N and tk == K) else {}
    if kind == "nt":
        b_spec = pl.BlockSpec((tn, tk), lambda i, j, k: (j, k), **b_mode)
    else:
        b_spec = pl.BlockSpec((tk, tn), lambda i, j, k: (k, j), **b_mode)

    def body(a_ref, b_ref, *rest):
        o_ref, acc = rest[len(extra)], rest[len(extra) + 1 :]
        p = _mx(a_ref[...], b_ref[...], kind)
        if nk == 1:
            o_ref[...] = (p.T if out_t else p).astype(out_dtype)
        else:
            assert not out_t
            acc_ref = acc[0]
            k = pl.program_id(2)

            @pl.when(k == 0)
            def _():
                acc_ref[...] = p

            @pl.when(k > 0)
            def _():
                acc_ref[...] += p

            @pl.when(k == nk - 1)
            def _():
                o_ref[...] = acc_ref[...].astype(out_dtype)

    return pl.pallas_call(
        body,
        out_shape=jax.ShapeDtypeStruct((N, M) if out_t else (M, N), out_dtype),
        grid=(M // tm, N // tn, nk),
        in_specs=[a_spec, b_spec] + [pl.BlockSpec((8, LANES), lambda i, j, k: (0, 0))] * len(extra),
        out_specs=pl.BlockSpec((tn, tm), lambda i, j, k: (j, i)) if out_t else pl.BlockSpec((tm, tn), lambda i, j, k: (i, j)),
        scratch_shapes=[] if nk == 1 else [pltpu.VMEM((tm, tn), f32)],
        compiler_params=_cparams(("parallel", "parallel", "arbitrary")),
        name=name,
    )(a, b, *extra)


ROW_TILE = 512


def _rms_pre(x, w):
    T = x.shape[0]
    tb = min(ROW_TILE, T)

    def body(x_ref, w_ref, h_ref, ht_ref):
        xv = x_ref[...]
        r = lax.rsqrt(jnp.mean(xv * xv, axis=1, keepdims=True) + NORM_EPS)
        h = xv * r * w_ref[...]
        h_ref[...] = h.astype(h_ref.dtype)
        ht_ref[...] = h.T.astype(ht_ref.dtype)

    return pl.pallas_call(
        body,
        out_shape=(jax.ShapeDtypeStruct((T, D_MODEL), MXU_DTYPE), jax.ShapeDtypeStruct((D_MODEL, T), MXU_DTYPE)),
        grid=(T // tb,),
        in_specs=[pl.BlockSpec((tb, D_MODEL), lambda i: (i, 0)), pl.BlockSpec((1, D_MODEL), lambda i: (0, 0))],
        out_specs=(pl.BlockSpec((tb, D_MODEL), lambda i: (i, 0)), pl.BlockSpec((D_MODEL, tb), lambda i: (0, i))),
        compiler_params=_cparams(("parallel",)),
        name="rms_pre",
    )(x, w)


def _rms_pre_bwd(x, w, dh, dy):
    T = x.shape[0]
    tb = min(ROW_TILE, T)

    def body(x_ref, w_ref, dh_ref, dy_ref, dx_ref, dw_ref):
        i = pl.program_id(0)
        xv, dhv = x_ref[...], dh_ref[...]
        r = lax.rsqrt(jnp.mean(xv * xv, axis=1, keepdims=True) + NORM_EPS)
        dhw = dhv * w_ref[...]
        dx_ref[...] = dy_ref[...] + r * dhw - xv * (r * r * r) * jnp.mean(dhw * xv, axis=1, keepdims=True)
        part = jnp.sum(dhv * xv * r, axis=0, keepdims=True)

        @pl.when(i == 0)
        def _():
            dw_ref[...] = part

        @pl.when(i > 0)
        def _():
            dw_ref[...] += part

    blk = pl.BlockSpec((tb, D_MODEL), lambda i: (i, 0))
    row = pl.BlockSpec((1, D_MODEL), lambda i: (0, 0))
    return pl.pallas_call(
        body,
        out_shape=(jax.ShapeDtypeStruct((T, D_MODEL), f32), jax.ShapeDtypeStruct((1, D_MODEL), f32)),
        grid=(T // tb,),
        in_specs=[blk, row, blk, blk],
        out_specs=(blk, row),
        compiler_params=_cparams(("arbitrary",)),
        name="rms_pre_bwd",
    )(x, w, dh, dy)


SHIFTS = (-2, -1, 1, 2)


SHIFT_PAD = 8
PREP_HEADS, PREP_HEADS_BWD = 4, 1


def _shifted(x, pad_ref):
    T, cb = x.shape
    pad_ref[0:SHIFT_PAD] = jnp.zeros((SHIFT_PAD, cb), f32)
    pad_ref[SHIFT_PAD + T :] = jnp.zeros((SHIFT_PAD, cb), f32)
    pad_ref[SHIFT_PAD : SHIFT_PAD + T] = x
    return {d: pad_ref[SHIFT_PAD + d : SHIFT_PAD + d + T] for d in SHIFTS}


def _conv5(u, ush, cw):
    acc = u * cw[2:3, :]
    for j in (0, 1, 3, 4):
        acc = acc + ush[j - 2] * cw[j : j + 1, :]
    return acc


def _gdn_prep(proj, conv_w):
    T = proj.shape[0]
    hps, W = PREP_HEADS, PREP_HEADS * GDN_DIM
    per = GDN_HEADS // hps

    def body(u_ref, cw_ref, o_ref, pad_ref):
        j = pl.program_id(0)
        for i in range(hps):
            sl = slice(i * GDN_DIM, (i + 1) * GDN_DIM)
            u = _mxr(u_ref[:, sl])
            s = _silu(_conv5(u, _shifted(u, pad_ref), _mxr(cw_ref[:, sl])))
            o_ref[0, :, sl] = jnp.where(j < 2 * per, s * lax.rsqrt(jnp.sum(s * s, axis=1, keepdims=True) + NORM_EPS), s)

    return pl.pallas_call(
        body,
        out_shape=jax.ShapeDtypeStruct((3, T, GDN_HEADS * GDN_DIM), f32),
        grid=(3 * per,),
        in_specs=[pl.BlockSpec((T, W), lambda j: (0, j)), pl.BlockSpec((CONV_K, W), lambda j: (0, j))],
        out_specs=pl.BlockSpec((1, T, W), lambda j: (j // per, 0, j % per)),
        scratch_shapes=[pltpu.VMEM((T + 2 * SHIFT_PAD, GDN_DIM), f32)],
        compiler_params=_cparams(("parallel",)),
        name="gdn_prep",
    )(proj, conv_w)


def _gdn_prep_bwd(proj, conv_w, dqkv_f, dqkv_b, dproj):
    T = proj.shape[0]
    hps, W = PREP_HEADS_BWD, PREP_HEADS_BWD * GDN_DIM
    per = GDN_HEADS // hps

    def body(u_ref, cw_ref, df_ref, db_ref, _, du_ref, dcw_ref, upad_ref, dcpad_ref):
        j = pl.program_id(0)
        for i in range(hps):
            sl = slice(i * GDN_DIM, (i + 1) * GDN_DIM)
            u, cw = _mxr(u_ref[:, sl]), _mxr(cw_ref[:, sl])
            ush = _shifted(u, upad_ref)
            c = _conv5(u, ush, cw)
            s = _silu(c)
            dn = df_ref[0, :, sl] + db_ref[0, :, sl]
            rinv = lax.rsqrt(jnp.sum(s * s, axis=1, keepdims=True) + NORM_EPS)
            ds_norm = rinv * dn - s * (rinv * rinv * rinv) * jnp.sum(dn * s, axis=1, keepdims=True)
            ds = jnp.where(j < 2 * per, ds_norm, dn)
            dc = _mxr(ds * _silu_grad(c))
            dcsh = _shifted(dc, dcpad_ref)
            du = dc * cw[2:3, :]
            for jj in range(CONV_K):
                d = jj - 2
                dcw_ref[jj : jj + 1, sl] = jnp.sum(dc * (u if d == 0 else ush[d]), axis=0, keepdims=True)
                if d != 0:
                    du = du + dcsh[-d] * cw[jj : jj + 1, :]
            du_ref[:, sl] = du.astype(du_ref.dtype)

    qspec = pl.BlockSpec((1, T, W), lambda j: (j // per, 0, j % per))
    return pl.pallas_call(
        body,
        out_shape=(jax.ShapeDtypeStruct(dproj.shape, dproj.dtype), jax.ShapeDtypeStruct((CONV_K, 3 * GDN_HEADS * GDN_DIM), f32)),
        grid=(3 * per,),
        in_specs=[pl.BlockSpec((T, W), lambda j: (0, j)), pl.BlockSpec((CONV_K, W), lambda j: (0, j)), qspec, qspec, _ANY_SPEC],
        out_specs=(pl.BlockSpec((T, W), lambda j: (0, j)), pl.BlockSpec((CONV_K, W), lambda j: (0, j))),
        input_output_aliases={4: 0},
        scratch_shapes=[pltpu.VMEM((T + 2 * SHIFT_PAD, GDN_DIM), f32)] * 2,
        compiler_params=_cparams(("parallel",)),
        name="gdn_prep_bwd",
    )(proj, conv_w, dqkv_f, dqkv_b, dproj)


def _chunk_tri(n, rev):
    i, j = _iota2((n, n), 0), _iota2((n, n), 1)
    same = jnp.right_shift(i, 6) == jnp.right_shift(j, 6)
    order = (j >= i) if rev else (j <= i)
    return jnp.where(jnp.logical_and(same, order), 1.0, 0.0).astype(f32)


def _gate_prep(proj, avec, dvec, w2f, w2b, b2f, b2b, tb=256):
    T = proj.shape[0]
    W = GLA_HEADS * GLA_DK

    def body(s_ref, a_ref, d_ref, wf_ref, wb_ref, bf_ref, bb_ref, g_ref, gc_ref, gf_ref, gb_ref):
        small = s_ref[...]
        tri_f, tri_r = _chunk_tri(tb, False), _chunk_tri(tb, True)
        lane = _iota2(small.shape, 1)
        lg = -jnp.exp(a_ref[...]) * _softplus(small + d_ref[...])
        beta = _sigmoid(small)
        g_ref[...] = jnp.where(lane < 16, lg, jnp.where(lane < 32, beta, 0.0))
        lgm = jnp.where(lane < 16, lg, 0.0)
        gc_ref[...] = jnp.where(lane < 8, _exact01(tri_f, lgm), _exact01(tri_r, lgm))
        gkf = -_softplus(-(_mx(small, wf_ref[...], "nn") + bf_ref[...])) * (1.0 / GATE_NORMALIZER)
        gkb = -_softplus(-(_mx(small, wb_ref[...], "nn") + bb_ref[...])) * (1.0 / GATE_NORMALIZER)
        gf_ref[...] = _exact01(tri_f, gkf)
        gb_ref[...] = _exact01(tri_r, gkb)

    blk = pl.BlockSpec((tb, LANES), lambda i: (i, 0))
    row = pl.BlockSpec((1, LANES), lambda i: (0, 0))
    wide = pl.BlockSpec((tb, W), lambda i: (i, 0))
    wsp = pl.BlockSpec((LANES, W), lambda i: (0, 0))
    wrow = pl.BlockSpec((1, W), lambda i: (0, 0))
    return pl.pallas_call(
        body,
        out_shape=(jax.ShapeDtypeStruct((T, LANES), f32), jax.ShapeDtypeStruct((T, LANES), f32),
                   jax.ShapeDtypeStruct((T, W), f32), jax.ShapeDtypeStruct((T, W), f32)),
        grid=(T // tb,),
        in_specs=[pl.BlockSpec((tb, LANES), lambda i: (i, SMALL_OFF // LANES)), row, row, wsp, wsp, wrow, wrow],
        out_specs=(blk, blk, wide, wide),
        compiler_params=_cparams(("parallel",)),
        name="gate_prep",
    )(proj, avec, dvec, w2f, w2b, b2f, b2b)


def _gdn_gates_bwd(proj, avec, dvec, dg_f, dg_b, dsmall_gla, dproj):
    T = proj.shape[0]
    tb = min(ROW_TILE, T)
    pad = N_CAT - SMALL_OFF

    def body(s_ref, a_ref, d_ref, gf_ref, gb_ref, dl_ref, _, ds_ref, da_ref, dd_ref):
        i = pl.program_id(0)
        small = s_ref[...]
        lane = _iota2(small.shape, 1)
        dgate = gf_ref[...] + gb_ref[...]
        z = small + d_ref[...]
        nega = -jnp.exp(a_ref[...])
        dz = dgate * nega * _sigmoid(z)
        beta = _sigmoid(small)
        dsm = jnp.where(lane < 16, dz, jnp.where(lane < 32, dgate * beta * (1.0 - beta), 0.0))
        ds_ref[:, :LANES] = (dsm + dl_ref[...]).astype(ds_ref.dtype)
        ds_ref[:, LANES:] = jnp.zeros((tb, pad - LANES), ds_ref.dtype)
        lg = nega * _softplus(z)
        pa = jnp.sum(jnp.where(lane < 16, dgate * lg, 0.0), axis=0, keepdims=True)
        pd = jnp.sum(jnp.where(lane < 16, dz, 0.0), axis=0, keepdims=True)

        @pl.when(i == 0)
        def _():
            da_ref[...] = pa
            dd_ref[...] = pd

        @pl.when(i > 0)
        def _():
            da_ref[...] += pa
            dd_ref[...] += pd

    blk = pl.BlockSpec((tb, LANES), lambda i: (i, 0))
    row = pl.BlockSpec((1, LANES), lambda i: (0, 0))
    return pl.pallas_call(
        body,
        out_shape=(jax.ShapeDtypeStruct(dproj.shape, dproj.dtype), jax.ShapeDtypeStruct((1, LANES), f32), jax.ShapeDtypeStruct((1, LANES), f32)),
        grid=(T // tb,),
        in_specs=[pl.BlockSpec((tb, LANES), lambda i: (i, SMALL_OFF // LANES)), row, row, blk, blk, blk, _ANY_SPEC],
        out_specs=(pl.BlockSpec((tb, pad), lambda i: (i, DP_SMALL_BLK)), row, row),
        input_output_aliases={6: 0},
        compiler_params=_cparams(("arbitrary",)),
        name="gdn_gates_bwd",
    )(proj, avec, dvec, dg_f, dg_b, dsmall_gla, dproj)


GDN_HB = 8


def _inv_unit_lower(L, eye):
    A = eye - L
    P2 = _split2(L)
    for _ in range(3):
        P2 = _split2(_h3s(P2, P2, "nn"))
        A = A + _h3s(_split2(A), P2, "nn")
    Ph = P2[0]
    for _ in range(2):
        Ph = _dot16(Ph, Ph, "nn").astype(jnp.bfloat16)
        A = A + _dot16(A.astype(jnp.bfloat16), Ph, "nn")
    return A


def _lane_col(blk, idx):
    lane = _iota2(blk.shape, 1)
    return jnp.sum(jnp.where(lane == idx, blk, 0.0), axis=1, keepdims=True)


def _out_norm(o_f, o_b, proj, gate_blk, w, hd, name):
    T, W = o_f.shape
    tb = min(ROW_TILE, T)
    nh = W // hd

    def body(of_ref, ob_ref, z_ref, w_ref, y_ref):
        wv = w_ref[...]
        for i in range(nh):
            sl = slice(i * hd, (i + 1) * hd)
            o = of_ref[:, sl] + ob_ref[:, sl]
            r = lax.rsqrt(jnp.mean(o * o, axis=1, keepdims=True) + NORM_EPS)
            y_ref[:, sl] = (o * r * wv * _silu(z_ref[:, sl])).astype(y_ref.dtype)

    blk = pl.BlockSpec((tb, W), lambda i: (i, 0))
    return pl.pallas_call(
        body,
        out_shape=jax.ShapeDtypeStruct((T, W), MXU_DTYPE),
        grid=(T // tb,),
        in_specs=[blk, blk, pl.BlockSpec((tb, W), lambda i: (i, gate_blk)), pl.BlockSpec((1, hd), lambda i: (0, 0))],
        out_specs=blk,
        compiler_params=_cparams(("parallel",)),
        name=name,
    )(o_f, o_b, proj, w)


def _out_norm_bwd(dy, o_f, o_b, proj, gate_blk, w, hd, name, dproj):
    T, W = o_f.shape
    tb = min(ROW_TILE, T)
    nh = W // hd

    def body(dy_ref, of_ref, ob_ref, z_ref, w_ref, _, do_ref, dz_ref, dw_ref):
        i = pl.program_id(0)
        wv = w_ref[...]
        dw = jnp.zeros((1, hd), f32)
        for a in range(nh):
            sl = slice(a * hd, (a + 1) * hd)
            o = of_ref[:, sl] + ob_ref[:, sl]
            z = z_ref[:, sl]
            dyv = dy_ref[:, sl]
            r = lax.rsqrt(jnp.mean(o * o, axis=1, keepdims=True) + NORM_EPS)
            orr = o * r
            dn = dyv * _silu(z)
            dz_ref[:, sl] = (dyv * orr * wv * _silu_grad(z)).astype(dz_ref.dtype)
            dw = dw + jnp.sum(dn * orr, axis=0, keepdims=True)
            dnw = dn * wv
            do_ref[:, sl] = r * dnw - o * (r * r * r) * jnp.mean(dnw * o, axis=1, keepdims=True)

        @pl.when(i == 0)
        def _():
            dw_ref[...] = dw

        @pl.when(i > 0)
        def _():
            dw_ref[...] += dw

    blk = pl.BlockSpec((tb, W), lambda i: (i, 0))
    row = pl.BlockSpec((1, hd), lambda i: (0, 0))
    return pl.pallas_call(
        body,
        out_shape=(jax.ShapeDtypeStruct((T, W), f32), jax.ShapeDtypeStruct(dproj.shape, dproj.dtype), jax.ShapeDtypeStruct((1, hd), f32)),
        grid=(T // tb,),
        in_specs=[blk, blk, blk, pl.BlockSpec((tb, W), lambda i: (i, gate_blk)), row, _ANY_SPEC],
        out_specs=(blk, pl.BlockSpec((tb, W), lambda i: (i, gate_blk)), row),
        input_output_aliases={5: 1},
        compiler_params=_cparams(("arbitrary",)),
        name=name,
    )(dy, o_f, o_b, proj, w, dproj)


def _gla_prep_bwd(proj, w2f, w2b, b2f, b2b, dgk_f, dgk_b, dqkv, dproj):
    T = proj.shape[0]
    tb = min(ROW_TILE, T)
    W = GLA_HEADS * GLA_DK
    WV = GLA_HEADS * GLA_DV

    def body(s_ref, wf_ref, wb_ref, bf_ref, bb_ref, df_ref, db_ref, qf, qb, kf, kb, vf, vb, _,
             dp_ref, ds_ref, dwf_ref, dwb_ref, dbf_ref, dbb_ref):
        i = pl.program_id(0)
        dp_ref[:, :W] = (qf[...] + qb[...]).astype(dp_ref.dtype)
        dp_ref[:, W : 2 * W] = (kf[...] + kb[...]).astype(dp_ref.dtype)
        dp_ref[:, 2 * W :] = (vf[...] + vb[...]).astype(dp_ref.dtype)
        small = s_ref[...]
        dsm = jnp.zeros((tb, LANES), f32)
        parts = []
        for w_ref, b_ref, d_ref in ((wf_ref, bf_ref, df_ref), (wb_ref, bb_ref, db_ref)):
            pre = _mx(small, w_ref[...], "nn") + b_ref[...]
            dpre = d_ref[...] * (1.0 / GATE_NORMALIZER) * _sigmoid(-pre)
            dsm = dsm + _mx(dpre, w_ref[...], "nt")
            parts.append((_mx(small, dpre, "tn"), jnp.sum(dpre, axis=0, keepdims=True)))
        ds_ref[...] = dsm

        @pl.when(i == 0)
        def _():
            dwf_ref[...], dbf_ref[...] = parts[0]
            dwb_ref[...], dbb_ref[...] = parts[1]

        @pl.when(i > 0)
        def _():
            dwf_ref[...] += parts[0][0]
            dbf_ref[...] += parts[0][1]
            dwb_ref[...] += parts[1][0]
            dbb_ref[...] += parts[1][1]

    blk = pl.BlockSpec((tb, W), lambda i: (i, 0))
    vblk = pl.BlockSpec((tb, WV), lambda i: (i, 0))
    wsp = pl.BlockSpec((LANES, W), lambda i: (0, 0))
    row = pl.BlockSpec((1, W), lambda i: (0, 0))
    return pl.pallas_call(
        body,
        out_shape=(jax.ShapeDtypeStruct(dproj.shape, dproj.dtype), jax.ShapeDtypeStruct((T, LANES), f32), jax.ShapeDtypeStruct((LANES, W), f32),
                   jax.ShapeDtypeStruct((LANES, W), f32), jax.ShapeDtypeStruct((1, W), f32), jax.ShapeDtypeStruct((1, W), f32)),
        grid=(T // tb,),
        in_specs=[pl.BlockSpec((tb, LANES), lambda i: (i, SMALL_OFF // LANES)), wsp, wsp, row, row, blk, blk,
                  blk, blk, blk, blk, vblk, vblk, _ANY_SPEC],
        out_specs=(pl.BlockSpec((tb, 2 * W + WV), lambda i: (i, DP_GLA_QKV_BLK)), pl.BlockSpec((tb, LANES), lambda i: (i, 0)), wsp, wsp, row, row),
        input_output_aliases={13: 0},
        compiler_params=_cparams(("arbitrary",)),
        name="gla_prep_bwd",
    )(proj, w2f, w2b, b2f, b2b, dgk_f, dgk_b, *dqkv, dproj)


GATE_A_BLK, GATE_B_BLK, Z_BLK, G_B_BLK = 6144 // 1024, 7168 // 1024, 3072 // 1024, 8192 // 1024
DP_GATES_BLK, DP_GLA_QKV_BLK, DP_SMALL_BLK = 6144 // 2048, 4096 // 2048, SMALL_OFF // 512
_ANY_SPEC = pl.BlockSpec(memory_space=pl.ANY)


def _mid(og, obg, proj, x, target, wpg, wpl, wout, w_post, tb=256):
    T = x.shape[0]
    Dm = D_MODEL

    def body(og_ref, obg_ref, ga_ref, gb_ref, x_ref, t_ref, wpg_ref, wpl_ref, wo_ref, wp_ref,
             loss_ref, dwp_ref, dy_ref, m_ref, dout_ref, dya_ref, dyb_ref, dgates_ref, dog_ref, dobg_ref):
        i = pl.program_id(0)
        ya = _mx(og_ref[...], wpg_ref[...], "nn")
        yb = _mx(obg_ref[...], wpl_ref[...], "nn")
        sa, sb = _sigmoid(ga_ref[...]), _sigmoid(gb_ref[...])
        m = sa * ya + sb * yb
        m_ref[...] = m.astype(m_ref.dtype)
        out = _mx(m, wo_ref[...], "nn")
        r = lax.rsqrt(jnp.mean(out * out, axis=1, keepdims=True) + NORM_EPS)
        wp = wp_ref[...]
        e = x_ref[...] + out * r * wp - t_ref[...]
        dy = e * (1.0 / Dm)
        dy_ref[...] = dy
        lpart = jnp.sum(e * e, axis=0, keepdims=True)
        wpart = jnp.sum(dy * out * r, axis=0, keepdims=True)
        dyn = dy * wp
        dout = r * dyn - out * (r * r * r) * jnp.mean(dyn * out, axis=1, keepdims=True)
        dout_ref[...] = dout.astype(dout_ref.dtype)
        dm = _mx(dout, wo_ref[...], "nt")
        dya, dyb = dm * sa, dm * sb
        dya_ref[...] = dya.astype(dya_ref.dtype)
        dyb_ref[...] = dyb.astype(dyb_ref.dtype)
        dgates_ref[:, :Dm] = (dm * ya * sa * (1.0 - sa)).astype(dgates_ref.dtype)
        dgates_ref[:, Dm:] = (dm * yb * sb * (1.0 - sb)).astype(dgates_ref.dtype)
        dog_ref[...] = _mx(dya, wpg_ref[...], "nt")
        dobg_ref[...] = _mx(dyb, wpl_ref[...], "nt")

        @pl.when(i == 0)
        def _():
            loss_ref[...] = lpart
            dwp_ref[...] = wpart

        @pl.when(i > 0)
        def _():
            loss_ref[...] += lpart
            dwp_ref[...] += wpart

    blk = pl.BlockSpec((tb, Dm), lambda i: (i, 0))
    row = pl.BlockSpec((1, Dm), lambda i: (0, 0))
    wsp = pl.BlockSpec((Dm, Dm), lambda i: (0, 0))
    act = jax.ShapeDtypeStruct((T, Dm), MXU_DTYPE)
    big = jax.ShapeDtypeStruct((T, Dm), f32)
    vec = jax.ShapeDtypeStruct((1, Dm), f32)
    return pl.pallas_call(
        body,
        out_shape=(vec, vec, big, act, act, act, act, jax.ShapeDtypeStruct((T, N_CAT), MXU_DTYPE), big, big),
        grid=(T // tb,),
        in_specs=[blk, blk, pl.BlockSpec((tb, Dm), lambda i: (i, GATE_A_BLK)), pl.BlockSpec((tb, Dm), lambda i: (i, GATE_B_BLK)),
                  blk, blk, wsp, wsp, wsp, row],
        out_specs=(row, row) + (blk,) * 5 + (pl.BlockSpec((tb, 2 * Dm), lambda i: (i, DP_GATES_BLK)), blk, blk),
        compiler_params=_cparams(("arbitrary",)),
        name="mid",
    )(og, obg, proj, proj, x, target, wpg, wpl, wout, w_post)


def _heads(ref, lead, n, width, rows=slice(None)):
    return [ref[lead + (rows, slice(i * width, (i + 1) * width))] for i in range(n)]


class _GdnBatch:
    def __init__(self, q, k, v, gcol, grow, beta, nf, saved=None, sol=None):
        B, Cn = q.shape[0], q.shape[1]
        self.incl, self.strict, eye = _dir_masks(B, nf, Cn)
        A = None
        if saved is None:
            self.decay = jnp.where(self.incl, jnp.exp(jnp.where(self.incl, gcol - grow, 0.0)), 0.0)
        else:
            A, self.L, self.attn, self.decay = (saved[:, i * Cn : (i + 1) * Cn] for i in range(4))
        self.k, self.v = k, v
        gl = _last_row(gcol, nf)
        self.beta = jnp.broadcast_to(beta, q.shape)
        self.eg = jnp.broadcast_to(jnp.exp(gcol), q.shape)
        self.ekd = jnp.broadcast_to(jnp.exp(gl - gcol), q.shape)
        self.qs = q * (GDN_DIM**-0.5)
        self.kb = k * self.beta
        if saved is None:
            self.L = jnp.where(self.strict, _mx(self.kb, k, "nt") * self.decay, 0.0)
            self.attn = jnp.where(self.incl, _mx(self.qs, k, "nt") * self.decay, 0.0)
            A = _inv_unit_lower(self.L, jnp.where(eye, 1.0, 0.0).astype(f32))
        self.A2 = _split2(A)
        self.saved = jnp.concatenate([A, self.L, self.attn, self.decay], axis=1) if saved is None else saved
        self.kbg = self.kb * self.eg
        if sol is None:
            sol = _h3s(self.A2, _split2(jnp.concatenate([v * self.beta, self.kbg], axis=2)), "nn")
        self.sol = sol
        self.u = self.sol[:, :, :GDN_DIM]
        self.w = self.sol[:, :, GDN_DIM:]
        self.qd = self.qs * self.eg
        self.kd = k * self.ekd
        self.egl = jnp.exp(gl)


def _gdn_chain_inputs(qf_ref, qb_ref, gf_ref, gb_ref, cf_ref, cb_ref, b, hb, need_rows=True):
    qkv = [jnp.stack(_heads(qf_ref, (i,), hb, GDN_DIM) + _heads(qb_ref, (i,), hb, GDN_DIM)) for i in range(3)]
    cols, rows, betas = [], [], []
    for d, (g_ref, c_ref) in enumerate(((gf_ref, cf_ref), (gb_ref, cb_ref))):
        gates, gc = g_ref[...], c_ref[...]
        for hh in range(hb):
            h = b * hb + hh
            col = _lane_col(gc, h + 8 * d)
            cols.append(col)
            if need_rows:
                rows.append(jnp.broadcast_to(col, (CHUNK, LANES)).T[0:1, :])
            betas.append(_lane_col(gates, 16 + h + 8 * d))
    return qkv[0], qkv[1], qkv[2], jnp.stack(cols), (jnp.stack(rows) if need_rows else None), jnp.stack(betas)


GDN_CPS_FWD, GDN_CPS_BWD = 2, 1


def _chunk_views(refs, s_f, s_b):
    out = []
    for i, r in enumerate(refs):
        s = s_f if i % 2 == 0 else s_b
        rows = pl.ds(s * CHUNK, CHUNK)
        out.append(r.at[pl.ds(s, 1)] if len(r.shape) == 4 else (r.at[:, rows, :] if len(r.shape) == 3 else r.at[rows, :]))
    return out


def _gdn_specs(cps, hb):
    R, W = cps * CHUNK, hb * GDN_DIM
    qsp = lambda ci: pl.BlockSpec((3, R, W), lambda n, b: (0, ci(n), b))
    gsp = lambda ci: pl.BlockSpec((R, LANES), lambda n, b: (ci(n), 0))
    osp = lambda ci: pl.BlockSpec((R, W), lambda n, b: (ci(n), b))
    ssp = lambda ci: pl.BlockSpec((cps, hb, GDN_DIM, GDN_DIM), lambda n, b: (ci(n), b, 0, 0))
    asp = lambda ci: pl.BlockSpec((cps, hb, 4 * CHUNK, CHUNK), lambda n, b: (ci(n), b, 0, 0))
    usp = lambda ci: pl.BlockSpec((cps, hb, CHUNK, 2 * GDN_DIM), lambda n, b: (ci(n), b, 0, 0))
    return qsp, gsp, osp, ssp, asp, usp


def _gdn_fwd(qkvn, gates, gc):
    T = qkvn.shape[1]
    N = T // CHUNK
    nh, hb = GDN_HEADS, GDN_HB
    cps = min(GDN_CPS_FWD, N)
    NB = N // cps

    def body(*refs):
        n, b = pl.program_id(0), pl.program_id(1)
        s_scr = refs[-1]

        @pl.when(n == 0)
        def _():
            s_scr[0, pl.ds(b * hb, hb)] = jnp.zeros((hb, GDN_DIM, GDN_DIM), f32)
            s_scr[1, pl.ds(b * hb, hb)] = jnp.zeros((hb, GDN_DIM, GDN_DIM), f32)

        for s in range(cps):
            chunk(*_chunk_views(refs[:-1], s, cps - 1 - s), s_scr, b)

    def chunk(qf_ref, qb_ref, gf_ref, gb_ref, cf_ref, cb_ref, of_ref, ob_ref, sf_ref, sb_ref, af_ref, ab_ref, uf_ref, ub_ref, s_scr, b):
        hs = pl.ds(b * hb, hb)
        q, k, v, gcol, grow, beta = _gdn_chain_inputs(qf_ref, qb_ref, gf_ref, gb_ref, cf_ref, cb_ref, b, hb)
        ck = _GdnBatch(q, k, v, gcol, grow, beta, hb)
        S = jnp.concatenate([s_scr[0, hs], s_scr[1, hs]], axis=0)
        sf_ref[0], sb_ref[0] = S[:hb], S[hb:]
        af_ref[0], ab_ref[0] = ck.saved[:hb], ck.saved[hb:]
        uf_ref[0], ub_ref[0] = ck.sol[:hb], ck.sol[hb:]
        v_new = ck.u - _mx(ck.w, S, "nn")
        o = _mx(ck.qd, S, "nn") + _mx(ck.attn, v_new, "nn")
        S_new = S * ck.egl + _mx(ck.kd, v_new, "tn")
        for hh in range(hb):
            sl = slice(hh * GDN_DIM, (hh + 1) * GDN_DIM)
            of_ref[:, sl] = o[hh]
            ob_ref[:, sl] = o[hb + hh]
        s_scr[0, hs] = S_new[:hb]
        s_scr[1, hs] = S_new[hb:]

    fw, bw = (lambda n: n), (lambda n: NB - 1 - n)
    qsp, gsp, osp, ssp, asp, usp = _gdn_specs(cps, hb)
    o_shape = jax.ShapeDtypeStruct((T, nh * GDN_DIM), f32)
    s_shape = jax.ShapeDtypeStruct((N, nh, GDN_DIM, GDN_DIM), f32)
    a_shape = jax.ShapeDtypeStruct((N, nh, 4 * CHUNK, CHUNK), f32)
    u_shape = jax.ShapeDtypeStruct((N, nh, CHUNK, 2 * GDN_DIM), f32)
    return pl.pallas_call(
        body,
        out_shape=(o_shape, o_shape, s_shape, s_shape, a_shape, a_shape, u_shape, u_shape),
        grid=(NB, nh // hb),
        in_specs=[qsp(fw), qsp(bw), gsp(fw), gsp(bw), gsp(fw), gsp(bw)],
        out_specs=(osp(fw), osp(bw), ssp(fw), ssp(bw), asp(fw), asp(bw), usp(fw), usp(bw)),
        scratch_shapes=[pltpu.VMEM((2, nh, GDN_DIM, GDN_DIM), f32)],
        compiler_params=_cparams(("arbitrary", "arbitrary")),
        name="gdn_scan_fwd",
    )(qkvn, qkvn, gates, gates, gc, gc)


def _gdn_bwd(qkvn, gates, gc, st_f, st_b, a_f, a_b, u_f, u_b, do):
    T = qkvn.shape[1]
    N = T // CHUNK
    nh, hb = GDN_HEADS, GDN_HB
    nb = nh // hb
    cps = min(GDN_CPS_BWD, N)
    NB = N // cps

    def body(*refs):
        n, b = pl.program_id(0), pl.program_id(1)
        ds_scr = refs[-1]

        @pl.when(n == 0)
        def _():
            ds_scr[0, pl.ds(b * hb, hb)] = jnp.zeros((hb, GDN_DIM, GDN_DIM), f32)
            ds_scr[1, pl.ds(b * hb, hb)] = jnp.zeros((hb, GDN_DIM, GDN_DIM), f32)

        for s in range(cps):
            chunk(*_chunk_views(refs[:-1], cps - 1 - s, s), ds_scr, b)

    def chunk(qf_ref, qb_ref, gf_ref, gb_ref, cf_ref, cb_ref, sf_ref, sb_ref, af_ref, ab_ref, uf_ref, ub_ref, dof_ref, dob_ref,
              dqf_ref, dqb_ref, dgf_ref, dgb_ref, ds_scr, b):
        hs = pl.ds(b * hb, hb)

        @pl.when(b == 0)
        def _():
            dgf_ref[...] = jnp.zeros((CHUNK, LANES), f32)
            dgb_ref[...] = jnp.zeros((CHUNK, LANES), f32)

        q, k, v, gcol, grow, beta = _gdn_chain_inputs(qf_ref, qb_ref, gf_ref, gb_ref, cf_ref, cb_ref, b, hb, need_rows=False)
        ck = _GdnBatch(q, k, v, gcol, grow, beta, hb, saved=jnp.concatenate([af_ref[0], ab_ref[0]], axis=0),
                       sol=jnp.concatenate([uf_ref[0], ub_ref[0]], axis=0))
        S = jnp.concatenate([sf_ref[0], sb_ref[0]], axis=0)
        dS = jnp.concatenate([ds_scr[0, hs], ds_scr[1, hs]], axis=0)
        dov = jnp.stack(_heads(dof_ref, (), hb, GDN_DIM) + _heads(dob_ref, (), hb, GDN_DIM))
        v_new = ck.u - _mx(ck.w, S, "nn")
        d_vnew = _mx(ck.attn, dov, "tn") + _mx(ck.kd, dS, "nn")
        d_attn = jnp.where(ck.incl, _mx(dov, v_new, "nt"), 0.0)
        d_qd = _mx(dov, S, "nt")
        d_kd = _mx(v_new, dS, "nt")
        d_gl = jnp.sum(jnp.sum(dS * S, axis=2, keepdims=True), axis=1, keepdims=True) * ck.egl
        dS_new = dS * ck.egl + _mx(ck.qd, dov, "tn") - _mx(ck.w, d_vnew, "tn")
        ds_scr[0, hs] = dS_new[:hb]
        ds_scr[1, hs] = dS_new[hb:]
        d_w = -_mx(d_vnew, S, "nt")
        d_rhs = _h3s(ck.A2, _split2(jnp.concatenate([d_vnew, d_w], axis=2)), "tn")
        d_vb, d_kbg = d_rhs[:, :, :GDN_DIM], d_rhs[:, :, GDN_DIM:]
        dL = -jnp.where(ck.strict, _h3(d_rhs, ck.sol, "nt"), 0.0)
        P = dL * ck.decay
        Q = d_attn * ck.decay
        d_kb = _mx(P, k, "nn") + d_kbg * ck.eg
        d_qs = _mx(Q, k, "nn") + d_qd * ck.eg
        d_k = _mx(P, ck.kb, "tn") + _mx(Q, ck.qs, "tn") + d_kd * ck.ekd + d_kb * ck.beta
        E = dL * ck.L + d_attn * ck.attn
        kdsum = jnp.sum(d_kd * ck.kd, axis=2, keepdims=True)
        d_gc = (jnp.sum(E, axis=2, keepdims=True) - _colsum_col(E)
                + jnp.sum(d_kbg * ck.kbg, axis=2, keepdims=True)
                + jnp.sum(d_qd * ck.qd, axis=2, keepdims=True) - kdsum)
        d_gc = _add_at_last_row(d_gc, d_gl + jnp.sum(kdsum, axis=1, keepdims=True), hb)
        d_beta = jnp.sum(d_vb * v, axis=2, keepdims=True) + jnp.sum(d_kb * k, axis=2, keepdims=True)
        d_q = d_qs * (GDN_DIM**-0.5)
        d_v = d_vb * ck.beta
        lane = _iota2((CHUNK, LANES), 1)
        for d, (dq_ref, dg_ref) in enumerate(((dqf_ref, dgf_ref), (dqb_ref, dgb_ref))):
            acc = dg_ref[...]
            for hh in range(hb):
                c, h = d * hb + hh, b * hb + hh
                sl = slice(hh * GDN_DIM, (hh + 1) * GDN_DIM)
                dq_ref[0, :, sl] = d_q[c]
                dq_ref[1, :, sl] = d_k[c]
                dq_ref[2, :, sl] = d_v[c]
                acc = jnp.where(lane == h + 8 * d, d_gc[c], acc)
                acc = jnp.where(lane == 16 + h + 8 * d, d_beta[c], acc)
            dg_ref[...] = acc

        @pl.when(b == nb - 1)
        def _():
            for d, dg_ref in enumerate((dgf_ref, dgb_ref)):
                blk = dg_ref[...]
                dg_ref[...] = jnp.where(lane < 16, _exact01(_chunk_tri(CHUNK, d == 0), blk), blk)

    fw, bw = (lambda n: NB - 1 - n), (lambda n: n)
    qsp, gsp, osp, ssp, asp, usp = _gdn_specs(cps, hb)
    dq_shape = jax.ShapeDtypeStruct((3, T, nh * GDN_DIM), f32)
    dg_shape = jax.ShapeDtypeStruct((T, LANES), f32)
    return pl.pallas_call(
        body,
        out_shape=(dq_shape, dq_shape, dg_shape, dg_shape),
        grid=(NB, nb),
        in_specs=[qsp(fw), qsp(bw), gsp(fw), gsp(bw), gsp(fw), gsp(bw), ssp(fw), ssp(bw), asp(fw), asp(bw), usp(fw), usp(bw),
                  osp(fw), osp(bw)],
        out_specs=(qsp(fw), qsp(bw), gsp(fw), gsp(bw)),
        scratch_shapes=[pltpu.VMEM((2, nh, GDN_DIM, GDN_DIM), f32)],
        compiler_params=_cparams(("arbitrary", "arbitrary")),
        name="gdn_scan_bwd",
    )(qkvn, qkvn, gates, gates, gc, gc, st_f, st_b, a_f, a_b, u_f, u_b, do, do)


class _GlaBatch:
    def __init__(self, q, k, G, nf):
        B, Cn = q.shape[0], q.shape[1]
        self.incl, _, _ = _dir_masks(B, nf, Cn)
        self.eG = jnp.exp(G)
        self.enG = jnp.exp(-G)
        self.qg = q * (GLA_DK**-0.5) * self.eG
        self.kg = k * self.enG
        self.attn = jnp.where(self.incl, _mx(self.qg, self.kg, "nt"), 0.0)
        gl = _last_row(G, nf)
        self.ekd = jnp.exp(gl - G)
        self.kd = k * self.ekd
        self.egl = jnp.exp(gl)


GLA_CPS = 4


def _chunk_rows(s):
    return slice(s * CHUNK, (s + 1) * CHUNK)


def _gla_chain_inputs(refs_f, refs_b, rows_f, rows_b):
    nh = GLA_HEADS
    out = []
    for i, width in enumerate((GLA_DK, GLA_DK, GLA_DV, GLA_DK)):
        out.append(jnp.stack(_heads(refs_f[i], (), nh, width, rows_f) + _heads(refs_b[i], (), nh, width, rows_b)))
    return out


def _gla_specs(rows, ci):
    nh = GLA_HEADS
    return [
        pl.BlockSpec((rows, nh * GLA_DK), lambda n: (ci(n), 4096 // (nh * GLA_DK))),
        pl.BlockSpec((rows, nh * GLA_DK), lambda n: (ci(n), 4608 // (nh * GLA_DK))),
        pl.BlockSpec((rows, nh * GLA_DV), lambda n: (ci(n), 5120 // (nh * GLA_DV))),
        pl.BlockSpec((rows, nh * GLA_DK), lambda n: (ci(n), 0)),
    ]


def _gla_fwd(proj, G_f, G_b):
    T = proj.shape[0]
    N = T // CHUNK
    nh = GLA_HEADS
    cps = min(GLA_CPS, N)
    NB = N // cps

    def body(qf, kf, vf, gf, qb, kb, vb, gb, of_ref, ob_ref, sf_ref, sb_ref, s_scr):
        n = pl.program_id(0)

        @pl.when(n == 0)
        def _():
            s_scr[...] = jnp.zeros((2 * nh, GLA_DV, GLA_DK), f32)

        for s in range(cps):
            sb = cps - 1 - s
            q, k, v, G = _gla_chain_inputs((qf, kf, vf, gf), (qb, kb, vb, gb), _chunk_rows(s), _chunk_rows(sb))
            ck = _GlaBatch(q, k, G, nh)
            St = s_scr[...]
            sf_ref[s], sb_ref[sb] = St[:nh].astype(sf_ref.dtype), St[nh:].astype(sb_ref.dtype)
            o = _mx(ck.qg, St, "nt") + _mx(ck.attn, v, "nn")
            for i in range(nh):
                sl = slice(i * GLA_DV, (i + 1) * GLA_DV)
                of_ref[_chunk_rows(s), sl] = o[i]
                ob_ref[_chunk_rows(sb), sl] = o[nh + i]
            s_scr[...] = St * ck.egl + _mx(v, ck.kd, "tn")

    fw, bw = (lambda n: n), (lambda n: NB - 1 - n)
    osp = lambda ci: pl.BlockSpec((cps * CHUNK, nh * GLA_DV), lambda n: (ci(n), 0))
    ssp = lambda ci: pl.BlockSpec((cps, nh, GLA_DV, GLA_DK), lambda n: (ci(n), 0, 0, 0))
    o_shape = jax.ShapeDtypeStruct((T, nh * GLA_DV), f32)
    s_shape = jax.ShapeDtypeStruct((N, nh, GLA_DV, GLA_DK), MXU_DTYPE)
    return pl.pallas_call(
        body,
        out_shape=(o_shape, o_shape, s_shape, s_shape),
        grid=(NB,),
        in_specs=_gla_specs(cps * CHUNK, fw) + _gla_specs(cps * CHUNK, bw),
        out_specs=(osp(fw), osp(bw), ssp(fw), ssp(bw)),
        scratch_shapes=[pltpu.VMEM((2 * nh, GLA_DV, GLA_DK), f32)],
        compiler_params=_cparams(("arbitrary",)),
        name="gla_scan_fwd",
    )(proj, proj, proj, G_f, proj, proj, proj, G_b)


def _gla_bwd(proj, G_f, G_b, st_f, st_b, do):
    T = proj.shape[0]
    N = T // CHUNK
    nh = GLA_HEADS
    cps = min(GLA_CPS, N)
    NB = N // cps

    def body(qf, kf, vf, gf, sf_ref, dof_ref, qb, kb, vb, gb, sb_ref, dob_ref,
             dqf_ref, dkf_ref, dvf_ref, dgf_ref, dqb_ref, dkb_ref, dvb_ref, dgb_ref, ds_scr):
        n = pl.program_id(0)

        @pl.when(n == 0)
        def _():
            ds_scr[...] = jnp.zeros((2 * nh, GLA_DV, GLA_DK), f32)

        shp = (2 * nh, CHUNK, CHUNK)
        bb, ii, jj = _iota3(shp, 0), _iota3(shp, 1), _iota3(shp, 2)
        tri = jnp.where(jnp.logical_or(jnp.logical_and(bb < nh, jj >= ii), jnp.logical_and(bb >= nh, jj <= ii)), 1.0, 0.0)
        for s in range(cps):
            sf = cps - 1 - s
            rows = (_chunk_rows(sf), _chunk_rows(s))
            q, k, v, G = _gla_chain_inputs((qf, kf, vf, gf), (qb, kb, vb, gb), *rows)
            ck = _GlaBatch(q, k, G, nh)
            St = jnp.concatenate([sf_ref[sf], sb_ref[s]], axis=0).astype(f32)
            dSt = ds_scr[...]
            dov = jnp.stack(_heads(dof_ref, (), nh, GLA_DV, rows[0]) + _heads(dob_ref, (), nh, GLA_DV, rows[1]))
            d_attn = jnp.where(ck.incl, _mx(dov, v, "nt"), 0.0)
            d_qg = _mx(dov, St, "nn") + _mx(d_attn, ck.kg, "nn")
            d_kg = _mx(d_attn, ck.qg, "tn")
            d_v = _mx(ck.attn, dov, "tn") + _mx(ck.kd, dSt, "nt")
            d_kd = _mx(v, dSt, "nn")
            ds_scr[...] = dSt * ck.egl + _mx(dov, ck.qg, "tn")
            kdd = d_kd * ck.kd
            d_gl = jnp.sum(dSt * St, axis=1, keepdims=True) * ck.egl + jnp.sum(kdd, axis=1, keepdims=True)
            d_q = d_qg * ck.eG * (GLA_DK**-0.5)
            d_k = d_kg * ck.enG + d_kd * ck.ekd
            d_gk = _exact01(tri, _add_at_last_row(d_qg * ck.qg - d_kg * ck.kg - kdd, d_gl, nh))
            for d, (dq_ref, dk_ref, dv_ref, dg_ref) in enumerate(((dqf_ref, dkf_ref, dvf_ref, dgf_ref), (dqb_ref, dkb_ref, dvb_ref, dgb_ref))):
                for i in range(nh):
                    c = d * nh + i
                    ks, vs = slice(i * GLA_DK, (i + 1) * GLA_DK), slice(i * GLA_DV, (i + 1) * GLA_DV)
                    dq_ref[rows[d], ks] = d_q[c]
                    dk_ref[rows[d], ks] = d_k[c]
                    dv_ref[rows[d], vs] = d_v[c]
                    dg_ref[rows[d], ks] = d_gk[c]

    R = cps * CHUNK

    def in_specs(ci):
        return _gla_specs(R, ci) + [
            pl.BlockSpec((cps, nh, GLA_DV, GLA_DK), lambda n: (ci(n), 0, 0, 0)),
            pl.BlockSpec((R, nh * GLA_DV), lambda n: (ci(n), 0)),
        ]

    def out_specs(ci):
        ksp = pl.BlockSpec((R, nh * GLA_DK), lambda n: (ci(n), 0))
        return [ksp, ksp, pl.BlockSpec((R, nh * GLA_DV), lambda n: (ci(n), 0)), ksp]

    fw, bw = (lambda n: NB - 1 - n), (lambda n: n)
    k_shape = jax.ShapeDtypeStruct((T, nh * GLA_DK), f32)
    v_shape = jax.ShapeDtypeStruct((T, nh * GLA_DV), f32)
    return pl.pallas_call(
        body,
        out_shape=(k_shape, k_shape, v_shape, k_shape) * 2,
        grid=(NB,),
        in_specs=in_specs(fw) + in_specs(bw),
        out_specs=tuple(out_specs(fw) + out_specs(bw)),
        scratch_shapes=[pltpu.VMEM((2 * nh, GLA_DV, GLA_DK), f32)],
        compiler_params=_cparams(("arbitrary",)),
        name="gla_scan_bwd",
    )(proj, proj, proj, G_f, st_f, do, proj, proj, proj, G_b, st_b, do)


def _local_step(x, target, wcat, conv_w, avec, dvec, gdn_norm_w, w2f, w2b, b2f, b2b, gla_norm_w, proj_weights, w_pre, w_post,
                send_proj_grads=None, send_in_grads=None):
    h, h_t = _rms_pre(x, w_pre)
    proj = _matmul(h, wcat, "nt", f32, "proj_in", 2048, 512, 1024)
    qkvn = _gdn_prep(proj, conv_w)
    gates, gc, G_f, G_b = _gate_prep(proj, avec, dvec, w2f, w2b, b2f, b2b)
    oa_f, oa_b, sa_f, sa_b, inv_f, inv_b, sol_f, sol_b = _gdn_fwd(qkvn, gates, gc)
    og = _out_norm(oa_f, oa_b, proj, Z_BLK, gdn_norm_w, GDN_DIM, "gdn_out")
    ob_f, ob_b, sb_f, sb_b = _gla_fwd(proj, G_f, G_b)
    obg = _out_norm(ob_f, ob_b, proj, G_B_BLK, gla_norm_w, GLA_DV, "gla_out")
    wpg, wpl, wout = proj_weights(obg)
    (loss_row, d_wpost, dy, m, dout, dya, dyb, dproj, dog, dobg) = _mid(og, obg, proj, x, target, wpg, wpl, wout, w_post)

    d_wout = _matmul(m, dout, "tn", MXU_DTYPE, "dw_out", 1024, 512, 2048)
    d_wpg = _matmul(og, dya, "tn", MXU_DTYPE, "dw_proj_gdn", 1024, 512, 2048)
    d_wpl = _matmul(obg, dyb, "tn", MXU_DTYPE, "dw_proj_gla", 1024, 512, 2048)
    if send_proj_grads is not None:
        gla_norm_w = gla_norm_w + send_proj_grads(d_wpg, d_wpl, d_wout)

    do_b, dproj, d_gla_norm = _out_norm_bwd(dobg, ob_f, ob_b, proj, G_B_BLK, gla_norm_w, GLA_DV, "gla_out_bwd", dproj)
    dq_f, dk_f, dv_f, dgk_f, dq_b, dk_b, dv_b, dgk_b = _gla_bwd(proj, G_f, G_b, sb_f, sb_b, do_b)
    dproj, dsmall_gla, d_w2f, d_w2b, d_b2f, d_b2b = _gla_prep_bwd(proj, w2f, w2b, b2f, b2b, dgk_f, dgk_b,
                                                                 (dq_f, dq_b, dk_f, dk_b, dv_f, dv_b), dproj)

    do_a, dproj, d_gdn_norm = _out_norm_bwd(dog, oa_f, oa_b, proj, Z_BLK, gdn_norm_w, GDN_DIM, "gdn_out_bwd", dproj)
    dqkv_f, dqkv_b, dg_f, dg_b = _gdn_bwd(qkvn, gates, gc, sa_f, sa_b, inv_f, inv_b, sol_f, sol_b, do_a)
    dproj, d_conv = _gdn_prep_bwd(proj, conv_w, dqkv_f, dqkv_b, dproj)
    dproj, d_alog, d_dtb = _gdn_gates_bwd(proj, avec, dvec, dg_f, dg_b, dsmall_gla, dproj)

    d_wcat = _matmul(h_t, dproj, "nn", MXU_DTYPE, "dw_in", 1024, 512, 2048, out_t=True)
    started = None if send_in_grads is None else send_in_grads(d_wcat)
    dh = _matmul(dproj, wcat, "nn", f32, "dh", 512, 1024, N_CAT, after=started)
    grad_x, d_wpre = _rms_pre_bwd(x, w_pre, dh, dy)
    return dict(loss_row=loss_row, grad_x=grad_x, d_wcat=d_wcat, d_conv=d_conv, d_alog=d_alog, d_dtb=d_dtb,
                d_gdn_norm=d_gdn_norm, d_wpg=d_wpg, d_w2f=d_w2f, d_w2b=d_w2b, d_b2f=d_b2f, d_b2b=d_b2b,
                d_gla_norm=d_gla_norm, d_wpl=d_wpl, d_wout=d_wout, d_wpost=d_wpost, d_wpre=d_wpre)


def _to_cat(wf):
    pad = jnp.zeros((N_CAT - N_IN,) + wf.shape[1:], wf.dtype)
    return jnp.concatenate([wf[:4096], wf[4128:6176], wf[7232:9280], wf[6176:7200], wf[4096:4128], wf[7200:7232], pad], axis=0)


def _from_cat(wc):
    return jnp.concatenate([wc[:4096], wc[9216:9248], wc[4096:6144], wc[8192:9216], wc[9248:9280], wc[6144:8192]], axis=0)


def _lane_row(*pieces):
    row = jnp.concatenate(pieces, axis=1)
    return jnp.pad(row, ((0, 0), (0, LANES - row.shape[1])))


def _pad_w2(w2, row0):
    return jnp.pad(w2, ((row0, LANES - row0 - w2.shape[0]), (0, 0)))


SM_ROWS, SM_COLS = 32, 384


def _pack_small_shard(conv, w2f, w2b):
    lead = conv.shape[:-2]
    z = lambda r, c: jnp.zeros(lead + (r, c), f32)
    top = jnp.concatenate([conv, z(8 - CONV_K, SM_COLS)], axis=-2)
    mid = jnp.concatenate([w2f, w2b, z(16, SM_COLS - 128)], axis=-1)
    return jnp.concatenate([top, mid, z(SM_ROWS - 24, SM_COLS)], axis=-2)


SMALL_REGION = {
    "ln_pre_w": (0, 1, 0, 1024), "ln_post_w": (1, 1, 0, 1024), "gk_b2_fwd": (2, 1, 0, 512), "gk_b2_bwd": (2, 1, 512, 512),
    "gla_norm_w": (3, 1, 0, 256), "gdn_norm_w": (3, 1, 256, 128), "a_log_fwd": (4, 1, 0, 8), "a_log_bwd": (5, 1, 0, 8),
    "dt_bias_fwd": (6, 1, 0, 8), "dt_bias_bwd": (7, 1, 0, 8), "conv_w": (8, CONV_K, 0, SM_COLS),
    "gk_w2_fwd": (16, 16, 0, 64), "gk_w2_bwd": (16, 16, 128, 64),
}
SMALL_NAMES = tuple(SMALL_REGION)
SMALL_SHARDED = ("conv_w", "gk_w2_fwd", "gk_w2_bwd")
PACK_ROWS, PACK_COLS = 32, 1024
LOSS_AT = (13, 0)


def _region(name):
    r0, nr, l0, nl = SMALL_REGION[name]
    return slice(r0, r0 + nr), slice(l0, l0 + nl)


def _pack_small_grads(G):
    names = ("d_wpre", "d_wpost", "d_b2f", "d_b2b", "d_gla_norm", "d_gdn_norm", "d_alog", "d_dtb", "loss_row", "d_conv", "d_w2f", "d_w2b")

    def body(wpre, wpost, b2f, b2b, glan, gdnn, alog, dtb, lrow, conv, w2f, w2b, o_ref):
        o_ref[...] = jnp.zeros(o_ref.shape, f32)
        loss = 0.5 * jnp.sum(lrow[...], axis=1, keepdims=True) * (1.0 / D_MODEL)
        whole = {"ln_pre_w": wpre, "ln_post_w": wpost, "gk_b2_fwd": b2f, "gk_b2_bwd": b2b, "gla_norm_w": glan, "gdn_norm_w": gdnn}
        for j in range(N_DEV):
            for name, ref in whole.items():
                o_ref[(j,) + _region(name)] = ref[...]
            o_ref[(j,) + _region("a_log_fwd")] = alog[0:1, 0:8]
            o_ref[(j,) + _region("a_log_bwd")] = alog[0:1, 8:16]
            o_ref[(j,) + _region("dt_bias_fwd")] = dtb[0:1, 0:8]
            o_ref[(j,) + _region("dt_bias_bwd")] = dtb[0:1, 8:16]
            o_ref[(j,) + _region("conv_w")] = conv[:, j * SM_COLS : (j + 1) * SM_COLS]
            o_ref[(j,) + _region("gk_w2_fwd")] = w2f[32:48, j * 64 : (j + 1) * 64]
            o_ref[(j,) + _region("gk_w2_bwd")] = w2b[48:64, j * 64 : (j + 1) * 64]
            o_ref[j, LOSS_AT[0] : LOSS_AT[0] + 1, LOSS_AT[1] : LOSS_AT[1] + 1] = loss

    return pl.pallas_call(
        body, out_shape=jax.ShapeDtypeStruct((N_DEV, PACK_ROWS, PACK_COLS), f32), name="pack_small_grads",
    )(*[G[n] for n in names])


def _small_adam(land, own, W, M, V):
    bc1 = 1.0 - ADAM_B1**ADAM_STEP
    bc2 = 1.0 - ADAM_B2**ADAM_STEP
    n = len(SMALL_NAMES)

    def body(land_ref, own_ref, *refs):
        w_refs, m_refs, v_refs = refs[0:n], refs[n : 2 * n], refs[2 * n : 3 * n]
        outs, loss_ref, g_scr = refs[3 * n : 7 * n], refs[7 * n], refs[7 * n + 1]
        me = _my_device()
        g = jnp.where(me == 0, own_ref[0], land_ref[0])
        for j in range(1, N_DEV):
            g = g + jnp.where(me == j, own_ref[j], land_ref[j])
        g_scr[...] = g
        loss_ref[...] = g_scr[LOSS_AT[0] : LOSS_AT[0] + 1, LOSS_AT[1] : LOSS_AT[1] + 1]
        for i, name in enumerate(SMALL_NAMES):
            gp = g_scr[_region(name)]
            lead = (0,) if name in SMALL_SHARDED else (Ellipsis,)
            w, m, v = w_refs[i][lead], m_refs[i][lead], v_refs[i][lead]
            m2 = ADAM_B1 * m + (1.0 - ADAM_B1) * gp
            v2 = ADAM_B2 * v + (1.0 - ADAM_B2) * (gp * gp)
            delta = -ADAM_LR * ((m2 / bc1) / (jnp.sqrt(v2 / bc2) + ADAM_EPS) + ADAM_WD * w)
            for k, val in enumerate((gp, delta, m2, v2)):
                outs[4 * i + k][lead] = val

    shapes = [jax.ShapeDtypeStruct(W[name].shape, f32) for name in SMALL_NAMES for _ in range(4)]
    res = pl.pallas_call(
        body,
        out_shape=tuple(shapes) + (jax.ShapeDtypeStruct((1, 1), f32),),
        scratch_shapes=[pltpu.VMEM((PACK_ROWS, PACK_COLS), f32)],
        name="adam_small",
    )(land, own, *[P[name] for P in (W, M, V) for name in SMALL_NAMES])
    return {name: res[4 * i : 4 * i + 4] for i, name in enumerate(SMALL_NAMES)}, res[4 * n]


_MESH = pl.DeviceIdType.MESH


N_BIG_COPIES, N_SMALL_COPIES = 9, 7


def _gather_weights(w_in_s, sm_s):
    shapes = (w_in_s.shape, sm_s.shape)
    dtypes = (MXU_DTYPE, f32)

    def body(win_ref, sm_ref, gin_ref, gsm_ref, own_ref, send_sems, recv_sems, own_sem):
        x, y, c = lax.axis_index("x"), lax.axis_index("y"), lax.axis_index("c")
        me, sibling = (x, y, c), (x, y, 1 - c)
        chips = [(1 - x, y), (x, 1 - y), (1 - x, 1 - y)]

        def idx(px, py, pc):
            return 4 * px + 2 * py + pc

        for r in range(w_in_s.shape[1] // LANES):
            cols = slice(r * LANES, (r + 1) * LANES)
            own_ref[:, cols] = win_ref[:, cols].astype(MXU_DTYPE)
        keep_own = pltpu.make_async_copy(own_ref, gin_ref.at[idx(*me)], own_sem.at[0])
        keep_own.start()
        gsm_ref[idx(*me)] = sm_ref[...]

        def copy(sem, blk, to, src=None):
            return pltpu.make_async_remote_copy(src_ref=blk if src is None else src, dst_ref=blk, send_sem=send_sems.at[sem],
                                                recv_sem=recv_sems.at[sem], device_id=to, device_id_type=_MESH)

        def small(k, block, to):
            return copy(N_BIG_COPIES + k, gsm_ref.at[idx(*block)], to)

        half = w_in_s.shape[1] // 2
        xn, yn, dg = (1 - x, y), (x, 1 - y), (1 - x, 1 - y)

        def big(sem, block, to, part=None, src=None):
            blk = gin_ref.at[idx(*block)]
            return copy(sem, blk if part is None else blk.at[:, pl.ds(part * half, half)], to, src)

        started = [big(0, me, sibling, src=own_ref), big(1, me, (*xn, c), src=own_ref), big(2, me, (*yn, c), src=own_ref),
                   small(0, me, sibling)]
        started += [small(1 + j, me, (*chip, c)) for j, chip in enumerate(chips)]
        for cp in started:
            cp.start()

        def then(arrived, forwards):
            arrived.wait_recv()
            for cp in forwards:
                cp.start()
            started.extend(forwards)

        then(big(1, (*xn, c), me), [big(3, (*xn, c), (*yn, c), 1), big(5, (*xn, c), sibling)])
        then(big(2, (*yn, c), me), [big(4, (*yn, c), (*xn, c), 0), big(6, (*yn, c), sibling)])
        then(big(4, (*dg, c), me, 0), [big(7, (*dg, c), sibling, 0)])
        then(big(3, (*dg, c), me, 1), [big(8, (*dg, c), sibling, 1)])
        for j, chip in enumerate(chips):
            then(small(1 + j, (*chip, c), me), [small(4 + j, (*chip, c), sibling)])
        big(0, sibling, me).wait_recv()
        big(5, (*xn, 1 - c), me).wait_recv()
        big(6, (*yn, 1 - c), me).wait_recv()
        big(7, (*dg, 1 - c), me, 0).wait_recv()
        big(8, (*dg, 1 - c), me, 1).wait_recv()
        small(0, sibling, me).wait_recv()
        for j, chip in enumerate(chips):
            small(4 + j, (*chip, 1 - c), me).wait_recv()
        for cp in started:
            cp.wait_send()
        keep_own.wait()

    vm = pl.BlockSpec(memory_space=pltpu.VMEM)
    n_copies = N_BIG_COPIES + N_SMALL_COPIES
    return pl.pallas_call(
        body,
        out_shape=tuple(jax.ShapeDtypeStruct((N_DEV,) + s, d) for s, d in zip(shapes, dtypes)),
        in_specs=[vm, vm],
        out_specs=(_HBM_SPEC, vm),
        scratch_shapes=[pltpu.VMEM(w_in_s.shape, MXU_DTYPE), pltpu.SemaphoreType.DMA((n_copies,)), pltpu.SemaphoreType.DMA((n_copies,)),
                        pltpu.SemaphoreType.DMA((1,))],
        compiler_params=pltpu.CompilerParams(vmem_limit_bytes=VMEM_LIMIT),
        name="gather_weights",
    )(w_in_s, sm_s)


N_CHIP = 4
_EFFECT = pltpu.SideEffectType.DATAFLOW_SIDE_EFFECTING
_HBM_SPEC = pl.BlockSpec(memory_space=pltpu.HBM)
_SEM_SPEC = pl.BlockSpec(memory_space=pltpu.SEMAPHORE)


def _relation_peer(k, x, y, c):
    px = (1 - x) if (k & 4) else x
    py = (1 - y) if (k & 2) else y
    pc = (1 - c) if (k & 1) else c
    return (px, py, pc), 4 * px + 2 * py + pc


_PLAN_COPIES = {"gather": N_DEV - 1, "blocks": N_DEV - 1, "sibling": 4, "chips": 3}
_PLAN_SLOTS = {"gather": N_DEV, "blocks": N_DEV, "sibling": 4, "chips": 4}


def _plan_copies(plan, src_ref, land_ref, send_sems, recv_sems, arrival):
    x, y, c = lax.axis_index("x"), lax.axis_index("y"), lax.axis_index("c")
    me, my_chip = 4 * x + 2 * y + c, 2 * x + y
    out = []

    def add(i, src, there, here, dev):
        out.append(pltpu.make_async_remote_copy(src_ref=src, dst_ref=land_ref.at[here if arrival else there], send_sem=send_sems.at[i],
                                                recv_sem=recv_sems.at[i], device_id=dev, device_id_type=_MESH))

    if plan in ("gather", "blocks"):
        for k in range(1, N_DEV):
            dev, p = _relation_peer(k, x, y, c)
            add(k - 1, src_ref.at[p] if plan == "blocks" else src_ref, me, p, dev)
    elif plan == "sibling":
        for j in range(4):
            add(j, src_ref.at[2 * j + 1 - c], j, j, (x, y, 1 - c))
    else:
        for i, k in enumerate((2, 4, 6)):
            (px, py, pc), _ = _relation_peer(k, x, y, c)
            add(i, src_ref.at[2 * px + py], my_chip, 2 * px + py, (px, py, pc))
    return out


def _exchange_start(src, plan, name, after=None):
    n = _PLAN_COPIES[plan]
    land = lax.empty((_PLAN_SLOTS[plan],) + (src.shape if plan == "gather" else src.shape[1:]), src.dtype)
    extra = [] if after is None else [after]

    def body(src_ref, land_ref, *rest):
        send_sems, recv_sems, _, _, token = rest[len(extra) :]
        for cp in _plan_copies(plan, src_ref, land_ref, send_sems, recv_sems, False):
            cp.start()
        token[...] = jnp.zeros_like(token)

    return pl.pallas_call(
        body,
        name=name,
        out_shape=(pltpu.SemaphoreType.DMA((n,)), pltpu.SemaphoreType.DMA((n,)), pltpu.HBM(src.shape, src.dtype),
                   pltpu.HBM(land.shape, land.dtype), jax.ShapeDtypeStruct((8, LANES), f32)),
        in_specs=(_HBM_SPEC, _HBM_SPEC) + (_ANY_SPEC,) * len(extra),
        out_specs=(_SEM_SPEC, _SEM_SPEC, _HBM_SPEC, _HBM_SPEC, pl.BlockSpec(memory_space=pltpu.VMEM)),
        input_output_aliases={0: 2, 1: 3},
        compiler_params=pltpu.CompilerParams(has_side_effects=_EFFECT),
    )(pltpu.with_memory_space_constraint(src, pltpu.HBM), pltpu.with_memory_space_constraint(land, pltpu.HBM), *extra)


def _exchange_wait(started, after, plan, name):
    send_sems, recv_sems, src_thru, land_thru, _ = started

    def body(src_ref, land_ref, send_sems, recv_sems, after_ref, src_dead, got_ref):
        for cp in _plan_copies(plan, src_ref, land_ref, send_sems, recv_sems, True):
            cp.wait_send()
            cp.wait_recv()

    return pl.pallas_call(
        body,
        name=name,
        out_shape=(pltpu.HBM(src_thru.shape, src_thru.dtype), pltpu.HBM(land_thru.shape, land_thru.dtype)),
        in_specs=(_HBM_SPEC, _HBM_SPEC, _SEM_SPEC, _SEM_SPEC, pl.BlockSpec(memory_space=pl.ANY)),
        out_specs=(_HBM_SPEC, _HBM_SPEC),
        input_output_aliases={0: 0, 1: 1},
        compiler_params=pltpu.CompilerParams(has_side_effects=_EFFECT),
    )(src_thru, land_thru, send_sems, recv_sems, after)


def _pair_sum(g, r, name, tc=LANES):
    _, R, Cc = g.shape

    def body(g_ref, r_ref, o_ref):
        o_ref[...] = (g_ref[...].astype(f32) + r_ref[...].astype(f32)).astype(o_ref.dtype)

    return pl.pallas_call(
        body,
        out_shape=jax.ShapeDtypeStruct(r.shape, r.dtype),
        grid=(N_CHIP, Cc // tc),
        in_specs=[pl.BlockSpec((1, R, tc), lambda k, i: (2 * k + lax.axis_index("c"), 0, i)), pl.BlockSpec((1, R, tc), lambda k, i: (k, 0, i))],
        out_specs=pl.BlockSpec((1, R, tc), lambda k, i: (k, 0, i)),
        compiler_params=_cparams(("parallel", "parallel")),
        name=name,
    )(g, r)


def _sum_adam(parts, w, m, v, name, tb, tc=None, own=None, own_slot=None):
    R, Cc = w.shape
    n_parts = parts.shape[0]
    tb = R if tc else min(tb, R)
    tc = tc or Cc
    assert R % tb == 0 and Cc % tc == 0
    bc1 = 1.0 - ADAM_B1**ADAM_STEP
    bc2 = 1.0 - ADAM_B2**ADAM_STEP
    extra = [] if own is None else [own]

    def body(p_ref, w_ref, m_ref, v_ref, *rest):
        g_ref, d_ref, nm_ref, nv_ref = rest[len(extra) :]

        def part(j):
            pj = p_ref[j].astype(f32)
            return pj if own is None else jnp.where(own_slot() == j, rest[0][0].astype(f32), pj)

        g = part(0)
        for j in range(1, n_parts):
            g = g + part(j)
        g_ref[...] = g
        m2 = ADAM_B1 * m_ref[...] + (1.0 - ADAM_B1) * g
        v2 = ADAM_B2 * v_ref[...] + (1.0 - ADAM_B2) * (g * g)
        nm_ref[...] = m2
        nv_ref[...] = v2
        d_ref[...] = -ADAM_LR * ((m2 / bc1) / (jnp.sqrt(v2 / bc2) + ADAM_EPS) + ADAM_WD * w_ref[...])

    blk = pl.BlockSpec((tb, tc), lambda i, j: (i, j))
    o = jax.ShapeDtypeStruct((R, Cc), f32)
    return pl.pallas_call(
        body,
        out_shape=(o, o, o, o),
        grid=(R // tb, Cc // tc),
        in_specs=[pl.BlockSpec((n_parts, tb, tc), lambda i, j: (0, i, j)), blk, blk, blk]
        + [pl.BlockSpec((1, tb, tc), lambda i, j: (own_slot(), i, j))] * len(extra),
        out_specs=(blk, blk, blk, blk),
        compiler_params=_cparams(("parallel", "parallel")),
        name=name,
    )(parts, w, m, v, *extra)


def _my_device():
    return 4 * lax.axis_index("x") + 2 * lax.axis_index("y") + lax.axis_index("c")


def _my_chip():
    return 2 * lax.axis_index("x") + lax.axis_index("y")


_WEIGHTS = ["ln_pre_w", "w_in", "conv_w", "a_log_fwd", "a_log_bwd", "dt_bias_fwd", "dt_bias_bwd", "gdn_norm_w", "w_proj_gdn",
            "gk_w2_fwd", "gk_b2_fwd", "gk_w2_bwd", "gk_b2_bwd", "gla_norm_w", "w_proj_gla", "w_out", "ln_post_w"]


def kernel(x, ln_pre_w, w_in, conv_w, a_log_fwd, a_log_bwd, dt_bias_fwd, dt_bias_bwd, gdn_norm_w, w_proj_gdn, gk_w2_fwd, gk_b2_fwd, gk_w2_bwd, gk_b2_bwd, gla_norm_w, w_proj_gla, w_out, ln_post_w, loss_target, m_ln_pre_w, m_w_in, m_conv_w, m_a_log_fwd, m_a_log_bwd, m_dt_bias_fwd, m_dt_bias_bwd, m_gdn_norm_w, m_w_proj_gdn, m_gk_w2_fwd, m_gk_b2_fwd, m_gk_w2_bwd, m_gk_b2_bwd, m_gla_norm_w, m_w_proj_gla, m_w_out, m_ln_post_w, v_ln_pre_w, v_w_in, v_conv_w, v_a_log_fwd, v_a_log_bwd, v_dt_bias_fwd, v_dt_bias_bwd, v_gdn_norm_w, v_w_proj_gdn, v_gk_w2_fwd, v_gk_b2_fwd, v_gk_w2_bwd, v_gk_b2_bwd, v_gla_norm_w, v_w_proj_gla, v_w_out, v_ln_post_w):
    args = locals()
    W = {n: args[n] for n in _WEIGHTS}
    M = {n: args["m_" + n] for n in _WEIGHTS}
    V = {n: args["v_" + n] for n in _WEIGHTS}

    wp_stack = lambda P: jnp.concatenate([P["w_proj_gdn"], P["w_proj_gla"], P["w_out"]], axis=0)
    sm_pack = lambda P: _pack_small_shard(P["conv_w"][0], P["gk_w2_fwd"][0], P["gk_w2_bwd"][0])
    me = 4 * lax.axis_index("x") + 2 * lax.axis_index("y") + lax.axis_index("c")
    w_in_t, m_in_t, v_in_t = (jnp.transpose(a[0]) for a in (w_in, m_w_in, v_w_in))
    g_in, g_sm = _gather_weights(w_in_t, sm_pack(W))
    wp_own = wp_stack(W).astype(MXU_DTYPE)
    wp_started = _exchange_start(wp_own, "gather", "gather_proj_start", after=g_sm)

    def proj_weights(after):
        own, land = _exchange_wait(wp_started, after, "gather", "gather_proj_wait")
        g_p = lax.dynamic_update_slice(land, own[None], (me, 0, 0, 0))
        return tuple(g_p[:, i].reshape(D_MODEL, D_MODEL) for i in range(3))

    rows_p = 3 * D_MODEL // N_DEV
    sent = {}

    def send_proj_grads(d_wpg, d_wpl, d_wout):
        p_p = jnp.stack([d_wpg, d_wpl, d_wout]).reshape(3, N_DEV, D_MODEL // N_DEV, D_MODEL).transpose(1, 0, 2, 3)
        sent["p_p"] = p_p.reshape(N_DEV, rows_p, D_MODEL)
        sent["started"] = _exchange_start(sent["p_p"], "blocks", "exchange_proj_start")
        return sent["started"][4][0:1, 0:1]

    def send_in_grads(d_wcat):
        p_in = _from_cat(d_wcat).reshape(N_DEV, SHARD_IN, D_MODEL)
        sib = _exchange_start(p_in, "sibling", "exchange_in_sibling_start")
        p_in, sib_land = _exchange_wait(sib, sib[4], "sibling", "exchange_in_sibling_wait")
        sent["chips"] = _exchange_start(_pair_sum(p_in, sib_land, "pair_sum_w_in", tc=512), "chips", "exchange_in_chips_start")
        p_p, land_p = _exchange_wait(sent["started"], sent["chips"][4], "blocks", "exchange_proj_wait")
        sent["o_p"] = _sum_adam(land_p, wp_stack(W).reshape(rows_p, D_MODEL), wp_stack(M).reshape(rows_p, D_MODEL),
                                wp_stack(V).reshape(rows_p, D_MODEL), "adam_w_proj", 128, own=p_p, own_slot=_my_device)
        return sent["o_p"][0]

    wcat = _to_cat(g_in.reshape(N_IN, D_MODEL))
    conv_full = g_sm[:, 0:CONV_K, :].transpose(1, 0, 2).reshape(CONV_K, N_DEV * SM_COLS)
    w2f_full = g_sm[:, 8:24, 0:64].transpose(1, 0, 2).reshape(16, 512)
    w2b_full = g_sm[:, 8:24, 64:128].transpose(1, 0, 2).reshape(16, 512)

    avec = _lane_row(a_log_fwd, a_log_bwd)
    dvec = _lane_row(dt_bias_fwd, dt_bias_bwd)
    G = _local_step(x[0], loss_target[0], wcat, conv_full, avec, dvec, gdn_norm_w, _pad_w2(w2f_full, 32), _pad_w2(w2b_full, 48),
                    gk_b2_fwd, gk_b2_bwd, gla_norm_w, proj_weights, ln_pre_w + wp_started[4][0:1, 0:1], ln_post_w, send_proj_grads,
                    send_in_grads)

    small_started = _exchange_start(_pack_small_grads(G), "blocks", "exchange_small_start")
    q_in, land_in = _exchange_wait(sent["chips"], small_started[4], "chips", "exchange_in_chips_wait")
    o_in = _sum_adam(land_in, w_in_t, m_in_t, v_in_t, "adam_w_in", SHARD_IN, tc=256, own=q_in, own_slot=_my_chip)
    o_p = sent["o_p"]
    p_small, land_small = _exchange_wait(small_started, o_in[0], "blocks", "exchange_small_wait")
    o_small, loss = _small_adam(land_small, p_small, W, M, V)

    res = []
    for kind in range(4):
        per = {name: o_small[name][kind] for name in SMALL_NAMES}
        per["w_in"] = jnp.transpose(o_in[kind])[None]
        pp = o_p[kind].reshape(3, 1, D_MODEL // N_DEV, D_MODEL)
        per["w_proj_gdn"], per["w_proj_gla"], per["w_out"] = pp[0], pp[1], pp[2]
        res.append([per[n] for n in _WEIGHTS])
    return (loss[0, 0], G["grad_x"][None], *res[0], *res[1], *res[2], *res[3])
```

```python
import jax
import jax.numpy as jnp
from jax import lax
from jax.experimental import pallas as pl
from jax.experimental.pallas import tpu as pltpu

f32 = jnp.float32
MXU_DTYPE = jnp.bfloat16

D_MODEL = 1024
CHUNK = 64
NORM_EPS = 1e-6
GDN_HEADS, GDN_DIM = 8, 128
GLA_HEADS, GLA_DK, GLA_DV = 4, 128, 256
GATE_NORMALIZER = 16.0
CONV_K = 5
N_IN = 9280
N_DEV = 8
SHARD_IN = N_IN // N_DEV

N_CAT = 9728
SMALL_OFF = 9216
LANES = 128

ADAM_LR, ADAM_B1, ADAM_B2, ADAM_EPS, ADAM_WD, ADAM_STEP = 0.001, 0.9, 0.999, 1e-08, 0.01, 10

VMEM_LIMIT = 56 * 1024 * 1024


def _cparams(sem=None):
    return pltpu.CompilerParams(dimension_semantics=sem, vmem_limit_bytes=VMEM_LIMIT)


_DN = {"nn": (((1,), (0,)), ((), ())), "nt": (((1,), (1,)), ((), ())), "tn": (((0,), (0,)), ((), ()))}
_BDN = {"nn": (((2,), (1,)), ((0,), (0,))), "nt": (((2,), (2,)), ((0,), (0,))), "tn": (((1,), (1,)), ((0,), (0,)))}


def _dot16(a, b, kind):
    return lax.dot_general(a, b, (_DN if a.ndim == 2 else _BDN)[kind], preferred_element_type=f32)


def _mx(a, b, kind):
    return _dot16(a.astype(MXU_DTYPE), b.astype(MXU_DTYPE), kind)


def _split2(a):
    hi = a.astype(jnp.bfloat16)
    return hi, (a - hi.astype(f32)).astype(jnp.bfloat16)


def _split3(a):
    p1 = a.astype(jnp.bfloat16)
    r = a - p1.astype(f32)
    p2 = r.astype(jnp.bfloat16)
    return p1, p2, (r - p2.astype(f32)).astype(jnp.bfloat16)


def _h3s(a2, b2, kind):
    (ah, al), (bh, bl) = a2, b2
    return _dot16(ah, bh, kind) + (_dot16(ah, bl, kind) + _dot16(al, bh, kind))


def _h3(a, b, kind):
    return _h3s(_split2(a), _split2(b), kind)


def _exact01(t01, x, kind="nn"):
    t = t01.astype(jnp.bfloat16)
    x1, x2, x3 = _split3(x)
    return _dot16(t, x1, kind) + (_dot16(t, x2, kind) + _dot16(t, x3, kind))


def _colsum_col(e):
    ones = jnp.ones(e.shape[:-1] + (LANES,), jnp.bfloat16)
    e1, e2, e3 = _split3(e)
    return (_dot16(e1, ones, "tn") + (_dot16(e2, ones, "tn") + _dot16(e3, ones, "tn")))[..., 0:1]


def _iota3(shape, axis):
    return lax.broadcasted_iota(jnp.int32, shape, axis)


def _dir_masks(B, nf, Cn):
    shp = (B, Cn, Cn)
    bb, ii, jj = _iota3(shp, 0), _iota3(shp, 1), _iota3(shp, 2)
    fwd = bb < nf
    rev = jnp.logical_not(fwd)
    incl = jnp.logical_or(jnp.logical_and(fwd, ii >= jj), jnp.logical_and(rev, ii <= jj))
    strict = jnp.logical_and(incl, ii != jj)
    return incl, strict, ii == jj


def _last_row(x, nf):
    B, Cn = x.shape[0], x.shape[1]
    fwd = _iota3((B, 1, 1), 0) < nf
    return jnp.where(fwd, x[:, Cn - 1 : Cn, :], x[:, 0:1, :])


def _add_at_last_row(x, val, nf):
    B, Cn = x.shape[0], x.shape[1]
    bb, rr = _iota3((B, Cn, 1), 0), _iota3((B, Cn, 1), 1)
    at = jnp.logical_or(jnp.logical_and(bb < nf, rr == Cn - 1), jnp.logical_and(bb >= nf, rr == 0))
    return x + jnp.where(at, val, 0.0)


def _sigmoid(x):
    return jax.nn.sigmoid(x)


def _silu(x):
    return x * _sigmoid(x)


def _silu_grad(x):
    s = _sigmoid(x)
    return s * (1.0 + x * (1.0 - s))


def _softplus(x):
    u = jnp.exp(-jnp.abs(x))
    l1p = jnp.where(u < 1e-3, u * (1.0 - u * (0.5 - u * (1.0 / 3.0))), jnp.log(1.0 + u))
    return jnp.maximum(x, 0.0) + l1p


def _mxr(x):
    return x.astype(MXU_DTYPE).astype(f32)


def _iota2(shape, axis):
    return lax.broadcasted_iota(jnp.int32, shape, axis)


def _matmul(a, b, kind, out_dtype, name, tm, tn, tk, out_t=False, after=None):
    extra = [] if after is None else [after]
    if kind == "nn":
        (M, K), N = a.shape, b.shape[1]
    elif kind == "nt":
        (M, K), N = a.shape, b.shape[0]
    else:
        (K, M), N = a.shape, b.shape[1]
    tm, tn, tk = min(tm, M), min(tn, N), min(tk, K)
    assert M % tm == 0 and N % tn == 0 and K % tk == 0, (name, M, N, K)
    nk = K // tk
    if kind == "tn":
        a_spec = pl.BlockSpec((tk, tm), lambda i, j, k: (k, i))
    else:
        a_spec = pl.BlockSpec((tm, tk), lambda i, j, k: (i, k))
    b_mode = dict(pipeline_mode=pl.Buffered(1)) if (tn == N and tk == K) else {}
    if kind == "nt":
        b_spec = pl.BlockSpec((tn, tk), lambda i, j, k: (j, k), **b_mode)
    else:
        b_spec = pl.BlockSpec((tk, tn), lambda i, j, k: (k, j), **b_mode)

    def body(a_ref, b_ref, *rest):
        o_ref, acc = rest[len(extra)], rest[len(extra) + 1 :]
        p = _mx(a_ref[...], b_ref[...], kind)
        if nk == 1:
            o_ref[...] = (p.T if out_t else p).astype(out_dtype)
        else:
            assert not out_t
            acc_ref = acc[0]
            k = pl.program_id(2)

            @pl.when(k == 0)
            def _():
                acc_ref[...] = p

            @pl.when(k > 0)
            def _():
                acc_ref[...] += p

            @pl.when(k == nk - 1)
            def _():
                o_ref[...] = acc_ref[...].astype(out_dtype)

    return pl.pallas_call(
        body,
        out_shape=jax.ShapeDtypeStruct((N, M) if out_t else (M, N), out_dtype),
        grid=(M // tm, N // tn, nk),
        in_specs=[a_spec, b_spec] + [pl.BlockSpec((8, LANES), lambda i, j, k: (0, 0))] * len(extra),
        out_specs=pl.BlockSpec((tn, tm), lambda i, j, k: (j, i)) if out_t else pl.BlockSpec((tm, tn), lambda i, j, k: (i, j)),
        scratch_shapes=[] if nk == 1 else [pltpu.VMEM((tm, tn), f32)],
        compiler_params=_cparams(("parallel", "parallel", "arbitrary")),
        name=name,
    )(a, b, *extra)


def _dw_proj(pairs, name, tn=512):
    T = pairs[0][0].shape[0]
    nj, rows = D_MODEL // tn, D_MODEL // N_DEV

    def body(*refs):
        o_ref = refs[2 * len(pairs)]
        for p in range(len(pairs)):

            @pl.when(pl.program_id(0) == p)
            def _(p=p):
                r = _mx(refs[2 * p][...], refs[2 * p + 1][...], "tn")
                o_ref[:, 0] = r.reshape(N_DEV, rows, tn).astype(o_ref.dtype)

    def b_map(p):
        return lambda g, j: (0, jnp.where(g == p, j, jnp.where(g < p, 0, nj - 1)))

    in_specs = []
    for p in range(len(pairs)):
        in_specs += [pl.BlockSpec((T, D_MODEL), lambda g, j: (0, 0), pipeline_mode=pl.Buffered(1)), pl.BlockSpec((T, tn), b_map(p))]
    return pl.pallas_call(
        body,
        out_shape=jax.ShapeDtypeStruct((N_DEV, len(pairs), rows, D_MODEL), MXU_DTYPE),
        grid=(len(pairs), nj),
        in_specs=in_specs,
        out_specs=pl.BlockSpec((N_DEV, 1, rows, tn), lambda g, j: (0, g, 0, j)),
        compiler_params=_cparams(("arbitrary", "arbitrary")),
        name=name,
    )(*[t for pair in pairs for t in pair])


ROW_TILE = 512


def _rms_pre(x, w):
    T = x.shape[0]
    tb = min(ROW_TILE, T)

    def body(x_ref, w_ref, h_ref, ht_ref):
        xv = x_ref[...]
        r = lax.rsqrt(jnp.mean(xv * xv, axis=1, keepdims=True) + NORM_EPS)
        h = xv * r * w_ref[...]
        h_ref[...] = h.astype(h_ref.dtype)
        ht_ref[...] = h.T.astype(ht_ref.dtype)

    return pl.pallas_call(
        body,
        out_shape=(jax.ShapeDtypeStruct((T, D_MODEL), MXU_DTYPE), jax.ShapeDtypeStruct((D_MODEL, T), MXU_DTYPE)),
        grid=(T // tb,),
        in_specs=[pl.BlockSpec((tb, D_MODEL), lambda i: (i, 0)), pl.BlockSpec((1, D_MODEL), lambda i: (0, 0))],
        out_specs=(pl.BlockSpec((tb, D_MODEL), lambda i: (i, 0)), pl.BlockSpec((D_MODEL, tb), lambda i: (0, i))),
        compiler_params=_cparams(("parallel",)),
        name="rms_pre",
    )(x, w)


def _rms_pre_bwd(x, w, dh, dy):
    T = x.shape[0]
    tb = min(ROW_TILE, T)

    def body(x_ref, w_ref, dh_ref, dy_ref, dx_ref, dw_ref):
        i = pl.program_id(0)
        xv, dhv = x_ref[...], dh_ref[...]
        r = lax.rsqrt(jnp.mean(xv * xv, axis=1, keepdims=True) + NORM_EPS)
        dhw = dhv * w_ref[...]
        dx_ref[...] = dy_ref[...] + r * dhw - xv * (r * r * r) * jnp.mean(dhw * xv, axis=1, keepdims=True)
        part = jnp.sum(dhv * xv * r, axis=0, keepdims=True)

        @pl.when(i == 0)
        def _():
            dw_ref[...] = part

        @pl.when(i > 0)
        def _():
            dw_ref[...] += part

    blk = pl.BlockSpec((tb, D_MODEL), lambda i: (i, 0))
    row = pl.BlockSpec((1, D_MODEL), lambda i: (0, 0))
    return pl.pallas_call(
        body,
        out_shape=(jax.ShapeDtypeStruct((T, D_MODEL), f32), jax.ShapeDtypeStruct((1, D_MODEL), f32)),
        grid=(T // tb,),
        in_specs=[blk, row, blk, blk],
        out_specs=(blk, row),
        compiler_params=_cparams(("arbitrary",)),
        name="rms_pre_bwd",
    )(x, w, dh, dy)


SHIFTS = (-2, -1, 1, 2)


SHIFT_PAD = 8
PREP_HEADS, PREP_HEADS_BWD = 4, 1


def _shifted(x, pad_ref):
    T, cb = x.shape
    pad_ref[0:SHIFT_PAD] = jnp.zeros((SHIFT_PAD, cb), f32)
    pad_ref[SHIFT_PAD + T :] = jnp.zeros((SHIFT_PAD, cb), f32)
    pad_ref[SHIFT_PAD : SHIFT_PAD + T] = x
    return {d: pad_ref[SHIFT_PAD + d : SHIFT_PAD + d + T] for d in SHIFTS}


def _conv5(u, ush, cw):
    acc = u * cw[2:3, :]
    for j in (0, 1, 3, 4):
        acc = acc + ush[j - 2] * cw[j : j + 1, :]
    return acc


def _gdn_prep(proj, conv_w):
    T = proj.shape[0]
    hps, W = PREP_HEADS, PREP_HEADS * GDN_DIM
    per = GDN_HEADS // hps

    def body(u_ref, cw_ref, o_ref, pad_ref):
        j = pl.program_id(0)
        for i in range(hps):
            sl = slice(i * GDN_DIM, (i + 1) * GDN_DIM)
            u = _mxr(u_ref[:, sl])
            s = _silu(_conv5(u, _shifted(u, pad_ref), _mxr(cw_ref[:, sl])))
            o_ref[0, :, sl] = jnp.where(j < 2 * per, s * lax.rsqrt(jnp.sum(s * s, axis=1, keepdims=True) + NORM_EPS), s)

    return pl.pallas_call(
        body,
        out_shape=jax.ShapeDtypeStruct((3, T, GDN_HEADS * GDN_DIM), f32),
        grid=(3 * per,),
        in_specs=[pl.BlockSpec((T, W), lambda j: (0, j)), pl.BlockSpec((CONV_K, W), lambda j: (0, j))],
        out_specs=pl.BlockSpec((1, T, W), lambda j: (j // per, 0, j % per)),
        scratch_shapes=[pltpu.VMEM((T + 2 * SHIFT_PAD, GDN_DIM), f32)],
        compiler_params=_cparams(("parallel",)),
        name="gdn_prep",
    )(proj, conv_w)


def _gdn_prep_bwd(proj, conv_w, dqkv_f, dqkv_b, dproj):
    T = proj.shape[0]
    hps, W = PREP_HEADS_BWD, PREP_HEADS_BWD * GDN_DIM
    per = GDN_HEADS // hps

    def body(u_ref, cw_ref, df_ref, db_ref, _, du_ref, dcw_ref, upad_ref, dcpad_ref):
        j = pl.program_id(0)
        for i in range(hps):
            sl = slice(i * GDN_DIM, (i + 1) * GDN_DIM)
            u, cw = _mxr(u_ref[:, sl]), _mxr(cw_ref[:, sl])
            ush = _shifted(u, upad_ref)
            c = _conv5(u, ush, cw)
            s = _silu(c)
            dn = df_ref[0, :, sl] + db_ref[0, :, sl]
            rinv = lax.rsqrt(jnp.sum(s * s, axis=1, keepdims=True) + NORM_EPS)
            ds_norm = rinv * dn - s * (rinv * rinv * rinv) * jnp.sum(dn * s, axis=1, keepdims=True)
            ds = jnp.where(j < 2 * per, ds_norm, dn)
            dc = _mxr(ds * _silu_grad(c))
            dcsh = _shifted(dc, dcpad_ref)
            du = dc * cw[2:3, :]
            for jj in range(CONV_K):
                d = jj - 2
                dcw_ref[jj : jj + 1, sl] = jnp.sum(dc * (u if d == 0 else ush[d]), axis=0, keepdims=True)
                if d != 0:
                    du = du + dcsh[-d] * cw[jj : jj + 1, :]
            du_ref[:, sl] = du.astype(du_ref.dtype)

    qspec = pl.BlockSpec((1, T, W), lambda j: (j // per, 0, j % per))
    return pl.pallas_call(
        body,
        out_shape=(jax.ShapeDtypeStruct(dproj.shape, dproj.dtype), jax.ShapeDtypeStruct((CONV_K, 3 * GDN_HEADS * GDN_DIM), f32)),
        grid=(3 * per,),
        in_specs=[pl.BlockSpec((T, W), lambda j: (0, j)), pl.BlockSpec((CONV_K, W), lambda j: (0, j)), qspec, qspec, _ANY_SPEC],
        out_specs=(pl.BlockSpec((T, W), lambda j: (0, j)), pl.BlockSpec((CONV_K, W), lambda j: (0, j))),
        input_output_aliases={4: 0},
        scratch_shapes=[pltpu.VMEM((T + 2 * SHIFT_PAD, GDN_DIM), f32)] * 2,
        compiler_params=_cparams(("parallel",)),
        name="gdn_prep_bwd",
    )(proj, conv_w, dqkv_f, dqkv_b, dproj)


def _chunk_tri(n, rev):
    i, j = _iota2((n, n), 0), _iota2((n, n), 1)
    same = jnp.right_shift(i, 6) == jnp.right_shift(j, 6)
    order = (j >= i) if rev else (j <= i)
    return jnp.where(jnp.logical_and(same, order), 1.0, 0.0).astype(f32)


def _gate_prep(proj, avec, dvec, w2f, w2b, b2f, b2b, tb=256):
    T = proj.shape[0]
    W = GLA_HEADS * GLA_DK

    def body(s_ref, a_ref, d_ref, wf_ref, wb_ref, bf_ref, bb_ref, g_ref, gc_ref, gf_ref, gb_ref):
        small = s_ref[...]
        tri_f, tri_r = _chunk_tri(tb, False), _chunk_tri(tb, True)
        lane = _iota2(small.shape, 1)
        lg = -jnp.exp(a_ref[...]) * _softplus(small + d_ref[...])
        beta = _sigmoid(small)
        g_ref[...] = jnp.where(lane < 16, lg, jnp.where(lane < 32, beta, 0.0))
        lgm = jnp.where(lane < 16, lg, 0.0)
        gc_ref[...] = jnp.where(lane < 8, _exact01(tri_f, lgm), _exact01(tri_r, lgm))
        gkf = -_softplus(-(_mx(small, wf_ref[...], "nn") + bf_ref[...])) * (1.0 / GATE_NORMALIZER)
        gkb = -_softplus(-(_mx(small, wb_ref[...], "nn") + bb_ref[...])) * (1.0 / GATE_NORMALIZER)
        gf_ref[...] = _exact01(tri_f, gkf)
        gb_ref[...] = _exact01(tri_r, gkb)

    blk = pl.BlockSpec((tb, LANES), lambda i: (i, 0))
    row = pl.BlockSpec((1, LANES), lambda i: (0, 0))
    wide = pl.BlockSpec((tb, W), lambda i: (i, 0))
    wsp = pl.BlockSpec((LANES, W), lambda i: (0, 0))
    wrow = pl.BlockSpec((1, W), lambda i: (0, 0))
    return pl.pallas_call(
        body,
        out_shape=(jax.ShapeDtypeStruct((T, LANES), f32), jax.ShapeDtypeStruct((T, LANES), f32),
                   jax.ShapeDtypeStruct((T, W), f32), jax.ShapeDtypeStruct((T, W), f32)),
        grid=(T // tb,),
        in_specs=[pl.BlockSpec((tb, LANES), lambda i: (i, SMALL_OFF // LANES)), row, row, wsp, wsp, wrow, wrow],
        out_specs=(blk, blk, wide, wide),
        compiler_params=_cparams(("parallel",)),
        name="gate_prep",
    )(proj, avec, dvec, w2f, w2b, b2f, b2b)


def _gdn_gates_bwd(proj, avec, dvec, dg_f, dg_b, dsmall_gla, dproj):
    T = proj.shape[0]
    tb = min(ROW_TILE, T)
    pad = N_CAT - SMALL_OFF

    def body(s_ref, a_ref, d_ref, gf_ref, gb_ref, dl_ref, _, ds_ref, da_ref, dd_ref):
        i = pl.program_id(0)
        small = s_ref[...]
        lane = _iota2(small.shape, 1)
        dgate = gf_ref[...] + gb_ref[...]
        z = small + d_ref[...]
        nega = -jnp.exp(a_ref[...])
        dz = dgate * nega * _sigmoid(z)
        beta = _sigmoid(small)
        dsm = jnp.where(lane < 16, dz, jnp.where(lane < 32, dgate * beta * (1.0 - beta), 0.0))
        ds_ref[:, :LANES] = (dsm + dl_ref[...]).astype(ds_ref.dtype)
        ds_ref[:, LANES:] = jnp.zeros((tb, pad - LANES), ds_ref.dtype)
        lg = nega * _softplus(z)
        pa = jnp.sum(jnp.where(lane < 16, dgate * lg, 0.0), axis=0, keepdims=True)
        pd = jnp.sum(jnp.where(lane < 16, dz, 0.0), axis=0, keepdims=True)

        @pl.when(i == 0)
        def _():
            da_ref[...] = pa
            dd_ref[...] = pd

        @pl.when(i > 0)
        def _():
            da_ref[...] += pa
            dd_ref[...] += pd

    blk = pl.BlockSpec((tb, LANES), lambda i: (i, 0))
    row = pl.BlockSpec((1, LANES), lambda i: (0, 0))
    return pl.pallas_call(
        body,
        out_shape=(jax.ShapeDtypeStruct(dproj.shape, dproj.dtype), jax.ShapeDtypeStruct((1, LANES), f32), jax.ShapeDtypeStruct((1, LANES), f32)),
        grid=(T // tb,),
        in_specs=[pl.BlockSpec((tb, LANES), lambda i: (i, SMALL_OFF // LANES)), row, row, blk, blk, blk, _ANY_SPEC],
        out_specs=(pl.BlockSpec((tb, pad), lambda i: (i, DP_SMALL_BLK)), row, row),
        input_output_aliases={6: 0},
        compiler_params=_cparams(("arbitrary",)),
        name="gdn_gates_bwd",
    )(proj, avec, dvec, dg_f, dg_b, dsmall_gla, dproj)


GDN_HB = 8


def _inv_unit_lower(L, eye):
    A = eye - L
    P2 = _split2(L)
    for _ in range(3):
        P2 = _split2(_h3s(P2, P2, "nn"))
        A = A + _h3s(_split2(A), P2, "nn")
    Ph = P2[0]
    for _ in range(2):
        Ph = _dot16(Ph, Ph, "nn").astype(jnp.bfloat16)
        A = A + _dot16(A.astype(jnp.bfloat16), Ph, "nn")
    return A


def _lane_col(blk, idx):
    lane = _iota2(blk.shape, 1)
    return jnp.sum(jnp.where(lane == idx, blk, 0.0), axis=1, keepdims=True)


def _out_norm(o_f, o_b, proj, gate_blk, w, hd, name):
    T, W = o_f.shape
    tb = min(ROW_TILE, T)
    nh = W // hd

    def body(of_ref, ob_ref, z_ref, w_ref, y_ref):
        wv = w_ref[...]
        for i in range(nh):
            sl = slice(i * hd, (i + 1) * hd)
            o = of_ref[:, sl] + ob_ref[:, sl]
            r = lax.rsqrt(jnp.mean(o * o, axis=1, keepdims=True) + NORM_EPS)
            y_ref[:, sl] = (o * r * wv * _silu(z_ref[:, sl])).astype(y_ref.dtype)

    blk = pl.BlockSpec((tb, W), lambda i: (i, 0))
    return pl.pallas_call(
        body,
        out_shape=jax.ShapeDtypeStruct((T, W), MXU_DTYPE),
        grid=(T // tb,),
        in_specs=[blk, blk, pl.BlockSpec((tb, W), lambda i: (i, gate_blk)), pl.BlockSpec((1, hd), lambda i: (0, 0))],
        out_specs=blk,
        compiler_params=_cparams(("parallel",)),
        name=name,
    )(o_f, o_b, proj, w)


def _out_norm_bwd(dy, o_f, o_b, proj, gate_blk, w, hd, name, dproj):
    T, W = o_f.shape
    tb = min(ROW_TILE, T)
    nh = W // hd

    def body(dy_ref, of_ref, ob_ref, z_ref, w_ref, _, do_ref, dz_ref, dw_ref):
        i = pl.program_id(0)
        wv = w_ref[...]
        dw = jnp.zeros((1, hd), f32)
        for a in range(nh):
            sl = slice(a * hd, (a + 1) * hd)
            o = of_ref[:, sl] + ob_ref[:, sl]
            z = z_ref[:, sl]
            dyv = dy_ref[:, sl]
            r = lax.rsqrt(jnp.mean(o * o, axis=1, keepdims=True) + NORM_EPS)
            orr = o * r
            dn = dyv * _silu(z)
            dz_ref[:, sl] = (dyv * orr * wv * _silu_grad(z)).astype(dz_ref.dtype)
            dw = dw + jnp.sum(dn * orr, axis=0, keepdims=True)
            dnw = dn * wv
            do_ref[:, sl] = r * dnw - o * (r * r * r) * jnp.mean(dnw * o, axis=1, keepdims=True)

        @pl.when(i == 0)
        def _():
            dw_ref[...] = dw

        @pl.when(i > 0)
        def _():
            dw_ref[...] += dw

    blk = pl.BlockSpec((tb, W), lambda i: (i, 0))
    row = pl.BlockSpec((1, hd), lambda i: (0, 0))
    return pl.pallas_call(
        body,
        out_shape=(jax.ShapeDtypeStruct((T, W), f32), jax.ShapeDtypeStruct(dproj.shape, dproj.dtype), jax.ShapeDtypeStruct((1, hd), f32)),
        grid=(T // tb,),
        in_specs=[blk, blk, blk, pl.BlockSpec((tb, W), lambda i: (i, gate_blk)), row, _ANY_SPEC],
        out_specs=(blk, pl.BlockSpec((tb, W), lambda i: (i, gate_blk)), row),
        input_output_aliases={5: 1},
        compiler_params=_cparams(("arbitrary",)),
        name=name,
    )(dy, o_f, o_b, proj, w, dproj)


def _gla_prep_bwd(proj, w2f, w2b, b2f, b2b, dgk_f, dgk_b, dqkv, dproj):
    T = proj.shape[0]
    tb = min(ROW_TILE, T)
    W = GLA_HEADS * GLA_DK
    WV = GLA_HEADS * GLA_DV

    def body(s_ref, wf_ref, wb_ref, bf_ref, bb_ref, df_ref, db_ref, qf, qb, kf, kb, vf, vb, _,
             dp_ref, ds_ref, dwf_ref, dwb_ref, dbf_ref, dbb_ref):
        i = pl.program_id(0)
        dp_ref[:, :W] = (qf[...] + qb[...]).astype(dp_ref.dtype)
        dp_ref[:, W : 2 * W] = (kf[...] + kb[...]).astype(dp_ref.dtype)
        dp_ref[:, 2 * W :] = (vf[...] + vb[...]).astype(dp_ref.dtype)
        small = s_ref[...]
        dsm = jnp.zeros((tb, LANES), f32)
        parts = []
        for w_ref, b_ref, d_ref in ((wf_ref, bf_ref, df_ref), (wb_ref, bb_ref, db_ref)):
            pre = _mx(small, w_ref[...], "nn") + b_ref[...]
            dpre = d_ref[...] * (1.0 / GATE_NORMALIZER) * _sigmoid(-pre)
            dsm = dsm + _mx(dpre, w_ref[...], "nt")
            parts.append((_mx(small, dpre, "tn"), jnp.sum(dpre, axis=0, keepdims=True)))
        ds_ref[...] = dsm

        @pl.when(i == 0)
        def _():
            dwf_ref[...], dbf_ref[...] = parts[0]
            dwb_ref[...], dbb_ref[...] = parts[1]

        @pl.when(i > 0)
        def _():
            dwf_ref[...] += parts[0][0]
            dbf_ref[...] += parts[0][1]
            dwb_ref[...] += parts[1][0]
            dbb_ref[...] += parts[1][1]

    blk = pl.BlockSpec((tb, W), lambda i: (i, 0))
    vblk = pl.BlockSpec((tb, WV), lambda i: (i, 0))
    wsp = pl.BlockSpec((LANES, W), lambda i: (0, 0))
    row = pl.BlockSpec((1, W), lambda i: (0, 0))
    return pl.pallas_call(
        body,
        out_shape=(jax.ShapeDtypeStruct(dproj.shape, dproj.dtype), jax.ShapeDtypeStruct((T, LANES), f32), jax.ShapeDtypeStruct((LANES, W), f32),
                   jax.ShapeDtypeStruct((LANES, W), f32), jax.ShapeDtypeStruct((1, W), f32), jax.ShapeDtypeStruct((1, W), f32)),
        grid=(T // tb,),
        in_specs=[pl.BlockSpec((tb, LANES), lambda i: (i, SMALL_OFF // LANES)), wsp, wsp, row, row, blk, blk,
                  blk, blk, blk, blk, vblk, vblk, _ANY_SPEC],
        out_specs=(pl.BlockSpec((tb, 2 * W + WV), lambda i: (i, DP_GLA_QKV_BLK)), pl.BlockSpec((tb, LANES), lambda i: (i, 0)), wsp, wsp, row, row),
        input_output_aliases={13: 0},
        compiler_params=_cparams(("arbitrary",)),
        name="gla_prep_bwd",
    )(proj, w2f, w2b, b2f, b2b, dgk_f, dgk_b, *dqkv, dproj)


GATE_A_BLK, GATE_B_BLK, Z_BLK, G_B_BLK = 6144 // 1024, 7168 // 1024, 3072 // 1024, 8192 // 1024
DP_GATES_BLK, DP_GLA_QKV_BLK, DP_SMALL_BLK = 6144 // 2048, 4096 // 2048, SMALL_OFF // 512
_ANY_SPEC = pl.BlockSpec(memory_space=pl.ANY)


def _mid(og, obg, proj, x, target, wpg, wpl, wout, w_post, tb=256):
    T = x.shape[0]
    Dm = D_MODEL

    def body(og_ref, obg_ref, ga_ref, gb_ref, x_ref, t_ref, wpg_ref, wpl_ref, wo_ref, wp_ref,
             loss_ref, dwp_ref, dy_ref, m_ref, dout_ref, dya_ref, dyb_ref, dgates_ref, dog_ref, dobg_ref):
        i = pl.program_id(0)
        ya = _mx(og_ref[...], wpg_ref[...], "nn")
        yb = _mx(obg_ref[...], wpl_ref[...], "nn")
        sa, sb = _sigmoid(ga_ref[...]), _sigmoid(gb_ref[...])
        m = sa * ya + sb * yb
        m_ref[...] = m.astype(m_ref.dtype)
        out = _mx(m, wo_ref[...], "nn")
        r = lax.rsqrt(jnp.mean(out * out, axis=1, keepdims=True) + NORM_EPS)
        wp = wp_ref[...]
        e = x_ref[...] + out * r * wp - t_ref[...]
        dy = e * (1.0 / Dm)
        dy_ref[...] = dy
        lpart = jnp.sum(e * e, axis=0, keepdims=True)
        wpart = jnp.sum(dy * out * r, axis=0, keepdims=True)
        dyn = dy * wp
        dout = r * dyn - out * (r * r * r) * jnp.mean(dyn * out, axis=1, keepdims=True)
        dout_ref[...] = dout.astype(dout_ref.dtype)
        dm = _mx(dout, wo_ref[...], "nt")
        dya, dyb = dm * sa, dm * sb
        dya_ref[...] = dya.astype(dya_ref.dtype)
        dyb_ref[...] = dyb.astype(dyb_ref.dtype)
        dgates_ref[:, :Dm] = (dm * ya * sa * (1.0 - sa)).astype(dgates_ref.dtype)
        dgates_ref[:, Dm:] = (dm * yb * sb * (1.0 - sb)).astype(dgates_ref.dtype)
        dog_ref[...] = _mx(dya, wpg_ref[...], "nt")
        dobg_ref[...] = _mx(dyb, wpl_ref[...], "nt")

        @pl.when(i == 0)
        def _():
            loss_ref[...] = lpart
            dwp_ref[...] = wpart

        @pl.when(i > 0)
        def _():
            loss_ref[...] += lpart
            dwp_ref[...] += wpart

    blk = pl.BlockSpec((tb, Dm), lambda i: (i, 0))
    row = pl.BlockSpec((1, Dm), lambda i: (0, 0))
    wsp = pl.BlockSpec((Dm, Dm), lambda i: (0, 0))
    act = jax.ShapeDtypeStruct((T, Dm), MXU_DTYPE)
    big = jax.ShapeDtypeStruct((T, Dm), f32)
    vec = jax.ShapeDtypeStruct((1, Dm), f32)
    return pl.pallas_call(
        body,
        out_shape=(vec, vec, big, act, act, act, act, jax.ShapeDtypeStruct((T, N_CAT), MXU_DTYPE), big, big),
        grid=(T // tb,),
        in_specs=[blk, blk, pl.BlockSpec((tb, Dm), lambda i: (i, GATE_A_BLK)), pl.BlockSpec((tb, Dm), lambda i: (i, GATE_B_BLK)),
                  blk, blk, wsp, wsp, wsp, row],
        out_specs=(row, row) + (blk,) * 5 + (pl.BlockSpec((tb, 2 * Dm), lambda i: (i, DP_GATES_BLK)), blk, blk),
        compiler_params=_cparams(("arbitrary",)),
        name="mid",
    )(og, obg, proj, proj, x, target, wpg, wpl, wout, w_post)


def _heads(ref, lead, n, width, rows=slice(None)):
    return [ref[lead + (rows, slice(i * width, (i + 1) * width))] for i in range(n)]


class _GdnBatch:
    def __init__(self, q, k, v, gcol, grow, beta, nf, saved=None, sol=None):
        B, Cn = q.shape[0], q.shape[1]
        self.incl, self.strict, eye = _dir_masks(B, nf, Cn)
        A = None
        if saved is None:
            self.decay = jnp.where(self.incl, jnp.exp(jnp.where(self.incl, gcol - grow, 0.0)), 0.0)
        else:
            A, self.L, self.attn, self.decay = (saved[:, i * Cn : (i + 1) * Cn] for i in range(4))
        self.k, self.v = k, v
        gl = _last_row(gcol, nf)
        self.beta = jnp.broadcast_to(beta, q.shape)
        self.eg = jnp.broadcast_to(jnp.exp(gcol), q.shape)
        self.ekd = jnp.broadcast_to(jnp.exp(gl - gcol), q.shape)
        self.qs = q * (GDN_DIM**-0.5)
        self.kb = k * self.beta
        if saved is None:
            self.L = jnp.where(self.strict, _mx(self.kb, k, "nt") * self.decay, 0.0)
            self.attn = jnp.where(self.incl, _mx(self.qs, k, "nt") * self.decay, 0.0)
            A = _inv_unit_lower(self.L, jnp.where(eye, 1.0, 0.0).astype(f32))
        self.A2 = _split2(A)
        self.saved = jnp.concatenate([A, self.L, self.attn, self.decay], axis=1) if saved is None else saved
        self.kbg = self.kb * self.eg
        if sol is None:
            sol = _h3s(self.A2, _split2(jnp.concatenate([v * self.beta, self.kbg], axis=2)), "nn")
        self.sol = sol
        self.u = self.sol[:, :, :GDN_DIM]
        self.w = self.sol[:, :, GDN_DIM:]
        self.qd = self.qs * self.eg
        self.kd = k * self.ekd
        self.egl = jnp.exp(gl)


def _gdn_chain_inputs(qf_ref, qb_ref, gf_ref, gb_ref, cf_ref, cb_ref, b, hb, need_rows=True):
    qkv = [jnp.stack(_heads(qf_ref, (i,), hb, GDN_DIM) + _heads(qb_ref, (i,), hb, GDN_DIM)) for i in range(3)]
    cols, rows, betas = [], [], []
    for d, (g_ref, c_ref) in enumerate(((gf_ref, cf_ref), (gb_ref, cb_ref))):
        gates, gc = g_ref[...], c_ref[...]
        for hh in range(hb):
            h = b * hb + hh
            col = _lane_col(gc, h + 8 * d)
            cols.append(col)
            if need_rows:
                rows.append(jnp.broadcast_to(col, (CHUNK, LANES)).T[0:1, :])
            betas.append(_lane_col(gates, 16 + h + 8 * d))
    return qkv[0], qkv[1], qkv[2], jnp.stack(cols), (jnp.stack(rows) if need_rows else None), jnp.stack(betas)


GDN_CPS_FWD, GDN_CPS_BWD = 2, 1


def _chunk_views(refs, s_f, s_b):
    out = []
    for i, r in enumerate(refs):
        s = s_f if i % 2 == 0 else s_b
        rows = pl.ds(s * CHUNK, CHUNK)
        out.append(r.at[pl.ds(s, 1)] if len(r.shape) == 4 else (r.at[:, rows, :] if len(r.shape) == 3 else r.at[rows, :]))
    return out


def _gdn_specs(cps, hb):
    R, W = cps * CHUNK, hb * GDN_DIM
    qsp = lambda ci: pl.BlockSpec((3, R, W), lambda n, b: (0, ci(n), b))
    gsp = lambda ci: pl.BlockSpec((R, LANES), lambda n, b: (ci(n), 0))
    osp = lambda ci: pl.BlockSpec((R, W), lambda n, b: (ci(n), b))
    ssp = lambda ci: pl.BlockSpec((cps, hb, GDN_DIM, GDN_DIM), lambda n, b: (ci(n), b, 0, 0))
    asp = lambda ci: pl.BlockSpec((cps, hb, 4 * CHUNK, CHUNK), lambda n, b: (ci(n), b, 0, 0))
    usp = lambda ci: pl.BlockSpec((cps, hb, CHUNK, 2 * GDN_DIM), lambda n, b: (ci(n), b, 0, 0))
    return qsp, gsp, osp, ssp, asp, usp


def _gdn_fwd(qkvn, gates, gc):
    T = qkvn.shape[1]
    N = T // CHUNK
    nh, hb = GDN_HEADS, GDN_HB
    cps = min(GDN_CPS_FWD, N)
    NB = N // cps

    def body(*refs):
        n, b = pl.program_id(0), pl.program_id(1)
        s_scr = refs[-1]

        @pl.when(n == 0)
        def _():
            s_scr[0, pl.ds(b * hb, hb)] = jnp.zeros((hb, GDN_DIM, GDN_DIM), f32)
            s_scr[1, pl.ds(b * hb, hb)] = jnp.zeros((hb, GDN_DIM, GDN_DIM), f32)

        for s in range(cps):
            chunk(*_chunk_views(refs[:-1], s, cps - 1 - s), s_scr, b)

    def chunk(qf_ref, qb_ref, gf_ref, gb_ref, cf_ref, cb_ref, of_ref, ob_ref, sf_ref, sb_ref, af_ref, ab_ref, uf_ref, ub_ref, s_scr, b):
        hs = pl.ds(b * hb, hb)
        q, k, v, gcol, grow, beta = _gdn_chain_inputs(qf_ref, qb_ref, gf_ref, gb_ref, cf_ref, cb_ref, b, hb)
        ck = _GdnBatch(q, k, v, gcol, grow, beta, hb)
        S = jnp.concatenate([s_scr[0, hs], s_scr[1, hs]], axis=0)
        sf_ref[0], sb_ref[0] = S[:hb], S[hb:]
        af_ref[0], ab_ref[0] = ck.saved[:hb], ck.saved[hb:]
        uf_ref[0], ub_ref[0] = ck.sol[:hb], ck.sol[hb:]
        v_new = ck.u - _mx(ck.w, S, "nn")
        o = _mx(ck.qd, S, "nn") + _mx(ck.attn, v_new, "nn")
        S_new = S * ck.egl + _mx(ck.kd, v_new, "tn")
        for hh in range(hb):
            sl = slice(hh * GDN_DIM, (hh + 1) * GDN_DIM)
            of_ref[:, sl] = o[hh]
            ob_ref[:, sl] = o[hb + hh]
        s_scr[0, hs] = S_new[:hb]
        s_scr[1, hs] = S_new[hb:]

    fw, bw = (lambda n: n), (lambda n: NB - 1 - n)
    qsp, gsp, osp, ssp, asp, usp = _gdn_specs(cps, hb)
    o_shape = jax.ShapeDtypeStruct((T, nh * GDN_DIM), f32)
    s_shape = jax.ShapeDtypeStruct((N, nh, GDN_DIM, GDN_DIM), f32)
    a_shape = jax.ShapeDtypeStruct((N, nh, 4 * CHUNK, CHUNK), f32)
    u_shape = jax.ShapeDtypeStruct((N, nh, CHUNK, 2 * GDN_DIM), f32)
    return pl.pallas_call(
        body,
        out_shape=(o_shape, o_shape, s_shape, s_shape, a_shape, a_shape, u_shape, u_shape),
        grid=(NB, nh // hb),
        in_specs=[qsp(fw), qsp(bw), gsp(fw), gsp(bw), gsp(fw), gsp(bw)],
        out_specs=(osp(fw), osp(bw), ssp(fw), ssp(bw), asp(fw), asp(bw), usp(fw), usp(bw)),
        scratch_shapes=[pltpu.VMEM((2, nh, GDN_DIM, GDN_DIM), f32)],
        compiler_params=_cparams(("arbitrary", "arbitrary")),
        name="gdn_scan_fwd",
    )(qkvn, qkvn, gates, gates, gc, gc)


def _gdn_bwd(qkvn, gates, gc, st_f, st_b, a_f, a_b, u_f, u_b, do):
    T = qkvn.shape[1]
    N = T // CHUNK
    nh, hb = GDN_HEADS, GDN_HB
    nb = nh // hb
    cps = min(GDN_CPS_BWD, N)
    NB = N // cps

    def body(*refs):
        n, b = pl.program_id(0), pl.program_id(1)
        ds_scr = refs[-1]

        @pl.when(n == 0)
        def _():
            ds_scr[0, pl.ds(b * hb, hb)] = jnp.zeros((hb, GDN_DIM, GDN_DIM), f32)
            ds_scr[1, pl.ds(b * hb, hb)] = jnp.zeros((hb, GDN_DIM, GDN_DIM), f32)

        for s in range(cps):
            chunk(*_chunk_views(refs[:-1], cps - 1 - s, s), ds_scr, b)

    def chunk(qf_ref, qb_ref, gf_ref, gb_ref, cf_ref, cb_ref, sf_ref, sb_ref, af_ref, ab_ref, uf_ref, ub_ref, dof_ref, dob_ref,
              dqf_ref, dqb_ref, dgf_ref, dgb_ref, ds_scr, b):
        hs = pl.ds(b * hb, hb)

        @pl.when(b == 0)
        def _():
            dgf_ref[...] = jnp.zeros((CHUNK, LANES), f32)
            dgb_ref[...] = jnp.zeros((CHUNK, LANES), f32)

        q, k, v, gcol, grow, beta = _gdn_chain_inputs(qf_ref, qb_ref, gf_ref, gb_ref, cf_ref, cb_ref, b, hb, need_rows=False)
        ck = _GdnBatch(q, k, v, gcol, grow, beta, hb, saved=jnp.concatenate([af_ref[0], ab_ref[0]], axis=0),
                       sol=jnp.concatenate([uf_ref[0], ub_ref[0]], axis=0))
        S = jnp.concatenate([sf_ref[0], sb_ref[0]], axis=0)
        dS = jnp.concatenate([ds_scr[0, hs], ds_scr[1, hs]], axis=0)
        dov = jnp.stack(_heads(dof_ref, (), hb, GDN_DIM) + _heads(dob_ref, (), hb, GDN_DIM))
        v_new = ck.u - _mx(ck.w, S, "nn")
        d_vnew = _mx(ck.attn, dov, "tn") + _mx(ck.kd, dS, "nn")
        d_attn = jnp.where(ck.incl, _mx(dov, v_new, "nt"), 0.0)
        d_qd = _mx(dov, S, "nt")
        d_kd = _mx(v_new, dS, "nt")
        d_gl = jnp.sum(jnp.sum(dS * S, axis=2, keepdims=True), axis=1, keepdims=True) * ck.egl
        dS_new = dS * ck.egl + _mx(ck.qd, dov, "tn") - _mx(ck.w, d_vnew, "tn")
        ds_scr[0, hs] = dS_new[:hb]
        ds_scr[1, hs] = dS_new[hb:]
        d_w = -_mx(d_vnew, S, "nt")
        d_rhs = _h3s(ck.A2, _split2(jnp.concatenate([d_vnew, d_w], axis=2)), "tn")
        d_vb, d_kbg = d_rhs[:, :, :GDN_DIM], d_rhs[:, :, GDN_DIM:]
        dL = -jnp.where(ck.strict, _h3(d_rhs, ck.sol, "nt"), 0.0)
        P = dL * ck.decay
        Q = d_attn * ck.decay
        d_kb = _mx(P, k, "nn") + d_kbg * ck.eg
        d_qs = _mx(Q, k, "nn") + d_qd * ck.eg
        d_k = _mx(P, ck.kb, "tn") + _mx(Q, ck.qs, "tn") + d_kd * ck.ekd + d_kb * ck.beta
        E = dL * ck.L + d_attn * ck.attn
        kdsum = jnp.sum(d_kd * ck.kd, axis=2, keepdims=True)
        d_gc = (jnp.sum(E, axis=2, keepdims=True) - _colsum_col(E)
                + jnp.sum(d_kbg * ck.kbg, axis=2, keepdims=True)
                + jnp.sum(d_qd * ck.qd, axis=2, keepdims=True) - kdsum)
        d_gc = _add_at_last_row(d_gc, d_gl + jnp.sum(kdsum, axis=1, keepdims=True), hb)
        d_beta = jnp.sum(d_vb * v, axis=2, keepdims=True) + jnp.sum(d_kb * k, axis=2, keepdims=True)
        d_q = d_qs * (GDN_DIM**-0.5)
        d_v = d_vb * ck.beta
        lane = _iota2((CHUNK, LANES), 1)
        for d, (dq_ref, dg_ref) in enumerate(((dqf_ref, dgf_ref), (dqb_ref, dgb_ref))):
            acc = dg_ref[...]
            for hh in range(hb):
                c, h = d * hb + hh, b * hb + hh
                sl = slice(hh * GDN_DIM, (hh + 1) * GDN_DIM)
                dq_ref[0, :, sl] = d_q[c]
                dq_ref[1, :, sl] = d_k[c]
                dq_ref[2, :, sl] = d_v[c]
                acc = jnp.where(lane == h + 8 * d, d_gc[c], acc)
                acc = jnp.where(lane == 16 + h + 8 * d, d_beta[c], acc)
            dg_ref[...] = acc

        @pl.when(b == nb - 1)
        def _():
            for d, dg_ref in enumerate((dgf_ref, dgb_ref)):
                blk = dg_ref[...]
                dg_ref[...] = jnp.where(lane < 16, _exact01(_chunk_tri(CHUNK, d == 0), blk), blk)

    fw, bw = (lambda n: NB - 1 - n), (lambda n: n)
    qsp, gsp, osp, ssp, asp, usp = _gdn_specs(cps, hb)
    dq_shape = jax.ShapeDtypeStruct((3, T, nh * GDN_DIM), f32)
    dg_shape = jax.ShapeDtypeStruct((T, LANES), f32)
    return pl.pallas_call(
        body,
        out_shape=(dq_shape, dq_shape, dg_shape, dg_shape),
        grid=(NB, nb),
        in_specs=[qsp(fw), qsp(bw), gsp(fw), gsp(bw), gsp(fw), gsp(bw), ssp(fw), ssp(bw), asp(fw), asp(bw), usp(fw), usp(bw),
                  osp(fw), osp(bw)],
        out_specs=(qsp(fw), qsp(bw), gsp(fw), gsp(bw)),
        scratch_shapes=[pltpu.VMEM((2, nh, GDN_DIM, GDN_DIM), f32)],
        compiler_params=_cparams(("arbitrary", "arbitrary")),
        name="gdn_scan_bwd",
    )(qkvn, qkvn, gates, gates, gc, gc, st_f, st_b, a_f, a_b, u_f, u_b, do, do)


class _GlaBatch:
    def __init__(self, q, k, G, nf):
        B, Cn = q.shape[0], q.shape[1]
        self.incl, _, _ = _dir_masks(B, nf, Cn)
        self.eG = jnp.exp(G)
        self.enG = jnp.exp(-G)
        self.qg = q * (GLA_DK**-0.5) * self.eG
        self.kg = k * self.enG
        self.attn = jnp.where(self.incl, _mx(self.qg, self.kg, "nt"), 0.0)
        gl = _last_row(G, nf)
        self.ekd = jnp.exp(gl - G)
        self.kd = k * self.ekd
        self.egl = jnp.exp(gl)


GLA_CPS = 4


def _chunk_rows(s):
    return slice(s * CHUNK, (s + 1) * CHUNK)


def _gla_chain_inputs(refs_f, refs_b, rows_f, rows_b):
    nh = GLA_HEADS
    out = []
    for i, width in enumerate((GLA_DK, GLA_DK, GLA_DV, GLA_DK)):
        out.append(jnp.stack(_heads(refs_f[i], (), nh, width, rows_f) + _heads(refs_b[i], (), nh, width, rows_b)))
    return out


def _gla_specs(rows, ci):
    nh = GLA_HEADS
    return [
        pl.BlockSpec((rows, nh * GLA_DK), lambda n: (ci(n), 4096 // (nh * GLA_DK))),
        pl.BlockSpec((rows, nh * GLA_DK), lambda n: (ci(n), 4608 // (nh * GLA_DK))),
        pl.BlockSpec((rows, nh * GLA_DV), lambda n: (ci(n), 5120 // (nh * GLA_DV))),
        pl.BlockSpec((rows, nh * GLA_DK), lambda n: (ci(n), 0)),
    ]


def _gla_fwd(proj, G_f, G_b):
    T = proj.shape[0]
    N = T // CHUNK
    nh = GLA_HEADS
    cps = min(GLA_CPS, N)
    NB = N // cps

    def body(qf, kf, vf, gf, qb, kb, vb, gb, of_ref, ob_ref, sf_ref, sb_ref, s_scr):
        n = pl.program_id(0)

        @pl.when(n == 0)
        def _():
            s_scr[...] = jnp.zeros((2 * nh, GLA_DV, GLA_DK), f32)

        for s in range(cps):
            sb = cps - 1 - s
            q, k, v, G = _gla_chain_inputs((qf, kf, vf, gf), (qb, kb, vb, gb), _chunk_rows(s), _chunk_rows(sb))
            ck = _GlaBatch(q, k, G, nh)
            St = s_scr[...]
            sf_ref[s], sb_ref[sb] = St[:nh].astype(sf_ref.dtype), St[nh:].astype(sb_ref.dtype)
            o = _mx(ck.qg, St, "nt") + _mx(ck.attn, v, "nn")
            for i in range(nh):
                sl = slice(i * GLA_DV, (i + 1) * GLA_DV)
                of_ref[_chunk_rows(s), sl] = o[i]
                ob_ref[_chunk_rows(sb), sl] = o[nh + i]
            s_scr[...] = St * ck.egl + _mx(v, ck.kd, "tn")

    fw, bw = (lambda n: n), (lambda n: NB - 1 - n)
    osp = lambda ci: pl.BlockSpec((cps * CHUNK, nh * GLA_DV), lambda n: (ci(n), 0))
    ssp = lambda ci: pl.BlockSpec((cps, nh, GLA_DV, GLA_DK), lambda n: (ci(n), 0, 0, 0))
    o_shape = jax.ShapeDtypeStruct((T, nh * GLA_DV), f32)
    s_shape = jax.ShapeDtypeStruct((N, nh, GLA_DV, GLA_DK), MXU_DTYPE)
    return pl.pallas_call(
        body,
        out_shape=(o_shape, o_shape, s_shape, s_shape),
        grid=(NB,),
        in_specs=_gla_specs(cps * CHUNK, fw) + _gla_specs(cps * CHUNK, bw),
        out_specs=(osp(fw), osp(bw), ssp(fw), ssp(bw)),
        scratch_shapes=[pltpu.VMEM((2 * nh, GLA_DV, GLA_DK), f32)],
        compiler_params=_cparams(("arbitrary",)),
        name="gla_scan_fwd",
    )(proj, proj, proj, G_f, proj, proj, proj, G_b)


def _gla_bwd(proj, G_f, G_b, st_f, st_b, do):
    T = proj.shape[0]
    N = T // CHUNK
    nh = GLA_HEADS
    cps = min(GLA_CPS, N)
    NB = N // cps

    def body(qf, kf, vf, gf, sf_ref, dof_ref, qb, kb, vb, gb, sb_ref, dob_ref,
             dqf_ref, dkf_ref, dvf_ref, dgf_ref, dqb_ref, dkb_ref, dvb_ref, dgb_ref, ds_scr):
        n = pl.program_id(0)

        @pl.when(n == 0)
        def _():
            ds_scr[...] = jnp.zeros((2 * nh, GLA_DV, GLA_DK), f32)

        shp = (2 * nh, CHUNK, CHUNK)
        bb, ii, jj = _iota3(shp, 0), _iota3(shp, 1), _iota3(shp, 2)
        tri = jnp.where(jnp.logical_or(jnp.logical_and(bb < nh, jj >= ii), jnp.logical_and(bb >= nh, jj <= ii)), 1.0, 0.0)
        for s in range(cps):
            sf = cps - 1 - s
            rows = (_chunk_rows(sf), _chunk_rows(s))
            q, k, v, G = _gla_chain_inputs((qf, kf, vf, gf), (qb, kb, vb, gb), *rows)
            ck = _GlaBatch(q, k, G, nh)
            St = jnp.concatenate([sf_ref[sf], sb_ref[s]], axis=0).astype(f32)
            dSt = ds_scr[...]
            dov = jnp.stack(_heads(dof_ref, (), nh, GLA_DV, rows[0]) + _heads(dob_ref, (), nh, GLA_DV, rows[1]))
            d_attn = jnp.where(ck.incl, _mx(dov, v, "nt"), 0.0)
            d_qg = _mx(dov, St, "nn") + _mx(d_attn, ck.kg, "nn")
            d_kg = _mx(d_attn, ck.qg, "tn")
            d_v = _mx(ck.attn, dov, "tn") + _mx(ck.kd, dSt, "nt")
            d_kd = _mx(v, dSt, "nn")
            ds_scr[...] = dSt * ck.egl + _mx(dov, ck.qg, "tn")
            kdd = d_kd * ck.kd
            d_gl = jnp.sum(dSt * St, axis=1, keepdims=True) * ck.egl + jnp.sum(kdd, axis=1, keepdims=True)
            d_q = d_qg * ck.eG * (GLA_DK**-0.5)
            d_k = d_kg * ck.enG + d_kd * ck.ekd
            d_gk = _exact01(tri, _add_at_last_row(d_qg * ck.qg - d_kg * ck.kg - kdd, d_gl, nh))
            for d, (dq_ref, dk_ref, dv_ref, dg_ref) in enumerate(((dqf_ref, dkf_ref, dvf_ref, dgf_ref), (dqb_ref, dkb_ref, dvb_ref, dgb_ref))):
                for i in range(nh):
                    c = d * nh + i
                    ks, vs = slice(i * GLA_DK, (i + 1) * GLA_DK), slice(i * GLA_DV, (i + 1) * GLA_DV)
                    dq_ref[rows[d], ks] = d_q[c]
                    dk_ref[rows[d], ks] = d_k[c]
                    dv_ref[rows[d], vs] = d_v[c]
                    dg_ref[rows[d], ks] = d_gk[c]

    R = cps * CHUNK

    def in_specs(ci):
        return _gla_specs(R, ci) + [
            pl.BlockSpec((cps, nh, GLA_DV, GLA_DK), lambda n: (ci(n), 0, 0, 0)),
            pl.BlockSpec((R, nh * GLA_DV), lambda n: (ci(n), 0)),
        ]

    def out_specs(ci):
        ksp = pl.BlockSpec((R, nh * GLA_DK), lambda n: (ci(n), 0))
        return [ksp, ksp, pl.BlockSpec((R, nh * GLA_DV), lambda n: (ci(n), 0)), ksp]

    fw, bw = (lambda n: NB - 1 - n), (lambda n: n)
    k_shape = jax.ShapeDtypeStruct((T, nh * GLA_DK), f32)
    v_shape = jax.ShapeDtypeStruct((T, nh * GLA_DV), f32)
    return pl.pallas_call(
        body,
        out_shape=(k_shape, k_shape, v_shape, k_shape) * 2,
        grid=(NB,),
        in_specs=in_specs(fw) + in_specs(bw),
        out_specs=tuple(out_specs(fw) + out_specs(bw)),
        scratch_shapes=[pltpu.VMEM((2 * nh, GLA_DV, GLA_DK), f32)],
        compiler_params=_cparams(("arbitrary",)),
        name="gla_scan_bwd",
    )(proj, proj, proj, G_f, st_f, do, proj, proj, proj, G_b, st_b, do)


def _local_step(x, target, wcat, conv_w, avec, dvec, gdn_norm_w, w2f, w2b, b2f, b2b, gla_norm_w, proj_weights, w_pre, w_post,
                send_proj_grads=None, send_in_grads=None):
    h, h_t = _rms_pre(x, w_pre)
    proj = _matmul(h, wcat, "nt", f32, "proj_in", 2048, 512, 1024)
    qkvn = _gdn_prep(proj, conv_w)
    gates, gc, G_f, G_b = _gate_prep(proj, avec, dvec, w2f, w2b, b2f, b2b)
    oa_f, oa_b, sa_f, sa_b, inv_f, inv_b, sol_f, sol_b = _gdn_fwd(qkvn, gates, gc)
    og = _out_norm(oa_f, oa_b, proj, Z_BLK, gdn_norm_w, GDN_DIM, "gdn_out")
    ob_f, ob_b, sb_f, sb_b = _gla_fwd(proj, G_f, G_b)
    obg = _out_norm(ob_f, ob_b, proj, G_B_BLK, gla_norm_w, GLA_DV, "gla_out")
    wpg, wpl, wout = proj_weights(obg)
    (loss_row, d_wpost, dy, m, dout, dya, dyb, dproj, dog, dobg) = _mid(og, obg, proj, x, target, wpg, wpl, wout, w_post)

    dw_p = _dw_proj(((og, dya), (obg, dyb), (m, dout)), "dw_proj")
    d_wpg, d_wpl, d_wout = (dw_p[:, i].reshape(D_MODEL, D_MODEL) for i in range(3))
    if send_proj_grads is not None:
        gla_norm_w = gla_norm_w + send_proj_grads(dw_p)

    do_b, dproj, d_gla_norm = _out_norm_bwd(dobg, ob_f, ob_b, proj, G_B_BLK, gla_norm_w, GLA_DV, "gla_out_bwd", dproj)
    dq_f, dk_f, dv_f, dgk_f, dq_b, dk_b, dv_b, dgk_b = _gla_bwd(proj, G_f, G_b, sb_f, sb_b, do_b)
    dproj, dsmall_gla, d_w2f, d_w2b, d_b2f, d_b2b = _gla_prep_bwd(proj, w2f, w2b, b2f, b2b, dgk_f, dgk_b,
                                                                 (dq_f, dq_b, dk_f, dk_b, dv_f, dv_b), dproj)

    do_a, dproj, d_gdn_norm = _out_norm_bwd(dog, oa_f, oa_b, proj, Z_BLK, gdn_norm_w, GDN_DIM, "gdn_out_bwd", dproj)
    dqkv_f, dqkv_b, dg_f, dg_b = _gdn_bwd(qkvn, gates, gc, sa_f, sa_b, inv_f, inv_b, sol_f, sol_b, do_a)
    dproj, d_conv = _gdn_prep_bwd(proj, conv_w, dqkv_f, dqkv_b, dproj)
    dproj, d_alog, d_dtb = _gdn_gates_bwd(proj, avec, dvec, dg_f, dg_b, dsmall_gla, dproj)

    d_wcat = _matmul(h_t, dproj, "nn", MXU_DTYPE, "dw_in", 1024, 512, 2048, out_t=True)
    started = None if send_in_grads is None else send_in_grads(d_wcat)
    dh = _matmul(dproj, wcat, "nn", f32, "dh", 512, 1024, N_CAT, after=started)
    grad_x, d_wpre = _rms_pre_bwd(x, w_pre, dh, dy)
    return dict(loss_row=loss_row, grad_x=grad_x, d_wcat=d_wcat, d_conv=d_conv, d_alog=d_alog, d_dtb=d_dtb,
                d_gdn_norm=d_gdn_norm, d_wpg=d_wpg, d_w2f=d_w2f, d_w2b=d_w2b, d_b2f=d_b2f, d_b2b=d_b2b,
                d_gla_norm=d_gla_norm, d_wpl=d_wpl, d_wout=d_wout, d_wpost=d_wpost, d_wpre=d_wpre)


def _to_cat(wf):
    pad = jnp.zeros((N_CAT - N_IN,) + wf.shape[1:], wf.dtype)
    return jnp.concatenate([wf[:4096], wf[4128:6176], wf[7232:9280], wf[6176:7200], wf[4096:4128], wf[7200:7232], pad], axis=0)


def _from_cat(wc):
    return jnp.concatenate([wc[:4096], wc[9216:9248], wc[4096:6144], wc[8192:9216], wc[9248:9280], wc[6144:8192]], axis=0)


def _lane_row(*pieces):
    row = jnp.concatenate(pieces, axis=1)
    return jnp.pad(row, ((0, 0), (0, LANES - row.shape[1])))


def _pad_w2(w2, row0):
    return jnp.pad(w2, ((row0, LANES - row0 - w2.shape[0]), (0, 0)))


SM_ROWS, SM_COLS = 32, 384


def _pack_small_shard(conv, w2f, w2b):
    lead = conv.shape[:-2]
    z = lambda r, c: jnp.zeros(lead + (r, c), f32)
    top = jnp.concatenate([conv, z(8 - CONV_K, SM_COLS)], axis=-2)
    mid = jnp.concatenate([w2f, w2b, z(16, SM_COLS - 128)], axis=-1)
    return jnp.concatenate([top, mid, z(SM_ROWS - 24, SM_COLS)], axis=-2)


SMALL_REGION = {
    "ln_pre_w": (0, 1, 0, 1024), "ln_post_w": (1, 1, 0, 1024), "gk_b2_fwd": (2, 1, 0, 512), "gk_b2_bwd": (2, 1, 512, 512),
    "gla_norm_w": (3, 1, 0, 256), "gdn_norm_w": (3, 1, 256, 128), "a_log_fwd": (4, 1, 0, 8), "a_log_bwd": (5, 1, 0, 8),
    "dt_bias_fwd": (6, 1, 0, 8), "dt_bias_bwd": (7, 1, 0, 8), "conv_w": (8, CONV_K, 0, SM_COLS),
    "gk_w2_fwd": (16, 16, 0, 64), "gk_w2_bwd": (16, 16, 128, 64),
}
SMALL_NAMES = tuple(SMALL_REGION)
SMALL_SHARDED = ("conv_w", "gk_w2_fwd", "gk_w2_bwd")
PACK_ROWS, PACK_COLS = 32, 1024
LOSS_AT = (13, 0)


def _region(name):
    r0, nr, l0, nl = SMALL_REGION[name]
    return slice(r0, r0 + nr), slice(l0, l0 + nl)


def _pack_small_grads(G):
    names = ("d_wpre", "d_wpost", "d_b2f", "d_b2b", "d_gla_norm", "d_gdn_norm", "d_alog", "d_dtb", "loss_row", "d_conv", "d_w2f", "d_w2b")

    def body(wpre, wpost, b2f, b2b, glan, gdnn, alog, dtb, lrow, conv, w2f, w2b, o_ref):
        o_ref[...] = jnp.zeros(o_ref.shape, f32)
        loss = 0.5 * jnp.sum(lrow[...], axis=1, keepdims=True) * (1.0 / D_MODEL)
        whole = {"ln_pre_w": wpre, "ln_post_w": wpost, "gk_b2_fwd": b2f, "gk_b2_bwd": b2b, "gla_norm_w": glan, "gdn_norm_w": gdnn}
        for j in range(N_DEV):
            for name, ref in whole.items():
                o_ref[(j,) + _region(name)] = ref[...]
            o_ref[(j,) + _region("a_log_fwd")] = alog[0:1, 0:8]
            o_ref[(j,) + _region("a_log_bwd")] = alog[0:1, 8:16]
            o_ref[(j,) + _region("dt_bias_fwd")] = dtb[0:1, 0:8]
            o_ref[(j,) + _region("dt_bias_bwd")] = dtb[0:1, 8:16]
            o_ref[(j,) + _region("conv_w")] = conv[:, j * SM_COLS : (j + 1) * SM_COLS]
            o_ref[(j,) + _region("gk_w2_fwd")] = w2f[32:48, j * 64 : (j + 1) * 64]
            o_ref[(j,) + _region("gk_w2_bwd")] = w2b[48:64, j * 64 : (j + 1) * 64]
            o_ref[j, LOSS_AT[0] : LOSS_AT[0] + 1, LOSS_AT[1] : LOSS_AT[1] + 1] = loss

    return pl.pallas_call(
        body, out_shape=jax.ShapeDtypeStruct((N_DEV, PACK_ROWS, PACK_COLS), f32), name="pack_small_grads",
    )(*[G[n] for n in names])


def _small_adam(land, own, W, M, V):
    bc1 = 1.0 - ADAM_B1**ADAM_STEP
    bc2 = 1.0 - ADAM_B2**ADAM_STEP
    n = len(SMALL_NAMES)

    def body(land_ref, own_ref, *refs):
        w_refs, m_refs, v_refs = refs[0:n], refs[n : 2 * n], refs[2 * n : 3 * n]
        outs, loss_ref, g_scr = refs[3 * n : 7 * n], refs[7 * n], refs[7 * n + 1]
        me = _my_device()
        g = jnp.where(me == 0, own_ref[0], land_ref[0])
        for j in range(1, N_DEV):
            g = g + jnp.where(me == j, own_ref[j], land_ref[j])
        g_scr[...] = g
        loss_ref[...] = g_scr[LOSS_AT[0] : LOSS_AT[0] + 1, LOSS_AT[1] : LOSS_AT[1] + 1]
        for i, name in enumerate(SMALL_NAMES):
            gp = g_scr[_region(name)]
            lead = (0,) if name in SMALL_SHARDED else (Ellipsis,)
            w, m, v = w_refs[i][lead], m_refs[i][lead], v_refs[i][lead]
            m2 = ADAM_B1 * m + (1.0 - ADAM_B1) * gp
            v2 = ADAM_B2 * v + (1.0 - ADAM_B2) * (gp * gp)
            delta = -ADAM_LR * ((m2 / bc1) / (jnp.sqrt(v2 / bc2) + ADAM_EPS) + ADAM_WD * w)
            for k, val in enumerate((gp, delta, m2, v2)):
                outs[4 * i + k][lead] = val

    shapes = [jax.ShapeDtypeStruct(W[name].shape, f32) for name in SMALL_NAMES for _ in range(4)]
    res = pl.pallas_call(
        body,
        out_shape=tuple(shapes) + (jax.ShapeDtypeStruct((1, 1), f32),),
        scratch_shapes=[pltpu.VMEM((PACK_ROWS, PACK_COLS), f32)],
        name="adam_small",
    )(land, own, *[P[name] for P in (W, M, V) for name in SMALL_NAMES])
    return {name: res[4 * i : 4 * i + 4] for i, name in enumerate(SMALL_NAMES)}, res[4 * n]


_MESH = pl.DeviceIdType.MESH


N_BIG_COPIES, N_SMALL_COPIES = 9, 7


def _gather_weights(w_in_s, sm_s):
    shapes = (w_in_s.shape, sm_s.shape)
    dtypes = (MXU_DTYPE, f32)

    def body(win_ref, sm_ref, gin_ref, gsm_ref, own_ref, send_sems, recv_sems, own_sem):
        x, y, c = lax.axis_index("x"), lax.axis_index("y"), lax.axis_index("c")
        me, sibling = (x, y, c), (x, y, 1 - c)
        chips = [(1 - x, y), (x, 1 - y), (1 - x, 1 - y)]

        def idx(px, py, pc):
            return 4 * px + 2 * py + pc

        for r in range(w_in_s.shape[1] // LANES):
            cols = slice(r * LANES, (r + 1) * LANES)
            own_ref[:, cols] = win_ref[:, cols].astype(MXU_DTYPE)
        keep_own = pltpu.make_async_copy(own_ref, gin_ref.at[idx(*me)], own_sem.at[0])
        keep_own.start()
        gsm_ref[idx(*me)] = sm_ref[...]

        def copy(sem, blk, to, src=None):
            return pltpu.make_async_remote_copy(src_ref=blk if src is None else src, dst_ref=blk, send_sem=send_sems.at[sem],
                                                recv_sem=recv_sems.at[sem], device_id=to, device_id_type=_MESH)

        def small(k, block, to):
            return copy(N_BIG_COPIES + k, gsm_ref.at[idx(*block)], to)

        half = w_in_s.shape[1] // 2
        xn, yn, dg = (1 - x, y), (x, 1 - y), (1 - x, 1 - y)

        def big(sem, block, to, part=None, src=None):
            blk = gin_ref.at[idx(*block)]
            return copy(sem, blk if part is None else blk.at[:, pl.ds(part * half, half)], to, src)

        started = [big(0, me, sibling, src=own_ref), big(1, me, (*xn, c), src=own_ref), big(2, me, (*yn, c), src=own_ref),
                   small(0, me, sibling)]
        started += [small(1 + j, me, (*chip, c)) for j, chip in enumerate(chips)]
        for cp in started:
            cp.start()

        def then(arrived, forwards):
            arrived.wait_recv()
            for cp in forwards:
                cp.start()
            started.extend(forwards)

        then(big(1, (*xn, c), me), [big(3, (*xn, c), (*yn, c), 1), big(5, (*xn, c), sibling)])
        then(big(2, (*yn, c), me), [big(4, (*yn, c), (*xn, c), 0), big(6, (*yn, c), sibling)])
        then(big(4, (*dg, c), me, 0), [big(7, (*dg, c), sibling, 0)])
        then(big(3, (*dg, c), me, 1), [big(8, (*dg, c), sibling, 1)])
        for j, chip in enumerate(chips):
            then(small(1 + j, (*chip, c), me), [small(4 + j, (*chip, c), sibling)])
        big(0, sibling, me).wait_recv()
        big(5, (*xn, 1 - c), me).wait_recv()
        big(6, (*yn, 1 - c), me).wait_recv()
        big(7, (*dg, 1 - c), me, 0).wait_recv()
        big(8, (*dg, 1 - c), me, 1).wait_recv()
        small(0, sibling, me).wait_recv()
        for j, chip in enumerate(chips):
            small(4 + j, (*chip, 1 - c), me).wait_recv()
        for cp in started:
            cp.wait_send()
        keep_own.wait()

    vm = pl.BlockSpec(memory_space=pltpu.VMEM)
    n_copies = N_BIG_COPIES + N_SMALL_COPIES
    return pl.pallas_call(
        body,
        out_shape=tuple(jax.ShapeDtypeStruct((N_DEV,) + s, d) for s, d in zip(shapes, dtypes)),
        in_specs=[vm, vm],
        out_specs=(_HBM_SPEC, vm),
        scratch_shapes=[pltpu.VMEM(w_in_s.shape, MXU_DTYPE), pltpu.SemaphoreType.DMA((n_copies,)), pltpu.SemaphoreType.DMA((n_copies,)),
                        pltpu.SemaphoreType.DMA((1,))],
        compiler_params=pltpu.CompilerParams(vmem_limit_bytes=VMEM_LIMIT),
        name="gather_weights",
    )(w_in_s, sm_s)


N_CHIP = 4
_EFFECT = pltpu.SideEffectType.DATAFLOW_SIDE_EFFECTING
_HBM_SPEC = pl.BlockSpec(memory_space=pltpu.HBM)
_SEM_SPEC = pl.BlockSpec(memory_space=pltpu.SEMAPHORE)


def _relation_peer(k, x, y, c):
    px = (1 - x) if (k & 4) else x
    py = (1 - y) if (k & 2) else y
    pc = (1 - c) if (k & 1) else c
    return (px, py, pc), 4 * px + 2 * py + pc


_PLAN_COPIES = {"gather": N_DEV - 1, "blocks": N_DEV - 1, "sibling": 4, "chips": 3}
_PLAN_SLOTS = {"gather": N_DEV, "blocks": N_DEV, "sibling": 4, "chips": 4}


def _plan_copies(plan, src_ref, land_ref, send_sems, recv_sems, arrival):
    x, y, c = lax.axis_index("x"), lax.axis_index("y"), lax.axis_index("c")
    me, my_chip = 4 * x + 2 * y + c, 2 * x + y
    out = []

    def add(i, src, there, here, dev):
        out.append(pltpu.make_async_remote_copy(src_ref=src, dst_ref=land_ref.at[here if arrival else there], send_sem=send_sems.at[i],
                                                recv_sem=recv_sems.at[i], device_id=dev, device_id_type=_MESH))

    if plan in ("gather", "blocks"):
        for k in range(1, N_DEV):
            dev, p = _relation_peer(k, x, y, c)
            add(k - 1, src_ref.at[p] if plan == "blocks" else src_ref, me, p, dev)
    elif plan == "sibling":
        for j in range(4):
            add(j, src_ref.at[2 * j + 1 - c], j, j, (x, y, 1 - c))
    else:
        for i, k in enumerate((2, 4, 6)):
            (px, py, pc), _ = _relation_peer(k, x, y, c)
            add(i, src_ref.at[2 * px + py], my_chip, 2 * px + py, (px, py, pc))
    return out


def _exchange_start(src, plan, name, after=None):
    n = _PLAN_COPIES[plan]
    land = lax.empty((_PLAN_SLOTS[plan],) + (src.shape if plan == "gather" else src.shape[1:]), src.dtype)
    extra = [] if after is None else [after]

    def body(src_ref, land_ref, *rest):
        send_sems, recv_sems, _, _, token = rest[len(extra) :]
        for cp in _plan_copies(plan, src_ref, land_ref, send_sems, recv_sems, False):
            cp.start()
        token[...] = jnp.zeros_like(token)

    return pl.pallas_call(
        body,
        name=name,
        out_shape=(pltpu.SemaphoreType.DMA((n,)), pltpu.SemaphoreType.DMA((n,)), pltpu.HBM(src.shape, src.dtype),
                   pltpu.HBM(land.shape, land.dtype), jax.ShapeDtypeStruct((8, LANES), f32)),
        in_specs=(_HBM_SPEC, _HBM_SPEC) + (_ANY_SPEC,) * len(extra),
        out_specs=(_SEM_SPEC, _SEM_SPEC, _HBM_SPEC, _HBM_SPEC, pl.BlockSpec(memory_space=pltpu.VMEM)),
        input_output_aliases={0: 2, 1: 3},
        compiler_params=pltpu.CompilerParams(has_side_effects=_EFFECT),
    )(pltpu.with_memory_space_constraint(src, pltpu.HBM), pltpu.with_memory_space_constraint(land, pltpu.HBM), *extra)


def _exchange_wait(started, after, plan, name):
    send_sems, recv_sems, src_thru, land_thru, _ = started

    def body(src_ref, land_ref, send_sems, recv_sems, after_ref, src_dead, got_ref):
        for cp in _plan_copies(plan, src_ref, land_ref, send_sems, recv_sems, True):
            cp.wait_send()
            cp.wait_recv()

    return pl.pallas_call(
        body,
        name=name,
        out_shape=(pltpu.HBM(src_thru.shape, src_thru.dtype), pltpu.HBM(land_thru.shape, land_thru.dtype)),
        in_specs=(_HBM_SPEC, _HBM_SPEC, _SEM_SPEC, _SEM_SPEC, pl.BlockSpec(memory_space=pl.ANY)),
        out_specs=(_HBM_SPEC, _HBM_SPEC),
        input_output_aliases={0: 0, 1: 1},
        compiler_params=pltpu.CompilerParams(has_side_effects=_EFFECT),
    )(src_thru, land_thru, send_sems, recv_sems, after)


def _pair_sum(g, r, name, tc=LANES):
    _, R, Cc = g.shape

    def body(g_ref, r_ref, o_ref):
        o_ref[...] = (g_ref[...].astype(f32) + r_ref[...].astype(f32)).astype(o_ref.dtype)

    return pl.pallas_call(
        body,
        out_shape=jax.ShapeDtypeStruct(r.shape, r.dtype),
        grid=(N_CHIP, Cc // tc),
        in_specs=[pl.BlockSpec((1, R, tc), lambda k, i: (2 * k + lax.axis_index("c"), 0, i)), pl.BlockSpec((1, R, tc), lambda k, i: (k, 0, i))],
        out_specs=pl.BlockSpec((1, R, tc), lambda k, i: (k, 0, i)),
        compiler_params=_cparams(("parallel", "parallel")),
        name=name,
    )(g, r)


def _sum_adam(parts, w, m, v, name, tb, tc=None, own=None, own_slot=None):
    R, Cc = w.shape
    n_parts = parts.shape[0]
    tb = R if tc else min(tb, R)
    tc = tc or Cc
    assert R % tb == 0 and Cc % tc == 0
    bc1 = 1.0 - ADAM_B1**ADAM_STEP
    bc2 = 1.0 - ADAM_B2**ADAM_STEP
    extra = [] if own is None else [own]

    def body(p_ref, w_ref, m_ref, v_ref, *rest):
        g_ref, d_ref, nm_ref, nv_ref = rest[len(extra) :]

        def part(j):
            pj = p_ref[j].astype(f32)
            return pj if own is None else jnp.where(own_slot() == j, rest[0][0].astype(f32), pj)

        g = part(0)
        for j in range(1, n_parts):
            g = g + part(j)
        g_ref[...] = g
        m2 = ADAM_B1 * m_ref[...] + (1.0 - ADAM_B1) * g
        v2 = ADAM_B2 * v_ref[...] + (1.0 - ADAM_B2) * (g * g)
        nm_ref[...] = m2
        nv_ref[...] = v2
        d_ref[...] = -ADAM_LR * ((m2 / bc1) / (jnp.sqrt(v2 / bc2) + ADAM_EPS) + ADAM_WD * w_ref[...])

    blk = pl.BlockSpec((tb, tc), lambda i, j: (i, j))
    o = jax.ShapeDtypeStruct((R, Cc), f32)
    return pl.pallas_call(
        body,
        out_shape=(o, o, o, o),
        grid=(R // tb, Cc // tc),
        in_specs=[pl.BlockSpec((n_parts, tb, tc), lambda i, j: (0, i, j)), blk, blk, blk]
        + [pl.BlockSpec((1, tb, tc), lambda i, j: (own_slot(), i, j))] * len(extra),
        out_specs=(blk, blk, blk, blk),
        compiler_params=_cparams(("parallel", "parallel")),
        name=name,
    )(parts, w, m, v, *extra)


def _my_device():
    return 4 * lax.axis_index("x") + 2 * lax.axis_index("y") + lax.axis_index("c")


def _my_chip():
    return 2 * lax.axis_index("x") + lax.axis_index("y")


_WEIGHTS = ["ln_pre_w", "w_in", "conv_w", "a_log_fwd", "a_log_bwd", "dt_bias_fwd", "dt_bias_bwd", "gdn_norm_w", "w_proj_gdn",
            "gk_w2_fwd", "gk_b2_fwd", "gk_w2_bwd", "gk_b2_bwd", "gla_norm_w", "w_proj_gla", "w_out", "ln_post_w"]


def kernel(x, ln_pre_w, w_in, conv_w, a_log_fwd, a_log_bwd, dt_bias_fwd, dt_bias_bwd, gdn_norm_w, w_proj_gdn, gk_w2_fwd, gk_b2_fwd, gk_w2_bwd, gk_b2_bwd, gla_norm_w, w_proj_gla, w_out, ln_post_w, loss_target, m_ln_pre_w, m_w_in, m_conv_w, m_a_log_fwd, m_a_log_bwd, m_dt_bias_fwd, m_dt_bias_bwd, m_gdn_norm_w, m_w_proj_gdn, m_gk_w2_fwd, m_gk_b2_fwd, m_gk_w2_bwd, m_gk_b2_bwd, m_gla_norm_w, m_w_proj_gla, m_w_out, m_ln_post_w, v_ln_pre_w, v_w_in, v_conv_w, v_a_log_fwd, v_a_log_bwd, v_dt_bias_fwd, v_dt_bias_bwd, v_gdn_norm_w, v_w_proj_gdn, v_gk_w2_fwd, v_gk_b2_fwd, v_gk_w2_bwd, v_gk_b2_bwd, v_gla_norm_w, v_w_proj_gla, v_w_out, v_ln_post_w):
    args = locals()
    W = {n: args[n] for n in _WEIGHTS}
    M = {n: args["m_" + n] for n in _WEIGHTS}
    V = {n: args["v_" + n] for n in _WEIGHTS}

    wp_stack = lambda P: jnp.concatenate([P["w_proj_gdn"], P["w_proj_gla"], P["w_out"]], axis=0)
    sm_pack = lambda P: _pack_small_shard(P["conv_w"][0], P["gk_w2_fwd"][0], P["gk_w2_bwd"][0])
    me = 4 * lax.axis_index("x") + 2 * lax.axis_index("y") + lax.axis_index("c")
    w_in_t, m_in_t, v_in_t = (jnp.transpose(a[0]) for a in (w_in, m_w_in, v_w_in))
    g_in, g_sm = _gather_weights(w_in_t, sm_pack(W))
    wp_own = wp_stack(W).astype(MXU_DTYPE)
    wp_started = _exchange_start(wp_own, "gather", "gather_proj_start", after=g_sm)

    def proj_weights(after):
        own, land = _exchange_wait(wp_started, after, "gather", "gather_proj_wait")
        g_p = lax.dynamic_update_slice(land, own[None], (me, 0, 0, 0))
        return tuple(g_p[:, i].reshape(D_MODEL, D_MODEL) for i in range(3))

    rows_p = 3 * D_MODEL // N_DEV
    sent = {}

    def send_proj_grads(dw_p):
        sent["p_p"] = dw_p.reshape(N_DEV, rows_p, D_MODEL)
        sent["started"] = _exchange_start(sent["p_p"], "blocks", "exchange_proj_start")
        return sent["started"][4][0:1, 0:1]

    def send_in_grads(d_wcat):
        p_in = _from_cat(d_wcat).reshape(N_DEV, SHARD_IN, D_MODEL)
        sib = _exchange_start(p_in, "sibling", "exchange_in_sibling_start")
        p_in, sib_land = _exchange_wait(sib, sib[4], "sibling", "exchange_in_sibling_wait")
        sent["chips"] = _exchange_start(_pair_sum(p_in, sib_land, "pair_sum_w_in", tc=512), "chips", "exchange_in_chips_start")
        p_p, land_p = _exchange_wait(sent["started"], sent["chips"][4], "blocks", "exchange_proj_wait")
        sent["o_p"] = _sum_adam(land_p, wp_stack(W).reshape(rows_p, D_MODEL), wp_stack(M).reshape(rows_p, D_MODEL),
                                wp_stack(V).reshape(rows_p, D_MODEL), "adam_w_proj", 128, own=p_p, own_slot=_my_device)
        return sent["o_p"][0]

    wcat = _to_cat(g_in.reshape(N_IN, D_MODEL))
    conv_full = g_sm[:, 0:CONV_K, :].transpose(1, 0, 2).reshape(CONV_K, N_DEV * SM_COLS)
    w2f_full = g_sm[:, 8:24, 0:64].transpose(1, 0, 2).reshape(16, 512)
    w2b_full = g_sm[:, 8:24, 64:128].transpose(1, 0, 2).reshape(16, 512)

    avec = _lane_row(a_log_fwd, a_log_bwd)
    dvec = _lane_row(dt_bias_fwd, dt_bias_bwd)
    G = _local_step(x[0], loss_target[0], wcat, conv_full, avec, dvec, gdn_norm_w, _pad_w2(w2f_full, 32), _pad_w2(w2b_full, 48),
                    gk_b2_fwd, gk_b2_bwd, gla_norm_w, proj_weights, ln_pre_w + wp_started[4][0:1, 0:1], ln_post_w, send_proj_grads,
                    send_in_grads)

    small_started = _exchange_start(_pack_small_grads(G), "blocks", "exchange_small_start")
    q_in, land_in = _exchange_wait(sent["chips"], small_started[4], "chips", "exchange_in_chips_wait")
    o_in = _sum_adam(land_in, w_in_t, m_in_t, v_in_t, "adam_w_in", SHARD_IN, tc=256, own=q_in, own_slot=_my_chip)
    o_p = sent["o_p"]
    p_small, land_small = _exchange_wait(small_started, o_in[0], "blocks", "exchange_small_wait")
    o_small, loss = _small_adam(land_small, p_small, W, M, V)

    res = []
    for kind in range(4):
        per = {name: o_small[name][kind] for name in SMALL_NAMES}
        per["w_in"] = jnp.transpose(o_in[kind])[None]
        pp = o_p[kind].reshape(3, 1, D_MODEL // N_DEV, D_MODEL)
        per["w_proj_gdn"], per["w_proj_gla"], per["w_out"] = pp[0], pp[1], pp[2]
        res.append([per[n] for n in _WEIGHTS])
    return (loss[0, 0], G["grad_x"][None], *res[0], *res[1], *res[2], *res[3])
```

```python
import jax
import jax.numpy as jnp
from jax import lax
from jax.experimental import pallas as pl
from jax.experimental.pallas import tpu as pltpu

f32 = jnp.float32
MXU_DTYPE = jnp.bfloat16

D_MODEL = 1024
CHUNK = 64
NORM_EPS = 1e-6
GDN_HEADS, GDN_DIM = 8, 128
GLA_HEADS, GLA_DK, GLA_DV = 4, 128, 256
GATE_NORMALIZER = 16.0
CONV_K = 5
N_IN = 9280
N_DEV = 8
SHARD_IN = N_IN // N_DEV

N_CAT = 9728
SMALL_OFF = 9216
LANES = 128

ADAM_LR, ADAM_B1, ADAM_B2, ADAM_EPS, ADAM_WD, ADAM_STEP = 0.001, 0.9, 0.999, 1e-08, 0.01, 10

VMEM_LIMIT = 56 * 1024 * 1024


def _cparams(sem=None):
    return pltpu.CompilerParams(dimension_semantics=sem, vmem_limit_bytes=VMEM_LIMIT)


_DN = {"nn": (((1,), (0,)), ((), ())), "nt": (((1,), (1,)), ((), ())), "tn": (((0,), (0,)), ((), ()))}
_BDN = {"nn": (((2,), (1,)), ((0,), (0,))), "nt": (((2,), (2,)), ((0,), (0,))), "tn": (((1,), (1,)), ((0,), (0,)))}


def _dot16(a, b, kind):
    return lax.dot_general(a, b, (_DN if a.ndim == 2 else _BDN)[kind], preferred_element_type=f32)


def _mx(a, b, kind):
    return _dot16(a.astype(MXU_DTYPE), b.astype(MXU_DTYPE), kind)


def _split2(a):
    hi = a.astype(jnp.bfloat16)
    return hi, (a - hi.astype(f32)).astype(jnp.bfloat16)


def _split3(a):
    p1 = a.astype(jnp.bfloat16)
    r = a - p1.astype(f32)
    p2 = r.astype(jnp.bfloat16)
    return p1, p2, (r - p2.astype(f32)).astype(jnp.bfloat16)


def _h3s(a2, b2, kind):
    (ah, al), (bh, bl) = a2, b2
    return _dot16(ah, bh, kind) + (_dot16(ah, bl, kind) + _dot16(al, bh, kind))


def _h3(a, b, kind):
    return _h3s(_split2(a), _split2(b), kind)


def _exact01(t01, x, kind="nn"):
    t = t01.astype(jnp.bfloat16)
    x1, x2, x3 = _split3(x)
    return _dot16(t, x1, kind) + (_dot16(t, x2, kind) + _dot16(t, x3, kind))


def _colsum_col(e):
    ones = jnp.ones(e.shape[:-1] + (LANES,), jnp.bfloat16)
    e1, e2, e3 = _split3(e)
    return (_dot16(e1, ones, "tn") + (_dot16(e2, ones, "tn") + _dot16(e3, ones, "tn")))[..., 0:1]


def _iota3(shape, axis):
    return lax.broadcasted_iota(jnp.int32, shape, axis)


def _dir_masks(B, nf, Cn):
    shp = (B, Cn, Cn)
    bb, ii, jj = _iota3(shp, 0), _iota3(shp, 1), _iota3(shp, 2)
    fwd = bb < nf
    rev = jnp.logical_not(fwd)
    incl = jnp.logical_or(jnp.logical_and(fwd, ii >= jj), jnp.logical_and(rev, ii <= jj))
    strict = jnp.logical_and(incl, ii != jj)
    return incl, strict, ii == jj


def _last_row(x, nf):
    B, Cn = x.shape[0], x.shape[1]
    fwd = _iota3((B, 1, 1), 0) < nf
    return jnp.where(fwd, x[:, Cn - 1 : Cn, :], x[:, 0:1, :])


def _add_at_last_row(x, val, nf):
    B, Cn = x.shape[0], x.shape[1]
    bb, rr = _iota3((B, Cn, 1), 0), _iota3((B, Cn, 1), 1)
    at = jnp.logical_or(jnp.logical_and(bb < nf, rr == Cn - 1), jnp.logical_and(bb >= nf, rr == 0))
    return x + jnp.where(at, val, 0.0)


def _sigmoid(x):
    return jax.nn.sigmoid(x)


def _silu(x):
    return x * _sigmoid(x)


def _silu_grad(x):
    s = _sigmoid(x)
    return s * (1.0 + x * (1.0 - s))


def _softplus(x):
    u = jnp.exp(-jnp.abs(x))
    l1p = jnp.where(u < 1e-3, u * (1.0 - u * (0.5 - u * (1.0 / 3.0))), jnp.log(1.0 + u))
    return jnp.maximum(x, 0.0) + l1p


def _mxr(x):
    return x.astype(MXU_DTYPE).astype(f32)


def _iota2(shape, axis):
    return lax.broadcasted_iota(jnp.int32, shape, axis)


def _matmul(a, b, kind, out_dtype, name, tm, tn, tk, out_t=False, after=None):
    extra = [] if after is None else [after]
    if kind == "nn":
        (M, K), N = a.shape, b.shape[1]
    elif kind == "nt":
        (M, K), N = a.shape, b.shape[0]
    else:
        (K, M), N = a.shape, b.shape[1]
    tm, tn, tk = min(tm, M), min(tn, N), min(tk, K)
    assert M % tm == 0 and N % tn == 0 and K % tk == 0, (name, M, N, K)
    nk = K // tk
    if kind == "tn":
        a_spec = pl.BlockSpec((tk, tm), lambda i, j, k: (k, i))
    else:
        a_spec = pl.BlockSpec((tm, tk), lambda i, j, k: (i, k))
    b_mode = dict(pipeline_mode=pl.Buffered(1)) if (tn == N and tk == K) else {}
    if kind == "nt":
        b_spec = pl.BlockSpec((tn, tk), lambda i, j, k: (j, k), **b_mode)
    else:
        b_spec = pl.BlockSpec((tk, tn), lambda i, j, k: (k, j), **b_mode)

    def body(a_ref, b_ref, *rest):
        o_ref, acc = rest[len(extra)], rest[len(extra) + 1 :]
        p = _mx(a_ref[...], b_ref[...], kind)
        if nk == 1:
            o_ref[...] = (p.T if out_t else p).astype(out_dtype)
        else:
            assert not out_t
            acc_ref = acc[0]
            k = pl.program_id(2)

            @pl.when(k == 0)
            def _():
                acc_ref[...] = p

            @pl.when(k > 0)
            def _():
                acc_ref[...] += p

            @pl.when(k == nk - 1)
            def _():
                o_ref[...] = acc_ref[...].astype(out_dtype)

    return pl.pallas_call(
        body,
        out_shape=jax.ShapeDtypeStruct((N, M) if out_t else (M, N), out_dtype),
        grid=(M // tm, N // tn, nk),
        in_specs=[a_spec, b_spec] + [pl.BlockSpec((8, LANES), lambda i, j, k: (0, 0))] * len(extra),
        out_specs=pl.BlockSpec((tn, tm), lambda i, j, k: (j, i)) if out_t else pl.BlockSpec((tm, tn), lambda i, j, k: (i, j)),
        scratch_shapes=[] if nk == 1 else [pltpu.VMEM((tm, tn), f32)],
        compiler_params=_cparams(("parallel", "parallel", "arbitrary")),
        name=name,
    )(a, b, *extra)


def _dw_proj(pairs, name, tn=512):
    T = pairs[0][0].shape[0]
    nj, rows = D_MODEL // tn, D_MODEL // N_DEV

    def body(*refs):
        o_ref = refs[2 * len(pairs)]
        for p in range(len(pairs)):

            @pl.when(pl.program_id(0) == p)
            def _(p=p):
                r = _mx(refs[2 * p][...], refs[2 * p + 1][...], "tn")
                o_ref[:, 0] = r.reshape(N_DEV, rows, tn).astype(o_ref.dtype)

    def b_map(p):
        return lambda g, j: (0, jnp.where(g == p, j, jnp.where(g < p, 0, nj - 1)))

    in_specs = []
    for p in range(len(pairs)):
        in_specs += [pl.BlockSpec((T, D_MODEL), lambda g, j: (0, 0), pipeline_mode=pl.Buffered(1)), pl.BlockSpec((T, tn), b_map(p))]
    return pl.pallas_call(
        body,
        out_shape=jax.ShapeDtypeStruct((N_DEV, len(pairs), rows, D_MODEL), MXU_DTYPE),
        grid=(len(pairs), nj),
        in_specs=in_specs,
        out_specs=pl.BlockSpec((N_DEV, 1, rows, tn), lambda g, j: (0, g, 0, j)),
        compiler_params=_cparams(("arbitrary", "arbitrary")),
        name=name,
    )(*[t for pair in pairs for t in pair])


ROW_TILE = 512


def _rms_pre(x, w):
    T = x.shape[0]
    tb = min(ROW_TILE, T)

    def body(x_ref, w_ref, h_ref, ht_ref):
        xv = x_ref[...]
        r = lax.rsqrt(jnp.mean(xv * xv, axis=1, keepdims=True) + NORM_EPS)
        h = xv * r * w_ref[...]
        h_ref[...] = h.astype(h_ref.dtype)
        ht_ref[...] = h.T.astype(ht_ref.dtype)

    return pl.pallas_call(
        body,
        out_shape=(jax.ShapeDtypeStruct((T, D_MODEL), MXU_DTYPE), jax.ShapeDtypeStruct((D_MODEL, T), MXU_DTYPE)),
        grid=(T // tb,),
        in_specs=[pl.BlockSpec((tb, D_MODEL), lambda i: (i, 0)), pl.BlockSpec((1, D_MODEL), lambda i: (0, 0))],
        out_specs=(pl.BlockSpec((tb, D_MODEL), lambda i: (i, 0)), pl.BlockSpec((D_MODEL, tb), lambda i: (0, i))),
        compiler_params=_cparams(("parallel",)),
        name="rms_pre",
    )(x, w)


def _rms_pre_bwd(x, w, dh, dy):
    T = x.shape[0]
    tb = min(ROW_TILE, T)

    def body(x_ref, w_ref, dh_ref, dy_ref, dx_ref, dw_ref):
        i = pl.program_id(0)
        xv, dhv = x_ref[...], dh_ref[...]
        r = lax.rsqrt(jnp.mean(xv * xv, axis=1, keepdims=True) + NORM_EPS)
        dhw = dhv * w_ref[...]
        dx_ref[...] = dy_ref[...] + r * dhw - xv * (r * r * r) * jnp.mean(dhw * xv, axis=1, keepdims=True)
        part = jnp.sum(dhv * xv * r, axis=0, keepdims=True)

        @pl.when(i == 0)
        def _():
            dw_ref[...] = part

        @pl.when(i > 0)
        def _():
            dw_ref[...] += part

    blk = pl.BlockSpec((tb, D_MODEL), lambda i: (i, 0))
    row = pl.BlockSpec((1, D_MODEL), lambda i: (0, 0))
    return pl.pallas_call(
        body,
        out_shape=(jax.ShapeDtypeStruct((T, D_MODEL), f32), jax.ShapeDtypeStruct((1, D_MODEL), f32)),
        grid=(T // tb,),
        in_specs=[blk, row, blk, blk],
        out_specs=(blk, row),
        compiler_params=_cparams(("arbitrary",)),
        name="rms_pre_bwd",
    )(x, w, dh, dy)


SHIFTS = (-2, -1, 1, 2)


SHIFT_PAD = 8
PREP_HEADS, PREP_HEADS_BWD = 4, 1


def _shifted(x, pad_ref):
    T, cb = x.shape
    pad_ref[0:SHIFT_PAD] = jnp.zeros((SHIFT_PAD, cb), f32)
    pad_ref[SHIFT_PAD + T :] = jnp.zeros((SHIFT_PAD, cb), f32)
    pad_ref[SHIFT_PAD : SHIFT_PAD + T] = x
    return {d: pad_ref[SHIFT_PAD + d : SHIFT_PAD + d + T] for d in SHIFTS}


def _conv5(u, ush, cw):
    acc = u * cw[2:3, :]
    for j in (0, 1, 3, 4):
        acc = acc + ush[j - 2] * cw[j : j + 1, :]
    return acc


def _gdn_prep(proj, conv_w):
    T = proj.shape[0]
    hps, W = PREP_HEADS, PREP_HEADS * GDN_DIM
    per = GDN_HEADS // hps

    def body(u_ref, cw_ref, o_ref, pad_ref):
        j = pl.program_id(0)
        for i in range(hps):
            sl = slice(i * GDN_DIM, (i + 1) * GDN_DIM)
            u = _mxr(u_ref[:, sl])
            s = _silu(_conv5(u, _shifted(u, pad_ref), _mxr(cw_ref[:, sl])))
            o_ref[0, :, sl] = jnp.where(j < 2 * per, s * lax.rsqrt(jnp.sum(s * s, axis=1, keepdims=True) + NORM_EPS), s)

    return pl.pallas_call(
        body,
        out_shape=jax.ShapeDtypeStruct((3, T, GDN_HEADS * GDN_DIM), f32),
        grid=(3 * per,),
        in_specs=[pl.BlockSpec((T, W), lambda j: (0, j)), pl.BlockSpec((CONV_K, W), lambda j: (0, j))],
        out_specs=pl.BlockSpec((1, T, W), lambda j: (j // per, 0, j % per)),
        scratch_shapes=[pltpu.VMEM((T + 2 * SHIFT_PAD, GDN_DIM), f32)],
        compiler_params=_cparams(("parallel",)),
        name="gdn_prep",
    )(proj, conv_w)


def _gdn_prep_bwd(proj, conv_w, dqkv_f, dqkv_b, dproj):
    T = proj.shape[0]
    hps, W = PREP_HEADS_BWD, PREP_HEADS_BWD * GDN_DIM
    per = GDN_HEADS // hps

    def body(u_ref, cw_ref, df_ref, db_ref, _, du_ref, dcw_ref, upad_ref, dcpad_ref):
        j = pl.program_id(0)
        for i in range(hps):
            sl = slice(i * GDN_DIM, (i + 1) * GDN_DIM)
            u, cw = _mxr(u_ref[:, sl]), _mxr(cw_ref[:, sl])
            ush = _shifted(u, upad_ref)
            c = _conv5(u, ush, cw)
            s = _silu(c)
            dn = df_ref[0, :, sl] + db_ref[0, :, sl]
            rinv = lax.rsqrt(jnp.sum(s * s, axis=1, keepdims=True) + NORM_EPS)
            ds_norm = rinv * dn - s * (rinv * rinv * rinv) * jnp.sum(dn * s, axis=1, keepdims=True)
            ds = jnp.where(j < 2 * per, ds_norm, dn)
            dc = _mxr(ds * _silu_grad(c))
            dcsh = _shifted(dc, dcpad_ref)
            du = dc * cw[2:3, :]
            for jj in range(CONV_K):
                d = jj - 2
                dcw_ref[jj : jj + 1, sl] = jnp.sum(dc * (u if d == 0 else ush[d]), axis=0, keepdims=True)
                if d != 0:
                    du = du + dcsh[-d] * cw[jj : jj + 1, :]
            du_ref[:, sl] = du.astype(du_ref.dtype)

    qspec = pl.BlockSpec((1, T, W), lambda j: (j // per, 0, j % per))
    return pl.pallas_call(
        body,
        out_shape=(jax.ShapeDtypeStruct(dproj.shape, dproj.dtype), jax.ShapeDtypeStruct((CONV_K, 3 * GDN_HEADS * GDN_DIM), f32)),
        grid=(3 * per,),
        in_specs=[pl.BlockSpec((T, W), lambda j: (0, j)), pl.BlockSpec((CONV_K, W), lambda j: (0, j)), qspec, qspec, _ANY_SPEC],
        out_specs=(pl.BlockSpec((T, W), lambda j: (0, j)), pl.BlockSpec((CONV_K, W), lambda j: (0, j))),
        input_output_aliases={4: 0},
        scratch_shapes=[pltpu.VMEM((T + 2 * SHIFT_PAD, GDN_DIM), f32)] * 2,
        compiler_params=_cparams(("parallel",)),
        name="gdn_prep_bwd",
    )(proj, conv_w, dqkv_f, dqkv_b, dproj)


def _chunk_tri(n, rev):
    i, j = _iota2((n, n), 0), _iota2((n, n), 1)
    same = jnp.right_shift(i, 6) == jnp.right_shift(j, 6)
    order = (j >= i) if rev else (j <= i)
    return jnp.where(jnp.logical_and(same, order), 1.0, 0.0).astype(f32)


def _gate_prep(proj, avec, dvec, w2f, w2b, b2f, b2b, tb=256):
    T = proj.shape[0]
    W = GLA_HEADS * GLA_DK

    def body(s_ref, a_ref, d_ref, wf_ref, wb_ref, bf_ref, bb_ref, g_ref, gc_ref, gf_ref, gb_ref):
        small = s_ref[...]
        tri_f, tri_r = _chunk_tri(tb, False), _chunk_tri(tb, True)
        lane = _iota2(small.shape, 1)
        lg = -jnp.exp(a_ref[...]) * _softplus(small + d_ref[...])
        beta = _sigmoid(small)
        g_ref[...] = jnp.where(lane < 16, lg, jnp.where(lane < 32, beta, 0.0))
        lgm = jnp.where(lane < 16, lg, 0.0)
        gc_ref[...] = jnp.where(lane < 8, _exact01(tri_f, lgm), _exact01(tri_r, lgm))
        gkf = -_softplus(-(_mx(small, wf_ref[...], "nn") + bf_ref[...])) * (1.0 / GATE_NORMALIZER)
        gkb = -_softplus(-(_mx(small, wb_ref[...], "nn") + bb_ref[...])) * (1.0 / GATE_NORMALIZER)
        gf_ref[...] = _exact01(tri_f, gkf)
        gb_ref[...] = _exact01(tri_r, gkb)

    blk = pl.BlockSpec((tb, LANES), lambda i: (i, 0))
    row = pl.BlockSpec((1, LANES), lambda i: (0, 0))
    wide = pl.BlockSpec((tb, W), lambda i: (i, 0))
    wsp = pl.BlockSpec((LANES, W), lambda i: (0, 0))
    wrow = pl.BlockSpec((1, W), lambda i: (0, 0))
    return pl.pallas_call(
        body,
        out_shape=(jax.ShapeDtypeStruct((T, LANES), f32), jax.ShapeDtypeStruct((T, LANES), f32),
                   jax.ShapeDtypeStruct((T, W), f32), jax.ShapeDtypeStruct((T, W), f32)),
        grid=(T // tb,),
        in_specs=[pl.BlockSpec((tb, LANES), lambda i: (i, SMALL_OFF // LANES)), row, row, wsp, wsp, wrow, wrow],
        out_specs=(blk, blk, wide, wide),
        compiler_params=_cparams(("parallel",)),
        name="gate_prep",
    )(proj, avec, dvec, w2f, w2b, b2f, b2b)


def _gdn_gates_bwd(proj, avec, dvec, dg_f, dg_b, dsmall_gla, dproj):
    T = proj.shape[0]
    tb = min(ROW_TILE, T)
    pad = N_CAT - SMALL_OFF

    def body(s_ref, a_ref, d_ref, gf_ref, gb_ref, dl_ref, _, ds_ref, da_ref, dd_ref):
        i = pl.program_id(0)
        small = s_ref[...]
        lane = _iota2(small.shape, 1)
        dgate = gf_ref[...] + gb_ref[...]
        z = small + d_ref[...]
        nega = -jnp.exp(a_ref[...])
        dz = dgate * nega * _sigmoid(z)
        beta = _sigmoid(small)
        dsm = jnp.where(lane < 16, dz, jnp.where(lane < 32, dgate * beta * (1.0 - beta), 0.0))
        ds_ref[:, :LANES] = (dsm + dl_ref[...]).astype(ds_ref.dtype)
        ds_ref[:, LANES:] = jnp.zeros((tb, pad - LANES), ds_ref.dtype)
        lg = nega * _softplus(z)
        pa = jnp.sum(jnp.where(lane < 16, dgate * lg, 0.0), axis=0, keepdims=True)
        pd = jnp.sum(jnp.where(lane < 16, dz, 0.0), axis=0, keepdims=True)

        @pl.when(i == 0)
        def _():
            da_ref[...] = pa
            dd_ref[...] = pd

        @pl.when(i > 0)
        def _():
            da_ref[...] += pa
            dd_ref[...] += pd

    blk = pl.BlockSpec((tb, LANES), lambda i: (i, 0))
    row = pl.BlockSpec((1, LANES), lambda i: (0, 0))
    return pl.pallas_call(
        body,
        out_shape=(jax.ShapeDtypeStruct(dproj.shape, dproj.dtype), jax.ShapeDtypeStruct((1, LANES), f32), jax.ShapeDtypeStruct((1, LANES), f32)),
        grid=(T // tb,),
        in_specs=[pl.BlockSpec((tb, LANES), lambda i: (i, SMALL_OFF // LANES)), row, row, blk, blk, blk, _ANY_SPEC],
        out_specs=(pl.BlockSpec((tb, pad), lambda i: (i, DP_SMALL_BLK)), row, row),
        input_output_aliases={6: 0},
        compiler_params=_cparams(("arbitrary",)),
        name="gdn_gates_bwd",
    )(proj, avec, dvec, dg_f, dg_b, dsmall_gla, dproj)


GDN_HB = 8


def _inv_unit_lower(L, eye):
    A = eye - L
    P2 = _split2(L)
    for _ in range(3):
        P2 = _split2(_h3s(P2, P2, "nn"))
        A = A + _h3s(_split2(A), P2, "nn")
    Ph = P2[0]
    for _ in range(2):
        Ph = _dot16(Ph, Ph, "nn").astype(jnp.bfloat16)
        A = A + _dot16(A.astype(jnp.bfloat16), Ph, "nn")
    return A


def _lane_col(blk, idx):
    lane = _iota2(blk.shape, 1)
    return jnp.sum(jnp.where(lane == idx, blk, 0.0), axis=1, keepdims=True)


def _out_norm(o_f, o_b, proj, gate_blk, w, hd, name):
    T, W = o_f.shape
    tb = min(ROW_TILE, T)
    nh = W // hd

    def body(of_ref, ob_ref, z_ref, w_ref, y_ref):
        wv = w_ref[...]
        for i in range(nh):
            sl = slice(i * hd, (i + 1) * hd)
            o = of_ref[:, sl] + ob_ref[:, sl]
            r = lax.rsqrt(jnp.mean(o * o, axis=1, keepdims=True) + NORM_EPS)
            y_ref[:, sl] = (o * r * wv * _silu(z_ref[:, sl])).astype(y_ref.dtype)

    blk = pl.BlockSpec((tb, W), lambda i: (i, 0))
    return pl.pallas_call(
        body,
        out_shape=jax.ShapeDtypeStruct((T, W), MXU_DTYPE),
        grid=(T // tb,),
        in_specs=[blk, blk, pl.BlockSpec((tb, W), lambda i: (i, gate_blk)), pl.BlockSpec((1, hd), lambda i: (0, 0))],
        out_specs=blk,
        compiler_params=_cparams(("parallel",)),
        name=name,
    )(o_f, o_b, proj, w)


def _out_norm_bwd(dy, o_f, o_b, proj, gate_blk, w, hd, name, dproj):
    T, W = o_f.shape
    tb = min(ROW_TILE, T)
    nh = W // hd

    def body(dy_ref, of_ref, ob_ref, z_ref, w_ref, _, do_ref, dz_ref, dw_ref):
        i = pl.program_id(0)
        wv = w_ref[...]
        dw = jnp.zeros((1, hd), f32)
        for a in range(nh):
            sl = slice(a * hd, (a + 1) * hd)
            o = of_ref[:, sl] + ob_ref[:, sl]
            z = z_ref[:, sl]
            dyv = dy_ref[:, sl]
            r = lax.rsqrt(jnp.mean(o * o, axis=1, keepdims=True) + NORM_EPS)
            orr = o * r
            dn = dyv * _silu(z)
            dz_ref[:, sl] = (dyv * orr * wv * _silu_grad(z)).astype(dz_ref.dtype)
            dw = dw + jnp.sum(dn * orr, axis=0, keepdims=True)
            dnw = dn * wv
            do_ref[:, sl] = r * dnw - o * (r * r * r) * jnp.mean(dnw * o, axis=1, keepdims=True)

        @pl.when(i == 0)
        def _():
            dw_ref[...] = dw

        @pl.when(i > 0)
        def _():
            dw_ref[...] += dw

    blk = pl.BlockSpec((tb, W), lambda i: (i, 0))
    row = pl.BlockSpec((1, hd), lambda i: (0, 0))
    return pl.pallas_call(
        body,
        out_shape=(jax.ShapeDtypeStruct((T, W), f32), jax.ShapeDtypeStruct(dproj.shape, dproj.dtype), jax.ShapeDtypeStruct((1, hd), f32)),
        grid=(T // tb,),
        in_specs=[blk, blk, blk, pl.BlockSpec((tb, W), lambda i: (i, gate_blk)), row, _ANY_SPEC],
        out_specs=(blk, pl.BlockSpec((tb, W), lambda i: (i, gate_blk)), row),
        input_output_aliases={5: 1},
        compiler_params=_cparams(("arbitrary",)),
        name=name,
    )(dy, o_f, o_b, proj, w, dproj)


def _gla_prep_bwd(proj, w2f, w2b, b2f, b2b, dgk_f, dgk_b, dqkv, dproj):
    T = proj.shape[0]
    tb = min(ROW_TILE, T)
    W = GLA_HEADS * GLA_DK
    WV = GLA_HEADS * GLA_DV

    def body(s_ref, wf_ref, wb_ref, bf_ref, bb_ref, df_ref, db_ref, qf, qb, kf, kb, vf, vb, _,
             dp_ref, ds_ref, dwf_ref, dwb_ref, dbf_ref, dbb_ref):
        i = pl.program_id(0)
        dp_ref[:, :W] = (qf[...] + qb[...]).astype(dp_ref.dtype)
        dp_ref[:, W : 2 * W] = (kf[...] + kb[...]).astype(dp_ref.dtype)
        dp_ref[:, 2 * W :] = (vf[...] + vb[...]).astype(dp_ref.dtype)
        small = s_ref[...]
        dsm = jnp.zeros((tb, LANES), f32)
        parts = []
        for w_ref, b_ref, d_ref in ((wf_ref, bf_ref, df_ref), (wb_ref, bb_ref, db_ref)):
            pre = _mx(small, w_ref[...], "nn") + b_ref[...]
            dpre = d_ref[...] * (1.0 / GATE_NORMALIZER) * _sigmoid(-pre)
            dsm = dsm + _mx(dpre, w_ref[...], "nt")
            parts.append((_mx(small, dpre, "tn"), jnp.sum(dpre, axis=0, keepdims=True)))
        ds_ref[...] = dsm

        @pl.when(i == 0)
        def _():
            dwf_ref[...], dbf_ref[...] = parts[0]
            dwb_ref[...], dbb_ref[...] = parts[1]

        @pl.when(i > 0)
        def _():
            dwf_ref[...] += parts[0][0]
            dbf_ref[...] += parts[0][1]
            dwb_ref[...] += parts[1][0]
            dbb_ref[...] += parts[1][1]

    blk = pl.BlockSpec((tb, W), lambda i: (i, 0))
    vblk = pl.BlockSpec((tb, WV), lambda i: (i, 0))
    wsp = pl.BlockSpec((LANES, W), lambda i: (0, 0))
    row = pl.BlockSpec((1, W), lambda i: (0, 0))
    return pl.pallas_call(
        body,
        out_shape=(jax.ShapeDtypeStruct(dproj.shape, dproj.dtype), jax.ShapeDtypeStruct((T, LANES), f32), jax.ShapeDtypeStruct((LANES, W), f32),
                   jax.ShapeDtypeStruct((LANES, W), f32), jax.ShapeDtypeStruct((1, W), f32), jax.ShapeDtypeStruct((1, W), f32)),
        grid=(T // tb,),
        in_specs=[pl.BlockSpec((tb, LANES), lambda i: (i, SMALL_OFF // LANES)), wsp, wsp, row, row, blk, blk,
                  blk, blk, blk, blk, vblk, vblk, _ANY_SPEC],
        out_specs=(pl.BlockSpec((tb, 2 * W + WV), lambda i: (i, DP_GLA_QKV_BLK)), pl.BlockSpec((tb, LANES), lambda i: (i, 0)), wsp, wsp, row, row),
        input_output_aliases={13: 0},
        compiler_params=_cparams(("arbitrary",)),
        name="gla_prep_bwd",
    )(proj, w2f, w2b, b2f, b2b, dgk_f, dgk_b, *dqkv, dproj)


GATE_A_BLK, GATE_B_BLK, Z_BLK, G_B_BLK = 6144 // 1024, 7168 // 1024, 3072 // 1024, 8192 // 1024
DP_GATES_BLK, DP_GLA_QKV_BLK, DP_SMALL_BLK = 6144 // 2048, 4096 // 2048, SMALL_OFF // 512
_ANY_SPEC = pl.BlockSpec(memory_space=pl.ANY)


def _mid(og, obg, proj, x, target, wpg, wpl, wout, w_post, tb=256):
    T = x.shape[0]
    Dm = D_MODEL

    def body(og_ref, obg_ref, ga_ref, gb_ref, x_ref, t_ref, wpg_ref, wpl_ref, wo_ref, wp_ref,
             loss_ref, dwp_ref, dy_ref, m_ref, dout_ref, dya_ref, dyb_ref, dgates_ref, dog_ref, dobg_ref):
        i = pl.program_id(0)
        ya = _mx(og_ref[...], wpg_ref[...], "nn")
        yb = _mx(obg_ref[...], wpl_ref[...], "nn")
        sa, sb = _sigmoid(ga_ref[...]), _sigmoid(gb_ref[...])
        m = sa * ya + sb * yb
        m_ref[...] = m.astype(m_ref.dtype)
        out = _mx(m, wo_ref[...], "nn")
        r = lax.rsqrt(jnp.mean(out * out, axis=1, keepdims=True) + NORM_EPS)
        wp = wp_ref[...]
        e = x_ref[...] + out * r * wp - t_ref[...]
        dy = e * (1.0 / Dm)
        dy_ref[...] = dy
        lpart = jnp.sum(e * e, axis=0, keepdims=True)
        wpart = jnp.sum(dy * out * r, axis=0, keepdims=True)
        dyn = dy * wp
        dout = r * dyn - out * (r * r * r) * jnp.mean(dyn * out, axis=1, keepdims=True)
        dout_ref[...] = dout.astype(dout_ref.dtype)
        dm = _mx(dout, wo_ref[...], "nt")
        dya, dyb = dm * sa, dm * sb
        dya_ref[...] = dya.astype(dya_ref.dtype)
        dyb_ref[...] = dyb.astype(dyb_ref.dtype)
        dgates_ref[:, :Dm] = (dm * ya * sa * (1.0 - sa)).astype(dgates_ref.dtype)
        dgates_ref[:, Dm:] = (dm * yb * sb * (1.0 - sb)).astype(dgates_ref.dtype)
        dog_ref[...] = _mx(dya, wpg_ref[...], "nt")
        dobg_ref[...] = _mx(dyb, wpl_ref[...], "nt")

        @pl.when(i == 0)
        def _():
            loss_ref[...] = lpart
            dwp_ref[...] = wpart

        @pl.when(i > 0)
        def _():
            loss_ref[...] += lpart
            dwp_ref[...] += wpart

    blk = pl.BlockSpec((tb, Dm), lambda i: (i, 0))
    row = pl.BlockSpec((1, Dm), lambda i: (0, 0))
    wsp = pl.BlockSpec((Dm, Dm), lambda i: (0, 0))
    act = jax.ShapeDtypeStruct((T, Dm), MXU_DTYPE)
    big = jax.ShapeDtypeStruct((T, Dm), f32)
    vec = jax.ShapeDtypeStruct((1, Dm), f32)
    return pl.pallas_call(
        body,
        out_shape=(vec, vec, big, act, act, act, act, jax.ShapeDtypeStruct((T, N_CAT), MXU_DTYPE), big, big),
        grid=(T // tb,),
        in_specs=[blk, blk, pl.BlockSpec((tb, Dm), lambda i: (i, GATE_A_BLK)), pl.BlockSpec((tb, Dm), lambda i: (i, GATE_B_BLK)),
                  blk, blk, wsp, wsp, wsp, row],
        out_specs=(row, row) + (blk,) * 5 + (pl.BlockSpec((tb, 2 * Dm), lambda i: (i, DP_GATES_BLK)), blk, blk),
        compiler_params=_cparams(("arbitrary",)),
        name="mid",
    )(og, obg, proj, proj, x, target, wpg, wpl, wout, w_post)


def _heads(ref, lead, n, width, rows=slice(None)):
    return [ref[lead + (rows, slice(i * width, (i + 1) * width))] for i in range(n)]


class _GdnBatch:
    def __init__(self, q, k, v, gcol, grow, beta, nf, saved=None, sol=None):
        B, Cn = q.shape[0], q.shape[1]
        self.incl, self.strict, eye = _dir_masks(B, nf, Cn)
        A = None
        if saved is None:
            self.decay = jnp.where(self.incl, jnp.exp(jnp.where(self.incl, gcol - grow, 0.0)), 0.0)
        else:
            A, self.L, self.attn, self.decay = (saved[:, i * Cn : (i + 1) * Cn] for i in range(4))
        self.k, self.v = k, v
        gl = _last_row(gcol, nf)
        self.beta = jnp.broadcast_to(beta, q.shape)
        self.eg = jnp.broadcast_to(jnp.exp(gcol), q.shape)
        self.ekd = jnp.broadcast_to(jnp.exp(gl - gcol), q.shape)
        self.qs = q * (GDN_DIM**-0.5)
        self.kb = k * self.beta
        if saved is None:
            self.L = jnp.where(self.strict, _mx(self.kb, k, "nt") * self.decay, 0.0)
            self.attn = jnp.where(self.incl, _mx(self.qs, k, "nt") * self.decay, 0.0)
            A = _inv_unit_lower(self.L, jnp.where(eye, 1.0, 0.0).astype(f32))
        self.A2 = _split2(A)
        self.saved = jnp.concatenate([A, self.L, self.attn, self.decay], axis=1) if saved is None else saved
        self.kbg = self.kb * self.eg
        if sol is None:
            sol = _h3s(self.A2, _split2(jnp.concatenate([v * self.beta, self.kbg], axis=2)), "nn")
        self.sol = sol
        self.u = self.sol[:, :, :GDN_DIM]
        self.w = self.sol[:, :, GDN_DIM:]
        self.qd = self.qs * self.eg
        self.kd = k * self.ekd
        self.egl = jnp.exp(gl)


def _gdn_chain_inputs(qf_ref, qb_ref, gf_ref, gb_ref, cf_ref, cb_ref, b, hb, need_rows=True):
    qkv = [jnp.stack(_heads(qf_ref, (i,), hb, GDN_DIM) + _heads(qb_ref, (i,), hb, GDN_DIM)) for i in range(3)]
    cols, rows, betas = [], [], []
    for d, (g_ref, c_ref) in enumerate(((gf_ref, cf_ref), (gb_ref, cb_ref))):
        gates, gc = g_ref[...], c_ref[...]
        for hh in range(hb):
            h = b * hb + hh
            col = _lane_col(gc, h + 8 * d)
            cols.append(col)
            if need_rows:
                rows.append(jnp.broadcast_to(col, (CHUNK, LANES)).T[0:1, :])
            betas.append(_lane_col(gates, 16 + h + 8 * d))
    return qkv[0], qkv[1], qkv[2], jnp.stack(cols), (jnp.stack(rows) if need_rows else None), jnp.stack(betas)


GDN_CPS_FWD, GDN_CPS_BWD = 2, 1


def _chunk_views(refs, s_f, s_b):
    out = []
    for i, r in enumerate(refs):
        s = s_f if i % 2 == 0 else s_b
        rows = pl.ds(s * CHUNK, CHUNK)
        out.append(r.at[pl.ds(s, 1)] if len(r.shape) == 4 else (r.at[:, rows, :] if len(r.shape) == 3 else r.at[rows, :]))
    return out


def _gdn_specs(cps, hb):
    R, W = cps * CHUNK, hb * GDN_DIM
    qsp = lambda ci: pl.BlockSpec((3, R, W), lambda n, b: (0, ci(n), b))
    gsp = lambda ci: pl.BlockSpec((R, LANES), lambda n, b: (ci(n), 0))
    osp = lambda ci: pl.BlockSpec((R, W), lambda n, b: (ci(n), b))
    ssp = lambda ci: pl.BlockSpec((cps, hb, GDN_DIM, GDN_DIM), lambda n, b: (ci(n), b, 0, 0))
    asp = lambda ci: pl.BlockSpec((cps, hb, 4 * CHUNK, CHUNK), lambda n, b: (ci(n), b, 0, 0))
    usp = lambda ci: pl.BlockSpec((cps, hb, CHUNK, 2 * GDN_DIM), lambda n, b: (ci(n), b, 0, 0))
    return qsp, gsp, osp, ssp, asp, usp


def _gdn_fwd(qkvn, gates, gc):
    T = qkvn.shape[1]
    N = T // CHUNK
    nh, hb = GDN_HEADS, GDN_HB
    cps = min(GDN_CPS_FWD, N)
    NB = N // cps

    def body(*refs):
        n, b = pl.program_id(0), pl.program_id(1)
        s_scr = refs[-1]

        @pl.when(n == 0)
        def _():
            s_scr[0, pl.ds(b * hb, hb)] = jnp.zeros((hb, GDN_DIM, GDN_DIM), f32)
            s_scr[1, pl.ds(b * hb, hb)] = jnp.zeros((hb, GDN_DIM, GDN_DIM), f32)

        for s in range(cps):
            chunk(*_chunk_views(refs[:-1], s, cps - 1 - s), s_scr, b)

    def chunk(qf_ref, qb_ref, gf_ref, gb_ref, cf_ref, cb_ref, of_ref, ob_ref, sf_ref, sb_ref, af_ref, ab_ref, uf_ref, ub_ref, s_scr, b):
        hs = pl.ds(b * hb, hb)
        q, k, v, gcol, grow, beta = _gdn_chain_inputs(qf_ref, qb_ref, gf_ref, gb_ref, cf_ref, cb_ref, b, hb)
        ck = _GdnBatch(q, k, v, gcol, grow, beta, hb)
        S = jnp.concatenate([s_scr[0, hs], s_scr[1, hs]], axis=0)
        sf_ref[0], sb_ref[0] = S[:hb], S[hb:]
        af_ref[0], ab_ref[0] = ck.saved[:hb], ck.saved[hb:]
        uf_ref[0], ub_ref[0] = ck.sol[:hb], ck.sol[hb:]
        v_new = ck.u - _mx(ck.w, S, "nn")
        o = _mx(ck.qd, S, "nn") + _mx(ck.attn, v_new, "nn")
        S_new = S * ck.egl + _mx(ck.kd, v_new, "tn")
        for hh in range(hb):
            sl = slice(hh * GDN_DIM, (hh + 1) * GDN_DIM)
            of_ref[:, sl] = o[hh]
            ob_ref[:, sl] = o[hb + hh]
        s_scr[0, hs] = S_new[:hb]
        s_scr[1, hs] = S_new[hb:]

    fw, bw = (lambda n: n), (lambda n: NB - 1 - n)
    qsp, gsp, osp, ssp, asp, usp = _gdn_specs(cps, hb)
    o_shape = jax.ShapeDtypeStruct((T, nh * GDN_DIM), f32)
    s_shape = jax.ShapeDtypeStruct((N, nh, GDN_DIM, GDN_DIM), f32)
    a_shape = jax.ShapeDtypeStruct((N, nh, 4 * CHUNK, CHUNK), f32)
    u_shape = jax.ShapeDtypeStruct((N, nh, CHUNK, 2 * GDN_DIM), f32)
    return pl.pallas_call(
        body,
        out_shape=(o_shape, o_shape, s_shape, s_shape, a_shape, a_shape, u_shape, u_shape),
        grid=(NB, nh // hb),
        in_specs=[qsp(fw), qsp(bw), gsp(fw), gsp(bw), gsp(fw), gsp(bw)],
        out_specs=(osp(fw), osp(bw), ssp(fw), ssp(bw), asp(fw), asp(bw), usp(fw), usp(bw)),
        scratch_shapes=[pltpu.VMEM((2, nh, GDN_DIM, GDN_DIM), f32)],
        compiler_params=_cparams(("arbitrary", "arbitrary")),
        name="gdn_scan_fwd",
    )(qkvn, qkvn, gates, gates, gc, gc)


def _gdn_bwd(qkvn, gates, gc, st_f, st_b, a_f, a_b, u_f, u_b, do):
    T = qkvn.shape[1]
    N = T // CHUNK
    nh, hb = GDN_HEADS, GDN_HB
    nb = nh // hb
    cps = min(GDN_CPS_BWD, N)
    NB = N // cps

    def body(*refs):
        n, b = pl.program_id(0), pl.program_id(1)
        ds_scr = refs[-1]

        @pl.when(n == 0)
        def _():
            ds_scr[0, pl.ds(b * hb, hb)] = jnp.zeros((hb, GDN_DIM, GDN_DIM), f32)
            ds_scr[1, pl.ds(b * hb, hb)] = jnp.zeros((hb, GDN_DIM, GDN_DIM), f32)

        for s in range(cps):
            chunk(*_chunk_views(refs[:-1], cps - 1 - s, s), ds_scr, b)

    def chunk(qf_ref, qb_ref, gf_ref, gb_ref, cf_ref, cb_ref, sf_ref, sb_ref, af_ref, ab_ref, uf_ref, ub_ref, dof_ref, dob_ref,
              dqf_ref, dqb_ref, dgf_ref, dgb_ref, ds_scr, b):
        hs = pl.ds(b * hb, hb)

        @pl.when(b == 0)
        def _():
            dgf_ref[...] = jnp.zeros((CHUNK, LANES), f32)
            dgb_ref[...] = jnp.zeros((CHUNK, LANES), f32)

        q, k, v, gcol, grow, beta = _gdn_chain_inputs(qf_ref, qb_ref, gf_ref, gb_ref, cf_ref, cb_ref, b, hb, need_rows=False)
        ck = _GdnBatch(q, k, v, gcol, grow, beta, hb, saved=jnp.concatenate([af_ref[0], ab_ref[0]], axis=0),
                       sol=jnp.concatenate([uf_ref[0], ub_ref[0]], axis=0))
        S = jnp.concatenate([sf_ref[0], sb_ref[0]], axis=0)
        dS = jnp.concatenate([ds_scr[0, hs], ds_scr[1, hs]], axis=0)
        dov = jnp.stack(_heads(dof_ref, (), hb, GDN_DIM) + _heads(dob_ref, (), hb, GDN_DIM))
        v_new = ck.u - _mx(ck.w, S, "nn")
        d_vnew = _mx(ck.attn, dov, "tn") + _mx(ck.kd, dS, "nn")
        d_attn = jnp.where(ck.incl, _mx(dov, v_new, "nt"), 0.0)
        d_qd = _mx(dov, S, "nt")
        d_kd = _mx(v_new, dS, "nt")
        d_gl = jnp.sum(jnp.sum(dS * S, axis=2, keepdims=True), axis=1, keepdims=True) * ck.egl
        dS_new = dS * ck.egl + _mx(ck.qd, dov, "tn") - _mx(ck.w, d_vnew, "tn")
        ds_scr[0, hs] = dS_new[:hb]
        ds_scr[1, hs] = dS_new[hb:]
        d_w = -_mx(d_vnew, S, "nt")
        d_rhs = _h3s(ck.A2, _split2(jnp.concatenate([d_vnew, d_w], axis=2)), "tn")
        d_vb, d_kbg = d_rhs[:, :, :GDN_DIM], d_rhs[:, :, GDN_DIM:]
        dL = -jnp.where(ck.strict, _h3(d_rhs, ck.sol, "nt"), 0.0)
        P = dL * ck.decay
        Q = d_attn * ck.decay
        d_kb = _mx(P, k, "nn") + d_kbg * ck.eg
        d_qs = _mx(Q, k, "nn") + d_qd * ck.eg
        d_k = _mx(P, ck.kb, "tn") + _mx(Q, ck.qs, "tn") + d_kd * ck.ekd + d_kb * ck.beta
        E = dL * ck.L + d_attn * ck.attn
        kdsum = jnp.sum(d_kd * ck.kd, axis=2, keepdims=True)
        d_gc = (jnp.sum(E, axis=2, keepdims=True) - _colsum_col(E)
                + jnp.sum(d_kbg * ck.kbg, axis=2, keepdims=True)
                + jnp.sum(d_qd * ck.qd, axis=2, keepdims=True) - kdsum)
        d_gc = _add_at_last_row(d_gc, d_gl + jnp.sum(kdsum, axis=1, keepdims=True), hb)
        d_beta = jnp.sum(d_vb * v, axis=2, keepdims=True) + jnp.sum(d_kb * k, axis=2, keepdims=True)
        d_q = d_qs * (GDN_DIM**-0.5)
        d_v = d_vb * ck.beta
        lane = _iota2((CHUNK, LANES), 1)
        for d, (dq_ref, dg_ref) in enumerate(((dqf_ref, dgf_ref), (dqb_ref, dgb_ref))):
            acc = dg_ref[...]
            for hh in range(hb):
                c, h = d * hb + hh, b * hb + hh
                sl = slice(hh * GDN_DIM, (hh + 1) * GDN_DIM)
                dq_ref[0, :, sl] = d_q[c]
                dq_ref[1, :, sl] = d_k[c]
                dq_ref[2, :, sl] = d_v[c]
                acc = jnp.where(lane == h + 8 * d, d_gc[c], acc)
                acc = jnp.where(lane == 16 + h + 8 * d, d_beta[c], acc)
            dg_ref[...] = acc

        @pl.when(b == nb - 1)
        def _():
            for d, dg_ref in enumerate((dgf_ref, dgb_ref)):
                blk = dg_ref[...]
                dg_ref[...] = jnp.where(lane < 16, _exact01(_chunk_tri(CHUNK, d == 0), blk), blk)

    fw, bw = (lambda n: NB - 1 - n), (lambda n: n)
    qsp, gsp, osp, ssp, asp, usp = _gdn_specs(cps, hb)
    dq_shape = jax.ShapeDtypeStruct((3, T, nh * GDN_DIM), f32)
    dg_shape = jax.ShapeDtypeStruct((T, LANES), f32)
    return pl.pallas_call(
        body,
        out_shape=(dq_shape, dq_shape, dg_shape, dg_shape),
        grid=(NB, nb),
        in_specs=[qsp(fw), qsp(bw), gsp(fw), gsp(bw), gsp(fw), gsp(bw), ssp(fw), ssp(bw), asp(fw), asp(bw), usp(fw), usp(bw),
                  osp(fw), osp(bw)],
        out_specs=(qsp(fw), qsp(bw), gsp(fw), gsp(bw)),
        scratch_shapes=[pltpu.VMEM((2, nh, GDN_DIM, GDN_DIM), f32)],
        compiler_params=_cparams(("arbitrary", "arbitrary")),
        name="gdn_scan_bwd",
    )(qkvn, qkvn, gates, gates, gc, gc, st_f, st_b, a_f, a_b, u_f, u_b, do, do)


class _GlaBatch:
    def __init__(self, q, k, G, nf):
        B, Cn = q.shape[0], q.shape[1]
        self.incl, _, _ = _dir_masks(B, nf, Cn)
        self.eG = jnp.exp(G)
        self.enG = jnp.exp(-G)
        self.qg = q * (GLA_DK**-0.5) * self.eG
        self.kg = k * self.enG
        self.attn = jnp.where(self.incl, _mx(self.qg, self.kg, "nt"), 0.0)
        gl = _last_row(G, nf)
        self.ekd = jnp.exp(gl - G)
        self.kd = k * self.ekd
        self.egl = jnp.exp(gl)


GLA_CPS = 4


def _chunk_rows(s):
    return slice(s * CHUNK, (s + 1) * CHUNK)


def _gla_chain_inputs(refs_f, refs_b, rows_f, rows_b):
    nh = GLA_HEADS
    out = []
    for i, width in enumerate((GLA_DK, GLA_DK, GLA_DV, GLA_DK)):
        out.append(jnp.stack(_heads(refs_f[i], (), nh, width, rows_f) + _heads(refs_b[i], (), nh, width, rows_b)))
    return out


def _gla_specs(rows, ci):
    nh = GLA_HEADS
    return [
        pl.BlockSpec((rows, nh * GLA_DK), lambda n: (ci(n), 4096 // (nh * GLA_DK))),
        pl.BlockSpec((rows, nh * GLA_DK), lambda n: (ci(n), 4608 // (nh * GLA_DK))),
        pl.BlockSpec((rows, nh * GLA_DV), lambda n: (ci(n), 5120 // (nh * GLA_DV))),
        pl.BlockSpec((rows, nh * GLA_DK), lambda n: (ci(n), 0)),
    ]


def _gla_fwd(proj, G_f, G_b):
    T = proj.shape[0]
    N = T // CHUNK
    nh = GLA_HEADS
    cps = min(GLA_CPS, N)
    NB = N // cps

    def body(qf, kf, vf, gf, qb, kb, vb, gb, of_ref, ob_ref, sf_ref, sb_ref, s_scr):
        n = pl.program_id(0)

        @pl.when(n == 0)
        def _():
            s_scr[...] = jnp.zeros((2 * nh, GLA_DV, GLA_DK), f32)

        for s in range(cps):
            sb = cps - 1 - s
            q, k, v, G = _gla_chain_inputs((qf, kf, vf, gf), (qb, kb, vb, gb), _chunk_rows(s), _chunk_rows(sb))
            ck = _GlaBatch(q, k, G, nh)
            St = s_scr[...]
            sf_ref[s], sb_ref[sb] = St[:nh].astype(sf_ref.dtype), St[nh:].astype(sb_ref.dtype)
            o = _mx(ck.qg, St, "nt") + _mx(ck.attn, v, "nn")
            for i in range(nh):
                sl = slice(i * GLA_DV, (i + 1) * GLA_DV)
                of_ref[_chunk_rows(s), sl] = o[i]
                ob_ref[_chunk_rows(sb), sl] = o[nh + i]
            s_scr[...] = St * ck.egl + _mx(v, ck.kd, "tn")

    fw, bw = (lambda n: n), (lambda n: NB - 1 - n)
    osp = lambda ci: pl.BlockSpec((cps * CHUNK, nh * GLA_DV), lambda n: (ci(n), 0))
    ssp = lambda ci: pl.BlockSpec((cps, nh, GLA_DV, GLA_DK), lambda n: (ci(n), 0, 0, 0))
    o_shape = jax.ShapeDtypeStruct((T, nh * GLA_DV), f32)
    s_shape = jax.ShapeDtypeStruct((N, nh, GLA_DV, GLA_DK), MXU_DTYPE)
    return pl.pallas_call(
        body,
        out_shape=(o_shape, o_shape, s_shape, s_shape),
        grid=(NB,),
        in_specs=_gla_specs(cps * CHUNK, fw) + _gla_specs(cps * CHUNK, bw),
        out_specs=(osp(fw), osp(bw), ssp(fw), ssp(bw)),
        scratch_shapes=[pltpu.VMEM((2 * nh, GLA_DV, GLA_DK), f32)],
        compiler_params=_cparams(("arbitrary",)),
        name="gla_scan_fwd",
    )(proj, proj, proj, G_f, proj, proj, proj, G_b)


def _gla_bwd(proj, G_f, G_b, st_f, st_b, do):
    T = proj.shape[0]
    N = T // CHUNK
    nh = GLA_HEADS
    cps = min(GLA_CPS, N)
    NB = N // cps

    def body(qf, kf, vf, gf, sf_ref, dof_ref, qb, kb, vb, gb, sb_ref, dob_ref,
             dqf_ref, dkf_ref, dvf_ref, dgf_ref, dqb_ref, dkb_ref, dvb_ref, dgb_ref, ds_scr):
        n = pl.program_id(0)

        @pl.when(n == 0)
        def _():
            ds_scr[...] = jnp.zeros((2 * nh, GLA_DV, GLA_DK), f32)

        shp = (2 * nh, CHUNK, CHUNK)
        bb, ii, jj = _iota3(shp, 0), _iota3(shp, 1), _iota3(shp, 2)
        tri = jnp.where(jnp.logical_or(jnp.logical_and(bb < nh, jj >= ii), jnp.logical_and(bb >= nh, jj <= ii)), 1.0, 0.0)
        for s in range(cps):
            sf = cps - 1 - s
            rows = (_chunk_rows(sf), _chunk_rows(s))
            q, k, v, G = _gla_chain_inputs((qf, kf, vf, gf), (qb, kb, vb, gb), *rows)
            ck = _GlaBatch(q, k, G, nh)
            St = jnp.concatenate([sf_ref[sf], sb_ref[s]], axis=0).astype(f32)
            dSt = ds_scr[...]
            dov = jnp.stack(_heads(dof_ref, (), nh, GLA_DV, rows[0]) + _heads(dob_ref, (), nh, GLA_DV, rows[1]))
            d_attn = jnp.where(ck.incl, _mx(dov, v, "nt"), 0.0)
            d_qg = _mx(dov, St, "nn") + _mx(d_attn, ck.kg, "nn")
            d_kg = _mx(d_attn, ck.qg, "tn")
            d_v = _mx(ck.attn, dov, "tn") + _mx(ck.kd, dSt, "nt")
            d_kd = _mx(v, dSt, "nn")
            ds_scr[...] = dSt * ck.egl + _mx(dov, ck.qg, "tn")
            kdd = d_kd * ck.kd
            d_gl = jnp.sum(dSt * St, axis=1, keepdims=True) * ck.egl + jnp.sum(kdd, axis=1, keepdims=True)
            d_q = d_qg * ck.eG * (GLA_DK**-0.5)
            d_k = d_kg * ck.enG + d_kd * ck.ekd
            d_gk = _exact01(tri, _add_at_last_row(d_qg * ck.qg - d_kg * ck.kg - kdd, d_gl, nh))
            for d, (dq_ref, dk_ref, dv_ref, dg_ref) in enumerate(((dqf_ref, dkf_ref, dvf_ref, dgf_ref), (dqb_ref, dkb_ref, dvb_ref, dgb_ref))):
                for i in range(nh):
                    c = d * nh + i
                    ks, vs = slice(i * GLA_DK, (i + 1) * GLA_DK), slice(i * GLA_DV, (i + 1) * GLA_DV)
                    dq_ref[rows[d], ks] = d_q[c]
                    dk_ref[rows[d], ks] = d_k[c]
                    dv_ref[rows[d], vs] = d_v[c]
                    dg_ref[rows[d], ks] = d_gk[c]

    R = cps * CHUNK

    def in_specs(ci):
        return _gla_specs(R, ci) + [
            pl.BlockSpec((cps, nh, GLA_DV, GLA_DK), lambda n: (ci(n), 0, 0, 0)),
            pl.BlockSpec((R, nh * GLA_DV), lambda n: (ci(n), 0)),
        ]

    def out_specs(ci):
        ksp = pl.BlockSpec((R, nh * GLA_DK), lambda n: (ci(n), 0))
        return [ksp, ksp, pl.BlockSpec((R, nh * GLA_DV), lambda n: (ci(n), 0)), ksp]

    fw, bw = (lambda n: NB - 1 - n), (lambda n: n)
    k_shape = jax.ShapeDtypeStruct((T, nh * GLA_DK), f32)
    v_shape = jax.ShapeDtypeStruct((T, nh * GLA_DV), f32)
    return pl.pallas_call(
        body,
        out_shape=(k_shape, k_shape, v_shape, k_shape) * 2,
        grid=(NB,),
        in_specs=in_specs(fw) + in_specs(bw),
        out_specs=tuple(out_specs(fw) + out_specs(bw)),
        scratch_shapes=[pltpu.VMEM((2 * nh, GLA_DV, GLA_DK), f32)],
        compiler_params=_cparams(("arbitrary",)),
        name="gla_scan_bwd",
    )(proj, proj, proj, G_f, st_f, do, proj, proj, proj, G_b, st_b, do)


def _local_step(x, target, wcat, conv_w, avec, dvec, gdn_norm_w, w2f, w2b, b2f, b2b, gla_norm_w, proj_weights, w_pre, w_post,
                send_proj_grads=None, send_in_grads=None):
    h, h_t = _rms_pre(x, w_pre)
    proj = _matmul(h, wcat, "nt", f32, "proj_in", 2048, 512, 1024)
    qkvn = _gdn_prep(proj, conv_w)
    gates, gc, G_f, G_b = _gate_prep(proj, avec, dvec, w2f, w2b, b2f, b2b)
    oa_f, oa_b, sa_f, sa_b, inv_f, inv_b, sol_f, sol_b = _gdn_fwd(qkvn, gates, gc)
    og = _out_norm(oa_f, oa_b, proj, Z_BLK, gdn_norm_w, GDN_DIM, "gdn_out")
    ob_f, ob_b, sb_f, sb_b = _gla_fwd(proj, G_f, G_b)
    obg = _out_norm(ob_f, ob_b, proj, G_B_BLK, gla_norm_w, GLA_DV, "gla_out")
    wpg, wpl, wout = proj_weights(obg)
    (loss_row, d_wpost, dy, m, dout, dya, dyb, dproj, dog, dobg) = _mid(og, obg, proj, x, target, wpg, wpl, wout, w_post)

    dw_p = _dw_proj(((og, dya), (obg, dyb), (m, dout)), "dw_proj")
    d_wpg, d_wpl, d_wout = (dw_p[:, i].reshape(D_MODEL, D_MODEL) for i in range(3))
    if send_proj_grads is not None:
        gla_norm_w = gla_norm_w + send_proj_grads(dw_p)

    do_b, dproj, d_gla_norm = _out_norm_bwd(dobg, ob_f, ob_b, proj, G_B_BLK, gla_norm_w, GLA_DV, "gla_out_bwd", dproj)
    dq_f, dk_f, dv_f, dgk_f, dq_b, dk_b, dv_b, dgk_b = _gla_bwd(proj, G_f, G_b, sb_f, sb_b, do_b)
    dproj, dsmall_gla, d_w2f, d_w2b, d_b2f, d_b2b = _gla_prep_bwd(proj, w2f, w2b, b2f, b2b, dgk_f, dgk_b,
                                                                 (dq_f, dq_b, dk_f, dk_b, dv_f, dv_b), dproj)

    do_a, dproj, d_gdn_norm = _out_norm_bwd(dog, oa_f, oa_b, proj, Z_BLK, gdn_norm_w, GDN_DIM, "gdn_out_bwd", dproj)
    dqkv_f, dqkv_b, dg_f, dg_b = _gdn_bwd(qkvn, gates, gc, sa_f, sa_b, inv_f, inv_b, sol_f, sol_b, do_a)
    dproj, d_conv = _gdn_prep_bwd(proj, conv_w, dqkv_f, dqkv_b, dproj)
    dproj, d_alog, d_dtb = _gdn_gates_bwd(proj, avec, dvec, dg_f, dg_b, dsmall_gla, dproj)

    d_wcat = _matmul(h_t, dproj, "nn", MXU_DTYPE, "dw_in", 1024, 512, 2048, out_t=True)
    started = None if send_in_grads is None else send_in_grads(d_wcat)
    dh = _matmul(dproj, wcat, "nn", f32, "dh", 512, 1024, N_CAT, after=started)
    grad_x, d_wpre = _rms_pre_bwd(x, w_pre, dh, dy)
    return dict(loss_row=loss_row, grad_x=grad_x, d_wcat=d_wcat, d_conv=d_conv, d_alog=d_alog, d_dtb=d_dtb,
                d_gdn_norm=d_gdn_norm, d_wpg=d_wpg, d_w2f=d_w2f, d_w2b=d_w2b, d_b2f=d_b2f, d_b2b=d_b2b,
                d_gla_norm=d_gla_norm, d_wpl=d_wpl, d_wout=d_wout, d_wpost=d_wpost, d_wpre=d_wpre)


def _to_cat(wf):
    pad = jnp.zeros((N_CAT - N_IN,) + wf.shape[1:], wf.dtype)
    return jnp.concatenate([wf[:4096], wf[4128:6176], wf[7232:9280], wf[6176:7200], wf[4096:4128], wf[7200:7232], pad], axis=0)


def _from_cat(wc):
    return jnp.concatenate([wc[:4096], wc[9216:9248], wc[4096:6144], wc[8192:9216], wc[9248:9280], wc[6144:8192]], axis=0)


def _lane_row(*pieces):
    row = jnp.concatenate(pieces, axis=1)
    return jnp.pad(row, ((0, 0), (0, LANES - row.shape[1])))


def _pad_w2(w2, row0):
    return jnp.pad(w2, ((row0, LANES - row0 - w2.shape[0]), (0, 0)))


SM_ROWS, SM_COLS = 32, 384


def _pack_small_shard(conv, w2f, w2b):
    lead = conv.shape[:-2]
    z = lambda r, c: jnp.zeros(lead + (r, c), f32)
    top = jnp.concatenate([conv, z(8 - CONV_K, SM_COLS)], axis=-2)
    mid = jnp.concatenate([w2f, w2b, z(16, SM_COLS - 128)], axis=-1)
    return jnp.concatenate([top, mid, z(SM_ROWS - 24, SM_COLS)], axis=-2)


SMALL_REGION = {
    "ln_pre_w": (0, 1, 0, 1024), "ln_post_w": (1, 1, 0, 1024), "gk_b2_fwd": (2, 1, 0, 512), "gk_b2_bwd": (2, 1, 512, 512),
    "gla_norm_w": (3, 1, 0, 256), "gdn_norm_w": (3, 1, 256, 128), "a_log_fwd": (4, 1, 0, 8), "a_log_bwd": (5, 1, 0, 8),
    "dt_bias_fwd": (6, 1, 0, 8), "dt_bias_bwd": (7, 1, 0, 8), "conv_w": (8, CONV_K, 0, SM_COLS),
    "gk_w2_fwd": (16, 16, 0, 64), "gk_w2_bwd": (16, 16, 128, 64),
}
SMALL_NAMES = tuple(SMALL_REGION)
SMALL_SHARDED = ("conv_w", "gk_w2_fwd", "gk_w2_bwd")
PACK_ROWS, PACK_COLS = 32, 1024
LOSS_AT = (13, 0)


def _region(name):
    r0, nr, l0, nl = SMALL_REGION[name]
    return slice(r0, r0 + nr), slice(l0, l0 + nl)


def _pack_small_grads(G):
    names = ("d_wpre", "d_wpost", "d_b2f", "d_b2b", "d_gla_norm", "d_gdn_norm", "d_alog", "d_dtb", "loss_row", "d_conv", "d_w2f", "d_w2b")

    def body(wpre, wpost, b2f, b2b, glan, gdnn, alog, dtb, lrow, conv, w2f, w2b, o_ref):
        o_ref[...] = jnp.zeros(o_ref.shape, f32)
        loss = 0.5 * jnp.sum(lrow[...], axis=1, keepdims=True) * (1.0 / D_MODEL)
        whole = {"ln_pre_w": wpre, "ln_post_w": wpost, "gk_b2_fwd": b2f, "gk_b2_bwd": b2b, "gla_norm_w": glan, "gdn_norm_w": gdnn}
        for j in range(N_DEV):
            for name, ref in whole.items():
                o_ref[(j,) + _region(name)] = ref[...]
            o_ref[(j,) + _region("a_log_fwd")] = alog[0:1, 0:8]
            o_ref[(j,) + _region("a_log_bwd")] = alog[0:1, 8:16]
            o_ref[(j,) + _region("dt_bias_fwd")] = dtb[0:1, 0:8]
            o_ref[(j,) + _region("dt_bias_bwd")] = dtb[0:1, 8:16]
            o_ref[(j,) + _region("conv_w")] = conv[:, j * SM_COLS : (j + 1) * SM_COLS]
            o_ref[(j,) + _region("gk_w2_fwd")] = w2f[32:48, j * 64 : (j + 1) * 64]
            o_ref[(j,) + _region("gk_w2_bwd")] = w2b[48:64, j * 64 : (j + 1) * 64]
            o_ref[j, LOSS_AT[0] : LOSS_AT[0] + 1, LOSS_AT[1] : LOSS_AT[1] + 1] = loss

    return pl.pallas_call(
        body, out_shape=jax.ShapeDtypeStruct((N_DEV, PACK_ROWS, PACK_COLS), f32), name="pack_small_grads",
    )(*[G[n] for n in names])


def _small_adam(land, own, W, M, V):
    bc1 = 1.0 - ADAM_B1**ADAM_STEP
    bc2 = 1.0 - ADAM_B2**ADAM_STEP
    n = len(SMALL_NAMES)

    def body(land_ref, own_ref, *refs):
        w_refs, m_refs, v_refs = refs[0:n], refs[n : 2 * n], refs[2 * n : 3 * n]
        outs, loss_ref, g_scr = refs[3 * n : 7 * n], refs[7 * n], refs[7 * n + 1]
        me = _my_device()
        g = jnp.where(me == 0, own_ref[0], land_ref[0])
        for j in range(1, N_DEV):
            g = g + jnp.where(me == j, own_ref[j], land_ref[j])
        g_scr[...] = g
        loss_ref[...] = g_scr[LOSS_AT[0] : LOSS_AT[0] + 1, LOSS_AT[1] : LOSS_AT[1] + 1]
        for i, name in enumerate(SMALL_NAMES):
            gp = g_scr[_region(name)]
            lead = (0,) if name in SMALL_SHARDED else (Ellipsis,)
            w, m, v = w_refs[i][lead], m_refs[i][lead], v_refs[i][lead]
            m2 = ADAM_B1 * m + (1.0 - ADAM_B1) * gp
            v2 = ADAM_B2 * v + (1.0 - ADAM_B2) * (gp * gp)
            delta = -ADAM_LR * ((m2 / bc1) / (jnp.sqrt(v2 / bc2) + ADAM_EPS) + ADAM_WD * w)
            for k, val in enumerate((gp, delta, m2, v2)):
                outs[4 * i + k][lead] = val

    shapes = [jax.ShapeDtypeStruct(W[name].shape, f32) for name in SMALL_NAMES for _ in range(4)]
    res = pl.pallas_call(
        body,
        out_shape=tuple(shapes) + (jax.ShapeDtypeStruct((1, 1), f32),),
        scratch_shapes=[pltpu.VMEM((PACK_ROWS, PACK_COLS), f32)],
        name="adam_small",
    )(land, own, *[P[name] for P in (W, M, V) for name in SMALL_NAMES])
    return {name: res[4 * i : 4 * i + 4] for i, name in enumerate(SMALL_NAMES)}, res[4 * n]


_MESH = pl.DeviceIdType.MESH


N_BIG_COPIES, N_SMALL_COPIES = 9, 7


def _gather_weights(w_in_s, sm_s):
    shapes = (w_in_s.shape, sm_s.shape)
    dtypes = (MXU_DTYPE, f32)

    def body(win_ref, sm_ref, gin_ref, gsm_ref, own_ref, send_sems, recv_sems, own_sem):
        x, y, c = lax.axis_index("x"), lax.axis_index("y"), lax.axis_index("c")
        me, sibling = (x, y, c), (x, y, 1 - c)
        chips = [(1 - x, y), (x, 1 - y), (1 - x, 1 - y)]

        def idx(px, py, pc):
            return 4 * px + 2 * py + pc

        for r in range(w_in_s.shape[1] // LANES):
            cols = slice(r * LANES, (r + 1) * LANES)
            own_ref[:, cols] = win_ref[:, cols].astype(MXU_DTYPE)
        keep_own = pltpu.make_async_copy(own_ref, gin_ref.at[idx(*me)], own_sem.at[0])
        keep_own.start()
        gsm_ref[idx(*me)] = sm_ref[...]

        def copy(sem, blk, to, src=None):
            return pltpu.make_async_remote_copy(src_ref=blk if src is None else src, dst_ref=blk, send_sem=send_sems.at[sem],
                                                recv_sem=recv_sems.at[sem], device_id=to, device_id_type=_MESH)

        def small(k, block, to):
            return copy(N_BIG_COPIES + k, gsm_ref.at[idx(*block)], to)

        half = w_in_s.shape[1] // 2
        xn, yn, dg = (1 - x, y), (x, 1 - y), (1 - x, 1 - y)

        def big(sem, block, to, part=None, src=None):
            blk = gin_ref.at[idx(*block)]
            return copy(sem, blk if part is None else blk.at[:, pl.ds(part * half, half)], to, src)

        started = [big(0, me, sibling, src=own_ref), big(1, me, (*xn, c), src=own_ref), big(2, me, (*yn, c), src=own_ref),
                   small(0, me, sibling)]
        started += [small(1 + j, me, (*chip, c)) for j, chip in enumerate(chips)]
        for cp in started:
            cp.start()

        def then(arrived, forwards):
            arrived.wait_recv()
            for cp in forwards:
                cp.start()
            started.extend(forwards)

        then(big(1, (*xn, c), me), [big(3, (*xn, c), (*yn, c), 1), big(5, (*xn, c), sibling)])
        then(big(2, (*yn, c), me), [big(4, (*yn, c), (*xn, c), 0), big(6, (*yn, c), sibling)])
        then(big(4, (*dg, c), me, 0), [big(7, (*dg, c), sibling, 0)])
        then(big(3, (*dg, c), me, 1), [big(8, (*dg, c), sibling, 1)])
        for j, chip in enumerate(chips):
            then(small(1 + j, (*chip, c), me), [small(4 + j, (*chip, c), sibling)])
        big(0, sibling, me).wait_recv()
        big(5, (*xn, 1 - c), me).wait_recv()
        big(6, (*yn, 1 - c), me).wait_recv()
        big(7, (*dg, 1 - c), me, 0).wait_recv()
        big(8, (*dg, 1 - c), me, 1).wait_recv()
        small(0, sibling, me).wait_recv()
        for j, chip in enumerate(chips):
            small(4 + j, (*chip, 1 - c), me).wait_recv()
        for cp in started:
            cp.wait_send()
        keep_own.wait()

    vm = pl.BlockSpec(memory_space=pltpu.VMEM)
    n_copies = N_BIG_COPIES + N_SMALL_COPIES
    return pl.pallas_call(
        body,
        out_shape=tuple(jax.ShapeDtypeStruct((N_DEV,) + s, d) for s, d in zip(shapes, dtypes)),
        in_specs=[vm, vm],
        out_specs=(_HBM_SPEC, vm),
        scratch_shapes=[pltpu.VMEM(w_in_s.shape, MXU_DTYPE), pltpu.SemaphoreType.DMA((n_copies,)), pltpu.SemaphoreType.DMA((n_copies,)),
                        pltpu.SemaphoreType.DMA((1,))],
        compiler_params=pltpu.CompilerParams(vmem_limit_bytes=VMEM_LIMIT),
        name="gather_weights",
    )(w_in_s, sm_s)


N_CHIP = 4
_EFFECT = pltpu.SideEffectType.DATAFLOW_SIDE_EFFECTING
_HBM_SPEC = pl.BlockSpec(memory_space=pltpu.HBM)
_SEM_SPEC = pl.BlockSpec(memory_space=pltpu.SEMAPHORE)


def _relation_peer(k, x, y, c):
    px = (1 - x) if (k & 4) else x
    py = (1 - y) if (k & 2) else y
    pc = (1 - c) if (k & 1) else c
    return (px, py, pc), 4 * px + 2 * py + pc


_PLAN_COPIES = {"gather": N_DEV - 1, "blocks": N_DEV - 1, "sibling": 4, "chips": 3}
_PLAN_SLOTS = {"gather": N_DEV, "blocks": N_DEV, "sibling": 4, "chips": 4}


def _plan_copies(plan, src_ref, land_ref, send_sems, recv_sems, arrival):
    x, y, c = lax.axis_index("x"), lax.axis_index("y"), lax.axis_index("c")
    me, my_chip = 4 * x + 2 * y + c, 2 * x + y
    out = []

    def add(i, src, there, here, dev):
        out.append(pltpu.make_async_remote_copy(src_ref=src, dst_ref=land_ref.at[here if arrival else there], send_sem=send_sems.at[i],
                                                recv_sem=recv_sems.at[i], device_id=dev, device_id_type=_MESH))

    if plan in ("gather", "blocks"):
        for k in range(1, N_DEV):
            dev, p = _relation_peer(k, x, y, c)
            add(k - 1, src_ref.at[p] if plan == "blocks" else src_ref, me, p, dev)
    elif plan == "sibling":
        for j in range(4):
            add(j, src_ref.at[2 * j + 1 - c], j, j, (x, y, 1 - c))
    else:
        for i, k in enumerate((2, 4, 6)):
            (px, py, pc), _ = _relation_peer(k, x, y, c)
            add(i, src_ref.at[2 * px + py], my_chip, 2 * px + py, (px, py, pc))
    return out


def _exchange_start(src, plan, name, after=None):
    n = _PLAN_COPIES[plan]
    land = lax.empty((_PLAN_SLOTS[plan],) + (src.shape if plan == "gather" else src.shape[1:]), src.dtype)
    extra = [] if after is None else [after]

    def body(src_ref, land_ref, *rest):
        send_sems, recv_sems, _, _, token = rest[len(extra) :]
        for cp in _plan_copies(plan, src_ref, land_ref, send_sems, recv_sems, False):
            cp.start()
        token[...] = jnp.zeros_like(token)

    return pl.pallas_call(
        body,
        name=name,
        out_shape=(pltpu.SemaphoreType.DMA((n,)), pltpu.SemaphoreType.DMA((n,)), pltpu.HBM(src.shape, src.dtype),
                   pltpu.HBM(land.shape, land.dtype), jax.ShapeDtypeStruct((8, LANES), f32)),
        in_specs=(_HBM_SPEC, _HBM_SPEC) + (_ANY_SPEC,) * len(extra),
        out_specs=(_SEM_SPEC, _SEM_SPEC, _HBM_SPEC, _HBM_SPEC, pl.BlockSpec(memory_space=pltpu.VMEM)),
        input_output_aliases={0: 2, 1: 3},
        compiler_params=pltpu.CompilerParams(has_side_effects=_EFFECT),
    )(pltpu.with_memory_space_constraint(src, pltpu.HBM), pltpu.with_memory_space_constraint(land, pltpu.HBM), *extra)


def _exchange_wait(started, after, plan, name):
    send_sems, recv_sems, src_thru, land_thru, _ = started

    def body(src_ref, land_ref, send_sems, recv_sems, after_ref, src_dead, got_ref):
        for cp in _plan_copies(plan, src_ref, land_ref, send_sems, recv_sems, True):
            cp.wait_send()
            cp.wait_recv()

    return pl.pallas_call(
        body,
        name=name,
        out_shape=(pltpu.HBM(src_thru.shape, src_thru.dtype), pltpu.HBM(land_thru.shape, land_thru.dtype)),
        in_specs=(_HBM_SPEC, _HBM_SPEC, _SEM_SPEC, _SEM_SPEC, pl.BlockSpec(memory_space=pl.ANY)),
        out_specs=(_HBM_SPEC, _HBM_SPEC),
        input_output_aliases={0: 0, 1: 1},
        compiler_params=pltpu.CompilerParams(has_side_effects=_EFFECT),
    )(src_thru, land_thru, send_sems, recv_sems, after)


def _pair_sum(g, r, name, tc=LANES):
    _, R, Cc = g.shape

    def body(g_ref, r_ref, o_ref):
        o_ref[...] = (g_ref[...].astype(f32) + r_ref[...].astype(f32)).astype(o_ref.dtype)

    return pl.pallas_call(
        body,
        out_shape=jax.ShapeDtypeStruct(r.shape, r.dtype),
        grid=(N_CHIP, Cc // tc),
        in_specs=[pl.BlockSpec((1, R, tc), lambda k, i: (2 * k + lax.axis_index("c"), 0, i)), pl.BlockSpec((1, R, tc), lambda k, i: (k, 0, i))],
        out_specs=pl.BlockSpec((1, R, tc), lambda k, i: (k, 0, i)),
        compiler_params=_cparams(("parallel", "parallel")),
        name=name,
    )(g, r)


def _sum_adam(parts, w, m, v, name, tb, tc=None, own=None, own_slot=None):
    R, Cc = w.shape
    n_parts = parts.shape[0]
    tb = R if tc else min(tb, R)
    tc = tc or Cc
    assert R % tb == 0 and Cc % tc == 0
    bc1 = 1.0 - ADAM_B1**ADAM_STEP
    bc2 = 1.0 - ADAM_B2**ADAM_STEP
    extra = [] if own is None else [own]

    def body(p_ref, w_ref, m_ref, v_ref, *rest):
        g_ref, d_ref, nm_ref, nv_ref = rest[len(extra) :]

        def part(j):
            pj = p_ref[j].astype(f32)
            return pj if own is None else jnp.where(own_slot() == j, rest[0][0].astype(f32), pj)

        g = part(0)
        for j in range(1, n_parts):
            g = g + part(j)
        g_ref[...] = g
        m2 = ADAM_B1 * m_ref[...] + (1.0 - ADAM_B1) * g
        v2 = ADAM_B2 * v_ref[...] + (1.0 - ADAM_B2) * (g * g)
        nm_ref[...] = m2
        nv_ref[...] = v2
        d_ref[...] = -ADAM_LR * ((m2 / bc1) / (jnp.sqrt(v2 / bc2) + ADAM_EPS) + ADAM_WD * w_ref[...])

    blk = pl.BlockSpec((tb, tc), lambda i, j: (i, j))
    o = jax.ShapeDtypeStruct((R, Cc), f32)
    return pl.pallas_call(
        body,
        out_shape=(o, o, o, o),
        grid=(R // tb, Cc // tc),
        in_specs=[pl.BlockSpec((n_parts, tb, tc), lambda i, j: (0, i, j)), blk, blk, blk]
        + [pl.BlockSpec((1, tb, tc), lambda i, j: (own_slot(), i, j))] * len(extra),
        out_specs=(blk, blk, blk, blk),
        compiler_params=_cparams(("parallel", "parallel")),
        name=name,
    )(parts, w, m, v, *extra)


def _my_device():
    return 4 * lax.axis_index("x") + 2 * lax.axis_index("y") + lax.axis_index("c")


def _my_chip():
    return 2 * lax.axis_index("x") + lax.axis_index("y")


_WEIGHTS = ["ln_pre_w", "w_in", "conv_w", "a_log_fwd", "a_log_bwd", "dt_bias_fwd", "dt_bias_bwd", "gdn_norm_w", "w_proj_gdn",
            "gk_w2_fwd", "gk_b2_fwd", "gk_w2_bwd", "gk_b2_bwd", "gla_norm_w", "w_proj_gla", "w_out", "ln_post_w"]


def kernel(x, ln_pre_w, w_in, conv_w, a_log_fwd, a_log_bwd, dt_bias_fwd, dt_bias_bwd, gdn_norm_w, w_proj_gdn, gk_w2_fwd, gk_b2_fwd, gk_w2_bwd, gk_b2_bwd, gla_norm_w, w_proj_gla, w_out, ln_post_w, loss_target, m_ln_pre_w, m_w_in, m_conv_w, m_a_log_fwd, m_a_log_bwd, m_dt_bias_fwd, m_dt_bias_bwd, m_gdn_norm_w, m_w_proj_gdn, m_gk_w2_fwd, m_gk_b2_fwd, m_gk_w2_bwd, m_gk_b2_bwd, m_gla_norm_w, m_w_proj_gla, m_w_out, m_ln_post_w, v_ln_pre_w, v_w_in, v_conv_w, v_a_log_fwd, v_a_log_bwd, v_dt_bias_fwd, v_dt_bias_bwd, v_gdn_norm_w, v_w_proj_gdn, v_gk_w2_fwd, v_gk_b2_fwd, v_gk_w2_bwd, v_gk_b2_bwd, v_gla_norm_w, v_w_proj_gla, v_w_out, v_ln_post_w):
    args = locals()
    W = {n: args[n] for n in _WEIGHTS}
    M = {n: args["m_" + n] for n in _WEIGHTS}
    V = {n: args["v_" + n] for n in _WEIGHTS}

    wp_stack = lambda P: jnp.concatenate([P["w_proj_gdn"], P["w_proj_gla"], P["w_out"]], axis=0)
    sm_pack = lambda P: _pack_small_shard(P["conv_w"][0], P["gk_w2_fwd"][0], P["gk_w2_bwd"][0])
    me = 4 * lax.axis_index("x") + 2 * lax.axis_index("y") + lax.axis_index("c")
    w_in_t, m_in_t, v_in_t = (jnp.transpose(a[0]) for a in (w_in, m_w_in, v_w_in))
    g_in, g_sm = _gather_weights(w_in_t, sm_pack(W))
    wp_own = wp_stack(W).astype(MXU_DTYPE)
    wp_started = _exchange_start(wp_own, "gather", "gather_proj_start", after=g_sm)

    def proj_weights(after):
        own, land = _exchange_wait(wp_started, after, "gather", "gather_proj_wait")
        g_p = lax.dynamic_update_slice(land, own[None], (me, 0, 0, 0))
        return tuple(g_p[:, i].reshape(D_MODEL, D_MODEL) for i in range(3))

    rows_p = 3 * D_MODEL // N_DEV
    sent = {}

    def send_proj_grads(dw_p):
        sent["p_p"] = dw_p.reshape(N_DEV, rows_p, D_MODEL)
        sent["started"] = _exchange_start(sent["p_p"], "blocks", "exchange_proj_start")
        return sent["started"][4][0:1, 0:1]

    def send_in_grads(d_wcat):
        p_in = _from_cat(d_wcat).reshape(N_DEV, SHARD_IN, D_MODEL)
        sib = _exchange_start(p_in, "sibling", "exchange_in_sibling_start")
        p_in, sib_land = _exchange_wait(sib, sib[4], "sibling", "exchange_in_sibling_wait")
        sent["chips"] = _exchange_start(_pair_sum(p_in, sib_land, "pair_sum_w_in", tc=1024), "chips", "exchange_in_chips_start")
        p_p, land_p = _exchange_wait(sent["started"], sent["chips"][4], "blocks", "exchange_proj_wait")
        sent["o_p"] = _sum_adam(land_p, wp_stack(W).reshape(rows_p, D_MODEL), wp_stack(M).reshape(rows_p, D_MODEL),
                                wp_stack(V).reshape(rows_p, D_MODEL), "adam_w_proj", 128, own=p_p, own_slot=_my_device)
        return sent["o_p"][0]

    wcat = _to_cat(g_in.reshape(N_IN, D_MODEL))
    conv_full = g_sm[:, 0:CONV_K, :].transpose(1, 0, 2).reshape(CONV_K, N_DEV * SM_COLS)
    w2f_full = g_sm[:, 8:24, 0:64].transpose(1, 0, 2).reshape(16, 512)
    w2b_full = g_sm[:, 8:24, 64:128].transpose(1, 0, 2).reshape(16, 512)

    avec = _lane_row(a_log_fwd, a_log_bwd)
    dvec = _lane_row(dt_bias_fwd, dt_bias_bwd)
    G = _local_step(x[0], loss_target[0], wcat, conv_full, avec, dvec, gdn_norm_w, _pad_w2(w2f_full, 32), _pad_w2(w2b_full, 48),
                    gk_b2_fwd, gk_b2_bwd, gla_norm_w, proj_weights, ln_pre_w + wp_started[4][0:1, 0:1], ln_post_w, send_proj_grads,
                    send_in_grads)

    small_started = _exchange_start(_pack_small_grads(G), "blocks", "exchange_small_start")
    q_in, land_in = _exchange_wait(sent["chips"], small_started[4], "chips", "exchange_in_chips_wait")
    o_in = _sum_adam(land_in, w_in_t, m_in_t, v_in_t, "adam_w_in", SHARD_IN, tc=256, own=q_in, own_slot=_my_chip)
    o_p = sent["o_p"]
    p_small, land_small = _exchange_wait(small_started, o_in[0], "blocks", "exchange_small_wait")
    o_small, loss = _small_adam(land_small, p_small, W, M, V)

    res = []
    for kind in range(4):
        per = {name: o_small[name][kind] for name in SMALL_NAMES}
        per["w_in"] = jnp.transpose(o_in[kind])[None]
        pp = o_p[kind].reshape(3, 1, D_MODEL // N_DEV, D_MODEL)
        per["w_proj_gdn"], per["w_proj_gla"], per["w_out"] = pp[0], pp[1], pp[2]
        res.append([per[n] for n in _WEIGHTS])
    return (loss[0, 0], G["grad_x"][None], *res[0], *res[1], *res[2], *res[3])
```
